```python
import math
import jax, jax.numpy as jnp
from jax import lax
import numpy as np

D_MODEL = 1024
BATCH = 8
SEQ = 8192
DEPTH = 1

SSD_EXPAND = 2
D_INNER = SSD_EXPAND * D_MODEL
SSD_HEAD_DIM = 64
SSD_HEADS = D_INNER // SSD_HEAD_DIM
SSD_GROUPS = 4
SSD_HEADS_PER_GROUP = SSD_HEADS // SSD_GROUPS
D_STATE = 128
D_CONV = 5
SSD_CHUNK = 128
CONV_DIM = D_INNER + 2 * SSD_GROUPS * D_STATE
NORM_EPS = 1e-5

ATTN_HEAD_DIM = 64
DIL_PATTERNS = ((128, 1), (512, 4), (2048, 16))
N_PATTERNS = len(DIL_PATTERNS)
HEADS_PER_PATTERN = 4
ATTN_HEADS = N_PATTERNS * HEADS_PER_PATTERN
ATTN_WIDTH = ATTN_HEADS * ATTN_HEAD_DIM
ATTN_OUT_WIDTH = HEADS_PER_PATTERN * ATTN_HEAD_DIM

D_FF = 4 * D_MODEL
N_BRANCHES = 2
IN_SPLITS = (D_INNER, CONV_DIM, SSD_HEADS, SSD_HEADS, ATTN_WIDTH, ATTN_WIDTH, ATTN_WIDTH, N_BRANCHES * D_MODEL)
IN_COLS = sum(IN_SPLITS)

kernel_name = "hybrid_ssd_dilated_attn_gated_deepnorm"


def layer_norm(x, g, b):
    xf = x.astype(jnp.float32)
    mu = jnp.mean(xf, axis=-1, keepdims=True)
    var = jnp.mean(jnp.square(xf - mu), axis=-1, keepdims=True)
    return ((xf - mu) * lax.rsqrt(var + NORM_EPS) * g.astype(jnp.float32) + b.astype(jnp.float32)).astype(x.dtype)


def centred_depthwise_conv(u, w, b):
    pad = D_CONV // 2
    out = lax.conv_general_dilated(u, w[:, None, :].astype(u.dtype), window_strides=(1,), padding=((pad, pad),),
                                   dimension_numbers=('NWC', 'WIO', 'NWC'), feature_group_count=u.shape[-1])
    return out + b.astype(u.dtype)


def segsum_exp(a_cs):
    q = a_cs.shape[-1]
    diff = a_cs[..., :, None] - a_cs[..., None, :]
    mask = jnp.tril(jnp.ones((q, q), dtype=bool))
    return jnp.where(mask, jnp.exp(jnp.where(mask, diff, 0.0)), 0.0)


def ssd_chunked(xh, dt, a_coef, bm, cm):
    bsz, s = xh.shape[:2]
    nc = s // SSD_CHUNK
    g, r, p = SSD_GROUPS, SSD_HEADS_PER_GROUP, SSD_HEAD_DIM
    xc = (xh * dt[..., None]).reshape(bsz, nc, SSD_CHUNK, g, r, p)
    a = (dt * a_coef).reshape(bsz, nc, SSD_CHUNK, g, r).transpose(0, 3, 4, 1, 2)
    bc = bm.reshape(bsz, nc, SSD_CHUNK, g, D_STATE)
    cc = cm.reshape(bsz, nc, SSD_CHUNK, g, D_STATE)
    a_cs = jnp.cumsum(a, axis=-1)
    lmat = segsum_exp(a_cs)
    cb = jnp.einsum('bclgn,bcsgn->bgcls', cc, bc)
    y_diag = jnp.einsum('bgcls,bgrcls,bcsgrp->bclgrp', cb, lmat, xc)
    decay_states = jnp.exp(a_cs[..., -1:] - a_cs)
    states = jnp.einsum('bclgn,bgrcl,bclgrp->bcgrpn', bc, decay_states, xc)
    chunk_decay = jnp.exp(a_cs[..., -1])

    def step(h, inp):
        dec, st = inp
        return dec[..., None, None] * h + st, h

    h0 = jnp.zeros_like(states[:, 0])
    _, prev = lax.scan(step, h0, (jnp.moveaxis(chunk_decay, -1, 0), jnp.moveaxis(states, 1, 0)))
    prev = jnp.moveaxis(prev, 0, 1)
    y_off = jnp.einsum('bclgn,bcgrpn,bgrcl->bclgrp', cc, prev, jnp.exp(a_cs))
    return (y_diag + y_off).reshape(bsz, s, g * r, p)


def ssd_branch(z, xbc, dt_f_raw, dt_b_raw, conv_w, conv_b, dt_bias_f, dt_bias_b, a_log_f, a_log_b, d_skip, ssd_norm_w):
    bsz, s = z.shape[:2]
    xbc = jax.nn.silu(centred_depthwise_conv(xbc, conv_w, conv_b)).astype(jnp.float32)
    xs, bm, cm = jnp.split(xbc, [D_INNER, D_INNER + SSD_GROUPS * D_STATE], axis=-1)
    xh = xs.reshape(bsz, s, SSD_HEADS, SSD_HEAD_DIM)
    bm = bm.reshape(bsz, s, SSD_GROUPS, D_STATE)
    cm = cm.reshape(bsz, s, SSD_GROUPS, D_STATE)
    dt_f = jax.nn.softplus(dt_f_raw.astype(jnp.float32) + dt_bias_f.astype(jnp.float32))
    dt_b = jax.nn.softplus(dt_b_raw.astype(jnp.float32) + dt_bias_b.astype(jnp.float32))
    a_f = -jnp.exp(a_log_f.astype(jnp.float32))
    a_b = -jnp.exp(a_log_b.astype(jnp.float32))
    y_f = ssd_chunked(xh, dt_f, a_f, bm, cm)
    flip = lambda t: jnp.flip(t, axis=1)
    y_b = flip(ssd_chunked(flip(xh), flip(dt_b), a_b, flip(bm), flip(cm)))
    y = y_f + y_b + d_skip.astype(jnp.float32)[:, None] * xh
    y = y.reshape(bsz, s, D_INNER) * jax.nn.silu(z.astype(jnp.float32))
    yg = y.reshape(bsz, s, SSD_GROUPS, D_INNER // SSD_GROUPS)
    yg = yg * lax.rsqrt(jnp.mean(jnp.square(yg), axis=-1, keepdims=True) + NORM_EPS)
    return (yg.reshape(bsz, s, D_INNER) * ssd_norm_w.astype(jnp.float32)).astype(z.dtype)


def dilated_window_attention(q, k, v, slopes, dilation, half):
    bsz, s, h, e = q.shape
    seq_l = s // dilation
    blk = half
    nb = -(-seq_l // blk)
    lp = nb * blk

    def to_strided(a):
        return a.astype(jnp.float32).reshape(bsz, seq_l, dilation, h, e).transpose(0, 2, 3, 1, 4)

    qs = jnp.pad(to_strided(q), ((0, 0), (0, 0), (0, 0), (0, lp - seq_l), (0, 0))).reshape(bsz, dilation, h, nb, blk, e)

    def windows(a):
        a = jnp.pad(to_strided(a), ((0, 0), (0, 0), (0, 0), (blk, blk + lp - seq_l), (0, 0)))
        a = a.reshape(bsz, dilation, h, nb + 2, blk, e)
        return jnp.concatenate([a[:, :, :, :-2], a[:, :, :, 1:-1], a[:, :, :, 2:]], axis=4)

    ks, vs = windows(k), windows(v)
    qpos = jnp.arange(nb)[:, None] * blk + jnp.arange(blk)[None, :]
    kpos = jnp.arange(nb)[:, None] * blk - blk + jnp.arange(3 * blk)[None, :]
    rel = kpos[:, None, :] - qpos[:, :, None]
    valid = (jnp.abs(rel) <= half) & (kpos[:, None, :] >= 0) & (kpos[:, None, :] < seq_l)
    dist = (jnp.abs(rel) * dilation).astype(jnp.float32)
    scores = jnp.einsum('bdhine,bdhime->bdhinm', qs, ks) * (1.0 / math.sqrt(e))
    scores = scores - slopes.astype(jnp.float32)[:, None, None, None] * dist
    scores = jnp.where(valid, scores, -jnp.inf)
    m = jnp.max(scores, axis=-1, keepdims=True)
    p = jnp.exp(scores - m)
    den = jnp.sum(p, axis=-1, keepdims=True)
    o = jnp.einsum('bdhinm,bdhime->bdhine', p, vs) / den
    lse = (m + jnp.log(den))[..., 0]

    def from_strided(a):
        a = a.reshape(bsz, dilation, h, lp, *a.shape[5:])[:, :, :, :seq_l]
        a = jnp.moveaxis(a, 3, 1)
        return a.reshape(bsz, s, h, *a.shape[4:])

    return from_strided(o), from_strided(lse)


def attention_branch(q, k, v):
    bsz, s = q.shape[:2]
    q = q.reshape(bsz, s, ATTN_HEADS, ATTN_HEAD_DIM)
    k = k.reshape(bsz, s, ATTN_HEADS, ATTN_HEAD_DIM)
    v = v.reshape(bsz, s, ATTN_HEADS, ATTN_HEAD_DIM)
    slopes = jnp.asarray(2.0 ** (-8.0 * np.arange(1, ATTN_HEADS + 1) / ATTN_HEADS), dtype=jnp.float32)
    outs, lses = [], []
    for gi, (window, dilation) in enumerate(DIL_PATTERNS):
        hs = slice(gi * HEADS_PER_PATTERN, (gi + 1) * HEADS_PER_PATTERN)
        o, l = dilated_window_attention(q[:, :, hs], k[:, :, hs], v[:, :, hs], slopes[hs], dilation, window // (2 * dilation))
        outs.append(o)
        lses.append(l)
    o = jnp.stack(outs, axis=0)
    lse = jnp.stack(lses, axis=0)
    w = jax.nn.softmax(lse, axis=0)
    y = jnp.sum(w[..., None] * o, axis=0)
    return y.reshape(bsz, s, ATTN_OUT_WIDTH).astype(q.dtype)


def _fwd_setup_inputs(seed: int = 0) -> dict:
    key = jax.random.key(seed)
    ks = jax.random.split(key, 24)
    beta = (8.0 * DEPTH) ** -0.25
    nrm = lambda k, shape: jax.random.normal(k, shape, dtype=jnp.float32)

    def dt_bias(k):
        dt = jnp.exp(jax.random.uniform(k, (SSD_HEADS,), minval=math.log(1e-3), maxval=math.log(1e-1)))
        return dt + jnp.log(-jnp.expm1(-dt))

    return {
        "x": nrm(ks[0], (BATCH, SEQ, D_MODEL)),
        "w_in": nrm(ks[1], (D_MODEL, IN_COLS)) * D_MODEL ** -0.5,
        "b_gate": 0.02 * nrm(ks[2], (N_BRANCHES * D_MODEL,)),
        "conv_w": nrm(ks[3], (D_CONV, CONV_DIM)) * D_CONV ** -0.5,
        "conv_b": 0.02 * nrm(ks[4], (CONV_DIM,)),
        "dt_bias_f": dt_bias(ks[5]),
        "dt_bias_b": dt_bias(ks[6]),
        "a_log_f": jnp.log(jax.random.uniform(ks[7], (SSD_HEADS,), minval=1.0, maxval=16.0)),
        "a_log_b": jnp.log(jax.random.uniform(ks[8], (SSD_HEADS,), minval=1.0, maxval=16.0)),
        "d_skip": 1.0 + 0.1 * nrm(ks[9], (SSD_HEADS,)),
        "ssd_norm_w": 1.0 + 0.1 * nrm(ks[10], (D_INNER,)),
        "w_proj_ssd": nrm(ks[11], (D_INNER, D_MODEL)) * D_INNER ** -0.5 * beta,
        "w_proj_attn": nrm(ks[12], (ATTN_OUT_WIDTH, D_MODEL)) * ATTN_OUT_WIDTH ** -0.5 * beta,
        "w_out": nrm(ks[13], (D_MODEL, D_MODEL)) * D_MODEL ** -0.5 * beta,
        "ln1_g": 1.0 + 0.1 * nrm(ks[14], (D_MODEL,)),
        "ln1_b": 0.02 * nrm(ks[15], (D_MODEL,)),
        "w_up": nrm(ks[16], (D_MODEL, D_FF)) * D_MODEL ** -0.5 * beta,
        "w_down": nrm(ks[17], (D_FF, D_MODEL)) * D_FF ** -0.5 * beta,
        "ln2_g": 1.0 + 0.1 * nrm(ks[18], (D_MODEL,)),
        "ln2_b": 0.02 * nrm(ks[19], (D_MODEL,)),
    }


def _fwd_reference(x, w_in, b_gate, conv_w, conv_b, dt_bias_f, dt_bias_b, a_log_f, a_log_b, d_skip, ssd_norm_w,
              w_proj_ssd, w_proj_attn, w_out, ln1_g, ln1_b, w_up, w_down, ln2_g, ln2_b):
    alpha = (2.0 * DEPTH) ** 0.25
    h = x
    for _ in range(DEPTH):
        bsz, s = h.shape[:2]
        u = h @ w_in
        z, xbc, dt_f_raw, dt_b_raw, q, k, v, gate_logits = jnp.split(u, list(np.cumsum(IN_SPLITS)[:-1]), axis=-1)
        y_ssd = ssd_branch(z, xbc, dt_f_raw, dt_b_raw, conv_w, conv_b, dt_bias_f, dt_bias_b,
                           a_log_f, a_log_b, d_skip, ssd_norm_w) @ w_proj_ssd
        y_att = attention_branch(q, k, v) @ w_proj_attn
        gates = jax.nn.sigmoid(gate_logits + b_gate).reshape(bsz, s, N_BRANCHES, D_MODEL)
        mix = (gates[:, :, 0] * y_ssd + gates[:, :, 1] * y_att) @ w_out
        h = layer_norm(alpha * h + mix, ln1_g, ln1_b)
        f = jnp.square(jax.nn.relu(h @ w_up)) @ w_down
        h = layer_norm(alpha * h + f, ln2_g, ln2_b)
    return h


import jax as _jax
import jax.numpy as _jnp

TWIN_FORMAT = 'train_step'
FWD_PARAMS = ['x', 'w_in', 'b_gate', 'conv_w', 'conv_b', 'dt_bias_f', 'dt_bias_b', 'a_log_f', 'a_log_b', 'd_skip', 'ssd_norm_w', 'w_proj_ssd', 'w_proj_attn', 'w_out', 'ln1_g', 'ln1_b', 'w_up', 'w_down', 'ln2_g', 'ln2_b']
TWIN_WEIGHTS = ['w_in', 'b_gate', 'conv_w', 'conv_b', 'dt_bias_f', 'dt_bias_b', 'a_log_f', 'a_log_b', 'd_skip', 'ssd_norm_w', 'w_proj_ssd', 'w_proj_attn', 'w_out', 'ln1_g', 'ln1_b', 'w_up', 'w_down', 'ln2_g', 'ln2_b']
TWIN_DIFF_INPUT = 'x'
TWIN_INPUTS = ['x', 'w_in', 'b_gate', 'conv_w', 'conv_b', 'dt_bias_f', 'dt_bias_b', 'a_log_f', 'a_log_b', 'd_skip', 'ssd_norm_w', 'w_proj_ssd', 'w_proj_attn', 'w_out', 'ln1_g', 'ln1_b', 'w_up', 'w_down', 'ln2_g', 'ln2_b', 'loss_target', 'm_w_in', 'm_b_gate', 'm_conv_w', 'm_conv_b', 'm_dt_bias_f', 'm_dt_bias_b', 'm_a_log_f', 'm_a_log_b', 'm_d_skip', 'm_ssd_norm_w', 'm_w_proj_ssd', 'm_w_proj_attn', 'm_w_out', 'm_ln1_g', 'm_ln1_b', 'm_w_up', 'm_w_down', 'm_ln2_g', 'm_ln2_b', 'v_w_in', 'v_b_gate', 'v_conv_w', 'v_conv_b', 'v_dt_bias_f', 'v_dt_bias_b', 'v_a_log_f', 'v_a_log_b', 'v_d_skip', 'v_ssd_norm_w', 'v_w_proj_ssd', 'v_w_proj_attn', 'v_w_out', 'v_ln1_g', 'v_ln1_b', 'v_w_up', 'v_w_down', 'v_ln2_g', 'v_ln2_b']
TWIN_OUTPUTS = ['loss', 'grad_x', 'grad_w_in', 'grad_b_gate', 'grad_conv_w', 'grad_conv_b', 'grad_dt_bias_f', 'grad_dt_bias_b', 'grad_a_log_f', 'grad_a_log_b', 'grad_d_skip', 'grad_ssd_norm_w', 'grad_w_proj_ssd', 'grad_w_proj_attn', 'grad_w_out', 'grad_ln1_g', 'grad_ln1_b', 'grad_w_up', 'grad_w_down', 'grad_ln2_g', 'grad_ln2_b', 'delta_w_in', 'delta_b_gate', 'delta_conv_w', 'delta_conv_b', 'delta_dt_bias_f', 'delta_dt_bias_b', 'delta_a_log_f', 'delta_a_log_b', 'delta_d_skip', 'delta_ssd_norm_w', 'delta_w_proj_ssd', 'delta_w_proj_attn', 'delta_w_out', 'delta_ln1_g', 'delta_ln1_b', 'delta_w_up', 'delta_w_down', 'delta_ln2_g', 'delta_ln2_b', 'new_m_w_in', 'new_m_b_gate', 'new_m_conv_w', 'new_m_conv_b', 'new_m_dt_bias_f', 'new_m_dt_bias_b', 'new_m_a_log_f', 'new_m_a_log_b', 'new_m_d_skip', 'new_m_ssd_norm_w', 'new_m_w_proj_ssd', 'new_m_w_proj_attn', 'new_m_w_out', 'new_m_ln1_g', 'new_m_ln1_b', 'new_m_w_up', 'new_m_w_down', 'new_m_ln2_g', 'new_m_ln2_b', 'new_v_w_in', 'new_v_b_gate', 'new_v_conv_w', 'new_v_conv_b', 'new_v_dt_bias_f', 'new_v_dt_bias_b', 'new_v_a_log_f', 'new_v_a_log_b', 'new_v_d_skip', 'new_v_ssd_norm_w', 'new_v_w_proj_ssd', 'new_v_w_proj_attn', 'new_v_w_out', 'new_v_ln1_g', 'new_v_ln1_b', 'new_v_w_up', 'new_v_w_down', 'new_v_ln2_g', 'new_v_ln2_b']
TWIN_LEAF_KINDS = {'loss': 'loss', 'grad_x': 'grad_x', 'grad_w_in': 'grad_w', 'grad_b_gate': 'grad_w', 'grad_conv_w': 'grad_w', 'grad_conv_b': 'grad_w', 'grad_dt_bias_f': 'grad_w', 'grad_dt_bias_b': 'grad_w', 'grad_a_log_f': 'grad_w', 'grad_a_log_b': 'grad_w', 'grad_d_skip': 'grad_w', 'grad_ssd_norm_w': 'grad_w', 'grad_w_proj_ssd': 'grad_w', 'grad_w_proj_attn': 'grad_w', 'grad_w_out': 'grad_w', 'grad_ln1_g': 'grad_w', 'grad_ln1_b': 'grad_w', 'grad_w_up': 'grad_w', 'grad_w_down': 'grad_w', 'grad_ln2_g': 'grad_w', 'grad_ln2_b': 'grad_w', 'delta_w_in': 'delta_w', 'delta_b_gate': 'delta_w', 'delta_conv_w': 'delta_w', 'delta_conv_b': 'delta_w', 'delta_dt_bias_f': 'delta_w', 'delta_dt_bias_b': 'delta_w', 'delta_a_log_f': 'delta_w', 'delta_a_log_b': 'delta_w', 'delta_d_skip': 'delta_w', 'delta_ssd_norm_w': 'delta_w', 'delta_w_proj_ssd': 'delta_w', 'delta_w_proj_attn': 'delta_w', 'delta_w_out': 'delta_w', 'delta_ln1_g': 'delta_w', 'delta_ln1_b': 'delta_w', 'delta_w_up': 'delta_w', 'delta_w_down': 'delta_w', 'delta_ln2_g': 'delta_w', 'delta_ln2_b': 'delta_w', 'new_m_w_in': 'new_m', 'new_m_b_gate': 'new_m', 'new_m_conv_w': 'new_m', 'new_m_conv_b': 'new_m', 'new_m_dt_bias_f': 'new_m', 'new_m_dt_bias_b': 'new_m', 'new_m_a_log_f': 'new_m', 'new_m_a_log_b': 'new_m', 'new_m_d_skip': 'new_m', 'new_m_ssd_norm_w': 'new_m', 'new_m_w_proj_ssd': 'new_m', 'new_m_w_proj_attn': 'new_m', 'new_m_w_out': 'new_m', 'new_m_ln1_g': 'new_m', 'new_m_ln1_b': 'new_m', 'new_m_w_up': 'new_m', 'new_m_w_down': 'new_m', 'new_m_ln2_g': 'new_m', 'new_m_ln2_b': 'new_m', 'new_v_w_in': 'new_v', 'new_v_b_gate': 'new_v', 'new_v_conv_w': 'new_v', 'new_v_conv_b': 'new_v', 'new_v_dt_bias_f': 'new_v', 'new_v_dt_bias_b': 'new_v', 'new_v_a_log_f': 'new_v', 'new_v_a_log_b': 'new_v', 'new_v_d_skip': 'new_v', 'new_v_ssd_norm_w': 'new_v', 'new_v_w_proj_ssd': 'new_v', 'new_v_w_proj_attn': 'new_v', 'new_v_w_out': 'new_v', 'new_v_ln1_g': 'new_v', 'new_v_ln1_b': 'new_v', 'new_v_w_up': 'new_v', 'new_v_w_down': 'new_v', 'new_v_ln2_g': 'new_v', 'new_v_ln2_b': 'new_v'}


def _forward(args):
    return _fwd_reference(*[args[k] for k in FWD_PARAMS])


def _output_shape():
    out = _jax.eval_shape(lambda: _forward(_fwd_setup_inputs(0)))
    return out.shape, out.dtype

N_MICROBATCH = 1
ADAM_LR = 0.001
ADAM_B1 = 0.9
ADAM_B2 = 0.999
ADAM_EPS = 1e-08
ADAM_WD = 0.01
ADAM_STEP = 10
PER_EXAMPLE_BATCH_AXIS = {'x': 0, 'loss_target': 0}
SHARED_INPUTS = []
_WEIGHT_DTYPES = {'w_in': _jnp.float32, 'b_gate': _jnp.float32, 'conv_w': _jnp.float32, 'conv_b': _jnp.float32, 'dt_bias_f': _jnp.float32, 'dt_bias_b': _jnp.float32, 'a_log_f': _jnp.float32, 'a_log_b': _jnp.float32, 'd_skip': _jnp.float32, 'ssd_norm_w': _jnp.float32, 'w_proj_ssd': _jnp.float32, 'w_proj_attn': _jnp.float32, 'w_out': _jnp.float32, 'ln1_g': _jnp.float32, 'ln1_b': _jnp.float32, 'w_up': _jnp.float32, 'w_down': _jnp.float32, 'ln2_g': _jnp.float32, 'ln2_b': _jnp.float32}
MOMENT_SCALE = {'w_in': 2.185680e-02, 'b_gate': 1.435889e-02, 'conv_w': 2.619591e-02, 'conv_b': 5.131956e-02, 'dt_bias_f': 4.798931e-02, 'dt_bias_b': 4.598499e-02, 'a_log_f': 1.076622e-01, 'a_log_b': 9.876688e-02, 'd_skip': 9.789241e-02, 'ssd_norm_w': 3.861641e-02, 'w_proj_ssd': 8.293135e-02, 'w_proj_attn': 1.678349e-02, 'w_out': 8.348510e-02, 'ln1_g': 1.190623e+01, 'ln1_b': 1.030002e+00, 'w_up': 5.663964e-02, 'w_down': 1.519413e-01, 'ln2_g': 6.531958e+01, 'ln2_b': 5.430254e+00}


def _to_microbatches(a, axis):
    t = _jnp.moveaxis(a, axis, 0)
    t = t.reshape((N_MICROBATCH, t.shape[0] // N_MICROBATCH) + t.shape[1:])
    return _jnp.moveaxis(t, 1, axis + 1)


def setup_inputs(seed: int = 0) -> dict:
    inp = _fwd_setup_inputs(seed)
    key = _jax.random.fold_in(_jax.random.key(seed), 7919)
    shape, _ = _output_shape()
    out = dict(inp)
    out["loss_target"] = _jax.random.normal(_jax.random.fold_in(key, 0), shape, _jnp.float32)
    for i, name in enumerate(TWIN_WEIGHTS):
        w = inp[name].astype(_jnp.float32)
        if MOMENT_SCALE is None:
            s = _jnp.sqrt(_jnp.mean(_jnp.square(w)) + 1e-30)
        else:
            s = MOMENT_SCALE[name]
        km, kv = _jax.random.split(_jax.random.fold_in(key, i + 1))
        out[name] = w
        out["m_" + name] = s * _jax.random.normal(km, w.shape, _jnp.float32)
        out["v_" + name] = (s * s) * _jax.random.uniform(kv, w.shape, _jnp.float32, 0.5, 1.5)
    if N_MICROBATCH > 1:
        for name, axis in PER_EXAMPLE_BATCH_AXIS.items():
            out[name] = _to_microbatches(out[name], axis)
    return {'x': out['x'], 'w_in': out['w_in'], 'b_gate': out['b_gate'], 'conv_w': out['conv_w'], 'conv_b': out['conv_b'], 'dt_bias_f': out['dt_bias_f'], 'dt_bias_b': out['dt_bias_b'], 'a_log_f': out['a_log_f'], 'a_log_b': out['a_log_b'], 'd_skip': out['d_skip'], 'ssd_norm_w': out['ssd_norm_w'], 'w_proj_ssd': out['w_proj_ssd'], 'w_proj_attn': out['w_proj_attn'], 'w_out': out['w_out'], 'ln1_g': out['ln1_g'], 'ln1_b': out['ln1_b'], 'w_up': out['w_up'], 'w_down': out['w_down'], 'ln2_g': out['ln2_g'], 'ln2_b': out['ln2_b'], 'loss_target': out['loss_target'], 'm_w_in': out['m_w_in'], 'm_b_gate': out['m_b_gate'], 'm_conv_w': out['m_conv_w'], 'm_conv_b': out['m_conv_b'], 'm_dt_bias_f': out['m_dt_bias_f'], 'm_dt_bias_b': out['m_dt_bias_b'], 'm_a_log_f': out['m_a_log_f'], 'm_a_log_b': out['m_a_log_b'], 'm_d_skip': out['m_d_skip'], 'm_ssd_norm_w': out['m_ssd_norm_w'], 'm_w_proj_ssd': out['m_w_proj_ssd'], 'm_w_proj_attn': out['m_w_proj_attn'], 'm_w_out': out['m_w_out'], 'm_ln1_g': out['m_ln1_g'], 'm_ln1_b': out['m_ln1_b'], 'm_w_up': out['m_w_up'], 'm_w_down': out['m_w_down'], 'm_ln2_g': out['m_ln2_g'], 'm_ln2_b': out['m_ln2_b'], 'v_w_in': out['v_w_in'], 'v_b_gate': out['v_b_gate'], 'v_conv_w': out['v_conv_w'], 'v_conv_b': out['v_conv_b'], 'v_dt_bias_f': out['v_dt_bias_f'], 'v_dt_bias_b': out['v_dt_bias_b'], 'v_a_log_f': out['v_a_log_f'], 'v_a_log_b': out['v_a_log_b'], 'v_d_skip': out['v_d_skip'], 'v_ssd_norm_w': out['v_ssd_norm_w'], 'v_w_proj_ssd': out['v_w_proj_ssd'], 'v_w_proj_attn': out['v_w_proj_attn'], 'v_w_out': out['v_w_out'], 'v_ln1_g': out['v_ln1_g'], 'v_ln1_b': out['v_ln1_b'], 'v_w_up': out['v_w_up'], 'v_w_down': out['v_w_down'], 'v_ln2_g': out['v_ln2_g'], 'v_ln2_b': out['v_ln2_b']}


def _loss(weights, diff, rest, loss_target):
    with _jax.named_scope("forward"):
        args = {**rest, TWIN_DIFF_INPUT: diff, **{k: w.astype(_WEIGHT_DTYPES[k]) for k, w in weights.items()}}
        y = _forward(args)
    with _jax.named_scope("loss_head"):
        err = _jnp.square(y.astype(_jnp.float32) - loss_target)
        return 0.5 * _jnp.sum(_jnp.mean(err, axis=-1)) if err.ndim else 0.5 * err


def _adamw(w, g, m, v):
    m = ADAM_B1 * m + (1.0 - ADAM_B1) * g
    v = ADAM_B2 * v + (1.0 - ADAM_B2) * _jnp.square(g)
    m_hat = m / (1.0 - ADAM_B1 ** ADAM_STEP)
    v_hat = v / (1.0 - ADAM_B2 ** ADAM_STEP)
    delta = -ADAM_LR * (m_hat / (_jnp.sqrt(v_hat) + ADAM_EPS) + ADAM_WD * w)
    return delta, m, v


def reference(x, w_in, b_gate, conv_w, conv_b, dt_bias_f, dt_bias_b, a_log_f, a_log_b, d_skip, ssd_norm_w, w_proj_ssd, w_proj_attn, w_out, ln1_g, ln1_b, w_up, w_down, ln2_g, ln2_b, loss_target, m_w_in, m_b_gate, m_conv_w, m_conv_b, m_dt_bias_f, m_dt_bias_b, m_a_log_f, m_a_log_b, m_d_skip, m_ssd_norm_w, m_w_proj_ssd, m_w_proj_attn, m_w_out, m_ln1_g, m_ln1_b, m_w_up, m_w_down, m_ln2_g, m_ln2_b, v_w_in, v_b_gate, v_conv_w, v_conv_b, v_dt_bias_f, v_dt_bias_b, v_a_log_f, v_a_log_b, v_d_skip, v_ssd_norm_w, v_w_proj_ssd, v_w_proj_attn, v_w_out, v_ln1_g, v_ln1_b, v_w_up, v_w_down, v_ln2_g, v_ln2_b):
    given = dict(x=x, w_in=w_in, b_gate=b_gate, conv_w=conv_w, conv_b=conv_b, dt_bias_f=dt_bias_f, dt_bias_b=dt_bias_b, a_log_f=a_log_f, a_log_b=a_log_b, d_skip=d_skip, ssd_norm_w=ssd_norm_w, w_proj_ssd=w_proj_ssd, w_proj_attn=w_proj_attn, w_out=w_out, ln1_g=ln1_g, ln1_b=ln1_b, w_up=w_up, w_down=w_down, ln2_g=ln2_g, ln2_b=ln2_b, loss_target=loss_target, m_w_in=m_w_in, m_b_gate=m_b_gate, m_conv_w=m_conv_w, m_conv_b=m_conv_b, m_dt_bias_f=m_dt_bias_f, m_dt_bias_b=m_dt_bias_b, m_a_log_f=m_a_log_f, m_a_log_b=m_a_log_b, m_d_skip=m_d_skip, m_ssd_norm_w=m_ssd_norm_w, m_w_proj_ssd=m_w_proj_ssd, m_w_proj_attn=m_w_proj_attn, m_w_out=m_w_out, m_ln1_g=m_ln1_g, m_ln1_b=m_ln1_b, m_w_up=m_w_up, m_w_down=m_w_down, m_ln2_g=m_ln2_g, m_ln2_b=m_ln2_b, v_w_in=v_w_in, v_b_gate=v_b_gate, v_conv_w=v_conv_w, v_conv_b=v_conv_b, v_dt_bias_f=v_dt_bias_f, v_dt_bias_b=v_dt_bias_b, v_a_log_f=v_a_log_f, v_a_log_b=v_a_log_b, v_d_skip=v_d_skip, v_ssd_norm_w=v_ssd_norm_w, v_w_proj_ssd=v_w_proj_ssd, v_w_proj_attn=v_w_proj_attn, v_w_out=v_w_out, v_ln1_g=v_ln1_g, v_ln1_b=v_ln1_b, v_w_up=v_w_up, v_w_down=v_w_down, v_ln2_g=v_ln2_g, v_ln2_b=v_ln2_b)
    weights = {n: given[n] for n in TWIN_WEIGHTS}
    shared = {n: given[n] for n in SHARED_INPUTS}
    per_example = {n: given[n] for n in ['x']}
    grad_fn = _jax.value_and_grad(_loss, argnums=(0, 1))

    def one_microbatch(ex, loss_target):
        ex = dict(ex)
        diff = ex.pop(TWIN_DIFF_INPUT)
        return grad_fn(weights, diff, {**shared, **ex}, loss_target)

    if N_MICROBATCH == 1:
        loss, (grad_w, grad_x) = one_microbatch(per_example, given["loss_target"])
    else:
        def body(carry, xs):
            loss_sum, grad_sum = carry
            l_k, (gw_k, gx_k) = one_microbatch(xs[0], xs[1])
            with _jax.named_scope("update"):
                return (loss_sum + l_k, _jax.tree.map(_jnp.add, grad_sum, gw_k)), gx_k

        init = (_jnp.zeros((), _jnp.float32), _jax.tree.map(_jnp.zeros_like, weights))
        (loss, grad_w), grad_x = _jax.lax.scan(body, init, (per_example, given["loss_target"]))
    with _jax.named_scope("update"):
        delta_w, new_m, new_v = {}, {}, {}
        for n in TWIN_WEIGHTS:
            delta_w[n], new_m[n], new_v[n] = _adamw(weights[n], grad_w[n], given["m_" + n], given["v_" + n])
    return (loss, grad_x, *[grad_w[n] for n in TWIN_WEIGHTS], *[delta_w[n] for n in TWIN_WEIGHTS],
            *[new_m[n] for n in TWIN_WEIGHTS], *[new_v[n] for n in TWIN_WEIGHTS])
```

```python
import math

import jax
import jax.numpy as jnp
from jax import lax
from jax.experimental import pallas as pl
from jax.experimental.pallas import tpu as pltpu

F32, BF16 = jnp.float32, jnp.bfloat16
MESH = pl.DeviceIdType.MESH

D = 1024
DI = 2048
NH = 32
HP = 64
NG = 4
NS = 128
Q = 128
CONVD = 3072
KCONV = 5
DFF = 4096
AH = 64
ATT_HALF = 64
DILATIONS = (1, 4, 16)
IN_COLS = 9536
OZ, OGATE, OXBC, OKV, OQ, ODT, UW = 0, 2048, 4096, 7168, 8704, 9472, 9728
ALPHA = 2.0 ** 0.25
NORM_EPS = 1e-5
ADAM_LR, ADAM_B1, ADAM_B2, ADAM_EPS, ADAM_WD, ADAM_STEP = 0.001, 0.9, 0.999, 1e-8, 0.01, 10
VMEM_LIMIT = 56 * 2 ** 20
NEG = -1e30


def _params(sem):
    return pltpu.CompilerParams(dimension_semantics=sem, vmem_limit_bytes=VMEM_LIMIT)


def _sigmoid(x):
    return 1.0 / (1.0 + jnp.exp(-x))


def _softplus(x):
    e = jnp.exp(-jnp.abs(x))
    small = e * (1.0 - e * (0.5 - e * (1.0 / 3.0)))
    return jnp.maximum(x, 0.0) + jnp.where(e < 0.01, small, jnp.log(1.0 + e))


def _split3(a):
    hi = a.astype(BF16)
    r = a - hi.astype(F32)
    mid = r.astype(BF16)
    lo = (r - mid.astype(F32)).astype(BF16)
    return hi, mid, lo


def _dot01(a, m01):
    hi, mid, lo = _split3(a)
    d = lambda p: jnp.dot(p, m01, preferred_element_type=F32)
    return d(hi) + d(mid) + d(lo)


def _dot01_l(m01, a):
    hi, mid, lo = _split3(a)
    d = lambda p: jnp.dot(m01, p, preferred_element_type=F32)
    return d(hi) + d(mid) + d(lo)


def _dot_nt(a, b):
    return lax.dot_general(a, b, (((1,), (1,)), ((), ())), preferred_element_type=F32)


def _iota(shape, dim):
    return lax.broadcasted_iota(jnp.int32, shape, dim)


def _mm_nn(a, b, *, tm, tn, name, out_dtype=F32):
    m, k = a.shape
    if b.ndim == 3:
        assert tn == b.shape[2]
        n = b.shape[0] * b.shape[2]
        b_spec = pl.BlockSpec((None, k, tn), lambda j, i: (j, 0, 0))
    else:
        n = b.shape[1]
        b_spec = pl.BlockSpec((k, tn), lambda j, i: (0, j))

    def body(a_ref, b_ref, o_ref):
        o_ref[...] = jnp.dot(a_ref[...].astype(BF16), b_ref[...], preferred_element_type=F32).astype(out_dtype)

    return pl.pallas_call(
        body, out_shape=jax.ShapeDtypeStruct((m, n), out_dtype), grid=(n // tn, m // tm),
        in_specs=[pl.BlockSpec((tm, k), lambda j, i: (i, 0)), b_spec],
        out_specs=pl.BlockSpec((tm, tn), lambda j, i: (i, j)),
        name=name, compiler_params=_params(("parallel", "parallel")))(a, b)


def _mm_nt(a, b, *, tm, tk, tc, name, add=None, add_scale=1.0):
    m, n = a.shape
    if b.ndim == 3:
        assert tc == b.shape[2]
        k, nc = b.shape[1], b.shape[0]
        b_spec = pl.BlockSpec((None, tk, tc), lambda j, i, c: (c, j, 0))
    else:
        k, nc = b.shape[0], n // tc
        b_spec = pl.BlockSpec((tk, tc), lambda j, i, c: (j, c))

    def body(*refs):
        if add is None:
            a_ref, b_ref, o_ref = refs
        else:
            a_ref, b_ref, add_ref, o_ref = refs
        c = pl.program_id(2)
        part = _dot_nt(a_ref[...].astype(BF16), b_ref[...])

        @pl.when(c == 0)
        def _():
            if add is None:
                o_ref[...] = part
            else:
                o_ref[...] = part + add_scale * add_ref[...]

        @pl.when(c > 0)
        def _():
            o_ref[...] += part

    in_specs = [pl.BlockSpec((tm, tc), lambda j, i, c: (i, c)), b_spec]
    args = [a, b]
    if add is not None:
        in_specs.append(pl.BlockSpec((tm, tk), lambda j, i, c: (i, j)))
        args.append(add)
    return pl.pallas_call(
        body, out_shape=jax.ShapeDtypeStruct((m, k), F32), grid=(k // tk, m // tm, nc),
        in_specs=in_specs, out_specs=pl.BlockSpec((tm, tk), lambda j, i, c: (i, j)),
        name=name, compiler_params=_params(("parallel", "parallel", "arbitrary")))(*args)


def _mm_tn(a, b, *, tka, tn, tt, name, out_shards=None):
    t, ka = a.shape
    n = b.shape[1]
    if out_shards:
        assert tn == n // out_shards
        out_shape = jax.ShapeDtypeStruct((out_shards, ka, tn), F32)
        o_spec = pl.BlockSpec((None, tka, tn), lambda i, j, s: (j, i, 0))
    else:
        out_shape = jax.ShapeDtypeStruct((ka, n), F32)
        o_spec = pl.BlockSpec((tka, tn), lambda i, j, s: (i, j))

    def body(a_ref, b_ref, o_ref):
        s = pl.program_id(2)
        part = lax.dot_general(a_ref[...].astype(BF16), b_ref[...].astype(BF16), (((0,), (0,)), ((), ())),
                               preferred_element_type=F32)

        @pl.when(s == 0)
        def _():
            o_ref[...] = part

        @pl.when(s > 0)
        def _():
            o_ref[...] += part

    return pl.pallas_call(
        body, out_shape=out_shape, grid=(ka // tka, n // tn, t // tt),
        in_specs=[pl.BlockSpec((tt, tka), lambda i, j, s: (s, i)), pl.BlockSpec((tt, tn), lambda i, j, s: (s, j))],
        out_specs=o_spec, name=name, compiler_params=_params(("parallel", "parallel", "arbitrary")))(a, b)


CONV_TM = 512
CONV_TC = 1024


def _halo_specs(t, tm, tc, col0):
    nb8 = t // 8
    r8 = tm // 8
    return [
        pl.BlockSpec((8, tc), lambda i, j: (jnp.maximum(i * r8 - 1, 0), col0 + j)),
        pl.BlockSpec((tm, tc), lambda i, j: (i, col0 + j)),
        pl.BlockSpec((8, tc), lambda i, j: (jnp.minimum((i + 1) * r8, nb8 - 1), col0 + j)),
    ]


def _fill_ext(ext, prev_ref, cur_ref, next_ref, tm, i, last):
    ext[0:8, :] = jnp.where(i > 0, prev_ref[...], 0.0)
    ext[8:8 + tm, :] = cur_ref[...]
    ext[8 + tm:16 + tm, :] = jnp.where(i < last, next_ref[...], 0.0)


def _conv_fwd(u, conv_w, conv_b):
    t = u.shape[0]
    tm, tc = CONV_TM, CONV_TC

    def body(prev_ref, cur_ref, next_ref, w_ref, b_ref, o_ref, ext):
        _fill_ext(ext, prev_ref, cur_ref, next_ref, tm, pl.program_id(0), t // tm - 1)
        acc = jnp.broadcast_to(b_ref[...], (tm, tc))
        for k in range(KCONV):
            acc = acc + w_ref[k:k + 1, :] * ext[pl.ds(6 + k, tm), :]
        o_ref[...] = acc * _sigmoid(acc)

    return pl.pallas_call(
        body, out_shape=jax.ShapeDtypeStruct((t, CONVD), F32), grid=(t // tm, CONVD // tc),
        in_specs=_halo_specs(t, tm, tc, OXBC // tc) + [
            pl.BlockSpec((KCONV, tc), lambda i, j: (0, j)), pl.BlockSpec((1, tc), lambda i, j: (0, j))],
        out_specs=pl.BlockSpec((tm, tc), lambda i, j: (i, j)),
        scratch_shapes=[pltpu.VMEM((tm + 16, tc), F32)],
        name="conv_fwd", compiler_params=_params(("parallel", "parallel")))(u, u, u, conv_w, conv_b)


def _conv_dpre(u, dxs_f, dxs_b, dy, dbc_f, dbc_b, dsk_row, conv_w, conv_b):
    t = u.shape[0]
    tm, tc = CONV_TM, CONV_TC
    r8 = tm // 8
    nb8 = t // 8
    c0 = OXBC // tc

    def body(uprev, ucur, unext, f_ref, b_ref, y_ref, cf_ref, cb_ref, dsk_ref, w_ref, bias_ref,
             dpre_ref, dw_ref, db_ref, ext):
        j = pl.program_id(0)
        i = pl.program_id(1)
        _fill_ext(ext, uprev, ucur, unext, tm, i, t // tm - 1)
        pre = jnp.broadcast_to(bias_ref[...], (tm, tc))
        for k in range(KCONV):
            pre = pre + w_ref[k:k + 1, :] * ext[pl.ds(6 + k, tm), :]
        s = _sigmoid(pre)
        xs_part = f_ref[...] + b_ref[...] + dsk_ref[...] * y_ref[...]
        up = jnp.where(j < 2, xs_part, cf_ref[...] + cb_ref[...])
        dpre = up * (s * (1.0 + pre * (1.0 - s)))
        dpre_ref[...] = dpre
        rows = [jnp.sum(dpre * ext[pl.ds(6 + k, tm), :], axis=0, keepdims=True) for k in range(KCONV)]
        rows += [jnp.zeros((1, tc), F32)] * (8 - KCONV)
        dw_part = jnp.concatenate(rows, axis=0)
        db_part = jnp.broadcast_to(jnp.sum(dpre, axis=0, keepdims=True), (8, tc))

        @pl.when(i == 0)
        def _():
            dw_ref[...] = dw_part
            db_ref[...] = db_part

        @pl.when(i > 0)
        def _():
            dw_ref[...] += dw_part
            db_ref[...] += db_part

    xs_spec = pl.BlockSpec((tm, tc), lambda j, i: (i, jnp.minimum(j, 1)))
    bc_spec = pl.BlockSpec((tm, tc), lambda j, i: (i, 0))
    in_specs = [
        pl.BlockSpec((8, tc), lambda j, i: (jnp.maximum(i * r8 - 1, 0), c0 + j)),
        pl.BlockSpec((tm, tc), lambda j, i: (i, c0 + j)),
        pl.BlockSpec((8, tc), lambda j, i: (jnp.minimum((i + 1) * r8, nb8 - 1), c0 + j)),
        xs_spec, xs_spec, xs_spec, bc_spec, bc_spec,
        pl.BlockSpec((1, tc), lambda j, i: (0, jnp.minimum(j, 1))),
        pl.BlockSpec((KCONV, tc), lambda j, i: (0, j)), pl.BlockSpec((1, tc), lambda j, i: (0, j)),
    ]
    return pl.pallas_call(
        body,
        out_shape=(jax.ShapeDtypeStruct((t, CONVD), F32), jax.ShapeDtypeStruct((8, CONVD), F32),
                   jax.ShapeDtypeStruct((8, CONVD), F32)),
        grid=(CONVD // tc, t // tm), in_specs=in_specs,
        out_specs=(pl.BlockSpec((tm, tc), lambda j, i: (i, j)),
                   pl.BlockSpec((8, tc), lambda j, i: (0, j)), pl.BlockSpec((8, tc), lambda j, i: (0, j))),
        scratch_shapes=[pltpu.VMEM((tm + 16, tc), F32)],
        name="conv_dpre", compiler_params=_params(("parallel", "arbitrary")))(
            u, u, u, dxs_f, dxs_b, dy, dbc_f, dbc_b, dsk_row, conv_w, conv_b)


def _conv_dx(du, dpre, conv_w):
    t = dpre.shape[0]
    tm, tc = CONV_TM, CONV_TC
    r8 = tm // 8
    nb8 = t // 8

    def body(prev_ref, cur_ref, next_ref, w_ref, du_in, du_out, ext):
        del du_in
        _fill_ext(ext, prev_ref, cur_ref, next_ref, tm, pl.program_id(1), t // tm - 1)
        acc = jnp.zeros((tm, tc), F32)
        for k in range(KCONV):
            acc = acc + w_ref[k:k + 1, :] * ext[pl.ds(10 - k, tm), :]
        du_out[...] = acc

    in_specs = [
        pl.BlockSpec((8, tc), lambda j, i: (jnp.maximum(i * r8 - 1, 0), j)),
        pl.BlockSpec((tm, tc), lambda j, i: (i, j)),
        pl.BlockSpec((8, tc), lambda j, i: (jnp.minimum((i + 1) * r8, nb8 - 1), j)),
        pl.BlockSpec((KCONV, tc), lambda j, i: (0, j)),
        pl.BlockSpec(memory_space=pl.ANY),
    ]
    return pl.pallas_call(
        body, out_shape=jax.ShapeDtypeStruct(du.shape, F32), grid=(CONVD // tc, t // tm), in_specs=in_specs,
        out_specs=pl.BlockSpec((tm, tc), lambda j, i: (i, OXBC // tc + j)),
        scratch_shapes=[pltpu.VMEM((tm + 16, tc), F32)], input_output_aliases={4: 0},
        name="conv_dx", compiler_params=_params(("parallel", "parallel")))(dpre, dpre, dpre, conv_w, du)


def _ssd_common(dtr_ref, par_ref, rev):
    raw = dtr_ref[...]
    lane = _iota((1, 128), 1)
    mine = (lane >= 32 * rev) & (lane < 32 * rev + 32)
    bias = par_ref[0:1, :]
    arow = jnp.where(mine, -jnp.exp(par_ref[1:2, :]), 0.0)
    dt = _softplus(raw + bias)
    a = dt * arow
    ri = _iota((Q, Q), 0)
    ci = _iota((Q, Q), 1)
    tri = (ci >= ri) if rev else (ci <= ri)
    trit = (ci <= ri) if rev else (ci >= ri)
    cs = _dot01_l(tri.astype(BF16), a)
    return raw, bias, arow, mine, dt, cs, tri, trit


def _expand_mat(rev):
    r = _iota((128, DI), 0)
    c = _iota((128, DI), 1)
    return (r == (c // HP) + 32 * rev).astype(BF16)


def _sum_mat(rev):
    r = _iota((DI, 128), 0)
    c = _iota((DI, 128), 1)
    return (c == (r // HP) + 32 * rev).astype(BF16)


def _ssd_fwd(xbc, u, par, *, rev):
    t = xbc.shape[0]
    nc = t // Q
    end = 0 if rev else Q - 1
    cmap = (lambda c: nc - 1 - c) if rev else (lambda c: c)

    def body(xbc_ref, dtr_ref, par_ref, y_ref, st_ref, h_scr):
        step = pl.program_id(0)

        @pl.when(step == 0)
        def _():
            h_scr[...] = jnp.zeros((NS, DI), F32)

        raw, bias, arow, mine, dt, cs, tri, trit = _ssd_common(dtr_ref, par_ref, rev)
        cst = cs.T
        dtt = dt.T
        tot_col = cst[:, end:end + 1]
        wt = dtt * jnp.exp(tot_col - cst)
        gam = jnp.exp(cs[end:end + 1, :])
        gam_x = _dot01(jnp.broadcast_to(gam, (8, 128)), _expand_mat(rev))[0:1, :]
        lane = _iota((Q, 128), 1)
        sel = lane < HP
        st_ref[...] = h_scr[...]
        for g in range(NG):
            bg = xbc_ref[:, DI + NS * g:DI + NS * (g + 1)]
            cg = xbc_ref[:, DI + NG * NS + NS * g:DI + NG * NS + NS * (g + 1)]
            cb = _dot_nt(cg.astype(BF16), bg.astype(BF16))
            bt = bg.T
            for k in range(4):
                lo = 512 * g + 128 * k
                xp = xbc_ref[:, lo:lo + 128].astype(BF16)
                hp = h_scr[:, lo:lo + 128]
                rhs = jnp.concatenate([xp, hp.astype(BF16)], axis=0)
                ys, ss = [], []
                for j in range(2):
                    hc = 8 * g + 2 * k + j + 32 * rev
                    csc = jnp.broadcast_to(cs[:, hc:hc + 1], (Q, Q))
                    lm = jnp.exp(jnp.where(tri, csc - cst[hc:hc + 1, :], NEG)) * dtt[hc:hc + 1, :]
                    mh = (cb * lm).astype(BF16)
                    ec = (jnp.exp(csc) * cg).astype(BF16)
                    lhs = jnp.concatenate([mh, ec], axis=1)
                    ys.append(jnp.dot(lhs, rhs, preferred_element_type=F32))
                    bts = (bt * wt[hc:hc + 1, :]).astype(BF16)
                    ss.append(jnp.dot(bts, xp, preferred_element_type=F32))
                y_ref[:, lo:lo + 128] = jnp.where(sel, ys[0], ys[1])
                h_scr[:, lo:lo + 128] = gam_x[:, lo:lo + 128] * hp + jnp.where(sel, ss[0], ss[1])

    return pl.pallas_call(
        body,
        out_shape=(jax.ShapeDtypeStruct((t, DI), F32), jax.ShapeDtypeStruct((nc, NS, DI), F32)),
        grid=(nc,),
        in_specs=[pl.BlockSpec((Q, CONVD), lambda c: (cmap(c), 0)),
                  pl.BlockSpec((Q, 128), lambda c: (cmap(c), ODT // 128)),
                  pl.BlockSpec((8, 128), lambda c: (0, 0))],
        out_specs=(pl.BlockSpec((Q, DI), lambda c: (cmap(c), 0)),
                   pl.BlockSpec((None, NS, DI), lambda c: (cmap(c), 0, 0))),
        scratch_shapes=[pltpu.VMEM((NS, DI), F32)],
        name="ssd_fwd_rev" if rev else "ssd_fwd", compiler_params=_params(("arbitrary",)))(xbc, u, par)


def _ssd_bwd(xbc, u, par, dy, st, *, rev):
    t = xbc.shape[0]
    nc = t // Q
    end = 0 if rev else Q - 1
    cmap = (lambda c: c) if rev else (lambda c: nc - 1 - c)

    def body(xbc_ref, dtr_ref, par_ref, dy_ref, hin_ref, dxs_ref, dbc_ref, ddt_ref, acc_ref, dh_scr):
        step = pl.program_id(0)

        @pl.when(step == 0)
        def _():
            dh_scr[...] = jnp.zeros((NS, DI), F32)

        raw, bias, arow, mine, dt, cs, tri, trit = _ssd_common(dtr_ref, par_ref, rev)
        ri = _iota((Q, Q), 0)
        ci = _iota((Q, Q), 1)
        stri = ((ri > ci) if rev else (ri < ci)).astype(BF16)
        strit = ((ci > ri) if rev else (ci < ri)).astype(BF16)
        cst = cs.T
        dtt = dt.T
        et = jnp.exp(cst)
        expand = _expand_mat(rev)
        summat = _sum_mat(rev)
        gam = jnp.exp(cs[end:end + 1, :])
        gam_x = _dot01(jnp.broadcast_to(gam, (8, 128)), expand)[0:1, :]
        dtx = _dot01(dt, expand)
        lane = _iota((Q, 128), 1)
        sel = lane < HP
        dho = dh_scr[...]
        t3 = jnp.sum(dho * hin_ref[...], axis=0, keepdims=True) * gam_x
        dxs_cols, dxs2_cols, yoff_cols, a1_rows = [], [], [], []
        for g in range(NG):
            bg = xbc_ref[:, DI + NS * g:DI + NS * (g + 1)]
            cg = xbc_ref[:, DI + NG * NS + NS * g:DI + NG * NS + NS * (g + 1)]
            bb = bg.astype(BF16)
            cbf = cg.astype(BF16)
            cb = _dot_nt(cbf, bb)
            cbt = _dot_nt(bb, cbf)
            ct = cg.T
            bdh = jnp.dot(bb, dho[:, 512 * g:512 * (g + 1)].astype(BF16), preferred_element_type=F32)
            dcb = jnp.zeros((Q, Q), F32)
            dcg = jnp.zeros((Q, NS), F32)
            dbg = jnp.zeros((Q, NS), F32)
            for k in range(4):
                lo = 512 * g + 128 * k
                xpf = xbc_ref[:, lo:lo + 128]
                xp = xpf.astype(BF16)
                dyp = dy_ref[:, lo:lo + 128]
                dypb = dyp.astype(BF16)
                hinp = hin_ref[:, lo:lo + 128].astype(BF16)
                dhp = dho[:, lo:lo + 128]
                d1, es, ws, dhs, yo = [], [], [], [], []
                for j in range(2):
                    hc = 8 * g + 2 * k + j + 32 * rev
                    csc = jnp.broadcast_to(cs[:, hc:hc + 1], (Q, Q))
                    csr = cst[hc:hc + 1, :]
                    lmd = jnp.exp(jnp.where(tri, csc - csr, NEG)) * dtt[hc:hc + 1, :]
                    lmb = jnp.exp(jnp.where(trit, csr - csc, NEG))
                    mt = (cbt * lmb).astype(BF16)
                    d1.append(jnp.dot(mt, dypb, preferred_element_type=F32))
                    dym = jnp.where(sel if j == 0 else ~sel, dyp, 0.0).astype(BF16)
                    dm = _dot_nt(dym, xp) * lmd
                    dcb = dcb + dm
                    gh, gm, _ = _split3(dm * cb)
                    rr = jnp.dot(gh, stri, preferred_element_type=F32) + jnp.dot(gm, stri, preferred_element_type=F32)
                    a1_rows.append(jnp.sum(jnp.where(tri, rr, 0.0), axis=0, keepdims=True))
                    ecs = jnp.exp(csc)
                    es.append(ecs)
                    ws.append(jnp.exp(cst[hc:hc + 1, end:end + 1] - csc))
                    yo.append(jnp.dot((ecs * cg).astype(BF16), hinp, preferred_element_type=F32))
                    cte = (ct * et[hc:hc + 1, :]).astype(BF16)
                    dhs.append(jnp.dot(cte, dypb, preferred_element_type=F32))
                e_p = jnp.where(sel, es[0], es[1])
                w_p = jnp.where(sel, ws[0], ws[1])
                d2 = w_p * bdh[:, 128 * k:128 * (k + 1)]
                dxs2_cols.append(d2)
                dxs_cols.append(jnp.where(sel, d1[0], d1[1]) + d2)
                yoff_cols.append(jnp.where(sel, yo[0], yo[1]))
                dcg = dcg + _dot_nt((e_p * dyp).astype(BF16), hinp)
                dbg = dbg + _dot_nt((w_p * dtx[:, lo:lo + 128] * xpf).astype(BF16), dhp.astype(BF16))
                dh_scr[:, lo:lo + 128] = gam_x[:, lo:lo + 128] * dhp + jnp.where(sel, dhs[0], dhs[1])
            dcg = dcg + jnp.dot(dcb.astype(BF16), bb, preferred_element_type=F32)
            dbg = dbg + jnp.dot(dcb.T.astype(BF16), cbf, preferred_element_type=F32)
            dbc_ref[:, NS * g:NS * (g + 1)] = dbg
            dbc_ref[:, NG * NS + NS * g:NG * NS + NS * (g + 1)] = dcg
        dxs = jnp.concatenate(dxs_cols, axis=1)
        dxs_ref[...] = dxs * dtx
        xs = xbc_ref[:, 0:DI]
        rx = _dot01(xs * dxs, summat)
        rx2 = _dot01(xs * jnp.concatenate(dxs2_cols, axis=1), summat)
        ryo = _dot01(dy_ref[...] * jnp.concatenate(yoff_cols, axis=1), summat)
        c0 = _dot01(jnp.broadcast_to(t3, (8, DI)), summat)[0:1, :]
        zero32 = jnp.zeros((32, Q), F32)
        a1t = jnp.concatenate(([zero32] if rev else []) + a1_rows + [zero32] * (2 if rev else 3), axis=0)
        da = a1t.T + _dot01_l(trit.astype(BF16), ryo) + _dot01_l(strit, dt * rx2) + jnp.where(mine, c0, 0.0)
        ddt = rx + da * arow
        ddtr = ddt * _sigmoid(raw + bias)
        ddt_ref[...] = ddtr
        part = jnp.concatenate([jnp.sum(ddtr, axis=0, keepdims=True),
                                jnp.sum(da * dt, axis=0, keepdims=True) * arow,
                                jnp.zeros((6, 128), F32)], axis=0)

        @pl.when(step == 0)
        def _():
            acc_ref[...] = part

        @pl.when(step > 0)
        def _():
            acc_ref[...] += part

    return pl.pallas_call(
        body,
        out_shape=(jax.ShapeDtypeStruct((t, DI), F32), jax.ShapeDtypeStruct((t, 2 * NG * NS), F32),
                   jax.ShapeDtypeStruct((t, 128), F32), jax.ShapeDtypeStruct((8, 128), F32)),
        grid=(nc,),
        in_specs=[pl.BlockSpec((Q, CONVD), lambda c: (cmap(c), 0)),
                  pl.BlockSpec((Q, 128), lambda c: (cmap(c), ODT // 128)),
                  pl.BlockSpec((8, 128), lambda c: (0, 0)),
                  pl.BlockSpec((Q, DI), lambda c: (cmap(c), 0)),
                  pl.BlockSpec((None, NS, DI), lambda c: (cmap(c), 0, 0))],
        out_specs=(pl.BlockSpec((Q, DI), lambda c: (cmap(c), 0)),
                   pl.BlockSpec((Q, 2 * NG * NS), lambda c: (cmap(c), 0)),
                   pl.BlockSpec((Q, 128), lambda c: (cmap(c), 0)),
                   pl.BlockSpec((8, 128), lambda c: (0, 0))),
        scratch_shapes=[pltpu.VMEM((NS, DI), F32)],
        name="ssd_bwd_rev" if rev else "ssd_bwd", compiler_params=_params(("arbitrary",)))(
            xbc, u, par, dy, st)


GN_TM = 256
GN_GROUP = DI // NG


def _gn_forward_vals(yf, yb, xs, z, dsk):
    y = yf + yb + dsk * xs
    sz = _sigmoid(z)
    gate = z * sz
    y2 = y * gate
    parts, rs = [], []
    for g in range(NG):
        seg = y2[:, GN_GROUP * g:GN_GROUP * (g + 1)]
        r = lax.rsqrt(jnp.mean(seg * seg, axis=1, keepdims=True) + NORM_EPS)
        rs.append(r)
        parts.append(seg * r)
    yn = jnp.concatenate(parts, axis=1)
    return y, sz, gate, yn, rs


def _gatenorm_fwd(y_f, y_b, xbc, u, dsk_row, nw_row):
    t = y_f.shape[0]
    tm = GN_TM

    def body(yf_ref, yb_ref, xs_ref, z_ref, dsk_ref, nw_ref, o_ref):
        _, _, _, yn, _ = _gn_forward_vals(yf_ref[...], yb_ref[...], xs_ref[...], z_ref[...], dsk_ref[...])
        o_ref[...] = (yn * nw_ref[...]).astype(BF16)

    blk = pl.BlockSpec((tm, DI), lambda i: (i, 0))
    row = pl.BlockSpec((1, DI), lambda i: (0, 0))
    return pl.pallas_call(
        body, out_shape=jax.ShapeDtypeStruct((t, DI), BF16), grid=(t // tm,),
        in_specs=[blk, blk, blk, pl.BlockSpec((tm, DI), lambda i: (i, OZ // DI)), row, row],
        out_specs=blk, name="gatenorm_fwd", compiler_params=_params(("parallel",)))(y_f, y_b, xbc, u, dsk_row, nw_row)


def _gatenorm_bwd(ds_out, y_f, y_b, xbc, u, du, dsk_row, nw_row):
    t = y_f.shape[0]
    tm = GN_TM

    def body(ds_ref, yf_ref, yb_ref, xs_ref, z_ref, dsk_ref, nw_ref, du_in, dy_ref, du_out, dnw_ref, dds_ref):
        del du_in
        i = pl.program_id(0)
        xs = xs_ref[...]
        z = z_ref[...]
        y, sz, gate, yn, rs = _gn_forward_vals(yf_ref[...], yb_ref[...], xs, z, dsk_ref[...])
        ds = ds_ref[...]
        gsc = ds * nw_ref[...]
        parts = []
        for g in range(NG):
            sl = slice(GN_GROUP * g, GN_GROUP * (g + 1))
            m = jnp.mean(gsc[:, sl] * yn[:, sl], axis=1, keepdims=True)
            parts.append(rs[g] * (gsc[:, sl] - yn[:, sl] * m))
        dy2 = jnp.concatenate(parts, axis=1)
        dy = dy2 * gate
        dy_ref[...] = dy
        du_out[...] = dy2 * y * (sz * (1.0 + z * (1.0 - sz)))
        dnw = jnp.broadcast_to(jnp.sum(ds * yn, axis=0, keepdims=True), (8, DI))
        drow = jnp.broadcast_to(jnp.sum(dy * xs, axis=0, keepdims=True), (8, DI))
        dds = _dot01(drow, _sum_mat(0))

        @pl.when(i == 0)
        def _():
            dnw_ref[...] = dnw
            dds_ref[...] = dds

        @pl.when(i > 0)
        def _():
            dnw_ref[...] += dnw
            dds_ref[...] += dds

    blk = pl.BlockSpec((tm, DI), lambda i: (i, 0))
    row = pl.BlockSpec((1, DI), lambda i: (0, 0))
    return pl.pallas_call(
        body,
        out_shape=(jax.ShapeDtypeStruct((t, DI), F32), jax.ShapeDtypeStruct(du.shape, F32),
                   jax.ShapeDtypeStruct((8, DI), F32), jax.ShapeDtypeStruct((8, 128), F32)),
        grid=(t // tm,),
        in_specs=[blk, blk, blk, blk, pl.BlockSpec((tm, DI), lambda i: (i, OZ // DI)), row, row,
                  pl.BlockSpec(memory_space=pl.ANY)],
        out_specs=(blk, pl.BlockSpec((tm, DI), lambda i: (i, OZ // DI)),
                   pl.BlockSpec((8, DI), lambda i: (0, 0)), pl.BlockSpec((8, 128), lambda i: (0, 0))),
        input_output_aliases={7: 1},
        name="gatenorm_bwd", compiler_params=_params(("arbitrary",)))(ds_out, y_f, y_b, xbc, u, dsk_row, nw_row, du)


AT_B = 128
AT_W = AT_B + 2 * ATT_HALF
AT_L = 2 * AH
SCALE = 1.0 / math.sqrt(AH)


def _slope(g, hh):
    return 2.0 ** (-8.0 * (4 * g + hh + 1) / 12.0)


def _qcol(g):
    return lambda p: OQ // AT_L + 2 * g + p


def _kcol(g):
    return lambda p: OKV // AT_L + 4 * g + 2 * p


def _vcol(g):
    return lambda p: OKV // AT_L + 4 * g + 2 * p + 1


def _pcol(p):
    return p


def _win_specs(col, t, d):
    tb, hb = AT_B * d, ATT_HALF * d
    nh = t // hb
    return [
        pl.BlockSpec((hb, AT_L), lambda p, i: (jnp.maximum(2 * i - 1, 0), col(p))),
        pl.BlockSpec((tb, AT_L), lambda p, i: (i, col(p))),
        pl.BlockSpec((hb, AT_L), lambda p, i: (jnp.minimum(2 * i + 2, nh - 1), col(p))),
    ]


def _blk_spec(col, d):
    return pl.BlockSpec((AT_B * d, AT_L), lambda p, i: (i, col(p)))


def _rows(ref, r, n, d):
    return ref[pl.ds(r, n, stride=d), :] if d > 1 else ref[...]


def _win(p_ref, c_ref, n_ref, r, d):
    return jnp.concatenate([_rows(p_ref, r, ATT_HALF, d), _rows(c_ref, r, AT_B, d), _rows(n_ref, r, ATT_HALF, d)], axis=0)


def _put_rows(ref, r, d, val):
    if d > 1:
        ref[pl.ds(r, AT_B, stride=d), :] = val
    else:
        ref[...] = val


def _for_residues(d, fn):
    if d == 1:
        fn(0)
    else:
        def step(r, c):
            fn(r)
            return c
        lax.fori_loop(0, d, step, 0)


def _attn_geometry(i, ln, d):
    a = i * AT_B + _iota((AT_B, AT_W), 0)
    b = i * AT_B - ATT_HALF + _iota((AT_B, AT_W), 1)
    rel = jnp.abs(a - b)
    valid = (rel <= ATT_HALF) & (b >= 0) & (b < ln)
    return valid, (rel * d).astype(F32)


def _attn_fwd(u, g):
    t = u.shape[0]
    d = DILATIONS[g]
    ln = t // d

    def body(q_ref, kp, kc, kn, vp, vc, vn, o_ref, l_ref):
        p_id = pl.program_id(0)
        i = pl.program_id(1)
        valid, dist = _attn_geometry(i, ln, d)
        lane = _iota((AT_B, AT_L), 1)

        def one(r):
            q = _rows(q_ref, r, AT_B, d)
            kw = _win(kp, kc, kn, r, d).astype(BF16)
            vw = _win(vp, vc, vn, r, d).astype(BF16)
            o = jnp.zeros((AT_B, AT_L), F32)
            lse = jnp.zeros((AT_B, AT_L), F32)
            for hh in range(2):
                hm = (lane // AH) == hh
                slope = jnp.where(p_id == 0, _slope(g, hh), _slope(g, 2 + hh))
                qm = jnp.where(hm, q, 0.0).astype(BF16)
                s = _dot_nt(qm, kw) * SCALE - slope * dist
                s = jnp.where(valid, s, NEG)
                m = jnp.max(s, axis=1, keepdims=True)
                pr = jnp.exp(s - m)
                den = jnp.sum(pr, axis=1, keepdims=True)
                oh = jnp.dot(pr.astype(BF16), vw, preferred_element_type=F32)
                o = jnp.where(hm, oh / den, o)
                lse = jnp.where(hm, m + jnp.log(den), lse)
            _put_rows(o_ref, r, d, o)
            _put_rows(l_ref, r, d, lse)

        _for_residues(d, one)

    oshape = jax.ShapeDtypeStruct((t, 2 * AT_L), F32)
    ospec = _blk_spec(_pcol, d)
    return pl.pallas_call(
        body, out_shape=(oshape, oshape), grid=(2, t // (AT_B * d)),
        in_specs=[_blk_spec(_qcol(g), d)] + _win_specs(_kcol(g), t, d) + _win_specs(_vcol(g), t, d),
        out_specs=(ospec, ospec), name=f"attn_fwd_{g}", compiler_params=_params(("parallel", "parallel")))(
            u, u, u, u, u, u, u)


def _attn_dq(u, du, do, lse, e, g):
    t = u.shape[0]
    d = DILATIONS[g]
    ln = t // d

    def body(q_ref, kp, kc, kn, vp, vc, vn, do_ref, l_ref, e_ref, du_in, dq_ref):
        del du_in
        p_id = pl.program_id(0)
        i = pl.program_id(1)
        valid, dist = _attn_geometry(i, ln, d)
        lane = _iota((AT_B, AT_L), 1)

        def one(r):
            q = _rows(q_ref, r, AT_B, d)
            kw = _win(kp, kc, kn, r, d).astype(BF16)
            vw = _win(vp, vc, vn, r, d).astype(BF16)
            do_ = _rows(do_ref, r, AT_B, d)
            lv = _rows(l_ref, r, AT_B, d)
            ev = _rows(e_ref, r, AT_B, d)
            dq = jnp.zeros((AT_B, AT_L), F32)
            for hh in range(2):
                hm = (lane // AH) == hh
                slope = jnp.where(p_id == 0, _slope(g, hh), _slope(g, 2 + hh))
                qm = jnp.where(hm, q, 0.0).astype(BF16)
                s = _dot_nt(qm, kw) * SCALE - slope * dist
                lcol = jnp.broadcast_to(lv[:, AH * hh:AH * hh + 1], (AT_B, AT_W))
                ecol = jnp.broadcast_to(ev[:, AH * hh:AH * hh + 1], (AT_B, AT_W))
                pr = jnp.exp(jnp.where(valid, s - lcol, NEG))
                dom = jnp.where(hm, do_, 0.0).astype(BF16)
                ds = pr * (_dot_nt(dom, vw) + ecol)
                dqh = jnp.dot(ds.astype(BF16), kw, preferred_element_type=F32) * SCALE
                dq = jnp.where(hm, dqh, dq)
            _put_rows(dq_ref, r, d, dq)

        _for_residues(d, one)

    rspec = _blk_spec(_pcol, d)
    return pl.pallas_call(
        body, out_shape=jax.ShapeDtypeStruct(du.shape, F32), grid=(2, t // (AT_B * d)),
        in_specs=[_blk_spec(_qcol(g), d)] + _win_specs(_kcol(g), t, d) + _win_specs(_vcol(g), t, d)
        + [rspec, rspec, rspec, pl.BlockSpec(memory_space=pl.ANY)],
        out_specs=_blk_spec(_qcol(g), d), input_output_aliases={10: 0},
        name=f"attn_dq_{g}", compiler_params=_params(("parallel", "parallel")))(
            u, u, u, u, u, u, u, do, lse, e, du)


def _attn_dkv(u, du, do, lse, e, g):
    t = u.shape[0]
    d = DILATIONS[g]
    ln = t // d

    def body(k_ref, v_ref, qp, qc, qn, dp_, dc_, dn_, lp, lc, ln_, ep, ec, en, du_in, dkv_ref, dk_scr, dv_scr):
        del du_in
        p_id = pl.program_id(0)
        j = pl.program_id(1)
        valid, dist = _attn_geometry(j, ln, d)
        lane = _iota((AT_B, AT_L), 1)

        def one(r):
            k = _rows(k_ref, r, AT_B, d)
            v = _rows(v_ref, r, AT_B, d)
            qw = _win(qp, qc, qn, r, d).astype(BF16)
            dow = _win(dp_, dc_, dn_, r, d).astype(BF16)
            lt = _win(lp, lc, ln_, r, d).T
            et = _win(ep, ec, en, r, d).T
            dk = jnp.zeros((AT_B, AT_L), F32)
            dv = jnp.zeros((AT_B, AT_L), F32)
            for hh in range(2):
                hm = (lane // AH) == hh
                slope = jnp.where(p_id == 0, _slope(g, hh), _slope(g, 2 + hh))
                km = jnp.where(hm, k, 0.0).astype(BF16)
                st = _dot_nt(km, qw) * SCALE - slope * dist
                pt = jnp.exp(jnp.where(valid, st - lt[AH * hh:AH * hh + 1, :], NEG))
                dvh = jnp.dot(pt.astype(BF16), dow, preferred_element_type=F32)
                vm = jnp.where(hm, v, 0.0).astype(BF16)
                dst = pt * (_dot_nt(vm, dow) + et[AH * hh:AH * hh + 1, :])
                dkh = jnp.dot(dst.astype(BF16), qw, preferred_element_type=F32) * SCALE
                dk = jnp.where(hm, dkh, dk)
                dv = jnp.where(hm, dvh, dv)
            _put_rows(dk_scr, r, d, dk)
            _put_rows(dv_scr, r, d, dv)

        _for_residues(d, one)
        dkv_ref[:, 0:AT_L] = dk_scr[...]
        dkv_ref[:, AT_L:2 * AT_L] = dv_scr[...]

    return pl.pallas_call(
        body, out_shape=jax.ShapeDtypeStruct(du.shape, F32), grid=(2, t // (AT_B * d)),
        in_specs=[_blk_spec(_kcol(g), d), _blk_spec(_vcol(g), d)]
        + _win_specs(_qcol(g), t, d) + _win_specs(_pcol, t, d) + _win_specs(_pcol, t, d) + _win_specs(_pcol, t, d)
        + [pl.BlockSpec(memory_space=pl.ANY)],
        out_specs=pl.BlockSpec((AT_B * d, 2 * AT_L), lambda p, i: (i, OKV // (2 * AT_L) + 2 * g + p)),
        input_output_aliases={14: 0},
        scratch_shapes=[pltpu.VMEM((AT_B * d, AT_L), F32), pltpu.VMEM((AT_B * d, AT_L), F32)],
        name=f"attn_dkv_{g}", compiler_params=_params(("parallel", "parallel")))(
            u, u, u, u, u, do, do, do, lse, lse, lse, e, e, e, du)


CMB_TM = 1024


def _combine_weights(l0, l1, l2):
    m = jnp.maximum(jnp.maximum(l0, l1), l2)
    e0, e1, e2 = jnp.exp(l0 - m), jnp.exp(l1 - m), jnp.exp(l2 - m)
    inv = 1.0 / (e0 + e1 + e2)
    return e0 * inv, e1 * inv, e2 * inv


def _combine_fwd(os_, ls_):
    t = os_[0].shape[0]
    tm = CMB_TM

    def body(o0, o1, o2, l0, l1, l2, a_ref):
        w0, w1, w2 = _combine_weights(l0[...], l1[...], l2[...])
        a_ref[...] = w0 * o0[...] + w1 * o1[...] + w2 * o2[...]

    blk = pl.BlockSpec((tm, 2 * AT_L), lambda i: (i, 0))
    return pl.pallas_call(
        body, out_shape=jax.ShapeDtypeStruct((t, 2 * AT_L), F32), grid=(t // tm,), in_specs=[blk] * 6, out_specs=blk,
        name="combine_fwd", compiler_params=_params(("parallel",)))(*os_, *ls_)


def _combine_bwd(datt, os_, ls_):
    t = datt.shape[0]
    tm = CMB_TM

    def body(da_ref, o0, o1, o2, l0, l1, l2, d0, d1, d2, e0, e1, e2):
        w = _combine_weights(l0[...], l1[...], l2[...])
        da = da_ref[...]
        att = w[0] * o0[...] + w[1] * o1[...] + w[2] * o2[...]
        r = _iota((2 * AT_L, 2 * AT_L), 0) // AH
        c = _iota((2 * AT_L, 2 * AT_L), 1) // AH
        hs = _dot01(da * att, (r == c).astype(BF16))
        for wg, dref, eref in zip(w, (d0, d1, d2), (e0, e1, e2)):
            dref[...] = wg * da
            eref[...] = -wg * hs

    blk = pl.BlockSpec((tm, 2 * AT_L), lambda i: (i, 0))
    shp = jax.ShapeDtypeStruct((t, 2 * AT_L), F32)
    outs = pl.pallas_call(
        body, out_shape=(shp,) * 6, grid=(t // tm,), in_specs=[blk] * 7, out_specs=(blk,) * 6,
        name="combine_bwd", compiler_params=_params(("parallel",)))(datt, *os_, *ls_)
    return outs[0:3], outs[3:6]


ROW_TM = 512


def _mix_fwd(y_ssd, y_att, u, bg_row):
    t = y_ssd.shape[0]
    tm = ROW_TM

    def body(ys_ref, ya_ref, g0_ref, g1_ref, b0_ref, b1_ref, o_ref):
        g0 = _sigmoid(g0_ref[...] + b0_ref[...])
        g1 = _sigmoid(g1_ref[...] + b1_ref[...])
        o_ref[...] = (g0 * ys_ref[...] + g1 * ya_ref[...]).astype(BF16)

    blk = pl.BlockSpec((tm, D), lambda i: (i, 0))
    return pl.pallas_call(
        body, out_shape=jax.ShapeDtypeStruct((t, D), BF16), grid=(t // tm,),
        in_specs=[blk, blk, pl.BlockSpec((tm, D), lambda i: (i, OGATE // D)), pl.BlockSpec((tm, D), lambda i: (i, OGATE // D + 1)),
                  pl.BlockSpec((1, D), lambda i: (0, 0)), pl.BlockSpec((1, D), lambda i: (0, 1))],
        out_specs=blk, name="mix_fwd", compiler_params=_params(("parallel",)))(y_ssd, y_att, u, u, bg_row, bg_row)


def _mix_bwd(dmixin, y_ssd, y_att, u, bg_row):
    t = y_ssd.shape[0]
    tm = ROW_TM

    def body(dm_ref, ys_ref, ya_ref, g0_ref, g1_ref, b0_ref, b1_ref, dys_ref, dya_ref, du_ref, db_ref):
        i = pl.program_id(0)
        g0 = _sigmoid(g0_ref[...] + b0_ref[...])
        g1 = _sigmoid(g1_ref[...] + b1_ref[...])
        dm = dm_ref[...]
        dys_ref[...] = (dm * g0).astype(BF16)
        dya_ref[...] = (dm * g1).astype(BF16)
        dl0 = dm * ys_ref[...] * g0 * (1.0 - g0)
        dl1 = dm * ya_ref[...] * g1 * (1.0 - g1)
        du_ref[:, 0:D] = dl0
        du_ref[:, D:2 * D] = dl1
        part = jnp.concatenate([jnp.broadcast_to(jnp.sum(dl0, axis=0, keepdims=True), (8, D)),
                                jnp.broadcast_to(jnp.sum(dl1, axis=0, keepdims=True), (8, D))], axis=1)

        @pl.when(i == 0)
        def _():
            db_ref[...] = part

        @pl.when(i > 0)
        def _():
            db_ref[...] += part

    blk = pl.BlockSpec((tm, D), lambda i: (i, 0))
    return pl.pallas_call(
        body,
        out_shape=(jax.ShapeDtypeStruct((t, D), BF16), jax.ShapeDtypeStruct((t, D), BF16),
                   jax.ShapeDtypeStruct((t, UW), F32), jax.ShapeDtypeStruct((8, 2 * D), F32)),
        grid=(t // tm,),
        in_specs=[blk, blk, blk, pl.BlockSpec((tm, D), lambda i: (i, OGATE // D)), pl.BlockSpec((tm, D), lambda i: (i, OGATE // D + 1)),
                  pl.BlockSpec((1, D), lambda i: (0, 0)), pl.BlockSpec((1, D), lambda i: (0, 1))],
        out_specs=(blk, blk, pl.BlockSpec((tm, 2 * D), lambda i: (i, OGATE // (2 * D))),
                   pl.BlockSpec((8, 2 * D), lambda i: (0, 0))),
        name="mix_bwd", compiler_params=_params(("arbitrary",)))(dmixin, y_ssd, y_att, u, u, bg_row, bg_row)


def _ln(x, g, b):
    mu = jnp.mean(x, axis=1, keepdims=True)
    xc = x - mu
    var = jnp.mean(xc * xc, axis=1, keepdims=True)
    rstd = lax.rsqrt(var + NORM_EPS)
    xhat = xc * rstd
    return xhat * g + b, xhat, rstd


def _ln_back(dh, xhat, rstd, g):
    dxh = dh * g
    m1 = jnp.mean(dxh, axis=1, keepdims=True)
    m2 = jnp.mean(dxh * xhat, axis=1, keepdims=True)
    return rstd * (dxh - m1 - xhat * m2)


def _ln1_fwd(x, mix, g_row, b_row):
    t = x.shape[0]
    tm = ROW_TM

    def body(x_ref, m_ref, g_ref, b_ref, pre_ref, h_ref):
        pre = ALPHA * x_ref[...] + m_ref[...]
        pre_ref[...] = pre
        h, _, _ = _ln(pre, g_ref[...], b_ref[...])
        h_ref[...] = h.astype(BF16)

    blk = pl.BlockSpec((tm, D), lambda i: (i, 0))
    row = pl.BlockSpec((1, D), lambda i: (0, 0))
    return pl.pallas_call(
        body, out_shape=(jax.ShapeDtypeStruct((t, D), F32), jax.ShapeDtypeStruct((t, D), BF16)), grid=(t // tm,),
        in_specs=[blk, blk, row, row], out_specs=(blk, blk),
        name="ln1_fwd", compiler_params=_params(("parallel",)))(x, mix, g_row, b_row)


def _ln1_bwd(dh, pre, g_row, b_row):
    t = dh.shape[0]
    tm = ROW_TM

    def body(dh_ref, pre_ref, g_ref, b_ref, dpre_ref, acc_ref):
        i = pl.program_id(0)
        dh_ = dh_ref[...]
        _, xhat, rstd = _ln(pre_ref[...], g_ref[...], b_ref[...])
        dpre_ref[...] = _ln_back(dh_, xhat, rstd, g_ref[...])
        part = jnp.concatenate([jnp.sum(dh_ * xhat, axis=0, keepdims=True), jnp.sum(dh_, axis=0, keepdims=True),
                                jnp.zeros((6, D), F32)], axis=0)

        @pl.when(i == 0)
        def _():
            acc_ref[...] = part

        @pl.when(i > 0)
        def _():
            acc_ref[...] += part

    blk = pl.BlockSpec((tm, D), lambda i: (i, 0))
    row = pl.BlockSpec((1, D), lambda i: (0, 0))
    return pl.pallas_call(
        body, out_shape=(jax.ShapeDtypeStruct((t, D), F32), jax.ShapeDtypeStruct((8, D), F32)), grid=(t // tm,),
        in_specs=[blk, blk, row, row], out_specs=(blk, pl.BlockSpec((8, D), lambda i: (0, 0))),
        name="ln1_bwd", compiler_params=_params(("arbitrary",)))(dh, pre, g_row, b_row)


def _ln2_loss(pre1, f, tgt, g1_row, b1_row, g2_row, b2_row):
    t = pre1.shape[0]
    tm = ROW_TM

    def body(p1_ref, f_ref, t_ref, g1_ref, b1_ref, g2_ref, b2_ref, dpre_ref, acc_ref):
        i = pl.program_id(0)
        h1, _, _ = _ln(p1_ref[...], g1_ref[...], b1_ref[...])
        pre2 = ALPHA * h1 + f_ref[...]
        h2, xhat, rstd = _ln(pre2, g2_ref[...], b2_ref[...])
        err = h2 - t_ref[...]
        dh = err * (1.0 / D)
        dpre_ref[...] = _ln_back(dh, xhat, rstd, g2_ref[...])
        loss = jnp.sum(jnp.sum(err * err, axis=1, keepdims=True), axis=0, keepdims=True) * (0.5 / D)
        part = jnp.concatenate([jnp.sum(dh * xhat, axis=0, keepdims=True), jnp.sum(dh, axis=0, keepdims=True),
                                jnp.broadcast_to(loss, (1, D)), jnp.zeros((5, D), F32)], axis=0)

        @pl.when(i == 0)
        def _():
            acc_ref[...] = part

        @pl.when(i > 0)
        def _():
            acc_ref[...] += part

    blk = pl.BlockSpec((tm, D), lambda i: (i, 0))
    row = pl.BlockSpec((1, D), lambda i: (0, 0))
    return pl.pallas_call(
        body, out_shape=(jax.ShapeDtypeStruct((t, D), F32), jax.ShapeDtypeStruct((8, D), F32)), grid=(t // tm,),
        in_specs=[blk, blk, blk, row, row, row, row], out_specs=(blk, pl.BlockSpec((8, D), lambda i: (0, 0))),
        name="ln2_loss", compiler_params=_params(("arbitrary",)))(pre1, f, tgt, g1_row, b1_row, g2_row, b2_row)


def _act_fwd(up):
    t = up.shape[0]
    tm = ROW_TM

    def body(u_ref, o_ref):
        r = jnp.maximum(u_ref[...], 0.0)
        o_ref[...] = (r * r).astype(BF16)

    blk = pl.BlockSpec((tm, 2048), lambda i, j: (i, j))
    return pl.pallas_call(body, out_shape=jax.ShapeDtypeStruct(up.shape, BF16), grid=(t // tm, DFF // 2048),
                          in_specs=[blk], out_specs=blk, name="act_fwd",
                          compiler_params=_params(("parallel", "parallel")))(up)


def _act_bwd(dact, up):
    t = up.shape[0]
    tm = ROW_TM

    def body(d_ref, u_ref, o_ref):
        o_ref[...] = (d_ref[...] * 2.0 * jnp.maximum(u_ref[...], 0.0)).astype(BF16)

    blk = pl.BlockSpec((tm, 2048), lambda i, j: (i, j))
    return pl.pallas_call(body, out_shape=jax.ShapeDtypeStruct(up.shape, BF16), grid=(t // tm, DFF // 2048),
                          in_specs=[blk, blk], out_specs=blk, name="act_bwd",
                          compiler_params=_params(("parallel", "parallel")))(dact, up)


def _dt_bwd(du, ddt_f, ddt_b):
    t = ddt_f.shape[0]
    tm = 1024

    def body(f_ref, b_ref, du_in, o_ref):
        del du_in
        o_ref[:, 0:128] = f_ref[...] + b_ref[...]
        o_ref[:, 128:256] = jnp.zeros((tm, 128), F32)

    blk = pl.BlockSpec((tm, 128), lambda i: (i, 0))
    return pl.pallas_call(
        body, out_shape=jax.ShapeDtypeStruct(du.shape, F32), grid=(t // tm,),
        in_specs=[blk, blk, pl.BlockSpec(memory_space=pl.ANY)],
        out_specs=pl.BlockSpec((tm, 256), lambda i: (i, ODT // 256)), input_output_aliases={2: 0},
        name="dt_bwd", compiler_params=_params(("parallel",)))(ddt_f, ddt_b, du)


def _adamw(w, g, m, v, name):
    r, c = w.shape
    tr = r
    for cand in (256, 128, 64, 32, 16, 8):
        if r % cand == 0 and cand * c * 4 <= 2 ** 21:
            tr = cand
            break
    bc1 = 1.0 / (1.0 - ADAM_B1 ** ADAM_STEP)
    bc2 = 1.0 / (1.0 - ADAM_B2 ** ADAM_STEP)

    def body(w_ref, g_ref, m_ref, v_ref, d_ref, nm_ref, nv_ref):
        gg = g_ref[...]
        nm = ADAM_B1 * m_ref[...] + (1.0 - ADAM_B1) * gg
        nv = ADAM_B2 * v_ref[...] + (1.0 - ADAM_B2) * (gg * gg)
        nm_ref[...] = nm
        nv_ref[...] = nv
        d_ref[...] = -ADAM_LR * ((nm * bc1) / (jnp.sqrt(nv * bc2) + ADAM_EPS) + ADAM_WD * w_ref[...])

    blk = pl.BlockSpec((tr, c), lambda i: (i, 0))
    shp = jax.ShapeDtypeStruct((r, c), F32)
    return pl.pallas_call(body, out_shape=(shp, shp, shp), grid=(r // tr,), in_specs=[blk] * 4, out_specs=(blk,) * 3,
                          name=name, compiler_params=_params(("parallel",)))(w, g, m, v)


def _perm_cols(w):
    z, xbc, dt = w[:, 0:2048], w[:, 2048:5120], w[:, 5120:5184]
    q, k, v, gate = w[:, 5184:5952], w[:, 5952:6720], w[:, 6720:7488], w[:, 7488:9536]
    kv = []
    for g in range(3):
        for p in range(2):
            lo = 256 * g + 128 * p
            kv += [k[:, lo:lo + 128], v[:, lo:lo + 128]]
    pad = jnp.zeros((w.shape[0], UW - IN_COLS), w.dtype)
    return jnp.concatenate([z, gate, xbc] + kv + [q, dt, pad], axis=1)


def _unperm_cols(wp):
    z, gate, xbc = wp[:, OZ:OZ + 2048], wp[:, OGATE:OGATE + 2048], wp[:, OXBC:OXBC + CONVD]
    q, dt = wp[:, OQ:OQ + 768], wp[:, ODT:ODT + 64]
    ks, vs = [], []
    for g in range(3):
        for p in range(2):
            lo = OKV + 128 * (4 * g + 2 * p)
            ks.append(wp[:, lo:lo + 128])
            vs.append(wp[:, lo + 128:lo + 256])
    return jnp.concatenate([z, xbc, dt, q] + ks + vs + [gate], axis=1)


def _lanes128(*vecs):
    v = jnp.concatenate([a.reshape(-1) for a in vecs])
    return jnp.pad(v, (0, 128 - v.shape[0])).reshape(1, 128)


def _local_grads(x, tgt, wts, sm):
    row = lambda a: a.reshape(1, -1)
    bg_row, cb_row = row(sm["b_gate"]), row(sm["conv_b"])
    par = jnp.concatenate([_lanes128(sm["dt_bias_f"], sm["dt_bias_b"]), _lanes128(sm["a_log_f"], sm["a_log_b"]),
                           jnp.zeros((6, 128), F32)], axis=0)
    dsk_row = row(jnp.repeat(sm["d_skip"], HP))
    nw_row = row(sm["ssd_norm_w"])
    g1, b1, g2, b2 = row(sm["ln1_g"]), row(sm["ln1_b"]), row(sm["ln2_g"]), row(sm["ln2_b"])

    u = _mm_nn(x, wts["w_in_p"], tm=512, tn=2432, name="in_proj")
    xbc = _conv_fwd(u, sm["conv_w"], cb_row)
    y_f, st_f = _ssd_fwd(xbc, u, par, rev=False)
    y_b, st_b = _ssd_fwd(xbc, u, par, rev=True)
    s_out = _gatenorm_fwd(y_f, y_b, xbc, u, dsk_row, nw_row)
    y_ssd = _mm_nn(s_out, wts["w_proj_ssd"], tm=512, tn=1024, name="proj_ssd")
    att_o, att_l = [], []
    for g in range(3):
        o, l = _attn_fwd(u, g)
        att_o.append(o)
        att_l.append(l)
    att = _combine_fwd(att_o, att_l)
    y_att = _mm_nn(att, wts["w_proj_attn"], tm=512, tn=256, name="proj_attn")
    mixin = _mix_fwd(y_ssd, y_att, u, bg_row)
    mix = _mm_nn(mixin, wts["w_out"], tm=512, tn=1024, name="out_proj")
    pre1, h1 = _ln1_fwd(x, mix, g1, b1)
    up = _mm_nn(h1, wts["w_up"], tm=512, tn=1024, name="mlp_up")
    act = _act_fwd(up)
    f = _mm_nn(act, wts["w_down"], tm=512, tn=1024, name="mlp_down")
    dpre2, acc2 = _ln2_loss(pre1, f, tgt, g1, b1, g2, b2)

    dw_down = _mm_tn(act, dpre2, tka=1024, tn=1024, tt=512, name="dw_down")
    dact = _mm_nt(dpre2, wts["w_down"], tm=512, tk=1024, tc=1024, name="d_act")
    dup = _act_bwd(dact, up)
    dw_up = _mm_tn(h1, dup, tka=1024, tn=1024, tt=512, name="dw_up", out_shards=4)
    dh1 = _mm_nt(dup, wts["w_up"], tm=512, tk=1024, tc=1024, name="d_h1", add=dpre2, add_scale=ALPHA)
    dpre1, acc1 = _ln1_bwd(dh1, pre1, g1, b1)
    dmixin = _mm_nt(dpre1, wts["w_out"], tm=512, tk=1024, tc=1024, name="d_mixin")
    dw_out = _mm_tn(mixin, dpre1, tka=1024, tn=1024, tt=512, name="dw_out")
    dy_ssd, dy_att, du, dbg = _mix_bwd(dmixin, y_ssd, y_att, u, bg_row)
    dw_proj_ssd = _mm_tn(s_out, dy_ssd, tka=1024, tn=1024, tt=512, name="dw_proj_ssd")
    ds_out = _mm_nt(dy_ssd, wts["w_proj_ssd"], tm=512, tk=1024, tc=1024, name="d_s_out")
    dw_proj_attn = _mm_tn(att, dy_att, tka=256, tn=256, tt=512, name="dw_proj_attn", out_shards=4)
    datt = _mm_nt(dy_att, wts["w_proj_attn"], tm=512, tk=256, tc=256, name="d_att")
    do_g, e_g = _combine_bwd(datt, att_o, att_l)
    for g in range(3):
        du = _attn_dq(u, du, do_g[g], att_l[g], e_g[g], g)
        du = _attn_dkv(u, du, do_g[g], att_l[g], e_g[g], g)
    dy, du, dnw, dds = _gatenorm_bwd(ds_out, y_f, y_b, xbc, u, du, dsk_row, nw_row)
    dxs_f, dbc_f, ddt_f, sacc_f = _ssd_bwd(xbc, u, par, dy, st_f, rev=False)
    dxs_b, dbc_b, ddt_b, sacc_b = _ssd_bwd(xbc, u, par, dy, st_b, rev=True)
    dpre_c, dcw, dcb = _conv_dpre(u, dxs_f, dxs_b, dy, dbc_f, dbc_b, dsk_row, sm["conv_w"], cb_row)
    du = _conv_dx(du, dpre_c, sm["conv_w"])
    du = _dt_bwd(du, ddt_f, ddt_b)
    dw_in_p = _mm_tn(x, du, tka=1024, tn=2432, tt=512, name="dw_in")
    dx = _mm_nt(du, wts["w_in_p"], tm=512, tk=1024, tc=2432, name="d_x", add=dpre1, add_scale=ALPHA)

    sacc = sacc_f + sacc_b
    small = {
        "b_gate": dbg[0], "conv_w": dcw[0:KCONV], "conv_b": dcb[0],
        "dt_bias_f": sacc[0, 0:32], "dt_bias_b": sacc[0, 32:64], "a_log_f": sacc[1, 0:32], "a_log_b": sacc[1, 32:64],
        "d_skip": dds[0, 0:32], "ssd_norm_w": dnw[0],
        "ln1_g": acc1[0], "ln1_b": acc1[1], "ln2_g": acc2[0], "ln2_b": acc2[1], "loss": acc2[2, 0:1],
    }
    dw_in = _unperm_cols(dw_in_p)
    big = {
        "w_in": dw_in.reshape(D, 4, IN_COLS // 4).transpose(1, 0, 2),
        "w_proj_ssd": dw_proj_ssd.reshape(4, DI // 4, D),
        "w_proj_attn": dw_proj_attn,
        "w_out": dw_out.reshape(4, D // 4, D),
        "w_up": dw_up,
        "w_down": dw_down.reshape(4, DFF // 4, D),
    }
    return dx, big, small


HBM_SPEC = pl.BlockSpec(memory_space=pl.ANY)


def _place():
    x, y, c = lax.axis_index("x"), lax.axis_index("y"), lax.axis_index("c")
    chips = [(1 - x, y), (x, 1 - y), (1 - x, 1 - y)]
    return x, y, c, chips


def _allgather_weights(shards):
    n = len(shards)

    def body(*refs):
        ins, outs = refs[:n], refs[n:2 * n]
        send_sems, recv_sems, local_sems = refs[2 * n:]
        x, y, c, chips = _place()
        q = 2 * x + y
        sibling = (x, y, 1 - c)

        def copy(w, k, src, dst, to):
            return pltpu.make_async_remote_copy(src_ref=src, dst_ref=dst, send_sem=send_sems.at[w, k],
                                                recv_sem=recv_sems.at[w, k], device_id=to, device_id_type=MESH)

        def halves(w):
            rh = ins[w].shape[0] // 2
            return pl.ds(c * rh, rh), pl.ds((1 - c) * rh, rh)

        locals_, firsts, passed = [], [], []
        for w in range(n):
            mine, _ = halves(w)
            local = pltpu.make_async_copy(ins[w], outs[w].at[q], local_sems.at[w])
            local.start()
            locals_.append(local)
            first = [copy(w, j, ins[w].at[mine, :], outs[w].at[q, mine, :], (px, py, c)) for j, (px, py) in enumerate(chips)]
            for cp in first:
                cp.start()
            firsts.append(first)
        for w in range(n):
            mine, _ = halves(w)
            fw = []
            for j, (px, py) in enumerate(chips):
                blk = outs[w].at[2 * px + py, mine, :]
                copy(w, j, blk, blk, (px, py, c)).wait_recv()
                fwd = copy(w, 3 + j, blk, blk, sibling)
                fwd.start()
                fw.append(fwd)
            passed.append(fw)
        for w in range(n):
            _, theirs = halves(w)
            for j, (px, py) in enumerate(chips):
                blk = outs[w].at[2 * px + py, theirs, :]
                copy(w, 3 + j, blk, blk, sibling).wait_recv()
            for cp in firsts[w] + passed[w]:
                cp.wait_send()
            locals_[w].wait()

    return pl.pallas_call(
        body, out_shape=[jax.ShapeDtypeStruct((4,) + s.shape, s.dtype) for s in shards],
        in_specs=[HBM_SPEC] * n, out_specs=[HBM_SPEC] * n,
        scratch_shapes=[pltpu.SemaphoreType.DMA((n, 6)), pltpu.SemaphoreType.DMA((n, 6)), pltpu.SemaphoreType.DMA((n,))],
        name="allgather_weights")(*shards)


def _swap_halves(grads):
    n = len(grads)

    def body(*refs):
        ins, outs = refs[:n], refs[n:2 * n]
        send_sems, recv_sems = refs[2 * n:]
        x, y, c, _ = _place()
        copies = []
        for w in range(n):
            rh = ins[w].shape[1] // 2
            for p in range(4):
                cp = pltpu.make_async_remote_copy(
                    src_ref=ins[w].at[p, pl.ds((1 - c) * rh, rh), :], dst_ref=outs[w].at[p],
                    send_sem=send_sems.at[w, p], recv_sem=recv_sems.at[w, p],
                    device_id=(x, y, 1 - c), device_id_type=MESH)
                cp.start()
                copies.append(cp)
        for cp in copies:
            cp.wait()

    return pl.pallas_call(
        body, out_shape=[jax.ShapeDtypeStruct((4, g.shape[1] // 2, g.shape[2]), F32) for g in grads],
        in_specs=[HBM_SPEC] * n, out_specs=[HBM_SPEC] * n,
        scratch_shapes=[pltpu.SemaphoreType.DMA((n, 4)), pltpu.SemaphoreType.DMA((n, 4))],
        name="rs_swap_halves")(*grads)


def _exchange_chips(parts):
    n = len(parts)

    def body(*refs):
        ins, outs = refs[:n], refs[n:2 * n]
        send_sems, recv_sems, local_sems = refs[2 * n:]
        x, y, c, chips = _place()
        q = 2 * x + y

        def copy(w, j, px, py, slot):
            return pltpu.make_async_remote_copy(
                src_ref=ins[w].at[2 * px + py], dst_ref=outs[w].at[slot],
                send_sem=send_sems.at[w, j], recv_sem=recv_sems.at[w, j],
                device_id=(px, py, c), device_id_type=MESH)

        for w in range(n):
            pltpu.make_async_copy(ins[w].at[q], outs[w].at[q], local_sems.at[w]).start()
            for j, (px, py) in enumerate(chips):
                copy(w, j, px, py, q).start()
        for w in range(n):
            for j, (px, py) in enumerate(chips):
                copy(w, j, px, py, 2 * px + py).wait_recv()
        for w in range(n):
            for j, (px, py) in enumerate(chips):
                copy(w, j, px, py, q).wait_send()
            pltpu.make_async_copy(ins[w].at[q], outs[w].at[q], local_sems.at[w]).wait()

    return pl.pallas_call(
        body, out_shape=[jax.ShapeDtypeStruct(p.shape, F32) for p in parts],
        in_specs=[HBM_SPEC] * n, out_specs=[HBM_SPEC] * n,
        scratch_shapes=[pltpu.SemaphoreType.DMA((n, 3)), pltpu.SemaphoreType.DMA((n, 3)), pltpu.SemaphoreType.DMA((n,))],
        name="rs_exchange_chips")(*parts)


def _join_halves(pieces):
    n = len(pieces)

    def body(*refs):
        ins, outs = refs[:n], refs[n:2 * n]
        send_sems, recv_sems, local_sems = refs[2 * n:]
        x, y, c, _ = _place()

        def copy(w, slot):
            return pltpu.make_async_remote_copy(
                src_ref=ins[w], dst_ref=outs[w].at[slot], send_sem=send_sems.at[w], recv_sem=recv_sems.at[w],
                device_id=(x, y, 1 - c), device_id_type=MESH)

        for w in range(n):
            pltpu.make_async_copy(ins[w], outs[w].at[c], local_sems.at[w]).start()
            copy(w, c).start()
        for w in range(n):
            copy(w, 1 - c).wait_recv()
            copy(w, c).wait_send()
            pltpu.make_async_copy(ins[w], outs[w].at[c], local_sems.at[w]).wait()

    return pl.pallas_call(
        body, out_shape=[jax.ShapeDtypeStruct((2,) + p.shape, F32) for p in pieces],
        in_specs=[HBM_SPEC] * n, out_specs=[HBM_SPEC] * n,
        scratch_shapes=[pltpu.SemaphoreType.DMA((n,)), pltpu.SemaphoreType.DMA((n,)), pltpu.SemaphoreType.DMA((n,))],
        name="rs_join_halves")(*pieces)


def _add_tile_rows(rh, c):
    for cand in (512, 256, 128, 64, 32, 16, 8):
        if rh % cand == 0 and cand * c * 4 <= 2 ** 21:
            return cand
    return rh


def _add_half(grad, recv, c_idx, name):
    _, r, cc = grad.shape
    rh = r // 2
    tr = _add_tile_rows(rh, cc)
    nb = rh // tr

    def body(c_ref, g_ref, r_ref, o_ref):
        del c_ref
        o_ref[...] = g_ref[...] + r_ref[...]

    grid_spec = pltpu.PrefetchScalarGridSpec(
        num_scalar_prefetch=1, grid=(4, nb),
        in_specs=[pl.BlockSpec((None, tr, cc), lambda p, i, c_ref: (p, c_ref[0] * nb + i, 0)),
                  pl.BlockSpec((None, tr, cc), lambda p, i, c_ref: (p, i, 0))],
        out_specs=pl.BlockSpec((None, tr, cc), lambda p, i, c_ref: (p, i, 0)))
    return pl.pallas_call(body, out_shape=jax.ShapeDtypeStruct((4, rh, cc), F32), grid_spec=grid_spec,
                          name=name, compiler_params=_params(("parallel", "parallel")))(c_idx, grad, recv)


def _sum_chips(slots, name):
    _, rh, cc = slots.shape
    tr = _add_tile_rows(rh, cc)

    def body(s_ref, o_ref):
        o_ref[...] = ((s_ref[0] + s_ref[1]) + s_ref[2]) + s_ref[3]

    return pl.pallas_call(body, out_shape=jax.ShapeDtypeStruct((rh, cc), F32), grid=(rh // tr,),
                          in_specs=[pl.BlockSpec((4, tr, cc), lambda i: (0, i, 0))],
                          out_specs=pl.BlockSpec((tr, cc), lambda i: (i, 0)),
                          name=name, compiler_params=_params(("parallel",)))(slots)


def _allreduce_small(slab):
    r = slab.shape[0]

    def body(x_ref, o_ref, buf, send_sems, recv_sems):
        x, y, c, _ = _place()
        me = 4 * x + 2 * y + c
        buf[me] = x_ref[...]
        peers = []
        for k in range(1, 8):
            kx, ky, kc = (k >> 2) & 1, (k >> 1) & 1, k & 1
            peers.append((x + kx - 2 * x * kx, y + ky - 2 * y * ky, c + kc - 2 * c * kc))

        def copy(k, slot):
            return pltpu.make_async_remote_copy(src_ref=x_ref, dst_ref=buf.at[slot], send_sem=send_sems.at[k],
                                                recv_sem=recv_sems.at[k], device_id=peers[k], device_id_type=MESH)

        for k in range(7):
            copy(k, me).start()
        for k, (px, py, pc) in enumerate(peers):
            copy(k, 4 * px + 2 * py + pc).wait_recv()
        for k in range(7):
            copy(k, me).wait_send()
        acc = buf[0]
        for j in range(1, 8):
            acc = acc + buf[j]
        o_ref[...] = acc

    vm = pl.BlockSpec(memory_space=pltpu.VMEM)
    return pl.pallas_call(
        body, out_shape=jax.ShapeDtypeStruct((r, 128), F32), in_specs=[vm], out_specs=vm,
        scratch_shapes=[pltpu.VMEM((8, r, 128), F32), pltpu.SemaphoreType.DMA((7,)), pltpu.SemaphoreType.DMA((7,))],
        name="allreduce_small")(slab)


def _pack(arrs):
    rows = []
    for a in arrs:
        v = a.reshape(-1)
        v = jnp.pad(v, (0, (-v.shape[0]) % 128))
        rows.append(v.reshape(-1, 128))
    slab = jnp.concatenate(rows, axis=0)
    return jnp.pad(slab, ((0, (-slab.shape[0]) % 8), (0, 0)))


def _unpack(slab, shapes):
    out, r0 = [], 0
    for shp in shapes:
        size = math.prod(shp)
        nr = -(-size // 128)
        out.append(slab[r0:r0 + nr].reshape(-1)[:size].reshape(shp))
        r0 += nr
    return out


BIG = ("w_in", "w_proj_ssd", "w_proj_attn", "w_out", "w_up", "w_down")
SMALL = ("b_gate", "conv_w", "conv_b", "dt_bias_f", "dt_bias_b", "a_log_f", "a_log_b", "d_skip", "ssd_norm_w",
         "ln1_g", "ln1_b", "ln2_g", "ln2_b")
ORDER = ("w_in", "b_gate", "conv_w", "conv_b", "dt_bias_f", "dt_bias_b", "a_log_f", "a_log_b", "d_skip", "ssd_norm_w",
         "w_proj_ssd", "w_proj_attn", "w_out", "ln1_g", "ln1_b", "w_up", "w_down", "ln2_g", "ln2_b")


def kernel(x, w_in, b_gate, conv_w, conv_b, dt_bias_f, dt_bias_b, a_log_f, a_log_b, d_skip, ssd_norm_w, w_proj_ssd, w_proj_attn, w_out, ln1_g, ln1_b, w_up, w_down, ln2_g, ln2_b, loss_target, m_w_in, m_b_gate, m_conv_w, m_conv_b, m_dt_bias_f, m_dt_bias_b, m_a_log_f, m_a_log_b, m_d_skip, m_ssd_norm_w, m_w_proj_ssd, m_w_proj_attn, m_w_out, m_ln1_g, m_ln1_b, m_w_up, m_w_down, m_ln2_g, m_ln2_b, v_w_in, v_b_gate, v_conv_w, v_conv_b, v_dt_bias_f, v_dt_bias_b, v_a_log_f, v_a_log_b, v_d_skip, v_ssd_norm_w, v_w_proj_ssd, v_w_proj_attn, v_w_out, v_ln1_g, v_ln1_b, v_w_up, v_w_down, v_ln2_g, v_ln2_b):
    w = dict(w_in=w_in, b_gate=b_gate, conv_w=conv_w, conv_b=conv_b, dt_bias_f=dt_bias_f, dt_bias_b=dt_bias_b,
             a_log_f=a_log_f, a_log_b=a_log_b, d_skip=d_skip, ssd_norm_w=ssd_norm_w, w_proj_ssd=w_proj_ssd,
             w_proj_attn=w_proj_attn, w_out=w_out, ln1_g=ln1_g, ln1_b=ln1_b, w_up=w_up, w_down=w_down, ln2_g=ln2_g, ln2_b=ln2_b)
    m = dict(w_in=m_w_in, b_gate=m_b_gate, conv_w=m_conv_w, conv_b=m_conv_b, dt_bias_f=m_dt_bias_f, dt_bias_b=m_dt_bias_b,
             a_log_f=m_a_log_f, a_log_b=m_a_log_b, d_skip=m_d_skip, ssd_norm_w=m_ssd_norm_w, w_proj_ssd=m_w_proj_ssd,
             w_proj_attn=m_w_proj_attn, w_out=m_w_out, ln1_g=m_ln1_g, ln1_b=m_ln1_b, w_up=m_w_up, w_down=m_w_down,
             ln2_g=m_ln2_g, ln2_b=m_ln2_b)
    v = dict(w_in=v_w_in, b_gate=v_b_gate, conv_w=v_conv_w, conv_b=v_conv_b, dt_bias_f=v_dt_bias_f, dt_bias_b=v_dt_bias_b,
             a_log_f=v_a_log_f, a_log_b=v_a_log_b, d_skip=v_d_skip, ssd_norm_w=v_ssd_norm_w, w_proj_ssd=v_w_proj_ssd,
             w_proj_attn=v_w_proj_attn, w_out=v_w_out, ln1_g=v_ln1_g, ln1_b=v_ln1_b, w_up=v_w_up, w_down=v_w_down,
             ln2_g=v_ln2_g, ln2_b=v_ln2_b)
    xi, yi, ci = lax.axis_index("x"), lax.axis_index("y"), lax.axis_index("c")
    shard = 2 * xi + yi

    g_in, g_ps, g_pa, g_o, g_up, g_dn = _allgather_weights([w[n].astype(BF16) for n in BIG])
    w_in_full = jnp.concatenate([g_in[s] for s in range(4)], axis=1)
    wts = {"w_in_p": _perm_cols(w_in_full), "w_proj_ssd": g_ps.reshape(DI, D), "w_proj_attn": g_pa,
           "w_out": g_o.reshape(D, D), "w_up": g_up, "w_down": g_dn.reshape(DFF, D)}

    cw_slab = jnp.zeros((KCONV, 4, CONVD // 4), F32)
    cw_slab = lax.dynamic_update_slice(cw_slab, conv_w[:, None, :] * 0.5, (0, shard, 0))
    conv_w_all = _unpack(_allreduce_small(_pack([cw_slab])), [(KCONV, CONVD)])[0]

    sm = {n: w[n] for n in SMALL}
    sm["conv_w"] = conv_w_all
    dx, big, small = _local_grads(x[0], loss_target[0], wts, sm)

    names = list(SMALL) + ["loss"]
    shapes = [small[n].shape for n in names]
    red = dict(zip(names, _unpack(_allreduce_small(_pack([small[n] for n in names])), shapes)))
    loss = red["loss"].reshape(())
    gsm = {n: red[n] for n in SMALL}
    conv_w_grad_shard = lax.dynamic_slice_in_dim(gsm["conv_w"].reshape(KCONV, 4, CONVD // 4), shard, 1, axis=1)
    gsm["conv_w"] = conv_w_grad_shard.reshape(KCONV, CONVD // 4)

    c_idx = jnp.reshape(ci, (1,)).astype(jnp.int32)
    glist = [big[n] for n in BIG]
    recv = _swap_halves(glist)
    parts = [_add_half(g, r, c_idx, f"rs_add_half_{n}") for g, r, n in zip(glist, recv, BIG)]
    slots = _exchange_chips(parts)
    pieces = [_sum_chips(s, f"rs_sum_chips_{n}") for s, n in zip(slots, BIG)]
    joined = _join_halves(pieces)
    gbig = {n: j.reshape(w[n].shape) for n, j in zip(BIG, joined)}

    grads, deltas, new_m, new_v = {}, {}, {}, {}
    for n in BIG:
        grads[n] = gbig[n]
        deltas[n], new_m[n], new_v[n] = _adamw(w[n], gbig[n], m[n], v[n], f"adamw_{n}")
    sshapes = [w[n].shape for n in SMALL]
    d_s, m_s, v_s = _adamw(_pack([w[n] for n in SMALL]), _pack([gsm[n] for n in SMALL]),
                           _pack([m[n] for n in SMALL]), _pack([v[n] for n in SMALL]), "adamw_small")
    for n, dd, mm, vv in zip(SMALL, _unpack(d_s, sshapes), _unpack(m_s, sshapes), _unpack(v_s, sshapes)):
        grads[n], deltas[n], new_m[n], new_v[n] = gsm[n], dd, mm, vv

    return (loss, dx[None], *[grads[n] for n in ORDER], *[deltas[n] for n in ORDER],
            *[new_m[n] for n in ORDER], *[new_v[n] for n in ORDER])
```

```python
import math

import jax
import jax.numpy as jnp
from jax import lax
from jax.experimental import pallas as pl
from jax.experimental.pallas import tpu as pltpu

F32, BF16 = jnp.float32, jnp.bfloat16
MESH = pl.DeviceIdType.MESH

D = 1024
DI = 2048
NH = 32
HP = 64
NG = 4
NS = 128
Q = 128
CONVD = 3072
KCONV = 5
DFF = 4096
AH = 64
ATT_HALF = 64
DILATIONS = (1, 4, 16)
IN_COLS = 9536
OZ, OGATE, OXBC, OKV, OQ, ODT, UW = 0, 2048, 4096, 7168, 8704, 9472, 9728
ALPHA = 2.0 ** 0.25
NORM_EPS = 1e-5
ADAM_LR, ADAM_B1, ADAM_B2, ADAM_EPS, ADAM_WD, ADAM_STEP = 0.001, 0.9, 0.999, 1e-8, 0.01, 10
VMEM_LIMIT = 56 * 2 ** 20
NEG = -1e30


def _params(sem):
    return pltpu.CompilerParams(dimension_semantics=sem, vmem_limit_bytes=VMEM_LIMIT)


def _sigmoid(x):
    return 1.0 / (1.0 + jnp.exp(-x))


def _softplus(x):
    e = jnp.exp(-jnp.abs(x))
    small = e * (1.0 - e * (0.5 - e * (1.0 / 3.0)))
    return jnp.maximum(x, 0.0) + jnp.where(e < 0.01, small, jnp.log(1.0 + e))


def _split3(a):
    hi = a.astype(BF16)
    r = a - hi.astype(F32)
    mid = r.astype(BF16)
    lo = (r - mid.astype(F32)).astype(BF16)
    return hi, mid, lo


def _dot01(a, m01):
    hi, mid, lo = _split3(a)
    d = lambda p: jnp.dot(p, m01, preferred_element_type=F32)
    return d(hi) + d(mid) + d(lo)


def _dot01_l(m01, a):
    hi, mid, lo = _split3(a)
    d = lambda p: jnp.dot(m01, p, preferred_element_type=F32)
    return d(hi) + d(mid) + d(lo)


def _dot_nt(a, b):
    return lax.dot_general(a, b, (((1,), (1,)), ((), ())), preferred_element_type=F32)


def _iota(shape, dim):
    return lax.broadcasted_iota(jnp.int32, shape, dim)


def _mm_nn(a, b, *, tm, tn, name, out_dtype=F32):
    m, k = a.shape
    if b.ndim == 3:
        assert tn == b.shape[2]
        n = b.shape[0] * b.shape[2]
        b_spec = pl.BlockSpec((None, k, tn), lambda j, i: (j, 0, 0))
    else:
        n = b.shape[1]
        b_spec = pl.BlockSpec((k, tn), lambda j, i: (0, j))

    def body(a_ref, b_ref, o_ref):
        o_ref[...] = jnp.dot(a_ref[...].astype(BF16), b_ref[...], preferred_element_type=F32).astype(out_dtype)

    return pl.pallas_call(
        body, out_shape=jax.ShapeDtypeStruct((m, n), out_dtype), grid=(n // tn, m // tm),
        in_specs=[pl.BlockSpec((tm, k), lambda j, i: (i, 0)), b_spec],
        out_specs=pl.BlockSpec((tm, tn), lambda j, i: (i, j)),
        name=name, compiler_params=_params(("parallel", "parallel")))(a, b)


def _mm_nt(a, b, *, tm, tk, tc, name, add=None, add_scale=1.0):
    m, n = a.shape
    if b.ndim == 3:
        assert tc == b.shape[2]
        k, nc = b.shape[1], b.shape[0]
        b_spec = pl.BlockSpec((None, tk, tc), lambda j, i, c: (c, j, 0))
    else:
        k, nc = b.shape[0], n // tc
        b_spec = pl.BlockSpec((tk, tc), lambda j, i, c: (j, c))

    def body(*refs):
        if add is None:
            a_ref, b_ref, o_ref = refs
        else:
            a_ref, b_ref, add_ref, o_ref = refs
        c = pl.program_id(2)
        part = _dot_nt(a_ref[...].astype(BF16), b_ref[...])

        @pl.when(c == 0)
        def _():
            if add is None:
                o_ref[...] = part
            else:
                o_ref[...] = part + add_scale * add_ref[...]

        @pl.when(c > 0)
        def _():
            o_ref[...] += part

    in_specs = [pl.BlockSpec((tm, tc), lambda j, i, c: (i, c)), b_spec]
    args = [a, b]
    if add is not None:
        in_specs.append(pl.BlockSpec((tm, tk), lambda j, i, c: (i, j)))
        args.append(add)
    return pl.pallas_call(
        body, out_shape=jax.ShapeDtypeStruct((m, k), F32), grid=(k // tk, m // tm, nc),
        in_specs=in_specs, out_specs=pl.BlockSpec((tm, tk), lambda j, i, c: (i, j)),
        name=name, compiler_params=_params(("parallel", "parallel", "arbitrary")))(*args)


def _mm_tn(a, b, *, tka, tn, tt, name, out_shards=None):
    t, ka = a.shape
    n = b.shape[1]
    if out_shards:
        assert tn == n // out_shards
        out_shape = jax.ShapeDtypeStruct((out_shards, ka, tn), F32)
        o_spec = pl.BlockSpec((None, tka, tn), lambda i, j, s: (j, i, 0))
    else:
        out_shape = jax.ShapeDtypeStruct((ka, n), F32)
        o_spec = pl.BlockSpec((tka, tn), lambda i, j, s: (i, j))

    def body(a_ref, b_ref, o_ref):
        s = pl.program_id(2)
        part = lax.dot_general(a_ref[...].astype(BF16), b_ref[...].astype(BF16), (((0,), (0,)), ((), ())),
                               preferred_element_type=F32)

        @pl.when(s == 0)
        def _():
            o_ref[...] = part

        @pl.when(s > 0)
        def _():
            o_ref[...] += part

    return pl.pallas_call(
        body, out_shape=out_shape, grid=(ka // tka, n // tn, t // tt),
        in_specs=[pl.BlockSpec((tt, tka), lambda i, j, s: (s, i)), pl.BlockSpec((tt, tn), lambda i, j, s: (s, j))],
        out_specs=o_spec, name=name, compiler_params=_params(("parallel", "parallel", "arbitrary")))(a, b)


CONV_TM = 512
CONV_TC = 1024


def _halo_specs(t, tm, tc, col0):
    nb8 = t // 8
    r8 = tm // 8
    return [
        pl.BlockSpec((8, tc), lambda i, j: (jnp.maximum(i * r8 - 1, 0), col0 + j)),
        pl.BlockSpec((tm, tc), lambda i, j: (i, col0 + j)),
        pl.BlockSpec((8, tc), lambda i, j: (jnp.minimum((i + 1) * r8, nb8 - 1), col0 + j)),
    ]


def _fill_ext(ext, prev_ref, cur_ref, next_ref, tm, i, last):
    ext[0:8, :] = jnp.where(i > 0, prev_ref[...], 0.0)
    ext[8:8 + tm, :] = cur_ref[...]
    ext[8 + tm:16 + tm, :] = jnp.where(i < last, next_ref[...], 0.0)


def _conv_fwd(u, conv_w, conv_b):
    t = u.shape[0]
    tm, tc = CONV_TM, CONV_TC

    def body(prev_ref, cur_ref, next_ref, w_ref, b_ref, o_ref, ext):
        _fill_ext(ext, prev_ref, cur_ref, next_ref, tm, pl.program_id(0), t // tm - 1)
        acc = jnp.broadcast_to(b_ref[...], (tm, tc))
        for k in range(KCONV):
            acc = acc + w_ref[k:k + 1, :] * ext[pl.ds(6 + k, tm), :]
        o_ref[...] = acc * _sigmoid(acc)

    return pl.pallas_call(
        body, out_shape=jax.ShapeDtypeStruct((t, CONVD), F32), grid=(t // tm, CONVD // tc),
        in_specs=_halo_specs(t, tm, tc, OXBC // tc) + [
            pl.BlockSpec((KCONV, tc), lambda i, j: (0, j)), pl.BlockSpec((1, tc), lambda i, j: (0, j))],
        out_specs=pl.BlockSpec((tm, tc), lambda i, j: (i, j)),
        scratch_shapes=[pltpu.VMEM((tm + 16, tc), F32)],
        name="conv_fwd", compiler_params=_params(("parallel", "parallel")))(u, u, u, conv_w, conv_b)


def _conv_dpre(u, dxs_f, dxs_b, dy, dbc_f, dbc_b, dsk_row, conv_w, conv_b):
    t = u.shape[0]
    tm, tc = CONV_TM, CONV_TC
    r8 = tm // 8
    nb8 = t // 8
    c0 = OXBC // tc

    def body(uprev, ucur, unext, f_ref, b_ref, y_ref, cf_ref, cb_ref, dsk_ref, w_ref, bias_ref,
             dpre_ref, dw_ref, db_ref, ext):
        j = pl.program_id(0)
        i = pl.program_id(1)
        _fill_ext(ext, uprev, ucur, unext, tm, i, t // tm - 1)
        pre = jnp.broadcast_to(bias_ref[...], (tm, tc))
        for k in range(KCONV):
            pre = pre + w_ref[k:k + 1, :] * ext[pl.ds(6 + k, tm), :]
        s = _sigmoid(pre)
        xs_part = f_ref[...] + b_ref[...] + dsk_ref[...] * y_ref[...]
        up = jnp.where(j < 2, xs_part, cf_ref[...] + cb_ref[...])
        dpre = up * (s * (1.0 + pre * (1.0 - s)))
        dpre_ref[...] = dpre
        rows = [jnp.sum(dpre * ext[pl.ds(6 + k, tm), :], axis=0, keepdims=True) for k in range(KCONV)]
        rows += [jnp.zeros((1, tc), F32)] * (8 - KCONV)
        dw_part = jnp.concatenate(rows, axis=0)
        db_part = jnp.broadcast_to(jnp.sum(dpre, axis=0, keepdims=True), (8, tc))

        @pl.when(i == 0)
        def _():
            dw_ref[...] = dw_part
            db_ref[...] = db_part

        @pl.when(i > 0)
        def _():
            dw_ref[...] += dw_part
            db_ref[...] += db_part

    xs_spec = pl.BlockSpec((tm, tc), lambda j, i: (i, jnp.minimum(j, 1)))
    bc_spec = pl.BlockSpec((tm, tc), lambda j, i: (i, 0))
    in_specs = [
        pl.BlockSpec((8, tc), lambda j, i: (jnp.maximum(i * r8 - 1, 0), c0 + j)),
        pl.BlockSpec((tm, tc), lambda j, i: (i, c0 + j)),
        pl.BlockSpec((8, tc), lambda j, i: (jnp.minimum((i + 1) * r8, nb8 - 1), c0 + j)),
        xs_spec, xs_spec, xs_spec, bc_spec, bc_spec,
        pl.BlockSpec((1, tc), lambda j, i: (0, jnp.minimum(j, 1))),
        pl.BlockSpec((KCONV, tc), lambda j, i: (0, j)), pl.BlockSpec((1, tc), lambda j, i: (0, j)),
    ]
    return pl.pallas_call(
        body,
        out_shape=(jax.ShapeDtypeStruct((t, CONVD), F32), jax.ShapeDtypeStruct((8, CONVD), F32),
                   jax.ShapeDtypeStruct((8, CONVD), F32)),
        grid=(CONVD // tc, t // tm), in_specs=in_specs,
        out_specs=(pl.BlockSpec((tm, tc), lambda j, i: (i, j)),
                   pl.BlockSpec((8, tc), lambda j, i: (0, j)), pl.BlockSpec((8, tc), lambda j, i: (0, j))),
        scratch_shapes=[pltpu.VMEM((tm + 16, tc), F32)],
        name="conv_dpre", compiler_params=_params(("parallel", "arbitrary")))(
            u, u, u, dxs_f, dxs_b, dy, dbc_f, dbc_b, dsk_row, conv_w, conv_b)


def _conv_dx(du, dpre, conv_w):
    t = dpre.shape[0]
    tm, tc = CONV_TM, CONV_TC
    r8 = tm // 8
    nb8 = t // 8

    def body(prev_ref, cur_ref, next_ref, w_ref, du_in, du_out, ext):
        del du_in
        _fill_ext(ext, prev_ref, cur_ref, next_ref, tm, pl.program_id(1), t // tm - 1)
        acc = jnp.zeros((tm, tc), F32)
        for k in range(KCONV):
            acc = acc + w_ref[k:k + 1, :] * ext[pl.ds(10 - k, tm), :]
        du_out[...] = acc

    in_specs = [
        pl.BlockSpec((8, tc), lambda j, i: (jnp.maximum(i * r8 - 1, 0), j)),
        pl.BlockSpec((tm, tc), lambda j, i: (i, j)),
        pl.BlockSpec((8, tc), lambda j, i: (jnp.minimum((i + 1) * r8, nb8 - 1), j)),
        pl.BlockSpec((KCONV, tc), lambda j, i: (0, j)),
        pl.BlockSpec(memory_space=pl.ANY),
    ]
    return pl.pallas_call(
        body, out_shape=jax.ShapeDtypeStruct(du.shape, F32), grid=(CONVD // tc, t // tm), in_specs=in_specs,
        out_specs=pl.BlockSpec((tm, tc), lambda j, i: (i, OXBC // tc + j)),
        scratch_shapes=[pltpu.VMEM((tm + 16, tc), F32)], input_output_aliases={4: 0},
        name="conv_dx", compiler_params=_params(("parallel", "parallel")))(dpre, dpre, dpre, conv_w, du)


def _ssd_common(dtr_ref, par_ref, rev):
    raw = dtr_ref[...]
    lane = _iota((1, 128), 1)
    mine = (lane >= 32 * rev) & (lane < 32 * rev + 32)
    bias = par_ref[0:1, :]
    arow = jnp.where(mine, -jnp.exp(par_ref[1:2, :]), 0.0)
    dt = _softplus(raw + bias)
    a = dt * arow
    ri = _iota((Q, Q), 0)
    ci = _iota((Q, Q), 1)
    tri = (ci >= ri) if rev else (ci <= ri)
    trit = (ci <= ri) if rev else (ci >= ri)
    cs = _dot01_l(tri.astype(BF16), a)
    return raw, bias, arow, mine, dt, cs, tri, trit


def _expand_mat(rev):
    r = _iota((128, DI), 0)
    c = _iota((128, DI), 1)
    return (r == (c // HP) + 32 * rev).astype(BF16)


def _sum_mat(rev):
    r = _iota((DI, 128), 0)
    c = _iota((DI, 128), 1)
    return (c == (r // HP) + 32 * rev).astype(BF16)


def _ssd_fwd(xbc, u, par, *, rev):
    t = xbc.shape[0]
    nc = t // Q
    end = 0 if rev else Q - 1
    cmap = (lambda c: nc - 1 - c) if rev else (lambda c: c)

    def body(xbc_ref, dtr_ref, par_ref, y_ref, st_ref, h_scr):
        step = pl.program_id(0)

        @pl.when(step == 0)
        def _():
            h_scr[...] = jnp.zeros((NS, DI), F32)

        raw, bias, arow, mine, dt, cs, tri, trit = _ssd_common(dtr_ref, par_ref, rev)
        cst = cs.T
        dtt = dt.T
        tot_col = cst[:, end:end + 1]
        wt = dtt * jnp.exp(tot_col - cst)
        gam = jnp.exp(cs[end:end + 1, :])
        gam_x = _dot01(jnp.broadcast_to(gam, (8, 128)), _expand_mat(rev))[0:1, :]
        lane = _iota((Q, 128), 1)
        sel = lane < HP
        st_ref[...] = h_scr[...]
        for g in range(NG):
            bg = xbc_ref[:, DI + NS * g:DI + NS * (g + 1)]
            cg = xbc_ref[:, DI + NG * NS + NS * g:DI + NG * NS + NS * (g + 1)]
            cb = _dot_nt(cg.astype(BF16), bg.astype(BF16))
            bt = bg.T
            for k in range(4):
                lo = 512 * g + 128 * k
                xp = xbc_ref[:, lo:lo + 128].astype(BF16)
                hp = h_scr[:, lo:lo + 128]
                rhs = jnp.concatenate([xp, hp.astype(BF16)], axis=0)
                ys, ss = [], []
                for j in range(2):
                    hc = 8 * g + 2 * k + j + 32 * rev
                    csc = jnp.broadcast_to(cs[:, hc:hc + 1], (Q, Q))
                    lm = jnp.exp(jnp.where(tri, csc - cst[hc:hc + 1, :], NEG)) * dtt[hc:hc + 1, :]
                    mh = (cb * lm).astype(BF16)
                    ec = (jnp.exp(csc) * cg).astype(BF16)
                    lhs = jnp.concatenate([mh, ec], axis=1)
                    ys.append(jnp.dot(lhs, rhs, preferred_element_type=F32))
                    bts = (bt * wt[hc:hc + 1, :]).astype(BF16)
                    ss.append(jnp.dot(bts, xp, preferred_element_type=F32))
                y_ref[:, lo:lo + 128] = jnp.where(sel, ys[0], ys[1])
                h_scr[:, lo:lo + 128] = gam_x[:, lo:lo + 128] * hp + jnp.where(sel, ss[0], ss[1])

    return pl.pallas_call(
        body,
        out_shape=(jax.ShapeDtypeStruct((t, DI), F32), jax.ShapeDtypeStruct((nc, NS, DI), F32)),
        grid=(nc,),
        in_specs=[pl.BlockSpec((Q, CONVD), lambda c: (cmap(c), 0)),
                  pl.BlockSpec((Q, 128), lambda c: (cmap(c), ODT // 128)),
                  pl.BlockSpec((8, 128), lambda c: (0, 0))],
        out_specs=(pl.BlockSpec((Q, DI), lambda c: (cmap(c), 0)),
                   pl.BlockSpec((None, NS, DI), lambda c: (cmap(c), 0, 0))),
        scratch_shapes=[pltpu.VMEM((NS, DI), F32)],
        name="ssd_fwd_rev" if rev else "ssd_fwd", compiler_params=_params(("arbitrary",)))(xbc, u, par)


def _ssd_bwd(xbc, u, par, dy, st, *, rev):
    t = xbc.shape[0]
    nc = t // Q
    end = 0 if rev else Q - 1
    cmap = (lambda c: c) if rev else (lambda c: nc - 1 - c)

    def body(xbc_ref, dtr_ref, par_ref, dy_ref, hin_ref, dxs_ref, dbc_ref, ddt_ref, acc_ref, dh_scr):
        step = pl.program_id(0)

        @pl.when(step == 0)
        def _():
            dh_scr[...] = jnp.zeros((NS, DI), F32)

        raw, bias, arow, mine, dt, cs, tri, trit = _ssd_common(dtr_ref, par_ref, rev)
        ri = _iota((Q, Q), 0)
        ci = _iota((Q, Q), 1)
        stri = ((ri > ci) if rev else (ri < ci)).astype(BF16)
        strit = ((ci > ri) if rev else (ci < ri)).astype(BF16)
        cst = cs.T
        dtt = dt.T
        et = jnp.exp(cst)
        expand = _expand_mat(rev)
        summat = _sum_mat(rev)
        gam = jnp.exp(cs[end:end + 1, :])
        gam_x = _dot01(jnp.broadcast_to(gam, (8, 128)), expand)[0:1, :]
        dt_hi, dt_mid, _ = _split3(dt)
        dtx = (jnp.dot(dt_hi, expand, preferred_element_type=F32)
               + jnp.dot(dt_mid, expand, preferred_element_type=F32))
        lane = _iota((Q, 128), 1)
        sel = lane < HP
        dho = dh_scr[...]
        t3 = jnp.sum(dho * hin_ref[...], axis=0, keepdims=True) * gam_x
        dxs_cols, dxs2_cols, yoff_cols, a1_rows = [], [], [], []
        for g in range(NG):
            bg = xbc_ref[:, DI + NS * g:DI + NS * (g + 1)]
            cg = xbc_ref[:, DI + NG * NS + NS * g:DI + NG * NS + NS * (g + 1)]
            bb = bg.astype(BF16)
            cbf = cg.astype(BF16)
            cb = _dot_nt(cbf, bb)
            cbt = _dot_nt(bb, cbf)
            ct = cg.T
            bdh = jnp.dot(bb, dho[:, 512 * g:512 * (g + 1)].astype(BF16), preferred_element_type=F32)
            dcb = jnp.zeros((Q, Q), F32)
            dcg = jnp.zeros((Q, NS), F32)
            dbg = jnp.zeros((Q, NS), F32)
            for k in range(4):
                lo = 512 * g + 128 * k
                xpf = xbc_ref[:, lo:lo + 128]
                xp = xpf.astype(BF16)
                dyp = dy_ref[:, lo:lo + 128]
                dypb = dyp.astype(BF16)
                hinp = hin_ref[:, lo:lo + 128].astype(BF16)
                dhp = dho[:, lo:lo + 128]
                d1, es, ws, dhs, yo = [], [], [], [], []
                for j in range(2):
                    hc = 8 * g + 2 * k + j + 32 * rev
                    csc = jnp.broadcast_to(cs[:, hc:hc + 1], (Q, Q))
                    csr = cst[hc:hc + 1, :]
                    lmd = jnp.exp(jnp.where(tri, csc - csr, NEG)) * dtt[hc:hc + 1, :]
                    lmb = jnp.exp(jnp.where(trit, csr - csc, NEG))
                    mt = (cbt * lmb).astype(BF16)
                    d1.append(jnp.dot(mt, dypb, preferred_element_type=F32))
                    dym = jnp.where(sel if j == 0 else ~sel, dyp, 0.0).astype(BF16)
                    dm = _dot_nt(dym, xp) * lmd
                    dcb = dcb + dm
                    gh, gm, _ = _split3(dm * cb)
                    rr = jnp.dot(gh, stri, preferred_element_type=F32) + jnp.dot(gm, stri, preferred_element_type=F32)
                    a1_rows.append(jnp.sum(jnp.where(tri, rr, 0.0), axis=0, keepdims=True))
                    ecs = jnp.exp(csc)
                    es.append(ecs)
                    ws.append(jnp.exp(cst[hc:hc + 1, end:end + 1] - csc))
                    yo.append(jnp.dot((ecs * cg).astype(BF16), hinp, preferred_element_type=F32))
                    cte = (ct * et[hc:hc + 1, :]).astype(BF16)
                    dhs.append(jnp.dot(cte, dypb, preferred_element_type=F32))
                e_p = jnp.where(sel, es[0], es[1])
                w_p = jnp.where(sel, ws[0], ws[1])
                d2 = w_p * bdh[:, 128 * k:128 * (k + 1)]
                dxs2_cols.append(d2)
                dxs_cols.append(jnp.where(sel, d1[0], d1[1]) + d2)
                yoff_cols.append(jnp.where(sel, yo[0], yo[1]))
                dcg = dcg + _dot_nt((e_p * dyp).astype(BF16), hinp)
                dbg = dbg + _dot_nt((w_p * dtx[:, lo:lo + 128] * xpf).astype(BF16), dhp.astype(BF16))
                dh_scr[:, lo:lo + 128] = gam_x[:, lo:lo + 128] * dhp + jnp.where(sel, dhs[0], dhs[1])
            dcg = dcg + jnp.dot(dcb.astype(BF16), bb, preferred_element_type=F32)
            dbg = dbg + jnp.dot(dcb.T.astype(BF16), cbf, preferred_element_type=F32)
            dbc_ref[:, NS * g:NS * (g + 1)] = dbg
            dbc_ref[:, NG * NS + NS * g:NG * NS + NS * (g + 1)] = dcg
        dxs = jnp.concatenate(dxs_cols, axis=1)
        dxs_ref[...] = dxs * dtx
        xs = xbc_ref[:, 0:DI]
        stacked = jnp.concatenate([xs * dxs, xs * jnp.concatenate(dxs2_cols, axis=1),
                                   dy_ref[...] * jnp.concatenate(yoff_cols, axis=1),
                                   jnp.broadcast_to(t3, (8, DI))], axis=0).astype(BF16)
        sums = jnp.dot(stacked, summat, preferred_element_type=F32)
        rx, rx2, ryo, c0 = sums[0:Q], sums[Q:2 * Q], sums[2 * Q:3 * Q], sums[3 * Q:3 * Q + 1]
        zero32 = jnp.zeros((32, Q), F32)
        a1t = jnp.concatenate(([zero32] if rev else []) + a1_rows + [zero32] * (2 if rev else 3), axis=0)
        da = a1t.T + _dot01_l(trit.astype(BF16), ryo) + _dot01_l(strit, dt * rx2) + jnp.where(mine, c0, 0.0)
        ddt = rx + da * arow
        ddtr = ddt * _sigmoid(raw + bias)
        ddt_ref[...] = ddtr
        part = jnp.concatenate([jnp.sum(ddtr, axis=0, keepdims=True),
                                jnp.sum(da * dt, axis=0, keepdims=True) * arow,
                                jnp.zeros((6, 128), F32)], axis=0)

        @pl.when(step == 0)
        def _():
            acc_ref[...] = part

        @pl.when(step > 0)
        def _():
            acc_ref[...] += part

    return pl.pallas_call(
        body,
        out_shape=(jax.ShapeDtypeStruct((t, DI), F32), jax.ShapeDtypeStruct((t, 2 * NG * NS), F32),
                   jax.ShapeDtypeStruct((t, 128), F32), jax.ShapeDtypeStruct((8, 128), F32)),
        grid=(nc,),
        in_specs=[pl.BlockSpec((Q, CONVD), lambda c: (cmap(c), 0)),
                  pl.BlockSpec((Q, 128), lambda c: (cmap(c), ODT // 128)),
                  pl.BlockSpec((8, 128), lambda c: (0, 0)),
                  pl.BlockSpec((Q, DI), lambda c: (cmap(c), 0)),
                  pl.BlockSpec((None, NS, DI), lambda c: (cmap(c), 0, 0))],
        out_specs=(pl.BlockSpec((Q, DI), lambda c: (cmap(c), 0)),
                   pl.BlockSpec((Q, 2 * NG * NS), lambda c: (cmap(c), 0)),
                   pl.BlockSpec((Q, 128), lambda c: (cmap(c), 0)),
                   pl.BlockSpec((8, 128), lambda c: (0, 0))),
        scratch_shapes=[pltpu.VMEM((NS, DI), F32)],
        name="ssd_bwd_rev" if rev else "ssd_bwd", compiler_params=_params(("arbitrary",)))(
            xbc, u, par, dy, st)


GN_TM = 256
GN_GROUP = DI // NG


def _gn_forward_vals(yf, yb, xs, z, dsk):
    y = yf + yb + dsk * xs
    sz = _sigmoid(z)
    gate = z * sz
    y2 = y * gate
    parts, rs = [], []
    for g in range(NG):
        seg = y2[:, GN_GROUP * g:GN_GROUP * (g + 1)]
        r = lax.rsqrt(jnp.mean(seg * seg, axis=1, keepdims=True) + NORM_EPS)
        rs.append(r)
        parts.append(seg * r)
    yn = jnp.concatenate(parts, axis=1)
    return y, sz, gate, yn, rs


def _gatenorm_fwd(y_f, y_b, xbc, u, dsk_row, nw_row):
    t = y_f.shape[0]
    tm = GN_TM

    def body(yf_ref, yb_ref, xs_ref, z_ref, dsk_ref, nw_ref, o_ref):
        _, _, _, yn, _ = _gn_forward_vals(yf_ref[...], yb_ref[...], xs_ref[...], z_ref[...], dsk_ref[...])
        o_ref[...] = (yn * nw_ref[...]).astype(BF16)

    blk = pl.BlockSpec((tm, DI), lambda i: (i, 0))
    row = pl.BlockSpec((1, DI), lambda i: (0, 0))
    return pl.pallas_call(
        body, out_shape=jax.ShapeDtypeStruct((t, DI), BF16), grid=(t // tm,),
        in_specs=[blk, blk, blk, pl.BlockSpec((tm, DI), lambda i: (i, OZ // DI)), row, row],
        out_specs=blk, name="gatenorm_fwd", compiler_params=_params(("parallel",)))(y_f, y_b, xbc, u, dsk_row, nw_row)


def _gatenorm_bwd(ds_out, y_f, y_b, xbc, u, du, dsk_row, nw_row):
    t = y_f.shape[0]
    tm = GN_TM

    def body(ds_ref, yf_ref, yb_ref, xs_ref, z_ref, dsk_ref, nw_ref, du_in, dy_ref, du_out, dnw_ref, dds_ref):
        del du_in
        i = pl.program_id(0)
        xs = xs_ref[...]
        z = z_ref[...]
        y, sz, gate, yn, rs = _gn_forward_vals(yf_ref[...], yb_ref[...], xs, z, dsk_ref[...])
        ds = ds_ref[...]
        gsc = ds * nw_ref[...]
        parts = []
        for g in range(NG):
            sl = slice(GN_GROUP * g, GN_GROUP * (g + 1))
            m = jnp.mean(gsc[:, sl] * yn[:, sl], axis=1, keepdims=True)
            parts.append(rs[g] * (gsc[:, sl] - yn[:, sl] * m))
        dy2 = jnp.concatenate(parts, axis=1)
        dy = dy2 * gate
        dy_ref[...] = dy
        du_out[...] = dy2 * y * (sz * (1.0 + z * (1.0 - sz)))
        dnw = jnp.broadcast_to(jnp.sum(ds * yn, axis=0, keepdims=True), (8, DI))
        drow = jnp.broadcast_to(jnp.sum(dy * xs, axis=0, keepdims=True), (8, DI))
        dds = _dot01(drow, _sum_mat(0))

        @pl.when(i == 0)
        def _():
            dnw_ref[...] = dnw
            dds_ref[...] = dds

        @pl.when(i > 0)
        def _():
            dnw_ref[...] += dnw
            dds_ref[...] += dds

    blk = pl.BlockSpec((tm, DI), lambda i: (i, 0))
    row = pl.BlockSpec((1, DI), lambda i: (0, 0))
    return pl.pallas_call(
        body,
        out_shape=(jax.ShapeDtypeStruct((t, DI), F32), jax.ShapeDtypeStruct(du.shape, F32),
                   jax.ShapeDtypeStruct((8, DI), F32), jax.ShapeDtypeStruct((8, 128), F32)),
        grid=(t // tm,),
        in_specs=[blk, blk, blk, blk, pl.BlockSpec((tm, DI), lambda i: (i, OZ // DI)), row, row,
                  pl.BlockSpec(memory_space=pl.ANY)],
        out_specs=(blk, pl.BlockSpec((tm, DI), lambda i: (i, OZ // DI)),
                   pl.BlockSpec((8, DI), lambda i: (0, 0)), pl.BlockSpec((8, 128), lambda i: (0, 0))),
        input_output_aliases={7: 1},
        name="gatenorm_bwd", compiler_params=_params(("arbitrary",)))(ds_out, y_f, y_b, xbc, u, dsk_row, nw_row, du)


AT_B = 128
AT_W = AT_B + 2 * ATT_HALF
AT_L = 2 * AH
SCALE = 1.0 / math.sqrt(AH)


def _slope(g, hh):
    return 2.0 ** (-8.0 * (4 * g + hh + 1) / 12.0)


def _qcol(g):
    return lambda p: OQ // AT_L + 2 * g + p


def _kcol(g):
    return lambda p: OKV // AT_L + 4 * g + 2 * p


def _vcol(g):
    return lambda p: OKV // AT_L + 4 * g + 2 * p + 1


def _pcol(p):
    return p


def _win_specs(col, t, d):
    tb, hb = AT_B * d, ATT_HALF * d
    nh = t // hb
    return [
        pl.BlockSpec((hb, AT_L), lambda p, i: (jnp.maximum(2 * i - 1, 0), col(p))),
        pl.BlockSpec((tb, AT_L), lambda p, i: (i, col(p))),
        pl.BlockSpec((hb, AT_L), lambda p, i: (jnp.minimum(2 * i + 2, nh - 1), col(p))),
    ]


def _blk_spec(col, d):
    return pl.BlockSpec((AT_B * d, AT_L), lambda p, i: (i, col(p)))


def _rows(ref, r, n, d):
    return ref[pl.ds(r, n, stride=d), :] if d > 1 else ref[...]


def _win(p_ref, c_ref, n_ref, r, d):
    return jnp.concatenate([_rows(p_ref, r, ATT_HALF, d), _rows(c_ref, r, AT_B, d), _rows(n_ref, r, ATT_HALF, d)], axis=0)


def _put_rows(ref, r, d, val):
    if d > 1:
        ref[pl.ds(r, AT_B, stride=d), :] = val
    else:
        ref[...] = val


def _for_residues(d, fn):
    if d == 1:
        fn(0)
    else:
        def step(r, c):
            fn(r)
            return c
        lax.fori_loop(0, d, step, 0)


def _attn_geometry(i, ln, d):
    a = i * AT_B + _iota((AT_B, AT_W), 0)
    b = i * AT_B - ATT_HALF + _iota((AT_B, AT_W), 1)
    rel = jnp.abs(a - b)
    valid = (rel <= ATT_HALF) & (b >= 0) & (b < ln)
    return valid, (rel * d).astype(F32)


def _attn_fwd(u, g):
    t = u.shape[0]
    d = DILATIONS[g]
    ln = t // d

    def body(q_ref, kp, kc, kn, vp, vc, vn, o_ref, l_ref):
        p_id = pl.program_id(0)
        i = pl.program_id(1)
        valid, dist = _attn_geometry(i, ln, d)
        lane = _iota((AT_B, AT_L), 1)

        def one(r):
            q = _rows(q_ref, r, AT_B, d)
            kw = _win(kp, kc, kn, r, d).astype(BF16)
            vw = _win(vp, vc, vn, r, d).astype(BF16)
            o = jnp.zeros((AT_B, AT_L), F32)
            lse = jnp.zeros((AT_B, AT_L), F32)
            for hh in range(2):
                hm = (lane // AH) == hh
                slope = jnp.where(p_id == 0, _slope(g, hh), _slope(g, 2 + hh))
                qm = jnp.where(hm, q, 0.0).astype(BF16)
                s = _dot_nt(qm, kw) * SCALE - slope * dist
                s = jnp.where(valid, s, NEG)
                m = jnp.max(s, axis=1, keepdims=True)
                pr = jnp.exp(s - m)
                den = jnp.sum(pr, axis=1, keepdims=True)
                oh = jnp.dot(pr.astype(BF16), vw, preferred_element_type=F32)
                o = jnp.where(hm, oh / den, o)
                lse = jnp.where(hm, m + jnp.log(den), lse)
            _put_rows(o_ref, r, d, o)
            _put_rows(l_ref, r, d, lse)

        _for_residues(d, one)

    oshape = jax.ShapeDtypeStruct((t, 2 * AT_L), F32)
    ospec = _blk_spec(_pcol, d)
    return pl.pallas_call(
        body, out_shape=(oshape, oshape), grid=(2, t // (AT_B * d)),
        in_specs=[_blk_spec(_qcol(g), d)] + _win_specs(_kcol(g), t, d) + _win_specs(_vcol(g), t, d),
        out_specs=(ospec, ospec), name=f"attn_fwd_{g}", compiler_params=_params(("parallel", "parallel")))(
            u, u, u, u, u, u, u)


def _attn_dq(u, du, do, lse, e, g):
    t = u.shape[0]
    d = DILATIONS[g]
    ln = t // d

    def body(q_ref, kp, kc, kn, vp, vc, vn, do_ref, l_ref, e_ref, du_in, dq_ref):
        del du_in
        p_id = pl.program_id(0)
        i = pl.program_id(1)
        valid, dist = _attn_geometry(i, ln, d)
        lane = _iota((AT_B, AT_L), 1)

        def one(r):
            q = _rows(q_ref, r, AT_B, d)
            kw = _win(kp, kc, kn, r, d).astype(BF16)
            vw = _win(vp, vc, vn, r, d).astype(BF16)
            do_ = _rows(do_ref, r, AT_B, d)
            lv = _rows(l_ref, r, AT_B, d)
            ev = _rows(e_ref, r, AT_B, d)
            dq = jnp.zeros((AT_B, AT_L), F32)
            for hh in range(2):
                hm = (lane // AH) == hh
                slope = jnp.where(p_id == 0, _slope(g, hh), _slope(g, 2 + hh))
                qm = jnp.where(hm, q, 0.0).astype(BF16)
                s = _dot_nt(qm, kw) * SCALE - slope * dist
                lcol = jnp.broadcast_to(lv[:, AH * hh:AH * hh + 1], (AT_B, AT_W))
                ecol = jnp.broadcast_to(ev[:, AH * hh:AH * hh + 1], (AT_B, AT_W))
                pr = jnp.exp(jnp.where(valid, s - lcol, NEG))
                dom = jnp.where(hm, do_, 0.0).astype(BF16)
                ds = pr * (_dot_nt(dom, vw) + ecol)
                dqh = jnp.dot(ds.astype(BF16), kw, preferred_element_type=F32) * SCALE
                dq = jnp.where(hm, dqh, dq)
            _put_rows(dq_ref, r, d, dq)

        _for_residues(d, one)

    rspec = _blk_spec(_pcol, d)
    return pl.pallas_call(
        body, out_shape=jax.ShapeDtypeStruct(du.shape, F32), grid=(2, t // (AT_B * d)),
        in_specs=[_blk_spec(_qcol(g), d)] + _win_specs(_kcol(g), t, d) + _win_specs(_vcol(g), t, d)
        + [rspec, rspec, rspec, pl.BlockSpec(memory_space=pl.ANY)],
        out_specs=_blk_spec(_qcol(g), d), input_output_aliases={10: 0},
        name=f"attn_dq_{g}", compiler_params=_params(("parallel", "parallel")))(
            u, u, u, u, u, u, u, do, lse, e, du)


def _attn_dkv(u, du, do, lse, e, g):
    t = u.shape[0]
    d = DILATIONS[g]
    ln = t // d

    def body(k_ref, v_ref, qp, qc, qn, dp_, dc_, dn_, lp, lc, ln_, ep, ec, en, du_in, dkv_ref, dk_scr, dv_scr):
        del du_in
        p_id = pl.program_id(0)
        j = pl.program_id(1)
        valid, dist = _attn_geometry(j, ln, d)
        lane = _iota((AT_B, AT_L), 1)

        def one(r):
            k = _rows(k_ref, r, AT_B, d)
            v = _rows(v_ref, r, AT_B, d)
            qw = _win(qp, qc, qn, r, d).astype(BF16)
            dow = _win(dp_, dc_, dn_, r, d).astype(BF16)
            lt = _win(lp, lc, ln_, r, d).T
            et = _win(ep, ec, en, r, d).T
            dk = jnp.zeros((AT_B, AT_L), F32)
            dv = jnp.zeros((AT_B, AT_L), F32)
            for hh in range(2):
                hm = (lane // AH) == hh
                slope = jnp.where(p_id == 0, _slope(g, hh), _slope(g, 2 + hh))
                km = jnp.where(hm, k, 0.0).astype(BF16)
                st = _dot_nt(km, qw) * SCALE - slope * dist
                pt = jnp.exp(jnp.where(valid, st - lt[AH * hh:AH * hh + 1, :], NEG))
                dvh = jnp.dot(pt.astype(BF16), dow, preferred_element_type=F32)
                vm = jnp.where(hm, v, 0.0).astype(BF16)
                dst = pt * (_dot_nt(vm, dow) + et[AH * hh:AH * hh + 1, :])
                dkh = jnp.dot(dst.astype(BF16), qw, preferred_element_type=F32) * SCALE
                dk = jnp.where(hm, dkh, dk)
                dv = jnp.where(hm, dvh, dv)
            _put_rows(dk_scr, r, d, dk)
            _put_rows(dv_scr, r, d, dv)

        _for_residues(d, one)
        dkv_ref[:, 0:AT_L] = dk_scr[...]
        dkv_ref[:, AT_L:2 * AT_L] = dv_scr[...]

    return pl.pallas_call(
        body, out_shape=jax.ShapeDtypeStruct(du.shape, F32), grid=(2, t // (AT_B * d)),
        in_specs=[_blk_spec(_kcol(g), d), _blk_spec(_vcol(g), d)]
        + _win_specs(_qcol(g), t, d) + _win_specs(_pcol, t, d) + _win_specs(_pcol, t, d) + _win_specs(_pcol, t, d)
        + [pl.BlockSpec(memory_space=pl.ANY)],
        out_specs=pl.BlockSpec((AT_B * d, 2 * AT_L), lambda p, i: (i, OKV // (2 * AT_L) + 2 * g + p)),
        input_output_aliases={14: 0},
        scratch_shapes=[pltpu.VMEM((AT_B * d, AT_L), F32), pltpu.VMEM((AT_B * d, AT_L), F32)],
        name=f"attn_dkv_{g}", compiler_params=_params(("parallel", "parallel")))(
            u, u, u, u, u, do, do, do, lse, lse, lse, e, e, e, du)


CMB_TM = 1024


def _combine_weights(l0, l1, l2):
    m = jnp.maximum(jnp.maximum(l0, l1), l2)
    e0, e1, e2 = jnp.exp(l0 - m), jnp.exp(l1 - m), jnp.exp(l2 - m)
    inv = 1.0 / (e0 + e1 + e2)
    return e0 * inv, e1 * inv, e2 * inv


def _combine_fwd(os_, ls_):
    t = os_[0].shape[0]
    tm = CMB_TM

    def body(o0, o1, o2, l0, l1, l2, a_ref):
        w0, w1, w2 = _combine_weights(l0[...], l1[...], l2[...])
        a_ref[...] = w0 * o0[...] + w1 * o1[...] + w2 * o2[...]

    blk = pl.BlockSpec((tm, 2 * AT_L), lambda i: (i, 0))
    return pl.pallas_call(
        body, out_shape=jax.ShapeDtypeStruct((t, 2 * AT_L), F32), grid=(t // tm,), in_specs=[blk] * 6, out_specs=blk,
        name="combine_fwd", compiler_params=_params(("parallel",)))(*os_, *ls_)


def _combine_bwd(datt, os_, ls_):
    t = datt.shape[0]
    tm = CMB_TM

    def body(da_ref, o0, o1, o2, l0, l1, l2, d0, d1, d2, e0, e1, e2):
        w = _combine_weights(l0[...], l1[...], l2[...])
        da = da_ref[...]
        att = w[0] * o0[...] + w[1] * o1[...] + w[2] * o2[...]
        r = _iota((2 * AT_L, 2 * AT_L), 0) // AH
        c = _iota((2 * AT_L, 2 * AT_L), 1) // AH
        hs = _dot01(da * att, (r == c).astype(BF16))
        for wg, dref, eref in zip(w, (d0, d1, d2), (e0, e1, e2)):
            dref[...] = wg * da
            eref[...] = -wg * hs

    blk = pl.BlockSpec((tm, 2 * AT_L), lambda i: (i, 0))
    shp = jax.ShapeDtypeStruct((t, 2 * AT_L), F32)
    outs = pl.pallas_call(
        body, out_shape=(shp,) * 6, grid=(t // tm,), in_specs=[blk] * 7, out_specs=(blk,) * 6,
        name="combine_bwd", compiler_params=_params(("parallel",)))(datt, *os_, *ls_)
    return outs[0:3], outs[3:6]


ROW_TM = 512


def _mix_fwd(y_ssd, y_att, u, bg_row):
    t = y_ssd.shape[0]
    tm = ROW_TM

    def body(ys_ref, ya_ref, g0_ref, g1_ref, b0_ref, b1_ref, o_ref):
        g0 = _sigmoid(g0_ref[...] + b0_ref[...])
        g1 = _sigmoid(g1_ref[...] + b1_ref[...])
        o_ref[...] = (g0 * ys_ref[...] + g1 * ya_ref[...]).astype(BF16)

    blk = pl.BlockSpec((tm, D), lambda i: (i, 0))
    return pl.pallas_call(
        body, out_shape=jax.ShapeDtypeStruct((t, D), BF16), grid=(t // tm,),
        in_specs=[blk, blk, pl.BlockSpec((tm, D), lambda i: (i, OGATE // D)), pl.BlockSpec((tm, D), lambda i: (i, OGATE // D + 1)),
                  pl.BlockSpec((1, D), lambda i: (0, 0)), pl.BlockSpec((1, D), lambda i: (0, 1))],
        out_specs=blk, name="mix_fwd", compiler_params=_params(("parallel",)))(y_ssd, y_att, u, u, bg_row, bg_row)


def _mix_bwd(dmixin, y_ssd, y_att, u, bg_row):
    t = y_ssd.shape[0]
    tm = ROW_TM

    def body(dm_ref, ys_ref, ya_ref, g0_ref, g1_ref, b0_ref, b1_ref, dys_ref, dya_ref, du_ref, db_ref):
        i = pl.program_id(0)
        g0 = _sigmoid(g0_ref[...] + b0_ref[...])
        g1 = _sigmoid(g1_ref[...] + b1_ref[...])
        dm = dm_ref[...]
        dys_ref[...] = (dm * g0).astype(BF16)
        dya_ref[...] = (dm * g1).astype(BF16)
        dl0 = dm * ys_ref[...] * g0 * (1.0 - g0)
        dl1 = dm * ya_ref[...] * g1 * (1.0 - g1)
        du_ref[:, 0:D] = dl0
        du_ref[:, D:2 * D] = dl1
        part = jnp.concatenate([jnp.broadcast_to(jnp.sum(dl0, axis=0, keepdims=True), (8, D)),
                                jnp.broadcast_to(jnp.sum(dl1, axis=0, keepdims=True), (8, D))], axis=1)

        @pl.when(i == 0)
        def _():
            db_ref[...] = part

        @pl.when(i > 0)
        def _():
            db_ref[...] += part

    blk = pl.BlockSpec((tm, D), lambda i: (i, 0))
    return pl.pallas_call(
        body,
        out_shape=(jax.ShapeDtypeStruct((t, D), BF16), jax.ShapeDtypeStruct((t, D), BF16),
                   jax.ShapeDtypeStruct((t, UW), F32), jax.ShapeDtypeStruct((8, 2 * D), F32)),
        grid=(t // tm,),
        in_specs=[blk, blk, blk, pl.BlockSpec((tm, D), lambda i: (i, OGATE // D)), pl.BlockSpec((tm, D), lambda i: (i, OGATE // D + 1)),
                  pl.BlockSpec((1, D), lambda i: (0, 0)), pl.BlockSpec((1, D), lambda i: (0, 1))],
        out_specs=(blk, blk, pl.BlockSpec((tm, 2 * D), lambda i: (i, OGATE // (2 * D))),
                   pl.BlockSpec((8, 2 * D), lambda i: (0, 0))),
        name="mix_bwd", compiler_params=_params(("arbitrary",)))(dmixin, y_ssd, y_att, u, u, bg_row, bg_row)


def _ln(x, g, b):
    mu = jnp.mean(x, axis=1, keepdims=True)
    xc = x - mu
    var = jnp.mean(xc * xc, axis=1, keepdims=True)
    rstd = lax.rsqrt(var + NORM_EPS)
    xhat = xc * rstd
    return xhat * g + b, xhat, rstd


def _ln_back(dh, xhat, rstd, g):
    dxh = dh * g
    m1 = jnp.mean(dxh, axis=1, keepdims=True)
    m2 = jnp.mean(dxh * xhat, axis=1, keepdims=True)
    return rstd * (dxh - m1 - xhat * m2)


def _ln1_fwd(x, mix, g_row, b_row):
    t = x.shape[0]
    tm = ROW_TM

    def body(x_ref, m_ref, g_ref, b_ref, pre_ref, h_ref):
        pre = ALPHA * x_ref[...] + m_ref[...]
        pre_ref[...] = pre
        h, _, _ = _ln(pre, g_ref[...], b_ref[...])
        h_ref[...] = h.astype(BF16)

    blk = pl.BlockSpec((tm, D), lambda i: (i, 0))
    row = pl.BlockSpec((1, D), lambda i: (0, 0))
    return pl.pallas_call(
        body, out_shape=(jax.ShapeDtypeStruct((t, D), F32), jax.ShapeDtypeStruct((t, D), BF16)), grid=(t // tm,),
        in_specs=[blk, blk, row, row], out_specs=(blk, blk),
        name="ln1_fwd", compiler_params=_params(("parallel",)))(x, mix, g_row, b_row)


def _ln1_bwd(dh, pre, g_row, b_row):
    t = dh.shape[0]
    tm = ROW_TM

    def body(dh_ref, pre_ref, g_ref, b_ref, dpre_ref, acc_ref):
        i = pl.program_id(0)
        dh_ = dh_ref[...]
        _, xhat, rstd = _ln(pre_ref[...], g_ref[...], b_ref[...])
        dpre_ref[...] = _ln_back(dh_, xhat, rstd, g_ref[...])
        part = jnp.concatenate([jnp.sum(dh_ * xhat, axis=0, keepdims=True), jnp.sum(dh_, axis=0, keepdims=True),
                                jnp.zeros((6, D), F32)], axis=0)

        @pl.when(i == 0)
        def _():
            acc_ref[...] = part

        @pl.when(i > 0)
        def _():
            acc_ref[...] += part

    blk = pl.BlockSpec((tm, D), lambda i: (i, 0))
    row = pl.BlockSpec((1, D), lambda i: (0, 0))
    return pl.pallas_call(
        body, out_shape=(jax.ShapeDtypeStruct((t, D), F32), jax.ShapeDtypeStruct((8, D), F32)), grid=(t // tm,),
        in_specs=[blk, blk, row, row], out_specs=(blk, pl.BlockSpec((8, D), lambda i: (0, 0))),
        name="ln1_bwd", compiler_params=_params(("arbitrary",)))(dh, pre, g_row, b_row)


def _ln2_loss(pre1, f, tgt, g1_row, b1_row, g2_row, b2_row):
    t = pre1.shape[0]
    tm = ROW_TM

    def body(p1_ref, f_ref, t_ref, g1_ref, b1_ref, g2_ref, b2_ref, dpre_ref, acc_ref):
        i = pl.program_id(0)
        h1, _, _ = _ln(p1_ref[...], g1_ref[...], b1_ref[...])
        pre2 = ALPHA * h1 + f_ref[...]
        h2, xhat, rstd = _ln(pre2, g2_ref[...], b2_ref[...])
        err = h2 - t_ref[...]
        dh = err * (1.0 / D)
        dpre_ref[...] = _ln_back(dh, xhat, rstd, g2_ref[...])
        loss = jnp.sum(jnp.sum(err * err, axis=1, keepdims=True), axis=0, keepdims=True) * (0.5 / D)
        part = jnp.concatenate([jnp.sum(dh * xhat, axis=0, keepdims=True), jnp.sum(dh, axis=0, keepdims=True),
                                jnp.broadcast_to(loss, (1, D)), jnp.zeros((5, D), F32)], axis=0)

        @pl.when(i == 0)
        def _():
            acc_ref[...] = part

        @pl.when(i > 0)
        def _():
            acc_ref[...] += part

    blk = pl.BlockSpec((tm, D), lambda i: (i, 0))
    row = pl.BlockSpec((1, D), lambda i: (0, 0))
    return pl.pallas_call(
        body, out_shape=(jax.ShapeDtypeStruct((t, D), F32), jax.ShapeDtypeStruct((8, D), F32)), grid=(t // tm,),
        in_specs=[blk, blk, blk, row, row, row, row], out_specs=(blk, pl.BlockSpec((8, D), lambda i: (0, 0))),
        name="ln2_loss", compiler_params=_params(("arbitrary",)))(pre1, f, tgt, g1_row, b1_row, g2_row, b2_row)


def _mlp_up(h1, w_up):
    t = h1.shape[0]
    tm, tn = ROW_TM, D

    def body(a_ref, b_ref, up_ref, act_ref):
        up = jnp.dot(a_ref[...], b_ref[...], preferred_element_type=F32)
        up_ref[...] = up
        r = jnp.maximum(up, 0.0)
        act_ref[...] = (r * r).astype(BF16)

    blk = pl.BlockSpec((tm, tn), lambda j, i: (i, j))
    return pl.pallas_call(
        body, out_shape=(jax.ShapeDtypeStruct((t, DFF), F32), jax.ShapeDtypeStruct((t, DFF), BF16)),
        grid=(DFF // tn, t // tm),
        in_specs=[pl.BlockSpec((tm, D), lambda j, i: (i, 0)), pl.BlockSpec((None, D, tn), lambda j, i: (j, 0, 0))],
        out_specs=(blk, blk), name="mlp_up", compiler_params=_params(("parallel", "parallel")))(h1, w_up)


def _d_up(dpre2, w_down, up):
    t = up.shape[0]
    tm, tk = ROW_TM, D

    def body(a_ref, b_ref, u_ref, o_ref):
        dact = _dot_nt(a_ref[...].astype(BF16), b_ref[...])
        o_ref[...] = (dact * 2.0 * jnp.maximum(u_ref[...], 0.0)).astype(BF16)

    blk = pl.BlockSpec((tm, tk), lambda j, i: (i, j))
    return pl.pallas_call(
        body, out_shape=jax.ShapeDtypeStruct((t, DFF), BF16), grid=(DFF // tk, t // tm),
        in_specs=[pl.BlockSpec((tm, D), lambda j, i: (i, 0)), pl.BlockSpec((tk, D), lambda j, i: (j, 0)), blk],
        out_specs=blk, name="d_up", compiler_params=_params(("parallel", "parallel")))(dpre2, w_down, up)


def _dt_bwd(du, ddt_f, ddt_b):
    t = ddt_f.shape[0]
    tm = 1024

    def body(f_ref, b_ref, du_in, o_ref):
        del du_in
        o_ref[:, 0:128] = f_ref[...] + b_ref[...]
        o_ref[:, 128:256] = jnp.zeros((tm, 128), F32)

    blk = pl.BlockSpec((tm, 128), lambda i: (i, 0))
    return pl.pallas_call(
        body, out_shape=jax.ShapeDtypeStruct(du.shape, F32), grid=(t // tm,),
        in_specs=[blk, blk, pl.BlockSpec(memory_space=pl.ANY)],
        out_specs=pl.BlockSpec((tm, 256), lambda i: (i, ODT // 256)), input_output_aliases={2: 0},
        name="dt_bwd", compiler_params=_params(("parallel",)))(ddt_f, ddt_b, du)


def _adamw(w, g, m, v, name):
    r, c = w.shape
    tr = r
    for cand in (256, 128, 64, 32, 16, 8):
        if r % cand == 0 and cand * c * 4 <= 2 ** 21:
            tr = cand
            break
    bc1 = 1.0 / (1.0 - ADAM_B1 ** ADAM_STEP)
    bc2 = 1.0 / (1.0 - ADAM_B2 ** ADAM_STEP)

    def body(w_ref, g_ref, m_ref, v_ref, d_ref, nm_ref, nv_ref):
        gg = g_ref[...]
        nm = ADAM_B1 * m_ref[...] + (1.0 - ADAM_B1) * gg
        nv = ADAM_B2 * v_ref[...] + (1.0 - ADAM_B2) * (gg * gg)
        nm_ref[...] = nm
        nv_ref[...] = nv
        d_ref[...] = -ADAM_LR * ((nm * bc1) / (jnp.sqrt(nv * bc2) + ADAM_EPS) + ADAM_WD * w_ref[...])

    blk = pl.BlockSpec((tr, c), lambda i: (i, 0))
    shp = jax.ShapeDtypeStruct((r, c), F32)
    return pl.pallas_call(body, out_shape=(shp, shp, shp), grid=(r // tr,), in_specs=[blk] * 4, out_specs=(blk,) * 3,
                          name=name, compiler_params=_params(("parallel",)))(w, g, m, v)


def _perm_cols(w):
    z, xbc, dt = w[:, 0:2048], w[:, 2048:5120], w[:, 5120:5184]
    q, k, v, gate = w[:, 5184:5952], w[:, 5952:6720], w[:, 6720:7488], w[:, 7488:9536]
    kv = []
    for g in range(3):
        for p in range(2):
            lo = 256 * g + 128 * p
            kv += [k[:, lo:lo + 128], v[:, lo:lo + 128]]
    pad = jnp.zeros((w.shape[0], UW - IN_COLS), w.dtype)
    return jnp.concatenate([z, gate, xbc] + kv + [q, dt, pad], axis=1)


def _unperm_cols(wp):
    z, gate, xbc = wp[:, OZ:OZ + 2048], wp[:, OGATE:OGATE + 2048], wp[:, OXBC:OXBC + CONVD]
    q, dt = wp[:, OQ:OQ + 768], wp[:, ODT:ODT + 64]
    ks, vs = [], []
    for g in range(3):
        for p in range(2):
            lo = OKV + 128 * (4 * g + 2 * p)
            ks.append(wp[:, lo:lo + 128])
            vs.append(wp[:, lo + 128:lo + 256])
    return jnp.concatenate([z, xbc, dt, q] + ks + vs + [gate], axis=1)


def _lanes128(*vecs):
    v = jnp.concatenate([a.reshape(-1) for a in vecs])
    return jnp.pad(v, (0, 128 - v.shape[0])).reshape(1, 128)


def _local_grads(x, tgt, wts, sm):
    row = lambda a: a.reshape(1, -1)
    bg_row, cb_row = row(sm["b_gate"]), row(sm["conv_b"])
    par = jnp.concatenate([_lanes128(sm["dt_bias_f"], sm["dt_bias_b"]), _lanes128(sm["a_log_f"], sm["a_log_b"]),
                           jnp.zeros((6, 128), F32)], axis=0)
    dsk_row = row(jnp.repeat(sm["d_skip"], HP))
    nw_row = row(sm["ssd_norm_w"])
    g1, b1, g2, b2 = row(sm["ln1_g"]), row(sm["ln1_b"]), row(sm["ln2_g"]), row(sm["ln2_b"])

    u = _mm_nn(x, wts["w_in_p"], tm=512, tn=2432, name="in_proj")
    xbc = _conv_fwd(u, sm["conv_w"], cb_row)
    y_f, st_f = _ssd_fwd(xbc, u, par, rev=False)
    y_b, st_b = _ssd_fwd(xbc, u, par, rev=True)
    s_out = _gatenorm_fwd(y_f, y_b, xbc, u, dsk_row, nw_row)
    y_ssd = _mm_nn(s_out, wts["w_proj_ssd"], tm=512, tn=1024, name="proj_ssd")
    att_o, att_l = [], []
    for g in range(3):
        o, l = _attn_fwd(u, g)
        att_o.append(o)
        att_l.append(l)
    att = _combine_fwd(att_o, att_l)
    y_att = _mm_nn(att, wts["w_proj_attn"], tm=512, tn=256, name="proj_attn")
    mixin = _mix_fwd(y_ssd, y_att, u, bg_row)
    mix = _mm_nn(mixin, wts["w_out"], tm=512, tn=1024, name="out_proj")
    pre1, h1 = _ln1_fwd(x, mix, g1, b1)
    up, act = _mlp_up(h1, wts["w_up"])
    f = _mm_nn(act, wts["w_down"], tm=512, tn=1024, name="mlp_down")
    dpre2, acc2 = _ln2_loss(pre1, f, tgt, g1, b1, g2, b2)

    dw_down = _mm_tn(act, dpre2, tka=1024, tn=1024, tt=512, name="dw_down")
    dup = _d_up(dpre2, wts["w_down"], up)
    dw_up = _mm_tn(h1, dup, tka=1024, tn=1024, tt=512, name="dw_up", out_shards=4)
    dh1 = _mm_nt(dup, wts["w_up"], tm=512, tk=1024, tc=1024, name="d_h1", add=dpre2, add_scale=ALPHA)
    dpre1, acc1 = _ln1_bwd(dh1, pre1, g1, b1)
    dmixin = _mm_nt(dpre1, wts["w_out"], tm=512, tk=1024, tc=1024, name="d_mixin")
    dw_out = _mm_tn(mixin, dpre1, tka=1024, tn=1024, tt=512, name="dw_out")
    dy_ssd, dy_att, du, dbg = _mix_bwd(dmixin, y_ssd, y_att, u, bg_row)
    dw_proj_ssd = _mm_tn(s_out, dy_ssd, tka=1024, tn=1024, tt=512, name="dw_proj_ssd")
    ds_out = _mm_nt(dy_ssd, wts["w_proj_ssd"], tm=512, tk=1024, tc=1024, name="d_s_out")
    dw_proj_attn = _mm_tn(att, dy_att, tka=256, tn=256, tt=512, name="dw_proj_attn", out_shards=4)
    datt = _mm_nt(dy_att, wts["w_proj_attn"], tm=512, tk=256, tc=256, name="d_att")
    do_g, e_g = _combine_bwd(datt, att_o, att_l)
    for g in range(3):
        du = _attn_dq(u, du, do_g[g], att_l[g], e_g[g], g)
        du = _attn_dkv(u, du, do_g[g], att_l[g], e_g[g], g)
    dy, du, dnw, dds = _gatenorm_bwd(ds_out, y_f, y_b, xbc, u, du, dsk_row, nw_row)
    dxs_f, dbc_f, ddt_f, sacc_f = _ssd_bwd(xbc, u, par, dy, st_f, rev=False)
    dxs_b, dbc_b, ddt_b, sacc_b = _ssd_bwd(xbc, u, par, dy, st_b, rev=True)
    dpre_c, dcw, dcb = _conv_dpre(u, dxs_f, dxs_b, dy, dbc_f, dbc_b, dsk_row, sm["conv_w"], cb_row)
    du = _conv_dx(du, dpre_c, sm["conv_w"])
    du = _dt_bwd(du, ddt_f, ddt_b)
    dw_in_p = _mm_tn(x, du, tka=1024, tn=2432, tt=512, name="dw_in")
    dx = _mm_nt(du, wts["w_in_p"], tm=512, tk=1024, tc=2432, name="d_x", add=dpre1, add_scale=ALPHA)

    sacc = sacc_f + sacc_b
    small = {
        "b_gate": dbg[0], "conv_w": dcw[0:KCONV], "conv_b": dcb[0],
        "dt_bias_f": sacc[0, 0:32], "dt_bias_b": sacc[0, 32:64], "a_log_f": sacc[1, 0:32], "a_log_b": sacc[1, 32:64],
        "d_skip": dds[0, 0:32], "ssd_norm_w": dnw[0],
        "ln1_g": acc1[0], "ln1_b": acc1[1], "ln2_g": acc2[0], "ln2_b": acc2[1], "loss": acc2[2, 0:1],
    }
    dw_in = _unperm_cols(dw_in_p)
    big = {
        "w_in": dw_in.reshape(D, 4, IN_COLS // 4).transpose(1, 0, 2),
        "w_proj_ssd": dw_proj_ssd.reshape(4, DI // 4, D),
        "w_proj_attn": dw_proj_attn,
        "w_out": dw_out.reshape(4, D // 4, D),
        "w_up": dw_up,
        "w_down": dw_down.reshape(4, DFF // 4, D),
    }
    return dx, big, small


HBM_SPEC = pl.BlockSpec(memory_space=pl.ANY)


def _place():
    x, y, c = lax.axis_index("x"), lax.axis_index("y"), lax.axis_index("c")
    chips = [(1 - x, y), (x, 1 - y), (1 - x, 1 - y)]
    return x, y, c, chips


def _allgather_weights(shards):
    n = len(shards)

    def body(*refs):
        ins, outs = refs[:n], refs[n:2 * n]
        send_sems, recv_sems, local_sems = refs[2 * n:]
        x, y, c, _ = _place()
        q, q_x, q_y, q_d = 2 * x + y, 2 * (1 - x) + y, 2 * x + 1 - y, 2 * (1 - x) + 1 - y
        x_nbr, y_nbr, sibling = (1 - x, y, c), (x, 1 - y, c), (x, y, 1 - c)

        def copy(w, k, src, dst, to):
            return pltpu.make_async_remote_copy(src_ref=src, dst_ref=dst, send_sem=send_sems.at[w, k],
                                                recv_sem=recv_sems.at[w, k], device_id=to, device_id_type=MESH)

        def rows(w, core, part):
            rh = ins[w].shape[0] // 2
            if part is None:
                return pl.ds(core * rh, rh)
            return pl.ds(core * rh + part * (rh // 2), rh // 2)

        def same(w, k, slot, core, part, to):
            blk = outs[w].at[slot, rows(w, core, part), :]
            return copy(w, k, blk, blk, to)

        started, locals_ = [], []
        for w in range(n):
            local = pltpu.make_async_copy(ins[w], outs[w].at[q], local_sems.at[w])
            local.start()
            locals_.append(local)
            mine = rows(w, c, None)
            for k, to in ((0, x_nbr), (1, y_nbr)):
                cp = copy(w, k, ins[w].at[mine, :], outs[w].at[q, mine, :], to)
                cp.start()
                started.append(cp)
        for w in range(n):
            same(w, 0, q_x, c, None, x_nbr).wait_recv()
            for cp in (same(w, 2, q_x, c, 0, y_nbr), same(w, 4, q_x, c, None, sibling)):
                cp.start()
                started.append(cp)
            same(w, 1, q_y, c, None, y_nbr).wait_recv()
            for cp in (same(w, 3, q_y, c, 1, x_nbr), same(w, 5, q_y, c, None, sibling)):
                cp.start()
                started.append(cp)
        for w in range(n):
            same(w, 2, q_d, c, 0, y_nbr).wait_recv()
            cp = same(w, 6, q_d, c, 0, sibling)
            cp.start()
            started.append(cp)
            same(w, 3, q_d, c, 1, x_nbr).wait_recv()
            cp = same(w, 7, q_d, c, 1, sibling)
            cp.start()
            started.append(cp)
        for w in range(n):
            same(w, 4, q_x, 1 - c, None, sibling).wait_recv()
            same(w, 5, q_y, 1 - c, None, sibling).wait_recv()
            same(w, 6, q_d, 1 - c, 0, sibling).wait_recv()
            same(w, 7, q_d, 1 - c, 1, sibling).wait_recv()
        for cp in started:
            cp.wait_send()
        for local in locals_:
            local.wait()

    return pl.pallas_call(
        body, out_shape=[jax.ShapeDtypeStruct((4,) + s.shape, s.dtype) for s in shards],
        in_specs=[HBM_SPEC] * n, out_specs=[HBM_SPEC] * n,
        scratch_shapes=[pltpu.SemaphoreType.DMA((n, 8)), pltpu.SemaphoreType.DMA((n, 8)), pltpu.SemaphoreType.DMA((n,))],
        name="allgather_weights")(*shards)


def _swap_halves(grads):
    n = len(grads)

    def body(*refs):
        ins, outs = refs[:n], refs[n:2 * n]
        send_sems, recv_sems = refs[2 * n:]
        x, y, c, _ = _place()
        copies = []
        for w in range(n):
            rh = ins[w].shape[1] // 2
            for p in range(4):
                cp = pltpu.make_async_remote_copy(
                    src_ref=ins[w].at[p, pl.ds((1 - c) * rh, rh), :], dst_ref=outs[w].at[p],
                    send_sem=send_sems.at[w, p], recv_sem=recv_sems.at[w, p],
                    device_id=(x, y, 1 - c), device_id_type=MESH)
                cp.start()
                copies.append(cp)
        for cp in copies:
            cp.wait()

    return pl.pallas_call(
        body, out_shape=[jax.ShapeDtypeStruct((4, g.shape[1] // 2, g.shape[2]), F32) for g in grads],
        in_specs=[HBM_SPEC] * n, out_specs=[HBM_SPEC] * n,
        scratch_shapes=[pltpu.SemaphoreType.DMA((n, 4)), pltpu.SemaphoreType.DMA((n, 4))],
        name="rs_swap_halves")(*grads)


def _rs_step1(parts):
    n = len(parts)

    def body(*refs):
        ins, out_a, out_b = refs[:n], refs[n:2 * n], refs[2 * n:3 * n]
        send_sems, recv_sems = refs[3 * n:]
        x, y, c, _ = _place()
        copies = []
        for w in range(n):
            rq = ins[w].shape[1] // 2
            for i in range(2):
                copies.append(pltpu.make_async_remote_copy(
                    src_ref=ins[w].at[2 * (1 - x) + i, pl.ds(0, rq), :], dst_ref=out_a[w].at[i],
                    send_sem=send_sems.at[w, i], recv_sem=recv_sems.at[w, i],
                    device_id=(1 - x, y, c), device_id_type=MESH))
                copies.append(pltpu.make_async_remote_copy(
                    src_ref=ins[w].at[2 * i + 1 - y, pl.ds(rq, rq), :], dst_ref=out_b[w].at[i],
                    send_sem=send_sems.at[w, 2 + i], recv_sem=recv_sems.at[w, 2 + i],
                    device_id=(x, 1 - y, c), device_id_type=MESH))
        for cp in copies:
            cp.start()
        for cp in copies:
            cp.wait()

    quarter = lambda p: jax.ShapeDtypeStruct((2, p.shape[1] // 2, p.shape[2]), p.dtype)
    outs = pl.pallas_call(
        body, out_shape=[quarter(p) for p in parts] * 2,
        in_specs=[HBM_SPEC] * n, out_specs=[HBM_SPEC] * (2 * n),
        scratch_shapes=[pltpu.SemaphoreType.DMA((n, 4)), pltpu.SemaphoreType.DMA((n, 4))],
        name="rs_step1")(*parts)
    return outs[:n], outs[n:]


def _rs_step2(tas, tbs):
    n = len(tas)

    def body(*refs):
        in_a, in_b, out_a, out_b = refs[:n], refs[n:2 * n], refs[2 * n:3 * n], refs[3 * n:4 * n]
        send_sems, recv_sems = refs[4 * n:]
        x, y, c, _ = _place()
        copies = []
        for w in range(n):
            copies.append(pltpu.make_async_remote_copy(
                src_ref=in_a[w].at[1 - y], dst_ref=out_a[w], send_sem=send_sems.at[w, 0], recv_sem=recv_sems.at[w, 0],
                device_id=(x, 1 - y, c), device_id_type=MESH))
            copies.append(pltpu.make_async_remote_copy(
                src_ref=in_b[w].at[1 - x], dst_ref=out_b[w], send_sem=send_sems.at[w, 1], recv_sem=recv_sems.at[w, 1],
                device_id=(1 - x, y, c), device_id_type=MESH))
        for cp in copies:
            cp.start()
        for cp in copies:
            cp.wait()

    one = lambda p: jax.ShapeDtypeStruct(p.shape[1:], p.dtype)
    outs = pl.pallas_call(
        body, out_shape=[one(p) for p in tas] + [one(p) for p in tbs],
        in_specs=[HBM_SPEC] * (2 * n), out_specs=[HBM_SPEC] * (2 * n),
        scratch_shapes=[pltpu.SemaphoreType.DMA((n, 2)), pltpu.SemaphoreType.DMA((n, 2))],
        name="rs_step2")(*tas, *tbs)
    return outs[:n], outs[n:]


def _join_halves(pieces):
    n = len(pieces)

    def body(*refs):
        ins, outs = refs[:n], refs[n:2 * n]
        send_sems, recv_sems, local_sems = refs[2 * n:]
        x, y, c, _ = _place()

        def copy(w, slot):
            return pltpu.make_async_remote_copy(
                src_ref=ins[w], dst_ref=outs[w].at[slot], send_sem=send_sems.at[w], recv_sem=recv_sems.at[w],
                device_id=(x, y, 1 - c), device_id_type=MESH)

        for w in range(n):
            pltpu.make_async_copy(ins[w], outs[w].at[c], local_sems.at[w]).start()
            copy(w, c).start()
        for w in range(n):
            copy(w, 1 - c).wait_recv()
            copy(w, c).wait_send()
            pltpu.make_async_copy(ins[w], outs[w].at[c], local_sems.at[w]).wait()

    return pl.pallas_call(
        body, out_shape=[jax.ShapeDtypeStruct((2,) + p.shape, F32) for p in pieces],
        in_specs=[HBM_SPEC] * n, out_specs=[HBM_SPEC] * n,
        scratch_shapes=[pltpu.SemaphoreType.DMA((n,)), pltpu.SemaphoreType.DMA((n,)), pltpu.SemaphoreType.DMA((n,))],
        name="rs_join_halves")(*pieces)


def _add_tile_rows(rh, c):
    for cand in (512, 256, 128, 64, 32, 16, 8):
        if rh % cand == 0 and cand * c * 4 <= 2 ** 21:
            return cand
    return rh


def _add_half(grad, recv, c_idx, name):
    _, r, cc = grad.shape
    rh = r // 2
    tr = _add_tile_rows(rh, cc)
    nb = rh // tr

    def body(c_ref, g_ref, r_ref, o_ref, ob_ref):
        del c_ref
        s = g_ref[...] + r_ref[...]
        o_ref[...] = s
        ob_ref[...] = s.astype(BF16)

    blk = pl.BlockSpec((None, tr, cc), lambda p, i, c_ref: (p, i, 0))
    grid_spec = pltpu.PrefetchScalarGridSpec(
        num_scalar_prefetch=1, grid=(4, nb),
        in_specs=[pl.BlockSpec((None, tr, cc), lambda p, i, c_ref: (p, c_ref[0] * nb + i, 0)), blk],
        out_specs=(blk, blk))
    return pl.pallas_call(
        body, out_shape=(jax.ShapeDtypeStruct((4, rh, cc), F32), jax.ShapeDtypeStruct((4, rh, cc), BF16)),
        grid_spec=grid_spec, name=name, compiler_params=_params(("parallel", "parallel")))(c_idx, grad, recv)


def _rs_add1(part, recv_a, recv_b, xy_idx, name):
    _, rh, cc = part.shape
    rq = rh // 2
    tr = _add_tile_rows(rq, cc)
    nb = rq // tr

    def body(xy_ref, pa_ref, pb_ref, ra_ref, rb_ref, ta_ref, tb_ref, tab_ref, tbb_ref):
        del xy_ref
        ta = pa_ref[...] + ra_ref[...].astype(F32)
        tb = pb_ref[...] + rb_ref[...].astype(F32)
        ta_ref[...] = ta
        tb_ref[...] = tb
        tab_ref[...] = ta.astype(BF16)
        tbb_ref[...] = tb.astype(BF16)

    blk = pl.BlockSpec((None, tr, cc), lambda i, j, xy: (i, j, 0))
    grid_spec = pltpu.PrefetchScalarGridSpec(
        num_scalar_prefetch=1, grid=(2, nb),
        in_specs=[pl.BlockSpec((None, tr, cc), lambda i, j, xy: (2 * xy[0] + i, j, 0)),
                  pl.BlockSpec((None, tr, cc), lambda i, j, xy: (2 * i + xy[1], nb + j, 0)), blk, blk],
        out_specs=(blk, blk, blk, blk))
    f32s, b16s = jax.ShapeDtypeStruct((2, rq, cc), F32), jax.ShapeDtypeStruct((2, rq, cc), BF16)
    return pl.pallas_call(body, out_shape=(f32s, f32s, b16s, b16s), grid_spec=grid_spec, name=name,
                          compiler_params=_params(("parallel", "parallel")))(xy_idx, part, part, recv_a, recv_b)


def _rs_add2(ta, tb, recv_a, recv_b, xy_idx, name):
    _, rq, cc = ta.shape
    tr = _add_tile_rows(rq, cc)
    nb = rq // tr

    def body(xy_ref, ta_ref, tb_ref, ra_ref, rb_ref, o_ref):
        del xy_ref
        s = pl.program_id(0)
        fa = ta_ref[...] + ra_ref[...].astype(F32)
        fb = tb_ref[...] + rb_ref[...].astype(F32)
        o_ref[...] = jnp.where(s == 0, fa, fb)

    rblk = pl.BlockSpec((tr, cc), lambda s, j, xy: (j, 0))
    grid_spec = pltpu.PrefetchScalarGridSpec(
        num_scalar_prefetch=1, grid=(2, nb),
        in_specs=[pl.BlockSpec((None, tr, cc), lambda s, j, xy: (xy[1], j, 0)),
                  pl.BlockSpec((None, tr, cc), lambda s, j, xy: (xy[0], j, 0)), rblk, rblk],
        out_specs=pl.BlockSpec((tr, cc), lambda s, j, xy: (s * nb + j, 0)))
    return pl.pallas_call(body, out_shape=jax.ShapeDtypeStruct((2 * rq, cc), F32), grid_spec=grid_spec, name=name,
                          compiler_params=_params(("parallel", "parallel")))(xy_idx, ta, tb, recv_a, recv_b)


def _allreduce_small(slab):
    r = slab.shape[0]

    def body(x_ref, o_ref, buf, send_sems, recv_sems):
        x, y, c, _ = _place()
        me = 4 * x + 2 * y + c
        buf[me] = x_ref[...]
        peers = []
        for k in range(1, 8):
            kx, ky, kc = (k >> 2) & 1, (k >> 1) & 1, k & 1
            peers.append((x + kx - 2 * x * kx, y + ky - 2 * y * ky, c + kc - 2 * c * kc))

        def copy(k, slot):
            return pltpu.make_async_remote_copy(src_ref=x_ref, dst_ref=buf.at[slot], send_sem=send_sems.at[k],
                                                recv_sem=recv_sems.at[k], device_id=peers[k], device_id_type=MESH)

        for k in range(7):
            copy(k, me).start()
        for k, (px, py, pc) in enumerate(peers):
            copy(k, 4 * px + 2 * py + pc).wait_recv()
        for k in range(7):
            copy(k, me).wait_send()
        acc = buf[0]
        for j in range(1, 8):
            acc = acc + buf[j]
        o_ref[...] = acc

    vm = pl.BlockSpec(memory_space=pltpu.VMEM)
    return pl.pallas_call(
        body, out_shape=jax.ShapeDtypeStruct((r, 128), F32), in_specs=[vm], out_specs=vm,
        scratch_shapes=[pltpu.VMEM((8, r, 128), F32), pltpu.SemaphoreType.DMA((7,)), pltpu.SemaphoreType.DMA((7,))],
        name="allreduce_small")(slab)


def _pack(arrs):
    rows = []
    for a in arrs:
        v = a.reshape(-1)
        v = jnp.pad(v, (0, (-v.shape[0]) % 128))
        rows.append(v.reshape(-1, 128))
    slab = jnp.concatenate(rows, axis=0)
    return jnp.pad(slab, ((0, (-slab.shape[0]) % 8), (0, 0)))


def _unpack(slab, shapes):
    out, r0 = [], 0
    for shp in shapes:
        size = math.prod(shp)
        nr = -(-size // 128)
        out.append(slab[r0:r0 + nr].reshape(-1)[:size].reshape(shp))
        r0 += nr
    return out


BIG = ("w_in", "w_proj_ssd", "w_proj_attn", "w_out", "w_up", "w_down")
SMALL = ("b_gate", "conv_w", "conv_b", "dt_bias_f", "dt_bias_b", "a_log_f", "a_log_b", "d_skip", "ssd_norm_w",
         "ln1_g", "ln1_b", "ln2_g", "ln2_b")
ORDER = ("w_in", "b_gate", "conv_w", "conv_b", "dt_bias_f", "dt_bias_b", "a_log_f", "a_log_b", "d_skip", "ssd_norm_w",
         "w_proj_ssd", "w_proj_attn", "w_out", "ln1_g", "ln1_b", "w_up", "w_down", "ln2_g", "ln2_b")


def kernel(x, w_in, b_gate, conv_w, conv_b, dt_bias_f, dt_bias_b, a_log_f, a_log_b, d_skip, ssd_norm_w, w_proj_ssd, w_proj_attn, w_out, ln1_g, ln1_b, w_up, w_down, ln2_g, ln2_b, loss_target, m_w_in, m_b_gate, m_conv_w, m_conv_b, m_dt_bias_f, m_dt_bias_b, m_a_log_f, m_a_log_b, m_d_skip, m_ssd_norm_w, m_w_proj_ssd, m_w_proj_attn, m_w_out, m_ln1_g, m_ln1_b, m_w_up, m_w_down, m_ln2_g, m_ln2_b, v_w_in, v_b_gate, v_conv_w, v_conv_b, v_dt_bias_f, v_dt_bias_b, v_a_log_f, v_a_log_b, v_d_skip, v_ssd_norm_w, v_w_proj_ssd, v_w_proj_attn, v_w_out, v_ln1_g, v_ln1_b, v_w_up, v_w_down, v_ln2_g, v_ln2_b):
    w = dict(w_in=w_in, b_gate=b_gate, conv_w=conv_w, conv_b=conv_b, dt_bias_f=dt_bias_f, dt_bias_b=dt_bias_b,
             a_log_f=a_log_f, a_log_b=a_log_b, d_skip=d_skip, ssd_norm_w=ssd_norm_w, w_proj_ssd=w_proj_ssd,
             w_proj_attn=w_proj_attn, w_out=w_out, ln1_g=ln1_g, ln1_b=ln1_b, w_up=w_up, w_down=w_down, ln2_g=ln2_g, ln2_b=ln2_b)
    m = dict(w_in=m_w_in, b_gate=m_b_gate, conv_w=m_conv_w, conv_b=m_conv_b, dt_bias_f=m_dt_bias_f, dt_bias_b=m_dt_bias_b,
             a_log_f=m_a_log_f, a_log_b=m_a_log_b, d_skip=m_d_skip, ssd_norm_w=m_ssd_norm_w, w_proj_ssd=m_w_proj_ssd,
             w_proj_attn=m_w_proj_attn, w_out=m_w_out, ln1_g=m_ln1_g, ln1_b=m_ln1_b, w_up=m_w_up, w_down=m_w_down,
             ln2_g=m_ln2_g, ln2_b=m_ln2_b)
    v = dict(w_in=v_w_in, b_gate=v_b_gate, conv_w=v_conv_w, conv_b=v_conv_b, dt_bias_f=v_dt_bias_f, dt_bias_b=v_dt_bias_b,
             a_log_f=v_a_log_f, a_log_b=v_a_log_b, d_skip=v_d_skip, ssd_norm_w=v_ssd_norm_w, w_proj_ssd=v_w_proj_ssd,
             w_proj_attn=v_w_proj_attn, w_out=v_w_out, ln1_g=v_ln1_g, ln1_b=v_ln1_b, w_up=v_w_up, w_down=v_w_down,
             ln2_g=v_ln2_g, ln2_b=v_ln2_b)
    xi, yi, ci = lax.axis_index("x"), lax.axis_index("y"), lax.axis_index("c")
    shard = 2 * xi + yi

    g_in, g_ps, g_pa, g_o, g_up, g_dn = _allgather_weights([w[n].astype(BF16) for n in BIG])
    w_in_full = jnp.concatenate([g_in[s] for s in range(4)], axis=1)
    wts = {"w_in_p": _perm_cols(w_in_full), "w_proj_ssd": g_ps.reshape(DI, D), "w_proj_attn": g_pa,
           "w_out": g_o.reshape(D, D), "w_up": g_up, "w_down": g_dn.reshape(DFF, D)}

    cw_slab = jnp.zeros((KCONV, 4, CONVD // 4), F32)
    cw_slab = lax.dynamic_update_slice(cw_slab, conv_w[:, None, :] * 0.5, (0, shard, 0))
    conv_w_all = _unpack(_allreduce_small(_pack([cw_slab])), [(KCONV, CONVD)])[0]

    sm = {n: w[n] for n in SMALL}
    sm["conv_w"] = conv_w_all
    dx, big, small = _local_grads(x[0], loss_target[0], wts, sm)

    names = list(SMALL) + ["loss"]
    shapes = [small[n].shape for n in names]
    red = dict(zip(names, _unpack(_allreduce_small(_pack([small[n] for n in names])), shapes)))
    loss = red["loss"].reshape(())
    gsm = {n: red[n] for n in SMALL}
    conv_w_grad_shard = lax.dynamic_slice_in_dim(gsm["conv_w"].reshape(KCONV, 4, CONVD // 4), shard, 1, axis=1)
    gsm["conv_w"] = conv_w_grad_shard.reshape(KCONV, CONVD // 4)

    c_idx = jnp.reshape(ci, (1,)).astype(jnp.int32)
    glist = [big[n] for n in BIG]
    xy_idx = jnp.stack([xi, yi]).astype(jnp.int32)
    recv = _swap_halves(glist)
    halves = [_add_half(g, r, c_idx, f"rs_add_half_{n}") for g, r, n in zip(glist, recv, BIG)]
    recv_a, recv_b = _rs_step1([h[1] for h in halves])
    sums1 = [_rs_add1(h[0], ra, rb, xy_idx, f"rs_add1_{n}") for h, ra, rb, n in zip(halves, recv_a, recv_b, BIG)]
    recv_a2, recv_b2 = _rs_step2([s1[2] for s1 in sums1], [s1[3] for s1 in sums1])
    pieces = [_rs_add2(s1[0], s1[1], ra, rb, xy_idx, f"rs_add2_{n}")
              for s1, ra, rb, n in zip(sums1, recv_a2, recv_b2, BIG)]
    joined = _join_halves(pieces)
    gbig = {n: j.reshape(w[n].shape) for n, j in zip(BIG, joined)}

    grads, deltas, new_m, new_v = {}, {}, {}, {}
    for n in BIG:
        grads[n] = gbig[n]
        deltas[n], new_m[n], new_v[n] = _adamw(w[n], gbig[n], m[n], v[n], f"adamw_{n}")
    sshapes = [w[n].shape for n in SMALL]
    d_s, m_s, v_s = _adamw(_pack([w[n] for n in SMALL]), _pack([gsm[n] for n in SMALL]),
                           _pack([m[n] for n in SMALL]), _pack([v[n] for n in SMALL]), "adamw_small")
    for n, dd, mm, vv in zip(SMALL, _unpack(d_s, sshapes), _unpack(m_s, sshapes), _unpack(v_s, sshapes)):
        grads[n], deltas[n], new_m[n], new_v[n] = gsm[n], dd, mm, vv

    return (loss, dx[None], *[grads[n] for n in ORDER], *[deltas[n] for n in ORDER],
            *[new_m[n] for n in ORDER], *[new_v[n] for n in ORDER])
```

```python
import math

import jax
import jax.numpy as jnp
from jax import lax
from jax.experimental import pallas as pl
from jax.experimental.pallas import tpu as pltpu

F32, BF16 = jnp.float32, jnp.bfloat16
MESH = pl.DeviceIdType.MESH

D = 1024
DI = 2048
NH = 32
HP = 64
NG = 4
NS = 128
Q = 128
CONVD = 3072
KCONV = 5
DFF = 4096
AH = 64
ATT_HALF = 64
DILATIONS = (1, 4, 16)
IN_COLS = 9536
OZ, OGATE, OXBC, OKV, OQ, ODT, UW = 0, 2048, 4096, 7168, 8704, 9472, 9728
ALPHA = 2.0 ** 0.25
NORM_EPS = 1e-5
ADAM_LR, ADAM_B1, ADAM_B2, ADAM_EPS, ADAM_WD, ADAM_STEP = 0.001, 0.9, 0.999, 1e-8, 0.01, 10
VMEM_LIMIT = 56 * 2 ** 20
NEG = -1e30


def _params(sem):
    return pltpu.CompilerParams(dimension_semantics=sem, vmem_limit_bytes=VMEM_LIMIT)


def _sigmoid(x):
    return 1.0 / (1.0 + jnp.exp(-x))


def _softplus(x):
    e = jnp.exp(-jnp.abs(x))
    small = e * (1.0 - e * (0.5 - e * (1.0 / 3.0)))
    return jnp.maximum(x, 0.0) + jnp.where(e < 0.01, small, jnp.log(1.0 + e))


def _split3(a):
    hi = a.astype(BF16)
    r = a - hi.astype(F32)
    mid = r.astype(BF16)
    lo = (r - mid.astype(F32)).astype(BF16)
    return hi, mid, lo


def _dot01(a, m01):
    hi, mid, lo = _split3(a)
    d = lambda p: jnp.dot(p, m01, preferred_element_type=F32)
    return d(hi) + d(mid) + d(lo)


def _dot01_l(m01, a):
    hi, mid, lo = _split3(a)
    d = lambda p: jnp.dot(m01, p, preferred_element_type=F32)
    return d(hi) + d(mid) + d(lo)


def _dot_nt(a, b):
    return lax.dot_general(a, b, (((1,), (1,)), ((), ())), preferred_element_type=F32)


def _iota(shape, dim):
    return lax.broadcasted_iota(jnp.int32, shape, dim)


def _mm_nn(a, b, *, tm, tn, name, out_dtype=F32):
    m, k = a.shape
    if b.ndim == 3:
        assert tn == b.shape[2]
        n = b.shape[0] * b.shape[2]
        b_spec = pl.BlockSpec((None, k, tn), lambda j, i: (j, 0, 0))
    else:
        n = b.shape[1]
        b_spec = pl.BlockSpec((k, tn), lambda j, i: (0, j))

    def body(a_ref, b_ref, o_ref):
        o_ref[...] = jnp.dot(a_ref[...].astype(BF16), b_ref[...], preferred_element_type=F32).astype(out_dtype)

    return pl.pallas_call(
        body, out_shape=jax.ShapeDtypeStruct((m, n), out_dtype), grid=(n // tn, m // tm),
        in_specs=[pl.BlockSpec((tm, k), lambda j, i: (i, 0)), b_spec],
        out_specs=pl.BlockSpec((tm, tn), lambda j, i: (i, j)),
        name=name, compiler_params=_params(("parallel", "parallel")))(a, b)


def _mm_nt(a, b, *, tm, tk, tc, name, add=None, add_scale=1.0):
    m, n = a.shape
    if b.ndim == 3:
        assert tc == b.shape[2]
        k, nc = b.shape[1], b.shape[0]
        b_spec = pl.BlockSpec((None, tk, tc), lambda j, i, c: (c, j, 0))
    else:
        k, nc = b.shape[0], n // tc
        b_spec = pl.BlockSpec((tk, tc), lambda j, i, c: (j, c))

    def body(*refs):
        if add is None:
            a_ref, b_ref, o_ref = refs
        else:
            a_ref, b_ref, add_ref, o_ref = refs
        c = pl.program_id(2)
        part = _dot_nt(a_ref[...].astype(BF16), b_ref[...])

        @pl.when(c == 0)
        def _():
            if add is None:
                o_ref[...] = part
            else:
                o_ref[...] = part + add_scale * add_ref[...]

        @pl.when(c > 0)
        def _():
            o_ref[...] += part

    in_specs = [pl.BlockSpec((tm, tc), lambda j, i, c: (i, c)), b_spec]
    args = [a, b]
    if add is not None:
        in_specs.append(pl.BlockSpec((tm, tk), lambda j, i, c: (i, j)))
        args.append(add)
    return pl.pallas_call(
        body, out_shape=jax.ShapeDtypeStruct((m, k), F32), grid=(k // tk, m // tm, nc),
        in_specs=in_specs, out_specs=pl.BlockSpec((tm, tk), lambda j, i, c: (i, j)),
        name=name, compiler_params=_params(("parallel", "parallel", "arbitrary")))(*args)


def _mm_tn(a, b, *, tka, tn, tt, name, out_shards=None):
    t, ka = a.shape
    n = b.shape[1]
    if out_shards:
        assert tn == n // out_shards
        out_shape = jax.ShapeDtypeStruct((out_shards, ka, tn), F32)
        o_spec = pl.BlockSpec((None, tka, tn), lambda i, j, s: (j, i, 0))
    else:
        out_shape = jax.ShapeDtypeStruct((ka, n), F32)
        o_spec = pl.BlockSpec((tka, tn), lambda i, j, s: (i, j))

    def body(a_ref, b_ref, o_ref):
        s = pl.program_id(2)
        part = lax.dot_general(a_ref[...].astype(BF16), b_ref[...].astype(BF16), (((0,), (0,)), ((), ())),
                               preferred_element_type=F32)

        @pl.when(s == 0)
        def _():
            o_ref[...] = part

        @pl.when(s > 0)
        def _():
            o_ref[...] += part

    return pl.pallas_call(
        body, out_shape=out_shape, grid=(ka // tka, n // tn, t // tt),
        in_specs=[pl.BlockSpec((tt, tka), lambda i, j, s: (s, i)), pl.BlockSpec((tt, tn), lambda i, j, s: (s, j))],
        out_specs=o_spec, name=name, compiler_params=_params(("parallel", "parallel", "arbitrary")))(a, b)


CONV_TM = 512
CONV_TC = 1024


def _halo_specs(t, tm, tc, col0):
    nb8 = t // 8
    r8 = tm // 8
    return [
        pl.BlockSpec((8, tc), lambda i, j: (jnp.maximum(i * r8 - 1, 0), col0 + j)),
        pl.BlockSpec((tm, tc), lambda i, j: (i, col0 + j)),
        pl.BlockSpec((8, tc), lambda i, j: (jnp.minimum((i + 1) * r8, nb8 - 1), col0 + j)),
    ]


def _fill_ext(ext, prev_ref, cur_ref, next_ref, tm, i, last):
    ext[0:8, :] = jnp.where(i > 0, prev_ref[...], 0.0)
    ext[8:8 + tm, :] = cur_ref[...]
    ext[8 + tm:16 + tm, :] = jnp.where(i < last, next_ref[...], 0.0)


def _conv_fwd(u, conv_w, conv_b):
    t = u.shape[0]
    tm, tc = CONV_TM, CONV_TC

    def body(prev_ref, cur_ref, next_ref, w_ref, b_ref, o_ref, ext):
        _fill_ext(ext, prev_ref, cur_ref, next_ref, tm, pl.program_id(0), t // tm - 1)
        acc = jnp.broadcast_to(b_ref[...], (tm, tc))
        for k in range(KCONV):
            acc = acc + w_ref[k:k + 1, :] * ext[pl.ds(6 + k, tm), :]
        o_ref[...] = acc * _sigmoid(acc)

    return pl.pallas_call(
        body, out_shape=jax.ShapeDtypeStruct((t, CONVD), F32), grid=(t // tm, CONVD // tc),
        in_specs=_halo_specs(t, tm, tc, OXBC // tc) + [
            pl.BlockSpec((KCONV, tc), lambda i, j: (0, j)), pl.BlockSpec((1, tc), lambda i, j: (0, j))],
        out_specs=pl.BlockSpec((tm, tc), lambda i, j: (i, j)),
        scratch_shapes=[pltpu.VMEM((tm + 16, tc), F32)],
        name="conv_fwd", compiler_params=_params(("parallel", "parallel")))(u, u, u, conv_w, conv_b)


def _conv_dpre(u, dxs_f, dxs_b, dy, dbc_f, dbc_b, dsk_row, conv_w, conv_b):
    t = u.shape[0]
    tm, tc = CONV_TM, CONV_TC
    r8 = tm // 8
    nb8 = t // 8
    c0 = OXBC // tc

    def body(uprev, ucur, unext, f_ref, b_ref, y_ref, cf_ref, cb_ref, dsk_ref, w_ref, bias_ref,
             dpre_ref, dw_ref, db_ref, ext):
        j = pl.program_id(0)
        i = pl.program_id(1)
        _fill_ext(ext, uprev, ucur, unext, tm, i, t // tm - 1)
        pre = jnp.broadcast_to(bias_ref[...], (tm, tc))
        for k in range(KCONV):
            pre = pre + w_ref[k:k + 1, :] * ext[pl.ds(6 + k, tm), :]
        s = _sigmoid(pre)
        xs_part = f_ref[...] + b_ref[...] + dsk_ref[...] * y_ref[...]
        up = jnp.where(j < 2, xs_part, cf_ref[...] + cb_ref[...])
        dpre = up * (s * (1.0 + pre * (1.0 - s)))
        dpre_ref[...] = dpre
        rows = [jnp.sum(dpre * ext[pl.ds(6 + k, tm), :], axis=0, keepdims=True) for k in range(KCONV)]
        rows += [jnp.zeros((1, tc), F32)] * (8 - KCONV)
        dw_part = jnp.concatenate(rows, axis=0)
        db_part = jnp.broadcast_to(jnp.sum(dpre, axis=0, keepdims=True), (8, tc))

        @pl.when(i == 0)
        def _():
            dw_ref[...] = dw_part
            db_ref[...] = db_part

        @pl.when(i > 0)
        def _():
            dw_ref[...] += dw_part
            db_ref[...] += db_part

    xs_spec = pl.BlockSpec((tm, tc), lambda j, i: (jnp.where(j < 2, i, 0), jnp.minimum(j, 1)))
    bc_spec = pl.BlockSpec((tm, tc), lambda j, i: (jnp.where(j == 2, i, 0), 0))
    in_specs = [
        pl.BlockSpec((8, tc), lambda j, i: (jnp.maximum(i * r8 - 1, 0), c0 + j)),
        pl.BlockSpec((tm, tc), lambda j, i: (i, c0 + j)),
        pl.BlockSpec((8, tc), lambda j, i: (jnp.minimum((i + 1) * r8, nb8 - 1), c0 + j)),
        xs_spec, xs_spec, xs_spec, bc_spec, bc_spec,
        pl.BlockSpec((1, tc), lambda j, i: (0, jnp.minimum(j, 1))),
        pl.BlockSpec((KCONV, tc), lambda j, i: (0, j)), pl.BlockSpec((1, tc), lambda j, i: (0, j)),
    ]
    return pl.pallas_call(
        body,
        out_shape=(jax.ShapeDtypeStruct((t, CONVD), F32), jax.ShapeDtypeStruct((8, CONVD), F32),
                   jax.ShapeDtypeStruct((8, CONVD), F32)),
        grid=(CONVD // tc, t // tm), in_specs=in_specs,
        out_specs=(pl.BlockSpec((tm, tc), lambda j, i: (i, j)),
                   pl.BlockSpec((8, tc), lambda j, i: (0, j)), pl.BlockSpec((8, tc), lambda j, i: (0, j))),
        scratch_shapes=[pltpu.VMEM((tm + 16, tc), F32)],
        name="conv_dpre", compiler_params=_params(("parallel", "arbitrary")))(
            u, u, u, dxs_f, dxs_b, dy, dbc_f, dbc_b, dsk_row, conv_w, conv_b)


def _conv_dx(du, dpre, conv_w):
    t = dpre.shape[0]
    tm, tc = CONV_TM, CONV_TC
    r8 = tm // 8
    nb8 = t // 8

    def body(prev_ref, cur_ref, next_ref, w_ref, du_in, du_out, ext):
        del du_in
        _fill_ext(ext, prev_ref, cur_ref, next_ref, tm, pl.program_id(1), t // tm - 1)
        acc = jnp.zeros((tm, tc), F32)
        for k in range(KCONV):
            acc = acc + w_ref[k:k + 1, :] * ext[pl.ds(10 - k, tm), :]
        du_out[...] = acc

    in_specs = [
        pl.BlockSpec((8, tc), lambda j, i: (jnp.maximum(i * r8 - 1, 0), j)),
        pl.BlockSpec((tm, tc), lambda j, i: (i, j)),
        pl.BlockSpec((8, tc), lambda j, i: (jnp.minimum((i + 1) * r8, nb8 - 1), j)),
        pl.BlockSpec((KCONV, tc), lambda j, i: (0, j)),
        pl.BlockSpec(memory_space=pl.ANY),
    ]
    return pl.pallas_call(
        body, out_shape=jax.ShapeDtypeStruct(du.shape, F32), grid=(CONVD // tc, t // tm), in_specs=in_specs,
        out_specs=pl.BlockSpec((tm, tc), lambda j, i: (i, OXBC // tc + j)),
        scratch_shapes=[pltpu.VMEM((tm + 16, tc), F32)], input_output_aliases={4: 0},
        name="conv_dx", compiler_params=_params(("parallel", "parallel")))(dpre, dpre, dpre, conv_w, du)


def _ssd_common(dtr_ref, par_ref, rev):
    raw = dtr_ref[...]
    lane = _iota((1, 128), 1)
    mine = (lane >= 32 * rev) & (lane < 32 * rev + 32)
    bias = par_ref[0:1, :]
    arow = jnp.where(mine, -jnp.exp(par_ref[1:2, :]), 0.0)
    dt = _softplus(raw + bias)
    a = dt * arow
    ri = _iota((Q, Q), 0)
    ci = _iota((Q, Q), 1)
    tri = (ci >= ri) if rev else (ci <= ri)
    trit = (ci <= ri) if rev else (ci >= ri)
    cs = _dot01_l(tri.astype(BF16), a)
    return raw, bias, arow, mine, dt, cs, tri, trit


def _expand_mat(rev):
    r = _iota((128, DI), 0)
    c = _iota((128, DI), 1)
    return (r == (c // HP) + 32 * rev).astype(BF16)


def _sum_mat(rev):
    r = _iota((DI, 128), 0)
    c = _iota((DI, 128), 1)
    return (c == (r // HP) + 32 * rev).astype(BF16)


def _ssd_fwd(xbc, u, par, *, rev):
    t = xbc.shape[0]
    nc = t // Q
    end = 0 if rev else Q - 1
    cmap = (lambda c: nc - 1 - c) if rev else (lambda c: c)

    def body(xbc_ref, dtr_ref, par_ref, y_ref, st_ref, h_scr):
        step = pl.program_id(0)

        @pl.when(step == 0)
        def _():
            h_scr[...] = jnp.zeros((NS, DI), F32)

        raw, bias, arow, mine, dt, cs, tri, trit = _ssd_common(dtr_ref, par_ref, rev)
        cst = cs.T
        dtt = dt.T
        tot_col = cst[:, end:end + 1]
        wt = dtt * jnp.exp(tot_col - cst)
        gam = jnp.exp(cs[end:end + 1, :])
        gam_x = _dot01(jnp.broadcast_to(gam, (8, 128)), _expand_mat(rev))[0:1, :]
        lane = _iota((Q, 128), 1)
        sel = lane < HP
        st_ref[...] = h_scr[...]
        for g in range(NG):
            bg = xbc_ref[:, DI + NS * g:DI + NS * (g + 1)]
            cg = xbc_ref[:, DI + NG * NS + NS * g:DI + NG * NS + NS * (g + 1)]
            cb = _dot_nt(cg.astype(BF16), bg.astype(BF16))
            bt = bg.T
            for k in range(4):
                lo = 512 * g + 128 * k
                xp = xbc_ref[:, lo:lo + 128].astype(BF16)
                hp = h_scr[:, lo:lo + 128]
                rhs = jnp.concatenate([xp, hp.astype(BF16)], axis=0)
                lhs, bts = [], []
                for j in range(2):
                    hc = 8 * g + 2 * k + j + 32 * rev
                    csc = jnp.broadcast_to(cs[:, hc:hc + 1], (Q, Q))
                    lm = jnp.exp(jnp.where(tri, csc - cst[hc:hc + 1, :], NEG)) * dtt[hc:hc + 1, :]
                    mh = (cb * lm).astype(BF16)
                    ec = (jnp.exp(csc) * cg).astype(BF16)
                    lhs.append(jnp.concatenate([mh, ec], axis=1))
                    bts.append((bt * wt[hc:hc + 1, :]).astype(BF16))
                ys = jnp.dot(jnp.concatenate(lhs, axis=0), rhs, preferred_element_type=F32)
                ss = jnp.dot(jnp.concatenate(bts, axis=0), xp, preferred_element_type=F32)
                y_ref[:, lo:lo + 128] = jnp.where(sel, ys[0:Q], ys[Q:2 * Q])
                h_scr[:, lo:lo + 128] = gam_x[:, lo:lo + 128] * hp + jnp.where(sel, ss[0:NS], ss[NS:2 * NS])

    return pl.pallas_call(
        body,
        out_shape=(jax.ShapeDtypeStruct((t, DI), F32), jax.ShapeDtypeStruct((nc, NS, DI), F32)),
        grid=(nc,),
        in_specs=[pl.BlockSpec((Q, CONVD), lambda c: (cmap(c), 0)),
                  pl.BlockSpec((Q, 128), lambda c: (cmap(c), ODT // 128)),
                  pl.BlockSpec((8, 128), lambda c: (0, 0))],
        out_specs=(pl.BlockSpec((Q, DI), lambda c: (cmap(c), 0)),
                   pl.BlockSpec((None, NS, DI), lambda c: (cmap(c), 0, 0))),
        scratch_shapes=[pltpu.VMEM((NS, DI), F32)],
        name="ssd_fwd_rev" if rev else "ssd_fwd", compiler_params=_params(("arbitrary",)))(xbc, u, par)


def _ssd_bwd(xbc, u, par, dy, st, *, rev):
    t = xbc.shape[0]
    nc = t // Q
    end = 0 if rev else Q - 1
    cmap = (lambda c: c) if rev else (lambda c: nc - 1 - c)

    def body(xbc_ref, dtr_ref, par_ref, dy_ref, hin_ref, dxs_ref, dbc_ref, ddt_ref, acc_ref, dh_scr):
        step = pl.program_id(0)

        @pl.when(step == 0)
        def _():
            dh_scr[...] = jnp.zeros((NS, DI), F32)

        raw, bias, arow, mine, dt, cs, tri, trit = _ssd_common(dtr_ref, par_ref, rev)
        ri = _iota((Q, Q), 0)
        ci = _iota((Q, Q), 1)
        stri = ((ri > ci) if rev else (ri < ci)).astype(BF16)
        strit = ((ci > ri) if rev else (ci < ri)).astype(BF16)
        cst = cs.T
        dtt = dt.T
        et = jnp.exp(cst)
        expand = _expand_mat(rev)
        summat = _sum_mat(rev)
        gam = jnp.exp(cs[end:end + 1, :])
        gam_x = _dot01(jnp.broadcast_to(gam, (8, 128)), expand)[0:1, :]
        dt_hi, dt_mid, _ = _split3(dt)
        dtx = (jnp.dot(dt_hi, expand, preferred_element_type=F32)
               + jnp.dot(dt_mid, expand, preferred_element_type=F32))
        lane = _iota((Q, 128), 1)
        sel = lane < HP
        dho = dh_scr[...]
        t3 = jnp.sum(dho * hin_ref[...], axis=0, keepdims=True) * gam_x
        dxs_cols, dxs2_cols, yoff_cols, a1_rows = [], [], [], []
        for g in range(NG):
            bg = xbc_ref[:, DI + NS * g:DI + NS * (g + 1)]
            cg = xbc_ref[:, DI + NG * NS + NS * g:DI + NG * NS + NS * (g + 1)]
            bb = bg.astype(BF16)
            cbf = cg.astype(BF16)
            cb = _dot_nt(cbf, bb)
            cbt = _dot_nt(bb, cbf)
            ct = cg.T
            bdh = jnp.dot(bb, dho[:, 512 * g:512 * (g + 1)].astype(BF16), preferred_element_type=F32)
            dcb = jnp.zeros((Q, Q), F32)
            dcg = jnp.zeros((Q, NS), F32)
            dbg = jnp.zeros((Q, NS), F32)
            for k in range(4):
                lo = 512 * g + 128 * k
                xpf = xbc_ref[:, lo:lo + 128]
                xp = xpf.astype(BF16)
                dyp = dy_ref[:, lo:lo + 128]
                dypb = dyp.astype(BF16)
                hinp = hin_ref[:, lo:lo + 128].astype(BF16)
                dhp = dho[:, lo:lo + 128]
                es, ws, lmds, mts, ctes, dyms, ecbs = [], [], [], [], [], [], []
                for j in range(2):
                    hc = 8 * g + 2 * k + j + 32 * rev
                    csc = jnp.broadcast_to(cs[:, hc:hc + 1], (Q, Q))
                    csr = cst[hc:hc + 1, :]
                    lmds.append(jnp.exp(jnp.where(tri, csc - csr, NEG)) * dtt[hc:hc + 1, :])
                    lmb = jnp.exp(jnp.where(trit, csr - csc, NEG))
                    mts.append((cbt * lmb).astype(BF16))
                    dyms.append(jnp.where(sel if j == 0 else ~sel, dyp, 0.0).astype(BF16))
                    ecs = jnp.exp(csc)
                    es.append(ecs)
                    ws.append(jnp.exp(cst[hc:hc + 1, end:end + 1] - csc))
                    ecbs.append((ecs * cg).astype(BF16))
                    ctes.append((ct * et[hc:hc + 1, :]).astype(BF16))
                by_dy = jnp.dot(jnp.concatenate(mts + ctes, axis=0), dypb, preferred_element_type=F32)
                dmm = _dot_nt(jnp.concatenate(dyms, axis=0), xp)
                dm0, dm1 = dmm[0:Q] * lmds[0], dmm[Q:2 * Q] * lmds[1]
                dcb = dcb + dm0 + dm1
                rr = jnp.dot(jnp.concatenate([dm0 * cb, dm1 * cb], axis=0).astype(BF16), stri, preferred_element_type=F32)
                a1_rows.append(jnp.sum(jnp.where(tri, rr[0:Q], 0.0), axis=0, keepdims=True))
                a1_rows.append(jnp.sum(jnp.where(tri, rr[Q:2 * Q], 0.0), axis=0, keepdims=True))
                yo = jnp.dot(jnp.concatenate(ecbs, axis=0), hinp, preferred_element_type=F32)
                e_p = jnp.where(sel, es[0], es[1])
                w_p = jnp.where(sel, ws[0], ws[1])
                d2 = w_p * bdh[:, 128 * k:128 * (k + 1)]
                dxs2_cols.append(d2)
                dxs_cols.append(jnp.where(sel, by_dy[0:Q], by_dy[Q:2 * Q]) + d2)
                yoff_cols.append(jnp.where(sel, yo[0:Q], yo[Q:2 * Q]))
                dcg = dcg + _dot_nt((e_p * dyp).astype(BF16), hinp)
                dbg = dbg + _dot_nt((w_p * dtx[:, lo:lo + 128] * xpf).astype(BF16), dhp.astype(BF16))
                dh_scr[:, lo:lo + 128] = (gam_x[:, lo:lo + 128] * dhp
                                          + jnp.where(sel, by_dy[2 * Q:3 * Q], by_dy[3 * Q:4 * Q]))
            dcg = dcg + jnp.dot(dcb.astype(BF16), bb, preferred_element_type=F32)
            dbg = dbg + jnp.dot(dcb.T.astype(BF16), cbf, preferred_element_type=F32)
            dbc_ref[:, NS * g:NS * (g + 1)] = dbg
            dbc_ref[:, NG * NS + NS * g:NG * NS + NS * (g + 1)] = dcg
        dxs = jnp.concatenate(dxs_cols, axis=1)
        dxs_ref[...] = dxs * dtx
        xs = xbc_ref[:, 0:DI]
        stacked = jnp.concatenate([xs * dxs, xs * jnp.concatenate(dxs2_cols, axis=1),
                                   dy_ref[...] * jnp.concatenate(yoff_cols, axis=1),
                                   jnp.broadcast_to(t3, (8, DI))], axis=0).astype(BF16)
        sums = jnp.dot(stacked, summat, preferred_element_type=F32)
        rx, rx2, ryo, c0 = sums[0:Q], sums[Q:2 * Q], sums[2 * Q:3 * Q], sums[3 * Q:3 * Q + 1]
        zero32 = jnp.zeros((32, Q), F32)
        a1t = jnp.concatenate(([zero32] if rev else []) + a1_rows + [zero32] * (2 if rev else 3), axis=0)
        da = (a1t.T + jnp.dot(trit.astype(BF16), ryo.astype(BF16), preferred_element_type=F32)
              + jnp.dot(strit, (dt * rx2).astype(BF16), preferred_element_type=F32) + jnp.where(mine, c0, 0.0))
        ddt = rx + da * arow
        ddtr = ddt * _sigmoid(raw + bias)
        ddt_ref[...] = ddtr
        part = jnp.concatenate([jnp.sum(ddtr, axis=0, keepdims=True),
                                jnp.sum(da * dt, axis=0, keepdims=True) * arow,
                                jnp.zeros((6, 128), F32)], axis=0)

        @pl.when(step == 0)
        def _():
            acc_ref[...] = part

        @pl.when(step > 0)
        def _():
            acc_ref[...] += part

    return pl.pallas_call(
        body,
        out_shape=(jax.ShapeDtypeStruct((t, DI), F32), jax.ShapeDtypeStruct((t, 2 * NG * NS), F32),
                   jax.ShapeDtypeStruct((t, 128), F32), jax.ShapeDtypeStruct((8, 128), F32)),
        grid=(nc,),
        in_specs=[pl.BlockSpec((Q, CONVD), lambda c: (cmap(c), 0)),
                  pl.BlockSpec((Q, 128), lambda c: (cmap(c), ODT // 128)),
                  pl.BlockSpec((8, 128), lambda c: (0, 0)),
                  pl.BlockSpec((Q, DI), lambda c: (cmap(c), 0)),
                  pl.BlockSpec((None, NS, DI), lambda c: (cmap(c), 0, 0))],
        out_specs=(pl.BlockSpec((Q, DI), lambda c: (cmap(c), 0)),
                   pl.BlockSpec((Q, 2 * NG * NS), lambda c: (cmap(c), 0)),
                   pl.BlockSpec((Q, 128), lambda c: (cmap(c), 0)),
                   pl.BlockSpec((8, 128), lambda c: (0, 0))),
        scratch_shapes=[pltpu.VMEM((NS, DI), F32)],
        name="ssd_bwd_rev" if rev else "ssd_bwd", compiler_params=_params(("arbitrary",)))(
            xbc, u, par, dy, st)


GN_TM = 256
GN_GROUP = DI // NG


def _gn_forward_vals(yf, yb, xs, z, dsk):
    y = yf + yb + dsk * xs
    sz = _sigmoid(z)
    gate = z * sz
    y2 = y * gate
    parts, rs = [], []
    for g in range(NG):
        seg = y2[:, GN_GROUP * g:GN_GROUP * (g + 1)]
        r = lax.rsqrt(jnp.mean(seg * seg, axis=1, keepdims=True) + NORM_EPS)
        rs.append(r)
        parts.append(seg * r)
    yn = jnp.concatenate(parts, axis=1)
    return y, sz, gate, yn, rs


def _gatenorm_fwd(y_f, y_b, xbc, u, dsk_row, nw_row):
    t = y_f.shape[0]
    tm = GN_TM

    def body(yf_ref, yb_ref, xs_ref, z_ref, dsk_ref, nw_ref, o_ref):
        _, _, _, yn, _ = _gn_forward_vals(yf_ref[...], yb_ref[...], xs_ref[...], z_ref[...], dsk_ref[...])
        o_ref[...] = (yn * nw_ref[...]).astype(BF16)

    blk = pl.BlockSpec((tm, DI), lambda i: (i, 0))
    row = pl.BlockSpec((1, DI), lambda i: (0, 0))
    return pl.pallas_call(
        body, out_shape=jax.ShapeDtypeStruct((t, DI), BF16), grid=(t // tm,),
        in_specs=[blk, blk, blk, pl.BlockSpec((tm, DI), lambda i: (i, OZ // DI)), row, row],
        out_specs=blk, name="gatenorm_fwd", compiler_params=_params(("parallel",)))(y_f, y_b, xbc, u, dsk_row, nw_row)


def _gatenorm_bwd(ds_out, y_f, y_b, xbc, u, du, dsk_row, nw_row):
    t = y_f.shape[0]
    tm = GN_TM

    def body(ds_ref, yf_ref, yb_ref, xs_ref, z_ref, dsk_ref, nw_ref, du_in, dy_ref, du_out, dnw_ref, dds_ref):
        del du_in
        i = pl.program_id(0)
        xs = xs_ref[...]
        z = z_ref[...]
        y, sz, gate, yn, rs = _gn_forward_vals(yf_ref[...], yb_ref[...], xs, z, dsk_ref[...])
        ds = ds_ref[...]
        gsc = ds * nw_ref[...]
        parts = []
        for g in range(NG):
            sl = slice(GN_GROUP * g, GN_GROUP * (g + 1))
            m = jnp.mean(gsc[:, sl] * yn[:, sl], axis=1, keepdims=True)
            parts.append(rs[g] * (gsc[:, sl] - yn[:, sl] * m))
        dy2 = jnp.concatenate(parts, axis=1)
        dy = dy2 * gate
        dy_ref[...] = dy
        du_out[...] = dy2 * y * (sz * (1.0 + z * (1.0 - sz)))
        dnw = jnp.broadcast_to(jnp.sum(ds * yn, axis=0, keepdims=True), (8, DI))
        drow = jnp.broadcast_to(jnp.sum(dy * xs, axis=0, keepdims=True), (8, DI))
        dds = _dot01(drow, _sum_mat(0))

        @pl.when(i == 0)
        def _():
            dnw_ref[...] = dnw
            dds_ref[...] = dds

        @pl.when(i > 0)
        def _():
            dnw_ref[...] += dnw
            dds_ref[...] += dds

    blk = pl.BlockSpec((tm, DI), lambda i: (i, 0))
    row = pl.BlockSpec((1, DI), lambda i: (0, 0))
    return pl.pallas_call(
        body,
        out_shape=(jax.ShapeDtypeStruct((t, DI), F32), jax.ShapeDtypeStruct(du.shape, F32),
                   jax.ShapeDtypeStruct((8, DI), F32), jax.ShapeDtypeStruct((8, 128), F32)),
        grid=(t // tm,),
        in_specs=[blk, blk, blk, blk, pl.BlockSpec((tm, DI), lambda i: (i, OZ // DI)), row, row,
                  pl.BlockSpec(memory_space=pl.ANY)],
        out_specs=(blk, pl.BlockSpec((tm, DI), lambda i: (i, OZ // DI)),
                   pl.BlockSpec((8, DI), lambda i: (0, 0)), pl.BlockSpec((8, 128), lambda i: (0, 0))),
        input_output_aliases={7: 1},
        name="gatenorm_bwd", compiler_params=_params(("arbitrary",)))(ds_out, y_f, y_b, xbc, u, dsk_row, nw_row, du)


AT_B = 128
AT_W = AT_B + 2 * ATT_HALF
AT_L = 2 * AH
SCALE = 1.0 / math.sqrt(AH)


def _slope(g, hh):
    return 2.0 ** (-8.0 * (4 * g + hh + 1) / 12.0)


def _qcol(g):
    return lambda p: OQ // AT_L + 2 * g + p


def _kcol(g):
    return lambda p: OKV // AT_L + 4 * g + 2 * p


def _vcol(g):
    return lambda p: OKV // AT_L + 4 * g + 2 * p + 1


def _pcol(p):
    return p


def _sub(d):
    return 2 if d == 1 else 1


def _win_specs(col, t, d):
    tb, hb = AT_B * d * _sub(d), ATT_HALF * d
    per = tb // hb
    nh = t // hb
    return [
        pl.BlockSpec((hb, AT_L), lambda p, i: (jnp.maximum(per * i - 1, 0), col(p))),
        pl.BlockSpec((tb, AT_L), lambda p, i: (i, col(p))),
        pl.BlockSpec((hb, AT_L), lambda p, i: (jnp.minimum(per * (i + 1), nh - 1), col(p))),
    ]


def _blk_spec(col, d):
    return pl.BlockSpec((AT_B * d * _sub(d), AT_L), lambda p, i: (i, col(p)))


def _rows(ref, r, s, d):
    return ref[pl.ds(r, AT_B, stride=d), :] if d > 1 else ref[AT_B * s:AT_B * (s + 1), :]


def _win(p_ref, c_ref, n_ref, r, s, d):
    if d > 1:
        return jnp.concatenate([p_ref[pl.ds(r, ATT_HALF, stride=d), :], c_ref[pl.ds(r, AT_B, stride=d), :],
                                n_ref[pl.ds(r, ATT_HALF, stride=d), :]], axis=0)
    if s == 0:
        return jnp.concatenate([p_ref[...], c_ref[0:AT_B + ATT_HALF, :]], axis=0)
    return jnp.concatenate([c_ref[ATT_HALF:2 * AT_B, :], n_ref[...]], axis=0)


def _put_rows(ref, r, s, d, val):
    if d > 1:
        ref[pl.ds(r, AT_B, stride=d), :] = val
    else:
        ref[AT_B * s:AT_B * (s + 1), :] = val


def _for_blocks(d, fn):
    if d == 1:
        for s in range(_sub(d)):
            fn(0, s)
    else:
        def step(r, c):
            fn(r, 0)
            return c
        lax.fori_loop(0, d, step, 0, unroll=2)


def _attn_geometry(i, ln, d):
    a = i * AT_B + _iota((AT_B, AT_W), 0)
    b = i * AT_B - ATT_HALF + _iota((AT_B, AT_W), 1)
    rel = jnp.abs(a - b)
    valid = (rel <= ATT_HALF) & (b >= 0) & (b < ln)
    return valid, (rel * d).astype(F32)


def _attn_fwd(u, g):
    t = u.shape[0]
    d = DILATIONS[g]
    ln = t // d

    def body(q_ref, kp, kc, kn, vp, vc, vn, o_ref, l_ref):
        p_id = pl.program_id(0)
        i = pl.program_id(1)
        lane = _iota((AT_B, AT_L), 1)

        def one(r, s):
            valid, dist = _attn_geometry(i * _sub(d) + s, ln, d)
            q = _rows(q_ref, r, s, d)
            kw = _win(kp, kc, kn, r, s, d).astype(BF16)
            vw = _win(vp, vc, vn, r, s, d).astype(BF16)
            o = jnp.zeros((AT_B, AT_L), F32)
            lse = jnp.zeros((AT_B, AT_L), F32)
            for hh in range(2):
                hm = (lane // AH) == hh
                slope = jnp.where(p_id == 0, _slope(g, hh), _slope(g, 2 + hh))
                qm = jnp.where(hm, q, 0.0).astype(BF16)
                sc = _dot_nt(qm, kw) * SCALE - slope * dist
                sc = jnp.where(valid, sc, NEG)
                m = jnp.max(sc, axis=1, keepdims=True)
                pr = jnp.exp(sc - m)
                den = jnp.sum(pr, axis=1, keepdims=True)
                oh = jnp.dot(pr.astype(BF16), vw, preferred_element_type=F32)
                o = jnp.where(hm, oh / den, o)
                lse = jnp.where(hm, m + jnp.log(den), lse)
            _put_rows(o_ref, r, s, d, o)
            _put_rows(l_ref, r, s, d, lse)

        _for_blocks(d, one)

    oshape = jax.ShapeDtypeStruct((t, 2 * AT_L), F32)
    ospec = _blk_spec(_pcol, d)
    return pl.pallas_call(
        body, out_shape=(oshape, oshape), grid=(2, t // (AT_B * d * _sub(d))),
        in_specs=[_blk_spec(_qcol(g), d)] + _win_specs(_kcol(g), t, d) + _win_specs(_vcol(g), t, d),
        out_specs=(ospec, ospec), name=f"attn_fwd_{g}", compiler_params=_params(("parallel", "parallel")))(
            u, u, u, u, u, u, u)


def _attn_dq(u, du, do, lse, e, g):
    t = u.shape[0]
    d = DILATIONS[g]
    ln = t // d

    def body(q_ref, kp, kc, kn, vp, vc, vn, do_ref, l_ref, e_ref, du_in, dq_ref):
        del du_in
        p_id = pl.program_id(0)
        i = pl.program_id(1)
        lane = _iota((AT_B, AT_L), 1)

        def one(r, s):
            valid, dist = _attn_geometry(i * _sub(d) + s, ln, d)
            q = _rows(q_ref, r, s, d)
            kw = _win(kp, kc, kn, r, s, d).astype(BF16)
            vw = _win(vp, vc, vn, r, s, d).astype(BF16)
            do_ = _rows(do_ref, r, s, d)
            lv = _rows(l_ref, r, s, d)
            ev = _rows(e_ref, r, s, d)
            dq = jnp.zeros((AT_B, AT_L), F32)
            for hh in range(2):
                hm = (lane // AH) == hh
                slope = jnp.where(p_id == 0, _slope(g, hh), _slope(g, 2 + hh))
                qm = jnp.where(hm, q, 0.0).astype(BF16)
                sc = _dot_nt(qm, kw) * SCALE - slope * dist
                lcol = jnp.broadcast_to(lv[:, AH * hh:AH * hh + 1], (AT_B, AT_W))
                ecol = jnp.broadcast_to(ev[:, AH * hh:AH * hh + 1], (AT_B, AT_W))
                pr = jnp.exp(jnp.where(valid, sc - lcol, NEG))
                dom = jnp.where(hm, do_, 0.0).astype(BF16)
                ds = pr * (_dot_nt(dom, vw) + ecol)
                dqh = jnp.dot(ds.astype(BF16), kw, preferred_element_type=F32) * SCALE
                dq = jnp.where(hm, dqh, dq)
            _put_rows(dq_ref, r, s, d, dq)

        _for_blocks(d, one)

    rspec = _blk_spec(_pcol, d)
    return pl.pallas_call(
        body, out_shape=jax.ShapeDtypeStruct(du.shape, F32), grid=(2, t // (AT_B * d * _sub(d))),
        in_specs=[_blk_spec(_qcol(g), d)] + _win_specs(_kcol(g), t, d) + _win_specs(_vcol(g), t, d)
        + [rspec, rspec, rspec, pl.BlockSpec(memory_space=pl.ANY)],
        out_specs=_blk_spec(_qcol(g), d), input_output_aliases={10: 0},
        name=f"attn_dq_{g}", compiler_params=_params(("parallel", "parallel")))(
            u, u, u, u, u, u, u, do, lse, e, du)


def _attn_dkv(u, du, do, lse, e, g):
    t = u.shape[0]
    d = DILATIONS[g]
    ln = t // d

    def body(k_ref, v_ref, qp, qc, qn, dp_, dc_, dn_, lp, lc, ln_, ep, ec, en, du_in, dkv_ref, dk_scr, dv_scr):
        del du_in
        p_id = pl.program_id(0)
        jb = pl.program_id(1)
        lane = _iota((AT_B, AT_L), 1)

        def one(r, s):
            valid, dist = _attn_geometry(jb * _sub(d) + s, ln, d)
            k = _rows(k_ref, r, s, d)
            v = _rows(v_ref, r, s, d)
            qw = _win(qp, qc, qn, r, s, d).astype(BF16)
            dow = _win(dp_, dc_, dn_, r, s, d).astype(BF16)
            lt = _win(lp, lc, ln_, r, s, d).T
            et = _win(ep, ec, en, r, s, d).T
            dk = jnp.zeros((AT_B, AT_L), F32)
            dv = jnp.zeros((AT_B, AT_L), F32)
            for hh in range(2):
                hm = (lane // AH) == hh
                slope = jnp.where(p_id == 0, _slope(g, hh), _slope(g, 2 + hh))
                km = jnp.where(hm, k, 0.0).astype(BF16)
                st = _dot_nt(km, qw) * SCALE - slope * dist
                pt = jnp.exp(jnp.where(valid, st - lt[AH * hh:AH * hh + 1, :], NEG))
                dvh = jnp.dot(pt.astype(BF16), dow, preferred_element_type=F32)
                vm = jnp.where(hm, v, 0.0).astype(BF16)
                dst = pt * (_dot_nt(vm, dow) + et[AH * hh:AH * hh + 1, :])
                dkh = jnp.dot(dst.astype(BF16), qw, preferred_element_type=F32) * SCALE
                dk = jnp.where(hm, dkh, dk)
                dv = jnp.where(hm, dvh, dv)
            _put_rows(dk_scr, r, s, d, dk)
            _put_rows(dv_scr, r, s, d, dv)

        _for_blocks(d, one)
        dkv_ref[:, 0:AT_L] = dk_scr[...]
        dkv_ref[:, AT_L:2 * AT_L] = dv_scr[...]

    return pl.pallas_call(
        body, out_shape=jax.ShapeDtypeStruct(du.shape, F32), grid=(2, t // (AT_B * d * _sub(d))),
        in_specs=[_blk_spec(_kcol(g), d), _blk_spec(_vcol(g), d)]
        + _win_specs(_qcol(g), t, d) + _win_specs(_pcol, t, d) + _win_specs(_pcol, t, d) + _win_specs(_pcol, t, d)
        + [pl.BlockSpec(memory_space=pl.ANY)],
        out_specs=pl.BlockSpec((AT_B * d * _sub(d), 2 * AT_L), lambda p, i: (i, OKV // (2 * AT_L) + 2 * g + p)),
        input_output_aliases={14: 0},
        scratch_shapes=[pltpu.VMEM((AT_B * d * _sub(d), AT_L), F32), pltpu.VMEM((AT_B * d * _sub(d), AT_L), F32)],
        name=f"attn_dkv_{g}", compiler_params=_params(("parallel", "parallel")))(
            u, u, u, u, u, do, do, do, lse, lse, lse, e, e, e, du)


CMB_TM = 1024


def _combine_weights(l0, l1, l2):
    m = jnp.maximum(jnp.maximum(l0, l1), l2)
    e0, e1, e2 = jnp.exp(l0 - m), jnp.exp(l1 - m), jnp.exp(l2 - m)
    inv = 1.0 / (e0 + e1 + e2)
    return e0 * inv, e1 * inv, e2 * inv


def _combine_fwd(os_, ls_):
    t = os_[0].shape[0]
    tm = CMB_TM

    def body(o0, o1, o2, l0, l1, l2, a_ref):
        w0, w1, w2 = _combine_weights(l0[...], l1[...], l2[...])
        a_ref[...] = w0 * o0[...] + w1 * o1[...] + w2 * o2[...]

    blk = pl.BlockSpec((tm, 2 * AT_L), lambda i: (i, 0))
    return pl.pallas_call(
        body, out_shape=jax.ShapeDtypeStruct((t, 2 * AT_L), F32), grid=(t // tm,), in_specs=[blk] * 6, out_specs=blk,
        name="combine_fwd", compiler_params=_params(("parallel",)))(*os_, *ls_)


def _combine_bwd(datt, os_, ls_):
    t = datt.shape[0]
    tm = CMB_TM

    def body(da_ref, o0, o1, o2, l0, l1, l2, d0, d1, d2, e0, e1, e2):
        w = _combine_weights(l0[...], l1[...], l2[...])
        da = da_ref[...]
        att = w[0] * o0[...] + w[1] * o1[...] + w[2] * o2[...]
        r = _iota((2 * AT_L, 2 * AT_L), 0) // AH
        c = _iota((2 * AT_L, 2 * AT_L), 1) // AH
        hs = _dot01(da * att, (r == c).astype(BF16))
        for wg, dref, eref in zip(w, (d0, d1, d2), (e0, e1, e2)):
            dref[...] = wg * da
            eref[...] = -wg * hs

    blk = pl.BlockSpec((tm, 2 * AT_L), lambda i: (i, 0))
    shp = jax.ShapeDtypeStruct((t, 2 * AT_L), F32)
    outs = pl.pallas_call(
        body, out_shape=(shp,) * 6, grid=(t // tm,), in_specs=[blk] * 7, out_specs=(blk,) * 6,
        name="combine_bwd", compiler_params=_params(("parallel",)))(datt, *os_, *ls_)
    return outs[0:3], outs[3:6]


ROW_TM = 512


def _mix_fwd(y_ssd, y_att, u, bg_row):
    t = y_ssd.shape[0]
    tm = ROW_TM

    def body(ys_ref, ya_ref, g0_ref, g1_ref, b0_ref, b1_ref, o_ref):
        g0 = _sigmoid(g0_ref[...] + b0_ref[...])
        g1 = _sigmoid(g1_ref[...] + b1_ref[...])
        o_ref[...] = (g0 * ys_ref[...] + g1 * ya_ref[...]).astype(BF16)

    blk = pl.BlockSpec((tm, D), lambda i: (i, 0))
    return pl.pallas_call(
        body, out_shape=jax.ShapeDtypeStruct((t, D), BF16), grid=(t // tm,),
        in_specs=[blk, blk, pl.BlockSpec((tm, D), lambda i: (i, OGATE // D)), pl.BlockSpec((tm, D), lambda i: (i, OGATE // D + 1)),
                  pl.BlockSpec((1, D), lambda i: (0, 0)), pl.BlockSpec((1, D), lambda i: (0, 1))],
        out_specs=blk, name="mix_fwd", compiler_params=_params(("parallel",)))(y_ssd, y_att, u, u, bg_row, bg_row)


def _mix_bwd(dmixin, y_ssd, y_att, u, bg_row):
    t = y_ssd.shape[0]
    tm = ROW_TM

    def body(dm_ref, ys_ref, ya_ref, g0_ref, g1_ref, b0_ref, b1_ref, dys_ref, dya_ref, du_ref, db_ref):
        i = pl.program_id(0)
        g0 = _sigmoid(g0_ref[...] + b0_ref[...])
        g1 = _sigmoid(g1_ref[...] + b1_ref[...])
        dm = dm_ref[...]
        dys_ref[...] = (dm * g0).astype(BF16)
        dya_ref[...] = (dm * g1).astype(BF16)
        dl0 = dm * ys_ref[...] * g0 * (1.0 - g0)
        dl1 = dm * ya_ref[...] * g1 * (1.0 - g1)
        du_ref[:, 0:D] = dl0
        du_ref[:, D:2 * D] = dl1
        part = jnp.concatenate([jnp.broadcast_to(jnp.sum(dl0, axis=0, keepdims=True), (8, D)),
                                jnp.broadcast_to(jnp.sum(dl1, axis=0, keepdims=True), (8, D))], axis=1)

        @pl.when(i == 0)
        def _():
            db_ref[...] = part

        @pl.when(i > 0)
        def _():
            db_ref[...] += part

    blk = pl.BlockSpec((tm, D), lambda i: (i, 0))
    return pl.pallas_call(
        body,
        out_shape=(jax.ShapeDtypeStruct((t, D), BF16), jax.ShapeDtypeStruct((t, D), BF16),
                   jax.ShapeDtypeStruct((t, UW), F32), jax.ShapeDtypeStruct((8, 2 * D), F32)),
        grid=(t // tm,),
        in_specs=[blk, blk, blk, pl.BlockSpec((tm, D), lambda i: (i, OGATE // D)), pl.BlockSpec((tm, D), lambda i: (i, OGATE // D + 1)),
                  pl.BlockSpec((1, D), lambda i: (0, 0)), pl.BlockSpec((1, D), lambda i: (0, 1))],
        out_specs=(blk, blk, pl.BlockSpec((tm, 2 * D), lambda i: (i, OGATE // (2 * D))),
                   pl.BlockSpec((8, 2 * D), lambda i: (0, 0))),
        name="mix_bwd", compiler_params=_params(("arbitrary",)))(dmixin, y_ssd, y_att, u, u, bg_row, bg_row)


def _ln(x, g, b):
    mu = jnp.mean(x, axis=1, keepdims=True)
    xc = x - mu
    var = jnp.mean(xc * xc, axis=1, keepdims=True)
    rstd = lax.rsqrt(var + NORM_EPS)
    xhat = xc * rstd
    return xhat * g + b, xhat, rstd


def _ln_back(dh, xhat, rstd, g):
    dxh = dh * g
    m1 = jnp.mean(dxh, axis=1, keepdims=True)
    m2 = jnp.mean(dxh * xhat, axis=1, keepdims=True)
    return rstd * (dxh - m1 - xhat * m2)


def _ln1_fwd(x, mix, g_row, b_row):
    t = x.shape[0]
    tm = ROW_TM

    def body(x_ref, m_ref, g_ref, b_ref, pre_ref, h_ref):
        pre = ALPHA * x_ref[...] + m_ref[...]
        pre_ref[...] = pre
        h, _, _ = _ln(pre, g_ref[...], b_ref[...])
        h_ref[...] = h.astype(BF16)

    blk = pl.BlockSpec((tm, D), lambda i: (i, 0))
    row = pl.BlockSpec((1, D), lambda i: (0, 0))
    return pl.pallas_call(
        body, out_shape=(jax.ShapeDtypeStruct((t, D), F32), jax.ShapeDtypeStruct((t, D), BF16)), grid=(t // tm,),
        in_specs=[blk, blk, row, row], out_specs=(blk, blk),
        name="ln1_fwd", compiler_params=_params(("parallel",)))(x, mix, g_row, b_row)


def _ln1_bwd(dh, pre, g_row, b_row):
    t = dh.shape[0]
    tm = ROW_TM

    def body(dh_ref, pre_ref, g_ref, b_ref, dpre_ref, acc_ref):
        i = pl.program_id(0)
        dh_ = dh_ref[...]
        _, xhat, rstd = _ln(pre_ref[...], g_ref[...], b_ref[...])
        dpre_ref[...] = _ln_back(dh_, xhat, rstd, g_ref[...])
        part = jnp.concatenate([jnp.sum(dh_ * xhat, axis=0, keepdims=True), jnp.sum(dh_, axis=0, keepdims=True),
                                jnp.zeros((6, D), F32)], axis=0)

        @pl.when(i == 0)
        def _():
            acc_ref[...] = part

        @pl.when(i > 0)
        def _():
            acc_ref[...] += part

    blk = pl.BlockSpec((tm, D), lambda i: (i, 0))
    row = pl.BlockSpec((1, D), lambda i: (0, 0))
    return pl.pallas_call(
        body, out_shape=(jax.ShapeDtypeStruct((t, D), F32), jax.ShapeDtypeStruct((8, D), F32)), grid=(t // tm,),
        in_specs=[blk, blk, row, row], out_specs=(blk, pl.BlockSpec((8, D), lambda i: (0, 0))),
        name="ln1_bwd", compiler_params=_params(("arbitrary",)))(dh, pre, g_row, b_row)


def _ln2_loss(pre1, f, tgt, g1_row, b1_row, g2_row, b2_row):
    t = pre1.shape[0]
    tm = ROW_TM

    def body(p1_ref, f_ref, t_ref, g1_ref, b1_ref, g2_ref, b2_ref, dpre_ref, acc_ref):
        i = pl.program_id(0)
        h1, _, _ = _ln(p1_ref[...], g1_ref[...], b1_ref[...])
        pre2 = ALPHA * h1 + f_ref[...]
        h2, xhat, rstd = _ln(pre2, g2_ref[...], b2_ref[...])
        err = h2 - t_ref[...]
        dh = err * (1.0 / D)
        dpre_ref[...] = _ln_back(dh, xhat, rstd, g2_ref[...])
        loss = jnp.sum(jnp.sum(err * err, axis=1, keepdims=True), axis=0, keepdims=True) * (0.5 / D)
        part = jnp.concatenate([jnp.sum(dh * xhat, axis=0, keepdims=True), jnp.sum(dh, axis=0, keepdims=True),
                                jnp.broadcast_to(loss, (1, D)), jnp.zeros((5, D), F32)], axis=0)

        @pl.when(i == 0)
        def _():
            acc_ref[...] = part

        @pl.when(i > 0)
        def _():
            acc_ref[...] += part

    blk = pl.BlockSpec((tm, D), lambda i: (i, 0))
    row = pl.BlockSpec((1, D), lambda i: (0, 0))
    return pl.pallas_call(
        body, out_shape=(jax.ShapeDtypeStruct((t, D), F32), jax.ShapeDtypeStruct((8, D), F32)), grid=(t // tm,),
        in_specs=[blk, blk, blk, row, row, row, row], out_specs=(blk, pl.BlockSpec((8, D), lambda i: (0, 0))),
        name="ln2_loss", compiler_params=_params(("arbitrary",)))(pre1, f, tgt, g1_row, b1_row, g2_row, b2_row)


def _mlp_up(h1, w_up):
    t = h1.shape[0]
    tm, tn = ROW_TM, D

    def body(a_ref, b_ref, up_ref, act_ref):
        up = jnp.dot(a_ref[...], b_ref[...], preferred_element_type=F32)
        up_ref[...] = up
        r = jnp.maximum(up, 0.0)
        act_ref[...] = (r * r).astype(BF16)

    blk = pl.BlockSpec((tm, tn), lambda j, i: (i, j))
    return pl.pallas_call(
        body, out_shape=(jax.ShapeDtypeStruct((t, DFF), F32), jax.ShapeDtypeStruct((t, DFF), BF16)),
        grid=(DFF // tn, t // tm),
        in_specs=[pl.BlockSpec((tm, D), lambda j, i: (i, 0)), pl.BlockSpec((None, D, tn), lambda j, i: (j, 0, 0))],
        out_specs=(blk, blk), name="mlp_up", compiler_params=_params(("parallel", "parallel")))(h1, w_up)


def _d_up(dpre2, w_down, up):
    t = up.shape[0]
    tm, tk = ROW_TM, D

    def body(a_ref, b_ref, u_ref, o_ref):
        dact = _dot_nt(a_ref[...].astype(BF16), b_ref[...])
        o_ref[...] = (dact * 2.0 * jnp.maximum(u_ref[...], 0.0)).astype(BF16)

    blk = pl.BlockSpec((tm, tk), lambda j, i: (i, j))
    return pl.pallas_call(
        body, out_shape=jax.ShapeDtypeStruct((t, DFF), BF16), grid=(DFF // tk, t // tm),
        in_specs=[pl.BlockSpec((tm, D), lambda j, i: (i, 0)), pl.BlockSpec((tk, D), lambda j, i: (j, 0)), blk],
        out_specs=blk, name="d_up", compiler_params=_params(("parallel", "parallel")))(dpre2, w_down, up)


def _dt_bwd(du, ddt_f, ddt_b):
    t = ddt_f.shape[0]
    tm = 1024

    def body(f_ref, b_ref, du_in, o_ref):
        del du_in
        o_ref[:, 0:128] = f_ref[...] + b_ref[...]
        o_ref[:, 128:256] = jnp.zeros((tm, 128), F32)

    blk = pl.BlockSpec((tm, 128), lambda i: (i, 0))
    return pl.pallas_call(
        body, out_shape=jax.ShapeDtypeStruct(du.shape, F32), grid=(t // tm,),
        in_specs=[blk, blk, pl.BlockSpec(memory_space=pl.ANY)],
        out_specs=pl.BlockSpec((tm, 256), lambda i: (i, ODT // 256)), input_output_aliases={2: 0},
        name="dt_bwd", compiler_params=_params(("parallel",)))(ddt_f, ddt_b, du)


def _adamw(w, g, m, v, name):
    r, c = w.shape
    tr = r
    for cand in (256, 128, 64, 32, 16, 8):
        if r % cand == 0 and cand * c * 4 <= 2 ** 21:
            tr = cand
            break
    bc1 = 1.0 / (1.0 - ADAM_B1 ** ADAM_STEP)
    bc2 = 1.0 / (1.0 - ADAM_B2 ** ADAM_STEP)

    def body(w_ref, g_ref, m_ref, v_ref, d_ref, nm_ref, nv_ref):
        gg = g_ref[...]
        nm = ADAM_B1 * m_ref[...] + (1.0 - ADAM_B1) * gg
        nv = ADAM_B2 * v_ref[...] + (1.0 - ADAM_B2) * (gg * gg)
        nm_ref[...] = nm
        nv_ref[...] = nv
        d_ref[...] = -ADAM_LR * ((nm * bc1) / (jnp.sqrt(nv * bc2) + ADAM_EPS) + ADAM_WD * w_ref[...])

    blk = pl.BlockSpec((tr, c), lambda i: (i, 0))
    shp = jax.ShapeDtypeStruct((r, c), F32)
    return pl.pallas_call(body, out_shape=(shp, shp, shp), grid=(r // tr,), in_specs=[blk] * 4, out_specs=(blk,) * 3,
                          name=name, compiler_params=_params(("parallel",)))(w, g, m, v)


def _perm_cols(w):
    z, xbc, dt = w[:, 0:2048], w[:, 2048:5120], w[:, 5120:5184]
    q, k, v, gate = w[:, 5184:5952], w[:, 5952:6720], w[:, 6720:7488], w[:, 7488:9536]
    kv = []
    for g in range(3):
        for p in range(2):
            lo = 256 * g + 128 * p
            kv += [k[:, lo:lo + 128], v[:, lo:lo + 128]]
    pad = jnp.zeros((w.shape[0], UW - IN_COLS), w.dtype)
    return jnp.concatenate([z, gate, xbc] + kv + [q, dt, pad], axis=1)


def _unperm_cols(wp):
    z, gate, xbc = wp[:, OZ:OZ + 2048], wp[:, OGATE:OGATE + 2048], wp[:, OXBC:OXBC + CONVD]
    q, dt = wp[:, OQ:OQ + 768], wp[:, ODT:ODT + 64]
    ks, vs = [], []
    for g in range(3):
        for p in range(2):
            lo = OKV + 128 * (4 * g + 2 * p)
            ks.append(wp[:, lo:lo + 128])
            vs.append(wp[:, lo + 128:lo + 256])
    return jnp.concatenate([z, xbc, dt, q] + ks + vs + [gate], axis=1)


def _lanes128(*vecs):
    v = jnp.concatenate([a.reshape(-1) for a in vecs])
    return jnp.pad(v, (0, 128 - v.shape[0])).reshape(1, 128)


def _local_grads(x, tgt, wts, sm):
    row = lambda a: a.reshape(1, -1)
    bg_row, cb_row = row(sm["b_gate"]), row(sm["conv_b"])
    par = jnp.concatenate([_lanes128(sm["dt_bias_f"], sm["dt_bias_b"]), _lanes128(sm["a_log_f"], sm["a_log_b"]),
                           jnp.zeros((6, 128), F32)], axis=0)
    dsk_row = row(jnp.repeat(sm["d_skip"], HP))
    nw_row = row(sm["ssd_norm_w"])
    g1, b1, g2, b2 = row(sm["ln1_g"]), row(sm["ln1_b"]), row(sm["ln2_g"]), row(sm["ln2_b"])

    u = _mm_nn(x, wts["w_in_p"], tm=512, tn=2432, name="in_proj")
    xbc = _conv_fwd(u, sm["conv_w"], cb_row)
    y_f, st_f = _ssd_fwd(xbc, u, par, rev=False)
    y_b, st_b = _ssd_fwd(xbc, u, par, rev=True)
    s_out = _gatenorm_fwd(y_f, y_b, xbc, u, dsk_row, nw_row)
    y_ssd = _mm_nn(s_out, wts["w_proj_ssd"], tm=512, tn=1024, name="proj_ssd")
    att_o, att_l = [], []
    for g in range(3):
        o, l = _attn_fwd(u, g)
        att_o.append(o)
        att_l.append(l)
    att = _combine_fwd(att_o, att_l)
    y_att = _mm_nn(att, wts["w_proj_attn"], tm=512, tn=256, name="proj_attn")
    mixin = _mix_fwd(y_ssd, y_att, u, bg_row)
    mix = _mm_nn(mixin, wts["w_out"], tm=512, tn=1024, name="out_proj")
    pre1, h1 = _ln1_fwd(x, mix, g1, b1)
    up, act = _mlp_up(h1, wts["w_up"])
    f = _mm_nn(act, wts["w_down"], tm=512, tn=1024, name="mlp_down")
    dpre2, acc2 = _ln2_loss(pre1, f, tgt, g1, b1, g2, b2)

    dw_down = _mm_tn(act, dpre2, tka=1024, tn=1024, tt=512, name="dw_down")
    dup = _d_up(dpre2, wts["w_down"], up)
    dw_up = _mm_tn(h1, dup, tka=1024, tn=1024, tt=512, name="dw_up", out_shards=4)
    dh1 = _mm_nt(dup, wts["w_up"], tm=512, tk=1024, tc=1024, name="d_h1", add=dpre2, add_scale=ALPHA)
    dpre1, acc1 = _ln1_bwd(dh1, pre1, g1, b1)
    dmixin = _mm_nt(dpre1, wts["w_out"], tm=512, tk=1024, tc=1024, name="d_mixin")
    dw_out = _mm_tn(mixin, dpre1, tka=1024, tn=1024, tt=512, name="dw_out")
    dy_ssd, dy_att, du, dbg = _mix_bwd(dmixin, y_ssd, y_att, u, bg_row)
    dw_proj_ssd = _mm_tn(s_out, dy_ssd, tka=1024, tn=1024, tt=512, name="dw_proj_ssd")
    ds_out = _mm_nt(dy_ssd, wts["w_proj_ssd"], tm=512, tk=1024, tc=1024, name="d_s_out")
    dw_proj_attn = _mm_tn(att, dy_att, tka=256, tn=256, tt=512, name="dw_proj_attn", out_shards=4)
    datt = _mm_nt(dy_att, wts["w_proj_attn"], tm=512, tk=256, tc=256, name="d_att")
    do_g, e_g = _combine_bwd(datt, att_o, att_l)
    for g in range(3):
        du = _attn_dq(u, du, do_g[g], att_l[g], e_g[g], g)
        du = _attn_dkv(u, du, do_g[g], att_l[g], e_g[g], g)
    dy, du, dnw, dds = _gatenorm_bwd(ds_out, y_f, y_b, xbc, u, du, dsk_row, nw_row)
    dxs_f, dbc_f, ddt_f, sacc_f = _ssd_bwd(xbc, u, par, dy, st_f, rev=False)
    dxs_b, dbc_b, ddt_b, sacc_b = _ssd_bwd(xbc, u, par, dy, st_b, rev=True)
    dpre_c, dcw, dcb = _conv_dpre(u, dxs_f, dxs_b, dy, dbc_f, dbc_b, dsk_row, sm["conv_w"], cb_row)
    du = _conv_dx(du, dpre_c, sm["conv_w"])
    du = _dt_bwd(du, ddt_f, ddt_b)
    dw_in_p = _mm_tn(x, du, tka=1024, tn=2432, tt=512, name="dw_in")
    dx = _mm_nt(du, wts["w_in_p"], tm=512, tk=1024, tc=2432, name="d_x", add=dpre1, add_scale=ALPHA)

    sacc = sacc_f + sacc_b
    small = {
        "b_gate": dbg[0], "conv_w": dcw[0:KCONV], "conv_b": dcb[0],
        "dt_bias_f": sacc[0, 0:32], "dt_bias_b": sacc[0, 32:64], "a_log_f": sacc[1, 0:32], "a_log_b": sacc[1, 32:64],
        "d_skip": dds[0, 0:32], "ssd_norm_w": dnw[0],
        "ln1_g": acc1[0], "ln1_b": acc1[1], "ln2_g": acc2[0], "ln2_b": acc2[1], "loss": acc2[2, 0:1],
    }
    dw_in = _unperm_cols(dw_in_p)
    big = {
        "w_in": dw_in.reshape(D, 4, IN_COLS // 4).transpose(1, 0, 2),
        "w_proj_ssd": dw_proj_ssd.reshape(4, DI // 4, D),
        "w_proj_attn": dw_proj_attn,
        "w_out": dw_out.reshape(4, D // 4, D),
        "w_up": dw_up,
        "w_down": dw_down.reshape(4, DFF // 4, D),
    }
    return dx, big, small


HBM_SPEC = pl.BlockSpec(memory_space=pl.ANY)


def _place():
    x, y, c = lax.axis_index("x"), lax.axis_index("y"), lax.axis_index("c")
    chips = [(1 - x, y), (x, 1 - y), (1 - x, 1 - y)]
    return x, y, c, chips


def _allgather_weights(shards):
    n = len(shards)

    def body(*refs):
        ins, outs = refs[:n], refs[n:2 * n]
        send_sems, recv_sems = refs[2 * n:]
        x, y, c, _ = _place()
        q, q_x, q_y, q_d = 2 * x + y, 2 * (1 - x) + y, 2 * x + 1 - y, 2 * (1 - x) + 1 - y
        x_nbr, y_nbr, sibling = (1 - x, y, c), (x, 1 - y, c), (x, y, 1 - c)

        def copy(w, k, src, dst, to):
            return pltpu.make_async_remote_copy(src_ref=src, dst_ref=dst, send_sem=send_sems.at[w, k],
                                                recv_sem=recv_sems.at[w, k], device_id=to, device_id_type=MESH)

        def rows(w, core, part):
            rh = ins[w].shape[0] // 2
            if part is None:
                return pl.ds(core * rh, rh)
            return pl.ds(core * rh + part * (rh // 2), rh // 2)

        def same(w, k, slot, core, part, to):
            blk = outs[w].at[slot, rows(w, core, part), :]
            return copy(w, k, blk, blk, to)

        started = []
        for w in range(n):
            cp = copy(w, 8, ins[w], outs[w].at[q], sibling)
            cp.start()
            started.append(cp)
            mine = rows(w, c, None)
            for k, to in ((0, x_nbr), (1, y_nbr)):
                cp = copy(w, k, ins[w].at[mine, :], outs[w].at[q, mine, :], to)
                cp.start()
                started.append(cp)
        for w in range(n):
            same(w, 0, q_x, c, None, x_nbr).wait_recv()
            for cp in (same(w, 2, q_x, c, 0, y_nbr), same(w, 4, q_x, c, None, sibling)):
                cp.start()
                started.append(cp)
            same(w, 1, q_y, c, None, y_nbr).wait_recv()
            for cp in (same(w, 3, q_y, c, 1, x_nbr), same(w, 5, q_y, c, None, sibling)):
                cp.start()
                started.append(cp)
        for w in range(n):
            same(w, 2, q_d, c, 0, y_nbr).wait_recv()
            cp = same(w, 6, q_d, c, 0, sibling)
            cp.start()
            started.append(cp)
            same(w, 3, q_d, c, 1, x_nbr).wait_recv()
            cp = same(w, 7, q_d, c, 1, sibling)
            cp.start()
            started.append(cp)
        for w in range(n):
            same(w, 4, q_x, 1 - c, None, sibling).wait_recv()
            same(w, 5, q_y, 1 - c, None, sibling).wait_recv()
            same(w, 6, q_d, 1 - c, 0, sibling).wait_recv()
            same(w, 7, q_d, 1 - c, 1, sibling).wait_recv()
            copy(w, 8, ins[w], outs[w].at[q], sibling).wait_recv()
        for cp in started:
            cp.wait_send()

    return pl.pallas_call(
        body, out_shape=[jax.ShapeDtypeStruct((4,) + s.shape, s.dtype) for s in shards],
        in_specs=[HBM_SPEC] * n, out_specs=[HBM_SPEC] * n,
        scratch_shapes=[pltpu.SemaphoreType.DMA((n, 9)), pltpu.SemaphoreType.DMA((n, 9))],
        name="allgather_weights")(*shards)


def _swap_halves(grads):
    n = len(grads)

    def body(*refs):
        ins, outs = refs[:n], refs[n:2 * n]
        send_sems, recv_sems = refs[2 * n:]
        x, y, c, _ = _place()
        copies = []
        for w in range(n):
            rh = ins[w].shape[1] // 2
            for p in range(4):
                cp = pltpu.make_async_remote_copy(
                    src_ref=ins[w].at[p, pl.ds((1 - c) * rh, rh), :], dst_ref=outs[w].at[p],
                    send_sem=send_sems.at[w, p], recv_sem=recv_sems.at[w, p],
                    device_id=(x, y, 1 - c), device_id_type=MESH)
                cp.start()
                copies.append(cp)
        for cp in copies:
            cp.wait()

    return pl.pallas_call(
        body, out_shape=[jax.ShapeDtypeStruct((4, g.shape[1] // 2, g.shape[2]), F32) for g in grads],
        in_specs=[HBM_SPEC] * n, out_specs=[HBM_SPEC] * n,
        scratch_shapes=[pltpu.SemaphoreType.DMA((n, 4)), pltpu.SemaphoreType.DMA((n, 4))],
        name="rs_swap_halves")(*grads)


def _rs_step1(parts):
    n = len(parts)

    def body(*refs):
        ins, out_a, out_b = refs[:n], refs[n:2 * n], refs[2 * n:3 * n]
        send_sems, recv_sems = refs[3 * n:]
        x, y, c, _ = _place()
        copies = []
        for w in range(n):
            rq = ins[w].shape[1] // 2
            for i in range(2):
                copies.append(pltpu.make_async_remote_copy(
                    src_ref=ins[w].at[2 * (1 - x) + i, pl.ds(0, rq), :], dst_ref=out_a[w].at[i],
                    send_sem=send_sems.at[w, i], recv_sem=recv_sems.at[w, i],
                    device_id=(1 - x, y, c), device_id_type=MESH))
                copies.append(pltpu.make_async_remote_copy(
                    src_ref=ins[w].at[2 * i + 1 - y, pl.ds(rq, rq), :], dst_ref=out_b[w].at[i],
                    send_sem=send_sems.at[w, 2 + i], recv_sem=recv_sems.at[w, 2 + i],
                    device_id=(x, 1 - y, c), device_id_type=MESH))
        for cp in copies:
            cp.start()
        for cp in copies:
            cp.wait()

    quarter = lambda p: jax.ShapeDtypeStruct((2, p.shape[1] // 2, p.shape[2]), p.dtype)
    outs = pl.pallas_call(
        body, out_shape=[quarter(p) for p in parts] * 2,
        in_specs=[HBM_SPEC] * n, out_specs=[HBM_SPEC] * (2 * n),
        scratch_shapes=[pltpu.SemaphoreType.DMA((n, 4)), pltpu.SemaphoreType.DMA((n, 4))],
        name="rs_step1")(*parts)
    return outs[:n], outs[n:]


def _rs_step2(tas, tbs):
    n = len(tas)

    def body(*refs):
        in_a, in_b, out_a, out_b = refs[:n], refs[n:2 * n], refs[2 * n:3 * n], refs[3 * n:4 * n]
        send_sems, recv_sems = refs[4 * n:]
        x, y, c, _ = _place()
        copies = []
        for w in range(n):
            copies.append(pltpu.make_async_remote_copy(
                src_ref=in_a[w].at[1 - y], dst_ref=out_a[w], send_sem=send_sems.at[w, 0], recv_sem=recv_sems.at[w, 0],
                device_id=(x, 1 - y, c), device_id_type=MESH))
            copies.append(pltpu.make_async_remote_copy(
                src_ref=in_b[w].at[1 - x], dst_ref=out_b[w], send_sem=send_sems.at[w, 1], recv_sem=recv_sems.at[w, 1],
                device_id=(1 - x, y, c), device_id_type=MESH))
        for cp in copies:
            cp.start()
        for cp in copies:
            cp.wait()

    one = lambda p: jax.ShapeDtypeStruct(p.shape[1:], p.dtype)
    outs = pl.pallas_call(
        body, out_shape=[one(p) for p in tas] + [one(p) for p in tbs],
        in_specs=[HBM_SPEC] * (2 * n), out_specs=[HBM_SPEC] * (2 * n),
        scratch_shapes=[pltpu.SemaphoreType.DMA((n, 2)), pltpu.SemaphoreType.DMA((n, 2))],
        name="rs_step2")(*tas, *tbs)
    return outs[:n], outs[n:]


def _join_halves(pieces):
    n = len(pieces)

    def body(*refs):
        outs = refs[n:2 * n]
        send_sems, recv_sems = refs[2 * n:]
        x, y, c, _ = _place()

        def copy(w, slot):
            return pltpu.make_async_remote_copy(
                src_ref=outs[w].at[slot], dst_ref=outs[w].at[slot], send_sem=send_sems.at[w], recv_sem=recv_sems.at[w],
                device_id=(x, y, 1 - c), device_id_type=MESH)

        for w in range(n):
            copy(w, c).start()
        for w in range(n):
            copy(w, 1 - c).wait_recv()
            copy(w, c).wait_send()

    return pl.pallas_call(
        body, out_shape=[jax.ShapeDtypeStruct(p.shape, F32) for p in pieces],
        in_specs=[HBM_SPEC] * n, out_specs=[HBM_SPEC] * n, input_output_aliases={w: w for w in range(n)},
        scratch_shapes=[pltpu.SemaphoreType.DMA((n,)), pltpu.SemaphoreType.DMA((n,))],
        name="rs_join_halves")(*pieces)


def _add_tile_rows(rh, c):
    for cand in (512, 256, 128, 64, 32, 16, 8):
        if rh % cand == 0 and cand * c * 4 <= 2 ** 21:
            return cand
    return rh


def _add_half(grad, recv, c_idx, name):
    _, r, cc = grad.shape
    rh = r // 2
    tr = _add_tile_rows(rh, cc)
    nb = rh // tr

    def body(c_ref, g_ref, r_ref, o_ref, ob_ref):
        del c_ref
        s = g_ref[...] + r_ref[...]
        o_ref[...] = s
        ob_ref[...] = s.astype(BF16)

    blk = pl.BlockSpec((None, tr, cc), lambda p, i, c_ref: (p, i, 0))
    grid_spec = pltpu.PrefetchScalarGridSpec(
        num_scalar_prefetch=1, grid=(4, nb),
        in_specs=[pl.BlockSpec((None, tr, cc), lambda p, i, c_ref: (p, c_ref[0] * nb + i, 0)), blk],
        out_specs=(blk, blk))
    return pl.pallas_call(
        body, out_shape=(jax.ShapeDtypeStruct((4, rh, cc), F32), jax.ShapeDtypeStruct((4, rh, cc), BF16)),
        grid_spec=grid_spec, name=name, compiler_params=_params(("parallel", "parallel")))(c_idx, grad, recv)


def _rs_add1(part, recv_a, recv_b, xy_idx, name):
    _, rh, cc = part.shape
    rq = rh // 2
    tr = _add_tile_rows(rq, cc)
    nb = rq // tr

    def body(xy_ref, pa_ref, pb_ref, ra_ref, rb_ref, ta_ref, tb_ref, tab_ref, tbb_ref):
        del xy_ref
        ta = pa_ref[...] + ra_ref[...].astype(F32)
        tb = pb_ref[...] + rb_ref[...].astype(F32)
        ta_ref[...] = ta
        tb_ref[...] = tb
        tab_ref[...] = ta.astype(BF16)
        tbb_ref[...] = tb.astype(BF16)

    blk = pl.BlockSpec((None, tr, cc), lambda i, j, xy: (i, j, 0))
    grid_spec = pltpu.PrefetchScalarGridSpec(
        num_scalar_prefetch=1, grid=(2, nb),
        in_specs=[pl.BlockSpec((None, tr, cc), lambda i, j, xy: (2 * xy[0] + i, j, 0)),
                  pl.BlockSpec((None, tr, cc), lambda i, j, xy: (2 * i + xy[1], nb + j, 0)), blk, blk],
        out_specs=(blk, blk, blk, blk))
    f32s, b16s = jax.ShapeDtypeStruct((2, rq, cc), F32), jax.ShapeDtypeStruct((2, rq, cc), BF16)
    return pl.pallas_call(body, out_shape=(f32s, f32s, b16s, b16s), grid_spec=grid_spec, name=name,
                          compiler_params=_params(("parallel", "parallel")))(xy_idx, part, part, recv_a, recv_b)


def _rs_add2(ta, tb, recv_a, recv_b, xy_idx, name):
    _, rq, cc = ta.shape
    tr = _add_tile_rows(rq, cc)
    nb = rq // tr

    def body(xy_ref, ta_ref, tb_ref, ra_ref, rb_ref, o_ref):
        del xy_ref
        s = pl.program_id(0)
        fa = ta_ref[...] + ra_ref[...].astype(F32)
        fb = tb_ref[...] + rb_ref[...].astype(F32)
        o_ref[...] = jnp.where(s == 0, fa, fb)

    rblk = pl.BlockSpec((tr, cc), lambda s, j, xy: (j, 0))
    grid_spec = pltpu.PrefetchScalarGridSpec(
        num_scalar_prefetch=1, grid=(2, nb),
        in_specs=[pl.BlockSpec((None, tr, cc), lambda s, j, xy: (xy[1], j, 0)),
                  pl.BlockSpec((None, tr, cc), lambda s, j, xy: (xy[0], j, 0)), rblk, rblk],
        out_specs=pl.BlockSpec((None, tr, cc), lambda s, j, xy: (xy[2], s * nb + j, 0)))
    return pl.pallas_call(body, out_shape=jax.ShapeDtypeStruct((2, 2 * rq, cc), F32), grid_spec=grid_spec, name=name,
                          compiler_params=_params(("parallel", "parallel")))(xy_idx, ta, tb, recv_a, recv_b)


def _allreduce_small(slab):
    r = slab.shape[0]

    def body(x_ref, o_ref, buf, send_sems, recv_sems):
        x, y, c, _ = _place()
        me = 4 * x + 2 * y + c
        buf[me] = x_ref[...]
        peers = []
        for k in range(1, 8):
            kx, ky, kc = (k >> 2) & 1, (k >> 1) & 1, k & 1
            peers.append((x + kx - 2 * x * kx, y + ky - 2 * y * ky, c + kc - 2 * c * kc))

        def copy(k, slot):
            return pltpu.make_async_remote_copy(src_ref=x_ref, dst_ref=buf.at[slot], send_sem=send_sems.at[k],
                                                recv_sem=recv_sems.at[k], device_id=peers[k], device_id_type=MESH)

        for k in range(7):
            copy(k, me).start()
        for k, (px, py, pc) in enumerate(peers):
            copy(k, 4 * px + 2 * py + pc).wait_recv()
        for k in range(7):
            copy(k, me).wait_send()
        acc = buf[0]
        for j in range(1, 8):
            acc = acc + buf[j]
        o_ref[...] = acc

    vm = pl.BlockSpec(memory_space=pltpu.VMEM)
    return pl.pallas_call(
        body, out_shape=jax.ShapeDtypeStruct((r, 128), F32), in_specs=[vm], out_specs=vm,
        scratch_shapes=[pltpu.VMEM((8, r, 128), F32), pltpu.SemaphoreType.DMA((7,)), pltpu.SemaphoreType.DMA((7,))],
        name="allreduce_small")(slab)


def _pack(arrs):
    rows = []
    for a in arrs:
        v = a.reshape(-1)
        v = jnp.pad(v, (0, (-v.shape[0]) % 128))
        rows.append(v.reshape(-1, 128))
    slab = jnp.concatenate(rows, axis=0)
    return jnp.pad(slab, ((0, (-slab.shape[0]) % 8), (0, 0)))


def _unpack(slab, shapes):
    out, r0 = [], 0
    for shp in shapes:
        size = math.prod(shp)
        nr = -(-size // 128)
        out.append(slab[r0:r0 + nr].reshape(-1)[:size].reshape(shp))
        r0 += nr
    return out


BIG = ("w_in", "w_proj_ssd", "w_proj_attn", "w_out", "w_up", "w_down")
SMALL = ("b_gate", "conv_w", "conv_b", "dt_bias_f", "dt_bias_b", "a_log_f", "a_log_b", "d_skip", "ssd_norm_w",
         "ln1_g", "ln1_b", "ln2_g", "ln2_b")
ORDER = ("w_in", "b_gate", "conv_w", "conv_b", "dt_bias_f", "dt_bias_b", "a_log_f", "a_log_b", "d_skip", "ssd_norm_w",
         "w_proj_ssd", "w_proj_attn", "w_out", "ln1_g", "ln1_b", "w_up", "w_down", "ln2_g", "ln2_b")


def kernel(x, w_in, b_gate, conv_w, conv_b, dt_bias_f, dt_bias_b, a_log_f, a_log_b, d_skip, ssd_norm_w, w_proj_ssd, w_proj_attn, w_out, ln1_g, ln1_b, w_up, w_down, ln2_g, ln2_b, loss_target, m_w_in, m_b_gate, m_conv_w, m_conv_b, m_dt_bias_f, m_dt_bias_b, m_a_log_f, m_a_log_b, m_d_skip, m_ssd_norm_w, m_w_proj_ssd, m_w_proj_attn, m_w_out, m_ln1_g, m_ln1_b, m_w_up, m_w_down, m_ln2_g, m_ln2_b, v_w_in, v_b_gate, v_conv_w, v_conv_b, v_dt_bias_f, v_dt_bias_b, v_a_log_f, v_a_log_b, v_d_skip, v_ssd_norm_w, v_w_proj_ssd, v_w_proj_attn, v_w_out, v_ln1_g, v_ln1_b, v_w_up, v_w_down, v_ln2_g, v_ln2_b):
    w = dict(w_in=w_in, b_gate=b_gate, conv_w=conv_w, conv_b=conv_b, dt_bias_f=dt_bias_f, dt_bias_b=dt_bias_b,
             a_log_f=a_log_f, a_log_b=a_log_b, d_skip=d_skip, ssd_norm_w=ssd_norm_w, w_proj_ssd=w_proj_ssd,
             w_proj_attn=w_proj_attn, w_out=w_out, ln1_g=ln1_g, ln1_b=ln1_b, w_up=w_up, w_down=w_down, ln2_g=ln2_g, ln2_b=ln2_b)
    m = dict(w_in=m_w_in, b_gate=m_b_gate, conv_w=m_conv_w, conv_b=m_conv_b, dt_bias_f=m_dt_bias_f, dt_bias_b=m_dt_bias_b,
             a_log_f=m_a_log_f, a_log_b=m_a_log_b, d_skip=m_d_skip, ssd_norm_w=m_ssd_norm_w, w_proj_ssd=m_w_proj_ssd,
             w_proj_attn=m_w_proj_attn, w_out=m_w_out, ln1_g=m_ln1_g, ln1_b=m_ln1_b, w_up=m_w_up, w_down=m_w_down,
             ln2_g=m_ln2_g, ln2_b=m_ln2_b)
    v = dict(w_in=v_w_in, b_gate=v_b_gate, conv_w=v_conv_w, conv_b=v_conv_b, dt_bias_f=v_dt_bias_f, dt_bias_b=v_dt_bias_b,
             a_log_f=v_a_log_f, a_log_b=v_a_log_b, d_skip=v_d_skip, ssd_norm_w=v_ssd_norm_w, w_proj_ssd=v_w_proj_ssd,
             w_proj_attn=v_w_proj_attn, w_out=v_w_out, ln1_g=v_ln1_g, ln1_b=v_ln1_b, w_up=v_w_up, w_down=v_w_down,
             ln2_g=v_ln2_g, ln2_b=v_ln2_b)
    xi, yi, ci = lax.axis_index("x"), lax.axis_index("y"), lax.axis_index("c")
    shard = 2 * xi + yi

    g_in, g_ps, g_pa, g_o, g_up, g_dn = _allgather_weights([w[n].astype(BF16) for n in BIG])
    w_in_full = jnp.concatenate([g_in[s] for s in range(4)], axis=1)
    wts = {"w_in_p": _perm_cols(w_in_full), "w_proj_ssd": g_ps.reshape(DI, D), "w_proj_attn": g_pa,
           "w_out": g_o.reshape(D, D), "w_up": g_up, "w_down": g_dn.reshape(DFF, D)}

    cw_slab = jnp.zeros((KCONV, 4, CONVD // 4), F32)
    cw_slab = lax.dynamic_update_slice(cw_slab, conv_w[:, None, :] * 0.5, (0, shard, 0))
    conv_w_all = _unpack(_allreduce_small(_pack([cw_slab])), [(KCONV, CONVD)])[0]

    sm = {n: w[n] for n in SMALL}
    sm["conv_w"] = conv_w_all
    dx, big, small = _local_grads(x[0], loss_target[0], wts, sm)

    names = list(SMALL) + ["loss"]
    shapes = [small[n].shape for n in names]
    red = dict(zip(names, _unpack(_allreduce_small(_pack([small[n] for n in names])), shapes)))
    loss = red["loss"].reshape(())
    gsm = {n: red[n] for n in SMALL}
    conv_w_grad_shard = lax.dynamic_slice_in_dim(gsm["conv_w"].reshape(KCONV, 4, CONVD // 4), shard, 1, axis=1)
    gsm["conv_w"] = conv_w_grad_shard.reshape(KCONV, CONVD // 4)

    c_idx = jnp.reshape(ci, (1,)).astype(jnp.int32)
    glist = [big[n] for n in BIG]
    xy_idx = jnp.stack([xi, yi, ci]).astype(jnp.int32)
    recv = _swap_halves(glist)
    halves = [_add_half(g, r, c_idx, f"rs_add_half_{n}") for g, r, n in zip(glist, recv, BIG)]
    recv_a, recv_b = _rs_step1([h[1] for h in halves])
    sums1 = [_rs_add1(h[0], ra, rb, xy_idx, f"rs_add1_{n}") for h, ra, rb, n in zip(halves, recv_a, recv_b, BIG)]
    recv_a2, recv_b2 = _rs_step2([s1[2] for s1 in sums1], [s1[3] for s1 in sums1])
    pieces = [_rs_add2(s1[0], s1[1], ra, rb, xy_idx, f"rs_add2_{n}")
              for s1, ra, rb, n in zip(sums1, recv_a2, recv_b2, BIG)]
    joined = _join_halves(pieces)
    gbig = {n: j.reshape(w[n].shape) for n, j in zip(BIG, joined)}

    grads, deltas, new_m, new_v = {}, {}, {}, {}
    for n in BIG:
        grads[n] = gbig[n]
        deltas[n], new_m[n], new_v[n] = _adamw(w[n], gbig[n], m[n], v[n], f"adamw_{n}")
    sshapes = [w[n].shape for n in SMALL]
    d_s, m_s, v_s = _adamw(_pack([w[n] for n in SMALL]), _pack([gsm[n] for n in SMALL]),
                           _pack([m[n] for n in SMALL]), _pack([v[n] for n in SMALL]), "adamw_small")
    for n, dd, mm, vv in zip(SMALL, _unpack(d_s, sshapes), _unpack(m_s, sshapes), _unpack(v_s, sshapes)):
        grads[n], deltas[n], new_m[n], new_v[n] = gsm[n], dd, mm, vv

    return (loss, dx[None], *[grads[n] for n in ORDER], *[deltas[n] for n in ORDER],
            *[new_m[n] for n in ORDER], *[new_v[n] for n in ORDER])
```

```python
import math

import jax
import jax.numpy as jnp
from jax import lax
from jax.experimental import pallas as pl
from jax.experimental.pallas import tpu as pltpu

F32, BF16 = jnp.float32, jnp.bfloat16
MESH = pl.DeviceIdType.MESH

D = 1024
DI = 2048
NH = 32
HP = 64
NG = 4
NS = 128
Q = 128
CONVD = 3072
KCONV = 5
DFF = 4096
AH = 64
ATT_HALF = 64
DILATIONS = (1, 4, 16)
IN_COLS = 9536
OZ, OGATE, OXBC, OKV, OQ, ODT, UW = 0, 2048, 4096, 7168, 8704, 9472, 9728
ALPHA = 2.0 ** 0.25
NORM_EPS = 1e-5
ADAM_LR, ADAM_B1, ADAM_B2, ADAM_EPS, ADAM_WD, ADAM_STEP = 0.001, 0.9, 0.999, 1e-8, 0.01, 10
VMEM_LIMIT = 56 * 2 ** 20
NEG = -1e30


def _params(sem):
    return pltpu.CompilerParams(dimension_semantics=sem, vmem_limit_bytes=VMEM_LIMIT)


def _sigmoid(x):
    return 1.0 / (1.0 + jnp.exp(-x))


def _softplus(x):
    e = jnp.exp(-jnp.abs(x))
    small = e * (1.0 - e * (0.5 - e * (1.0 / 3.0)))
    return jnp.maximum(x, 0.0) + jnp.where(e < 0.01, small, jnp.log(1.0 + e))


def _split3(a):
    hi = a.astype(BF16)
    r = a - hi.astype(F32)
    mid = r.astype(BF16)
    lo = (r - mid.astype(F32)).astype(BF16)
    return hi, mid, lo


def _dot01(a, m01):
    hi, mid, lo = _split3(a)
    d = lambda p: jnp.dot(p, m01, preferred_element_type=F32)
    return d(hi) + d(mid) + d(lo)


def _dot01_l(m01, a):
    hi, mid, lo = _split3(a)
    d = lambda p: jnp.dot(m01, p, preferred_element_type=F32)
    return d(hi) + d(mid) + d(lo)


def _dot_nt(a, b):
    return lax.dot_general(a, b, (((1,), (1,)), ((), ())), preferred_element_type=F32)


def _iota(shape, dim):
    return lax.broadcasted_iota(jnp.int32, shape, dim)


def _mm_nn(a, b, *, tm, tn, name, out_dtype=F32):
    m, k = a.shape
    if b.ndim == 3:
        assert tn == b.shape[2]
        n = b.shape[0] * b.shape[2]
        b_spec = pl.BlockSpec((None, k, tn), lambda j, i: (j, 0, 0))
    else:
        n = b.shape[1]
        b_spec = pl.BlockSpec((k, tn), lambda j, i: (0, j))

    def body(a_ref, b_ref, o_ref):
        o_ref[...] = jnp.dot(a_ref[...].astype(BF16), b_ref[...], preferred_element_type=F32).astype(out_dtype)

    return pl.pallas_call(
        body, out_shape=jax.ShapeDtypeStruct((m, n), out_dtype), grid=(n // tn, m // tm),
        in_specs=[pl.BlockSpec((tm, k), lambda j, i: (i, 0)), b_spec],
        out_specs=pl.BlockSpec((tm, tn), lambda j, i: (i, j)),
        name=name, compiler_params=_params(("parallel", "parallel")))(a, b)


def _mm_nt(a, b, *, tm, tk, tc, name, add=None, add_scale=1.0):
    m, n = a.shape
    if b.ndim == 3:
        assert tc == b.shape[2]
        k, nc = b.shape[1], b.shape[0]
        b_spec = pl.BlockSpec((None, tk, tc), lambda j, i, c: (c, j, 0))
    else:
        k, nc = b.shape[0], n // tc
        b_spec = pl.BlockSpec((tk, tc), lambda j, i, c: (j, c))

    def body(*refs):
        if add is None:
            a_ref, b_ref, o_ref = refs
        else:
            a_ref, b_ref, add_ref, o_ref = refs
        c = pl.program_id(2)
        part = _dot_nt(a_ref[...].astype(BF16), b_ref[...])

        @pl.when(c == 0)
        def _():
            if add is None:
                o_ref[...] = part
            else:
                o_ref[...] = part + add_scale * add_ref[...]

        @pl.when(c > 0)
        def _():
            o_ref[...] += part

    in_specs = [pl.BlockSpec((tm, tc), lambda j, i, c: (i, c)), b_spec]
    args = [a, b]
    if add is not None:
        in_specs.append(pl.BlockSpec((tm, tk), lambda j, i, c: (i, j)))
        args.append(add)
    return pl.pallas_call(
        body, out_shape=jax.ShapeDtypeStruct((m, k), F32), grid=(k // tk, m // tm, nc),
        in_specs=in_specs, out_specs=pl.BlockSpec((tm, tk), lambda j, i, c: (i, j)),
        name=name, compiler_params=_params(("parallel", "parallel", "arbitrary")))(*args)


def _mm_tn(a, b, *, tka, tn, tt, name, out_shards=None):
    t, ka = a.shape
    n = b.shape[1]
    if out_shards:
        assert tn == n // out_shards
        out_shape = jax.ShapeDtypeStruct((out_shards, ka, tn), F32)
        o_spec = pl.BlockSpec((None, tka, tn), lambda i, j, s: (j, i, 0))
    else:
        out_shape = jax.ShapeDtypeStruct((ka, n), F32)
        o_spec = pl.BlockSpec((tka, tn), lambda i, j, s: (i, j))

    def body(a_ref, b_ref, o_ref):
        s = pl.program_id(2)
        part = lax.dot_general(a_ref[...].astype(BF16), b_ref[...].astype(BF16), (((0,), (0,)), ((), ())),
                               preferred_element_type=F32)

        @pl.when(s == 0)
        def _():
            o_ref[...] = part

        @pl.when(s > 0)
        def _():
            o_ref[...] += part

    return pl.pallas_call(
        body, out_shape=out_shape, grid=(ka // tka, n // tn, t // tt),
        in_specs=[pl.BlockSpec((tt, tka), lambda i, j, s: (s, i)), pl.BlockSpec((tt, tn), lambda i, j, s: (s, j))],
        out_specs=o_spec, name=name, compiler_params=_params(("parallel", "parallel", "arbitrary")))(a, b)


CONV_TM = 512
CONV_TC = 1024
CONV_RC = 64
CONV_CC = 256


def _halo_specs(t, tm, tc, col0):
    nb8 = t // 8
    r8 = tm // 8
    return [
        pl.BlockSpec((8, tc), lambda i, j: (jnp.maximum(i * r8 - 1, 0), col0 + j)),
        pl.BlockSpec((tm, tc), lambda i, j: (i, col0 + j)),
        pl.BlockSpec((8, tc), lambda i, j: (jnp.minimum((i + 1) * r8, nb8 - 1), col0 + j)),
    ]


def _fill_ext(ext, prev_ref, cur_ref, next_ref, tm, i, last):
    ext[0:8, :] = jnp.where(i > 0, prev_ref[...], 0.0)
    ext[8:8 + tm, :] = cur_ref[...]
    ext[8 + tm:16 + tm, :] = jnp.where(i < last, next_ref[...], 0.0)


def _conv_fwd(u, conv_w, conv_b):
    t = u.shape[0]
    tm, tc = CONV_TM, CONV_TC

    def body(prev_ref, cur_ref, next_ref, w_ref, b_ref, o_ref, ext):
        _fill_ext(ext, prev_ref, cur_ref, next_ref, tm, pl.program_id(0), t // tm - 1)
        for c0 in range(0, tc, CONV_CC):
            cs = slice(c0, c0 + CONV_CC)
            w = w_ref[:, cs]
            for r0 in range(0, tm, CONV_RC):
                acc = jnp.broadcast_to(b_ref[:, cs], (CONV_RC, CONV_CC))
                for k in range(KCONV):
                    acc = acc + w[k:k + 1, :] * ext[pl.ds(r0 + 6 + k, CONV_RC), cs]
                o_ref[r0:r0 + CONV_RC, cs] = acc * _sigmoid(acc)

    return pl.pallas_call(
        body, out_shape=jax.ShapeDtypeStruct((t, CONVD), F32), grid=(t // tm, CONVD // tc),
        in_specs=_halo_specs(t, tm, tc, OXBC // tc) + [
            pl.BlockSpec((KCONV, tc), lambda i, j: (0, j)), pl.BlockSpec((1, tc), lambda i, j: (0, j))],
        out_specs=pl.BlockSpec((tm, tc), lambda i, j: (i, j)),
        scratch_shapes=[pltpu.VMEM((tm + 16, tc), F32)],
        name="conv_fwd", compiler_params=_params(("parallel", "parallel")))(u, u, u, conv_w, conv_b)


def _conv_dpre(u, dxs_f, dxs_b, dy, dbc_f, dbc_b, dsk_row, conv_w, conv_b):
    t = u.shape[0]
    tm, tc = CONV_TM, CONV_TC
    r8 = tm // 8
    nb8 = t // 8
    c0 = OXBC // tc

    def body(uprev, ucur, unext, f_ref, b_ref, y_ref, cf_ref, cb_ref, dsk_ref, w_ref, bias_ref,
             dpre_ref, dw_ref, db_ref, ext):
        j = pl.program_id(0)
        i = pl.program_id(1)
        _fill_ext(ext, uprev, ucur, unext, tm, i, t // tm - 1)
        is_xs = j < 2
        dw_cols, db_cols = [], []
        for c0 in range(0, tc, CONV_CC):
            cs = slice(c0, c0 + CONV_CC)
            w = w_ref[:, cs]
            dsk = dsk_ref[:, cs]
            dw_acc = [jnp.zeros((1, CONV_CC), F32) for _ in range(KCONV)]
            db_acc = jnp.zeros((1, CONV_CC), F32)
            for r0 in range(0, tm, CONV_RC):
                rs = slice(r0, r0 + CONV_RC)
                taps = [ext[pl.ds(r0 + 6 + k, CONV_RC), cs] for k in range(KCONV)]
                pre = jnp.broadcast_to(bias_ref[:, cs], (CONV_RC, CONV_CC))
                for k in range(KCONV):
                    pre = pre + w[k:k + 1, :] * taps[k]
                s = _sigmoid(pre)
                xs_part = f_ref[rs, cs] + b_ref[rs, cs] + dsk * y_ref[rs, cs]
                up = jnp.where(is_xs, xs_part, cf_ref[rs, cs] + cb_ref[rs, cs])
                dpre = up * (s * (1.0 + pre * (1.0 - s)))
                dpre_ref[rs, cs] = dpre
                for k in range(KCONV):
                    dw_acc[k] = dw_acc[k] + jnp.sum(dpre * taps[k], axis=0, keepdims=True)
                db_acc = db_acc + jnp.sum(dpre, axis=0, keepdims=True)
            dw_cols.append(jnp.concatenate(dw_acc + [jnp.zeros((8 - KCONV, CONV_CC), F32)], axis=0))
            db_cols.append(jnp.broadcast_to(db_acc, (8, CONV_CC)))
        dw_part = jnp.concatenate(dw_cols, axis=1)
        db_part = jnp.concatenate(db_cols, axis=1)

        @pl.when(i == 0)
        def _():
            dw_ref[...] = dw_part
            db_ref[...] = db_part

        @pl.when(i > 0)
        def _():
            dw_ref[...] += dw_part
            db_ref[...] += db_part

    xs_spec = pl.BlockSpec((tm, tc), lambda j, i: (jnp.where(j < 2, i, 0), jnp.minimum(j, 1)))
    bc_spec = pl.BlockSpec((tm, tc), lambda j, i: (jnp.where(j == 2, i, 0), 0))
    in_specs = [
        pl.BlockSpec((8, tc), lambda j, i: (jnp.maximum(i * r8 - 1, 0), c0 + j)),
        pl.BlockSpec((tm, tc), lambda j, i: (i, c0 + j)),
        pl.BlockSpec((8, tc), lambda j, i: (jnp.minimum((i + 1) * r8, nb8 - 1), c0 + j)),
        xs_spec, xs_spec, xs_spec, bc_spec, bc_spec,
        pl.BlockSpec((1, tc), lambda j, i: (0, jnp.minimum(j, 1))),
        pl.BlockSpec((KCONV, tc), lambda j, i: (0, j)), pl.BlockSpec((1, tc), lambda j, i: (0, j)),
    ]
    return pl.pallas_call(
        body,
        out_shape=(jax.ShapeDtypeStruct((t, CONVD), F32), jax.ShapeDtypeStruct((8, CONVD), F32),
                   jax.ShapeDtypeStruct((8, CONVD), F32)),
        grid=(CONVD // tc, t // tm), in_specs=in_specs,
        out_specs=(pl.BlockSpec((tm, tc), lambda j, i: (i, j)),
                   pl.BlockSpec((8, tc), lambda j, i: (0, j)), pl.BlockSpec((8, tc), lambda j, i: (0, j))),
        scratch_shapes=[pltpu.VMEM((tm + 16, tc), F32)],
        name="conv_dpre", compiler_params=_params(("parallel", "arbitrary")))(
            u, u, u, dxs_f, dxs_b, dy, dbc_f, dbc_b, dsk_row, conv_w, conv_b)


def _conv_dx(du, dpre, conv_w):
    t = dpre.shape[0]
    tm, tc = CONV_TM, CONV_TC
    r8 = tm // 8
    nb8 = t // 8

    def body(prev_ref, cur_ref, next_ref, w_ref, du_in, du_out, ext):
        del du_in
        _fill_ext(ext, prev_ref, cur_ref, next_ref, tm, pl.program_id(1), t // tm - 1)
        for c0 in range(0, tc, CONV_CC):
            cs = slice(c0, c0 + CONV_CC)
            w = w_ref[:, cs]
            for r0 in range(0, tm, CONV_RC):
                acc = jnp.zeros((CONV_RC, CONV_CC), F32)
                for k in range(KCONV):
                    acc = acc + w[k:k + 1, :] * ext[pl.ds(r0 + 10 - k, CONV_RC), cs]
                du_out[r0:r0 + CONV_RC, cs] = acc.astype(du_out.dtype)

    in_specs = [
        pl.BlockSpec((8, tc), lambda j, i: (jnp.maximum(i * r8 - 1, 0), j)),
        pl.BlockSpec((tm, tc), lambda j, i: (i, j)),
        pl.BlockSpec((8, tc), lambda j, i: (jnp.minimum((i + 1) * r8, nb8 - 1), j)),
        pl.BlockSpec((KCONV, tc), lambda j, i: (0, j)),
        pl.BlockSpec(memory_space=pl.ANY),
    ]
    return pl.pallas_call(
        body, out_shape=jax.ShapeDtypeStruct(du.shape, du.dtype), grid=(CONVD // tc, t // tm), in_specs=in_specs,
        out_specs=pl.BlockSpec((tm, tc), lambda j, i: (i, OXBC // tc + j)),
        scratch_shapes=[pltpu.VMEM((tm + 16, tc), F32)], input_output_aliases={4: 0},
        name="conv_dx", compiler_params=_params(("parallel", "parallel")))(dpre, dpre, dpre, conv_w, du)


def _ssd_common(dtr_ref, par_ref, rev):
    raw = dtr_ref[...]
    lane = _iota((1, 128), 1)
    mine = (lane >= 32 * rev) & (lane < 32 * rev + 32)
    bias = par_ref[0:1, :]
    arow = jnp.where(mine, -jnp.exp(par_ref[1:2, :]), 0.0)
    dt = _softplus(raw + bias)
    a = dt * arow
    ri = _iota((Q, Q), 0)
    ci = _iota((Q, Q), 1)
    tri = (ci >= ri) if rev else (ci <= ri)
    trit = (ci <= ri) if rev else (ci >= ri)
    cs = _dot01_l(tri.astype(BF16), a)
    return raw, bias, arow, mine, dt, cs, tri, trit


def _expand_mat(rev):
    r = _iota((128, DI), 0)
    c = _iota((128, DI), 1)
    return (r == (c // HP) + 32 * rev).astype(BF16)


def _sum_mat(rev):
    r = _iota((DI, 128), 0)
    c = _iota((DI, 128), 1)
    return (c == (r // HP) + 32 * rev).astype(BF16)


def _ssd_fwd(xbc, u, par, *, rev):
    t = xbc.shape[0]
    nc = t // Q
    end = 0 if rev else Q - 1
    cmap = (lambda c: nc - 1 - c) if rev else (lambda c: c)

    def body(xbc_ref, dtr_ref, par_ref, y_ref, st_ref, h_scr):
        step = pl.program_id(0)

        @pl.when(step == 0)
        def _():
            h_scr[...] = jnp.zeros((NS, DI), F32)

        raw, bias, arow, mine, dt, cs, tri, trit = _ssd_common(dtr_ref, par_ref, rev)
        cst = cs.T
        dtt = dt.T
        tot_col = cst[:, end:end + 1]
        wt = dtt * jnp.exp(tot_col - cst)
        gam = jnp.exp(cs[end:end + 1, :])
        gam_x = _dot01(jnp.broadcast_to(gam, (8, 128)), _expand_mat(rev))[0:1, :]
        lane = _iota((Q, 128), 1)
        sel = lane < HP
        st_ref[...] = h_scr[...]
        for g in range(NG):
            bg = xbc_ref[:, DI + NS * g:DI + NS * (g + 1)]
            cg = xbc_ref[:, DI + NG * NS + NS * g:DI + NG * NS + NS * (g + 1)]
            cb = _dot_nt(cg.astype(BF16), bg.astype(BF16))
            bt = bg.T
            for k in range(4):
                lo = 512 * g + 128 * k
                xp = xbc_ref[:, lo:lo + 128].astype(BF16)
                hp = h_scr[:, lo:lo + 128]
                rhs = jnp.concatenate([xp, hp.astype(BF16)], axis=0)
                lhs, bts = [], []
                for j in range(2):
                    hc = 8 * g + 2 * k + j + 32 * rev
                    csc = jnp.broadcast_to(cs[:, hc:hc + 1], (Q, Q))
                    lm = jnp.exp(jnp.where(tri, csc - cst[hc:hc + 1, :], NEG)) * dtt[hc:hc + 1, :]
                    mh = (cb * lm).astype(BF16)
                    ec = (jnp.exp(csc) * cg).astype(BF16)
                    lhs.append(jnp.concatenate([mh, ec], axis=1))
                    bts.append((bt * wt[hc:hc + 1, :]).astype(BF16))
                ys = jnp.dot(jnp.concatenate(lhs, axis=0), rhs, preferred_element_type=F32)
                ss = jnp.dot(jnp.concatenate(bts, axis=0), xp, preferred_element_type=F32)
                y_ref[:, lo:lo + 128] = jnp.where(sel, ys[0:Q], ys[Q:2 * Q])
                h_scr[:, lo:lo + 128] = gam_x[:, lo:lo + 128] * hp + jnp.where(sel, ss[0:NS], ss[NS:2 * NS])

    return pl.pallas_call(
        body,
        out_shape=(jax.ShapeDtypeStruct((t, DI), F32), jax.ShapeDtypeStruct((nc, NS, DI), F32)),
        grid=(nc,),
        in_specs=[pl.BlockSpec((Q, CONVD), lambda c: (cmap(c), 0)),
                  pl.BlockSpec((Q, 128), lambda c: (cmap(c), ODT // 128)),
                  pl.BlockSpec((8, 128), lambda c: (0, 0))],
        out_specs=(pl.BlockSpec((Q, DI), lambda c: (cmap(c), 0)),
                   pl.BlockSpec((None, NS, DI), lambda c: (cmap(c), 0, 0))),
        scratch_shapes=[pltpu.VMEM((NS, DI), F32)],
        name="ssd_fwd_rev" if rev else "ssd_fwd", compiler_params=_params(("arbitrary",)))(xbc, u, par)


def _ssd_bwd(xbc, u, par, dy, st, *, rev):
    t = xbc.shape[0]
    nc = t // Q
    end = 0 if rev else Q - 1
    cmap = (lambda c: c) if rev else (lambda c: nc - 1 - c)

    def body(xbc_ref, dtr_ref, par_ref, dy_ref, hin_ref, dxs_ref, dbc_ref, ddt_ref, acc_ref, dh_scr):
        step = pl.program_id(0)

        @pl.when(step == 0)
        def _():
            dh_scr[...] = jnp.zeros((NS, DI), F32)

        raw, bias, arow, mine, dt, cs, tri, trit = _ssd_common(dtr_ref, par_ref, rev)
        ri = _iota((Q, Q), 0)
        ci = _iota((Q, Q), 1)
        stri = ((ri > ci) if rev else (ri < ci)).astype(BF16)
        strit = ((ci > ri) if rev else (ci < ri)).astype(BF16)
        cst = cs.T
        dtt = dt.T
        et = jnp.exp(cst)
        expand = _expand_mat(rev)
        summat = _sum_mat(rev)
        gam = jnp.exp(cs[end:end + 1, :])
        gam_x = _dot01(jnp.broadcast_to(gam, (8, 128)), expand)[0:1, :]
        dt_hi, dt_mid, _ = _split3(dt)
        dtx = (jnp.dot(dt_hi, expand, preferred_element_type=F32)
               + jnp.dot(dt_mid, expand, preferred_element_type=F32))
        lane = _iota((Q, 128), 1)
        sel = lane < HP
        dho = dh_scr[...]
        t3 = jnp.sum(dho * hin_ref[...], axis=0, keepdims=True) * gam_x
        dxs_cols, dxs2_cols, yoff_cols, a1_rows = [], [], [], []
        for g in range(NG):
            bg = xbc_ref[:, DI + NS * g:DI + NS * (g + 1)]
            cg = xbc_ref[:, DI + NG * NS + NS * g:DI + NG * NS + NS * (g + 1)]
            bb = bg.astype(BF16)
            cbf = cg.astype(BF16)
            cb = _dot_nt(cbf, bb)
            cbt = _dot_nt(bb, cbf)
            ct = cg.T
            bdh = jnp.dot(bb, dho[:, 512 * g:512 * (g + 1)].astype(BF16), preferred_element_type=F32)
            dcb = jnp.zeros((Q, Q), F32)
            dcg = jnp.zeros((Q, NS), F32)
            dbg = jnp.zeros((Q, NS), F32)
            for k in range(4):
                lo = 512 * g + 128 * k
                xpf = xbc_ref[:, lo:lo + 128]
                xp = xpf.astype(BF16)
                dyp = dy_ref[:, lo:lo + 128]
                dypb = dyp.astype(BF16)
                hinp = hin_ref[:, lo:lo + 128].astype(BF16)
                dhp = dho[:, lo:lo + 128]
                es, ws, lmds, mts, ctes, dyms, ecbs = [], [], [], [], [], [], []
                for j in range(2):
                    hc = 8 * g + 2 * k + j + 32 * rev
                    csc = jnp.broadcast_to(cs[:, hc:hc + 1], (Q, Q))
                    csr = cst[hc:hc + 1, :]
                    lmds.append(jnp.exp(jnp.where(tri, csc - csr, NEG)) * dtt[hc:hc + 1, :])
                    lmb = jnp.exp(jnp.where(trit, csr - csc, NEG))
                    mts.append((cbt * lmb).astype(BF16))
                    dyms.append(jnp.where(sel if j == 0 else ~sel, dyp, 0.0).astype(BF16))
                    ecs = jnp.exp(csc)
                    es.append(ecs)
                    ws.append(jnp.exp(cst[hc:hc + 1, end:end + 1] - csc))
                    ecbs.append((ecs * cg).astype(BF16))
                    ctes.append((ct * et[hc:hc + 1, :]).astype(BF16))
                by_dy = jnp.dot(jnp.concatenate(mts + ctes, axis=0), dypb, preferred_element_type=F32)
                dmm = _dot_nt(jnp.concatenate(dyms, axis=0), xp)
                dm0, dm1 = dmm[0:Q] * lmds[0], dmm[Q:2 * Q] * lmds[1]
                dcb = dcb + dm0 + dm1
                rr = jnp.dot(jnp.concatenate([dm0 * cb, dm1 * cb], axis=0).astype(BF16), stri, preferred_element_type=F32)
                a1_rows.append(jnp.sum(jnp.where(tri, rr[0:Q], 0.0), axis=0, keepdims=True))
                a1_rows.append(jnp.sum(jnp.where(tri, rr[Q:2 * Q], 0.0), axis=0, keepdims=True))
                yo = jnp.dot(jnp.concatenate(ecbs, axis=0), hinp, preferred_element_type=F32)
                e_p = jnp.where(sel, es[0], es[1])
                w_p = jnp.where(sel, ws[0], ws[1])
                d2 = w_p * bdh[:, 128 * k:128 * (k + 1)]
                dxs2_cols.append(d2)
                dxs_cols.append(jnp.where(sel, by_dy[0:Q], by_dy[Q:2 * Q]) + d2)
                yoff_cols.append(jnp.where(sel, yo[0:Q], yo[Q:2 * Q]))
                dcg = dcg + _dot_nt((e_p * dyp).astype(BF16), hinp)
                dbg = dbg + _dot_nt((w_p * dtx[:, lo:lo + 128] * xpf).astype(BF16), dhp.astype(BF16))
                dh_scr[:, lo:lo + 128] = (gam_x[:, lo:lo + 128] * dhp
                                          + jnp.where(sel, by_dy[2 * Q:3 * Q], by_dy[3 * Q:4 * Q]))
            dcg = dcg + jnp.dot(dcb.astype(BF16), bb, preferred_element_type=F32)
            dbg = dbg + jnp.dot(dcb.T.astype(BF16), cbf, preferred_element_type=F32)
            dbc_ref[:, NS * g:NS * (g + 1)] = dbg
            dbc_ref[:, NG * NS + NS * g:NG * NS + NS * (g + 1)] = dcg
        dxs = jnp.concatenate(dxs_cols, axis=1)
        dxs_ref[...] = dxs * dtx
        xs = xbc_ref[:, 0:DI]
        stacked = jnp.concatenate([xs * dxs, xs * jnp.concatenate(dxs2_cols, axis=1),
                                   dy_ref[...] * jnp.concatenate(yoff_cols, axis=1),
                                   jnp.broadcast_to(t3, (8, DI))], axis=0).astype(BF16)
        sums = jnp.dot(stacked, summat, preferred_element_type=F32)
        rx, rx2, ryo, c0 = sums[0:Q], sums[Q:2 * Q], sums[2 * Q:3 * Q], sums[3 * Q:3 * Q + 1]
        zero32 = jnp.zeros((32, Q), F32)
        a1t = jnp.concatenate(([zero32] if rev else []) + a1_rows + [zero32] * (2 if rev else 3), axis=0)
        da = (a1t.T + jnp.dot(trit.astype(BF16), ryo.astype(BF16), preferred_element_type=F32)
              + jnp.dot(strit, (dt * rx2).astype(BF16), preferred_element_type=F32) + jnp.where(mine, c0, 0.0))
        ddt = rx + da * arow
        ddtr = ddt * _sigmoid(raw + bias)
        ddt_ref[...] = ddtr
        part = jnp.concatenate([jnp.sum(ddtr, axis=0, keepdims=True),
                                jnp.sum(da * dt, axis=0, keepdims=True) * arow,
                                jnp.zeros((6, 128), F32)], axis=0)

        @pl.when(step == 0)
        def _():
            acc_ref[...] = part

        @pl.when(step > 0)
        def _():
            acc_ref[...] += part

    return pl.pallas_call(
        body,
        out_shape=(jax.ShapeDtypeStruct((t, DI), F32), jax.ShapeDtypeStruct((t, 2 * NG * NS), F32),
                   jax.ShapeDtypeStruct((t, 128), F32), jax.ShapeDtypeStruct((8, 128), F32)),
        grid=(nc,),
        in_specs=[pl.BlockSpec((Q, CONVD), lambda c: (cmap(c), 0)),
                  pl.BlockSpec((Q, 128), lambda c: (cmap(c), ODT // 128)),
                  pl.BlockSpec((8, 128), lambda c: (0, 0)),
                  pl.BlockSpec((Q, DI), lambda c: (cmap(c), 0)),
                  pl.BlockSpec((None, NS, DI), lambda c: (cmap(c), 0, 0))],
        out_specs=(pl.BlockSpec((Q, DI), lambda c: (cmap(c), 0)),
                   pl.BlockSpec((Q, 2 * NG * NS), lambda c: (cmap(c), 0)),
                   pl.BlockSpec((Q, 128), lambda c: (cmap(c), 0)),
                   pl.BlockSpec((8, 128), lambda c: (0, 0))),
        scratch_shapes=[pltpu.VMEM((NS, DI), F32)],
        name="ssd_bwd_rev" if rev else "ssd_bwd", compiler_params=_params(("arbitrary",)))(
            xbc, u, par, dy, st)


GN_TM = 256
GN_GROUP = DI // NG


def _gn_forward_vals(yf, yb, xs, z, dsk):
    y = yf + yb + dsk * xs
    sz = _sigmoid(z)
    gate = z * sz
    y2 = y * gate
    parts, rs = [], []
    for g in range(NG):
        seg = y2[:, GN_GROUP * g:GN_GROUP * (g + 1)]
        r = lax.rsqrt(jnp.mean(seg * seg, axis=1, keepdims=True) + NORM_EPS)
        rs.append(r)
        parts.append(seg * r)
    yn = jnp.concatenate(parts, axis=1)
    return y, sz, gate, yn, rs


def _gatenorm_fwd(y_f, y_b, xbc, u, dsk_row, nw_row):
    t = y_f.shape[0]
    tm = GN_TM

    def body(yf_ref, yb_ref, xs_ref, z_ref, dsk_ref, nw_ref, o_ref):
        _, _, _, yn, _ = _gn_forward_vals(yf_ref[...], yb_ref[...], xs_ref[...], z_ref[...], dsk_ref[...])
        o_ref[...] = (yn * nw_ref[...]).astype(BF16)

    blk = pl.BlockSpec((tm, DI), lambda i: (i, 0))
    row = pl.BlockSpec((1, DI), lambda i: (0, 0))
    return pl.pallas_call(
        body, out_shape=jax.ShapeDtypeStruct((t, DI), BF16), grid=(t // tm,),
        in_specs=[blk, blk, blk, pl.BlockSpec((tm, DI), lambda i: (i, OZ // DI)), row, row],
        out_specs=blk, name="gatenorm_fwd", compiler_params=_params(("parallel",)))(y_f, y_b, xbc, u, dsk_row, nw_row)


def _gatenorm_bwd(ds_out, y_f, y_b, xbc, u, du, dsk_row, nw_row):
    t = y_f.shape[0]
    tm = GN_TM

    def body(ds_ref, yf_ref, yb_ref, xs_ref, z_ref, dsk_ref, nw_ref, du_in, dy_ref, du_out, dnw_ref, dds_ref):
        del du_in
        i = pl.program_id(0)
        xs = xs_ref[...]
        z = z_ref[...]
        y, sz, gate, yn, rs = _gn_forward_vals(yf_ref[...], yb_ref[...], xs, z, dsk_ref[...])
        ds = ds_ref[...]
        gsc = ds * nw_ref[...]
        parts = []
        for g in range(NG):
            sl = slice(GN_GROUP * g, GN_GROUP * (g + 1))
            m = jnp.mean(gsc[:, sl] * yn[:, sl], axis=1, keepdims=True)
            parts.append(rs[g] * (gsc[:, sl] - yn[:, sl] * m))
        dy2 = jnp.concatenate(parts, axis=1)
        dy = dy2 * gate
        dy_ref[...] = dy
        du_out[...] = (dy2 * y * (sz * (1.0 + z * (1.0 - sz)))).astype(du_out.dtype)
        dnw = jnp.broadcast_to(jnp.sum(ds * yn, axis=0, keepdims=True), (8, DI))
        drow = jnp.broadcast_to(jnp.sum(dy * xs, axis=0, keepdims=True), (8, DI))
        dds = _dot01(drow, _sum_mat(0))

        @pl.when(i == 0)
        def _():
            dnw_ref[...] = dnw
            dds_ref[...] = dds

        @pl.when(i > 0)
        def _():
            dnw_ref[...] += dnw
            dds_ref[...] += dds

    blk = pl.BlockSpec((tm, DI), lambda i: (i, 0))
    row = pl.BlockSpec((1, DI), lambda i: (0, 0))
    return pl.pallas_call(
        body,
        out_shape=(jax.ShapeDtypeStruct((t, DI), F32), jax.ShapeDtypeStruct(du.shape, du.dtype),
                   jax.ShapeDtypeStruct((8, DI), F32), jax.ShapeDtypeStruct((8, 128), F32)),
        grid=(t // tm,),
        in_specs=[blk, blk, blk, blk, pl.BlockSpec((tm, DI), lambda i: (i, OZ // DI)), row, row,
                  pl.BlockSpec(memory_space=pl.ANY)],
        out_specs=(blk, pl.BlockSpec((tm, DI), lambda i: (i, OZ // DI)),
                   pl.BlockSpec((8, DI), lambda i: (0, 0)), pl.BlockSpec((8, 128), lambda i: (0, 0))),
        input_output_aliases={7: 1},
        name="gatenorm_bwd", compiler_params=_params(("arbitrary",)))(ds_out, y_f, y_b, xbc, u, dsk_row, nw_row, du)


AT_B = 128
AT_W = AT_B + 2 * ATT_HALF
AT_L = 2 * AH
SCALE = 1.0 / math.sqrt(AH)


def _slope(g, hh):
    return 2.0 ** (-8.0 * (4 * g + hh + 1) / 12.0)


def _qcol(g):
    return lambda p: OQ // AT_L + 2 * g + p


def _kcol(g):
    return lambda p: OKV // AT_L + 4 * g + 2 * p


def _vcol(g):
    return lambda p: OKV // AT_L + 4 * g + 2 * p + 1


def _pcol(p):
    return p


def _sub(d):
    return 2 if d == 1 else 1


def _win_specs(col, t, d):
    tb, hb = AT_B * d * _sub(d), ATT_HALF * d
    per = tb // hb
    nh = t // hb
    return [
        pl.BlockSpec((hb, AT_L), lambda p, i: (jnp.maximum(per * i - 1, 0), col(p))),
        pl.BlockSpec((tb, AT_L), lambda p, i: (i, col(p))),
        pl.BlockSpec((hb, AT_L), lambda p, i: (jnp.minimum(per * (i + 1), nh - 1), col(p))),
    ]


def _blk_spec(col, d):
    return pl.BlockSpec((AT_B * d * _sub(d), AT_L), lambda p, i: (i, col(p)))


def _rows(ref, r, s, d):
    return ref[pl.ds(r, AT_B, stride=d), :] if d > 1 else ref[AT_B * s:AT_B * (s + 1), :]


def _win(p_ref, c_ref, n_ref, r, s, d):
    if d > 1:
        return jnp.concatenate([p_ref[pl.ds(r, ATT_HALF, stride=d), :], c_ref[pl.ds(r, AT_B, stride=d), :],
                                n_ref[pl.ds(r, ATT_HALF, stride=d), :]], axis=0)
    if s == 0:
        return jnp.concatenate([p_ref[...], c_ref[0:AT_B + ATT_HALF, :]], axis=0)
    return jnp.concatenate([c_ref[ATT_HALF:2 * AT_B, :], n_ref[...]], axis=0)


def _put_rows(ref, r, s, d, val):
    if d > 1:
        ref[pl.ds(r, AT_B, stride=d), :] = val
    else:
        ref[AT_B * s:AT_B * (s + 1), :] = val


def _for_blocks(d, fn):
    if d == 1:
        for s in range(_sub(d)):
            fn(0, s)
    else:
        def step(r, c):
            fn(r, 0)
            return c
        lax.fori_loop(0, d, step, 0, unroll=2)


def _attn_geometry(i, ln, d):
    a = i * AT_B + _iota((AT_B, AT_W), 0)
    b = i * AT_B - ATT_HALF + _iota((AT_B, AT_W), 1)
    rel = jnp.abs(a - b)
    valid = (rel <= ATT_HALF) & (b >= 0) & (b < ln)
    return valid, (rel * d).astype(F32)


def _attn_fwd(u, g):
    t = u.shape[0]
    d = DILATIONS[g]
    ln = t // d

    def body(q_ref, kp, kc, kn, vp, vc, vn, o_ref, l_ref):
        p_id = pl.program_id(0)
        i = pl.program_id(1)
        lane = _iota((AT_B, AT_L), 1)

        def one(r, s):
            valid, dist = _attn_geometry(i * _sub(d) + s, ln, d)
            q = _rows(q_ref, r, s, d)
            kw = _win(kp, kc, kn, r, s, d).astype(BF16)
            vw = _win(vp, vc, vn, r, s, d).astype(BF16)
            o = jnp.zeros((AT_B, AT_L), F32)
            lse = jnp.zeros((AT_B, AT_L), F32)
            for hh in range(2):
                hm = (lane // AH) == hh
                slope = jnp.where(p_id == 0, _slope(g, hh), _slope(g, 2 + hh))
                qm = jnp.where(hm, q, 0.0).astype(BF16)
                sc = _dot_nt(qm, kw) * SCALE - slope * dist
                sc = jnp.where(valid, sc, NEG)
                m = jnp.max(sc, axis=1, keepdims=True)
                pr = jnp.exp(sc - m)
                den = jnp.sum(pr, axis=1, keepdims=True)
                oh = jnp.dot(pr.astype(BF16), vw, preferred_element_type=F32)
                o = jnp.where(hm, oh / den, o)
                lse = jnp.where(hm, m + jnp.log(den), lse)
            _put_rows(o_ref, r, s, d, o)
            _put_rows(l_ref, r, s, d, lse)

        _for_blocks(d, one)

    oshape = jax.ShapeDtypeStruct((t, 2 * AT_L), F32)
    ospec = _blk_spec(_pcol, d)
    return pl.pallas_call(
        body, out_shape=(oshape, oshape), grid=(2, t // (AT_B * d * _sub(d))),
        in_specs=[_blk_spec(_qcol(g), d)] + _win_specs(_kcol(g), t, d) + _win_specs(_vcol(g), t, d),
        out_specs=(ospec, ospec), name=f"attn_fwd_{g}", compiler_params=_params(("parallel", "parallel")))(
            u, u, u, u, u, u, u)


def _attn_dq(u, du, do, lse, e, g):
    t = u.shape[0]
    d = DILATIONS[g]
    ln = t // d

    def body(q_ref, kp, kc, kn, vp, vc, vn, do_ref, l_ref, e_ref, du_in, dq_ref, dq_scr):
        del du_in
        p_id = pl.program_id(0)
        i = pl.program_id(1)
        lane = _iota((AT_B, AT_L), 1)

        def one(r, s):
            valid, dist = _attn_geometry(i * _sub(d) + s, ln, d)
            q = _rows(q_ref, r, s, d)
            kw = _win(kp, kc, kn, r, s, d).astype(BF16)
            vw = _win(vp, vc, vn, r, s, d).astype(BF16)
            do_ = _rows(do_ref, r, s, d)
            lv = _rows(l_ref, r, s, d)
            ev = _rows(e_ref, r, s, d)
            dq = jnp.zeros((AT_B, AT_L), F32)
            for hh in range(2):
                hm = (lane // AH) == hh
                slope = jnp.where(p_id == 0, _slope(g, hh), _slope(g, 2 + hh))
                qm = jnp.where(hm, q, 0.0).astype(BF16)
                sc = _dot_nt(qm, kw) * SCALE - slope * dist
                lcol = jnp.broadcast_to(lv[:, AH * hh:AH * hh + 1], (AT_B, AT_W))
                ecol = jnp.broadcast_to(ev[:, AH * hh:AH * hh + 1], (AT_B, AT_W))
                pr = jnp.exp(jnp.where(valid, sc - lcol, NEG))
                dom = jnp.where(hm, do_, 0.0).astype(BF16)
                ds = pr * (_dot_nt(dom, vw) + ecol)
                dqh = jnp.dot(ds.astype(BF16), kw, preferred_element_type=F32) * SCALE
                dq = jnp.where(hm, dqh, dq)
            _put_rows(dq_scr, r, s, d, dq)

        _for_blocks(d, one)
        dq_ref[...] = dq_scr[...].astype(dq_ref.dtype)

    rspec = _blk_spec(_pcol, d)
    return pl.pallas_call(
        body, out_shape=jax.ShapeDtypeStruct(du.shape, du.dtype), grid=(2, t // (AT_B * d * _sub(d))),
        in_specs=[_blk_spec(_qcol(g), d)] + _win_specs(_kcol(g), t, d) + _win_specs(_vcol(g), t, d)
        + [rspec, rspec, rspec, pl.BlockSpec(memory_space=pl.ANY)],
        out_specs=_blk_spec(_qcol(g), d), input_output_aliases={10: 0},
        scratch_shapes=[pltpu.VMEM((AT_B * d * _sub(d), AT_L), F32)],
        name=f"attn_dq_{g}", compiler_params=_params(("parallel", "parallel")))(
            u, u, u, u, u, u, u, do, lse, e, du)


def _attn_dkv(u, du, do, lse, e, g):
    t = u.shape[0]
    d = DILATIONS[g]
    ln = t // d

    def body(k_ref, v_ref, qp, qc, qn, dp_, dc_, dn_, lp, lc, ln_, ep, ec, en, du_in, dkv_ref, dk_scr, dv_scr):
        del du_in
        p_id = pl.program_id(0)
        jb = pl.program_id(1)
        lane = _iota((AT_B, AT_L), 1)

        def one(r, s):
            valid, dist = _attn_geometry(jb * _sub(d) + s, ln, d)
            k = _rows(k_ref, r, s, d)
            v = _rows(v_ref, r, s, d)
            qw = _win(qp, qc, qn, r, s, d).astype(BF16)
            dow = _win(dp_, dc_, dn_, r, s, d).astype(BF16)
            lt = _win(lp, lc, ln_, r, s, d).T
            et = _win(ep, ec, en, r, s, d).T
            dk = jnp.zeros((AT_B, AT_L), F32)
            dv = jnp.zeros((AT_B, AT_L), F32)
            for hh in range(2):
                hm = (lane // AH) == hh
                slope = jnp.where(p_id == 0, _slope(g, hh), _slope(g, 2 + hh))
                km = jnp.where(hm, k, 0.0).astype(BF16)
                st = _dot_nt(km, qw) * SCALE - slope * dist
                pt = jnp.exp(jnp.where(valid, st - lt[AH * hh:AH * hh + 1, :], NEG))
                dvh = jnp.dot(pt.astype(BF16), dow, preferred_element_type=F32)
                vm = jnp.where(hm, v, 0.0).astype(BF16)
                dst = pt * (_dot_nt(vm, dow) + et[AH * hh:AH * hh + 1, :])
                dkh = jnp.dot(dst.astype(BF16), qw, preferred_element_type=F32) * SCALE
                dk = jnp.where(hm, dkh, dk)
                dv = jnp.where(hm, dvh, dv)
            _put_rows(dk_scr, r, s, d, dk)
            _put_rows(dv_scr, r, s, d, dv)

        _for_blocks(d, one)
        dkv_ref[:, 0:AT_L] = dk_scr[...].astype(dkv_ref.dtype)
        dkv_ref[:, AT_L:2 * AT_L] = dv_scr[...].astype(dkv_ref.dtype)

    return pl.pallas_call(
        body, out_shape=jax.ShapeDtypeStruct(du.shape, du.dtype), grid=(2, t // (AT_B * d * _sub(d))),
        in_specs=[_blk_spec(_kcol(g), d), _blk_spec(_vcol(g), d)]
        + _win_specs(_qcol(g), t, d) + _win_specs(_pcol, t, d) + _win_specs(_pcol, t, d) + _win_specs(_pcol, t, d)
        + [pl.BlockSpec(memory_space=pl.ANY)],
        out_specs=pl.BlockSpec((AT_B * d * _sub(d), 2 * AT_L), lambda p, i: (i, OKV // (2 * AT_L) + 2 * g + p)),
        input_output_aliases={14: 0},
        scratch_shapes=[pltpu.VMEM((AT_B * d * _sub(d), AT_L), F32), pltpu.VMEM((AT_B * d * _sub(d), AT_L), F32)],
        name=f"attn_dkv_{g}", compiler_params=_params(("parallel", "parallel")))(
            u, u, u, u, u, do, do, do, lse, lse, lse, e, e, e, du)


CMB_TM = 1024


def _combine_weights(l0, l1, l2):
    m = jnp.maximum(jnp.maximum(l0, l1), l2)
    e0, e1, e2 = jnp.exp(l0 - m), jnp.exp(l1 - m), jnp.exp(l2 - m)
    inv = 1.0 / (e0 + e1 + e2)
    return e0 * inv, e1 * inv, e2 * inv


def _combine_fwd(os_, ls_):
    t = os_[0].shape[0]
    tm = CMB_TM

    def body(o0, o1, o2, l0, l1, l2, a_ref):
        w0, w1, w2 = _combine_weights(l0[...], l1[...], l2[...])
        a_ref[...] = w0 * o0[...] + w1 * o1[...] + w2 * o2[...]

    blk = pl.BlockSpec((tm, 2 * AT_L), lambda i: (i, 0))
    return pl.pallas_call(
        body, out_shape=jax.ShapeDtypeStruct((t, 2 * AT_L), F32), grid=(t // tm,), in_specs=[blk] * 6, out_specs=blk,
        name="combine_fwd", compiler_params=_params(("parallel",)))(*os_, *ls_)


def _combine_bwd(datt, os_, ls_):
    t = datt.shape[0]
    tm = CMB_TM

    def body(da_ref, o0, o1, o2, l0, l1, l2, d0, d1, d2, e0, e1, e2):
        w = _combine_weights(l0[...], l1[...], l2[...])
        da = da_ref[...]
        att = w[0] * o0[...] + w[1] * o1[...] + w[2] * o2[...]
        r = _iota((2 * AT_L, 2 * AT_L), 0) // AH
        c = _iota((2 * AT_L, 2 * AT_L), 1) // AH
        hs = _dot01(da * att, (r == c).astype(BF16))
        for wg, dref, eref in zip(w, (d0, d1, d2), (e0, e1, e2)):
            dref[...] = wg * da
            eref[...] = -wg * hs

    blk = pl.BlockSpec((tm, 2 * AT_L), lambda i: (i, 0))
    shp = jax.ShapeDtypeStruct((t, 2 * AT_L), F32)
    outs = pl.pallas_call(
        body, out_shape=(shp,) * 6, grid=(t // tm,), in_specs=[blk] * 7, out_specs=(blk,) * 6,
        name="combine_bwd", compiler_params=_params(("parallel",)))(datt, *os_, *ls_)
    return outs[0:3], outs[3:6]


ROW_TM = 512


def _mix_fwd(y_ssd, y_att, u, bg_row):
    t = y_ssd.shape[0]
    tm = ROW_TM

    def body(ys_ref, ya_ref, g0_ref, g1_ref, b0_ref, b1_ref, o_ref):
        g0 = _sigmoid(g0_ref[...] + b0_ref[...])
        g1 = _sigmoid(g1_ref[...] + b1_ref[...])
        o_ref[...] = (g0 * ys_ref[...] + g1 * ya_ref[...]).astype(BF16)

    blk = pl.BlockSpec((tm, D), lambda i: (i, 0))
    return pl.pallas_call(
        body, out_shape=jax.ShapeDtypeStruct((t, D), BF16), grid=(t // tm,),
        in_specs=[blk, blk, pl.BlockSpec((tm, D), lambda i: (i, OGATE // D)), pl.BlockSpec((tm, D), lambda i: (i, OGATE // D + 1)),
                  pl.BlockSpec((1, D), lambda i: (0, 0)), pl.BlockSpec((1, D), lambda i: (0, 1))],
        out_specs=blk, name="mix_fwd", compiler_params=_params(("parallel",)))(y_ssd, y_att, u, u, bg_row, bg_row)


def _mix_bwd(dmixin, y_ssd, y_att, u, bg_row):
    t = y_ssd.shape[0]
    tm = ROW_TM

    def body(dm_ref, ys_ref, ya_ref, g0_ref, g1_ref, b0_ref, b1_ref, dys_ref, dya_ref, du_ref, db_ref):
        i = pl.program_id(0)
        g0 = _sigmoid(g0_ref[...] + b0_ref[...])
        g1 = _sigmoid(g1_ref[...] + b1_ref[...])
        dm = dm_ref[...]
        dys_ref[...] = (dm * g0).astype(BF16)
        dya_ref[...] = (dm * g1).astype(BF16)
        dl0 = dm * ys_ref[...] * g0 * (1.0 - g0)
        dl1 = dm * ya_ref[...] * g1 * (1.0 - g1)
        du_ref[:, 0:D] = dl0.astype(BF16)
        du_ref[:, D:2 * D] = dl1.astype(BF16)
        part = jnp.concatenate([jnp.broadcast_to(jnp.sum(dl0, axis=0, keepdims=True), (8, D)),
                                jnp.broadcast_to(jnp.sum(dl1, axis=0, keepdims=True), (8, D))], axis=1)

        @pl.when(i == 0)
        def _():
            db_ref[...] = part

        @pl.when(i > 0)
        def _():
            db_ref[...] += part

    blk = pl.BlockSpec((tm, D), lambda i: (i, 0))
    return pl.pallas_call(
        body,
        out_shape=(jax.ShapeDtypeStruct((t, D), BF16), jax.ShapeDtypeStruct((t, D), BF16),
                   jax.ShapeDtypeStruct((t, UW), BF16), jax.ShapeDtypeStruct((8, 2 * D), F32)),
        grid=(t // tm,),
        in_specs=[blk, blk, blk, pl.BlockSpec((tm, D), lambda i: (i, OGATE // D)), pl.BlockSpec((tm, D), lambda i: (i, OGATE // D + 1)),
                  pl.BlockSpec((1, D), lambda i: (0, 0)), pl.BlockSpec((1, D), lambda i: (0, 1))],
        out_specs=(blk, blk, pl.BlockSpec((tm, 2 * D), lambda i: (i, OGATE // (2 * D))),
                   pl.BlockSpec((8, 2 * D), lambda i: (0, 0))),
        name="mix_bwd", compiler_params=_params(("arbitrary",)))(dmixin, y_ssd, y_att, u, u, bg_row, bg_row)


def _ln(x, g, b):
    mu = jnp.mean(x, axis=1, keepdims=True)
    xc = x - mu
    var = jnp.mean(xc * xc, axis=1, keepdims=True)
    rstd = lax.rsqrt(var + NORM_EPS)
    xhat = xc * rstd
    return xhat * g + b, xhat, rstd


def _ln_back(dh, xhat, rstd, g):
    dxh = dh * g
    m1 = jnp.mean(dxh, axis=1, keepdims=True)
    m2 = jnp.mean(dxh * xhat, axis=1, keepdims=True)
    return rstd * (dxh - m1 - xhat * m2)


def _ln1_fwd(x, mix, g_row, b_row):
    t = x.shape[0]
    tm = ROW_TM

    def body(x_ref, m_ref, g_ref, b_ref, pre_ref, h_ref):
        pre = ALPHA * x_ref[...] + m_ref[...]
        pre_ref[...] = pre
        h, _, _ = _ln(pre, g_ref[...], b_ref[...])
        h_ref[...] = h.astype(BF16)

    blk = pl.BlockSpec((tm, D), lambda i: (i, 0))
    row = pl.BlockSpec((1, D), lambda i: (0, 0))
    return pl.pallas_call(
        body, out_shape=(jax.ShapeDtypeStruct((t, D), F32), jax.ShapeDtypeStruct((t, D), BF16)), grid=(t // tm,),
        in_specs=[blk, blk, row, row], out_specs=(blk, blk),
        name="ln1_fwd", compiler_params=_params(("parallel",)))(x, mix, g_row, b_row)


def _ln1_bwd(dh, pre, g_row, b_row):
    t = dh.shape[0]
    tm = ROW_TM

    def body(dh_ref, pre_ref, g_ref, b_ref, dpre_ref, acc_ref):
        i = pl.program_id(0)
        dh_ = dh_ref[...]
        _, xhat, rstd = _ln(pre_ref[...], g_ref[...], b_ref[...])
        dpre_ref[...] = _ln_back(dh_, xhat, rstd, g_ref[...])
        part = jnp.concatenate([jnp.sum(dh_ * xhat, axis=0, keepdims=True), jnp.sum(dh_, axis=0, keepdims=True),
                                jnp.zeros((6, D), F32)], axis=0)

        @pl.when(i == 0)
        def _():
            acc_ref[...] = part

        @pl.when(i > 0)
        def _():
            acc_ref[...] += part

    blk = pl.BlockSpec((tm, D), lambda i: (i, 0))
    row = pl.BlockSpec((1, D), lambda i: (0, 0))
    return pl.pallas_call(
        body, out_shape=(jax.ShapeDtypeStruct((t, D), F32), jax.ShapeDtypeStruct((8, D), F32)), grid=(t // tm,),
        in_specs=[blk, blk, row, row], out_specs=(blk, pl.BlockSpec((8, D), lambda i: (0, 0))),
        name="ln1_bwd", compiler_params=_params(("arbitrary",)))(dh, pre, g_row, b_row)


def _ln2_loss(pre1, f, tgt, g1_row, b1_row, g2_row, b2_row):
    t = pre1.shape[0]
    tm = ROW_TM

    def body(p1_ref, f_ref, t_ref, g1_ref, b1_ref, g2_ref, b2_ref, dpre_ref, acc_ref):
        i = pl.program_id(0)
        h1, _, _ = _ln(p1_ref[...], g1_ref[...], b1_ref[...])
        pre2 = ALPHA * h1 + f_ref[...]
        h2, xhat, rstd = _ln(pre2, g2_ref[...], b2_ref[...])
        err = h2 - t_ref[...]
        dh = err * (1.0 / D)
        dpre_ref[...] = _ln_back(dh, xhat, rstd, g2_ref[...])
        loss = jnp.sum(jnp.sum(err * err, axis=1, keepdims=True), axis=0, keepdims=True) * (0.5 / D)
        part = jnp.concatenate([jnp.sum(dh * xhat, axis=0, keepdims=True), jnp.sum(dh, axis=0, keepdims=True),
                                jnp.broadcast_to(loss, (1, D)), jnp.zeros((5, D), F32)], axis=0)

        @pl.when(i == 0)
        def _():
            acc_ref[...] = part

        @pl.when(i > 0)
        def _():
            acc_ref[...] += part

    blk = pl.BlockSpec((tm, D), lambda i: (i, 0))
    row = pl.BlockSpec((1, D), lambda i: (0, 0))
    return pl.pallas_call(
        body, out_shape=(jax.ShapeDtypeStruct((t, D), F32), jax.ShapeDtypeStruct((8, D), F32)), grid=(t // tm,),
        in_specs=[blk, blk, blk, row, row, row, row], out_specs=(blk, pl.BlockSpec((8, D), lambda i: (0, 0))),
        name="ln2_loss", compiler_params=_params(("arbitrary",)))(pre1, f, tgt, g1_row, b1_row, g2_row, b2_row)


def _mlp_up(h1, w_up):
    t = h1.shape[0]
    tm, tn = ROW_TM, D

    def body(a_ref, b_ref, up_ref, act_ref):
        up = jnp.dot(a_ref[...], b_ref[...], preferred_element_type=F32)
        up_ref[...] = up
        r = jnp.maximum(up, 0.0)
        act_ref[...] = (r * r).astype(BF16)

    blk = pl.BlockSpec((tm, tn), lambda j, i: (i, j))
    return pl.pallas_call(
        body, out_shape=(jax.ShapeDtypeStruct((t, DFF), F32), jax.ShapeDtypeStruct((t, DFF), BF16)),
        grid=(DFF // tn, t // tm),
        in_specs=[pl.BlockSpec((tm, D), lambda j, i: (i, 0)), pl.BlockSpec((None, D, tn), lambda j, i: (j, 0, 0))],
        out_specs=(blk, blk), name="mlp_up", compiler_params=_params(("parallel", "parallel")))(h1, w_up)


def _d_up(dpre2, w_down, up):
    t = up.shape[0]
    tm, tk = ROW_TM, D

    def body(a_ref, b_ref, u_ref, o_ref):
        dact = _dot_nt(a_ref[...].astype(BF16), b_ref[...])
        o_ref[...] = (dact * 2.0 * jnp.maximum(u_ref[...], 0.0)).astype(BF16)

    blk = pl.BlockSpec((tm, tk), lambda j, i: (i, j))
    return pl.pallas_call(
        body, out_shape=jax.ShapeDtypeStruct((t, DFF), BF16), grid=(DFF // tk, t // tm),
        in_specs=[pl.BlockSpec((tm, D), lambda j, i: (i, 0)), pl.BlockSpec((tk, D), lambda j, i: (j, 0)), blk],
        out_specs=blk, name="d_up", compiler_params=_params(("parallel", "parallel")))(dpre2, w_down, up)


def _dt_bwd(du, ddt_f, ddt_b):
    t = ddt_f.shape[0]
    tm = 1024

    def body(f_ref, b_ref, du_in, o_ref):
        del du_in
        o_ref[:, 0:128] = (f_ref[...] + b_ref[...]).astype(o_ref.dtype)
        o_ref[:, 128:256] = jnp.zeros((tm, 128), o_ref.dtype)

    blk = pl.BlockSpec((tm, 128), lambda i: (i, 0))
    return pl.pallas_call(
        body, out_shape=jax.ShapeDtypeStruct(du.shape, du.dtype), grid=(t // tm,),
        in_specs=[blk, blk, pl.BlockSpec(memory_space=pl.ANY)],
        out_specs=pl.BlockSpec((tm, 256), lambda i: (i, ODT // 256)), input_output_aliases={2: 0},
        name="dt_bwd", compiler_params=_params(("parallel",)))(ddt_f, ddt_b, du)


def _adamw(w, g, m, v, name):
    r, c = w.shape
    tr = r
    for cand in (256, 128, 64, 32, 16, 8):
        if r % cand == 0 and cand * c * 4 <= 2 ** 21:
            tr = cand
            break
    bc1 = 1.0 / (1.0 - ADAM_B1 ** ADAM_STEP)
    bc2 = 1.0 / (1.0 - ADAM_B2 ** ADAM_STEP)

    def body(w_ref, g_ref, m_ref, v_ref, d_ref, nm_ref, nv_ref):
        gg = g_ref[...]
        nm = ADAM_B1 * m_ref[...] + (1.0 - ADAM_B1) * gg
        nv = ADAM_B2 * v_ref[...] + (1.0 - ADAM_B2) * (gg * gg)
        nm_ref[...] = nm
        nv_ref[...] = nv
        d_ref[...] = -ADAM_LR * ((nm * bc1) / (jnp.sqrt(nv * bc2) + ADAM_EPS) + ADAM_WD * w_ref[...])

    blk = pl.BlockSpec((tr, c), lambda i: (i, 0))
    shp = jax.ShapeDtypeStruct((r, c), F32)
    return pl.pallas_call(body, out_shape=(shp, shp, shp), grid=(r // tr,), in_specs=[blk] * 4, out_specs=(blk,) * 3,
                          name=name, compiler_params=_params(("parallel",)))(w, g, m, v)


def _perm_cols(w):
    z, xbc, dt = w[:, 0:2048], w[:, 2048:5120], w[:, 5120:5184]
    q, k, v, gate = w[:, 5184:5952], w[:, 5952:6720], w[:, 6720:7488], w[:, 7488:9536]
    kv = []
    for g in range(3):
        for p in range(2):
            lo = 256 * g + 128 * p
            kv += [k[:, lo:lo + 128], v[:, lo:lo + 128]]
    pad = jnp.zeros((w.shape[0], UW - IN_COLS), w.dtype)
    return jnp.concatenate([z, gate, xbc] + kv + [q, dt, pad], axis=1)


def _unperm_cols(wp):
    z, gate, xbc = wp[:, OZ:OZ + 2048], wp[:, OGATE:OGATE + 2048], wp[:, OXBC:OXBC + CONVD]
    q, dt = wp[:, OQ:OQ + 768], wp[:, ODT:ODT + 64]
    ks, vs = [], []
    for g in range(3):
        for p in range(2):
            lo = OKV + 128 * (4 * g + 2 * p)
            ks.append(wp[:, lo:lo + 128])
            vs.append(wp[:, lo + 128:lo + 256])
    return jnp.concatenate([z, xbc, dt, q] + ks + vs + [gate], axis=1)


def _lanes128(*vecs):
    v = jnp.concatenate([a.reshape(-1) for a in vecs])
    return jnp.pad(v, (0, 128 - v.shape[0])).reshape(1, 128)


def _local_grads(x, tgt, wts, sm):
    row = lambda a: a.reshape(1, -1)
    bg_row, cb_row = row(sm["b_gate"]), row(sm["conv_b"])
    par = jnp.concatenate([_lanes128(sm["dt_bias_f"], sm["dt_bias_b"]), _lanes128(sm["a_log_f"], sm["a_log_b"]),
                           jnp.zeros((6, 128), F32)], axis=0)
    dsk_row = row(jnp.repeat(sm["d_skip"], HP))
    nw_row = row(sm["ssd_norm_w"])
    g1, b1, g2, b2 = row(sm["ln1_g"]), row(sm["ln1_b"]), row(sm["ln2_g"]), row(sm["ln2_b"])

    xb = x.astype(BF16)
    u = _mm_nn(xb, wts["w_in_p"], tm=512, tn=2432, name="in_proj")
    xbc = _conv_fwd(u, sm["conv_w"], cb_row)
    y_f, st_f = _ssd_fwd(xbc, u, par, rev=False)
    y_b, st_b = _ssd_fwd(xbc, u, par, rev=True)
    s_out = _gatenorm_fwd(y_f, y_b, xbc, u, dsk_row, nw_row)
    y_ssd = _mm_nn(s_out, wts["w_proj_ssd"], tm=512, tn=1024, name="proj_ssd")
    att_o, att_l = [], []
    for g in range(3):
        o, l = _attn_fwd(u, g)
        att_o.append(o)
        att_l.append(l)
    att = _combine_fwd(att_o, att_l)
    y_att = _mm_nn(att, wts["w_proj_attn"], tm=512, tn=256, name="proj_attn")
    mixin = _mix_fwd(y_ssd, y_att, u, bg_row)
    mix = _mm_nn(mixin, wts["w_out"], tm=512, tn=1024, name="out_proj")
    pre1, h1 = _ln1_fwd(x, mix, g1, b1)
    up, act = _mlp_up(h1, wts["w_up"])
    f = _mm_nn(act, wts["w_down"], tm=512, tn=1024, name="mlp_down")
    dpre2, acc2 = _ln2_loss(pre1, f, tgt, g1, b1, g2, b2)

    dw_down = _mm_tn(act, dpre2, tka=1024, tn=1024, tt=512, name="dw_down")
    dup = _d_up(dpre2, wts["w_down"], up)
    dw_up = _mm_tn(h1, dup, tka=1024, tn=1024, tt=512, name="dw_up", out_shards=4)
    dh1 = _mm_nt(dup, wts["w_up"], tm=512, tk=1024, tc=1024, name="d_h1", add=dpre2, add_scale=ALPHA)
    dpre1, acc1 = _ln1_bwd(dh1, pre1, g1, b1)
    dmixin = _mm_nt(dpre1, wts["w_out"], tm=512, tk=1024, tc=1024, name="d_mixin")
    dw_out = _mm_tn(mixin, dpre1, tka=1024, tn=1024, tt=512, name="dw_out")
    dy_ssd, dy_att, du, dbg = _mix_bwd(dmixin, y_ssd, y_att, u, bg_row)
    dw_proj_ssd = _mm_tn(s_out, dy_ssd, tka=1024, tn=1024, tt=512, name="dw_proj_ssd")
    ds_out = _mm_nt(dy_ssd, wts["w_proj_ssd"], tm=512, tk=1024, tc=1024, name="d_s_out")
    dw_proj_attn = _mm_tn(att, dy_att, tka=256, tn=256, tt=512, name="dw_proj_attn", out_shards=4)
    datt = _mm_nt(dy_att, wts["w_proj_attn"], tm=512, tk=256, tc=256, name="d_att")
    do_g, e_g = _combine_bwd(datt, att_o, att_l)
    for g in range(3):
        du = _attn_dq(u, du, do_g[g], att_l[g], e_g[g], g)
        du = _attn_dkv(u, du, do_g[g], att_l[g], e_g[g], g)
    dy, du, dnw, dds = _gatenorm_bwd(ds_out, y_f, y_b, xbc, u, du, dsk_row, nw_row)
    dxs_f, dbc_f, ddt_f, sacc_f = _ssd_bwd(xbc, u, par, dy, st_f, rev=False)
    dxs_b, dbc_b, ddt_b, sacc_b = _ssd_bwd(xbc, u, par, dy, st_b, rev=True)
    dpre_c, dcw, dcb = _conv_dpre(u, dxs_f, dxs_b, dy, dbc_f, dbc_b, dsk_row, sm["conv_w"], cb_row)
    du = _conv_dx(du, dpre_c, sm["conv_w"])
    du = _dt_bwd(du, ddt_f, ddt_b)
    dw_in_p = _mm_tn(xb, du, tka=1024, tn=2432, tt=512, name="dw_in")
    dx = _mm_nt(du, wts["w_in_p"], tm=512, tk=1024, tc=2432, name="d_x", add=dpre1, add_scale=ALPHA)

    sacc = sacc_f + sacc_b
    small = {
        "b_gate": dbg[0], "conv_w": dcw[0:KCONV], "conv_b": dcb[0],
        "dt_bias_f": sacc[0, 0:32], "dt_bias_b": sacc[0, 32:64], "a_log_f": sacc[1, 0:32], "a_log_b": sacc[1, 32:64],
        "d_skip": dds[0, 0:32], "ssd_norm_w": dnw[0],
        "ln1_g": acc1[0], "ln1_b": acc1[1], "ln2_g": acc2[0], "ln2_b": acc2[1], "loss": acc2[2, 0:1],
    }
    dw_in = _unperm_cols(dw_in_p)
    big = {
        "w_in": dw_in.reshape(D, 4, IN_COLS // 4).transpose(1, 0, 2),
        "w_proj_ssd": dw_proj_ssd.reshape(4, DI // 4, D),
        "w_proj_attn": dw_proj_attn,
        "w_out": dw_out.reshape(4, D // 4, D),
        "w_up": dw_up,
        "w_down": dw_down.reshape(4, DFF // 4, D),
    }
    return dx, big, small


HBM_SPEC = pl.BlockSpec(memory_space=pl.ANY)


def _place():
    x, y, c = lax.axis_index("x"), lax.axis_index("y"), lax.axis_index("c")
    chips = [(1 - x, y), (x, 1 - y), (1 - x, 1 - y)]
    return x, y, c, chips


def _allgather_weights(shards):
    n = len(shards)

    def body(*refs):
        ins, outs = refs[:n], refs[n:2 * n]
        send_sems, recv_sems = refs[2 * n:]
        x, y, c, _ = _place()
        q, q_x, q_y, q_d = 2 * x + y, 2 * (1 - x) + y, 2 * x + 1 - y, 2 * (1 - x) + 1 - y
        x_nbr, y_nbr, sibling = (1 - x, y, c), (x, 1 - y, c), (x, y, 1 - c)

        def copy(w, k, src, dst, to):
            return pltpu.make_async_remote_copy(src_ref=src, dst_ref=dst, send_sem=send_sems.at[w, k],
                                                recv_sem=recv_sems.at[w, k], device_id=to, device_id_type=MESH)

        def rows(w, core, part):
            rh = ins[w].shape[0] // 2
            if part is None:
                return pl.ds(core * rh, rh)
            return pl.ds(core * rh + part * (rh // 2), rh // 2)

        def same(w, k, slot, core, part, to):
            blk = outs[w].at[slot, rows(w, core, part), :]
            return copy(w, k, blk, blk, to)

        started = []
        for w in range(n):
            cp = copy(w, 8, ins[w], outs[w].at[q], sibling)
            cp.start()
            started.append(cp)
            mine = rows(w, c, None)
            for k, to in ((0, x_nbr), (1, y_nbr)):
                cp = copy(w, k, ins[w].at[mine, :], outs[w].at[q, mine, :], to)
                cp.start()
                started.append(cp)
        for w in range(n):
            same(w, 0, q_x, c, None, x_nbr).wait_recv()
            for cp in (same(w, 2, q_x, c, 0, y_nbr), same(w, 4, q_x, c, None, sibling)):
                cp.start()
                started.append(cp)
            same(w, 1, q_y, c, None, y_nbr).wait_recv()
            for cp in (same(w, 3, q_y, c, 1, x_nbr), same(w, 5, q_y, c, None, sibling)):
                cp.start()
                started.append(cp)
        for w in range(n):
            same(w, 2, q_d, c, 0, y_nbr).wait_recv()
            cp = same(w, 6, q_d, c, 0, sibling)
            cp.start()
            started.append(cp)
            same(w, 3, q_d, c, 1, x_nbr).wait_recv()
            cp = same(w, 7, q_d, c, 1, sibling)
            cp.start()
            started.append(cp)
        for w in range(n):
            same(w, 4, q_x, 1 - c, None, sibling).wait_recv()
            same(w, 5, q_y, 1 - c, None, sibling).wait_recv()
            same(w, 6, q_d, 1 - c, 0, sibling).wait_recv()
            same(w, 7, q_d, 1 - c, 1, sibling).wait_recv()
            copy(w, 8, ins[w], outs[w].at[q], sibling).wait_recv()
        for cp in started:
            cp.wait_send()

    return pl.pallas_call(
        body, out_shape=[jax.ShapeDtypeStruct((4,) + s.shape, s.dtype) for s in shards],
        in_specs=[HBM_SPEC] * n, out_specs=[HBM_SPEC] * n,
        scratch_shapes=[pltpu.SemaphoreType.DMA((n, 9)), pltpu.SemaphoreType.DMA((n, 9))],
        name="allgather_weights")(*shards)


def _swap_halves(grads):
    n = len(grads)

    def body(*refs):
        ins, outs = refs[:n], refs[n:2 * n]
        send_sems, recv_sems = refs[2 * n:]
        x, y, c, _ = _place()
        copies = []
        for w in range(n):
            rh = ins[w].shape[1] // 2
            for p in range(4):
                cp = pltpu.make_async_remote_copy(
                    src_ref=ins[w].at[p, pl.ds((1 - c) * rh, rh), :], dst_ref=outs[w].at[p],
                    send_sem=send_sems.at[w, p], recv_sem=recv_sems.at[w, p],
                    device_id=(x, y, 1 - c), device_id_type=MESH)
                cp.start()
                copies.append(cp)
        for cp in copies:
            cp.wait()

    return pl.pallas_call(
        body, out_shape=[jax.ShapeDtypeStruct((4, g.shape[1] // 2, g.shape[2]), F32) for g in grads],
        in_specs=[HBM_SPEC] * n, out_specs=[HBM_SPEC] * n,
        scratch_shapes=[pltpu.SemaphoreType.DMA((n, 4)), pltpu.SemaphoreType.DMA((n, 4))],
        name="rs_swap_halves")(*grads)


def _rs_step1(parts):
    n = len(parts)

    def body(*refs):
        ins, out_a, out_b = refs[:n], refs[n:2 * n], refs[2 * n:3 * n]
        send_sems, recv_sems = refs[3 * n:]
        x, y, c, _ = _place()
        copies = []
        for w in range(n):
            rq = ins[w].shape[1] // 2
            for i in range(2):
                copies.append(pltpu.make_async_remote_copy(
                    src_ref=ins[w].at[2 * (1 - x) + i, pl.ds(0, rq), :], dst_ref=out_a[w].at[i],
                    send_sem=send_sems.at[w, i], recv_sem=recv_sems.at[w, i],
                    device_id=(1 - x, y, c), device_id_type=MESH))
                copies.append(pltpu.make_async_remote_copy(
                    src_ref=ins[w].at[2 * i + 1 - y, pl.ds(rq, rq), :], dst_ref=out_b[w].at[i],
                    send_sem=send_sems.at[w, 2 + i], recv_sem=recv_sems.at[w, 2 + i],
                    device_id=(x, 1 - y, c), device_id_type=MESH))
        for cp in copies:
            cp.start()
        for cp in copies:
            cp.wait()

    quarter = lambda p: jax.ShapeDtypeStruct((2, p.shape[1] // 2, p.shape[2]), p.dtype)
    outs = pl.pallas_call(
        body, out_shape=[quarter(p) for p in parts] * 2,
        in_specs=[HBM_SPEC] * n, out_specs=[HBM_SPEC] * (2 * n),
        scratch_shapes=[pltpu.SemaphoreType.DMA((n, 4)), pltpu.SemaphoreType.DMA((n, 4))],
        name="rs_step1")(*parts)
    return outs[:n], outs[n:]


def _rs_step2(tas, tbs):
    n = len(tas)

    def body(*refs):
        in_a, in_b, out_a, out_b = refs[:n], refs[n:2 * n], refs[2 * n:3 * n], refs[3 * n:4 * n]
        send_sems, recv_sems = refs[4 * n:]
        x, y, c, _ = _place()
        copies = []
        for w in range(n):
            copies.append(pltpu.make_async_remote_copy(
                src_ref=in_a[w].at[1 - y], dst_ref=out_a[w], send_sem=send_sems.at[w, 0], recv_sem=recv_sems.at[w, 0],
                device_id=(x, 1 - y, c), device_id_type=MESH))
            copies.append(pltpu.make_async_remote_copy(
                src_ref=in_b[w].at[1 - x], dst_ref=out_b[w], send_sem=send_sems.at[w, 1], recv_sem=recv_sems.at[w, 1],
                device_id=(1 - x, y, c), device_id_type=MESH))
        for cp in copies:
            cp.start()
        for cp in copies:
            cp.wait()

    one = lambda p: jax.ShapeDtypeStruct(p.shape[1:], p.dtype)
    outs = pl.pallas_call(
        body, out_shape=[one(p) for p in tas] + [one(p) for p in tbs],
        in_specs=[HBM_SPEC] * (2 * n), out_specs=[HBM_SPEC] * (2 * n),
        scratch_shapes=[pltpu.SemaphoreType.DMA((n, 2)), pltpu.SemaphoreType.DMA((n, 2))],
        name="rs_step2")(*tas, *tbs)
    return outs[:n], outs[n:]


def _join_halves(pieces):
    n = len(pieces)

    def body(*refs):
        outs = refs[n:2 * n]
        send_sems, recv_sems = refs[2 * n:]
        x, y, c, _ = _place()

        def copy(w, slot):
            return pltpu.make_async_remote_copy(
                src_ref=outs[w].at[slot], dst_ref=outs[w].at[slot], send_sem=send_sems.at[w], recv_sem=recv_sems.at[w],
                device_id=(x, y, 1 - c), device_id_type=MESH)

        for w in range(n):
            copy(w, c).start()
        for w in range(n):
            copy(w, 1 - c).wait_recv()
            copy(w, c).wait_send()

    return pl.pallas_call(
        body, out_shape=[jax.ShapeDtypeStruct(p.shape, F32) for p in pieces],
        in_specs=[HBM_SPEC] * n, out_specs=[HBM_SPEC] * n, input_output_aliases={w: w for w in range(n)},
        scratch_shapes=[pltpu.SemaphoreType.DMA((n,)), pltpu.SemaphoreType.DMA((n,))],
        name="rs_join_halves")(*pieces)


def _add_tile_rows(rh, c):
    for cand in (512, 256, 128, 64, 32, 16, 8):
        if rh % cand == 0 and cand * c * 4 <= 2 ** 21:
            return cand
    return rh


def _add_half(grad, recv, c_idx, name):
    _, r, cc = grad.shape
    rh = r // 2
    tr = _add_tile_rows(rh, cc)
    nb = rh // tr

    def body(c_ref, g_ref, r_ref, o_ref, ob_ref):
        del c_ref
        s = g_ref[...] + r_ref[...]
        o_ref[...] = s
        ob_ref[...] = s.astype(BF16)

    blk = pl.BlockSpec((None, tr, cc), lambda p, i, c_ref: (p, i, 0))
    grid_spec = pltpu.PrefetchScalarGridSpec(
        num_scalar_prefetch=1, grid=(4, nb),
        in_specs=[pl.BlockSpec((None, tr, cc), lambda p, i, c_ref: (p, c_ref[0] * nb + i, 0)), blk],
        out_specs=(blk, blk))
    return pl.pallas_call(
        body, out_shape=(jax.ShapeDtypeStruct((4, rh, cc), F32), jax.ShapeDtypeStruct((4, rh, cc), BF16)),
        grid_spec=grid_spec, name=name, compiler_params=_params(("parallel", "parallel")))(c_idx, grad, recv)


def _rs_add1(part, recv_a, recv_b, xy_idx, name):
    _, rh, cc = part.shape
    rq = rh // 2
    tr = _add_tile_rows(rq, cc)
    nb = rq // tr

    def body(xy_ref, pa_ref, pb_ref, ra_ref, rb_ref, ta_ref, tb_ref, tab_ref, tbb_ref):
        del xy_ref
        ta = pa_ref[...] + ra_ref[...].astype(F32)
        tb = pb_ref[...] + rb_ref[...].astype(F32)
        ta_ref[...] = ta
        tb_ref[...] = tb
        tab_ref[...] = ta.astype(BF16)
        tbb_ref[...] = tb.astype(BF16)

    blk = pl.BlockSpec((None, tr, cc), lambda i, j, xy: (i, j, 0))
    grid_spec = pltpu.PrefetchScalarGridSpec(
        num_scalar_prefetch=1, grid=(2, nb),
        in_specs=[pl.BlockSpec((None, tr, cc), lambda i, j, xy: (2 * xy[0] + i, j, 0)),
                  pl.BlockSpec((None, tr, cc), lambda i, j, xy: (2 * i + xy[1], nb + j, 0)), blk, blk],
        out_specs=(blk, blk, blk, blk))
    f32s, b16s = jax.ShapeDtypeStruct((2, rq, cc), F32), jax.ShapeDtypeStruct((2, rq, cc), BF16)
    return pl.pallas_call(body, out_shape=(f32s, f32s, b16s, b16s), grid_spec=grid_spec, name=name,
                          compiler_params=_params(("parallel", "parallel")))(xy_idx, part, part, recv_a, recv_b)


def _rs_add2(ta, tb, recv_a, recv_b, xy_idx, name):
    _, rq, cc = ta.shape
    tr = _add_tile_rows(rq, cc)
    nb = rq // tr

    def body(xy_ref, ta_ref, tb_ref, ra_ref, rb_ref, o_ref):
        del xy_ref
        s = pl.program_id(0)
        fa = ta_ref[...] + ra_ref[...].astype(F32)
        fb = tb_ref[...] + rb_ref[...].astype(F32)
        o_ref[...] = jnp.where(s == 0, fa, fb)

    rblk = pl.BlockSpec((tr, cc), lambda s, j, xy: (j, 0))
    grid_spec = pltpu.PrefetchScalarGridSpec(
        num_scalar_prefetch=1, grid=(2, nb),
        in_specs=[pl.BlockSpec((None, tr, cc), lambda s, j, xy: (xy[1], j, 0)),
                  pl.BlockSpec((None, tr, cc), lambda s, j, xy: (xy[0], j, 0)), rblk, rblk],
        out_specs=pl.BlockSpec((None, tr, cc), lambda s, j, xy: (xy[2], s * nb + j, 0)))
    return pl.pallas_call(body, out_shape=jax.ShapeDtypeStruct((2, 2 * rq, cc), F32), grid_spec=grid_spec, name=name,
                          compiler_params=_params(("parallel", "parallel")))(xy_idx, ta, tb, recv_a, recv_b)


def _allreduce_small(slab):
    r = slab.shape[0]

    def body(x_ref, o_ref, buf, send_sems, recv_sems):
        x, y, c, _ = _place()
        me = 4 * x + 2 * y + c
        buf[me] = x_ref[...]
        peers = []
        for k in range(1, 8):
            kx, ky, kc = (k >> 2) & 1, (k >> 1) & 1, k & 1
            peers.append((x + kx - 2 * x * kx, y + ky - 2 * y * ky, c + kc - 2 * c * kc))

        def copy(k, slot):
            return pltpu.make_async_remote_copy(src_ref=x_ref, dst_ref=buf.at[slot], send_sem=send_sems.at[k],
                                                recv_sem=recv_sems.at[k], device_id=peers[k], device_id_type=MESH)

        for k in range(7):
            copy(k, me).start()
        for k, (px, py, pc) in enumerate(peers):
            copy(k, 4 * px + 2 * py + pc).wait_recv()
        for k in range(7):
            copy(k, me).wait_send()
        acc = buf[0]
        for j in range(1, 8):
            acc = acc + buf[j]
        o_ref[...] = acc

    vm = pl.BlockSpec(memory_space=pltpu.VMEM)
    return pl.pallas_call(
        body, out_shape=jax.ShapeDtypeStruct((r, 128), F32), in_specs=[vm], out_specs=vm,
        scratch_shapes=[pltpu.VMEM((8, r, 128), F32), pltpu.SemaphoreType.DMA((7,)), pltpu.SemaphoreType.DMA((7,))],
        name="allreduce_small")(slab)


def _pack(arrs):
    rows = []
    for a in arrs:
        v = a.reshape(-1)
        v = jnp.pad(v, (0, (-v.shape[0]) % 128))
        rows.append(v.reshape(-1, 128))
    slab = jnp.concatenate(rows, axis=0)
    return jnp.pad(slab, ((0, (-slab.shape[0]) % 8), (0, 0)))


def _unpack(slab, shapes):
    out, r0 = [], 0
    for shp in shapes:
        size = math.prod(shp)
        nr = -(-size // 128)
        out.append(slab[r0:r0 + nr].reshape(-1)[:size].reshape(shp))
        r0 += nr
    return out


BIG = ("w_in", "w_proj_ssd", "w_proj_attn", "w_out", "w_up", "w_down")
SMALL = ("b_gate", "conv_w", "conv_b", "dt_bias_f", "dt_bias_b", "a_log_f", "a_log_b", "d_skip", "ssd_norm_w",
         "ln1_g", "ln1_b", "ln2_g", "ln2_b")
ORDER = ("w_in", "b_gate", "conv_w", "conv_b", "dt_bias_f", "dt_bias_b", "a_log_f", "a_log_b", "d_skip", "ssd_norm_w",
         "w_proj_ssd", "w_proj_attn", "w_out", "ln1_g", "ln1_b", "w_up", "w_down", "ln2_g", "ln2_b")


def kernel(x, w_in, b_gate, conv_w, conv_b, dt_bias_f, dt_bias_b, a_log_f, a_log_b, d_skip, ssd_norm_w, w_proj_ssd, w_proj_attn, w_out, ln1_g, ln1_b, w_up, w_down, ln2_g, ln2_b, loss_target, m_w_in, m_b_gate, m_conv_w, m_conv_b, m_dt_bias_f, m_dt_bias_b, m_a_log_f, m_a_log_b, m_d_skip, m_ssd_norm_w, m_w_proj_ssd, m_w_proj_attn, m_w_out, m_ln1_g, m_ln1_b, m_w_up, m_w_down, m_ln2_g, m_ln2_b, v_w_in, v_b_gate, v_conv_w, v_conv_b, v_dt_bias_f, v_dt_bias_b, v_a_log_f, v_a_log_b, v_d_skip, v_ssd_norm_w, v_w_proj_ssd, v_w_proj_attn, v_w_out, v_ln1_g, v_ln1_b, v_w_up, v_w_down, v_ln2_g, v_ln2_b):
    w = dict(w_in=w_in, b_gate=b_gate, conv_w=conv_w, conv_b=conv_b, dt_bias_f=dt_bias_f, dt_bias_b=dt_bias_b,
             a_log_f=a_log_f, a_log_b=a_log_b, d_skip=d_skip, ssd_norm_w=ssd_norm_w, w_proj_ssd=w_proj_ssd,
             w_proj_attn=w_proj_attn, w_out=w_out, ln1_g=ln1_g, ln1_b=ln1_b, w_up=w_up, w_down=w_down, ln2_g=ln2_g, ln2_b=ln2_b)
    m = dict(w_in=m_w_in, b_gate=m_b_gate, conv_w=m_conv_w, conv_b=m_conv_b, dt_bias_f=m_dt_bias_f, dt_bias_b=m_dt_bias_b,
             a_log_f=m_a_log_f, a_log_b=m_a_log_b, d_skip=m_d_skip, ssd_norm_w=m_ssd_norm_w, w_proj_ssd=m_w_proj_ssd,
             w_proj_attn=m_w_proj_attn, w_out=m_w_out, ln1_g=m_ln1_g, ln1_b=m_ln1_b, w_up=m_w_up, w_down=m_w_down,
             ln2_g=m_ln2_g, ln2_b=m_ln2_b)
    v = dict(w_in=v_w_in, b_gate=v_b_gate, conv_w=v_conv_w, conv_b=v_conv_b, dt_bias_f=v_dt_bias_f, dt_bias_b=v_dt_bias_b,
             a_log_f=v_a_log_f, a_log_b=v_a_log_b, d_skip=v_d_skip, ssd_norm_w=v_ssd_norm_w, w_proj_ssd=v_w_proj_ssd,
             w_proj_attn=v_w_proj_attn, w_out=v_w_out, ln1_g=v_ln1_g, ln1_b=v_ln1_b, w_up=v_w_up, w_down=v_w_down,
             ln2_g=v_ln2_g, ln2_b=v_ln2_b)
    xi, yi, ci = lax.axis_index("x"), lax.axis_index("y"), lax.axis_index("c")
    shard = 2 * xi + yi

    g_in, g_ps, g_pa, g_o, g_up, g_dn = _allgather_weights([w[n].astype(BF16) for n in BIG])
    w_in_full = jnp.concatenate([g_in[s] for s in range(4)], axis=1)
    wts = {"w_in_p": _perm_cols(w_in_full), "w_proj_ssd": g_ps.reshape(DI, D), "w_proj_attn": g_pa,
           "w_out": g_o.reshape(D, D), "w_up": g_up, "w_down": g_dn.reshape(DFF, D)}

    cw_slab = jnp.zeros((KCONV, 4, CONVD // 4), F32)
    cw_slab = lax.dynamic_update_slice(cw_slab, conv_w[:, None, :] * 0.5, (0, shard, 0))
    conv_w_all = _unpack(_allreduce_small(_pack([cw_slab])), [(KCONV, CONVD)])[0]

    sm = {n: w[n] for n in SMALL}
    sm["conv_w"] = conv_w_all
    dx, big, small = _local_grads(x[0], loss_target[0], wts, sm)

    names = list(SMALL) + ["loss"]
    shapes = [small[n].shape for n in names]
    red = dict(zip(names, _unpack(_allreduce_small(_pack([small[n] for n in names])), shapes)))
    loss = red["loss"].reshape(())
    gsm = {n: red[n] for n in SMALL}
    conv_w_grad_shard = lax.dynamic_slice_in_dim(gsm["conv_w"].reshape(KCONV, 4, CONVD // 4), shard, 1, axis=1)
    gsm["conv_w"] = conv_w_grad_shard.reshape(KCONV, CONVD // 4)

    c_idx = jnp.reshape(ci, (1,)).astype(jnp.int32)
    glist = [big[n] for n in BIG]
    xy_idx = jnp.stack([xi, yi, ci]).astype(jnp.int32)
    recv = _swap_halves(glist)
    halves = [_add_half(g, r, c_idx, f"rs_add_half_{n}") for g, r, n in zip(glist, recv, BIG)]
    recv_a, recv_b = _rs_step1([h[1] for h in halves])
    sums1 = [_rs_add1(h[0], ra, rb, xy_idx, f"rs_add1_{n}") for h, ra, rb, n in zip(halves, recv_a, recv_b, BIG)]
    recv_a2, recv_b2 = _rs_step2([s1[2] for s1 in sums1], [s1[3] for s1 in sums1])
    pieces = [_rs_add2(s1[0], s1[1], ra, rb, xy_idx, f"rs_add2_{n}")
              for s1, ra, rb, n in zip(sums1, recv_a2, recv_b2, BIG)]
    joined = _join_halves(pieces)
    gbig = {n: j.reshape(w[n].shape) for n, j in zip(BIG, joined)}

    grads, deltas, new_m, new_v = {}, {}, {}, {}
    for n in BIG:
        grads[n] = gbig[n]
        deltas[n], new_m[n], new_v[n] = _adamw(w[n], gbig[n], m[n], v[n], f"adamw_{n}")
    sshapes = [w[n].shape for n in SMALL]
    d_s, m_s, v_s = _adamw(_pack([w[n] for n in SMALL]), _pack([gsm[n] for n in SMALL]),
                           _pack([m[n] for n in SMALL]), _pack([v[n] for n in SMALL]), "adamw_small")
    for n, dd, mm, vv in zip(SMALL, _unpack(d_s, sshapes), _unpack(m_s, sshapes), _unpack(v_s, sshapes)):
        grads[n], deltas[n], new_m[n], new_v[n] = gsm[n], dd, mm, vv

    return (loss, dx[None], *[grads[n] for n in ORDER], *[deltas[n] for n in ORDER],
            *[new_m[n] for n in ORDER], *[new_v[n] for n in ORDER])
```

```python
import math

import jax
import numpy as np
import jax.numpy as jnp
from jax import lax
from jax.experimental import pallas as pl
from jax.experimental.pallas import tpu as pltpu

F32, BF16 = jnp.float32, jnp.bfloat16
MESH = pl.DeviceIdType.MESH

D = 1024
DI = 2048
NH = 32
HP = 64
NG = 4
NS = 128
Q = 128
CONVD = 3072
KCONV = 5
DFF = 4096
AH = 64
ATT_HALF = 64
DILATIONS = (1, 4, 16)
IN_COLS = 9536
OZ, OGATE, OXBC, OKV, OQ, ODT, UW = 0, 2048, 4096, 7168, 8704, 9472, 9728
ALPHA = 2.0 ** 0.25
NORM_EPS = 1e-5
ADAM_LR, ADAM_B1, ADAM_B2, ADAM_EPS, ADAM_WD, ADAM_STEP = 0.001, 0.9, 0.999, 1e-8, 0.01, 10
VMEM_LIMIT = 56 * 2 ** 20
NEG = -1e30


def _params(sem):
    return pltpu.CompilerParams(dimension_semantics=sem, vmem_limit_bytes=VMEM_LIMIT)


def _sigmoid(x):
    return 1.0 / (1.0 + jnp.exp(-x))


def _softplus(x):
    e = jnp.exp(-jnp.abs(x))
    small = e * (1.0 - e * (0.5 - e * (1.0 / 3.0)))
    return jnp.maximum(x, 0.0) + jnp.where(e < 0.01, small, jnp.log(1.0 + e))


def _split3(a):
    hi = a.astype(BF16)
    r = a - hi.astype(F32)
    mid = r.astype(BF16)
    lo = (r - mid.astype(F32)).astype(BF16)
    return hi, mid, lo


def _dot01(a, m01):
    hi, mid, lo = _split3(a)
    d = lambda p: jnp.dot(p, m01, preferred_element_type=F32)
    return d(hi) + d(mid) + d(lo)


def _dot01_l(m01, a):
    hi, mid, lo = _split3(a)
    d = lambda p: jnp.dot(m01, p, preferred_element_type=F32)
    return d(hi) + d(mid) + d(lo)


def _dot_nt(a, b):
    return lax.dot_general(a, b, (((1,), (1,)), ((), ())), preferred_element_type=F32)


def _iota(shape, dim):
    return lax.broadcasted_iota(jnp.int32, shape, dim)


def _mm_nn(a, b, *, tm, tn, name, out_dtype=F32):
    m, k = a.shape
    if b.ndim == 3:
        assert tn == b.shape[2]
        n = b.shape[0] * b.shape[2]
        b_spec = pl.BlockSpec((None, k, tn), lambda j, i: (j, 0, 0))
    else:
        n = b.shape[1]
        b_spec = pl.BlockSpec((k, tn), lambda j, i: (0, j))

    def body(a_ref, b_ref, o_ref):
        o_ref[...] = jnp.dot(a_ref[...].astype(BF16), b_ref[...], preferred_element_type=F32).astype(out_dtype)

    return pl.pallas_call(
        body, out_shape=jax.ShapeDtypeStruct((m, n), out_dtype), grid=(n // tn, m // tm),
        in_specs=[pl.BlockSpec((tm, k), lambda j, i: (i, 0)), b_spec],
        out_specs=pl.BlockSpec((tm, tn), lambda j, i: (i, j)),
        name=name, compiler_params=_params(("parallel", "parallel")))(a, b)


def _mm_nt(a, b, *, tm, tk, tc, name, add=None, add_scale=1.0):
    m, n = a.shape
    if b.ndim == 3:
        assert tc == b.shape[2]
        k, nc = b.shape[1], b.shape[0]
        b_spec = pl.BlockSpec((None, tk, tc), lambda j, i, c: (c, j, 0))
    else:
        k, nc = b.shape[0], n // tc
        b_spec = pl.BlockSpec((tk, tc), lambda j, i, c: (j, c))

    def body(*refs):
        if add is None:
            a_ref, b_ref, o_ref = refs
        else:
            a_ref, b_ref, add_ref, o_ref = refs
        c = pl.program_id(2)
        part = _dot_nt(a_ref[...].astype(BF16), b_ref[...])

        @pl.when(c == 0)
        def _():
            if add is None:
                o_ref[...] = part
            else:
                o_ref[...] = part + add_scale * add_ref[...]

        @pl.when(c > 0)
        def _():
            o_ref[...] += part

    in_specs = [pl.BlockSpec((tm, tc), lambda j, i, c: (i, c)), b_spec]
    args = [a, b]
    if add is not None:
        in_specs.append(pl.BlockSpec((tm, tk), lambda j, i, c: (i, j)))
        args.append(add)
    return pl.pallas_call(
        body, out_shape=jax.ShapeDtypeStruct((m, k), F32), grid=(k // tk, m // tm, nc),
        in_specs=in_specs, out_specs=pl.BlockSpec((tm, tk), lambda j, i, c: (i, j)),
        name=name, compiler_params=_params(("parallel", "parallel", "arbitrary")))(*args)


def _mm_tn(a, b, *, tka, tn, tt, name, out_shards=None):
    t, ka = a.shape
    n = b.shape[1]
    if out_shards:
        assert tn == n // out_shards
        out_shape = jax.ShapeDtypeStruct((out_shards, ka, tn), F32)
        o_spec = pl.BlockSpec((None, tka, tn), lambda i, j, s: (j, i, 0))
    else:
        out_shape = jax.ShapeDtypeStruct((ka, n), F32)
        o_spec = pl.BlockSpec((tka, tn), lambda i, j, s: (i, j))

    def body(a_ref, b_ref, o_ref):
        s = pl.program_id(2)
        part = lax.dot_general(a_ref[...].astype(BF16), b_ref[...].astype(BF16), (((0,), (0,)), ((), ())),
                               preferred_element_type=F32)

        @pl.when(s == 0)
        def _():
            o_ref[...] = part

        @pl.when(s > 0)
        def _():
            o_ref[...] += part

    return pl.pallas_call(
        body, out_shape=out_shape, grid=(ka // tka, n // tn, t // tt),
        in_specs=[pl.BlockSpec((tt, tka), lambda i, j, s: (s, i)), pl.BlockSpec((tt, tn), lambda i, j, s: (s, j))],
        out_specs=o_spec, name=name, compiler_params=_params(("parallel", "parallel", "arbitrary")))(a, b)


CONV_TM = 512
CONV_TC = 1024
CONV_RC = 64
CONV_CC = 256


def _halo_specs(t, tm, tc, col0):
    nb8 = t // 8
    r8 = tm // 8
    return [
        pl.BlockSpec((8, tc), lambda i, j: (jnp.maximum(i * r8 - 1, 0), col0 + j)),
        pl.BlockSpec((tm, tc), lambda i, j: (i, col0 + j)),
        pl.BlockSpec((8, tc), lambda i, j: (jnp.minimum((i + 1) * r8, nb8 - 1), col0 + j)),
    ]


def _fill_ext(ext, prev_ref, cur_ref, next_ref, tm, i, last):
    ext[0:8, :] = jnp.where(i > 0, prev_ref[...], 0.0)
    ext[8:8 + tm, :] = cur_ref[...]
    ext[8 + tm:16 + tm, :] = jnp.where(i < last, next_ref[...], 0.0)


def _conv_fwd(u, conv_w, conv_b):
    t = u.shape[0]
    tm, tc = CONV_TM, CONV_TC

    def body(prev_ref, cur_ref, next_ref, w_ref, b_ref, o_ref, ext):
        _fill_ext(ext, prev_ref, cur_ref, next_ref, tm, pl.program_id(0), t // tm - 1)
        for c0 in range(0, tc, CONV_CC):
            cs = slice(c0, c0 + CONV_CC)
            w = w_ref[:, cs]
            for r0 in range(0, tm, CONV_RC):
                acc = jnp.broadcast_to(b_ref[:, cs], (CONV_RC, CONV_CC))
                for k in range(KCONV):
                    acc = acc + w[k:k + 1, :] * ext[pl.ds(r0 + 6 + k, CONV_RC), cs]
                o_ref[r0:r0 + CONV_RC, cs] = acc * _sigmoid(acc)

    return pl.pallas_call(
        body, out_shape=jax.ShapeDtypeStruct((t, CONVD), F32), grid=(t // tm, CONVD // tc),
        in_specs=_halo_specs(t, tm, tc, OXBC // tc) + [
            pl.BlockSpec((KCONV, tc), lambda i, j: (0, j)), pl.BlockSpec((1, tc), lambda i, j: (0, j))],
        out_specs=pl.BlockSpec((tm, tc), lambda i, j: (i, j)),
        scratch_shapes=[pltpu.VMEM((tm + 16, tc), F32)],
        name="conv_fwd", compiler_params=_params(("parallel", "parallel")))(u, u, u, conv_w, conv_b)


def _conv_dpre(u, dxs_f, dxs_b, dy, dbc_f, dbc_b, dsk_row, conv_w, conv_b):
    t = u.shape[0]
    tm, tc = CONV_TM, CONV_TC
    r8 = tm // 8
    nb8 = t // 8
    c0 = OXBC // tc

    def body(uprev, ucur, unext, f_ref, b_ref, y_ref, cf_ref, cb_ref, dsk_ref, w_ref, bias_ref,
             dpre_ref, dw_ref, db_ref, ext):
        j = pl.program_id(0)
        i = pl.program_id(1)
        _fill_ext(ext, uprev, ucur, unext, tm, i, t // tm - 1)
        is_xs = j < 2
        dw_cols, db_cols = [], []
        for c0 in range(0, tc, CONV_CC):
            cs = slice(c0, c0 + CONV_CC)
            w = w_ref[:, cs]
            dsk = dsk_ref[:, cs]
            dw_acc = [jnp.zeros((1, CONV_CC), F32) for _ in range(KCONV)]
            db_acc = jnp.zeros((1, CONV_CC), F32)
            for r0 in range(0, tm, CONV_RC):
                rs = slice(r0, r0 + CONV_RC)
                taps = [ext[pl.ds(r0 + 6 + k, CONV_RC), cs] for k in range(KCONV)]
                pre = jnp.broadcast_to(bias_ref[:, cs], (CONV_RC, CONV_CC))
                for k in range(KCONV):
                    pre = pre + w[k:k + 1, :] * taps[k]
                s = _sigmoid(pre)
                xs_part = f_ref[rs, cs] + b_ref[rs, cs] + dsk * y_ref[rs, cs]
                up = jnp.where(is_xs, xs_part, cf_ref[rs, cs] + cb_ref[rs, cs])
                dpre = up * (s * (1.0 + pre * (1.0 - s)))
                dpre_ref[rs, cs] = dpre
                for k in range(KCONV):
                    dw_acc[k] = dw_acc[k] + jnp.sum(dpre * taps[k], axis=0, keepdims=True)
                db_acc = db_acc + jnp.sum(dpre, axis=0, keepdims=True)
            dw_cols.append(jnp.concatenate(dw_acc + [jnp.zeros((8 - KCONV, CONV_CC), F32)], axis=0))
            db_cols.append(jnp.broadcast_to(db_acc, (8, CONV_CC)))
        dw_part = jnp.concatenate(dw_cols, axis=1)
        db_part = jnp.concatenate(db_cols, axis=1)

        @pl.when(i == 0)
        def _():
            dw_ref[...] = dw_part
            db_ref[...] = db_part

        @pl.when(i > 0)
        def _():
            dw_ref[...] += dw_part
            db_ref[...] += db_part

    xs_spec = pl.BlockSpec((tm, tc), lambda j, i: (jnp.where(j < 2, i, 0), jnp.minimum(j, 1)))
    bc_spec = pl.BlockSpec((tm, tc), lambda j, i: (jnp.where(j == 2, i, 0), 0))
    in_specs = [
        pl.BlockSpec((8, tc), lambda j, i: (jnp.maximum(i * r8 - 1, 0), c0 + j)),
        pl.BlockSpec((tm, tc), lambda j, i: (i, c0 + j)),
        pl.BlockSpec((8, tc), lambda j, i: (jnp.minimum((i + 1) * r8, nb8 - 1), c0 + j)),
        xs_spec, xs_spec, xs_spec, bc_spec, bc_spec,
        pl.BlockSpec((1, tc), lambda j, i: (0, jnp.minimum(j, 1))),
        pl.BlockSpec((KCONV, tc), lambda j, i: (0, j)), pl.BlockSpec((1, tc), lambda j, i: (0, j)),
    ]
    return pl.pallas_call(
        body,
        out_shape=(jax.ShapeDtypeStruct((t, CONVD), F32), jax.ShapeDtypeStruct((8, CONVD), F32),
                   jax.ShapeDtypeStruct((8, CONVD), F32)),
        grid=(CONVD // tc, t // tm), in_specs=in_specs,
        out_specs=(pl.BlockSpec((tm, tc), lambda j, i: (i, j)),
                   pl.BlockSpec((8, tc), lambda j, i: (0, j)), pl.BlockSpec((8, tc), lambda j, i: (0, j))),
        scratch_shapes=[pltpu.VMEM((tm + 16, tc), F32)],
        name="conv_dpre", compiler_params=_params(("parallel", "arbitrary")))(
            u, u, u, dxs_f, dxs_b, dy, dbc_f, dbc_b, dsk_row, conv_w, conv_b)


def _conv_dx(du, dpre, conv_w):
    t = dpre.shape[0]
    tm, tc = CONV_TM, CONV_TC
    r8 = tm // 8
    nb8 = t // 8

    def body(prev_ref, cur_ref, next_ref, w_ref, du_in, du_out, ext):
        del du_in
        _fill_ext(ext, prev_ref, cur_ref, next_ref, tm, pl.program_id(1), t // tm - 1)
        for c0 in range(0, tc, CONV_CC):
            cs = slice(c0, c0 + CONV_CC)
            w = w_ref[:, cs]
            for r0 in range(0, tm, CONV_RC):
                acc = jnp.zeros((CONV_RC, CONV_CC), F32)
                for k in range(KCONV):
                    acc = acc + w[k:k + 1, :] * ext[pl.ds(r0 + 10 - k, CONV_RC), cs]
                du_out[r0:r0 + CONV_RC, cs] = acc.astype(du_out.dtype)

    in_specs = [
        pl.BlockSpec((8, tc), lambda j, i: (jnp.maximum(i * r8 - 1, 0), j)),
        pl.BlockSpec((tm, tc), lambda j, i: (i, j)),
        pl.BlockSpec((8, tc), lambda j, i: (jnp.minimum((i + 1) * r8, nb8 - 1), j)),
        pl.BlockSpec((KCONV, tc), lambda j, i: (0, j)),
        pl.BlockSpec(memory_space=pl.ANY),
    ]
    return pl.pallas_call(
        body, out_shape=jax.ShapeDtypeStruct(du.shape, du.dtype), grid=(CONVD // tc, t // tm), in_specs=in_specs,
        out_specs=pl.BlockSpec((tm, tc), lambda j, i: (i, OXBC // tc + j)),
        scratch_shapes=[pltpu.VMEM((tm + 16, tc), F32)], input_output_aliases={4: 0},
        name="conv_dx", compiler_params=_params(("parallel", "parallel")))(dpre, dpre, dpre, conv_w, du)


def _ssd_common(dtr_ref, par_ref, rev):
    raw = dtr_ref[...]
    lane = _iota((1, 128), 1)
    mine = (lane >= 32 * rev) & (lane < 32 * rev + 32)
    bias = par_ref[0:1, :]
    arow = jnp.where(mine, -jnp.exp(par_ref[1:2, :]), 0.0)
    dt = _softplus(raw + bias)
    a = dt * arow
    ri = _iota((Q, Q), 0)
    ci = _iota((Q, Q), 1)
    tri = (ci >= ri) if rev else (ci <= ri)
    trit = (ci <= ri) if rev else (ci >= ri)
    cs = _dot01_l(tri.astype(BF16), a)
    return raw, bias, arow, mine, dt, cs, tri, trit


def _expand_mat(rev):
    r = np.arange(128)[:, None]
    c = np.arange(DI)[None, :]
    return jnp.asarray(r == (c // HP) + 32 * rev, BF16)


def _sum_mat(rev):
    r = np.arange(DI)[:, None]
    c = np.arange(128)[None, :]
    return jnp.asarray(c == (r // HP) + 32 * rev, BF16)


def _ssd_fwd(xbc, u, par, y_add=None, *, rev):
    t = xbc.shape[0]
    nc = t // Q
    end = 0 if rev else Q - 1
    cmap = (lambda c: nc - 1 - c) if rev else (lambda c: c)

    def body(xbc_ref, dtr_ref, par_ref, ex_ref, *rest):
        yadd_ref = rest[0] if y_add is not None else None
        y_ref, st_ref, h_scr = rest[-3:]
        step = pl.program_id(0)

        @pl.when(step == 0)
        def _():
            h_scr[...] = jnp.zeros((NS, DI), F32)

        raw, bias, arow, mine, dt, cs, tri, trit = _ssd_common(dtr_ref, par_ref, rev)
        cst = cs.T
        dtt = dt.T
        tot_col = cst[:, end:end + 1]
        wt = dtt * jnp.exp(tot_col - cst)
        gam = jnp.exp(cs[end:end + 1, :])
        gam_x = _dot01(jnp.broadcast_to(gam, (8, 128)), ex_ref[...])[0:1, :]
        lane = _iota((Q, 128), 1)
        sel = lane < HP
        st_ref[...] = h_scr[...]
        for g in range(NG):
            bg = xbc_ref[:, DI + NS * g:DI + NS * (g + 1)]
            cg = xbc_ref[:, DI + NG * NS + NS * g:DI + NG * NS + NS * (g + 1)]
            cb = _dot_nt(cg.astype(BF16), bg.astype(BF16))
            bt = bg.T
            for k in range(4):
                lo = 512 * g + 128 * k
                xp = xbc_ref[:, lo:lo + 128].astype(BF16)
                hp = h_scr[:, lo:lo + 128]
                rhs = jnp.concatenate([xp, hp.astype(BF16)], axis=0)
                lhs, bts = [], []
                for j in range(2):
                    hc = 8 * g + 2 * k + j + 32 * rev
                    csc = jnp.broadcast_to(cs[:, hc:hc + 1], (Q, Q))
                    lm = jnp.exp(jnp.where(tri, csc - cst[hc:hc + 1, :], NEG)) * dtt[hc:hc + 1, :]
                    mh = (cb * lm).astype(BF16)
                    ec = (jnp.exp(csc) * cg).astype(BF16)
                    lhs.append(jnp.concatenate([mh, ec], axis=1))
                    bts.append((bt * wt[hc:hc + 1, :]).astype(BF16))
                ys = jnp.dot(jnp.concatenate(lhs, axis=0), rhs, preferred_element_type=F32)
                ss = jnp.dot(jnp.concatenate(bts, axis=0), xp, preferred_element_type=F32)
                yp = jnp.where(sel, ys[0:Q], ys[Q:2 * Q])
                y_ref[:, lo:lo + 128] = yp if yadd_ref is None else yp + yadd_ref[:, lo:lo + 128]
                h_scr[:, lo:lo + 128] = gam_x[:, lo:lo + 128] * hp + jnp.where(sel, ss[0:NS], ss[NS:2 * NS])

    return pl.pallas_call(
        body,
        out_shape=(jax.ShapeDtypeStruct((t, DI), F32), jax.ShapeDtypeStruct((nc, NS, DI), F32)),
        grid=(nc,),
        in_specs=[pl.BlockSpec((Q, CONVD), lambda c: (cmap(c), 0)),
                  pl.BlockSpec((Q, 128), lambda c: (cmap(c), ODT // 128)),
                  pl.BlockSpec((8, 128), lambda c: (0, 0)),
                  pl.BlockSpec((128, DI), lambda c: (0, 0))]
        + ([pl.BlockSpec((Q, DI), lambda c: (cmap(c), 0))] if y_add is not None else []),
        out_specs=(pl.BlockSpec((Q, DI), lambda c: (cmap(c), 0)),
                   pl.BlockSpec((None, NS, DI), lambda c: (cmap(c), 0, 0))),
        scratch_shapes=[pltpu.VMEM((NS, DI), F32)],
        name="ssd_fwd_rev" if rev else "ssd_fwd", compiler_params=_params(("arbitrary",)))(
            xbc, u, par, _expand_mat(rev), *([y_add] if y_add is not None else []))


def _ssd_bwd(xbc, u, par, dy, st, *, rev):
    t = xbc.shape[0]
    nc = t // Q
    end = 0 if rev else Q - 1
    cmap = (lambda c: c) if rev else (lambda c: nc - 1 - c)

    def body(xbc_ref, dtr_ref, par_ref, dy_ref, hin_ref, ex_ref, sm_ref, dxs_ref, dbc_ref, ddt_ref, acc_ref, dh_scr):
        step = pl.program_id(0)

        @pl.when(step == 0)
        def _():
            dh_scr[...] = jnp.zeros((NS, DI), F32)

        raw, bias, arow, mine, dt, cs, tri, trit = _ssd_common(dtr_ref, par_ref, rev)
        ri = _iota((Q, Q), 0)
        ci = _iota((Q, Q), 1)
        stri = ((ri > ci) if rev else (ri < ci)).astype(BF16)
        strit = ((ci > ri) if rev else (ci < ri)).astype(BF16)
        cst = cs.T
        dtt = dt.T
        et = jnp.exp(cst)
        expand = ex_ref[...]
        summat = sm_ref[...]
        gam = jnp.exp(cs[end:end + 1, :])
        gam_x = _dot01(jnp.broadcast_to(gam, (8, 128)), expand)[0:1, :]
        dt_hi, dt_mid, _ = _split3(dt)
        dtx = (jnp.dot(dt_hi, expand, preferred_element_type=F32)
               + jnp.dot(dt_mid, expand, preferred_element_type=F32))
        lane = _iota((Q, 128), 1)
        sel = lane < HP
        dho = dh_scr[...]
        t3 = jnp.sum(dho * hin_ref[...], axis=0, keepdims=True) * gam_x
        dxs_cols, dxs2_cols, yoff_cols, a1_rows = [], [], [], []
        for g in range(NG):
            bg = xbc_ref[:, DI + NS * g:DI + NS * (g + 1)]
            cg = xbc_ref[:, DI + NG * NS + NS * g:DI + NG * NS + NS * (g + 1)]
            bb = bg.astype(BF16)
            cbf = cg.astype(BF16)
            cb = _dot_nt(cbf, bb)
            cbt = _dot_nt(bb, cbf)
            ct = cg.T
            bdh = jnp.dot(bb, dho[:, 512 * g:512 * (g + 1)].astype(BF16), preferred_element_type=F32)
            dcb = jnp.zeros((Q, Q), F32)
            dcg = jnp.zeros((Q, NS), F32)
            dbg = jnp.zeros((Q, NS), F32)
            for k in range(4):
                lo = 512 * g + 128 * k
                xpf = xbc_ref[:, lo:lo + 128]
                xp = xpf.astype(BF16)
                dyp = dy_ref[:, lo:lo + 128]
                dypb = dyp.astype(BF16)
                hinp = hin_ref[:, lo:lo + 128].astype(BF16)
                dhp = dho[:, lo:lo + 128]
                es, ws, lmds, mts, ctes, dyms, ecbs = [], [], [], [], [], [], []
                for j in range(2):
                    hc = 8 * g + 2 * k + j + 32 * rev
                    csc = jnp.broadcast_to(cs[:, hc:hc + 1], (Q, Q))
                    csr = cst[hc:hc + 1, :]
                    lmds.append(jnp.exp(jnp.where(tri, csc - csr, NEG)) * dtt[hc:hc + 1, :])
                    lmb = jnp.exp(jnp.where(trit, csr - csc, NEG))
                    mts.append((cbt * lmb).astype(BF16))
                    dyms.append(jnp.where(sel if j == 0 else ~sel, dyp, 0.0).astype(BF16))
                    ecs = jnp.exp(csc)
                    es.append(ecs)
                    ws.append(jnp.exp(cst[hc:hc + 1, end:end + 1] - csc))
                    ecbs.append((ecs * cg).astype(BF16))
                    ctes.append((ct * et[hc:hc + 1, :]).astype(BF16))
                by_dy = jnp.dot(jnp.concatenate(mts + ctes, axis=0), dypb, preferred_element_type=F32)
                dmm = _dot_nt(jnp.concatenate(dyms, axis=0), xp)
                dm0, dm1 = dmm[0:Q] * lmds[0], dmm[Q:2 * Q] * lmds[1]
                dcb = dcb + dm0 + dm1
                rr = jnp.dot(jnp.concatenate([dm0 * cb, dm1 * cb], axis=0).astype(BF16), stri, preferred_element_type=F32)
                a1_rows.append(jnp.sum(jnp.where(tri, rr[0:Q], 0.0), axis=0, keepdims=True))
                a1_rows.append(jnp.sum(jnp.where(tri, rr[Q:2 * Q], 0.0), axis=0, keepdims=True))
                yo = jnp.dot(jnp.concatenate(ecbs, axis=0), hinp, preferred_element_type=F32)
                e_p = jnp.where(sel, es[0], es[1])
                w_p = jnp.where(sel, ws[0], ws[1])
                d2 = w_p * bdh[:, 128 * k:128 * (k + 1)]
                dxs2_cols.append(d2)
                dxs_cols.append(jnp.where(sel, by_dy[0:Q], by_dy[Q:2 * Q]) + d2)
                yoff_cols.append(jnp.where(sel, yo[0:Q], yo[Q:2 * Q]))
                dcg = dcg + _dot_nt((e_p * dyp).astype(BF16), hinp)
                dbg = dbg + _dot_nt((w_p * dtx[:, lo:lo + 128] * xpf).astype(BF16), dhp.astype(BF16))
                dh_scr[:, lo:lo + 128] = (gam_x[:, lo:lo + 128] * dhp
                                          + jnp.where(sel, by_dy[2 * Q:3 * Q], by_dy[3 * Q:4 * Q]))
            dcg = dcg + jnp.dot(dcb.astype(BF16), bb, preferred_element_type=F32)
            dbg = dbg + jnp.dot(dcb.T.astype(BF16), cbf, preferred_element_type=F32)
            dbc_ref[:, NS * g:NS * (g + 1)] = dbg
            dbc_ref[:, NG * NS + NS * g:NG * NS + NS * (g + 1)] = dcg
        dxs = jnp.concatenate(dxs_cols, axis=1)
        dxs_ref[...] = dxs * dtx
        xs = xbc_ref[:, 0:DI]
        stacked = jnp.concatenate([xs * dxs, xs * jnp.concatenate(dxs2_cols, axis=1),
                                   dy_ref[...] * jnp.concatenate(yoff_cols, axis=1),
                                   jnp.broadcast_to(t3, (8, DI))], axis=0).astype(BF16)
        sums = jnp.dot(stacked, summat, preferred_element_type=F32)
        rx, rx2, ryo, c0 = sums[0:Q], sums[Q:2 * Q], sums[2 * Q:3 * Q], sums[3 * Q:3 * Q + 1]
        zero32 = jnp.zeros((32, Q), F32)
        a1t = jnp.concatenate(([zero32] if rev else []) + a1_rows + [zero32] * (2 if rev else 3), axis=0)
        da = (a1t.T + jnp.dot(trit.astype(BF16), ryo.astype(BF16), preferred_element_type=F32)
              + jnp.dot(strit, (dt * rx2).astype(BF16), preferred_element_type=F32) + jnp.where(mine, c0, 0.0))
        ddt = rx + da * arow
        ddtr = ddt * _sigmoid(raw + bias)
        ddt_ref[...] = ddtr
        part = jnp.concatenate([jnp.sum(ddtr, axis=0, keepdims=True),
                                jnp.sum(da * dt, axis=0, keepdims=True) * arow,
                                jnp.zeros((6, 128), F32)], axis=0)

        @pl.when(step == 0)
        def _():
            acc_ref[...] = part

        @pl.when(step > 0)
        def _():
            acc_ref[...] += part

    return pl.pallas_call(
        body,
        out_shape=(jax.ShapeDtypeStruct((t, DI), F32), jax.ShapeDtypeStruct((t, 2 * NG * NS), F32),
                   jax.ShapeDtypeStruct((t, 128), F32), jax.ShapeDtypeStruct((8, 128), F32)),
        grid=(nc,),
        in_specs=[pl.BlockSpec((Q, CONVD), lambda c: (cmap(c), 0)),
                  pl.BlockSpec((Q, 128), lambda c: (cmap(c), ODT // 128)),
                  pl.BlockSpec((8, 128), lambda c: (0, 0)),
                  pl.BlockSpec((Q, DI), lambda c: (cmap(c), 0)),
                  pl.BlockSpec((None, NS, DI), lambda c: (cmap(c), 0, 0)),
                  pl.BlockSpec((128, DI), lambda c: (0, 0)), pl.BlockSpec((DI, 128), lambda c: (0, 0))],
        out_specs=(pl.BlockSpec((Q, DI), lambda c: (cmap(c), 0)),
                   pl.BlockSpec((Q, 2 * NG * NS), lambda c: (cmap(c), 0)),
                   pl.BlockSpec((Q, 128), lambda c: (cmap(c), 0)),
                   pl.BlockSpec((8, 128), lambda c: (0, 0))),
        scratch_shapes=[pltpu.VMEM((NS, DI), F32)],
        name="ssd_bwd_rev" if rev else "ssd_bwd", compiler_params=_params(("arbitrary",)))(
            xbc, u, par, dy, st, _expand_mat(rev), _sum_mat(rev))


GN_TM = 256
GN_GROUP = DI // NG


def _gn_forward_vals(y0, xs, z, dsk):
    y = y0 + dsk * xs
    sz = _sigmoid(z)
    gate = z * sz
    y2 = y * gate
    parts, rs = [], []
    for g in range(NG):
        seg = y2[:, GN_GROUP * g:GN_GROUP * (g + 1)]
        r = lax.rsqrt(jnp.mean(seg * seg, axis=1, keepdims=True) + NORM_EPS)
        rs.append(r)
        parts.append(seg * r)
    yn = jnp.concatenate(parts, axis=1)
    return y, sz, gate, yn, rs


def _gatenorm_fwd(y_fb, xbc, u, dsk_row, nw_row):
    t = y_fb.shape[0]
    tm = GN_TM

    def body(y_ref, xs_ref, z_ref, dsk_ref, nw_ref, o_ref):
        _, _, _, yn, _ = _gn_forward_vals(y_ref[...], xs_ref[...], z_ref[...], dsk_ref[...])
        o_ref[...] = (yn * nw_ref[...]).astype(BF16)

    blk = pl.BlockSpec((tm, DI), lambda i: (i, 0))
    row = pl.BlockSpec((1, DI), lambda i: (0, 0))
    return pl.pallas_call(
        body, out_shape=jax.ShapeDtypeStruct((t, DI), BF16), grid=(t // tm,),
        in_specs=[blk, blk, pl.BlockSpec((tm, DI), lambda i: (i, OZ // DI)), row, row],
        out_specs=blk, name="gatenorm_fwd", compiler_params=_params(("parallel",)))(y_fb, xbc, u, dsk_row, nw_row)


def _gatenorm_bwd(ds_out, y_fb, xbc, u, du, dsk_row, nw_row):
    t = y_fb.shape[0]
    tm = GN_TM

    def body(ds_ref, y_ref, xs_ref, z_ref, dsk_ref, nw_ref, sm_ref, du_in, dy_ref, du_out, dnw_ref, dds_ref):
        del du_in
        i = pl.program_id(0)
        xs = xs_ref[...]
        z = z_ref[...]
        y, sz, gate, yn, rs = _gn_forward_vals(y_ref[...], xs, z, dsk_ref[...])
        ds = ds_ref[...]
        gsc = ds * nw_ref[...]
        parts = []
        for g in range(NG):
            sl = slice(GN_GROUP * g, GN_GROUP * (g + 1))
            m = jnp.mean(gsc[:, sl] * yn[:, sl], axis=1, keepdims=True)
            parts.append(rs[g] * (gsc[:, sl] - yn[:, sl] * m))
        dy2 = jnp.concatenate(parts, axis=1)
        dy = dy2 * gate
        dy_ref[...] = dy
        du_out[...] = (dy2 * y * (sz * (1.0 + z * (1.0 - sz)))).astype(du_out.dtype)
        dnw = jnp.broadcast_to(jnp.sum(ds * yn, axis=0, keepdims=True), (8, DI))
        drow = jnp.broadcast_to(jnp.sum(dy * xs, axis=0, keepdims=True), (8, DI))
        dds = _dot01(drow, sm_ref[...])

        @pl.when(i == 0)
        def _():
            dnw_ref[...] = dnw
            dds_ref[...] = dds

        @pl.when(i > 0)
        def _():
            dnw_ref[...] += dnw
            dds_ref[...] += dds

    blk = pl.BlockSpec((tm, DI), lambda i: (i, 0))
    row = pl.BlockSpec((1, DI), lambda i: (0, 0))
    return pl.pallas_call(
        body,
        out_shape=(jax.ShapeDtypeStruct((t, DI), F32), jax.ShapeDtypeStruct(du.shape, du.dtype),
                   jax.ShapeDtypeStruct((8, DI), F32), jax.ShapeDtypeStruct((8, 128), F32)),
        grid=(t // tm,),
        in_specs=[blk, blk, blk, pl.BlockSpec((tm, DI), lambda i: (i, OZ // DI)), row, row,
                  pl.BlockSpec((DI, 128), lambda i: (0, 0)), pl.BlockSpec(memory_space=pl.ANY)],
        out_specs=(blk, pl.BlockSpec((tm, DI), lambda i: (i, OZ // DI)),
                   pl.BlockSpec((8, DI), lambda i: (0, 0)), pl.BlockSpec((8, 128), lambda i: (0, 0))),
        input_output_aliases={7: 1},
        name="gatenorm_bwd", compiler_params=_params(("arbitrary",)))(
            ds_out, y_fb, xbc, u, dsk_row, nw_row, _sum_mat(0), du)


AT_B = 128
AT_W = AT_B + 2 * ATT_HALF
AT_L = 2 * AH
SCALE = 1.0 / math.sqrt(AH)


def _slope(g, hh):
    return 2.0 ** (-8.0 * (4 * g + hh + 1) / 12.0)


def _qcol(g):
    return lambda p: OQ // AT_L + 2 * g + p


def _kcol(g):
    return lambda p: OKV // AT_L + 4 * g + 2 * p


def _vcol(g):
    return lambda p: OKV // AT_L + 4 * g + 2 * p + 1


def _pcol(p):
    return p


def _sub(d):
    return 2 if d == 1 else 1


def _win_specs(col, t, d):
    tb, hb = AT_B * d * _sub(d), ATT_HALF * d
    per = tb // hb
    nh = t // hb
    return [
        pl.BlockSpec((hb, AT_L), lambda p, i: (jnp.maximum(per * i - 1, 0), col(p))),
        pl.BlockSpec((tb, AT_L), lambda p, i: (i, col(p))),
        pl.BlockSpec((hb, AT_L), lambda p, i: (jnp.minimum(per * (i + 1), nh - 1), col(p))),
    ]


def _blk_spec(col, d):
    return pl.BlockSpec((AT_B * d * _sub(d), AT_L), lambda p, i: (i, col(p)))


def _rows(ref, r, s, d):
    return ref[pl.ds(r, AT_B, stride=d), :] if d > 1 else ref[AT_B * s:AT_B * (s + 1), :]


def _win(p_ref, c_ref, n_ref, r, s, d):
    if d > 1:
        return jnp.concatenate([p_ref[pl.ds(r, ATT_HALF, stride=d), :], c_ref[pl.ds(r, AT_B, stride=d), :],
                                n_ref[pl.ds(r, ATT_HALF, stride=d), :]], axis=0)
    if s == 0:
        return jnp.concatenate([p_ref[...], c_ref[0:AT_B + ATT_HALF, :]], axis=0)
    return jnp.concatenate([c_ref[ATT_HALF:2 * AT_B, :], n_ref[...]], axis=0)


def _put_rows(ref, r, s, d, val):
    if d > 1:
        ref[pl.ds(r, AT_B, stride=d), :] = val
    else:
        ref[AT_B * s:AT_B * (s + 1), :] = val


def _for_blocks(d, fn):
    if d == 1:
        for s in range(_sub(d)):
            fn(0, s)
    else:
        def step(r, c):
            fn(r, 0)
            return c
        lax.fori_loop(0, d, step, 0, unroll=2)


def _attn_geometry(i, ln, d):
    a = i * AT_B + _iota((AT_B, AT_W), 0)
    b = i * AT_B - ATT_HALF + _iota((AT_B, AT_W), 1)
    rel = jnp.abs(a - b)
    valid = (rel <= ATT_HALF) & (b >= 0) & (b < ln)
    return valid, (rel * d).astype(F32)


def _attn_fwd(u, g):
    t = u.shape[0]
    d = DILATIONS[g]
    ln = t // d

    def body(q_ref, kp, kc, kn, vp, vc, vn, o_ref, l_ref):
        p_id = pl.program_id(0)
        i = pl.program_id(1)
        lane = _iota((AT_B, AT_L), 1)

        def one(r, s):
            valid, dist = _attn_geometry(i * _sub(d) + s, ln, d)
            q = _rows(q_ref, r, s, d)
            kw = _win(kp, kc, kn, r, s, d).astype(BF16)
            vw = _win(vp, vc, vn, r, s, d).astype(BF16)
            o = jnp.zeros((AT_B, AT_L), F32)
            lse = jnp.zeros((AT_B, AT_L), F32)
            for hh in range(2):
                hm = (lane // AH) == hh
                slope = jnp.where(p_id == 0, _slope(g, hh), _slope(g, 2 + hh))
                qm = jnp.where(hm, q, 0.0).astype(BF16)
                sc = _dot_nt(qm, kw) * SCALE - slope * dist
                sc = jnp.where(valid, sc, NEG)
                m = jnp.max(sc, axis=1, keepdims=True)
                pr = jnp.exp(sc - m)
                den = jnp.sum(pr, axis=1, keepdims=True)
                oh = jnp.dot(pr.astype(BF16), vw, preferred_element_type=F32)
                o = jnp.where(hm, oh / den, o)
                lse = jnp.where(hm, m + jnp.log(den), lse)
            _put_rows(o_ref, r, s, d, o)
            _put_rows(l_ref, r, s, d, lse)

        _for_blocks(d, one)

    oshape = jax.ShapeDtypeStruct((t, 2 * AT_L), F32)
    ospec = _blk_spec(_pcol, d)
    return pl.pallas_call(
        body, out_shape=(oshape, oshape), grid=(2, t // (AT_B * d * _sub(d))),
        in_specs=[_blk_spec(_qcol(g), d)] + _win_specs(_kcol(g), t, d) + _win_specs(_vcol(g), t, d),
        out_specs=(ospec, ospec), name=f"attn_fwd_{g}", compiler_params=_params(("parallel", "parallel")))(
            u, u, u, u, u, u, u)


def _attn_dq(u, du, do, lse, e, g):
    t = u.shape[0]
    d = DILATIONS[g]
    ln = t // d

    def body(q_ref, kp, kc, kn, vp, vc, vn, do_ref, l_ref, e_ref, du_in, dq_ref, dq_scr):
        del du_in
        p_id = pl.program_id(0)
        i = pl.program_id(1)
        lane = _iota((AT_B, AT_L), 1)

        def one(r, s):
            valid, dist = _attn_geometry(i * _sub(d) + s, ln, d)
            q = _rows(q_ref, r, s, d)
            kw = _win(kp, kc, kn, r, s, d).astype(BF16)
            vw = _win(vp, vc, vn, r, s, d).astype(BF16)
            do_ = _rows(do_ref, r, s, d)
            lv = _rows(l_ref, r, s, d)
            ev = _rows(e_ref, r, s, d)
            dq = jnp.zeros((AT_B, AT_L), F32)
            for hh in range(2):
                hm = (lane // AH) == hh
                slope = jnp.where(p_id == 0, _slope(g, hh), _slope(g, 2 + hh))
                qm = jnp.where(hm, q, 0.0).astype(BF16)
                sc = _dot_nt(qm, kw) * SCALE - slope * dist
                lcol = jnp.broadcast_to(lv[:, AH * hh:AH * hh + 1], (AT_B, AT_W))
                ecol = jnp.broadcast_to(ev[:, AH * hh:AH * hh + 1], (AT_B, AT_W))
                pr = jnp.exp(jnp.where(valid, sc - lcol, NEG))
                dom = jnp.where(hm, do_, 0.0).astype(BF16)
                ds = pr * (_dot_nt(dom, vw) + ecol)
                dqh = jnp.dot(ds.astype(BF16), kw, preferred_element_type=F32) * SCALE
                dq = jnp.where(hm, dqh, dq)
            _put_rows(dq_scr, r, s, d, dq)

        _for_blocks(d, one)
        dq_ref[...] = dq_scr[...].astype(dq_ref.dtype)

    rspec = _blk_spec(_pcol, d)
    return pl.pallas_call(
        body, out_shape=jax.ShapeDtypeStruct(du.shape, du.dtype), grid=(2, t // (AT_B * d * _sub(d))),
        in_specs=[_blk_spec(_qcol(g), d)] + _win_specs(_kcol(g), t, d) + _win_specs(_vcol(g), t, d)
        + [rspec, rspec, rspec, pl.BlockSpec(memory_space=pl.ANY)],
        out_specs=_blk_spec(_qcol(g), d), input_output_aliases={10: 0},
        scratch_shapes=[pltpu.VMEM((AT_B * d * _sub(d), AT_L), F32)],
        name=f"attn_dq_{g}", compiler_params=_params(("parallel", "parallel")))(
            u, u, u, u, u, u, u, do, lse, e, du)


def _attn_dkv(u, du, do, lse, e, g):
    t = u.shape[0]
    d = DILATIONS[g]
    ln = t // d

    def body(k_ref, v_ref, qp, qc, qn, dp_, dc_, dn_, lp, lc, ln_, ep, ec, en, du_in, dkv_ref, dk_scr, dv_scr):
        del du_in
        p_id = pl.program_id(0)
        jb = pl.program_id(1)
        lane = _iota((AT_B, AT_L), 1)

        def one(r, s):
            valid, dist = _attn_geometry(jb * _sub(d) + s, ln, d)
            k = _rows(k_ref, r, s, d)
            v = _rows(v_ref, r, s, d)
            qw = _win(qp, qc, qn, r, s, d).astype(BF16)
            dow = _win(dp_, dc_, dn_, r, s, d).astype(BF16)
            lt = _win(lp, lc, ln_, r, s, d).T
            et = _win(ep, ec, en, r, s, d).T
            dk = jnp.zeros((AT_B, AT_L), F32)
            dv = jnp.zeros((AT_B, AT_L), F32)
            for hh in range(2):
                hm = (lane // AH) == hh
                slope = jnp.where(p_id == 0, _slope(g, hh), _slope(g, 2 + hh))
                km = jnp.where(hm, k, 0.0).astype(BF16)
                st = _dot_nt(km, qw) * SCALE - slope * dist
                pt = jnp.exp(jnp.where(valid, st - lt[AH * hh:AH * hh + 1, :], NEG))
                dvh = jnp.dot(pt.astype(BF16), dow, preferred_element_type=F32)
                vm = jnp.where(hm, v, 0.0).astype(BF16)
                dst = pt * (_dot_nt(vm, dow) + et[AH * hh:AH * hh + 1, :])
                dkh = jnp.dot(dst.astype(BF16), qw, preferred_element_type=F32) * SCALE
                dk = jnp.where(hm, dkh, dk)
                dv = jnp.where(hm, dvh, dv)
            _put_rows(dk_scr, r, s, d, dk)
            _put_rows(dv_scr, r, s, d, dv)

        _for_blocks(d, one)
        dkv_ref[:, 0:AT_L] = dk_scr[...].astype(dkv_ref.dtype)
        dkv_ref[:, AT_L:2 * AT_L] = dv_scr[...].astype(dkv_ref.dtype)

    return pl.pallas_call(
        body, out_shape=jax.ShapeDtypeStruct(du.shape, du.dtype), grid=(2, t // (AT_B * d * _sub(d))),
        in_specs=[_blk_spec(_kcol(g), d), _blk_spec(_vcol(g), d)]
        + _win_specs(_qcol(g), t, d) + _win_specs(_pcol, t, d) + _win_specs(_pcol, t, d) + _win_specs(_pcol, t, d)
        + [pl.BlockSpec(memory_space=pl.ANY)],
        out_specs=pl.BlockSpec((AT_B * d * _sub(d), 2 * AT_L), lambda p, i: (i, OKV // (2 * AT_L) + 2 * g + p)),
        input_output_aliases={14: 0},
        scratch_shapes=[pltpu.VMEM((AT_B * d * _sub(d), AT_L), F32), pltpu.VMEM((AT_B * d * _sub(d), AT_L), F32)],
        name=f"attn_dkv_{g}", compiler_params=_params(("parallel", "parallel")))(
            u, u, u, u, u, do, do, do, lse, lse, lse, e, e, e, du)


CMB_TM = 1024


def _combine_weights(l0, l1, l2):
    m = jnp.maximum(jnp.maximum(l0, l1), l2)
    e0, e1, e2 = jnp.exp(l0 - m), jnp.exp(l1 - m), jnp.exp(l2 - m)
    inv = 1.0 / (e0 + e1 + e2)
    return e0 * inv, e1 * inv, e2 * inv


def _combine_fwd(os_, ls_):
    t = os_[0].shape[0]
    tm = CMB_TM

    def body(o0, o1, o2, l0, l1, l2, a_ref):
        w0, w1, w2 = _combine_weights(l0[...], l1[...], l2[...])
        a_ref[...] = w0 * o0[...] + w1 * o1[...] + w2 * o2[...]

    blk = pl.BlockSpec((tm, 2 * AT_L), lambda i: (i, 0))
    return pl.pallas_call(
        body, out_shape=jax.ShapeDtypeStruct((t, 2 * AT_L), F32), grid=(t // tm,), in_specs=[blk] * 6, out_specs=blk,
        name="combine_fwd", compiler_params=_params(("parallel",)))(*os_, *ls_)


def _combine_bwd(datt, os_, ls_):
    t = datt.shape[0]
    tm = CMB_TM

    def body(da_ref, o0, o1, o2, l0, l1, l2, d0, d1, d2, e0, e1, e2):
        w = _combine_weights(l0[...], l1[...], l2[...])
        da = da_ref[...]
        att = w[0] * o0[...] + w[1] * o1[...] + w[2] * o2[...]
        r = _iota((2 * AT_L, 2 * AT_L), 0) // AH
        c = _iota((2 * AT_L, 2 * AT_L), 1) // AH
        hs = _dot01(da * att, (r == c).astype(BF16))
        for wg, dref, eref in zip(w, (d0, d1, d2), (e0, e1, e2)):
            dref[...] = wg * da
            eref[...] = -wg * hs

    blk = pl.BlockSpec((tm, 2 * AT_L), lambda i: (i, 0))
    shp = jax.ShapeDtypeStruct((t, 2 * AT_L), F32)
    outs = pl.pallas_call(
        body, out_shape=(shp,) * 6, grid=(t // tm,), in_specs=[blk] * 7, out_specs=(blk,) * 6,
        name="combine_bwd", compiler_params=_params(("parallel",)))(datt, *os_, *ls_)
    return outs[0:3], outs[3:6]


ROW_TM = 512


def _mix_fwd(y_ssd, y_att, u, bg_row):
    t = y_ssd.shape[0]
    tm = ROW_TM

    def body(ys_ref, ya_ref, g0_ref, g1_ref, b0_ref, b1_ref, o_ref):
        g0 = _sigmoid(g0_ref[...] + b0_ref[...])
        g1 = _sigmoid(g1_ref[...] + b1_ref[...])
        o_ref[...] = (g0 * ys_ref[...] + g1 * ya_ref[...]).astype(BF16)

    blk = pl.BlockSpec((tm, D), lambda i: (i, 0))
    return pl.pallas_call(
        body, out_shape=jax.ShapeDtypeStruct((t, D), BF16), grid=(t // tm,),
        in_specs=[blk, blk, pl.BlockSpec((tm, D), lambda i: (i, OGATE // D)), pl.BlockSpec((tm, D), lambda i: (i, OGATE // D + 1)),
                  pl.BlockSpec((1, D), lambda i: (0, 0)), pl.BlockSpec((1, D), lambda i: (0, 1))],
        out_specs=blk, name="mix_fwd", compiler_params=_params(("parallel",)))(y_ssd, y_att, u, u, bg_row, bg_row)


def _mix_bwd(dmixin, y_ssd, y_att, u, bg_row):
    t = y_ssd.shape[0]
    tm = ROW_TM

    def body(dm_ref, ys_ref, ya_ref, g0_ref, g1_ref, b0_ref, b1_ref, dys_ref, dya_ref, du_ref, db_ref):
        i = pl.program_id(0)
        g0 = _sigmoid(g0_ref[...] + b0_ref[...])
        g1 = _sigmoid(g1_ref[...] + b1_ref[...])
        dm = dm_ref[...]
        dys_ref[...] = (dm * g0).astype(BF16)
        dya_ref[...] = (dm * g1).astype(BF16)
        dl0 = dm * ys_ref[...] * g0 * (1.0 - g0)
        dl1 = dm * ya_ref[...] * g1 * (1.0 - g1)
        du_ref[:, 0:D] = dl0.astype(BF16)
        du_ref[:, D:2 * D] = dl1.astype(BF16)
        part = jnp.concatenate([jnp.broadcast_to(jnp.sum(dl0, axis=0, keepdims=True), (8, D)),
                                jnp.broadcast_to(jnp.sum(dl1, axis=0, keepdims=True), (8, D))], axis=1)

        @pl.when(i == 0)
        def _():
            db_ref[...] = part

        @pl.when(i > 0)
        def _():
            db_ref[...] += part

    blk = pl.BlockSpec((tm, D), lambda i: (i, 0))
    return pl.pallas_call(
        body,
        out_shape=(jax.ShapeDtypeStruct((t, D), BF16), jax.ShapeDtypeStruct((t, D), BF16),
                   jax.ShapeDtypeStruct((t, UW), BF16), jax.ShapeDtypeStruct((8, 2 * D), F32)),
        grid=(t // tm,),
        in_specs=[blk, blk, blk, pl.BlockSpec((tm, D), lambda i: (i, OGATE // D)), pl.BlockSpec((tm, D), lambda i: (i, OGATE // D + 1)),
                  pl.BlockSpec((1, D), lambda i: (0, 0)), pl.BlockSpec((1, D), lambda i: (0, 1))],
        out_specs=(blk, blk, pl.BlockSpec((tm, 2 * D), lambda i: (i, OGATE // (2 * D))),
                   pl.BlockSpec((8, 2 * D), lambda i: (0, 0))),
        name="mix_bwd", compiler_params=_params(("arbitrary",)))(dmixin, y_ssd, y_att, u, u, bg_row, bg_row)


def _ln(x, g, b):
    mu = jnp.mean(x, axis=1, keepdims=True)
    xc = x - mu
    var = jnp.mean(xc * xc, axis=1, keepdims=True)
    rstd = lax.rsqrt(var + NORM_EPS)
    xhat = xc * rstd
    return xhat * g + b, xhat, rstd


def _ln_back(dh, xhat, rstd, g):
    dxh = dh * g
    m1 = jnp.mean(dxh, axis=1, keepdims=True)
    m2 = jnp.mean(dxh * xhat, axis=1, keepdims=True)
    return rstd * (dxh - m1 - xhat * m2)


def _ln1_fwd(x, mix, g_row, b_row):
    t = x.shape[0]
    tm = ROW_TM

    def body(x_ref, m_ref, g_ref, b_ref, pre_ref, h_ref):
        pre = ALPHA * x_ref[...] + m_ref[...]
        pre_ref[...] = pre
        h, _, _ = _ln(pre, g_ref[...], b_ref[...])
        h_ref[...] = h.astype(BF16)

    blk = pl.BlockSpec((tm, D), lambda i: (i, 0))
    row = pl.BlockSpec((1, D), lambda i: (0, 0))
    return pl.pallas_call(
        body, out_shape=(jax.ShapeDtypeStruct((t, D), F32), jax.ShapeDtypeStruct((t, D), BF16)), grid=(t // tm,),
        in_specs=[blk, blk, row, row], out_specs=(blk, blk),
        name="ln1_fwd", compiler_params=_params(("parallel",)))(x, mix, g_row, b_row)


def _ln1_bwd(dh, pre, g_row, b_row):
    t = dh.shape[0]
    tm = ROW_TM

    def body(dh_ref, pre_ref, g_ref, b_ref, dpre_ref, acc_ref):
        i = pl.program_id(0)
        dh_ = dh_ref[...]
        _, xhat, rstd = _ln(pre_ref[...], g_ref[...], b_ref[...])
        dpre_ref[...] = _ln_back(dh_, xhat, rstd, g_ref[...])
        part = jnp.concatenate([jnp.sum(dh_ * xhat, axis=0, keepdims=True), jnp.sum(dh_, axis=0, keepdims=True),
                                jnp.zeros((6, D), F32)], axis=0)

        @pl.when(i == 0)
        def _():
            acc_ref[...] = part

        @pl.when(i > 0)
        def _():
            acc_ref[...] += part

    blk = pl.BlockSpec((tm, D), lambda i: (i, 0))
    row = pl.BlockSpec((1, D), lambda i: (0, 0))
    return pl.pallas_call(
        body, out_shape=(jax.ShapeDtypeStruct((t, D), F32), jax.ShapeDtypeStruct((8, D), F32)), grid=(t // tm,),
        in_specs=[blk, blk, row, row], out_specs=(blk, pl.BlockSpec((8, D), lambda i: (0, 0))),
        name="ln1_bwd", compiler_params=_params(("arbitrary",)))(dh, pre, g_row, b_row)


def _ln2_loss(pre1, f, tgt, g1_row, b1_row, g2_row, b2_row):
    t = pre1.shape[0]
    tm = ROW_TM

    def body(p1_ref, f_ref, t_ref, g1_ref, b1_ref, g2_ref, b2_ref, dpre_ref, acc_ref):
        i = pl.program_id(0)
        h1, _, _ = _ln(p1_ref[...], g1_ref[...], b1_ref[...])
        pre2 = ALPHA * h1 + f_ref[...]
        h2, xhat, rstd = _ln(pre2, g2_ref[...], b2_ref[...])
        err = h2 - t_ref[...]
        dh = err * (1.0 / D)
        dpre_ref[...] = _ln_back(dh, xhat, rstd, g2_ref[...])
        loss = jnp.sum(jnp.sum(err * err, axis=1, keepdims=True), axis=0, keepdims=True) * (0.5 / D)
        part = jnp.concatenate([jnp.sum(dh * xhat, axis=0, keepdims=True), jnp.sum(dh, axis=0, keepdims=True),
                                jnp.broadcast_to(loss, (1, D)), jnp.zeros((5, D), F32)], axis=0)

        @pl.when(i == 0)
        def _():
            acc_ref[...] = part

        @pl.when(i > 0)
        def _():
            acc_ref[...] += part

    blk = pl.BlockSpec((tm, D), lambda i: (i, 0))
    row = pl.BlockSpec((1, D), lambda i: (0, 0))
    return pl.pallas_call(
        body, out_shape=(jax.ShapeDtypeStruct((t, D), F32), jax.ShapeDtypeStruct((8, D), F32)), grid=(t // tm,),
        in_specs=[blk, blk, blk, row, row, row, row], out_specs=(blk, pl.BlockSpec((8, D), lambda i: (0, 0))),
        name="ln2_loss", compiler_params=_params(("arbitrary",)))(pre1, f, tgt, g1_row, b1_row, g2_row, b2_row)


def _mlp_up(h1, w_up):
    t = h1.shape[0]
    tm, tn = ROW_TM, D

    def body(a_ref, b_ref, up_ref, act_ref):
        up = jnp.dot(a_ref[...], b_ref[...], preferred_element_type=F32)
        up_ref[...] = up.astype(BF16)
        r = jnp.maximum(up, 0.0)
        act_ref[...] = (r * r).astype(BF16)

    blk = pl.BlockSpec((tm, tn), lambda j, i: (i, j))
    return pl.pallas_call(
        body, out_shape=(jax.ShapeDtypeStruct((t, DFF), BF16), jax.ShapeDtypeStruct((t, DFF), BF16)),
        grid=(DFF // tn, t // tm),
        in_specs=[pl.BlockSpec((tm, D), lambda j, i: (i, 0)), pl.BlockSpec((None, D, tn), lambda j, i: (j, 0, 0))],
        out_specs=(blk, blk), name="mlp_up", compiler_params=_params(("parallel", "parallel")))(h1, w_up)


def _d_up(dpre2, w_down, up):
    t = up.shape[0]
    tm, tk = ROW_TM, D

    def body(a_ref, b_ref, u_ref, o_ref):
        dact = _dot_nt(a_ref[...], b_ref[...])
        o_ref[...] = (dact * 2.0 * jnp.maximum(u_ref[...].astype(F32), 0.0)).astype(BF16)

    blk = pl.BlockSpec((tm, tk), lambda j, i: (i, j))
    return pl.pallas_call(
        body, out_shape=jax.ShapeDtypeStruct((t, DFF), BF16), grid=(DFF // tk, t // tm),
        in_specs=[pl.BlockSpec((tm, D), lambda j, i: (i, 0)), pl.BlockSpec((tk, D), lambda j, i: (j, 0)), blk],
        out_specs=blk, name="d_up", compiler_params=_params(("parallel", "parallel")))(dpre2, w_down, up)


def _dt_bwd(du, ddt_f, ddt_b):
    t = ddt_f.shape[0]
    tm = 1024

    def body(f_ref, b_ref, du_in, o_ref):
        del du_in
        o_ref[:, 0:128] = (f_ref[...] + b_ref[...]).astype(o_ref.dtype)
        o_ref[:, 128:256] = jnp.zeros((tm, 128), o_ref.dtype)

    blk = pl.BlockSpec((tm, 128), lambda i: (i, 0))
    return pl.pallas_call(
        body, out_shape=jax.ShapeDtypeStruct(du.shape, du.dtype), grid=(t // tm,),
        in_specs=[blk, blk, pl.BlockSpec(memory_space=pl.ANY)],
        out_specs=pl.BlockSpec((tm, 256), lambda i: (i, ODT // 256)), input_output_aliases={2: 0},
        name="dt_bwd", compiler_params=_params(("parallel",)))(ddt_f, ddt_b, du)


def _mix_out_ln1(y_ssd, y_att, u, bg_row, x, w_out, g_row, b_row):
    t = x.shape[0]
    tm = ROW_TM

    def body(ys_ref, ya_ref, g0_ref, g1_ref, b0_ref, b1_ref, x_ref, w_ref, g_ref, b_ref, mixin_ref, pre_ref, h_ref):
        g0 = _sigmoid(g0_ref[...] + b0_ref[...])
        g1 = _sigmoid(g1_ref[...] + b1_ref[...])
        mixin = (g0 * ys_ref[...] + g1 * ya_ref[...]).astype(BF16)
        mixin_ref[...] = mixin
        pre = ALPHA * x_ref[...] + jnp.dot(mixin, w_ref[...], preferred_element_type=F32)
        pre_ref[...] = pre
        h, _, _ = _ln(pre, g_ref[...], b_ref[...])
        h_ref[...] = h.astype(BF16)

    blk = pl.BlockSpec((tm, D), lambda i: (i, 0))
    row = pl.BlockSpec((1, D), lambda i: (0, 0))
    return pl.pallas_call(
        body,
        out_shape=(jax.ShapeDtypeStruct((t, D), BF16), jax.ShapeDtypeStruct((t, D), F32), jax.ShapeDtypeStruct((t, D), BF16)),
        grid=(t // tm,),
        in_specs=[blk, blk, pl.BlockSpec((tm, D), lambda i: (i, OGATE // D)), pl.BlockSpec((tm, D), lambda i: (i, OGATE // D + 1)),
                  row, pl.BlockSpec((1, D), lambda i: (0, 1)), blk, pl.BlockSpec((D, D), lambda i: (0, 0)), row, row],
        out_specs=(blk, blk, blk), name="mix_out_ln1", compiler_params=_params(("parallel",)))(
            y_ssd, y_att, u, u, bg_row, bg_row, x, w_out, g_row, b_row)


def _mlp_down_ln2_loss(act, w_down, pre1, tgt, g1_row, b1_row, g2_row, b2_row):
    t = pre1.shape[0]
    tm = ROW_TM

    def body(a_ref, w_ref, p1_ref, t_ref, g1_ref, b1_ref, g2_ref, b2_ref, dpre_ref, dpreb_ref, acc_ref):
        i = pl.program_id(0)
        f = jnp.dot(a_ref[...], w_ref[...], preferred_element_type=F32)
        h1, _, _ = _ln(p1_ref[...], g1_ref[...], b1_ref[...])
        pre2 = ALPHA * h1 + f
        h2, xhat, rstd = _ln(pre2, g2_ref[...], b2_ref[...])
        err = h2 - t_ref[...]
        dh = err * (1.0 / D)
        dpre = _ln_back(dh, xhat, rstd, g2_ref[...])
        dpre_ref[...] = dpre
        dpreb_ref[...] = dpre.astype(BF16)
        loss = jnp.sum(jnp.sum(err * err, axis=1, keepdims=True), axis=0, keepdims=True) * (0.5 / D)
        part = jnp.concatenate([jnp.sum(dh * xhat, axis=0, keepdims=True), jnp.sum(dh, axis=0, keepdims=True),
                                jnp.broadcast_to(loss, (1, D)), jnp.zeros((5, D), F32)], axis=0)

        @pl.when(i == 0)
        def _():
            acc_ref[...] = part

        @pl.when(i > 0)
        def _():
            acc_ref[...] += part

    blk = pl.BlockSpec((tm, D), lambda i: (i, 0))
    row = pl.BlockSpec((1, D), lambda i: (0, 0))
    return pl.pallas_call(
        body,
        out_shape=(jax.ShapeDtypeStruct((t, D), F32), jax.ShapeDtypeStruct((t, D), BF16), jax.ShapeDtypeStruct((8, D), F32)),
        grid=(t // tm,),
        in_specs=[pl.BlockSpec((tm, DFF), lambda i: (i, 0)), pl.BlockSpec((DFF, D), lambda i: (0, 0)), blk, blk, row, row, row, row],
        out_specs=(blk, blk, pl.BlockSpec((8, D), lambda i: (0, 0))),
        name="mlp_down_ln2_loss", compiler_params=_params(("arbitrary",)))(act, w_down, pre1, tgt, g1_row, b1_row, g2_row, b2_row)


def _d_h1_ln1_bwd(dup, w_up, dpre2, pre1, g_row, b_row):
    t = dup.shape[0]
    tm = ROW_TM
    nsh = w_up.shape[0]

    def body(a_ref, w_ref, add_ref, pre_ref, g_ref, b_ref, dpre_ref, acc_ref, dh_scr):
        i = pl.program_id(0)
        c = pl.program_id(1)
        part = _dot_nt(a_ref[...], w_ref[...])

        @pl.when(c == 0)
        def _():
            dh_scr[...] = part + ALPHA * add_ref[...]

        @pl.when(c > 0)
        def _():
            dh_scr[...] += part

        @pl.when(c == nsh - 1)
        def _():
            dh_ = dh_scr[...]
            _, xhat, rstd = _ln(pre_ref[...], g_ref[...], b_ref[...])
            dpre_ref[...] = _ln_back(dh_, xhat, rstd, g_ref[...])
            rows = jnp.concatenate([jnp.sum(dh_ * xhat, axis=0, keepdims=True), jnp.sum(dh_, axis=0, keepdims=True),
                                    jnp.zeros((6, D), F32)], axis=0)

            @pl.when(i == 0)
            def _():
                acc_ref[...] = rows

            @pl.when(i > 0)
            def _():
                acc_ref[...] += rows

    blk = pl.BlockSpec((tm, D), lambda i, c: (i, 0))
    row = pl.BlockSpec((1, D), lambda i, c: (0, 0))
    return pl.pallas_call(
        body, out_shape=(jax.ShapeDtypeStruct((t, D), F32), jax.ShapeDtypeStruct((8, D), F32)),
        grid=(t // tm, nsh),
        in_specs=[pl.BlockSpec((tm, D), lambda i, c: (i, c)), pl.BlockSpec((None, D, D), lambda i, c: (c, 0, 0)),
                  blk, blk, row, row],
        out_specs=(blk, pl.BlockSpec((8, D), lambda i, c: (0, 0))),
        scratch_shapes=[pltpu.VMEM((tm, D), F32)],
        name="d_h1_ln1_bwd", compiler_params=_params(("arbitrary", "arbitrary")))(dup, w_up, dpre2, pre1, g_row, b_row)


def _d_mixin_mix_bwd(dpre1, w_out, y_ssd, y_att, u, bg_row):
    t = y_ssd.shape[0]
    tm = ROW_TM

    def body(a_ref, w_ref, ys_ref, ya_ref, g0_ref, g1_ref, b0_ref, b1_ref, dys_ref, dya_ref, du_ref, db_ref):
        i = pl.program_id(0)
        dm = _dot_nt(a_ref[...].astype(BF16), w_ref[...])
        g0 = _sigmoid(g0_ref[...] + b0_ref[...])
        g1 = _sigmoid(g1_ref[...] + b1_ref[...])
        dys_ref[...] = (dm * g0).astype(BF16)
        dya_ref[...] = (dm * g1).astype(BF16)
        dl0 = dm * ys_ref[...] * g0 * (1.0 - g0)
        dl1 = dm * ya_ref[...] * g1 * (1.0 - g1)
        du_ref[:, 0:D] = dl0.astype(BF16)
        du_ref[:, D:2 * D] = dl1.astype(BF16)
        part = jnp.concatenate([jnp.broadcast_to(jnp.sum(dl0, axis=0, keepdims=True), (8, D)),
                                jnp.broadcast_to(jnp.sum(dl1, axis=0, keepdims=True), (8, D))], axis=1)

        @pl.when(i == 0)
        def _():
            db_ref[...] = part

        @pl.when(i > 0)
        def _():
            db_ref[...] += part

    blk = pl.BlockSpec((tm, D), lambda i: (i, 0))
    return pl.pallas_call(
        body,
        out_shape=(jax.ShapeDtypeStruct((t, D), BF16), jax.ShapeDtypeStruct((t, D), BF16),
                   jax.ShapeDtypeStruct((t, UW), BF16), jax.ShapeDtypeStruct((8, 2 * D), F32)),
        grid=(t // tm,),
        in_specs=[blk, pl.BlockSpec((D, D), lambda i: (0, 0)), blk, blk,
                  pl.BlockSpec((tm, D), lambda i: (i, OGATE // D)), pl.BlockSpec((tm, D), lambda i: (i, OGATE // D + 1)),
                  pl.BlockSpec((1, D), lambda i: (0, 0)), pl.BlockSpec((1, D), lambda i: (0, 1))],
        out_specs=(blk, blk, pl.BlockSpec((tm, 2 * D), lambda i: (i, OGATE // (2 * D))),
                   pl.BlockSpec((8, 2 * D), lambda i: (0, 0))),
        name="d_mixin_mix_bwd", compiler_params=_params(("arbitrary",)))(dpre1, w_out, y_ssd, y_att, u, u, bg_row, bg_row)


def _adamw(w, g, m, v, name):
    r, c = w.shape
    tr = r
    for cand in (256, 128, 64, 32, 16, 8):
        if r % cand == 0 and cand * c * 4 <= 2 ** 21:
            tr = cand
            break
    bc1 = 1.0 / (1.0 - ADAM_B1 ** ADAM_STEP)
    bc2 = 1.0 / (1.0 - ADAM_B2 ** ADAM_STEP)

    def body(w_ref, g_ref, m_ref, v_ref, d_ref, nm_ref, nv_ref):
        gg = g_ref[...]
        nm = ADAM_B1 * m_ref[...] + (1.0 - ADAM_B1) * gg
        nv = ADAM_B2 * v_ref[...] + (1.0 - ADAM_B2) * (gg * gg)
        nm_ref[...] = nm
        nv_ref[...] = nv
        d_ref[...] = -ADAM_LR * ((nm * bc1) / (jnp.sqrt(nv * bc2) + ADAM_EPS) + ADAM_WD * w_ref[...])

    blk = pl.BlockSpec((tr, c), lambda i: (i, 0))
    shp = jax.ShapeDtypeStruct((r, c), F32)
    return pl.pallas_call(body, out_shape=(shp, shp, shp), grid=(r // tr,), in_specs=[blk] * 4, out_specs=(blk,) * 3,
                          name=name, compiler_params=_params(("parallel",)))(w, g, m, v)


def _perm_cols(w):
    z, xbc, dt = w[:, 0:2048], w[:, 2048:5120], w[:, 5120:5184]
    q, k, v, gate = w[:, 5184:5952], w[:, 5952:6720], w[:, 6720:7488], w[:, 7488:9536]
    kv = []
    for g in range(3):
        for p in range(2):
            lo = 256 * g + 128 * p
            kv += [k[:, lo:lo + 128], v[:, lo:lo + 128]]
    pad = jnp.zeros((w.shape[0], UW - IN_COLS), w.dtype)
    return jnp.concatenate([z, gate, xbc] + kv + [q, dt, pad], axis=1)


def _unperm_cols(wp):
    z, gate, xbc = wp[:, OZ:OZ + 2048], wp[:, OGATE:OGATE + 2048], wp[:, OXBC:OXBC + CONVD]
    q, dt = wp[:, OQ:OQ + 768], wp[:, ODT:ODT + 64]
    ks, vs = [], []
    for g in range(3):
        for p in range(2):
            lo = OKV + 128 * (4 * g + 2 * p)
            ks.append(wp[:, lo:lo + 128])
            vs.append(wp[:, lo + 128:lo + 256])
    return jnp.concatenate([z, xbc, dt, q] + ks + vs + [gate], axis=1)


def _lanes128(*vecs):
    v = jnp.concatenate([a.reshape(-1) for a in vecs])
    return jnp.pad(v, (0, 128 - v.shape[0])).reshape(1, 128)


def _local_grads(x, tgt, wts, sm):
    row = lambda a: a.reshape(1, -1)
    bg_row, cb_row = row(sm["b_gate"]), row(sm["conv_b"])
    par = jnp.concatenate([_lanes128(sm["dt_bias_f"], sm["dt_bias_b"]), _lanes128(sm["a_log_f"], sm["a_log_b"]),
                           jnp.zeros((6, 128), F32)], axis=0)
    dsk_row = row(jnp.repeat(sm["d_skip"], HP))
    nw_row = row(sm["ssd_norm_w"])
    g1, b1, g2, b2 = row(sm["ln1_g"]), row(sm["ln1_b"]), row(sm["ln2_g"]), row(sm["ln2_b"])

    xb = x.astype(BF16)
    u = _mm_nn(xb, wts["w_in_p"], tm=512, tn=2432, name="in_proj")
    xbc = _conv_fwd(u, sm["conv_w"], cb_row)
    y_f, st_f = _ssd_fwd(xbc, u, par, rev=False)
    y_fb, st_b = _ssd_fwd(xbc, u, par, y_f, rev=True)
    s_out = _gatenorm_fwd(y_fb, xbc, u, dsk_row, nw_row)
    y_ssd = _mm_nn(s_out, wts["w_proj_ssd"], tm=512, tn=1024, name="proj_ssd")
    att_o, att_l = [], []
    for g in range(3):
        o, l = _attn_fwd(u, g)
        att_o.append(o)
        att_l.append(l)
    att = _combine_fwd(att_o, att_l)
    y_att = _mm_nn(att, wts["w_proj_attn"], tm=512, tn=256, name="proj_attn")
    mixin, pre1, h1 = _mix_out_ln1(y_ssd, y_att, u, bg_row, x, wts["w_out"], g1, b1)
    up, act = _mlp_up(h1, wts["w_up"])
    dpre2, dpre2_b, acc2 = _mlp_down_ln2_loss(act, wts["w_down"], pre1, tgt, g1, b1, g2, b2)

    dw_down = _mm_tn(act, dpre2_b, tka=1024, tn=1024, tt=1024, name="dw_down")
    dup = _d_up(dpre2_b, wts["w_down"], up)
    dw_up = _mm_tn(h1, dup, tka=1024, tn=1024, tt=1024, name="dw_up", out_shards=4)
    dpre1, acc1 = _d_h1_ln1_bwd(dup, wts["w_up"], dpre2, pre1, g1, b1)
    dw_out = _mm_tn(mixin, dpre1, tka=1024, tn=1024, tt=1024, name="dw_out")
    dy_ssd, dy_att, du, dbg = _d_mixin_mix_bwd(dpre1, wts["w_out"], y_ssd, y_att, u, bg_row)
    dw_proj_ssd = _mm_tn(s_out, dy_ssd, tka=1024, tn=1024, tt=1024, name="dw_proj_ssd")
    ds_out = _mm_nt(dy_ssd, wts["w_proj_ssd"], tm=512, tk=1024, tc=1024, name="d_s_out")
    dw_proj_attn = _mm_tn(att, dy_att, tka=256, tn=256, tt=1024, name="dw_proj_attn", out_shards=4)
    datt = _mm_nt(dy_att, wts["w_proj_attn"], tm=512, tk=256, tc=256, name="d_att")
    do_g, e_g = _combine_bwd(datt, att_o, att_l)
    for g in range(3):
        du = _attn_dq(u, du, do_g[g], att_l[g], e_g[g], g)
        du = _attn_dkv(u, du, do_g[g], att_l[g], e_g[g], g)
    dy, du, dnw, dds = _gatenorm_bwd(ds_out, y_fb, xbc, u, du, dsk_row, nw_row)
    dxs_f, dbc_f, ddt_f, sacc_f = _ssd_bwd(xbc, u, par, dy, st_f, rev=False)
    dxs_b, dbc_b, ddt_b, sacc_b = _ssd_bwd(xbc, u, par, dy, st_b, rev=True)
    dpre_c, dcw, dcb = _conv_dpre(u, dxs_f, dxs_b, dy, dbc_f, dbc_b, dsk_row, sm["conv_w"], cb_row)
    du = _conv_dx(du, dpre_c, sm["conv_w"])
    du = _dt_bwd(du, ddt_f, ddt_b)
    dw_in_p = _mm_tn(xb, du, tka=1024, tn=2432, tt=1024, name="dw_in")
    dx = _mm_nt(du, wts["w_in_p"], tm=512, tk=1024, tc=2432, name="d_x", add=dpre1, add_scale=ALPHA)

    sacc = sacc_f + sacc_b
    small = {
        "b_gate": dbg[0], "conv_w": dcw[0:KCONV], "conv_b": dcb[0],
        "dt_bias_f": sacc[0, 0:32], "dt_bias_b": sacc[0, 32:64], "a_log_f": sacc[1, 0:32], "a_log_b": sacc[1, 32:64],
        "d_skip": dds[0, 0:32], "ssd_norm_w": dnw[0],
        "ln1_g": acc1[0], "ln1_b": acc1[1], "ln2_g": acc2[0], "ln2_b": acc2[1], "loss": acc2[2, 0:1],
    }
    dw_in = _unperm_cols(dw_in_p)
    big = {
        "w_in": dw_in.reshape(D, 4, IN_COLS // 4).transpose(1, 0, 2),
        "w_proj_ssd": dw_proj_ssd.reshape(4, DI // 4, D),
        "w_proj_attn": dw_proj_attn,
        "w_out": dw_out.reshape(4, D // 4, D),
        "w_up": dw_up,
        "w_down": dw_down.reshape(4, DFF // 4, D),
    }
    return dx, big, small


HBM_SPEC = pl.BlockSpec(memory_space=pl.ANY)


def _place():
    x, y, c = lax.axis_index("x"), lax.axis_index("y"), lax.axis_index("c")
    chips = [(1 - x, y), (x, 1 - y), (1 - x, 1 - y)]
    return x, y, c, chips


def _allgather_weights(shards):
    n = len(shards)

    def body(*refs):
        ins, outs = refs[:n], refs[n:2 * n]
        send_sems, recv_sems = refs[2 * n:]
        x, y, c, _ = _place()
        q, q_x, q_y, q_d = 2 * x + y, 2 * (1 - x) + y, 2 * x + 1 - y, 2 * (1 - x) + 1 - y
        x_nbr, y_nbr, sibling = (1 - x, y, c), (x, 1 - y, c), (x, y, 1 - c)

        def copy(w, k, src, dst, to):
            return pltpu.make_async_remote_copy(src_ref=src, dst_ref=dst, send_sem=send_sems.at[w, k],
                                                recv_sem=recv_sems.at[w, k], device_id=to, device_id_type=MESH)

        def rows(w, core, part):
            rh = ins[w].shape[0] // 2
            if part is None:
                return pl.ds(core * rh, rh)
            return pl.ds(core * rh + part * (rh // 2), rh // 2)

        def same(w, k, slot, core, part, to):
            blk = outs[w].at[slot, rows(w, core, part), :]
            return copy(w, k, blk, blk, to)

        started = []
        for w in range(n):
            cp = copy(w, 8, ins[w], outs[w].at[q], sibling)
            cp.start()
            started.append(cp)
            mine = rows(w, c, None)
            for k, to in ((0, x_nbr), (1, y_nbr)):
                cp = copy(w, k, ins[w].at[mine, :], outs[w].at[q, mine, :], to)
                cp.start()
                started.append(cp)
        for w in range(n):
            same(w, 0, q_x, c, None, x_nbr).wait_recv()
            for cp in (same(w, 2, q_x, c, 0, y_nbr), same(w, 4, q_x, c, None, sibling)):
                cp.start()
                started.append(cp)
            same(w, 1, q_y, c, None, y_nbr).wait_recv()
            for cp in (same(w, 3, q_y, c, 1, x_nbr), same(w, 5, q_y, c, None, sibling)):
                cp.start()
                started.append(cp)
        for w in range(n):
            same(w, 2, q_d, c, 0, y_nbr).wait_recv()
            cp = same(w, 6, q_d, c, 0, sibling)
            cp.start()
            started.append(cp)
            same(w, 3, q_d, c, 1, x_nbr).wait_recv()
            cp = same(w, 7, q_d, c, 1, sibling)
            cp.start()
            started.append(cp)
        for w in range(n):
            same(w, 4, q_x, 1 - c, None, sibling).wait_recv()
            same(w, 5, q_y, 1 - c, None, sibling).wait_recv()
            same(w, 6, q_d, 1 - c, 0, sibling).wait_recv()
            same(w, 7, q_d, 1 - c, 1, sibling).wait_recv()
            copy(w, 8, ins[w], outs[w].at[q], sibling).wait_recv()
        for cp in started:
            cp.wait_send()

    return pl.pallas_call(
        body, out_shape=[jax.ShapeDtypeStruct((4,) + s.shape, s.dtype) for s in shards],
        in_specs=[HBM_SPEC] * n, out_specs=[HBM_SPEC] * n,
        scratch_shapes=[pltpu.SemaphoreType.DMA((n, 9)), pltpu.SemaphoreType.DMA((n, 9))],
        name="allgather_weights")(*shards)


def _swap_halves(grads):
    n = len(grads)

    def body(*refs):
        ins, outs = refs[:n], refs[n:2 * n]
        send_sems, recv_sems = refs[2 * n:]
        x, y, c, _ = _place()
        copies = []
        for w in range(n):
            rh = ins[w].shape[1] // 2
            for p in range(4):
                cp = pltpu.make_async_remote_copy(
                    src_ref=ins[w].at[p, pl.ds((1 - c) * rh, rh), :], dst_ref=outs[w].at[p],
                    send_sem=send_sems.at[w, p], recv_sem=recv_sems.at[w, p],
                    device_id=(x, y, 1 - c), device_id_type=MESH)
                cp.start()
                copies.append(cp)
        for cp in copies:
            cp.wait()

    return pl.pallas_call(
        body, out_shape=[jax.ShapeDtypeStruct((4, g.shape[1] // 2, g.shape[2]), F32) for g in grads],
        in_specs=[HBM_SPEC] * n, out_specs=[HBM_SPEC] * n,
        scratch_shapes=[pltpu.SemaphoreType.DMA((n, 4)), pltpu.SemaphoreType.DMA((n, 4))],
        name="rs_swap_halves")(*grads)


def _rs_step1(parts):
    n = len(parts)

    def body(*refs):
        ins, out_a, out_b = refs[:n], refs[n:2 * n], refs[2 * n:3 * n]
        send_sems, recv_sems = refs[3 * n:]
        x, y, c, _ = _place()
        copies = []
        for w in range(n):
            rq = ins[w].shape[1] // 2
            for i in range(2):
                copies.append(pltpu.make_async_remote_copy(
                    src_ref=ins[w].at[2 * (1 - x) + i, pl.ds(0, rq), :], dst_ref=out_a[w].at[i],
                    send_sem=send_sems.at[w, i], recv_sem=recv_sems.at[w, i],
                    device_id=(1 - x, y, c), device_id_type=MESH))
                copies.append(pltpu.make_async_remote_copy(
                    src_ref=ins[w].at[2 * i + 1 - y, pl.ds(rq, rq), :], dst_ref=out_b[w].at[i],
                    send_sem=send_sems.at[w, 2 + i], recv_sem=recv_sems.at[w, 2 + i],
                    device_id=(x, 1 - y, c), device_id_type=MESH))
        for cp in copies:
            cp.start()
        for cp in copies:
            cp.wait()

    quarter = lambda p: jax.ShapeDtypeStruct((2, p.shape[1] // 2, p.shape[2]), p.dtype)
    outs = pl.pallas_call(
        body, out_shape=[quarter(p) for p in parts] * 2,
        in_specs=[HBM_SPEC] * n, out_specs=[HBM_SPEC] * (2 * n),
        scratch_shapes=[pltpu.SemaphoreType.DMA((n, 4)), pltpu.SemaphoreType.DMA((n, 4))],
        name="rs_step1")(*parts)
    return outs[:n], outs[n:]


def _rs_step2(tas, tbs):
    n = len(tas)

    def body(*refs):
        in_a, in_b, out_a, out_b = refs[:n], refs[n:2 * n], refs[2 * n:3 * n], refs[3 * n:4 * n]
        send_sems, recv_sems = refs[4 * n:]
        x, y, c, _ = _place()
        copies = []
        for w in range(n):
            copies.append(pltpu.make_async_remote_copy(
                src_ref=in_a[w].at[1 - y], dst_ref=out_a[w], send_sem=send_sems.at[w, 0], recv_sem=recv_sems.at[w, 0],
                device_id=(x, 1 - y, c), device_id_type=MESH))
            copies.append(pltpu.make_async_remote_copy(
                src_ref=in_b[w].at[1 - x], dst_ref=out_b[w], send_sem=send_sems.at[w, 1], recv_sem=recv_sems.at[w, 1],
                device_id=(1 - x, y, c), device_id_type=MESH))
        for cp in copies:
            cp.start()
        for cp in copies:
            cp.wait()

    one = lambda p: jax.ShapeDtypeStruct(p.shape[1:], p.dtype)
    outs = pl.pallas_call(
        body, out_shape=[one(p) for p in tas] + [one(p) for p in tbs],
        in_specs=[HBM_SPEC] * (2 * n), out_specs=[HBM_SPEC] * (2 * n),
        scratch_shapes=[pltpu.SemaphoreType.DMA((n, 2)), pltpu.SemaphoreType.DMA((n, 2))],
        name="rs_step2")(*tas, *tbs)
    return outs[:n], outs[n:]


def _join_halves(pieces):
    n = len(pieces)

    def body(*refs):
        outs = refs[n:2 * n]
        send_sems, recv_sems = refs[2 * n:]
        x, y, c, _ = _place()

        def copy(w, slot):
            return pltpu.make_async_remote_copy(
                src_ref=outs[w].at[slot], dst_ref=outs[w].at[slot], send_sem=send_sems.at[w], recv_sem=recv_sems.at[w],
                device_id=(x, y, 1 - c), device_id_type=MESH)

        for w in range(n):
            copy(w, c).start()
        for w in range(n):
            copy(w, 1 - c).wait_recv()
            copy(w, c).wait_send()

    return pl.pallas_call(
        body, out_shape=[jax.ShapeDtypeStruct(p.shape, F32) for p in pieces],
        in_specs=[HBM_SPEC] * n, out_specs=[HBM_SPEC] * n, input_output_aliases={w: w for w in range(n)},
        scratch_shapes=[pltpu.SemaphoreType.DMA((n,)), pltpu.SemaphoreType.DMA((n,))],
        name="rs_join_halves")(*pieces)


def _add_tile_rows(rh, c):
    for cand in (512, 256, 128, 64, 32, 16, 8):
        if rh % cand == 0 and cand * c * 4 <= 2 ** 21:
            return cand
    return rh


def _add_half(grad, recv, c_idx, name):
    _, r, cc = grad.shape
    rh = r // 2
    tr = _add_tile_rows(rh, cc)
    nb = rh // tr

    def body(c_ref, g_ref, r_ref, o_ref, ob_ref):
        del c_ref
        s = g_ref[...] + r_ref[...]
        o_ref[...] = s
        ob_ref[...] = s.astype(BF16)

    blk = pl.BlockSpec((None, tr, cc), lambda p, i, c_ref: (p, i, 0))
    grid_spec = pltpu.PrefetchScalarGridSpec(
        num_scalar_prefetch=1, grid=(4, nb),
        in_specs=[pl.BlockSpec((None, tr, cc), lambda p, i, c_ref: (p, c_ref[0] * nb + i, 0)), blk],
        out_specs=(blk, blk))
    return pl.pallas_call(
        body, out_shape=(jax.ShapeDtypeStruct((4, rh, cc), F32), jax.ShapeDtypeStruct((4, rh, cc), BF16)),
        grid_spec=grid_spec, name=name, compiler_params=_params(("parallel", "parallel")))(c_idx, grad, recv)


def _rs_add1(part, recv_a, recv_b, xy_idx, name):
    _, rh, cc = part.shape
    rq = rh // 2
    tr = _add_tile_rows(rq, cc)
    nb = rq // tr

    def body(xy_ref, pa_ref, pb_ref, ra_ref, rb_ref, ta_ref, tb_ref, tab_ref, tbb_ref):
        del xy_ref
        ta = pa_ref[...] + ra_ref[...].astype(F32)
        tb = pb_ref[...] + rb_ref[...].astype(F32)
        ta_ref[...] = ta
        tb_ref[...] = tb
        tab_ref[...] = ta.astype(BF16)
        tbb_ref[...] = tb.astype(BF16)

    blk = pl.BlockSpec((None, tr, cc), lambda i, j, xy: (i, j, 0))
    grid_spec = pltpu.PrefetchScalarGridSpec(
        num_scalar_prefetch=1, grid=(2, nb),
        in_specs=[pl.BlockSpec((None, tr, cc), lambda i, j, xy: (2 * xy[0] + i, j, 0)),
                  pl.BlockSpec((None, tr, cc), lambda i, j, xy: (2 * i + xy[1], nb + j, 0)), blk, blk],
        out_specs=(blk, blk, blk, blk))
    f32s, b16s = jax.ShapeDtypeStruct((2, rq, cc), F32), jax.ShapeDtypeStruct((2, rq, cc), BF16)
    return pl.pallas_call(body, out_shape=(f32s, f32s, b16s, b16s), grid_spec=grid_spec, name=name,
                          compiler_params=_params(("parallel", "parallel")))(xy_idx, part, part, recv_a, recv_b)


def _rs_add2(ta, tb, recv_a, recv_b, xy_idx, name):
    _, rq, cc = ta.shape
    tr = _add_tile_rows(rq, cc)
    nb = rq // tr

    def body(xy_ref, ta_ref, tb_ref, ra_ref, rb_ref, o_ref):
        del xy_ref
        s = pl.program_id(0)
        fa = ta_ref[...] + ra_ref[...].astype(F32)
        fb = tb_ref[...] + rb_ref[...].astype(F32)
        o_ref[...] = jnp.where(s == 0, fa, fb)

    rblk = pl.BlockSpec((tr, cc), lambda s, j, xy: (j, 0))
    grid_spec = pltpu.PrefetchScalarGridSpec(
        num_scalar_prefetch=1, grid=(2, nb),
        in_specs=[pl.BlockSpec((None, tr, cc), lambda s, j, xy: (xy[1], j, 0)),
                  pl.BlockSpec((None, tr, cc), lambda s, j, xy: (xy[0], j, 0)), rblk, rblk],
        out_specs=pl.BlockSpec((None, tr, cc), lambda s, j, xy: (xy[2], s * nb + j, 0)))
    return pl.pallas_call(body, out_shape=jax.ShapeDtypeStruct((2, 2 * rq, cc), F32), grid_spec=grid_spec, name=name,
                          compiler_params=_params(("parallel", "parallel")))(xy_idx, ta, tb, recv_a, recv_b)


def _allreduce_small(slab):
    r = slab.shape[0]

    def body(x_ref, o_ref, buf, send_sems, recv_sems):
        x, y, c, _ = _place()
        me = 4 * x + 2 * y + c
        buf[me] = x_ref[...]
        peers = []
        for k in range(1, 8):
            kx, ky, kc = (k >> 2) & 1, (k >> 1) & 1, k & 1
            peers.append((x + kx - 2 * x * kx, y + ky - 2 * y * ky, c + kc - 2 * c * kc))

        def copy(k, slot):
            return pltpu.make_async_remote_copy(src_ref=x_ref, dst_ref=buf.at[slot], send_sem=send_sems.at[k],
                                                recv_sem=recv_sems.at[k], device_id=peers[k], device_id_type=MESH)

        for k in range(7):
            copy(k, me).start()
        for k, (px, py, pc) in enumerate(peers):
            copy(k, 4 * px + 2 * py + pc).wait_recv()
        for k in range(7):
            copy(k, me).wait_send()
        acc = buf[0]
        for j in range(1, 8):
            acc = acc + buf[j]
        o_ref[...] = acc

    vm = pl.BlockSpec(memory_space=pltpu.VMEM)
    return pl.pallas_call(
        body, out_shape=jax.ShapeDtypeStruct((r, 128), F32), in_specs=[vm], out_specs=vm,
        scratch_shapes=[pltpu.VMEM((8, r, 128), F32), pltpu.SemaphoreType.DMA((7,)), pltpu.SemaphoreType.DMA((7,))],
        name="allreduce_small")(slab)


def _pack(arrs):
    rows = []
    for a in arrs:
        v = a.reshape(-1)
        v = jnp.pad(v, (0, (-v.shape[0]) % 128))
        rows.append(v.reshape(-1, 128))
    slab = jnp.concatenate(rows, axis=0)
    return jnp.pad(slab, ((0, (-slab.shape[0]) % 8), (0, 0)))


def _unpack(slab, shapes):
    out, r0 = [], 0
    for shp in shapes:
        size = math.prod(shp)
        nr = -(-size // 128)
        out.append(slab[r0:r0 + nr].reshape(-1)[:size].reshape(shp))
        r0 += nr
    return out


BIG = ("w_in", "w_proj_ssd", "w_proj_attn", "w_out", "w_up", "w_down")
SMALL = ("b_gate", "conv_w", "conv_b", "dt_bias_f", "dt_bias_b", "a_log_f", "a_log_b", "d_skip", "ssd_norm_w",
         "ln1_g", "ln1_b", "ln2_g", "ln2_b")
ORDER = ("w_in", "b_gate", "conv_w", "conv_b", "dt_bias_f", "dt_bias_b", "a_log_f", "a_log_b", "d_skip", "ssd_norm_w",
         "w_proj_ssd", "w_proj_attn", "w_out", "ln1_g", "ln1_b", "w_up", "w_down", "ln2_g", "ln2_b")


def kernel(x, w_in, b_gate, conv_w, conv_b, dt_bias_f, dt_bias_b, a_log_f, a_log_b, d_skip, ssd_norm_w, w_proj_ssd, w_proj_attn, w_out, ln1_g, ln1_b, w_up, w_down, ln2_g, ln2_b, loss_target, m_w_in, m_b_gate, m_conv_w, m_conv_b, m_dt_bias_f, m_dt_bias_b, m_a_log_f, m_a_log_b, m_d_skip, m_ssd_norm_w, m_w_proj_ssd, m_w_proj_attn, m_w_out, m_ln1_g, m_ln1_b, m_w_up, m_w_down, m_ln2_g, m_ln2_b, v_w_in, v_b_gate, v_conv_w, v_conv_b, v_dt_bias_f, v_dt_bias_b, v_a_log_f, v_a_log_b, v_d_skip, v_ssd_norm_w, v_w_proj_ssd, v_w_proj_attn, v_w_out, v_ln1_g, v_ln1_b, v_w_up, v_w_down, v_ln2_g, v_ln2_b):
    w = dict(w_in=w_in, b_gate=b_gate, conv_w=conv_w, conv_b=conv_b, dt_bias_f=dt_bias_f, dt_bias_b=dt_bias_b,
             a_log_f=a_log_f, a_log_b=a_log_b, d_skip=d_skip, ssd_norm_w=ssd_norm_w, w_proj_ssd=w_proj_ssd,
             w_proj_attn=w_proj_attn, w_out=w_out, ln1_g=ln1_g, ln1_b=ln1_b, w_up=w_up, w_down=w_down, ln2_g=ln2_g, ln2_b=ln2_b)
    m = dict(w_in=m_w_in, b_gate=m_b_gate, conv_w=m_conv_w, conv_b=m_conv_b, dt_bias_f=m_dt_bias_f, dt_bias_b=m_dt_bias_b,
             a_log_f=m_a_log_f, a_log_b=m_a_log_b, d_skip=m_d_skip, ssd_norm_w=m_ssd_norm_w, w_proj_ssd=m_w_proj_ssd,
             w_proj_attn=m_w_proj_attn, w_out=m_w_out, ln1_g=m_ln1_g, ln1_b=m_ln1_b, w_up=m_w_up, w_down=m_w_down,
             ln2_g=m_ln2_g, ln2_b=m_ln2_b)
    v = dict(w_in=v_w_in, b_gate=v_b_gate, conv_w=v_conv_w, conv_b=v_conv_b, dt_bias_f=v_dt_bias_f, dt_bias_b=v_dt_bias_b,
             a_log_f=v_a_log_f, a_log_b=v_a_log_b, d_skip=v_d_skip, ssd_norm_w=v_ssd_norm_w, w_proj_ssd=v_w_proj_ssd,
             w_proj_attn=v_w_proj_attn, w_out=v_w_out, ln1_g=v_ln1_g, ln1_b=v_ln1_b, w_up=v_w_up, w_down=v_w_down,
             ln2_g=v_ln2_g, ln2_b=v_ln2_b)
    xi, yi, ci = lax.axis_index("x"), lax.axis_index("y"), lax.axis_index("c")
    shard = 2 * xi + yi

    g_in, g_ps, g_pa, g_o, g_up, g_dn = _allgather_weights([w[n].astype(BF16) for n in BIG])
    w_in_full = jnp.concatenate([g_in[s] for s in range(4)], axis=1)
    wts = {"w_in_p": _perm_cols(w_in_full), "w_proj_ssd": g_ps.reshape(DI, D), "w_proj_attn": g_pa,
           "w_out": g_o.reshape(D, D), "w_up": g_up, "w_down": g_dn.reshape(DFF, D)}

    cw_slab = jnp.zeros((KCONV, 4, CONVD // 4), F32)
    cw_slab = lax.dynamic_update_slice(cw_slab, conv_w[:, None, :] * 0.5, (0, shard, 0))
    conv_w_all = _unpack(_allreduce_small(_pack([cw_slab])), [(KCONV, CONVD)])[0]

    sm = {n: w[n] for n in SMALL}
    sm["conv_w"] = conv_w_all
    dx, big, small = _local_grads(x[0], loss_target[0], wts, sm)

    names = list(SMALL) + ["loss"]
    shapes = [small[n].shape for n in names]
    red = dict(zip(names, _unpack(_allreduce_small(_pack([small[n] for n in names])), shapes)))
    loss = red["loss"].reshape(())
    gsm = {n: red[n] for n in SMALL}
    conv_w_grad_shard = lax.dynamic_slice_in_dim(gsm["conv_w"].reshape(KCONV, 4, CONVD // 4), shard, 1, axis=1)
    gsm["conv_w"] = conv_w_grad_shard.reshape(KCONV, CONVD // 4)

    c_idx = jnp.reshape(ci, (1,)).astype(jnp.int32)
    glist = [big[n] for n in BIG]
    xy_idx = jnp.stack([xi, yi, ci]).astype(jnp.int32)
    recv = _swap_halves(glist)
    halves = [_add_half(g, r, c_idx, f"rs_add_half_{n}") for g, r, n in zip(glist, recv, BIG)]
    recv_a, recv_b = _rs_step1([h[1] for h in halves])
    sums1 = [_rs_add1(h[0], ra, rb, xy_idx, f"rs_add1_{n}") for h, ra, rb, n in zip(halves, recv_a, recv_b, BIG)]
    recv_a2, recv_b2 = _rs_step2([s1[2] for s1 in sums1], [s1[3] for s1 in sums1])
    pieces = [_rs_add2(s1[0], s1[1], ra, rb, xy_idx, f"rs_add2_{n}")
              for s1, ra, rb, n in zip(sums1, recv_a2, recv_b2, BIG)]
    joined = _join_halves(pieces)
    gbig = {n: j.reshape(w[n].shape) for n, j in zip(BIG, joined)}

    grads, deltas, new_m, new_v = {}, {}, {}, {}
    for n in BIG:
        grads[n] = gbig[n]
        deltas[n], new_m[n], new_v[n] = _adamw(w[n], gbig[n], m[n], v[n], f"adamw_{n}")
    sshapes = [w[n].shape for n in SMALL]
    d_s, m_s, v_s = _adamw(_pack([w[n] for n in SMALL]), _pack([gsm[n] for n in SMALL]),
                           _pack([m[n] for n in SMALL]), _pack([v[n] for n in SMALL]), "adamw_small")
    for n, dd, mm, vv in zip(SMALL, _unpack(d_s, sshapes), _unpack(m_s, sshapes), _unpack(v_s, sshapes)):
        grads[n], deltas[n], new_m[n], new_v[n] = gsm[n], dd, mm, vv

    return (loss, dx[None], *[grads[n] for n in ORDER], *[deltas[n] for n in ORDER],
            *[new_m[n] for n in ORDER], *[new_v[n] for n in ORDER])
```

```python
import math

import jax
import numpy as np
import jax.numpy as jnp
from jax import lax
from jax.experimental import pallas as pl
from jax.experimental.pallas import tpu as pltpu

F32, BF16 = jnp.float32, jnp.bfloat16
MESH = pl.DeviceIdType.MESH

D = 1024
DI = 2048
NH = 32
HP = 64
NG = 4
NS = 128
Q = 128
CONVD = 3072
KCONV = 5
DFF = 4096
AH = 64
ATT_HALF = 64
DILATIONS = (1, 4, 16)
IN_COLS = 9536
OZ, OGATE, OXBC, OKV, OQ, ODT, UW = 0, 2048, 4096, 7168, 8704, 9472, 9728
ALPHA = 2.0 ** 0.25
NORM_EPS = 1e-5
ADAM_LR, ADAM_B1, ADAM_B2, ADAM_EPS, ADAM_WD, ADAM_STEP = 0.001, 0.9, 0.999, 1e-8, 0.01, 10
VMEM_LIMIT = 56 * 2 ** 20
NEG = -1e30


def _params(sem):
    return pltpu.CompilerParams(dimension_semantics=sem, vmem_limit_bytes=VMEM_LIMIT)


def _sigmoid(x):
    return 1.0 / (1.0 + jnp.exp(-x))


def _softplus(x):
    e = jnp.exp(-jnp.abs(x))
    small = e * (1.0 - e * (0.5 - e * (1.0 / 3.0)))
    return jnp.maximum(x, 0.0) + jnp.where(e < 0.01, small, jnp.log(1.0 + e))


def _split3(a):
    hi = a.astype(BF16)
    r = a - hi.astype(F32)
    mid = r.astype(BF16)
    lo = (r - mid.astype(F32)).astype(BF16)
    return hi, mid, lo


def _dot01(a, m01):
    hi, mid, lo = _split3(a)
    d = lambda p: jnp.dot(p, m01, preferred_element_type=F32)
    return d(hi) + d(mid) + d(lo)


def _dot01_l(m01, a):
    hi, mid, lo = _split3(a)
    d = lambda p: jnp.dot(m01, p, preferred_element_type=F32)
    return d(hi) + d(mid) + d(lo)


def _dot_nt(a, b):
    return lax.dot_general(a, b, (((1,), (1,)), ((), ())), preferred_element_type=F32)


def _iota(shape, dim):
    return lax.broadcasted_iota(jnp.int32, shape, dim)


def _mm_nn(a, b, *, tm, tn, name, out_dtype=F32):
    m, k = a.shape
    if b.ndim == 3:
        assert tn == b.shape[2]
        n = b.shape[0] * b.shape[2]
        b_spec = pl.BlockSpec((None, k, tn), lambda j, i: (j, 0, 0))
    else:
        n = b.shape[1]
        b_spec = pl.BlockSpec((k, tn), lambda j, i: (0, j))

    def body(a_ref, b_ref, o_ref):
        o_ref[...] = jnp.dot(a_ref[...].astype(BF16), b_ref[...], preferred_element_type=F32).astype(out_dtype)

    return pl.pallas_call(
        body, out_shape=jax.ShapeDtypeStruct((m, n), out_dtype), grid=(n // tn, m // tm),
        in_specs=[pl.BlockSpec((tm, k), lambda j, i: (i, 0)), b_spec],
        out_specs=pl.BlockSpec((tm, tn), lambda j, i: (i, j)),
        name=name, compiler_params=_params(("parallel", "parallel")))(a, b)


def _mm_nt(a, b, *, tm, tk, tc, name, add=None, add_scale=1.0):
    m, n = a.shape
    if b.ndim == 3:
        assert tc == b.shape[2]
        k, nc = b.shape[1], b.shape[0]
        b_spec = pl.BlockSpec((None, tk, tc), lambda j, i, c: (c, j, 0))
    else:
        k, nc = b.shape[0], n // tc
        b_spec = pl.BlockSpec((tk, tc), lambda j, i, c: (j, c))

    def body(*refs):
        if add is None:
            a_ref, b_ref, o_ref = refs
        else:
            a_ref, b_ref, add_ref, o_ref = refs
        c = pl.program_id(2)
        part = _dot_nt(a_ref[...].astype(BF16), b_ref[...])

        @pl.when(c == 0)
        def _():
            if add is None:
                o_ref[...] = part
            else:
                o_ref[...] = part + add_scale * add_ref[...]

        @pl.when(c > 0)
        def _():
            o_ref[...] += part

    in_specs = [pl.BlockSpec((tm, tc), lambda j, i, c: (i, c)), b_spec]
    args = [a, b]
    if add is not None:
        in_specs.append(pl.BlockSpec((tm, tk), lambda j, i, c: (i, j)))
        args.append(add)
    return pl.pallas_call(
        body, out_shape=jax.ShapeDtypeStruct((m, k), F32), grid=(k // tk, m // tm, nc),
        in_specs=in_specs, out_specs=pl.BlockSpec((tm, tk), lambda j, i, c: (i, j)),
        name=name, compiler_params=_params(("parallel", "parallel", "arbitrary")))(*args)


def _mm_tn(a, b, *, tka, tn, tt, name, out_shards=None):
    t, ka = a.shape
    n = b.shape[1]
    if out_shards:
        assert tn == n // out_shards
        out_shape = jax.ShapeDtypeStruct((out_shards, ka, tn), F32)
        o_spec = pl.BlockSpec((None, tka, tn), lambda i, j, s: (j, i, 0))
    else:
        out_shape = jax.ShapeDtypeStruct((ka, n), F32)
        o_spec = pl.BlockSpec((tka, tn), lambda i, j, s: (i, j))

    def body(a_ref, b_ref, o_ref):
        s = pl.program_id(2)
        part = lax.dot_general(a_ref[...].astype(BF16), b_ref[...].astype(BF16), (((0,), (0,)), ((), ())),
                               preferred_element_type=F32)

        @pl.when(s == 0)
        def _():
            o_ref[...] = part

        @pl.when(s > 0)
        def _():
            o_ref[...] += part

    return pl.pallas_call(
        body, out_shape=out_shape, grid=(ka // tka, n // tn, t // tt),
        in_specs=[pl.BlockSpec((tt, tka), lambda i, j, s: (s, i)), pl.BlockSpec((tt, tn), lambda i, j, s: (s, j))],
        out_specs=o_spec, name=name, compiler_params=_params(("parallel", "parallel", "arbitrary")))(a, b)


CONV_TM = 512
CONV_TC = 1024
CONV_RC = 64
CONV_CC = 256


def _halo_specs(t, tm, tc, col0):
    nb8 = t // 8
    r8 = tm // 8
    return [
        pl.BlockSpec((8, tc), lambda i, j: (jnp.maximum(i * r8 - 1, 0), col0 + j)),
        pl.BlockSpec((tm, tc), lambda i, j: (i, col0 + j)),
        pl.BlockSpec((8, tc), lambda i, j: (jnp.minimum((i + 1) * r8, nb8 - 1), col0 + j)),
    ]


def _fill_ext(ext, prev_ref, cur_ref, next_ref, tm, i, last):
    ext[0:8, :] = jnp.where(i > 0, prev_ref[...], 0.0)
    ext[8:8 + tm, :] = cur_ref[...]
    ext[8 + tm:16 + tm, :] = jnp.where(i < last, next_ref[...], 0.0)


def _conv_fwd(u, conv_w, conv_b):
    t = u.shape[0]
    tm, tc = CONV_TM, CONV_TC

    def body(prev_ref, cur_ref, next_ref, w_ref, b_ref, o_ref, ext):
        _fill_ext(ext, prev_ref, cur_ref, next_ref, tm, pl.program_id(0), t // tm - 1)
        for c0 in range(0, tc, CONV_CC):
            cs = slice(c0, c0 + CONV_CC)
            w = w_ref[:, cs]
            for r0 in range(0, tm, CONV_RC):
                acc = jnp.broadcast_to(b_ref[:, cs], (CONV_RC, CONV_CC))
                for k in range(KCONV):
                    acc = acc + w[k:k + 1, :] * ext[pl.ds(r0 + 6 + k, CONV_RC), cs]
                o_ref[r0:r0 + CONV_RC, cs] = acc * _sigmoid(acc)

    return pl.pallas_call(
        body, out_shape=jax.ShapeDtypeStruct((t, CONVD), F32), grid=(t // tm, CONVD // tc),
        in_specs=_halo_specs(t, tm, tc, OXBC // tc) + [
            pl.BlockSpec((KCONV, tc), lambda i, j: (0, j)), pl.BlockSpec((1, tc), lambda i, j: (0, j))],
        out_specs=pl.BlockSpec((tm, tc), lambda i, j: (i, j)),
        scratch_shapes=[pltpu.VMEM((tm + 16, tc), F32)],
        name="conv_fwd", compiler_params=_params(("parallel", "parallel")))(u, u, u, conv_w, conv_b)


def _conv_dpre(u, dxs_f, dxs_b, dy, dbc_f, dbc_b, dsk_row, conv_w, conv_b):
    t = u.shape[0]
    tm, tc = CONV_TM, CONV_TC
    r8 = tm // 8
    nb8 = t // 8
    c0 = OXBC // tc

    def body(uprev, ucur, unext, f_ref, b_ref, y_ref, cf_ref, cb_ref, dsk_ref, w_ref, bias_ref,
             dpre_ref, dw_ref, db_ref, ext):
        j = pl.program_id(0)
        i = pl.program_id(1)
        _fill_ext(ext, uprev, ucur, unext, tm, i, t // tm - 1)
        is_xs = j < 2
        dw_cols, db_cols = [], []
        for c0 in range(0, tc, CONV_CC):
            cs = slice(c0, c0 + CONV_CC)
            w = w_ref[:, cs]
            dsk = dsk_ref[:, cs]
            dw_acc = [jnp.zeros((1, CONV_CC), F32) for _ in range(KCONV)]
            db_acc = jnp.zeros((1, CONV_CC), F32)
            for r0 in range(0, tm, CONV_RC):
                rs = slice(r0, r0 + CONV_RC)
                taps = [ext[pl.ds(r0 + 6 + k, CONV_RC), cs] for k in range(KCONV)]
                pre = jnp.broadcast_to(bias_ref[:, cs], (CONV_RC, CONV_CC))
                for k in range(KCONV):
                    pre = pre + w[k:k + 1, :] * taps[k]
                s = _sigmoid(pre)
                xs_part = f_ref[rs, cs] + b_ref[rs, cs] + dsk * y_ref[rs, cs]
                up = jnp.where(is_xs, xs_part, cf_ref[rs, cs] + cb_ref[rs, cs])
                dpre = up * (s * (1.0 + pre * (1.0 - s)))
                dpre_ref[rs, cs] = dpre
                for k in range(KCONV):
                    dw_acc[k] = dw_acc[k] + jnp.sum(dpre * taps[k], axis=0, keepdims=True)
                db_acc = db_acc + jnp.sum(dpre, axis=0, keepdims=True)
            dw_cols.append(jnp.concatenate(dw_acc + [jnp.zeros((8 - KCONV, CONV_CC), F32)], axis=0))
            db_cols.append(jnp.broadcast_to(db_acc, (8, CONV_CC)))
        dw_part = jnp.concatenate(dw_cols, axis=1)
        db_part = jnp.concatenate(db_cols, axis=1)

        @pl.when(i == 0)
        def _():
            dw_ref[...] = dw_part
            db_ref[...] = db_part

        @pl.when(i > 0)
        def _():
            dw_ref[...] += dw_part
            db_ref[...] += db_part

    xs_spec = pl.BlockSpec((tm, tc), lambda j, i: (jnp.where(j < 2, i, 0), jnp.minimum(j, 1)))
    bc_spec = pl.BlockSpec((tm, tc), lambda j, i: (jnp.where(j == 2, i, 0), 0))
    in_specs = [
        pl.BlockSpec((8, tc), lambda j, i: (jnp.maximum(i * r8 - 1, 0), c0 + j)),
        pl.BlockSpec((tm, tc), lambda j, i: (i, c0 + j)),
        pl.BlockSpec((8, tc), lambda j, i: (jnp.minimum((i + 1) * r8, nb8 - 1), c0 + j)),
        xs_spec, xs_spec, xs_spec, bc_spec, bc_spec,
        pl.BlockSpec((1, tc), lambda j, i: (0, jnp.minimum(j, 1))),
        pl.BlockSpec((KCONV, tc), lambda j, i: (0, j)), pl.BlockSpec((1, tc), lambda j, i: (0, j)),
    ]
    return pl.pallas_call(
        body,
        out_shape=(jax.ShapeDtypeStruct((t, CONVD), F32), jax.ShapeDtypeStruct((8, CONVD), F32),
                   jax.ShapeDtypeStruct((8, CONVD), F32)),
        grid=(CONVD // tc, t // tm), in_specs=in_specs,
        out_specs=(pl.BlockSpec((tm, tc), lambda j, i: (i, j)),
                   pl.BlockSpec((8, tc), lambda j, i: (0, j)), pl.BlockSpec((8, tc), lambda j, i: (0, j))),
        scratch_shapes=[pltpu.VMEM((tm + 16, tc), F32)],
        name="conv_dpre", compiler_params=_params(("parallel", "arbitrary")))(
            u, u, u, dxs_f, dxs_b, dy, dbc_f, dbc_b, dsk_row, conv_w, conv_b)


def _conv_dx(du, dpre, conv_w):
    t = dpre.shape[0]
    tm, tc = CONV_TM, CONV_TC
    r8 = tm // 8
    nb8 = t // 8

    def body(prev_ref, cur_ref, next_ref, w_ref, du_in, du_out, ext):
        del du_in
        _fill_ext(ext, prev_ref, cur_ref, next_ref, tm, pl.program_id(1), t // tm - 1)
        for c0 in range(0, tc, CONV_CC):
            cs = slice(c0, c0 + CONV_CC)
            w = w_ref[:, cs]
            for r0 in range(0, tm, CONV_RC):
                acc = jnp.zeros((CONV_RC, CONV_CC), F32)
                for k in range(KCONV):
                    acc = acc + w[k:k + 1, :] * ext[pl.ds(r0 + 10 - k, CONV_RC), cs]
                du_out[r0:r0 + CONV_RC, cs] = acc.astype(du_out.dtype)

    in_specs = [
        pl.BlockSpec((8, tc), lambda j, i: (jnp.maximum(i * r8 - 1, 0), j)),
        pl.BlockSpec((tm, tc), lambda j, i: (i, j)),
        pl.BlockSpec((8, tc), lambda j, i: (jnp.minimum((i + 1) * r8, nb8 - 1), j)),
        pl.BlockSpec((KCONV, tc), lambda j, i: (0, j)),
        pl.BlockSpec(memory_space=pl.ANY),
    ]
    return pl.pallas_call(
        body, out_shape=jax.ShapeDtypeStruct(du.shape, du.dtype), grid=(CONVD // tc, t // tm), in_specs=in_specs,
        out_specs=pl.BlockSpec((tm, tc), lambda j, i: (i, OXBC // tc + j)),
        scratch_shapes=[pltpu.VMEM((tm + 16, tc), F32)], input_output_aliases={4: 0},
        name="conv_dx", compiler_params=_params(("parallel", "parallel")))(dpre, dpre, dpre, conv_w, du)


def _ssd_common(dtr_ref, par_ref, rev):
    raw = dtr_ref[...]
    lane = _iota((1, 128), 1)
    mine = (lane >= 32 * rev) & (lane < 32 * rev + 32)
    bias = par_ref[0:1, :]
    arow = jnp.where(mine, -jnp.exp(par_ref[1:2, :]), 0.0)
    dt = _softplus(raw + bias)
    a = dt * arow
    ri = _iota((Q, Q), 0)
    ci = _iota((Q, Q), 1)
    tri = (ci >= ri) if rev else (ci <= ri)
    trit = (ci <= ri) if rev else (ci >= ri)
    cs = _dot01_l(tri.astype(BF16), a)
    return raw, bias, arow, mine, dt, cs, tri, trit


def _expand_mat(rev):
    r = np.arange(128)[:, None]
    c = np.arange(DI)[None, :]
    return jnp.asarray(r == (c // HP) + 32 * rev, BF16)


def _sum_mat(rev):
    r = np.arange(DI)[:, None]
    c = np.arange(128)[None, :]
    return jnp.asarray(c == (r // HP) + 32 * rev, BF16)


def _ssd_fwd(xbc, u, par, y_add=None, *, rev):
    t = xbc.shape[0]
    nc = t // Q
    end = 0 if rev else Q - 1
    cmap = (lambda c: nc - 1 - c) if rev else (lambda c: c)

    def body(xbc_ref, dtr_ref, par_ref, ex_ref, *rest):
        yadd_ref = rest[0] if y_add is not None else None
        y_ref, st_ref, h_scr = rest[-3:]
        step = pl.program_id(0)

        @pl.when(step == 0)
        def _():
            h_scr[...] = jnp.zeros((NS, DI), F32)

        raw, bias, arow, mine, dt, cs, tri, trit = _ssd_common(dtr_ref, par_ref, rev)
        cst = cs.T
        dtt = dt.T
        tot_col = cst[:, end:end + 1]
        wt = dtt * jnp.exp(tot_col - cst)
        gam = jnp.exp(cs[end:end + 1, :])
        gam_x = _dot01(jnp.broadcast_to(gam, (8, 128)), ex_ref[...])[0:1, :]
        lane = _iota((Q, 128), 1)
        sel = lane < HP
        st_ref[...] = h_scr[...]
        for g in range(NG):
            bg = xbc_ref[:, DI + NS * g:DI + NS * (g + 1)]
            cg = xbc_ref[:, DI + NG * NS + NS * g:DI + NG * NS + NS * (g + 1)]
            cb = _dot_nt(cg.astype(BF16), bg.astype(BF16))
            bt = bg.T
            for k in range(4):
                lo = 512 * g + 128 * k
                xp = xbc_ref[:, lo:lo + 128].astype(BF16)
                hp = h_scr[:, lo:lo + 128]
                rhs = jnp.concatenate([xp, hp.astype(BF16)], axis=0)
                lhs, bts = [], []
                for j in range(2):
                    hc = 8 * g + 2 * k + j + 32 * rev
                    csc = jnp.broadcast_to(cs[:, hc:hc + 1], (Q, Q))
                    lm = jnp.exp(jnp.where(tri, csc - cst[hc:hc + 1, :], NEG)) * dtt[hc:hc + 1, :]
                    mh = (cb * lm).astype(BF16)
                    ec = (jnp.exp(csc) * cg).astype(BF16)
                    lhs.append(jnp.concatenate([mh, ec], axis=1))
                    bts.append((bt * wt[hc:hc + 1, :]).astype(BF16))
                ys = jnp.dot(jnp.concatenate(lhs, axis=0), rhs, preferred_element_type=F32)
                ss = jnp.dot(jnp.concatenate(bts, axis=0), xp, preferred_element_type=F32)
                yp = jnp.where(sel, ys[0:Q], ys[Q:2 * Q])
                y_ref[:, lo:lo + 128] = yp if yadd_ref is None else yp + yadd_ref[:, lo:lo + 128]
                h_scr[:, lo:lo + 128] = gam_x[:, lo:lo + 128] * hp + jnp.where(sel, ss[0:NS], ss[NS:2 * NS])

    return pl.pallas_call(
        body,
        out_shape=(jax.ShapeDtypeStruct((t, DI), F32), jax.ShapeDtypeStruct((nc, NS, DI), F32)),
        grid=(nc,),
        in_specs=[pl.BlockSpec((Q, CONVD), lambda c: (cmap(c), 0)),
                  pl.BlockSpec((Q, 128), lambda c: (cmap(c), ODT // 128)),
                  pl.BlockSpec((8, 128), lambda c: (0, 0)),
                  pl.BlockSpec((128, DI), lambda c: (0, 0))]
        + ([pl.BlockSpec((Q, DI), lambda c: (cmap(c), 0))] if y_add is not None else []),
        out_specs=(pl.BlockSpec((Q, DI), lambda c: (cmap(c), 0)),
                   pl.BlockSpec((None, NS, DI), lambda c: (cmap(c), 0, 0))),
        scratch_shapes=[pltpu.VMEM((NS, DI), F32)],
        name="ssd_fwd_rev" if rev else "ssd_fwd", compiler_params=_params(("arbitrary",)))(
            xbc, u, par, _expand_mat(rev), *([y_add] if y_add is not None else []))


def _ssd_bwd(xbc, u, par, dy, st, *, rev):
    t = xbc.shape[0]
    nc = t // Q
    end = 0 if rev else Q - 1
    cmap = (lambda c: c) if rev else (lambda c: nc - 1 - c)

    def body(xbc_ref, dtr_ref, par_ref, dy_ref, hin_ref, ex_ref, sm_ref, dxs_ref, dbc_ref, ddt_ref, acc_ref, dh_scr):
        step = pl.program_id(0)

        @pl.when(step == 0)
        def _():
            dh_scr[...] = jnp.zeros((NS, DI), F32)

        raw, bias, arow, mine, dt, cs, tri, trit = _ssd_common(dtr_ref, par_ref, rev)
        ri = _iota((Q, Q), 0)
        ci = _iota((Q, Q), 1)
        stri = ((ri > ci) if rev else (ri < ci)).astype(BF16)
        strit = ((ci > ri) if rev else (ci < ri)).astype(BF16)
        cst = cs.T
        dtt = dt.T
        et = jnp.exp(cst)
        expand = ex_ref[...]
        summat = sm_ref[...]
        gam = jnp.exp(cs[end:end + 1, :])
        gam_x = _dot01(jnp.broadcast_to(gam, (8, 128)), expand)[0:1, :]
        dt_hi, dt_mid, _ = _split3(dt)
        dtx = (jnp.dot(dt_hi, expand, preferred_element_type=F32)
               + jnp.dot(dt_mid, expand, preferred_element_type=F32))
        lane = _iota((Q, 128), 1)
        sel = lane < HP
        dho = dh_scr[...]
        t3 = jnp.sum(dho * hin_ref[...], axis=0, keepdims=True) * gam_x
        dxs_cols, dxs2_cols, yoff_cols, a1_rows = [], [], [], []
        for g in range(NG):
            bg = xbc_ref[:, DI + NS * g:DI + NS * (g + 1)]
            cg = xbc_ref[:, DI + NG * NS + NS * g:DI + NG * NS + NS * (g + 1)]
            bb = bg.astype(BF16)
            cbf = cg.astype(BF16)
            cb = _dot_nt(cbf, bb)
            cbt = _dot_nt(bb, cbf)
            ct = cg.T
            bdh = jnp.dot(bb, dho[:, 512 * g:512 * (g + 1)].astype(BF16), preferred_element_type=F32)
            dcb = jnp.zeros((Q, Q), F32)
            dcg = jnp.zeros((Q, NS), F32)
            dbg = jnp.zeros((Q, NS), F32)
            for k in range(4):
                lo = 512 * g + 128 * k
                xpf = xbc_ref[:, lo:lo + 128]
                xp = xpf.astype(BF16)
                dyp = dy_ref[:, lo:lo + 128]
                dypb = dyp.astype(BF16)
                hinp = hin_ref[:, lo:lo + 128].astype(BF16)
                dhp = dho[:, lo:lo + 128]
                es, ws, lmds, mts, ctes, dyms, ecbs = [], [], [], [], [], [], []
                for j in range(2):
                    hc = 8 * g + 2 * k + j + 32 * rev
                    csc = jnp.broadcast_to(cs[:, hc:hc + 1], (Q, Q))
                    csr = cst[hc:hc + 1, :]
                    lmds.append(jnp.exp(jnp.where(tri, csc - csr, NEG)) * dtt[hc:hc + 1, :])
                    lmb = jnp.exp(jnp.where(trit, csr - csc, NEG))
                    mts.append((cbt * lmb).astype(BF16))
                    dyms.append(jnp.where(sel if j == 0 else ~sel, dyp, 0.0).astype(BF16))
                    ecs = jnp.exp(csc)
                    es.append(ecs)
                    ws.append(jnp.exp(cst[hc:hc + 1, end:end + 1] - csc))
                    ecbs.append((ecs * cg).astype(BF16))
                    ctes.append((ct * et[hc:hc + 1, :]).astype(BF16))
                by_dy = jnp.dot(jnp.concatenate(mts + ctes, axis=0), dypb, preferred_element_type=F32)
                dmm = _dot_nt(jnp.concatenate(dyms, axis=0), xp)
                dm0, dm1 = dmm[0:Q] * lmds[0], dmm[Q:2 * Q] * lmds[1]
                dcb = dcb + dm0 + dm1
                rr = jnp.dot(jnp.concatenate([dm0 * cb, dm1 * cb], axis=0).astype(BF16), stri, preferred_element_type=F32)
                a1_rows.append(jnp.sum(jnp.where(tri, rr[0:Q], 0.0), axis=0, keepdims=True))
                a1_rows.append(jnp.sum(jnp.where(tri, rr[Q:2 * Q], 0.0), axis=0, keepdims=True))
                yo = jnp.dot(jnp.concatenate(ecbs, axis=0), hinp, preferred_element_type=F32)
                e_p = jnp.where(sel, es[0], es[1])
                w_p = jnp.where(sel, ws[0], ws[1])
                d2 = w_p * bdh[:, 128 * k:128 * (k + 1)]
                dxs2_cols.append(d2)
                dxs_cols.append(jnp.where(sel, by_dy[0:Q], by_dy[Q:2 * Q]) + d2)
                yoff_cols.append(jnp.where(sel, yo[0:Q], yo[Q:2 * Q]))
                dcg = dcg + _dot_nt((e_p * dyp).astype(BF16), hinp)
                dbg = dbg + _dot_nt((w_p * dtx[:, lo:lo + 128] * xpf).astype(BF16), dhp.astype(BF16))
                dh_scr[:, lo:lo + 128] = (gam_x[:, lo:lo + 128] * dhp
                                          + jnp.where(sel, by_dy[2 * Q:3 * Q], by_dy[3 * Q:4 * Q]))
            dcg = dcg + jnp.dot(dcb.astype(BF16), bb, preferred_element_type=F32)
            dbg = dbg + jnp.dot(dcb.T.astype(BF16), cbf, preferred_element_type=F32)
            dbc_ref[:, NS * g:NS * (g + 1)] = dbg
            dbc_ref[:, NG * NS + NS * g:NG * NS + NS * (g + 1)] = dcg
        dxs = jnp.concatenate(dxs_cols, axis=1)
        dxs_ref[...] = dxs * dtx
        xs = xbc_ref[:, 0:DI]
        stacked = jnp.concatenate([xs * dxs, xs * jnp.concatenate(dxs2_cols, axis=1),
                                   dy_ref[...] * jnp.concatenate(yoff_cols, axis=1),
                                   jnp.broadcast_to(t3, (8, DI))], axis=0).astype(BF16)
        sums = jnp.dot(stacked, summat, preferred_element_type=F32)
        rx, rx2, ryo, c0 = sums[0:Q], sums[Q:2 * Q], sums[2 * Q:3 * Q], sums[3 * Q:3 * Q + 1]
        zero32 = jnp.zeros((32, Q), F32)
        a1t = jnp.concatenate(([zero32] if rev else []) + a1_rows + [zero32] * (2 if rev else 3), axis=0)
        da = (a1t.T + jnp.dot(trit.astype(BF16), ryo.astype(BF16), preferred_element_type=F32)
              + jnp.dot(strit, (dt * rx2).astype(BF16), preferred_element_type=F32) + jnp.where(mine, c0, 0.0))
        ddt = rx + da * arow
        ddtr = ddt * _sigmoid(raw + bias)
        ddt_ref[...] = ddtr
        part = jnp.concatenate([jnp.sum(ddtr, axis=0, keepdims=True),
                                jnp.sum(da * dt, axis=0, keepdims=True) * arow,
                                jnp.zeros((6, 128), F32)], axis=0)

        @pl.when(step == 0)
        def _():
            acc_ref[...] = part

        @pl.when(step > 0)
        def _():
            acc_ref[...] += part

    return pl.pallas_call(
        body,
        out_shape=(jax.ShapeDtypeStruct((t, DI), F32), jax.ShapeDtypeStruct((t, 2 * NG * NS), F32),
                   jax.ShapeDtypeStruct((t, 128), F32), jax.ShapeDtypeStruct((8, 128), F32)),
        grid=(nc,),
        in_specs=[pl.BlockSpec((Q, CONVD), lambda c: (cmap(c), 0)),
                  pl.BlockSpec((Q, 128), lambda c: (cmap(c), ODT // 128)),
                  pl.BlockSpec((8, 128), lambda c: (0, 0)),
                  pl.BlockSpec((Q, DI), lambda c: (cmap(c), 0)),
                  pl.BlockSpec((None, NS, DI), lambda c: (cmap(c), 0, 0)),
                  pl.BlockSpec((128, DI), lambda c: (0, 0)), pl.BlockSpec((DI, 128), lambda c: (0, 0))],
        out_specs=(pl.BlockSpec((Q, DI), lambda c: (cmap(c), 0)),
                   pl.BlockSpec((Q, 2 * NG * NS), lambda c: (cmap(c), 0)),
                   pl.BlockSpec((Q, 128), lambda c: (cmap(c), 0)),
                   pl.BlockSpec((8, 128), lambda c: (0, 0))),
        scratch_shapes=[pltpu.VMEM((NS, DI), F32)],
        name="ssd_bwd_rev" if rev else "ssd_bwd", compiler_params=_params(("arbitrary",)))(
            xbc, u, par, dy, st, _expand_mat(rev), _sum_mat(rev))


GN_TM = 256
GN_GROUP = DI // NG


def _gn_forward_vals(y0, xs, z, dsk):
    y = y0 + dsk * xs
    sz = _sigmoid(z)
    gate = z * sz
    y2 = y * gate
    parts, rs = [], []
    for g in range(NG):
        seg = y2[:, GN_GROUP * g:GN_GROUP * (g + 1)]
        r = lax.rsqrt(jnp.mean(seg * seg, axis=1, keepdims=True) + NORM_EPS)
        rs.append(r)
        parts.append(seg * r)
    yn = jnp.concatenate(parts, axis=1)
    return y, sz, gate, yn, rs


def _gatenorm_fwd(y_fb, xbc, u, dsk_row, nw_row):
    t = y_fb.shape[0]
    tm = GN_TM

    def body(y_ref, xs_ref, z_ref, dsk_ref, nw_ref, o_ref):
        _, _, _, yn, _ = _gn_forward_vals(y_ref[...], xs_ref[...], z_ref[...], dsk_ref[...])
        o_ref[...] = (yn * nw_ref[...]).astype(BF16)

    blk = pl.BlockSpec((tm, DI), lambda i: (i, 0))
    row = pl.BlockSpec((1, DI), lambda i: (0, 0))
    return pl.pallas_call(
        body, out_shape=jax.ShapeDtypeStruct((t, DI), BF16), grid=(t // tm,),
        in_specs=[blk, blk, pl.BlockSpec((tm, DI), lambda i: (i, OZ // DI)), row, row],
        out_specs=blk, name="gatenorm_fwd", compiler_params=_params(("parallel",)))(y_fb, xbc, u, dsk_row, nw_row)


def _gatenorm_bwd(ds_out, y_fb, xbc, u, du, dsk_row, nw_row):
    t = y_fb.shape[0]
    tm = GN_TM

    def body(ds_ref, y_ref, xs_ref, z_ref, dsk_ref, nw_ref, sm_ref, du_in, dy_ref, du_out, dnw_ref, dds_ref):
        del du_in
        i = pl.program_id(0)
        xs = xs_ref[...]
        z = z_ref[...]
        y, sz, gate, yn, rs = _gn_forward_vals(y_ref[...], xs, z, dsk_ref[...])
        ds = ds_ref[...]
        gsc = ds * nw_ref[...]
        parts = []
        for g in range(NG):
            sl = slice(GN_GROUP * g, GN_GROUP * (g + 1))
            m = jnp.mean(gsc[:, sl] * yn[:, sl], axis=1, keepdims=True)
            parts.append(rs[g] * (gsc[:, sl] - yn[:, sl] * m))
        dy2 = jnp.concatenate(parts, axis=1)
        dy = dy2 * gate
        dy_ref[...] = dy
        du_out[...] = (dy2 * y * (sz * (1.0 + z * (1.0 - sz)))).astype(du_out.dtype)
        dnw = jnp.broadcast_to(jnp.sum(ds * yn, axis=0, keepdims=True), (8, DI))
        drow = jnp.broadcast_to(jnp.sum(dy * xs, axis=0, keepdims=True), (8, DI))
        dds = _dot01(drow, sm_ref[...])

        @pl.when(i == 0)
        def _():
            dnw_ref[...] = dnw
            dds_ref[...] = dds

        @pl.when(i > 0)
        def _():
            dnw_ref[...] += dnw
            dds_ref[...] += dds

    blk = pl.BlockSpec((tm, DI), lambda i: (i, 0))
    row = pl.BlockSpec((1, DI), lambda i: (0, 0))
    return pl.pallas_call(
        body,
        out_shape=(jax.ShapeDtypeStruct((t, DI), F32), jax.ShapeDtypeStruct(du.shape, du.dtype),
                   jax.ShapeDtypeStruct((8, DI), F32), jax.ShapeDtypeStruct((8, 128), F32)),
        grid=(t // tm,),
        in_specs=[blk, blk, blk, pl.BlockSpec((tm, DI), lambda i: (i, OZ // DI)), row, row,
                  pl.BlockSpec((DI, 128), lambda i: (0, 0)), pl.BlockSpec(memory_space=pl.ANY)],
        out_specs=(blk, pl.BlockSpec((tm, DI), lambda i: (i, OZ // DI)),
                   pl.BlockSpec((8, DI), lambda i: (0, 0)), pl.BlockSpec((8, 128), lambda i: (0, 0))),
        input_output_aliases={7: 1},
        name="gatenorm_bwd", compiler_params=_params(("arbitrary",)))(
            ds_out, y_fb, xbc, u, dsk_row, nw_row, _sum_mat(0), du)


AT_B = 128
AT_W = AT_B + 2 * ATT_HALF
AT_L = 2 * AH
SCALE = 1.0 / math.sqrt(AH)


def _slope(g, hh):
    return 2.0 ** (-8.0 * (4 * g + hh + 1) / 12.0)


def _qcol(g):
    return lambda p: OQ // AT_L + 2 * g + p


def _kcol(g):
    return lambda p: OKV // AT_L + 4 * g + 2 * p


def _vcol(g):
    return lambda p: OKV // AT_L + 4 * g + 2 * p + 1


def _pcol(p):
    return p


def _sub(d):
    return 4 if d == 1 else 1


def _win_specs(col, t, d):
    tb, hb = AT_B * d * _sub(d), ATT_HALF * d
    per = tb // hb
    nh = t // hb
    return [
        pl.BlockSpec((hb, AT_L), lambda p, i: (jnp.maximum(per * i - 1, 0), col(p))),
        pl.BlockSpec((tb, AT_L), lambda p, i: (i, col(p))),
        pl.BlockSpec((hb, AT_L), lambda p, i: (jnp.minimum(per * (i + 1), nh - 1), col(p))),
    ]


def _blk_spec(col, d):
    return pl.BlockSpec((AT_B * d * _sub(d), AT_L), lambda p, i: (i, col(p)))


def _rows(ref, r, s, d):
    return ref[pl.ds(r, AT_B, stride=d), :] if d > 1 else ref[AT_B * s:AT_B * (s + 1), :]


def _win(p_ref, c_ref, n_ref, r, s, d):
    if d > 1:
        return jnp.concatenate([p_ref[pl.ds(r, ATT_HALF, stride=d), :], c_ref[pl.ds(r, AT_B, stride=d), :],
                                n_ref[pl.ds(r, ATT_HALF, stride=d), :]], axis=0)
    if s == 0:
        return jnp.concatenate([p_ref[...], c_ref[0:AT_B + ATT_HALF, :]], axis=0)
    if s == _sub(d) - 1:
        return jnp.concatenate([c_ref[AT_B * s - ATT_HALF:AT_B * (s + 1), :], n_ref[...]], axis=0)
    return c_ref[AT_B * s - ATT_HALF:AT_B * (s + 1) + ATT_HALF, :]


def _put_rows(ref, r, s, d, val):
    if d > 1:
        ref[pl.ds(r, AT_B, stride=d), :] = val
    else:
        ref[AT_B * s:AT_B * (s + 1), :] = val


def _for_blocks(d, fn):
    if d == 1:
        for s in range(_sub(d)):
            fn(0, s)
    else:
        def step(r, c):
            fn(r, 0)
            return c
        lax.fori_loop(0, d, step, 0, unroll=4)


def _attn_bias(blk, ln, d, g, p_id):
    a = blk * AT_B + _iota((AT_B, AT_W), 0)
    b = blk * AT_B - ATT_HALF + _iota((AT_B, AT_W), 1)
    rel = jnp.abs(a - b)
    valid = (rel <= ATT_HALF) & (b >= 0) & (b < ln)
    dist = (rel * d).astype(F32)
    out = []
    for hh in range(2):
        slope = jnp.where(p_id == 0, _slope(g, hh), _slope(g, 2 + hh))
        out.append(jnp.where(valid, -slope * dist, NEG))
    return out


def _attn_fwd(u, g):
    t = u.shape[0]
    d = DILATIONS[g]
    ln = t // d

    def body(q_ref, kp, kc, kn, vp, vc, vn, o_ref, l_ref):
        p_id = pl.program_id(0)
        i = pl.program_id(1)
        lane = _iota((AT_B, AT_L), 1)
        biases = [_attn_bias(i * _sub(d) + s, ln, d, g, p_id) for s in range(_sub(d))]

        def one(r, s):
            q = _rows(q_ref, r, s, d)
            kw = _win(kp, kc, kn, r, s, d).astype(BF16)
            vw = _win(vp, vc, vn, r, s, d).astype(BF16)
            o = jnp.zeros((AT_B, AT_L), F32)
            lse = jnp.zeros((AT_B, AT_L), F32)
            for hh in range(2):
                hm = (lane // AH) == hh
                qm = jnp.where(hm, q, 0.0).astype(BF16)
                sc = _dot_nt(qm, kw) * SCALE + biases[s][hh]
                m = jnp.max(sc, axis=1, keepdims=True)
                pr = jnp.exp(sc - m)
                den = jnp.sum(pr, axis=1, keepdims=True)
                oh = jnp.dot(pr.astype(BF16), vw, preferred_element_type=F32)
                o = jnp.where(hm, oh / den, o)
                lse = jnp.where(hm, m + jnp.log(den), lse)
            _put_rows(o_ref, r, s, d, o)
            _put_rows(l_ref, r, s, d, lse)

        _for_blocks(d, one)

    oshape = jax.ShapeDtypeStruct((t, 2 * AT_L), F32)
    ospec = _blk_spec(_pcol, d)
    return pl.pallas_call(
        body, out_shape=(oshape, oshape), grid=(2, t // (AT_B * d * _sub(d))),
        in_specs=[_blk_spec(_qcol(g), d)] + _win_specs(_kcol(g), t, d) + _win_specs(_vcol(g), t, d),
        out_specs=(ospec, ospec), name=f"attn_fwd_{g}", compiler_params=_params(("parallel", "parallel")))(
            u, u, u, u, u, u, u)


def _attn_dq(u, du, do, lse, e, g):
    t = u.shape[0]
    d = DILATIONS[g]
    ln = t // d

    def body(q_ref, kp, kc, kn, vp, vc, vn, do_ref, l_ref, e_ref, du_in, dq_ref, dq_scr):
        del du_in
        p_id = pl.program_id(0)
        i = pl.program_id(1)
        lane = _iota((AT_B, AT_L), 1)
        biases = [_attn_bias(i * _sub(d) + s, ln, d, g, p_id) for s in range(_sub(d))]

        def one(r, s):
            q = _rows(q_ref, r, s, d)
            kw = _win(kp, kc, kn, r, s, d).astype(BF16)
            vw = _win(vp, vc, vn, r, s, d).astype(BF16)
            do_ = _rows(do_ref, r, s, d)
            lv = _rows(l_ref, r, s, d)
            ev = _rows(e_ref, r, s, d)
            dq = jnp.zeros((AT_B, AT_L), F32)
            for hh in range(2):
                hm = (lane // AH) == hh
                qm = jnp.where(hm, q, 0.0).astype(BF16)
                sc = _dot_nt(qm, kw) * SCALE + biases[s][hh]
                lcol = jnp.broadcast_to(lv[:, AH * hh:AH * hh + 1], (AT_B, AT_W))
                ecol = jnp.broadcast_to(ev[:, AH * hh:AH * hh + 1], (AT_B, AT_W))
                pr = jnp.exp(sc - lcol)
                dom = jnp.where(hm, do_, 0.0).astype(BF16)
                ds = pr * (_dot_nt(dom, vw) + ecol)
                dqh = jnp.dot(ds.astype(BF16), kw, preferred_element_type=F32) * SCALE
                dq = jnp.where(hm, dqh, dq)
            _put_rows(dq_scr, r, s, d, dq)

        _for_blocks(d, one)
        dq_ref[...] = dq_scr[...].astype(dq_ref.dtype)

    rspec = _blk_spec(_pcol, d)
    return pl.pallas_call(
        body, out_shape=jax.ShapeDtypeStruct(du.shape, du.dtype), grid=(2, t // (AT_B * d * _sub(d))),
        in_specs=[_blk_spec(_qcol(g), d)] + _win_specs(_kcol(g), t, d) + _win_specs(_vcol(g), t, d)
        + [rspec, rspec, rspec, pl.BlockSpec(memory_space=pl.ANY)],
        out_specs=_blk_spec(_qcol(g), d), input_output_aliases={10: 0},
        scratch_shapes=[pltpu.VMEM((AT_B * d * _sub(d), AT_L), F32)],
        name=f"attn_dq_{g}", compiler_params=_params(("parallel", "parallel")))(
            u, u, u, u, u, u, u, do, lse, e, du)


def _attn_dkv(u, du, do, lse, e, g):
    t = u.shape[0]
    d = DILATIONS[g]
    ln = t // d

    def body(k_ref, v_ref, qp, qc, qn, dp_, dc_, dn_, lp, lc, ln_, ep, ec, en, du_in, dkv_ref, dk_scr, dv_scr):
        del du_in
        p_id = pl.program_id(0)
        jb = pl.program_id(1)
        lane = _iota((AT_B, AT_L), 1)
        biases = [_attn_bias(jb * _sub(d) + s, ln, d, g, p_id) for s in range(_sub(d))]

        def one(r, s):
            k = _rows(k_ref, r, s, d)
            v = _rows(v_ref, r, s, d)
            qw = _win(qp, qc, qn, r, s, d).astype(BF16)
            dow = _win(dp_, dc_, dn_, r, s, d).astype(BF16)
            lt = _win(lp, lc, ln_, r, s, d).T
            et = _win(ep, ec, en, r, s, d).T
            dk = jnp.zeros((AT_B, AT_L), F32)
            dv = jnp.zeros((AT_B, AT_L), F32)
            for hh in range(2):
                hm = (lane // AH) == hh
                km = jnp.where(hm, k, 0.0).astype(BF16)
                st = _dot_nt(km, qw) * SCALE + biases[s][hh]
                pt = jnp.exp(st - lt[AH * hh:AH * hh + 1, :])
                dvh = jnp.dot(pt.astype(BF16), dow, preferred_element_type=F32)
                vm = jnp.where(hm, v, 0.0).astype(BF16)
                dst = pt * (_dot_nt(vm, dow) + et[AH * hh:AH * hh + 1, :])
                dkh = jnp.dot(dst.astype(BF16), qw, preferred_element_type=F32) * SCALE
                dk = jnp.where(hm, dkh, dk)
                dv = jnp.where(hm, dvh, dv)
            _put_rows(dk_scr, r, s, d, dk)
            _put_rows(dv_scr, r, s, d, dv)

        _for_blocks(d, one)
        dkv_ref[:, 0:AT_L] = dk_scr[...].astype(dkv_ref.dtype)
        dkv_ref[:, AT_L:2 * AT_L] = dv_scr[...].astype(dkv_ref.dtype)

    return pl.pallas_call(
        body, out_shape=jax.ShapeDtypeStruct(du.shape, du.dtype), grid=(2, t // (AT_B * d * _sub(d))),
        in_specs=[_blk_spec(_kcol(g), d), _blk_spec(_vcol(g), d)]
        + _win_specs(_qcol(g), t, d) + _win_specs(_pcol, t, d) + _win_specs(_pcol, t, d) + _win_specs(_pcol, t, d)
        + [pl.BlockSpec(memory_space=pl.ANY)],
        out_specs=pl.BlockSpec((AT_B * d * _sub(d), 2 * AT_L), lambda p, i: (i, OKV // (2 * AT_L) + 2 * g + p)),
        input_output_aliases={14: 0},
        scratch_shapes=[pltpu.VMEM((AT_B * d * _sub(d), AT_L), F32), pltpu.VMEM((AT_B * d * _sub(d), AT_L), F32)],
        name=f"attn_dkv_{g}", compiler_params=_params(("parallel", "parallel")))(
            u, u, u, u, u, do, do, do, lse, lse, lse, e, e, e, du)


CMB_TM = 1024


def _combine_weights(l0, l1, l2):
    m = jnp.maximum(jnp.maximum(l0, l1), l2)
    e0, e1, e2 = jnp.exp(l0 - m), jnp.exp(l1 - m), jnp.exp(l2 - m)
    inv = 1.0 / (e0 + e1 + e2)
    return e0 * inv, e1 * inv, e2 * inv


def _combine_fwd(os_, ls_):
    t = os_[0].shape[0]
    tm = CMB_TM

    def body(o0, o1, o2, l0, l1, l2, a_ref):
        w0, w1, w2 = _combine_weights(l0[...], l1[...], l2[...])
        a_ref[...] = w0 * o0[...] + w1 * o1[...] + w2 * o2[...]

    blk = pl.BlockSpec((tm, 2 * AT_L), lambda i: (i, 0))
    return pl.pallas_call(
        body, out_shape=jax.ShapeDtypeStruct((t, 2 * AT_L), F32), grid=(t // tm,), in_specs=[blk] * 6, out_specs=blk,
        name="combine_fwd", compiler_params=_params(("parallel",)))(*os_, *ls_)


def _combine_bwd(datt, os_, ls_):
    t = datt.shape[0]
    tm = CMB_TM

    def body(da_ref, o0, o1, o2, l0, l1, l2, d0, d1, d2, e0, e1, e2):
        w = _combine_weights(l0[...], l1[...], l2[...])
        da = da_ref[...]
        att = w[0] * o0[...] + w[1] * o1[...] + w[2] * o2[...]
        r = _iota((2 * AT_L, 2 * AT_L), 0) // AH
        c = _iota((2 * AT_L, 2 * AT_L), 1) // AH
        hs = _dot01(da * att, (r == c).astype(BF16))
        for wg, dref, eref in zip(w, (d0, d1, d2), (e0, e1, e2)):
            dref[...] = wg * da
            eref[...] = -wg * hs

    blk = pl.BlockSpec((tm, 2 * AT_L), lambda i: (i, 0))
    shp = jax.ShapeDtypeStruct((t, 2 * AT_L), F32)
    outs = pl.pallas_call(
        body, out_shape=(shp,) * 6, grid=(t // tm,), in_specs=[blk] * 7, out_specs=(blk,) * 6,
        name="combine_bwd", compiler_params=_params(("parallel",)))(datt, *os_, *ls_)
    return outs[0:3], outs[3:6]


def _combine_proj(os_, ls_, w_pa):
    t = os_[0].shape[0]
    tm = ROW_TM
    nsh, _, ws = w_pa.shape

    def body(o0, o1, o2, l0, l1, l2, w_ref, a_ref, y_ref):
        w0, w1, w2 = _combine_weights(l0[...], l1[...], l2[...])
        att = w0 * o0[...] + w1 * o1[...] + w2 * o2[...]
        a_ref[...] = att
        ab = att.astype(BF16)
        for sh in range(nsh):
            y_ref[:, ws * sh:ws * (sh + 1)] = jnp.dot(ab, w_ref[sh], preferred_element_type=F32)

    blk = pl.BlockSpec((tm, 2 * AT_L), lambda i: (i, 0))
    return pl.pallas_call(
        body, out_shape=(jax.ShapeDtypeStruct((t, 2 * AT_L), F32), jax.ShapeDtypeStruct((t, nsh * ws), F32)),
        grid=(t // tm,), in_specs=[blk] * 6 + [pl.BlockSpec(w_pa.shape, lambda i: (0, 0, 0))],
        out_specs=(blk, pl.BlockSpec((tm, nsh * ws), lambda i: (i, 0))),
        name="combine_proj", compiler_params=_params(("parallel",)))(*os_, *ls_, w_pa)


def _d_att_combine_bwd(dy_att, w_pa, os_, ls_):
    t = dy_att.shape[0]
    tm = ROW_TM
    nsh, _, ws = w_pa.shape

    def body(dy_ref, w_ref, o0, o1, o2, l0, l1, l2, d0, d1, d2, e0, e1, e2):
        da = jnp.zeros((tm, 2 * AT_L), F32)
        for sh in range(nsh):
            da = da + _dot_nt(dy_ref[:, ws * sh:ws * (sh + 1)], w_ref[sh])
        w = _combine_weights(l0[...], l1[...], l2[...])
        att = w[0] * o0[...] + w[1] * o1[...] + w[2] * o2[...]
        r = _iota((2 * AT_L, 2 * AT_L), 0) // AH
        c = _iota((2 * AT_L, 2 * AT_L), 1) // AH
        hs = _dot01(da * att, (r == c).astype(BF16))
        for wg, dref, eref in zip(w, (d0, d1, d2), (e0, e1, e2)):
            dref[...] = wg * da
            eref[...] = -wg * hs

    blk = pl.BlockSpec((tm, 2 * AT_L), lambda i: (i, 0))
    shp = jax.ShapeDtypeStruct((t, 2 * AT_L), F32)
    outs = pl.pallas_call(
        body, out_shape=(shp,) * 6, grid=(t // tm,),
        in_specs=[pl.BlockSpec((tm, nsh * ws), lambda i: (i, 0)), pl.BlockSpec(w_pa.shape, lambda i: (0, 0, 0))] + [blk] * 6,
        out_specs=(blk,) * 6, name="d_att_combine_bwd", compiler_params=_params(("parallel",)))(dy_att, w_pa, *os_, *ls_)
    return outs[0:3], outs[3:6]


ROW_TM = 512


def _mix_fwd(y_ssd, y_att, u, bg_row):
    t = y_ssd.shape[0]
    tm = ROW_TM

    def body(ys_ref, ya_ref, g0_ref, g1_ref, b0_ref, b1_ref, o_ref):
        g0 = _sigmoid(g0_ref[...] + b0_ref[...])
        g1 = _sigmoid(g1_ref[...] + b1_ref[...])
        o_ref[...] = (g0 * ys_ref[...] + g1 * ya_ref[...]).astype(BF16)

    blk = pl.BlockSpec((tm, D), lambda i: (i, 0))
    return pl.pallas_call(
        body, out_shape=jax.ShapeDtypeStruct((t, D), BF16), grid=(t // tm,),
        in_specs=[blk, blk, pl.BlockSpec((tm, D), lambda i: (i, OGATE // D)), pl.BlockSpec((tm, D), lambda i: (i, OGATE // D + 1)),
                  pl.BlockSpec((1, D), lambda i: (0, 0)), pl.BlockSpec((1, D), lambda i: (0, 1))],
        out_specs=blk, name="mix_fwd", compiler_params=_params(("parallel",)))(y_ssd, y_att, u, u, bg_row, bg_row)


def _mix_bwd(dmixin, y_ssd, y_att, u, bg_row):
    t = y_ssd.shape[0]
    tm = ROW_TM

    def body(dm_ref, ys_ref, ya_ref, g0_ref, g1_ref, b0_ref, b1_ref, dys_ref, dya_ref, du_ref, db_ref):
        i = pl.program_id(0)
        g0 = _sigmoid(g0_ref[...] + b0_ref[...])
        g1 = _sigmoid(g1_ref[...] + b1_ref[...])
        dm = dm_ref[...]
        dys_ref[...] = (dm * g0).astype(BF16)
        dya_ref[...] = (dm * g1).astype(BF16)
        dl0 = dm * ys_ref[...] * g0 * (1.0 - g0)
        dl1 = dm * ya_ref[...] * g1 * (1.0 - g1)
        du_ref[:, 0:D] = dl0.astype(BF16)
        du_ref[:, D:2 * D] = dl1.astype(BF16)
        part = jnp.concatenate([jnp.broadcast_to(jnp.sum(dl0, axis=0, keepdims=True), (8, D)),
                                jnp.broadcast_to(jnp.sum(dl1, axis=0, keepdims=True), (8, D))], axis=1)

        @pl.when(i == 0)
        def _():
            db_ref[...] = part

        @pl.when(i > 0)
        def _():
            db_ref[...] += part

    blk = pl.BlockSpec((tm, D), lambda i: (i, 0))
    return pl.pallas_call(
        body,
        out_shape=(jax.ShapeDtypeStruct((t, D), BF16), jax.ShapeDtypeStruct((t, D), BF16),
                   jax.ShapeDtypeStruct((t, UW), BF16), jax.ShapeDtypeStruct((8, 2 * D), F32)),
        grid=(t // tm,),
        in_specs=[blk, blk, blk, pl.BlockSpec((tm, D), lambda i: (i, OGATE // D)), pl.BlockSpec((tm, D), lambda i: (i, OGATE // D + 1)),
                  pl.BlockSpec((1, D), lambda i: (0, 0)), pl.BlockSpec((1, D), lambda i: (0, 1))],
        out_specs=(blk, blk, pl.BlockSpec((tm, 2 * D), lambda i: (i, OGATE // (2 * D))),
                   pl.BlockSpec((8, 2 * D), lambda i: (0, 0))),
        name="mix_bwd", compiler_params=_params(("arbitrary",)))(dmixin, y_ssd, y_att, u, u, bg_row, bg_row)


def _ln(x, g, b):
    mu = jnp.mean(x, axis=1, keepdims=True)
    xc = x - mu
    var = jnp.mean(xc * xc, axis=1, keepdims=True)
    rstd = lax.rsqrt(var + NORM_EPS)
    xhat = xc * rstd
    return xhat * g + b, xhat, rstd


def _ln_back(dh, xhat, rstd, g):
    dxh = dh * g
    m1 = jnp.mean(dxh, axis=1, keepdims=True)
    m2 = jnp.mean(dxh * xhat, axis=1, keepdims=True)
    return rstd * (dxh - m1 - xhat * m2)


def _ln1_fwd(x, mix, g_row, b_row):
    t = x.shape[0]
    tm = ROW_TM

    def body(x_ref, m_ref, g_ref, b_ref, pre_ref, h_ref):
        pre = ALPHA * x_ref[...] + m_ref[...]
        pre_ref[...] = pre
        h, _, _ = _ln(pre, g_ref[...], b_ref[...])
        h_ref[...] = h.astype(BF16)

    blk = pl.BlockSpec((tm, D), lambda i: (i, 0))
    row = pl.BlockSpec((1, D), lambda i: (0, 0))
    return pl.pallas_call(
        body, out_shape=(jax.ShapeDtypeStruct((t, D), F32), jax.ShapeDtypeStruct((t, D), BF16)), grid=(t // tm,),
        in_specs=[blk, blk, row, row], out_specs=(blk, blk),
        name="ln1_fwd", compiler_params=_params(("parallel",)))(x, mix, g_row, b_row)


def _ln1_bwd(dh, pre, g_row, b_row):
    t = dh.shape[0]
    tm = ROW_TM

    def body(dh_ref, pre_ref, g_ref, b_ref, dpre_ref, acc_ref):
        i = pl.program_id(0)
        dh_ = dh_ref[...]
        _, xhat, rstd = _ln(pre_ref[...], g_ref[...], b_ref[...])
        dpre_ref[...] = _ln_back(dh_, xhat, rstd, g_ref[...])
        part = jnp.concatenate([jnp.sum(dh_ * xhat, axis=0, keepdims=True), jnp.sum(dh_, axis=0, keepdims=True),
                                jnp.zeros((6, D), F32)], axis=0)

        @pl.when(i == 0)
        def _():
            acc_ref[...] = part

        @pl.when(i > 0)
        def _():
            acc_ref[...] += part

    blk = pl.BlockSpec((tm, D), lambda i: (i, 0))
    row = pl.BlockSpec((1, D), lambda i: (0, 0))
    return pl.pallas_call(
        body, out_shape=(jax.ShapeDtypeStruct((t, D), F32), jax.ShapeDtypeStruct((8, D), F32)), grid=(t // tm,),
        in_specs=[blk, blk, row, row], out_specs=(blk, pl.BlockSpec((8, D), lambda i: (0, 0))),
        name="ln1_bwd", compiler_params=_params(("arbitrary",)))(dh, pre, g_row, b_row)


def _ln2_loss(pre1, f, tgt, g1_row, b1_row, g2_row, b2_row):
    t = pre1.shape[0]
    tm = ROW_TM

    def body(p1_ref, f_ref, t_ref, g1_ref, b1_ref, g2_ref, b2_ref, dpre_ref, acc_ref):
        i = pl.program_id(0)
        h1, _, _ = _ln(p1_ref[...], g1_ref[...], b1_ref[...])
        pre2 = ALPHA * h1 + f_ref[...]
        h2, xhat, rstd = _ln(pre2, g2_ref[...], b2_ref[...])
        err = h2 - t_ref[...]
        dh = err * (1.0 / D)
        dpre_ref[...] = _ln_back(dh, xhat, rstd, g2_ref[...])
        loss = jnp.sum(jnp.sum(err * err, axis=1, keepdims=True), axis=0, keepdims=True) * (0.5 / D)
        part = jnp.concatenate([jnp.sum(dh * xhat, axis=0, keepdims=True), jnp.sum(dh, axis=0, keepdims=True),
                                jnp.broadcast_to(loss, (1, D)), jnp.zeros((5, D), F32)], axis=0)

        @pl.when(i == 0)
        def _():
            acc_ref[...] = part

        @pl.when(i > 0)
        def _():
            acc_ref[...] += part

    blk = pl.BlockSpec((tm, D), lambda i: (i, 0))
    row = pl.BlockSpec((1, D), lambda i: (0, 0))
    return pl.pallas_call(
        body, out_shape=(jax.ShapeDtypeStruct((t, D), F32), jax.ShapeDtypeStruct((8, D), F32)), grid=(t // tm,),
        in_specs=[blk, blk, blk, row, row, row, row], out_specs=(blk, pl.BlockSpec((8, D), lambda i: (0, 0))),
        name="ln2_loss", compiler_params=_params(("arbitrary",)))(pre1, f, tgt, g1_row, b1_row, g2_row, b2_row)


def _mlp_up(h1, w_up):
    t = h1.shape[0]
    tm, tn = ROW_TM, D

    def body(a_ref, b_ref, up_ref, act_ref):
        up = jnp.dot(a_ref[...], b_ref[...], preferred_element_type=F32)
        up_ref[...] = up.astype(BF16)
        r = jnp.maximum(up, 0.0)
        act_ref[...] = (r * r).astype(BF16)

    blk = pl.BlockSpec((tm, tn), lambda j, i: (i, j))
    return pl.pallas_call(
        body, out_shape=(jax.ShapeDtypeStruct((t, DFF), BF16), jax.ShapeDtypeStruct((t, DFF), BF16)),
        grid=(DFF // tn, t // tm),
        in_specs=[pl.BlockSpec((tm, D), lambda j, i: (i, 0)), pl.BlockSpec((None, D, tn), lambda j, i: (j, 0, 0))],
        out_specs=(blk, blk), name="mlp_up", compiler_params=_params(("parallel", "parallel")))(h1, w_up)


def _d_up(dpre2, w_down, up):
    t = up.shape[0]
    tm, tk = ROW_TM, D

    def body(a_ref, b_ref, u_ref, o_ref):
        dact = _dot_nt(a_ref[...], b_ref[...])
        o_ref[...] = (dact * 2.0 * jnp.maximum(u_ref[...].astype(F32), 0.0)).astype(BF16)

    blk = pl.BlockSpec((tm, tk), lambda j, i: (i, j))
    return pl.pallas_call(
        body, out_shape=jax.ShapeDtypeStruct((t, DFF), BF16), grid=(DFF // tk, t // tm),
        in_specs=[pl.BlockSpec((tm, D), lambda j, i: (i, 0)), pl.BlockSpec((tk, D), lambda j, i: (j, 0)), blk],
        out_specs=blk, name="d_up", compiler_params=_params(("parallel", "parallel")))(dpre2, w_down, up)


def _dt_bwd(du, ddt_f, ddt_b):
    t = ddt_f.shape[0]
    tm = 1024

    def body(f_ref, b_ref, du_in, o_ref):
        del du_in
        o_ref[:, 0:128] = (f_ref[...] + b_ref[...]).astype(o_ref.dtype)
        o_ref[:, 128:256] = jnp.zeros((tm, 128), o_ref.dtype)

    blk = pl.BlockSpec((tm, 128), lambda i: (i, 0))
    return pl.pallas_call(
        body, out_shape=jax.ShapeDtypeStruct(du.shape, du.dtype), grid=(t // tm,),
        in_specs=[blk, blk, pl.BlockSpec(memory_space=pl.ANY)],
        out_specs=pl.BlockSpec((tm, 256), lambda i: (i, ODT // 256)), input_output_aliases={2: 0},
        name="dt_bwd", compiler_params=_params(("parallel",)))(ddt_f, ddt_b, du)


def _mix_out_ln1(y_ssd, y_att, u, bg_row, x, w_out, g_row, b_row):
    t = x.shape[0]
    tm = ROW_TM

    def body(ys_ref, ya_ref, g0_ref, g1_ref, b0_ref, b1_ref, x_ref, w_ref, g_ref, b_ref, mixin_ref, pre_ref, h_ref):
        g0 = _sigmoid(g0_ref[...] + b0_ref[...])
        g1 = _sigmoid(g1_ref[...] + b1_ref[...])
        mixin = (g0 * ys_ref[...] + g1 * ya_ref[...]).astype(BF16)
        mixin_ref[...] = mixin
        pre = ALPHA * x_ref[...] + jnp.dot(mixin, w_ref[...], preferred_element_type=F32)
        pre_ref[...] = pre
        h, _, _ = _ln(pre, g_ref[...], b_ref[...])
        h_ref[...] = h.astype(BF16)

    blk = pl.BlockSpec((tm, D), lambda i: (i, 0))
    row = pl.BlockSpec((1, D), lambda i: (0, 0))
    return pl.pallas_call(
        body,
        out_shape=(jax.ShapeDtypeStruct((t, D), BF16), jax.ShapeDtypeStruct((t, D), F32), jax.ShapeDtypeStruct((t, D), BF16)),
        grid=(t // tm,),
        in_specs=[blk, blk, pl.BlockSpec((tm, D), lambda i: (i, OGATE // D)), pl.BlockSpec((tm, D), lambda i: (i, OGATE // D + 1)),
                  row, pl.BlockSpec((1, D), lambda i: (0, 1)), blk, pl.BlockSpec((D, D), lambda i: (0, 0)), row, row],
        out_specs=(blk, blk, blk), name="mix_out_ln1", compiler_params=_params(("parallel",)))(
            y_ssd, y_att, u, u, bg_row, bg_row, x, w_out, g_row, b_row)


def _mlp_down_ln2_loss(act, w_down, pre1, tgt, g1_row, b1_row, g2_row, b2_row):
    t = pre1.shape[0]
    tm = ROW_TM

    def body(a_ref, w_ref, p1_ref, t_ref, g1_ref, b1_ref, g2_ref, b2_ref, dpre_ref, dpreb_ref, acc_ref):
        i = pl.program_id(0)
        f = jnp.dot(a_ref[...], w_ref[...], preferred_element_type=F32)
        h1, _, _ = _ln(p1_ref[...], g1_ref[...], b1_ref[...])
        pre2 = ALPHA * h1 + f
        h2, xhat, rstd = _ln(pre2, g2_ref[...], b2_ref[...])
        err = h2 - t_ref[...]
        dh = err * (1.0 / D)
        dpre = _ln_back(dh, xhat, rstd, g2_ref[...])
        dpre_ref[...] = dpre
        dpreb_ref[...] = dpre.astype(BF16)
        loss = jnp.sum(jnp.sum(err * err, axis=1, keepdims=True), axis=0, keepdims=True) * (0.5 / D)
        part = jnp.concatenate([jnp.sum(dh * xhat, axis=0, keepdims=True), jnp.sum(dh, axis=0, keepdims=True),
                                jnp.broadcast_to(loss, (1, D)), jnp.zeros((5, D), F32)], axis=0)

        @pl.when(i == 0)
        def _():
            acc_ref[...] = part

        @pl.when(i > 0)
        def _():
            acc_ref[...] += part

    blk = pl.BlockSpec((tm, D), lambda i: (i, 0))
    row = pl.BlockSpec((1, D), lambda i: (0, 0))
    return pl.pallas_call(
        body,
        out_shape=(jax.ShapeDtypeStruct((t, D), F32), jax.ShapeDtypeStruct((t, D), BF16), jax.ShapeDtypeStruct((8, D), F32)),
        grid=(t // tm,),
        in_specs=[pl.BlockSpec((tm, DFF), lambda i: (i, 0)), pl.BlockSpec((DFF, D), lambda i: (0, 0)), blk, blk, row, row, row, row],
        out_specs=(blk, blk, pl.BlockSpec((8, D), lambda i: (0, 0))),
        name="mlp_down_ln2_loss", compiler_params=_params(("arbitrary",)))(act, w_down, pre1, tgt, g1_row, b1_row, g2_row, b2_row)


def _d_h1_ln1_bwd(dup, w_up, dpre2, pre1, g_row, b_row):
    t = dup.shape[0]
    tm = ROW_TM
    nsh = w_up.shape[0]

    def body(a_ref, w_ref, add_ref, pre_ref, g_ref, b_ref, dpre_ref, acc_ref, dh_scr):
        i = pl.program_id(0)
        c = pl.program_id(1)
        part = _dot_nt(a_ref[...], w_ref[...])

        @pl.when(c == 0)
        def _():
            dh_scr[...] = part + ALPHA * add_ref[...]

        @pl.when(c > 0)
        def _():
            dh_scr[...] += part

        @pl.when(c == nsh - 1)
        def _():
            dh_ = dh_scr[...]
            _, xhat, rstd = _ln(pre_ref[...], g_ref[...], b_ref[...])
            dpre_ref[...] = _ln_back(dh_, xhat, rstd, g_ref[...])
            rows = jnp.concatenate([jnp.sum(dh_ * xhat, axis=0, keepdims=True), jnp.sum(dh_, axis=0, keepdims=True),
                                    jnp.zeros((6, D), F32)], axis=0)

            @pl.when(i == 0)
            def _():
                acc_ref[...] = rows

            @pl.when(i > 0)
            def _():
                acc_ref[...] += rows

    blk = pl.BlockSpec((tm, D), lambda i, c: (i, 0))
    row = pl.BlockSpec((1, D), lambda i, c: (0, 0))
    return pl.pallas_call(
        body, out_shape=(jax.ShapeDtypeStruct((t, D), F32), jax.ShapeDtypeStruct((8, D), F32)),
        grid=(t // tm, nsh),
        in_specs=[pl.BlockSpec((tm, D), lambda i, c: (i, c)), pl.BlockSpec((None, D, D), lambda i, c: (c, 0, 0)),
                  blk, blk, row, row],
        out_specs=(blk, pl.BlockSpec((8, D), lambda i, c: (0, 0))),
        scratch_shapes=[pltpu.VMEM((tm, D), F32)],
        name="d_h1_ln1_bwd", compiler_params=_params(("arbitrary", "arbitrary")))(dup, w_up, dpre2, pre1, g_row, b_row)


def _d_mixin_mix_bwd(dpre1, w_out, y_ssd, y_att, u, bg_row):
    t = y_ssd.shape[0]
    tm = ROW_TM

    def body(a_ref, w_ref, ys_ref, ya_ref, g0_ref, g1_ref, b0_ref, b1_ref, dys_ref, dya_ref, du_ref, db_ref):
        i = pl.program_id(0)
        dm = _dot_nt(a_ref[...].astype(BF16), w_ref[...])
        g0 = _sigmoid(g0_ref[...] + b0_ref[...])
        g1 = _sigmoid(g1_ref[...] + b1_ref[...])
        dys_ref[...] = (dm * g0).astype(BF16)
        dya_ref[...] = (dm * g1).astype(BF16)
        dl0 = dm * ys_ref[...] * g0 * (1.0 - g0)
        dl1 = dm * ya_ref[...] * g1 * (1.0 - g1)
        du_ref[:, 0:D] = dl0.astype(BF16)
        du_ref[:, D:2 * D] = dl1.astype(BF16)
        part = jnp.concatenate([jnp.broadcast_to(jnp.sum(dl0, axis=0, keepdims=True), (8, D)),
                                jnp.broadcast_to(jnp.sum(dl1, axis=0, keepdims=True), (8, D))], axis=1)

        @pl.when(i == 0)
        def _():
            db_ref[...] = part

        @pl.when(i > 0)
        def _():
            db_ref[...] += part

    blk = pl.BlockSpec((tm, D), lambda i: (i, 0))
    return pl.pallas_call(
        body,
        out_shape=(jax.ShapeDtypeStruct((t, D), BF16), jax.ShapeDtypeStruct((t, D), BF16),
                   jax.ShapeDtypeStruct((t, UW), BF16), jax.ShapeDtypeStruct((8, 2 * D), F32)),
        grid=(t // tm,),
        in_specs=[blk, pl.BlockSpec((D, D), lambda i: (0, 0)), blk, blk,
                  pl.BlockSpec((tm, D), lambda i: (i, OGATE // D)), pl.BlockSpec((tm, D), lambda i: (i, OGATE // D + 1)),
                  pl.BlockSpec((1, D), lambda i: (0, 0)), pl.BlockSpec((1, D), lambda i: (0, 1))],
        out_specs=(blk, blk, pl.BlockSpec((tm, 2 * D), lambda i: (i, OGATE // (2 * D))),
                   pl.BlockSpec((8, 2 * D), lambda i: (0, 0))),
        name="d_mixin_mix_bwd", compiler_params=_params(("arbitrary",)))(dpre1, w_out, y_ssd, y_att, u, u, bg_row, bg_row)


def _adamw(w, g, m, v, name):
    r, c = w.shape
    tr = r
    for cand in (256, 128, 64, 32, 16, 8):
        if r % cand == 0 and cand * c * 4 <= 2 ** 21:
            tr = cand
            break
    bc1 = 1.0 / (1.0 - ADAM_B1 ** ADAM_STEP)
    bc2 = 1.0 / (1.0 - ADAM_B2 ** ADAM_STEP)

    def body(w_ref, g_ref, m_ref, v_ref, d_ref, nm_ref, nv_ref):
        gg = g_ref[...]
        nm = ADAM_B1 * m_ref[...] + (1.0 - ADAM_B1) * gg
        nv = ADAM_B2 * v_ref[...] + (1.0 - ADAM_B2) * (gg * gg)
        nm_ref[...] = nm
        nv_ref[...] = nv
        d_ref[...] = -ADAM_LR * ((nm * bc1) / (jnp.sqrt(nv * bc2) + ADAM_EPS) + ADAM_WD * w_ref[...])

    blk = pl.BlockSpec((tr, c), lambda i: (i, 0))
    shp = jax.ShapeDtypeStruct((r, c), F32)
    return pl.pallas_call(body, out_shape=(shp, shp, shp), grid=(r // tr,), in_specs=[blk] * 4, out_specs=(blk,) * 3,
                          name=name, compiler_params=_params(("parallel",)))(w, g, m, v)


def _perm_cols(w):
    z, xbc, dt = w[:, 0:2048], w[:, 2048:5120], w[:, 5120:5184]
    q, k, v, gate = w[:, 5184:5952], w[:, 5952:6720], w[:, 6720:7488], w[:, 7488:9536]
    kv = []
    for g in range(3):
        for p in range(2):
            lo = 256 * g + 128 * p
            kv += [k[:, lo:lo + 128], v[:, lo:lo + 128]]
    pad = jnp.zeros((w.shape[0], UW - IN_COLS), w.dtype)
    return jnp.concatenate([z, gate, xbc] + kv + [q, dt, pad], axis=1)


def _unperm_cols(wp):
    z, gate, xbc = wp[:, OZ:OZ + 2048], wp[:, OGATE:OGATE + 2048], wp[:, OXBC:OXBC + CONVD]
    q, dt = wp[:, OQ:OQ + 768], wp[:, ODT:ODT + 64]
    ks, vs = [], []
    for g in range(3):
        for p in range(2):
            lo = OKV + 128 * (4 * g + 2 * p)
            ks.append(wp[:, lo:lo + 128])
            vs.append(wp[:, lo + 128:lo + 256])
    return jnp.concatenate([z, xbc, dt, q] + ks + vs + [gate], axis=1)


def _segments():
    segs = [(0, 2048), (7488, 9536), (2048, 5120)]
    for g in range(3):
        for p in range(2):
            lo = 256 * g + 128 * p
            segs += [(5952 + lo, 5952 + lo + 128), (6720 + lo, 6720 + lo + 128)]
    segs += [(5184, 5952), (5120, 5184)]
    out, pos = [], 0
    for a, b in segs:
        out.append((a, b, pos))
        pos += b - a
    return out


SHARD_COLS = IN_COLS // 4


def _perm_from_shards(w_shards):
    pieces = []
    for a, b, _ in _segments():
        while a < b:
            s = a // SHARD_COLS
            e = min(b, (s + 1) * SHARD_COLS)
            pieces.append(w_shards[s][:, a - s * SHARD_COLS:e - s * SHARD_COLS])
            a = e
    pieces.append(jnp.zeros((w_shards.shape[1], UW - IN_COLS), w_shards.dtype))
    return jnp.concatenate(pieces, axis=1)


def _shards_from_perm(wp):
    segs = sorted(_segments())
    shards = []
    for s in range(4):
        lo, hi = s * SHARD_COLS, (s + 1) * SHARD_COLS
        pieces = []
        for a, b, pos in segs:
            x, y = max(a, lo), min(b, hi)
            if x < y:
                pieces.append(wp[:, pos + x - a:pos + y - a])
        shards.append(jnp.concatenate(pieces, axis=1))
    return jnp.stack(shards)


def _lanes128(*vecs):
    v = jnp.concatenate([a.reshape(-1) for a in vecs])
    return jnp.pad(v, (0, 128 - v.shape[0])).reshape(1, 128)


def _local_grads(x, tgt, wts, sm):
    row = lambda a: a.reshape(1, -1)
    bg_row, cb_row = row(sm["b_gate"]), row(sm["conv_b"])
    par = jnp.concatenate([_lanes128(sm["dt_bias_f"], sm["dt_bias_b"]), _lanes128(sm["a_log_f"], sm["a_log_b"]),
                           jnp.zeros((6, 128), F32)], axis=0)
    dsk_row = row(jnp.repeat(sm["d_skip"], HP))
    nw_row = row(sm["ssd_norm_w"])
    g1, b1, g2, b2 = row(sm["ln1_g"]), row(sm["ln1_b"]), row(sm["ln2_g"]), row(sm["ln2_b"])

    xb = x.astype(BF16)
    u = _mm_nn(xb, wts["w_in_p"], tm=512, tn=2432, name="in_proj")
    xbc = _conv_fwd(u, sm["conv_w"], cb_row)
    y_f, st_f = _ssd_fwd(xbc, u, par, rev=False)
    y_fb, st_b = _ssd_fwd(xbc, u, par, y_f, rev=True)
    s_out = _gatenorm_fwd(y_fb, xbc, u, dsk_row, nw_row)
    y_ssd = _mm_nn(s_out, wts["w_proj_ssd"], tm=512, tn=1024, name="proj_ssd")
    att_o, att_l = [], []
    for g in range(3):
        o, l = _attn_fwd(u, g)
        att_o.append(o)
        att_l.append(l)
    att, y_att = _combine_proj(att_o, att_l, wts["w_proj_attn"])
    mixin, pre1, h1 = _mix_out_ln1(y_ssd, y_att, u, bg_row, x, wts["w_out"], g1, b1)
    up, act = _mlp_up(h1, wts["w_up"])
    dpre2, dpre2_b, acc2 = _mlp_down_ln2_loss(act, wts["w_down"], pre1, tgt, g1, b1, g2, b2)

    dw_down = _mm_tn(act, dpre2_b, tka=1024, tn=1024, tt=1024, name="dw_down")
    dup = _d_up(dpre2_b, wts["w_down"], up)
    dw_up = _mm_tn(h1, dup, tka=1024, tn=1024, tt=1024, name="dw_up", out_shards=4)
    dpre1, acc1 = _d_h1_ln1_bwd(dup, wts["w_up"], dpre2, pre1, g1, b1)
    dw_out = _mm_tn(mixin, dpre1, tka=1024, tn=1024, tt=1024, name="dw_out")
    dy_ssd, dy_att, du, dbg = _d_mixin_mix_bwd(dpre1, wts["w_out"], y_ssd, y_att, u, bg_row)
    dw_proj_ssd = _mm_tn(s_out, dy_ssd, tka=1024, tn=1024, tt=1024, name="dw_proj_ssd")
    ds_out = _mm_nt(dy_ssd, wts["w_proj_ssd"], tm=512, tk=1024, tc=1024, name="d_s_out")
    dw_proj_attn = _mm_tn(att, dy_att, tka=256, tn=256, tt=1024, name="dw_proj_attn", out_shards=4)
    do_g, e_g = _d_att_combine_bwd(dy_att, wts["w_proj_attn"], att_o, att_l)
    for g in range(3):
        du = _attn_dq(u, du, do_g[g], att_l[g], e_g[g], g)
        du = _attn_dkv(u, du, do_g[g], att_l[g], e_g[g], g)
    dy, du, dnw, dds = _gatenorm_bwd(ds_out, y_fb, xbc, u, du, dsk_row, nw_row)
    dxs_f, dbc_f, ddt_f, sacc_f = _ssd_bwd(xbc, u, par, dy, st_f, rev=False)
    dxs_b, dbc_b, ddt_b, sacc_b = _ssd_bwd(xbc, u, par, dy, st_b, rev=True)
    dpre_c, dcw, dcb = _conv_dpre(u, dxs_f, dxs_b, dy, dbc_f, dbc_b, dsk_row, sm["conv_w"], cb_row)
    du = _conv_dx(du, dpre_c, sm["conv_w"])
    du = _dt_bwd(du, ddt_f, ddt_b)
    dw_in_p = _mm_tn(xb, du, tka=1024, tn=2432, tt=1024, name="dw_in")
    dx = _mm_nt(du, wts["w_in_p"], tm=512, tk=1024, tc=2432, name="d_x", add=dpre1, add_scale=ALPHA)

    sacc = sacc_f + sacc_b
    small = {
        "b_gate": dbg[0], "conv_w": dcw[0:KCONV], "conv_b": dcb[0],
        "dt_bias_f": sacc[0, 0:32], "dt_bias_b": sacc[0, 32:64], "a_log_f": sacc[1, 0:32], "a_log_b": sacc[1, 32:64],
        "d_skip": dds[0, 0:32], "ssd_norm_w": dnw[0],
        "ln1_g": acc1[0], "ln1_b": acc1[1], "ln2_g": acc2[0], "ln2_b": acc2[1], "loss": acc2[2, 0:1],
    }
    big = {
        "w_in": _shards_from_perm(dw_in_p),
        "w_proj_ssd": dw_proj_ssd.reshape(4, DI // 4, D),
        "w_proj_attn": dw_proj_attn,
        "w_out": dw_out.reshape(4, D // 4, D),
        "w_up": dw_up,
        "w_down": dw_down.reshape(4, DFF // 4, D),
    }
    return dx, big, small


HBM_SPEC = pl.BlockSpec(memory_space=pl.ANY)


def _place():
    x, y, c = lax.axis_index("x"), lax.axis_index("y"), lax.axis_index("c")
    chips = [(1 - x, y), (x, 1 - y), (1 - x, 1 - y)]
    return x, y, c, chips


def _allgather_weights(shards):
    n = len(shards)

    def body(*refs):
        ins, outs = refs[:n], refs[n:2 * n]
        send_sems, recv_sems = refs[2 * n:]
        x, y, c, _ = _place()
        q, q_x, q_y, q_d = 2 * x + y, 2 * (1 - x) + y, 2 * x + 1 - y, 2 * (1 - x) + 1 - y
        x_nbr, y_nbr, sibling = (1 - x, y, c), (x, 1 - y, c), (x, y, 1 - c)

        def copy(w, k, src, dst, to):
            return pltpu.make_async_remote_copy(src_ref=src, dst_ref=dst, send_sem=send_sems.at[w, k],
                                                recv_sem=recv_sems.at[w, k], device_id=to, device_id_type=MESH)

        def rows(w, core, part):
            rh = ins[w].shape[0] // 2
            if part is None:
                return pl.ds(core * rh, rh)
            return pl.ds(core * rh + part * (rh // 2), rh // 2)

        def same(w, k, slot, core, part, to):
            blk = outs[w].at[slot, rows(w, core, part), :]
            return copy(w, k, blk, blk, to)

        started = []
        for w in range(n):
            cp = copy(w, 8, ins[w], outs[w].at[q], sibling)
            cp.start()
            started.append(cp)
            mine = rows(w, c, None)
            for k, to in ((0, x_nbr), (1, y_nbr)):
                cp = copy(w, k, ins[w].at[mine, :], outs[w].at[q, mine, :], to)
                cp.start()
                started.append(cp)
        for w in range(n):
            same(w, 0, q_x, c, None, x_nbr).wait_recv()
            for cp in (same(w, 2, q_x, c, 0, y_nbr), same(w, 4, q_x, c, None, sibling)):
                cp.start()
                started.append(cp)
            same(w, 1, q_y, c, None, y_nbr).wait_recv()
            for cp in (same(w, 3, q_y, c, 1, x_nbr), same(w, 5, q_y, c, None, sibling)):
                cp.start()
                started.append(cp)
        for w in range(n):
            same(w, 2, q_d, c, 0, y_nbr).wait_recv()
            cp = same(w, 6, q_d, c, 0, sibling)
            cp.start()
            started.append(cp)
            same(w, 3, q_d, c, 1, x_nbr).wait_recv()
            cp = same(w, 7, q_d, c, 1, sibling)
            cp.start()
            started.append(cp)
        for w in range(n):
            same(w, 4, q_x, 1 - c, None, sibling).wait_recv()
            same(w, 5, q_y, 1 - c, None, sibling).wait_recv()
            same(w, 6, q_d, 1 - c, 0, sibling).wait_recv()
            same(w, 7, q_d, 1 - c, 1, sibling).wait_recv()
            copy(w, 8, ins[w], outs[w].at[q], sibling).wait_recv()
        for cp in started:
            cp.wait_send()

    return pl.pallas_call(
        body, out_shape=[jax.ShapeDtypeStruct((4,) + s.shape, s.dtype) for s in shards],
        in_specs=[HBM_SPEC] * n, out_specs=[HBM_SPEC] * n,
        scratch_shapes=[pltpu.SemaphoreType.DMA((n, 9)), pltpu.SemaphoreType.DMA((n, 9))],
        name="allgather_weights")(*shards)


def _swap_halves(grads):
    n = len(grads)

    def body(*refs):
        ins, outs = refs[:n], refs[n:2 * n]
        send_sems, recv_sems = refs[2 * n:]
        x, y, c, _ = _place()
        copies = []
        for w in range(n):
            rh = ins[w].shape[1] // 2
            for p in range(4):
                cp = pltpu.make_async_remote_copy(
                    src_ref=ins[w].at[p, pl.ds((1 - c) * rh, rh), :], dst_ref=outs[w].at[p],
                    send_sem=send_sems.at[w, p], recv_sem=recv_sems.at[w, p],
                    device_id=(x, y, 1 - c), device_id_type=MESH)
                cp.start()
                copies.append(cp)
        for cp in copies:
            cp.wait()

    return pl.pallas_call(
        body, out_shape=[jax.ShapeDtypeStruct((4, g.shape[1] // 2, g.shape[2]), F32) for g in grads],
        in_specs=[HBM_SPEC] * n, out_specs=[HBM_SPEC] * n,
        scratch_shapes=[pltpu.SemaphoreType.DMA((n, 4)), pltpu.SemaphoreType.DMA((n, 4))],
        name="rs_swap_halves")(*grads)


def _rs_step1(parts):
    n = len(parts)

    def body(*refs):
        ins, out_a, out_b = refs[:n], refs[n:2 * n], refs[2 * n:3 * n]
        send_sems, recv_sems = refs[3 * n:]
        x, y, c, _ = _place()
        copies = []
        for w in range(n):
            rq = ins[w].shape[1] // 2
            for i in range(2):
                copies.append(pltpu.make_async_remote_copy(
                    src_ref=ins[w].at[2 * (1 - x) + i, pl.ds(0, rq), :], dst_ref=out_a[w].at[i],
                    send_sem=send_sems.at[w, i], recv_sem=recv_sems.at[w, i],
                    device_id=(1 - x, y, c), device_id_type=MESH))
                copies.append(pltpu.make_async_remote_copy(
                    src_ref=ins[w].at[2 * i + 1 - y, pl.ds(rq, rq), :], dst_ref=out_b[w].at[i],
                    send_sem=send_sems.at[w, 2 + i], recv_sem=recv_sems.at[w, 2 + i],
                    device_id=(x, 1 - y, c), device_id_type=MESH))
        for cp in copies:
            cp.start()
        for cp in copies:
            cp.wait()

    quarter = lambda p: jax.ShapeDtypeStruct((2, p.shape[1] // 2, p.shape[2]), p.dtype)
    outs = pl.pallas_call(
        body, out_shape=[quarter(p) for p in parts] * 2,
        in_specs=[HBM_SPEC] * n, out_specs=[HBM_SPEC] * (2 * n),
        scratch_shapes=[pltpu.SemaphoreType.DMA((n, 4)), pltpu.SemaphoreType.DMA((n, 4))],
        name="rs_step1")(*parts)
    return outs[:n], outs[n:]


def _rs_step2(tas, tbs):
    n = len(tas)

    def body(*refs):
        in_a, in_b, out_a, out_b = refs[:n], refs[n:2 * n], refs[2 * n:3 * n], refs[3 * n:4 * n]
        send_sems, recv_sems = refs[4 * n:]
        x, y, c, _ = _place()
        copies = []
        for w in range(n):
            copies.append(pltpu.make_async_remote_copy(
                src_ref=in_a[w].at[1 - y], dst_ref=out_a[w], send_sem=send_sems.at[w, 0], recv_sem=recv_sems.at[w, 0],
                device_id=(x, 1 - y, c), device_id_type=MESH))
            copies.append(pltpu.make_async_remote_copy(
                src_ref=in_b[w].at[1 - x], dst_ref=out_b[w], send_sem=send_sems.at[w, 1], recv_sem=recv_sems.at[w, 1],
                device_id=(1 - x, y, c), device_id_type=MESH))
        for cp in copies:
            cp.start()
        for cp in copies:
            cp.wait()

    one = lambda p: jax.ShapeDtypeStruct(p.shape[1:], p.dtype)
    outs = pl.pallas_call(
        body, out_shape=[one(p) for p in tas] + [one(p) for p in tbs],
        in_specs=[HBM_SPEC] * (2 * n), out_specs=[HBM_SPEC] * (2 * n),
        scratch_shapes=[pltpu.SemaphoreType.DMA((n, 2)), pltpu.SemaphoreType.DMA((n, 2))],
        name="rs_step2")(*tas, *tbs)
    return outs[:n], outs[n:]


def _join_halves(pieces):
    n = len(pieces)

    def body(*refs):
        outs = refs[n:2 * n]
        send_sems, recv_sems = refs[2 * n:]
        x, y, c, _ = _place()

        def copy(w, slot):
            return pltpu.make_async_remote_copy(
                src_ref=outs[w].at[slot], dst_ref=outs[w].at[slot], send_sem=send_sems.at[w], recv_sem=recv_sems.at[w],
                device_id=(x, y, 1 - c), device_id_type=MESH)

        for w in range(n):
            copy(w, c).start()
        for w in range(n):
            copy(w, 1 - c).wait_recv()
            copy(w, c).wait_send()

    return pl.pallas_call(
        body, out_shape=[jax.ShapeDtypeStruct(p.shape, F32) for p in pieces],
        in_specs=[HBM_SPEC] * n, out_specs=[HBM_SPEC] * n, input_output_aliases={w: w for w in range(n)},
        scratch_shapes=[pltpu.SemaphoreType.DMA((n,)), pltpu.SemaphoreType.DMA((n,))],
        name="rs_join_halves")(*pieces)


def _add_tile_rows(rh, c):
    for cand in (512, 256, 128, 64, 32, 16, 8):
        if rh % cand == 0 and cand * c * 4 <= 2 ** 21:
            return cand
    return rh


def _add_half(grad, recv, c_idx, name):
    _, r, cc = grad.shape
    rh = r // 2
    tr = _add_tile_rows(rh, cc)
    nb = rh // tr

    def body(c_ref, g_ref, r_ref, o_ref, ob_ref):
        del c_ref
        s = g_ref[...] + r_ref[...]
        o_ref[...] = s
        ob_ref[...] = s.astype(BF16)

    blk = pl.BlockSpec((None, tr, cc), lambda p, i, c_ref: (p, i, 0))
    grid_spec = pltpu.PrefetchScalarGridSpec(
        num_scalar_prefetch=1, grid=(4, nb),
        in_specs=[pl.BlockSpec((None, tr, cc), lambda p, i, c_ref: (p, c_ref[0] * nb + i, 0)), blk],
        out_specs=(blk, blk))
    return pl.pallas_call(
        body, out_shape=(jax.ShapeDtypeStruct((4, rh, cc), F32), jax.ShapeDtypeStruct((4, rh, cc), BF16)),
        grid_spec=grid_spec, name=name, compiler_params=_params(("parallel", "parallel")))(c_idx, grad, recv)


def _rs_add1(part, recv_a, recv_b, xy_idx, name):
    _, rh, cc = part.shape
    rq = rh // 2
    tr = _add_tile_rows(rq, cc)
    nb = rq // tr

    def body(xy_ref, pa_ref, pb_ref, ra_ref, rb_ref, ta_ref, tb_ref, tab_ref, tbb_ref):
        del xy_ref
        ta = pa_ref[...] + ra_ref[...].astype(F32)
        tb = pb_ref[...] + rb_ref[...].astype(F32)
        ta_ref[...] = ta
        tb_ref[...] = tb
        tab_ref[...] = ta.astype(BF16)
        tbb_ref[...] = tb.astype(BF16)

    blk = pl.BlockSpec((None, tr, cc), lambda i, j, xy: (i, j, 0))
    grid_spec = pltpu.PrefetchScalarGridSpec(
        num_scalar_prefetch=1, grid=(2, nb),
        in_specs=[pl.BlockSpec((None, tr, cc), lambda i, j, xy: (2 * xy[0] + i, j, 0)),
                  pl.BlockSpec((None, tr, cc), lambda i, j, xy: (2 * i + xy[1], nb + j, 0)), blk, blk],
        out_specs=(blk, blk, blk, blk))
    f32s, b16s = jax.ShapeDtypeStruct((2, rq, cc), F32), jax.ShapeDtypeStruct((2, rq, cc), BF16)
    return pl.pallas_call(body, out_shape=(f32s, f32s, b16s, b16s), grid_spec=grid_spec, name=name,
                          compiler_params=_params(("parallel", "parallel")))(xy_idx, part, part, recv_a, recv_b)


def _rs_add2(ta, tb, recv_a, recv_b, xy_idx, name):
    _, rq, cc = ta.shape
    tr = _add_tile_rows(rq, cc)
    nb = rq // tr

    def body(xy_ref, ta_ref, tb_ref, ra_ref, rb_ref, o_ref):
        del xy_ref
        s = pl.program_id(0)
        fa = ta_ref[...] + ra_ref[...].astype(F32)
        fb = tb_ref[...] + rb_ref[...].astype(F32)
        o_ref[...] = jnp.where(s == 0, fa, fb)

    rblk = pl.BlockSpec((tr, cc), lambda s, j, xy: (j, 0))
    grid_spec = pltpu.PrefetchScalarGridSpec(
        num_scalar_prefetch=1, grid=(2, nb),
        in_specs=[pl.BlockSpec((None, tr, cc), lambda s, j, xy: (xy[1], j, 0)),
                  pl.BlockSpec((None, tr, cc), lambda s, j, xy: (xy[0], j, 0)), rblk, rblk],
        out_specs=pl.BlockSpec((None, tr, cc), lambda s, j, xy: (xy[2], s * nb + j, 0)))
    return pl.pallas_call(body, out_shape=jax.ShapeDtypeStruct((2, 2 * rq, cc), F32), grid_spec=grid_spec, name=name,
                          compiler_params=_params(("parallel", "parallel")))(xy_idx, ta, tb, recv_a, recv_b)


def _allreduce_small(slab):
    r = slab.shape[0]

    def body(x_ref, o_ref, buf, send_sems, recv_sems):
        x, y, c, _ = _place()
        me = 4 * x + 2 * y + c
        buf[me] = x_ref[...]
        peers = []
        for k in range(1, 8):
            kx, ky, kc = (k >> 2) & 1, (k >> 1) & 1, k & 1
            peers.append((x + kx - 2 * x * kx, y + ky - 2 * y * ky, c + kc - 2 * c * kc))

        def copy(k, slot):
            return pltpu.make_async_remote_copy(src_ref=x_ref, dst_ref=buf.at[slot], send_sem=send_sems.at[k],
                                                recv_sem=recv_sems.at[k], device_id=peers[k], device_id_type=MESH)

        for k in range(7):
            copy(k, me).start()
        for k, (px, py, pc) in enumerate(peers):
            copy(k, 4 * px + 2 * py + pc).wait_recv()
        for k in range(7):
            copy(k, me).wait_send()
        acc = buf[0]
        for j in range(1, 8):
            acc = acc + buf[j]
        o_ref[...] = acc

    vm = pl.BlockSpec(memory_space=pltpu.VMEM)
    return pl.pallas_call(
        body, out_shape=jax.ShapeDtypeStruct((r, 128), F32), in_specs=[vm], out_specs=vm,
        scratch_shapes=[pltpu.VMEM((8, r, 128), F32), pltpu.SemaphoreType.DMA((7,)), pltpu.SemaphoreType.DMA((7,))],
        name="allreduce_small")(slab)


def _pack(arrs):
    rows = []
    for a in arrs:
        v = a.reshape(-1)
        v = jnp.pad(v, (0, (-v.shape[0]) % 128))
        rows.append(v.reshape(-1, 128))
    slab = jnp.concatenate(rows, axis=0)
    return jnp.pad(slab, ((0, (-slab.shape[0]) % 8), (0, 0)))


def _unpack(slab, shapes):
    out, r0 = [], 0
    for shp in shapes:
        size = math.prod(shp)
        nr = -(-size // 128)
        out.append(slab[r0:r0 + nr].reshape(-1)[:size].reshape(shp))
        r0 += nr
    return out


BIG = ("w_in", "w_proj_ssd", "w_proj_attn", "w_out", "w_up", "w_down")
SMALL = ("b_gate", "conv_w", "conv_b", "dt_bias_f", "dt_bias_b", "a_log_f", "a_log_b", "d_skip", "ssd_norm_w",
         "ln1_g", "ln1_b", "ln2_g", "ln2_b")
ORDER = ("w_in", "b_gate", "conv_w", "conv_b", "dt_bias_f", "dt_bias_b", "a_log_f", "a_log_b", "d_skip", "ssd_norm_w",
         "w_proj_ssd", "w_proj_attn", "w_out", "ln1_g", "ln1_b", "w_up", "w_down", "ln2_g", "ln2_b")


def kernel(x, w_in, b_gate, conv_w, conv_b, dt_bias_f, dt_bias_b, a_log_f, a_log_b, d_skip, ssd_norm_w, w_proj_ssd, w_proj_attn, w_out, ln1_g, ln1_b, w_up, w_down, ln2_g, ln2_b, loss_target, m_w_in, m_b_gate, m_conv_w, m_conv_b, m_dt_bias_f, m_dt_bias_b, m_a_log_f, m_a_log_b, m_d_skip, m_ssd_norm_w, m_w_proj_ssd, m_w_proj_attn, m_w_out, m_ln1_g, m_ln1_b, m_w_up, m_w_down, m_ln2_g, m_ln2_b, v_w_in, v_b_gate, v_conv_w, v_conv_b, v_dt_bias_f, v_dt_bias_b, v_a_log_f, v_a_log_b, v_d_skip, v_ssd_norm_w, v_w_proj_ssd, v_w_proj_attn, v_w_out, v_ln1_g, v_ln1_b, v_w_up, v_w_down, v_ln2_g, v_ln2_b):
    w = dict(w_in=w_in, b_gate=b_gate, conv_w=conv_w, conv_b=conv_b, dt_bias_f=dt_bias_f, dt_bias_b=dt_bias_b,
             a_log_f=a_log_f, a_log_b=a_log_b, d_skip=d_skip, ssd_norm_w=ssd_norm_w, w_proj_ssd=w_proj_ssd,
             w_proj_attn=w_proj_attn, w_out=w_out, ln1_g=ln1_g, ln1_b=ln1_b, w_up=w_up, w_down=w_down, ln2_g=ln2_g, ln2_b=ln2_b)
    m = dict(w_in=m_w_in, b_gate=m_b_gate, conv_w=m_conv_w, conv_b=m_conv_b, dt_bias_f=m_dt_bias_f, dt_bias_b=m_dt_bias_b,
             a_log_f=m_a_log_f, a_log_b=m_a_log_b, d_skip=m_d_skip, ssd_norm_w=m_ssd_norm_w, w_proj_ssd=m_w_proj_ssd,
             w_proj_attn=m_w_proj_attn, w_out=m_w_out, ln1_g=m_ln1_g, ln1_b=m_ln1_b, w_up=m_w_up, w_down=m_w_down,
             ln2_g=m_ln2_g, ln2_b=m_ln2_b)
    v = dict(w_in=v_w_in, b_gate=v_b_gate, conv_w=v_conv_w, conv_b=v_conv_b, dt_bias_f=v_dt_bias_f, dt_bias_b=v_dt_bias_b,
             a_log_f=v_a_log_f, a_log_b=v_a_log_b, d_skip=v_d_skip, ssd_norm_w=v_ssd_norm_w, w_proj_ssd=v_w_proj_ssd,
             w_proj_attn=v_w_proj_attn, w_out=v_w_out, ln1_g=v_ln1_g, ln1_b=v_ln1_b, w_up=v_w_up, w_down=v_w_down,
             ln2_g=v_ln2_g, ln2_b=v_ln2_b)
    xi, yi, ci = lax.axis_index("x"), lax.axis_index("y"), lax.axis_index("c")
    shard = 2 * xi + yi

    g_in, g_ps, g_pa, g_o, g_up, g_dn = _allgather_weights([w[n].astype(BF16) for n in BIG])
    wts = {"w_in_p": _perm_from_shards(g_in),"w_proj_ssd": g_ps.reshape(DI, D), "w_proj_attn": g_pa,
           "w_out": g_o.reshape(D, D), "w_up": g_up, "w_down": g_dn.reshape(DFF, D)}

    cw_slab = jnp.zeros((KCONV, 4, CONVD // 4), F32)
    cw_slab = lax.dynamic_update_slice(cw_slab, conv_w[:, None, :] * 0.5, (0, shard, 0))
    conv_w_all = _unpack(_allreduce_small(_pack([cw_slab])), [(KCONV, CONVD)])[0]

    sm = {n: w[n] for n in SMALL}
    sm["conv_w"] = conv_w_all
    dx, big, small = _local_grads(x[0], loss_target[0], wts, sm)

    names = list(SMALL) + ["loss"]
    shapes = [small[n].shape for n in names]
    red = dict(zip(names, _unpack(_allreduce_small(_pack([small[n] for n in names])), shapes)))
    loss = red["loss"].reshape(())
    gsm = {n: red[n] for n in SMALL}
    conv_w_grad_shard = lax.dynamic_slice_in_dim(gsm["conv_w"].reshape(KCONV, 4, CONVD // 4), shard, 1, axis=1)
    gsm["conv_w"] = conv_w_grad_shard.reshape(KCONV, CONVD // 4)

    c_idx = jnp.reshape(ci, (1,)).astype(jnp.int32)
    glist = [big[n] for n in BIG]
    xy_idx = jnp.stack([xi, yi, ci]).astype(jnp.int32)
    recv = _swap_halves(glist)
    halves = [_add_half(g, r, c_idx, f"rs_add_half_{n}") for g, r, n in zip(glist, recv, BIG)]
    recv_a, recv_b = _rs_step1([h[1] for h in halves])
    sums1 = [_rs_add1(h[0], ra, rb, xy_idx, f"rs_add1_{n}") for h, ra, rb, n in zip(halves, recv_a, recv_b, BIG)]
    recv_a2, recv_b2 = _rs_step2([s1[2] for s1 in sums1], [s1[3] for s1 in sums1])
    pieces = [_rs_add2(s1[0], s1[1], ra, rb, xy_idx, f"rs_add2_{n}")
              for s1, ra, rb, n in zip(sums1, recv_a2, recv_b2, BIG)]
    joined = _join_halves(pieces)
    gbig = {n: j.reshape(w[n].shape) for n, j in zip(BIG, joined)}

    grads, deltas, new_m, new_v = {}, {}, {}, {}
    for n in BIG:
        grads[n] = gbig[n]
        deltas[n], new_m[n], new_v[n] = _adamw(w[n], gbig[n], m[n], v[n], f"adamw_{n}")
    sshapes = [w[n].shape for n in SMALL]
    d_s, m_s, v_s = _adamw(_pack([w[n] for n in SMALL]), _pack([gsm[n] for n in SMALL]),
                           _pack([m[n] for n in SMALL]), _pack([v[n] for n in SMALL]), "adamw_small")
    for n, dd, mm, vv in zip(SMALL, _unpack(d_s, sshapes), _unpack(m_s, sshapes), _unpack(v_s, sshapes)):
        grads[n], deltas[n], new_m[n], new_v[n] = gsm[n], dd, mm, vv

    return (loss, dx[None], *[grads[n] for n in ORDER], *[deltas[n] for n in ORDER],
            *[new_m[n] for n in ORDER], *[new_v[n] for n in ORDER])
```

```python
import math
from typing import Callable, NamedTuple

import jax
import numpy as np
import jax.numpy as jnp
from jax import lax
from jax.experimental import pallas as pl
from jax.experimental.pallas import tpu as pltpu

F32, BF16 = jnp.float32, jnp.bfloat16
MESH = pl.DeviceIdType.MESH

D = 1024
DI = 2048
NH = 32
HP = 64
NG = 4
NS = 128
Q = 128
CONVD = 3072
KCONV = 5
DFF = 4096
AH = 64
ATT_HALF = 64
DILATIONS = (1, 4, 16)
IN_COLS = 9536
OZ, OGATE, OXBC, OKV, OQ, ODT, UW = 0, 2048, 4096, 7168, 8704, 9472, 9728
ALPHA = 2.0 ** 0.25
NORM_EPS = 1e-5
ADAM_LR, ADAM_B1, ADAM_B2, ADAM_EPS, ADAM_WD, ADAM_STEP = 0.001, 0.9, 0.999, 1e-8, 0.01, 10
VMEM_LIMIT = 56 * 2 ** 20
NEG = -1e30


def _params(sem):
    return pltpu.CompilerParams(dimension_semantics=sem, vmem_limit_bytes=VMEM_LIMIT)


def _sigmoid(x):
    return 1.0 / (1.0 + jnp.exp(-x))


def _softplus(x):
    e = jnp.exp(-jnp.abs(x))
    small = e * (1.0 - e * (0.5 - e * (1.0 / 3.0)))
    return jnp.maximum(x, 0.0) + jnp.where(e < 0.01, small, jnp.log(1.0 + e))


def _split3(a):
    hi = a.astype(BF16)
    r = a - hi.astype(F32)
    mid = r.astype(BF16)
    lo = (r - mid.astype(F32)).astype(BF16)
    return hi, mid, lo


def _dot01(a, m01):
    hi, mid, lo = _split3(a)
    d = lambda p: jnp.dot(p, m01, preferred_element_type=F32)
    return d(hi) + d(mid) + d(lo)


def _dot01_l(m01, a):
    hi, mid, lo = _split3(a)
    d = lambda p: jnp.dot(m01, p, preferred_element_type=F32)
    return d(hi) + d(mid) + d(lo)


def _dot_nt(a, b):
    return lax.dot_general(a, b, (((1,), (1,)), ((), ())), preferred_element_type=F32)


def _iota(shape, dim):
    return lax.broadcasted_iota(jnp.int32, shape, dim)


def _mm_nn(a, b, *, tm, tn, name, out_dtype=F32):
    m, k = a.shape
    if b.ndim == 3:
        assert tn == b.shape[2]
        n = b.shape[0] * b.shape[2]
        b_spec = pl.BlockSpec((None, k, tn), lambda j, i: (j, 0, 0))
    else:
        n = b.shape[1]
        b_spec = pl.BlockSpec((k, tn), lambda j, i: (0, j))

    def body(a_ref, b_ref, o_ref):
        o_ref[...] = jnp.dot(a_ref[...].astype(BF16), b_ref[...], preferred_element_type=F32).astype(out_dtype)

    return pl.pallas_call(
        body, out_shape=jax.ShapeDtypeStruct((m, n), out_dtype), grid=(n // tn, m // tm),
        in_specs=[pl.BlockSpec((tm, k), lambda j, i: (i, 0)), b_spec],
        out_specs=pl.BlockSpec((tm, tn), lambda j, i: (i, j)),
        name=name, compiler_params=_params(("parallel", "parallel")))(a, b)


def _mm_nt(a, b, *, tm, tk, tc, name, add=None, add_scale=1.0):
    m, n = a.shape
    if b.ndim == 3:
        assert tc == b.shape[2]
        k, nc = b.shape[1], b.shape[0]
        b_spec = pl.BlockSpec((None, tk, tc), lambda j, i, c: (c, j, 0))
    else:
        k, nc = b.shape[0], n // tc
        b_spec = pl.BlockSpec((tk, tc), lambda j, i, c: (j, c))

    def body(*refs):
        if add is None:
            a_ref, b_ref, o_ref = refs
        else:
            a_ref, b_ref, add_ref, o_ref = refs
        c = pl.program_id(2)
        part = _dot_nt(a_ref[...].astype(BF16), b_ref[...])

        @pl.when(c == 0)
        def _():
            if add is None:
                o_ref[...] = part
            else:
                o_ref[...] = part + add_scale * add_ref[...]

        @pl.when(c > 0)
        def _():
            o_ref[...] += part

    in_specs = [pl.BlockSpec((tm, tc), lambda j, i, c: (i, c)), b_spec]
    args = [a, b]
    if add is not None:
        in_specs.append(pl.BlockSpec((tm, tk), lambda j, i, c: (i, j)))
        args.append(add)
    return pl.pallas_call(
        body, out_shape=jax.ShapeDtypeStruct((m, k), F32), grid=(k // tk, m // tm, nc),
        in_specs=in_specs, out_specs=pl.BlockSpec((tm, tk), lambda j, i, c: (i, j)),
        name=name, compiler_params=_params(("parallel", "parallel", "arbitrary")))(*args)


def _mm_tn(a, b, *, tka, tn, tt, name, out_shards=None):
    t, ka = a.shape
    n = b.shape[1]
    if out_shards:
        assert tn == n // out_shards
        out_shape = jax.ShapeDtypeStruct((out_shards, ka, tn), F32)
        o_spec = pl.BlockSpec((None, tka, tn), lambda i, j, s: (j, i, 0))
    else:
        out_shape = jax.ShapeDtypeStruct((ka, n), F32)
        o_spec = pl.BlockSpec((tka, tn), lambda i, j, s: (i, j))

    def body(a_ref, b_ref, o_ref):
        s = pl.program_id(2)
        part = lax.dot_general(a_ref[...].astype(BF16), b_ref[...].astype(BF16), (((0,), (0,)), ((), ())),
                               preferred_element_type=F32)

        @pl.when(s == 0)
        def _():
            o_ref[...] = part

        @pl.when(s > 0)
        def _():
            o_ref[...] += part

    return pl.pallas_call(
        body, out_shape=out_shape, grid=(ka // tka, n // tn, t // tt),
        in_specs=[pl.BlockSpec((tt, tka), lambda i, j, s: (s, i)), pl.BlockSpec((tt, tn), lambda i, j, s: (s, j))],
        out_specs=o_spec, name=name, compiler_params=_params(("parallel", "parallel", "arbitrary")))(a, b)


CONV_TM = 512
CONV_TC = 1024
CONV_RC = 64
CONV_CC = 256


def _halo_specs(t, tm, tc, col0):
    nb8 = t // 8
    r8 = tm // 8
    return [
        pl.BlockSpec((8, tc), lambda i, j: (jnp.maximum(i * r8 - 1, 0), col0 + j)),
        pl.BlockSpec((tm, tc), lambda i, j: (i, col0 + j)),
        pl.BlockSpec((8, tc), lambda i, j: (jnp.minimum((i + 1) * r8, nb8 - 1), col0 + j)),
    ]


def _fill_ext(ext, prev_ref, cur_ref, next_ref, tm, i, last):
    ext[0:8, :] = jnp.where(i > 0, prev_ref[...], 0.0)
    ext[8:8 + tm, :] = cur_ref[...]
    ext[8 + tm:16 + tm, :] = jnp.where(i < last, next_ref[...], 0.0)


def _conv_fwd(u, conv_w, conv_b):
    t = u.shape[0]
    tm, tc = CONV_TM, CONV_TC

    def body(prev_ref, cur_ref, next_ref, w_ref, b_ref, o_ref, ext):
        _fill_ext(ext, prev_ref, cur_ref, next_ref, tm, pl.program_id(0), t // tm - 1)
        for c0 in range(0, tc, CONV_CC):
            cs = slice(c0, c0 + CONV_CC)
            w = w_ref[:, cs]
            for r0 in range(0, tm, CONV_RC):
                acc = jnp.broadcast_to(b_ref[:, cs], (CONV_RC, CONV_CC))
                for k in range(KCONV):
                    acc = acc + w[k:k + 1, :] * ext[pl.ds(r0 + 6 + k, CONV_RC), cs]
                o_ref[r0:r0 + CONV_RC, cs] = acc * _sigmoid(acc)

    return pl.pallas_call(
        body, out_shape=jax.ShapeDtypeStruct((t, CONVD), F32), grid=(t // tm, CONVD // tc),
        in_specs=_halo_specs(t, tm, tc, OXBC // tc) + [
            pl.BlockSpec((KCONV, tc), lambda i, j: (0, j)), pl.BlockSpec((1, tc), lambda i, j: (0, j))],
        out_specs=pl.BlockSpec((tm, tc), lambda i, j: (i, j)),
        scratch_shapes=[pltpu.VMEM((tm + 16, tc), F32)],
        name="conv_fwd", compiler_params=_params(("parallel", "parallel")))(u, u, u, conv_w, conv_b)


def _conv_dpre(u, dxs_f, dxs_b, dy, dbc_f, dbc_b, dsk_row, conv_w, conv_b):
    t = u.shape[0]
    tm, tc = CONV_TM, CONV_TC
    r8 = tm // 8
    nb8 = t // 8
    c0 = OXBC // tc

    def body(uprev, ucur, unext, f_ref, b_ref, y_ref, cf_ref, cb_ref, dsk_ref, w_ref, bias_ref,
             dpre_ref, dw_ref, db_ref, ext):
        j = pl.program_id(0)
        i = pl.program_id(1)
        _fill_ext(ext, uprev, ucur, unext, tm, i, t // tm - 1)
        is_xs = j < 2
        dw_cols, db_cols = [], []
        for c0 in range(0, tc, CONV_CC):
            cs = slice(c0, c0 + CONV_CC)
            w = w_ref[:, cs]
            dsk = dsk_ref[:, cs]
            dw_acc = [jnp.zeros((1, CONV_CC), F32) for _ in range(KCONV)]
            db_acc = jnp.zeros((1, CONV_CC), F32)
            for r0 in range(0, tm, CONV_RC):
                rs = slice(r0, r0 + CONV_RC)
                taps = [ext[pl.ds(r0 + 6 + k, CONV_RC), cs] for k in range(KCONV)]
                pre = jnp.broadcast_to(bias_ref[:, cs], (CONV_RC, CONV_CC))
                for k in range(KCONV):
                    pre = pre + w[k:k + 1, :] * taps[k]
                s = _sigmoid(pre)
                xs_part = f_ref[rs, cs] + b_ref[rs, cs] + dsk * y_ref[rs, cs]
                up = jnp.where(is_xs, xs_part, cf_ref[rs, cs] + cb_ref[rs, cs])
                dpre = up * (s * (1.0 + pre * (1.0 - s)))
                dpre_ref[rs, cs] = dpre
                for k in range(KCONV):
                    dw_acc[k] = dw_acc[k] + jnp.sum(dpre * taps[k], axis=0, keepdims=True)
                db_acc = db_acc + jnp.sum(dpre, axis=0, keepdims=True)
            dw_cols.append(jnp.concatenate(dw_acc + [jnp.zeros((8 - KCONV, CONV_CC), F32)], axis=0))
            db_cols.append(jnp.broadcast_to(db_acc, (8, CONV_CC)))
        dw_part = jnp.concatenate(dw_cols, axis=1)
        db_part = jnp.concatenate(db_cols, axis=1)

        @pl.when(i == 0)
        def _():
            dw_ref[...] = dw_part
            db_ref[...] = db_part

        @pl.when(i > 0)
        def _():
            dw_ref[...] += dw_part
            db_ref[...] += db_part

    xs_spec = pl.BlockSpec((tm, tc), lambda j, i: (jnp.where(j < 2, i, 0), jnp.minimum(j, 1)))
    bc_spec = pl.BlockSpec((tm, tc), lambda j, i: (jnp.where(j == 2, i, 0), 0))
    in_specs = [
        pl.BlockSpec((8, tc), lambda j, i: (jnp.maximum(i * r8 - 1, 0), c0 + j)),
        pl.BlockSpec((tm, tc), lambda j, i: (i, c0 + j)),
        pl.BlockSpec((8, tc), lambda j, i: (jnp.minimum((i + 1) * r8, nb8 - 1), c0 + j)),
        xs_spec, xs_spec, xs_spec, bc_spec, bc_spec,
        pl.BlockSpec((1, tc), lambda j, i: (0, jnp.minimum(j, 1))),
        pl.BlockSpec((KCONV, tc), lambda j, i: (0, j)), pl.BlockSpec((1, tc), lambda j, i: (0, j)),
    ]
    return pl.pallas_call(
        body,
        out_shape=(jax.ShapeDtypeStruct((t, CONVD), F32), jax.ShapeDtypeStruct((8, CONVD), F32),
                   jax.ShapeDtypeStruct((8, CONVD), F32)),
        grid=(CONVD // tc, t // tm), in_specs=in_specs,
        out_specs=(pl.BlockSpec((tm, tc), lambda j, i: (i, j)),
                   pl.BlockSpec((8, tc), lambda j, i: (0, j)), pl.BlockSpec((8, tc), lambda j, i: (0, j))),
        scratch_shapes=[pltpu.VMEM((tm + 16, tc), F32)],
        name="conv_dpre", compiler_params=_params(("parallel", "arbitrary")))(
            u, u, u, dxs_f, dxs_b, dy, dbc_f, dbc_b, dsk_row, conv_w, conv_b)


def _conv_dx(du, dpre, conv_w):
    t = dpre.shape[0]
    tm, tc = CONV_TM, CONV_TC
    r8 = tm // 8
    nb8 = t // 8

    def body(prev_ref, cur_ref, next_ref, w_ref, du_in, du_out, ext):
        del du_in
        _fill_ext(ext, prev_ref, cur_ref, next_ref, tm, pl.program_id(1), t // tm - 1)
        for c0 in range(0, tc, CONV_CC):
            cs = slice(c0, c0 + CONV_CC)
            w = w_ref[:, cs]
            for r0 in range(0, tm, CONV_RC):
                acc = jnp.zeros((CONV_RC, CONV_CC), F32)
                for k in range(KCONV):
                    acc = acc + w[k:k + 1, :] * ext[pl.ds(r0 + 10 - k, CONV_RC), cs]
                du_out[r0:r0 + CONV_RC, cs] = acc.astype(du_out.dtype)

    in_specs = [
        pl.BlockSpec((8, tc), lambda j, i: (jnp.maximum(i * r8 - 1, 0), j)),
        pl.BlockSpec((tm, tc), lambda j, i: (i, j)),
        pl.BlockSpec((8, tc), lambda j, i: (jnp.minimum((i + 1) * r8, nb8 - 1), j)),
        pl.BlockSpec((KCONV, tc), lambda j, i: (0, j)),
        pl.BlockSpec(memory_space=pl.ANY),
    ]
    return pl.pallas_call(
        body, out_shape=jax.ShapeDtypeStruct(du.shape, du.dtype), grid=(CONVD // tc, t // tm), in_specs=in_specs,
        out_specs=pl.BlockSpec((tm, tc), lambda j, i: (i, OXBC // tc + j)),
        scratch_shapes=[pltpu.VMEM((tm + 16, tc), F32)], input_output_aliases={4: 0},
        name="conv_dx", compiler_params=_params(("parallel", "parallel")))(dpre, dpre, dpre, conv_w, du)


def _ssd_common(dtr_ref, par_ref, rev):
    raw = dtr_ref[...]
    lane = _iota((1, 128), 1)
    mine = (lane >= 32 * rev) & (lane < 32 * rev + 32)
    bias = par_ref[0:1, :]
    arow = jnp.where(mine, -jnp.exp(par_ref[1:2, :]), 0.0)
    dt = _softplus(raw + bias)
    a = dt * arow
    ri = _iota((Q, Q), 0)
    ci = _iota((Q, Q), 1)
    tri = (ci >= ri) if rev else (ci <= ri)
    trit = (ci <= ri) if rev else (ci >= ri)
    cs = _dot01_l(tri.astype(BF16), a)
    return raw, bias, arow, mine, dt, cs, tri, trit


def _expand_mat(rev):
    r = np.arange(128)[:, None]
    c = np.arange(DI)[None, :]
    return jnp.asarray(r == (c // HP) + 32 * rev, BF16)


def _sum_mat(rev):
    r = np.arange(DI)[:, None]
    c = np.arange(128)[None, :]
    return jnp.asarray(c == (r // HP) + 32 * rev, BF16)


def _ssd_fwd(xbc, u, par, y_add=None, *, rev):
    t = xbc.shape[0]
    nc = t // Q
    end = 0 if rev else Q - 1
    cmap = (lambda c: nc - 1 - c) if rev else (lambda c: c)

    def body(xbc_ref, dtr_ref, par_ref, ex_ref, *rest):
        yadd_ref = rest[0] if y_add is not None else None
        y_ref, st_ref, h_scr = rest[-3:]
        step = pl.program_id(0)

        @pl.when(step == 0)
        def _():
            h_scr[...] = jnp.zeros((NS, DI), F32)

        raw, bias, arow, mine, dt, cs, tri, trit = _ssd_common(dtr_ref, par_ref, rev)
        cst = cs.T
        dtt = dt.T
        tot_col = cst[:, end:end + 1]
        wt = dtt * jnp.exp(tot_col - cst)
        gam = jnp.exp(cs[end:end + 1, :])
        gam_x = _dot01(jnp.broadcast_to(gam, (8, 128)), ex_ref[...])[0:1, :]
        lane = _iota((Q, 128), 1)
        sel = lane < HP
        st_ref[...] = h_scr[...]
        for g in range(NG):
            bg = xbc_ref[:, DI + NS * g:DI + NS * (g + 1)]
            cg = xbc_ref[:, DI + NG * NS + NS * g:DI + NG * NS + NS * (g + 1)]
            cb = _dot_nt(cg.astype(BF16), bg.astype(BF16))
            bt = bg.T
            for k in range(4):
                lo = 512 * g + 128 * k
                xp = xbc_ref[:, lo:lo + 128].astype(BF16)
                hp = h_scr[:, lo:lo + 128]
                rhs = jnp.concatenate([xp, hp.astype(BF16)], axis=0)
                lhs, bts = [], []
                for j in range(2):
                    hc = 8 * g + 2 * k + j + 32 * rev
                    csc = jnp.broadcast_to(cs[:, hc:hc + 1], (Q, Q))
                    lm = jnp.exp(jnp.where(tri, csc - cst[hc:hc + 1, :], NEG)) * dtt[hc:hc + 1, :]
                    mh = (cb * lm).astype(BF16)
                    ec = (jnp.exp(csc) * cg).astype(BF16)
                    lhs.append(jnp.concatenate([mh, ec], axis=1))
                    bts.append((bt * wt[hc:hc + 1, :]).astype(BF16))
                ys = jnp.dot(jnp.concatenate(lhs, axis=0), rhs, preferred_element_type=F32)
                ss = jnp.dot(jnp.concatenate(bts, axis=0), xp, preferred_element_type=F32)
                yp = jnp.where(sel, ys[0:Q], ys[Q:2 * Q])
                y_ref[:, lo:lo + 128] = yp if yadd_ref is None else yp + yadd_ref[:, lo:lo + 128]
                h_scr[:, lo:lo + 128] = gam_x[:, lo:lo + 128] * hp + jnp.where(sel, ss[0:NS], ss[NS:2 * NS])

    return pl.pallas_call(
        body,
        out_shape=(jax.ShapeDtypeStruct((t, DI), F32), jax.ShapeDtypeStruct((nc, NS, DI), F32)),
        grid=(nc,),
        in_specs=[pl.BlockSpec((Q, CONVD), lambda c: (cmap(c), 0)),
                  pl.BlockSpec((Q, 128), lambda c: (cmap(c), ODT // 128)),
                  pl.BlockSpec((8, 128), lambda c: (0, 0)),
                  pl.BlockSpec((128, DI), lambda c: (0, 0))]
        + ([pl.BlockSpec((Q, DI), lambda c: (cmap(c), 0))] if y_add is not None else []),
        out_specs=(pl.BlockSpec((Q, DI), lambda c: (cmap(c), 0)),
                   pl.BlockSpec((None, NS, DI), lambda c: (cmap(c), 0, 0))),
        scratch_shapes=[pltpu.VMEM((NS, DI), F32)],
        name="ssd_fwd_rev" if rev else "ssd_fwd", compiler_params=_params(("arbitrary",)))(
            xbc, u, par, _expand_mat(rev), *([y_add] if y_add is not None else []))


def _ssd_bwd(xbc, u, par, dy, st, *, rev, side=None):
    t = xbc.shape[0]
    nc = t // Q
    end = 0 if rev else Q - 1
    cmap = (lambda c: c) if rev else (lambda c: nc - 1 - c)

    def body(xbc_ref, dtr_ref, par_ref, dy_ref, hin_ref, ex_ref, sm_ref, dxs_ref, dbc_ref, ddt_ref, acc_ref, dh_scr):
        step = pl.program_id(0)

        @pl.when(step == 0)
        def _():
            dh_scr[...] = jnp.zeros((NS, DI), F32)

        raw, bias, arow, mine, dt, cs, tri, trit = _ssd_common(dtr_ref, par_ref, rev)
        ri = _iota((Q, Q), 0)
        ci = _iota((Q, Q), 1)
        stri = ((ri > ci) if rev else (ri < ci)).astype(BF16)
        strit = ((ci > ri) if rev else (ci < ri)).astype(BF16)
        cst = cs.T
        dtt = dt.T
        et = jnp.exp(cst)
        expand = ex_ref[...]
        summat = sm_ref[...]
        gam = jnp.exp(cs[end:end + 1, :])
        gam_x = _dot01(jnp.broadcast_to(gam, (8, 128)), expand)[0:1, :]
        dt_hi, dt_mid, _ = _split3(dt)
        dtx = (jnp.dot(dt_hi, expand, preferred_element_type=F32)
               + jnp.dot(dt_mid, expand, preferred_element_type=F32))
        lane = _iota((Q, 128), 1)
        sel = lane < HP
        dho = dh_scr[...]
        t3 = jnp.sum(dho * hin_ref[...], axis=0, keepdims=True) * gam_x
        dxs_cols, dxs2_cols, yoff_cols, a1_rows = [], [], [], []
        for g in range(NG):
            bg = xbc_ref[:, DI + NS * g:DI + NS * (g + 1)]
            cg = xbc_ref[:, DI + NG * NS + NS * g:DI + NG * NS + NS * (g + 1)]
            bb = bg.astype(BF16)
            cbf = cg.astype(BF16)
            cb = _dot_nt(cbf, bb)
            cbt = _dot_nt(bb, cbf)
            ct = cg.T
            bdh = jnp.dot(bb, dho[:, 512 * g:512 * (g + 1)].astype(BF16), preferred_element_type=F32)
            dcb = jnp.zeros((Q, Q), F32)
            dcg = jnp.zeros((Q, NS), F32)
            dbg = jnp.zeros((Q, NS), F32)
            for k in range(4):
                lo = 512 * g + 128 * k
                xpf = xbc_ref[:, lo:lo + 128]
                xp = xpf.astype(BF16)
                dyp = dy_ref[:, lo:lo + 128]
                dypb = dyp.astype(BF16)
                hinp = hin_ref[:, lo:lo + 128].astype(BF16)
                dhp = dho[:, lo:lo + 128]
                es, ws, lmds, mts, ctes, dyms, ecbs = [], [], [], [], [], [], []
                for j in range(2):
                    hc = 8 * g + 2 * k + j + 32 * rev
                    csc = jnp.broadcast_to(cs[:, hc:hc + 1], (Q, Q))
                    csr = cst[hc:hc + 1, :]
                    lmds.append(jnp.exp(jnp.where(tri, csc - csr, NEG)) * dtt[hc:hc + 1, :])
                    lmb = jnp.exp(jnp.where(trit, csr - csc, NEG))
                    mts.append((cbt * lmb).astype(BF16))
                    dyms.append(jnp.where(sel if j == 0 else ~sel, dyp, 0.0).astype(BF16))
                    ecs = jnp.exp(csc)
                    es.append(ecs)
                    ws.append(jnp.exp(cst[hc:hc + 1, end:end + 1] - csc))
                    ecbs.append((ecs * cg).astype(BF16))
                    ctes.append((ct * et[hc:hc + 1, :]).astype(BF16))
                by_dy = jnp.dot(jnp.concatenate(mts + ctes, axis=0), dypb, preferred_element_type=F32)
                dmm = _dot_nt(jnp.concatenate(dyms, axis=0), xp)
                dm0, dm1 = dmm[0:Q] * lmds[0], dmm[Q:2 * Q] * lmds[1]
                dcb = dcb + dm0 + dm1
                rr = jnp.dot(jnp.concatenate([dm0 * cb, dm1 * cb], axis=0).astype(BF16), stri, preferred_element_type=F32)
                a1_rows.append(jnp.sum(jnp.where(tri, rr[0:Q], 0.0), axis=0, keepdims=True))
                a1_rows.append(jnp.sum(jnp.where(tri, rr[Q:2 * Q], 0.0), axis=0, keepdims=True))
                yo = jnp.dot(jnp.concatenate(ecbs, axis=0), hinp, preferred_element_type=F32)
                e_p = jnp.where(sel, es[0], es[1])
                w_p = jnp.where(sel, ws[0], ws[1])
                d2 = w_p * bdh[:, 128 * k:128 * (k + 1)]
                dxs2_cols.append(d2)
                dxs_cols.append(jnp.where(sel, by_dy[0:Q], by_dy[Q:2 * Q]) + d2)
                yoff_cols.append(jnp.where(sel, yo[0:Q], yo[Q:2 * Q]))
                dcg = dcg + _dot_nt((e_p * dyp).astype(BF16), hinp)
                dbg = dbg + _dot_nt((w_p * dtx[:, lo:lo + 128] * xpf).astype(BF16), dhp.astype(BF16))
                dh_scr[:, lo:lo + 128] = (gam_x[:, lo:lo + 128] * dhp
                                          + jnp.where(sel, by_dy[2 * Q:3 * Q], by_dy[3 * Q:4 * Q]))
            dcg = dcg + jnp.dot(dcb.astype(BF16), bb, preferred_element_type=F32)
            dbg = dbg + jnp.dot(dcb.T.astype(BF16), cbf, preferred_element_type=F32)
            dbc_ref[:, NS * g:NS * (g + 1)] = dbg
            dbc_ref[:, NG * NS + NS * g:NG * NS + NS * (g + 1)] = dcg
        dxs = jnp.concatenate(dxs_cols, axis=1)
        dxs_ref[...] = dxs * dtx
        xs = xbc_ref[:, 0:DI]
        stacked = jnp.concatenate([xs * dxs, xs * jnp.concatenate(dxs2_cols, axis=1),
                                   dy_ref[...] * jnp.concatenate(yoff_cols, axis=1),
                                   jnp.broadcast_to(t3, (8, DI))], axis=0).astype(BF16)
        sums = jnp.dot(stacked, summat, preferred_element_type=F32)
        rx, rx2, ryo, c0 = sums[0:Q], sums[Q:2 * Q], sums[2 * Q:3 * Q], sums[3 * Q:3 * Q + 1]
        zero32 = jnp.zeros((32, Q), F32)
        a1t = jnp.concatenate(([zero32] if rev else []) + a1_rows + [zero32] * (2 if rev else 3), axis=0)
        da = (a1t.T + jnp.dot(trit.astype(BF16), ryo.astype(BF16), preferred_element_type=F32)
              + jnp.dot(strit, (dt * rx2).astype(BF16), preferred_element_type=F32) + jnp.where(mine, c0, 0.0))
        ddt = rx + da * arow
        ddtr = ddt * _sigmoid(raw + bias)
        ddt_ref[...] = ddtr
        part = jnp.concatenate([jnp.sum(ddtr, axis=0, keepdims=True),
                                jnp.sum(da * dt, axis=0, keepdims=True) * arow,
                                jnp.zeros((6, 128), F32)], axis=0)

        @pl.when(step == 0)
        def _():
            acc_ref[...] = part

        @pl.when(step > 0)
        def _():
            acc_ref[...] += part

    outs, side_outs = _host_call(
        body, side, nc,
        out_shape=(jax.ShapeDtypeStruct((t, DI), F32), jax.ShapeDtypeStruct((t, 2 * NG * NS), F32),
                   jax.ShapeDtypeStruct((t, 128), F32), jax.ShapeDtypeStruct((8, 128), F32)),
        in_specs=[pl.BlockSpec((Q, CONVD), lambda c: (cmap(c), 0)),
                  pl.BlockSpec((Q, 128), lambda c: (cmap(c), ODT // 128)),
                  pl.BlockSpec((8, 128), lambda c: (0, 0)),
                  pl.BlockSpec((Q, DI), lambda c: (cmap(c), 0)),
                  pl.BlockSpec((None, NS, DI), lambda c: (cmap(c), 0, 0)),
                  pl.BlockSpec((128, DI), lambda c: (0, 0)), pl.BlockSpec((DI, 128), lambda c: (0, 0))],
        out_specs=(pl.BlockSpec((Q, DI), lambda c: (cmap(c), 0)),
                   pl.BlockSpec((Q, 2 * NG * NS), lambda c: (cmap(c), 0)),
                   pl.BlockSpec((Q, 128), lambda c: (cmap(c), 0)),
                   pl.BlockSpec((8, 128), lambda c: (0, 0))),
        scratch_shapes=[pltpu.VMEM((NS, DI), F32)],
        args=(xbc, u, par, dy, st, _expand_mat(rev), _sum_mat(rev)), aliases={},
        name="ssd_bwd_rev" if rev else "ssd_bwd", sem=("arbitrary",))
    return (*outs, side_outs)


GN_TM = 256
GN_GROUP = DI // NG


def _gn_forward_vals(y0, xs, z, dsk):
    y = y0 + dsk * xs
    sz = _sigmoid(z)
    gate = z * sz
    y2 = y * gate
    parts, rs = [], []
    for g in range(NG):
        seg = y2[:, GN_GROUP * g:GN_GROUP * (g + 1)]
        r = lax.rsqrt(jnp.mean(seg * seg, axis=1, keepdims=True) + NORM_EPS)
        rs.append(r)
        parts.append(seg * r)
    yn = jnp.concatenate(parts, axis=1)
    return y, sz, gate, yn, rs


def _gatenorm_fwd(y_fb, xbc, u, dsk_row, nw_row):
    t = y_fb.shape[0]
    tm = GN_TM

    def body(y_ref, xs_ref, z_ref, dsk_ref, nw_ref, o_ref):
        _, _, _, yn, _ = _gn_forward_vals(y_ref[...], xs_ref[...], z_ref[...], dsk_ref[...])
        o_ref[...] = (yn * nw_ref[...]).astype(BF16)

    blk = pl.BlockSpec((tm, DI), lambda i: (i, 0))
    row = pl.BlockSpec((1, DI), lambda i: (0, 0))
    return pl.pallas_call(
        body, out_shape=jax.ShapeDtypeStruct((t, DI), BF16), grid=(t // tm,),
        in_specs=[blk, blk, pl.BlockSpec((tm, DI), lambda i: (i, OZ // DI)), row, row],
        out_specs=blk, name="gatenorm_fwd", compiler_params=_params(("parallel",)))(y_fb, xbc, u, dsk_row, nw_row)


def _gatenorm_bwd(ds_out, y_fb, xbc, u, du, dsk_row, nw_row, side=None):
    t = y_fb.shape[0]
    tm = GN_TM

    def body(ds_ref, y_ref, xs_ref, z_ref, dsk_ref, nw_ref, sm_ref, du_in, dy_ref, du_out, dnw_ref, dds_ref):
        del du_in
        i = pl.program_id(0)
        xs = xs_ref[...]
        z = z_ref[...]
        y, sz, gate, yn, rs = _gn_forward_vals(y_ref[...], xs, z, dsk_ref[...])
        ds = ds_ref[...]
        gsc = ds * nw_ref[...]
        parts = []
        for g in range(NG):
            sl = slice(GN_GROUP * g, GN_GROUP * (g + 1))
            m = jnp.mean(gsc[:, sl] * yn[:, sl], axis=1, keepdims=True)
            parts.append(rs[g] * (gsc[:, sl] - yn[:, sl] * m))
        dy2 = jnp.concatenate(parts, axis=1)
        dy = dy2 * gate
        dy_ref[...] = dy
        du_out[...] = (dy2 * y * (sz * (1.0 + z * (1.0 - sz)))).astype(du_out.dtype)
        dnw = jnp.broadcast_to(jnp.sum(ds * yn, axis=0, keepdims=True), (8, DI))
        drow = jnp.broadcast_to(jnp.sum(dy * xs, axis=0, keepdims=True), (8, DI))
        dds = _dot01(drow, sm_ref[...])

        @pl.when(i == 0)
        def _():
            dnw_ref[...] = dnw
            dds_ref[...] = dds

        @pl.when(i > 0)
        def _():
            dnw_ref[...] += dnw
            dds_ref[...] += dds

    blk = pl.BlockSpec((tm, DI), lambda i: (i, 0))
    row = pl.BlockSpec((1, DI), lambda i: (0, 0))
    outs, side_outs = _host_call(
        body, side, t // tm,
        out_shape=(jax.ShapeDtypeStruct((t, DI), F32), jax.ShapeDtypeStruct(du.shape, du.dtype),
                   jax.ShapeDtypeStruct((8, DI), F32), jax.ShapeDtypeStruct((8, 128), F32)),
        in_specs=[blk, blk, blk, pl.BlockSpec((tm, DI), lambda i: (i, OZ // DI)), row, row,
                  pl.BlockSpec((DI, 128), lambda i: (0, 0)), pl.BlockSpec(memory_space=pl.ANY)],
        out_specs=(blk, pl.BlockSpec((tm, DI), lambda i: (i, OZ // DI)),
                   pl.BlockSpec((8, DI), lambda i: (0, 0)), pl.BlockSpec((8, 128), lambda i: (0, 0))),
        scratch_shapes=[], args=(ds_out, y_fb, xbc, u, dsk_row, nw_row, _sum_mat(0), du), aliases={7: 1},
        name="gatenorm_bwd", sem=("arbitrary",))
    return (*outs, side_outs)


AT_B = 128
AT_W = AT_B + 2 * ATT_HALF
AT_L = 2 * AH
SCALE = 1.0 / math.sqrt(AH)


def _slope(g, hh):
    return 2.0 ** (-8.0 * (4 * g + hh + 1) / 12.0)


def _qcol(g):
    return lambda p: OQ // AT_L + 2 * g + p


def _kcol(g):
    return lambda p: OKV // AT_L + 4 * g + 2 * p


def _vcol(g):
    return lambda p: OKV // AT_L + 4 * g + 2 * p + 1


def _pcol(p):
    return p


def _sub(d):
    return 4 if d == 1 else 1


def _win_specs(col, t, d):
    tb, hb = AT_B * d * _sub(d), ATT_HALF * d
    per = tb // hb
    nh = t // hb
    return [
        pl.BlockSpec((hb, AT_L), lambda p, i: (jnp.maximum(per * i - 1, 0), col(p))),
        pl.BlockSpec((tb, AT_L), lambda p, i: (i, col(p))),
        pl.BlockSpec((hb, AT_L), lambda p, i: (jnp.minimum(per * (i + 1), nh - 1), col(p))),
    ]


def _blk_spec(col, d):
    return pl.BlockSpec((AT_B * d * _sub(d), AT_L), lambda p, i: (i, col(p)))


def _rows(ref, r, s, d):
    return ref[pl.ds(r, AT_B, stride=d), :] if d > 1 else ref[AT_B * s:AT_B * (s + 1), :]


def _win(p_ref, c_ref, n_ref, r, s, d):
    if d > 1:
        return jnp.concatenate([p_ref[pl.ds(r, ATT_HALF, stride=d), :], c_ref[pl.ds(r, AT_B, stride=d), :],
                                n_ref[pl.ds(r, ATT_HALF, stride=d), :]], axis=0)
    if s == 0:
        return jnp.concatenate([p_ref[...], c_ref[0:AT_B + ATT_HALF, :]], axis=0)
    if s == _sub(d) - 1:
        return jnp.concatenate([c_ref[AT_B * s - ATT_HALF:AT_B * (s + 1), :], n_ref[...]], axis=0)
    return c_ref[AT_B * s - ATT_HALF:AT_B * (s + 1) + ATT_HALF, :]


def _put_rows(ref, r, s, d, val):
    if d > 1:
        ref[pl.ds(r, AT_B, stride=d), :] = val
    else:
        ref[AT_B * s:AT_B * (s + 1), :] = val


def _for_blocks(d, fn):
    if d == 1:
        for s in range(_sub(d)):
            fn(0, s)
    else:
        def step(r, c):
            fn(r, 0)
            return c
        lax.fori_loop(0, d, step, 0, unroll=4)


def _attn_bias(blk, ln, d, g, p_id):
    a = blk * AT_B + _iota((AT_B, AT_W), 0)
    b = blk * AT_B - ATT_HALF + _iota((AT_B, AT_W), 1)
    rel = jnp.abs(a - b)
    valid = (rel <= ATT_HALF) & (b >= 0) & (b < ln)
    dist = (rel * d).astype(F32)
    out = []
    for hh in range(2):
        slope = jnp.where(p_id == 0, _slope(g, hh), _slope(g, 2 + hh))
        out.append(jnp.where(valid, -slope * dist, NEG))
    return out


def _attn_fwd(u, g):
    t = u.shape[0]
    d = DILATIONS[g]
    ln = t // d

    def body(q_ref, kp, kc, kn, vp, vc, vn, o_ref, l_ref):
        p_id = pl.program_id(0)
        i = pl.program_id(1)
        lane = _iota((AT_B, AT_L), 1)
        biases = [_attn_bias(i * _sub(d) + s, ln, d, g, p_id) for s in range(_sub(d))]

        def one(r, s):
            q = _rows(q_ref, r, s, d)
            kw = _win(kp, kc, kn, r, s, d).astype(BF16)
            vw = _win(vp, vc, vn, r, s, d).astype(BF16)
            o = jnp.zeros((AT_B, AT_L), F32)
            lse = jnp.zeros((AT_B, AT_L), F32)
            for hh in range(2):
                hm = (lane // AH) == hh
                qm = jnp.where(hm, q, 0.0).astype(BF16)
                sc = _dot_nt(qm, kw) * SCALE + biases[s][hh]
                m = jnp.max(sc, axis=1, keepdims=True)
                pr = jnp.exp(sc - m)
                den = jnp.sum(pr, axis=1, keepdims=True)
                oh = jnp.dot(pr.astype(BF16), vw, preferred_element_type=F32)
                o = jnp.where(hm, oh / den, o)
                lse = jnp.where(hm, m + jnp.log(den), lse)
            _put_rows(o_ref, r, s, d, o)
            _put_rows(l_ref, r, s, d, lse)

        _for_blocks(d, one)

    oshape = jax.ShapeDtypeStruct((t, 2 * AT_L), F32)
    ospec = _blk_spec(_pcol, d)
    return pl.pallas_call(
        body, out_shape=(oshape, oshape), grid=(2, t // (AT_B * d * _sub(d))),
        in_specs=[_blk_spec(_qcol(g), d)] + _win_specs(_kcol(g), t, d) + _win_specs(_vcol(g), t, d),
        out_specs=(ospec, ospec), name=f"attn_fwd_{g}", compiler_params=_params(("parallel", "parallel")))(
            u, u, u, u, u, u, u)


def _attn_dq(u, du, do, lse, e, g):
    t = u.shape[0]
    d = DILATIONS[g]
    ln = t // d

    def body(q_ref, kp, kc, kn, vp, vc, vn, do_ref, l_ref, e_ref, du_in, dq_ref, dq_scr):
        del du_in
        p_id = pl.program_id(0)
        i = pl.program_id(1)
        lane = _iota((AT_B, AT_L), 1)
        biases = [_attn_bias(i * _sub(d) + s, ln, d, g, p_id) for s in range(_sub(d))]

        def one(r, s):
            q = _rows(q_ref, r, s, d)
            kw = _win(kp, kc, kn, r, s, d).astype(BF16)
            vw = _win(vp, vc, vn, r, s, d).astype(BF16)
            do_ = _rows(do_ref, r, s, d)
            lv = _rows(l_ref, r, s, d)
            ev = _rows(e_ref, r, s, d)
            dq = jnp.zeros((AT_B, AT_L), F32)
            for hh in range(2):
                hm = (lane // AH) == hh
                qm = jnp.where(hm, q, 0.0).astype(BF16)
                sc = _dot_nt(qm, kw) * SCALE + biases[s][hh]
                lcol = jnp.broadcast_to(lv[:, AH * hh:AH * hh + 1], (AT_B, AT_W))
                ecol = jnp.broadcast_to(ev[:, AH * hh:AH * hh + 1], (AT_B, AT_W))
                pr = jnp.exp(sc - lcol)
                dom = jnp.where(hm, do_, 0.0).astype(BF16)
                ds = pr * (_dot_nt(dom, vw) + ecol)
                dqh = jnp.dot(ds.astype(BF16), kw, preferred_element_type=F32) * SCALE
                dq = jnp.where(hm, dqh, dq)
            _put_rows(dq_scr, r, s, d, dq)

        _for_blocks(d, one)
        dq_ref[...] = dq_scr[...].astype(dq_ref.dtype)

    rspec = _blk_spec(_pcol, d)
    return pl.pallas_call(
        body, out_shape=jax.ShapeDtypeStruct(du.shape, du.dtype), grid=(2, t // (AT_B * d * _sub(d))),
        in_specs=[_blk_spec(_qcol(g), d)] + _win_specs(_kcol(g), t, d) + _win_specs(_vcol(g), t, d)
        + [rspec, rspec, rspec, pl.BlockSpec(memory_space=pl.ANY)],
        out_specs=_blk_spec(_qcol(g), d), input_output_aliases={10: 0},
        scratch_shapes=[pltpu.VMEM((AT_B * d * _sub(d), AT_L), F32)],
        name=f"attn_dq_{g}", compiler_params=_params(("parallel", "parallel")))(
            u, u, u, u, u, u, u, do, lse, e, du)


def _attn_dkv(u, du, do, lse, e, g):
    t = u.shape[0]
    d = DILATIONS[g]
    ln = t // d

    def body(k_ref, v_ref, qp, qc, qn, dp_, dc_, dn_, lp, lc, ln_, ep, ec, en, du_in, dkv_ref, dk_scr, dv_scr):
        del du_in
        p_id = pl.program_id(0)
        jb = pl.program_id(1)
        lane = _iota((AT_B, AT_L), 1)
        biases = [_attn_bias(jb * _sub(d) + s, ln, d, g, p_id) for s in range(_sub(d))]

        def one(r, s):
            k = _rows(k_ref, r, s, d)
            v = _rows(v_ref, r, s, d)
            qw = _win(qp, qc, qn, r, s, d).astype(BF16)
            dow = _win(dp_, dc_, dn_, r, s, d).astype(BF16)
            lt = _win(lp, lc, ln_, r, s, d).T
            et = _win(ep, ec, en, r, s, d).T
            dk = jnp.zeros((AT_B, AT_L), F32)
            dv = jnp.zeros((AT_B, AT_L), F32)
            for hh in range(2):
                hm = (lane // AH) == hh
                km = jnp.where(hm, k, 0.0).astype(BF16)
                st = _dot_nt(km, qw) * SCALE + biases[s][hh]
                pt = jnp.exp(st - lt[AH * hh:AH * hh + 1, :])
                dvh = jnp.dot(pt.astype(BF16), dow, preferred_element_type=F32)
                vm = jnp.where(hm, v, 0.0).astype(BF16)
                dst = pt * (_dot_nt(vm, dow) + et[AH * hh:AH * hh + 1, :])
                dkh = jnp.dot(dst.astype(BF16), qw, preferred_element_type=F32) * SCALE
                dk = jnp.where(hm, dkh, dk)
                dv = jnp.where(hm, dvh, dv)
            _put_rows(dk_scr, r, s, d, dk)
            _put_rows(dv_scr, r, s, d, dv)

        _for_blocks(d, one)
        dkv_ref[:, 0:AT_L] = dk_scr[...].astype(dkv_ref.dtype)
        dkv_ref[:, AT_L:2 * AT_L] = dv_scr[...].astype(dkv_ref.dtype)

    return pl.pallas_call(
        body, out_shape=jax.ShapeDtypeStruct(du.shape, du.dtype), grid=(2, t // (AT_B * d * _sub(d))),
        in_specs=[_blk_spec(_kcol(g), d), _blk_spec(_vcol(g), d)]
        + _win_specs(_qcol(g), t, d) + _win_specs(_pcol, t, d) + _win_specs(_pcol, t, d) + _win_specs(_pcol, t, d)
        + [pl.BlockSpec(memory_space=pl.ANY)],
        out_specs=pl.BlockSpec((AT_B * d * _sub(d), 2 * AT_L), lambda p, i: (i, OKV // (2 * AT_L) + 2 * g + p)),
        input_output_aliases={14: 0},
        scratch_shapes=[pltpu.VMEM((AT_B * d * _sub(d), AT_L), F32), pltpu.VMEM((AT_B * d * _sub(d), AT_L), F32)],
        name=f"attn_dkv_{g}", compiler_params=_params(("parallel", "parallel")))(
            u, u, u, u, u, do, do, do, lse, lse, lse, e, e, e, du)


CMB_TM = 1024


def _combine_weights(l0, l1, l2):
    m = jnp.maximum(jnp.maximum(l0, l1), l2)
    e0, e1, e2 = jnp.exp(l0 - m), jnp.exp(l1 - m), jnp.exp(l2 - m)
    inv = 1.0 / (e0 + e1 + e2)
    return e0 * inv, e1 * inv, e2 * inv


def _combine_fwd(os_, ls_):
    t = os_[0].shape[0]
    tm = CMB_TM

    def body(o0, o1, o2, l0, l1, l2, a_ref):
        w0, w1, w2 = _combine_weights(l0[...], l1[...], l2[...])
        a_ref[...] = w0 * o0[...] + w1 * o1[...] + w2 * o2[...]

    blk = pl.BlockSpec((tm, 2 * AT_L), lambda i: (i, 0))
    return pl.pallas_call(
        body, out_shape=jax.ShapeDtypeStruct((t, 2 * AT_L), F32), grid=(t // tm,), in_specs=[blk] * 6, out_specs=blk,
        name="combine_fwd", compiler_params=_params(("parallel",)))(*os_, *ls_)


def _combine_bwd(datt, os_, ls_):
    t = datt.shape[0]
    tm = CMB_TM

    def body(da_ref, o0, o1, o2, l0, l1, l2, d0, d1, d2, e0, e1, e2):
        w = _combine_weights(l0[...], l1[...], l2[...])
        da = da_ref[...]
        att = w[0] * o0[...] + w[1] * o1[...] + w[2] * o2[...]
        r = _iota((2 * AT_L, 2 * AT_L), 0) // AH
        c = _iota((2 * AT_L, 2 * AT_L), 1) // AH
        hs = _dot01(da * att, (r == c).astype(BF16))
        for wg, dref, eref in zip(w, (d0, d1, d2), (e0, e1, e2)):
            dref[...] = wg * da
            eref[...] = -wg * hs

    blk = pl.BlockSpec((tm, 2 * AT_L), lambda i: (i, 0))
    shp = jax.ShapeDtypeStruct((t, 2 * AT_L), F32)
    outs = pl.pallas_call(
        body, out_shape=(shp,) * 6, grid=(t // tm,), in_specs=[blk] * 7, out_specs=(blk,) * 6,
        name="combine_bwd", compiler_params=_params(("parallel",)))(datt, *os_, *ls_)
    return outs[0:3], outs[3:6]


def _combine_proj(os_, ls_, w_pa):
    t = os_[0].shape[0]
    tm = ROW_TM
    nsh, _, ws = w_pa.shape

    def body(o0, o1, o2, l0, l1, l2, w_ref, a_ref, y_ref):
        w0, w1, w2 = _combine_weights(l0[...], l1[...], l2[...])
        att = w0 * o0[...] + w1 * o1[...] + w2 * o2[...]
        a_ref[...] = att
        ab = att.astype(BF16)
        for sh in range(nsh):
            y_ref[:, ws * sh:ws * (sh + 1)] = jnp.dot(ab, w_ref[sh], preferred_element_type=F32)

    blk = pl.BlockSpec((tm, 2 * AT_L), lambda i: (i, 0))
    return pl.pallas_call(
        body, out_shape=(jax.ShapeDtypeStruct((t, 2 * AT_L), F32), jax.ShapeDtypeStruct((t, nsh * ws), F32)),
        grid=(t // tm,), in_specs=[blk] * 6 + [pl.BlockSpec(w_pa.shape, lambda i: (0, 0, 0))],
        out_specs=(blk, pl.BlockSpec((tm, nsh * ws), lambda i: (i, 0))),
        name="combine_proj", compiler_params=_params(("parallel",)))(*os_, *ls_, w_pa)


def _d_att_combine_bwd(dy_att, w_pa, os_, ls_):
    t = dy_att.shape[0]
    tm = ROW_TM
    nsh, _, ws = w_pa.shape

    def body(dy_ref, w_ref, o0, o1, o2, l0, l1, l2, d0, d1, d2, e0, e1, e2):
        da = jnp.zeros((tm, 2 * AT_L), F32)
        for sh in range(nsh):
            da = da + _dot_nt(dy_ref[:, ws * sh:ws * (sh + 1)], w_ref[sh])
        w = _combine_weights(l0[...], l1[...], l2[...])
        att = w[0] * o0[...] + w[1] * o1[...] + w[2] * o2[...]
        r = _iota((2 * AT_L, 2 * AT_L), 0) // AH
        c = _iota((2 * AT_L, 2 * AT_L), 1) // AH
        hs = _dot01(da * att, (r == c).astype(BF16))
        for wg, dref, eref in zip(w, (d0, d1, d2), (e0, e1, e2)):
            dref[...] = wg * da
            eref[...] = -wg * hs

    blk = pl.BlockSpec((tm, 2 * AT_L), lambda i: (i, 0))
    shp = jax.ShapeDtypeStruct((t, 2 * AT_L), F32)
    outs = pl.pallas_call(
        body, out_shape=(shp,) * 6, grid=(t // tm,),
        in_specs=[pl.BlockSpec((tm, nsh * ws), lambda i: (i, 0)), pl.BlockSpec(w_pa.shape, lambda i: (0, 0, 0))] + [blk] * 6,
        out_specs=(blk,) * 6, name="d_att_combine_bwd", compiler_params=_params(("parallel",)))(dy_att, w_pa, *os_, *ls_)
    return outs[0:3], outs[3:6]


ROW_TM = 512


def _mix_fwd(y_ssd, y_att, u, bg_row):
    t = y_ssd.shape[0]
    tm = ROW_TM

    def body(ys_ref, ya_ref, g0_ref, g1_ref, b0_ref, b1_ref, o_ref):
        g0 = _sigmoid(g0_ref[...] + b0_ref[...])
        g1 = _sigmoid(g1_ref[...] + b1_ref[...])
        o_ref[...] = (g0 * ys_ref[...] + g1 * ya_ref[...]).astype(BF16)

    blk = pl.BlockSpec((tm, D), lambda i: (i, 0))
    return pl.pallas_call(
        body, out_shape=jax.ShapeDtypeStruct((t, D), BF16), grid=(t // tm,),
        in_specs=[blk, blk, pl.BlockSpec((tm, D), lambda i: (i, OGATE // D)), pl.BlockSpec((tm, D), lambda i: (i, OGATE // D + 1)),
                  pl.BlockSpec((1, D), lambda i: (0, 0)), pl.BlockSpec((1, D), lambda i: (0, 1))],
        out_specs=blk, name="mix_fwd", compiler_params=_params(("parallel",)))(y_ssd, y_att, u, u, bg_row, bg_row)


def _mix_bwd(dmixin, y_ssd, y_att, u, bg_row):
    t = y_ssd.shape[0]
    tm = ROW_TM

    def body(dm_ref, ys_ref, ya_ref, g0_ref, g1_ref, b0_ref, b1_ref, dys_ref, dya_ref, du_ref, db_ref):
        i = pl.program_id(0)
        g0 = _sigmoid(g0_ref[...] + b0_ref[...])
        g1 = _sigmoid(g1_ref[...] + b1_ref[...])
        dm = dm_ref[...]
        dys_ref[...] = (dm * g0).astype(BF16)
        dya_ref[...] = (dm * g1).astype(BF16)
        dl0 = dm * ys_ref[...] * g0 * (1.0 - g0)
        dl1 = dm * ya_ref[...] * g1 * (1.0 - g1)
        du_ref[:, 0:D] = dl0.astype(BF16)
        du_ref[:, D:2 * D] = dl1.astype(BF16)
        part = jnp.concatenate([jnp.broadcast_to(jnp.sum(dl0, axis=0, keepdims=True), (8, D)),
                                jnp.broadcast_to(jnp.sum(dl1, axis=0, keepdims=True), (8, D))], axis=1)

        @pl.when(i == 0)
        def _():
            db_ref[...] = part

        @pl.when(i > 0)
        def _():
            db_ref[...] += part

    blk = pl.BlockSpec((tm, D), lambda i: (i, 0))
    return pl.pallas_call(
        body,
        out_shape=(jax.ShapeDtypeStruct((t, D), BF16), jax.ShapeDtypeStruct((t, D), BF16),
                   jax.ShapeDtypeStruct((t, UW), BF16), jax.ShapeDtypeStruct((8, 2 * D), F32)),
        grid=(t // tm,),
        in_specs=[blk, blk, blk, pl.BlockSpec((tm, D), lambda i: (i, OGATE // D)), pl.BlockSpec((tm, D), lambda i: (i, OGATE // D + 1)),
                  pl.BlockSpec((1, D), lambda i: (0, 0)), pl.BlockSpec((1, D), lambda i: (0, 1))],
        out_specs=(blk, blk, pl.BlockSpec((tm, 2 * D), lambda i: (i, OGATE // (2 * D))),
                   pl.BlockSpec((8, 2 * D), lambda i: (0, 0))),
        name="mix_bwd", compiler_params=_params(("arbitrary",)))(dmixin, y_ssd, y_att, u, u, bg_row, bg_row)


def _ln(x, g, b):
    mu = jnp.mean(x, axis=1, keepdims=True)
    xc = x - mu
    var = jnp.mean(xc * xc, axis=1, keepdims=True)
    rstd = lax.rsqrt(var + NORM_EPS)
    xhat = xc * rstd
    return xhat * g + b, xhat, rstd


def _ln_back(dh, xhat, rstd, g):
    dxh = dh * g
    m1 = jnp.mean(dxh, axis=1, keepdims=True)
    m2 = jnp.mean(dxh * xhat, axis=1, keepdims=True)
    return rstd * (dxh - m1 - xhat * m2)


def _ln1_fwd(x, mix, g_row, b_row):
    t = x.shape[0]
    tm = ROW_TM

    def body(x_ref, m_ref, g_ref, b_ref, pre_ref, h_ref):
        pre = ALPHA * x_ref[...] + m_ref[...]
        pre_ref[...] = pre
        h, _, _ = _ln(pre, g_ref[...], b_ref[...])
        h_ref[...] = h.astype(BF16)

    blk = pl.BlockSpec((tm, D), lambda i: (i, 0))
    row = pl.BlockSpec((1, D), lambda i: (0, 0))
    return pl.pallas_call(
        body, out_shape=(jax.ShapeDtypeStruct((t, D), F32), jax.ShapeDtypeStruct((t, D), BF16)), grid=(t // tm,),
        in_specs=[blk, blk, row, row], out_specs=(blk, blk),
        name="ln1_fwd", compiler_params=_params(("parallel",)))(x, mix, g_row, b_row)


def _ln1_bwd(dh, pre, g_row, b_row):
    t = dh.shape[0]
    tm = ROW_TM

    def body(dh_ref, pre_ref, g_ref, b_ref, dpre_ref, acc_ref):
        i = pl.program_id(0)
        dh_ = dh_ref[...]
        _, xhat, rstd = _ln(pre_ref[...], g_ref[...], b_ref[...])
        dpre_ref[...] = _ln_back(dh_, xhat, rstd, g_ref[...])
        part = jnp.concatenate([jnp.sum(dh_ * xhat, axis=0, keepdims=True), jnp.sum(dh_, axis=0, keepdims=True),
                                jnp.zeros((6, D), F32)], axis=0)

        @pl.when(i == 0)
        def _():
            acc_ref[...] = part

        @pl.when(i > 0)
        def _():
            acc_ref[...] += part

    blk = pl.BlockSpec((tm, D), lambda i: (i, 0))
    row = pl.BlockSpec((1, D), lambda i: (0, 0))
    return pl.pallas_call(
        body, out_shape=(jax.ShapeDtypeStruct((t, D), F32), jax.ShapeDtypeStruct((8, D), F32)), grid=(t // tm,),
        in_specs=[blk, blk, row, row], out_specs=(blk, pl.BlockSpec((8, D), lambda i: (0, 0))),
        name="ln1_bwd", compiler_params=_params(("arbitrary",)))(dh, pre, g_row, b_row)


def _ln2_loss(pre1, f, tgt, g1_row, b1_row, g2_row, b2_row):
    t = pre1.shape[0]
    tm = ROW_TM

    def body(p1_ref, f_ref, t_ref, g1_ref, b1_ref, g2_ref, b2_ref, dpre_ref, acc_ref):
        i = pl.program_id(0)
        h1, _, _ = _ln(p1_ref[...], g1_ref[...], b1_ref[...])
        pre2 = ALPHA * h1 + f_ref[...]
        h2, xhat, rstd = _ln(pre2, g2_ref[...], b2_ref[...])
        err = h2 - t_ref[...]
        dh = err * (1.0 / D)
        dpre_ref[...] = _ln_back(dh, xhat, rstd, g2_ref[...])
        loss = jnp.sum(jnp.sum(err * err, axis=1, keepdims=True), axis=0, keepdims=True) * (0.5 / D)
        part = jnp.concatenate([jnp.sum(dh * xhat, axis=0, keepdims=True), jnp.sum(dh, axis=0, keepdims=True),
                                jnp.broadcast_to(loss, (1, D)), jnp.zeros((5, D), F32)], axis=0)

        @pl.when(i == 0)
        def _():
            acc_ref[...] = part

        @pl.when(i > 0)
        def _():
            acc_ref[...] += part

    blk = pl.BlockSpec((tm, D), lambda i: (i, 0))
    row = pl.BlockSpec((1, D), lambda i: (0, 0))
    return pl.pallas_call(
        body, out_shape=(jax.ShapeDtypeStruct((t, D), F32), jax.ShapeDtypeStruct((8, D), F32)), grid=(t // tm,),
        in_specs=[blk, blk, blk, row, row, row, row], out_specs=(blk, pl.BlockSpec((8, D), lambda i: (0, 0))),
        name="ln2_loss", compiler_params=_params(("arbitrary",)))(pre1, f, tgt, g1_row, b1_row, g2_row, b2_row)


def _mlp_up(h1, w_up):
    t = h1.shape[0]
    tm, tn = ROW_TM, D

    def body(a_ref, b_ref, up_ref, act_ref):
        up = jnp.dot(a_ref[...], b_ref[...], preferred_element_type=F32)
        up_ref[...] = up.astype(BF16)
        r = jnp.maximum(up, 0.0)
        act_ref[...] = (r * r).astype(BF16)

    blk = pl.BlockSpec((tm, tn), lambda j, i: (i, j))
    return pl.pallas_call(
        body, out_shape=(jax.ShapeDtypeStruct((t, DFF), BF16), jax.ShapeDtypeStruct((t, DFF), BF16)),
        grid=(DFF // tn, t // tm),
        in_specs=[pl.BlockSpec((tm, D), lambda j, i: (i, 0)), pl.BlockSpec((None, D, tn), lambda j, i: (j, 0, 0))],
        out_specs=(blk, blk), name="mlp_up", compiler_params=_params(("parallel", "parallel")))(h1, w_up)


def _d_up(dpre2, w_down, up):
    t = up.shape[0]
    tm, tk = ROW_TM, D

    def body(a_ref, b_ref, u_ref, o_ref):
        dact = _dot_nt(a_ref[...], b_ref[...])
        o_ref[...] = (dact * 2.0 * jnp.maximum(u_ref[...].astype(F32), 0.0)).astype(BF16)

    blk = pl.BlockSpec((tm, tk), lambda j, i: (i, j))
    return pl.pallas_call(
        body, out_shape=jax.ShapeDtypeStruct((t, DFF), BF16), grid=(DFF // tk, t // tm),
        in_specs=[pl.BlockSpec((tm, D), lambda j, i: (i, 0)), pl.BlockSpec((tk, D), lambda j, i: (j, 0)), blk],
        out_specs=blk, name="d_up", compiler_params=_params(("parallel", "parallel")))(dpre2, w_down, up)


def _dt_bwd(du, ddt_f, ddt_b):
    t = ddt_f.shape[0]
    tm = 1024

    def body(f_ref, b_ref, du_in, o_ref):
        del du_in
        o_ref[:, 0:128] = (f_ref[...] + b_ref[...]).astype(o_ref.dtype)
        o_ref[:, 128:256] = jnp.zeros((tm, 128), o_ref.dtype)

    blk = pl.BlockSpec((tm, 128), lambda i: (i, 0))
    return pl.pallas_call(
        body, out_shape=jax.ShapeDtypeStruct(du.shape, du.dtype), grid=(t // tm,),
        in_specs=[blk, blk, pl.BlockSpec(memory_space=pl.ANY)],
        out_specs=pl.BlockSpec((tm, 256), lambda i: (i, ODT // 256)), input_output_aliases={2: 0},
        name="dt_bwd", compiler_params=_params(("parallel",)))(ddt_f, ddt_b, du)


def _mix_out_ln1(y_ssd, y_att, u, bg_row, x, w_out, g_row, b_row):
    t = x.shape[0]
    tm = ROW_TM

    def body(ys_ref, ya_ref, g0_ref, g1_ref, b0_ref, b1_ref, x_ref, w_ref, g_ref, b_ref, mixin_ref, pre_ref, h_ref):
        g0 = _sigmoid(g0_ref[...] + b0_ref[...])
        g1 = _sigmoid(g1_ref[...] + b1_ref[...])
        mixin = (g0 * ys_ref[...] + g1 * ya_ref[...]).astype(BF16)
        mixin_ref[...] = mixin
        pre = ALPHA * x_ref[...] + jnp.dot(mixin, w_ref[...], preferred_element_type=F32)
        pre_ref[...] = pre
        h, _, _ = _ln(pre, g_ref[...], b_ref[...])
        h_ref[...] = h.astype(BF16)

    blk = pl.BlockSpec((tm, D), lambda i: (i, 0))
    row = pl.BlockSpec((1, D), lambda i: (0, 0))
    return pl.pallas_call(
        body,
        out_shape=(jax.ShapeDtypeStruct((t, D), BF16), jax.ShapeDtypeStruct((t, D), F32), jax.ShapeDtypeStruct((t, D), BF16)),
        grid=(t // tm,),
        in_specs=[blk, blk, pl.BlockSpec((tm, D), lambda i: (i, OGATE // D)), pl.BlockSpec((tm, D), lambda i: (i, OGATE // D + 1)),
                  row, pl.BlockSpec((1, D), lambda i: (0, 1)), blk, pl.BlockSpec((D, D), lambda i: (0, 0)), row, row],
        out_specs=(blk, blk, blk), name="mix_out_ln1", compiler_params=_params(("parallel",)))(
            y_ssd, y_att, u, u, bg_row, bg_row, x, w_out, g_row, b_row)


def _mlp_down_ln2_loss(act, w_down, pre1, tgt, g1_row, b1_row, g2_row, b2_row):
    t = pre1.shape[0]
    tm = ROW_TM

    def body(a_ref, w_ref, p1_ref, t_ref, g1_ref, b1_ref, g2_ref, b2_ref, dpre_ref, dpreb_ref, acc_ref):
        i = pl.program_id(0)
        f = jnp.dot(a_ref[...], w_ref[...], preferred_element_type=F32)
        h1, _, _ = _ln(p1_ref[...], g1_ref[...], b1_ref[...])
        pre2 = ALPHA * h1 + f
        h2, xhat, rstd = _ln(pre2, g2_ref[...], b2_ref[...])
        err = h2 - t_ref[...]
        dh = err * (1.0 / D)
        dpre = _ln_back(dh, xhat, rstd, g2_ref[...])
        dpre_ref[...] = dpre
        dpreb_ref[...] = dpre.astype(BF16)
        loss = jnp.sum(jnp.sum(err * err, axis=1, keepdims=True), axis=0, keepdims=True) * (0.5 / D)
        part = jnp.concatenate([jnp.sum(dh * xhat, axis=0, keepdims=True), jnp.sum(dh, axis=0, keepdims=True),
                                jnp.broadcast_to(loss, (1, D)), jnp.zeros((5, D), F32)], axis=0)

        @pl.when(i == 0)
        def _():
            acc_ref[...] = part

        @pl.when(i > 0)
        def _():
            acc_ref[...] += part

    blk = pl.BlockSpec((tm, D), lambda i: (i, 0))
    row = pl.BlockSpec((1, D), lambda i: (0, 0))
    return pl.pallas_call(
        body,
        out_shape=(jax.ShapeDtypeStruct((t, D), F32), jax.ShapeDtypeStruct((t, D), BF16), jax.ShapeDtypeStruct((8, D), F32)),
        grid=(t // tm,),
        in_specs=[pl.BlockSpec((tm, DFF), lambda i: (i, 0)), pl.BlockSpec((DFF, D), lambda i: (0, 0)), blk, blk, row, row, row, row],
        out_specs=(blk, blk, pl.BlockSpec((8, D), lambda i: (0, 0))),
        name="mlp_down_ln2_loss", compiler_params=_params(("arbitrary",)))(act, w_down, pre1, tgt, g1_row, b1_row, g2_row, b2_row)


def _d_h1_ln1_bwd(dup, w_up, dpre2, pre1, g_row, b_row):
    t = dup.shape[0]
    tm = ROW_TM
    nsh = w_up.shape[0]

    def body(a_ref, w_ref, add_ref, pre_ref, g_ref, b_ref, dpre_ref, acc_ref, dh_scr):
        i = pl.program_id(0)
        c = pl.program_id(1)
        part = _dot_nt(a_ref[...], w_ref[...])

        @pl.when(c == 0)
        def _():
            dh_scr[...] = part + ALPHA * add_ref[...]

        @pl.when(c > 0)
        def _():
            dh_scr[...] += part

        @pl.when(c == nsh - 1)
        def _():
            dh_ = dh_scr[...]
            _, xhat, rstd = _ln(pre_ref[...], g_ref[...], b_ref[...])
            dpre_ref[...] = _ln_back(dh_, xhat, rstd, g_ref[...])
            rows = jnp.concatenate([jnp.sum(dh_ * xhat, axis=0, keepdims=True), jnp.sum(dh_, axis=0, keepdims=True),
                                    jnp.zeros((6, D), F32)], axis=0)

            @pl.when(i == 0)
            def _():
                acc_ref[...] = rows

            @pl.when(i > 0)
            def _():
                acc_ref[...] += rows

    blk = pl.BlockSpec((tm, D), lambda i, c: (i, 0))
    row = pl.BlockSpec((1, D), lambda i, c: (0, 0))
    return pl.pallas_call(
        body, out_shape=(jax.ShapeDtypeStruct((t, D), F32), jax.ShapeDtypeStruct((8, D), F32)),
        grid=(t // tm, nsh),
        in_specs=[pl.BlockSpec((tm, D), lambda i, c: (i, c)), pl.BlockSpec((None, D, D), lambda i, c: (c, 0, 0)),
                  blk, blk, row, row],
        out_specs=(blk, pl.BlockSpec((8, D), lambda i, c: (0, 0))),
        scratch_shapes=[pltpu.VMEM((tm, D), F32)],
        name="d_h1_ln1_bwd", compiler_params=_params(("arbitrary", "arbitrary")))(dup, w_up, dpre2, pre1, g_row, b_row)


def _d_mixin_mix_bwd(dpre1, w_out, y_ssd, y_att, u, bg_row):
    t = y_ssd.shape[0]
    tm = ROW_TM

    def body(a_ref, w_ref, ys_ref, ya_ref, g0_ref, g1_ref, b0_ref, b1_ref, dys_ref, dya_ref, du_ref, db_ref):
        i = pl.program_id(0)
        dm = _dot_nt(a_ref[...].astype(BF16), w_ref[...])
        g0 = _sigmoid(g0_ref[...] + b0_ref[...])
        g1 = _sigmoid(g1_ref[...] + b1_ref[...])
        dys_ref[...] = (dm * g0).astype(BF16)
        dya_ref[...] = (dm * g1).astype(BF16)
        dl0 = dm * ys_ref[...] * g0 * (1.0 - g0)
        dl1 = dm * ya_ref[...] * g1 * (1.0 - g1)
        du_ref[:, 0:D] = dl0.astype(BF16)
        du_ref[:, D:2 * D] = dl1.astype(BF16)
        part = jnp.concatenate([jnp.broadcast_to(jnp.sum(dl0, axis=0, keepdims=True), (8, D)),
                                jnp.broadcast_to(jnp.sum(dl1, axis=0, keepdims=True), (8, D))], axis=1)

        @pl.when(i == 0)
        def _():
            db_ref[...] = part

        @pl.when(i > 0)
        def _():
            db_ref[...] += part

    blk = pl.BlockSpec((tm, D), lambda i: (i, 0))
    return pl.pallas_call(
        body,
        out_shape=(jax.ShapeDtypeStruct((t, D), BF16), jax.ShapeDtypeStruct((t, D), BF16),
                   jax.ShapeDtypeStruct((t, UW), BF16), jax.ShapeDtypeStruct((8, 2 * D), F32)),
        grid=(t // tm,),
        in_specs=[blk, pl.BlockSpec((D, D), lambda i: (0, 0)), blk, blk,
                  pl.BlockSpec((tm, D), lambda i: (i, OGATE // D)), pl.BlockSpec((tm, D), lambda i: (i, OGATE // D + 1)),
                  pl.BlockSpec((1, D), lambda i: (0, 0)), pl.BlockSpec((1, D), lambda i: (0, 1))],
        out_specs=(blk, blk, pl.BlockSpec((tm, 2 * D), lambda i: (i, OGATE // (2 * D))),
                   pl.BlockSpec((8, 2 * D), lambda i: (0, 0))),
        name="d_mixin_mix_bwd", compiler_params=_params(("arbitrary",)))(dpre1, w_out, y_ssd, y_att, u, u, bg_row, bg_row)


def _adamw(w, g, m, v, name):
    r, c = w.shape
    tr = r
    for cand in (256, 128, 64, 32, 16, 8):
        if r % cand == 0 and cand * c * 4 <= 2 ** 21:
            tr = cand
            break
    bc1 = 1.0 / (1.0 - ADAM_B1 ** ADAM_STEP)
    bc2 = 1.0 / (1.0 - ADAM_B2 ** ADAM_STEP)

    def body(w_ref, g_ref, m_ref, v_ref, d_ref, nm_ref, nv_ref):
        gg = g_ref[...]
        nm = ADAM_B1 * m_ref[...] + (1.0 - ADAM_B1) * gg
        nv = ADAM_B2 * v_ref[...] + (1.0 - ADAM_B2) * (gg * gg)
        nm_ref[...] = nm
        nv_ref[...] = nv
        d_ref[...] = -ADAM_LR * ((nm * bc1) / (jnp.sqrt(nv * bc2) + ADAM_EPS) + ADAM_WD * w_ref[...])

    blk = pl.BlockSpec((tr, c), lambda i: (i, 0))
    shp = jax.ShapeDtypeStruct((r, c), F32)
    return pl.pallas_call(body, out_shape=(shp, shp, shp), grid=(r // tr,), in_specs=[blk] * 4, out_specs=(blk,) * 3,
                          name=name, compiler_params=_params(("parallel",)))(w, g, m, v)


def _perm_cols(w):
    z, xbc, dt = w[:, 0:2048], w[:, 2048:5120], w[:, 5120:5184]
    q, k, v, gate = w[:, 5184:5952], w[:, 5952:6720], w[:, 6720:7488], w[:, 7488:9536]
    kv = []
    for g in range(3):
        for p in range(2):
            lo = 256 * g + 128 * p
            kv += [k[:, lo:lo + 128], v[:, lo:lo + 128]]
    pad = jnp.zeros((w.shape[0], UW - IN_COLS), w.dtype)
    return jnp.concatenate([z, gate, xbc] + kv + [q, dt, pad], axis=1)


def _unperm_cols(wp):
    z, gate, xbc = wp[:, OZ:OZ + 2048], wp[:, OGATE:OGATE + 2048], wp[:, OXBC:OXBC + CONVD]
    q, dt = wp[:, OQ:OQ + 768], wp[:, ODT:ODT + 64]
    ks, vs = [], []
    for g in range(3):
        for p in range(2):
            lo = OKV + 128 * (4 * g + 2 * p)
            ks.append(wp[:, lo:lo + 128])
            vs.append(wp[:, lo + 128:lo + 256])
    return jnp.concatenate([z, xbc, dt, q] + ks + vs + [gate], axis=1)


def _segments():
    segs = [(0, 2048), (7488, 9536), (2048, 5120)]
    for g in range(3):
        for p in range(2):
            lo = 256 * g + 128 * p
            segs += [(5952 + lo, 5952 + lo + 128), (6720 + lo, 6720 + lo + 128)]
    segs += [(5184, 5952), (5120, 5184)]
    out, pos = [], 0
    for a, b in segs:
        out.append((a, b, pos))
        pos += b - a
    return out


SHARD_COLS = IN_COLS // 4


def _perm_from_shards(w_shards):
    pieces = []
    for a, b, _ in _segments():
        while a < b:
            s = a // SHARD_COLS
            e = min(b, (s + 1) * SHARD_COLS)
            pieces.append(w_shards[s][:, a - s * SHARD_COLS:e - s * SHARD_COLS])
            a = e
    pieces.append(jnp.zeros((w_shards.shape[1], UW - IN_COLS), w_shards.dtype))
    return jnp.concatenate(pieces, axis=1)


def _shards_from_perm(wp):
    segs = sorted(_segments())
    shards = []
    for s in range(4):
        lo, hi = s * SHARD_COLS, (s + 1) * SHARD_COLS
        pieces = []
        for a, b, pos in segs:
            x, y = max(a, lo), min(b, hi)
            if x < y:
                pieces.append(wp[:, pos + x - a:pos + y - a])
        shards.append(jnp.concatenate(pieces, axis=1))
    return jnp.stack(shards)


def _lanes128(*vecs):
    v = jnp.concatenate([a.reshape(-1) for a in vecs])
    return jnp.pad(v, (0, 128 - v.shape[0])).reshape(1, 128)


EARLY = ("w_proj_ssd", "w_proj_attn", "w_out", "w_up", "w_down")


def _local_grads(x, tgt, wts, sm, rs_idx=None):
    row = lambda a: a.reshape(1, -1)
    bg_row, cb_row = row(sm["b_gate"]), row(sm["conv_b"])
    par = jnp.concatenate([_lanes128(sm["dt_bias_f"], sm["dt_bias_b"]), _lanes128(sm["a_log_f"], sm["a_log_b"]),
                           jnp.zeros((6, 128), F32)], axis=0)
    dsk_row = row(jnp.repeat(sm["d_skip"], HP))
    nw_row = row(sm["ssd_norm_w"])
    g1, b1, g2, b2 = row(sm["ln1_g"]), row(sm["ln1_b"]), row(sm["ln2_g"]), row(sm["ln2_b"])

    xb = x.astype(BF16)
    u = _mm_nn(xb, wts["w_in_p"], tm=512, tn=2432, name="in_proj")
    xbc = _conv_fwd(u, sm["conv_w"], cb_row)
    y_f, st_f = _ssd_fwd(xbc, u, par, rev=False)
    y_fb, st_b = _ssd_fwd(xbc, u, par, y_f, rev=True)
    s_out = _gatenorm_fwd(y_fb, xbc, u, dsk_row, nw_row)
    y_ssd = _mm_nn(s_out, wts["w_proj_ssd"], tm=512, tn=1024, name="proj_ssd")
    att_o, att_l = [], []
    for g in range(3):
        o, l = _attn_fwd(u, g)
        att_o.append(o)
        att_l.append(l)
    att, y_att = _combine_proj(att_o, att_l, wts["w_proj_attn"])
    mixin, pre1, h1 = _mix_out_ln1(y_ssd, y_att, u, bg_row, x, wts["w_out"], g1, b1)
    up, act = _mlp_up(h1, wts["w_up"])
    dpre2, dpre2_b, acc2 = _mlp_down_ln2_loss(act, wts["w_down"], pre1, tgt, g1, b1, g2, b2)

    dw_down = _mm_tn(act, dpre2_b, tka=1024, tn=1024, tt=1024, name="dw_down")
    dup = _d_up(dpre2_b, wts["w_down"], up)
    dw_up = _mm_tn(h1, dup, tka=1024, tn=1024, tt=1024, name="dw_up", out_shards=4)
    dpre1, acc1 = _d_h1_ln1_bwd(dup, wts["w_up"], dpre2, pre1, g1, b1)
    dw_out = _mm_tn(mixin, dpre1, tka=1024, tn=1024, tt=1024, name="dw_out")
    dy_ssd, dy_att, du, dbg = _d_mixin_mix_bwd(dpre1, wts["w_out"], y_ssd, y_att, u, bg_row)
    dw_proj_ssd = _mm_tn(s_out, dy_ssd, tka=1024, tn=1024, tt=1024, name="dw_proj_ssd")
    ds_out = _mm_nt(dy_ssd, wts["w_proj_ssd"], tm=512, tk=1024, tc=1024, name="d_s_out")
    dw_proj_attn = _mm_tn(att, dy_att, tka=256, tn=256, tt=1024, name="dw_proj_attn", out_shards=4)
    do_g, e_g = _d_att_combine_bwd(dy_att, wts["w_proj_attn"], att_o, att_l)
    for g in range(3):
        du = _attn_dq(u, du, do_g[g], att_l[g], e_g[g], g)
        du = _attn_dkv(u, du, do_g[g], att_l[g], e_g[g], g)
    big = {
        "w_proj_ssd": dw_proj_ssd.reshape(4, DI // 4, D),
        "w_proj_attn": dw_proj_attn,
        "w_out": dw_out.reshape(4, D // 4, D),
        "w_up": dw_up,
        "w_down": dw_down.reshape(4, DFF // 4, D),
    }
    early = [big[n] for n in EARLY]
    dy, du, dnw, dds, recv = _gatenorm_bwd(ds_out, y_fb, xbc, u, du, dsk_row, nw_row,
                                           side=_swap_side(early) if rs_idx else None)
    if rs_idx:
        halves = [_add_half(g, r, rs_idx[0], f"rs_add_half_{n}") for g, r, n in zip(early, recv, EARLY)]
    dxs_f, dbc_f, ddt_f, sacc_f, recv = _ssd_bwd(xbc, u, par, dy, st_f, rev=False,
                                                 side=_step1_side([h[1] for h in halves]) if rs_idx else None)
    if rs_idx:
        k = len(EARLY)
        sums1 = [_rs_add1(h[0], ra, rb, rs_idx[1], f"rs_add1_{n}")
                 for h, ra, rb, n in zip(halves, recv[:k], recv[k:], EARLY)]
    dxs_b, dbc_b, ddt_b, sacc_b, recv = _ssd_bwd(
        xbc, u, par, dy, st_b, rev=True,
        side=_step2_side([s1[2] for s1 in sums1], [s1[3] for s1 in sums1]) if rs_idx else None)
    pieces = None
    if rs_idx:
        pieces = {n: _rs_add2(s1[0], s1[1], ra, rb, rs_idx[1], f"rs_add2_{n}")
                  for s1, ra, rb, n in zip(sums1, recv[:k], recv[k:], EARLY)}
    dpre_c, dcw, dcb = _conv_dpre(u, dxs_f, dxs_b, dy, dbc_f, dbc_b, dsk_row, sm["conv_w"], cb_row)
    du = _conv_dx(du, dpre_c, sm["conv_w"])
    du = _dt_bwd(du, ddt_f, ddt_b)
    dw_in_p = _mm_tn(xb, du, tka=1024, tn=2432, tt=1024, name="dw_in")
    dx = _mm_nt(du, wts["w_in_p"], tm=512, tk=1024, tc=2432, name="d_x", add=dpre1, add_scale=ALPHA)

    sacc = sacc_f + sacc_b
    small = {
        "b_gate": dbg[0], "conv_w": dcw[0:KCONV], "conv_b": dcb[0],
        "dt_bias_f": sacc[0, 0:32], "dt_bias_b": sacc[0, 32:64], "a_log_f": sacc[1, 0:32], "a_log_b": sacc[1, 32:64],
        "d_skip": dds[0, 0:32], "ssd_norm_w": dnw[0],
        "ln1_g": acc1[0], "ln1_b": acc1[1], "ln2_g": acc2[0], "ln2_b": acc2[1], "loss": acc2[2, 0:1],
    }
    big["w_in"] = _shards_from_perm(dw_in_p)
    return dx, big, small, pieces


HBM_SPEC = pl.BlockSpec(memory_space=pl.ANY)


def _place():
    x, y, c = lax.axis_index("x"), lax.axis_index("y"), lax.axis_index("c")
    chips = [(1 - x, y), (x, 1 - y), (1 - x, 1 - y)]
    return x, y, c, chips


def _allgather_weights(shards):
    n = len(shards)

    def body(*refs):
        ins, outs = refs[:n], refs[n:2 * n]
        send_sems, recv_sems = refs[2 * n:]
        x, y, c, _ = _place()
        q, q_x, q_y, q_d = 2 * x + y, 2 * (1 - x) + y, 2 * x + 1 - y, 2 * (1 - x) + 1 - y
        x_nbr, y_nbr, sibling = (1 - x, y, c), (x, 1 - y, c), (x, y, 1 - c)

        def copy(w, k, src, dst, to):
            return pltpu.make_async_remote_copy(src_ref=src, dst_ref=dst, send_sem=send_sems.at[w, k],
                                                recv_sem=recv_sems.at[w, k], device_id=to, device_id_type=MESH)

        def rows(w, core, part):
            rh = ins[w].shape[0] // 2
            if part is None:
                return pl.ds(core * rh, rh)
            return pl.ds(core * rh + part * (rh // 2), rh // 2)

        def same(w, k, slot, core, part, to):
            blk = outs[w].at[slot, rows(w, core, part), :]
            return copy(w, k, blk, blk, to)

        started = []
        for w in range(n):
            cp = copy(w, 8, ins[w], outs[w].at[q], sibling)
            cp.start()
            started.append(cp)
            mine = rows(w, c, None)
            for k, to in ((0, x_nbr), (1, y_nbr)):
                cp = copy(w, k, ins[w].at[mine, :], outs[w].at[q, mine, :], to)
                cp.start()
                started.append(cp)
        for w in range(n):
            same(w, 0, q_x, c, None, x_nbr).wait_recv()
            for cp in (same(w, 2, q_x, c, 0, y_nbr), same(w, 4, q_x, c, None, sibling)):
                cp.start()
                started.append(cp)
            same(w, 1, q_y, c, None, y_nbr).wait_recv()
            for cp in (same(w, 3, q_y, c, 1, x_nbr), same(w, 5, q_y, c, None, sibling)):
                cp.start()
                started.append(cp)
        for w in range(n):
            same(w, 2, q_d, c, 0, y_nbr).wait_recv()
            cp = same(w, 6, q_d, c, 0, sibling)
            cp.start()
            started.append(cp)
            same(w, 3, q_d, c, 1, x_nbr).wait_recv()
            cp = same(w, 7, q_d, c, 1, sibling)
            cp.start()
            started.append(cp)
        for w in range(n):
            same(w, 4, q_x, 1 - c, None, sibling).wait_recv()
            same(w, 5, q_y, 1 - c, None, sibling).wait_recv()
            same(w, 6, q_d, 1 - c, 0, sibling).wait_recv()
            same(w, 7, q_d, 1 - c, 1, sibling).wait_recv()
            copy(w, 8, ins[w], outs[w].at[q], sibling).wait_recv()
        for cp in started:
            cp.wait_send()

    return pl.pallas_call(
        body, out_shape=[jax.ShapeDtypeStruct((4,) + s.shape, s.dtype) for s in shards],
        in_specs=[HBM_SPEC] * n, out_specs=[HBM_SPEC] * n,
        scratch_shapes=[pltpu.SemaphoreType.DMA((n, 9)), pltpu.SemaphoreType.DMA((n, 9))],
        name="allgather_weights")(*shards)


def _swap_halves(grads):
    n = len(grads)

    def body(*refs):
        ins, outs = refs[:n], refs[n:2 * n]
        send_sems, recv_sems = refs[2 * n:]
        x, y, c, _ = _place()
        copies = []
        for w in range(n):
            rh = ins[w].shape[1] // 2
            for p in range(4):
                cp = pltpu.make_async_remote_copy(
                    src_ref=ins[w].at[p, pl.ds((1 - c) * rh, rh), :], dst_ref=outs[w].at[p],
                    send_sem=send_sems.at[w, p], recv_sem=recv_sems.at[w, p],
                    device_id=(x, y, 1 - c), device_id_type=MESH)
                cp.start()
                copies.append(cp)
        for cp in copies:
            cp.wait()

    return pl.pallas_call(
        body, out_shape=[jax.ShapeDtypeStruct((4, g.shape[1] // 2, g.shape[2]), F32) for g in grads],
        in_specs=[HBM_SPEC] * n, out_specs=[HBM_SPEC] * n,
        scratch_shapes=[pltpu.SemaphoreType.DMA((n, 4)), pltpu.SemaphoreType.DMA((n, 4))],
        name="rs_swap_halves")(*grads)


def _rs_step1(parts):
    n = len(parts)

    def body(*refs):
        ins, out_a, out_b = refs[:n], refs[n:2 * n], refs[2 * n:3 * n]
        send_sems, recv_sems = refs[3 * n:]
        x, y, c, _ = _place()
        copies = []
        for w in range(n):
            rq = ins[w].shape[1] // 2
            for i in range(2):
                copies.append(pltpu.make_async_remote_copy(
                    src_ref=ins[w].at[2 * (1 - x) + i, pl.ds(0, rq), :], dst_ref=out_a[w].at[i],
                    send_sem=send_sems.at[w, i], recv_sem=recv_sems.at[w, i],
                    device_id=(1 - x, y, c), device_id_type=MESH))
                copies.append(pltpu.make_async_remote_copy(
                    src_ref=ins[w].at[2 * i + 1 - y, pl.ds(rq, rq), :], dst_ref=out_b[w].at[i],
                    send_sem=send_sems.at[w, 2 + i], recv_sem=recv_sems.at[w, 2 + i],
                    device_id=(x, 1 - y, c), device_id_type=MESH))
        for cp in copies:
            cp.start()
        for cp in copies:
            cp.wait()

    quarter = lambda p: jax.ShapeDtypeStruct((2, p.shape[1] // 2, p.shape[2]), p.dtype)
    outs = pl.pallas_call(
        body, out_shape=[quarter(p) for p in parts] * 2,
        in_specs=[HBM_SPEC] * n, out_specs=[HBM_SPEC] * (2 * n),
        scratch_shapes=[pltpu.SemaphoreType.DMA((n, 4)), pltpu.SemaphoreType.DMA((n, 4))],
        name="rs_step1")(*parts)
    return outs[:n], outs[n:]


def _rs_step2(tas, tbs):
    n = len(tas)

    def body(*refs):
        in_a, in_b, out_a, out_b = refs[:n], refs[n:2 * n], refs[2 * n:3 * n], refs[3 * n:4 * n]
        send_sems, recv_sems = refs[4 * n:]
        x, y, c, _ = _place()
        copies = []
        for w in range(n):
            copies.append(pltpu.make_async_remote_copy(
                src_ref=in_a[w].at[1 - y], dst_ref=out_a[w], send_sem=send_sems.at[w, 0], recv_sem=recv_sems.at[w, 0],
                device_id=(x, 1 - y, c), device_id_type=MESH))
            copies.append(pltpu.make_async_remote_copy(
                src_ref=in_b[w].at[1 - x], dst_ref=out_b[w], send_sem=send_sems.at[w, 1], recv_sem=recv_sems.at[w, 1],
                device_id=(1 - x, y, c), device_id_type=MESH))
        for cp in copies:
            cp.start()
        for cp in copies:
            cp.wait()

    one = lambda p: jax.ShapeDtypeStruct(p.shape[1:], p.dtype)
    outs = pl.pallas_call(
        body, out_shape=[one(p) for p in tas] + [one(p) for p in tbs],
        in_specs=[HBM_SPEC] * (2 * n), out_specs=[HBM_SPEC] * (2 * n),
        scratch_shapes=[pltpu.SemaphoreType.DMA((n, 2)), pltpu.SemaphoreType.DMA((n, 2))],
        name="rs_step2")(*tas, *tbs)
    return outs[:n], outs[n:]


class _Side(NamedTuple):
    ins: tuple
    out_shapes: tuple
    nsem: tuple
    make: Callable


def _swap_copies(ins, outs, send_sems, recv_sems):
    x, y, c, _ = _place()
    copies = []
    for w in range(len(ins)):
        rh = ins[w].shape[1] // 2
        for p in range(4):
            copies.append(pltpu.make_async_remote_copy(
                src_ref=ins[w].at[p, pl.ds((1 - c) * rh, rh), :], dst_ref=outs[w].at[p],
                send_sem=send_sems.at[w, p], recv_sem=recv_sems.at[w, p],
                device_id=(x, y, 1 - c), device_id_type=MESH))
    return copies


def _swap_side(grads):
    shapes = tuple(jax.ShapeDtypeStruct((4, g.shape[1] // 2, g.shape[2]), F32) for g in grads)
    return _Side(tuple(grads), shapes, (len(grads), 4), _swap_copies)


def _step1_copies(ins, outs, send_sems, recv_sems):
    n = len(ins)
    out_a, out_b = outs[:n], outs[n:]
    x, y, c, _ = _place()
    copies = []
    for w in range(n):
        rq = ins[w].shape[1] // 2
        for i in range(2):
            copies.append(pltpu.make_async_remote_copy(
                src_ref=ins[w].at[2 * (1 - x) + i, pl.ds(0, rq), :], dst_ref=out_a[w].at[i],
                send_sem=send_sems.at[w, i], recv_sem=recv_sems.at[w, i],
                device_id=(1 - x, y, c), device_id_type=MESH))
            copies.append(pltpu.make_async_remote_copy(
                src_ref=ins[w].at[2 * i + 1 - y, pl.ds(rq, rq), :], dst_ref=out_b[w].at[i],
                send_sem=send_sems.at[w, 2 + i], recv_sem=recv_sems.at[w, 2 + i],
                device_id=(x, 1 - y, c), device_id_type=MESH))
    return copies


def _step1_side(parts):
    quarter = tuple(jax.ShapeDtypeStruct((2, p.shape[1] // 2, p.shape[2]), p.dtype) for p in parts)
    return _Side(tuple(parts), quarter + quarter, (len(parts), 4), _step1_copies)


def _step2_copies(ins, outs, send_sems, recv_sems):
    n = len(ins) // 2
    in_a, in_b, out_a, out_b = ins[:n], ins[n:], outs[:n], outs[n:]
    x, y, c, _ = _place()
    copies = []
    for w in range(n):
        copies.append(pltpu.make_async_remote_copy(
            src_ref=in_a[w].at[1 - y], dst_ref=out_a[w], send_sem=send_sems.at[w, 0], recv_sem=recv_sems.at[w, 0],
            device_id=(x, 1 - y, c), device_id_type=MESH))
        copies.append(pltpu.make_async_remote_copy(
            src_ref=in_b[w].at[1 - x], dst_ref=out_b[w], send_sem=send_sems.at[w, 1], recv_sem=recv_sems.at[w, 1],
            device_id=(1 - x, y, c), device_id_type=MESH))
    return copies


def _step2_side(tas, tbs):
    one = tuple(jax.ShapeDtypeStruct(p.shape[1:], p.dtype) for p in tuple(tas) + tuple(tbs))
    return _Side(tuple(tas) + tuple(tbs), one, (len(tas), 2), _step2_copies)


def _run_side(side, name):
    n_in, n_out = len(side.ins), len(side.out_shapes)

    def body(*refs):
        copies = side.make(refs[:n_in], refs[n_in:n_in + n_out], *refs[n_in + n_out:])
        for cp in copies:
            cp.start()
        for cp in copies:
            cp.wait()

    return pl.pallas_call(
        body, out_shape=list(side.out_shapes), in_specs=[HBM_SPEC] * n_in, out_specs=[HBM_SPEC] * n_out,
        scratch_shapes=[pltpu.SemaphoreType.DMA(side.nsem), pltpu.SemaphoreType.DMA(side.nsem)], name=name)(*side.ins)


def _host_call(body, side, n_steps, *, out_shape, in_specs, out_specs, scratch_shapes, args, aliases, name, sem):
    n_in, n_out, n_scr = len(in_specs), len(out_shape), len(scratch_shapes)
    if side is None:
        outs = pl.pallas_call(body, out_shape=tuple(out_shape), grid=(n_steps,), in_specs=list(in_specs),
                              out_specs=tuple(out_specs), scratch_shapes=list(scratch_shapes),
                              input_output_aliases=aliases, name=name, compiler_params=_params(sem))(*args)
        return tuple(outs), ()
    ns_in, ns_out = len(side.ins), len(side.out_shapes)

    def wrapped(*refs):
        h_in, s_in = refs[:n_in], refs[n_in:n_in + ns_in]
        o0 = n_in + ns_in
        h_out, s_out = refs[o0:o0 + n_out], refs[o0 + n_out:o0 + n_out + ns_out]
        c0 = o0 + n_out + ns_out
        h_scr, sems = refs[c0:c0 + n_scr], refs[c0 + n_scr:]
        step = pl.program_id(0)

        @pl.when(step == 0)
        def _():
            for cp in side.make(s_in, s_out, *sems):
                cp.start()

        body(*h_in, *h_out, *h_scr)

        @pl.when(step == n_steps - 1)
        def _():
            for cp in side.make(s_in, s_out, *sems):
                cp.wait()

    outs = pl.pallas_call(
        wrapped, out_shape=tuple(out_shape) + tuple(side.out_shapes), grid=(n_steps,),
        in_specs=list(in_specs) + [HBM_SPEC] * ns_in, out_specs=tuple(out_specs) + (HBM_SPEC,) * ns_out,
        scratch_shapes=list(scratch_shapes) + [pltpu.SemaphoreType.DMA(side.nsem), pltpu.SemaphoreType.DMA(side.nsem)],
        input_output_aliases=aliases, name=name, compiler_params=_params(sem))(*args, *side.ins)
    return tuple(outs[:n_out]), tuple(outs[n_out:])


def _join_halves(pieces):
    n = len(pieces)

    def body(*refs):
        outs = refs[n:2 * n]
        send_sems, recv_sems = refs[2 * n:]
        x, y, c, _ = _place()

        def copy(w, slot):
            return pltpu.make_async_remote_copy(
                src_ref=outs[w].at[slot], dst_ref=outs[w].at[slot], send_sem=send_sems.at[w], recv_sem=recv_sems.at[w],
                device_id=(x, y, 1 - c), device_id_type=MESH)

        for w in range(n):
            copy(w, c).start()
        for w in range(n):
            copy(w, 1 - c).wait_recv()
            copy(w, c).wait_send()

    return pl.pallas_call(
        body, out_shape=[jax.ShapeDtypeStruct(p.shape, F32) for p in pieces],
        in_specs=[HBM_SPEC] * n, out_specs=[HBM_SPEC] * n, input_output_aliases={w: w for w in range(n)},
        scratch_shapes=[pltpu.SemaphoreType.DMA((n,)), pltpu.SemaphoreType.DMA((n,))],
        name="rs_join_halves")(*pieces)


def _add_tile_rows(rh, c):
    for cand in (512, 256, 128, 64, 32, 16, 8):
        if rh % cand == 0 and cand * c * 4 <= 2 ** 21:
            return cand
    return rh


def _add_half(grad, recv, c_idx, name):
    _, r, cc = grad.shape
    rh = r // 2
    tr = _add_tile_rows(rh, cc)
    nb = rh // tr

    def body(c_ref, g_ref, r_ref, o_ref, ob_ref):
        del c_ref
        s = g_ref[...] + r_ref[...]
        o_ref[...] = s
        ob_ref[...] = s.astype(BF16)

    blk = pl.BlockSpec((None, tr, cc), lambda p, i, c_ref: (p, i, 0))
    grid_spec = pltpu.PrefetchScalarGridSpec(
        num_scalar_prefetch=1, grid=(4, nb),
        in_specs=[pl.BlockSpec((None, tr, cc), lambda p, i, c_ref: (p, c_ref[0] * nb + i, 0)), blk],
        out_specs=(blk, blk))
    return pl.pallas_call(
        body, out_shape=(jax.ShapeDtypeStruct((4, rh, cc), F32), jax.ShapeDtypeStruct((4, rh, cc), BF16)),
        grid_spec=grid_spec, name=name, compiler_params=_params(("parallel", "parallel")))(c_idx, grad, recv)


def _rs_add1(part, recv_a, recv_b, xy_idx, name):
    _, rh, cc = part.shape
    rq = rh // 2
    tr = _add_tile_rows(rq, cc)
    nb = rq // tr

    def body(xy_ref, pa_ref, pb_ref, ra_ref, rb_ref, ta_ref, tb_ref, tab_ref, tbb_ref):
        del xy_ref
        ta = pa_ref[...] + ra_ref[...].astype(F32)
        tb = pb_ref[...] + rb_ref[...].astype(F32)
        ta_ref[...] = ta
        tb_ref[...] = tb
        tab_ref[...] = ta.astype(BF16)
        tbb_ref[...] = tb.astype(BF16)

    blk = pl.BlockSpec((None, tr, cc), lambda i, j, xy: (i, j, 0))
    grid_spec = pltpu.PrefetchScalarGridSpec(
        num_scalar_prefetch=1, grid=(2, nb),
        in_specs=[pl.BlockSpec((None, tr, cc), lambda i, j, xy: (2 * xy[0] + i, j, 0)),
                  pl.BlockSpec((None, tr, cc), lambda i, j, xy: (2 * i + xy[1], nb + j, 0)), blk, blk],
        out_specs=(blk, blk, blk, blk))
    f32s, b16s = jax.ShapeDtypeStruct((2, rq, cc), F32), jax.ShapeDtypeStruct((2, rq, cc), BF16)
    return pl.pallas_call(body, out_shape=(f32s, f32s, b16s, b16s), grid_spec=grid_spec, name=name,
                          compiler_params=_params(("parallel", "parallel")))(xy_idx, part, part, recv_a, recv_b)


def _rs_add2(ta, tb, recv_a, recv_b, xy_idx, name):
    _, rq, cc = ta.shape
    tr = _add_tile_rows(rq, cc)
    nb = rq // tr

    def body(xy_ref, ta_ref, tb_ref, ra_ref, rb_ref, o_ref):
        del xy_ref
        s = pl.program_id(0)
        fa = ta_ref[...] + ra_ref[...].astype(F32)
        fb = tb_ref[...] + rb_ref[...].astype(F32)
        o_ref[...] = jnp.where(s == 0, fa, fb)

    rblk = pl.BlockSpec((tr, cc), lambda s, j, xy: (j, 0))
    grid_spec = pltpu.PrefetchScalarGridSpec(
        num_scalar_prefetch=1, grid=(2, nb),
        in_specs=[pl.BlockSpec((None, tr, cc), lambda s, j, xy: (xy[1], j, 0)),
                  pl.BlockSpec((None, tr, cc), lambda s, j, xy: (xy[0], j, 0)), rblk, rblk],
        out_specs=pl.BlockSpec((None, tr, cc), lambda s, j, xy: (xy[2], s * nb + j, 0)))
    return pl.pallas_call(body, out_shape=jax.ShapeDtypeStruct((2, 2 * rq, cc), F32), grid_spec=grid_spec, name=name,
                          compiler_params=_params(("parallel", "parallel")))(xy_idx, ta, tb, recv_a, recv_b)


def _allreduce_small(slab):
    r = slab.shape[0]

    def body(x_ref, o_ref, buf, send_sems, recv_sems):
        x, y, c, _ = _place()
        me = 4 * x + 2 * y + c
        buf[me] = x_ref[...]
        peers = []
        for k in range(1, 8):
            kx, ky, kc = (k >> 2) & 1, (k >> 1) & 1, k & 1
            peers.append((x + kx - 2 * x * kx, y + ky - 2 * y * ky, c + kc - 2 * c * kc))

        def copy(k, slot):
            return pltpu.make_async_remote_copy(src_ref=x_ref, dst_ref=buf.at[slot], send_sem=send_sems.at[k],
                                                recv_sem=recv_sems.at[k], device_id=peers[k], device_id_type=MESH)

        for k in range(7):
            copy(k, me).start()
        for k, (px, py, pc) in enumerate(peers):
            copy(k, 4 * px + 2 * py + pc).wait_recv()
        for k in range(7):
            copy(k, me).wait_send()
        acc = buf[0]
        for j in range(1, 8):
            acc = acc + buf[j]
        o_ref[...] = acc

    vm = pl.BlockSpec(memory_space=pltpu.VMEM)
    return pl.pallas_call(
        body, out_shape=jax.ShapeDtypeStruct((r, 128), F32), in_specs=[vm], out_specs=vm,
        scratch_shapes=[pltpu.VMEM((8, r, 128), F32), pltpu.SemaphoreType.DMA((7,)), pltpu.SemaphoreType.DMA((7,))],
        name="allreduce_small")(slab)


def _pack(arrs):
    rows = []
    for a in arrs:
        v = a.reshape(-1)
        v = jnp.pad(v, (0, (-v.shape[0]) % 128))
        rows.append(v.reshape(-1, 128))
    slab = jnp.concatenate(rows, axis=0)
    return jnp.pad(slab, ((0, (-slab.shape[0]) % 8), (0, 0)))


def _unpack(slab, shapes):
    out, r0 = [], 0
    for shp in shapes:
        size = math.prod(shp)
        nr = -(-size // 128)
        out.append(slab[r0:r0 + nr].reshape(-1)[:size].reshape(shp))
        r0 += nr
    return out


BIG = ("w_in", "w_proj_ssd", "w_proj_attn", "w_out", "w_up", "w_down")
SMALL = ("b_gate", "conv_w", "conv_b", "dt_bias_f", "dt_bias_b", "a_log_f", "a_log_b", "d_skip", "ssd_norm_w",
         "ln1_g", "ln1_b", "ln2_g", "ln2_b")
ORDER = ("w_in", "b_gate", "conv_w", "conv_b", "dt_bias_f", "dt_bias_b", "a_log_f", "a_log_b", "d_skip", "ssd_norm_w",
         "w_proj_ssd", "w_proj_attn", "w_out", "ln1_g", "ln1_b", "w_up", "w_down", "ln2_g", "ln2_b")


def kernel(x, w_in, b_gate, conv_w, conv_b, dt_bias_f, dt_bias_b, a_log_f, a_log_b, d_skip, ssd_norm_w, w_proj_ssd, w_proj_attn, w_out, ln1_g, ln1_b, w_up, w_down, ln2_g, ln2_b, loss_target, m_w_in, m_b_gate, m_conv_w, m_conv_b, m_dt_bias_f, m_dt_bias_b, m_a_log_f, m_a_log_b, m_d_skip, m_ssd_norm_w, m_w_proj_ssd, m_w_proj_attn, m_w_out, m_ln1_g, m_ln1_b, m_w_up, m_w_down, m_ln2_g, m_ln2_b, v_w_in, v_b_gate, v_conv_w, v_conv_b, v_dt_bias_f, v_dt_bias_b, v_a_log_f, v_a_log_b, v_d_skip, v_ssd_norm_w, v_w_proj_ssd, v_w_proj_attn, v_w_out, v_ln1_g, v_ln1_b, v_w_up, v_w_down, v_ln2_g, v_ln2_b):
    w = dict(w_in=w_in, b_gate=b_gate, conv_w=conv_w, conv_b=conv_b, dt_bias_f=dt_bias_f, dt_bias_b=dt_bias_b,
             a_log_f=a_log_f, a_log_b=a_log_b, d_skip=d_skip, ssd_norm_w=ssd_norm_w, w_proj_ssd=w_proj_ssd,
             w_proj_attn=w_proj_attn, w_out=w_out, ln1_g=ln1_g, ln1_b=ln1_b, w_up=w_up, w_down=w_down, ln2_g=ln2_g, ln2_b=ln2_b)
    m = dict(w_in=m_w_in, b_gate=m_b_gate, conv_w=m_conv_w, conv_b=m_conv_b, dt_bias_f=m_dt_bias_f, dt_bias_b=m_dt_bias_b,
             a_log_f=m_a_log_f, a_log_b=m_a_log_b, d_skip=m_d_skip, ssd_norm_w=m_ssd_norm_w, w_proj_ssd=m_w_proj_ssd,
             w_proj_attn=m_w_proj_attn, w_out=m_w_out, ln1_g=m_ln1_g, ln1_b=m_ln1_b, w_up=m_w_up, w_down=m_w_down,
             ln2_g=m_ln2_g, ln2_b=m_ln2_b)
    v = dict(w_in=v_w_in, b_gate=v_b_gate, conv_w=v_conv_w, conv_b=v_conv_b, dt_bias_f=v_dt_bias_f, dt_bias_b=v_dt_bias_b,
             a_log_f=v_a_log_f, a_log_b=v_a_log_b, d_skip=v_d_skip, ssd_norm_w=v_ssd_norm_w, w_proj_ssd=v_w_proj_ssd,
             w_proj_attn=v_w_proj_attn, w_out=v_w_out, ln1_g=v_ln1_g, ln1_b=v_ln1_b, w_up=v_w_up, w_down=v_w_down,
             ln2_g=v_ln2_g, ln2_b=v_ln2_b)
    xi, yi, ci = lax.axis_index("x"), lax.axis_index("y"), lax.axis_index("c")
    shard = 2 * xi + yi

    g_in, g_ps, g_pa, g_o, g_up, g_dn = _allgather_weights([w[n].astype(BF16) for n in BIG])
    wts = {"w_in_p": _perm_from_shards(g_in),"w_proj_ssd": g_ps.reshape(DI, D), "w_proj_attn": g_pa,
           "w_out": g_o.reshape(D, D), "w_up": g_up, "w_down": g_dn.reshape(DFF, D)}

    cw_slab = jnp.zeros((KCONV, 4, CONVD // 4), F32)
    cw_slab = lax.dynamic_update_slice(cw_slab, conv_w[:, None, :] * 0.5, (0, shard, 0))
    conv_w_all = _unpack(_allreduce_small(_pack([cw_slab])), [(KCONV, CONVD)])[0]

    sm = {n: w[n] for n in SMALL}
    sm["conv_w"] = conv_w_all
    c_idx = jnp.reshape(ci, (1,)).astype(jnp.int32)
    xy_idx = jnp.stack([xi, yi, ci]).astype(jnp.int32)
    dx, big, small, pieces = _local_grads(x[0], loss_target[0], wts, sm, rs_idx=(c_idx, xy_idx))

    names = list(SMALL) + ["loss"]
    shapes = [small[n].shape for n in names]
    red = dict(zip(names, _unpack(_allreduce_small(_pack([small[n] for n in names])), shapes)))
    loss = red["loss"].reshape(())
    gsm = {n: red[n] for n in SMALL}
    conv_w_grad_shard = lax.dynamic_slice_in_dim(gsm["conv_w"].reshape(KCONV, 4, CONVD // 4), shard, 1, axis=1)
    gsm["conv_w"] = conv_w_grad_shard.reshape(KCONV, CONVD // 4)

    g = big["w_in"]
    half = _add_half(g, _run_side(_swap_side([g]), "rs_swap_halves")[0], c_idx, "rs_add_half_w_in")
    ra, rb = _run_side(_step1_side([half[1]]), "rs_step1")
    s1 = _rs_add1(half[0], ra, rb, xy_idx, "rs_add1_w_in")
    ra2, rb2 = _run_side(_step2_side([s1[2]], [s1[3]]), "rs_step2")
    pieces["w_in"] = _rs_add2(s1[0], s1[1], ra2, rb2, xy_idx, "rs_add2_w_in")
    joined = _join_halves([pieces[n] for n in BIG])
    gbig = {n: j.reshape(w[n].shape) for n, j in zip(BIG, joined)}

    grads, deltas, new_m, new_v = {}, {}, {}, {}
    for n in BIG:
        grads[n] = gbig[n]
        deltas[n], new_m[n], new_v[n] = _adamw(w[n], gbig[n], m[n], v[n], f"adamw_{n}")
    sshapes = [w[n].shape for n in SMALL]
    d_s, m_s, v_s = _adamw(_pack([w[n] for n in SMALL]), _pack([gsm[n] for n in SMALL]),
                           _pack([m[n] for n in SMALL]), _pack([v[n] for n in SMALL]), "adamw_small")
    for n, dd, mm, vv in zip(SMALL, _unpack(d_s, sshapes), _unpack(m_s, sshapes), _unpack(v_s, sshapes)):
        grads[n], deltas[n], new_m[n], new_v[n] = gsm[n], dd, mm, vv

    return (loss, dx[None], *[grads[n] for n in ORDER], *[deltas[n] for n in ORDER],
            *[new_m[n] for n in ORDER], *[new_v[n] for n in ORDER])
```

```python
import math
from typing import Callable, NamedTuple

import jax
import numpy as np
import jax.numpy as jnp
from jax import lax
from jax.experimental import pallas as pl
from jax.experimental.pallas import tpu as pltpu

F32, BF16 = jnp.float32, jnp.bfloat16
MESH = pl.DeviceIdType.MESH

D = 1024
DI = 2048
NH = 32
HP = 64
NG = 4
NS = 128
Q = 128
CONVD = 3072
KCONV = 5
DFF = 4096
AH = 64
ATT_HALF = 64
DILATIONS = (1, 4, 16)
IN_COLS = 9536
OZ, OGATE, OXBC, OKV, OQ, ODT, UW = 0, 2048, 4096, 7168, 8704, 9472, 9728
ALPHA = 2.0 ** 0.25
NORM_EPS = 1e-5
ADAM_LR, ADAM_B1, ADAM_B2, ADAM_EPS, ADAM_WD, ADAM_STEP = 0.001, 0.9, 0.999, 1e-8, 0.01, 10
VMEM_LIMIT = 56 * 2 ** 20
NEG = -1e30


def _params(sem):
    return pltpu.CompilerParams(dimension_semantics=sem, vmem_limit_bytes=VMEM_LIMIT)


def _sigmoid(x):
    return 1.0 / (1.0 + jnp.exp(-x))


def _softplus(x):
    e = jnp.exp(-jnp.abs(x))
    small = e * (1.0 - e * (0.5 - e * (1.0 / 3.0)))
    return jnp.maximum(x, 0.0) + jnp.where(e < 0.01, small, jnp.log(1.0 + e))


def _split3(a):
    hi = a.astype(BF16)
    r = a - hi.astype(F32)
    mid = r.astype(BF16)
    lo = (r - mid.astype(F32)).astype(BF16)
    return hi, mid, lo


def _dot01(a, m01):
    hi, mid, lo = _split3(a)
    d = lambda p: jnp.dot(p, m01, preferred_element_type=F32)
    return d(hi) + d(mid) + d(lo)


def _dot01_l(m01, a):
    hi, mid, lo = _split3(a)
    d = lambda p: jnp.dot(m01, p, preferred_element_type=F32)
    return d(hi) + d(mid) + d(lo)


def _dot_nt(a, b):
    return lax.dot_general(a, b, (((1,), (1,)), ((), ())), preferred_element_type=F32)


def _iota(shape, dim):
    return lax.broadcasted_iota(jnp.int32, shape, dim)


def _mm_nn(a, b, *, tm, tn, name, out_dtype=F32):
    m, k = a.shape
    if b.ndim == 3:
        assert tn == b.shape[2]
        n = b.shape[0] * b.shape[2]
        b_spec = pl.BlockSpec((None, k, tn), lambda j, i: (j, 0, 0))
    else:
        n = b.shape[1]
        b_spec = pl.BlockSpec((k, tn), lambda j, i: (0, j))

    def body(a_ref, b_ref, o_ref):
        o_ref[...] = jnp.dot(a_ref[...].astype(BF16), b_ref[...], preferred_element_type=F32).astype(out_dtype)

    return pl.pallas_call(
        body, out_shape=jax.ShapeDtypeStruct((m, n), out_dtype), grid=(n // tn, m // tm),
        in_specs=[pl.BlockSpec((tm, k), lambda j, i: (i, 0)), b_spec],
        out_specs=pl.BlockSpec((tm, tn), lambda j, i: (i, j)),
        name=name, compiler_params=_params(("parallel", "parallel")))(a, b)


def _mm_nt(a, b, *, tm, tk, tc, name, add=None, add_scale=1.0):
    m, n = a.shape
    if b.ndim == 3:
        assert tc == b.shape[2]
        k, nc = b.shape[1], b.shape[0]
        b_spec = pl.BlockSpec((None, tk, tc), lambda j, i, c: (c, j, 0))
    else:
        k, nc = b.shape[0], n // tc
        b_spec = pl.BlockSpec((tk, tc), lambda j, i, c: (j, c))

    def body(*refs):
        if add is None:
            a_ref, b_ref, o_ref = refs
        else:
            a_ref, b_ref, add_ref, o_ref = refs
        c = pl.program_id(2)
        part = _dot_nt(a_ref[...].astype(BF16), b_ref[...])

        @pl.when(c == 0)
        def _():
            if add is None:
                o_ref[...] = part
            else:
                o_ref[...] = part + add_scale * add_ref[...]

        @pl.when(c > 0)
        def _():
            o_ref[...] += part

    in_specs = [pl.BlockSpec((tm, tc), lambda j, i, c: (i, c)), b_spec]
    args = [a, b]
    if add is not None:
        in_specs.append(pl.BlockSpec((tm, tk), lambda j, i, c: (i, j)))
        args.append(add)
    return pl.pallas_call(
        body, out_shape=jax.ShapeDtypeStruct((m, k), F32), grid=(k // tk, m // tm, nc),
        in_specs=in_specs, out_specs=pl.BlockSpec((tm, tk), lambda j, i, c: (i, j)),
        name=name, compiler_params=_params(("parallel", "parallel", "arbitrary")))(*args)


def _mm_tn(a, b, *, tka, tn, tt, name, out_shards=None):
    t, ka = a.shape
    n = b.shape[1]
    if out_shards:
        assert tn == n // out_shards
        out_shape = jax.ShapeDtypeStruct((out_shards, ka, tn), F32)
        o_spec = pl.BlockSpec((None, tka, tn), lambda i, j, s: (j, i, 0))
    else:
        out_shape = jax.ShapeDtypeStruct((ka, n), F32)
        o_spec = pl.BlockSpec((tka, tn), lambda i, j, s: (i, j))

    def body(a_ref, b_ref, o_ref):
        s = pl.program_id(2)
        part = lax.dot_general(a_ref[...].astype(BF16), b_ref[...].astype(BF16), (((0,), (0,)), ((), ())),
                               preferred_element_type=F32)

        @pl.when(s == 0)
        def _():
            o_ref[...] = part

        @pl.when(s > 0)
        def _():
            o_ref[...] += part

    return pl.pallas_call(
        body, out_shape=out_shape, grid=(ka // tka, n // tn, t // tt),
        in_specs=[pl.BlockSpec((tt, tka), lambda i, j, s: (s, i)), pl.BlockSpec((tt, tn), lambda i, j, s: (s, j))],
        out_specs=o_spec, name=name, compiler_params=_params(("parallel", "parallel", "arbitrary")))(a, b)


CONV_TM = 512
CONV_TC = 1024
CONV_RC = 64
CONV_CC = 256


def _halo_specs(t, tm, tc, col0):
    nb8 = t // 8
    r8 = tm // 8
    return [
        pl.BlockSpec((8, tc), lambda i, j: (jnp.maximum(i * r8 - 1, 0), col0 + j)),
        pl.BlockSpec((tm, tc), lambda i, j: (i, col0 + j)),
        pl.BlockSpec((8, tc), lambda i, j: (jnp.minimum((i + 1) * r8, nb8 - 1), col0 + j)),
    ]


def _fill_ext(ext, prev_ref, cur_ref, next_ref, tm, i, last):
    ext[0:8, :] = jnp.where(i > 0, prev_ref[...], 0.0)
    ext[8:8 + tm, :] = cur_ref[...]
    ext[8 + tm:16 + tm, :] = jnp.where(i < last, next_ref[...], 0.0)


def _conv_fwd(u, conv_w, conv_b):
    t = u.shape[0]
    tm, tc = CONV_TM, CONV_TC

    def body(prev_ref, cur_ref, next_ref, w_ref, b_ref, o_ref, ext):
        _fill_ext(ext, prev_ref, cur_ref, next_ref, tm, pl.program_id(0), t // tm - 1)
        for c0 in range(0, tc, CONV_CC):
            cs = slice(c0, c0 + CONV_CC)
            w = w_ref[:, cs]
            for r0 in range(0, tm, CONV_RC):
                acc = jnp.broadcast_to(b_ref[:, cs], (CONV_RC, CONV_CC))
                for k in range(KCONV):
                    acc = acc + w[k:k + 1, :] * ext[pl.ds(r0 + 6 + k, CONV_RC), cs]
                o_ref[r0:r0 + CONV_RC, cs] = acc * _sigmoid(acc)

    return pl.pallas_call(
        body, out_shape=jax.ShapeDtypeStruct((t, CONVD), F32), grid=(t // tm, CONVD // tc),
        in_specs=_halo_specs(t, tm, tc, OXBC // tc) + [
            pl.BlockSpec((KCONV, tc), lambda i, j: (0, j)), pl.BlockSpec((1, tc), lambda i, j: (0, j))],
        out_specs=pl.BlockSpec((tm, tc), lambda i, j: (i, j)),
        scratch_shapes=[pltpu.VMEM((tm + 16, tc), F32)],
        name="conv_fwd", compiler_params=_params(("parallel", "parallel")))(u, u, u, conv_w, conv_b)


def _conv_dpre(u, dxs, dy, dbc, dsk_row, conv_w, conv_b):
    t = u.shape[0]
    tm, tc = CONV_TM, CONV_TC
    r8 = tm // 8
    nb8 = t // 8
    c0 = OXBC // tc

    def body(uprev, ucur, unext, f_ref, y_ref, cf_ref, dsk_ref, w_ref, bias_ref, dpre_ref, dw_ref, db_ref, ext):
        j = pl.program_id(0)
        i = pl.program_id(1)
        _fill_ext(ext, uprev, ucur, unext, tm, i, t // tm - 1)
        is_xs = j < 2
        dw_cols, db_cols = [], []
        for c0 in range(0, tc, CONV_CC):
            cs = slice(c0, c0 + CONV_CC)
            w = w_ref[:, cs]
            dsk = dsk_ref[:, cs]
            dw_acc = [jnp.zeros((1, CONV_CC), F32) for _ in range(KCONV)]
            db_acc = jnp.zeros((1, CONV_CC), F32)
            for r0 in range(0, tm, CONV_RC):
                rs = slice(r0, r0 + CONV_RC)
                taps = [ext[pl.ds(r0 + 6 + k, CONV_RC), cs] for k in range(KCONV)]
                pre = jnp.broadcast_to(bias_ref[:, cs], (CONV_RC, CONV_CC))
                for k in range(KCONV):
                    pre = pre + w[k:k + 1, :] * taps[k]
                s = _sigmoid(pre)
                up = jnp.where(is_xs, f_ref[rs, cs] + dsk * y_ref[rs, cs], cf_ref[rs, cs])
                dpre = up * (s * (1.0 + pre * (1.0 - s)))
                dpre_ref[rs, cs] = dpre
                for k in range(KCONV):
                    dw_acc[k] = dw_acc[k] + jnp.sum(dpre * taps[k], axis=0, keepdims=True)
                db_acc = db_acc + jnp.sum(dpre, axis=0, keepdims=True)
            dw_cols.append(jnp.concatenate(dw_acc + [jnp.zeros((8 - KCONV, CONV_CC), F32)], axis=0))
            db_cols.append(jnp.broadcast_to(db_acc, (8, CONV_CC)))
        dw_part = jnp.concatenate(dw_cols, axis=1)
        db_part = jnp.concatenate(db_cols, axis=1)

        @pl.when(i == 0)
        def _():
            dw_ref[...] = dw_part
            db_ref[...] = db_part

        @pl.when(i > 0)
        def _():
            dw_ref[...] += dw_part
            db_ref[...] += db_part

    xs_spec = pl.BlockSpec((tm, tc), lambda j, i: (jnp.where(j < 2, i, 0), jnp.minimum(j, 1)))
    bc_spec = pl.BlockSpec((tm, tc), lambda j, i: (jnp.where(j == 2, i, 0), 0))
    in_specs = [
        pl.BlockSpec((8, tc), lambda j, i: (jnp.maximum(i * r8 - 1, 0), c0 + j)),
        pl.BlockSpec((tm, tc), lambda j, i: (i, c0 + j)),
        pl.BlockSpec((8, tc), lambda j, i: (jnp.minimum((i + 1) * r8, nb8 - 1), c0 + j)),
        xs_spec, xs_spec, bc_spec,
        pl.BlockSpec((1, tc), lambda j, i: (0, jnp.minimum(j, 1))),
        pl.BlockSpec((KCONV, tc), lambda j, i: (0, j)), pl.BlockSpec((1, tc), lambda j, i: (0, j)),
    ]
    return pl.pallas_call(
        body,
        out_shape=(jax.ShapeDtypeStruct((t, CONVD), F32), jax.ShapeDtypeStruct((8, CONVD), F32),
                   jax.ShapeDtypeStruct((8, CONVD), F32)),
        grid=(CONVD // tc, t // tm), in_specs=in_specs,
        out_specs=(pl.BlockSpec((tm, tc), lambda j, i: (i, j)),
                   pl.BlockSpec((8, tc), lambda j, i: (0, j)), pl.BlockSpec((8, tc), lambda j, i: (0, j))),
        scratch_shapes=[pltpu.VMEM((tm + 16, tc), F32)],
        name="conv_dpre", compiler_params=_params(("parallel", "arbitrary")))(
            u, u, u, dxs, dy, dbc, dsk_row, conv_w, conv_b)


def _conv_dx(du, dpre, conv_w):
    t = dpre.shape[0]
    tm, tc = CONV_TM, CONV_TC
    r8 = tm // 8
    nb8 = t // 8

    def body(prev_ref, cur_ref, next_ref, w_ref, du_in, du_out, ext):
        del du_in
        _fill_ext(ext, prev_ref, cur_ref, next_ref, tm, pl.program_id(1), t // tm - 1)
        for c0 in range(0, tc, CONV_CC):
            cs = slice(c0, c0 + CONV_CC)
            w = w_ref[:, cs]
            for r0 in range(0, tm, CONV_RC):
                acc = jnp.zeros((CONV_RC, CONV_CC), F32)
                for k in range(KCONV):
                    acc = acc + w[k:k + 1, :] * ext[pl.ds(r0 + 10 - k, CONV_RC), cs]
                du_out[r0:r0 + CONV_RC, cs] = acc.astype(du_out.dtype)

    in_specs = [
        pl.BlockSpec((8, tc), lambda j, i: (jnp.maximum(i * r8 - 1, 0), j)),
        pl.BlockSpec((tm, tc), lambda j, i: (i, j)),
        pl.BlockSpec((8, tc), lambda j, i: (jnp.minimum((i + 1) * r8, nb8 - 1), j)),
        pl.BlockSpec((KCONV, tc), lambda j, i: (0, j)),
        pl.BlockSpec(memory_space=pl.ANY),
    ]
    return pl.pallas_call(
        body, out_shape=jax.ShapeDtypeStruct(du.shape, du.dtype), grid=(CONVD // tc, t // tm), in_specs=in_specs,
        out_specs=pl.BlockSpec((tm, tc), lambda j, i: (i, OXBC // tc + j)),
        scratch_shapes=[pltpu.VMEM((tm + 16, tc), F32)], input_output_aliases={4: 0},
        name="conv_dx", compiler_params=_params(("parallel", "parallel")))(dpre, dpre, dpre, conv_w, du)


def _ssd_common(dtr_ref, par_ref, rev):
    raw = dtr_ref[...]
    lane = _iota((1, 128), 1)
    mine = (lane >= 32 * rev) & (lane < 32 * rev + 32)
    bias = par_ref[0:1, :]
    arow = jnp.where(mine, -jnp.exp(par_ref[1:2, :]), 0.0)
    dt = _softplus(raw + bias)
    a = dt * arow
    ri = _iota((Q, Q), 0)
    ci = _iota((Q, Q), 1)
    tri = (ci >= ri) if rev else (ci <= ri)
    trit = (ci <= ri) if rev else (ci >= ri)
    cs = _dot01_l(tri.astype(BF16), a)
    return raw, bias, arow, mine, dt, cs, tri, trit


def _expand_mat(rev):
    r = np.arange(128)[:, None]
    c = np.arange(DI)[None, :]
    return jnp.asarray(r == (c // HP) + 32 * rev, BF16)


def _sum_mat(rev):
    r = np.arange(DI)[:, None]
    c = np.arange(128)[None, :]
    return jnp.asarray(c == (r // HP) + 32 * rev, BF16)


def _ssd_fwd(xbc, u, par, y_add=None, *, rev):
    t = xbc.shape[0]
    nc = t // Q
    end = 0 if rev else Q - 1
    cmap = (lambda c: nc - 1 - c) if rev else (lambda c: c)

    def body(xbc_ref, dtr_ref, par_ref, ex_ref, *rest):
        yadd_ref = rest[0] if y_add is not None else None
        y_ref, st_ref, h_scr = rest[-3:]
        step = pl.program_id(0)

        @pl.when(step == 0)
        def _():
            h_scr[...] = jnp.zeros((NS, DI), F32)

        raw, bias, arow, mine, dt, cs, tri, trit = _ssd_common(dtr_ref, par_ref, rev)
        cst = cs.T
        dtt = dt.T
        tot_col = cst[:, end:end + 1]
        wt = dtt * jnp.exp(tot_col - cst)
        gam = jnp.exp(cs[end:end + 1, :])
        gam_x = _dot01(jnp.broadcast_to(gam, (8, 128)), ex_ref[...])[0:1, :]
        lane = _iota((Q, 128), 1)
        sel = lane < HP
        st_ref[...] = h_scr[...]
        for g in range(NG):
            bg = xbc_ref[:, DI + NS * g:DI + NS * (g + 1)]
            cg = xbc_ref[:, DI + NG * NS + NS * g:DI + NG * NS + NS * (g + 1)]
            cb = _dot_nt(cg.astype(BF16), bg.astype(BF16))
            bt = bg.T
            for k in range(4):
                lo = 512 * g + 128 * k
                xp = xbc_ref[:, lo:lo + 128].astype(BF16)
                hp = h_scr[:, lo:lo + 128]
                rhs = jnp.concatenate([xp, hp.astype(BF16)], axis=0)
                lhs, bts = [], []
                for j in range(2):
                    hc = 8 * g + 2 * k + j + 32 * rev
                    csc = jnp.broadcast_to(cs[:, hc:hc + 1], (Q, Q))
                    lm = jnp.exp(jnp.where(tri, csc - cst[hc:hc + 1, :], NEG)) * dtt[hc:hc + 1, :]
                    mh = (cb * lm).astype(BF16)
                    ec = (jnp.exp(csc) * cg).astype(BF16)
                    lhs.append(jnp.concatenate([mh, ec], axis=1))
                    bts.append((bt * wt[hc:hc + 1, :]).astype(BF16))
                ys = jnp.dot(jnp.concatenate(lhs, axis=0), rhs, preferred_element_type=F32)
                ss = jnp.dot(jnp.concatenate(bts, axis=0), xp, preferred_element_type=F32)
                yp = jnp.where(sel, ys[0:Q], ys[Q:2 * Q])
                y_ref[:, lo:lo + 128] = yp if yadd_ref is None else yp + yadd_ref[:, lo:lo + 128]
                h_scr[:, lo:lo + 128] = gam_x[:, lo:lo + 128] * hp + jnp.where(sel, ss[0:NS], ss[NS:2 * NS])

    return pl.pallas_call(
        body,
        out_shape=(jax.ShapeDtypeStruct((t, DI), F32), jax.ShapeDtypeStruct((nc, NS, DI), F32)),
        grid=(nc,),
        in_specs=[pl.BlockSpec((Q, CONVD), lambda c: (cmap(c), 0)),
                  pl.BlockSpec((Q, 128), lambda c: (cmap(c), ODT // 128)),
                  pl.BlockSpec((8, 128), lambda c: (0, 0)),
                  pl.BlockSpec((128, DI), lambda c: (0, 0))]
        + ([pl.BlockSpec((Q, DI), lambda c: (cmap(c), 0))] if y_add is not None else []),
        out_specs=(pl.BlockSpec((Q, DI), lambda c: (cmap(c), 0)),
                   pl.BlockSpec((None, NS, DI), lambda c: (cmap(c), 0, 0))),
        scratch_shapes=[pltpu.VMEM((NS, DI), F32)],
        name="ssd_fwd_rev" if rev else "ssd_fwd", compiler_params=_params(("arbitrary",)))(
            xbc, u, par, _expand_mat(rev), *([y_add] if y_add is not None else []))


def _ssd_bwd(xbc, u, par, dy, st, *, rev, add=None, side=None):
    t = xbc.shape[0]
    nc = t // Q
    end = 0 if rev else Q - 1
    cmap = (lambda c: c) if rev else (lambda c: nc - 1 - c)

    def body(xbc_ref, dtr_ref, par_ref, dy_ref, hin_ref, ex_ref, sm_ref, *rest):
        addx_ref, addbc_ref, addt_ref = rest[:3] if add is not None else (None, None, None)
        dxs_ref, dbc_ref, ddt_ref, acc_ref, dh_scr = rest[-5:]
        step = pl.program_id(0)

        @pl.when(step == 0)
        def _():
            dh_scr[...] = jnp.zeros((NS, DI), F32)

        raw, bias, arow, mine, dt, cs, tri, trit = _ssd_common(dtr_ref, par_ref, rev)
        ri = _iota((Q, Q), 0)
        ci = _iota((Q, Q), 1)
        stri = ((ri > ci) if rev else (ri < ci)).astype(BF16)
        strit = ((ci > ri) if rev else (ci < ri)).astype(BF16)
        cst = cs.T
        dtt = dt.T
        et = jnp.exp(cst)
        expand = ex_ref[...]
        summat = sm_ref[...]
        gam = jnp.exp(cs[end:end + 1, :])
        gam_x = _dot01(jnp.broadcast_to(gam, (8, 128)), expand)[0:1, :]
        dt_hi, dt_mid, _ = _split3(dt)
        dtx = (jnp.dot(dt_hi, expand, preferred_element_type=F32)
               + jnp.dot(dt_mid, expand, preferred_element_type=F32))
        lane = _iota((Q, 128), 1)
        sel = lane < HP
        dho = dh_scr[...]
        t3 = jnp.sum(dho * hin_ref[...], axis=0, keepdims=True) * gam_x
        dxs_cols, dxs2_cols, yoff_cols, a1_rows = [], [], [], []
        for g in range(NG):
            bg = xbc_ref[:, DI + NS * g:DI + NS * (g + 1)]
            cg = xbc_ref[:, DI + NG * NS + NS * g:DI + NG * NS + NS * (g + 1)]
            bb = bg.astype(BF16)
            cbf = cg.astype(BF16)
            cb = _dot_nt(cbf, bb)
            cbt = _dot_nt(bb, cbf)
            ct = cg.T
            bdh = jnp.dot(bb, dho[:, 512 * g:512 * (g + 1)].astype(BF16), preferred_element_type=F32)
            dcb = jnp.zeros((Q, Q), F32)
            dcg = jnp.zeros((Q, NS), F32)
            dbg = jnp.zeros((Q, NS), F32)
            for k in range(4):
                lo = 512 * g + 128 * k
                xpf = xbc_ref[:, lo:lo + 128]
                xp = xpf.astype(BF16)
                dyp = dy_ref[:, lo:lo + 128]
                dypb = dyp.astype(BF16)
                hinp = hin_ref[:, lo:lo + 128].astype(BF16)
                dhp = dho[:, lo:lo + 128]
                es, ws, lmds, mts, ctes, dyms, ecbs = [], [], [], [], [], [], []
                for j in range(2):
                    hc = 8 * g + 2 * k + j + 32 * rev
                    csc = jnp.broadcast_to(cs[:, hc:hc + 1], (Q, Q))
                    csr = cst[hc:hc + 1, :]
                    lmds.append(jnp.exp(jnp.where(tri, csc - csr, NEG)) * dtt[hc:hc + 1, :])
                    lmb = jnp.exp(jnp.where(trit, csr - csc, NEG))
                    mts.append((cbt * lmb).astype(BF16))
                    dyms.append(jnp.where(sel if j == 0 else ~sel, dyp, 0.0).astype(BF16))
                    ecs = jnp.exp(csc)
                    es.append(ecs)
                    ws.append(jnp.exp(cst[hc:hc + 1, end:end + 1] - csc))
                    ecbs.append((ecs * cg).astype(BF16))
                    ctes.append((ct * et[hc:hc + 1, :]).astype(BF16))
                by_dy = jnp.dot(jnp.concatenate(mts + ctes, axis=0), dypb, preferred_element_type=F32)
                dmm = _dot_nt(jnp.concatenate(dyms, axis=0), xp)
                dm0, dm1 = dmm[0:Q] * lmds[0], dmm[Q:2 * Q] * lmds[1]
                dcb = dcb + dm0 + dm1
                rr = jnp.dot(jnp.concatenate([dm0 * cb, dm1 * cb], axis=0).astype(BF16), stri, preferred_element_type=F32)
                a1_rows.append(jnp.sum(jnp.where(tri, rr[0:Q], 0.0), axis=0, keepdims=True))
                a1_rows.append(jnp.sum(jnp.where(tri, rr[Q:2 * Q], 0.0), axis=0, keepdims=True))
                yo = jnp.dot(jnp.concatenate(ecbs, axis=0), hinp, preferred_element_type=F32)
                e_p = jnp.where(sel, es[0], es[1])
                w_p = jnp.where(sel, ws[0], ws[1])
                d2 = w_p * bdh[:, 128 * k:128 * (k + 1)]
                dxs2_cols.append(d2)
                dxs_cols.append(jnp.where(sel, by_dy[0:Q], by_dy[Q:2 * Q]) + d2)
                yoff_cols.append(jnp.where(sel, yo[0:Q], yo[Q:2 * Q]))
                dcg = dcg + _dot_nt((e_p * dyp).astype(BF16), hinp)
                dbg = dbg + _dot_nt((w_p * dtx[:, lo:lo + 128] * xpf).astype(BF16), dhp.astype(BF16))
                dh_scr[:, lo:lo + 128] = (gam_x[:, lo:lo + 128] * dhp
                                          + jnp.where(sel, by_dy[2 * Q:3 * Q], by_dy[3 * Q:4 * Q]))
            dcg = dcg + jnp.dot(dcb.astype(BF16), bb, preferred_element_type=F32)
            dbg = dbg + jnp.dot(dcb.T.astype(BF16), cbf, preferred_element_type=F32)
            lo_b, lo_c = NS * g, NG * NS + NS * g
            if addbc_ref is not None:
                dbg = dbg + addbc_ref[:, lo_b:lo_b + NS]
                dcg = dcg + addbc_ref[:, lo_c:lo_c + NS]
            dbc_ref[:, lo_b:lo_b + NS] = dbg
            dbc_ref[:, lo_c:lo_c + NS] = dcg
        dxs = jnp.concatenate(dxs_cols, axis=1)
        dxs_ref[...] = dxs * dtx if addx_ref is None else dxs * dtx + addx_ref[...]
        xs = xbc_ref[:, 0:DI]
        stacked = jnp.concatenate([xs * dxs, xs * jnp.concatenate(dxs2_cols, axis=1),
                                   dy_ref[...] * jnp.concatenate(yoff_cols, axis=1),
                                   jnp.broadcast_to(t3, (8, DI))], axis=0).astype(BF16)
        sums = jnp.dot(stacked, summat, preferred_element_type=F32)
        rx, rx2, ryo, c0 = sums[0:Q], sums[Q:2 * Q], sums[2 * Q:3 * Q], sums[3 * Q:3 * Q + 1]
        zero32 = jnp.zeros((32, Q), F32)
        a1t = jnp.concatenate(([zero32] if rev else []) + a1_rows + [zero32] * (2 if rev else 3), axis=0)
        da = (a1t.T + jnp.dot(trit.astype(BF16), ryo.astype(BF16), preferred_element_type=F32)
              + jnp.dot(strit, (dt * rx2).astype(BF16), preferred_element_type=F32) + jnp.where(mine, c0, 0.0))
        ddt = rx + da * arow
        ddtr = ddt * _sigmoid(raw + bias)
        ddt_ref[...] = ddtr if addt_ref is None else ddtr + addt_ref[...]
        part = jnp.concatenate([jnp.sum(ddtr, axis=0, keepdims=True),
                                jnp.sum(da * dt, axis=0, keepdims=True) * arow,
                                jnp.zeros((6, 128), F32)], axis=0)

        @pl.when(step == 0)
        def _():
            acc_ref[...] = part

        @pl.when(step > 0)
        def _():
            acc_ref[...] += part

    outs, side_outs = _host_call(
        body, side, nc,
        out_shape=(jax.ShapeDtypeStruct((t, DI), F32), jax.ShapeDtypeStruct((t, 2 * NG * NS), F32),
                   jax.ShapeDtypeStruct((t, 128), F32), jax.ShapeDtypeStruct((8, 128), F32)),
        in_specs=[pl.BlockSpec((Q, CONVD), lambda c: (cmap(c), 0)),
                  pl.BlockSpec((Q, 128), lambda c: (cmap(c), ODT // 128)),
                  pl.BlockSpec((8, 128), lambda c: (0, 0)),
                  pl.BlockSpec((Q, DI), lambda c: (cmap(c), 0)),
                  pl.BlockSpec((None, NS, DI), lambda c: (cmap(c), 0, 0)),
                  pl.BlockSpec((128, DI), lambda c: (0, 0)), pl.BlockSpec((DI, 128), lambda c: (0, 0))]
        + ([pl.BlockSpec((Q, DI), lambda c: (cmap(c), 0)), pl.BlockSpec((Q, 2 * NG * NS), lambda c: (cmap(c), 0)),
            pl.BlockSpec((Q, 128), lambda c: (cmap(c), 0))] if add is not None else []),
        out_specs=(pl.BlockSpec((Q, DI), lambda c: (cmap(c), 0)),
                   pl.BlockSpec((Q, 2 * NG * NS), lambda c: (cmap(c), 0)),
                   pl.BlockSpec((Q, 128), lambda c: (cmap(c), 0)),
                   pl.BlockSpec((8, 128), lambda c: (0, 0))),
        scratch_shapes=[pltpu.VMEM((NS, DI), F32)],
        args=(xbc, u, par, dy, st, _expand_mat(rev), _sum_mat(rev)) + (tuple(add) if add is not None else ()), aliases={},
        name="ssd_bwd_rev" if rev else "ssd_bwd", sem=("arbitrary",))
    return (*outs, side_outs)


GN_TM = 256
GN_GROUP = DI // NG


def _gn_forward_vals(y0, xs, z, dsk):
    y = y0 + dsk * xs
    sz = _sigmoid(z)
    gate = z * sz
    y2 = y * gate
    parts, rs = [], []
    for g in range(NG):
        seg = y2[:, GN_GROUP * g:GN_GROUP * (g + 1)]
        r = lax.rsqrt(jnp.mean(seg * seg, axis=1, keepdims=True) + NORM_EPS)
        rs.append(r)
        parts.append(seg * r)
    yn = jnp.concatenate(parts, axis=1)
    return y, sz, gate, yn, rs


def _gatenorm_fwd(y_fb, xbc, u, dsk_row, nw_row):
    t = y_fb.shape[0]
    tm = GN_TM

    def body(y_ref, xs_ref, z_ref, dsk_ref, nw_ref, o_ref):
        _, _, _, yn, _ = _gn_forward_vals(y_ref[...], xs_ref[...], z_ref[...], dsk_ref[...])
        o_ref[...] = (yn * nw_ref[...]).astype(BF16)

    blk = pl.BlockSpec((tm, DI), lambda i: (i, 0))
    row = pl.BlockSpec((1, DI), lambda i: (0, 0))
    return pl.pallas_call(
        body, out_shape=jax.ShapeDtypeStruct((t, DI), BF16), grid=(t // tm,),
        in_specs=[blk, blk, pl.BlockSpec((tm, DI), lambda i: (i, OZ // DI)), row, row],
        out_specs=blk, name="gatenorm_fwd", compiler_params=_params(("parallel",)))(y_fb, xbc, u, dsk_row, nw_row)


def _gatenorm_bwd(ds_out, y_fb, xbc, u, du, dsk_row, nw_row, side=None):
    t = y_fb.shape[0]
    tm = GN_TM

    def body(ds_ref, y_ref, xs_ref, z_ref, dsk_ref, nw_ref, sm_ref, du_in, dy_ref, du_out, dnw_ref, dds_ref):
        del du_in
        i = pl.program_id(0)
        xs = xs_ref[...]
        z = z_ref[...]
        y, sz, gate, yn, rs = _gn_forward_vals(y_ref[...], xs, z, dsk_ref[...])
        ds = ds_ref[...]
        gsc = ds * nw_ref[...]
        parts = []
        for g in range(NG):
            sl = slice(GN_GROUP * g, GN_GROUP * (g + 1))
            m = jnp.mean(gsc[:, sl] * yn[:, sl], axis=1, keepdims=True)
            parts.append(rs[g] * (gsc[:, sl] - yn[:, sl] * m))
        dy2 = jnp.concatenate(parts, axis=1)
        dy = dy2 * gate
        dy_ref[...] = dy
        du_out[...] = (dy2 * y * (sz * (1.0 + z * (1.0 - sz)))).astype(du_out.dtype)
        dnw = jnp.broadcast_to(jnp.sum(ds * yn, axis=0, keepdims=True), (8, DI))
        drow = jnp.broadcast_to(jnp.sum(dy * xs, axis=0, keepdims=True), (8, DI))
        dds = _dot01(drow, sm_ref[...])

        @pl.when(i == 0)
        def _():
            dnw_ref[...] = dnw
            dds_ref[...] = dds

        @pl.when(i > 0)
        def _():
            dnw_ref[...] += dnw
            dds_ref[...] += dds

    blk = pl.BlockSpec((tm, DI), lambda i: (i, 0))
    row = pl.BlockSpec((1, DI), lambda i: (0, 0))
    outs, side_outs = _host_call(
        body, side, t // tm,
        out_shape=(jax.ShapeDtypeStruct((t, DI), F32), jax.ShapeDtypeStruct(du.shape, du.dtype),
                   jax.ShapeDtypeStruct((8, DI), F32), jax.ShapeDtypeStruct((8, 128), F32)),
        in_specs=[blk, blk, blk, pl.BlockSpec((tm, DI), lambda i: (i, OZ // DI)), row, row,
                  pl.BlockSpec((DI, 128), lambda i: (0, 0)), pl.BlockSpec(memory_space=pl.ANY)],
        out_specs=(blk, pl.BlockSpec((tm, DI), lambda i: (i, OZ // DI)),
                   pl.BlockSpec((8, DI), lambda i: (0, 0)), pl.BlockSpec((8, 128), lambda i: (0, 0))),
        scratch_shapes=[], args=(ds_out, y_fb, xbc, u, dsk_row, nw_row, _sum_mat(0), du), aliases={7: 1},
        name="gatenorm_bwd", sem=("arbitrary",))
    return (*outs, side_outs)


AT_B = 128
AT_W = AT_B + 2 * ATT_HALF
AT_L = 2 * AH
SCALE = 1.0 / math.sqrt(AH)


def _slope(g, hh):
    return 2.0 ** (-8.0 * (4 * g + hh + 1) / 12.0)


def _qcol(g):
    return lambda p: OQ // AT_L + 2 * g + p


def _kcol(g):
    return lambda p: OKV // AT_L + 4 * g + 2 * p


def _vcol(g):
    return lambda p: OKV // AT_L + 4 * g + 2 * p + 1


def _pcol(p):
    return p


def _sub(d):
    return 4 if d == 1 else 1


def _win_specs(col, t, d):
    tb, hb = AT_B * d * _sub(d), ATT_HALF * d
    per = tb // hb
    nh = t // hb
    return [
        pl.BlockSpec((hb, AT_L), lambda p, i: (jnp.maximum(per * i - 1, 0), col(p))),
        pl.BlockSpec((tb, AT_L), lambda p, i: (i, col(p))),
        pl.BlockSpec((hb, AT_L), lambda p, i: (jnp.minimum(per * (i + 1), nh - 1), col(p))),
    ]


def _blk_spec(col, d):
    return pl.BlockSpec((AT_B * d * _sub(d), AT_L), lambda p, i: (i, col(p)))


def _rows(ref, r, s, d):
    return ref[pl.ds(r, AT_B, stride=d), :] if d > 1 else ref[AT_B * s:AT_B * (s + 1), :]


def _win(p_ref, c_ref, n_ref, r, s, d):
    if d > 1:
        return jnp.concatenate([p_ref[pl.ds(r, ATT_HALF, stride=d), :], c_ref[pl.ds(r, AT_B, stride=d), :],
                                n_ref[pl.ds(r, ATT_HALF, stride=d), :]], axis=0)
    if s == 0:
        return jnp.concatenate([p_ref[...], c_ref[0:AT_B + ATT_HALF, :]], axis=0)
    if s == _sub(d) - 1:
        return jnp.concatenate([c_ref[AT_B * s - ATT_HALF:AT_B * (s + 1), :], n_ref[...]], axis=0)
    return c_ref[AT_B * s - ATT_HALF:AT_B * (s + 1) + ATT_HALF, :]


def _put_rows(ref, r, s, d, val):
    if d > 1:
        ref[pl.ds(r, AT_B, stride=d), :] = val
    else:
        ref[AT_B * s:AT_B * (s + 1), :] = val


def _for_blocks(d, fn):
    if d == 1:
        for s in range(_sub(d)):
            fn(0, s)
    else:
        def step(r, c):
            fn(r, 0)
            return c
        lax.fori_loop(0, d, step, 0, unroll=4)


def _attn_bias(blk, ln, d, g, p_id):
    a = blk * AT_B + _iota((AT_B, AT_W), 0)
    b = blk * AT_B - ATT_HALF + _iota((AT_B, AT_W), 1)
    rel = jnp.abs(a - b)
    valid = (rel <= ATT_HALF) & (b >= 0) & (b < ln)
    dist = (rel * d).astype(F32)
    out = []
    for hh in range(2):
        slope = jnp.where(p_id == 0, _slope(g, hh), _slope(g, 2 + hh))
        out.append(jnp.where(valid, -slope * dist, NEG))
    return out


def _attn_fwd(u, g):
    t = u.shape[0]
    d = DILATIONS[g]
    ln = t // d

    def body(q_ref, kp, kc, kn, vp, vc, vn, o_ref, l_ref):
        p_id = pl.program_id(0)
        i = pl.program_id(1)
        lane = _iota((AT_B, AT_L), 1)
        biases = [_attn_bias(i * _sub(d) + s, ln, d, g, p_id) for s in range(_sub(d))]

        def one(r, s):
            q = _rows(q_ref, r, s, d)
            kw = _win(kp, kc, kn, r, s, d).astype(BF16)
            vw = _win(vp, vc, vn, r, s, d).astype(BF16)
            o = jnp.zeros((AT_B, AT_L), F32)
            lse = jnp.zeros((AT_B, AT_L), F32)
            for hh in range(2):
                hm = (lane // AH) == hh
                qm = jnp.where(hm, q, 0.0).astype(BF16)
                sc = _dot_nt(qm, kw) * SCALE + biases[s][hh]
                m = jnp.max(sc, axis=1, keepdims=True)
                pr = jnp.exp(sc - m)
                den = jnp.sum(pr, axis=1, keepdims=True)
                oh = jnp.dot(pr.astype(BF16), vw, preferred_element_type=F32)
                o = jnp.where(hm, oh / den, o)
                lse = jnp.where(hm, m + jnp.log(den), lse)
            _put_rows(o_ref, r, s, d, o)
            _put_rows(l_ref, r, s, d, lse)

        _for_blocks(d, one)

    oshape = jax.ShapeDtypeStruct((t, 2 * AT_L), F32)
    ospec = _blk_spec(_pcol, d)
    return pl.pallas_call(
        body, out_shape=(oshape, oshape), grid=(2, t // (AT_B * d * _sub(d))),
        in_specs=[_blk_spec(_qcol(g), d)] + _win_specs(_kcol(g), t, d) + _win_specs(_vcol(g), t, d),
        out_specs=(ospec, ospec), name=f"attn_fwd_{g}", compiler_params=_params(("parallel", "parallel")))(
            u, u, u, u, u, u, u)


def _attn_dq(u, du, do, lse, e, g):
    t = u.shape[0]
    d = DILATIONS[g]
    ln = t // d

    def body(q_ref, kp, kc, kn, vp, vc, vn, do_ref, l_ref, e_ref, du_in, dq_ref, dq_scr):
        del du_in
        p_id = pl.program_id(0)
        i = pl.program_id(1)
        lane = _iota((AT_B, AT_L), 1)
        biases = [_attn_bias(i * _sub(d) + s, ln, d, g, p_id) for s in range(_sub(d))]

        def one(r, s):
            q = _rows(q_ref, r, s, d)
            kw = _win(kp, kc, kn, r, s, d).astype(BF16)
            vw = _win(vp, vc, vn, r, s, d).astype(BF16)
            do_ = _rows(do_ref, r, s, d)
            lv = _rows(l_ref, r, s, d)
            ev = _rows(e_ref, r, s, d)
            dq = jnp.zeros((AT_B, AT_L), F32)
            for hh in range(2):
                hm = (lane // AH) == hh
                qm = jnp.where(hm, q, 0.0).astype(BF16)
                sc = _dot_nt(qm, kw) * SCALE + biases[s][hh]
                lcol = jnp.broadcast_to(lv[:, AH * hh:AH * hh + 1], (AT_B, AT_W))
                ecol = jnp.broadcast_to(ev[:, AH * hh:AH * hh + 1], (AT_B, AT_W))
                pr = jnp.exp(sc - lcol)
                dom = jnp.where(hm, do_, 0.0).astype(BF16)
                ds = pr * (_dot_nt(dom, vw) + ecol)
                dqh = jnp.dot(ds.astype(BF16), kw, preferred_element_type=F32) * SCALE
                dq = jnp.where(hm, dqh, dq)
            _put_rows(dq_scr, r, s, d, dq)

        _for_blocks(d, one)
        dq_ref[...] = dq_scr[...].astype(dq_ref.dtype)

    rspec = _blk_spec(_pcol, d)
    return pl.pallas_call(
        body, out_shape=jax.ShapeDtypeStruct(du.shape, du.dtype), grid=(2, t // (AT_B * d * _sub(d))),
        in_specs=[_blk_spec(_qcol(g), d)] + _win_specs(_kcol(g), t, d) + _win_specs(_vcol(g), t, d)
        + [rspec, rspec, rspec, pl.BlockSpec(memory_space=pl.ANY)],
        out_specs=_blk_spec(_qcol(g), d), input_output_aliases={10: 0},
        scratch_shapes=[pltpu.VMEM((AT_B * d * _sub(d), AT_L), F32)],
        name=f"attn_dq_{g}", compiler_params=_params(("parallel", "parallel")))(
            u, u, u, u, u, u, u, do, lse, e, du)


def _attn_dkv(u, du, do, lse, e, g):
    t = u.shape[0]
    d = DILATIONS[g]
    ln = t // d

    def body(k_ref, v_ref, qp, qc, qn, dp_, dc_, dn_, lp, lc, ln_, ep, ec, en, du_in, dkv_ref, dk_scr, dv_scr):
        del du_in
        p_id = pl.program_id(0)
        jb = pl.program_id(1)
        lane = _iota((AT_B, AT_L), 1)
        biases = [_attn_bias(jb * _sub(d) + s, ln, d, g, p_id) for s in range(_sub(d))]

        def one(r, s):
            k = _rows(k_ref, r, s, d)
            v = _rows(v_ref, r, s, d)
            qw = _win(qp, qc, qn, r, s, d).astype(BF16)
            dow = _win(dp_, dc_, dn_, r, s, d).astype(BF16)
            lt = _win(lp, lc, ln_, r, s, d).T
            et = _win(ep, ec, en, r, s, d).T
            dk = jnp.zeros((AT_B, AT_L), F32)
            dv = jnp.zeros((AT_B, AT_L), F32)
            for hh in range(2):
                hm = (lane // AH) == hh
                km = jnp.where(hm, k, 0.0).astype(BF16)
                st = _dot_nt(km, qw) * SCALE + biases[s][hh]
                pt = jnp.exp(st - lt[AH * hh:AH * hh + 1, :])
                dvh = jnp.dot(pt.astype(BF16), dow, preferred_element_type=F32)
                vm = jnp.where(hm, v, 0.0).astype(BF16)
                dst = pt * (_dot_nt(vm, dow) + et[AH * hh:AH * hh + 1, :])
                dkh = jnp.dot(dst.astype(BF16), qw, preferred_element_type=F32) * SCALE
                dk = jnp.where(hm, dkh, dk)
                dv = jnp.where(hm, dvh, dv)
            _put_rows(dk_scr, r, s, d, dk)
            _put_rows(dv_scr, r, s, d, dv)

        _for_blocks(d, one)
        dkv_ref[:, 0:AT_L] = dk_scr[...].astype(dkv_ref.dtype)
        dkv_ref[:, AT_L:2 * AT_L] = dv_scr[...].astype(dkv_ref.dtype)

    return pl.pallas_call(
        body, out_shape=jax.ShapeDtypeStruct(du.shape, du.dtype), grid=(2, t // (AT_B * d * _sub(d))),
        in_specs=[_blk_spec(_kcol(g), d), _blk_spec(_vcol(g), d)]
        + _win_specs(_qcol(g), t, d) + _win_specs(_pcol, t, d) + _win_specs(_pcol, t, d) + _win_specs(_pcol, t, d)
        + [pl.BlockSpec(memory_space=pl.ANY)],
        out_specs=pl.BlockSpec((AT_B * d * _sub(d), 2 * AT_L), lambda p, i: (i, OKV // (2 * AT_L) + 2 * g + p)),
        input_output_aliases={14: 0},
        scratch_shapes=[pltpu.VMEM((AT_B * d * _sub(d), AT_L), F32), pltpu.VMEM((AT_B * d * _sub(d), AT_L), F32)],
        name=f"attn_dkv_{g}", compiler_params=_params(("parallel", "parallel")))(
            u, u, u, u, u, do, do, do, lse, lse, lse, e, e, e, du)


CMB_TM = 1024


def _combine_weights(l0, l1, l2):
    m = jnp.maximum(jnp.maximum(l0, l1), l2)
    e0, e1, e2 = jnp.exp(l0 - m), jnp.exp(l1 - m), jnp.exp(l2 - m)
    inv = 1.0 / (e0 + e1 + e2)
    return e0 * inv, e1 * inv, e2 * inv


def _combine_fwd(os_, ls_):
    t = os_[0].shape[0]
    tm = CMB_TM

    def body(o0, o1, o2, l0, l1, l2, a_ref):
        w0, w1, w2 = _combine_weights(l0[...], l1[...], l2[...])
        a_ref[...] = w0 * o0[...] + w1 * o1[...] + w2 * o2[...]

    blk = pl.BlockSpec((tm, 2 * AT_L), lambda i: (i, 0))
    return pl.pallas_call(
        body, out_shape=jax.ShapeDtypeStruct((t, 2 * AT_L), F32), grid=(t // tm,), in_specs=[blk] * 6, out_specs=blk,
        name="combine_fwd", compiler_params=_params(("parallel",)))(*os_, *ls_)


def _combine_bwd(datt, os_, ls_):
    t = datt.shape[0]
    tm = CMB_TM

    def body(da_ref, o0, o1, o2, l0, l1, l2, d0, d1, d2, e0, e1, e2):
        w = _combine_weights(l0[...], l1[...], l2[...])
        da = da_ref[...]
        att = w[0] * o0[...] + w[1] * o1[...] + w[2] * o2[...]
        r = _iota((2 * AT_L, 2 * AT_L), 0) // AH
        c = _iota((2 * AT_L, 2 * AT_L), 1) // AH
        hs = _dot01(da * att, (r == c).astype(BF16))
        for wg, dref, eref in zip(w, (d0, d1, d2), (e0, e1, e2)):
            dref[...] = wg * da
            eref[...] = -wg * hs

    blk = pl.BlockSpec((tm, 2 * AT_L), lambda i: (i, 0))
    shp = jax.ShapeDtypeStruct((t, 2 * AT_L), F32)
    outs = pl.pallas_call(
        body, out_shape=(shp,) * 6, grid=(t // tm,), in_specs=[blk] * 7, out_specs=(blk,) * 6,
        name="combine_bwd", compiler_params=_params(("parallel",)))(datt, *os_, *ls_)
    return outs[0:3], outs[3:6]


def _combine_proj(os_, ls_, w_pa):
    t = os_[0].shape[0]
    tm = ROW_TM
    nsh, _, ws = w_pa.shape

    def body(o0, o1, o2, l0, l1, l2, w_ref, a_ref, y_ref):
        w0, w1, w2 = _combine_weights(l0[...], l1[...], l2[...])
        att = w0 * o0[...] + w1 * o1[...] + w2 * o2[...]
        a_ref[...] = att
        ab = att.astype(BF16)
        for sh in range(nsh):
            y_ref[:, ws * sh:ws * (sh + 1)] = jnp.dot(ab, w_ref[sh], preferred_element_type=F32)

    blk = pl.BlockSpec((tm, 2 * AT_L), lambda i: (i, 0))
    return pl.pallas_call(
        body, out_shape=(jax.ShapeDtypeStruct((t, 2 * AT_L), F32), jax.ShapeDtypeStruct((t, nsh * ws), F32)),
        grid=(t // tm,), in_specs=[blk] * 6 + [pl.BlockSpec(w_pa.shape, lambda i: (0, 0, 0))],
        out_specs=(blk, pl.BlockSpec((tm, nsh * ws), lambda i: (i, 0))),
        name="combine_proj", compiler_params=_params(("parallel",)))(*os_, *ls_, w_pa)


def _d_att_combine_bwd(dy_att, w_pa, os_, ls_):
    t = dy_att.shape[0]
    tm = ROW_TM
    nsh, _, ws = w_pa.shape

    def body(dy_ref, w_ref, o0, o1, o2, l0, l1, l2, d0, d1, d2, e0, e1, e2):
        da = jnp.zeros((tm, 2 * AT_L), F32)
        for sh in range(nsh):
            da = da + _dot_nt(dy_ref[:, ws * sh:ws * (sh + 1)], w_ref[sh])
        w = _combine_weights(l0[...], l1[...], l2[...])
        att = w[0] * o0[...] + w[1] * o1[...] + w[2] * o2[...]
        r = _iota((2 * AT_L, 2 * AT_L), 0) // AH
        c = _iota((2 * AT_L, 2 * AT_L), 1) // AH
        hs = _dot01(da * att, (r == c).astype(BF16))
        for wg, dref, eref in zip(w, (d0, d1, d2), (e0, e1, e2)):
            dref[...] = wg * da
            eref[...] = -wg * hs

    blk = pl.BlockSpec((tm, 2 * AT_L), lambda i: (i, 0))
    shp = jax.ShapeDtypeStruct((t, 2 * AT_L), F32)
    outs = pl.pallas_call(
        body, out_shape=(shp,) * 6, grid=(t // tm,),
        in_specs=[pl.BlockSpec((tm, nsh * ws), lambda i: (i, 0)), pl.BlockSpec(w_pa.shape, lambda i: (0, 0, 0))] + [blk] * 6,
        out_specs=(blk,) * 6, name="d_att_combine_bwd", compiler_params=_params(("parallel",)))(dy_att, w_pa, *os_, *ls_)
    return outs[0:3], outs[3:6]


ROW_TM = 512


def _mix_fwd(y_ssd, y_att, u, bg_row):
    t = y_ssd.shape[0]
    tm = ROW_TM

    def body(ys_ref, ya_ref, g0_ref, g1_ref, b0_ref, b1_ref, o_ref):
        g0 = _sigmoid(g0_ref[...] + b0_ref[...])
        g1 = _sigmoid(g1_ref[...] + b1_ref[...])
        o_ref[...] = (g0 * ys_ref[...] + g1 * ya_ref[...]).astype(BF16)

    blk = pl.BlockSpec((tm, D), lambda i: (i, 0))
    return pl.pallas_call(
        body, out_shape=jax.ShapeDtypeStruct((t, D), BF16), grid=(t // tm,),
        in_specs=[blk, blk, pl.BlockSpec((tm, D), lambda i: (i, OGATE // D)), pl.BlockSpec((tm, D), lambda i: (i, OGATE // D + 1)),
                  pl.BlockSpec((1, D), lambda i: (0, 0)), pl.BlockSpec((1, D), lambda i: (0, 1))],
        out_specs=blk, name="mix_fwd", compiler_params=_params(("parallel",)))(y_ssd, y_att, u, u, bg_row, bg_row)


def _mix_bwd(dmixin, y_ssd, y_att, u, bg_row):
    t = y_ssd.shape[0]
    tm = ROW_TM

    def body(dm_ref, ys_ref, ya_ref, g0_ref, g1_ref, b0_ref, b1_ref, dys_ref, dya_ref, du_ref, db_ref):
        i = pl.program_id(0)
        g0 = _sigmoid(g0_ref[...] + b0_ref[...])
        g1 = _sigmoid(g1_ref[...] + b1_ref[...])
        dm = dm_ref[...]
        dys_ref[...] = (dm * g0).astype(BF16)
        dya_ref[...] = (dm * g1).astype(BF16)
        dl0 = dm * ys_ref[...] * g0 * (1.0 - g0)
        dl1 = dm * ya_ref[...] * g1 * (1.0 - g1)
        du_ref[:, 0:D] = dl0.astype(BF16)
        du_ref[:, D:2 * D] = dl1.astype(BF16)
        part = jnp.concatenate([jnp.broadcast_to(jnp.sum(dl0, axis=0, keepdims=True), (8, D)),
                                jnp.broadcast_to(jnp.sum(dl1, axis=0, keepdims=True), (8, D))], axis=1)

        @pl.when(i == 0)
        def _():
            db_ref[...] = part

        @pl.when(i > 0)
        def _():
            db_ref[...] += part

    blk = pl.BlockSpec((tm, D), lambda i: (i, 0))
    return pl.pallas_call(
        body,
        out_shape=(jax.ShapeDtypeStruct((t, D), BF16), jax.ShapeDtypeStruct((t, D), BF16),
                   jax.ShapeDtypeStruct((t, UW), BF16), jax.ShapeDtypeStruct((8, 2 * D), F32)),
        grid=(t // tm,),
        in_specs=[blk, blk, blk, pl.BlockSpec((tm, D), lambda i: (i, OGATE // D)), pl.BlockSpec((tm, D), lambda i: (i, OGATE // D + 1)),
                  pl.BlockSpec((1, D), lambda i: (0, 0)), pl.BlockSpec((1, D), lambda i: (0, 1))],
        out_specs=(blk, blk, pl.BlockSpec((tm, 2 * D), lambda i: (i, OGATE // (2 * D))),
                   pl.BlockSpec((8, 2 * D), lambda i: (0, 0))),
        name="mix_bwd", compiler_params=_params(("arbitrary",)))(dmixin, y_ssd, y_att, u, u, bg_row, bg_row)


def _ln(x, g, b):
    mu = jnp.mean(x, axis=1, keepdims=True)
    xc = x - mu
    var = jnp.mean(xc * xc, axis=1, keepdims=True)
    rstd = lax.rsqrt(var + NORM_EPS)
    xhat = xc * rstd
    return xhat * g + b, xhat, rstd


def _ln_back(dh, xhat, rstd, g):
    dxh = dh * g
    m1 = jnp.mean(dxh, axis=1, keepdims=True)
    m2 = jnp.mean(dxh * xhat, axis=1, keepdims=True)
    return rstd * (dxh - m1 - xhat * m2)


def _ln1_fwd(x, mix, g_row, b_row):
    t = x.shape[0]
    tm = ROW_TM

    def body(x_ref, m_ref, g_ref, b_ref, pre_ref, h_ref):
        pre = ALPHA * x_ref[...] + m_ref[...]
        pre_ref[...] = pre
        h, _, _ = _ln(pre, g_ref[...], b_ref[...])
        h_ref[...] = h.astype(BF16)

    blk = pl.BlockSpec((tm, D), lambda i: (i, 0))
    row = pl.BlockSpec((1, D), lambda i: (0, 0))
    return pl.pallas_call(
        body, out_shape=(jax.ShapeDtypeStruct((t, D), F32), jax.ShapeDtypeStruct((t, D), BF16)), grid=(t // tm,),
        in_specs=[blk, blk, row, row], out_specs=(blk, blk),
        name="ln1_fwd", compiler_params=_params(("parallel",)))(x, mix, g_row, b_row)


def _ln1_bwd(dh, pre, g_row, b_row):
    t = dh.shape[0]
    tm = ROW_TM

    def body(dh_ref, pre_ref, g_ref, b_ref, dpre_ref, acc_ref):
        i = pl.program_id(0)
        dh_ = dh_ref[...]
        _, xhat, rstd = _ln(pre_ref[...], g_ref[...], b_ref[...])
        dpre_ref[...] = _ln_back(dh_, xhat, rstd, g_ref[...])
        part = jnp.concatenate([jnp.sum(dh_ * xhat, axis=0, keepdims=True), jnp.sum(dh_, axis=0, keepdims=True),
                                jnp.zeros((6, D), F32)], axis=0)

        @pl.when(i == 0)
        def _():
            acc_ref[...] = part

        @pl.when(i > 0)
        def _():
            acc_ref[...] += part

    blk = pl.BlockSpec((tm, D), lambda i: (i, 0))
    row = pl.BlockSpec((1, D), lambda i: (0, 0))
    return pl.pallas_call(
        body, out_shape=(jax.ShapeDtypeStruct((t, D), F32), jax.ShapeDtypeStruct((8, D), F32)), grid=(t // tm,),
        in_specs=[blk, blk, row, row], out_specs=(blk, pl.BlockSpec((8, D), lambda i: (0, 0))),
        name="ln1_bwd", compiler_params=_params(("arbitrary",)))(dh, pre, g_row, b_row)


def _ln2_loss(pre1, f, tgt, g1_row, b1_row, g2_row, b2_row):
    t = pre1.shape[0]
    tm = ROW_TM

    def body(p1_ref, f_ref, t_ref, g1_ref, b1_ref, g2_ref, b2_ref, dpre_ref, acc_ref):
        i = pl.program_id(0)
        h1, _, _ = _ln(p1_ref[...], g1_ref[...], b1_ref[...])
        pre2 = ALPHA * h1 + f_ref[...]
        h2, xhat, rstd = _ln(pre2, g2_ref[...], b2_ref[...])
        err = h2 - t_ref[...]
        dh = err * (1.0 / D)
        dpre_ref[...] = _ln_back(dh, xhat, rstd, g2_ref[...])
        loss = jnp.sum(jnp.sum(err * err, axis=1, keepdims=True), axis=0, keepdims=True) * (0.5 / D)
        part = jnp.concatenate([jnp.sum(dh * xhat, axis=0, keepdims=True), jnp.sum(dh, axis=0, keepdims=True),
                                jnp.broadcast_to(loss, (1, D)), jnp.zeros((5, D), F32)], axis=0)

        @pl.when(i == 0)
        def _():
            acc_ref[...] = part

        @pl.when(i > 0)
        def _():
            acc_ref[...] += part

    blk = pl.BlockSpec((tm, D), lambda i: (i, 0))
    row = pl.BlockSpec((1, D), lambda i: (0, 0))
    return pl.pallas_call(
        body, out_shape=(jax.ShapeDtypeStruct((t, D), F32), jax.ShapeDtypeStruct((8, D), F32)), grid=(t // tm,),
        in_specs=[blk, blk, blk, row, row, row, row], out_specs=(blk, pl.BlockSpec((8, D), lambda i: (0, 0))),
        name="ln2_loss", compiler_params=_params(("arbitrary",)))(pre1, f, tgt, g1_row, b1_row, g2_row, b2_row)


def _mlp_up(h1, w_up):
    t = h1.shape[0]
    tm, tn = ROW_TM, D

    def body(a_ref, b_ref, up_ref, act_ref):
        up = jnp.dot(a_ref[...], b_ref[...], preferred_element_type=F32)
        up_ref[...] = up.astype(BF16)
        r = jnp.maximum(up, 0.0)
        act_ref[...] = (r * r).astype(BF16)

    blk = pl.BlockSpec((tm, tn), lambda j, i: (i, j))
    return pl.pallas_call(
        body, out_shape=(jax.ShapeDtypeStruct((t, DFF), BF16), jax.ShapeDtypeStruct((t, DFF), BF16)),
        grid=(DFF // tn, t // tm),
        in_specs=[pl.BlockSpec((tm, D), lambda j, i: (i, 0)), pl.BlockSpec((None, D, tn), lambda j, i: (j, 0, 0))],
        out_specs=(blk, blk), name="mlp_up", compiler_params=_params(("parallel", "parallel")))(h1, w_up)


def _d_up(dpre2, w_down, up):
    t = up.shape[0]
    tm, tk = ROW_TM, D

    def body(a_ref, b_ref, u_ref, o_ref):
        dact = _dot_nt(a_ref[...], b_ref[...])
        o_ref[...] = (dact * 2.0 * jnp.maximum(u_ref[...].astype(F32), 0.0)).astype(BF16)

    blk = pl.BlockSpec((tm, tk), lambda j, i: (i, j))
    return pl.pallas_call(
        body, out_shape=jax.ShapeDtypeStruct((t, DFF), BF16), grid=(DFF // tk, t // tm),
        in_specs=[pl.BlockSpec((tm, D), lambda j, i: (i, 0)), pl.BlockSpec((tk, D), lambda j, i: (j, 0)), blk],
        out_specs=blk, name="d_up", compiler_params=_params(("parallel", "parallel")))(dpre2, w_down, up)


def _dt_bwd(du, ddt):
    t = ddt.shape[0]
    tm = 1024

    def body(f_ref, du_in, o_ref):
        del du_in
        o_ref[:, 0:128] = f_ref[...].astype(o_ref.dtype)
        o_ref[:, 128:256] = jnp.zeros((tm, 128), o_ref.dtype)

    blk = pl.BlockSpec((tm, 128), lambda i: (i, 0))
    return pl.pallas_call(
        body, out_shape=jax.ShapeDtypeStruct(du.shape, du.dtype), grid=(t // tm,),
        in_specs=[blk, pl.BlockSpec(memory_space=pl.ANY)],
        out_specs=pl.BlockSpec((tm, 256), lambda i: (i, ODT // 256)), input_output_aliases={1: 0},
        name="dt_bwd", compiler_params=_params(("parallel",)))(ddt, du)


def _mix_out_ln1(y_ssd, y_att, u, bg_row, x, w_out, g_row, b_row):
    t = x.shape[0]
    tm = ROW_TM

    def body(ys_ref, ya_ref, g0_ref, g1_ref, b0_ref, b1_ref, x_ref, w_ref, g_ref, b_ref, mixin_ref, pre_ref, h_ref):
        g0 = _sigmoid(g0_ref[...] + b0_ref[...])
        g1 = _sigmoid(g1_ref[...] + b1_ref[...])
        mixin = (g0 * ys_ref[...] + g1 * ya_ref[...]).astype(BF16)
        mixin_ref[...] = mixin
        pre = ALPHA * x_ref[...] + jnp.dot(mixin, w_ref[...], preferred_element_type=F32)
        pre_ref[...] = pre
        h, _, _ = _ln(pre, g_ref[...], b_ref[...])
        h_ref[...] = h.astype(BF16)

    blk = pl.BlockSpec((tm, D), lambda i: (i, 0))
    row = pl.BlockSpec((1, D), lambda i: (0, 0))
    return pl.pallas_call(
        body,
        out_shape=(jax.ShapeDtypeStruct((t, D), BF16), jax.ShapeDtypeStruct((t, D), F32), jax.ShapeDtypeStruct((t, D), BF16)),
        grid=(t // tm,),
        in_specs=[blk, blk, pl.BlockSpec((tm, D), lambda i: (i, OGATE // D)), pl.BlockSpec((tm, D), lambda i: (i, OGATE // D + 1)),
                  row, pl.BlockSpec((1, D), lambda i: (0, 1)), blk, pl.BlockSpec((D, D), lambda i: (0, 0)), row, row],
        out_specs=(blk, blk, blk), name="mix_out_ln1", compiler_params=_params(("parallel",)))(
            y_ssd, y_att, u, u, bg_row, bg_row, x, w_out, g_row, b_row)


def _mlp_down_ln2_loss(act, w_down, pre1, tgt, g1_row, b1_row, g2_row, b2_row):
    t = pre1.shape[0]
    tm = ROW_TM

    def body(a_ref, w_ref, p1_ref, t_ref, g1_ref, b1_ref, g2_ref, b2_ref, dpre_ref, dpreb_ref, acc_ref):
        i = pl.program_id(0)
        f = jnp.dot(a_ref[...], w_ref[...], preferred_element_type=F32)
        h1, _, _ = _ln(p1_ref[...], g1_ref[...], b1_ref[...])
        pre2 = ALPHA * h1 + f
        h2, xhat, rstd = _ln(pre2, g2_ref[...], b2_ref[...])
        err = h2 - t_ref[...]
        dh = err * (1.0 / D)
        dpre = _ln_back(dh, xhat, rstd, g2_ref[...])
        dpre_ref[...] = dpre
        dpreb_ref[...] = dpre.astype(BF16)
        loss = jnp.sum(jnp.sum(err * err, axis=1, keepdims=True), axis=0, keepdims=True) * (0.5 / D)
        part = jnp.concatenate([jnp.sum(dh * xhat, axis=0, keepdims=True), jnp.sum(dh, axis=0, keepdims=True),
                                jnp.broadcast_to(loss, (1, D)), jnp.zeros((5, D), F32)], axis=0)

        @pl.when(i == 0)
        def _():
            acc_ref[...] = part

        @pl.when(i > 0)
        def _():
            acc_ref[...] += part

    blk = pl.BlockSpec((tm, D), lambda i: (i, 0))
    row = pl.BlockSpec((1, D), lambda i: (0, 0))
    return pl.pallas_call(
        body,
        out_shape=(jax.ShapeDtypeStruct((t, D), F32), jax.ShapeDtypeStruct((t, D), BF16), jax.ShapeDtypeStruct((8, D), F32)),
        grid=(t // tm,),
        in_specs=[pl.BlockSpec((tm, DFF), lambda i: (i, 0)), pl.BlockSpec((DFF, D), lambda i: (0, 0)), blk, blk, row, row, row, row],
        out_specs=(blk, blk, pl.BlockSpec((8, D), lambda i: (0, 0))),
        name="mlp_down_ln2_loss", compiler_params=_params(("arbitrary",)))(act, w_down, pre1, tgt, g1_row, b1_row, g2_row, b2_row)


def _d_h1_ln1_bwd(dup, w_up, dpre2, pre1, g_row, b_row):
    t = dup.shape[0]
    tm = ROW_TM
    nsh = w_up.shape[0]

    def body(a_ref, w_ref, add_ref, pre_ref, g_ref, b_ref, dpre_ref, acc_ref):
        i = pl.program_id(0)
        dh_ = ALPHA * add_ref[...]
        for sh in range(nsh):
            dh_ = dh_ + _dot_nt(a_ref[:, D * sh:D * (sh + 1)], w_ref[sh])
        _, xhat, rstd = _ln(pre_ref[...], g_ref[...], b_ref[...])
        dpre_ref[...] = _ln_back(dh_, xhat, rstd, g_ref[...])
        rows = jnp.concatenate([jnp.sum(dh_ * xhat, axis=0, keepdims=True), jnp.sum(dh_, axis=0, keepdims=True),
                                jnp.zeros((6, D), F32)], axis=0)

        @pl.when(i == 0)
        def _():
            acc_ref[...] = rows

        @pl.when(i > 0)
        def _():
            acc_ref[...] += rows

    blk = pl.BlockSpec((tm, D), lambda i: (i, 0))
    row = pl.BlockSpec((1, D), lambda i: (0, 0))
    return pl.pallas_call(
        body, out_shape=(jax.ShapeDtypeStruct((t, D), F32), jax.ShapeDtypeStruct((8, D), F32)),
        grid=(t // tm,),
        in_specs=[pl.BlockSpec((tm, nsh * D), lambda i: (i, 0)), pl.BlockSpec(w_up.shape, lambda i: (0, 0, 0)),
                  blk, blk, row, row],
        out_specs=(blk, pl.BlockSpec((8, D), lambda i: (0, 0))),
        name="d_h1_ln1_bwd", compiler_params=_params(("arbitrary",)))(dup, w_up, dpre2, pre1, g_row, b_row)


def _d_mixin_mix_bwd(dpre1, w_out, y_ssd, y_att, u, bg_row):
    t = y_ssd.shape[0]
    tm = ROW_TM

    def body(a_ref, w_ref, ys_ref, ya_ref, g0_ref, g1_ref, b0_ref, b1_ref, dys_ref, dya_ref, du_ref, db_ref):
        i = pl.program_id(0)
        dm = _dot_nt(a_ref[...].astype(BF16), w_ref[...])
        g0 = _sigmoid(g0_ref[...] + b0_ref[...])
        g1 = _sigmoid(g1_ref[...] + b1_ref[...])
        dys_ref[...] = (dm * g0).astype(BF16)
        dya_ref[...] = (dm * g1).astype(BF16)
        dl0 = dm * ys_ref[...] * g0 * (1.0 - g0)
        dl1 = dm * ya_ref[...] * g1 * (1.0 - g1)
        du_ref[:, 0:D] = dl0.astype(BF16)
        du_ref[:, D:2 * D] = dl1.astype(BF16)
        part = jnp.concatenate([jnp.broadcast_to(jnp.sum(dl0, axis=0, keepdims=True), (8, D)),
                                jnp.broadcast_to(jnp.sum(dl1, axis=0, keepdims=True), (8, D))], axis=1)

        @pl.when(i == 0)
        def _():
            db_ref[...] = part

        @pl.when(i > 0)
        def _():
            db_ref[...] += part

    blk = pl.BlockSpec((tm, D), lambda i: (i, 0))
    return pl.pallas_call(
        body,
        out_shape=(jax.ShapeDtypeStruct((t, D), BF16), jax.ShapeDtypeStruct((t, D), BF16),
                   jax.ShapeDtypeStruct((t, UW), BF16), jax.ShapeDtypeStruct((8, 2 * D), F32)),
        grid=(t // tm,),
        in_specs=[blk, pl.BlockSpec((D, D), lambda i: (0, 0)), blk, blk,
                  pl.BlockSpec((tm, D), lambda i: (i, OGATE // D)), pl.BlockSpec((tm, D), lambda i: (i, OGATE // D + 1)),
                  pl.BlockSpec((1, D), lambda i: (0, 0)), pl.BlockSpec((1, D), lambda i: (0, 1))],
        out_specs=(blk, blk, pl.BlockSpec((tm, 2 * D), lambda i: (i, OGATE // (2 * D))),
                   pl.BlockSpec((8, 2 * D), lambda i: (0, 0))),
        name="d_mixin_mix_bwd", compiler_params=_params(("arbitrary",)))(dpre1, w_out, y_ssd, y_att, u, u, bg_row, bg_row)


def _adamw(w, g, m, v, name):
    r, c = w.shape
    tr = r
    for cand in (256, 128, 64, 32, 16, 8):
        if r % cand == 0 and cand * c * 4 <= 2 ** 21:
            tr = cand
            break
    bc1 = 1.0 / (1.0 - ADAM_B1 ** ADAM_STEP)
    bc2 = 1.0 / (1.0 - ADAM_B2 ** ADAM_STEP)

    def body(w_ref, g_ref, m_ref, v_ref, d_ref, nm_ref, nv_ref):
        gg = g_ref[...]
        nm = ADAM_B1 * m_ref[...] + (1.0 - ADAM_B1) * gg
        nv = ADAM_B2 * v_ref[...] + (1.0 - ADAM_B2) * (gg * gg)
        nm_ref[...] = nm
        nv_ref[...] = nv
        d_ref[...] = -ADAM_LR * ((nm * bc1) / (jnp.sqrt(nv * bc2) + ADAM_EPS) + ADAM_WD * w_ref[...])

    blk = pl.BlockSpec((tr, c), lambda i: (i, 0))
    shp = jax.ShapeDtypeStruct((r, c), F32)
    return pl.pallas_call(body, out_shape=(shp, shp, shp), grid=(r // tr,), in_specs=[blk] * 4, out_specs=(blk,) * 3,
                          name=name, compiler_params=_params(("parallel",)))(w, g, m, v)


def _perm_cols(w):
    z, xbc, dt = w[:, 0:2048], w[:, 2048:5120], w[:, 5120:5184]
    q, k, v, gate = w[:, 5184:5952], w[:, 5952:6720], w[:, 6720:7488], w[:, 7488:9536]
    kv = []
    for g in range(3):
        for p in range(2):
            lo = 256 * g + 128 * p
            kv += [k[:, lo:lo + 128], v[:, lo:lo + 128]]
    pad = jnp.zeros((w.shape[0], UW - IN_COLS), w.dtype)
    return jnp.concatenate([z, gate, xbc] + kv + [q, dt, pad], axis=1)


def _unperm_cols(wp):
    z, gate, xbc = wp[:, OZ:OZ + 2048], wp[:, OGATE:OGATE + 2048], wp[:, OXBC:OXBC + CONVD]
    q, dt = wp[:, OQ:OQ + 768], wp[:, ODT:ODT + 64]
    ks, vs = [], []
    for g in range(3):
        for p in range(2):
            lo = OKV + 128 * (4 * g + 2 * p)
            ks.append(wp[:, lo:lo + 128])
            vs.append(wp[:, lo + 128:lo + 256])
    return jnp.concatenate([z, xbc, dt, q] + ks + vs + [gate], axis=1)


def _segments():
    segs = [(0, 2048), (7488, 9536), (2048, 5120)]
    for g in range(3):
        for p in range(2):
            lo = 256 * g + 128 * p
            segs += [(5952 + lo, 5952 + lo + 128), (6720 + lo, 6720 + lo + 128)]
    segs += [(5184, 5952), (5120, 5184)]
    out, pos = [], 0
    for a, b in segs:
        out.append((a, b, pos))
        pos += b - a
    return out


SHARD_COLS = IN_COLS // 4


def _perm_from_shards(w_shards):
    pieces = []
    for a, b, _ in _segments():
        while a < b:
            s = a // SHARD_COLS
            e = min(b, (s + 1) * SHARD_COLS)
            pieces.append(w_shards[s][:, a - s * SHARD_COLS:e - s * SHARD_COLS])
            a = e
    pieces.append(jnp.zeros((w_shards.shape[1], UW - IN_COLS), w_shards.dtype))
    return jnp.concatenate(pieces, axis=1)


def _shards_from_perm(wp):
    segs = sorted(_segments())
    shards = []
    for s in range(4):
        lo, hi = s * SHARD_COLS, (s + 1) * SHARD_COLS
        pieces = []
        for a, b, pos in segs:
            x, y = max(a, lo), min(b, hi)
            if x < y:
                pieces.append(wp[:, pos + x - a:pos + y - a])
        shards.append(jnp.concatenate(pieces, axis=1))
    return jnp.stack(shards)


def _lanes128(*vecs):
    v = jnp.concatenate([a.reshape(-1) for a in vecs])
    return jnp.pad(v, (0, 128 - v.shape[0])).reshape(1, 128)


EARLY = ("w_proj_ssd", "w_proj_attn", "w_out", "w_up", "w_down")


def _local_grads(x, tgt, wts, sm, rs_idx=None):
    row = lambda a: a.reshape(1, -1)
    bg_row, cb_row = row(sm["b_gate"]), row(sm["conv_b"])
    par = jnp.concatenate([_lanes128(sm["dt_bias_f"], sm["dt_bias_b"]), _lanes128(sm["a_log_f"], sm["a_log_b"]),
                           jnp.zeros((6, 128), F32)], axis=0)
    dsk_row = row(jnp.repeat(sm["d_skip"], HP))
    nw_row = row(sm["ssd_norm_w"])
    g1, b1, g2, b2 = row(sm["ln1_g"]), row(sm["ln1_b"]), row(sm["ln2_g"]), row(sm["ln2_b"])

    xb = x.astype(BF16)
    u = _mm_nn(xb, wts["w_in_p"], tm=512, tn=2432, name="in_proj")
    xbc = _conv_fwd(u, sm["conv_w"], cb_row)
    y_f, st_f = _ssd_fwd(xbc, u, par, rev=False)
    y_fb, st_b = _ssd_fwd(xbc, u, par, y_f, rev=True)
    s_out = _gatenorm_fwd(y_fb, xbc, u, dsk_row, nw_row)
    y_ssd = _mm_nn(s_out, wts["w_proj_ssd"], tm=512, tn=1024, name="proj_ssd")
    att_o, att_l = [], []
    for g in range(3):
        o, l = _attn_fwd(u, g)
        att_o.append(o)
        att_l.append(l)
    att, y_att = _combine_proj(att_o, att_l, wts["w_proj_attn"])
    mixin, pre1, h1 = _mix_out_ln1(y_ssd, y_att, u, bg_row, x, wts["w_out"], g1, b1)
    up, act = _mlp_up(h1, wts["w_up"])
    dpre2, dpre2_b, acc2 = _mlp_down_ln2_loss(act, wts["w_down"], pre1, tgt, g1, b1, g2, b2)

    dw_down = _mm_tn(act, dpre2_b, tka=1024, tn=1024, tt=1024, name="dw_down")
    dup = _d_up(dpre2_b, wts["w_down"], up)
    dw_up = _mm_tn(h1, dup, tka=1024, tn=1024, tt=1024, name="dw_up", out_shards=4)
    dpre1, acc1 = _d_h1_ln1_bwd(dup, wts["w_up"], dpre2, pre1, g1, b1)
    dw_out = _mm_tn(mixin, dpre1, tka=1024, tn=1024, tt=1024, name="dw_out")
    dy_ssd, dy_att, du, dbg = _d_mixin_mix_bwd(dpre1, wts["w_out"], y_ssd, y_att, u, bg_row)
    dw_proj_ssd = _mm_tn(s_out, dy_ssd, tka=1024, tn=1024, tt=1024, name="dw_proj_ssd")
    ds_out = _mm_nt(dy_ssd, wts["w_proj_ssd"], tm=512, tk=1024, tc=1024, name="d_s_out")
    dw_proj_attn = _mm_tn(att, dy_att, tka=256, tn=256, tt=1024, name="dw_proj_attn", out_shards=4)
    do_g, e_g = _d_att_combine_bwd(dy_att, wts["w_proj_attn"], att_o, att_l)
    for g in range(3):
        du = _attn_dq(u, du, do_g[g], att_l[g], e_g[g], g)
        du = _attn_dkv(u, du, do_g[g], att_l[g], e_g[g], g)
    big = {
        "w_proj_ssd": dw_proj_ssd.reshape(4, DI // 4, D),
        "w_proj_attn": dw_proj_attn,
        "w_out": dw_out.reshape(4, D // 4, D),
        "w_up": dw_up,
        "w_down": dw_down.reshape(4, DFF // 4, D),
    }
    early = [big[n] for n in EARLY]
    dy, du, dnw, dds, recv = _gatenorm_bwd(ds_out, y_fb, xbc, u, du, dsk_row, nw_row,
                                           side=_swap_side(early) if rs_idx else None)
    if rs_idx:
        halves = [_add_half(g, r, rs_idx[0], f"rs_add_half_{n}") for g, r, n in zip(early, recv, EARLY)]
    dxs_f, dbc_f, ddt_f, sacc_f, recv = _ssd_bwd(xbc, u, par, dy, st_f, rev=False,
                                                 side=_step1_side([h[1] for h in halves]) if rs_idx else None)
    if rs_idx:
        k = len(EARLY)
        sums1 = [_rs_add1(h[0], ra, rb, rs_idx[1], f"rs_add1_{n}")
                 for h, ra, rb, n in zip(halves, recv[:k], recv[k:], EARLY)]
    dxs, dbc, ddt, sacc_b, recv = _ssd_bwd(
        xbc, u, par, dy, st_b, rev=True, add=(dxs_f, dbc_f, ddt_f),
        side=_step2_side([s1[2] for s1 in sums1], [s1[3] for s1 in sums1]) if rs_idx else None)
    pieces = None
    if rs_idx:
        pieces = {n: _rs_add2(s1[0], s1[1], ra, rb, rs_idx[1], f"rs_add2_{n}")
                  for s1, ra, rb, n in zip(sums1, recv[:k], recv[k:], EARLY)}
    dpre_c, dcw, dcb = _conv_dpre(u, dxs, dy, dbc, dsk_row, sm["conv_w"], cb_row)
    du = _conv_dx(du, dpre_c, sm["conv_w"])
    du = _dt_bwd(du, ddt)
    dw_in_p = _mm_tn(xb, du, tka=1024, tn=2432, tt=1024, name="dw_in")
    dx = _mm_nt(du, wts["w_in_p"], tm=1024, tk=1024, tc=2432, name="d_x", add=dpre1, add_scale=ALPHA)

    sacc = sacc_f + sacc_b
    small = {
        "b_gate": dbg[0], "conv_w": dcw[0:KCONV], "conv_b": dcb[0],
        "dt_bias_f": sacc[0, 0:32], "dt_bias_b": sacc[0, 32:64], "a_log_f": sacc[1, 0:32], "a_log_b": sacc[1, 32:64],
        "d_skip": dds[0, 0:32], "ssd_norm_w": dnw[0],
        "ln1_g": acc1[0], "ln1_b": acc1[1], "ln2_g": acc2[0], "ln2_b": acc2[1], "loss": acc2[2, 0:1],
    }
    big["w_in"] = _shards_from_perm(dw_in_p)
    return dx, big, small, pieces


HBM_SPEC = pl.BlockSpec(memory_space=pl.ANY)


def _place():
    x, y, c = lax.axis_index("x"), lax.axis_index("y"), lax.axis_index("c")
    chips = [(1 - x, y), (x, 1 - y), (1 - x, 1 - y)]
    return x, y, c, chips


def _allgather_weights(shards):
    n = len(shards)

    def body(*refs):
        ins, outs = refs[:n], refs[n:2 * n]
        send_sems, recv_sems = refs[2 * n:]
        x, y, c, _ = _place()
        q, q_x, q_y, q_d = 2 * x + y, 2 * (1 - x) + y, 2 * x + 1 - y, 2 * (1 - x) + 1 - y
        x_nbr, y_nbr, sibling = (1 - x, y, c), (x, 1 - y, c), (x, y, 1 - c)

        def copy(w, k, src, dst, to):
            return pltpu.make_async_remote_copy(src_ref=src, dst_ref=dst, send_sem=send_sems.at[w, k],
                                                recv_sem=recv_sems.at[w, k], device_id=to, device_id_type=MESH)

        def rows(w, core, part):
            rh = ins[w].shape[0] // 2
            if part is None:
                return pl.ds(core * rh, rh)
            return pl.ds(core * rh + part * (rh // 2), rh // 2)

        def same(w, k, slot, core, part, to):
            blk = outs[w].at[slot, rows(w, core, part), :]
            return copy(w, k, blk, blk, to)

        started = []
        for w in range(n):
            cp = copy(w, 8, ins[w], outs[w].at[q], sibling)
            cp.start()
            started.append(cp)
            mine = rows(w, c, None)
            for k, to in ((0, x_nbr), (1, y_nbr)):
                cp = copy(w, k, ins[w].at[mine, :], outs[w].at[q, mine, :], to)
                cp.start()
                started.append(cp)
        for w in range(n):
            same(w, 0, q_x, c, None, x_nbr).wait_recv()
            for cp in (same(w, 2, q_x, c, 0, y_nbr), same(w, 4, q_x, c, None, sibling)):
                cp.start()
                started.append(cp)
            same(w, 1, q_y, c, None, y_nbr).wait_recv()
            for cp in (same(w, 3, q_y, c, 1, x_nbr), same(w, 5, q_y, c, None, sibling)):
                cp.start()
                started.append(cp)
        for w in range(n):
            same(w, 2, q_d, c, 0, y_nbr).wait_recv()
            cp = same(w, 6, q_d, c, 0, sibling)
            cp.start()
            started.append(cp)
            same(w, 3, q_d, c, 1, x_nbr).wait_recv()
            cp = same(w, 7, q_d, c, 1, sibling)
            cp.start()
            started.append(cp)
        for w in range(n):
            same(w, 4, q_x, 1 - c, None, sibling).wait_recv()
            same(w, 5, q_y, 1 - c, None, sibling).wait_recv()
            same(w, 6, q_d, 1 - c, 0, sibling).wait_recv()
            same(w, 7, q_d, 1 - c, 1, sibling).wait_recv()
            copy(w, 8, ins[w], outs[w].at[q], sibling).wait_recv()
        for cp in started:
            cp.wait_send()

    return pl.pallas_call(
        body, out_shape=[jax.ShapeDtypeStruct((4,) + s.shape, s.dtype) for s in shards],
        in_specs=[HBM_SPEC] * n, out_specs=[HBM_SPEC] * n,
        scratch_shapes=[pltpu.SemaphoreType.DMA((n, 9)), pltpu.SemaphoreType.DMA((n, 9))],
        name="allgather_weights")(*shards)


def _swap_halves(grads):
    n = len(grads)

    def body(*refs):
        ins, outs = refs[:n], refs[n:2 * n]
        send_sems, recv_sems = refs[2 * n:]
        x, y, c, _ = _place()
        copies = []
        for w in range(n):
            rh = ins[w].shape[1] // 2
            for p in range(4):
                cp = pltpu.make_async_remote_copy(
                    src_ref=ins[w].at[p, pl.ds((1 - c) * rh, rh), :], dst_ref=outs[w].at[p],
                    send_sem=send_sems.at[w, p], recv_sem=recv_sems.at[w, p],
                    device_id=(x, y, 1 - c), device_id_type=MESH)
                cp.start()
                copies.append(cp)
        for cp in copies:
            cp.wait()

    return pl.pallas_call(
        body, out_shape=[jax.ShapeDtypeStruct((4, g.shape[1] // 2, g.shape[2]), F32) for g in grads],
        in_specs=[HBM_SPEC] * n, out_specs=[HBM_SPEC] * n,
        scratch_shapes=[pltpu.SemaphoreType.DMA((n, 4)), pltpu.SemaphoreType.DMA((n, 4))],
        name="rs_swap_halves")(*grads)


def _rs_step1(parts):
    n = len(parts)

    def body(*refs):
        ins, out_a, out_b = refs[:n], refs[n:2 * n], refs[2 * n:3 * n]
        send_sems, recv_sems = refs[3 * n:]
        x, y, c, _ = _place()
        copies = []
        for w in range(n):
            rq = ins[w].shape[1] // 2
            for i in range(2):
                copies.append(pltpu.make_async_remote_copy(
                    src_ref=ins[w].at[2 * (1 - x) + i, pl.ds(0, rq), :], dst_ref=out_a[w].at[i],
                    send_sem=send_sems.at[w, i], recv_sem=recv_sems.at[w, i],
                    device_id=(1 - x, y, c), device_id_type=MESH))
                copies.append(pltpu.make_async_remote_copy(
                    src_ref=ins[w].at[2 * i + 1 - y, pl.ds(rq, rq), :], dst_ref=out_b[w].at[i],
                    send_sem=send_sems.at[w, 2 + i], recv_sem=recv_sems.at[w, 2 + i],
                    device_id=(x, 1 - y, c), device_id_type=MESH))
        for cp in copies:
            cp.start()
        for cp in copies:
            cp.wait()

    quarter = lambda p: jax.ShapeDtypeStruct((2, p.shape[1] // 2, p.shape[2]), p.dtype)
    outs = pl.pallas_call(
        body, out_shape=[quarter(p) for p in parts] * 2,
        in_specs=[HBM_SPEC] * n, out_specs=[HBM_SPEC] * (2 * n),
        scratch_shapes=[pltpu.SemaphoreType.DMA((n, 4)), pltpu.SemaphoreType.DMA((n, 4))],
        name="rs_step1")(*parts)
    return outs[:n], outs[n:]


def _rs_step2(tas, tbs):
    n = len(tas)

    def body(*refs):
        in_a, in_b, out_a, out_b = refs[:n], refs[n:2 * n], refs[2 * n:3 * n], refs[3 * n:4 * n]
        send_sems, recv_sems = refs[4 * n:]
        x, y, c, _ = _place()
        copies = []
        for w in range(n):
            copies.append(pltpu.make_async_remote_copy(
                src_ref=in_a[w].at[1 - y], dst_ref=out_a[w], send_sem=send_sems.at[w, 0], recv_sem=recv_sems.at[w, 0],
                device_id=(x, 1 - y, c), device_id_type=MESH))
            copies.append(pltpu.make_async_remote_copy(
                src_ref=in_b[w].at[1 - x], dst_ref=out_b[w], send_sem=send_sems.at[w, 1], recv_sem=recv_sems.at[w, 1],
                device_id=(1 - x, y, c), device_id_type=MESH))
        for cp in copies:
            cp.start()
        for cp in copies:
            cp.wait()

    one = lambda p: jax.ShapeDtypeStruct(p.shape[1:], p.dtype)
    outs = pl.pallas_call(
        body, out_shape=[one(p) for p in tas] + [one(p) for p in tbs],
        in_specs=[HBM_SPEC] * (2 * n), out_specs=[HBM_SPEC] * (2 * n),
        scratch_shapes=[pltpu.SemaphoreType.DMA((n, 2)), pltpu.SemaphoreType.DMA((n, 2))],
        name="rs_step2")(*tas, *tbs)
    return outs[:n], outs[n:]


class _Side(NamedTuple):
    ins: tuple
    out_shapes: tuple
    nsem: tuple
    make: Callable


def _swap_copies(ins, outs, send_sems, recv_sems):
    x, y, c, _ = _place()
    copies = []
    for w in range(len(ins)):
        rh = ins[w].shape[1] // 2
        for p in range(4):
            copies.append(pltpu.make_async_remote_copy(
                src_ref=ins[w].at[p, pl.ds((1 - c) * rh, rh), :], dst_ref=outs[w].at[p],
                send_sem=send_sems.at[w, p], recv_sem=recv_sems.at[w, p],
                device_id=(x, y, 1 - c), device_id_type=MESH))
    return copies


def _swap_side(grads):
    shapes = tuple(jax.ShapeDtypeStruct((4, g.shape[1] // 2, g.shape[2]), F32) for g in grads)
    return _Side(tuple(grads), shapes, (len(grads), 4), _swap_copies)


def _step1_copies(ins, outs, send_sems, recv_sems):
    n = len(ins)
    out_a, out_b = outs[:n], outs[n:]
    x, y, c, _ = _place()
    copies = []
    for w in range(n):
        rq = ins[w].shape[1] // 2
        for i in range(2):
            copies.append(pltpu.make_async_remote_copy(
                src_ref=ins[w].at[2 * (1 - x) + i, pl.ds(0, rq), :], dst_ref=out_a[w].at[i],
                send_sem=send_sems.at[w, i], recv_sem=recv_sems.at[w, i],
                device_id=(1 - x, y, c), device_id_type=MESH))
            copies.append(pltpu.make_async_remote_copy(
                src_ref=ins[w].at[2 * i + 1 - y, pl.ds(rq, rq), :], dst_ref=out_b[w].at[i],
                send_sem=send_sems.at[w, 2 + i], recv_sem=recv_sems.at[w, 2 + i],
                device_id=(x, 1 - y, c), device_id_type=MESH))
    return copies


def _step1_side(parts):
    quarter = tuple(jax.ShapeDtypeStruct((2, p.shape[1] // 2, p.shape[2]), p.dtype) for p in parts)
    return _Side(tuple(parts), quarter + quarter, (len(parts), 4), _step1_copies)


def _step2_copies(ins, outs, send_sems, recv_sems):
    n = len(ins) // 2
    in_a, in_b, out_a, out_b = ins[:n], ins[n:], outs[:n], outs[n:]
    x, y, c, _ = _place()
    copies = []
    for w in range(n):
        copies.append(pltpu.make_async_remote_copy(
            src_ref=in_a[w].at[1 - y], dst_ref=out_a[w], send_sem=send_sems.at[w, 0], recv_sem=recv_sems.at[w, 0],
            device_id=(x, 1 - y, c), device_id_type=MESH))
        copies.append(pltpu.make_async_remote_copy(
            src_ref=in_b[w].at[1 - x], dst_ref=out_b[w], send_sem=send_sems.at[w, 1], recv_sem=recv_sems.at[w, 1],
            device_id=(1 - x, y, c), device_id_type=MESH))
    return copies


def _step2_side(tas, tbs):
    one = tuple(jax.ShapeDtypeStruct(p.shape[1:], p.dtype) for p in tuple(tas) + tuple(tbs))
    return _Side(tuple(tas) + tuple(tbs), one, (len(tas), 2), _step2_copies)


def _run_side(side, name):
    n_in, n_out = len(side.ins), len(side.out_shapes)

    def body(*refs):
        copies = side.make(refs[:n_in], refs[n_in:n_in + n_out], *refs[n_in + n_out:])
        for cp in copies:
            cp.start()
        for cp in copies:
            cp.wait()

    return pl.pallas_call(
        body, out_shape=list(side.out_shapes), in_specs=[HBM_SPEC] * n_in, out_specs=[HBM_SPEC] * n_out,
        scratch_shapes=[pltpu.SemaphoreType.DMA(side.nsem), pltpu.SemaphoreType.DMA(side.nsem)], name=name)(*side.ins)


def _host_call(body, side, n_steps, *, out_shape, in_specs, out_specs, scratch_shapes, args, aliases, name, sem):
    n_in, n_out, n_scr = len(in_specs), len(out_shape), len(scratch_shapes)
    if side is None:
        outs = pl.pallas_call(body, out_shape=tuple(out_shape), grid=(n_steps,), in_specs=list(in_specs),
                              out_specs=tuple(out_specs), scratch_shapes=list(scratch_shapes),
                              input_output_aliases=aliases, name=name, compiler_params=_params(sem))(*args)
        return tuple(outs), ()
    ns_in, ns_out = len(side.ins), len(side.out_shapes)

    def wrapped(*refs):
        h_in, s_in = refs[:n_in], refs[n_in:n_in + ns_in]
        o0 = n_in + ns_in
        h_out, s_out = refs[o0:o0 + n_out], refs[o0 + n_out:o0 + n_out + ns_out]
        c0 = o0 + n_out + ns_out
        h_scr, sems = refs[c0:c0 + n_scr], refs[c0 + n_scr:]
        step = pl.program_id(0)

        @pl.when(step == 0)
        def _():
            for cp in side.make(s_in, s_out, *sems):
                cp.start()

        body(*h_in, *h_out, *h_scr)

        @pl.when(step == n_steps - 1)
        def _():
            for cp in side.make(s_in, s_out, *sems):
                cp.wait()

    outs = pl.pallas_call(
        wrapped, out_shape=tuple(out_shape) + tuple(side.out_shapes), grid=(n_steps,),
        in_specs=list(in_specs) + [HBM_SPEC] * ns_in, out_specs=tuple(out_specs) + (HBM_SPEC,) * ns_out,
        scratch_shapes=list(scratch_shapes) + [pltpu.SemaphoreType.DMA(side.nsem), pltpu.SemaphoreType.DMA(side.nsem)],
        input_output_aliases=aliases, name=name, compiler_params=_params(sem))(*args, *side.ins)
    return tuple(outs[:n_out]), tuple(outs[n_out:])


def _join_halves(pieces):
    n = len(pieces)

    def body(*refs):
        outs = refs[n:2 * n]
        send_sems, recv_sems = refs[2 * n:]
        x, y, c, _ = _place()

        def copy(w, slot):
            return pltpu.make_async_remote_copy(
                src_ref=outs[w].at[slot], dst_ref=outs[w].at[slot], send_sem=send_sems.at[w], recv_sem=recv_sems.at[w],
                device_id=(x, y, 1 - c), device_id_type=MESH)

        for w in range(n):
            copy(w, c).start()
        for w in range(n):
            copy(w, 1 - c).wait_recv()
            copy(w, c).wait_send()

    return pl.pallas_call(
        body, out_shape=[jax.ShapeDtypeStruct(p.shape, F32) for p in pieces],
        in_specs=[HBM_SPEC] * n, out_specs=[HBM_SPEC] * n, input_output_aliases={w: w for w in range(n)},
        scratch_shapes=[pltpu.SemaphoreType.DMA((n,)), pltpu.SemaphoreType.DMA((n,))],
        name="rs_join_halves")(*pieces)


def _add_tile_rows(rh, c):
    for cand in (512, 256, 128, 64, 32, 16, 8):
        if rh % cand == 0 and cand * c * 4 <= 2 ** 21:
            return cand
    return rh


def _add_half(grad, recv, c_idx, name):
    _, r, cc = grad.shape
    rh = r // 2
    tr = _add_tile_rows(rh, cc)
    nb = rh // tr

    def body(c_ref, g_ref, r_ref, o_ref, ob_ref):
        del c_ref
        s = g_ref[...] + r_ref[...]
        o_ref[...] = s
        ob_ref[...] = s.astype(BF16)

    blk = pl.BlockSpec((None, tr, cc), lambda p, i, c_ref: (p, i, 0))
    grid_spec = pltpu.PrefetchScalarGridSpec(
        num_scalar_prefetch=1, grid=(4, nb),
        in_specs=[pl.BlockSpec((None, tr, cc), lambda p, i, c_ref: (p, c_ref[0] * nb + i, 0)), blk],
        out_specs=(blk, blk))
    return pl.pallas_call(
        body, out_shape=(jax.ShapeDtypeStruct((4, rh, cc), F32), jax.ShapeDtypeStruct((4, rh, cc), BF16)),
        grid_spec=grid_spec, name=name, compiler_params=_params(("parallel", "parallel")))(c_idx, grad, recv)


def _rs_add1(part, recv_a, recv_b, xy_idx, name):
    _, rh, cc = part.shape
    rq = rh // 2
    tr = _add_tile_rows(rq, cc)
    nb = rq // tr

    def body(xy_ref, pa_ref, pb_ref, ra_ref, rb_ref, ta_ref, tb_ref, tab_ref, tbb_ref):
        del xy_ref
        ta = pa_ref[...] + ra_ref[...].astype(F32)
        tb = pb_ref[...] + rb_ref[...].astype(F32)
        ta_ref[...] = ta
        tb_ref[...] = tb
        tab_ref[...] = ta.astype(BF16)
        tbb_ref[...] = tb.astype(BF16)

    blk = pl.BlockSpec((None, tr, cc), lambda i, j, xy: (i, j, 0))
    grid_spec = pltpu.PrefetchScalarGridSpec(
        num_scalar_prefetch=1, grid=(2, nb),
        in_specs=[pl.BlockSpec((None, tr, cc), lambda i, j, xy: (2 * xy[0] + i, j, 0)),
                  pl.BlockSpec((None, tr, cc), lambda i, j, xy: (2 * i + xy[1], nb + j, 0)), blk, blk],
        out_specs=(blk, blk, blk, blk))
    f32s, b16s = jax.ShapeDtypeStruct((2, rq, cc), F32), jax.ShapeDtypeStruct((2, rq, cc), BF16)
    return pl.pallas_call(body, out_shape=(f32s, f32s, b16s, b16s), grid_spec=grid_spec, name=name,
                          compiler_params=_params(("parallel", "parallel")))(xy_idx, part, part, recv_a, recv_b)


def _rs_add2(ta, tb, recv_a, recv_b, xy_idx, name):
    _, rq, cc = ta.shape
    tr = _add_tile_rows(rq, cc)
    nb = rq // tr

    def body(xy_ref, ta_ref, tb_ref, ra_ref, rb_ref, o_ref):
        del xy_ref
        s = pl.program_id(0)
        fa = ta_ref[...] + ra_ref[...].astype(F32)
        fb = tb_ref[...] + rb_ref[...].astype(F32)
        o_ref[...] = jnp.where(s == 0, fa, fb)

    rblk = pl.BlockSpec((tr, cc), lambda s, j, xy: (j, 0))
    grid_spec = pltpu.PrefetchScalarGridSpec(
        num_scalar_prefetch=1, grid=(2, nb),
        in_specs=[pl.BlockSpec((None, tr, cc), lambda s, j, xy: (xy[1], j, 0)),
                  pl.BlockSpec((None, tr, cc), lambda s, j, xy: (xy[0], j, 0)), rblk, rblk],
        out_specs=pl.BlockSpec((None, tr, cc), lambda s, j, xy: (xy[2], s * nb + j, 0)))
    return pl.pallas_call(body, out_shape=jax.ShapeDtypeStruct((2, 2 * rq, cc), F32), grid_spec=grid_spec, name=name,
                          compiler_params=_params(("parallel", "parallel")))(xy_idx, ta, tb, recv_a, recv_b)


def _allreduce_small(slab):
    r = slab.shape[0]

    def body(x_ref, o_ref, buf, send_sems, recv_sems):
        x, y, c, _ = _place()
        me = 4 * x + 2 * y + c
        buf[me] = x_ref[...]
        peers = []
        for k in range(1, 8):
            kx, ky, kc = (k >> 2) & 1, (k >> 1) & 1, k & 1
            peers.append((x + kx - 2 * x * kx, y + ky - 2 * y * ky, c + kc - 2 * c * kc))

        def copy(k, slot):
            return pltpu.make_async_remote_copy(src_ref=x_ref, dst_ref=buf.at[slot], send_sem=send_sems.at[k],
                                                recv_sem=recv_sems.at[k], device_id=peers[k], device_id_type=MESH)

        for k in range(7):
            copy(k, me).start()
        for k, (px, py, pc) in enumerate(peers):
            copy(k, 4 * px + 2 * py + pc).wait_recv()
        for k in range(7):
            copy(k, me).wait_send()
        acc = buf[0]
        for j in range(1, 8):
            acc = acc + buf[j]
        o_ref[...] = acc

    vm = pl.BlockSpec(memory_space=pltpu.VMEM)
    return pl.pallas_call(
        body, out_shape=jax.ShapeDtypeStruct((r, 128), F32), in_specs=[vm], out_specs=vm,
        scratch_shapes=[pltpu.VMEM((8, r, 128), F32), pltpu.SemaphoreType.DMA((7,)), pltpu.SemaphoreType.DMA((7,))],
        name="allreduce_small")(slab)


def _pack(arrs):
    rows = []
    for a in arrs:
        v = a.reshape(-1)
        v = jnp.pad(v, (0, (-v.shape[0]) % 128))
        rows.append(v.reshape(-1, 128))
    slab = jnp.concatenate(rows, axis=0)
    return jnp.pad(slab, ((0, (-slab.shape[0]) % 8), (0, 0)))


def _unpack(slab, shapes):
    out, r0 = [], 0
    for shp in shapes:
        size = math.prod(shp)
        nr = -(-size // 128)
        out.append(slab[r0:r0 + nr].reshape(-1)[:size].reshape(shp))
        r0 += nr
    return out


BIG = ("w_in", "w_proj_ssd", "w_proj_attn", "w_out", "w_up", "w_down")
SMALL = ("b_gate", "conv_w", "conv_b", "dt_bias_f", "dt_bias_b", "a_log_f", "a_log_b", "d_skip", "ssd_norm_w",
         "ln1_g", "ln1_b", "ln2_g", "ln2_b")
ORDER = ("w_in", "b_gate", "conv_w", "conv_b", "dt_bias_f", "dt_bias_b", "a_log_f", "a_log_b", "d_skip", "ssd_norm_w",
         "w_proj_ssd", "w_proj_attn", "w_out", "ln1_g", "ln1_b", "w_up", "w_down", "ln2_g", "ln2_b")


def kernel(x, w_in, b_gate, conv_w, conv_b, dt_bias_f, dt_bias_b, a_log_f, a_log_b, d_skip, ssd_norm_w, w_proj_ssd, w_proj_attn, w_out, ln1_g, ln1_b, w_up, w_down, ln2_g, ln2_b, loss_target, m_w_in, m_b_gate, m_conv_w, m_conv_b, m_dt_bias_f, m_dt_bias_b, m_a_log_f, m_a_log_b, m_d_skip, m_ssd_norm_w, m_w_proj_ssd, m_w_proj_attn, m_w_out, m_ln1_g, m_ln1_b, m_w_up, m_w_down, m_ln2_g, m_ln2_b, v_w_in, v_b_gate, v_conv_w, v_conv_b, v_dt_bias_f, v_dt_bias_b, v_a_log_f, v_a_log_b, v_d_skip, v_ssd_norm_w, v_w_proj_ssd, v_w_proj_attn, v_w_out, v_ln1_g, v_ln1_b, v_w_up, v_w_down, v_ln2_g, v_ln2_b):
    w = dict(w_in=w_in, b_gate=b_gate, conv_w=conv_w, conv_b=conv_b, dt_bias_f=dt_bias_f, dt_bias_b=dt_bias_b,
             a_log_f=a_log_f, a_log_b=a_log_b, d_skip=d_skip, ssd_norm_w=ssd_norm_w, w_proj_ssd=w_proj_ssd,
             w_proj_attn=w_proj_attn, w_out=w_out, ln1_g=ln1_g, ln1_b=ln1_b, w_up=w_up, w_down=w_down, ln2_g=ln2_g, ln2_b=ln2_b)
    m = dict(w_in=m_w_in, b_gate=m_b_gate, conv_w=m_conv_w, conv_b=m_conv_b, dt_bias_f=m_dt_bias_f, dt_bias_b=m_dt_bias_b,
             a_log_f=m_a_log_f, a_log_b=m_a_log_b, d_skip=m_d_skip, ssd_norm_w=m_ssd_norm_w, w_proj_ssd=m_w_proj_ssd,
             w_proj_attn=m_w_proj_attn, w_out=m_w_out, ln1_g=m_ln1_g, ln1_b=m_ln1_b, w_up=m_w_up, w_down=m_w_down,
             ln2_g=m_ln2_g, ln2_b=m_ln2_b)
    v = dict(w_in=v_w_in, b_gate=v_b_gate, conv_w=v_conv_w, conv_b=v_conv_b, dt_bias_f=v_dt_bias_f, dt_bias_b=v_dt_bias_b,
             a_log_f=v_a_log_f, a_log_b=v_a_log_b, d_skip=v_d_skip, ssd_norm_w=v_ssd_norm_w, w_proj_ssd=v_w_proj_ssd,
             w_proj_attn=v_w_proj_attn, w_out=v_w_out, ln1_g=v_ln1_g, ln1_b=v_ln1_b, w_up=v_w_up, w_down=v_w_down,
             ln2_g=v_ln2_g, ln2_b=v_ln2_b)
    xi, yi, ci = lax.axis_index("x"), lax.axis_index("y"), lax.axis_index("c")
    shard = 2 * xi + yi

    g_in, g_ps, g_pa, g_o, g_up, g_dn = _allgather_weights([w[n].astype(BF16) for n in BIG])
    wts = {"w_in_p": _perm_from_shards(g_in),"w_proj_ssd": g_ps.reshape(DI, D), "w_proj_attn": g_pa,
           "w_out": g_o.reshape(D, D), "w_up": g_up, "w_down": g_dn.reshape(DFF, D)}

    cw_slab = jnp.zeros((KCONV, 4, CONVD // 4), F32)
    cw_slab = lax.dynamic_update_slice(cw_slab, conv_w[:, None, :] * 0.5, (0, shard, 0))
    conv_w_all = _unpack(_allreduce_small(_pack([cw_slab])), [(KCONV, CONVD)])[0]

    sm = {n: w[n] for n in SMALL}
    sm["conv_w"] = conv_w_all
    c_idx = jnp.reshape(ci, (1,)).astype(jnp.int32)
    xy_idx = jnp.stack([xi, yi, ci]).astype(jnp.int32)
    dx, big, small, pieces = _local_grads(x[0], loss_target[0], wts, sm, rs_idx=(c_idx, xy_idx))

    names = list(SMALL) + ["loss"]
    shapes = [small[n].shape for n in names]
    red = dict(zip(names, _unpack(_allreduce_small(_pack([small[n] for n in names])), shapes)))
    loss = red["loss"].reshape(())
    gsm = {n: red[n] for n in SMALL}
    conv_w_grad_shard = lax.dynamic_slice_in_dim(gsm["conv_w"].reshape(KCONV, 4, CONVD // 4), shard, 1, axis=1)
    gsm["conv_w"] = conv_w_grad_shard.reshape(KCONV, CONVD // 4)

    g = big["w_in"]
    half = _add_half(g, _run_side(_swap_side([g]), "rs_swap_halves")[0], c_idx, "rs_add_half_w_in")
    ra, rb = _run_side(_step1_side([half[1]]), "rs_step1")
    s1 = _rs_add1(half[0], ra, rb, xy_idx, "rs_add1_w_in")
    ra2, rb2 = _run_side(_step2_side([s1[2]], [s1[3]]), "rs_step2")
    pieces["w_in"] = _rs_add2(s1[0], s1[1], ra2, rb2, xy_idx, "rs_add2_w_in")
    joined = _join_halves([pieces[n] for n in BIG])
    gbig = {n: j.reshape(w[n].shape) for n, j in zip(BIG, joined)}

    grads, deltas, new_m, new_v = {}, {}, {}, {}
    for n in BIG:
        grads[n] = gbig[n]
        deltas[n], new_m[n], new_v[n] = _adamw(w[n], gbig[n], m[n], v[n], f"adamw_{n}")
    sshapes = [w[n].shape for n in SMALL]
    d_s, m_s, v_s = _adamw(_pack([w[n] for n in SMALL]), _pack([gsm[n] for n in SMALL]),
                           _pack([m[n] for n in SMALL]), _pack([v[n] for n in SMALL]), "adamw_small")
    for n, dd, mm, vv in zip(SMALL, _unpack(d_s, sshapes), _unpack(m_s, sshapes), _unpack(v_s, sshapes)):
        grads[n], deltas[n], new_m[n], new_v[n] = gsm[n], dd, mm, vv

    return (loss, dx[None], *[grads[n] for n in ORDER], *[deltas[n] for n in ORDER],
            *[new_m[n] for n in ORDER], *[new_v[n] for n in ORDER])
```

```python
import math
from typing import Callable, NamedTuple

import jax
import numpy as np
import jax.numpy as jnp
from jax import lax
from jax.experimental import pallas as pl
from jax.experimental.pallas import tpu as pltpu

F32, BF16 = jnp.float32, jnp.bfloat16
MESH = pl.DeviceIdType.MESH

D = 1024
DI = 2048
NH = 32
HP = 64
NG = 4
NS = 128
Q = 128
CONVD = 3072
KCONV = 5
DFF = 4096
AH = 64
ATT_HALF = 64
DILATIONS = (1, 4, 16)
IN_COLS = 9536
OZ, OGATE, OXBC, OKV, OQ, ODT, UW = 0, 2048, 4096, 7168, 8704, 9472, 9728
ALPHA = 2.0 ** 0.25
NORM_EPS = 1e-5
ADAM_LR, ADAM_B1, ADAM_B2, ADAM_EPS, ADAM_WD, ADAM_STEP = 0.001, 0.9, 0.999, 1e-8, 0.01, 10
VMEM_LIMIT = 56 * 2 ** 20
NEG = -1e30


def _params(sem):
    return pltpu.CompilerParams(dimension_semantics=sem, vmem_limit_bytes=VMEM_LIMIT)


def _sigmoid(x):
    return 1.0 / (1.0 + jnp.exp(-x))


def _softplus(x):
    e = jnp.exp(-jnp.abs(x))
    small = e * (1.0 - e * (0.5 - e * (1.0 / 3.0)))
    return jnp.maximum(x, 0.0) + jnp.where(e < 0.01, small, jnp.log(1.0 + e))


def _split3(a):
    hi = a.astype(BF16)
    r = a - hi.astype(F32)
    mid = r.astype(BF16)
    lo = (r - mid.astype(F32)).astype(BF16)
    return hi, mid, lo


def _dot01(a, m01):
    hi, mid, lo = _split3(a)
    d = lambda p: jnp.dot(p, m01, preferred_element_type=F32)
    return d(hi) + d(mid) + d(lo)


def _dot01_l(m01, a):
    hi, mid, lo = _split3(a)
    d = lambda p: jnp.dot(m01, p, preferred_element_type=F32)
    return d(hi) + d(mid) + d(lo)


def _dot_nt(a, b):
    return lax.dot_general(a, b, (((1,), (1,)), ((), ())), preferred_element_type=F32)


def _iota(shape, dim):
    return lax.broadcasted_iota(jnp.int32, shape, dim)


def _mm_nn(a, b, *, tm, tn, name, out_dtype=F32):
    m, k = a.shape
    if b.ndim == 3:
        assert tn == b.shape[2]
        n = b.shape[0] * b.shape[2]
        b_spec = pl.BlockSpec((None, k, tn), lambda j, i: (j, 0, 0))
    else:
        n = b.shape[1]
        b_spec = pl.BlockSpec((k, tn), lambda j, i: (0, j))

    def body(a_ref, b_ref, o_ref):
        o_ref[...] = jnp.dot(a_ref[...].astype(BF16), b_ref[...], preferred_element_type=F32).astype(out_dtype)

    return pl.pallas_call(
        body, out_shape=jax.ShapeDtypeStruct((m, n), out_dtype), grid=(n // tn, m // tm),
        in_specs=[pl.BlockSpec((tm, k), lambda j, i: (i, 0)), b_spec],
        out_specs=pl.BlockSpec((tm, tn), lambda j, i: (i, j)),
        name=name, compiler_params=_params(("parallel", "parallel")))(a, b)


def _mm_nt(a, b, *, tm, tk, tc, name, add=None, add_scale=1.0):
    m, n = a.shape
    if b.ndim == 3:
        assert tc == b.shape[2]
        k, nc = b.shape[1], b.shape[0]
        b_spec = pl.BlockSpec((None, tk, tc), lambda j, i, c: (c, j, 0))
    else:
        k, nc = b.shape[0], n // tc
        b_spec = pl.BlockSpec((tk, tc), lambda j, i, c: (j, c))

    def body(*refs):
        if add is None:
            a_ref, b_ref, o_ref = refs
        else:
            a_ref, b_ref, add_ref, o_ref = refs
        c = pl.program_id(2)
        part = _dot_nt(a_ref[...].astype(BF16), b_ref[...])

        @pl.when(c == 0)
        def _():
            if add is None:
                o_ref[...] = part
            else:
                o_ref[...] = part + add_scale * add_ref[...]

        @pl.when(c > 0)
        def _():
            o_ref[...] += part

    in_specs = [pl.BlockSpec((tm, tc), lambda j, i, c: (i, c)), b_spec]
    args = [a, b]
    if add is not None:
        in_specs.append(pl.BlockSpec((tm, tk), lambda j, i, c: (i, j)))
        args.append(add)
    return pl.pallas_call(
        body, out_shape=jax.ShapeDtypeStruct((m, k), F32), grid=(k // tk, m // tm, nc),
        in_specs=in_specs, out_specs=pl.BlockSpec((tm, tk), lambda j, i, c: (i, j)),
        name=name, compiler_params=_params(("parallel", "parallel", "arbitrary")))(*args)


def _mm_tn(a, b, *, tka, tn, tt, name, out_shards=None):
    t, ka = a.shape
    n = b.shape[1]
    if out_shards:
        assert tn == n // out_shards
        out_shape = jax.ShapeDtypeStruct((out_shards, ka, tn), F32)
        o_spec = pl.BlockSpec((None, tka, tn), lambda i, j, s: (j, i, 0))
    else:
        out_shape = jax.ShapeDtypeStruct((ka, n), F32)
        o_spec = pl.BlockSpec((tka, tn), lambda i, j, s: (i, j))

    def body(a_ref, b_ref, o_ref):
        s = pl.program_id(2)
        part = lax.dot_general(a_ref[...].astype(BF16), b_ref[...].astype(BF16), (((0,), (0,)), ((), ())),
                               preferred_element_type=F32)

        @pl.when(s == 0)
        def _():
            o_ref[...] = part

        @pl.when(s > 0)
        def _():
            o_ref[...] += part

    return pl.pallas_call(
        body, out_shape=out_shape, grid=(ka // tka, n // tn, t // tt),
        in_specs=[pl.BlockSpec((tt, tka), lambda i, j, s: (s, i)), pl.BlockSpec((tt, tn), lambda i, j, s: (s, j))],
        out_specs=o_spec, name=name, compiler_params=_params(("parallel", "parallel", "arbitrary")))(a, b)


CONV_TM = 512
CONV_TC = 1024
CONV_RC = 64
CONV_CC = 256


def _halo_specs(t, tm, tc, col0):
    nb8 = t // 8
    r8 = tm // 8
    return [
        pl.BlockSpec((8, tc), lambda i, j: (jnp.maximum(i * r8 - 1, 0), col0 + j)),
        pl.BlockSpec((tm, tc), lambda i, j: (i, col0 + j)),
        pl.BlockSpec((8, tc), lambda i, j: (jnp.minimum((i + 1) * r8, nb8 - 1), col0 + j)),
    ]


def _fill_ext(ext, prev_ref, cur_ref, next_ref, tm, i, last):
    ext[0:8, :] = jnp.where(i > 0, prev_ref[...], 0.0)
    ext[8:8 + tm, :] = cur_ref[...]
    ext[8 + tm:16 + tm, :] = jnp.where(i < last, next_ref[...], 0.0)


def _conv_fwd(u, conv_w, conv_b):
    t = u.shape[0]
    tm, tc = CONV_TM, CONV_TC

    def body(prev_ref, cur_ref, next_ref, w_ref, b_ref, o_ref, ext):
        _fill_ext(ext, prev_ref, cur_ref, next_ref, tm, pl.program_id(0), t // tm - 1)
        for c0 in range(0, tc, CONV_CC):
            cs = slice(c0, c0 + CONV_CC)
            w = w_ref[:, cs]
            for r0 in range(0, tm, CONV_RC):
                acc = jnp.broadcast_to(b_ref[:, cs], (CONV_RC, CONV_CC))
                for k in range(KCONV):
                    acc = acc + w[k:k + 1, :] * ext[pl.ds(r0 + 6 + k, CONV_RC), cs]
                o_ref[r0:r0 + CONV_RC, cs] = acc * _sigmoid(acc)

    return pl.pallas_call(
        body, out_shape=jax.ShapeDtypeStruct((t, CONVD), F32), grid=(t // tm, CONVD // tc),
        in_specs=_halo_specs(t, tm, tc, OXBC // tc) + [
            pl.BlockSpec((KCONV, tc), lambda i, j: (0, j)), pl.BlockSpec((1, tc), lambda i, j: (0, j))],
        out_specs=pl.BlockSpec((tm, tc), lambda i, j: (i, j)),
        scratch_shapes=[pltpu.VMEM((tm + 16, tc), F32)],
        name="conv_fwd", compiler_params=_params(("parallel", "parallel")))(u, u, u, conv_w, conv_b)


def _conv_dpre(u, dxs, dy, dbc, dsk_row, conv_w, conv_b):
    t = u.shape[0]
    tm, tc = CONV_TM, CONV_TC
    r8 = tm // 8
    nb8 = t // 8
    c0 = OXBC // tc

    def body(uprev, ucur, unext, f_ref, y_ref, cf_ref, dsk_ref, w_ref, bias_ref, dpre_ref, dw_ref, db_ref, ext):
        j = pl.program_id(0)
        i = pl.program_id(1)
        _fill_ext(ext, uprev, ucur, unext, tm, i, t // tm - 1)
        is_xs = j < 2
        dw_cols, db_cols = [], []
        for c0 in range(0, tc, CONV_CC):
            cs = slice(c0, c0 + CONV_CC)
            w = w_ref[:, cs]
            dsk = dsk_ref[:, cs]
            dw_acc = [jnp.zeros((1, CONV_CC), F32) for _ in range(KCONV)]
            db_acc = jnp.zeros((1, CONV_CC), F32)
            for r0 in range(0, tm, CONV_RC):
                rs = slice(r0, r0 + CONV_RC)
                taps = [ext[pl.ds(r0 + 6 + k, CONV_RC), cs] for k in range(KCONV)]
                pre = jnp.broadcast_to(bias_ref[:, cs], (CONV_RC, CONV_CC))
                for k in range(KCONV):
                    pre = pre + w[k:k + 1, :] * taps[k]
                s = _sigmoid(pre)
                up = jnp.where(is_xs, f_ref[rs, cs] + dsk * y_ref[rs, cs], cf_ref[rs, cs])
                dpre = up * (s * (1.0 + pre * (1.0 - s)))
                dpre_ref[rs, cs] = dpre
                for k in range(KCONV):
                    dw_acc[k] = dw_acc[k] + jnp.sum(dpre * taps[k], axis=0, keepdims=True)
                db_acc = db_acc + jnp.sum(dpre, axis=0, keepdims=True)
            dw_cols.append(jnp.concatenate(dw_acc + [jnp.zeros((8 - KCONV, CONV_CC), F32)], axis=0))
            db_cols.append(jnp.broadcast_to(db_acc, (8, CONV_CC)))
        dw_part = jnp.concatenate(dw_cols, axis=1)
        db_part = jnp.concatenate(db_cols, axis=1)

        @pl.when(i == 0)
        def _():
            dw_ref[...] = dw_part
            db_ref[...] = db_part

        @pl.when(i > 0)
        def _():
            dw_ref[...] += dw_part
            db_ref[...] += db_part

    xs_spec = pl.BlockSpec((tm, tc), lambda j, i: (jnp.where(j < 2, i, 0), jnp.minimum(j, 1)))
    bc_spec = pl.BlockSpec((tm, tc), lambda j, i: (jnp.where(j == 2, i, 0), 0))
    in_specs = [
        pl.BlockSpec((8, tc), lambda j, i: (jnp.maximum(i * r8 - 1, 0), c0 + j)),
        pl.BlockSpec((tm, tc), lambda j, i: (i, c0 + j)),
        pl.BlockSpec((8, tc), lambda j, i: (jnp.minimum((i + 1) * r8, nb8 - 1), c0 + j)),
        xs_spec, xs_spec, bc_spec,
        pl.BlockSpec((1, tc), lambda j, i: (0, jnp.minimum(j, 1))),
        pl.BlockSpec((KCONV, tc), lambda j, i: (0, j)), pl.BlockSpec((1, tc), lambda j, i: (0, j)),
    ]
    return pl.pallas_call(
        body,
        out_shape=(jax.ShapeDtypeStruct((t, CONVD), F32), jax.ShapeDtypeStruct((8, CONVD), F32),
                   jax.ShapeDtypeStruct((8, CONVD), F32)),
        grid=(CONVD // tc, t // tm), in_specs=in_specs,
        out_specs=(pl.BlockSpec((tm, tc), lambda j, i: (i, j)),
                   pl.BlockSpec((8, tc), lambda j, i: (0, j)), pl.BlockSpec((8, tc), lambda j, i: (0, j))),
        scratch_shapes=[pltpu.VMEM((tm + 16, tc), F32)],
        name="conv_dpre", compiler_params=_params(("parallel", "arbitrary")))(
            u, u, u, dxs, dy, dbc, dsk_row, conv_w, conv_b)


def _conv_dx(du, dpre, conv_w):
    t = dpre.shape[0]
    tm, tc = CONV_TM, CONV_TC
    r8 = tm // 8
    nb8 = t // 8

    def body(prev_ref, cur_ref, next_ref, w_ref, du_in, du_out, ext):
        del du_in
        _fill_ext(ext, prev_ref, cur_ref, next_ref, tm, pl.program_id(1), t // tm - 1)
        for c0 in range(0, tc, CONV_CC):
            cs = slice(c0, c0 + CONV_CC)
            w = w_ref[:, cs]
            for r0 in range(0, tm, CONV_RC):
                acc = jnp.zeros((CONV_RC, CONV_CC), F32)
                for k in range(KCONV):
                    acc = acc + w[k:k + 1, :] * ext[pl.ds(r0 + 10 - k, CONV_RC), cs]
                du_out[r0:r0 + CONV_RC, cs] = acc.astype(du_out.dtype)

    in_specs = [
        pl.BlockSpec((8, tc), lambda j, i: (jnp.maximum(i * r8 - 1, 0), j)),
        pl.BlockSpec((tm, tc), lambda j, i: (i, j)),
        pl.BlockSpec((8, tc), lambda j, i: (jnp.minimum((i + 1) * r8, nb8 - 1), j)),
        pl.BlockSpec((KCONV, tc), lambda j, i: (0, j)),
        pl.BlockSpec(memory_space=pl.ANY),
    ]
    return pl.pallas_call(
        body, out_shape=jax.ShapeDtypeStruct(du.shape, du.dtype), grid=(CONVD // tc, t // tm), in_specs=in_specs,
        out_specs=pl.BlockSpec((tm, tc), lambda j, i: (i, OXBC // tc + j)),
        scratch_shapes=[pltpu.VMEM((tm + 16, tc), F32)], input_output_aliases={4: 0},
        name="conv_dx", compiler_params=_params(("parallel", "parallel")))(dpre, dpre, dpre, conv_w, du)


def _ssd_common(dtr_ref, par_ref, rev):
    raw = dtr_ref[...]
    lane = _iota((1, 128), 1)
    mine = (lane >= 32 * rev) & (lane < 32 * rev + 32)
    bias = par_ref[0:1, :]
    arow = jnp.where(mine, -jnp.exp(par_ref[1:2, :]), 0.0)
    dt = _softplus(raw + bias)
    a = dt * arow
    ri = _iota((Q, Q), 0)
    ci = _iota((Q, Q), 1)
    tri = (ci >= ri) if rev else (ci <= ri)
    trit = (ci <= ri) if rev else (ci >= ri)
    cs = _dot01_l(tri.astype(BF16), a)
    return raw, bias, arow, mine, dt, cs, tri, trit


def _expand_mat(rev):
    r = np.arange(128)[:, None]
    c = np.arange(DI)[None, :]
    return jnp.asarray(r == (c // HP) + 32 * rev, BF16)


def _sum_mat(rev):
    r = np.arange(DI)[:, None]
    c = np.arange(128)[None, :]
    return jnp.asarray(c == (r // HP) + 32 * rev, BF16)


def _ssd_fwd(xbc, u, par, y_add=None, *, rev):
    t = xbc.shape[0]
    nc = t // Q
    end = 0 if rev else Q - 1
    cmap = (lambda c: nc - 1 - c) if rev else (lambda c: c)

    def body(xbc_ref, dtr_ref, par_ref, ex_ref, *rest):
        yadd_ref = rest[0] if y_add is not None else None
        y_ref, st_ref, h_scr = rest[-3:]
        step = pl.program_id(0)

        @pl.when(step == 0)
        def _():
            h_scr[...] = jnp.zeros((NS, DI), F32)

        raw, bias, arow, mine, dt, cs, tri, trit = _ssd_common(dtr_ref, par_ref, rev)
        cst = cs.T
        dtt = dt.T
        tot_col = cst[:, end:end + 1]
        wt = dtt * jnp.exp(tot_col - cst)
        ecs_all = jnp.exp(cs)
        gam = jnp.exp(cs[end:end + 1, :])
        gam_x = _dot01(jnp.broadcast_to(gam, (8, 128)), ex_ref[...])[0:1, :]
        lane = _iota((Q, 128), 1)
        sel = lane < HP
        st_ref[...] = h_scr[...]
        for g in range(NG):
            bg = xbc_ref[:, DI + NS * g:DI + NS * (g + 1)]
            cg = xbc_ref[:, DI + NG * NS + NS * g:DI + NG * NS + NS * (g + 1)]
            cb = _dot_nt(cg.astype(BF16), bg.astype(BF16))
            bt = bg.T
            for k in range(4):
                lo = 512 * g + 128 * k
                xp = xbc_ref[:, lo:lo + 128].astype(BF16)
                hp = h_scr[:, lo:lo + 128]
                rhs = jnp.concatenate([xp, hp.astype(BF16)], axis=0)
                lhs, bts = [], []
                for j in range(2):
                    hc = 8 * g + 2 * k + j + 32 * rev
                    csc = jnp.broadcast_to(cs[:, hc:hc + 1], (Q, Q))
                    lm = jnp.exp(jnp.where(tri, csc - cst[hc:hc + 1, :], NEG)) * dtt[hc:hc + 1, :]
                    mh = (cb * lm).astype(BF16)
                    ec = (jnp.broadcast_to(ecs_all[:, hc:hc + 1], (Q, NS)) * cg).astype(BF16)
                    lhs.append(jnp.concatenate([mh, ec], axis=1))
                    bts.append((bt * wt[hc:hc + 1, :]).astype(BF16))
                ys = jnp.dot(jnp.concatenate(lhs, axis=0), rhs, preferred_element_type=F32)
                ss = jnp.dot(jnp.concatenate(bts, axis=0), xp, preferred_element_type=F32)
                yp = jnp.where(sel, ys[0:Q], ys[Q:2 * Q])
                y_ref[:, lo:lo + 128] = yp if yadd_ref is None else yp + yadd_ref[:, lo:lo + 128]
                h_scr[:, lo:lo + 128] = gam_x[:, lo:lo + 128] * hp + jnp.where(sel, ss[0:NS], ss[NS:2 * NS])

    return pl.pallas_call(
        body,
        out_shape=(jax.ShapeDtypeStruct((t, DI), F32), jax.ShapeDtypeStruct((nc, NS, DI), F32)),
        grid=(nc,),
        in_specs=[pl.BlockSpec((Q, CONVD), lambda c: (cmap(c), 0)),
                  pl.BlockSpec((Q, 128), lambda c: (cmap(c), ODT // 128)),
                  pl.BlockSpec((8, 128), lambda c: (0, 0)),
                  pl.BlockSpec((128, DI), lambda c: (0, 0))]
        + ([pl.BlockSpec((Q, DI), lambda c: (cmap(c), 0))] if y_add is not None else []),
        out_specs=(pl.BlockSpec((Q, DI), lambda c: (cmap(c), 0)),
                   pl.BlockSpec((None, NS, DI), lambda c: (cmap(c), 0, 0))),
        scratch_shapes=[pltpu.VMEM((NS, DI), F32)],
        name="ssd_fwd_rev" if rev else "ssd_fwd", compiler_params=_params(("arbitrary",)))(
            xbc, u, par, _expand_mat(rev), *([y_add] if y_add is not None else []))


def _ssd_bwd(xbc, u, par, dy, st, *, rev, add=None, side=None):
    t = xbc.shape[0]
    nc = t // Q
    end = 0 if rev else Q - 1
    cmap = (lambda c: c) if rev else (lambda c: nc - 1 - c)

    def body(xbc_ref, dtr_ref, par_ref, dy_ref, hin_ref, ex_ref, sm_ref, *rest):
        addx_ref, addbc_ref, addt_ref = rest[:3] if add is not None else (None, None, None)
        dxs_ref, dbc_ref, ddt_ref, acc_ref, dh_scr = rest[-5:]
        step = pl.program_id(0)

        @pl.when(step == 0)
        def _():
            dh_scr[...] = jnp.zeros((NS, DI), F32)

        raw, bias, arow, mine, dt, cs, tri, trit = _ssd_common(dtr_ref, par_ref, rev)
        ri = _iota((Q, Q), 0)
        ci = _iota((Q, Q), 1)
        stri = ((ri > ci) if rev else (ri < ci)).astype(BF16)
        strit = ((ci > ri) if rev else (ci < ri)).astype(BF16)
        cst = cs.T
        dtt = dt.T
        et = jnp.exp(cst)
        ecs_all = jnp.exp(cs)
        ws_all = jnp.exp(cs[end:end + 1, :] - cs)
        expand = ex_ref[...]
        summat = sm_ref[...]
        gam = jnp.exp(cs[end:end + 1, :])
        gam_x = _dot01(jnp.broadcast_to(gam, (8, 128)), expand)[0:1, :]
        dt_hi, dt_mid, _ = _split3(dt)
        dtx = (jnp.dot(dt_hi, expand, preferred_element_type=F32)
               + jnp.dot(dt_mid, expand, preferred_element_type=F32))
        lane = _iota((Q, 128), 1)
        sel = lane < HP
        dho = dh_scr[...]
        t3 = jnp.sum(dho * hin_ref[...], axis=0, keepdims=True) * gam_x
        dxs_cols, dxs2_cols, yoff_cols, a1_rows = [], [], [], []
        for g in range(NG):
            bg = xbc_ref[:, DI + NS * g:DI + NS * (g + 1)]
            cg = xbc_ref[:, DI + NG * NS + NS * g:DI + NG * NS + NS * (g + 1)]
            bb = bg.astype(BF16)
            cbf = cg.astype(BF16)
            cb = _dot_nt(cbf, bb)
            cbt = _dot_nt(bb, cbf)
            ct = cg.T
            bdh = jnp.dot(bb, dho[:, 512 * g:512 * (g + 1)].astype(BF16), preferred_element_type=F32)
            dcb = jnp.zeros((Q, Q), F32)
            dcg = jnp.zeros((Q, NS), F32)
            dbg = jnp.zeros((Q, NS), F32)
            for k in range(4):
                lo = 512 * g + 128 * k
                xpf = xbc_ref[:, lo:lo + 128]
                xp = xpf.astype(BF16)
                dyp = dy_ref[:, lo:lo + 128]
                dypb = dyp.astype(BF16)
                hinp = hin_ref[:, lo:lo + 128].astype(BF16)
                dhp = dho[:, lo:lo + 128]
                es, ws, lmds, mts, ctes, dyms, ecbs = [], [], [], [], [], [], []
                for j in range(2):
                    hc = 8 * g + 2 * k + j + 32 * rev
                    csc = jnp.broadcast_to(cs[:, hc:hc + 1], (Q, Q))
                    csr = cst[hc:hc + 1, :]
                    lmds.append(jnp.exp(jnp.where(tri, csc - csr, NEG)) * dtt[hc:hc + 1, :])
                    lmb = jnp.exp(jnp.where(trit, csr - csc, NEG))
                    mts.append((cbt * lmb).astype(BF16))
                    dyms.append(jnp.where(sel if j == 0 else ~sel, dyp, 0.0).astype(BF16))
                    ecs = jnp.broadcast_to(ecs_all[:, hc:hc + 1], (Q, NS))
                    es.append(ecs)
                    ws.append(jnp.broadcast_to(ws_all[:, hc:hc + 1], (Q, NS)))
                    ecbs.append((ecs * cg).astype(BF16))
                    ctes.append((ct * et[hc:hc + 1, :]).astype(BF16))
                by_dy = jnp.dot(jnp.concatenate(mts + ctes, axis=0), dypb, preferred_element_type=F32)
                dmm = _dot_nt(jnp.concatenate(dyms, axis=0), xp)
                dm0, dm1 = dmm[0:Q] * lmds[0], dmm[Q:2 * Q] * lmds[1]
                dcb = dcb + dm0 + dm1
                rr = jnp.dot(jnp.concatenate([dm0 * cb, dm1 * cb], axis=0).astype(BF16), stri, preferred_element_type=F32)
                a1_rows.append(jnp.sum(jnp.where(tri, rr[0:Q], 0.0), axis=0, keepdims=True))
                a1_rows.append(jnp.sum(jnp.where(tri, rr[Q:2 * Q], 0.0), axis=0, keepdims=True))
                yo = jnp.dot(jnp.concatenate(ecbs, axis=0), hinp, preferred_element_type=F32)
                e_p = jnp.where(sel, es[0], es[1])
                w_p = jnp.where(sel, ws[0], ws[1])
                d2 = w_p * bdh[:, 128 * k:128 * (k + 1)]
                dxs2_cols.append(d2)
                dxs_cols.append(jnp.where(sel, by_dy[0:Q], by_dy[Q:2 * Q]) + d2)
                yoff_cols.append(jnp.where(sel, yo[0:Q], yo[Q:2 * Q]))
                dcg = dcg + _dot_nt((e_p * dyp).astype(BF16), hinp)
                dbg = dbg + _dot_nt((w_p * dtx[:, lo:lo + 128] * xpf).astype(BF16), dhp.astype(BF16))
                dh_scr[:, lo:lo + 128] = (gam_x[:, lo:lo + 128] * dhp
                                          + jnp.where(sel, by_dy[2 * Q:3 * Q], by_dy[3 * Q:4 * Q]))
            dcg = dcg + jnp.dot(dcb.astype(BF16), bb, preferred_element_type=F32)
            dbg = dbg + jnp.dot(dcb.T.astype(BF16), cbf, preferred_element_type=F32)
            lo_b, lo_c = NS * g, NG * NS + NS * g
            if addbc_ref is not None:
                dbg = dbg + addbc_ref[:, lo_b:lo_b + NS]
                dcg = dcg + addbc_ref[:, lo_c:lo_c + NS]
            dbc_ref[:, lo_b:lo_b + NS] = dbg
            dbc_ref[:, lo_c:lo_c + NS] = dcg
        dxs = jnp.concatenate(dxs_cols, axis=1)
        dxs_ref[...] = dxs * dtx if addx_ref is None else dxs * dtx + addx_ref[...]
        xs = xbc_ref[:, 0:DI]
        stacked = jnp.concatenate([xs * dxs, xs * jnp.concatenate(dxs2_cols, axis=1),
                                   dy_ref[...] * jnp.concatenate(yoff_cols, axis=1),
                                   jnp.broadcast_to(t3, (8, DI))], axis=0).astype(BF16)
        sums = jnp.dot(stacked, summat, preferred_element_type=F32)
        rx, rx2, ryo, c0 = sums[0:Q], sums[Q:2 * Q], sums[2 * Q:3 * Q], sums[3 * Q:3 * Q + 1]
        zero32 = jnp.zeros((32, Q), F32)
        a1t = jnp.concatenate(([zero32] if rev else []) + a1_rows + [zero32] * (2 if rev else 3), axis=0)
        da = (a1t.T + jnp.dot(trit.astype(BF16), ryo.astype(BF16), preferred_element_type=F32)
              + jnp.dot(strit, (dt * rx2).astype(BF16), preferred_element_type=F32) + jnp.where(mine, c0, 0.0))
        ddt = rx + da * arow
        ddtr = ddt * _sigmoid(raw + bias)
        ddt_ref[...] = ddtr if addt_ref is None else ddtr + addt_ref[...]
        part = jnp.concatenate([jnp.sum(ddtr, axis=0, keepdims=True),
                                jnp.sum(da * dt, axis=0, keepdims=True) * arow,
                                jnp.zeros((6, 128), F32)], axis=0)

        @pl.when(step == 0)
        def _():
            acc_ref[...] = part

        @pl.when(step > 0)
        def _():
            acc_ref[...] += part

    outs, side_outs = _host_call(
        body, side, nc,
        out_shape=(jax.ShapeDtypeStruct((t, DI), F32), jax.ShapeDtypeStruct((t, 2 * NG * NS), F32),
                   jax.ShapeDtypeStruct((t, 128), F32), jax.ShapeDtypeStruct((8, 128), F32)),
        in_specs=[pl.BlockSpec((Q, CONVD), lambda c: (cmap(c), 0)),
                  pl.BlockSpec((Q, 128), lambda c: (cmap(c), ODT // 128)),
                  pl.BlockSpec((8, 128), lambda c: (0, 0)),
                  pl.BlockSpec((Q, DI), lambda c: (cmap(c), 0)),
                  pl.BlockSpec((None, NS, DI), lambda c: (cmap(c), 0, 0)),
                  pl.BlockSpec((128, DI), lambda c: (0, 0)), pl.BlockSpec((DI, 128), lambda c: (0, 0))]
        + ([pl.BlockSpec((Q, DI), lambda c: (cmap(c), 0)), pl.BlockSpec((Q, 2 * NG * NS), lambda c: (cmap(c), 0)),
            pl.BlockSpec((Q, 128), lambda c: (cmap(c), 0))] if add is not None else []),
        out_specs=(pl.BlockSpec((Q, DI), lambda c: (cmap(c), 0)),
                   pl.BlockSpec((Q, 2 * NG * NS), lambda c: (cmap(c), 0)),
                   pl.BlockSpec((Q, 128), lambda c: (cmap(c), 0)),
                   pl.BlockSpec((8, 128), lambda c: (0, 0))),
        scratch_shapes=[pltpu.VMEM((NS, DI), F32)],
        args=(xbc, u, par, dy, st, _expand_mat(rev), _sum_mat(rev)) + (tuple(add) if add is not None else ()), aliases={},
        name="ssd_bwd_rev" if rev else "ssd_bwd", sem=("arbitrary",))
    return (*outs, side_outs)


GN_TM = 256
GN_GROUP = DI // NG


def _gn_forward_vals(y0, xs, z, dsk):
    y = y0 + dsk * xs
    sz = _sigmoid(z)
    gate = z * sz
    y2 = y * gate
    parts, rs = [], []
    for g in range(NG):
        seg = y2[:, GN_GROUP * g:GN_GROUP * (g + 1)]
        r = lax.rsqrt(jnp.mean(seg * seg, axis=1, keepdims=True) + NORM_EPS)
        rs.append(r)
        parts.append(seg * r)
    yn = jnp.concatenate(parts, axis=1)
    return y, sz, gate, yn, rs


def _gatenorm_fwd(y_fb, xbc, u, dsk_row, nw_row):
    t = y_fb.shape[0]
    tm = GN_TM

    def body(y_ref, xs_ref, z_ref, dsk_ref, nw_ref, o_ref):
        _, _, _, yn, _ = _gn_forward_vals(y_ref[...], xs_ref[...], z_ref[...], dsk_ref[...])
        o_ref[...] = (yn * nw_ref[...]).astype(BF16)

    blk = pl.BlockSpec((tm, DI), lambda i: (i, 0))
    row = pl.BlockSpec((1, DI), lambda i: (0, 0))
    return pl.pallas_call(
        body, out_shape=jax.ShapeDtypeStruct((t, DI), BF16), grid=(t // tm,),
        in_specs=[blk, blk, pl.BlockSpec((tm, DI), lambda i: (i, OZ // DI)), row, row],
        out_specs=blk, name="gatenorm_fwd", compiler_params=_params(("parallel",)))(y_fb, xbc, u, dsk_row, nw_row)


def _gatenorm_bwd(ds_out, y_fb, xbc, u, du, dsk_row, nw_row, side=None):
    t = y_fb.shape[0]
    tm = GN_TM

    def body(ds_ref, y_ref, xs_ref, z_ref, dsk_ref, nw_ref, sm_ref, du_in, dy_ref, du_out, dnw_ref, dds_ref):
        del du_in
        i = pl.program_id(0)
        xs = xs_ref[...]
        z = z_ref[...]
        y, sz, gate, yn, rs = _gn_forward_vals(y_ref[...], xs, z, dsk_ref[...])
        ds = ds_ref[...]
        gsc = ds * nw_ref[...]
        parts = []
        for g in range(NG):
            sl = slice(GN_GROUP * g, GN_GROUP * (g + 1))
            m = jnp.mean(gsc[:, sl] * yn[:, sl], axis=1, keepdims=True)
            parts.append(rs[g] * (gsc[:, sl] - yn[:, sl] * m))
        dy2 = jnp.concatenate(parts, axis=1)
        dy = dy2 * gate
        dy_ref[...] = dy
        du_out[...] = (dy2 * y * (sz * (1.0 + z * (1.0 - sz)))).astype(du_out.dtype)
        dnw = jnp.broadcast_to(jnp.sum(ds * yn, axis=0, keepdims=True), (8, DI))
        drow = jnp.broadcast_to(jnp.sum(dy * xs, axis=0, keepdims=True), (8, DI))
        dds = _dot01(drow, sm_ref[...])

        @pl.when(i == 0)
        def _():
            dnw_ref[...] = dnw
            dds_ref[...] = dds

        @pl.when(i > 0)
        def _():
            dnw_ref[...] += dnw
            dds_ref[...] += dds

    blk = pl.BlockSpec((tm, DI), lambda i: (i, 0))
    row = pl.BlockSpec((1, DI), lambda i: (0, 0))
    outs, side_outs = _host_call(
        body, side, t // tm,
        out_shape=(jax.ShapeDtypeStruct((t, DI), F32), jax.ShapeDtypeStruct(du.shape, du.dtype),
                   jax.ShapeDtypeStruct((8, DI), F32), jax.ShapeDtypeStruct((8, 128), F32)),
        in_specs=[blk, blk, blk, pl.BlockSpec((tm, DI), lambda i: (i, OZ // DI)), row, row,
                  pl.BlockSpec((DI, 128), lambda i: (0, 0)), pl.BlockSpec(memory_space=pl.ANY)],
        out_specs=(blk, pl.BlockSpec((tm, DI), lambda i: (i, OZ // DI)),
                   pl.BlockSpec((8, DI), lambda i: (0, 0)), pl.BlockSpec((8, 128), lambda i: (0, 0))),
        scratch_shapes=[], args=(ds_out, y_fb, xbc, u, dsk_row, nw_row, _sum_mat(0), du), aliases={7: 1},
        name="gatenorm_bwd", sem=("arbitrary",))
    return (*outs, side_outs)


AT_B = 128
AT_W = AT_B + 2 * ATT_HALF
AT_L = 2 * AH
SCALE = 1.0 / math.sqrt(AH)


def _slope(g, hh):
    return 2.0 ** (-8.0 * (4 * g + hh + 1) / 12.0)


def _qcol(g):
    return lambda p: OQ // AT_L + 2 * g + p


def _kcol(g):
    return lambda p: OKV // AT_L + 4 * g + 2 * p


def _vcol(g):
    return lambda p: OKV // AT_L + 4 * g + 2 * p + 1


def _pcol(p):
    return p


def _sub(d):
    return 4 if d == 1 else 1


def _win_specs(col, t, d):
    tb, hb = AT_B * d * _sub(d), ATT_HALF * d
    per = tb // hb
    nh = t // hb
    return [
        pl.BlockSpec((hb, AT_L), lambda p, i: (jnp.maximum(per * i - 1, 0), col(p))),
        pl.BlockSpec((tb, AT_L), lambda p, i: (i, col(p))),
        pl.BlockSpec((hb, AT_L), lambda p, i: (jnp.minimum(per * (i + 1), nh - 1), col(p))),
    ]


def _blk_spec(col, d):
    return pl.BlockSpec((AT_B * d * _sub(d), AT_L), lambda p, i: (i, col(p)))


def _rows(ref, r, s, d):
    return ref[pl.ds(r, AT_B, stride=d), :] if d > 1 else ref[AT_B * s:AT_B * (s + 1), :]


def _win(p_ref, c_ref, n_ref, r, s, d):
    if d > 1:
        return jnp.concatenate([p_ref[pl.ds(r, ATT_HALF, stride=d), :], c_ref[pl.ds(r, AT_B, stride=d), :],
                                n_ref[pl.ds(r, ATT_HALF, stride=d), :]], axis=0)
    if s == 0:
        return jnp.concatenate([p_ref[...], c_ref[0:AT_B + ATT_HALF, :]], axis=0)
    if s == _sub(d) - 1:
        return jnp.concatenate([c_ref[AT_B * s - ATT_HALF:AT_B * (s + 1), :], n_ref[...]], axis=0)
    return c_ref[AT_B * s - ATT_HALF:AT_B * (s + 1) + ATT_HALF, :]


def _put_rows(ref, r, s, d, val):
    if d > 1:
        ref[pl.ds(r, AT_B, stride=d), :] = val
    else:
        ref[AT_B * s:AT_B * (s + 1), :] = val


def _for_blocks(d, fn):
    if d == 1:
        for s in range(_sub(d)):
            fn(0, s)
    else:
        def step(r, c):
            fn(r, 0)
            return c
        lax.fori_loop(0, d, step, 0, unroll=4)


def _attn_bias(blk, ln, d, g, p_id):
    a = blk * AT_B + _iota((AT_B, AT_W), 0)
    b = blk * AT_B - ATT_HALF + _iota((AT_B, AT_W), 1)
    rel = jnp.abs(a - b)
    valid = (rel <= ATT_HALF) & (b >= 0) & (b < ln)
    dist = (rel * d).astype(F32)
    out = []
    for hh in range(2):
        slope = jnp.where(p_id == 0, _slope(g, hh), _slope(g, 2 + hh))
        out.append(jnp.where(valid, -slope * dist, NEG))
    return out


def _attn_fwd(u, g):
    t = u.shape[0]
    d = DILATIONS[g]
    ln = t // d

    def body(q_ref, kp, kc, kn, vp, vc, vn, o_ref, l_ref):
        p_id = pl.program_id(0)
        i = pl.program_id(1)
        lane = _iota((AT_B, AT_L), 1)
        biases = [_attn_bias(i * _sub(d) + s, ln, d, g, p_id) for s in range(_sub(d))]

        def one(r, s):
            q = _rows(q_ref, r, s, d)
            kw = _win(kp, kc, kn, r, s, d).astype(BF16)
            vw = _win(vp, vc, vn, r, s, d).astype(BF16)
            o = jnp.zeros((AT_B, AT_L), F32)
            lse = jnp.zeros((AT_B, AT_L), F32)
            for hh in range(2):
                hm = (lane // AH) == hh
                qm = jnp.where(hm, q, 0.0).astype(BF16)
                sc = _dot_nt(qm, kw) * SCALE + biases[s][hh]
                m = jnp.max(sc, axis=1, keepdims=True)
                pr = jnp.exp(sc - m)
                den = jnp.sum(pr, axis=1, keepdims=True)
                oh = jnp.dot(pr.astype(BF16), vw, preferred_element_type=F32)
                o = jnp.where(hm, oh / den, o)
                lse = jnp.where(hm, m + jnp.log(den), lse)
            _put_rows(o_ref, r, s, d, o)
            _put_rows(l_ref, r, s, d, lse)

        _for_blocks(d, one)

    oshape = jax.ShapeDtypeStruct((t, 2 * AT_L), F32)
    ospec = _blk_spec(_pcol, d)
    return pl.pallas_call(
        body, out_shape=(oshape, oshape), grid=(2, t // (AT_B * d * _sub(d))),
        in_specs=[_blk_spec(_qcol(g), d)] + _win_specs(_kcol(g), t, d) + _win_specs(_vcol(g), t, d),
        out_specs=(ospec, ospec), name=f"attn_fwd_{g}", compiler_params=_params(("parallel", "parallel")))(
            u, u, u, u, u, u, u)


def _attn_dq(u, du, do, lse, e, g):
    t = u.shape[0]
    d = DILATIONS[g]
    ln = t // d

    def body(q_ref, kp, kc, kn, vp, vc, vn, do_ref, l_ref, e_ref, du_in, dq_ref, dq_scr):
        del du_in
        p_id = pl.program_id(0)
        i = pl.program_id(1)
        lane = _iota((AT_B, AT_L), 1)
        biases = [_attn_bias(i * _sub(d) + s, ln, d, g, p_id) for s in range(_sub(d))]

        def one(r, s):
            q = _rows(q_ref, r, s, d)
            kw = _win(kp, kc, kn, r, s, d).astype(BF16)
            vw = _win(vp, vc, vn, r, s, d).astype(BF16)
            do_ = _rows(do_ref, r, s, d)
            lv = _rows(l_ref, r, s, d)
            ev = _rows(e_ref, r, s, d)
            dq = jnp.zeros((AT_B, AT_L), F32)
            for hh in range(2):
                hm = (lane // AH) == hh
                qm = jnp.where(hm, q, 0.0).astype(BF16)
                sc = _dot_nt(qm, kw) * SCALE + biases[s][hh]
                lcol = jnp.broadcast_to(lv[:, AH * hh:AH * hh + 1], (AT_B, AT_W))
                ecol = jnp.broadcast_to(ev[:, AH * hh:AH * hh + 1], (AT_B, AT_W))
                pr = jnp.exp(sc - lcol)
                dom = jnp.where(hm, do_, 0.0).astype(BF16)
                ds = pr * (_dot_nt(dom, vw) + ecol)
                dqh = jnp.dot(ds.astype(BF16), kw, preferred_element_type=F32) * SCALE
                dq = jnp.where(hm, dqh, dq)
            _put_rows(dq_scr, r, s, d, dq)

        _for_blocks(d, one)
        dq_ref[...] = dq_scr[...].astype(dq_ref.dtype)

    rspec = _blk_spec(_pcol, d)
    return pl.pallas_call(
        body, out_shape=jax.ShapeDtypeStruct(du.shape, du.dtype), grid=(2, t // (AT_B * d * _sub(d))),
        in_specs=[_blk_spec(_qcol(g), d)] + _win_specs(_kcol(g), t, d) + _win_specs(_vcol(g), t, d)
        + [rspec, rspec, rspec, pl.BlockSpec(memory_space=pl.ANY)],
        out_specs=_blk_spec(_qcol(g), d), input_output_aliases={10: 0},
        scratch_shapes=[pltpu.VMEM((AT_B * d * _sub(d), AT_L), F32)],
        name=f"attn_dq_{g}", compiler_params=_params(("parallel", "parallel")))(
            u, u, u, u, u, u, u, do, lse, e, du)


def _attn_dkv(u, du, do, lse, e, g):
    t = u.shape[0]
    d = DILATIONS[g]
    ln = t // d

    def body(k_ref, v_ref, qp, qc, qn, dp_, dc_, dn_, lp, lc, ln_, ep, ec, en, du_in, dkv_ref, dk_scr, dv_scr):
        del du_in
        p_id = pl.program_id(0)
        jb = pl.program_id(1)
        lane = _iota((AT_B, AT_L), 1)
        biases = [_attn_bias(jb * _sub(d) + s, ln, d, g, p_id) for s in range(_sub(d))]

        def one(r, s):
            k = _rows(k_ref, r, s, d)
            v = _rows(v_ref, r, s, d)
            qw = _win(qp, qc, qn, r, s, d).astype(BF16)
            dow = _win(dp_, dc_, dn_, r, s, d).astype(BF16)
            lt = _win(lp, lc, ln_, r, s, d).T
            et = _win(ep, ec, en, r, s, d).T
            dk = jnp.zeros((AT_B, AT_L), F32)
            dv = jnp.zeros((AT_B, AT_L), F32)
            for hh in range(2):
                hm = (lane // AH) == hh
                km = jnp.where(hm, k, 0.0).astype(BF16)
                st = _dot_nt(km, qw) * SCALE + biases[s][hh]
                pt = jnp.exp(st - lt[AH * hh:AH * hh + 1, :])
                dvh = jnp.dot(pt.astype(BF16), dow, preferred_element_type=F32)
                vm = jnp.where(hm, v, 0.0).astype(BF16)
                dst = pt * (_dot_nt(vm, dow) + et[AH * hh:AH * hh + 1, :])
                dkh = jnp.dot(dst.astype(BF16), qw, preferred_element_type=F32) * SCALE
                dk = jnp.where(hm, dkh, dk)
                dv = jnp.where(hm, dvh, dv)
            _put_rows(dk_scr, r, s, d, dk)
            _put_rows(dv_scr, r, s, d, dv)

        _for_blocks(d, one)
        dkv_ref[:, 0:AT_L] = dk_scr[...].astype(dkv_ref.dtype)
        dkv_ref[:, AT_L:2 * AT_L] = dv_scr[...].astype(dkv_ref.dtype)

    return pl.pallas_call(
        body, out_shape=jax.ShapeDtypeStruct(du.shape, du.dtype), grid=(2, t // (AT_B * d * _sub(d))),
        in_specs=[_blk_spec(_kcol(g), d), _blk_spec(_vcol(g), d)]
        + _win_specs(_qcol(g), t, d) + _win_specs(_pcol, t, d) + _win_specs(_pcol, t, d) + _win_specs(_pcol, t, d)
        + [pl.BlockSpec(memory_space=pl.ANY)],
        out_specs=pl.BlockSpec((AT_B * d * _sub(d), 2 * AT_L), lambda p, i: (i, OKV // (2 * AT_L) + 2 * g + p)),
        input_output_aliases={14: 0},
        scratch_shapes=[pltpu.VMEM((AT_B * d * _sub(d), AT_L), F32), pltpu.VMEM((AT_B * d * _sub(d), AT_L), F32)],
        name=f"attn_dkv_{g}", compiler_params=_params(("parallel", "parallel")))(
            u, u, u, u, u, do, do, do, lse, lse, lse, e, e, e, du)


CMB_TM = 1024


def _combine_weights(l0, l1, l2):
    m = jnp.maximum(jnp.maximum(l0, l1), l2)
    e0, e1, e2 = jnp.exp(l0 - m), jnp.exp(l1 - m), jnp.exp(l2 - m)
    inv = 1.0 / (e0 + e1 + e2)
    return e0 * inv, e1 * inv, e2 * inv


def _combine_fwd(os_, ls_):
    t = os_[0].shape[0]
    tm = CMB_TM

    def body(o0, o1, o2, l0, l1, l2, a_ref):
        w0, w1, w2 = _combine_weights(l0[...], l1[...], l2[...])
        a_ref[...] = w0 * o0[...] + w1 * o1[...] + w2 * o2[...]

    blk = pl.BlockSpec((tm, 2 * AT_L), lambda i: (i, 0))
    return pl.pallas_call(
        body, out_shape=jax.ShapeDtypeStruct((t, 2 * AT_L), F32), grid=(t // tm,), in_specs=[blk] * 6, out_specs=blk,
        name="combine_fwd", compiler_params=_params(("parallel",)))(*os_, *ls_)


def _combine_bwd(datt, os_, ls_):
    t = datt.shape[0]
    tm = CMB_TM

    def body(da_ref, o0, o1, o2, l0, l1, l2, d0, d1, d2, e0, e1, e2):
        w = _combine_weights(l0[...], l1[...], l2[...])
        da = da_ref[...]
        att = w[0] * o0[...] + w[1] * o1[...] + w[2] * o2[...]
        r = _iota((2 * AT_L, 2 * AT_L), 0) // AH
        c = _iota((2 * AT_L, 2 * AT_L), 1) // AH
        hs = _dot01(da * att, (r == c).astype(BF16))
        for wg, dref, eref in zip(w, (d0, d1, d2), (e0, e1, e2)):
            dref[...] = wg * da
            eref[...] = -wg * hs

    blk = pl.BlockSpec((tm, 2 * AT_L), lambda i: (i, 0))
    shp = jax.ShapeDtypeStruct((t, 2 * AT_L), F32)
    outs = pl.pallas_call(
        body, out_shape=(shp,) * 6, grid=(t // tm,), in_specs=[blk] * 7, out_specs=(blk,) * 6,
        name="combine_bwd", compiler_params=_params(("parallel",)))(datt, *os_, *ls_)
    return outs[0:3], outs[3:6]


def _combine_proj(os_, ls_, w_pa):
    t = os_[0].shape[0]
    tm = ROW_TM
    nsh, _, ws = w_pa.shape

    def body(o0, o1, o2, l0, l1, l2, w_ref, a_ref, y_ref):
        w0, w1, w2 = _combine_weights(l0[...], l1[...], l2[...])
        att = w0 * o0[...] + w1 * o1[...] + w2 * o2[...]
        a_ref[...] = att
        ab = att.astype(BF16)
        for sh in range(nsh):
            y_ref[:, ws * sh:ws * (sh + 1)] = jnp.dot(ab, w_ref[sh], preferred_element_type=F32)

    blk = pl.BlockSpec((tm, 2 * AT_L), lambda i: (i, 0))
    return pl.pallas_call(
        body, out_shape=(jax.ShapeDtypeStruct((t, 2 * AT_L), F32), jax.ShapeDtypeStruct((t, nsh * ws), F32)),
        grid=(t // tm,), in_specs=[blk] * 6 + [pl.BlockSpec(w_pa.shape, lambda i: (0, 0, 0))],
        out_specs=(blk, pl.BlockSpec((tm, nsh * ws), lambda i: (i, 0))),
        name="combine_proj", compiler_params=_params(("parallel",)))(*os_, *ls_, w_pa)


def _d_att_combine_bwd(dy_att, w_pa, os_, ls_):
    t = dy_att.shape[0]
    tm = ROW_TM
    nsh, _, ws = w_pa.shape

    def body(dy_ref, w_ref, o0, o1, o2, l0, l1, l2, d0, d1, d2, e0, e1, e2):
        da = jnp.zeros((tm, 2 * AT_L), F32)
        for sh in range(nsh):
            da = da + _dot_nt(dy_ref[:, ws * sh:ws * (sh + 1)], w_ref[sh])
        w = _combine_weights(l0[...], l1[...], l2[...])
        att = w[0] * o0[...] + w[1] * o1[...] + w[2] * o2[...]
        r = _iota((2 * AT_L, 2 * AT_L), 0) // AH
        c = _iota((2 * AT_L, 2 * AT_L), 1) // AH
        hs = _dot01(da * att, (r == c).astype(BF16))
        for wg, dref, eref in zip(w, (d0, d1, d2), (e0, e1, e2)):
            dref[...] = wg * da
            eref[...] = -wg * hs

    blk = pl.BlockSpec((tm, 2 * AT_L), lambda i: (i, 0))
    shp = jax.ShapeDtypeStruct((t, 2 * AT_L), F32)
    outs = pl.pallas_call(
        body, out_shape=(shp,) * 6, grid=(t // tm,),
        in_specs=[pl.BlockSpec((tm, nsh * ws), lambda i: (i, 0)), pl.BlockSpec(w_pa.shape, lambda i: (0, 0, 0))] + [blk] * 6,
        out_specs=(blk,) * 6, name="d_att_combine_bwd", compiler_params=_params(("parallel",)))(dy_att, w_pa, *os_, *ls_)
    return outs[0:3], outs[3:6]


ROW_TM = 512


def _mix_fwd(y_ssd, y_att, u, bg_row):
    t = y_ssd.shape[0]
    tm = ROW_TM

    def body(ys_ref, ya_ref, g0_ref, g1_ref, b0_ref, b1_ref, o_ref):
        g0 = _sigmoid(g0_ref[...] + b0_ref[...])
        g1 = _sigmoid(g1_ref[...] + b1_ref[...])
        o_ref[...] = (g0 * ys_ref[...] + g1 * ya_ref[...]).astype(BF16)

    blk = pl.BlockSpec((tm, D), lambda i: (i, 0))
    return pl.pallas_call(
        body, out_shape=jax.ShapeDtypeStruct((t, D), BF16), grid=(t // tm,),
        in_specs=[blk, blk, pl.BlockSpec((tm, D), lambda i: (i, OGATE // D)), pl.BlockSpec((tm, D), lambda i: (i, OGATE // D + 1)),
                  pl.BlockSpec((1, D), lambda i: (0, 0)), pl.BlockSpec((1, D), lambda i: (0, 1))],
        out_specs=blk, name="mix_fwd", compiler_params=_params(("parallel",)))(y_ssd, y_att, u, u, bg_row, bg_row)


def _mix_bwd(dmixin, y_ssd, y_att, u, bg_row):
    t = y_ssd.shape[0]
    tm = ROW_TM

    def body(dm_ref, ys_ref, ya_ref, g0_ref, g1_ref, b0_ref, b1_ref, dys_ref, dya_ref, du_ref, db_ref):
        i = pl.program_id(0)
        g0 = _sigmoid(g0_ref[...] + b0_ref[...])
        g1 = _sigmoid(g1_ref[...] + b1_ref[...])
        dm = dm_ref[...]
        dys_ref[...] = (dm * g0).astype(BF16)
        dya_ref[...] = (dm * g1).astype(BF16)
        dl0 = dm * ys_ref[...] * g0 * (1.0 - g0)
        dl1 = dm * ya_ref[...] * g1 * (1.0 - g1)
        du_ref[:, 0:D] = dl0.astype(BF16)
        du_ref[:, D:2 * D] = dl1.astype(BF16)
        part = jnp.concatenate([jnp.broadcast_to(jnp.sum(dl0, axis=0, keepdims=True), (8, D)),
                                jnp.broadcast_to(jnp.sum(dl1, axis=0, keepdims=True), (8, D))], axis=1)

        @pl.when(i == 0)
        def _():
            db_ref[...] = part

        @pl.when(i > 0)
        def _():
            db_ref[...] += part

    blk = pl.BlockSpec((tm, D), lambda i: (i, 0))
    return pl.pallas_call(
        body,
        out_shape=(jax.ShapeDtypeStruct((t, D), BF16), jax.ShapeDtypeStruct((t, D), BF16),
                   jax.ShapeDtypeStruct((t, UW), BF16), jax.ShapeDtypeStruct((8, 2 * D), F32)),
        grid=(t // tm,),
        in_specs=[blk, blk, blk, pl.BlockSpec((tm, D), lambda i: (i, OGATE // D)), pl.BlockSpec((tm, D), lambda i: (i, OGATE // D + 1)),
                  pl.BlockSpec((1, D), lambda i: (0, 0)), pl.BlockSpec((1, D), lambda i: (0, 1))],
        out_specs=(blk, blk, pl.BlockSpec((tm, 2 * D), lambda i: (i, OGATE // (2 * D))),
                   pl.BlockSpec((8, 2 * D), lambda i: (0, 0))),
        name="mix_bwd", compiler_params=_params(("arbitrary",)))(dmixin, y_ssd, y_att, u, u, bg_row, bg_row)


def _ln(x, g, b):
    mu = jnp.mean(x, axis=1, keepdims=True)
    xc = x - mu
    var = jnp.mean(xc * xc, axis=1, keepdims=True)
    rstd = lax.rsqrt(var + NORM_EPS)
    xhat = xc * rstd
    return xhat * g + b, xhat, rstd


def _ln_back(dh, xhat, rstd, g):
    dxh = dh * g
    m1 = jnp.mean(dxh, axis=1, keepdims=True)
    m2 = jnp.mean(dxh * xhat, axis=1, keepdims=True)
    return rstd * (dxh - m1 - xhat * m2)


def _ln1_fwd(x, mix, g_row, b_row):
    t = x.shape[0]
    tm = ROW_TM

    def body(x_ref, m_ref, g_ref, b_ref, pre_ref, h_ref):
        pre = ALPHA * x_ref[...] + m_ref[...]
        pre_ref[...] = pre
        h, _, _ = _ln(pre, g_ref[...], b_ref[...])
        h_ref[...] = h.astype(BF16)

    blk = pl.BlockSpec((tm, D), lambda i: (i, 0))
    row = pl.BlockSpec((1, D), lambda i: (0, 0))
    return pl.pallas_call(
        body, out_shape=(jax.ShapeDtypeStruct((t, D), F32), jax.ShapeDtypeStruct((t, D), BF16)), grid=(t // tm,),
        in_specs=[blk, blk, row, row], out_specs=(blk, blk),
        name="ln1_fwd", compiler_params=_params(("parallel",)))(x, mix, g_row, b_row)


def _ln1_bwd(dh, pre, g_row, b_row):
    t = dh.shape[0]
    tm = ROW_TM

    def body(dh_ref, pre_ref, g_ref, b_ref, dpre_ref, acc_ref):
        i = pl.program_id(0)
        dh_ = dh_ref[...]
        _, xhat, rstd = _ln(pre_ref[...], g_ref[...], b_ref[...])
        dpre_ref[...] = _ln_back(dh_, xhat, rstd, g_ref[...])
        part = jnp.concatenate([jnp.sum(dh_ * xhat, axis=0, keepdims=True), jnp.sum(dh_, axis=0, keepdims=True),
                                jnp.zeros((6, D), F32)], axis=0)

        @pl.when(i == 0)
        def _():
            acc_ref[...] = part

        @pl.when(i > 0)
        def _():
            acc_ref[...] += part

    blk = pl.BlockSpec((tm, D), lambda i: (i, 0))
    row = pl.BlockSpec((1, D), lambda i: (0, 0))
    return pl.pallas_call(
        body, out_shape=(jax.ShapeDtypeStruct((t, D), F32), jax.ShapeDtypeStruct((8, D), F32)), grid=(t // tm,),
        in_specs=[blk, blk, row, row], out_specs=(blk, pl.BlockSpec((8, D), lambda i: (0, 0))),
        name="ln1_bwd", compiler_params=_params(("arbitrary",)))(dh, pre, g_row, b_row)


def _ln2_loss(pre1, f, tgt, g1_row, b1_row, g2_row, b2_row):
    t = pre1.shape[0]
    tm = ROW_TM

    def body(p1_ref, f_ref, t_ref, g1_ref, b1_ref, g2_ref, b2_ref, dpre_ref, acc_ref):
        i = pl.program_id(0)
        h1, _, _ = _ln(p1_ref[...], g1_ref[...], b1_ref[...])
        pre2 = ALPHA * h1 + f_ref[...]
        h2, xhat, rstd = _ln(pre2, g2_ref[...], b2_ref[...])
        err = h2 - t_ref[...]
        dh = err * (1.0 / D)
        dpre_ref[...] = _ln_back(dh, xhat, rstd, g2_ref[...])
        loss = jnp.sum(jnp.sum(err * err, axis=1, keepdims=True), axis=0, keepdims=True) * (0.5 / D)
        part = jnp.concatenate([jnp.sum(dh * xhat, axis=0, keepdims=True), jnp.sum(dh, axis=0, keepdims=True),
                                jnp.broadcast_to(loss, (1, D)), jnp.zeros((5, D), F32)], axis=0)

        @pl.when(i == 0)
        def _():
            acc_ref[...] = part

        @pl.when(i > 0)
        def _():
            acc_ref[...] += part

    blk = pl.BlockSpec((tm, D), lambda i: (i, 0))
    row = pl.BlockSpec((1, D), lambda i: (0, 0))
    return pl.pallas_call(
        body, out_shape=(jax.ShapeDtypeStruct((t, D), F32), jax.ShapeDtypeStruct((8, D), F32)), grid=(t // tm,),
        in_specs=[blk, blk, blk, row, row, row, row], out_specs=(blk, pl.BlockSpec((8, D), lambda i: (0, 0))),
        name="ln2_loss", compiler_params=_params(("arbitrary",)))(pre1, f, tgt, g1_row, b1_row, g2_row, b2_row)


def _mlp_up(h1, w_up):
    t = h1.shape[0]
    tm, tn = ROW_TM, D

    def body(a_ref, b_ref, up_ref, act_ref):
        up = jnp.dot(a_ref[...], b_ref[...], preferred_element_type=F32)
        up_ref[...] = up.astype(BF16)
        r = jnp.maximum(up, 0.0)
        act_ref[...] = (r * r).astype(BF16)

    blk = pl.BlockSpec((tm, tn), lambda j, i: (i, j))
    return pl.pallas_call(
        body, out_shape=(jax.ShapeDtypeStruct((t, DFF), BF16), jax.ShapeDtypeStruct((t, DFF), BF16)),
        grid=(DFF // tn, t // tm),
        in_specs=[pl.BlockSpec((tm, D), lambda j, i: (i, 0)), pl.BlockSpec((None, D, tn), lambda j, i: (j, 0, 0))],
        out_specs=(blk, blk), name="mlp_up", compiler_params=_params(("parallel", "parallel")))(h1, w_up)


def _d_up(dpre2, w_down, up):
    t = up.shape[0]
    tm, tk = ROW_TM, D

    def body(a_ref, b_ref, u_ref, o_ref):
        dact = _dot_nt(a_ref[...], b_ref[...])
        o_ref[...] = (dact * 2.0 * jnp.maximum(u_ref[...].astype(F32), 0.0)).astype(BF16)

    blk = pl.BlockSpec((tm, tk), lambda j, i: (i, j))
    return pl.pallas_call(
        body, out_shape=jax.ShapeDtypeStruct((t, DFF), BF16), grid=(DFF // tk, t // tm),
        in_specs=[pl.BlockSpec((tm, D), lambda j, i: (i, 0)), pl.BlockSpec((tk, D), lambda j, i: (j, 0)), blk],
        out_specs=blk, name="d_up", compiler_params=_params(("parallel", "parallel")))(dpre2, w_down, up)


def _dt_bwd(du, ddt):
    t = ddt.shape[0]
    tm = 1024

    def body(f_ref, du_in, o_ref):
        del du_in
        o_ref[:, 0:128] = f_ref[...].astype(o_ref.dtype)
        o_ref[:, 128:256] = jnp.zeros((tm, 128), o_ref.dtype)

    blk = pl.BlockSpec((tm, 128), lambda i: (i, 0))
    return pl.pallas_call(
        body, out_shape=jax.ShapeDtypeStruct(du.shape, du.dtype), grid=(t // tm,),
        in_specs=[blk, pl.BlockSpec(memory_space=pl.ANY)],
        out_specs=pl.BlockSpec((tm, 256), lambda i: (i, ODT // 256)), input_output_aliases={1: 0},
        name="dt_bwd", compiler_params=_params(("parallel",)))(ddt, du)


def _mix_out_ln1(y_ssd, y_att, u, bg_row, x, w_out, g_row, b_row):
    t = x.shape[0]
    tm = ROW_TM

    def body(ys_ref, ya_ref, g0_ref, g1_ref, b0_ref, b1_ref, x_ref, w_ref, g_ref, b_ref, mixin_ref, pre_ref, h_ref):
        g0 = _sigmoid(g0_ref[...] + b0_ref[...])
        g1 = _sigmoid(g1_ref[...] + b1_ref[...])
        mixin = (g0 * ys_ref[...] + g1 * ya_ref[...]).astype(BF16)
        mixin_ref[...] = mixin
        pre = ALPHA * x_ref[...] + jnp.dot(mixin, w_ref[...], preferred_element_type=F32)
        pre_ref[...] = pre
        h, _, _ = _ln(pre, g_ref[...], b_ref[...])
        h_ref[...] = h.astype(BF16)

    blk = pl.BlockSpec((tm, D), lambda i: (i, 0))
    row = pl.BlockSpec((1, D), lambda i: (0, 0))
    return pl.pallas_call(
        body,
        out_shape=(jax.ShapeDtypeStruct((t, D), BF16), jax.ShapeDtypeStruct((t, D), F32), jax.ShapeDtypeStruct((t, D), BF16)),
        grid=(t // tm,),
        in_specs=[blk, blk, pl.BlockSpec((tm, D), lambda i: (i, OGATE // D)), pl.BlockSpec((tm, D), lambda i: (i, OGATE // D + 1)),
                  row, pl.BlockSpec((1, D), lambda i: (0, 1)), blk, pl.BlockSpec((D, D), lambda i: (0, 0)), row, row],
        out_specs=(blk, blk, blk), name="mix_out_ln1", compiler_params=_params(("parallel",)))(
            y_ssd, y_att, u, u, bg_row, bg_row, x, w_out, g_row, b_row)


def _mlp_down_ln2_loss(act, w_down, pre1, tgt, g1_row, b1_row, g2_row, b2_row):
    t = pre1.shape[0]
    tm = ROW_TM

    def body(a_ref, w_ref, p1_ref, t_ref, g1_ref, b1_ref, g2_ref, b2_ref, dpre_ref, dpreb_ref, acc_ref):
        i = pl.program_id(0)
        f = jnp.dot(a_ref[...], w_ref[...], preferred_element_type=F32)
        h1, _, _ = _ln(p1_ref[...], g1_ref[...], b1_ref[...])
        pre2 = ALPHA * h1 + f
        h2, xhat, rstd = _ln(pre2, g2_ref[...], b2_ref[...])
        err = h2 - t_ref[...]
        dh = err * (1.0 / D)
        dpre = _ln_back(dh, xhat, rstd, g2_ref[...])
        dpre_ref[...] = dpre
        dpreb_ref[...] = dpre.astype(BF16)
        loss = jnp.sum(jnp.sum(err * err, axis=1, keepdims=True), axis=0, keepdims=True) * (0.5 / D)
        part = jnp.concatenate([jnp.sum(dh * xhat, axis=0, keepdims=True), jnp.sum(dh, axis=0, keepdims=True),
                                jnp.broadcast_to(loss, (1, D)), jnp.zeros((5, D), F32)], axis=0)

        @pl.when(i == 0)
        def _():
            acc_ref[...] = part

        @pl.when(i > 0)
        def _():
            acc_ref[...] += part

    blk = pl.BlockSpec((tm, D), lambda i: (i, 0))
    row = pl.BlockSpec((1, D), lambda i: (0, 0))
    return pl.pallas_call(
        body,
        out_shape=(jax.ShapeDtypeStruct((t, D), F32), jax.ShapeDtypeStruct((t, D), BF16), jax.ShapeDtypeStruct((8, D), F32)),
        grid=(t // tm,),
        in_specs=[pl.BlockSpec((tm, DFF), lambda i: (i, 0)), pl.BlockSpec((DFF, D), lambda i: (0, 0)), blk, blk, row, row, row, row],
        out_specs=(blk, blk, pl.BlockSpec((8, D), lambda i: (0, 0))),
        name="mlp_down_ln2_loss", compiler_params=_params(("arbitrary",)))(act, w_down, pre1, tgt, g1_row, b1_row, g2_row, b2_row)


def _d_h1_ln1_bwd(dup, w_up, dpre2, pre1, g_row, b_row):
    t = dup.shape[0]
    tm = ROW_TM
    nsh = w_up.shape[0]

    def body(a_ref, w_ref, add_ref, pre_ref, g_ref, b_ref, dpre_ref, acc_ref):
        i = pl.program_id(0)
        dh_ = ALPHA * add_ref[...]
        for sh in range(nsh):
            dh_ = dh_ + _dot_nt(a_ref[:, D * sh:D * (sh + 1)], w_ref[sh])
        _, xhat, rstd = _ln(pre_ref[...], g_ref[...], b_ref[...])
        dpre_ref[...] = _ln_back(dh_, xhat, rstd, g_ref[...])
        rows = jnp.concatenate([jnp.sum(dh_ * xhat, axis=0, keepdims=True), jnp.sum(dh_, axis=0, keepdims=True),
                                jnp.zeros((6, D), F32)], axis=0)

        @pl.when(i == 0)
        def _():
            acc_ref[...] = rows

        @pl.when(i > 0)
        def _():
            acc_ref[...] += rows

    blk = pl.BlockSpec((tm, D), lambda i: (i, 0))
    row = pl.BlockSpec((1, D), lambda i: (0, 0))
    return pl.pallas_call(
        body, out_shape=(jax.ShapeDtypeStruct((t, D), F32), jax.ShapeDtypeStruct((8, D), F32)),
        grid=(t // tm,),
        in_specs=[pl.BlockSpec((tm, nsh * D), lambda i: (i, 0)), pl.BlockSpec(w_up.shape, lambda i: (0, 0, 0)),
                  blk, blk, row, row],
        out_specs=(blk, pl.BlockSpec((8, D), lambda i: (0, 0))),
        name="d_h1_ln1_bwd", compiler_params=_params(("arbitrary",)))(dup, w_up, dpre2, pre1, g_row, b_row)


def _d_mixin_mix_bwd(dpre1, w_out, y_ssd, y_att, u, bg_row):
    t = y_ssd.shape[0]
    tm = ROW_TM

    def body(a_ref, w_ref, ys_ref, ya_ref, g0_ref, g1_ref, b0_ref, b1_ref, dys_ref, dya_ref, du_ref, db_ref):
        i = pl.program_id(0)
        dm = _dot_nt(a_ref[...].astype(BF16), w_ref[...])
        g0 = _sigmoid(g0_ref[...] + b0_ref[...])
        g1 = _sigmoid(g1_ref[...] + b1_ref[...])
        dys_ref[...] = (dm * g0).astype(BF16)
        dya_ref[...] = (dm * g1).astype(BF16)
        dl0 = dm * ys_ref[...] * g0 * (1.0 - g0)
        dl1 = dm * ya_ref[...] * g1 * (1.0 - g1)
        du_ref[:, 0:D] = dl0.astype(BF16)
        du_ref[:, D:2 * D] = dl1.astype(BF16)
        part = jnp.concatenate([jnp.broadcast_to(jnp.sum(dl0, axis=0, keepdims=True), (8, D)),
                                jnp.broadcast_to(jnp.sum(dl1, axis=0, keepdims=True), (8, D))], axis=1)

        @pl.when(i == 0)
        def _():
            db_ref[...] = part

        @pl.when(i > 0)
        def _():
            db_ref[...] += part

    blk = pl.BlockSpec((tm, D), lambda i: (i, 0))
    return pl.pallas_call(
        body,
        out_shape=(jax.ShapeDtypeStruct((t, D), BF16), jax.ShapeDtypeStruct((t, D), BF16),
                   jax.ShapeDtypeStruct((t, UW), BF16), jax.ShapeDtypeStruct((8, 2 * D), F32)),
        grid=(t // tm,),
        in_specs=[blk, pl.BlockSpec((D, D), lambda i: (0, 0)), blk, blk,
                  pl.BlockSpec((tm, D), lambda i: (i, OGATE // D)), pl.BlockSpec((tm, D), lambda i: (i, OGATE // D + 1)),
                  pl.BlockSpec((1, D), lambda i: (0, 0)), pl.BlockSpec((1, D), lambda i: (0, 1))],
        out_specs=(blk, blk, pl.BlockSpec((tm, 2 * D), lambda i: (i, OGATE // (2 * D))),
                   pl.BlockSpec((8, 2 * D), lambda i: (0, 0))),
        name="d_mixin_mix_bwd", compiler_params=_params(("arbitrary",)))(dpre1, w_out, y_ssd, y_att, u, u, bg_row, bg_row)


def _adamw(w, g, m, v, name):
    r, c = w.shape
    tr, tc = r, c
    for cand in (256, 128, 64, 32, 16, 8):
        if r % cand == 0 and cand * c * 4 <= 2 ** 21:
            tr = cand
            break
    if tr < 64 and c % 256 == 0:
        tr, tc = r, 256
    bc1 = 1.0 / (1.0 - ADAM_B1 ** ADAM_STEP)
    bc2 = 1.0 / (1.0 - ADAM_B2 ** ADAM_STEP)

    def body(w_ref, g_ref, m_ref, v_ref, d_ref, nm_ref, nv_ref):
        gg = g_ref[...]
        nm = ADAM_B1 * m_ref[...] + (1.0 - ADAM_B1) * gg
        nv = ADAM_B2 * v_ref[...] + (1.0 - ADAM_B2) * (gg * gg)
        nm_ref[...] = nm
        nv_ref[...] = nv
        d_ref[...] = -ADAM_LR * ((nm * bc1) / (jnp.sqrt(nv * bc2) + ADAM_EPS) + ADAM_WD * w_ref[...])

    blk = pl.BlockSpec((tr, tc), lambda i, j: (i, j))
    shp = jax.ShapeDtypeStruct((r, c), F32)
    return pl.pallas_call(body, out_shape=(shp, shp, shp), grid=(r // tr, c // tc), in_specs=[blk] * 4,
                          out_specs=(blk,) * 3, name=name, compiler_params=_params(("parallel", "parallel")))(w, g, m, v)


def _perm_cols(w):
    z, xbc, dt = w[:, 0:2048], w[:, 2048:5120], w[:, 5120:5184]
    q, k, v, gate = w[:, 5184:5952], w[:, 5952:6720], w[:, 6720:7488], w[:, 7488:9536]
    kv = []
    for g in range(3):
        for p in range(2):
            lo = 256 * g + 128 * p
            kv += [k[:, lo:lo + 128], v[:, lo:lo + 128]]
    pad = jnp.zeros((w.shape[0], UW - IN_COLS), w.dtype)
    return jnp.concatenate([z, gate, xbc] + kv + [q, dt, pad], axis=1)


def _unperm_cols(wp):
    z, gate, xbc = wp[:, OZ:OZ + 2048], wp[:, OGATE:OGATE + 2048], wp[:, OXBC:OXBC + CONVD]
    q, dt = wp[:, OQ:OQ + 768], wp[:, ODT:ODT + 64]
    ks, vs = [], []
    for g in range(3):
        for p in range(2):
            lo = OKV + 128 * (4 * g + 2 * p)
            ks.append(wp[:, lo:lo + 128])
            vs.append(wp[:, lo + 128:lo + 256])
    return jnp.concatenate([z, xbc, dt, q] + ks + vs + [gate], axis=1)


def _segments():
    segs = [(0, 2048), (7488, 9536), (2048, 5120)]
    for g in range(3):
        for p in range(2):
            lo = 256 * g + 128 * p
            segs += [(5952 + lo, 5952 + lo + 128), (6720 + lo, 6720 + lo + 128)]
    segs += [(5184, 5952), (5120, 5184)]
    out, pos = [], 0
    for a, b in segs:
        out.append((a, b, pos))
        pos += b - a
    return out


SHARD_COLS = IN_COLS // 4


def _perm_from_shards(w_shards):
    pieces = []
    for a, b, _ in _segments():
        while a < b:
            s = a // SHARD_COLS
            e = min(b, (s + 1) * SHARD_COLS)
            pieces.append(w_shards[s][:, a - s * SHARD_COLS:e - s * SHARD_COLS])
            a = e
    pieces.append(jnp.zeros((w_shards.shape[1], UW - IN_COLS), w_shards.dtype))
    return jnp.concatenate(pieces, axis=1)


def _shards_from_perm(wp):
    segs = sorted(_segments())
    shards = []
    for s in range(4):
        lo, hi = s * SHARD_COLS, (s + 1) * SHARD_COLS
        pieces = []
        for a, b, pos in segs:
            x, y = max(a, lo), min(b, hi)
            if x < y:
                pieces.append(wp[:, pos + x - a:pos + y - a])
        shards.append(jnp.concatenate(pieces, axis=1))
    return jnp.stack(shards)


def _lanes128(*vecs):
    v = jnp.concatenate([a.reshape(-1) for a in vecs])
    return jnp.pad(v, (0, 128 - v.shape[0])).reshape(1, 128)


EARLY = ("w_proj_ssd", "w_proj_attn", "w_out", "w_up", "w_down")


def _local_grads(x, tgt, wts, sm, rs_idx=None):
    row = lambda a: a.reshape(1, -1)
    bg_row, cb_row = row(sm["b_gate"]), row(sm["conv_b"])
    par = jnp.concatenate([_lanes128(sm["dt_bias_f"], sm["dt_bias_b"]), _lanes128(sm["a_log_f"], sm["a_log_b"]),
                           jnp.zeros((6, 128), F32)], axis=0)
    dsk_row = row(jnp.repeat(sm["d_skip"], HP))
    nw_row = row(sm["ssd_norm_w"])
    g1, b1, g2, b2 = row(sm["ln1_g"]), row(sm["ln1_b"]), row(sm["ln2_g"]), row(sm["ln2_b"])

    xb = x.astype(BF16)
    u = _mm_nn(xb, wts["w_in_p"], tm=512, tn=2432, name="in_proj")
    xbc = _conv_fwd(u, sm["conv_w"], cb_row)
    y_f, st_f = _ssd_fwd(xbc, u, par, rev=False)
    y_fb, st_b = _ssd_fwd(xbc, u, par, y_f, rev=True)
    s_out = _gatenorm_fwd(y_fb, xbc, u, dsk_row, nw_row)
    y_ssd = _mm_nn(s_out, wts["w_proj_ssd"], tm=512, tn=1024, name="proj_ssd")
    att_o, att_l = [], []
    for g in range(3):
        o, l = _attn_fwd(u, g)
        att_o.append(o)
        att_l.append(l)
    att, y_att = _combine_proj(att_o, att_l, wts["w_proj_attn"])
    mixin, pre1, h1 = _mix_out_ln1(y_ssd, y_att, u, bg_row, x, wts["w_out"], g1, b1)
    up, act = _mlp_up(h1, wts["w_up"])
    dpre2, dpre2_b, acc2 = _mlp_down_ln2_loss(act, wts["w_down"], pre1, tgt, g1, b1, g2, b2)

    dw_down = _mm_tn(act, dpre2_b, tka=1024, tn=1024, tt=1024, name="dw_down")
    dup = _d_up(dpre2_b, wts["w_down"], up)
    dw_up = _mm_tn(h1, dup, tka=1024, tn=1024, tt=1024, name="dw_up", out_shards=4)
    dpre1, acc1 = _d_h1_ln1_bwd(dup, wts["w_up"], dpre2, pre1, g1, b1)
    dw_out = _mm_tn(mixin, dpre1, tka=1024, tn=1024, tt=1024, name="dw_out")
    dy_ssd, dy_att, du, dbg = _d_mixin_mix_bwd(dpre1, wts["w_out"], y_ssd, y_att, u, bg_row)
    dw_proj_ssd = _mm_tn(s_out, dy_ssd, tka=1024, tn=1024, tt=1024, name="dw_proj_ssd")
    ds_out = _mm_nt(dy_ssd, wts["w_proj_ssd"], tm=512, tk=1024, tc=1024, name="d_s_out")
    dw_proj_attn = _mm_tn(att, dy_att, tka=256, tn=256, tt=1024, name="dw_proj_attn", out_shards=4)
    do_g, e_g = _d_att_combine_bwd(dy_att, wts["w_proj_attn"], att_o, att_l)
    for g in range(3):
        du = _attn_dq(u, du, do_g[g], att_l[g], e_g[g], g)
        du = _attn_dkv(u, du, do_g[g], att_l[g], e_g[g], g)
    big = {
        "w_proj_ssd": dw_proj_ssd.reshape(4, DI // 4, D),
        "w_proj_attn": dw_proj_attn,
        "w_out": dw_out.reshape(4, D // 4, D),
        "w_up": dw_up,
        "w_down": dw_down.reshape(4, DFF // 4, D),
    }
    early = [big[n] for n in EARLY]
    dy, du, dnw, dds, recv = _gatenorm_bwd(ds_out, y_fb, xbc, u, du, dsk_row, nw_row,
                                           side=_swap_side(early) if rs_idx else None)
    if rs_idx:
        halves = [_add_half(g, r, rs_idx[0], f"rs_add_half_{n}") for g, r, n in zip(early, recv, EARLY)]
    dxs_f, dbc_f, ddt_f, sacc_f, recv = _ssd_bwd(xbc, u, par, dy, st_f, rev=False,
                                                 side=_step1_side([h[1] for h in halves]) if rs_idx else None)
    if rs_idx:
        k = len(EARLY)
        sums1 = [_rs_add1(h[0], ra, rb, rs_idx[1], f"rs_add1_{n}")
                 for h, ra, rb, n in zip(halves, recv[:k], recv[k:], EARLY)]
    dxs, dbc, ddt, sacc_b, recv = _ssd_bwd(
        xbc, u, par, dy, st_b, rev=True, add=(dxs_f, dbc_f, ddt_f),
        side=_step2_side([s1[2] for s1 in sums1], [s1[3] for s1 in sums1]) if rs_idx else None)
    pieces = None
    if rs_idx:
        pieces = {n: _rs_add2(s1[0], s1[1], ra, rb, rs_idx[1], f"rs_add2_{n}")
                  for s1, ra, rb, n in zip(sums1, recv[:k], recv[k:], EARLY)}
    dpre_c, dcw, dcb = _conv_dpre(u, dxs, dy, dbc, dsk_row, sm["conv_w"], cb_row)
    du = _conv_dx(du, dpre_c, sm["conv_w"])
    du = _dt_bwd(du, ddt)
    dw_in_p = _mm_tn(xb, du, tka=1024, tn=2432, tt=1024, name="dw_in")
    dx = _mm_nt(du, wts["w_in_p"], tm=1024, tk=1024, tc=2432, name="d_x", add=dpre1, add_scale=ALPHA)

    sacc = sacc_f + sacc_b
    small = {
        "b_gate": dbg[0], "conv_w": dcw[0:KCONV], "conv_b": dcb[0],
        "dt_bias_f": sacc[0, 0:32], "dt_bias_b": sacc[0, 32:64], "a_log_f": sacc[1, 0:32], "a_log_b": sacc[1, 32:64],
        "d_skip": dds[0, 0:32], "ssd_norm_w": dnw[0],
        "ln1_g": acc1[0], "ln1_b": acc1[1], "ln2_g": acc2[0], "ln2_b": acc2[1], "loss": acc2[2, 0:1],
    }
    big["w_in"] = _shards_from_perm(dw_in_p)
    return dx, big, small, pieces


HBM_SPEC = pl.BlockSpec(memory_space=pl.ANY)


def _place():
    x, y, c = lax.axis_index("x"), lax.axis_index("y"), lax.axis_index("c")
    chips = [(1 - x, y), (x, 1 - y), (1 - x, 1 - y)]
    return x, y, c, chips


def _allgather_weights(shards):
    n = len(shards)

    def body(*refs):
        ins, outs = refs[:n], refs[n:2 * n]
        send_sems, recv_sems = refs[2 * n:]
        x, y, c, _ = _place()
        q, q_x, q_y, q_d = 2 * x + y, 2 * (1 - x) + y, 2 * x + 1 - y, 2 * (1 - x) + 1 - y
        x_nbr, y_nbr, sibling = (1 - x, y, c), (x, 1 - y, c), (x, y, 1 - c)

        def copy(w, k, src, dst, to):
            return pltpu.make_async_remote_copy(src_ref=src, dst_ref=dst, send_sem=send_sems.at[w, k],
                                                recv_sem=recv_sems.at[w, k], device_id=to, device_id_type=MESH)

        def rows(w, core, part):
            rh = ins[w].shape[0] // 2
            if part is None:
                return pl.ds(core * rh, rh)
            return pl.ds(core * rh + part * (rh // 2), rh // 2)

        def same(w, k, slot, core, part, to):
            blk = outs[w].at[slot, rows(w, core, part), :]
            return copy(w, k, blk, blk, to)

        started = []
        for w in range(n):
            cp = copy(w, 8, ins[w], outs[w].at[q], sibling)
            cp.start()
            started.append(cp)
            mine = rows(w, c, None)
            for k, to in ((0, x_nbr), (1, y_nbr)):
                cp = copy(w, k, ins[w].at[mine, :], outs[w].at[q, mine, :], to)
                cp.start()
                started.append(cp)
        for w in range(n):
            same(w, 0, q_x, c, None, x_nbr).wait_recv()
            for cp in (same(w, 2, q_x, c, 0, y_nbr), same(w, 4, q_x, c, None, sibling)):
                cp.start()
                started.append(cp)
            same(w, 1, q_y, c, None, y_nbr).wait_recv()
            for cp in (same(w, 3, q_y, c, 1, x_nbr), same(w, 5, q_y, c, None, sibling)):
                cp.start()
                started.append(cp)
        for w in range(n):
            same(w, 2, q_d, c, 0, y_nbr).wait_recv()
            cp = same(w, 6, q_d, c, 0, sibling)
            cp.start()
            started.append(cp)
            same(w, 3, q_d, c, 1, x_nbr).wait_recv()
            cp = same(w, 7, q_d, c, 1, sibling)
            cp.start()
            started.append(cp)
        for w in range(n):
            same(w, 4, q_x, 1 - c, None, sibling).wait_recv()
            same(w, 5, q_y, 1 - c, None, sibling).wait_recv()
            same(w, 6, q_d, 1 - c, 0, sibling).wait_recv()
            same(w, 7, q_d, 1 - c, 1, sibling).wait_recv()
            copy(w, 8, ins[w], outs[w].at[q], sibling).wait_recv()
        for cp in started:
            cp.wait_send()

    return pl.pallas_call(
        body, out_shape=[jax.ShapeDtypeStruct((4,) + s.shape, s.dtype) for s in shards],
        in_specs=[HBM_SPEC] * n, out_specs=[HBM_SPEC] * n,
        scratch_shapes=[pltpu.SemaphoreType.DMA((n, 9)), pltpu.SemaphoreType.DMA((n, 9))],
        name="allgather_weights")(*shards)


def _swap_halves(grads):
    n = len(grads)

    def body(*refs):
        ins, outs = refs[:n], refs[n:2 * n]
        send_sems, recv_sems = refs[2 * n:]
        x, y, c, _ = _place()
        copies = []
        for w in range(n):
            rh = ins[w].shape[1] // 2
            for p in range(4):
                cp = pltpu.make_async_remote_copy(
                    src_ref=ins[w].at[p, pl.ds((1 - c) * rh, rh), :], dst_ref=outs[w].at[p],
                    send_sem=send_sems.at[w, p], recv_sem=recv_sems.at[w, p],
                    device_id=(x, y, 1 - c), device_id_type=MESH)
                cp.start()
                copies.append(cp)
        for cp in copies:
            cp.wait()

    return pl.pallas_call(
        body, out_shape=[jax.ShapeDtypeStruct((4, g.shape[1] // 2, g.shape[2]), F32) for g in grads],
        in_specs=[HBM_SPEC] * n, out_specs=[HBM_SPEC] * n,
        scratch_shapes=[pltpu.SemaphoreType.DMA((n, 4)), pltpu.SemaphoreType.DMA((n, 4))],
        name="rs_swap_halves")(*grads)


def _rs_step1(parts):
    n = len(parts)

    def body(*refs):
        ins, out_a, out_b = refs[:n], refs[n:2 * n], refs[2 * n:3 * n]
        send_sems, recv_sems = refs[3 * n:]
        x, y, c, _ = _place()
        copies = []
        for w in range(n):
            rq = ins[w].shape[1] // 2
            for i in range(2):
                copies.append(pltpu.make_async_remote_copy(
                    src_ref=ins[w].at[2 * (1 - x) + i, pl.ds(0, rq), :], dst_ref=out_a[w].at[i],
                    send_sem=send_sems.at[w, i], recv_sem=recv_sems.at[w, i],
                    device_id=(1 - x, y, c), device_id_type=MESH))
                copies.append(pltpu.make_async_remote_copy(
                    src_ref=ins[w].at[2 * i + 1 - y, pl.ds(rq, rq), :], dst_ref=out_b[w].at[i],
                    send_sem=send_sems.at[w, 2 + i], recv_sem=recv_sems.at[w, 2 + i],
                    device_id=(x, 1 - y, c), device_id_type=MESH))
        for cp in copies:
            cp.start()
        for cp in copies:
            cp.wait()

    quarter = lambda p: jax.ShapeDtypeStruct((2, p.shape[1] // 2, p.shape[2]), p.dtype)
    outs = pl.pallas_call(
        body, out_shape=[quarter(p) for p in parts] * 2,
        in_specs=[HBM_SPEC] * n, out_specs=[HBM_SPEC] * (2 * n),
        scratch_shapes=[pltpu.SemaphoreType.DMA((n, 4)), pltpu.SemaphoreType.DMA((n, 4))],
        name="rs_step1")(*parts)
    return outs[:n], outs[n:]


def _rs_step2(tas, tbs):
    n = len(tas)

    def body(*refs):
        in_a, in_b, out_a, out_b = refs[:n], refs[n:2 * n], refs[2 * n:3 * n], refs[3 * n:4 * n]
        send_sems, recv_sems = refs[4 * n:]
        x, y, c, _ = _place()
        copies = []
        for w in range(n):
            copies.append(pltpu.make_async_remote_copy(
                src_ref=in_a[w].at[1 - y], dst_ref=out_a[w], send_sem=send_sems.at[w, 0], recv_sem=recv_sems.at[w, 0],
                device_id=(x, 1 - y, c), device_id_type=MESH))
            copies.append(pltpu.make_async_remote_copy(
                src_ref=in_b[w].at[1 - x], dst_ref=out_b[w], send_sem=send_sems.at[w, 1], recv_sem=recv_sems.at[w, 1],
                device_id=(1 - x, y, c), device_id_type=MESH))
        for cp in copies:
            cp.start()
        for cp in copies:
            cp.wait()

    one = lambda p: jax.ShapeDtypeStruct(p.shape[1:], p.dtype)
    outs = pl.pallas_call(
        body, out_shape=[one(p) for p in tas] + [one(p) for p in tbs],
        in_specs=[HBM_SPEC] * (2 * n), out_specs=[HBM_SPEC] * (2 * n),
        scratch_shapes=[pltpu.SemaphoreType.DMA((n, 2)), pltpu.SemaphoreType.DMA((n, 2))],
        name="rs_step2")(*tas, *tbs)
    return outs[:n], outs[n:]


class _Side(NamedTuple):
    ins: tuple
    out_shapes: tuple
    nsem: tuple
    make: Callable


def _swap_copies(ins, outs, send_sems, recv_sems):
    x, y, c, _ = _place()
    copies = []
    for w in range(len(ins)):
        rh = ins[w].shape[1] // 2
        for p in range(4):
            copies.append(pltpu.make_async_remote_copy(
                src_ref=ins[w].at[p, pl.ds((1 - c) * rh, rh), :], dst_ref=outs[w].at[p],
                send_sem=send_sems.at[w, p], recv_sem=recv_sems.at[w, p],
                device_id=(x, y, 1 - c), device_id_type=MESH))
    return copies


def _swap_side(grads):
    shapes = tuple(jax.ShapeDtypeStruct((4, g.shape[1] // 2, g.shape[2]), F32) for g in grads)
    return _Side(tuple(grads), shapes, (len(grads), 4), _swap_copies)


def _step1_copies(ins, outs, send_sems, recv_sems):
    n = len(ins)
    out_a, out_b = outs[:n], outs[n:]
    x, y, c, _ = _place()
    copies = []
    for w in range(n):
        rq = ins[w].shape[1] // 2
        for i in range(2):
            copies.append(pltpu.make_async_remote_copy(
                src_ref=ins[w].at[2 * (1 - x) + i, pl.ds(0, rq), :], dst_ref=out_a[w].at[i],
                send_sem=send_sems.at[w, i], recv_sem=recv_sems.at[w, i],
                device_id=(1 - x, y, c), device_id_type=MESH))
            copies.append(pltpu.make_async_remote_copy(
                src_ref=ins[w].at[2 * i + 1 - y, pl.ds(rq, rq), :], dst_ref=out_b[w].at[i],
                send_sem=send_sems.at[w, 2 + i], recv_sem=recv_sems.at[w, 2 + i],
                device_id=(x, 1 - y, c), device_id_type=MESH))
    return copies


def _step1_side(parts):
    quarter = tuple(jax.ShapeDtypeStruct((2, p.shape[1] // 2, p.shape[2]), p.dtype) for p in parts)
    return _Side(tuple(parts), quarter + quarter, (len(parts), 4), _step1_copies)


def _step2_copies(ins, outs, send_sems, recv_sems):
    n = len(ins) // 2
    in_a, in_b, out_a, out_b = ins[:n], ins[n:], outs[:n], outs[n:]
    x, y, c, _ = _place()
    copies = []
    for w in range(n):
        copies.append(pltpu.make_async_remote_copy(
            src_ref=in_a[w].at[1 - y], dst_ref=out_a[w], send_sem=send_sems.at[w, 0], recv_sem=recv_sems.at[w, 0],
            device_id=(x, 1 - y, c), device_id_type=MESH))
        copies.append(pltpu.make_async_remote_copy(
            src_ref=in_b[w].at[1 - x], dst_ref=out_b[w], send_sem=send_sems.at[w, 1], recv_sem=recv_sems.at[w, 1],
            device_id=(1 - x, y, c), device_id_type=MESH))
    return copies


def _step2_side(tas, tbs):
    one = tuple(jax.ShapeDtypeStruct(p.shape[1:], p.dtype) for p in tuple(tas) + tuple(tbs))
    return _Side(tuple(tas) + tuple(tbs), one, (len(tas), 2), _step2_copies)


def _run_side(side, name):
    n_in, n_out = len(side.ins), len(side.out_shapes)

    def body(*refs):
        copies = side.make(refs[:n_in], refs[n_in:n_in + n_out], *refs[n_in + n_out:])
        for cp in copies:
            cp.start()
        for cp in copies:
            cp.wait()

    return pl.pallas_call(
        body, out_shape=list(side.out_shapes), in_specs=[HBM_SPEC] * n_in, out_specs=[HBM_SPEC] * n_out,
        scratch_shapes=[pltpu.SemaphoreType.DMA(side.nsem), pltpu.SemaphoreType.DMA(side.nsem)], name=name)(*side.ins)


def _host_call(body, side, n_steps, *, out_shape, in_specs, out_specs, scratch_shapes, args, aliases, name, sem):
    n_in, n_out, n_scr = len(in_specs), len(out_shape), len(scratch_shapes)
    if side is None:
        outs = pl.pallas_call(body, out_shape=tuple(out_shape), grid=(n_steps,), in_specs=list(in_specs),
                              out_specs=tuple(out_specs), scratch_shapes=list(scratch_shapes),
                              input_output_aliases=aliases, name=name, compiler_params=_params(sem))(*args)
        return tuple(outs), ()
    ns_in, ns_out = len(side.ins), len(side.out_shapes)

    def wrapped(*refs):
        h_in, s_in = refs[:n_in], refs[n_in:n_in + ns_in]
        o0 = n_in + ns_in
        h_out, s_out = refs[o0:o0 + n_out], refs[o0 + n_out:o0 + n_out + ns_out]
        c0 = o0 + n_out + ns_out
        h_scr, sems = refs[c0:c0 + n_scr], refs[c0 + n_scr:]
        step = pl.program_id(0)

        @pl.when(step == 0)
        def _():
            for cp in side.make(s_in, s_out, *sems):
                cp.start()

        body(*h_in, *h_out, *h_scr)

        @pl.when(step == n_steps - 1)
        def _():
            for cp in side.make(s_in, s_out, *sems):
                cp.wait()

    outs = pl.pallas_call(
        wrapped, out_shape=tuple(out_shape) + tuple(side.out_shapes), grid=(n_steps,),
        in_specs=list(in_specs) + [HBM_SPEC] * ns_in, out_specs=tuple(out_specs) + (HBM_SPEC,) * ns_out,
        scratch_shapes=list(scratch_shapes) + [pltpu.SemaphoreType.DMA(side.nsem), pltpu.SemaphoreType.DMA(side.nsem)],
        input_output_aliases=aliases, name=name, compiler_params=_params(sem))(*args, *side.ins)
    return tuple(outs[:n_out]), tuple(outs[n_out:])


def _join_halves(pieces):
    n = len(pieces)

    def body(*refs):
        outs = refs[n:2 * n]
        send_sems, recv_sems = refs[2 * n:]
        x, y, c, _ = _place()

        def copy(w, slot):
            return pltpu.make_async_remote_copy(
                src_ref=outs[w].at[slot], dst_ref=outs[w].at[slot], send_sem=send_sems.at[w], recv_sem=recv_sems.at[w],
                device_id=(x, y, 1 - c), device_id_type=MESH)

        for w in range(n):
            copy(w, c).start()
        for w in range(n):
            copy(w, 1 - c).wait_recv()
            copy(w, c).wait_send()

    return pl.pallas_call(
        body, out_shape=[jax.ShapeDtypeStruct(p.shape, F32) for p in pieces],
        in_specs=[HBM_SPEC] * n, out_specs=[HBM_SPEC] * n, input_output_aliases={w: w for w in range(n)},
        scratch_shapes=[pltpu.SemaphoreType.DMA((n,)), pltpu.SemaphoreType.DMA((n,))],
        name="rs_join_halves")(*pieces)


def _add_tile_rows(rh, c):
    for cand in (512, 256, 128, 64, 32, 16, 8):
        if rh % cand == 0 and cand * c * 4 <= 2 ** 21:
            return cand
    return rh


def _add_half(grad, recv, c_idx, name):
    _, r, cc = grad.shape
    rh = r // 2
    tr = _add_tile_rows(rh, cc)
    nb = rh // tr

    def body(c_ref, g_ref, r_ref, o_ref, ob_ref):
        del c_ref
        s = g_ref[...] + r_ref[...]
        o_ref[...] = s
        ob_ref[...] = s.astype(BF16)

    blk = pl.BlockSpec((None, tr, cc), lambda p, i, c_ref: (p, i, 0))
    grid_spec = pltpu.PrefetchScalarGridSpec(
        num_scalar_prefetch=1, grid=(4, nb),
        in_specs=[pl.BlockSpec((None, tr, cc), lambda p, i, c_ref: (p, c_ref[0] * nb + i, 0)), blk],
        out_specs=(blk, blk))
    return pl.pallas_call(
        body, out_shape=(jax.ShapeDtypeStruct((4, rh, cc), F32), jax.ShapeDtypeStruct((4, rh, cc), BF16)),
        grid_spec=grid_spec, name=name, compiler_params=_params(("parallel", "parallel")))(c_idx, grad, recv)


def _rs_add1(part, recv_a, recv_b, xy_idx, name):
    _, rh, cc = part.shape
    rq = rh // 2
    tr = _add_tile_rows(rq, cc)
    nb = rq // tr

    def body(xy_ref, pa_ref, pb_ref, ra_ref, rb_ref, ta_ref, tb_ref, tab_ref, tbb_ref):
        del xy_ref
        ta = pa_ref[...] + ra_ref[...].astype(F32)
        tb = pb_ref[...] + rb_ref[...].astype(F32)
        ta_ref[...] = ta
        tb_ref[...] = tb
        tab_ref[...] = ta.astype(BF16)
        tbb_ref[...] = tb.astype(BF16)

    blk = pl.BlockSpec((None, tr, cc), lambda i, j, xy: (i, j, 0))
    grid_spec = pltpu.PrefetchScalarGridSpec(
        num_scalar_prefetch=1, grid=(2, nb),
        in_specs=[pl.BlockSpec((None, tr, cc), lambda i, j, xy: (2 * xy[0] + i, j, 0)),
                  pl.BlockSpec((None, tr, cc), lambda i, j, xy: (2 * i + xy[1], nb + j, 0)), blk, blk],
        out_specs=(blk, blk, blk, blk))
    f32s, b16s = jax.ShapeDtypeStruct((2, rq, cc), F32), jax.ShapeDtypeStruct((2, rq, cc), BF16)
    return pl.pallas_call(body, out_shape=(f32s, f32s, b16s, b16s), grid_spec=grid_spec, name=name,
                          compiler_params=_params(("parallel", "parallel")))(xy_idx, part, part, recv_a, recv_b)


def _rs_add2(ta, tb, recv_a, recv_b, xy_idx, name):
    _, rq, cc = ta.shape
    tr = _add_tile_rows(rq, cc)
    nb = rq // tr

    def body(xy_ref, ta_ref, tb_ref, ra_ref, rb_ref, o_ref):
        del xy_ref
        s = pl.program_id(0)
        fa = ta_ref[...] + ra_ref[...].astype(F32)
        fb = tb_ref[...] + rb_ref[...].astype(F32)
        o_ref[...] = jnp.where(s == 0, fa, fb)

    rblk = pl.BlockSpec((tr, cc), lambda s, j, xy: (j, 0))
    grid_spec = pltpu.PrefetchScalarGridSpec(
        num_scalar_prefetch=1, grid=(2, nb),
        in_specs=[pl.BlockSpec((None, tr, cc), lambda s, j, xy: (xy[1], j, 0)),
                  pl.BlockSpec((None, tr, cc), lambda s, j, xy: (xy[0], j, 0)), rblk, rblk],
        out_specs=pl.BlockSpec((None, tr, cc), lambda s, j, xy: (xy[2], s * nb + j, 0)))
    return pl.pallas_call(body, out_shape=jax.ShapeDtypeStruct((2, 2 * rq, cc), F32), grid_spec=grid_spec, name=name,
                          compiler_params=_params(("parallel", "parallel")))(xy_idx, ta, tb, recv_a, recv_b)


def _allreduce_small(slab):
    r = slab.shape[0]

    def body(x_ref, o_ref, buf, send_sems, recv_sems):
        x, y, c, _ = _place()
        me = 4 * x + 2 * y + c
        buf[me] = x_ref[...]
        peers = []
        for k in range(1, 8):
            kx, ky, kc = (k >> 2) & 1, (k >> 1) & 1, k & 1
            peers.append((x + kx - 2 * x * kx, y + ky - 2 * y * ky, c + kc - 2 * c * kc))

        def copy(k, slot):
            return pltpu.make_async_remote_copy(src_ref=x_ref, dst_ref=buf.at[slot], send_sem=send_sems.at[k],
                                                recv_sem=recv_sems.at[k], device_id=peers[k], device_id_type=MESH)

        for k in range(7):
            copy(k, me).start()
        for k, (px, py, pc) in enumerate(peers):
            copy(k, 4 * px + 2 * py + pc).wait_recv()
        for k in range(7):
            copy(k, me).wait_send()
        acc = buf[0]
        for j in range(1, 8):
            acc = acc + buf[j]
        o_ref[...] = acc

    vm = pl.BlockSpec(memory_space=pltpu.VMEM)
    return pl.pallas_call(
        body, out_shape=jax.ShapeDtypeStruct((r, 128), F32), in_specs=[vm], out_specs=vm,
        scratch_shapes=[pltpu.VMEM((8, r, 128), F32), pltpu.SemaphoreType.DMA((7,)), pltpu.SemaphoreType.DMA((7,))],
        name="allreduce_small")(slab)


def _pack(arrs):
    rows = []
    for a in arrs:
        v = a.reshape(-1)
        v = jnp.pad(v, (0, (-v.shape[0]) % 128))
        rows.append(v.reshape(-1, 128))
    slab = jnp.concatenate(rows, axis=0)
    return jnp.pad(slab, ((0, (-slab.shape[0]) % 8), (0, 0)))


def _unpack(slab, shapes):
    out, r0 = [], 0
    for shp in shapes:
        size = math.prod(shp)
        nr = -(-size // 128)
        out.append(slab[r0:r0 + nr].reshape(-1)[:size].reshape(shp))
        r0 += nr
    return out


BIG = ("w_in", "w_proj_ssd", "w_proj_attn", "w_out", "w_up", "w_down")
SMALL = ("b_gate", "conv_w", "conv_b", "dt_bias_f", "dt_bias_b", "a_log_f", "a_log_b", "d_skip", "ssd_norm_w",
         "ln1_g", "ln1_b", "ln2_g", "ln2_b")
ORDER = ("w_in", "b_gate", "conv_w", "conv_b", "dt_bias_f", "dt_bias_b", "a_log_f", "a_log_b", "d_skip", "ssd_norm_w",
         "w_proj_ssd", "w_proj_attn", "w_out", "ln1_g", "ln1_b", "w_up", "w_down", "ln2_g", "ln2_b")


def kernel(x, w_in, b_gate, conv_w, conv_b, dt_bias_f, dt_bias_b, a_log_f, a_log_b, d_skip, ssd_norm_w, w_proj_ssd, w_proj_attn, w_out, ln1_g, ln1_b, w_up, w_down, ln2_g, ln2_b, loss_target, m_w_in, m_b_gate, m_conv_w, m_conv_b, m_dt_bias_f, m_dt_bias_b, m_a_log_f, m_a_log_b, m_d_skip, m_ssd_norm_w, m_w_proj_ssd, m_w_proj_attn, m_w_out, m_ln1_g, m_ln1_b, m_w_up, m_w_down, m_ln2_g, m_ln2_b, v_w_in, v_b_gate, v_conv_w, v_conv_b, v_dt_bias_f, v_dt_bias_b, v_a_log_f, v_a_log_b, v_d_skip, v_ssd_norm_w, v_w_proj_ssd, v_w_proj_attn, v_w_out, v_ln1_g, v_ln1_b, v_w_up, v_w_down, v_ln2_g, v_ln2_b):
    w = dict(w_in=w_in, b_gate=b_gate, conv_w=conv_w, conv_b=conv_b, dt_bias_f=dt_bias_f, dt_bias_b=dt_bias_b,
             a_log_f=a_log_f, a_log_b=a_log_b, d_skip=d_skip, ssd_norm_w=ssd_norm_w, w_proj_ssd=w_proj_ssd,
             w_proj_attn=w_proj_attn, w_out=w_out, ln1_g=ln1_g, ln1_b=ln1_b, w_up=w_up, w_down=w_down, ln2_g=ln2_g, ln2_b=ln2_b)
    m = dict(w_in=m_w_in, b_gate=m_b_gate, conv_w=m_conv_w, conv_b=m_conv_b, dt_bias_f=m_dt_bias_f, dt_bias_b=m_dt_bias_b,
             a_log_f=m_a_log_f, a_log_b=m_a_log_b, d_skip=m_d_skip, ssd_norm_w=m_ssd_norm_w, w_proj_ssd=m_w_proj_ssd,
             w_proj_attn=m_w_proj_attn, w_out=m_w_out, ln1_g=m_ln1_g, ln1_b=m_ln1_b, w_up=m_w_up, w_down=m_w_down,
             ln2_g=m_ln2_g, ln2_b=m_ln2_b)
    v = dict(w_in=v_w_in, b_gate=v_b_gate, conv_w=v_conv_w, conv_b=v_conv_b, dt_bias_f=v_dt_bias_f, dt_bias_b=v_dt_bias_b,
             a_log_f=v_a_log_f, a_log_b=v_a_log_b, d_skip=v_d_skip, ssd_norm_w=v_ssd_norm_w, w_proj_ssd=v_w_proj_ssd,
             w_proj_attn=v_w_proj_attn, w_out=v_w_out, ln1_g=v_ln1_g, ln1_b=v_ln1_b, w_up=v_w_up, w_down=v_w_down,
             ln2_g=v_ln2_g, ln2_b=v_ln2_b)
    xi, yi, ci = lax.axis_index("x"), lax.axis_index("y"), lax.axis_index("c")
    shard = 2 * xi + yi

    g_in, g_ps, g_pa, g_o, g_up, g_dn = _allgather_weights([w[n].astype(BF16) for n in BIG])
    wts = {"w_in_p": _perm_from_shards(g_in),"w_proj_ssd": g_ps.reshape(DI, D), "w_proj_attn": g_pa,
           "w_out": g_o.reshape(D, D), "w_up": g_up, "w_down": g_dn.reshape(DFF, D)}

    cw_slab = jnp.zeros((KCONV, 4, CONVD // 4), F32)
    cw_slab = lax.dynamic_update_slice(cw_slab, conv_w[:, None, :] * 0.5, (0, shard, 0))
    conv_w_all = _unpack(_allreduce_small(_pack([cw_slab])), [(KCONV, CONVD)])[0]

    sm = {n: w[n] for n in SMALL}
    sm["conv_w"] = conv_w_all
    c_idx = jnp.reshape(ci, (1,)).astype(jnp.int32)
    xy_idx = jnp.stack([xi, yi, ci]).astype(jnp.int32)
    dx, big, small, pieces = _local_grads(x[0], loss_target[0], wts, sm, rs_idx=(c_idx, xy_idx))

    names = list(SMALL) + ["loss"]
    shapes = [small[n].shape for n in names]
    red = dict(zip(names, _unpack(_allreduce_small(_pack([small[n] for n in names])), shapes)))
    loss = red["loss"].reshape(())
    gsm = {n: red[n] for n in SMALL}
    conv_w_grad_shard = lax.dynamic_slice_in_dim(gsm["conv_w"].reshape(KCONV, 4, CONVD // 4), shard, 1, axis=1)
    gsm["conv_w"] = conv_w_grad_shard.reshape(KCONV, CONVD // 4)

    g = big["w_in"]
    half = _add_half(g, _run_side(_swap_side([g]), "rs_swap_halves")[0], c_idx, "rs_add_half_w_in")
    ra, rb = _run_side(_step1_side([half[1]]), "rs_step1")
    s1 = _rs_add1(half[0], ra, rb, xy_idx, "rs_add1_w_in")
    ra2, rb2 = _run_side(_step2_side([s1[2]], [s1[3]]), "rs_step2")
    pieces["w_in"] = _rs_add2(s1[0], s1[1], ra2, rb2, xy_idx, "rs_add2_w_in")
    joined = _join_halves([pieces[n] for n in BIG])
    gbig = {n: j.reshape(w[n].shape) for n, j in zip(BIG, joined)}

    grads, deltas, new_m, new_v = {}, {}, {}, {}
    for n in BIG:
        grads[n] = gbig[n]
        if n == "w_in":
            gt = gbig[n].T
            dlt, nmt, nvt = _adamw(w[n].T, gt, m[n].T, v[n].T, f"adamw_{n}")
            grads[n], deltas[n], new_m[n], new_v[n] = gt.T, dlt.T, nmt.T, nvt.T
            continue
        deltas[n], new_m[n], new_v[n] = _adamw(w[n], gbig[n], m[n], v[n], f"adamw_{n}")
    sshapes = [w[n].shape for n in SMALL]
    d_s, m_s, v_s = _adamw(_pack([w[n] for n in SMALL]), _pack([gsm[n] for n in SMALL]),
                           _pack([m[n] for n in SMALL]), _pack([v[n] for n in SMALL]), "adamw_small")
    for n, dd, mm, vv in zip(SMALL, _unpack(d_s, sshapes), _unpack(m_s, sshapes), _unpack(v_s, sshapes)):
        grads[n], deltas[n], new_m[n], new_v[n] = gsm[n], dd, mm, vv

    return (loss, dx[None], *[grads[n] for n in ORDER], *[deltas[n] for n in ORDER],
            *[new_m[n] for n in ORDER], *[new_v[n] for n in ORDER])
```

```python
import functools
import math
from typing import Callable, NamedTuple

import jax
import numpy as np
import jax.numpy as jnp
from jax import lax
from jax.experimental import pallas as pl
from jax.experimental.pallas import tpu as pltpu

F32, BF16 = jnp.float32, jnp.bfloat16
MESH = pl.DeviceIdType.MESH

D = 1024
DI = 2048
NH = 32
HP = 64
NG = 4
NS = 128
Q = 128
CONVD = 3072
KCONV = 5
DFF = 4096
AH = 64
ATT_HALF = 64
DILATIONS = (1, 4, 16)
IN_COLS = 9536
OZ, OGATE, OXBC, OKV, OQ, ODT, UW = 0, 2048, 4096, 7168, 8704, 9472, 9728
ALPHA = 2.0 ** 0.25
NORM_EPS = 1e-5
ADAM_LR, ADAM_B1, ADAM_B2, ADAM_EPS, ADAM_WD, ADAM_STEP = 0.001, 0.9, 0.999, 1e-8, 0.01, 10
VMEM_LIMIT = 56 * 2 ** 20
NEG = -1e30


def _params(sem):
    return pltpu.CompilerParams(dimension_semantics=sem, vmem_limit_bytes=VMEM_LIMIT)


def _sigmoid(x):
    return 1.0 / (1.0 + jnp.exp(-x))


def _softplus(x):
    e = jnp.exp(-jnp.abs(x))
    small = e * (1.0 - e * (0.5 - e * (1.0 / 3.0)))
    return jnp.maximum(x, 0.0) + jnp.where(e < 0.01, small, jnp.log(1.0 + e))


def _split3(a):
    hi = a.astype(BF16)
    r = a - hi.astype(F32)
    mid = r.astype(BF16)
    lo = (r - mid.astype(F32)).astype(BF16)
    return hi, mid, lo


def _dot01(a, m01):
    hi, mid, lo = _split3(a)
    d = lambda p: jnp.dot(p, m01, preferred_element_type=F32)
    return d(hi) + d(mid) + d(lo)


def _dot01_l(m01, a):
    hi, mid, lo = _split3(a)
    d = lambda p: jnp.dot(m01, p, preferred_element_type=F32)
    return d(hi) + d(mid) + d(lo)


def _dot_nt(a, b):
    return lax.dot_general(a, b, (((1,), (1,)), ((), ())), preferred_element_type=F32)


def _iota(shape, dim):
    return lax.broadcasted_iota(jnp.int32, shape, dim)


def _mm_nn(a, b, *, tm, tn, name, out_dtype=F32):
    m, k = a.shape
    if b.ndim == 3:
        assert tn == b.shape[2]
        n = b.shape[0] * b.shape[2]
        b_spec = pl.BlockSpec((None, k, tn), lambda j, i: (j, 0, 0))
    else:
        n = b.shape[1]
        b_spec = pl.BlockSpec((k, tn), lambda j, i: (0, j))

    def body(a_ref, b_ref, o_ref):
        o_ref[...] = jnp.dot(a_ref[...].astype(BF16), b_ref[...], preferred_element_type=F32).astype(out_dtype)

    return pl.pallas_call(
        body, out_shape=jax.ShapeDtypeStruct((m, n), out_dtype), grid=(n // tn, m // tm),
        in_specs=[pl.BlockSpec((tm, k), lambda j, i: (i, 0)), b_spec],
        out_specs=pl.BlockSpec((tm, tn), lambda j, i: (i, j)),
        name=name, compiler_params=_params(("parallel", "parallel")))(a, b)


def _mm_nt(a, b, *, tm, tk, tc, name, add=None, add_scale=1.0):
    m, n = a.shape
    if b.ndim == 3:
        assert tc == b.shape[2]
        k, nc = b.shape[1], b.shape[0]
        b_spec = pl.BlockSpec((None, tk, tc), lambda j, i, c: (c, j, 0))
    else:
        k, nc = b.shape[0], n // tc
        b_spec = pl.BlockSpec((tk, tc), lambda j, i, c: (j, c))

    def body(*refs):
        if add is None:
            a_ref, b_ref, o_ref = refs
        else:
            a_ref, b_ref, add_ref, o_ref = refs
        c = pl.program_id(2)
        part = _dot_nt(a_ref[...].astype(BF16), b_ref[...])

        @pl.when(c == 0)
        def _():
            if add is None:
                o_ref[...] = part
            else:
                o_ref[...] = part + add_scale * add_ref[...]

        @pl.when(c > 0)
        def _():
            o_ref[...] += part

    in_specs = [pl.BlockSpec((tm, tc), lambda j, i, c: (i, c)), b_spec]
    args = [a, b]
    if add is not None:
        in_specs.append(pl.BlockSpec((tm, tk), lambda j, i, c: (i, j)))
        args.append(add)
    return pl.pallas_call(
        body, out_shape=jax.ShapeDtypeStruct((m, k), F32), grid=(k // tk, m // tm, nc),
        in_specs=in_specs, out_specs=pl.BlockSpec((tm, tk), lambda j, i, c: (i, j)),
        name=name, compiler_params=_params(("parallel", "parallel", "arbitrary")))(*args)


def _mm_tn(a, b, *, tka, tn, tt, name, out_shards=None):
    t, ka = a.shape
    n = b.shape[1]
    if out_shards:
        assert tn == n // out_shards
        out_shape = jax.ShapeDtypeStruct((out_shards, ka, tn), F32)
        o_spec = pl.BlockSpec((None, tka, tn), lambda i, j, s: (j, i, 0))
    else:
        out_shape = jax.ShapeDtypeStruct((ka, n), F32)
        o_spec = pl.BlockSpec((tka, tn), lambda i, j, s: (i, j))

    def body(a_ref, b_ref, o_ref):
        s = pl.program_id(2)
        part = lax.dot_general(a_ref[...].astype(BF16), b_ref[...].astype(BF16), (((0,), (0,)), ((), ())),
                               preferred_element_type=F32)

        @pl.when(s == 0)
        def _():
            o_ref[...] = part

        @pl.when(s > 0)
        def _():
            o_ref[...] += part

    return pl.pallas_call(
        body, out_shape=out_shape, grid=(ka // tka, n // tn, t // tt),
        in_specs=[pl.BlockSpec((tt, tka), lambda i, j, s: (s, i)), pl.BlockSpec((tt, tn), lambda i, j, s: (s, j))],
        out_specs=o_spec, name=name, compiler_params=_params(("parallel", "parallel", "arbitrary")))(a, b)


def _in_proj(xb, w_in_p, side=None):
    t, k = xb.shape
    tm, tn = 512, 2432
    nm, nn = t // tm, UW // tn

    def body(a_ref, b_ref, o_ref):
        o_ref[...] = jnp.dot(a_ref[...], b_ref[...], preferred_element_type=F32)

    outs, side_outs = _host_call(
        body, side, nm * nn, out_shape=(jax.ShapeDtypeStruct((t, UW), F32),),
        in_specs=[pl.BlockSpec((tm, k), lambda s: (s % nm, 0)), pl.BlockSpec((k, tn), lambda s: (0, s // nm))],
        out_specs=(pl.BlockSpec((tm, tn), lambda s: (s % nm, s // nm)),),
        scratch_shapes=[], args=(xb, w_in_p), aliases={}, name="in_proj", sem=("arbitrary",))
    return outs[0], side_outs


CONV_TM = 512
CONV_TC = 1024
CONV_RC = 64
CONV_CC = 256


def _halo_specs(t, tm, tc, col0):
    nb8 = t // 8
    r8 = tm // 8
    return [
        pl.BlockSpec((8, tc), lambda i, j: (jnp.maximum(i * r8 - 1, 0), col0 + j)),
        pl.BlockSpec((tm, tc), lambda i, j: (i, col0 + j)),
        pl.BlockSpec((8, tc), lambda i, j: (jnp.minimum((i + 1) * r8, nb8 - 1), col0 + j)),
    ]


def _fill_ext(ext, prev_ref, cur_ref, next_ref, tm, i, last):
    ext[0:8, :] = jnp.where(i > 0, prev_ref[...], 0.0)
    ext[8:8 + tm, :] = cur_ref[...]
    ext[8 + tm:16 + tm, :] = jnp.where(i < last, next_ref[...], 0.0)


def _conv_fwd(u, conv_w, conv_b):
    t = u.shape[0]
    tm, tc = CONV_TM, CONV_TC

    def body(prev_ref, cur_ref, next_ref, w_ref, b_ref, o_ref, ext):
        _fill_ext(ext, prev_ref, cur_ref, next_ref, tm, pl.program_id(0), t // tm - 1)
        for c0 in range(0, tc, CONV_CC):
            cs = slice(c0, c0 + CONV_CC)
            w = w_ref[:, cs]
            for r0 in range(0, tm, CONV_RC):
                acc = jnp.broadcast_to(b_ref[:, cs], (CONV_RC, CONV_CC))
                for k in range(KCONV):
                    acc = acc + w[k:k + 1, :] * ext[pl.ds(r0 + 6 + k, CONV_RC), cs]
                o_ref[r0:r0 + CONV_RC, cs] = acc * _sigmoid(acc)

    return pl.pallas_call(
        body, out_shape=jax.ShapeDtypeStruct((t, CONVD), F32), grid=(t // tm, CONVD // tc),
        in_specs=_halo_specs(t, tm, tc, OXBC // tc) + [
            pl.BlockSpec((KCONV, tc), lambda i, j: (0, j)), pl.BlockSpec((1, tc), lambda i, j: (0, j))],
        out_specs=pl.BlockSpec((tm, tc), lambda i, j: (i, j)),
        scratch_shapes=[pltpu.VMEM((tm + 16, tc), F32)],
        name="conv_fwd", compiler_params=_params(("parallel", "parallel")))(u, u, u, conv_w, conv_b)


def _conv_dpre(u, dxs, dy, dbc, dsk_row, conv_w, conv_b):
    t = u.shape[0]
    tm, tc = CONV_TM, CONV_TC
    r8 = tm // 8
    nb8 = t // 8
    c0 = OXBC // tc

    def body(uprev, ucur, unext, f_ref, y_ref, cf_ref, dsk_ref, w_ref, bias_ref, dpre_ref, dw_ref, db_ref, ext):
        j = pl.program_id(0)
        i = pl.program_id(1)
        _fill_ext(ext, uprev, ucur, unext, tm, i, t // tm - 1)
        is_xs = j < 2
        dw_cols, db_cols = [], []
        for c0 in range(0, tc, CONV_CC):
            cs = slice(c0, c0 + CONV_CC)
            w = w_ref[:, cs]
            dsk = dsk_ref[:, cs]
            dw_acc = [jnp.zeros((1, CONV_CC), F32) for _ in range(KCONV)]
            db_acc = jnp.zeros((1, CONV_CC), F32)
            for r0 in range(0, tm, CONV_RC):
                rs = slice(r0, r0 + CONV_RC)
                taps = [ext[pl.ds(r0 + 6 + k, CONV_RC), cs] for k in range(KCONV)]
                pre = jnp.broadcast_to(bias_ref[:, cs], (CONV_RC, CONV_CC))
                for k in range(KCONV):
                    pre = pre + w[k:k + 1, :] * taps[k]
                s = _sigmoid(pre)
                up = jnp.where(is_xs, f_ref[rs, cs] + dsk * y_ref[rs, cs], cf_ref[rs, cs])
                dpre = up * (s * (1.0 + pre * (1.0 - s)))
                dpre_ref[rs, cs] = dpre
                for k in range(KCONV):
                    dw_acc[k] = dw_acc[k] + jnp.sum(dpre * taps[k], axis=0, keepdims=True)
                db_acc = db_acc + jnp.sum(dpre, axis=0, keepdims=True)
            dw_cols.append(jnp.concatenate(dw_acc + [jnp.zeros((8 - KCONV, CONV_CC), F32)], axis=0))
            db_cols.append(jnp.broadcast_to(db_acc, (8, CONV_CC)))
        dw_part = jnp.concatenate(dw_cols, axis=1)
        db_part = jnp.concatenate(db_cols, axis=1)

        @pl.when(i == 0)
        def _():
            dw_ref[...] = dw_part
            db_ref[...] = db_part

        @pl.when(i > 0)
        def _():
            dw_ref[...] += dw_part
            db_ref[...] += db_part

    xs_spec = pl.BlockSpec((tm, tc), lambda j, i: (jnp.where(j < 2, i, 0), jnp.minimum(j, 1)))
    bc_spec = pl.BlockSpec((tm, tc), lambda j, i: (jnp.where(j == 2, i, 0), 0))
    in_specs = [
        pl.BlockSpec((8, tc), lambda j, i: (jnp.maximum(i * r8 - 1, 0), c0 + j)),
        pl.BlockSpec((tm, tc), lambda j, i: (i, c0 + j)),
        pl.BlockSpec((8, tc), lambda j, i: (jnp.minimum((i + 1) * r8, nb8 - 1), c0 + j)),
        xs_spec, xs_spec, bc_spec,
        pl.BlockSpec((1, tc), lambda j, i: (0, jnp.minimum(j, 1))),
        pl.BlockSpec((KCONV, tc), lambda j, i: (0, j)), pl.BlockSpec((1, tc), lambda j, i: (0, j)),
    ]
    return pl.pallas_call(
        body,
        out_shape=(jax.ShapeDtypeStruct((t, CONVD), F32), jax.ShapeDtypeStruct((8, CONVD), F32),
                   jax.ShapeDtypeStruct((8, CONVD), F32)),
        grid=(CONVD // tc, t // tm), in_specs=in_specs,
        out_specs=(pl.BlockSpec((tm, tc), lambda j, i: (i, j)),
                   pl.BlockSpec((8, tc), lambda j, i: (0, j)), pl.BlockSpec((8, tc), lambda j, i: (0, j))),
        scratch_shapes=[pltpu.VMEM((tm + 16, tc), F32)],
        name="conv_dpre", compiler_params=_params(("parallel", "arbitrary")))(
            u, u, u, dxs, dy, dbc, dsk_row, conv_w, conv_b)


def _conv_dx(du, dpre, conv_w):
    t = dpre.shape[0]
    tm, tc = CONV_TM, CONV_TC
    r8 = tm // 8
    nb8 = t // 8

    def body(prev_ref, cur_ref, next_ref, w_ref, du_in, du_out, ext):
        del du_in
        _fill_ext(ext, prev_ref, cur_ref, next_ref, tm, pl.program_id(1), t // tm - 1)
        for c0 in range(0, tc, CONV_CC):
            cs = slice(c0, c0 + CONV_CC)
            w = w_ref[:, cs]
            for r0 in range(0, tm, CONV_RC):
                acc = jnp.zeros((CONV_RC, CONV_CC), F32)
                for k in range(KCONV):
                    acc = acc + w[k:k + 1, :] * ext[pl.ds(r0 + 10 - k, CONV_RC), cs]
                du_out[r0:r0 + CONV_RC, cs] = acc.astype(du_out.dtype)

    in_specs = [
        pl.BlockSpec((8, tc), lambda j, i: (jnp.maximum(i * r8 - 1, 0), j)),
        pl.BlockSpec((tm, tc), lambda j, i: (i, j)),
        pl.BlockSpec((8, tc), lambda j, i: (jnp.minimum((i + 1) * r8, nb8 - 1), j)),
        pl.BlockSpec((KCONV, tc), lambda j, i: (0, j)),
        pl.BlockSpec(memory_space=pl.ANY),
    ]
    return pl.pallas_call(
        body, out_shape=jax.ShapeDtypeStruct(du.shape, du.dtype), grid=(CONVD // tc, t // tm), in_specs=in_specs,
        out_specs=pl.BlockSpec((tm, tc), lambda j, i: (i, OXBC // tc + j)),
        scratch_shapes=[pltpu.VMEM((tm + 16, tc), F32)], input_output_aliases={4: 0},
        name="conv_dx", compiler_params=_params(("parallel", "parallel")))(dpre, dpre, dpre, conv_w, du)


def _ssd_common(dtr_ref, par_ref, rev):
    raw = dtr_ref[...]
    lane = _iota((1, 128), 1)
    mine = (lane >= 32 * rev) & (lane < 32 * rev + 32)
    bias = par_ref[0:1, :]
    arow = jnp.where(mine, -jnp.exp(par_ref[1:2, :]), 0.0)
    dt = _softplus(raw + bias)
    a = dt * arow
    ri = _iota((Q, Q), 0)
    ci = _iota((Q, Q), 1)
    tri = (ci >= ri) if rev else (ci <= ri)
    trit = (ci <= ri) if rev else (ci >= ri)
    cs = _dot01_l(tri.astype(BF16), a)
    return raw, bias, arow, mine, dt, cs, tri, trit


def _expand_mat(rev):
    r = np.arange(128)[:, None]
    c = np.arange(DI)[None, :]
    return jnp.asarray(r == (c // HP) + 32 * rev, BF16)


def _sum_mat(rev):
    r = np.arange(DI)[:, None]
    c = np.arange(128)[None, :]
    return jnp.asarray(c == (r // HP) + 32 * rev, BF16)


def _ssd_fwd(xbc, u, par, y_add=None, *, rev):
    t = xbc.shape[0]
    nc = t // Q
    end = 0 if rev else Q - 1
    cmap = (lambda c: nc - 1 - c) if rev else (lambda c: c)

    def body(xbc_ref, dtr_ref, par_ref, ex_ref, *rest):
        yadd_ref = rest[0] if y_add is not None else None
        y_ref, st_ref, h_scr = rest[-3:]
        step = pl.program_id(0)

        @pl.when(step == 0)
        def _():
            h_scr[...] = jnp.zeros((NS, DI), F32)

        raw, bias, arow, mine, dt, cs, tri, trit = _ssd_common(dtr_ref, par_ref, rev)
        cst = cs.T
        dtt = dt.T
        tot_col = cst[:, end:end + 1]
        wt = dtt * jnp.exp(tot_col - cst)
        ecs_all = jnp.exp(cs)
        gam = jnp.exp(cs[end:end + 1, :])
        gam_x = _dot01(jnp.broadcast_to(gam, (8, 128)), ex_ref[...])[0:1, :]
        lane = _iota((Q, 128), 1)
        sel = lane < HP
        st_ref[...] = h_scr[...]
        for g in range(NG):
            bg = xbc_ref[:, DI + NS * g:DI + NS * (g + 1)]
            cg = xbc_ref[:, DI + NG * NS + NS * g:DI + NG * NS + NS * (g + 1)]
            cb = _dot_nt(cg.astype(BF16), bg.astype(BF16))
            bt = bg.T
            for k in range(4):
                lo = 512 * g + 128 * k
                xp = xbc_ref[:, lo:lo + 128].astype(BF16)
                hp = h_scr[:, lo:lo + 128]
                rhs = jnp.concatenate([xp, hp.astype(BF16)], axis=0)
                lhs, bts = [], []
                for j in range(2):
                    hc = 8 * g + 2 * k + j + 32 * rev
                    csc = jnp.broadcast_to(cs[:, hc:hc + 1], (Q, Q))
                    lm = jnp.exp(jnp.where(tri, csc - cst[hc:hc + 1, :], NEG)) * dtt[hc:hc + 1, :]
                    mh = (cb * lm).astype(BF16)
                    ec = (jnp.broadcast_to(ecs_all[:, hc:hc + 1], (Q, NS)) * cg).astype(BF16)
                    lhs.append(jnp.concatenate([mh, ec], axis=1))
                    bts.append((bt * wt[hc:hc + 1, :]).astype(BF16))
                ys = jnp.dot(jnp.concatenate(lhs, axis=0), rhs, preferred_element_type=F32)
                ss = jnp.dot(jnp.concatenate(bts, axis=0), xp, preferred_element_type=F32)
                yp = jnp.where(sel, ys[0:Q], ys[Q:2 * Q])
                y_ref[:, lo:lo + 128] = yp if yadd_ref is None else yp + yadd_ref[:, lo:lo + 128]
                h_scr[:, lo:lo + 128] = gam_x[:, lo:lo + 128] * hp + jnp.where(sel, ss[0:NS], ss[NS:2 * NS])

    return pl.pallas_call(
        body,
        out_shape=(jax.ShapeDtypeStruct((t, DI), F32), jax.ShapeDtypeStruct((nc, NS, DI), F32)),
        grid=(nc,),
        in_specs=[pl.BlockSpec((Q, CONVD), lambda c: (cmap(c), 0)),
                  pl.BlockSpec((Q, 128), lambda c: (cmap(c), ODT // 128)),
                  pl.BlockSpec((8, 128), lambda c: (0, 0)),
                  pl.BlockSpec((128, DI), lambda c: (0, 0))]
        + ([pl.BlockSpec((Q, DI), lambda c: (cmap(c), 0))] if y_add is not None else []),
        out_specs=(pl.BlockSpec((Q, DI), lambda c: (cmap(c), 0)),
                   pl.BlockSpec((None, NS, DI), lambda c: (cmap(c), 0, 0))),
        scratch_shapes=[pltpu.VMEM((NS, DI), F32)],
        name="ssd_fwd_rev" if rev else "ssd_fwd", compiler_params=_params(("arbitrary",)))(
            xbc, u, par, _expand_mat(rev), *([y_add] if y_add is not None else []))


def _ssd_bwd(xbc, u, par, dy, st, *, rev, add=None, side=None):
    t = xbc.shape[0]
    nc = t // Q
    end = 0 if rev else Q - 1
    cmap = (lambda c: c) if rev else (lambda c: nc - 1 - c)

    def body(xbc_ref, dtr_ref, par_ref, dy_ref, hin_ref, ex_ref, sm_ref, *rest):
        addx_ref, addbc_ref, addt_ref = rest[:3] if add is not None else (None, None, None)
        dxs_ref, dbc_ref, ddt_ref, acc_ref, dh_scr = rest[-5:]
        step = pl.program_id(0)

        @pl.when(step == 0)
        def _():
            dh_scr[...] = jnp.zeros((NS, DI), F32)

        raw, bias, arow, mine, dt, cs, tri, trit = _ssd_common(dtr_ref, par_ref, rev)
        ri = _iota((Q, Q), 0)
        ci = _iota((Q, Q), 1)
        stri = ((ri > ci) if rev else (ri < ci)).astype(BF16)
        strit = ((ci > ri) if rev else (ci < ri)).astype(BF16)
        cst = cs.T
        dtt = dt.T
        et = jnp.exp(cst)
        ecs_all = jnp.exp(cs)
        ws_all = jnp.exp(cs[end:end + 1, :] - cs)
        expand = ex_ref[...]
        summat = sm_ref[...]
        gam = jnp.exp(cs[end:end + 1, :])
        gam_x = _dot01(jnp.broadcast_to(gam, (8, 128)), expand)[0:1, :]
        dt_hi, dt_mid, _ = _split3(dt)
        dtx = (jnp.dot(dt_hi, expand, preferred_element_type=F32)
               + jnp.dot(dt_mid, expand, preferred_element_type=F32))
        lane = _iota((Q, 128), 1)
        sel = lane < HP
        dho = dh_scr[...]
        t3 = jnp.sum(dho * hin_ref[...], axis=0, keepdims=True) * gam_x
        dxs_cols, dxs2_cols, yoff_cols, a1_rows = [], [], [], []
        for g in range(NG):
            bg = xbc_ref[:, DI + NS * g:DI + NS * (g + 1)]
            cg = xbc_ref[:, DI + NG * NS + NS * g:DI + NG * NS + NS * (g + 1)]
            bb = bg.astype(BF16)
            cbf = cg.astype(BF16)
            cb = _dot_nt(cbf, bb)
            cbt = _dot_nt(bb, cbf)
            ct = cg.T
            bdh = jnp.dot(bb, dho[:, 512 * g:512 * (g + 1)].astype(BF16), preferred_element_type=F32)
            dcb = jnp.zeros((Q, Q), F32)
            dcg = jnp.zeros((Q, NS), F32)
            dbg = jnp.zeros((Q, NS), F32)
            for k in range(4):
                lo = 512 * g + 128 * k
                xpf = xbc_ref[:, lo:lo + 128]
                xp = xpf.astype(BF16)
                dyp = dy_ref[:, lo:lo + 128]
                dypb = dyp.astype(BF16)
                hinp = hin_ref[:, lo:lo + 128].astype(BF16)
                dhp = dho[:, lo:lo + 128]
                es, ws, lmds, mts, ctes, dyms, ecbs = [], [], [], [], [], [], []
                for j in range(2):
                    hc = 8 * g + 2 * k + j + 32 * rev
                    csc = jnp.broadcast_to(cs[:, hc:hc + 1], (Q, Q))
                    csr = cst[hc:hc + 1, :]
                    lmds.append(jnp.exp(jnp.where(tri, csc - csr, NEG)) * dtt[hc:hc + 1, :])
                    lmb = jnp.exp(jnp.where(trit, csr - csc, NEG))
                    mts.append((cbt * lmb).astype(BF16))
                    dyms.append(jnp.where(sel if j == 0 else ~sel, dyp, 0.0).astype(BF16))
                    ecs = jnp.broadcast_to(ecs_all[:, hc:hc + 1], (Q, NS))
                    es.append(ecs)
                    ws.append(jnp.broadcast_to(ws_all[:, hc:hc + 1], (Q, NS)))
                    ecbs.append((ecs * cg).astype(BF16))
                    ctes.append((ct * et[hc:hc + 1, :]).astype(BF16))
                by_dy = jnp.dot(jnp.concatenate(mts + ctes, axis=0), dypb, preferred_element_type=F32)
                dmm = _dot_nt(jnp.concatenate(dyms, axis=0), xp)
                dm0, dm1 = dmm[0:Q] * lmds[0], dmm[Q:2 * Q] * lmds[1]
                dcb = dcb + dm0 + dm1
                rr = jnp.dot(jnp.concatenate([dm0 * cb, dm1 * cb], axis=0).astype(BF16), stri, preferred_element_type=F32)
                a1_rows.append(jnp.sum(jnp.where(tri, rr[0:Q], 0.0), axis=0, keepdims=True))
                a1_rows.append(jnp.sum(jnp.where(tri, rr[Q:2 * Q], 0.0), axis=0, keepdims=True))
                yo = jnp.dot(jnp.concatenate(ecbs, axis=0), hinp, preferred_element_type=F32)
                e_p = jnp.where(sel, es[0], es[1])
                w_p = jnp.where(sel, ws[0], ws[1])
                d2 = w_p * bdh[:, 128 * k:128 * (k + 1)]
                dxs2_cols.append(d2)
                dxs_cols.append(jnp.where(sel, by_dy[0:Q], by_dy[Q:2 * Q]) + d2)
                yoff_cols.append(jnp.where(sel, yo[0:Q], yo[Q:2 * Q]))
                dcg = dcg + _dot_nt((e_p * dyp).astype(BF16), hinp)
                dbg = dbg + _dot_nt((w_p * dtx[:, lo:lo + 128] * xpf).astype(BF16), dhp.astype(BF16))
                dh_scr[:, lo:lo + 128] = (gam_x[:, lo:lo + 128] * dhp
                                          + jnp.where(sel, by_dy[2 * Q:3 * Q], by_dy[3 * Q:4 * Q]))
            dcg = dcg + jnp.dot(dcb.astype(BF16), bb, preferred_element_type=F32)
            dbg = dbg + jnp.dot(dcb.T.astype(BF16), cbf, preferred_element_type=F32)
            lo_b, lo_c = NS * g, NG * NS + NS * g
            if addbc_ref is not None:
                dbg = dbg + addbc_ref[:, lo_b:lo_b + NS]
                dcg = dcg + addbc_ref[:, lo_c:lo_c + NS]
            dbc_ref[:, lo_b:lo_b + NS] = dbg
            dbc_ref[:, lo_c:lo_c + NS] = dcg
        dxs = jnp.concatenate(dxs_cols, axis=1)
        dxs_ref[...] = dxs * dtx if addx_ref is None else dxs * dtx + addx_ref[...]
        xs = xbc_ref[:, 0:DI]
        stacked = jnp.concatenate([xs * dxs, xs * jnp.concatenate(dxs2_cols, axis=1),
                                   dy_ref[...] * jnp.concatenate(yoff_cols, axis=1),
                                   jnp.broadcast_to(t3, (8, DI))], axis=0).astype(BF16)
        sums = jnp.dot(stacked, summat, preferred_element_type=F32)
        rx, rx2, ryo, c0 = sums[0:Q], sums[Q:2 * Q], sums[2 * Q:3 * Q], sums[3 * Q:3 * Q + 1]
        zero32 = jnp.zeros((32, Q), F32)
        a1t = jnp.concatenate(([zero32] if rev else []) + a1_rows + [zero32] * (2 if rev else 3), axis=0)
        da = (a1t.T + jnp.dot(trit.astype(BF16), ryo.astype(BF16), preferred_element_type=F32)
              + jnp.dot(strit, (dt * rx2).astype(BF16), preferred_element_type=F32) + jnp.where(mine, c0, 0.0))
        ddt = rx + da * arow
        ddtr = ddt * _sigmoid(raw + bias)
        ddt_ref[...] = ddtr if addt_ref is None else ddtr + addt_ref[...]
        part = jnp.concatenate([jnp.sum(ddtr, axis=0, keepdims=True),
                                jnp.sum(da * dt, axis=0, keepdims=True) * arow,
                                jnp.zeros((6, 128), F32)], axis=0)

        @pl.when(step == 0)
        def _():
            acc_ref[...] = part

        @pl.when(step > 0)
        def _():
            acc_ref[...] += part

    outs, side_outs = _host_call(
        body, side, nc,
        out_shape=(jax.ShapeDtypeStruct((t, DI), F32), jax.ShapeDtypeStruct((t, 2 * NG * NS), F32),
                   jax.ShapeDtypeStruct((t, 128), F32), jax.ShapeDtypeStruct((8, 128), F32)),
        in_specs=[pl.BlockSpec((Q, CONVD), lambda c: (cmap(c), 0)),
                  pl.BlockSpec((Q, 128), lambda c: (cmap(c), ODT // 128)),
                  pl.BlockSpec((8, 128), lambda c: (0, 0)),
                  pl.BlockSpec((Q, DI), lambda c: (cmap(c), 0)),
                  pl.BlockSpec((None, NS, DI), lambda c: (cmap(c), 0, 0)),
                  pl.BlockSpec((128, DI), lambda c: (0, 0)), pl.BlockSpec((DI, 128), lambda c: (0, 0))]
        + ([pl.BlockSpec((Q, DI), lambda c: (cmap(c), 0)), pl.BlockSpec((Q, 2 * NG * NS), lambda c: (cmap(c), 0)),
            pl.BlockSpec((Q, 128), lambda c: (cmap(c), 0))] if add is not None else []),
        out_specs=(pl.BlockSpec((Q, DI), lambda c: (cmap(c), 0)),
                   pl.BlockSpec((Q, 2 * NG * NS), lambda c: (cmap(c), 0)),
                   pl.BlockSpec((Q, 128), lambda c: (cmap(c), 0)),
                   pl.BlockSpec((8, 128), lambda c: (0, 0))),
        scratch_shapes=[pltpu.VMEM((NS, DI), F32)],
        args=(xbc, u, par, dy, st, _expand_mat(rev), _sum_mat(rev)) + (tuple(add) if add is not None else ()), aliases={},
        name="ssd_bwd_rev" if rev else "ssd_bwd", sem=("arbitrary",))
    return (*outs, side_outs)


GN_TM = 256
GN_GROUP = DI // NG


def _gn_forward_vals(y0, xs, z, dsk):
    y = y0 + dsk * xs
    sz = _sigmoid(z)
    gate = z * sz
    y2 = y * gate
    parts, rs = [], []
    for g in range(NG):
        seg = y2[:, GN_GROUP * g:GN_GROUP * (g + 1)]
        r = lax.rsqrt(jnp.mean(seg * seg, axis=1, keepdims=True) + NORM_EPS)
        rs.append(r)
        parts.append(seg * r)
    yn = jnp.concatenate(parts, axis=1)
    return y, sz, gate, yn, rs


def _gatenorm_fwd(y_fb, xbc, u, dsk_row, nw_row):
    t = y_fb.shape[0]
    tm = GN_TM

    def body(y_ref, xs_ref, z_ref, dsk_ref, nw_ref, o_ref):
        _, _, _, yn, _ = _gn_forward_vals(y_ref[...], xs_ref[...], z_ref[...], dsk_ref[...])
        o_ref[...] = (yn * nw_ref[...]).astype(BF16)

    blk = pl.BlockSpec((tm, DI), lambda i: (i, 0))
    row = pl.BlockSpec((1, DI), lambda i: (0, 0))
    return pl.pallas_call(
        body, out_shape=jax.ShapeDtypeStruct((t, DI), BF16), grid=(t // tm,),
        in_specs=[blk, blk, pl.BlockSpec((tm, DI), lambda i: (i, OZ // DI)), row, row],
        out_specs=blk, name="gatenorm_fwd", compiler_params=_params(("parallel",)))(y_fb, xbc, u, dsk_row, nw_row)


def _gatenorm_bwd(ds_out, y_fb, xbc, u, du, dsk_row, nw_row, side=None):
    t = y_fb.shape[0]
    tm = GN_TM

    def body(ds_ref, y_ref, xs_ref, z_ref, dsk_ref, nw_ref, sm_ref, du_in, dy_ref, du_out, dnw_ref, dds_ref):
        del du_in
        i = pl.program_id(0)
        xs = xs_ref[...]
        z = z_ref[...]
        y, sz, gate, yn, rs = _gn_forward_vals(y_ref[...], xs, z, dsk_ref[...])
        ds = ds_ref[...]
        gsc = ds * nw_ref[...]
        parts = []
        for g in range(NG):
            sl = slice(GN_GROUP * g, GN_GROUP * (g + 1))
            m = jnp.mean(gsc[:, sl] * yn[:, sl], axis=1, keepdims=True)
            parts.append(rs[g] * (gsc[:, sl] - yn[:, sl] * m))
        dy2 = jnp.concatenate(parts, axis=1)
        dy = dy2 * gate
        dy_ref[...] = dy
        du_out[...] = (dy2 * y * (sz * (1.0 + z * (1.0 - sz)))).astype(du_out.dtype)
        dnw = jnp.broadcast_to(jnp.sum(ds * yn, axis=0, keepdims=True), (8, DI))
        drow = jnp.broadcast_to(jnp.sum(dy * xs, axis=0, keepdims=True), (8, DI))
        dds = _dot01(drow, sm_ref[...])

        @pl.when(i == 0)
        def _():
            dnw_ref[...] = dnw
            dds_ref[...] = dds

        @pl.when(i > 0)
        def _():
            dnw_ref[...] += dnw
            dds_ref[...] += dds

    blk = pl.BlockSpec((tm, DI), lambda i: (i, 0))
    row = pl.BlockSpec((1, DI), lambda i: (0, 0))
    outs, side_outs = _host_call(
        body, side, t // tm,
        out_shape=(jax.ShapeDtypeStruct((t, DI), F32), jax.ShapeDtypeStruct(du.shape, du.dtype),
                   jax.ShapeDtypeStruct((8, DI), F32), jax.ShapeDtypeStruct((8, 128), F32)),
        in_specs=[blk, blk, blk, pl.BlockSpec((tm, DI), lambda i: (i, OZ // DI)), row, row,
                  pl.BlockSpec((DI, 128), lambda i: (0, 0)), pl.BlockSpec(memory_space=pl.ANY)],
        out_specs=(blk, pl.BlockSpec((tm, DI), lambda i: (i, OZ // DI)),
                   pl.BlockSpec((8, DI), lambda i: (0, 0)), pl.BlockSpec((8, 128), lambda i: (0, 0))),
        scratch_shapes=[], args=(ds_out, y_fb, xbc, u, dsk_row, nw_row, _sum_mat(0), du), aliases={7: 1},
        name="gatenorm_bwd", sem=("arbitrary",))
    return (*outs, side_outs)


AT_B = 128
AT_W = AT_B + 2 * ATT_HALF
AT_L = 2 * AH
SCALE = 1.0 / math.sqrt(AH)


def _slope(g, hh):
    return 2.0 ** (-8.0 * (4 * g + hh + 1) / 12.0)


def _qcol(g):
    return lambda p: OQ // AT_L + 2 * g + p


def _kcol(g):
    return lambda p: OKV // AT_L + 4 * g + 2 * p


def _vcol(g):
    return lambda p: OKV // AT_L + 4 * g + 2 * p + 1


def _pcol(p):
    return p


def _sub(d):
    return 4 if d == 1 else 1


def _win_specs(col, t, d):
    tb, hb = AT_B * d * _sub(d), ATT_HALF * d
    per = tb // hb
    nh = t // hb
    return [
        pl.BlockSpec((hb, AT_L), lambda p, i: (jnp.maximum(per * i - 1, 0), col(p))),
        pl.BlockSpec((tb, AT_L), lambda p, i: (i, col(p))),
        pl.BlockSpec((hb, AT_L), lambda p, i: (jnp.minimum(per * (i + 1), nh - 1), col(p))),
    ]


def _blk_spec(col, d):
    return pl.BlockSpec((AT_B * d * _sub(d), AT_L), lambda p, i: (i, col(p)))


def _rows(ref, r, s, d):
    return ref[pl.ds(r, AT_B, stride=d), :] if d > 1 else ref[AT_B * s:AT_B * (s + 1), :]


def _win(p_ref, c_ref, n_ref, r, s, d):
    if d > 1:
        return jnp.concatenate([p_ref[pl.ds(r, ATT_HALF, stride=d), :], c_ref[pl.ds(r, AT_B, stride=d), :],
                                n_ref[pl.ds(r, ATT_HALF, stride=d), :]], axis=0)
    if s == 0:
        return jnp.concatenate([p_ref[...], c_ref[0:AT_B + ATT_HALF, :]], axis=0)
    if s == _sub(d) - 1:
        return jnp.concatenate([c_ref[AT_B * s - ATT_HALF:AT_B * (s + 1), :], n_ref[...]], axis=0)
    return c_ref[AT_B * s - ATT_HALF:AT_B * (s + 1) + ATT_HALF, :]


def _put_rows(ref, r, s, d, val):
    if d > 1:
        ref[pl.ds(r, AT_B, stride=d), :] = val
    else:
        ref[AT_B * s:AT_B * (s + 1), :] = val


def _for_blocks(d, fn):
    if d == 1:
        for s in range(_sub(d)):
            fn(0, s)
    else:
        def step(r, c):
            fn(r, 0)
            return c
        lax.fori_loop(0, d, step, 0, unroll=4)


def _attn_bias(blk, ln, d, g, p_id):
    a = blk * AT_B + _iota((AT_B, AT_W), 0)
    b = blk * AT_B - ATT_HALF + _iota((AT_B, AT_W), 1)
    rel = jnp.abs(a - b)
    valid = (rel <= ATT_HALF) & (b >= 0) & (b < ln)
    dist = (rel * d).astype(F32)
    out = []
    for hh in range(2):
        slope = jnp.where(p_id == 0, _slope(g, hh), _slope(g, 2 + hh))
        out.append(jnp.where(valid, -slope * dist, NEG))
    return out


def _attn_fwd(u, g):
    t = u.shape[0]
    d = DILATIONS[g]
    ln = t // d

    def body(q_ref, kp, kc, kn, vp, vc, vn, o_ref, l_ref):
        p_id = pl.program_id(0)
        i = pl.program_id(1)
        lane = _iota((AT_B, AT_L), 1)
        biases = [_attn_bias(i * _sub(d) + s, ln, d, g, p_id) for s in range(_sub(d))]

        def one(r, s):
            q = _rows(q_ref, r, s, d)
            kw = _win(kp, kc, kn, r, s, d).astype(BF16)
            vw = _win(vp, vc, vn, r, s, d).astype(BF16)
            o = jnp.zeros((AT_B, AT_L), F32)
            lse = jnp.zeros((AT_B, AT_L), F32)
            for hh in range(2):
                hm = (lane // AH) == hh
                qm = jnp.where(hm, q, 0.0).astype(BF16)
                sc = _dot_nt(qm, kw) * SCALE + biases[s][hh]
                m = jnp.max(sc, axis=1, keepdims=True)
                pr = jnp.exp(sc - m)
                den = jnp.sum(pr, axis=1, keepdims=True)
                oh = jnp.dot(pr.astype(BF16), vw, preferred_element_type=F32)
                o = jnp.where(hm, oh / den, o)
                lse = jnp.where(hm, m + jnp.log(den), lse)
            _put_rows(o_ref, r, s, d, o)
            _put_rows(l_ref, r, s, d, lse)

        _for_blocks(d, one)

    oshape = jax.ShapeDtypeStruct((t, 2 * AT_L), F32)
    ospec = _blk_spec(_pcol, d)
    return pl.pallas_call(
        body, out_shape=(oshape, oshape), grid=(2, t // (AT_B * d * _sub(d))),
        in_specs=[_blk_spec(_qcol(g), d)] + _win_specs(_kcol(g), t, d) + _win_specs(_vcol(g), t, d),
        out_specs=(ospec, ospec), name=f"attn_fwd_{g}", compiler_params=_params(("parallel", "parallel")))(
            u, u, u, u, u, u, u)


def _attn_dq(u, du, do, lse, e, g):
    t = u.shape[0]
    d = DILATIONS[g]
    ln = t // d

    def body(q_ref, kp, kc, kn, vp, vc, vn, do_ref, l_ref, e_ref, du_in, dq_ref, dq_scr):
        del du_in
        p_id = pl.program_id(0)
        i = pl.program_id(1)
        lane = _iota((AT_B, AT_L), 1)
        biases = [_attn_bias(i * _sub(d) + s, ln, d, g, p_id) for s in range(_sub(d))]

        def one(r, s):
            q = _rows(q_ref, r, s, d)
            kw = _win(kp, kc, kn, r, s, d).astype(BF16)
            vw = _win(vp, vc, vn, r, s, d).astype(BF16)
            do_ = _rows(do_ref, r, s, d)
            lv = _rows(l_ref, r, s, d)
            ev = _rows(e_ref, r, s, d)
            dq = jnp.zeros((AT_B, AT_L), F32)
            for hh in range(2):
                hm = (lane // AH) == hh
                qm = jnp.where(hm, q, 0.0).astype(BF16)
                sc = _dot_nt(qm, kw) * SCALE + biases[s][hh]
                lcol = jnp.broadcast_to(lv[:, AH * hh:AH * hh + 1], (AT_B, AT_W))
                ecol = jnp.broadcast_to(ev[:, AH * hh:AH * hh + 1], (AT_B, AT_W))
                pr = jnp.exp(sc - lcol)
                dom = jnp.where(hm, do_, 0.0).astype(BF16)
                ds = pr * (_dot_nt(dom, vw) + ecol)
                dqh = jnp.dot(ds.astype(BF16), kw, preferred_element_type=F32) * SCALE
                dq = jnp.where(hm, dqh, dq)
            _put_rows(dq_scr, r, s, d, dq)

        _for_blocks(d, one)
        dq_ref[...] = dq_scr[...].astype(dq_ref.dtype)

    rspec = _blk_spec(_pcol, d)
    return pl.pallas_call(
        body, out_shape=jax.ShapeDtypeStruct(du.shape, du.dtype), grid=(2, t // (AT_B * d * _sub(d))),
        in_specs=[_blk_spec(_qcol(g), d)] + _win_specs(_kcol(g), t, d) + _win_specs(_vcol(g), t, d)
        + [rspec, rspec, rspec, pl.BlockSpec(memory_space=pl.ANY)],
        out_specs=_blk_spec(_qcol(g), d), input_output_aliases={10: 0},
        scratch_shapes=[pltpu.VMEM((AT_B * d * _sub(d), AT_L), F32)],
        name=f"attn_dq_{g}", compiler_params=_params(("parallel", "parallel")))(
            u, u, u, u, u, u, u, do, lse, e, du)


def _attn_dkv(u, du, do, lse, e, g):
    t = u.shape[0]
    d = DILATIONS[g]
    ln = t // d

    def body(k_ref, v_ref, qp, qc, qn, dp_, dc_, dn_, lp, lc, ln_, ep, ec, en, du_in, dkv_ref, dk_scr, dv_scr):
        del du_in
        p_id = pl.program_id(0)
        jb = pl.program_id(1)
        lane = _iota((AT_B, AT_L), 1)
        biases = [_attn_bias(jb * _sub(d) + s, ln, d, g, p_id) for s in range(_sub(d))]

        def one(r, s):
            k = _rows(k_ref, r, s, d)
            v = _rows(v_ref, r, s, d)
            qw = _win(qp, qc, qn, r, s, d).astype(BF16)
            dow = _win(dp_, dc_, dn_, r, s, d).astype(BF16)
            lt = _win(lp, lc, ln_, r, s, d).T
            et = _win(ep, ec, en, r, s, d).T
            dk = jnp.zeros((AT_B, AT_L), F32)
            dv = jnp.zeros((AT_B, AT_L), F32)
            for hh in range(2):
                hm = (lane // AH) == hh
                km = jnp.where(hm, k, 0.0).astype(BF16)
                st = _dot_nt(km, qw) * SCALE + biases[s][hh]
                pt = jnp.exp(st - lt[AH * hh:AH * hh + 1, :])
                dvh = jnp.dot(pt.astype(BF16), dow, preferred_element_type=F32)
                vm = jnp.where(hm, v, 0.0).astype(BF16)
                dst = pt * (_dot_nt(vm, dow) + et[AH * hh:AH * hh + 1, :])
                dkh = jnp.dot(dst.astype(BF16), qw, preferred_element_type=F32) * SCALE
                dk = jnp.where(hm, dkh, dk)
                dv = jnp.where(hm, dvh, dv)
            _put_rows(dk_scr, r, s, d, dk)
            _put_rows(dv_scr, r, s, d, dv)

        _for_blocks(d, one)
        dkv_ref[:, 0:AT_L] = dk_scr[...].astype(dkv_ref.dtype)
        dkv_ref[:, AT_L:2 * AT_L] = dv_scr[...].astype(dkv_ref.dtype)

    return pl.pallas_call(
        body, out_shape=jax.ShapeDtypeStruct(du.shape, du.dtype), grid=(2, t // (AT_B * d * _sub(d))),
        in_specs=[_blk_spec(_kcol(g), d), _blk_spec(_vcol(g), d)]
        + _win_specs(_qcol(g), t, d) + _win_specs(_pcol, t, d) + _win_specs(_pcol, t, d) + _win_specs(_pcol, t, d)
        + [pl.BlockSpec(memory_space=pl.ANY)],
        out_specs=pl.BlockSpec((AT_B * d * _sub(d), 2 * AT_L), lambda p, i: (i, OKV // (2 * AT_L) + 2 * g + p)),
        input_output_aliases={14: 0},
        scratch_shapes=[pltpu.VMEM((AT_B * d * _sub(d), AT_L), F32), pltpu.VMEM((AT_B * d * _sub(d), AT_L), F32)],
        name=f"attn_dkv_{g}", compiler_params=_params(("parallel", "parallel")))(
            u, u, u, u, u, do, do, do, lse, lse, lse, e, e, e, du)


CMB_TM = 1024


def _combine_weights(l0, l1, l2):
    m = jnp.maximum(jnp.maximum(l0, l1), l2)
    e0, e1, e2 = jnp.exp(l0 - m), jnp.exp(l1 - m), jnp.exp(l2 - m)
    inv = 1.0 / (e0 + e1 + e2)
    return e0 * inv, e1 * inv, e2 * inv


def _combine_fwd(os_, ls_):
    t = os_[0].shape[0]
    tm = CMB_TM

    def body(o0, o1, o2, l0, l1, l2, a_ref):
        w0, w1, w2 = _combine_weights(l0[...], l1[...], l2[...])
        a_ref[...] = w0 * o0[...] + w1 * o1[...] + w2 * o2[...]

    blk = pl.BlockSpec((tm, 2 * AT_L), lambda i: (i, 0))
    return pl.pallas_call(
        body, out_shape=jax.ShapeDtypeStruct((t, 2 * AT_L), F32), grid=(t // tm,), in_specs=[blk] * 6, out_specs=blk,
        name="combine_fwd", compiler_params=_params(("parallel",)))(*os_, *ls_)


def _combine_bwd(datt, os_, ls_):
    t = datt.shape[0]
    tm = CMB_TM

    def body(da_ref, o0, o1, o2, l0, l1, l2, d0, d1, d2, e0, e1, e2):
        w = _combine_weights(l0[...], l1[...], l2[...])
        da = da_ref[...]
        att = w[0] * o0[...] + w[1] * o1[...] + w[2] * o2[...]
        r = _iota((2 * AT_L, 2 * AT_L), 0) // AH
        c = _iota((2 * AT_L, 2 * AT_L), 1) // AH
        hs = _dot01(da * att, (r == c).astype(BF16))
        for wg, dref, eref in zip(w, (d0, d1, d2), (e0, e1, e2)):
            dref[...] = wg * da
            eref[...] = -wg * hs

    blk = pl.BlockSpec((tm, 2 * AT_L), lambda i: (i, 0))
    shp = jax.ShapeDtypeStruct((t, 2 * AT_L), F32)
    outs = pl.pallas_call(
        body, out_shape=(shp,) * 6, grid=(t // tm,), in_specs=[blk] * 7, out_specs=(blk,) * 6,
        name="combine_bwd", compiler_params=_params(("parallel",)))(datt, *os_, *ls_)
    return outs[0:3], outs[3:6]


def _combine_proj(os_, ls_, w_pa):
    t = os_[0].shape[0]
    tm = ROW_TM
    nsh, _, ws = w_pa.shape

    def body(o0, o1, o2, l0, l1, l2, w_ref, a_ref, y_ref):
        w0, w1, w2 = _combine_weights(l0[...], l1[...], l2[...])
        att = w0 * o0[...] + w1 * o1[...] + w2 * o2[...]
        a_ref[...] = att
        ab = att.astype(BF16)
        for sh in range(nsh):
            y_ref[:, ws * sh:ws * (sh + 1)] = jnp.dot(ab, w_ref[sh], preferred_element_type=F32)

    blk = pl.BlockSpec((tm, 2 * AT_L), lambda i: (i, 0))
    return pl.pallas_call(
        body, out_shape=(jax.ShapeDtypeStruct((t, 2 * AT_L), F32), jax.ShapeDtypeStruct((t, nsh * ws), F32)),
        grid=(t // tm,), in_specs=[blk] * 6 + [pl.BlockSpec(w_pa.shape, lambda i: (0, 0, 0))],
        out_specs=(blk, pl.BlockSpec((tm, nsh * ws), lambda i: (i, 0))),
        name="combine_proj", compiler_params=_params(("parallel",)))(*os_, *ls_, w_pa)


def _d_att_combine_bwd(dy_att, w_pa, os_, ls_):
    t = dy_att.shape[0]
    tm = ROW_TM
    nsh, _, ws = w_pa.shape

    def body(dy_ref, w_ref, o0, o1, o2, l0, l1, l2, d0, d1, d2, e0, e1, e2):
        da = jnp.zeros((tm, 2 * AT_L), F32)
        for sh in range(nsh):
            da = da + _dot_nt(dy_ref[:, ws * sh:ws * (sh + 1)], w_ref[sh])
        w = _combine_weights(l0[...], l1[...], l2[...])
        att = w[0] * o0[...] + w[1] * o1[...] + w[2] * o2[...]
        r = _iota((2 * AT_L, 2 * AT_L), 0) // AH
        c = _iota((2 * AT_L, 2 * AT_L), 1) // AH
        hs = _dot01(da * att, (r == c).astype(BF16))
        for wg, dref, eref in zip(w, (d0, d1, d2), (e0, e1, e2)):
            dref[...] = wg * da
            eref[...] = -wg * hs

    blk = pl.BlockSpec((tm, 2 * AT_L), lambda i: (i, 0))
    shp = jax.ShapeDtypeStruct((t, 2 * AT_L), F32)
    outs = pl.pallas_call(
        body, out_shape=(shp,) * 6, grid=(t // tm,),
        in_specs=[pl.BlockSpec((tm, nsh * ws), lambda i: (i, 0)), pl.BlockSpec(w_pa.shape, lambda i: (0, 0, 0))] + [blk] * 6,
        out_specs=(blk,) * 6, name="d_att_combine_bwd", compiler_params=_params(("parallel",)))(dy_att, w_pa, *os_, *ls_)
    return outs[0:3], outs[3:6]


ROW_TM = 512


def _mix_fwd(y_ssd, y_att, u, bg_row):
    t = y_ssd.shape[0]
    tm = ROW_TM

    def body(ys_ref, ya_ref, g0_ref, g1_ref, b0_ref, b1_ref, o_ref):
        g0 = _sigmoid(g0_ref[...] + b0_ref[...])
        g1 = _sigmoid(g1_ref[...] + b1_ref[...])
        o_ref[...] = (g0 * ys_ref[...] + g1 * ya_ref[...]).astype(BF16)

    blk = pl.BlockSpec((tm, D), lambda i: (i, 0))
    return pl.pallas_call(
        body, out_shape=jax.ShapeDtypeStruct((t, D), BF16), grid=(t // tm,),
        in_specs=[blk, blk, pl.BlockSpec((tm, D), lambda i: (i, OGATE // D)), pl.BlockSpec((tm, D), lambda i: (i, OGATE // D + 1)),
                  pl.BlockSpec((1, D), lambda i: (0, 0)), pl.BlockSpec((1, D), lambda i: (0, 1))],
        out_specs=blk, name="mix_fwd", compiler_params=_params(("parallel",)))(y_ssd, y_att, u, u, bg_row, bg_row)


def _mix_bwd(dmixin, y_ssd, y_att, u, bg_row):
    t = y_ssd.shape[0]
    tm = ROW_TM

    def body(dm_ref, ys_ref, ya_ref, g0_ref, g1_ref, b0_ref, b1_ref, dys_ref, dya_ref, du_ref, db_ref):
        i = pl.program_id(0)
        g0 = _sigmoid(g0_ref[...] + b0_ref[...])
        g1 = _sigmoid(g1_ref[...] + b1_ref[...])
        dm = dm_ref[...]
        dys_ref[...] = (dm * g0).astype(BF16)
        dya_ref[...] = (dm * g1).astype(BF16)
        dl0 = dm * ys_ref[...] * g0 * (1.0 - g0)
        dl1 = dm * ya_ref[...] * g1 * (1.0 - g1)
        du_ref[:, 0:D] = dl0.astype(BF16)
        du_ref[:, D:2 * D] = dl1.astype(BF16)
        part = jnp.concatenate([jnp.broadcast_to(jnp.sum(dl0, axis=0, keepdims=True), (8, D)),
                                jnp.broadcast_to(jnp.sum(dl1, axis=0, keepdims=True), (8, D))], axis=1)

        @pl.when(i == 0)
        def _():
            db_ref[...] = part

        @pl.when(i > 0)
        def _():
            db_ref[...] += part

    blk = pl.BlockSpec((tm, D), lambda i: (i, 0))
    return pl.pallas_call(
        body,
        out_shape=(jax.ShapeDtypeStruct((t, D), BF16), jax.ShapeDtypeStruct((t, D), BF16),
                   jax.ShapeDtypeStruct((t, UW), BF16), jax.ShapeDtypeStruct((8, 2 * D), F32)),
        grid=(t // tm,),
        in_specs=[blk, blk, blk, pl.BlockSpec((tm, D), lambda i: (i, OGATE // D)), pl.BlockSpec((tm, D), lambda i: (i, OGATE // D + 1)),
                  pl.BlockSpec((1, D), lambda i: (0, 0)), pl.BlockSpec((1, D), lambda i: (0, 1))],
        out_specs=(blk, blk, pl.BlockSpec((tm, 2 * D), lambda i: (i, OGATE // (2 * D))),
                   pl.BlockSpec((8, 2 * D), lambda i: (0, 0))),
        name="mix_bwd", compiler_params=_params(("arbitrary",)))(dmixin, y_ssd, y_att, u, u, bg_row, bg_row)


def _ln(x, g, b):
    mu = jnp.mean(x, axis=1, keepdims=True)
    xc = x - mu
    var = jnp.mean(xc * xc, axis=1, keepdims=True)
    rstd = lax.rsqrt(var + NORM_EPS)
    xhat = xc * rstd
    return xhat * g + b, xhat, rstd


def _ln_back(dh, xhat, rstd, g):
    dxh = dh * g
    m1 = jnp.mean(dxh, axis=1, keepdims=True)
    m2 = jnp.mean(dxh * xhat, axis=1, keepdims=True)
    return rstd * (dxh - m1 - xhat * m2)


def _ln1_fwd(x, mix, g_row, b_row):
    t = x.shape[0]
    tm = ROW_TM

    def body(x_ref, m_ref, g_ref, b_ref, pre_ref, h_ref):
        pre = ALPHA * x_ref[...] + m_ref[...]
        pre_ref[...] = pre
        h, _, _ = _ln(pre, g_ref[...], b_ref[...])
        h_ref[...] = h.astype(BF16)

    blk = pl.BlockSpec((tm, D), lambda i: (i, 0))
    row = pl.BlockSpec((1, D), lambda i: (0, 0))
    return pl.pallas_call(
        body, out_shape=(jax.ShapeDtypeStruct((t, D), F32), jax.ShapeDtypeStruct((t, D), BF16)), grid=(t // tm,),
        in_specs=[blk, blk, row, row], out_specs=(blk, blk),
        name="ln1_fwd", compiler_params=_params(("parallel",)))(x, mix, g_row, b_row)


def _ln1_bwd(dh, pre, g_row, b_row):
    t = dh.shape[0]
    tm = ROW_TM

    def body(dh_ref, pre_ref, g_ref, b_ref, dpre_ref, acc_ref):
        i = pl.program_id(0)
        dh_ = dh_ref[...]
        _, xhat, rstd = _ln(pre_ref[...], g_ref[...], b_ref[...])
        dpre_ref[...] = _ln_back(dh_, xhat, rstd, g_ref[...])
        part = jnp.concatenate([jnp.sum(dh_ * xhat, axis=0, keepdims=True), jnp.sum(dh_, axis=0, keepdims=True),
                                jnp.zeros((6, D), F32)], axis=0)

        @pl.when(i == 0)
        def _():
            acc_ref[...] = part

        @pl.when(i > 0)
        def _():
            acc_ref[...] += part

    blk = pl.BlockSpec((tm, D), lambda i: (i, 0))
    row = pl.BlockSpec((1, D), lambda i: (0, 0))
    return pl.pallas_call(
        body, out_shape=(jax.ShapeDtypeStruct((t, D), F32), jax.ShapeDtypeStruct((8, D), F32)), grid=(t // tm,),
        in_specs=[blk, blk, row, row], out_specs=(blk, pl.BlockSpec((8, D), lambda i: (0, 0))),
        name="ln1_bwd", compiler_params=_params(("arbitrary",)))(dh, pre, g_row, b_row)


def _ln2_loss(pre1, f, tgt, g1_row, b1_row, g2_row, b2_row):
    t = pre1.shape[0]
    tm = ROW_TM

    def body(p1_ref, f_ref, t_ref, g1_ref, b1_ref, g2_ref, b2_ref, dpre_ref, acc_ref):
        i = pl.program_id(0)
        h1, _, _ = _ln(p1_ref[...], g1_ref[...], b1_ref[...])
        pre2 = ALPHA * h1 + f_ref[...]
        h2, xhat, rstd = _ln(pre2, g2_ref[...], b2_ref[...])
        err = h2 - t_ref[...]
        dh = err * (1.0 / D)
        dpre_ref[...] = _ln_back(dh, xhat, rstd, g2_ref[...])
        loss = jnp.sum(jnp.sum(err * err, axis=1, keepdims=True), axis=0, keepdims=True) * (0.5 / D)
        part = jnp.concatenate([jnp.sum(dh * xhat, axis=0, keepdims=True), jnp.sum(dh, axis=0, keepdims=True),
                                jnp.broadcast_to(loss, (1, D)), jnp.zeros((5, D), F32)], axis=0)

        @pl.when(i == 0)
        def _():
            acc_ref[...] = part

        @pl.when(i > 0)
        def _():
            acc_ref[...] += part

    blk = pl.BlockSpec((tm, D), lambda i: (i, 0))
    row = pl.BlockSpec((1, D), lambda i: (0, 0))
    return pl.pallas_call(
        body, out_shape=(jax.ShapeDtypeStruct((t, D), F32), jax.ShapeDtypeStruct((8, D), F32)), grid=(t // tm,),
        in_specs=[blk, blk, blk, row, row, row, row], out_specs=(blk, pl.BlockSpec((8, D), lambda i: (0, 0))),
        name="ln2_loss", compiler_params=_params(("arbitrary",)))(pre1, f, tgt, g1_row, b1_row, g2_row, b2_row)


def _mlp_up(h1, w_up):
    t = h1.shape[0]
    tm, tn = ROW_TM, D

    def body(a_ref, b_ref, up_ref, act_ref):
        up = jnp.dot(a_ref[...], b_ref[...], preferred_element_type=F32)
        up_ref[...] = up.astype(BF16)
        r = jnp.maximum(up, 0.0)
        act_ref[...] = (r * r).astype(BF16)

    blk = pl.BlockSpec((tm, tn), lambda j, i: (i, j))
    return pl.pallas_call(
        body, out_shape=(jax.ShapeDtypeStruct((t, DFF), BF16), jax.ShapeDtypeStruct((t, DFF), BF16)),
        grid=(DFF // tn, t // tm),
        in_specs=[pl.BlockSpec((tm, D), lambda j, i: (i, 0)), pl.BlockSpec((None, D, tn), lambda j, i: (j, 0, 0))],
        out_specs=(blk, blk), name="mlp_up", compiler_params=_params(("parallel", "parallel")))(h1, w_up)


def _d_up(dpre2, w_down, up):
    t = up.shape[0]
    tm, tk = ROW_TM, D

    def body(a_ref, b_ref, u_ref, o_ref):
        dact = _dot_nt(a_ref[...], b_ref[...])
        o_ref[...] = (dact * 2.0 * jnp.maximum(u_ref[...].astype(F32), 0.0)).astype(BF16)

    blk = pl.BlockSpec((tm, tk), lambda j, i: (i, j))
    return pl.pallas_call(
        body, out_shape=jax.ShapeDtypeStruct((t, DFF), BF16), grid=(DFF // tk, t // tm),
        in_specs=[pl.BlockSpec((tm, D), lambda j, i: (i, 0)), pl.BlockSpec((tk, D), lambda j, i: (j, 0)), blk],
        out_specs=blk, name="d_up", compiler_params=_params(("parallel", "parallel")))(dpre2, w_down, up)


def _dt_bwd(du, ddt):
    t = ddt.shape[0]
    tm = 1024

    def body(f_ref, du_in, o_ref):
        del du_in
        o_ref[:, 0:128] = f_ref[...].astype(o_ref.dtype)
        o_ref[:, 128:256] = jnp.zeros((tm, 128), o_ref.dtype)

    blk = pl.BlockSpec((tm, 128), lambda i: (i, 0))
    return pl.pallas_call(
        body, out_shape=jax.ShapeDtypeStruct(du.shape, du.dtype), grid=(t // tm,),
        in_specs=[blk, pl.BlockSpec(memory_space=pl.ANY)],
        out_specs=pl.BlockSpec((tm, 256), lambda i: (i, ODT // 256)), input_output_aliases={1: 0},
        name="dt_bwd", compiler_params=_params(("parallel",)))(ddt, du)


def _mix_out_ln1(y_ssd, y_att, u, bg_row, x, w_out, g_row, b_row):
    t = x.shape[0]
    tm = ROW_TM

    def body(ys_ref, ya_ref, g0_ref, g1_ref, b0_ref, b1_ref, x_ref, w_ref, g_ref, b_ref, mixin_ref, pre_ref, h_ref):
        g0 = _sigmoid(g0_ref[...] + b0_ref[...])
        g1 = _sigmoid(g1_ref[...] + b1_ref[...])
        mixin = (g0 * ys_ref[...] + g1 * ya_ref[...]).astype(BF16)
        mixin_ref[...] = mixin
        pre = ALPHA * x_ref[...] + jnp.dot(mixin, w_ref[...], preferred_element_type=F32)
        pre_ref[...] = pre
        h, _, _ = _ln(pre, g_ref[...], b_ref[...])
        h_ref[...] = h.astype(BF16)

    blk = pl.BlockSpec((tm, D), lambda i: (i, 0))
    row = pl.BlockSpec((1, D), lambda i: (0, 0))
    return pl.pallas_call(
        body,
        out_shape=(jax.ShapeDtypeStruct((t, D), BF16), jax.ShapeDtypeStruct((t, D), F32), jax.ShapeDtypeStruct((t, D), BF16)),
        grid=(t // tm,),
        in_specs=[blk, blk, pl.BlockSpec((tm, D), lambda i: (i, OGATE // D)), pl.BlockSpec((tm, D), lambda i: (i, OGATE // D + 1)),
                  row, pl.BlockSpec((1, D), lambda i: (0, 1)), blk, pl.BlockSpec((D, D), lambda i: (0, 0)), row, row],
        out_specs=(blk, blk, blk), name="mix_out_ln1", compiler_params=_params(("parallel",)))(
            y_ssd, y_att, u, u, bg_row, bg_row, x, w_out, g_row, b_row)


def _mlp_down_ln2_loss(act, w_down, pre1, tgt, g1_row, b1_row, g2_row, b2_row):
    t = pre1.shape[0]
    tm = ROW_TM

    def body(a_ref, w_ref, p1_ref, t_ref, g1_ref, b1_ref, g2_ref, b2_ref, dpre_ref, dpreb_ref, acc_ref):
        i = pl.program_id(0)
        f = jnp.dot(a_ref[...], w_ref[...], preferred_element_type=F32)
        h1, _, _ = _ln(p1_ref[...], g1_ref[...], b1_ref[...])
        pre2 = ALPHA * h1 + f
        h2, xhat, rstd = _ln(pre2, g2_ref[...], b2_ref[...])
        err = h2 - t_ref[...]
        dh = err * (1.0 / D)
        dpre = _ln_back(dh, xhat, rstd, g2_ref[...])
        dpre_ref[...] = dpre
        dpreb_ref[...] = dpre.astype(BF16)
        loss = jnp.sum(jnp.sum(err * err, axis=1, keepdims=True), axis=0, keepdims=True) * (0.5 / D)
        part = jnp.concatenate([jnp.sum(dh * xhat, axis=0, keepdims=True), jnp.sum(dh, axis=0, keepdims=True),
                                jnp.broadcast_to(loss, (1, D)), jnp.zeros((5, D), F32)], axis=0)

        @pl.when(i == 0)
        def _():
            acc_ref[...] = part

        @pl.when(i > 0)
        def _():
            acc_ref[...] += part

    blk = pl.BlockSpec((tm, D), lambda i: (i, 0))
    row = pl.BlockSpec((1, D), lambda i: (0, 0))
    return pl.pallas_call(
        body,
        out_shape=(jax.ShapeDtypeStruct((t, D), F32), jax.ShapeDtypeStruct((t, D), BF16), jax.ShapeDtypeStruct((8, D), F32)),
        grid=(t // tm,),
        in_specs=[pl.BlockSpec((tm, DFF), lambda i: (i, 0)), pl.BlockSpec((DFF, D), lambda i: (0, 0)), blk, blk, row, row, row, row],
        out_specs=(blk, blk, pl.BlockSpec((8, D), lambda i: (0, 0))),
        name="mlp_down_ln2_loss", compiler_params=_params(("arbitrary",)))(act, w_down, pre1, tgt, g1_row, b1_row, g2_row, b2_row)


def _d_h1_ln1_bwd(dup, w_up, dpre2, pre1, g_row, b_row):
    t = dup.shape[0]
    tm = ROW_TM
    nsh = w_up.shape[0]

    def body(a_ref, w_ref, add_ref, pre_ref, g_ref, b_ref, dpre_ref, acc_ref):
        i = pl.program_id(0)
        dh_ = ALPHA * add_ref[...]
        for sh in range(nsh):
            dh_ = dh_ + _dot_nt(a_ref[:, D * sh:D * (sh + 1)], w_ref[sh])
        _, xhat, rstd = _ln(pre_ref[...], g_ref[...], b_ref[...])
        dpre_ref[...] = _ln_back(dh_, xhat, rstd, g_ref[...])
        rows = jnp.concatenate([jnp.sum(dh_ * xhat, axis=0, keepdims=True), jnp.sum(dh_, axis=0, keepdims=True),
                                jnp.zeros((6, D), F32)], axis=0)

        @pl.when(i == 0)
        def _():
            acc_ref[...] = rows

        @pl.when(i > 0)
        def _():
            acc_ref[...] += rows

    blk = pl.BlockSpec((tm, D), lambda i: (i, 0))
    row = pl.BlockSpec((1, D), lambda i: (0, 0))
    return pl.pallas_call(
        body, out_shape=(jax.ShapeDtypeStruct((t, D), F32), jax.ShapeDtypeStruct((8, D), F32)),
        grid=(t // tm,),
        in_specs=[pl.BlockSpec((tm, nsh * D), lambda i: (i, 0)), pl.BlockSpec(w_up.shape, lambda i: (0, 0, 0)),
                  blk, blk, row, row],
        out_specs=(blk, pl.BlockSpec((8, D), lambda i: (0, 0))),
        name="d_h1_ln1_bwd", compiler_params=_params(("arbitrary",)))(dup, w_up, dpre2, pre1, g_row, b_row)


def _d_mixin_mix_bwd(dpre1, w_out, y_ssd, y_att, u, bg_row):
    t = y_ssd.shape[0]
    tm = ROW_TM

    def body(a_ref, w_ref, ys_ref, ya_ref, g0_ref, g1_ref, b0_ref, b1_ref, dys_ref, dya_ref, du_ref, db_ref):
        i = pl.program_id(0)
        dm = _dot_nt(a_ref[...].astype(BF16), w_ref[...])
        g0 = _sigmoid(g0_ref[...] + b0_ref[...])
        g1 = _sigmoid(g1_ref[...] + b1_ref[...])
        dys_ref[...] = (dm * g0).astype(BF16)
        dya_ref[...] = (dm * g1).astype(BF16)
        dl0 = dm * ys_ref[...] * g0 * (1.0 - g0)
        dl1 = dm * ya_ref[...] * g1 * (1.0 - g1)
        du_ref[:, 0:D] = dl0.astype(BF16)
        du_ref[:, D:2 * D] = dl1.astype(BF16)
        part = jnp.concatenate([jnp.broadcast_to(jnp.sum(dl0, axis=0, keepdims=True), (8, D)),
                                jnp.broadcast_to(jnp.sum(dl1, axis=0, keepdims=True), (8, D))], axis=1)

        @pl.when(i == 0)
        def _():
            db_ref[...] = part

        @pl.when(i > 0)
        def _():
            db_ref[...] += part

    blk = pl.BlockSpec((tm, D), lambda i: (i, 0))
    return pl.pallas_call(
        body,
        out_shape=(jax.ShapeDtypeStruct((t, D), BF16), jax.ShapeDtypeStruct((t, D), BF16),
                   jax.ShapeDtypeStruct((t, UW), BF16), jax.ShapeDtypeStruct((8, 2 * D), F32)),
        grid=(t // tm,),
        in_specs=[blk, pl.BlockSpec((D, D), lambda i: (0, 0)), blk, blk,
                  pl.BlockSpec((tm, D), lambda i: (i, OGATE // D)), pl.BlockSpec((tm, D), lambda i: (i, OGATE // D + 1)),
                  pl.BlockSpec((1, D), lambda i: (0, 0)), pl.BlockSpec((1, D), lambda i: (0, 1))],
        out_specs=(blk, blk, pl.BlockSpec((tm, 2 * D), lambda i: (i, OGATE // (2 * D))),
                   pl.BlockSpec((8, 2 * D), lambda i: (0, 0))),
        name="d_mixin_mix_bwd", compiler_params=_params(("arbitrary",)))(dpre1, w_out, y_ssd, y_att, u, u, bg_row, bg_row)


def _adamw(w, g, m, v, name):
    r, c = w.shape
    tr, tc = r, c
    for cand in (256, 128, 64, 32, 16, 8):
        if r % cand == 0 and cand * c * 4 <= 2 ** 21:
            tr = cand
            break
    if tr < 64 and c % 256 == 0:
        tr, tc = r, 256
    bc1 = 1.0 / (1.0 - ADAM_B1 ** ADAM_STEP)
    bc2 = 1.0 / (1.0 - ADAM_B2 ** ADAM_STEP)

    def body(w_ref, g_ref, m_ref, v_ref, d_ref, nm_ref, nv_ref):
        gg = g_ref[...]
        nm = ADAM_B1 * m_ref[...] + (1.0 - ADAM_B1) * gg
        nv = ADAM_B2 * v_ref[...] + (1.0 - ADAM_B2) * (gg * gg)
        nm_ref[...] = nm
        nv_ref[...] = nv
        d_ref[...] = -ADAM_LR * ((nm * bc1) / (jnp.sqrt(nv * bc2) + ADAM_EPS) + ADAM_WD * w_ref[...])

    blk = pl.BlockSpec((tr, tc), lambda i, j: (i, j))
    shp = jax.ShapeDtypeStruct((r, c), F32)
    return pl.pallas_call(body, out_shape=(shp, shp, shp), grid=(r // tr, c // tc), in_specs=[blk] * 4,
                          out_specs=(blk,) * 3, name=name, compiler_params=_params(("parallel", "parallel")))(w, g, m, v)


def _perm_cols(w):
    z, xbc, dt = w[:, 0:2048], w[:, 2048:5120], w[:, 5120:5184]
    q, k, v, gate = w[:, 5184:5952], w[:, 5952:6720], w[:, 6720:7488], w[:, 7488:9536]
    kv = []
    for g in range(3):
        for p in range(2):
            lo = 256 * g + 128 * p
            kv += [k[:, lo:lo + 128], v[:, lo:lo + 128]]
    pad = jnp.zeros((w.shape[0], UW - IN_COLS), w.dtype)
    return jnp.concatenate([z, gate, xbc] + kv + [q, dt, pad], axis=1)


def _unperm_cols(wp):
    z, gate, xbc = wp[:, OZ:OZ + 2048], wp[:, OGATE:OGATE + 2048], wp[:, OXBC:OXBC + CONVD]
    q, dt = wp[:, OQ:OQ + 768], wp[:, ODT:ODT + 64]
    ks, vs = [], []
    for g in range(3):
        for p in range(2):
            lo = OKV + 128 * (4 * g + 2 * p)
            ks.append(wp[:, lo:lo + 128])
            vs.append(wp[:, lo + 128:lo + 256])
    return jnp.concatenate([z, xbc, dt, q] + ks + vs + [gate], axis=1)


def _segments():
    segs = [(0, 2048), (7488, 9536), (2048, 5120)]
    for g in range(3):
        for p in range(2):
            lo = 256 * g + 128 * p
            segs += [(5952 + lo, 5952 + lo + 128), (6720 + lo, 6720 + lo + 128)]
    segs += [(5184, 5952), (5120, 5184)]
    out, pos = [], 0
    for a, b in segs:
        out.append((a, b, pos))
        pos += b - a
    return out


SHARD_COLS = IN_COLS // 4


def _perm_from_shards(w_shards):
    pieces = []
    for a, b, _ in _segments():
        while a < b:
            s = a // SHARD_COLS
            e = min(b, (s + 1) * SHARD_COLS)
            pieces.append(w_shards[s][:, a - s * SHARD_COLS:e - s * SHARD_COLS])
            a = e
    pieces.append(jnp.zeros((w_shards.shape[1], UW - IN_COLS), w_shards.dtype))
    return jnp.concatenate(pieces, axis=1)


def _shards_from_perm(wp):
    segs = sorted(_segments())
    shards = []
    for s in range(4):
        lo, hi = s * SHARD_COLS, (s + 1) * SHARD_COLS
        pieces = []
        for a, b, pos in segs:
            x, y = max(a, lo), min(b, hi)
            if x < y:
                pieces.append(wp[:, pos + x - a:pos + y - a])
        shards.append(jnp.concatenate(pieces, axis=1))
    return jnp.stack(shards)


def _lanes128(*vecs):
    v = jnp.concatenate([a.reshape(-1) for a in vecs])
    return jnp.pad(v, (0, 128 - v.shape[0])).reshape(1, 128)


EARLY = ("w_proj_ssd", "w_proj_attn", "w_out", "w_up", "w_down")


def _weights_of(gathered):
    g_ps, g_pa, g_o, g_up, g_dn = gathered
    return {"w_proj_ssd": g_ps.reshape(DI, D), "w_proj_attn": g_pa, "w_out": g_o.reshape(D, D), "w_up": g_up,
            "w_down": g_dn.reshape(DFF, D)}


def _local_grads(x, tgt, wts, sm, rs_idx=None):
    row = lambda a: a.reshape(1, -1)
    bg_row, cb_row = row(sm["b_gate"]), row(sm["conv_b"])
    par = jnp.concatenate([_lanes128(sm["dt_bias_f"], sm["dt_bias_b"]), _lanes128(sm["a_log_f"], sm["a_log_b"]),
                           jnp.zeros((6, 128), F32)], axis=0)
    dsk_row = row(jnp.repeat(sm["d_skip"], HP))
    nw_row = row(sm["ssd_norm_w"])
    g1, b1, g2, b2 = row(sm["ln1_g"]), row(sm["ln1_b"]), row(sm["ln2_g"]), row(sm["ln2_b"])

    xb = x.astype(BF16)
    u, gathered = _in_proj(xb, wts["w_in_p"], side=_gather_side(wts["pending"]) if "pending" in wts else None)
    if gathered:
        wts = {**wts, **_weights_of(gathered)}
    xbc = _conv_fwd(u, sm["conv_w"], cb_row)
    y_f, st_f = _ssd_fwd(xbc, u, par, rev=False)
    y_fb, st_b = _ssd_fwd(xbc, u, par, y_f, rev=True)
    s_out = _gatenorm_fwd(y_fb, xbc, u, dsk_row, nw_row)
    y_ssd = _mm_nn(s_out, wts["w_proj_ssd"], tm=512, tn=1024, name="proj_ssd")
    att_o, att_l = [], []
    for g in range(3):
        o, l = _attn_fwd(u, g)
        att_o.append(o)
        att_l.append(l)
    att, y_att = _combine_proj(att_o, att_l, wts["w_proj_attn"])
    mixin, pre1, h1 = _mix_out_ln1(y_ssd, y_att, u, bg_row, x, wts["w_out"], g1, b1)
    up, act = _mlp_up(h1, wts["w_up"])
    dpre2, dpre2_b, acc2 = _mlp_down_ln2_loss(act, wts["w_down"], pre1, tgt, g1, b1, g2, b2)

    dw_down = _mm_tn(act, dpre2_b, tka=1024, tn=1024, tt=1024, name="dw_down")
    dup = _d_up(dpre2_b, wts["w_down"], up)
    dw_up = _mm_tn(h1, dup, tka=1024, tn=1024, tt=1024, name="dw_up", out_shards=4)
    dpre1, acc1 = _d_h1_ln1_bwd(dup, wts["w_up"], dpre2, pre1, g1, b1)
    dw_out = _mm_tn(mixin, dpre1, tka=1024, tn=1024, tt=1024, name="dw_out")
    dy_ssd, dy_att, du, dbg = _d_mixin_mix_bwd(dpre1, wts["w_out"], y_ssd, y_att, u, bg_row)
    dw_proj_ssd = _mm_tn(s_out, dy_ssd, tka=1024, tn=1024, tt=1024, name="dw_proj_ssd")
    ds_out = _mm_nt(dy_ssd, wts["w_proj_ssd"], tm=512, tk=1024, tc=1024, name="d_s_out")
    dw_proj_attn = _mm_tn(att, dy_att, tka=256, tn=256, tt=1024, name="dw_proj_attn", out_shards=4)
    do_g, e_g = _d_att_combine_bwd(dy_att, wts["w_proj_attn"], att_o, att_l)
    for g in range(3):
        du = _attn_dq(u, du, do_g[g], att_l[g], e_g[g], g)
        du = _attn_dkv(u, du, do_g[g], att_l[g], e_g[g], g)
    big = {
        "w_proj_ssd": dw_proj_ssd.reshape(4, DI // 4, D),
        "w_proj_attn": dw_proj_attn,
        "w_out": dw_out.reshape(4, D // 4, D),
        "w_up": dw_up,
        "w_down": dw_down.reshape(4, DFF // 4, D),
    }
    early = [big[n] for n in EARLY]
    dy, du, dnw, dds, recv = _gatenorm_bwd(ds_out, y_fb, xbc, u, du, dsk_row, nw_row,
                                           side=_swap_side(early) if rs_idx else None)
    if rs_idx:
        halves = [_add_half(g, r, rs_idx[0], f"rs_add_half_{n}") for g, r, n in zip(early, recv, EARLY)]
    dxs_f, dbc_f, ddt_f, sacc_f, recv = _ssd_bwd(xbc, u, par, dy, st_f, rev=False,
                                                 side=_step1_side([h[1] for h in halves]) if rs_idx else None)
    if rs_idx:
        k = len(EARLY)
        sums1 = [_rs_add1(h[0], ra, rb, rs_idx[1], f"rs_add1_{n}")
                 for h, ra, rb, n in zip(halves, recv[:k], recv[k:], EARLY)]
    dxs, dbc, ddt, sacc_b, recv = _ssd_bwd(
        xbc, u, par, dy, st_b, rev=True, add=(dxs_f, dbc_f, ddt_f),
        side=_step2_side([s1[2] for s1 in sums1], [s1[3] for s1 in sums1]) if rs_idx else None)
    pieces = None
    if rs_idx:
        pieces = {n: _rs_add2(s1[0], s1[1], ra, rb, rs_idx[1], f"rs_add2_{n}")
                  for s1, ra, rb, n in zip(sums1, recv[:k], recv[k:], EARLY)}
    dpre_c, dcw, dcb = _conv_dpre(u, dxs, dy, dbc, dsk_row, sm["conv_w"], cb_row)
    du = _conv_dx(du, dpre_c, sm["conv_w"])
    du = _dt_bwd(du, ddt)
    dw_in_p = _mm_tn(xb, du, tka=1024, tn=2432, tt=1024, name="dw_in")
    dx = _mm_nt(du, wts["w_in_p"], tm=1024, tk=1024, tc=2432, name="d_x", add=dpre1, add_scale=ALPHA)

    sacc = sacc_f + sacc_b
    small = {
        "b_gate": dbg[0], "conv_w": dcw[0:KCONV], "conv_b": dcb[0],
        "dt_bias_f": sacc[0, 0:32], "dt_bias_b": sacc[0, 32:64], "a_log_f": sacc[1, 0:32], "a_log_b": sacc[1, 32:64],
        "d_skip": dds[0, 0:32], "ssd_norm_w": dnw[0],
        "ln1_g": acc1[0], "ln1_b": acc1[1], "ln2_g": acc2[0], "ln2_b": acc2[1], "loss": acc2[2, 0:1],
    }
    big["w_in"] = _shards_from_perm(dw_in_p)
    return dx, big, small, pieces


HBM_SPEC = pl.BlockSpec(memory_space=pl.ANY)


def _place():
    x, y, c = lax.axis_index("x"), lax.axis_index("y"), lax.axis_index("c")
    chips = [(1 - x, y), (x, 1 - y), (1 - x, 1 - y)]
    return x, y, c, chips


def _gather_phases(n):
    def tools(ins, outs, send_sems, recv_sems):
        x, y, c, _ = _place()
        slots = (2 * x + y, 2 * (1 - x) + y, 2 * x + 1 - y, 2 * (1 - x) + 1 - y)
        peers = ((1 - x, y, c), (x, 1 - y, c), (x, y, 1 - c))

        def copy(w, k, src, dst, to):
            return pltpu.make_async_remote_copy(src_ref=src, dst_ref=dst, send_sem=send_sems.at[w, k],
                                                recv_sem=recv_sems.at[w, k], device_id=to, device_id_type=MESH)

        def rows(w, core, part):
            rh = ins[w].shape[0] // 2
            if part is None:
                return pl.ds(core * rh, rh)
            return pl.ds(core * rh + part * (rh // 2), rh // 2)

        def same(w, k, slot, core, part, to):
            blk = outs[w].at[slot, rows(w, core, part), :]
            return copy(w, k, blk, blk, to)

        def sends(w):
            q, q_x, q_y, q_d = slots
            x_nbr, y_nbr, sibling = peers
            mine = rows(w, c, None)
            mk = functools.partial
            return [mk(copy, w, 0, ins[w].at[mine, :], outs[w].at[q, mine, :], x_nbr),
                    mk(copy, w, 1, ins[w].at[mine, :], outs[w].at[q, mine, :], y_nbr),
                    mk(same, w, 2, q_x, c, 0, y_nbr), mk(same, w, 3, q_y, c, 1, x_nbr),
                    mk(same, w, 4, q_x, c, None, sibling), mk(same, w, 5, q_y, c, None, sibling),
                    mk(same, w, 6, q_d, c, 0, sibling), mk(same, w, 7, q_d, c, 1, sibling),
                    mk(copy, w, 8, ins[w], outs[w].at[q], sibling)]

        return c, slots, peers, same, sends

    def first(*refs):
        _, _, _, _, sends = tools(*refs)
        for w in range(n):
            cps = sends(w)
            for k in (8, 0, 1):
                cps[k]().start()

    def second(*refs):
        c, (_, q_x, q_y, _), (x_nbr, y_nbr, _), same, sends = tools(*refs)
        for w in range(n):
            cps = sends(w)
            same(w, 0, q_x, c, None, x_nbr).wait_recv()
            cps[2]().start()
            cps[4]().start()
            same(w, 1, q_y, c, None, y_nbr).wait_recv()
            cps[3]().start()
            cps[5]().start()

    def third(*refs):
        c, (_, _, _, q_d), (x_nbr, y_nbr, _), same, sends = tools(*refs)
        for w in range(n):
            cps = sends(w)
            same(w, 2, q_d, c, 0, y_nbr).wait_recv()
            cps[6]().start()
            same(w, 3, q_d, c, 1, x_nbr).wait_recv()
            cps[7]().start()

    def last(*refs):
        c, (_, q_x, q_y, q_d), (_, _, sibling), same, sends = tools(*refs)
        for w in range(n):
            same(w, 4, q_x, 1 - c, None, sibling).wait_recv()
            same(w, 5, q_y, 1 - c, None, sibling).wait_recv()
            same(w, 6, q_d, 1 - c, 0, sibling).wait_recv()
            same(w, 7, q_d, 1 - c, 1, sibling).wait_recv()
            sends(w)[8]().wait_recv()
        for w in range(n):
            for mk_cp in sends(w):
                mk_cp().wait_send()

    return first, second, third, last


def _gather_side(shards):
    first, second, third, last = _gather_phases(len(shards))
    shapes = tuple(jax.ShapeDtypeStruct((4,) + s.shape, s.dtype) for s in shards)
    return _Side(tuple(shards), shapes, (len(shards), 9), None, ((0.0, first), (0.36, second), (0.58, third), (1.0, last)))


def _allgather_weights(shards):
    n = len(shards)

    def body(*refs):
        ins, outs = refs[:n], refs[n:2 * n]
        send_sems, recv_sems = refs[2 * n:]
        x, y, c, _ = _place()
        q, q_x, q_y, q_d = 2 * x + y, 2 * (1 - x) + y, 2 * x + 1 - y, 2 * (1 - x) + 1 - y
        x_nbr, y_nbr, sibling = (1 - x, y, c), (x, 1 - y, c), (x, y, 1 - c)

        def copy(w, k, src, dst, to):
            return pltpu.make_async_remote_copy(src_ref=src, dst_ref=dst, send_sem=send_sems.at[w, k],
                                                recv_sem=recv_sems.at[w, k], device_id=to, device_id_type=MESH)

        def rows(w, core, part):
            rh = ins[w].shape[0] // 2
            if part is None:
                return pl.ds(core * rh, rh)
            return pl.ds(core * rh + part * (rh // 2), rh // 2)

        def same(w, k, slot, core, part, to):
            blk = outs[w].at[slot, rows(w, core, part), :]
            return copy(w, k, blk, blk, to)

        started = []
        for w in range(n):
            cp = copy(w, 8, ins[w], outs[w].at[q], sibling)
            cp.start()
            started.append(cp)
            mine = rows(w, c, None)
            for k, to in ((0, x_nbr), (1, y_nbr)):
                cp = copy(w, k, ins[w].at[mine, :], outs[w].at[q, mine, :], to)
                cp.start()
                started.append(cp)
        for w in range(n):
            same(w, 0, q_x, c, None, x_nbr).wait_recv()
            for cp in (same(w, 2, q_x, c, 0, y_nbr), same(w, 4, q_x, c, None, sibling)):
                cp.start()
                started.append(cp)
            same(w, 1, q_y, c, None, y_nbr).wait_recv()
            for cp in (same(w, 3, q_y, c, 1, x_nbr), same(w, 5, q_y, c, None, sibling)):
                cp.start()
                started.append(cp)
        for w in range(n):
            same(w, 2, q_d, c, 0, y_nbr).wait_recv()
            cp = same(w, 6, q_d, c, 0, sibling)
            cp.start()
            started.append(cp)
            same(w, 3, q_d, c, 1, x_nbr).wait_recv()
            cp = same(w, 7, q_d, c, 1, sibling)
            cp.start()
            started.append(cp)
        for w in range(n):
            same(w, 4, q_x, 1 - c, None, sibling).wait_recv()
            same(w, 5, q_y, 1 - c, None, sibling).wait_recv()
            same(w, 6, q_d, 1 - c, 0, sibling).wait_recv()
            same(w, 7, q_d, 1 - c, 1, sibling).wait_recv()
            copy(w, 8, ins[w], outs[w].at[q], sibling).wait_recv()
        for cp in started:
            cp.wait_send()

    return pl.pallas_call(
        body, out_shape=[jax.ShapeDtypeStruct((4,) + s.shape, s.dtype) for s in shards],
        in_specs=[HBM_SPEC] * n, out_specs=[HBM_SPEC] * n,
        scratch_shapes=[pltpu.SemaphoreType.DMA((n, 9)), pltpu.SemaphoreType.DMA((n, 9))],
        name="allgather_weights")(*shards)


def _swap_halves(grads):
    n = len(grads)

    def body(*refs):
        ins, outs = refs[:n], refs[n:2 * n]
        send_sems, recv_sems = refs[2 * n:]
        x, y, c, _ = _place()
        copies = []
        for w in range(n):
            rh = ins[w].shape[1] // 2
            for p in range(4):
                cp = pltpu.make_async_remote_copy(
                    src_ref=ins[w].at[p, pl.ds((1 - c) * rh, rh), :], dst_ref=outs[w].at[p],
                    send_sem=send_sems.at[w, p], recv_sem=recv_sems.at[w, p],
                    device_id=(x, y, 1 - c), device_id_type=MESH)
                cp.start()
                copies.append(cp)
        for cp in copies:
            cp.wait()

    return pl.pallas_call(
        body, out_shape=[jax.ShapeDtypeStruct((4, g.shape[1] // 2, g.shape[2]), F32) for g in grads],
        in_specs=[HBM_SPEC] * n, out_specs=[HBM_SPEC] * n,
        scratch_shapes=[pltpu.SemaphoreType.DMA((n, 4)), pltpu.SemaphoreType.DMA((n, 4))],
        name="rs_swap_halves")(*grads)


def _rs_step1(parts):
    n = len(parts)

    def body(*refs):
        ins, out_a, out_b = refs[:n], refs[n:2 * n], refs[2 * n:3 * n]
        send_sems, recv_sems = refs[3 * n:]
        x, y, c, _ = _place()
        copies = []
        for w in range(n):
            rq = ins[w].shape[1] // 2
            for i in range(2):
                copies.append(pltpu.make_async_remote_copy(
                    src_ref=ins[w].at[2 * (1 - x) + i, pl.ds(0, rq), :], dst_ref=out_a[w].at[i],
                    send_sem=send_sems.at[w, i], recv_sem=recv_sems.at[w, i],
                    device_id=(1 - x, y, c), device_id_type=MESH))
                copies.append(pltpu.make_async_remote_copy(
                    src_ref=ins[w].at[2 * i + 1 - y, pl.ds(rq, rq), :], dst_ref=out_b[w].at[i],
                    send_sem=send_sems.at[w, 2 + i], recv_sem=recv_sems.at[w, 2 + i],
                    device_id=(x, 1 - y, c), device_id_type=MESH))
        for cp in copies:
            cp.start()
        for cp in copies:
            cp.wait()

    quarter = lambda p: jax.ShapeDtypeStruct((2, p.shape[1] // 2, p.shape[2]), p.dtype)
    outs = pl.pallas_call(
        body, out_shape=[quarter(p) for p in parts] * 2,
        in_specs=[HBM_SPEC] * n, out_specs=[HBM_SPEC] * (2 * n),
        scratch_shapes=[pltpu.SemaphoreType.DMA((n, 4)), pltpu.SemaphoreType.DMA((n, 4))],
        name="rs_step1")(*parts)
    return outs[:n], outs[n:]


def _rs_step2(tas, tbs):
    n = len(tas)

    def body(*refs):
        in_a, in_b, out_a, out_b = refs[:n], refs[n:2 * n], refs[2 * n:3 * n], refs[3 * n:4 * n]
        send_sems, recv_sems = refs[4 * n:]
        x, y, c, _ = _place()
        copies = []
        for w in range(n):
            copies.append(pltpu.make_async_remote_copy(
                src_ref=in_a[w].at[1 - y], dst_ref=out_a[w], send_sem=send_sems.at[w, 0], recv_sem=recv_sems.at[w, 0],
                device_id=(x, 1 - y, c), device_id_type=MESH))
            copies.append(pltpu.make_async_remote_copy(
                src_ref=in_b[w].at[1 - x], dst_ref=out_b[w], send_sem=send_sems.at[w, 1], recv_sem=recv_sems.at[w, 1],
                device_id=(1 - x, y, c), device_id_type=MESH))
        for cp in copies:
            cp.start()
        for cp in copies:
            cp.wait()

    one = lambda p: jax.ShapeDtypeStruct(p.shape[1:], p.dtype)
    outs = pl.pallas_call(
        body, out_shape=[one(p) for p in tas] + [one(p) for p in tbs],
        in_specs=[HBM_SPEC] * (2 * n), out_specs=[HBM_SPEC] * (2 * n),
        scratch_shapes=[pltpu.SemaphoreType.DMA((n, 2)), pltpu.SemaphoreType.DMA((n, 2))],
        name="rs_step2")(*tas, *tbs)
    return outs[:n], outs[n:]


class _Side(NamedTuple):
    ins: tuple
    out_shapes: tuple
    nsem: tuple
    make: Callable
    phases: tuple = ()


def _swap_copies(ins, outs, send_sems, recv_sems):
    x, y, c, _ = _place()
    copies = []
    for w in range(len(ins)):
        rh = ins[w].shape[1] // 2
        for p in range(4):
            copies.append(pltpu.make_async_remote_copy(
                src_ref=ins[w].at[p, pl.ds((1 - c) * rh, rh), :], dst_ref=outs[w].at[p],
                send_sem=send_sems.at[w, p], recv_sem=recv_sems.at[w, p],
                device_id=(x, y, 1 - c), device_id_type=MESH))
    return copies


def _swap_side(grads):
    shapes = tuple(jax.ShapeDtypeStruct((4, g.shape[1] // 2, g.shape[2]), F32) for g in grads)
    return _Side(tuple(grads), shapes, (len(grads), 4), _swap_copies)


def _step1_copies(ins, outs, send_sems, recv_sems):
    n = len(ins)
    out_a, out_b = outs[:n], outs[n:]
    x, y, c, _ = _place()
    copies = []
    for w in range(n):
        rq = ins[w].shape[1] // 2
        for i in range(2):
            copies.append(pltpu.make_async_remote_copy(
                src_ref=ins[w].at[2 * (1 - x) + i, pl.ds(0, rq), :], dst_ref=out_a[w].at[i],
                send_sem=send_sems.at[w, i], recv_sem=recv_sems.at[w, i],
                device_id=(1 - x, y, c), device_id_type=MESH))
            copies.append(pltpu.make_async_remote_copy(
                src_ref=ins[w].at[2 * i + 1 - y, pl.ds(rq, rq), :], dst_ref=out_b[w].at[i],
                send_sem=send_sems.at[w, 2 + i], recv_sem=recv_sems.at[w, 2 + i],
                device_id=(x, 1 - y, c), device_id_type=MESH))
    return copies


def _step1_side(parts):
    quarter = tuple(jax.ShapeDtypeStruct((2, p.shape[1] // 2, p.shape[2]), p.dtype) for p in parts)
    return _Side(tuple(parts), quarter + quarter, (len(parts), 4), _step1_copies)


def _step2_copies(ins, outs, send_sems, recv_sems):
    n = len(ins) // 2
    in_a, in_b, out_a, out_b = ins[:n], ins[n:], outs[:n], outs[n:]
    x, y, c, _ = _place()
    copies = []
    for w in range(n):
        copies.append(pltpu.make_async_remote_copy(
            src_ref=in_a[w].at[1 - y], dst_ref=out_a[w], send_sem=send_sems.at[w, 0], recv_sem=recv_sems.at[w, 0],
            device_id=(x, 1 - y, c), device_id_type=MESH))
        copies.append(pltpu.make_async_remote_copy(
            src_ref=in_b[w].at[1 - x], dst_ref=out_b[w], send_sem=send_sems.at[w, 1], recv_sem=recv_sems.at[w, 1],
            device_id=(1 - x, y, c), device_id_type=MESH))
    return copies


def _step2_side(tas, tbs):
    one = tuple(jax.ShapeDtypeStruct(p.shape[1:], p.dtype) for p in tuple(tas) + tuple(tbs))
    return _Side(tuple(tas) + tuple(tbs), one, (len(tas), 2), _step2_copies)


def _phases_of(side, n_steps):
    if side.phases:
        return [(min(int(f * n_steps), n_steps - 1), fn) for f, fn in side.phases]

    def start(*refs):
        for cp in side.make(*refs):
            cp.start()

    def wait(*refs):
        for cp in side.make(*refs):
            cp.wait()

    return [(0, start), (n_steps - 1, wait)]


def _run_side(side, name):
    n_in, n_out = len(side.ins), len(side.out_shapes)

    def body(*refs):
        for _, fn in _phases_of(side, 1):
            fn(refs[:n_in], refs[n_in:n_in + n_out], *refs[n_in + n_out:])

    return pl.pallas_call(
        body, out_shape=list(side.out_shapes), in_specs=[HBM_SPEC] * n_in, out_specs=[HBM_SPEC] * n_out,
        scratch_shapes=[pltpu.SemaphoreType.DMA(side.nsem), pltpu.SemaphoreType.DMA(side.nsem)], name=name)(*side.ins)


def _host_call(body, side, n_steps, *, out_shape, in_specs, out_specs, scratch_shapes, args, aliases, name, sem):
    n_in, n_out, n_scr = len(in_specs), len(out_shape), len(scratch_shapes)
    if side is None:
        outs = pl.pallas_call(body, out_shape=tuple(out_shape), grid=(n_steps,), in_specs=list(in_specs),
                              out_specs=tuple(out_specs), scratch_shapes=list(scratch_shapes),
                              input_output_aliases=aliases, name=name, compiler_params=_params(sem))(*args)
        return tuple(outs), ()
    ns_in, ns_out = len(side.ins), len(side.out_shapes)

    def wrapped(*refs):
        h_in, s_in = refs[:n_in], refs[n_in:n_in + ns_in]
        o0 = n_in + ns_in
        h_out, s_out = refs[o0:o0 + n_out], refs[o0 + n_out:o0 + n_out + ns_out]
        c0 = o0 + n_out + ns_out
        h_scr, sems = refs[c0:c0 + n_scr], refs[c0 + n_scr:]
        step = pl.program_id(0)
        phases = _phases_of(side, n_steps)
        for at, fn in phases[:-1]:
            pl.when(step == at)(functools.partial(fn, s_in, s_out, *sems))
        body(*h_in, *h_out, *h_scr)
        pl.when(step == phases[-1][0])(functools.partial(phases[-1][1], s_in, s_out, *sems))

    outs = pl.pallas_call(
        wrapped, out_shape=tuple(out_shape) + tuple(side.out_shapes), grid=(n_steps,),
        in_specs=list(in_specs) + [HBM_SPEC] * ns_in, out_specs=tuple(out_specs) + (HBM_SPEC,) * ns_out,
        scratch_shapes=list(scratch_shapes) + [pltpu.SemaphoreType.DMA(side.nsem), pltpu.SemaphoreType.DMA(side.nsem)],
        input_output_aliases=aliases, name=name, compiler_params=_params(sem))(*args, *side.ins)
    return tuple(outs[:n_out]), tuple(outs[n_out:])


def _join_halves(pieces):
    n = len(pieces)

    def body(*refs):
        outs = refs[n:2 * n]
        send_sems, recv_sems = refs[2 * n:]
        x, y, c, _ = _place()

        def copy(w, slot):
            return pltpu.make_async_remote_copy(
                src_ref=outs[w].at[slot], dst_ref=outs[w].at[slot], send_sem=send_sems.at[w], recv_sem=recv_sems.at[w],
                device_id=(x, y, 1 - c), device_id_type=MESH)

        for w in range(n):
            copy(w, c).start()
        for w in range(n):
            copy(w, 1 - c).wait_recv()
            copy(w, c).wait_send()

    return pl.pallas_call(
        body, out_shape=[jax.ShapeDtypeStruct(p.shape, F32) for p in pieces],
        in_specs=[HBM_SPEC] * n, out_specs=[HBM_SPEC] * n, input_output_aliases={w: w for w in range(n)},
        scratch_shapes=[pltpu.SemaphoreType.DMA((n,)), pltpu.SemaphoreType.DMA((n,))],
        name="rs_join_halves")(*pieces)


def _add_tile_rows(rh, c):
    for cand in (512, 256, 128, 64, 32, 16, 8):
        if rh % cand == 0 and cand * c * 4 <= 2 ** 21:
            return cand
    return rh


def _add_half(grad, recv, c_idx, name):
    _, r, cc = grad.shape
    rh = r // 2
    tr = _add_tile_rows(rh, cc)
    nb = rh // tr

    def body(c_ref, g_ref, r_ref, o_ref, ob_ref):
        del c_ref
        s = g_ref[...] + r_ref[...]
        o_ref[...] = s
        ob_ref[...] = s.astype(BF16)

    blk = pl.BlockSpec((None, tr, cc), lambda p, i, c_ref: (p, i, 0))
    grid_spec = pltpu.PrefetchScalarGridSpec(
        num_scalar_prefetch=1, grid=(4, nb),
        in_specs=[pl.BlockSpec((None, tr, cc), lambda p, i, c_ref: (p, c_ref[0] * nb + i, 0)), blk],
        out_specs=(blk, blk))
    return pl.pallas_call(
        body, out_shape=(jax.ShapeDtypeStruct((4, rh, cc), F32), jax.ShapeDtypeStruct((4, rh, cc), BF16)),
        grid_spec=grid_spec, name=name, compiler_params=_params(("parallel", "parallel")))(c_idx, grad, recv)


def _rs_add1(part, recv_a, recv_b, xy_idx, name):
    _, rh, cc = part.shape
    rq = rh // 2
    tr = _add_tile_rows(rq, cc)
    nb = rq // tr

    def body(xy_ref, pa_ref, pb_ref, ra_ref, rb_ref, ta_ref, tb_ref, tab_ref, tbb_ref):
        del xy_ref
        ta = pa_ref[...] + ra_ref[...].astype(F32)
        tb = pb_ref[...] + rb_ref[...].astype(F32)
        ta_ref[...] = ta
        tb_ref[...] = tb
        tab_ref[...] = ta.astype(BF16)
        tbb_ref[...] = tb.astype(BF16)

    blk = pl.BlockSpec((None, tr, cc), lambda i, j, xy: (i, j, 0))
    grid_spec = pltpu.PrefetchScalarGridSpec(
        num_scalar_prefetch=1, grid=(2, nb),
        in_specs=[pl.BlockSpec((None, tr, cc), lambda i, j, xy: (2 * xy[0] + i, j, 0)),
                  pl.BlockSpec((None, tr, cc), lambda i, j, xy: (2 * i + xy[1], nb + j, 0)), blk, blk],
        out_specs=(blk, blk, blk, blk))
    f32s, b16s = jax.ShapeDtypeStruct((2, rq, cc), F32), jax.ShapeDtypeStruct((2, rq, cc), BF16)
    return pl.pallas_call(body, out_shape=(f32s, f32s, b16s, b16s), grid_spec=grid_spec, name=name,
                          compiler_params=_params(("parallel", "parallel")))(xy_idx, part, part, recv_a, recv_b)


def _rs_add2(ta, tb, recv_a, recv_b, xy_idx, name):
    _, rq, cc = ta.shape
    tr = _add_tile_rows(rq, cc)
    nb = rq // tr

    def body(xy_ref, ta_ref, tb_ref, ra_ref, rb_ref, o_ref):
        del xy_ref
        s = pl.program_id(0)
        fa = ta_ref[...] + ra_ref[...].astype(F32)
        fb = tb_ref[...] + rb_ref[...].astype(F32)
        o_ref[...] = jnp.where(s == 0, fa, fb)

    rblk = pl.BlockSpec((tr, cc), lambda s, j, xy: (j, 0))
    grid_spec = pltpu.PrefetchScalarGridSpec(
        num_scalar_prefetch=1, grid=(2, nb),
        in_specs=[pl.BlockSpec((None, tr, cc), lambda s, j, xy: (xy[1], j, 0)),
                  pl.BlockSpec((None, tr, cc), lambda s, j, xy: (xy[0], j, 0)), rblk, rblk],
        out_specs=pl.BlockSpec((None, tr, cc), lambda s, j, xy: (xy[2], s * nb + j, 0)))
    return pl.pallas_call(body, out_shape=jax.ShapeDtypeStruct((2, 2 * rq, cc), F32), grid_spec=grid_spec, name=name,
                          compiler_params=_params(("parallel", "parallel")))(xy_idx, ta, tb, recv_a, recv_b)


def _allreduce_small(slab):
    r = slab.shape[0]

    def body(x_ref, o_ref, buf, send_sems, recv_sems):
        x, y, c, _ = _place()
        me = 4 * x + 2 * y + c
        buf[me] = x_ref[...]
        peers = []
        for k in range(1, 8):
            kx, ky, kc = (k >> 2) & 1, (k >> 1) & 1, k & 1
            peers.append((x + kx - 2 * x * kx, y + ky - 2 * y * ky, c + kc - 2 * c * kc))

        def copy(k, slot):
            return pltpu.make_async_remote_copy(src_ref=x_ref, dst_ref=buf.at[slot], send_sem=send_sems.at[k],
                                                recv_sem=recv_sems.at[k], device_id=peers[k], device_id_type=MESH)

        for k in range(7):
            copy(k, me).start()
        for k, (px, py, pc) in enumerate(peers):
            copy(k, 4 * px + 2 * py + pc).wait_recv()
        for k in range(7):
            copy(k, me).wait_send()
        acc = buf[0]
        for j in range(1, 8):
            acc = acc + buf[j]
        o_ref[...] = acc

    vm = pl.BlockSpec(memory_space=pltpu.VMEM)
    return pl.pallas_call(
        body, out_shape=jax.ShapeDtypeStruct((r, 128), F32), in_specs=[vm], out_specs=vm,
        scratch_shapes=[pltpu.VMEM((8, r, 128), F32), pltpu.SemaphoreType.DMA((7,)), pltpu.SemaphoreType.DMA((7,))],
        name="allreduce_small")(slab)


def _pack(arrs):
    rows = []
    for a in arrs:
        v = a.reshape(-1)
        v = jnp.pad(v, (0, (-v.shape[0]) % 128))
        rows.append(v.reshape(-1, 128))
    slab = jnp.concatenate(rows, axis=0)
    return jnp.pad(slab, ((0, (-slab.shape[0]) % 8), (0, 0)))


def _unpack(slab, shapes):
    out, r0 = [], 0
    for shp in shapes:
        size = math.prod(shp)
        nr = -(-size // 128)
        out.append(slab[r0:r0 + nr].reshape(-1)[:size].reshape(shp))
        r0 += nr
    return out


BIG = ("w_in", "w_proj_ssd", "w_proj_attn", "w_out", "w_up", "w_down")
SMALL = ("b_gate", "conv_w", "conv_b", "dt_bias_f", "dt_bias_b", "a_log_f", "a_log_b", "d_skip", "ssd_norm_w",
         "ln1_g", "ln1_b", "ln2_g", "ln2_b")
ORDER = ("w_in", "b_gate", "conv_w", "conv_b", "dt_bias_f", "dt_bias_b", "a_log_f", "a_log_b", "d_skip", "ssd_norm_w",
         "w_proj_ssd", "w_proj_attn", "w_out", "ln1_g", "ln1_b", "w_up", "w_down", "ln2_g", "ln2_b")


def kernel(x, w_in, b_gate, conv_w, conv_b, dt_bias_f, dt_bias_b, a_log_f, a_log_b, d_skip, ssd_norm_w, w_proj_ssd, w_proj_attn, w_out, ln1_g, ln1_b, w_up, w_down, ln2_g, ln2_b, loss_target, m_w_in, m_b_gate, m_conv_w, m_conv_b, m_dt_bias_f, m_dt_bias_b, m_a_log_f, m_a_log_b, m_d_skip, m_ssd_norm_w, m_w_proj_ssd, m_w_proj_attn, m_w_out, m_ln1_g, m_ln1_b, m_w_up, m_w_down, m_ln2_g, m_ln2_b, v_w_in, v_b_gate, v_conv_w, v_conv_b, v_dt_bias_f, v_dt_bias_b, v_a_log_f, v_a_log_b, v_d_skip, v_ssd_norm_w, v_w_proj_ssd, v_w_proj_attn, v_w_out, v_ln1_g, v_ln1_b, v_w_up, v_w_down, v_ln2_g, v_ln2_b):
    w = dict(w_in=w_in, b_gate=b_gate, conv_w=conv_w, conv_b=conv_b, dt_bias_f=dt_bias_f, dt_bias_b=dt_bias_b,
             a_log_f=a_log_f, a_log_b=a_log_b, d_skip=d_skip, ssd_norm_w=ssd_norm_w, w_proj_ssd=w_proj_ssd,
             w_proj_attn=w_proj_attn, w_out=w_out, ln1_g=ln1_g, ln1_b=ln1_b, w_up=w_up, w_down=w_down, ln2_g=ln2_g, ln2_b=ln2_b)
    m = dict(w_in=m_w_in, b_gate=m_b_gate, conv_w=m_conv_w, conv_b=m_conv_b, dt_bias_f=m_dt_bias_f, dt_bias_b=m_dt_bias_b,
             a_log_f=m_a_log_f, a_log_b=m_a_log_b, d_skip=m_d_skip, ssd_norm_w=m_ssd_norm_w, w_proj_ssd=m_w_proj_ssd,
             w_proj_attn=m_w_proj_attn, w_out=m_w_out, ln1_g=m_ln1_g, ln1_b=m_ln1_b, w_up=m_w_up, w_down=m_w_down,
             ln2_g=m_ln2_g, ln2_b=m_ln2_b)
    v = dict(w_in=v_w_in, b_gate=v_b_gate, conv_w=v_conv_w, conv_b=v_conv_b, dt_bias_f=v_dt_bias_f, dt_bias_b=v_dt_bias_b,
             a_log_f=v_a_log_f, a_log_b=v_a_log_b, d_skip=v_d_skip, ssd_norm_w=v_ssd_norm_w, w_proj_ssd=v_w_proj_ssd,
             w_proj_attn=v_w_proj_attn, w_out=v_w_out, ln1_g=v_ln1_g, ln1_b=v_ln1_b, w_up=v_w_up, w_down=v_w_down,
             ln2_g=v_ln2_g, ln2_b=v_ln2_b)
    xi, yi, ci = lax.axis_index("x"), lax.axis_index("y"), lax.axis_index("c")
    shard = 2 * xi + yi

    (g_in,) = _run_side(_gather_side([w["w_in"].astype(BF16)]), "allgather_w_in")
    wts = {"w_in_p": _perm_from_shards(g_in), "pending": [w[n].astype(BF16) for n in EARLY]}

    cw_slab = jnp.zeros((KCONV, 4, CONVD // 4), F32)
    cw_slab = lax.dynamic_update_slice(cw_slab, conv_w[:, None, :] * 0.5, (0, shard, 0))
    conv_w_all = _unpack(_allreduce_small(_pack([cw_slab])), [(KCONV, CONVD)])[0]

    sm = {n: w[n] for n in SMALL}
    sm["conv_w"] = conv_w_all
    c_idx = jnp.reshape(ci, (1,)).astype(jnp.int32)
    xy_idx = jnp.stack([xi, yi, ci]).astype(jnp.int32)
    dx, big, small, pieces = _local_grads(x[0], loss_target[0], wts, sm, rs_idx=(c_idx, xy_idx))

    names = list(SMALL) + ["loss"]
    shapes = [small[n].shape for n in names]
    red = dict(zip(names, _unpack(_allreduce_small(_pack([small[n] for n in names])), shapes)))
    loss = red["loss"].reshape(())
    gsm = {n: red[n] for n in SMALL}
    conv_w_grad_shard = lax.dynamic_slice_in_dim(gsm["conv_w"].reshape(KCONV, 4, CONVD // 4), shard, 1, axis=1)
    gsm["conv_w"] = conv_w_grad_shard.reshape(KCONV, CONVD // 4)

    g = big["w_in"]
    half = _add_half(g, _run_side(_swap_side([g]), "rs_swap_halves")[0], c_idx, "rs_add_half_w_in")
    ra, rb = _run_side(_step1_side([half[1]]), "rs_step1")
    s1 = _rs_add1(half[0], ra, rb, xy_idx, "rs_add1_w_in")
    ra2, rb2 = _run_side(_step2_side([s1[2]], [s1[3]]), "rs_step2")
    pieces["w_in"] = _rs_add2(s1[0], s1[1], ra2, rb2, xy_idx, "rs_add2_w_in")
    joined = _join_halves([pieces[n] for n in BIG])
    gbig = {n: j.reshape(w[n].shape) for n, j in zip(BIG, joined)}

    grads, deltas, new_m, new_v = {}, {}, {}, {}
    for n in BIG:
        grads[n] = gbig[n]
        if n == "w_in":
            gt = gbig[n].T
            dlt, nmt, nvt = _adamw(w[n].T, gt, m[n].T, v[n].T, f"adamw_{n}")
            grads[n], deltas[n], new_m[n], new_v[n] = gt.T, dlt.T, nmt.T, nvt.T
            continue
        deltas[n], new_m[n], new_v[n] = _adamw(w[n], gbig[n], m[n], v[n], f"adamw_{n}")
    sshapes = [w[n].shape for n in SMALL]
    d_s, m_s, v_s = _adamw(_pack([w[n] for n in SMALL]), _pack([gsm[n] for n in SMALL]),
                           _pack([m[n] for n in SMALL]), _pack([v[n] for n in SMALL]), "adamw_small")
    for n, dd, mm, vv in zip(SMALL, _unpack(d_s, sshapes), _unpack(m_s, sshapes), _unpack(v_s, sshapes)):
        grads[n], deltas[n], new_m[n], new_v[n] = gsm[n], dd, mm, vv

    return (loss, dx[None], *[grads[n] for n in ORDER], *[deltas[n] for n in ORDER],
            *[new_m[n] for n in ORDER], *[new_v[n] for n in ORDER])
```

```python
import functools
import math
from typing import Callable, NamedTuple

import jax
import numpy as np
import jax.numpy as jnp
from jax import lax
from jax.experimental import pallas as pl
from jax.experimental.pallas import tpu as pltpu

F32, BF16 = jnp.float32, jnp.bfloat16
MESH = pl.DeviceIdType.MESH

D = 1024
DI = 2048
NH = 32
HP = 64
NG = 4
NS = 128
Q = 128
CONVD = 3072
KCONV = 5
DFF = 4096
AH = 64
ATT_HALF = 64
DILATIONS = (1, 4, 16)
IN_COLS = 9536
OZ, OGATE, OXBC, OKV, OQ, ODT, UW = 0, 2048, 4096, 7168, 8704, 9472, 9728
ALPHA = 2.0 ** 0.25
NORM_EPS = 1e-5
ADAM_LR, ADAM_B1, ADAM_B2, ADAM_EPS, ADAM_WD, ADAM_STEP = 0.001, 0.9, 0.999, 1e-8, 0.01, 10
VMEM_LIMIT = 56 * 2 ** 20
NEG = -1e30


def _params(sem):
    return pltpu.CompilerParams(dimension_semantics=sem, vmem_limit_bytes=VMEM_LIMIT)


def _sigmoid(x):
    return 1.0 / (1.0 + jnp.exp(-x))


def _softplus(x):
    e = jnp.exp(-jnp.abs(x))
    small = e * (1.0 - e * (0.5 - e * (1.0 / 3.0)))
    return jnp.maximum(x, 0.0) + jnp.where(e < 0.01, small, jnp.log(1.0 + e))


def _split3(a):
    hi = a.astype(BF16)
    r = a - hi.astype(F32)
    mid = r.astype(BF16)
    lo = (r - mid.astype(F32)).astype(BF16)
    return hi, mid, lo


def _dot01(a, m01):
    hi, mid, lo = _split3(a)
    d = lambda p: jnp.dot(p, m01, preferred_element_type=F32)
    return d(hi) + d(mid) + d(lo)


def _dot01_l(m01, a):
    hi, mid, lo = _split3(a)
    d = lambda p: jnp.dot(m01, p, preferred_element_type=F32)
    return d(hi) + d(mid) + d(lo)


def _dot_nt(a, b):
    return lax.dot_general(a, b, (((1,), (1,)), ((), ())), preferred_element_type=F32)


def _iota(shape, dim):
    return lax.broadcasted_iota(jnp.int32, shape, dim)


def _mm_nn(a, b, *, tm, tn, name, out_dtype=F32):
    m, k = a.shape
    if b.ndim == 3:
        assert tn == b.shape[2]
        n = b.shape[0] * b.shape[2]
        b_spec = pl.BlockSpec((None, k, tn), lambda j, i: (j, 0, 0))
    else:
        n = b.shape[1]
        b_spec = pl.BlockSpec((k, tn), lambda j, i: (0, j))

    def body(a_ref, b_ref, o_ref):
        o_ref[...] = jnp.dot(a_ref[...].astype(BF16), b_ref[...], preferred_element_type=F32).astype(out_dtype)

    return pl.pallas_call(
        body, out_shape=jax.ShapeDtypeStruct((m, n), out_dtype), grid=(n // tn, m // tm),
        in_specs=[pl.BlockSpec((tm, k), lambda j, i: (i, 0)), b_spec],
        out_specs=pl.BlockSpec((tm, tn), lambda j, i: (i, j)),
        name=name, compiler_params=_params(("parallel", "parallel")))(a, b)


def _mm_nt(a, b, *, tm, tk, tc, name, add=None, add_scale=1.0):
    m, n = a.shape
    if b.ndim == 3:
        assert tc == b.shape[2]
        k, nc = b.shape[1], b.shape[0]
        b_spec = pl.BlockSpec((None, tk, tc), lambda j, i, c: (c, j, 0))
    else:
        k, nc = b.shape[0], n // tc
        b_spec = pl.BlockSpec((tk, tc), lambda j, i, c: (j, c))

    def body(*refs):
        if add is None:
            a_ref, b_ref, o_ref = refs
        else:
            a_ref, b_ref, add_ref, o_ref = refs
        c = pl.program_id(2)
        part = _dot_nt(a_ref[...].astype(BF16), b_ref[...])

        @pl.when(c == 0)
        def _():
            if add is None:
                o_ref[...] = part
            else:
                o_ref[...] = part + add_scale * add_ref[...]

        @pl.when(c > 0)
        def _():
            o_ref[...] += part

    in_specs = [pl.BlockSpec((tm, tc), lambda j, i, c: (i, c)), b_spec]
    args = [a, b]
    if add is not None:
        in_specs.append(pl.BlockSpec((tm, tk), lambda j, i, c: (i, j)))
        args.append(add)
    return pl.pallas_call(
        body, out_shape=jax.ShapeDtypeStruct((m, k), F32), grid=(k // tk, m // tm, nc),
        in_specs=in_specs, out_specs=pl.BlockSpec((tm, tk), lambda j, i, c: (i, j)),
        name=name, compiler_params=_params(("parallel", "parallel", "arbitrary")))(*args)


def _mm_tn(a, b, *, tka, tn, tt, name, out_shards=None):
    t, ka = a.shape
    n = b.shape[1]
    if out_shards:
        assert tn == n // out_shards
        out_shape = jax.ShapeDtypeStruct((out_shards, ka, tn), F32)
        o_spec = pl.BlockSpec((None, tka, tn), lambda i, j, s: (j, i, 0))
    else:
        out_shape = jax.ShapeDtypeStruct((ka, n), F32)
        o_spec = pl.BlockSpec((tka, tn), lambda i, j, s: (i, j))

    def body(a_ref, b_ref, o_ref):
        s = pl.program_id(2)
        part = lax.dot_general(a_ref[...].astype(BF16), b_ref[...].astype(BF16), (((0,), (0,)), ((), ())),
                               preferred_element_type=F32)

        @pl.when(s == 0)
        def _():
            o_ref[...] = part

        @pl.when(s > 0)
        def _():
            o_ref[...] += part

    return pl.pallas_call(
        body, out_shape=out_shape, grid=(ka // tka, n // tn, t // tt),
        in_specs=[pl.BlockSpec((tt, tka), lambda i, j, s: (s, i)), pl.BlockSpec((tt, tn), lambda i, j, s: (s, j))],
        out_specs=o_spec, name=name, compiler_params=_params(("parallel", "parallel", "arbitrary")))(a, b)


def _d_x(du, w_in_p, dpre1, side=None):
    t = du.shape[0]
    tm, tc = 1024, 2432
    nc = UW // tc

    def body(a_ref, b_ref, add_ref, o_ref):
        c = pl.program_id(0) % nc
        part = _dot_nt(a_ref[...], b_ref[...])

        @pl.when(c == 0)
        def _():
            o_ref[...] = part + ALPHA * add_ref[...]

        @pl.when(c > 0)
        def _():
            o_ref[...] += part

    outs, side_outs = _host_call(
        body, side, (t // tm) * nc, out_shape=(jax.ShapeDtypeStruct((t, D), F32),),
        in_specs=[pl.BlockSpec((tm, tc), lambda s: (s // nc, s % nc)), pl.BlockSpec((D, tc), lambda s: (0, s % nc)),
                  pl.BlockSpec((tm, D), lambda s: (s // nc, 0))],
        out_specs=(pl.BlockSpec((tm, D), lambda s: (s // nc, 0)),),
        scratch_shapes=[], args=(du, w_in_p, dpre1), aliases={}, name="d_x", sem=("arbitrary",))
    return outs[0], side_outs


def _in_proj(xb, w_in_p, side=None):
    t, k = xb.shape
    tm, tn = 512, 2432
    nm, nn = t // tm, UW // tn

    def body(a_ref, b_ref, o_ref):
        o_ref[...] = jnp.dot(a_ref[...], b_ref[...], preferred_element_type=F32)

    outs, side_outs = _host_call(
        body, side, nm * nn, out_shape=(jax.ShapeDtypeStruct((t, UW), F32),),
        in_specs=[pl.BlockSpec((tm, k), lambda s: (s % nm, 0)), pl.BlockSpec((k, tn), lambda s: (0, s // nm))],
        out_specs=(pl.BlockSpec((tm, tn), lambda s: (s % nm, s // nm)),),
        scratch_shapes=[], args=(xb, w_in_p), aliases={}, name="in_proj", sem=("arbitrary",))
    return outs[0], side_outs


CONV_TM = 512
CONV_TC = 1024
CONV_RC = 64
CONV_CC = 256


def _halo_specs(t, tm, tc, col0):
    nb8 = t // 8
    r8 = tm // 8
    return [
        pl.BlockSpec((8, tc), lambda i, j: (jnp.maximum(i * r8 - 1, 0), col0 + j)),
        pl.BlockSpec((tm, tc), lambda i, j: (i, col0 + j)),
        pl.BlockSpec((8, tc), lambda i, j: (jnp.minimum((i + 1) * r8, nb8 - 1), col0 + j)),
    ]


def _fill_ext(ext, prev_ref, cur_ref, next_ref, tm, i, last):
    ext[0:8, :] = jnp.where(i > 0, prev_ref[...], 0.0)
    ext[8:8 + tm, :] = cur_ref[...]
    ext[8 + tm:16 + tm, :] = jnp.where(i < last, next_ref[...], 0.0)


def _conv_fwd(u, conv_w, conv_b):
    t = u.shape[0]
    tm, tc = CONV_TM, CONV_TC

    def body(prev_ref, cur_ref, next_ref, w_ref, b_ref, o_ref, ext):
        _fill_ext(ext, prev_ref, cur_ref, next_ref, tm, pl.program_id(0), t // tm - 1)
        for c0 in range(0, tc, CONV_CC):
            cs = slice(c0, c0 + CONV_CC)
            w = w_ref[:, cs]
            for r0 in range(0, tm, CONV_RC):
                acc = jnp.broadcast_to(b_ref[:, cs], (CONV_RC, CONV_CC))
                for k in range(KCONV):
                    acc = acc + w[k:k + 1, :] * ext[pl.ds(r0 + 6 + k, CONV_RC), cs]
                o_ref[r0:r0 + CONV_RC, cs] = acc * _sigmoid(acc)

    return pl.pallas_call(
        body, out_shape=jax.ShapeDtypeStruct((t, CONVD), F32), grid=(t // tm, CONVD // tc),
        in_specs=_halo_specs(t, tm, tc, OXBC // tc) + [
            pl.BlockSpec((KCONV, tc), lambda i, j: (0, j)), pl.BlockSpec((1, tc), lambda i, j: (0, j))],
        out_specs=pl.BlockSpec((tm, tc), lambda i, j: (i, j)),
        scratch_shapes=[pltpu.VMEM((tm + 16, tc), F32)],
        name="conv_fwd", compiler_params=_params(("parallel", "parallel")))(u, u, u, conv_w, conv_b)


def _conv_dpre(u, dxs, dy, dbc, dsk_row, conv_w, conv_b):
    t = u.shape[0]
    tm, tc = CONV_TM, CONV_TC
    r8 = tm // 8
    nb8 = t // 8
    c0 = OXBC // tc

    def body(uprev, ucur, unext, f_ref, y_ref, cf_ref, dsk_ref, w_ref, bias_ref, dpre_ref, dw_ref, db_ref, ext):
        j = pl.program_id(0)
        i = pl.program_id(1)
        _fill_ext(ext, uprev, ucur, unext, tm, i, t // tm - 1)
        is_xs = j < 2
        dw_cols, db_cols = [], []
        for c0 in range(0, tc, CONV_CC):
            cs = slice(c0, c0 + CONV_CC)
            w = w_ref[:, cs]
            dsk = dsk_ref[:, cs]
            dw_acc = [jnp.zeros((1, CONV_CC), F32) for _ in range(KCONV)]
            db_acc = jnp.zeros((1, CONV_CC), F32)
            for r0 in range(0, tm, CONV_RC):
                rs = slice(r0, r0 + CONV_RC)
                taps = [ext[pl.ds(r0 + 6 + k, CONV_RC), cs] for k in range(KCONV)]
                pre = jnp.broadcast_to(bias_ref[:, cs], (CONV_RC, CONV_CC))
                for k in range(KCONV):
                    pre = pre + w[k:k + 1, :] * taps[k]
                s = _sigmoid(pre)
                up = jnp.where(is_xs, f_ref[rs, cs] + dsk * y_ref[rs, cs], cf_ref[rs, cs])
                dpre = up * (s * (1.0 + pre * (1.0 - s)))
                dpre_ref[rs, cs] = dpre
                for k in range(KCONV):
                    dw_acc[k] = dw_acc[k] + jnp.sum(dpre * taps[k], axis=0, keepdims=True)
                db_acc = db_acc + jnp.sum(dpre, axis=0, keepdims=True)
            dw_cols.append(jnp.concatenate(dw_acc + [jnp.zeros((8 - KCONV, CONV_CC), F32)], axis=0))
            db_cols.append(jnp.broadcast_to(db_acc, (8, CONV_CC)))
        dw_part = jnp.concatenate(dw_cols, axis=1)
        db_part = jnp.concatenate(db_cols, axis=1)

        @pl.when(i == 0)
        def _():
            dw_ref[...] = dw_part
            db_ref[...] = db_part

        @pl.when(i > 0)
        def _():
            dw_ref[...] += dw_part
            db_ref[...] += db_part

    xs_spec = pl.BlockSpec((tm, tc), lambda j, i: (jnp.where(j < 2, i, 0), jnp.minimum(j, 1)))
    bc_spec = pl.BlockSpec((tm, tc), lambda j, i: (jnp.where(j == 2, i, 0), 0))
    in_specs = [
        pl.BlockSpec((8, tc), lambda j, i: (jnp.maximum(i * r8 - 1, 0), c0 + j)),
        pl.BlockSpec((tm, tc), lambda j, i: (i, c0 + j)),
        pl.BlockSpec((8, tc), lambda j, i: (jnp.minimum((i + 1) * r8, nb8 - 1), c0 + j)),
        xs_spec, xs_spec, bc_spec,
        pl.BlockSpec((1, tc), lambda j, i: (0, jnp.minimum(j, 1))),
        pl.BlockSpec((KCONV, tc), lambda j, i: (0, j)), pl.BlockSpec((1, tc), lambda j, i: (0, j)),
    ]
    return pl.pallas_call(
        body,
        out_shape=(jax.ShapeDtypeStruct((t, CONVD), F32), jax.ShapeDtypeStruct((8, CONVD), F32),
                   jax.ShapeDtypeStruct((8, CONVD), F32)),
        grid=(CONVD // tc, t // tm), in_specs=in_specs,
        out_specs=(pl.BlockSpec((tm, tc), lambda j, i: (i, j)),
                   pl.BlockSpec((8, tc), lambda j, i: (0, j)), pl.BlockSpec((8, tc), lambda j, i: (0, j))),
        scratch_shapes=[pltpu.VMEM((tm + 16, tc), F32)],
        name="conv_dpre", compiler_params=_params(("parallel", "arbitrary")))(
            u, u, u, dxs, dy, dbc, dsk_row, conv_w, conv_b)


def _conv_dx(du, dpre, conv_w):
    t = dpre.shape[0]
    tm, tc = CONV_TM, CONV_TC
    r8 = tm // 8
    nb8 = t // 8

    def body(prev_ref, cur_ref, next_ref, w_ref, du_in, du_out, ext):
        del du_in
        _fill_ext(ext, prev_ref, cur_ref, next_ref, tm, pl.program_id(1), t // tm - 1)
        for c0 in range(0, tc, CONV_CC):
            cs = slice(c0, c0 + CONV_CC)
            w = w_ref[:, cs]
            for r0 in range(0, tm, CONV_RC):
                acc = jnp.zeros((CONV_RC, CONV_CC), F32)
                for k in range(KCONV):
                    acc = acc + w[k:k + 1, :] * ext[pl.ds(r0 + 10 - k, CONV_RC), cs]
                du_out[r0:r0 + CONV_RC, cs] = acc.astype(du_out.dtype)

    in_specs = [
        pl.BlockSpec((8, tc), lambda j, i: (jnp.maximum(i * r8 - 1, 0), j)),
        pl.BlockSpec((tm, tc), lambda j, i: (i, j)),
        pl.BlockSpec((8, tc), lambda j, i: (jnp.minimum((i + 1) * r8, nb8 - 1), j)),
        pl.BlockSpec((KCONV, tc), lambda j, i: (0, j)),
        pl.BlockSpec(memory_space=pl.ANY),
    ]
    return pl.pallas_call(
        body, out_shape=jax.ShapeDtypeStruct(du.shape, du.dtype), grid=(CONVD // tc, t // tm), in_specs=in_specs,
        out_specs=pl.BlockSpec((tm, tc), lambda j, i: (i, OXBC // tc + j)),
        scratch_shapes=[pltpu.VMEM((tm + 16, tc), F32)], input_output_aliases={4: 0},
        name="conv_dx", compiler_params=_params(("parallel", "parallel")))(dpre, dpre, dpre, conv_w, du)


def _ssd_common(dtr_ref, par_ref, rev):
    raw = dtr_ref[...]
    lane = _iota((1, 128), 1)
    mine = (lane >= 32 * rev) & (lane < 32 * rev + 32)
    bias = par_ref[0:1, :]
    arow = jnp.where(mine, -jnp.exp(par_ref[1:2, :]), 0.0)
    dt = _softplus(raw + bias)
    a = dt * arow
    ri = _iota((Q, Q), 0)
    ci = _iota((Q, Q), 1)
    tri = (ci >= ri) if rev else (ci <= ri)
    trit = (ci <= ri) if rev else (ci >= ri)
    cs = _dot01_l(tri.astype(BF16), a)
    return raw, bias, arow, mine, dt, cs, tri, trit


def _expand_mat(rev):
    r = np.arange(128)[:, None]
    c = np.arange(DI)[None, :]
    return jnp.asarray(r == (c // HP) + 32 * rev, BF16)


def _sum_mat(rev):
    r = np.arange(DI)[:, None]
    c = np.arange(128)[None, :]
    return jnp.asarray(c == (r // HP) + 32 * rev, BF16)


def _ssd_fwd(xbc, u, par, y_add=None, *, rev):
    t = xbc.shape[0]
    nc = t // Q
    end = 0 if rev else Q - 1
    cmap = (lambda c: nc - 1 - c) if rev else (lambda c: c)

    def body(xbc_ref, dtr_ref, par_ref, ex_ref, *rest):
        yadd_ref = rest[0] if y_add is not None else None
        y_ref, st_ref, h_scr = rest[-3:]
        step = pl.program_id(0)

        @pl.when(step == 0)
        def _():
            h_scr[...] = jnp.zeros((NS, DI), F32)

        raw, bias, arow, mine, dt, cs, tri, trit = _ssd_common(dtr_ref, par_ref, rev)
        cst = cs.T
        dtt = dt.T
        tot_col = cst[:, end:end + 1]
        wt = dtt * jnp.exp(tot_col - cst)
        ecs_all = jnp.exp(cs)
        gam = jnp.exp(cs[end:end + 1, :])
        gam_x = _dot01(jnp.broadcast_to(gam, (8, 128)), ex_ref[...])[0:1, :]
        lane = _iota((Q, 128), 1)
        sel = lane < HP
        st_ref[...] = h_scr[...]
        for g in range(NG):
            bg = xbc_ref[:, DI + NS * g:DI + NS * (g + 1)]
            cg = xbc_ref[:, DI + NG * NS + NS * g:DI + NG * NS + NS * (g + 1)]
            cb = _dot_nt(cg.astype(BF16), bg.astype(BF16))
            bt = bg.T
            for k in range(4):
                lo = 512 * g + 128 * k
                xp = xbc_ref[:, lo:lo + 128].astype(BF16)
                hp = h_scr[:, lo:lo + 128]
                rhs = jnp.concatenate([xp, hp.astype(BF16)], axis=0)
                lhs, bts = [], []
                for j in range(2):
                    hc = 8 * g + 2 * k + j + 32 * rev
                    csc = jnp.broadcast_to(cs[:, hc:hc + 1], (Q, Q))
                    lm = jnp.exp(jnp.where(tri, csc - cst[hc:hc + 1, :], NEG)) * dtt[hc:hc + 1, :]
                    mh = (cb * lm).astype(BF16)
                    ec = (jnp.broadcast_to(ecs_all[:, hc:hc + 1], (Q, NS)) * cg).astype(BF16)
                    lhs.append(jnp.concatenate([mh, ec], axis=1))
                    bts.append((bt * wt[hc:hc + 1, :]).astype(BF16))
                ys = jnp.dot(jnp.concatenate(lhs, axis=0), rhs, preferred_element_type=F32)
                ss = jnp.dot(jnp.concatenate(bts, axis=0), xp, preferred_element_type=F32)
                yp = jnp.where(sel, ys[0:Q], ys[Q:2 * Q])
                y_ref[:, lo:lo + 128] = yp if yadd_ref is None else yp + yadd_ref[:, lo:lo + 128]
                h_scr[:, lo:lo + 128] = gam_x[:, lo:lo + 128] * hp + jnp.where(sel, ss[0:NS], ss[NS:2 * NS])

    return pl.pallas_call(
        body,
        out_shape=(jax.ShapeDtypeStruct((t, DI), F32), jax.ShapeDtypeStruct((nc, NS, DI), F32)),
        grid=(nc,),
        in_specs=[pl.BlockSpec((Q, CONVD), lambda c: (cmap(c), 0)),
                  pl.BlockSpec((Q, 128), lambda c: (cmap(c), ODT // 128)),
                  pl.BlockSpec((8, 128), lambda c: (0, 0)),
                  pl.BlockSpec((128, DI), lambda c: (0, 0))]
        + ([pl.BlockSpec((Q, DI), lambda c: (cmap(c), 0))] if y_add is not None else []),
        out_specs=(pl.BlockSpec((Q, DI), lambda c: (cmap(c), 0)),
                   pl.BlockSpec((None, NS, DI), lambda c: (cmap(c), 0, 0))),
        scratch_shapes=[pltpu.VMEM((NS, DI), F32)],
        name="ssd_fwd_rev" if rev else "ssd_fwd", compiler_params=_params(("arbitrary",)))(
            xbc, u, par, _expand_mat(rev), *([y_add] if y_add is not None else []))


def _ssd_bwd(xbc, u, par, dy, st, *, rev, add=None, side=None):
    t = xbc.shape[0]
    nc = t // Q
    end = 0 if rev else Q - 1
    cmap = (lambda c: c) if rev else (lambda c: nc - 1 - c)

    def body(xbc_ref, dtr_ref, par_ref, dy_ref, hin_ref, ex_ref, sm_ref, *rest):
        addx_ref, addbc_ref, addt_ref = rest[:3] if add is not None else (None, None, None)
        dxs_ref, dbc_ref, ddt_ref, acc_ref, dh_scr = rest[-5:]
        step = pl.program_id(0)

        @pl.when(step == 0)
        def _():
            dh_scr[...] = jnp.zeros((NS, DI), F32)

        raw, bias, arow, mine, dt, cs, tri, trit = _ssd_common(dtr_ref, par_ref, rev)
        ri = _iota((Q, Q), 0)
        ci = _iota((Q, Q), 1)
        stri = ((ri > ci) if rev else (ri < ci)).astype(BF16)
        strit = ((ci > ri) if rev else (ci < ri)).astype(BF16)
        cst = cs.T
        dtt = dt.T
        et = jnp.exp(cst)
        ecs_all = jnp.exp(cs)
        ws_all = jnp.exp(cs[end:end + 1, :] - cs)
        expand = ex_ref[...]
        summat = sm_ref[...]
        gam = jnp.exp(cs[end:end + 1, :])
        gam_x = _dot01(jnp.broadcast_to(gam, (8, 128)), expand)[0:1, :]
        dt_hi, dt_mid, _ = _split3(dt)
        dtx = (jnp.dot(dt_hi, expand, preferred_element_type=F32)
               + jnp.dot(dt_mid, expand, preferred_element_type=F32))
        lane = _iota((Q, 128), 1)
        sel = lane < HP
        dho = dh_scr[...]
        t3 = jnp.sum(dho * hin_ref[...], axis=0, keepdims=True) * gam_x
        dxs_cols, dxs2_cols, yoff_cols, a1_rows = [], [], [], []
        for g in range(NG):
            bg = xbc_ref[:, DI + NS * g:DI + NS * (g + 1)]
            cg = xbc_ref[:, DI + NG * NS + NS * g:DI + NG * NS + NS * (g + 1)]
            bb = bg.astype(BF16)
            cbf = cg.astype(BF16)
            cb = _dot_nt(cbf, bb)
            cbt = _dot_nt(bb, cbf)
            ct = cg.T
            bdh = jnp.dot(bb, dho[:, 512 * g:512 * (g + 1)].astype(BF16), preferred_element_type=F32)
            dcb = jnp.zeros((Q, Q), F32)
            dcg = jnp.zeros((Q, NS), F32)
            dbg = jnp.zeros((Q, NS), F32)
            for k in range(4):
                lo = 512 * g + 128 * k
                xpf = xbc_ref[:, lo:lo + 128]
                xp = xpf.astype(BF16)
                dyp = dy_ref[:, lo:lo + 128]
                dypb = dyp.astype(BF16)
                hinp = hin_ref[:, lo:lo + 128].astype(BF16)
                dhp = dho[:, lo:lo + 128]
                es, ws, lmds, mts, ctes, dyms, ecbs = [], [], [], [], [], [], []
                for j in range(2):
                    hc = 8 * g + 2 * k + j + 32 * rev
                    csc = jnp.broadcast_to(cs[:, hc:hc + 1], (Q, Q))
                    csr = cst[hc:hc + 1, :]
                    lmds.append(jnp.exp(jnp.where(tri, csc - csr, NEG)) * dtt[hc:hc + 1, :])
                    lmb = jnp.exp(jnp.where(trit, csr - csc, NEG))
                    mts.append((cbt * lmb).astype(BF16))
                    dyms.append(jnp.where(sel if j == 0 else ~sel, dyp, 0.0).astype(BF16))
                    ecs = jnp.broadcast_to(ecs_all[:, hc:hc + 1], (Q, NS))
                    es.append(ecs)
                    ws.append(jnp.broadcast_to(ws_all[:, hc:hc + 1], (Q, NS)))
                    ecbs.append((ecs * cg).astype(BF16))
                    ctes.append((ct * et[hc:hc + 1, :]).astype(BF16))
                by_dy = jnp.dot(jnp.concatenate(mts + ctes, axis=0), dypb, preferred_element_type=F32)
                dmm = _dot_nt(jnp.concatenate(dyms, axis=0), xp)
                dm0, dm1 = dmm[0:Q] * lmds[0], dmm[Q:2 * Q] * lmds[1]
                dcb = dcb + dm0 + dm1
                rr = jnp.dot(jnp.concatenate([dm0 * cb, dm1 * cb], axis=0).astype(BF16), stri, preferred_element_type=F32)
                a1_rows.append(jnp.sum(jnp.where(tri, rr[0:Q], 0.0), axis=0, keepdims=True))
                a1_rows.append(jnp.sum(jnp.where(tri, rr[Q:2 * Q], 0.0), axis=0, keepdims=True))
                yo = jnp.dot(jnp.concatenate(ecbs, axis=0), hinp, preferred_element_type=F32)
                e_p = jnp.where(sel, es[0], es[1])
                w_p = jnp.where(sel, ws[0], ws[1])
                d2 = w_p * bdh[:, 128 * k:128 * (k + 1)]
                dxs2_cols.append(d2)
                dxs_cols.append(jnp.where(sel, by_dy[0:Q], by_dy[Q:2 * Q]) + d2)
                yoff_cols.append(jnp.where(sel, yo[0:Q], yo[Q:2 * Q]))
                dcg = dcg + _dot_nt((e_p * dyp).astype(BF16), hinp)
                dbg = dbg + _dot_nt((w_p * dtx[:, lo:lo + 128] * xpf).astype(BF16), dhp.astype(BF16))
                dh_scr[:, lo:lo + 128] = (gam_x[:, lo:lo + 128] * dhp
                                          + jnp.where(sel, by_dy[2 * Q:3 * Q], by_dy[3 * Q:4 * Q]))
            dcg = dcg + jnp.dot(dcb.astype(BF16), bb, preferred_element_type=F32)
            dbg = dbg + jnp.dot(dcb.T.astype(BF16), cbf, preferred_element_type=F32)
            lo_b, lo_c = NS * g, NG * NS + NS * g
            if addbc_ref is not None:
                dbg = dbg + addbc_ref[:, lo_b:lo_b + NS]
                dcg = dcg + addbc_ref[:, lo_c:lo_c + NS]
            dbc_ref[:, lo_b:lo_b + NS] = dbg
            dbc_ref[:, lo_c:lo_c + NS] = dcg
        dxs = jnp.concatenate(dxs_cols, axis=1)
        dxs_ref[...] = dxs * dtx if addx_ref is None else dxs * dtx + addx_ref[...]
        xs = xbc_ref[:, 0:DI]
        stacked = jnp.concatenate([xs * dxs, xs * jnp.concatenate(dxs2_cols, axis=1),
                                   dy_ref[...] * jnp.concatenate(yoff_cols, axis=1),
                                   jnp.broadcast_to(t3, (8, DI))], axis=0).astype(BF16)
        sums = jnp.dot(stacked, summat, preferred_element_type=F32)
        rx, rx2, ryo, c0 = sums[0:Q], sums[Q:2 * Q], sums[2 * Q:3 * Q], sums[3 * Q:3 * Q + 1]
        zero32 = jnp.zeros((32, Q), F32)
        a1t = jnp.concatenate(([zero32] if rev else []) + a1_rows + [zero32] * (2 if rev else 3), axis=0)
        da = (a1t.T + jnp.dot(trit.astype(BF16), ryo.astype(BF16), preferred_element_type=F32)
              + jnp.dot(strit, (dt * rx2).astype(BF16), preferred_element_type=F32) + jnp.where(mine, c0, 0.0))
        ddt = rx + da * arow
        ddtr = ddt * _sigmoid(raw + bias)
        ddt_ref[...] = ddtr if addt_ref is None else ddtr + addt_ref[...]
        part = jnp.concatenate([jnp.sum(ddtr, axis=0, keepdims=True),
                                jnp.sum(da * dt, axis=0, keepdims=True) * arow,
                                jnp.zeros((6, 128), F32)], axis=0)

        @pl.when(step == 0)
        def _():
            acc_ref[...] = part

        @pl.when(step > 0)
        def _():
            acc_ref[...] += part

    outs, side_outs = _host_call(
        body, side, nc,
        out_shape=(jax.ShapeDtypeStruct((t, DI), F32), jax.ShapeDtypeStruct((t, 2 * NG * NS), F32),
                   jax.ShapeDtypeStruct((t, 128), F32), jax.ShapeDtypeStruct((8, 128), F32)),
        in_specs=[pl.BlockSpec((Q, CONVD), lambda c: (cmap(c), 0)),
                  pl.BlockSpec((Q, 128), lambda c: (cmap(c), ODT // 128)),
                  pl.BlockSpec((8, 128), lambda c: (0, 0)),
                  pl.BlockSpec((Q, DI), lambda c: (cmap(c), 0)),
                  pl.BlockSpec((None, NS, DI), lambda c: (cmap(c), 0, 0)),
                  pl.BlockSpec((128, DI), lambda c: (0, 0)), pl.BlockSpec((DI, 128), lambda c: (0, 0))]
        + ([pl.BlockSpec((Q, DI), lambda c: (cmap(c), 0)), pl.BlockSpec((Q, 2 * NG * NS), lambda c: (cmap(c), 0)),
            pl.BlockSpec((Q, 128), lambda c: (cmap(c), 0))] if add is not None else []),
        out_specs=(pl.BlockSpec((Q, DI), lambda c: (cmap(c), 0)),
                   pl.BlockSpec((Q, 2 * NG * NS), lambda c: (cmap(c), 0)),
                   pl.BlockSpec((Q, 128), lambda c: (cmap(c), 0)),
                   pl.BlockSpec((8, 128), lambda c: (0, 0))),
        scratch_shapes=[pltpu.VMEM((NS, DI), F32)],
        args=(xbc, u, par, dy, st, _expand_mat(rev), _sum_mat(rev)) + (tuple(add) if add is not None else ()), aliases={},
        name="ssd_bwd_rev" if rev else "ssd_bwd", sem=("arbitrary",))
    return (*outs, side_outs)


GN_TM = 256
GN_GROUP = DI // NG


def _gn_forward_vals(y0, xs, z, dsk):
    y = y0 + dsk * xs
    sz = _sigmoid(z)
    gate = z * sz
    y2 = y * gate
    parts, rs = [], []
    for g in range(NG):
        seg = y2[:, GN_GROUP * g:GN_GROUP * (g + 1)]
        r = lax.rsqrt(jnp.mean(seg * seg, axis=1, keepdims=True) + NORM_EPS)
        rs.append(r)
        parts.append(seg * r)
    yn = jnp.concatenate(parts, axis=1)
    return y, sz, gate, yn, rs


def _gatenorm_fwd(y_fb, xbc, u, dsk_row, nw_row):
    t = y_fb.shape[0]
    tm = GN_TM

    def body(y_ref, xs_ref, z_ref, dsk_ref, nw_ref, o_ref):
        _, _, _, yn, _ = _gn_forward_vals(y_ref[...], xs_ref[...], z_ref[...], dsk_ref[...])
        o_ref[...] = (yn * nw_ref[...]).astype(BF16)

    blk = pl.BlockSpec((tm, DI), lambda i: (i, 0))
    row = pl.BlockSpec((1, DI), lambda i: (0, 0))
    return pl.pallas_call(
        body, out_shape=jax.ShapeDtypeStruct((t, DI), BF16), grid=(t // tm,),
        in_specs=[blk, blk, pl.BlockSpec((tm, DI), lambda i: (i, OZ // DI)), row, row],
        out_specs=blk, name="gatenorm_fwd", compiler_params=_params(("parallel",)))(y_fb, xbc, u, dsk_row, nw_row)


def _gatenorm_bwd(ds_out, y_fb, xbc, u, du, dsk_row, nw_row, side=None):
    t = y_fb.shape[0]
    tm = GN_TM

    def body(ds_ref, y_ref, xs_ref, z_ref, dsk_ref, nw_ref, sm_ref, du_in, dy_ref, du_out, dnw_ref, dds_ref):
        del du_in
        i = pl.program_id(0)
        xs = xs_ref[...]
        z = z_ref[...]
        y, sz, gate, yn, rs = _gn_forward_vals(y_ref[...], xs, z, dsk_ref[...])
        ds = ds_ref[...]
        gsc = ds * nw_ref[...]
        parts = []
        for g in range(NG):
            sl = slice(GN_GROUP * g, GN_GROUP * (g + 1))
            m = jnp.mean(gsc[:, sl] * yn[:, sl], axis=1, keepdims=True)
            parts.append(rs[g] * (gsc[:, sl] - yn[:, sl] * m))
        dy2 = jnp.concatenate(parts, axis=1)
        dy = dy2 * gate
        dy_ref[...] = dy
        du_out[...] = (dy2 * y * (sz * (1.0 + z * (1.0 - sz)))).astype(du_out.dtype)
        dnw = jnp.broadcast_to(jnp.sum(ds * yn, axis=0, keepdims=True), (8, DI))
        drow = jnp.broadcast_to(jnp.sum(dy * xs, axis=0, keepdims=True), (8, DI))
        dds = _dot01(drow, sm_ref[...])

        @pl.when(i == 0)
        def _():
            dnw_ref[...] = dnw
            dds_ref[...] = dds

        @pl.when(i > 0)
        def _():
            dnw_ref[...] += dnw
            dds_ref[...] += dds

    blk = pl.BlockSpec((tm, DI), lambda i: (i, 0))
    row = pl.BlockSpec((1, DI), lambda i: (0, 0))
    outs, side_outs = _host_call(
        body, side, t // tm,
        out_shape=(jax.ShapeDtypeStruct((t, DI), F32), jax.ShapeDtypeStruct(du.shape, du.dtype),
                   jax.ShapeDtypeStruct((8, DI), F32), jax.ShapeDtypeStruct((8, 128), F32)),
        in_specs=[blk, blk, blk, pl.BlockSpec((tm, DI), lambda i: (i, OZ // DI)), row, row,
                  pl.BlockSpec((DI, 128), lambda i: (0, 0)), pl.BlockSpec(memory_space=pl.ANY)],
        out_specs=(blk, pl.BlockSpec((tm, DI), lambda i: (i, OZ // DI)),
                   pl.BlockSpec((8, DI), lambda i: (0, 0)), pl.BlockSpec((8, 128), lambda i: (0, 0))),
        scratch_shapes=[], args=(ds_out, y_fb, xbc, u, dsk_row, nw_row, _sum_mat(0), du), aliases={7: 1},
        name="gatenorm_bwd", sem=("arbitrary",))
    return (*outs, side_outs)


AT_B = 128
AT_W = AT_B + 2 * ATT_HALF
AT_L = 2 * AH
SCALE = 1.0 / math.sqrt(AH)


def _slope(g, hh):
    return 2.0 ** (-8.0 * (4 * g + hh + 1) / 12.0)


def _qcol(g):
    return lambda p: OQ // AT_L + 2 * g + p


def _kcol(g):
    return lambda p: OKV // AT_L + 4 * g + 2 * p


def _vcol(g):
    return lambda p: OKV // AT_L + 4 * g + 2 * p + 1


def _pcol(p):
    return p


def _sub(d):
    return 4 if d == 1 else 1


def _win_specs(col, t, d):
    tb, hb = AT_B * d * _sub(d), ATT_HALF * d
    per = tb // hb
    nh = t // hb
    return [
        pl.BlockSpec((hb, AT_L), lambda p, i: (jnp.maximum(per * i - 1, 0), col(p))),
        pl.BlockSpec((tb, AT_L), lambda p, i: (i, col(p))),
        pl.BlockSpec((hb, AT_L), lambda p, i: (jnp.minimum(per * (i + 1), nh - 1), col(p))),
    ]


def _blk_spec(col, d):
    return pl.BlockSpec((AT_B * d * _sub(d), AT_L), lambda p, i: (i, col(p)))


def _rows(ref, r, s, d):
    return ref[pl.ds(r, AT_B, stride=d), :] if d > 1 else ref[AT_B * s:AT_B * (s + 1), :]


def _win(p_ref, c_ref, n_ref, r, s, d):
    if d > 1:
        return jnp.concatenate([p_ref[pl.ds(r, ATT_HALF, stride=d), :], c_ref[pl.ds(r, AT_B, stride=d), :],
                                n_ref[pl.ds(r, ATT_HALF, stride=d), :]], axis=0)
    if s == 0:
        return jnp.concatenate([p_ref[...], c_ref[0:AT_B + ATT_HALF, :]], axis=0)
    if s == _sub(d) - 1:
        return jnp.concatenate([c_ref[AT_B * s - ATT_HALF:AT_B * (s + 1), :], n_ref[...]], axis=0)
    return c_ref[AT_B * s - ATT_HALF:AT_B * (s + 1) + ATT_HALF, :]


def _put_rows(ref, r, s, d, val):
    if d > 1:
        ref[pl.ds(r, AT_B, stride=d), :] = val
    else:
        ref[AT_B * s:AT_B * (s + 1), :] = val


def _for_blocks(d, fn):
    if d == 1:
        for s in range(_sub(d)):
            fn(0, s)
    else:
        def step(r, c):
            fn(r, 0)
            return c
        lax.fori_loop(0, d, step, 0, unroll=4)


def _attn_bias(blk, ln, d, g, p_id):
    a = blk * AT_B + _iota((AT_B, AT_W), 0)
    b = blk * AT_B - ATT_HALF + _iota((AT_B, AT_W), 1)
    rel = jnp.abs(a - b)
    valid = (rel <= ATT_HALF) & (b >= 0) & (b < ln)
    dist = (rel * d).astype(F32)
    out = []
    for hh in range(2):
        slope = jnp.where(p_id == 0, _slope(g, hh), _slope(g, 2 + hh))
        out.append(jnp.where(valid, -slope * dist, NEG))
    return out


def _attn_fwd(u, g):
    t = u.shape[0]
    d = DILATIONS[g]
    ln = t // d

    def body(q_ref, kp, kc, kn, vp, vc, vn, o_ref, l_ref):
        p_id = pl.program_id(0)
        i = pl.program_id(1)
        lane = _iota((AT_B, AT_L), 1)
        biases = [_attn_bias(i * _sub(d) + s, ln, d, g, p_id) for s in range(_sub(d))]

        def one(r, s):
            q = _rows(q_ref, r, s, d)
            kw = _win(kp, kc, kn, r, s, d).astype(BF16)
            vw = _win(vp, vc, vn, r, s, d).astype(BF16)
            o = jnp.zeros((AT_B, AT_L), F32)
            lse = jnp.zeros((AT_B, AT_L), F32)
            for hh in range(2):
                hm = (lane // AH) == hh
                qm = jnp.where(hm, q, 0.0).astype(BF16)
                sc = _dot_nt(qm, kw) * SCALE + biases[s][hh]
                m = jnp.max(sc, axis=1, keepdims=True)
                pr = jnp.exp(sc - m)
                den = jnp.sum(pr, axis=1, keepdims=True)
                oh = jnp.dot(pr.astype(BF16), vw, preferred_element_type=F32)
                o = jnp.where(hm, oh / den, o)
                lse = jnp.where(hm, m + jnp.log(den), lse)
            _put_rows(o_ref, r, s, d, o)
            _put_rows(l_ref, r, s, d, lse)

        _for_blocks(d, one)

    oshape = jax.ShapeDtypeStruct((t, 2 * AT_L), F32)
    ospec = _blk_spec(_pcol, d)
    return pl.pallas_call(
        body, out_shape=(oshape, oshape), grid=(2, t // (AT_B * d * _sub(d))),
        in_specs=[_blk_spec(_qcol(g), d)] + _win_specs(_kcol(g), t, d) + _win_specs(_vcol(g), t, d),
        out_specs=(ospec, ospec), name=f"attn_fwd_{g}", compiler_params=_params(("parallel", "parallel")))(
            u, u, u, u, u, u, u)


def _attn_dq(u, du, do, lse, e, g):
    t = u.shape[0]
    d = DILATIONS[g]
    ln = t // d

    def body(q_ref, kp, kc, kn, vp, vc, vn, do_ref, l_ref, e_ref, du_in, dq_ref, dq_scr):
        del du_in
        p_id = pl.program_id(0)
        i = pl.program_id(1)
        lane = _iota((AT_B, AT_L), 1)
        biases = [_attn_bias(i * _sub(d) + s, ln, d, g, p_id) for s in range(_sub(d))]

        def one(r, s):
            q = _rows(q_ref, r, s, d)
            kw = _win(kp, kc, kn, r, s, d).astype(BF16)
            vw = _win(vp, vc, vn, r, s, d).astype(BF16)
            do_ = _rows(do_ref, r, s, d)
            lv = _rows(l_ref, r, s, d)
            ev = _rows(e_ref, r, s, d)
            dq = jnp.zeros((AT_B, AT_L), F32)
            for hh in range(2):
                hm = (lane // AH) == hh
                qm = jnp.where(hm, q, 0.0).astype(BF16)
                sc = _dot_nt(qm, kw) * SCALE + biases[s][hh]
                lcol = jnp.broadcast_to(lv[:, AH * hh:AH * hh + 1], (AT_B, AT_W))
                ecol = jnp.broadcast_to(ev[:, AH * hh:AH * hh + 1], (AT_B, AT_W))
                pr = jnp.exp(sc - lcol)
                dom = jnp.where(hm, do_, 0.0).astype(BF16)
                ds = pr * (_dot_nt(dom, vw) + ecol)
                dqh = jnp.dot(ds.astype(BF16), kw, preferred_element_type=F32) * SCALE
                dq = jnp.where(hm, dqh, dq)
            _put_rows(dq_scr, r, s, d, dq)

        _for_blocks(d, one)
        dq_ref[...] = dq_scr[...].astype(dq_ref.dtype)

    rspec = _blk_spec(_pcol, d)
    return pl.pallas_call(
        body, out_shape=jax.ShapeDtypeStruct(du.shape, du.dtype), grid=(2, t // (AT_B * d * _sub(d))),
        in_specs=[_blk_spec(_qcol(g), d)] + _win_specs(_kcol(g), t, d) + _win_specs(_vcol(g), t, d)
        + [rspec, rspec, rspec, pl.BlockSpec(memory_space=pl.ANY)],
        out_specs=_blk_spec(_qcol(g), d), input_output_aliases={10: 0},
        scratch_shapes=[pltpu.VMEM((AT_B * d * _sub(d), AT_L), F32)],
        name=f"attn_dq_{g}", compiler_params=_params(("parallel", "parallel")))(
            u, u, u, u, u, u, u, do, lse, e, du)


def _attn_dkv(u, du, do, lse, e, g):
    t = u.shape[0]
    d = DILATIONS[g]
    ln = t // d

    def body(k_ref, v_ref, qp, qc, qn, dp_, dc_, dn_, lp, lc, ln_, ep, ec, en, du_in, dkv_ref, dk_scr, dv_scr):
        del du_in
        p_id = pl.program_id(0)
        jb = pl.program_id(1)
        lane = _iota((AT_B, AT_L), 1)
        biases = [_attn_bias(jb * _sub(d) + s, ln, d, g, p_id) for s in range(_sub(d))]

        def one(r, s):
            k = _rows(k_ref, r, s, d)
            v = _rows(v_ref, r, s, d)
            qw = _win(qp, qc, qn, r, s, d).astype(BF16)
            dow = _win(dp_, dc_, dn_, r, s, d).astype(BF16)
            lt = _win(lp, lc, ln_, r, s, d).T
            et = _win(ep, ec, en, r, s, d).T
            dk = jnp.zeros((AT_B, AT_L), F32)
            dv = jnp.zeros((AT_B, AT_L), F32)
            for hh in range(2):
                hm = (lane // AH) == hh
                km = jnp.where(hm, k, 0.0).astype(BF16)
                st = _dot_nt(km, qw) * SCALE + biases[s][hh]
                pt = jnp.exp(st - lt[AH * hh:AH * hh + 1, :])
                dvh = jnp.dot(pt.astype(BF16), dow, preferred_element_type=F32)
                vm = jnp.where(hm, v, 0.0).astype(BF16)
                dst = pt * (_dot_nt(vm, dow) + et[AH * hh:AH * hh + 1, :])
                dkh = jnp.dot(dst.astype(BF16), qw, preferred_element_type=F32) * SCALE
                dk = jnp.where(hm, dkh, dk)
                dv = jnp.where(hm, dvh, dv)
            _put_rows(dk_scr, r, s, d, dk)
            _put_rows(dv_scr, r, s, d, dv)

        _for_blocks(d, one)
        dkv_ref[:, 0:AT_L] = dk_scr[...].astype(dkv_ref.dtype)
        dkv_ref[:, AT_L:2 * AT_L] = dv_scr[...].astype(dkv_ref.dtype)

    return pl.pallas_call(
        body, out_shape=jax.ShapeDtypeStruct(du.shape, du.dtype), grid=(2, t // (AT_B * d * _sub(d))),
        in_specs=[_blk_spec(_kcol(g), d), _blk_spec(_vcol(g), d)]
        + _win_specs(_qcol(g), t, d) + _win_specs(_pcol, t, d) + _win_specs(_pcol, t, d) + _win_specs(_pcol, t, d)
        + [pl.BlockSpec(memory_space=pl.ANY)],
        out_specs=pl.BlockSpec((AT_B * d * _sub(d), 2 * AT_L), lambda p, i: (i, OKV // (2 * AT_L) + 2 * g + p)),
        input_output_aliases={14: 0},
        scratch_shapes=[pltpu.VMEM((AT_B * d * _sub(d), AT_L), F32), pltpu.VMEM((AT_B * d * _sub(d), AT_L), F32)],
        name=f"attn_dkv_{g}", compiler_params=_params(("parallel", "parallel")))(
            u, u, u, u, u, do, do, do, lse, lse, lse, e, e, e, du)


CMB_TM = 1024


def _combine_weights(l0, l1, l2):
    m = jnp.maximum(jnp.maximum(l0, l1), l2)
    e0, e1, e2 = jnp.exp(l0 - m), jnp.exp(l1 - m), jnp.exp(l2 - m)
    inv = 1.0 / (e0 + e1 + e2)
    return e0 * inv, e1 * inv, e2 * inv


def _combine_fwd(os_, ls_):
    t = os_[0].shape[0]
    tm = CMB_TM

    def body(o0, o1, o2, l0, l1, l2, a_ref):
        w0, w1, w2 = _combine_weights(l0[...], l1[...], l2[...])
        a_ref[...] = w0 * o0[...] + w1 * o1[...] + w2 * o2[...]

    blk = pl.BlockSpec((tm, 2 * AT_L), lambda i: (i, 0))
    return pl.pallas_call(
        body, out_shape=jax.ShapeDtypeStruct((t, 2 * AT_L), F32), grid=(t // tm,), in_specs=[blk] * 6, out_specs=blk,
        name="combine_fwd", compiler_params=_params(("parallel",)))(*os_, *ls_)


def _combine_bwd(datt, os_, ls_):
    t = datt.shape[0]
    tm = CMB_TM

    def body(da_ref, o0, o1, o2, l0, l1, l2, d0, d1, d2, e0, e1, e2):
        w = _combine_weights(l0[...], l1[...], l2[...])
        da = da_ref[...]
        att = w[0] * o0[...] + w[1] * o1[...] + w[2] * o2[...]
        r = _iota((2 * AT_L, 2 * AT_L), 0) // AH
        c = _iota((2 * AT_L, 2 * AT_L), 1) // AH
        hs = _dot01(da * att, (r == c).astype(BF16))
        for wg, dref, eref in zip(w, (d0, d1, d2), (e0, e1, e2)):
            dref[...] = wg * da
            eref[...] = -wg * hs

    blk = pl.BlockSpec((tm, 2 * AT_L), lambda i: (i, 0))
    shp = jax.ShapeDtypeStruct((t, 2 * AT_L), F32)
    outs = pl.pallas_call(
        body, out_shape=(shp,) * 6, grid=(t // tm,), in_specs=[blk] * 7, out_specs=(blk,) * 6,
        name="combine_bwd", compiler_params=_params(("parallel",)))(datt, *os_, *ls_)
    return outs[0:3], outs[3:6]


def _combine_proj(os_, ls_, w_pa):
    t = os_[0].shape[0]
    tm = ROW_TM
    nsh, _, ws = w_pa.shape

    def body(o0, o1, o2, l0, l1, l2, w_ref, a_ref, y_ref):
        w0, w1, w2 = _combine_weights(l0[...], l1[...], l2[...])
        att = w0 * o0[...] + w1 * o1[...] + w2 * o2[...]
        a_ref[...] = att
        ab = att.astype(BF16)
        for sh in range(nsh):
            y_ref[:, ws * sh:ws * (sh + 1)] = jnp.dot(ab, w_ref[sh], preferred_element_type=F32)

    blk = pl.BlockSpec((tm, 2 * AT_L), lambda i: (i, 0))
    return pl.pallas_call(
        body, out_shape=(jax.ShapeDtypeStruct((t, 2 * AT_L), F32), jax.ShapeDtypeStruct((t, nsh * ws), F32)),
        grid=(t // tm,), in_specs=[blk] * 6 + [pl.BlockSpec(w_pa.shape, lambda i: (0, 0, 0))],
        out_specs=(blk, pl.BlockSpec((tm, nsh * ws), lambda i: (i, 0))),
        name="combine_proj", compiler_params=_params(("parallel",)))(*os_, *ls_, w_pa)


def _d_att_combine_bwd(dy_att, w_pa, os_, ls_):
    t = dy_att.shape[0]
    tm = ROW_TM
    nsh, _, ws = w_pa.shape

    def body(dy_ref, w_ref, o0, o1, o2, l0, l1, l2, d0, d1, d2, e0, e1, e2):
        da = jnp.zeros((tm, 2 * AT_L), F32)
        for sh in range(nsh):
            da = da + _dot_nt(dy_ref[:, ws * sh:ws * (sh + 1)], w_ref[sh])
        w = _combine_weights(l0[...], l1[...], l2[...])
        att = w[0] * o0[...] + w[1] * o1[...] + w[2] * o2[...]
        r = _iota((2 * AT_L, 2 * AT_L), 0) // AH
        c = _iota((2 * AT_L, 2 * AT_L), 1) // AH
        hs = _dot01(da * att, (r == c).astype(BF16))
        for wg, dref, eref in zip(w, (d0, d1, d2), (e0, e1, e2)):
            dref[...] = wg * da
            eref[...] = -wg * hs

    blk = pl.BlockSpec((tm, 2 * AT_L), lambda i: (i, 0))
    shp = jax.ShapeDtypeStruct((t, 2 * AT_L), F32)
    outs = pl.pallas_call(
        body, out_shape=(shp,) * 6, grid=(t // tm,),
        in_specs=[pl.BlockSpec((tm, nsh * ws), lambda i: (i, 0)), pl.BlockSpec(w_pa.shape, lambda i: (0, 0, 0))] + [blk] * 6,
        out_specs=(blk,) * 6, name="d_att_combine_bwd", compiler_params=_params(("parallel",)))(dy_att, w_pa, *os_, *ls_)
    return outs[0:3], outs[3:6]


ROW_TM = 512


def _mix_fwd(y_ssd, y_att, u, bg_row):
    t = y_ssd.shape[0]
    tm = ROW_TM

    def body(ys_ref, ya_ref, g0_ref, g1_ref, b0_ref, b1_ref, o_ref):
        g0 = _sigmoid(g0_ref[...] + b0_ref[...])
        g1 = _sigmoid(g1_ref[...] + b1_ref[...])
        o_ref[...] = (g0 * ys_ref[...] + g1 * ya_ref[...]).astype(BF16)

    blk = pl.BlockSpec((tm, D), lambda i: (i, 0))
    return pl.pallas_call(
        body, out_shape=jax.ShapeDtypeStruct((t, D), BF16), grid=(t // tm,),
        in_specs=[blk, blk, pl.BlockSpec((tm, D), lambda i: (i, OGATE // D)), pl.BlockSpec((tm, D), lambda i: (i, OGATE // D + 1)),
                  pl.BlockSpec((1, D), lambda i: (0, 0)), pl.BlockSpec((1, D), lambda i: (0, 1))],
        out_specs=blk, name="mix_fwd", compiler_params=_params(("parallel",)))(y_ssd, y_att, u, u, bg_row, bg_row)


def _mix_bwd(dmixin, y_ssd, y_att, u, bg_row):
    t = y_ssd.shape[0]
    tm = ROW_TM

    def body(dm_ref, ys_ref, ya_ref, g0_ref, g1_ref, b0_ref, b1_ref, dys_ref, dya_ref, du_ref, db_ref):
        i = pl.program_id(0)
        g0 = _sigmoid(g0_ref[...] + b0_ref[...])
        g1 = _sigmoid(g1_ref[...] + b1_ref[...])
        dm = dm_ref[...]
        dys_ref[...] = (dm * g0).astype(BF16)
        dya_ref[...] = (dm * g1).astype(BF16)
        dl0 = dm * ys_ref[...] * g0 * (1.0 - g0)
        dl1 = dm * ya_ref[...] * g1 * (1.0 - g1)
        du_ref[:, 0:D] = dl0.astype(BF16)
        du_ref[:, D:2 * D] = dl1.astype(BF16)
        part = jnp.concatenate([jnp.broadcast_to(jnp.sum(dl0, axis=0, keepdims=True), (8, D)),
                                jnp.broadcast_to(jnp.sum(dl1, axis=0, keepdims=True), (8, D))], axis=1)

        @pl.when(i == 0)
        def _():
            db_ref[...] = part

        @pl.when(i > 0)
        def _():
            db_ref[...] += part

    blk = pl.BlockSpec((tm, D), lambda i: (i, 0))
    return pl.pallas_call(
        body,
        out_shape=(jax.ShapeDtypeStruct((t, D), BF16), jax.ShapeDtypeStruct((t, D), BF16),
                   jax.ShapeDtypeStruct((t, UW), BF16), jax.ShapeDtypeStruct((8, 2 * D), F32)),
        grid=(t // tm,),
        in_specs=[blk, blk, blk, pl.BlockSpec((tm, D), lambda i: (i, OGATE // D)), pl.BlockSpec((tm, D), lambda i: (i, OGATE // D + 1)),
                  pl.BlockSpec((1, D), lambda i: (0, 0)), pl.BlockSpec((1, D), lambda i: (0, 1))],
        out_specs=(blk, blk, pl.BlockSpec((tm, 2 * D), lambda i: (i, OGATE // (2 * D))),
                   pl.BlockSpec((8, 2 * D), lambda i: (0, 0))),
        name="mix_bwd", compiler_params=_params(("arbitrary",)))(dmixin, y_ssd, y_att, u, u, bg_row, bg_row)


def _ln(x, g, b):
    mu = jnp.mean(x, axis=1, keepdims=True)
    xc = x - mu
    var = jnp.mean(xc * xc, axis=1, keepdims=True)
    rstd = lax.rsqrt(var + NORM_EPS)
    xhat = xc * rstd
    return xhat * g + b, xhat, rstd


def _ln_back(dh, xhat, rstd, g):
    dxh = dh * g
    m1 = jnp.mean(dxh, axis=1, keepdims=True)
    m2 = jnp.mean(dxh * xhat, axis=1, keepdims=True)
    return rstd * (dxh - m1 - xhat * m2)


def _ln1_fwd(x, mix, g_row, b_row):
    t = x.shape[0]
    tm = ROW_TM

    def body(x_ref, m_ref, g_ref, b_ref, pre_ref, h_ref):
        pre = ALPHA * x_ref[...] + m_ref[...]
        pre_ref[...] = pre
        h, _, _ = _ln(pre, g_ref[...], b_ref[...])
        h_ref[...] = h.astype(BF16)

    blk = pl.BlockSpec((tm, D), lambda i: (i, 0))
    row = pl.BlockSpec((1, D), lambda i: (0, 0))
    return pl.pallas_call(
        body, out_shape=(jax.ShapeDtypeStruct((t, D), F32), jax.ShapeDtypeStruct((t, D), BF16)), grid=(t // tm,),
        in_specs=[blk, blk, row, row], out_specs=(blk, blk),
        name="ln1_fwd", compiler_params=_params(("parallel",)))(x, mix, g_row, b_row)


def _ln1_bwd(dh, pre, g_row, b_row):
    t = dh.shape[0]
    tm = ROW_TM

    def body(dh_ref, pre_ref, g_ref, b_ref, dpre_ref, acc_ref):
        i = pl.program_id(0)
        dh_ = dh_ref[...]
        _, xhat, rstd = _ln(pre_ref[...], g_ref[...], b_ref[...])
        dpre_ref[...] = _ln_back(dh_, xhat, rstd, g_ref[...])
        part = jnp.concatenate([jnp.sum(dh_ * xhat, axis=0, keepdims=True), jnp.sum(dh_, axis=0, keepdims=True),
                                jnp.zeros((6, D), F32)], axis=0)

        @pl.when(i == 0)
        def _():
            acc_ref[...] = part

        @pl.when(i > 0)
        def _():
            acc_ref[...] += part

    blk = pl.BlockSpec((tm, D), lambda i: (i, 0))
    row = pl.BlockSpec((1, D), lambda i: (0, 0))
    return pl.pallas_call(
        body, out_shape=(jax.ShapeDtypeStruct((t, D), F32), jax.ShapeDtypeStruct((8, D), F32)), grid=(t // tm,),
        in_specs=[blk, blk, row, row], out_specs=(blk, pl.BlockSpec((8, D), lambda i: (0, 0))),
        name="ln1_bwd", compiler_params=_params(("arbitrary",)))(dh, pre, g_row, b_row)


def _ln2_loss(pre1, f, tgt, g1_row, b1_row, g2_row, b2_row):
    t = pre1.shape[0]
    tm = ROW_TM

    def body(p1_ref, f_ref, t_ref, g1_ref, b1_ref, g2_ref, b2_ref, dpre_ref, acc_ref):
        i = pl.program_id(0)
        h1, _, _ = _ln(p1_ref[...], g1_ref[...], b1_ref[...])
        pre2 = ALPHA * h1 + f_ref[...]
        h2, xhat, rstd = _ln(pre2, g2_ref[...], b2_ref[...])
        err = h2 - t_ref[...]
        dh = err * (1.0 / D)
        dpre_ref[...] = _ln_back(dh, xhat, rstd, g2_ref[...])
        loss = jnp.sum(jnp.sum(err * err, axis=1, keepdims=True), axis=0, keepdims=True) * (0.5 / D)
        part = jnp.concatenate([jnp.sum(dh * xhat, axis=0, keepdims=True), jnp.sum(dh, axis=0, keepdims=True),
                                jnp.broadcast_to(loss, (1, D)), jnp.zeros((5, D), F32)], axis=0)

        @pl.when(i == 0)
        def _():
            acc_ref[...] = part

        @pl.when(i > 0)
        def _():
            acc_ref[...] += part

    blk = pl.BlockSpec((tm, D), lambda i: (i, 0))
    row = pl.BlockSpec((1, D), lambda i: (0, 0))
    return pl.pallas_call(
        body, out_shape=(jax.ShapeDtypeStruct((t, D), F32), jax.ShapeDtypeStruct((8, D), F32)), grid=(t // tm,),
        in_specs=[blk, blk, blk, row, row, row, row], out_specs=(blk, pl.BlockSpec((8, D), lambda i: (0, 0))),
        name="ln2_loss", compiler_params=_params(("arbitrary",)))(pre1, f, tgt, g1_row, b1_row, g2_row, b2_row)


def _mlp_up(h1, w_up):
    t = h1.shape[0]
    tm, tn = ROW_TM, D

    def body(a_ref, b_ref, up_ref, act_ref):
        up = jnp.dot(a_ref[...], b_ref[...], preferred_element_type=F32)
        up_ref[...] = up.astype(BF16)
        r = jnp.maximum(up, 0.0)
        act_ref[...] = (r * r).astype(BF16)

    blk = pl.BlockSpec((tm, tn), lambda j, i: (i, j))
    return pl.pallas_call(
        body, out_shape=(jax.ShapeDtypeStruct((t, DFF), BF16), jax.ShapeDtypeStruct((t, DFF), BF16)),
        grid=(DFF // tn, t // tm),
        in_specs=[pl.BlockSpec((tm, D), lambda j, i: (i, 0)), pl.BlockSpec((None, D, tn), lambda j, i: (j, 0, 0))],
        out_specs=(blk, blk), name="mlp_up", compiler_params=_params(("parallel", "parallel")))(h1, w_up)


def _d_up(dpre2, w_down, up):
    t = up.shape[0]
    tm, tk = ROW_TM, D

    def body(a_ref, b_ref, u_ref, o_ref):
        dact = _dot_nt(a_ref[...], b_ref[...])
        o_ref[...] = (dact * 2.0 * jnp.maximum(u_ref[...].astype(F32), 0.0)).astype(BF16)

    blk = pl.BlockSpec((tm, tk), lambda j, i: (i, j))
    return pl.pallas_call(
        body, out_shape=jax.ShapeDtypeStruct((t, DFF), BF16), grid=(DFF // tk, t // tm),
        in_specs=[pl.BlockSpec((tm, D), lambda j, i: (i, 0)), pl.BlockSpec((tk, D), lambda j, i: (j, 0)), blk],
        out_specs=blk, name="d_up", compiler_params=_params(("parallel", "parallel")))(dpre2, w_down, up)


def _dt_bwd(du, ddt):
    t = ddt.shape[0]
    tm = 1024

    def body(f_ref, du_in, o_ref):
        del du_in
        o_ref[:, 0:128] = f_ref[...].astype(o_ref.dtype)
        o_ref[:, 128:256] = jnp.zeros((tm, 128), o_ref.dtype)

    blk = pl.BlockSpec((tm, 128), lambda i: (i, 0))
    return pl.pallas_call(
        body, out_shape=jax.ShapeDtypeStruct(du.shape, du.dtype), grid=(t // tm,),
        in_specs=[blk, pl.BlockSpec(memory_space=pl.ANY)],
        out_specs=pl.BlockSpec((tm, 256), lambda i: (i, ODT // 256)), input_output_aliases={1: 0},
        name="dt_bwd", compiler_params=_params(("parallel",)))(ddt, du)


def _mix_out_ln1(y_ssd, y_att, u, bg_row, x, w_out, g_row, b_row):
    t = x.shape[0]
    tm = ROW_TM

    def body(ys_ref, ya_ref, g0_ref, g1_ref, b0_ref, b1_ref, x_ref, w_ref, g_ref, b_ref, mixin_ref, pre_ref, h_ref):
        g0 = _sigmoid(g0_ref[...] + b0_ref[...])
        g1 = _sigmoid(g1_ref[...] + b1_ref[...])
        mixin = (g0 * ys_ref[...] + g1 * ya_ref[...]).astype(BF16)
        mixin_ref[...] = mixin
        pre = ALPHA * x_ref[...] + jnp.dot(mixin, w_ref[...], preferred_element_type=F32)
        pre_ref[...] = pre
        h, _, _ = _ln(pre, g_ref[...], b_ref[...])
        h_ref[...] = h.astype(BF16)

    blk = pl.BlockSpec((tm, D), lambda i: (i, 0))
    row = pl.BlockSpec((1, D), lambda i: (0, 0))
    return pl.pallas_call(
        body,
        out_shape=(jax.ShapeDtypeStruct((t, D), BF16), jax.ShapeDtypeStruct((t, D), F32), jax.ShapeDtypeStruct((t, D), BF16)),
        grid=(t // tm,),
        in_specs=[blk, blk, pl.BlockSpec((tm, D), lambda i: (i, OGATE // D)), pl.BlockSpec((tm, D), lambda i: (i, OGATE // D + 1)),
                  row, pl.BlockSpec((1, D), lambda i: (0, 1)), blk, pl.BlockSpec((D, D), lambda i: (0, 0)), row, row],
        out_specs=(blk, blk, blk), name="mix_out_ln1", compiler_params=_params(("parallel",)))(
            y_ssd, y_att, u, u, bg_row, bg_row, x, w_out, g_row, b_row)


def _mlp_down_ln2_loss(act, w_down, pre1, tgt, g1_row, b1_row, g2_row, b2_row):
    t = pre1.shape[0]
    tm = ROW_TM

    def body(a_ref, w_ref, p1_ref, t_ref, g1_ref, b1_ref, g2_ref, b2_ref, dpre_ref, dpreb_ref, acc_ref):
        i = pl.program_id(0)
        f = jnp.dot(a_ref[...], w_ref[...], preferred_element_type=F32)
        h1, _, _ = _ln(p1_ref[...], g1_ref[...], b1_ref[...])
        pre2 = ALPHA * h1 + f
        h2, xhat, rstd = _ln(pre2, g2_ref[...], b2_ref[...])
        err = h2 - t_ref[...]
        dh = err * (1.0 / D)
        dpre = _ln_back(dh, xhat, rstd, g2_ref[...])
        dpre_ref[...] = dpre
        dpreb_ref[...] = dpre.astype(BF16)
        loss = jnp.sum(jnp.sum(err * err, axis=1, keepdims=True), axis=0, keepdims=True) * (0.5 / D)
        part = jnp.concatenate([jnp.sum(dh * xhat, axis=0, keepdims=True), jnp.sum(dh, axis=0, keepdims=True),
                                jnp.broadcast_to(loss, (1, D)), jnp.zeros((5, D), F32)], axis=0)

        @pl.when(i == 0)
        def _():
            acc_ref[...] = part

        @pl.when(i > 0)
        def _():
            acc_ref[...] += part

    blk = pl.BlockSpec((tm, D), lambda i: (i, 0))
    row = pl.BlockSpec((1, D), lambda i: (0, 0))
    return pl.pallas_call(
        body,
        out_shape=(jax.ShapeDtypeStruct((t, D), F32), jax.ShapeDtypeStruct((t, D), BF16), jax.ShapeDtypeStruct((8, D), F32)),
        grid=(t // tm,),
        in_specs=[pl.BlockSpec((tm, DFF), lambda i: (i, 0)), pl.BlockSpec((DFF, D), lambda i: (0, 0)), blk, blk, row, row, row, row],
        out_specs=(blk, blk, pl.BlockSpec((8, D), lambda i: (0, 0))),
        name="mlp_down_ln2_loss", compiler_params=_params(("arbitrary",)))(act, w_down, pre1, tgt, g1_row, b1_row, g2_row, b2_row)


def _d_h1_ln1_bwd(dup, w_up, dpre2, pre1, g_row, b_row):
    t = dup.shape[0]
    tm = ROW_TM
    nsh = w_up.shape[0]

    def body(a_ref, w_ref, add_ref, pre_ref, g_ref, b_ref, dpre_ref, acc_ref):
        i = pl.program_id(0)
        dh_ = ALPHA * add_ref[...]
        for sh in range(nsh):
            dh_ = dh_ + _dot_nt(a_ref[:, D * sh:D * (sh + 1)], w_ref[sh])
        _, xhat, rstd = _ln(pre_ref[...], g_ref[...], b_ref[...])
        dpre_ref[...] = _ln_back(dh_, xhat, rstd, g_ref[...])
        rows = jnp.concatenate([jnp.sum(dh_ * xhat, axis=0, keepdims=True), jnp.sum(dh_, axis=0, keepdims=True),
                                jnp.zeros((6, D), F32)], axis=0)

        @pl.when(i == 0)
        def _():
            acc_ref[...] = rows

        @pl.when(i > 0)
        def _():
            acc_ref[...] += rows

    blk = pl.BlockSpec((tm, D), lambda i: (i, 0))
    row = pl.BlockSpec((1, D), lambda i: (0, 0))
    return pl.pallas_call(
        body, out_shape=(jax.ShapeDtypeStruct((t, D), F32), jax.ShapeDtypeStruct((8, D), F32)),
        grid=(t // tm,),
        in_specs=[pl.BlockSpec((tm, nsh * D), lambda i: (i, 0)), pl.BlockSpec(w_up.shape, lambda i: (0, 0, 0)),
                  blk, blk, row, row],
        out_specs=(blk, pl.BlockSpec((8, D), lambda i: (0, 0))),
        name="d_h1_ln1_bwd", compiler_params=_params(("arbitrary",)))(dup, w_up, dpre2, pre1, g_row, b_row)


def _d_mixin_mix_bwd(dpre1, w_out, y_ssd, y_att, u, bg_row):
    t = y_ssd.shape[0]
    tm = ROW_TM

    def body(a_ref, w_ref, ys_ref, ya_ref, g0_ref, g1_ref, b0_ref, b1_ref, dys_ref, dya_ref, du_ref, db_ref):
        i = pl.program_id(0)
        dm = _dot_nt(a_ref[...].astype(BF16), w_ref[...])
        g0 = _sigmoid(g0_ref[...] + b0_ref[...])
        g1 = _sigmoid(g1_ref[...] + b1_ref[...])
        dys_ref[...] = (dm * g0).astype(BF16)
        dya_ref[...] = (dm * g1).astype(BF16)
        dl0 = dm * ys_ref[...] * g0 * (1.0 - g0)
        dl1 = dm * ya_ref[...] * g1 * (1.0 - g1)
        du_ref[:, 0:D] = dl0.astype(BF16)
        du_ref[:, D:2 * D] = dl1.astype(BF16)
        part = jnp.concatenate([jnp.broadcast_to(jnp.sum(dl0, axis=0, keepdims=True), (8, D)),
                                jnp.broadcast_to(jnp.sum(dl1, axis=0, keepdims=True), (8, D))], axis=1)

        @pl.when(i == 0)
        def _():
            db_ref[...] = part

        @pl.when(i > 0)
        def _():
            db_ref[...] += part

    blk = pl.BlockSpec((tm, D), lambda i: (i, 0))
    return pl.pallas_call(
        body,
        out_shape=(jax.ShapeDtypeStruct((t, D), BF16), jax.ShapeDtypeStruct((t, D), BF16),
                   jax.ShapeDtypeStruct((t, UW), BF16), jax.ShapeDtypeStruct((8, 2 * D), F32)),
        grid=(t // tm,),
        in_specs=[blk, pl.BlockSpec((D, D), lambda i: (0, 0)), blk, blk,
                  pl.BlockSpec((tm, D), lambda i: (i, OGATE // D)), pl.BlockSpec((tm, D), lambda i: (i, OGATE // D + 1)),
                  pl.BlockSpec((1, D), lambda i: (0, 0)), pl.BlockSpec((1, D), lambda i: (0, 1))],
        out_specs=(blk, blk, pl.BlockSpec((tm, 2 * D), lambda i: (i, OGATE // (2 * D))),
                   pl.BlockSpec((8, 2 * D), lambda i: (0, 0))),
        name="d_mixin_mix_bwd", compiler_params=_params(("arbitrary",)))(dpre1, w_out, y_ssd, y_att, u, u, bg_row, bg_row)


def _adamw(w, g, m, v, name):
    r, c = w.shape
    tr, tc = r, c
    for cand in (256, 128, 64, 32, 16, 8):
        if r % cand == 0 and cand * c * 4 <= 2 ** 21:
            tr = cand
            break
    if tr < 64 and c % 256 == 0:
        tr, tc = r, 256
    bc1 = 1.0 / (1.0 - ADAM_B1 ** ADAM_STEP)
    bc2 = 1.0 / (1.0 - ADAM_B2 ** ADAM_STEP)

    def body(w_ref, g_ref, m_ref, v_ref, d_ref, nm_ref, nv_ref):
        gg = g_ref[...]
        nm = ADAM_B1 * m_ref[...] + (1.0 - ADAM_B1) * gg
        nv = ADAM_B2 * v_ref[...] + (1.0 - ADAM_B2) * (gg * gg)
        nm_ref[...] = nm
        nv_ref[...] = nv
        d_ref[...] = -ADAM_LR * ((nm * bc1) / (jnp.sqrt(nv * bc2) + ADAM_EPS) + ADAM_WD * w_ref[...])

    blk = pl.BlockSpec((tr, tc), lambda i, j: (i, j))
    shp = jax.ShapeDtypeStruct((r, c), F32)
    return pl.pallas_call(body, out_shape=(shp, shp, shp), grid=(r // tr, c // tc), in_specs=[blk] * 4,
                          out_specs=(blk,) * 3, name=name, compiler_params=_params(("parallel", "parallel")))(w, g, m, v)


def _perm_cols(w):
    z, xbc, dt = w[:, 0:2048], w[:, 2048:5120], w[:, 5120:5184]
    q, k, v, gate = w[:, 5184:5952], w[:, 5952:6720], w[:, 6720:7488], w[:, 7488:9536]
    kv = []
    for g in range(3):
        for p in range(2):
            lo = 256 * g + 128 * p
            kv += [k[:, lo:lo + 128], v[:, lo:lo + 128]]
    pad = jnp.zeros((w.shape[0], UW - IN_COLS), w.dtype)
    return jnp.concatenate([z, gate, xbc] + kv + [q, dt, pad], axis=1)


def _unperm_cols(wp):
    z, gate, xbc = wp[:, OZ:OZ + 2048], wp[:, OGATE:OGATE + 2048], wp[:, OXBC:OXBC + CONVD]
    q, dt = wp[:, OQ:OQ + 768], wp[:, ODT:ODT + 64]
    ks, vs = [], []
    for g in range(3):
        for p in range(2):
            lo = OKV + 128 * (4 * g + 2 * p)
            ks.append(wp[:, lo:lo + 128])
            vs.append(wp[:, lo + 128:lo + 256])
    return jnp.concatenate([z, xbc, dt, q] + ks + vs + [gate], axis=1)


def _segments():
    segs = [(0, 2048), (7488, 9536), (2048, 5120)]
    for g in range(3):
        for p in range(2):
            lo = 256 * g + 128 * p
            segs += [(5952 + lo, 5952 + lo + 128), (6720 + lo, 6720 + lo + 128)]
    segs += [(5184, 5952), (5120, 5184)]
    out, pos = [], 0
    for a, b in segs:
        out.append((a, b, pos))
        pos += b - a
    return out


SHARD_COLS = IN_COLS // 4


def _perm_from_shards(w_shards):
    pieces = []
    for a, b, _ in _segments():
        while a < b:
            s = a // SHARD_COLS
            e = min(b, (s + 1) * SHARD_COLS)
            pieces.append(w_shards[s][:, a - s * SHARD_COLS:e - s * SHARD_COLS])
            a = e
    pieces.append(jnp.zeros((w_shards.shape[1], UW - IN_COLS), w_shards.dtype))
    return jnp.concatenate(pieces, axis=1)


def _shards_from_perm(wp):
    segs = sorted(_segments())
    shards = []
    for s in range(4):
        lo, hi = s * SHARD_COLS, (s + 1) * SHARD_COLS
        pieces = []
        for a, b, pos in segs:
            x, y = max(a, lo), min(b, hi)
            if x < y:
                pieces.append(wp[:, pos + x - a:pos + y - a])
        shards.append(jnp.concatenate(pieces, axis=1))
    return jnp.stack(shards)


def _lanes128(*vecs):
    v = jnp.concatenate([a.reshape(-1) for a in vecs])
    return jnp.pad(v, (0, 128 - v.shape[0])).reshape(1, 128)


EARLY = ("w_proj_ssd", "w_proj_attn", "w_out", "w_up", "w_down")


def _weights_of(gathered):
    g_ps, g_pa, g_o, g_up, g_dn = gathered
    return {"w_proj_ssd": g_ps.reshape(DI, D), "w_proj_attn": g_pa, "w_out": g_o.reshape(D, D), "w_up": g_up,
            "w_down": g_dn.reshape(DFF, D)}


def _local_grads(x, tgt, wts, sm, rs_idx=None):
    row = lambda a: a.reshape(1, -1)
    bg_row, cb_row = row(sm["b_gate"]), row(sm["conv_b"])
    par = jnp.concatenate([_lanes128(sm["dt_bias_f"], sm["dt_bias_b"]), _lanes128(sm["a_log_f"], sm["a_log_b"]),
                           jnp.zeros((6, 128), F32)], axis=0)
    dsk_row = row(jnp.repeat(sm["d_skip"], HP))
    nw_row = row(sm["ssd_norm_w"])
    g1, b1, g2, b2 = row(sm["ln1_g"]), row(sm["ln1_b"]), row(sm["ln2_g"]), row(sm["ln2_b"])

    xb = x.astype(BF16)
    u, gathered = _in_proj(xb, wts["w_in_p"], side=_gather_side(wts["pending"]) if "pending" in wts else None)
    if gathered:
        wts = {**wts, **_weights_of(gathered)}
    xbc = _conv_fwd(u, sm["conv_w"], cb_row)
    y_f, st_f = _ssd_fwd(xbc, u, par, rev=False)
    y_fb, st_b = _ssd_fwd(xbc, u, par, y_f, rev=True)
    s_out = _gatenorm_fwd(y_fb, xbc, u, dsk_row, nw_row)
    y_ssd = _mm_nn(s_out, wts["w_proj_ssd"], tm=512, tn=1024, name="proj_ssd")
    att_o, att_l = [], []
    for g in range(3):
        o, l = _attn_fwd(u, g)
        att_o.append(o)
        att_l.append(l)
    att, y_att = _combine_proj(att_o, att_l, wts["w_proj_attn"])
    mixin, pre1, h1 = _mix_out_ln1(y_ssd, y_att, u, bg_row, x, wts["w_out"], g1, b1)
    up, act = _mlp_up(h1, wts["w_up"])
    dpre2, dpre2_b, acc2 = _mlp_down_ln2_loss(act, wts["w_down"], pre1, tgt, g1, b1, g2, b2)

    dw_down = _mm_tn(act, dpre2_b, tka=1024, tn=1024, tt=1024, name="dw_down")
    dup = _d_up(dpre2_b, wts["w_down"], up)
    dw_up = _mm_tn(h1, dup, tka=1024, tn=1024, tt=1024, name="dw_up", out_shards=4)
    dpre1, acc1 = _d_h1_ln1_bwd(dup, wts["w_up"], dpre2, pre1, g1, b1)
    dw_out = _mm_tn(mixin, dpre1, tka=1024, tn=1024, tt=1024, name="dw_out")
    dy_ssd, dy_att, du, dbg = _d_mixin_mix_bwd(dpre1, wts["w_out"], y_ssd, y_att, u, bg_row)
    dw_proj_ssd = _mm_tn(s_out, dy_ssd, tka=1024, tn=1024, tt=1024, name="dw_proj_ssd")
    ds_out = _mm_nt(dy_ssd, wts["w_proj_ssd"], tm=512, tk=1024, tc=1024, name="d_s_out")
    dw_proj_attn = _mm_tn(att, dy_att, tka=256, tn=256, tt=1024, name="dw_proj_attn", out_shards=4)
    do_g, e_g = _d_att_combine_bwd(dy_att, wts["w_proj_attn"], att_o, att_l)
    for g in range(3):
        du = _attn_dq(u, du, do_g[g], att_l[g], e_g[g], g)
        du = _attn_dkv(u, du, do_g[g], att_l[g], e_g[g], g)
    big = {
        "w_proj_ssd": dw_proj_ssd.reshape(4, DI // 4, D),
        "w_proj_attn": dw_proj_attn,
        "w_out": dw_out.reshape(4, D // 4, D),
        "w_up": dw_up,
        "w_down": dw_down.reshape(4, DFF // 4, D),
    }
    early = [big[n] for n in EARLY]
    dy, du, dnw, dds, recv = _gatenorm_bwd(ds_out, y_fb, xbc, u, du, dsk_row, nw_row,
                                           side=_swap_side(early) if rs_idx else None)
    if rs_idx:
        halves = [_add_half(g, r, rs_idx[0], f"rs_add_half_{n}") for g, r, n in zip(early, recv, EARLY)]
    dxs_f, dbc_f, ddt_f, sacc_f, recv = _ssd_bwd(xbc, u, par, dy, st_f, rev=False,
                                                 side=_step1_side([h[1] for h in halves]) if rs_idx else None)
    if rs_idx:
        k = len(EARLY)
        sums1 = [_rs_add1(h[0], ra, rb, rs_idx[1], f"rs_add1_{n}")
                 for h, ra, rb, n in zip(halves, recv[:k], recv[k:], EARLY)]
    dxs, dbc, ddt, sacc_b, recv = _ssd_bwd(
        xbc, u, par, dy, st_b, rev=True, add=(dxs_f, dbc_f, ddt_f),
        side=_step2_side([s1[2] for s1 in sums1], [s1[3] for s1 in sums1]) if rs_idx else None)
    pieces = None
    if rs_idx:
        pieces = {n: _rs_add2(s1[0], s1[1], ra, rb, rs_idx[1], f"rs_add2_{n}")
                  for s1, ra, rb, n in zip(sums1, recv[:k], recv[k:], EARLY)}
    dpre_c, dcw, dcb = _conv_dpre(u, dxs, dy, dbc, dsk_row, sm["conv_w"], cb_row)
    du = _conv_dx(du, dpre_c, sm["conv_w"])
    du = _dt_bwd(du, ddt)
    dw_in_p = _mm_tn(xb, du, tka=1024, tn=2432, tt=1024, name="dw_in")
    big["w_in"] = _shards_from_perm(dw_in_p)
    side = None
    if rs_idx:
        g = big["w_in"]
        half = _add_half(g, _run_side(_swap_side([g]), "rs_swap_halves")[0], rs_idx[0], "rs_add_half_w_in")
        side = _step1_side([half[1]])
    dx, recv = _d_x(du, wts["w_in_p"], dpre1, side)
    if rs_idx:
        s1 = _rs_add1(half[0], recv[0], recv[1], rs_idx[1], "rs_add1_w_in")
        ra2, rb2 = _run_side(_step2_side([s1[2]], [s1[3]]), "rs_step2")
        pieces["w_in"] = _rs_add2(s1[0], s1[1], ra2, rb2, rs_idx[1], "rs_add2_w_in")

    sacc = sacc_f + sacc_b
    small = {
        "b_gate": dbg[0], "conv_w": dcw[0:KCONV], "conv_b": dcb[0],
        "dt_bias_f": sacc[0, 0:32], "dt_bias_b": sacc[0, 32:64], "a_log_f": sacc[1, 0:32], "a_log_b": sacc[1, 32:64],
        "d_skip": dds[0, 0:32], "ssd_norm_w": dnw[0],
        "ln1_g": acc1[0], "ln1_b": acc1[1], "ln2_g": acc2[0], "ln2_b": acc2[1], "loss": acc2[2, 0:1],
    }
    return dx, big, small, pieces


HBM_SPEC = pl.BlockSpec(memory_space=pl.ANY)


def _place():
    x, y, c = lax.axis_index("x"), lax.axis_index("y"), lax.axis_index("c")
    chips = [(1 - x, y), (x, 1 - y), (1 - x, 1 - y)]
    return x, y, c, chips


def _gather_phases(n):
    def tools(ins, outs, send_sems, recv_sems):
        x, y, c, _ = _place()
        slots = (2 * x + y, 2 * (1 - x) + y, 2 * x + 1 - y, 2 * (1 - x) + 1 - y)
        peers = ((1 - x, y, c), (x, 1 - y, c), (x, y, 1 - c))

        def copy(w, k, src, dst, to):
            return pltpu.make_async_remote_copy(src_ref=src, dst_ref=dst, send_sem=send_sems.at[w, k],
                                                recv_sem=recv_sems.at[w, k], device_id=to, device_id_type=MESH)

        def rows(w, core, part):
            rh = ins[w].shape[0] // 2
            if part is None:
                return pl.ds(core * rh, rh)
            return pl.ds(core * rh + part * (rh // 2), rh // 2)

        def same(w, k, slot, core, part, to):
            blk = outs[w].at[slot, rows(w, core, part), :]
            return copy(w, k, blk, blk, to)

        def sends(w):
            q, q_x, q_y, q_d = slots
            x_nbr, y_nbr, sibling = peers
            mine = rows(w, c, None)
            mk = functools.partial
            return [mk(copy, w, 0, ins[w].at[mine, :], outs[w].at[q, mine, :], x_nbr),
                    mk(copy, w, 1, ins[w].at[mine, :], outs[w].at[q, mine, :], y_nbr),
                    mk(same, w, 2, q_x, c, 0, y_nbr), mk(same, w, 3, q_y, c, 1, x_nbr),
                    mk(same, w, 4, q_x, c, None, sibling), mk(same, w, 5, q_y, c, None, sibling),
                    mk(same, w, 6, q_d, c, 0, sibling), mk(same, w, 7, q_d, c, 1, sibling),
                    mk(copy, w, 8, ins[w], outs[w].at[q], sibling)]

        return c, slots, peers, same, sends

    def first(*refs):
        _, _, _, _, sends = tools(*refs)
        for w in range(n):
            cps = sends(w)
            for k in (8, 0, 1):
                cps[k]().start()

    def second(*refs):
        c, (_, q_x, q_y, _), (x_nbr, y_nbr, _), same, sends = tools(*refs)
        for w in range(n):
            cps = sends(w)
            same(w, 0, q_x, c, None, x_nbr).wait_recv()
            cps[2]().start()
            cps[4]().start()
            same(w, 1, q_y, c, None, y_nbr).wait_recv()
            cps[3]().start()
            cps[5]().start()

    def third(*refs):
        c, (_, _, _, q_d), (x_nbr, y_nbr, _), same, sends = tools(*refs)
        for w in range(n):
            cps = sends(w)
            same(w, 2, q_d, c, 0, y_nbr).wait_recv()
            cps[6]().start()
            same(w, 3, q_d, c, 1, x_nbr).wait_recv()
            cps[7]().start()

    def last(*refs):
        c, (_, q_x, q_y, q_d), (_, _, sibling), same, sends = tools(*refs)
        for w in range(n):
            same(w, 4, q_x, 1 - c, None, sibling).wait_recv()
            same(w, 5, q_y, 1 - c, None, sibling).wait_recv()
            same(w, 6, q_d, 1 - c, 0, sibling).wait_recv()
            same(w, 7, q_d, 1 - c, 1, sibling).wait_recv()
            sends(w)[8]().wait_recv()
        for w in range(n):
            for mk_cp in sends(w):
                mk_cp().wait_send()

    return first, second, third, last


def _gather_side(shards):
    first, second, third, last = _gather_phases(len(shards))
    shapes = tuple(jax.ShapeDtypeStruct((4,) + s.shape, s.dtype) for s in shards)
    return _Side(tuple(shards), shapes, (len(shards), 9), None, ((0.0, first), (0.36, second), (0.58, third), (1.0, last)))


def _allgather_weights(shards):
    n = len(shards)

    def body(*refs):
        ins, outs = refs[:n], refs[n:2 * n]
        send_sems, recv_sems = refs[2 * n:]
        x, y, c, _ = _place()
        q, q_x, q_y, q_d = 2 * x + y, 2 * (1 - x) + y, 2 * x + 1 - y, 2 * (1 - x) + 1 - y
        x_nbr, y_nbr, sibling = (1 - x, y, c), (x, 1 - y, c), (x, y, 1 - c)

        def copy(w, k, src, dst, to):
            return pltpu.make_async_remote_copy(src_ref=src, dst_ref=dst, send_sem=send_sems.at[w, k],
                                                recv_sem=recv_sems.at[w, k], device_id=to, device_id_type=MESH)

        def rows(w, core, part):
            rh = ins[w].shape[0] // 2
            if part is None:
                return pl.ds(core * rh, rh)
            return pl.ds(core * rh + part * (rh // 2), rh // 2)

        def same(w, k, slot, core, part, to):
            blk = outs[w].at[slot, rows(w, core, part), :]
            return copy(w, k, blk, blk, to)

        started = []
        for w in range(n):
            cp = copy(w, 8, ins[w], outs[w].at[q], sibling)
            cp.start()
            started.append(cp)
            mine = rows(w, c, None)
            for k, to in ((0, x_nbr), (1, y_nbr)):
                cp = copy(w, k, ins[w].at[mine, :], outs[w].at[q, mine, :], to)
                cp.start()
                started.append(cp)
        for w in range(n):
            same(w, 0, q_x, c, None, x_nbr).wait_recv()
            for cp in (same(w, 2, q_x, c, 0, y_nbr), same(w, 4, q_x, c, None, sibling)):
                cp.start()
                started.append(cp)
            same(w, 1, q_y, c, None, y_nbr).wait_recv()
            for cp in (same(w, 3, q_y, c, 1, x_nbr), same(w, 5, q_y, c, None, sibling)):
                cp.start()
                started.append(cp)
        for w in range(n):
            same(w, 2, q_d, c, 0, y_nbr).wait_recv()
            cp = same(w, 6, q_d, c, 0, sibling)
            cp.start()
            started.append(cp)
            same(w, 3, q_d, c, 1, x_nbr).wait_recv()
            cp = same(w, 7, q_d, c, 1, sibling)
            cp.start()
            started.append(cp)
        for w in range(n):
            same(w, 4, q_x, 1 - c, None, sibling).wait_recv()
            same(w, 5, q_y, 1 - c, None, sibling).wait_recv()
            same(w, 6, q_d, 1 - c, 0, sibling).wait_recv()
            same(w, 7, q_d, 1 - c, 1, sibling).wait_recv()
            copy(w, 8, ins[w], outs[w].at[q], sibling).wait_recv()
        for cp in started:
            cp.wait_send()

    return pl.pallas_call(
        body, out_shape=[jax.ShapeDtypeStruct((4,) + s.shape, s.dtype) for s in shards],
        in_specs=[HBM_SPEC] * n, out_specs=[HBM_SPEC] * n,
        scratch_shapes=[pltpu.SemaphoreType.DMA((n, 9)), pltpu.SemaphoreType.DMA((n, 9))],
        name="allgather_weights")(*shards)


def _swap_halves(grads):
    n = len(grads)

    def body(*refs):
        ins, outs = refs[:n], refs[n:2 * n]
        send_sems, recv_sems = refs[2 * n:]
        x, y, c, _ = _place()
        copies = []
        for w in range(n):
            rh = ins[w].shape[1] // 2
            for p in range(4):
                cp = pltpu.make_async_remote_copy(
                    src_ref=ins[w].at[p, pl.ds((1 - c) * rh, rh), :], dst_ref=outs[w].at[p],
                    send_sem=send_sems.at[w, p], recv_sem=recv_sems.at[w, p],
                    device_id=(x, y, 1 - c), device_id_type=MESH)
                cp.start()
                copies.append(cp)
        for cp in copies:
            cp.wait()

    return pl.pallas_call(
        body, out_shape=[jax.ShapeDtypeStruct((4, g.shape[1] // 2, g.shape[2]), F32) for g in grads],
        in_specs=[HBM_SPEC] * n, out_specs=[HBM_SPEC] * n,
        scratch_shapes=[pltpu.SemaphoreType.DMA((n, 4)), pltpu.SemaphoreType.DMA((n, 4))],
        name="rs_swap_halves")(*grads)


def _rs_step1(parts):
    n = len(parts)

    def body(*refs):
        ins, out_a, out_b = refs[:n], refs[n:2 * n], refs[2 * n:3 * n]
        send_sems, recv_sems = refs[3 * n:]
        x, y, c, _ = _place()
        copies = []
        for w in range(n):
            rq = ins[w].shape[1] // 2
            for i in range(2):
                copies.append(pltpu.make_async_remote_copy(
                    src_ref=ins[w].at[2 * (1 - x) + i, pl.ds(0, rq), :], dst_ref=out_a[w].at[i],
                    send_sem=send_sems.at[w, i], recv_sem=recv_sems.at[w, i],
                    device_id=(1 - x, y, c), device_id_type=MESH))
                copies.append(pltpu.make_async_remote_copy(
                    src_ref=ins[w].at[2 * i + 1 - y, pl.ds(rq, rq), :], dst_ref=out_b[w].at[i],
                    send_sem=send_sems.at[w, 2 + i], recv_sem=recv_sems.at[w, 2 + i],
                    device_id=(x, 1 - y, c), device_id_type=MESH))
        for cp in copies:
            cp.start()
        for cp in copies:
            cp.wait()

    quarter = lambda p: jax.ShapeDtypeStruct((2, p.shape[1] // 2, p.shape[2]), p.dtype)
    outs = pl.pallas_call(
        body, out_shape=[quarter(p) for p in parts] * 2,
        in_specs=[HBM_SPEC] * n, out_specs=[HBM_SPEC] * (2 * n),
        scratch_shapes=[pltpu.SemaphoreType.DMA((n, 4)), pltpu.SemaphoreType.DMA((n, 4))],
        name="rs_step1")(*parts)
    return outs[:n], outs[n:]


def _rs_step2(tas, tbs):
    n = len(tas)

    def body(*refs):
        in_a, in_b, out_a, out_b = refs[:n], refs[n:2 * n], refs[2 * n:3 * n], refs[3 * n:4 * n]
        send_sems, recv_sems = refs[4 * n:]
        x, y, c, _ = _place()
        copies = []
        for w in range(n):
            copies.append(pltpu.make_async_remote_copy(
                src_ref=in_a[w].at[1 - y], dst_ref=out_a[w], send_sem=send_sems.at[w, 0], recv_sem=recv_sems.at[w, 0],
                device_id=(x, 1 - y, c), device_id_type=MESH))
            copies.append(pltpu.make_async_remote_copy(
                src_ref=in_b[w].at[1 - x], dst_ref=out_b[w], send_sem=send_sems.at[w, 1], recv_sem=recv_sems.at[w, 1],
                device_id=(1 - x, y, c), device_id_type=MESH))
        for cp in copies:
            cp.start()
        for cp in copies:
            cp.wait()

    one = lambda p: jax.ShapeDtypeStruct(p.shape[1:], p.dtype)
    outs = pl.pallas_call(
        body, out_shape=[one(p) for p in tas] + [one(p) for p in tbs],
        in_specs=[HBM_SPEC] * (2 * n), out_specs=[HBM_SPEC] * (2 * n),
        scratch_shapes=[pltpu.SemaphoreType.DMA((n, 2)), pltpu.SemaphoreType.DMA((n, 2))],
        name="rs_step2")(*tas, *tbs)
    return outs[:n], outs[n:]


class _Side(NamedTuple):
    ins: tuple
    out_shapes: tuple
    nsem: tuple
    make: Callable
    phases: tuple = ()


def _swap_copies(ins, outs, send_sems, recv_sems):
    x, y, c, _ = _place()
    copies = []
    for w in range(len(ins)):
        rh = ins[w].shape[1] // 2
        for p in range(4):
            copies.append(pltpu.make_async_remote_copy(
                src_ref=ins[w].at[p, pl.ds((1 - c) * rh, rh), :], dst_ref=outs[w].at[p],
                send_sem=send_sems.at[w, p], recv_sem=recv_sems.at[w, p],
                device_id=(x, y, 1 - c), device_id_type=MESH))
    return copies


def _swap_side(grads):
    shapes = tuple(jax.ShapeDtypeStruct((4, g.shape[1] // 2, g.shape[2]), F32) for g in grads)
    return _Side(tuple(grads), shapes, (len(grads), 4), _swap_copies)


def _step1_copies(ins, outs, send_sems, recv_sems):
    n = len(ins)
    out_a, out_b = outs[:n], outs[n:]
    x, y, c, _ = _place()
    copies = []
    for w in range(n):
        rq = ins[w].shape[1] // 2
        for i in range(2):
            copies.append(pltpu.make_async_remote_copy(
                src_ref=ins[w].at[2 * (1 - x) + i, pl.ds(0, rq), :], dst_ref=out_a[w].at[i],
                send_sem=send_sems.at[w, i], recv_sem=recv_sems.at[w, i],
                device_id=(1 - x, y, c), device_id_type=MESH))
            copies.append(pltpu.make_async_remote_copy(
                src_ref=ins[w].at[2 * i + 1 - y, pl.ds(rq, rq), :], dst_ref=out_b[w].at[i],
                send_sem=send_sems.at[w, 2 + i], recv_sem=recv_sems.at[w, 2 + i],
                device_id=(x, 1 - y, c), device_id_type=MESH))
    return copies


def _step1_side(parts):
    quarter = tuple(jax.ShapeDtypeStruct((2, p.shape[1] // 2, p.shape[2]), p.dtype) for p in parts)
    return _Side(tuple(parts), quarter + quarter, (len(parts), 4), _step1_copies)


def _step2_copies(ins, outs, send_sems, recv_sems):
    n = len(ins) // 2
    in_a, in_b, out_a, out_b = ins[:n], ins[n:], outs[:n], outs[n:]
    x, y, c, _ = _place()
    copies = []
    for w in range(n):
        copies.append(pltpu.make_async_remote_copy(
            src_ref=in_a[w].at[1 - y], dst_ref=out_a[w], send_sem=send_sems.at[w, 0], recv_sem=recv_sems.at[w, 0],
            device_id=(x, 1 - y, c), device_id_type=MESH))
        copies.append(pltpu.make_async_remote_copy(
            src_ref=in_b[w].at[1 - x], dst_ref=out_b[w], send_sem=send_sems.at[w, 1], recv_sem=recv_sems.at[w, 1],
            device_id=(1 - x, y, c), device_id_type=MESH))
    return copies


def _step2_side(tas, tbs):
    one = tuple(jax.ShapeDtypeStruct(p.shape[1:], p.dtype) for p in tuple(tas) + tuple(tbs))
    return _Side(tuple(tas) + tuple(tbs), one, (len(tas), 2), _step2_copies)


def _phases_of(side, n_steps):
    if side.phases:
        return [(min(int(f * n_steps), n_steps - 1), fn) for f, fn in side.phases]

    def start(*refs):
        for cp in side.make(*refs):
            cp.start()

    def wait(*refs):
        for cp in side.make(*refs):
            cp.wait()

    return [(0, start), (n_steps - 1, wait)]


def _run_side(side, name):
    n_in, n_out = len(side.ins), len(side.out_shapes)

    def body(*refs):
        for _, fn in _phases_of(side, 1):
            fn(refs[:n_in], refs[n_in:n_in + n_out], *refs[n_in + n_out:])

    return pl.pallas_call(
        body, out_shape=list(side.out_shapes), in_specs=[HBM_SPEC] * n_in, out_specs=[HBM_SPEC] * n_out,
        scratch_shapes=[pltpu.SemaphoreType.DMA(side.nsem), pltpu.SemaphoreType.DMA(side.nsem)], name=name)(*side.ins)


def _host_call(body, side, n_steps, *, out_shape, in_specs, out_specs, scratch_shapes, args, aliases, name, sem):
    n_in, n_out, n_scr = len(in_specs), len(out_shape), len(scratch_shapes)
    if side is None:
        outs = pl.pallas_call(body, out_shape=tuple(out_shape), grid=(n_steps,), in_specs=list(in_specs),
                              out_specs=tuple(out_specs), scratch_shapes=list(scratch_shapes),
                              input_output_aliases=aliases, name=name, compiler_params=_params(sem))(*args)
        return tuple(outs), ()
    ns_in, ns_out = len(side.ins), len(side.out_shapes)

    def wrapped(*refs):
        h_in, s_in = refs[:n_in], refs[n_in:n_in + ns_in]
        o0 = n_in + ns_in
        h_out, s_out = refs[o0:o0 + n_out], refs[o0 + n_out:o0 + n_out + ns_out]
        c0 = o0 + n_out + ns_out
        h_scr, sems = refs[c0:c0 + n_scr], refs[c0 + n_scr:]
        step = pl.program_id(0)
        phases = _phases_of(side, n_steps)
        for at, fn in phases[:-1]:
            pl.when(step == at)(functools.partial(fn, s_in, s_out, *sems))
        body(*h_in, *h_out, *h_scr)
        pl.when(step == phases[-1][0])(functools.partial(phases[-1][1], s_in, s_out, *sems))

    outs = pl.pallas_call(
        wrapped, out_shape=tuple(out_shape) + tuple(side.out_shapes), grid=(n_steps,),
        in_specs=list(in_specs) + [HBM_SPEC] * ns_in, out_specs=tuple(out_specs) + (HBM_SPEC,) * ns_out,
        scratch_shapes=list(scratch_shapes) + [pltpu.SemaphoreType.DMA(side.nsem), pltpu.SemaphoreType.DMA(side.nsem)],
        input_output_aliases=aliases, name=name, compiler_params=_params(sem))(*args, *side.ins)
    return tuple(outs[:n_out]), tuple(outs[n_out:])


def _join_halves(pieces):
    n = len(pieces)

    def body(*refs):
        outs = refs[n:2 * n]
        send_sems, recv_sems = refs[2 * n:]
        x, y, c, _ = _place()

        def copy(w, slot):
            return pltpu.make_async_remote_copy(
                src_ref=outs[w].at[slot], dst_ref=outs[w].at[slot], send_sem=send_sems.at[w], recv_sem=recv_sems.at[w],
                device_id=(x, y, 1 - c), device_id_type=MESH)

        for w in range(n):
            copy(w, c).start()
        for w in range(n):
            copy(w, 1 - c).wait_recv()
            copy(w, c).wait_send()

    return pl.pallas_call(
        body, out_shape=[jax.ShapeDtypeStruct(p.shape, F32) for p in pieces],
        in_specs=[HBM_SPEC] * n, out_specs=[HBM_SPEC] * n, input_output_aliases={w: w for w in range(n)},
        scratch_shapes=[pltpu.SemaphoreType.DMA((n,)), pltpu.SemaphoreType.DMA((n,))],
        name="rs_join_halves")(*pieces)


def _add_tile_rows(rh, c):
    for cand in (512, 256, 128, 64, 32, 16, 8):
        if rh % cand == 0 and cand * c * 4 <= 2 ** 21:
            return cand
    return rh


def _add_half(grad, recv, c_idx, name):
    _, r, cc = grad.shape
    rh = r // 2
    tr = _add_tile_rows(rh, cc)
    nb = rh // tr

    def body(c_ref, g_ref, r_ref, o_ref, ob_ref):
        del c_ref
        s = g_ref[...] + r_ref[...]
        o_ref[...] = s
        ob_ref[...] = s.astype(BF16)

    blk = pl.BlockSpec((None, tr, cc), lambda p, i, c_ref: (p, i, 0))
    grid_spec = pltpu.PrefetchScalarGridSpec(
        num_scalar_prefetch=1, grid=(4, nb),
        in_specs=[pl.BlockSpec((None, tr, cc), lambda p, i, c_ref: (p, c_ref[0] * nb + i, 0)), blk],
        out_specs=(blk, blk))
    return pl.pallas_call(
        body, out_shape=(jax.ShapeDtypeStruct((4, rh, cc), F32), jax.ShapeDtypeStruct((4, rh, cc), BF16)),
        grid_spec=grid_spec, name=name, compiler_params=_params(("parallel", "parallel")))(c_idx, grad, recv)


def _rs_add1(part, recv_a, recv_b, xy_idx, name):
    _, rh, cc = part.shape
    rq = rh // 2
    tr = _add_tile_rows(rq, cc)
    nb = rq // tr

    def body(xy_ref, pa_ref, pb_ref, ra_ref, rb_ref, ta_ref, tb_ref, tab_ref, tbb_ref):
        del xy_ref
        ta = pa_ref[...] + ra_ref[...].astype(F32)
        tb = pb_ref[...] + rb_ref[...].astype(F32)
        ta_ref[...] = ta
        tb_ref[...] = tb
        tab_ref[...] = ta.astype(BF16)
        tbb_ref[...] = tb.astype(BF16)

    blk = pl.BlockSpec((None, tr, cc), lambda i, j, xy: (i, j, 0))
    grid_spec = pltpu.PrefetchScalarGridSpec(
        num_scalar_prefetch=1, grid=(2, nb),
        in_specs=[pl.BlockSpec((None, tr, cc), lambda i, j, xy: (2 * xy[0] + i, j, 0)),
                  pl.BlockSpec((None, tr, cc), lambda i, j, xy: (2 * i + xy[1], nb + j, 0)), blk, blk],
        out_specs=(blk, blk, blk, blk))
    f32s, b16s = jax.ShapeDtypeStruct((2, rq, cc), F32), jax.ShapeDtypeStruct((2, rq, cc), BF16)
    return pl.pallas_call(body, out_shape=(f32s, f32s, b16s, b16s), grid_spec=grid_spec, name=name,
                          compiler_params=_params(("parallel", "parallel")))(xy_idx, part, part, recv_a, recv_b)


def _rs_add2(ta, tb, recv_a, recv_b, xy_idx, name):
    _, rq, cc = ta.shape
    tr = _add_tile_rows(rq, cc)
    nb = rq // tr

    def body(xy_ref, ta_ref, tb_ref, ra_ref, rb_ref, o_ref):
        del xy_ref
        s = pl.program_id(0)
        fa = ta_ref[...] + ra_ref[...].astype(F32)
        fb = tb_ref[...] + rb_ref[...].astype(F32)
        o_ref[...] = jnp.where(s == 0, fa, fb)

    rblk = pl.BlockSpec((tr, cc), lambda s, j, xy: (j, 0))
    grid_spec = pltpu.PrefetchScalarGridSpec(
        num_scalar_prefetch=1, grid=(2, nb),
        in_specs=[pl.BlockSpec((None, tr, cc), lambda s, j, xy: (xy[1], j, 0)),
                  pl.BlockSpec((None, tr, cc), lambda s, j, xy: (xy[0], j, 0)), rblk, rblk],
        out_specs=pl.BlockSpec((None, tr, cc), lambda s, j, xy: (xy[2], s * nb + j, 0)))
    return pl.pallas_call(body, out_shape=jax.ShapeDtypeStruct((2, 2 * rq, cc), F32), grid_spec=grid_spec, name=name,
                          compiler_params=_params(("parallel", "parallel")))(xy_idx, ta, tb, recv_a, recv_b)


def _allreduce_small(slab):
    r = slab.shape[0]

    def body(x_ref, o_ref, buf, send_sems, recv_sems):
        x, y, c, _ = _place()
        me = 4 * x + 2 * y + c
        buf[me] = x_ref[...]
        peers = []
        for k in range(1, 8):
            kx, ky, kc = (k >> 2) & 1, (k >> 1) & 1, k & 1
            peers.append((x + kx - 2 * x * kx, y + ky - 2 * y * ky, c + kc - 2 * c * kc))

        def copy(k, slot):
            return pltpu.make_async_remote_copy(src_ref=x_ref, dst_ref=buf.at[slot], send_sem=send_sems.at[k],
                                                recv_sem=recv_sems.at[k], device_id=peers[k], device_id_type=MESH)

        for k in range(7):
            copy(k, me).start()
        for k, (px, py, pc) in enumerate(peers):
            copy(k, 4 * px + 2 * py + pc).wait_recv()
        for k in range(7):
            copy(k, me).wait_send()
        acc = buf[0]
        for j in range(1, 8):
            acc = acc + buf[j]
        o_ref[...] = acc

    vm = pl.BlockSpec(memory_space=pltpu.VMEM)
    return pl.pallas_call(
        body, out_shape=jax.ShapeDtypeStruct((r, 128), F32), in_specs=[vm], out_specs=vm,
        scratch_shapes=[pltpu.VMEM((8, r, 128), F32), pltpu.SemaphoreType.DMA((7,)), pltpu.SemaphoreType.DMA((7,))],
        name="allreduce_small")(slab)


def _pack(arrs):
    rows = []
    for a in arrs:
        v = a.reshape(-1)
        v = jnp.pad(v, (0, (-v.shape[0]) % 128))
        rows.append(v.reshape(-1, 128))
    slab = jnp.concatenate(rows, axis=0)
    return jnp.pad(slab, ((0, (-slab.shape[0]) % 8), (0, 0)))


def _unpack(slab, shapes):
    out, r0 = [], 0
    for shp in shapes:
        size = math.prod(shp)
        nr = -(-size // 128)
        out.append(slab[r0:r0 + nr].reshape(-1)[:size].reshape(shp))
        r0 += nr
    return out


BIG = ("w_in", "w_proj_ssd", "w_proj_attn", "w_out", "w_up", "w_down")
SMALL = ("b_gate", "conv_w", "conv_b", "dt_bias_f", "dt_bias_b", "a_log_f", "a_log_b", "d_skip", "ssd_norm_w",
         "ln1_g", "ln1_b", "ln2_g", "ln2_b")
ORDER = ("w_in", "b_gate", "conv_w", "conv_b", "dt_bias_f", "dt_bias_b", "a_log_f", "a_log_b", "d_skip", "ssd_norm_w",
         "w_proj_ssd", "w_proj_attn", "w_out", "ln1_g", "ln1_b", "w_up", "w_down", "ln2_g", "ln2_b")


def kernel(x, w_in, b_gate, conv_w, conv_b, dt_bias_f, dt_bias_b, a_log_f, a_log_b, d_skip, ssd_norm_w, w_proj_ssd, w_proj_attn, w_out, ln1_g, ln1_b, w_up, w_down, ln2_g, ln2_b, loss_target, m_w_in, m_b_gate, m_conv_w, m_conv_b, m_dt_bias_f, m_dt_bias_b, m_a_log_f, m_a_log_b, m_d_skip, m_ssd_norm_w, m_w_proj_ssd, m_w_proj_attn, m_w_out, m_ln1_g, m_ln1_b, m_w_up, m_w_down, m_ln2_g, m_ln2_b, v_w_in, v_b_gate, v_conv_w, v_conv_b, v_dt_bias_f, v_dt_bias_b, v_a_log_f, v_a_log_b, v_d_skip, v_ssd_norm_w, v_w_proj_ssd, v_w_proj_attn, v_w_out, v_ln1_g, v_ln1_b, v_w_up, v_w_down, v_ln2_g, v_ln2_b):
    w = dict(w_in=w_in, b_gate=b_gate, conv_w=conv_w, conv_b=conv_b, dt_bias_f=dt_bias_f, dt_bias_b=dt_bias_b,
             a_log_f=a_log_f, a_log_b=a_log_b, d_skip=d_skip, ssd_norm_w=ssd_norm_w, w_proj_ssd=w_proj_ssd,
             w_proj_attn=w_proj_attn, w_out=w_out, ln1_g=ln1_g, ln1_b=ln1_b, w_up=w_up, w_down=w_down, ln2_g=ln2_g, ln2_b=ln2_b)
    m = dict(w_in=m_w_in, b_gate=m_b_gate, conv_w=m_conv_w, conv_b=m_conv_b, dt_bias_f=m_dt_bias_f, dt_bias_b=m_dt_bias_b,
             a_log_f=m_a_log_f, a_log_b=m_a_log_b, d_skip=m_d_skip, ssd_norm_w=m_ssd_norm_w, w_proj_ssd=m_w_proj_ssd,
             w_proj_attn=m_w_proj_attn, w_out=m_w_out, ln1_g=m_ln1_g, ln1_b=m_ln1_b, w_up=m_w_up, w_down=m_w_down,
             ln2_g=m_ln2_g, ln2_b=m_ln2_b)
    v = dict(w_in=v_w_in, b_gate=v_b_gate, conv_w=v_conv_w, conv_b=v_conv_b, dt_bias_f=v_dt_bias_f, dt_bias_b=v_dt_bias_b,
             a_log_f=v_a_log_f, a_log_b=v_a_log_b, d_skip=v_d_skip, ssd_norm_w=v_ssd_norm_w, w_proj_ssd=v_w_proj_ssd,
             w_proj_attn=v_w_proj_attn, w_out=v_w_out, ln1_g=v_ln1_g, ln1_b=v_ln1_b, w_up=v_w_up, w_down=v_w_down,
             ln2_g=v_ln2_g, ln2_b=v_ln2_b)
    xi, yi, ci = lax.axis_index("x"), lax.axis_index("y"), lax.axis_index("c")
    shard = 2 * xi + yi

    (g_in,) = _run_side(_gather_side([w["w_in"].astype(BF16)]), "allgather_w_in")
    wts = {"w_in_p": _perm_from_shards(g_in), "pending": [w[n].astype(BF16) for n in EARLY]}

    cw_slab = jnp.zeros((KCONV, 4, CONVD // 4), F32)
    cw_slab = lax.dynamic_update_slice(cw_slab, conv_w[:, None, :] * 0.5, (0, shard, 0))
    conv_w_all = _unpack(_allreduce_small(_pack([cw_slab])), [(KCONV, CONVD)])[0]

    sm = {n: w[n] for n in SMALL}
    sm["conv_w"] = conv_w_all
    c_idx = jnp.reshape(ci, (1,)).astype(jnp.int32)
    xy_idx = jnp.stack([xi, yi, ci]).astype(jnp.int32)
    dx, big, small, pieces = _local_grads(x[0], loss_target[0], wts, sm, rs_idx=(c_idx, xy_idx))

    names = list(SMALL) + ["loss"]
    shapes = [small[n].shape for n in names]
    red = dict(zip(names, _unpack(_allreduce_small(_pack([small[n] for n in names])), shapes)))
    loss = red["loss"].reshape(())
    gsm = {n: red[n] for n in SMALL}
    conv_w_grad_shard = lax.dynamic_slice_in_dim(gsm["conv_w"].reshape(KCONV, 4, CONVD // 4), shard, 1, axis=1)
    gsm["conv_w"] = conv_w_grad_shard.reshape(KCONV, CONVD // 4)

    joined = _join_halves([pieces[n] for n in BIG])
    gbig = {n: j.reshape(w[n].shape) for n, j in zip(BIG, joined)}

    grads, deltas, new_m, new_v = {}, {}, {}, {}
    for n in BIG:
        grads[n] = gbig[n]
        if n == "w_in":
            gt = gbig[n].T
            dlt, nmt, nvt = _adamw(w[n].T, gt, m[n].T, v[n].T, f"adamw_{n}")
            grads[n], deltas[n], new_m[n], new_v[n] = gt.T, dlt.T, nmt.T, nvt.T
            continue
        deltas[n], new_m[n], new_v[n] = _adamw(w[n], gbig[n], m[n], v[n], f"adamw_{n}")
    sshapes = [w[n].shape for n in SMALL]
    d_s, m_s, v_s = _adamw(_pack([w[n] for n in SMALL]), _pack([gsm[n] for n in SMALL]),
                           _pack([m[n] for n in SMALL]), _pack([v[n] for n in SMALL]), "adamw_small")
    for n, dd, mm, vv in zip(SMALL, _unpack(d_s, sshapes), _unpack(m_s, sshapes), _unpack(v_s, sshapes)):
        grads[n], deltas[n], new_m[n], new_v[n] = gsm[n], dd, mm, vv

    return (loss, dx[None], *[grads[n] for n in ORDER], *[deltas[n] for n in ORDER],
            *[new_m[n] for n in ORDER], *[new_v[n] for n in ORDER])
```

```python
import functools
import math
from typing import Callable, NamedTuple

import jax
import numpy as np
import jax.numpy as jnp
from jax import lax
from jax.experimental import pallas as pl
from jax.experimental.pallas import tpu as pltpu

F32, BF16 = jnp.float32, jnp.bfloat16
MESH = pl.DeviceIdType.MESH

D = 1024
DI = 2048
NH = 32
HP = 64
NG = 4
NS = 128
Q = 128
CONVD = 3072
KCONV = 5
DFF = 4096
AH = 64
ATT_HALF = 64
DILATIONS = (1, 4, 16)
IN_COLS = 9536
OZ, OGATE, OXBC, OKV, OQ, ODT, UW = 0, 2048, 4096, 7168, 8704, 9472, 9728
ALPHA = 2.0 ** 0.25
NORM_EPS = 1e-5
ADAM_LR, ADAM_B1, ADAM_B2, ADAM_EPS, ADAM_WD, ADAM_STEP = 0.001, 0.9, 0.999, 1e-8, 0.01, 10
VMEM_LIMIT = 56 * 2 ** 20
NEG = -1e30


def _params(sem):
    return pltpu.CompilerParams(dimension_semantics=sem, vmem_limit_bytes=VMEM_LIMIT)


def _sigmoid(x):
    return 1.0 / (1.0 + jnp.exp(-x))


def _softplus(x):
    e = jnp.exp(-jnp.abs(x))
    small = e * (1.0 - e * (0.5 - e * (1.0 / 3.0)))
    return jnp.maximum(x, 0.0) + jnp.where(e < 0.01, small, jnp.log(1.0 + e))


def _split3(a):
    hi = a.astype(BF16)
    r = a - hi.astype(F32)
    mid = r.astype(BF16)
    lo = (r - mid.astype(F32)).astype(BF16)
    return hi, mid, lo


def _dot01(a, m01):
    hi, mid, lo = _split3(a)
    d = lambda p: jnp.dot(p, m01, preferred_element_type=F32)
    return d(hi) + d(mid) + d(lo)


def _dot01_l(m01, a):
    hi, mid, lo = _split3(a)
    d = lambda p: jnp.dot(m01, p, preferred_element_type=F32)
    return d(hi) + d(mid) + d(lo)


def _dot_nt(a, b):
    return lax.dot_general(a, b, (((1,), (1,)), ((), ())), preferred_element_type=F32)


def _iota(shape, dim):
    return lax.broadcasted_iota(jnp.int32, shape, dim)


def _mm_nn(a, b, *, tm, tn, name, out_dtype=F32):
    m, k = a.shape
    if b.ndim == 3:
        assert tn == b.shape[2]
        n = b.shape[0] * b.shape[2]
        b_spec = pl.BlockSpec((None, k, tn), lambda j, i: (j, 0, 0))
    else:
        n = b.shape[1]
        b_spec = pl.BlockSpec((k, tn), lambda j, i: (0, j))

    def body(a_ref, b_ref, o_ref):
        o_ref[...] = jnp.dot(a_ref[...].astype(BF16), b_ref[...], preferred_element_type=F32).astype(out_dtype)

    return pl.pallas_call(
        body, out_shape=jax.ShapeDtypeStruct((m, n), out_dtype), grid=(n // tn, m // tm),
        in_specs=[pl.BlockSpec((tm, k), lambda j, i: (i, 0)), b_spec],
        out_specs=pl.BlockSpec((tm, tn), lambda j, i: (i, j)),
        name=name, compiler_params=_params(("parallel", "parallel")))(a, b)


def _mm_nt(a, b, *, tm, tk, tc, name, add=None, add_scale=1.0):
    m, n = a.shape
    if b.ndim == 3:
        assert tc == b.shape[2]
        k, nc = b.shape[1], b.shape[0]
        b_spec = pl.BlockSpec((None, tk, tc), lambda j, i, c: (c, j, 0))
    else:
        k, nc = b.shape[0], n // tc
        b_spec = pl.BlockSpec((tk, tc), lambda j, i, c: (j, c))

    def body(*refs):
        if add is None:
            a_ref, b_ref, o_ref = refs
        else:
            a_ref, b_ref, add_ref, o_ref = refs
        c = pl.program_id(2)
        part = _dot_nt(a_ref[...].astype(BF16), b_ref[...])

        @pl.when(c == 0)
        def _():
            if add is None:
                o_ref[...] = part
            else:
                o_ref[...] = part + add_scale * add_ref[...]

        @pl.when(c > 0)
        def _():
            o_ref[...] += part

    in_specs = [pl.BlockSpec((tm, tc), lambda j, i, c: (i, c)), b_spec]
    args = [a, b]
    if add is not None:
        in_specs.append(pl.BlockSpec((tm, tk), lambda j, i, c: (i, j)))
        args.append(add)
    return pl.pallas_call(
        body, out_shape=jax.ShapeDtypeStruct((m, k), F32), grid=(k // tk, m // tm, nc),
        in_specs=in_specs, out_specs=pl.BlockSpec((tm, tk), lambda j, i, c: (i, j)),
        name=name, compiler_params=_params(("parallel", "parallel", "arbitrary")))(*args)


def _mm_tn(a, b, *, tka, tn, tt, name, out_shards=None):
    t, ka = a.shape
    n = b.shape[1]
    if out_shards:
        assert tn == n // out_shards
        out_shape = jax.ShapeDtypeStruct((out_shards, ka, tn), F32)
        o_spec = pl.BlockSpec((None, tka, tn), lambda i, j, s: (j, i, 0))
    else:
        out_shape = jax.ShapeDtypeStruct((ka, n), F32)
        o_spec = pl.BlockSpec((tka, tn), lambda i, j, s: (i, j))

    def body(a_ref, b_ref, o_ref):
        s = pl.program_id(2)
        part = lax.dot_general(a_ref[...].astype(BF16), b_ref[...].astype(BF16), (((0,), (0,)), ((), ())),
                               preferred_element_type=F32)

        @pl.when(s == 0)
        def _():
            o_ref[...] = part

        @pl.when(s > 0)
        def _():
            o_ref[...] += part

    return pl.pallas_call(
        body, out_shape=out_shape, grid=(ka // tka, n // tn, t // tt),
        in_specs=[pl.BlockSpec((tt, tka), lambda i, j, s: (s, i)), pl.BlockSpec((tt, tn), lambda i, j, s: (s, j))],
        out_specs=o_spec, name=name, compiler_params=_params(("parallel", "parallel", "arbitrary")))(a, b)


def _d_x(du, w_in_p, dpre1, side=None):
    t = du.shape[0]
    tm, tc = 1024, 2432
    nc = UW // tc

    def body(a_ref, b_ref, add_ref, o_ref):
        c = pl.program_id(0) % nc
        part = _dot_nt(a_ref[...], b_ref[...])

        @pl.when(c == 0)
        def _():
            o_ref[...] = part + ALPHA * add_ref[...]

        @pl.when(c > 0)
        def _():
            o_ref[...] += part

    outs, side_outs = _host_call(
        body, side, (t // tm) * nc, out_shape=(jax.ShapeDtypeStruct((t, D), F32),),
        in_specs=[pl.BlockSpec((tm, tc), lambda s: (s // nc, s % nc)), pl.BlockSpec((D, tc), lambda s: (0, s % nc)),
                  pl.BlockSpec((tm, D), lambda s: (s // nc, 0))],
        out_specs=(pl.BlockSpec((tm, D), lambda s: (s // nc, 0)),),
        scratch_shapes=[], args=(du, w_in_p, dpre1), aliases={}, name="d_x", sem=("arbitrary",))
    return outs[0], side_outs


def _in_proj(xb, w_in_p, side=None):
    t, k = xb.shape
    tm, tn = 1024, 2432
    nm, nn = t // tm, UW // tn

    def body(a_ref, b_ref, o_ref):
        o_ref[...] = jnp.dot(a_ref[...], b_ref[...], preferred_element_type=F32)

    outs, side_outs = _host_call(
        body, side, nm * nn, out_shape=(jax.ShapeDtypeStruct((t, UW), F32),),
        in_specs=[pl.BlockSpec((tm, k), lambda s: (s % nm, 0)), pl.BlockSpec((k, tn), lambda s: (0, s // nm))],
        out_specs=(pl.BlockSpec((tm, tn), lambda s: (s % nm, s // nm)),),
        scratch_shapes=[], args=(xb, w_in_p), aliases={}, name="in_proj", sem=("arbitrary",))
    return outs[0], side_outs


CONV_TM = 512
CONV_TC = 1024
CONV_RC = 64
CONV_CC = 256


def _halo_specs(t, tm, tc, col0):
    nb8 = t // 8
    r8 = tm // 8
    return [
        pl.BlockSpec((8, tc), lambda i, j: (jnp.maximum(i * r8 - 1, 0), col0 + j)),
        pl.BlockSpec((tm, tc), lambda i, j: (i, col0 + j)),
        pl.BlockSpec((8, tc), lambda i, j: (jnp.minimum((i + 1) * r8, nb8 - 1), col0 + j)),
    ]


def _fill_ext(ext, prev_ref, cur_ref, next_ref, tm, i, last):
    ext[0:8, :] = jnp.where(i > 0, prev_ref[...], 0.0)
    ext[8:8 + tm, :] = cur_ref[...]
    ext[8 + tm:16 + tm, :] = jnp.where(i < last, next_ref[...], 0.0)


def _conv_fwd(u, conv_w, conv_b):
    t = u.shape[0]
    tm, tc = CONV_TM, CONV_TC

    def body(prev_ref, cur_ref, next_ref, w_ref, b_ref, o_ref, ext):
        _fill_ext(ext, prev_ref, cur_ref, next_ref, tm, pl.program_id(0), t // tm - 1)
        for c0 in range(0, tc, CONV_CC):
            cs = slice(c0, c0 + CONV_CC)
            w = w_ref[:, cs]
            for r0 in range(0, tm, CONV_RC):
                acc = jnp.broadcast_to(b_ref[:, cs], (CONV_RC, CONV_CC))
                for k in range(KCONV):
                    acc = acc + w[k:k + 1, :] * ext[pl.ds(r0 + 6 + k, CONV_RC), cs]
                o_ref[r0:r0 + CONV_RC, cs] = acc * _sigmoid(acc)

    return pl.pallas_call(
        body, out_shape=jax.ShapeDtypeStruct((t, CONVD), F32), grid=(t // tm, CONVD // tc),
        in_specs=_halo_specs(t, tm, tc, OXBC // tc) + [
            pl.BlockSpec((KCONV, tc), lambda i, j: (0, j)), pl.BlockSpec((1, tc), lambda i, j: (0, j))],
        out_specs=pl.BlockSpec((tm, tc), lambda i, j: (i, j)),
        scratch_shapes=[pltpu.VMEM((tm + 16, tc), F32)],
        name="conv_fwd", compiler_params=_params(("parallel", "parallel")))(u, u, u, conv_w, conv_b)


def _conv_dpre(u, dxs, dy, dbc, dsk_row, conv_w, conv_b):
    t = u.shape[0]
    tm, tc = CONV_TM, CONV_TC
    r8 = tm // 8
    nb8 = t // 8
    c0 = OXBC // tc

    def body(uprev, ucur, unext, f_ref, y_ref, cf_ref, dsk_ref, w_ref, bias_ref, dpre_ref, dw_ref, db_ref, ext):
        j = pl.program_id(0)
        i = pl.program_id(1)
        _fill_ext(ext, uprev, ucur, unext, tm, i, t // tm - 1)
        is_xs = j < 2
        dw_cols, db_cols = [], []
        for c0 in range(0, tc, CONV_CC):
            cs = slice(c0, c0 + CONV_CC)
            w = w_ref[:, cs]
            dsk = dsk_ref[:, cs]
            dw_acc = [jnp.zeros((1, CONV_CC), F32) for _ in range(KCONV)]
            db_acc = jnp.zeros((1, CONV_CC), F32)
            for r0 in range(0, tm, CONV_RC):
                rs = slice(r0, r0 + CONV_RC)
                taps = [ext[pl.ds(r0 + 6 + k, CONV_RC), cs] for k in range(KCONV)]
                pre = jnp.broadcast_to(bias_ref[:, cs], (CONV_RC, CONV_CC))
                for k in range(KCONV):
                    pre = pre + w[k:k + 1, :] * taps[k]
                s = _sigmoid(pre)
                up = jnp.where(is_xs, f_ref[rs, cs] + dsk * y_ref[rs, cs], cf_ref[rs, cs])
                dpre = up * (s * (1.0 + pre * (1.0 - s)))
                dpre_ref[rs, cs] = dpre
                for k in range(KCONV):
                    dw_acc[k] = dw_acc[k] + jnp.sum(dpre * taps[k], axis=0, keepdims=True)
                db_acc = db_acc + jnp.sum(dpre, axis=0, keepdims=True)
            dw_cols.append(jnp.concatenate(dw_acc + [jnp.zeros((8 - KCONV, CONV_CC), F32)], axis=0))
            db_cols.append(jnp.broadcast_to(db_acc, (8, CONV_CC)))
        dw_part = jnp.concatenate(dw_cols, axis=1)
        db_part = jnp.concatenate(db_cols, axis=1)

        @pl.when(i == 0)
        def _():
            dw_ref[...] = dw_part
            db_ref[...] = db_part

        @pl.when(i > 0)
        def _():
            dw_ref[...] += dw_part
            db_ref[...] += db_part

    xs_spec = pl.BlockSpec((tm, tc), lambda j, i: (jnp.where(j < 2, i, 0), jnp.minimum(j, 1)))
    bc_spec = pl.BlockSpec((tm, tc), lambda j, i: (jnp.where(j == 2, i, 0), 0))
    in_specs = [
        pl.BlockSpec((8, tc), lambda j, i: (jnp.maximum(i * r8 - 1, 0), c0 + j)),
        pl.BlockSpec((tm, tc), lambda j, i: (i, c0 + j)),
        pl.BlockSpec((8, tc), lambda j, i: (jnp.minimum((i + 1) * r8, nb8 - 1), c0 + j)),
        xs_spec, xs_spec, bc_spec,
        pl.BlockSpec((1, tc), lambda j, i: (0, jnp.minimum(j, 1))),
        pl.BlockSpec((KCONV, tc), lambda j, i: (0, j)), pl.BlockSpec((1, tc), lambda j, i: (0, j)),
    ]
    return pl.pallas_call(
        body,
        out_shape=(jax.ShapeDtypeStruct((t, CONVD), F32), jax.ShapeDtypeStruct((8, CONVD), F32),
                   jax.ShapeDtypeStruct((8, CONVD), F32)),
        grid=(CONVD // tc, t // tm), in_specs=in_specs,
        out_specs=(pl.BlockSpec((tm, tc), lambda j, i: (i, j)),
                   pl.BlockSpec((8, tc), lambda j, i: (0, j)), pl.BlockSpec((8, tc), lambda j, i: (0, j))),
        scratch_shapes=[pltpu.VMEM((tm + 16, tc), F32)],
        name="conv_dpre", compiler_params=_params(("parallel", "arbitrary")))(
            u, u, u, dxs, dy, dbc, dsk_row, conv_w, conv_b)


def _conv_dx(du, dpre, conv_w):
    t = dpre.shape[0]
    tm, tc = CONV_TM, CONV_TC
    r8 = tm // 8
    nb8 = t // 8

    def body(prev_ref, cur_ref, next_ref, w_ref, du_in, du_out, ext):
        del du_in
        _fill_ext(ext, prev_ref, cur_ref, next_ref, tm, pl.program_id(1), t // tm - 1)
        for c0 in range(0, tc, CONV_CC):
            cs = slice(c0, c0 + CONV_CC)
            w = w_ref[:, cs]
            for r0 in range(0, tm, CONV_RC):
                acc = jnp.zeros((CONV_RC, CONV_CC), F32)
                for k in range(KCONV):
                    acc = acc + w[k:k + 1, :] * ext[pl.ds(r0 + 10 - k, CONV_RC), cs]
                du_out[r0:r0 + CONV_RC, cs] = acc.astype(du_out.dtype)

    in_specs = [
        pl.BlockSpec((8, tc), lambda j, i: (jnp.maximum(i * r8 - 1, 0), j)),
        pl.BlockSpec((tm, tc), lambda j, i: (i, j)),
        pl.BlockSpec((8, tc), lambda j, i: (jnp.minimum((i + 1) * r8, nb8 - 1), j)),
        pl.BlockSpec((KCONV, tc), lambda j, i: (0, j)),
        pl.BlockSpec(memory_space=pl.ANY),
    ]
    return pl.pallas_call(
        body, out_shape=jax.ShapeDtypeStruct(du.shape, du.dtype), grid=(CONVD // tc, t // tm), in_specs=in_specs,
        out_specs=pl.BlockSpec((tm, tc), lambda j, i: (i, OXBC // tc + j)),
        scratch_shapes=[pltpu.VMEM((tm + 16, tc), F32)], input_output_aliases={4: 0},
        name="conv_dx", compiler_params=_params(("parallel", "parallel")))(dpre, dpre, dpre, conv_w, du)


def _ssd_common(dtr_ref, par_ref, rev):
    raw = dtr_ref[...]
    lane = _iota((1, 128), 1)
    mine = (lane >= 32 * rev) & (lane < 32 * rev + 32)
    bias = par_ref[0:1, :]
    arow = jnp.where(mine, -jnp.exp(par_ref[1:2, :]), 0.0)
    dt = _softplus(raw + bias)
    a = dt * arow
    ri = _iota((Q, Q), 0)
    ci = _iota((Q, Q), 1)
    tri = (ci >= ri) if rev else (ci <= ri)
    trit = (ci <= ri) if rev else (ci >= ri)
    cs = _dot01_l(tri.astype(BF16), a)
    return raw, bias, arow, mine, dt, cs, tri, trit


def _expand_mat(rev):
    r = np.arange(128)[:, None]
    c = np.arange(DI)[None, :]
    return jnp.asarray(r == (c // HP) + 32 * rev, BF16)


def _sum_mat(rev):
    r = np.arange(DI)[:, None]
    c = np.arange(128)[None, :]
    return jnp.asarray(c == (r // HP) + 32 * rev, BF16)


def _ssd_fwd(xbc, u, par, y_add=None, *, rev):
    t = xbc.shape[0]
    nc = t // Q
    end = 0 if rev else Q - 1
    cmap = (lambda c: nc - 1 - c) if rev else (lambda c: c)

    def body(xbc_ref, dtr_ref, par_ref, ex_ref, *rest):
        yadd_ref = rest[0] if y_add is not None else None
        y_ref, st_ref, h_scr = rest[-3:]
        step = pl.program_id(0)

        @pl.when(step == 0)
        def _():
            h_scr[...] = jnp.zeros((NS, DI), F32)

        raw, bias, arow, mine, dt, cs, tri, trit = _ssd_common(dtr_ref, par_ref, rev)
        cst = cs.T
        dtt = dt.T
        tot_col = cst[:, end:end + 1]
        wt = dtt * jnp.exp(tot_col - cst)
        ecs_all = jnp.exp(cs)
        gam = jnp.exp(cs[end:end + 1, :])
        gam_x = _dot01(jnp.broadcast_to(gam, (8, 128)), ex_ref[...])[0:1, :]
        lane = _iota((Q, 128), 1)
        sel = lane < HP
        st_ref[...] = h_scr[...]
        for g in range(NG):
            bg = xbc_ref[:, DI + NS * g:DI + NS * (g + 1)]
            cg = xbc_ref[:, DI + NG * NS + NS * g:DI + NG * NS + NS * (g + 1)]
            cb = _dot_nt(cg.astype(BF16), bg.astype(BF16))
            bt = bg.T
            for k in range(4):
                lo = 512 * g + 128 * k
                xp = xbc_ref[:, lo:lo + 128].astype(BF16)
                hp = h_scr[:, lo:lo + 128]
                rhs = jnp.concatenate([xp, hp.astype(BF16)], axis=0)
                lhs, bts = [], []
                for j in range(2):
                    hc = 8 * g + 2 * k + j + 32 * rev
                    csc = jnp.broadcast_to(cs[:, hc:hc + 1], (Q, Q))
                    lm = jnp.exp(jnp.where(tri, csc - cst[hc:hc + 1, :], NEG)) * dtt[hc:hc + 1, :]
                    mh = (cb * lm).astype(BF16)
                    ec = (jnp.broadcast_to(ecs_all[:, hc:hc + 1], (Q, NS)) * cg).astype(BF16)
                    lhs.append(jnp.concatenate([mh, ec], axis=1))
                    bts.append((bt * wt[hc:hc + 1, :]).astype(BF16))
                ys = jnp.dot(jnp.concatenate(lhs, axis=0), rhs, preferred_element_type=F32)
                ss = jnp.dot(jnp.concatenate(bts, axis=0), xp, preferred_element_type=F32)
                yp = jnp.where(sel, ys[0:Q], ys[Q:2 * Q])
                y_ref[:, lo:lo + 128] = yp if yadd_ref is None else yp + yadd_ref[:, lo:lo + 128]
                h_scr[:, lo:lo + 128] = gam_x[:, lo:lo + 128] * hp + jnp.where(sel, ss[0:NS], ss[NS:2 * NS])

    return pl.pallas_call(
        body,
        out_shape=(jax.ShapeDtypeStruct((t, DI), F32), jax.ShapeDtypeStruct((nc, NS, DI), F32)),
        grid=(nc,),
        in_specs=[pl.BlockSpec((Q, CONVD), lambda c: (cmap(c), 0)),
                  pl.BlockSpec((Q, 128), lambda c: (cmap(c), ODT // 128)),
                  pl.BlockSpec((8, 128), lambda c: (0, 0)),
                  pl.BlockSpec((128, DI), lambda c: (0, 0))]
        + ([pl.BlockSpec((Q, DI), lambda c: (cmap(c), 0))] if y_add is not None else []),
        out_specs=(pl.BlockSpec((Q, DI), lambda c: (cmap(c), 0)),
                   pl.BlockSpec((None, NS, DI), lambda c: (cmap(c), 0, 0))),
        scratch_shapes=[pltpu.VMEM((NS, DI), F32)],
        name="ssd_fwd_rev" if rev else "ssd_fwd", compiler_params=_params(("arbitrary",)))(
            xbc, u, par, _expand_mat(rev), *([y_add] if y_add is not None else []))


def _ssd_bwd(xbc, u, par, dy, st, *, rev, add=None, side=None):
    t = xbc.shape[0]
    nc = t // Q
    end = 0 if rev else Q - 1
    cmap = (lambda c: c) if rev else (lambda c: nc - 1 - c)

    def body(xbc_ref, dtr_ref, par_ref, dy_ref, hin_ref, ex_ref, sm_ref, *rest):
        addx_ref, addbc_ref, addt_ref = rest[:3] if add is not None else (None, None, None)
        dxs_ref, dbc_ref, ddt_ref, acc_ref, dh_scr = rest[-5:]
        step = pl.program_id(0)

        @pl.when(step == 0)
        def _():
            dh_scr[...] = jnp.zeros((NS, DI), F32)

        raw, bias, arow, mine, dt, cs, tri, trit = _ssd_common(dtr_ref, par_ref, rev)
        ri = _iota((Q, Q), 0)
        ci = _iota((Q, Q), 1)
        stri = ((ri > ci) if rev else (ri < ci)).astype(BF16)
        strit = ((ci > ri) if rev else (ci < ri)).astype(BF16)
        cst = cs.T
        dtt = dt.T
        et = jnp.exp(cst)
        ecs_all = jnp.exp(cs)
        ws_all = jnp.exp(cs[end:end + 1, :] - cs)
        expand = ex_ref[...]
        summat = sm_ref[...]
        gam = jnp.exp(cs[end:end + 1, :])
        gam_x = _dot01(jnp.broadcast_to(gam, (8, 128)), expand)[0:1, :]
        dt_hi, dt_mid, _ = _split3(dt)
        dtx = (jnp.dot(dt_hi, expand, preferred_element_type=F32)
               + jnp.dot(dt_mid, expand, preferred_element_type=F32))
        lane = _iota((Q, 128), 1)
        sel = lane < HP
        dho = dh_scr[...]
        t3 = jnp.sum(dho * hin_ref[...], axis=0, keepdims=True) * gam_x
        dxs_cols, dxs2_cols, yoff_cols, a1_rows = [], [], [], []
        for g in range(NG):
            bg = xbc_ref[:, DI + NS * g:DI + NS * (g + 1)]
            cg = xbc_ref[:, DI + NG * NS + NS * g:DI + NG * NS + NS * (g + 1)]
            bb = bg.astype(BF16)
            cbf = cg.astype(BF16)
            cb = _dot_nt(cbf, bb)
            cbt = _dot_nt(bb, cbf)
            ct = cg.T
            bdh = jnp.dot(bb, dho[:, 512 * g:512 * (g + 1)].astype(BF16), preferred_element_type=F32)
            dcb = jnp.zeros((Q, Q), F32)
            dcg = jnp.zeros((Q, NS), F32)
            dbg = jnp.zeros((Q, NS), F32)
            for k in range(4):
                lo = 512 * g + 128 * k
                xpf = xbc_ref[:, lo:lo + 128]
                xp = xpf.astype(BF16)
                dyp = dy_ref[:, lo:lo + 128]
                dypb = dyp.astype(BF16)
                hinp = hin_ref[:, lo:lo + 128].astype(BF16)
                dhp = dho[:, lo:lo + 128]
                es, ws, lmds, mts, ctes, dyms, ecbs = [], [], [], [], [], [], []
                for j in range(2):
                    hc = 8 * g + 2 * k + j + 32 * rev
                    csc = jnp.broadcast_to(cs[:, hc:hc + 1], (Q, Q))
                    csr = cst[hc:hc + 1, :]
                    lmds.append(jnp.exp(jnp.where(tri, csc - csr, NEG)) * dtt[hc:hc + 1, :])
                    lmb = jnp.exp(jnp.where(trit, csr - csc, NEG))
                    mts.append((cbt * lmb).astype(BF16))
                    dyms.append(jnp.where(sel if j == 0 else ~sel, dyp, 0.0).astype(BF16))
                    ecs = jnp.broadcast_to(ecs_all[:, hc:hc + 1], (Q, NS))
                    es.append(ecs)
                    ws.append(jnp.broadcast_to(ws_all[:, hc:hc + 1], (Q, NS)))
                    ecbs.append((ecs * cg).astype(BF16))
                    ctes.append((ct * et[hc:hc + 1, :]).astype(BF16))
                by_dy = jnp.dot(jnp.concatenate(mts + ctes, axis=0), dypb, preferred_element_type=F32)
                dmm = _dot_nt(jnp.concatenate(dyms, axis=0), xp)
                dm0, dm1 = dmm[0:Q] * lmds[0], dmm[Q:2 * Q] * lmds[1]
                dcb = dcb + dm0 + dm1
                rr = jnp.dot(jnp.concatenate([dm0 * cb, dm1 * cb], axis=0).astype(BF16), stri, preferred_element_type=F32)
                a1_rows.append(jnp.sum(jnp.where(tri, rr[0:Q], 0.0), axis=0, keepdims=True))
                a1_rows.append(jnp.sum(jnp.where(tri, rr[Q:2 * Q], 0.0), axis=0, keepdims=True))
                yo = jnp.dot(jnp.concatenate(ecbs, axis=0), hinp, preferred_element_type=F32)
                e_p = jnp.where(sel, es[0], es[1])
                w_p = jnp.where(sel, ws[0], ws[1])
                d2 = w_p * bdh[:, 128 * k:128 * (k + 1)]
                dxs2_cols.append(d2)
                dxs_cols.append(jnp.where(sel, by_dy[0:Q], by_dy[Q:2 * Q]) + d2)
                yoff_cols.append(jnp.where(sel, yo[0:Q], yo[Q:2 * Q]))
                dcg = dcg + _dot_nt((e_p * dyp).astype(BF16), hinp)
                dbg = dbg + _dot_nt((w_p * dtx[:, lo:lo + 128] * xpf).astype(BF16), dhp.astype(BF16))
                dh_scr[:, lo:lo + 128] = (gam_x[:, lo:lo + 128] * dhp
                                          + jnp.where(sel, by_dy[2 * Q:3 * Q], by_dy[3 * Q:4 * Q]))
            dcg = dcg + jnp.dot(dcb.astype(BF16), bb, preferred_element_type=F32)
            dbg = dbg + jnp.dot(dcb.T.astype(BF16), cbf, preferred_element_type=F32)
            lo_b, lo_c = NS * g, NG * NS + NS * g
            if addbc_ref is not None:
                dbg = dbg + addbc_ref[:, lo_b:lo_b + NS]
                dcg = dcg + addbc_ref[:, lo_c:lo_c + NS]
            dbc_ref[:, lo_b:lo_b + NS] = dbg
            dbc_ref[:, lo_c:lo_c + NS] = dcg
        dxs = jnp.concatenate(dxs_cols, axis=1)
        dxs_ref[...] = dxs * dtx if addx_ref is None else dxs * dtx + addx_ref[...]
        xs = xbc_ref[:, 0:DI]
        stacked = jnp.concatenate([xs * dxs, xs * jnp.concatenate(dxs2_cols, axis=1),
                                   dy_ref[...] * jnp.concatenate(yoff_cols, axis=1),
                                   jnp.broadcast_to(t3, (8, DI))], axis=0).astype(BF16)
        sums = jnp.dot(stacked, summat, preferred_element_type=F32)
        rx, rx2, ryo, c0 = sums[0:Q], sums[Q:2 * Q], sums[2 * Q:3 * Q], sums[3 * Q:3 * Q + 1]
        zero32 = jnp.zeros((32, Q), F32)
        a1t = jnp.concatenate(([zero32] if rev else []) + a1_rows + [zero32] * (2 if rev else 3), axis=0)
        da = (a1t.T + jnp.dot(trit.astype(BF16), ryo.astype(BF16), preferred_element_type=F32)
              + jnp.dot(strit, (dt * rx2).astype(BF16), preferred_element_type=F32) + jnp.where(mine, c0, 0.0))
        ddt = rx + da * arow
        ddtr = ddt * _sigmoid(raw + bias)
        ddt_ref[...] = ddtr if addt_ref is None else ddtr + addt_ref[...]
        part = jnp.concatenate([jnp.sum(ddtr, axis=0, keepdims=True),
                                jnp.sum(da * dt, axis=0, keepdims=True) * arow,
                                jnp.zeros((6, 128), F32)], axis=0)

        @pl.when(step == 0)
        def _():
            acc_ref[...] = part

        @pl.when(step > 0)
        def _():
            acc_ref[...] += part

    outs, side_outs = _host_call(
        body, side, nc,
        out_shape=(jax.ShapeDtypeStruct((t, DI), F32), jax.ShapeDtypeStruct((t, 2 * NG * NS), F32),
                   jax.ShapeDtypeStruct((t, 128), F32), jax.ShapeDtypeStruct((8, 128), F32)),
        in_specs=[pl.BlockSpec((Q, CONVD), lambda c: (cmap(c), 0)),
                  pl.BlockSpec((Q, 128), lambda c: (cmap(c), ODT // 128)),
                  pl.BlockSpec((8, 128), lambda c: (0, 0)),
                  pl.BlockSpec((Q, DI), lambda c: (cmap(c), 0)),
                  pl.BlockSpec((None, NS, DI), lambda c: (cmap(c), 0, 0)),
                  pl.BlockSpec((128, DI), lambda c: (0, 0)), pl.BlockSpec((DI, 128), lambda c: (0, 0))]
        + ([pl.BlockSpec((Q, DI), lambda c: (cmap(c), 0)), pl.BlockSpec((Q, 2 * NG * NS), lambda c: (cmap(c), 0)),
            pl.BlockSpec((Q, 128), lambda c: (cmap(c), 0))] if add is not None else []),
        out_specs=(pl.BlockSpec((Q, DI), lambda c: (cmap(c), 0)),
                   pl.BlockSpec((Q, 2 * NG * NS), lambda c: (cmap(c), 0)),
                   pl.BlockSpec((Q, 128), lambda c: (cmap(c), 0)),
                   pl.BlockSpec((8, 128), lambda c: (0, 0))),
        scratch_shapes=[pltpu.VMEM((NS, DI), F32)],
        args=(xbc, u, par, dy, st, _expand_mat(rev), _sum_mat(rev)) + (tuple(add) if add is not None else ()), aliases={},
        name="ssd_bwd_rev" if rev else "ssd_bwd", sem=("arbitrary",))
    return (*outs, side_outs)


GN_TM = 256
GN_GROUP = DI // NG


def _gn_forward_vals(y0, xs, z, dsk):
    y = y0 + dsk * xs
    sz = _sigmoid(z)
    gate = z * sz
    y2 = y * gate
    parts, rs = [], []
    for g in range(NG):
        seg = y2[:, GN_GROUP * g:GN_GROUP * (g + 1)]
        r = lax.rsqrt(jnp.mean(seg * seg, axis=1, keepdims=True) + NORM_EPS)
        rs.append(r)
        parts.append(seg * r)
    yn = jnp.concatenate(parts, axis=1)
    return y, sz, gate, yn, rs


def _gatenorm_fwd(y_fb, xbc, u, dsk_row, nw_row):
    t = y_fb.shape[0]
    tm = GN_TM

    def body(y_ref, xs_ref, z_ref, dsk_ref, nw_ref, o_ref):
        _, _, _, yn, _ = _gn_forward_vals(y_ref[...], xs_ref[...], z_ref[...], dsk_ref[...])
        o_ref[...] = (yn * nw_ref[...]).astype(BF16)

    blk = pl.BlockSpec((tm, DI), lambda i: (i, 0))
    row = pl.BlockSpec((1, DI), lambda i: (0, 0))
    return pl.pallas_call(
        body, out_shape=jax.ShapeDtypeStruct((t, DI), BF16), grid=(t // tm,),
        in_specs=[blk, blk, pl.BlockSpec((tm, DI), lambda i: (i, OZ // DI)), row, row],
        out_specs=blk, name="gatenorm_fwd", compiler_params=_params(("parallel",)))(y_fb, xbc, u, dsk_row, nw_row)


def _gatenorm_bwd(ds_out, y_fb, xbc, u, du, dsk_row, nw_row, side=None):
    t = y_fb.shape[0]
    tm = GN_TM

    def body(ds_ref, y_ref, xs_ref, z_ref, dsk_ref, nw_ref, sm_ref, du_in, dy_ref, du_out, dnw_ref, dds_ref):
        del du_in
        i = pl.program_id(0)
        xs = xs_ref[...]
        z = z_ref[...]
        y, sz, gate, yn, rs = _gn_forward_vals(y_ref[...], xs, z, dsk_ref[...])
        ds = ds_ref[...]
        gsc = ds * nw_ref[...]
        parts = []
        for g in range(NG):
            sl = slice(GN_GROUP * g, GN_GROUP * (g + 1))
            m = jnp.mean(gsc[:, sl] * yn[:, sl], axis=1, keepdims=True)
            parts.append(rs[g] * (gsc[:, sl] - yn[:, sl] * m))
        dy2 = jnp.concatenate(parts, axis=1)
        dy = dy2 * gate
        dy_ref[...] = dy
        du_out[...] = (dy2 * y * (sz * (1.0 + z * (1.0 - sz)))).astype(du_out.dtype)
        dnw = jnp.broadcast_to(jnp.sum(ds * yn, axis=0, keepdims=True), (8, DI))
        drow = jnp.broadcast_to(jnp.sum(dy * xs, axis=0, keepdims=True), (8, DI))
        dds = _dot01(drow, sm_ref[...])

        @pl.when(i == 0)
        def _():
            dnw_ref[...] = dnw
            dds_ref[...] = dds

        @pl.when(i > 0)
        def _():
            dnw_ref[...] += dnw
            dds_ref[...] += dds

    blk = pl.BlockSpec((tm, DI), lambda i: (i, 0))
    row = pl.BlockSpec((1, DI), lambda i: (0, 0))
    outs, side_outs = _host_call(
        body, side, t // tm,
        out_shape=(jax.ShapeDtypeStruct((t, DI), F32), jax.ShapeDtypeStruct(du.shape, du.dtype),
                   jax.ShapeDtypeStruct((8, DI), F32), jax.ShapeDtypeStruct((8, 128), F32)),
        in_specs=[blk, blk, blk, pl.BlockSpec((tm, DI), lambda i: (i, OZ // DI)), row, row,
                  pl.BlockSpec((DI, 128), lambda i: (0, 0)), pl.BlockSpec(memory_space=pl.ANY)],
        out_specs=(blk, pl.BlockSpec((tm, DI), lambda i: (i, OZ // DI)),
                   pl.BlockSpec((8, DI), lambda i: (0, 0)), pl.BlockSpec((8, 128), lambda i: (0, 0))),
        scratch_shapes=[], args=(ds_out, y_fb, xbc, u, dsk_row, nw_row, _sum_mat(0), du), aliases={7: 1},
        name="gatenorm_bwd", sem=("arbitrary",))
    return (*outs, side_outs)


AT_B = 128
AT_W = AT_B + 2 * ATT_HALF
AT_L = 2 * AH
SCALE = 1.0 / math.sqrt(AH)


def _slope(g, hh):
    return 2.0 ** (-8.0 * (4 * g + hh + 1) / 12.0)


def _qcol(g):
    return lambda p: OQ // AT_L + 2 * g + p


def _kcol(g):
    return lambda p: OKV // AT_L + 4 * g + 2 * p


def _vcol(g):
    return lambda p: OKV // AT_L + 4 * g + 2 * p + 1


def _pcol(p):
    return p


def _sub(d):
    return 4 if d == 1 else 1


def _win_specs(col, t, d):
    tb, hb = AT_B * d * _sub(d), ATT_HALF * d
    per = tb // hb
    nh = t // hb
    return [
        pl.BlockSpec((hb, AT_L), lambda p, i: (jnp.maximum(per * i - 1, 0), col(p))),
        pl.BlockSpec((tb, AT_L), lambda p, i: (i, col(p))),
        pl.BlockSpec((hb, AT_L), lambda p, i: (jnp.minimum(per * (i + 1), nh - 1), col(p))),
    ]


def _blk_spec(col, d):
    return pl.BlockSpec((AT_B * d * _sub(d), AT_L), lambda p, i: (i, col(p)))


def _rows(ref, r, s, d):
    return ref[pl.ds(r, AT_B, stride=d), :] if d > 1 else ref[AT_B * s:AT_B * (s + 1), :]


def _win(p_ref, c_ref, n_ref, r, s, d):
    if d > 1:
        return jnp.concatenate([p_ref[pl.ds(r, ATT_HALF, stride=d), :], c_ref[pl.ds(r, AT_B, stride=d), :],
                                n_ref[pl.ds(r, ATT_HALF, stride=d), :]], axis=0)
    if s == 0:
        return jnp.concatenate([p_ref[...], c_ref[0:AT_B + ATT_HALF, :]], axis=0)
    if s == _sub(d) - 1:
        return jnp.concatenate([c_ref[AT_B * s - ATT_HALF:AT_B * (s + 1), :], n_ref[...]], axis=0)
    return c_ref[AT_B * s - ATT_HALF:AT_B * (s + 1) + ATT_HALF, :]


def _put_rows(ref, r, s, d, val):
    if d > 1:
        ref[pl.ds(r, AT_B, stride=d), :] = val
    else:
        ref[AT_B * s:AT_B * (s + 1), :] = val


def _for_blocks(d, fn):
    if d == 1:
        for s in range(_sub(d)):
            fn(0, s)
    else:
        def step(r, c):
            fn(r, 0)
            return c
        lax.fori_loop(0, d, step, 0, unroll=4)


def _attn_bias(blk, ln, d, g, p_id):
    a = blk * AT_B + _iota((AT_B, AT_W), 0)
    b = blk * AT_B - ATT_HALF + _iota((AT_B, AT_W), 1)
    rel = jnp.abs(a - b)
    valid = (rel <= ATT_HALF) & (b >= 0) & (b < ln)
    dist = (rel * d).astype(F32)
    out = []
    for hh in range(2):
        slope = jnp.where(p_id == 0, _slope(g, hh), _slope(g, 2 + hh))
        out.append(jnp.where(valid, -slope * dist, NEG))
    return out


def _attn_fwd(u, g):
    t = u.shape[0]
    d = DILATIONS[g]
    ln = t // d

    def body(q_ref, kp, kc, kn, vp, vc, vn, o_ref, l_ref):
        p_id = pl.program_id(0)
        i = pl.program_id(1)
        lane = _iota((AT_B, AT_L), 1)
        biases = [_attn_bias(i * _sub(d) + s, ln, d, g, p_id) for s in range(_sub(d))]

        def one(r, s):
            q = _rows(q_ref, r, s, d) * SCALE
            kw = _win(kp, kc, kn, r, s, d).astype(BF16)
            vw = _win(vp, vc, vn, r, s, d).astype(BF16)
            o = jnp.zeros((AT_B, AT_L), F32)
            lse = jnp.zeros((AT_B, AT_L), F32)
            for hh in range(2):
                hm = (lane // AH) == hh
                qm = jnp.where(hm, q, 0.0).astype(BF16)
                sc = _dot_nt(qm, kw) + biases[s][hh]
                m = jnp.max(sc, axis=1, keepdims=True)
                pr = jnp.exp(sc - m)
                den = jnp.sum(pr, axis=1, keepdims=True)
                oh = jnp.dot(pr.astype(BF16), vw, preferred_element_type=F32)
                o = jnp.where(hm, oh / den, o)
                lse = jnp.where(hm, m + jnp.log(den), lse)
            _put_rows(o_ref, r, s, d, o)
            _put_rows(l_ref, r, s, d, lse)

        _for_blocks(d, one)

    oshape = jax.ShapeDtypeStruct((t, 2 * AT_L), F32)
    ospec = _blk_spec(_pcol, d)
    return pl.pallas_call(
        body, out_shape=(oshape, oshape), grid=(2, t // (AT_B * d * _sub(d))),
        in_specs=[_blk_spec(_qcol(g), d)] + _win_specs(_kcol(g), t, d) + _win_specs(_vcol(g), t, d),
        out_specs=(ospec, ospec), name=f"attn_fwd_{g}", compiler_params=_params(("parallel", "parallel")))(
            u, u, u, u, u, u, u)


def _attn_dq(u, du, do, lse, e, g):
    t = u.shape[0]
    d = DILATIONS[g]
    ln = t // d

    def body(q_ref, kp, kc, kn, vp, vc, vn, do_ref, l_ref, e_ref, du_in, dq_ref, dq_scr):
        del du_in
        p_id = pl.program_id(0)
        i = pl.program_id(1)
        lane = _iota((AT_B, AT_L), 1)
        biases = [_attn_bias(i * _sub(d) + s, ln, d, g, p_id) for s in range(_sub(d))]

        def one(r, s):
            q = _rows(q_ref, r, s, d) * SCALE
            kw = _win(kp, kc, kn, r, s, d).astype(BF16)
            vw = _win(vp, vc, vn, r, s, d).astype(BF16)
            do_ = _rows(do_ref, r, s, d)
            lv = _rows(l_ref, r, s, d)
            ev = _rows(e_ref, r, s, d)
            dq = jnp.zeros((AT_B, AT_L), F32)
            for hh in range(2):
                hm = (lane // AH) == hh
                qm = jnp.where(hm, q, 0.0).astype(BF16)
                sc = _dot_nt(qm, kw) + biases[s][hh]
                lcol = jnp.broadcast_to(lv[:, AH * hh:AH * hh + 1], (AT_B, AT_W))
                ecol = jnp.broadcast_to(ev[:, AH * hh:AH * hh + 1], (AT_B, AT_W))
                pr = jnp.exp(sc - lcol)
                dom = jnp.where(hm, do_, 0.0).astype(BF16)
                ds = pr * (_dot_nt(dom, vw) + ecol)
                dqh = jnp.dot(ds.astype(BF16), kw, preferred_element_type=F32) * SCALE
                dq = jnp.where(hm, dqh, dq)
            _put_rows(dq_scr, r, s, d, dq)

        _for_blocks(d, one)
        dq_ref[...] = dq_scr[...].astype(dq_ref.dtype)

    rspec = _blk_spec(_pcol, d)
    return pl.pallas_call(
        body, out_shape=jax.ShapeDtypeStruct(du.shape, du.dtype), grid=(2, t // (AT_B * d * _sub(d))),
        in_specs=[_blk_spec(_qcol(g), d)] + _win_specs(_kcol(g), t, d) + _win_specs(_vcol(g), t, d)
        + [rspec, rspec, rspec, pl.BlockSpec(memory_space=pl.ANY)],
        out_specs=_blk_spec(_qcol(g), d), input_output_aliases={10: 0},
        scratch_shapes=[pltpu.VMEM((AT_B * d * _sub(d), AT_L), F32)],
        name=f"attn_dq_{g}", compiler_params=_params(("parallel", "parallel")))(
            u, u, u, u, u, u, u, do, lse, e, du)


def _attn_dkv(u, du, do, lse, e, g):
    t = u.shape[0]
    d = DILATIONS[g]
    ln = t // d

    def body(k_ref, v_ref, qp, qc, qn, dp_, dc_, dn_, lp, lc, ln_, ep, ec, en, du_in, dkv_ref, dk_scr, dv_scr):
        del du_in
        p_id = pl.program_id(0)
        jb = pl.program_id(1)
        lane = _iota((AT_B, AT_L), 1)
        biases = [_attn_bias(jb * _sub(d) + s, ln, d, g, p_id) for s in range(_sub(d))]

        def one(r, s):
            k = _rows(k_ref, r, s, d) * SCALE
            v = _rows(v_ref, r, s, d)
            qw = _win(qp, qc, qn, r, s, d).astype(BF16)
            dow = _win(dp_, dc_, dn_, r, s, d).astype(BF16)
            lt = _win(lp, lc, ln_, r, s, d).T
            et = _win(ep, ec, en, r, s, d).T
            dk = jnp.zeros((AT_B, AT_L), F32)
            dv = jnp.zeros((AT_B, AT_L), F32)
            for hh in range(2):
                hm = (lane // AH) == hh
                km = jnp.where(hm, k, 0.0).astype(BF16)
                st = _dot_nt(km, qw) + biases[s][hh]
                pt = jnp.exp(st - lt[AH * hh:AH * hh + 1, :])
                dvh = jnp.dot(pt.astype(BF16), dow, preferred_element_type=F32)
                vm = jnp.where(hm, v, 0.0).astype(BF16)
                dst = pt * (_dot_nt(vm, dow) + et[AH * hh:AH * hh + 1, :])
                dkh = jnp.dot(dst.astype(BF16), qw, preferred_element_type=F32) * SCALE
                dk = jnp.where(hm, dkh, dk)
                dv = jnp.where(hm, dvh, dv)
            _put_rows(dk_scr, r, s, d, dk)
            _put_rows(dv_scr, r, s, d, dv)

        _for_blocks(d, one)
        dkv_ref[:, 0:AT_L] = dk_scr[...].astype(dkv_ref.dtype)
        dkv_ref[:, AT_L:2 * AT_L] = dv_scr[...].astype(dkv_ref.dtype)

    return pl.pallas_call(
        body, out_shape=jax.ShapeDtypeStruct(du.shape, du.dtype), grid=(2, t // (AT_B * d * _sub(d))),
        in_specs=[_blk_spec(_kcol(g), d), _blk_spec(_vcol(g), d)]
        + _win_specs(_qcol(g), t, d) + _win_specs(_pcol, t, d) + _win_specs(_pcol, t, d) + _win_specs(_pcol, t, d)
        + [pl.BlockSpec(memory_space=pl.ANY)],
        out_specs=pl.BlockSpec((AT_B * d * _sub(d), 2 * AT_L), lambda p, i: (i, OKV // (2 * AT_L) + 2 * g + p)),
        input_output_aliases={14: 0},
        scratch_shapes=[pltpu.VMEM((AT_B * d * _sub(d), AT_L), F32), pltpu.VMEM((AT_B * d * _sub(d), AT_L), F32)],
        name=f"attn_dkv_{g}", compiler_params=_params(("parallel", "parallel")))(
            u, u, u, u, u, do, do, do, lse, lse, lse, e, e, e, du)


CMB_TM = 1024


def _combine_weights(l0, l1, l2):
    m = jnp.maximum(jnp.maximum(l0, l1), l2)
    e0, e1, e2 = jnp.exp(l0 - m), jnp.exp(l1 - m), jnp.exp(l2 - m)
    inv = 1.0 / (e0 + e1 + e2)
    return e0 * inv, e1 * inv, e2 * inv


def _combine_fwd(os_, ls_):
    t = os_[0].shape[0]
    tm = CMB_TM

    def body(o0, o1, o2, l0, l1, l2, a_ref):
        w0, w1, w2 = _combine_weights(l0[...], l1[...], l2[...])
        a_ref[...] = w0 * o0[...] + w1 * o1[...] + w2 * o2[...]

    blk = pl.BlockSpec((tm, 2 * AT_L), lambda i: (i, 0))
    return pl.pallas_call(
        body, out_shape=jax.ShapeDtypeStruct((t, 2 * AT_L), F32), grid=(t // tm,), in_specs=[blk] * 6, out_specs=blk,
        name="combine_fwd", compiler_params=_params(("parallel",)))(*os_, *ls_)


def _combine_bwd(datt, os_, ls_):
    t = datt.shape[0]
    tm = CMB_TM

    def body(da_ref, o0, o1, o2, l0, l1, l2, d0, d1, d2, e0, e1, e2):
        w = _combine_weights(l0[...], l1[...], l2[...])
        da = da_ref[...]
        att = w[0] * o0[...] + w[1] * o1[...] + w[2] * o2[...]
        r = _iota((2 * AT_L, 2 * AT_L), 0) // AH
        c = _iota((2 * AT_L, 2 * AT_L), 1) // AH
        hs = _dot01(da * att, (r == c).astype(BF16))
        for wg, dref, eref in zip(w, (d0, d1, d2), (e0, e1, e2)):
            dref[...] = wg * da
            eref[...] = -wg * hs

    blk = pl.BlockSpec((tm, 2 * AT_L), lambda i: (i, 0))
    shp = jax.ShapeDtypeStruct((t, 2 * AT_L), F32)
    outs = pl.pallas_call(
        body, out_shape=(shp,) * 6, grid=(t // tm,), in_specs=[blk] * 7, out_specs=(blk,) * 6,
        name="combine_bwd", compiler_params=_params(("parallel",)))(datt, *os_, *ls_)
    return outs[0:3], outs[3:6]


def _combine_proj(os_, ls_, w_pa):
    t = os_[0].shape[0]
    tm = ROW_TM
    nsh, _, ws = w_pa.shape

    def body(o0, o1, o2, l0, l1, l2, w_ref, a_ref, y_ref):
        w0, w1, w2 = _combine_weights(l0[...], l1[...], l2[...])
        att = w0 * o0[...] + w1 * o1[...] + w2 * o2[...]
        a_ref[...] = att
        ab = att.astype(BF16)
        for sh in range(nsh):
            y_ref[:, ws * sh:ws * (sh + 1)] = jnp.dot(ab, w_ref[sh], preferred_element_type=F32)

    blk = pl.BlockSpec((tm, 2 * AT_L), lambda i: (i, 0))
    return pl.pallas_call(
        body, out_shape=(jax.ShapeDtypeStruct((t, 2 * AT_L), F32), jax.ShapeDtypeStruct((t, nsh * ws), F32)),
        grid=(t // tm,), in_specs=[blk] * 6 + [pl.BlockSpec(w_pa.shape, lambda i: (0, 0, 0))],
        out_specs=(blk, pl.BlockSpec((tm, nsh * ws), lambda i: (i, 0))),
        name="combine_proj", compiler_params=_params(("parallel",)))(*os_, *ls_, w_pa)


def _d_att_combine_bwd(dy_att, w_pa, os_, ls_):
    t = dy_att.shape[0]
    tm = ROW_TM
    nsh, _, ws = w_pa.shape

    def body(dy_ref, w_ref, o0, o1, o2, l0, l1, l2, d0, d1, d2, e0, e1, e2):
        da = jnp.zeros((tm, 2 * AT_L), F32)
        for sh in range(nsh):
            da = da + _dot_nt(dy_ref[:, ws * sh:ws * (sh + 1)], w_ref[sh])
        w = _combine_weights(l0[...], l1[...], l2[...])
        att = w[0] * o0[...] + w[1] * o1[...] + w[2] * o2[...]
        r = _iota((2 * AT_L, 2 * AT_L), 0) // AH
        c = _iota((2 * AT_L, 2 * AT_L), 1) // AH
        hs = _dot01(da * att, (r == c).astype(BF16))
        for wg, dref, eref in zip(w, (d0, d1, d2), (e0, e1, e2)):
            dref[...] = wg * da
            eref[...] = -wg * hs

    blk = pl.BlockSpec((tm, 2 * AT_L), lambda i: (i, 0))
    shp = jax.ShapeDtypeStruct((t, 2 * AT_L), F32)
    outs = pl.pallas_call(
        body, out_shape=(shp,) * 6, grid=(t // tm,),
        in_specs=[pl.BlockSpec((tm, nsh * ws), lambda i: (i, 0)), pl.BlockSpec(w_pa.shape, lambda i: (0, 0, 0))] + [blk] * 6,
        out_specs=(blk,) * 6, name="d_att_combine_bwd", compiler_params=_params(("parallel",)))(dy_att, w_pa, *os_, *ls_)
    return outs[0:3], outs[3:6]


ROW_TM = 512


def _mix_fwd(y_ssd, y_att, u, bg_row):
    t = y_ssd.shape[0]
    tm = ROW_TM

    def body(ys_ref, ya_ref, g0_ref, g1_ref, b0_ref, b1_ref, o_ref):
        g0 = _sigmoid(g0_ref[...] + b0_ref[...])
        g1 = _sigmoid(g1_ref[...] + b1_ref[...])
        o_ref[...] = (g0 * ys_ref[...] + g1 * ya_ref[...]).astype(BF16)

    blk = pl.BlockSpec((tm, D), lambda i: (i, 0))
    return pl.pallas_call(
        body, out_shape=jax.ShapeDtypeStruct((t, D), BF16), grid=(t // tm,),
        in_specs=[blk, blk, pl.BlockSpec((tm, D), lambda i: (i, OGATE // D)), pl.BlockSpec((tm, D), lambda i: (i, OGATE // D + 1)),
                  pl.BlockSpec((1, D), lambda i: (0, 0)), pl.BlockSpec((1, D), lambda i: (0, 1))],
        out_specs=blk, name="mix_fwd", compiler_params=_params(("parallel",)))(y_ssd, y_att, u, u, bg_row, bg_row)


def _mix_bwd(dmixin, y_ssd, y_att, u, bg_row):
    t = y_ssd.shape[0]
    tm = ROW_TM

    def body(dm_ref, ys_ref, ya_ref, g0_ref, g1_ref, b0_ref, b1_ref, dys_ref, dya_ref, du_ref, db_ref):
        i = pl.program_id(0)
        g0 = _sigmoid(g0_ref[...] + b0_ref[...])
        g1 = _sigmoid(g1_ref[...] + b1_ref[...])
        dm = dm_ref[...]
        dys_ref[...] = (dm * g0).astype(BF16)
        dya_ref[...] = (dm * g1).astype(BF16)
        dl0 = dm * ys_ref[...] * g0 * (1.0 - g0)
        dl1 = dm * ya_ref[...] * g1 * (1.0 - g1)
        du_ref[:, 0:D] = dl0.astype(BF16)
        du_ref[:, D:2 * D] = dl1.astype(BF16)
        part = jnp.concatenate([jnp.broadcast_to(jnp.sum(dl0, axis=0, keepdims=True), (8, D)),
                                jnp.broadcast_to(jnp.sum(dl1, axis=0, keepdims=True), (8, D))], axis=1)

        @pl.when(i == 0)
        def _():
            db_ref[...] = part

        @pl.when(i > 0)
        def _():
            db_ref[...] += part

    blk = pl.BlockSpec((tm, D), lambda i: (i, 0))
    return pl.pallas_call(
        body,
        out_shape=(jax.ShapeDtypeStruct((t, D), BF16), jax.ShapeDtypeStruct((t, D), BF16),
                   jax.ShapeDtypeStruct((t, UW), BF16), jax.ShapeDtypeStruct((8, 2 * D), F32)),
        grid=(t // tm,),
        in_specs=[blk, blk, blk, pl.BlockSpec((tm, D), lambda i: (i, OGATE // D)), pl.BlockSpec((tm, D), lambda i: (i, OGATE // D + 1)),
                  pl.BlockSpec((1, D), lambda i: (0, 0)), pl.BlockSpec((1, D), lambda i: (0, 1))],
        out_specs=(blk, blk, pl.BlockSpec((tm, 2 * D), lambda i: (i, OGATE // (2 * D))),
                   pl.BlockSpec((8, 2 * D), lambda i: (0, 0))),
        name="mix_bwd", compiler_params=_params(("arbitrary",)))(dmixin, y_ssd, y_att, u, u, bg_row, bg_row)


def _ln(x, g, b):
    mu = jnp.mean(x, axis=1, keepdims=True)
    xc = x - mu
    var = jnp.mean(xc * xc, axis=1, keepdims=True)
    rstd = lax.rsqrt(var + NORM_EPS)
    xhat = xc * rstd
    return xhat * g + b, xhat, rstd


def _ln_back(dh, xhat, rstd, g):
    dxh = dh * g
    m1 = jnp.mean(dxh, axis=1, keepdims=True)
    m2 = jnp.mean(dxh * xhat, axis=1, keepdims=True)
    return rstd * (dxh - m1 - xhat * m2)


def _ln1_fwd(x, mix, g_row, b_row):
    t = x.shape[0]
    tm = ROW_TM

    def body(x_ref, m_ref, g_ref, b_ref, pre_ref, h_ref):
        pre = ALPHA * x_ref[...] + m_ref[...]
        pre_ref[...] = pre
        h, _, _ = _ln(pre, g_ref[...], b_ref[...])
        h_ref[...] = h.astype(BF16)

    blk = pl.BlockSpec((tm, D), lambda i: (i, 0))
    row = pl.BlockSpec((1, D), lambda i: (0, 0))
    return pl.pallas_call(
        body, out_shape=(jax.ShapeDtypeStruct((t, D), F32), jax.ShapeDtypeStruct((t, D), BF16)), grid=(t // tm,),
        in_specs=[blk, blk, row, row], out_specs=(blk, blk),
        name="ln1_fwd", compiler_params=_params(("parallel",)))(x, mix, g_row, b_row)


def _ln1_bwd(dh, pre, g_row, b_row):
    t = dh.shape[0]
    tm = ROW_TM

    def body(dh_ref, pre_ref, g_ref, b_ref, dpre_ref, acc_ref):
        i = pl.program_id(0)
        dh_ = dh_ref[...]
        _, xhat, rstd = _ln(pre_ref[...], g_ref[...], b_ref[...])
        dpre_ref[...] = _ln_back(dh_, xhat, rstd, g_ref[...])
        part = jnp.concatenate([jnp.sum(dh_ * xhat, axis=0, keepdims=True), jnp.sum(dh_, axis=0, keepdims=True),
                                jnp.zeros((6, D), F32)], axis=0)

        @pl.when(i == 0)
        def _():
            acc_ref[...] = part

        @pl.when(i > 0)
        def _():
            acc_ref[...] += part

    blk = pl.BlockSpec((tm, D), lambda i: (i, 0))
    row = pl.BlockSpec((1, D), lambda i: (0, 0))
    return pl.pallas_call(
        body, out_shape=(jax.ShapeDtypeStruct((t, D), F32), jax.ShapeDtypeStruct((8, D), F32)), grid=(t // tm,),
        in_specs=[blk, blk, row, row], out_specs=(blk, pl.BlockSpec((8, D), lambda i: (0, 0))),
        name="ln1_bwd", compiler_params=_params(("arbitrary",)))(dh, pre, g_row, b_row)


def _ln2_loss(pre1, f, tgt, g1_row, b1_row, g2_row, b2_row):
    t = pre1.shape[0]
    tm = ROW_TM

    def body(p1_ref, f_ref, t_ref, g1_ref, b1_ref, g2_ref, b2_ref, dpre_ref, acc_ref):
        i = pl.program_id(0)
        h1, _, _ = _ln(p1_ref[...], g1_ref[...], b1_ref[...])
        pre2 = ALPHA * h1 + f_ref[...]
        h2, xhat, rstd = _ln(pre2, g2_ref[...], b2_ref[...])
        err = h2 - t_ref[...]
        dh = err * (1.0 / D)
        dpre_ref[...] = _ln_back(dh, xhat, rstd, g2_ref[...])
        loss = jnp.sum(jnp.sum(err * err, axis=1, keepdims=True), axis=0, keepdims=True) * (0.5 / D)
        part = jnp.concatenate([jnp.sum(dh * xhat, axis=0, keepdims=True), jnp.sum(dh, axis=0, keepdims=True),
                                jnp.broadcast_to(loss, (1, D)), jnp.zeros((5, D), F32)], axis=0)

        @pl.when(i == 0)
        def _():
            acc_ref[...] = part

        @pl.when(i > 0)
        def _():
            acc_ref[...] += part

    blk = pl.BlockSpec((tm, D), lambda i: (i, 0))
    row = pl.BlockSpec((1, D), lambda i: (0, 0))
    return pl.pallas_call(
        body, out_shape=(jax.ShapeDtypeStruct((t, D), F32), jax.ShapeDtypeStruct((8, D), F32)), grid=(t // tm,),
        in_specs=[blk, blk, blk, row, row, row, row], out_specs=(blk, pl.BlockSpec((8, D), lambda i: (0, 0))),
        name="ln2_loss", compiler_params=_params(("arbitrary",)))(pre1, f, tgt, g1_row, b1_row, g2_row, b2_row)


def _mlp_up(h1, w_up):
    t = h1.shape[0]
    tm, tn = 2 * ROW_TM, D

    def body(a_ref, b_ref, up_ref, act_ref):
        up = jnp.dot(a_ref[...], b_ref[...], preferred_element_type=F32)
        up_ref[...] = up.astype(BF16)
        r = jnp.maximum(up, 0.0)
        act_ref[...] = (r * r).astype(BF16)

    blk = pl.BlockSpec((tm, tn), lambda j, i: (i, j))
    return pl.pallas_call(
        body, out_shape=(jax.ShapeDtypeStruct((t, DFF), BF16), jax.ShapeDtypeStruct((t, DFF), BF16)),
        grid=(DFF // tn, t // tm),
        in_specs=[pl.BlockSpec((tm, D), lambda j, i: (i, 0)), pl.BlockSpec((None, D, tn), lambda j, i: (j, 0, 0))],
        out_specs=(blk, blk), name="mlp_up", compiler_params=_params(("parallel", "parallel")))(h1, w_up)


def _d_up(dpre2, w_down, up):
    t = up.shape[0]
    tm, tk = 2 * ROW_TM, D

    def body(a_ref, b_ref, u_ref, o_ref):
        dact = _dot_nt(a_ref[...], b_ref[...])
        o_ref[...] = (dact * 2.0 * jnp.maximum(u_ref[...].astype(F32), 0.0)).astype(BF16)

    blk = pl.BlockSpec((tm, tk), lambda j, i: (i, j))
    return pl.pallas_call(
        body, out_shape=jax.ShapeDtypeStruct((t, DFF), BF16), grid=(DFF // tk, t // tm),
        in_specs=[pl.BlockSpec((tm, D), lambda j, i: (i, 0)), pl.BlockSpec((tk, D), lambda j, i: (j, 0)), blk],
        out_specs=blk, name="d_up", compiler_params=_params(("parallel", "parallel")))(dpre2, w_down, up)


def _dt_bwd(du, ddt):
    t = ddt.shape[0]
    tm = 1024

    def body(f_ref, du_in, o_ref):
        del du_in
        o_ref[:, 0:128] = f_ref[...].astype(o_ref.dtype)
        o_ref[:, 128:256] = jnp.zeros((tm, 128), o_ref.dtype)

    blk = pl.BlockSpec((tm, 128), lambda i: (i, 0))
    return pl.pallas_call(
        body, out_shape=jax.ShapeDtypeStruct(du.shape, du.dtype), grid=(t // tm,),
        in_specs=[blk, pl.BlockSpec(memory_space=pl.ANY)],
        out_specs=pl.BlockSpec((tm, 256), lambda i: (i, ODT // 256)), input_output_aliases={1: 0},
        name="dt_bwd", compiler_params=_params(("parallel",)))(ddt, du)


def _mix_out_ln1(y_ssd, y_att, u, bg_row, x, w_out, g_row, b_row):
    t = x.shape[0]
    tm = ROW_TM

    def body(ys_ref, ya_ref, g0_ref, g1_ref, b0_ref, b1_ref, x_ref, w_ref, g_ref, b_ref, mixin_ref, pre_ref, h_ref):
        g0 = _sigmoid(g0_ref[...] + b0_ref[...])
        g1 = _sigmoid(g1_ref[...] + b1_ref[...])
        mixin = (g0 * ys_ref[...] + g1 * ya_ref[...]).astype(BF16)
        mixin_ref[...] = mixin
        pre = ALPHA * x_ref[...] + jnp.dot(mixin, w_ref[...], preferred_element_type=F32)
        pre_ref[...] = pre
        h, _, _ = _ln(pre, g_ref[...], b_ref[...])
        h_ref[...] = h.astype(BF16)

    blk = pl.BlockSpec((tm, D), lambda i: (i, 0))
    row = pl.BlockSpec((1, D), lambda i: (0, 0))
    return pl.pallas_call(
        body,
        out_shape=(jax.ShapeDtypeStruct((t, D), BF16), jax.ShapeDtypeStruct((t, D), F32), jax.ShapeDtypeStruct((t, D), BF16)),
        grid=(t // tm,),
        in_specs=[blk, blk, pl.BlockSpec((tm, D), lambda i: (i, OGATE // D)), pl.BlockSpec((tm, D), lambda i: (i, OGATE // D + 1)),
                  row, pl.BlockSpec((1, D), lambda i: (0, 1)), blk, pl.BlockSpec((D, D), lambda i: (0, 0)), row, row],
        out_specs=(blk, blk, blk), name="mix_out_ln1", compiler_params=_params(("parallel",)))(
            y_ssd, y_att, u, u, bg_row, bg_row, x, w_out, g_row, b_row)


def _mlp_down_ln2_loss(act, w_down, pre1, tgt, g1_row, b1_row, g2_row, b2_row):
    t = pre1.shape[0]
    tm = ROW_TM

    def body(a_ref, w_ref, p1_ref, t_ref, g1_ref, b1_ref, g2_ref, b2_ref, dpre_ref, dpreb_ref, acc_ref):
        i = pl.program_id(0)
        f = jnp.dot(a_ref[...], w_ref[...], preferred_element_type=F32)
        h1, _, _ = _ln(p1_ref[...], g1_ref[...], b1_ref[...])
        pre2 = ALPHA * h1 + f
        h2, xhat, rstd = _ln(pre2, g2_ref[...], b2_ref[...])
        err = h2 - t_ref[...]
        dh = err * (1.0 / D)
        dpre = _ln_back(dh, xhat, rstd, g2_ref[...])
        dpre_ref[...] = dpre
        dpreb_ref[...] = dpre.astype(BF16)
        loss = jnp.sum(jnp.sum(err * err, axis=1, keepdims=True), axis=0, keepdims=True) * (0.5 / D)
        part = jnp.concatenate([jnp.sum(dh * xhat, axis=0, keepdims=True), jnp.sum(dh, axis=0, keepdims=True),
                                jnp.broadcast_to(loss, (1, D)), jnp.zeros((5, D), F32)], axis=0)

        @pl.when(i == 0)
        def _():
            acc_ref[...] = part

        @pl.when(i > 0)
        def _():
            acc_ref[...] += part

    blk = pl.BlockSpec((tm, D), lambda i: (i, 0))
    row = pl.BlockSpec((1, D), lambda i: (0, 0))
    return pl.pallas_call(
        body,
        out_shape=(jax.ShapeDtypeStruct((t, D), F32), jax.ShapeDtypeStruct((t, D), BF16), jax.ShapeDtypeStruct((8, D), F32)),
        grid=(t // tm,),
        in_specs=[pl.BlockSpec((tm, DFF), lambda i: (i, 0)), pl.BlockSpec((DFF, D), lambda i: (0, 0)), blk, blk, row, row, row, row],
        out_specs=(blk, blk, pl.BlockSpec((8, D), lambda i: (0, 0))),
        name="mlp_down_ln2_loss", compiler_params=_params(("arbitrary",)))(act, w_down, pre1, tgt, g1_row, b1_row, g2_row, b2_row)


def _d_h1_ln1_bwd(dup, w_up, dpre2, pre1, g_row, b_row):
    t = dup.shape[0]
    tm = ROW_TM
    nsh = w_up.shape[0]

    def body(a_ref, w_ref, add_ref, pre_ref, g_ref, b_ref, dpre_ref, acc_ref):
        i = pl.program_id(0)
        dh_ = ALPHA * add_ref[...]
        for sh in range(nsh):
            dh_ = dh_ + _dot_nt(a_ref[:, D * sh:D * (sh + 1)], w_ref[sh])
        _, xhat, rstd = _ln(pre_ref[...], g_ref[...], b_ref[...])
        dpre_ref[...] = _ln_back(dh_, xhat, rstd, g_ref[...])
        rows = jnp.concatenate([jnp.sum(dh_ * xhat, axis=0, keepdims=True), jnp.sum(dh_, axis=0, keepdims=True),
                                jnp.zeros((6, D), F32)], axis=0)

        @pl.when(i == 0)
        def _():
            acc_ref[...] = rows

        @pl.when(i > 0)
        def _():
            acc_ref[...] += rows

    blk = pl.BlockSpec((tm, D), lambda i: (i, 0))
    row = pl.BlockSpec((1, D), lambda i: (0, 0))
    return pl.pallas_call(
        body, out_shape=(jax.ShapeDtypeStruct((t, D), F32), jax.ShapeDtypeStruct((8, D), F32)),
        grid=(t // tm,),
        in_specs=[pl.BlockSpec((tm, nsh * D), lambda i: (i, 0)), pl.BlockSpec(w_up.shape, lambda i: (0, 0, 0)),
                  blk, blk, row, row],
        out_specs=(blk, pl.BlockSpec((8, D), lambda i: (0, 0))),
        name="d_h1_ln1_bwd", compiler_params=_params(("arbitrary",)))(dup, w_up, dpre2, pre1, g_row, b_row)


def _d_mixin_mix_bwd(dpre1, w_out, y_ssd, y_att, u, bg_row):
    t = y_ssd.shape[0]
    tm = ROW_TM

    def body(a_ref, w_ref, ys_ref, ya_ref, g0_ref, g1_ref, b0_ref, b1_ref, dys_ref, dya_ref, du_ref, db_ref):
        i = pl.program_id(0)
        dm = _dot_nt(a_ref[...].astype(BF16), w_ref[...])
        g0 = _sigmoid(g0_ref[...] + b0_ref[...])
        g1 = _sigmoid(g1_ref[...] + b1_ref[...])
        dys_ref[...] = (dm * g0).astype(BF16)
        dya_ref[...] = (dm * g1).astype(BF16)
        dl0 = dm * ys_ref[...] * g0 * (1.0 - g0)
        dl1 = dm * ya_ref[...] * g1 * (1.0 - g1)
        du_ref[:, 0:D] = dl0.astype(BF16)
        du_ref[:, D:2 * D] = dl1.astype(BF16)
        part = jnp.concatenate([jnp.broadcast_to(jnp.sum(dl0, axis=0, keepdims=True), (8, D)),
                                jnp.broadcast_to(jnp.sum(dl1, axis=0, keepdims=True), (8, D))], axis=1)

        @pl.when(i == 0)
        def _():
            db_ref[...] = part

        @pl.when(i > 0)
        def _():
            db_ref[...] += part

    blk = pl.BlockSpec((tm, D), lambda i: (i, 0))
    return pl.pallas_call(
        body,
        out_shape=(jax.ShapeDtypeStruct((t, D), BF16), jax.ShapeDtypeStruct((t, D), BF16),
                   jax.ShapeDtypeStruct((t, UW), BF16), jax.ShapeDtypeStruct((8, 2 * D), F32)),
        grid=(t // tm,),
        in_specs=[blk, pl.BlockSpec((D, D), lambda i: (0, 0)), blk, blk,
                  pl.BlockSpec((tm, D), lambda i: (i, OGATE // D)), pl.BlockSpec((tm, D), lambda i: (i, OGATE // D + 1)),
                  pl.BlockSpec((1, D), lambda i: (0, 0)), pl.BlockSpec((1, D), lambda i: (0, 1))],
        out_specs=(blk, blk, pl.BlockSpec((tm, 2 * D), lambda i: (i, OGATE // (2 * D))),
                   pl.BlockSpec((8, 2 * D), lambda i: (0, 0))),
        name="d_mixin_mix_bwd", compiler_params=_params(("arbitrary",)))(dpre1, w_out, y_ssd, y_att, u, u, bg_row, bg_row)


def _adamw(w, g, m, v, name):
    r, c = w.shape
    tr, tc = r, c
    for cand in (256, 128, 64, 32, 16, 8):
        if r % cand == 0 and cand * c * 4 <= 2 ** 21:
            tr = cand
            break
    if tr < 64 and c % 256 == 0:
        tr, tc = r, 256
    bc1 = 1.0 / (1.0 - ADAM_B1 ** ADAM_STEP)
    bc2 = 1.0 / (1.0 - ADAM_B2 ** ADAM_STEP)

    def body(w_ref, g_ref, m_ref, v_ref, d_ref, nm_ref, nv_ref):
        gg = g_ref[...]
        nm = ADAM_B1 * m_ref[...] + (1.0 - ADAM_B1) * gg
        nv = ADAM_B2 * v_ref[...] + (1.0 - ADAM_B2) * (gg * gg)
        nm_ref[...] = nm
        nv_ref[...] = nv
        d_ref[...] = -ADAM_LR * ((nm * bc1) / (jnp.sqrt(nv * bc2) + ADAM_EPS) + ADAM_WD * w_ref[...])

    blk = pl.BlockSpec((tr, tc), lambda i, j: (i, j))
    shp = jax.ShapeDtypeStruct((r, c), F32)
    return pl.pallas_call(body, out_shape=(shp, shp, shp), grid=(r // tr, c // tc), in_specs=[blk] * 4,
                          out_specs=(blk,) * 3, name=name, compiler_params=_params(("parallel", "parallel")))(w, g, m, v)


def _perm_cols(w):
    z, xbc, dt = w[:, 0:2048], w[:, 2048:5120], w[:, 5120:5184]
    q, k, v, gate = w[:, 5184:5952], w[:, 5952:6720], w[:, 6720:7488], w[:, 7488:9536]
    kv = []
    for g in range(3):
        for p in range(2):
            lo = 256 * g + 128 * p
            kv += [k[:, lo:lo + 128], v[:, lo:lo + 128]]
    pad = jnp.zeros((w.shape[0], UW - IN_COLS), w.dtype)
    return jnp.concatenate([z, gate, xbc] + kv + [q, dt, pad], axis=1)


def _unperm_cols(wp):
    z, gate, xbc = wp[:, OZ:OZ + 2048], wp[:, OGATE:OGATE + 2048], wp[:, OXBC:OXBC + CONVD]
    q, dt = wp[:, OQ:OQ + 768], wp[:, ODT:ODT + 64]
    ks, vs = [], []
    for g in range(3):
        for p in range(2):
            lo = OKV + 128 * (4 * g + 2 * p)
            ks.append(wp[:, lo:lo + 128])
            vs.append(wp[:, lo + 128:lo + 256])
    return jnp.concatenate([z, xbc, dt, q] + ks + vs + [gate], axis=1)


def _segments():
    segs = [(0, 2048), (7488, 9536), (2048, 5120)]
    for g in range(3):
        for p in range(2):
            lo = 256 * g + 128 * p
            segs += [(5952 + lo, 5952 + lo + 128), (6720 + lo, 6720 + lo + 128)]
    segs += [(5184, 5952), (5120, 5184)]
    out, pos = [], 0
    for a, b in segs:
        out.append((a, b, pos))
        pos += b - a
    return out


SHARD_COLS = IN_COLS // 4


def _perm_from_shards(w_shards):
    pieces = []
    for a, b, _ in _segments():
        while a < b:
            s = a // SHARD_COLS
            e = min(b, (s + 1) * SHARD_COLS)
            pieces.append(w_shards[s][:, a - s * SHARD_COLS:e - s * SHARD_COLS])
            a = e
    pieces.append(jnp.zeros((w_shards.shape[1], UW - IN_COLS), w_shards.dtype))
    return jnp.concatenate(pieces, axis=1)


def _shards_from_perm(wp):
    segs = sorted(_segments())
    shards = []
    for s in range(4):
        lo, hi = s * SHARD_COLS, (s + 1) * SHARD_COLS
        pieces = []
        for a, b, pos in segs:
            x, y = max(a, lo), min(b, hi)
            if x < y:
                pieces.append(wp[:, pos + x - a:pos + y - a])
        shards.append(jnp.concatenate(pieces, axis=1))
    return jnp.stack(shards)


def _lanes128(*vecs):
    v = jnp.concatenate([a.reshape(-1) for a in vecs])
    return jnp.pad(v, (0, 128 - v.shape[0])).reshape(1, 128)


EARLY = ("w_proj_ssd", "w_proj_attn", "w_out", "w_up", "w_down")


def _weights_of(gathered):
    g_ps, g_pa, g_o, g_up, g_dn = gathered
    return {"w_proj_ssd": g_ps.reshape(DI, D), "w_proj_attn": g_pa, "w_out": g_o.reshape(D, D), "w_up": g_up,
            "w_down": g_dn.reshape(DFF, D)}


def _local_grads(x, tgt, wts, sm, rs_idx=None):
    row = lambda a: a.reshape(1, -1)
    bg_row, cb_row = row(sm["b_gate"]), row(sm["conv_b"])
    par = jnp.concatenate([_lanes128(sm["dt_bias_f"], sm["dt_bias_b"]), _lanes128(sm["a_log_f"], sm["a_log_b"]),
                           jnp.zeros((6, 128), F32)], axis=0)
    dsk_row = row(jnp.repeat(sm["d_skip"], HP))
    nw_row = row(sm["ssd_norm_w"])
    g1, b1, g2, b2 = row(sm["ln1_g"]), row(sm["ln1_b"]), row(sm["ln2_g"]), row(sm["ln2_b"])

    xb = x.astype(BF16)
    u, gathered = _in_proj(xb, wts["w_in_p"], side=_gather_side(wts["pending"]) if "pending" in wts else None)
    if gathered:
        wts = {**wts, **_weights_of(gathered)}
    xbc = _conv_fwd(u, sm["conv_w"], cb_row)
    y_f, st_f = _ssd_fwd(xbc, u, par, rev=False)
    y_fb, st_b = _ssd_fwd(xbc, u, par, y_f, rev=True)
    s_out = _gatenorm_fwd(y_fb, xbc, u, dsk_row, nw_row)
    y_ssd = _mm_nn(s_out, wts["w_proj_ssd"], tm=512, tn=1024, name="proj_ssd")
    att_o, att_l = [], []
    for g in range(3):
        o, l = _attn_fwd(u, g)
        att_o.append(o)
        att_l.append(l)
    att, y_att = _combine_proj(att_o, att_l, wts["w_proj_attn"])
    mixin, pre1, h1 = _mix_out_ln1(y_ssd, y_att, u, bg_row, x, wts["w_out"], g1, b1)
    up, act = _mlp_up(h1, wts["w_up"])
    dpre2, dpre2_b, acc2 = _mlp_down_ln2_loss(act, wts["w_down"], pre1, tgt, g1, b1, g2, b2)

    dw_down = _mm_tn(act, dpre2_b, tka=1024, tn=1024, tt=1024, name="dw_down")
    dup = _d_up(dpre2_b, wts["w_down"], up)
    dw_up = _mm_tn(h1, dup, tka=1024, tn=1024, tt=1024, name="dw_up", out_shards=4)
    dpre1, acc1 = _d_h1_ln1_bwd(dup, wts["w_up"], dpre2, pre1, g1, b1)
    dw_out = _mm_tn(mixin, dpre1, tka=1024, tn=1024, tt=1024, name="dw_out")
    dy_ssd, dy_att, du, dbg = _d_mixin_mix_bwd(dpre1, wts["w_out"], y_ssd, y_att, u, bg_row)
    dw_proj_ssd = _mm_tn(s_out, dy_ssd, tka=1024, tn=1024, tt=1024, name="dw_proj_ssd")
    ds_out = _mm_nt(dy_ssd, wts["w_proj_ssd"], tm=512, tk=1024, tc=1024, name="d_s_out")
    dw_proj_attn = _mm_tn(att, dy_att, tka=256, tn=256, tt=1024, name="dw_proj_attn", out_shards=4)
    do_g, e_g = _d_att_combine_bwd(dy_att, wts["w_proj_attn"], att_o, att_l)
    for g in range(3):
        du = _attn_dq(u, du, do_g[g], att_l[g], e_g[g], g)
        du = _attn_dkv(u, du, do_g[g], att_l[g], e_g[g], g)
    big = {
        "w_proj_ssd": dw_proj_ssd.reshape(4, DI // 4, D),
        "w_proj_attn": dw_proj_attn,
        "w_out": dw_out.reshape(4, D // 4, D),
        "w_up": dw_up,
        "w_down": dw_down.reshape(4, DFF // 4, D),
    }
    early = [big[n] for n in EARLY]
    dy, du, dnw, dds, recv = _gatenorm_bwd(ds_out, y_fb, xbc, u, du, dsk_row, nw_row,
                                           side=_swap_side(early) if rs_idx else None)
    if rs_idx:
        halves = [_add_half(g, r, rs_idx[0], f"rs_add_half_{n}") for g, r, n in zip(early, recv, EARLY)]
    dxs_f, dbc_f, ddt_f, sacc_f, recv = _ssd_bwd(xbc, u, par, dy, st_f, rev=False,
                                                 side=_step1_side([h[1] for h in halves]) if rs_idx else None)
    if rs_idx:
        k = len(EARLY)
        sums1 = [_rs_add1(h[0], ra, rb, rs_idx[1], f"rs_add1_{n}")
                 for h, ra, rb, n in zip(halves, recv[:k], recv[k:], EARLY)]
    dxs, dbc, ddt, sacc_b, recv = _ssd_bwd(
        xbc, u, par, dy, st_b, rev=True, add=(dxs_f, dbc_f, ddt_f),
        side=_step2_side([s1[2] for s1 in sums1], [s1[3] for s1 in sums1]) if rs_idx else None)
    pieces = None
    if rs_idx:
        pieces = {n: _rs_add2(s1[0], s1[1], ra, rb, rs_idx[1], f"rs_add2_{n}")
                  for s1, ra, rb, n in zip(sums1, recv[:k], recv[k:], EARLY)}
    dpre_c, dcw, dcb = _conv_dpre(u, dxs, dy, dbc, dsk_row, sm["conv_w"], cb_row)
    du = _conv_dx(du, dpre_c, sm["conv_w"])
    du = _dt_bwd(du, ddt)
    dw_in_p = _mm_tn(xb, du, tka=1024, tn=2432, tt=1024, name="dw_in")
    big["w_in"] = _shards_from_perm(dw_in_p)
    side = None
    if rs_idx:
        g = big["w_in"]
        half = _add_half(g, _run_side(_swap_side([g]), "rs_swap_halves")[0], rs_idx[0], "rs_add_half_w_in")
        side = _step1_side([half[1]])
    dx, recv = _d_x(du, wts["w_in_p"], dpre1, side)
    if rs_idx:
        s1 = _rs_add1(half[0], recv[0], recv[1], rs_idx[1], "rs_add1_w_in")
        ra2, rb2 = _run_side(_step2_side([s1[2]], [s1[3]]), "rs_step2")
        pieces["w_in"] = _rs_add2(s1[0], s1[1], ra2, rb2, rs_idx[1], "rs_add2_w_in")

    sacc = sacc_f + sacc_b
    small = {
        "b_gate": dbg[0], "conv_w": dcw[0:KCONV], "conv_b": dcb[0],
        "dt_bias_f": sacc[0, 0:32], "dt_bias_b": sacc[0, 32:64], "a_log_f": sacc[1, 0:32], "a_log_b": sacc[1, 32:64],
        "d_skip": dds[0, 0:32], "ssd_norm_w": dnw[0],
        "ln1_g": acc1[0], "ln1_b": acc1[1], "ln2_g": acc2[0], "ln2_b": acc2[1], "loss": acc2[2, 0:1],
    }
    return dx, big, small, pieces


HBM_SPEC = pl.BlockSpec(memory_space=pl.ANY)


def _place():
    x, y, c = lax.axis_index("x"), lax.axis_index("y"), lax.axis_index("c")
    chips = [(1 - x, y), (x, 1 - y), (1 - x, 1 - y)]
    return x, y, c, chips


def _gather_phases(n):
    def tools(ins, outs, send_sems, recv_sems):
        x, y, c, _ = _place()
        slots = (2 * x + y, 2 * (1 - x) + y, 2 * x + 1 - y, 2 * (1 - x) + 1 - y)
        peers = ((1 - x, y, c), (x, 1 - y, c), (x, y, 1 - c))

        def copy(w, k, src, dst, to):
            return pltpu.make_async_remote_copy(src_ref=src, dst_ref=dst, send_sem=send_sems.at[w, k],
                                                recv_sem=recv_sems.at[w, k], device_id=to, device_id_type=MESH)

        def rows(w, core, part):
            rh = ins[w].shape[0] // 2
            if part is None:
                return pl.ds(core * rh, rh)
            return pl.ds(core * rh + part * (rh // 2), rh // 2)

        def same(w, k, slot, core, part, to):
            blk = outs[w].at[slot, rows(w, core, part), :]
            return copy(w, k, blk, blk, to)

        def sends(w):
            q, q_x, q_y, q_d = slots
            x_nbr, y_nbr, sibling = peers
            mine = rows(w, c, None)
            mk = functools.partial
            return [mk(copy, w, 0, ins[w].at[mine, :], outs[w].at[q, mine, :], x_nbr),
                    mk(copy, w, 1, ins[w].at[mine, :], outs[w].at[q, mine, :], y_nbr),
                    mk(same, w, 2, q_x, c, 0, y_nbr), mk(same, w, 3, q_y, c, 1, x_nbr),
                    mk(same, w, 4, q_x, c, None, sibling), mk(same, w, 5, q_y, c, None, sibling),
                    mk(same, w, 6, q_d, c, 0, sibling), mk(same, w, 7, q_d, c, 1, sibling),
                    mk(copy, w, 8, ins[w], outs[w].at[q], sibling)]

        return c, slots, peers, same, sends

    def first(*refs):
        _, _, _, _, sends = tools(*refs)
        for w in range(n):
            cps = sends(w)
            for k in (8, 0, 1):
                cps[k]().start()

    def second(*refs):
        c, (_, q_x, q_y, _), (x_nbr, y_nbr, _), same, sends = tools(*refs)
        for w in range(n):
            cps = sends(w)
            same(w, 0, q_x, c, None, x_nbr).wait_recv()
            cps[2]().start()
            cps[4]().start()
            same(w, 1, q_y, c, None, y_nbr).wait_recv()
            cps[3]().start()
            cps[5]().start()

    def third(*refs):
        c, (_, _, _, q_d), (x_nbr, y_nbr, _), same, sends = tools(*refs)
        for w in range(n):
            cps = sends(w)
            same(w, 2, q_d, c, 0, y_nbr).wait_recv()
            cps[6]().start()
            same(w, 3, q_d, c, 1, x_nbr).wait_recv()
            cps[7]().start()

    def last(*refs):
        c, (_, q_x, q_y, q_d), (_, _, sibling), same, sends = tools(*refs)
        for w in range(n):
            same(w, 4, q_x, 1 - c, None, sibling).wait_recv()
            same(w, 5, q_y, 1 - c, None, sibling).wait_recv()
            same(w, 6, q_d, 1 - c, 0, sibling).wait_recv()
            same(w, 7, q_d, 1 - c, 1, sibling).wait_recv()
            sends(w)[8]().wait_recv()
        for w in range(n):
            for mk_cp in sends(w):
                mk_cp().wait_send()

    return first, second, third, last


def _gather_side(shards):
    first, second, third, last = _gather_phases(len(shards))
    shapes = tuple(jax.ShapeDtypeStruct((4,) + s.shape, s.dtype) for s in shards)
    return _Side(tuple(shards), shapes, (len(shards), 9), None, ((0.0, first), (0.36, second), (0.58, third), (1.0, last)))


def _allgather_weights(shards):
    n = len(shards)

    def body(*refs):
        ins, outs = refs[:n], refs[n:2 * n]
        send_sems, recv_sems = refs[2 * n:]
        x, y, c, _ = _place()
        q, q_x, q_y, q_d = 2 * x + y, 2 * (1 - x) + y, 2 * x + 1 - y, 2 * (1 - x) + 1 - y
        x_nbr, y_nbr, sibling = (1 - x, y, c), (x, 1 - y, c), (x, y, 1 - c)

        def copy(w, k, src, dst, to):
            return pltpu.make_async_remote_copy(src_ref=src, dst_ref=dst, send_sem=send_sems.at[w, k],
                                                recv_sem=recv_sems.at[w, k], device_id=to, device_id_type=MESH)

        def rows(w, core, part):
            rh = ins[w].shape[0] // 2
            if part is None:
                return pl.ds(core * rh, rh)
            return pl.ds(core * rh + part * (rh // 2), rh // 2)

        def same(w, k, slot, core, part, to):
            blk = outs[w].at[slot, rows(w, core, part), :]
            return copy(w, k, blk, blk, to)

        started = []
        for w in range(n):
            cp = copy(w, 8, ins[w], outs[w].at[q], sibling)
            cp.start()
            started.append(cp)
            mine = rows(w, c, None)
            for k, to in ((0, x_nbr), (1, y_nbr)):
                cp = copy(w, k, ins[w].at[mine, :], outs[w].at[q, mine, :], to)
                cp.start()
                started.append(cp)
        for w in range(n):
            same(w, 0, q_x, c, None, x_nbr).wait_recv()
            for cp in (same(w, 2, q_x, c, 0, y_nbr), same(w, 4, q_x, c, None, sibling)):
                cp.start()
                started.append(cp)
            same(w, 1, q_y, c, None, y_nbr).wait_recv()
            for cp in (same(w, 3, q_y, c, 1, x_nbr), same(w, 5, q_y, c, None, sibling)):
                cp.start()
                started.append(cp)
        for w in range(n):
            same(w, 2, q_d, c, 0, y_nbr).wait_recv()
            cp = same(w, 6, q_d, c, 0, sibling)
            cp.start()
            started.append(cp)
            same(w, 3, q_d, c, 1, x_nbr).wait_recv()
            cp = same(w, 7, q_d, c, 1, sibling)
            cp.start()
            started.append(cp)
        for w in range(n):
            same(w, 4, q_x, 1 - c, None, sibling).wait_recv()
            same(w, 5, q_y, 1 - c, None, sibling).wait_recv()
            same(w, 6, q_d, 1 - c, 0, sibling).wait_recv()
            same(w, 7, q_d, 1 - c, 1, sibling).wait_recv()
            copy(w, 8, ins[w], outs[w].at[q], sibling).wait_recv()
        for cp in started:
            cp.wait_send()

    return pl.pallas_call(
        body, out_shape=[jax.ShapeDtypeStruct((4,) + s.shape, s.dtype) for s in shards],
        in_specs=[HBM_SPEC] * n, out_specs=[HBM_SPEC] * n,
        scratch_shapes=[pltpu.SemaphoreType.DMA((n, 9)), pltpu.SemaphoreType.DMA((n, 9))],
        name="allgather_weights")(*shards)


def _swap_halves(grads):
    n = len(grads)

    def body(*refs):
        ins, outs = refs[:n], refs[n:2 * n]
        send_sems, recv_sems = refs[2 * n:]
        x, y, c, _ = _place()
        copies = []
        for w in range(n):
            rh = ins[w].shape[1] // 2
            for p in range(4):
                cp = pltpu.make_async_remote_copy(
                    src_ref=ins[w].at[p, pl.ds((1 - c) * rh, rh), :], dst_ref=outs[w].at[p],
                    send_sem=send_sems.at[w, p], recv_sem=recv_sems.at[w, p],
                    device_id=(x, y, 1 - c), device_id_type=MESH)
                cp.start()
                copies.append(cp)
        for cp in copies:
            cp.wait()

    return pl.pallas_call(
        body, out_shape=[jax.ShapeDtypeStruct((4, g.shape[1] // 2, g.shape[2]), F32) for g in grads],
        in_specs=[HBM_SPEC] * n, out_specs=[HBM_SPEC] * n,
        scratch_shapes=[pltpu.SemaphoreType.DMA((n, 4)), pltpu.SemaphoreType.DMA((n, 4))],
        name="rs_swap_halves")(*grads)


def _rs_step1(parts):
    n = len(parts)

    def body(*refs):
        ins, out_a, out_b = refs[:n], refs[n:2 * n], refs[2 * n:3 * n]
        send_sems, recv_sems = refs[3 * n:]
        x, y, c, _ = _place()
        copies = []
        for w in range(n):
            rq = ins[w].shape[1] // 2
            for i in range(2):
                copies.append(pltpu.make_async_remote_copy(
                    src_ref=ins[w].at[2 * (1 - x) + i, pl.ds(0, rq), :], dst_ref=out_a[w].at[i],
                    send_sem=send_sems.at[w, i], recv_sem=recv_sems.at[w, i],
                    device_id=(1 - x, y, c), device_id_type=MESH))
                copies.append(pltpu.make_async_remote_copy(
                    src_ref=ins[w].at[2 * i + 1 - y, pl.ds(rq, rq), :], dst_ref=out_b[w].at[i],
                    send_sem=send_sems.at[w, 2 + i], recv_sem=recv_sems.at[w, 2 + i],
                    device_id=(x, 1 - y, c), device_id_type=MESH))
        for cp in copies:
            cp.start()
        for cp in copies:
            cp.wait()

    quarter = lambda p: jax.ShapeDtypeStruct((2, p.shape[1] // 2, p.shape[2]), p.dtype)
    outs = pl.pallas_call(
        body, out_shape=[quarter(p) for p in parts] * 2,
        in_specs=[HBM_SPEC] * n, out_specs=[HBM_SPEC] * (2 * n),
        scratch_shapes=[pltpu.SemaphoreType.DMA((n, 4)), pltpu.SemaphoreType.DMA((n, 4))],
        name="rs_step1")(*parts)
    return outs[:n], outs[n:]


def _rs_step2(tas, tbs):
    n = len(tas)

    def body(*refs):
        in_a, in_b, out_a, out_b = refs[:n], refs[n:2 * n], refs[2 * n:3 * n], refs[3 * n:4 * n]
        send_sems, recv_sems = refs[4 * n:]
        x, y, c, _ = _place()
        copies = []
        for w in range(n):
            copies.append(pltpu.make_async_remote_copy(
                src_ref=in_a[w].at[1 - y], dst_ref=out_a[w], send_sem=send_sems.at[w, 0], recv_sem=recv_sems.at[w, 0],
                device_id=(x, 1 - y, c), device_id_type=MESH))
            copies.append(pltpu.make_async_remote_copy(
                src_ref=in_b[w].at[1 - x], dst_ref=out_b[w], send_sem=send_sems.at[w, 1], recv_sem=recv_sems.at[w, 1],
                device_id=(1 - x, y, c), device_id_type=MESH))
        for cp in copies:
            cp.start()
        for cp in copies:
            cp.wait()

    one = lambda p: jax.ShapeDtypeStruct(p.shape[1:], p.dtype)
    outs = pl.pallas_call(
        body, out_shape=[one(p) for p in tas] + [one(p) for p in tbs],
        in_specs=[HBM_SPEC] * (2 * n), out_specs=[HBM_SPEC] * (2 * n),
        scratch_shapes=[pltpu.SemaphoreType.DMA((n, 2)), pltpu.SemaphoreType.DMA((n, 2))],
        name="rs_step2")(*tas, *tbs)
    return outs[:n], outs[n:]


class _Side(NamedTuple):
    ins: tuple
    out_shapes: tuple
    nsem: tuple
    make: Callable
    phases: tuple = ()


def _swap_copies(ins, outs, send_sems, recv_sems):
    x, y, c, _ = _place()
    copies = []
    for w in range(len(ins)):
        rh = ins[w].shape[1] // 2
        for p in range(4):
            copies.append(pltpu.make_async_remote_copy(
                src_ref=ins[w].at[p, pl.ds((1 - c) * rh, rh), :], dst_ref=outs[w].at[p],
                send_sem=send_sems.at[w, p], recv_sem=recv_sems.at[w, p],
                device_id=(x, y, 1 - c), device_id_type=MESH))
    return copies


def _swap_side(grads):
    shapes = tuple(jax.ShapeDtypeStruct((4, g.shape[1] // 2, g.shape[2]), F32) for g in grads)
    return _Side(tuple(grads), shapes, (len(grads), 4), _swap_copies)


def _step1_copies(ins, outs, send_sems, recv_sems):
    n = len(ins)
    out_a, out_b = outs[:n], outs[n:]
    x, y, c, _ = _place()
    copies = []
    for w in range(n):
        rq = ins[w].shape[1] // 2
        for i in range(2):
            copies.append(pltpu.make_async_remote_copy(
                src_ref=ins[w].at[2 * (1 - x) + i, pl.ds(0, rq), :], dst_ref=out_a[w].at[i],
                send_sem=send_sems.at[w, i], recv_sem=recv_sems.at[w, i],
                device_id=(1 - x, y, c), device_id_type=MESH))
            copies.append(pltpu.make_async_remote_copy(
                src_ref=ins[w].at[2 * i + 1 - y, pl.ds(rq, rq), :], dst_ref=out_b[w].at[i],
                send_sem=send_sems.at[w, 2 + i], recv_sem=recv_sems.at[w, 2 + i],
                device_id=(x, 1 - y, c), device_id_type=MESH))
    return copies


def _step1_side(parts):
    quarter = tuple(jax.ShapeDtypeStruct((2, p.shape[1] // 2, p.shape[2]), p.dtype) for p in parts)
    return _Side(tuple(parts), quarter + quarter, (len(parts), 4), _step1_copies)


def _step2_copies(ins, outs, send_sems, recv_sems):
    n = len(ins) // 2
    in_a, in_b, out_a, out_b = ins[:n], ins[n:], outs[:n], outs[n:]
    x, y, c, _ = _place()
    copies = []
    for w in range(n):
        copies.append(pltpu.make_async_remote_copy(
            src_ref=in_a[w].at[1 - y], dst_ref=out_a[w], send_sem=send_sems.at[w, 0], recv_sem=recv_sems.at[w, 0],
            device_id=(x, 1 - y, c), device_id_type=MESH))
        copies.append(pltpu.make_async_remote_copy(
            src_ref=in_b[w].at[1 - x], dst_ref=out_b[w], send_sem=send_sems.at[w, 1], recv_sem=recv_sems.at[w, 1],
            device_id=(1 - x, y, c), device_id_type=MESH))
    return copies


def _step2_side(tas, tbs):
    one = tuple(jax.ShapeDtypeStruct(p.shape[1:], p.dtype) for p in tuple(tas) + tuple(tbs))
    return _Side(tuple(tas) + tuple(tbs), one, (len(tas), 2), _step2_copies)


def _phases_of(side, n_steps):
    if side.phases:
        return [(min(int(f * n_steps), n_steps - 1), fn) for f, fn in side.phases]

    def start(*refs):
        for cp in side.make(*refs):
            cp.start()

    def wait(*refs):
        for cp in side.make(*refs):
            cp.wait()

    return [(0, start), (n_steps - 1, wait)]


def _run_side(side, name):
    n_in, n_out = len(side.ins), len(side.out_shapes)

    def body(*refs):
        for _, fn in _phases_of(side, 1):
            fn(refs[:n_in], refs[n_in:n_in + n_out], *refs[n_in + n_out:])

    return pl.pallas_call(
        body, out_shape=list(side.out_shapes), in_specs=[HBM_SPEC] * n_in, out_specs=[HBM_SPEC] * n_out,
        scratch_shapes=[pltpu.SemaphoreType.DMA(side.nsem), pltpu.SemaphoreType.DMA(side.nsem)], name=name)(*side.ins)


def _host_call(body, side, n_steps, *, out_shape, in_specs, out_specs, scratch_shapes, args, aliases, name, sem):
    n_in, n_out, n_scr = len(in_specs), len(out_shape), len(scratch_shapes)
    if side is None:
        outs = pl.pallas_call(body, out_shape=tuple(out_shape), grid=(n_steps,), in_specs=list(in_specs),
                              out_specs=tuple(out_specs), scratch_shapes=list(scratch_shapes),
                              input_output_aliases=aliases, name=name, compiler_params=_params(sem))(*args)
        return tuple(outs), ()
    ns_in, ns_out = len(side.ins), len(side.out_shapes)

    def wrapped(*refs):
        h_in, s_in = refs[:n_in], refs[n_in:n_in + ns_in]
        o0 = n_in + ns_in
        h_out, s_out = refs[o0:o0 + n_out], refs[o0 + n_out:o0 + n_out + ns_out]
        c0 = o0 + n_out + ns_out
        h_scr, sems = refs[c0:c0 + n_scr], refs[c0 + n_scr:]
        step = pl.program_id(0)
        phases = _phases_of(side, n_steps)
        for at, fn in phases[:-1]:
            pl.when(step == at)(functools.partial(fn, s_in, s_out, *sems))
        body(*h_in, *h_out, *h_scr)
        pl.when(step == phases[-1][0])(functools.partial(phases[-1][1], s_in, s_out, *sems))

    outs = pl.pallas_call(
        wrapped, out_shape=tuple(out_shape) + tuple(side.out_shapes), grid=(n_steps,),
        in_specs=list(in_specs) + [HBM_SPEC] * ns_in, out_specs=tuple(out_specs) + (HBM_SPEC,) * ns_out,
        scratch_shapes=list(scratch_shapes) + [pltpu.SemaphoreType.DMA(side.nsem), pltpu.SemaphoreType.DMA(side.nsem)],
        input_output_aliases=aliases, name=name, compiler_params=_params(sem))(*args, *side.ins)
    return tuple(outs[:n_out]), tuple(outs[n_out:])


def _join_halves(pieces):
    n = len(pieces)

    def body(*refs):
        outs = refs[n:2 * n]
        send_sems, recv_sems = refs[2 * n:]
        x, y, c, _ = _place()

        def copy(w, slot):
            return pltpu.make_async_remote_copy(
                src_ref=outs[w].at[slot], dst_ref=outs[w].at[slot], send_sem=send_sems.at[w], recv_sem=recv_sems.at[w],
                device_id=(x, y, 1 - c), device_id_type=MESH)

        for w in range(n):
            copy(w, c).start()
        for w in range(n):
            copy(w, 1 - c).wait_recv()
            copy(w, c).wait_send()

    return pl.pallas_call(
        body, out_shape=[jax.ShapeDtypeStruct(p.shape, F32) for p in pieces],
        in_specs=[HBM_SPEC] * n, out_specs=[HBM_SPEC] * n, input_output_aliases={w: w for w in range(n)},
        scratch_shapes=[pltpu.SemaphoreType.DMA((n,)), pltpu.SemaphoreType.DMA((n,))],
        name="rs_join_halves")(*pieces)


def _add_tile_rows(rh, c):
    for cand in (512, 256, 128, 64, 32, 16, 8):
        if rh % cand == 0 and cand * c * 4 <= 2 ** 21:
            return cand
    return rh


def _add_half(grad, recv, c_idx, name):
    _, r, cc = grad.shape
    rh = r // 2
    tr = _add_tile_rows(rh, cc)
    nb = rh // tr

    def body(c_ref, g_ref, r_ref, o_ref, ob_ref):
        del c_ref
        s = g_ref[...] + r_ref[...]
        o_ref[...] = s
        ob_ref[...] = s.astype(BF16)

    blk = pl.BlockSpec((None, tr, cc), lambda p, i, c_ref: (p, i, 0))
    grid_spec = pltpu.PrefetchScalarGridSpec(
        num_scalar_prefetch=1, grid=(4, nb),
        in_specs=[pl.BlockSpec((None, tr, cc), lambda p, i, c_ref: (p, c_ref[0] * nb + i, 0)), blk],
        out_specs=(blk, blk))
    return pl.pallas_call(
        body, out_shape=(jax.ShapeDtypeStruct((4, rh, cc), F32), jax.ShapeDtypeStruct((4, rh, cc), BF16)),
        grid_spec=grid_spec, name=name, compiler_params=_params(("parallel", "parallel")))(c_idx, grad, recv)


def _rs_add1(part, recv_a, recv_b, xy_idx, name):
    _, rh, cc = part.shape
    rq = rh // 2
    tr = _add_tile_rows(rq, cc)
    nb = rq // tr

    def body(xy_ref, pa_ref, pb_ref, ra_ref, rb_ref, ta_ref, tb_ref, tab_ref, tbb_ref):
        del xy_ref
        ta = pa_ref[...] + ra_ref[...].astype(F32)
        tb = pb_ref[...] + rb_ref[...].astype(F32)
        ta_ref[...] = ta
        tb_ref[...] = tb
        tab_ref[...] = ta.astype(BF16)
        tbb_ref[...] = tb.astype(BF16)

    blk = pl.BlockSpec((None, tr, cc), lambda i, j, xy: (i, j, 0))
    grid_spec = pltpu.PrefetchScalarGridSpec(
        num_scalar_prefetch=1, grid=(2, nb),
        in_specs=[pl.BlockSpec((None, tr, cc), lambda i, j, xy: (2 * xy[0] + i, j, 0)),
                  pl.BlockSpec((None, tr, cc), lambda i, j, xy: (2 * i + xy[1], nb + j, 0)), blk, blk],
        out_specs=(blk, blk, blk, blk))
    f32s, b16s = jax.ShapeDtypeStruct((2, rq, cc), F32), jax.ShapeDtypeStruct((2, rq, cc), BF16)
    return pl.pallas_call(body, out_shape=(f32s, f32s, b16s, b16s), grid_spec=grid_spec, name=name,
                          compiler_params=_params(("parallel", "parallel")))(xy_idx, part, part, recv_a, recv_b)


def _rs_add2(ta, tb, recv_a, recv_b, xy_idx, name):
    _, rq, cc = ta.shape
    tr = _add_tile_rows(rq, cc)
    nb = rq // tr

    def body(xy_ref, ta_ref, tb_ref, ra_ref, rb_ref, o_ref):
        del xy_ref
        s = pl.program_id(0)
        fa = ta_ref[...] + ra_ref[...].astype(F32)
        fb = tb_ref[...] + rb_ref[...].astype(F32)
        o_ref[...] = jnp.where(s == 0, fa, fb)

    rblk = pl.BlockSpec((tr, cc), lambda s, j, xy: (j, 0))
    grid_spec = pltpu.PrefetchScalarGridSpec(
        num_scalar_prefetch=1, grid=(2, nb),
        in_specs=[pl.BlockSpec((None, tr, cc), lambda s, j, xy: (xy[1], j, 0)),
                  pl.BlockSpec((None, tr, cc), lambda s, j, xy: (xy[0], j, 0)), rblk, rblk],
        out_specs=pl.BlockSpec((None, tr, cc), lambda s, j, xy: (xy[2], s * nb + j, 0)))
    return pl.pallas_call(body, out_shape=jax.ShapeDtypeStruct((2, 2 * rq, cc), F32), grid_spec=grid_spec, name=name,
                          compiler_params=_params(("parallel", "parallel")))(xy_idx, ta, tb, recv_a, recv_b)


def _allreduce_small(slab):
    r = slab.shape[0]

    def body(x_ref, o_ref, buf, send_sems, recv_sems):
        x, y, c, _ = _place()
        me = 4 * x + 2 * y + c
        buf[me] = x_ref[...]
        peers = []
        for k in range(1, 8):
            kx, ky, kc = (k >> 2) & 1, (k >> 1) & 1, k & 1
            peers.append((x + kx - 2 * x * kx, y + ky - 2 * y * ky, c + kc - 2 * c * kc))

        def copy(k, slot):
            return pltpu.make_async_remote_copy(src_ref=x_ref, dst_ref=buf.at[slot], send_sem=send_sems.at[k],
                                                recv_sem=recv_sems.at[k], device_id=peers[k], device_id_type=MESH)

        for k in range(7):
            copy(k, me).start()
        for k, (px, py, pc) in enumerate(peers):
            copy(k, 4 * px + 2 * py + pc).wait_recv()
        for k in range(7):
            copy(k, me).wait_send()
        acc = buf[0]
        for j in range(1, 8):
            acc = acc + buf[j]
        o_ref[...] = acc

    vm = pl.BlockSpec(memory_space=pltpu.VMEM)
    return pl.pallas_call(
        body, out_shape=jax.ShapeDtypeStruct((r, 128), F32), in_specs=[vm], out_specs=vm,
        scratch_shapes=[pltpu.VMEM((8, r, 128), F32), pltpu.SemaphoreType.DMA((7,)), pltpu.SemaphoreType.DMA((7,))],
        name="allreduce_small")(slab)


def _pack(arrs):
    rows = []
    for a in arrs:
        v = a.reshape(-1)
        v = jnp.pad(v, (0, (-v.shape[0]) % 128))
        rows.append(v.reshape(-1, 128))
    slab = jnp.concatenate(rows, axis=0)
    return jnp.pad(slab, ((0, (-slab.shape[0]) % 8), (0, 0)))


def _unpack(slab, shapes):
    out, r0 = [], 0
    for shp in shapes:
        size = math.prod(shp)
        nr = -(-size // 128)
        out.append(slab[r0:r0 + nr].reshape(-1)[:size].reshape(shp))
        r0 += nr
    return out


BIG = ("w_in", "w_proj_ssd", "w_proj_attn", "w_out", "w_up", "w_down")
SMALL = ("b_gate", "conv_w", "conv_b", "dt_bias_f", "dt_bias_b", "a_log_f", "a_log_b", "d_skip", "ssd_norm_w",
         "ln1_g", "ln1_b", "ln2_g", "ln2_b")
ORDER = ("w_in", "b_gate", "conv_w", "conv_b", "dt_bias_f", "dt_bias_b", "a_log_f", "a_log_b", "d_skip", "ssd_norm_w",
         "w_proj_ssd", "w_proj_attn", "w_out", "ln1_g", "ln1_b", "w_up", "w_down", "ln2_g", "ln2_b")


def kernel(x, w_in, b_gate, conv_w, conv_b, dt_bias_f, dt_bias_b, a_log_f, a_log_b, d_skip, ssd_norm_w, w_proj_ssd, w_proj_attn, w_out, ln1_g, ln1_b, w_up, w_down, ln2_g, ln2_b, loss_target, m_w_in, m_b_gate, m_conv_w, m_conv_b, m_dt_bias_f, m_dt_bias_b, m_a_log_f, m_a_log_b, m_d_skip, m_ssd_norm_w, m_w_proj_ssd, m_w_proj_attn, m_w_out, m_ln1_g, m_ln1_b, m_w_up, m_w_down, m_ln2_g, m_ln2_b, v_w_in, v_b_gate, v_conv_w, v_conv_b, v_dt_bias_f, v_dt_bias_b, v_a_log_f, v_a_log_b, v_d_skip, v_ssd_norm_w, v_w_proj_ssd, v_w_proj_attn, v_w_out, v_ln1_g, v_ln1_b, v_w_up, v_w_down, v_ln2_g, v_ln2_b):
    w = dict(w_in=w_in, b_gate=b_gate, conv_w=conv_w, conv_b=conv_b, dt_bias_f=dt_bias_f, dt_bias_b=dt_bias_b,
             a_log_f=a_log_f, a_log_b=a_log_b, d_skip=d_skip, ssd_norm_w=ssd_norm_w, w_proj_ssd=w_proj_ssd,
             w_proj_attn=w_proj_attn, w_out=w_out, ln1_g=ln1_g, ln1_b=ln1_b, w_up=w_up, w_down=w_down, ln2_g=ln2_g, ln2_b=ln2_b)
    m = dict(w_in=m_w_in, b_gate=m_b_gate, conv_w=m_conv_w, conv_b=m_conv_b, dt_bias_f=m_dt_bias_f, dt_bias_b=m_dt_bias_b,
             a_log_f=m_a_log_f, a_log_b=m_a_log_b, d_skip=m_d_skip, ssd_norm_w=m_ssd_norm_w, w_proj_ssd=m_w_proj_ssd,
             w_proj_attn=m_w_proj_attn, w_out=m_w_out, ln1_g=m_ln1_g, ln1_b=m_ln1_b, w_up=m_w_up, w_down=m_w_down,
             ln2_g=m_ln2_g, ln2_b=m_ln2_b)
    v = dict(w_in=v_w_in, b_gate=v_b_gate, conv_w=v_conv_w, conv_b=v_conv_b, dt_bias_f=v_dt_bias_f, dt_bias_b=v_dt_bias_b,
             a_log_f=v_a_log_f, a_log_b=v_a_log_b, d_skip=v_d_skip, ssd_norm_w=v_ssd_norm_w, w_proj_ssd=v_w_proj_ssd,
             w_proj_attn=v_w_proj_attn, w_out=v_w_out, ln1_g=v_ln1_g, ln1_b=v_ln1_b, w_up=v_w_up, w_down=v_w_down,
             ln2_g=v_ln2_g, ln2_b=v_ln2_b)
    xi, yi, ci = lax.axis_index("x"), lax.axis_index("y"), lax.axis_index("c")
    shard = 2 * xi + yi

    (g_in,) = _run_side(_gather_side([w["w_in"].astype(BF16)]), "allgather_w_in")
    wts = {"w_in_p": _perm_from_shards(g_in), "pending": [w[n].astype(BF16) for n in EARLY]}

    cw_slab = jnp.zeros((KCONV, 4, CONVD // 4), F32)
    cw_slab = lax.dynamic_update_slice(cw_slab, conv_w[:, None, :] * 0.5, (0, shard, 0))
    conv_w_all = _unpack(_allreduce_small(_pack([cw_slab])), [(KCONV, CONVD)])[0]

    sm = {n: w[n] for n in SMALL}
    sm["conv_w"] = conv_w_all
    c_idx = jnp.reshape(ci, (1,)).astype(jnp.int32)
    xy_idx = jnp.stack([xi, yi, ci]).astype(jnp.int32)
    dx, big, small, pieces = _local_grads(x[0], loss_target[0], wts, sm, rs_idx=(c_idx, xy_idx))

    names = list(SMALL) + ["loss"]
    shapes = [small[n].shape for n in names]
    red = dict(zip(names, _unpack(_allreduce_small(_pack([small[n] for n in names])), shapes)))
    loss = red["loss"].reshape(())
    gsm = {n: red[n] for n in SMALL}
    conv_w_grad_shard = lax.dynamic_slice_in_dim(gsm["conv_w"].reshape(KCONV, 4, CONVD // 4), shard, 1, axis=1)
    gsm["conv_w"] = conv_w_grad_shard.reshape(KCONV, CONVD // 4)

    joined = _join_halves([pieces[n] for n in BIG])
    gbig = {n: j.reshape(w[n].shape) for n, j in zip(BIG, joined)}

    grads, deltas, new_m, new_v = {}, {}, {}, {}
    for n in BIG:
        grads[n] = gbig[n]
        if n == "w_in":
            gt = gbig[n].T
            dlt, nmt, nvt = _adamw(w[n].T, gt, m[n].T, v[n].T, f"adamw_{n}")
            grads[n], deltas[n], new_m[n], new_v[n] = gt.T, dlt.T, nmt.T, nvt.T
            continue
        deltas[n], new_m[n], new_v[n] = _adamw(w[n], gbig[n], m[n], v[n], f"adamw_{n}")
    sshapes = [w[n].shape for n in SMALL]
    d_s, m_s, v_s = _adamw(_pack([w[n] for n in SMALL]), _pack([gsm[n] for n in SMALL]),
                           _pack([m[n] for n in SMALL]), _pack([v[n] for n in SMALL]), "adamw_small")
    for n, dd, mm, vv in zip(SMALL, _unpack(d_s, sshapes), _unpack(m_s, sshapes), _unpack(v_s, sshapes)):
        grads[n], deltas[n], new_m[n], new_v[n] = gsm[n], dd, mm, vv

    return (loss, dx[None], *[grads[n] for n in ORDER], *[deltas[n] for n in ORDER],
            *[new_m[n] for n in ORDER], *[new_v[n] for n in ORDER])
```

```python
import functools
import math
from typing import Callable, NamedTuple

import jax
import numpy as np
import jax.numpy as jnp
from jax import lax
from jax.experimental import pallas as pl
from jax.experimental.pallas import tpu as pltpu

F32, BF16 = jnp.float32, jnp.bfloat16
MESH = pl.DeviceIdType.MESH

D = 1024
DI = 2048
NH = 32
HP = 64
NG = 4
NS = 128
Q = 128
CONVD = 3072
KCONV = 5
DFF = 4096
AH = 64
ATT_HALF = 64
DILATIONS = (1, 4, 16)
IN_COLS = 9536
OZ, OGATE, OXBC, OKV, OQ, ODT, UW = 0, 2048, 4096, 7168, 8704, 9472, 9728
ALPHA = 2.0 ** 0.25
NORM_EPS = 1e-5
ADAM_LR, ADAM_B1, ADAM_B2, ADAM_EPS, ADAM_WD, ADAM_STEP = 0.001, 0.9, 0.999, 1e-8, 0.01, 10
VMEM_LIMIT = 56 * 2 ** 20
NEG = -1e30


def _params(sem):
    return pltpu.CompilerParams(dimension_semantics=sem, vmem_limit_bytes=VMEM_LIMIT)


def _sigmoid(x):
    return 1.0 / (1.0 + jnp.exp(-x))


def _softplus(x):
    e = jnp.exp(-jnp.abs(x))
    small = e * (1.0 - e * (0.5 - e * (1.0 / 3.0)))
    return jnp.maximum(x, 0.0) + jnp.where(e < 0.01, small, jnp.log(1.0 + e))


def _split3(a):
    hi = a.astype(BF16)
    r = a - hi.astype(F32)
    mid = r.astype(BF16)
    lo = (r - mid.astype(F32)).astype(BF16)
    return hi, mid, lo


def _dot01(a, m01):
    hi, mid, lo = _split3(a)
    d = lambda p: jnp.dot(p, m01, preferred_element_type=F32)
    return d(hi) + d(mid) + d(lo)


def _dot01_l(m01, a):
    hi, mid, lo = _split3(a)
    d = lambda p: jnp.dot(m01, p, preferred_element_type=F32)
    return d(hi) + d(mid) + d(lo)


def _dot_nt(a, b):
    return lax.dot_general(a, b, (((1,), (1,)), ((), ())), preferred_element_type=F32)


def _iota(shape, dim):
    return lax.broadcasted_iota(jnp.int32, shape, dim)


def _mm_nn(a, b, *, tm, tn, name, out_dtype=F32):
    m, k = a.shape
    if b.ndim == 3:
        assert tn == b.shape[2]
        n = b.shape[0] * b.shape[2]
        b_spec = pl.BlockSpec((None, k, tn), lambda j, i: (j, 0, 0))
    else:
        n = b.shape[1]
        b_spec = pl.BlockSpec((k, tn), lambda j, i: (0, j))

    def body(a_ref, b_ref, o_ref):
        o_ref[...] = jnp.dot(a_ref[...].astype(BF16), b_ref[...], preferred_element_type=F32).astype(out_dtype)

    return pl.pallas_call(
        body, out_shape=jax.ShapeDtypeStruct((m, n), out_dtype), grid=(n // tn, m // tm),
        in_specs=[pl.BlockSpec((tm, k), lambda j, i: (i, 0)), b_spec],
        out_specs=pl.BlockSpec((tm, tn), lambda j, i: (i, j)),
        name=name, compiler_params=_params(("parallel", "parallel")))(a, b)


def _mm_nt(a, b, *, tm, tk, tc, name):
    m, n = a.shape
    if b.ndim == 3:
        assert tc == b.shape[2]
        k, nc = b.shape[1], b.shape[0]
        b_spec = pl.BlockSpec((None, tk, tc), lambda j, i, c: (c, j, 0))
    else:
        k, nc = b.shape[0], n // tc
        b_spec = pl.BlockSpec((tk, tc), lambda j, i, c: (j, c))

    def body(a_ref, b_ref, o_ref):
        c = pl.program_id(2)
        part = _dot_nt(a_ref[...].astype(BF16), b_ref[...])

        @pl.when(c == 0)
        def _():
            o_ref[...] = part

        @pl.when(c > 0)
        def _():
            o_ref[...] += part

    return pl.pallas_call(
        body, out_shape=jax.ShapeDtypeStruct((m, k), F32), grid=(k // tk, m // tm, nc),
        in_specs=[pl.BlockSpec((tm, tc), lambda j, i, c: (i, c)), b_spec],
        out_specs=pl.BlockSpec((tm, tk), lambda j, i, c: (i, j)),
        name=name, compiler_params=_params(("parallel", "parallel", "arbitrary")))(a, b)


def _mm_tn(a, b, *, tka, tn, tt, name, out_shards=None):
    t, ka = a.shape
    n = b.shape[1]
    if out_shards:
        assert tn == n // out_shards
        out_shape = jax.ShapeDtypeStruct((out_shards, ka, tn), F32)
        o_spec = pl.BlockSpec((None, tka, tn), lambda i, j, s: (j, i, 0))
    else:
        out_shape = jax.ShapeDtypeStruct((ka, n), F32)
        o_spec = pl.BlockSpec((tka, tn), lambda i, j, s: (i, j))

    def body(a_ref, b_ref, o_ref):
        s = pl.program_id(2)
        part = lax.dot_general(a_ref[...].astype(BF16), b_ref[...].astype(BF16), (((0,), (0,)), ((), ())),
                               preferred_element_type=F32)

        @pl.when(s == 0)
        def _():
            o_ref[...] = part

        @pl.when(s > 0)
        def _():
            o_ref[...] += part

    return pl.pallas_call(
        body, out_shape=out_shape, grid=(ka // tka, n // tn, t // tt),
        in_specs=[pl.BlockSpec((tt, tka), lambda i, j, s: (s, i)), pl.BlockSpec((tt, tn), lambda i, j, s: (s, j))],
        out_specs=o_spec, name=name, compiler_params=_params(("parallel", "parallel", "arbitrary")))(a, b)


def _d_x(du, w_in_p, dpre1, side=None):
    t = du.shape[0]
    tm, tc = 1024, 2432
    nc = UW // tc

    def body(a_ref, b_ref, add_ref, o_ref):
        c = pl.program_id(0) % nc
        part = _dot_nt(a_ref[...], b_ref[...])

        @pl.when(c == 0)
        def _():
            o_ref[...] = part + ALPHA * add_ref[...]

        @pl.when(c > 0)
        def _():
            o_ref[...] += part

    outs, side_outs = _host_call(
        body, side, (t // tm) * nc, out_shape=(jax.ShapeDtypeStruct((t, D), F32),),
        in_specs=[pl.BlockSpec((tm, tc), lambda s: (s // nc, s % nc)), pl.BlockSpec((D, tc), lambda s: (0, s % nc)),
                  pl.BlockSpec((tm, D), lambda s: (s // nc, 0))],
        out_specs=(pl.BlockSpec((tm, D), lambda s: (s // nc, 0)),),
        scratch_shapes=[], args=(du, w_in_p, dpre1), aliases={}, name="d_x", sem=("arbitrary",))
    return outs[0], side_outs


def _in_proj(xb, w_in_p, side=None):
    t, k = xb.shape
    tm, tn = 1024, 2432
    nm, nn = t // tm, UW // tn

    def body(a_ref, b_ref, o_ref):
        o_ref[...] = jnp.dot(a_ref[...], b_ref[...], preferred_element_type=F32)

    outs, side_outs = _host_call(
        body, side, nm * nn, out_shape=(jax.ShapeDtypeStruct((t, UW), F32),),
        in_specs=[pl.BlockSpec((tm, k), lambda s: (s % nm, 0)), pl.BlockSpec((k, tn), lambda s: (0, s // nm))],
        out_specs=(pl.BlockSpec((tm, tn), lambda s: (s % nm, s // nm)),),
        scratch_shapes=[], args=(xb, w_in_p), aliases={}, name="in_proj", sem=("arbitrary",))
    return outs[0], side_outs


CONV_TM = 512
CONV_TC = 1024
CONV_RC = 64
CONV_CC = 256


def _halo_specs(t, tm, tc, col0):
    nb8 = t // 8
    r8 = tm // 8
    return [
        pl.BlockSpec((8, tc), lambda i, j: (jnp.maximum(i * r8 - 1, 0), col0 + j)),
        pl.BlockSpec((tm, tc), lambda i, j: (i, col0 + j)),
        pl.BlockSpec((8, tc), lambda i, j: (jnp.minimum((i + 1) * r8, nb8 - 1), col0 + j)),
    ]


def _fill_ext(ext, prev_ref, cur_ref, next_ref, tm, i, last):
    ext[0:8, :] = jnp.where(i > 0, prev_ref[...], 0.0)
    ext[8:8 + tm, :] = cur_ref[...]
    ext[8 + tm:16 + tm, :] = jnp.where(i < last, next_ref[...], 0.0)


def _conv_fwd(u, conv_w, conv_b):
    t = u.shape[0]
    tm, tc = CONV_TM, CONV_TC

    def body(prev_ref, cur_ref, next_ref, w_ref, b_ref, o_ref, ext):
        _fill_ext(ext, prev_ref, cur_ref, next_ref, tm, pl.program_id(0), t // tm - 1)
        for c0 in range(0, tc, CONV_CC):
            cs = slice(c0, c0 + CONV_CC)
            w = w_ref[:, cs]
            for r0 in range(0, tm, CONV_RC):
                acc = jnp.broadcast_to(b_ref[:, cs], (CONV_RC, CONV_CC))
                for k in range(KCONV):
                    acc = acc + w[k:k + 1, :] * ext[pl.ds(r0 + 6 + k, CONV_RC), cs]
                o_ref[r0:r0 + CONV_RC, cs] = acc * _sigmoid(acc)

    return pl.pallas_call(
        body, out_shape=jax.ShapeDtypeStruct((t, CONVD), F32), grid=(t // tm, CONVD // tc),
        in_specs=_halo_specs(t, tm, tc, OXBC // tc) + [
            pl.BlockSpec((KCONV, tc), lambda i, j: (0, j)), pl.BlockSpec((1, tc), lambda i, j: (0, j))],
        out_specs=pl.BlockSpec((tm, tc), lambda i, j: (i, j)),
        scratch_shapes=[pltpu.VMEM((tm + 16, tc), F32)],
        name="conv_fwd", compiler_params=_params(("parallel", "parallel")))(u, u, u, conv_w, conv_b)


def _conv_dpre(u, dxs, dy, dbc, dsk_row, conv_w, conv_b):
    t = u.shape[0]
    tm, tc = CONV_TM, CONV_TC
    r8 = tm // 8
    nb8 = t // 8
    c0 = OXBC // tc

    def body(uprev, ucur, unext, f_ref, y_ref, cf_ref, dsk_ref, w_ref, bias_ref, dpre_ref, dw_ref, db_ref, ext):
        j = pl.program_id(0)
        i = pl.program_id(1)
        _fill_ext(ext, uprev, ucur, unext, tm, i, t // tm - 1)
        is_xs = j < 2
        dw_cols, db_cols = [], []
        for c0 in range(0, tc, CONV_CC):
            cs = slice(c0, c0 + CONV_CC)
            w = w_ref[:, cs]
            dsk = dsk_ref[:, cs]
            dw_acc = [jnp.zeros((1, CONV_CC), F32) for _ in range(KCONV)]
            db_acc = jnp.zeros((1, CONV_CC), F32)
            for r0 in range(0, tm, CONV_RC):
                rs = slice(r0, r0 + CONV_RC)
                taps = [ext[pl.ds(r0 + 6 + k, CONV_RC), cs] for k in range(KCONV)]
                pre = jnp.broadcast_to(bias_ref[:, cs], (CONV_RC, CONV_CC))
                for k in range(KCONV):
                    pre = pre + w[k:k + 1, :] * taps[k]
                s = _sigmoid(pre)
                up = jnp.where(is_xs, f_ref[rs, cs] + dsk * y_ref[rs, cs], cf_ref[rs, cs])
                dpre = up * (s * (1.0 + pre * (1.0 - s)))
                dpre_ref[rs, cs] = dpre
                for k in range(KCONV):
                    dw_acc[k] = dw_acc[k] + jnp.sum(dpre * taps[k], axis=0, keepdims=True)
                db_acc = db_acc + jnp.sum(dpre, axis=0, keepdims=True)
            dw_cols.append(jnp.concatenate(dw_acc + [jnp.zeros((8 - KCONV, CONV_CC), F32)], axis=0))
            db_cols.append(jnp.broadcast_to(db_acc, (8, CONV_CC)))
        dw_part = jnp.concatenate(dw_cols, axis=1)
        db_part = jnp.concatenate(db_cols, axis=1)

        @pl.when(i == 0)
        def _():
            dw_ref[...] = dw_part
            db_ref[...] = db_part

        @pl.when(i > 0)
        def _():
            dw_ref[...] += dw_part
            db_ref[...] += db_part

    xs_spec = pl.BlockSpec((tm, tc), lambda j, i: (jnp.where(j < 2, i, 0), jnp.minimum(j, 1)))
    bc_spec = pl.BlockSpec((tm, tc), lambda j, i: (jnp.where(j == 2, i, 0), 0))
    in_specs = [
        pl.BlockSpec((8, tc), lambda j, i: (jnp.maximum(i * r8 - 1, 0), c0 + j)),
        pl.BlockSpec((tm, tc), lambda j, i: (i, c0 + j)),
        pl.BlockSpec((8, tc), lambda j, i: (jnp.minimum((i + 1) * r8, nb8 - 1), c0 + j)),
        xs_spec, xs_spec, bc_spec,
        pl.BlockSpec((1, tc), lambda j, i: (0, jnp.minimum(j, 1))),
        pl.BlockSpec((KCONV, tc), lambda j, i: (0, j)), pl.BlockSpec((1, tc), lambda j, i: (0, j)),
    ]
    return pl.pallas_call(
        body,
        out_shape=(jax.ShapeDtypeStruct((t, CONVD), F32), jax.ShapeDtypeStruct((8, CONVD), F32),
                   jax.ShapeDtypeStruct((8, CONVD), F32)),
        grid=(CONVD // tc, t // tm), in_specs=in_specs,
        out_specs=(pl.BlockSpec((tm, tc), lambda j, i: (i, j)),
                   pl.BlockSpec((8, tc), lambda j, i: (0, j)), pl.BlockSpec((8, tc), lambda j, i: (0, j))),
        scratch_shapes=[pltpu.VMEM((tm + 16, tc), F32)],
        name="conv_dpre", compiler_params=_params(("parallel", "arbitrary")))(
            u, u, u, dxs, dy, dbc, dsk_row, conv_w, conv_b)


def _conv_dx(du, dpre, conv_w):
    t = dpre.shape[0]
    tm, tc = CONV_TM, CONV_TC
    r8 = tm // 8
    nb8 = t // 8

    def body(prev_ref, cur_ref, next_ref, w_ref, du_in, du_out, ext):
        del du_in
        _fill_ext(ext, prev_ref, cur_ref, next_ref, tm, pl.program_id(1), t // tm - 1)
        for c0 in range(0, tc, CONV_CC):
            cs = slice(c0, c0 + CONV_CC)
            w = w_ref[:, cs]
            for r0 in range(0, tm, CONV_RC):
                acc = jnp.zeros((CONV_RC, CONV_CC), F32)
                for k in range(KCONV):
                    acc = acc + w[k:k + 1, :] * ext[pl.ds(r0 + 10 - k, CONV_RC), cs]
                du_out[r0:r0 + CONV_RC, cs] = acc.astype(du_out.dtype)

    in_specs = [
        pl.BlockSpec((8, tc), lambda j, i: (jnp.maximum(i * r8 - 1, 0), j)),
        pl.BlockSpec((tm, tc), lambda j, i: (i, j)),
        pl.BlockSpec((8, tc), lambda j, i: (jnp.minimum((i + 1) * r8, nb8 - 1), j)),
        pl.BlockSpec((KCONV, tc), lambda j, i: (0, j)),
        pl.BlockSpec(memory_space=pl.ANY),
    ]
    return pl.pallas_call(
        body, out_shape=jax.ShapeDtypeStruct(du.shape, du.dtype), grid=(CONVD // tc, t // tm), in_specs=in_specs,
        out_specs=pl.BlockSpec((tm, tc), lambda j, i: (i, OXBC // tc + j)),
        scratch_shapes=[pltpu.VMEM((tm + 16, tc), F32)], input_output_aliases={4: 0},
        name="conv_dx", compiler_params=_params(("parallel", "parallel")))(dpre, dpre, dpre, conv_w, du)


def _ssd_common(dtr_ref, par_ref, rev):
    raw = dtr_ref[...]
    lane = _iota((1, 128), 1)
    mine = (lane >= 32 * rev) & (lane < 32 * rev + 32)
    bias = par_ref[0:1, :]
    arow = jnp.where(mine, -jnp.exp(par_ref[1:2, :]), 0.0)
    dt = _softplus(raw + bias)
    a = dt * arow
    ri = _iota((Q, Q), 0)
    ci = _iota((Q, Q), 1)
    tri = (ci >= ri) if rev else (ci <= ri)
    trit = (ci <= ri) if rev else (ci >= ri)
    cs = _dot01_l(tri.astype(BF16), a)
    return raw, bias, arow, mine, dt, cs, tri, trit


def _expand_mat(rev):
    r = np.arange(128)[:, None]
    c = np.arange(DI)[None, :]
    return jnp.asarray(r == (c // HP) + 32 * rev, BF16)


def _sum_mat(rev):
    r = np.arange(DI)[:, None]
    c = np.arange(128)[None, :]
    return jnp.asarray(c == (r // HP) + 32 * rev, BF16)


def _ssd_fwd(xbc, u, par, y_add=None, *, rev):
    t = xbc.shape[0]
    nc = t // Q
    end = 0 if rev else Q - 1
    cmap = (lambda c: nc - 1 - c) if rev else (lambda c: c)

    def body(xbc_ref, dtr_ref, par_ref, ex_ref, *rest):
        yadd_ref = rest[0] if y_add is not None else None
        y_ref, st_ref, h_scr = rest[-3:]
        step = pl.program_id(0)

        @pl.when(step == 0)
        def _():
            h_scr[...] = jnp.zeros((NS, DI), F32)

        raw, bias, arow, mine, dt, cs, tri, trit = _ssd_common(dtr_ref, par_ref, rev)
        cst = cs.T
        dtt = dt.T
        tot_col = cst[:, end:end + 1]
        wt = dtt * jnp.exp(tot_col - cst)
        ecs_all = jnp.exp(cs)
        gam = jnp.exp(cs[end:end + 1, :])
        gam_x = _dot01(jnp.broadcast_to(gam, (8, 128)), ex_ref[...])[0:1, :]
        lane = _iota((Q, 128), 1)
        sel = lane < HP
        st_ref[...] = h_scr[...]
        for g in range(NG):
            bg = xbc_ref[:, DI + NS * g:DI + NS * (g + 1)]
            cg = xbc_ref[:, DI + NG * NS + NS * g:DI + NG * NS + NS * (g + 1)]
            cb = _dot_nt(cg.astype(BF16), bg.astype(BF16))
            bt = bg.T
            for k in range(4):
                lo = 512 * g + 128 * k
                xp = xbc_ref[:, lo:lo + 128].astype(BF16)
                hp = h_scr[:, lo:lo + 128]
                rhs = jnp.concatenate([xp, hp.astype(BF16)], axis=0)
                lhs, bts = [], []
                for j in range(2):
                    hc = 8 * g + 2 * k + j + 32 * rev
                    csc = jnp.broadcast_to(cs[:, hc:hc + 1], (Q, Q))
                    lm = jnp.exp(jnp.where(tri, csc - cst[hc:hc + 1, :], NEG)) * dtt[hc:hc + 1, :]
                    mh = (cb * lm).astype(BF16)
                    ec = (jnp.broadcast_to(ecs_all[:, hc:hc + 1], (Q, NS)) * cg).astype(BF16)
                    lhs.append(jnp.concatenate([mh, ec], axis=1))
                    bts.append((bt * wt[hc:hc + 1, :]).astype(BF16))
                ys = jnp.dot(jnp.concatenate(lhs, axis=0), rhs, preferred_element_type=F32)
                ss = jnp.dot(jnp.concatenate(bts, axis=0), xp, preferred_element_type=F32)
                yp = jnp.where(sel, ys[0:Q], ys[Q:2 * Q])
                y_ref[:, lo:lo + 128] = yp if yadd_ref is None else yp + yadd_ref[:, lo:lo + 128]
                h_scr[:, lo:lo + 128] = gam_x[:, lo:lo + 128] * hp + jnp.where(sel, ss[0:NS], ss[NS:2 * NS])

    return pl.pallas_call(
        body,
        out_shape=(jax.ShapeDtypeStruct((t, DI), F32), jax.ShapeDtypeStruct((nc, NS, DI), F32)),
        grid=(nc,),
        in_specs=[pl.BlockSpec((Q, CONVD), lambda c: (cmap(c), 0)),
                  pl.BlockSpec((Q, 128), lambda c: (cmap(c), ODT // 128)),
                  pl.BlockSpec((8, 128), lambda c: (0, 0)),
                  pl.BlockSpec((128, DI), lambda c: (0, 0))]
        + ([pl.BlockSpec((Q, DI), lambda c: (cmap(c), 0))] if y_add is not None else []),
        out_specs=(pl.BlockSpec((Q, DI), lambda c: (cmap(c), 0)),
                   pl.BlockSpec((None, NS, DI), lambda c: (cmap(c), 0, 0))),
        scratch_shapes=[pltpu.VMEM((NS, DI), F32)],
        name="ssd_fwd_rev" if rev else "ssd_fwd", compiler_params=_params(("arbitrary",)))(
            xbc, u, par, _expand_mat(rev), *([y_add] if y_add is not None else []))


def _ssd_bwd(xbc, u, par, dy, st, *, rev, add=None, side=None):
    t = xbc.shape[0]
    nc = t // Q
    end = 0 if rev else Q - 1
    cmap = (lambda c: c) if rev else (lambda c: nc - 1 - c)

    def body(xbc_ref, dtr_ref, par_ref, dy_ref, hin_ref, ex_ref, sm_ref, *rest):
        addx_ref, addbc_ref, addt_ref = rest[:3] if add is not None else (None, None, None)
        dxs_ref, dbc_ref, ddt_ref, acc_ref, dh_scr = rest[-5:]
        step = pl.program_id(0)

        @pl.when(step == 0)
        def _():
            dh_scr[...] = jnp.zeros((NS, DI), F32)

        raw, bias, arow, mine, dt, cs, tri, trit = _ssd_common(dtr_ref, par_ref, rev)
        ri = _iota((Q, Q), 0)
        ci = _iota((Q, Q), 1)
        stri = ((ri > ci) if rev else (ri < ci)).astype(BF16)
        strit = ((ci > ri) if rev else (ci < ri)).astype(BF16)
        cst = cs.T
        dtt = dt.T
        et = jnp.exp(cst)
        ecs_all = jnp.exp(cs)
        ws_all = jnp.exp(cs[end:end + 1, :] - cs)
        expand = ex_ref[...]
        summat = sm_ref[...]
        gam = jnp.exp(cs[end:end + 1, :])
        gam_x = _dot01(jnp.broadcast_to(gam, (8, 128)), expand)[0:1, :]
        dt_hi, dt_mid, _ = _split3(dt)
        dtx = (jnp.dot(dt_hi, expand, preferred_element_type=F32)
               + jnp.dot(dt_mid, expand, preferred_element_type=F32))
        lane = _iota((Q, 128), 1)
        sel = lane < HP
        dho = dh_scr[...]
        t3 = jnp.sum(dho * hin_ref[...], axis=0, keepdims=True) * gam_x
        dxs_cols, dxs2_cols, yoff_cols, a1_rows = [], [], [], []
        for g in range(NG):
            bg = xbc_ref[:, DI + NS * g:DI + NS * (g + 1)]
            cg = xbc_ref[:, DI + NG * NS + NS * g:DI + NG * NS + NS * (g + 1)]
            bb = bg.astype(BF16)
            cbf = cg.astype(BF16)
            cb = _dot_nt(cbf, bb)
            cbt = _dot_nt(bb, cbf)
            ct = cg.T
            bdh = jnp.dot(bb, dho[:, 512 * g:512 * (g + 1)].astype(BF16), preferred_element_type=F32)
            dcb = jnp.zeros((Q, Q), F32)
            dcg = jnp.zeros((Q, NS), F32)
            dbg = jnp.zeros((Q, NS), F32)
            for k in range(4):
                lo = 512 * g + 128 * k
                xpf = xbc_ref[:, lo:lo + 128]
                xp = xpf.astype(BF16)
                dyp = dy_ref[:, lo:lo + 128]
                dypb = dyp.astype(BF16)
                hinp = hin_ref[:, lo:lo + 128].astype(BF16)
                dhp = dho[:, lo:lo + 128]
                es, ws, lmds, mts, ctes, dyms, ecbs = [], [], [], [], [], [], []
                for j in range(2):
                    hc = 8 * g + 2 * k + j + 32 * rev
                    csc = jnp.broadcast_to(cs[:, hc:hc + 1], (Q, Q))
                    csr = cst[hc:hc + 1, :]
                    lmds.append(jnp.exp(jnp.where(tri, csc - csr, NEG)) * dtt[hc:hc + 1, :])
                    lmb = jnp.exp(jnp.where(trit, csr - csc, NEG))
                    mts.append((cbt * lmb).astype(BF16))
                    dyms.append(jnp.where(sel if j == 0 else ~sel, dyp, 0.0).astype(BF16))
                    ecs = jnp.broadcast_to(ecs_all[:, hc:hc + 1], (Q, NS))
                    es.append(ecs)
                    ws.append(jnp.broadcast_to(ws_all[:, hc:hc + 1], (Q, NS)))
                    ecbs.append((ecs * cg).astype(BF16))
                    ctes.append((ct * et[hc:hc + 1, :]).astype(BF16))
                by_dy = jnp.dot(jnp.concatenate(mts + ctes, axis=0), dypb, preferred_element_type=F32)
                dmm = _dot_nt(jnp.concatenate(dyms, axis=0), xp)
                dm0, dm1 = dmm[0:Q] * lmds[0], dmm[Q:2 * Q] * lmds[1]
                dcb = dcb + dm0 + dm1
                rr = jnp.dot(jnp.concatenate([dm0 * cb, dm1 * cb], axis=0).astype(BF16), stri, preferred_element_type=F32)
                a1_rows.append(jnp.sum(jnp.where(tri, rr[0:Q], 0.0), axis=0, keepdims=True))
                a1_rows.append(jnp.sum(jnp.where(tri, rr[Q:2 * Q], 0.0), axis=0, keepdims=True))
                yo = jnp.dot(jnp.concatenate(ecbs, axis=0), hinp, preferred_element_type=F32)
                e_p = jnp.where(sel, es[0], es[1])
                w_p = jnp.where(sel, ws[0], ws[1])
                d2 = w_p * bdh[:, 128 * k:128 * (k + 1)]
                dxs2_cols.append(d2)
                dxs_cols.append(jnp.where(sel, by_dy[0:Q], by_dy[Q:2 * Q]) + d2)
                yoff_cols.append(jnp.where(sel, yo[0:Q], yo[Q:2 * Q]))
                dcg = dcg + _dot_nt((e_p * dyp).astype(BF16), hinp)
                dbg = dbg + _dot_nt((w_p * dtx[:, lo:lo + 128] * xpf).astype(BF16), dhp.astype(BF16))
                dh_scr[:, lo:lo + 128] = (gam_x[:, lo:lo + 128] * dhp
                                          + jnp.where(sel, by_dy[2 * Q:3 * Q], by_dy[3 * Q:4 * Q]))
            dcg = dcg + jnp.dot(dcb.astype(BF16), bb, preferred_element_type=F32)
            dbg = dbg + jnp.dot(dcb.T.astype(BF16), cbf, preferred_element_type=F32)
            lo_b, lo_c = NS * g, NG * NS + NS * g
            if addbc_ref is not None:
                dbg = dbg + addbc_ref[:, lo_b:lo_b + NS]
                dcg = dcg + addbc_ref[:, lo_c:lo_c + NS]
            dbc_ref[:, lo_b:lo_b + NS] = dbg
            dbc_ref[:, lo_c:lo_c + NS] = dcg
        dxs = jnp.concatenate(dxs_cols, axis=1)
        dxs_ref[...] = dxs * dtx if addx_ref is None else dxs * dtx + addx_ref[...]
        xs = xbc_ref[:, 0:DI]
        stacked = jnp.concatenate([xs * dxs, xs * jnp.concatenate(dxs2_cols, axis=1),
                                   dy_ref[...] * jnp.concatenate(yoff_cols, axis=1),
                                   jnp.broadcast_to(t3, (8, DI))], axis=0).astype(BF16)
        sums = jnp.dot(stacked, summat, preferred_element_type=F32)
        rx, rx2, ryo, c0 = sums[0:Q], sums[Q:2 * Q], sums[2 * Q:3 * Q], sums[3 * Q:3 * Q + 1]
        zero32 = jnp.zeros((32, Q), F32)
        a1t = jnp.concatenate(([zero32] if rev else []) + a1_rows + [zero32] * (2 if rev else 3), axis=0)
        da = (a1t.T + jnp.dot(trit.astype(BF16), ryo.astype(BF16), preferred_element_type=F32)
              + jnp.dot(strit, (dt * rx2).astype(BF16), preferred_element_type=F32) + jnp.where(mine, c0, 0.0))
        ddt = rx + da * arow
        ddtr = ddt * _sigmoid(raw + bias)
        ddt_ref[...] = ddtr if addt_ref is None else ddtr + addt_ref[...]
        part = jnp.concatenate([jnp.sum(ddtr, axis=0, keepdims=True),
                                jnp.sum(da * dt, axis=0, keepdims=True) * arow,
                                jnp.zeros((6, 128), F32)], axis=0)

        @pl.when(step == 0)
        def _():
            acc_ref[...] = part

        @pl.when(step > 0)
        def _():
            acc_ref[...] += part

    outs, side_outs = _host_call(
        body, side, nc,
        out_shape=(jax.ShapeDtypeStruct((t, DI), F32), jax.ShapeDtypeStruct((t, 2 * NG * NS), F32),
                   jax.ShapeDtypeStruct((t, 128), F32), jax.ShapeDtypeStruct((8, 128), F32)),
        in_specs=[pl.BlockSpec((Q, CONVD), lambda c: (cmap(c), 0)),
                  pl.BlockSpec((Q, 128), lambda c: (cmap(c), ODT // 128)),
                  pl.BlockSpec((8, 128), lambda c: (0, 0)),
                  pl.BlockSpec((Q, DI), lambda c: (cmap(c), 0)),
                  pl.BlockSpec((None, NS, DI), lambda c: (cmap(c), 0, 0)),
                  pl.BlockSpec((128, DI), lambda c: (0, 0)), pl.BlockSpec((DI, 128), lambda c: (0, 0))]
        + ([pl.BlockSpec((Q, DI), lambda c: (cmap(c), 0)), pl.BlockSpec((Q, 2 * NG * NS), lambda c: (cmap(c), 0)),
            pl.BlockSpec((Q, 128), lambda c: (cmap(c), 0))] if add is not None else []),
        out_specs=(pl.BlockSpec((Q, DI), lambda c: (cmap(c), 0)),
                   pl.BlockSpec((Q, 2 * NG * NS), lambda c: (cmap(c), 0)),
                   pl.BlockSpec((Q, 128), lambda c: (cmap(c), 0)),
                   pl.BlockSpec((8, 128), lambda c: (0, 0))),
        scratch_shapes=[pltpu.VMEM((NS, DI), F32)],
        args=(xbc, u, par, dy, st, _expand_mat(rev), _sum_mat(rev)) + (tuple(add) if add is not None else ()), aliases={},
        name="ssd_bwd_rev" if rev else "ssd_bwd", sem=("arbitrary",))
    return (*outs, side_outs)


GN_TM = 256
GN_GROUP = DI // NG


def _gn_forward_vals(y0, xs, z, dsk):
    y = y0 + dsk * xs
    sz = _sigmoid(z)
    gate = z * sz
    y2 = y * gate
    parts, rs = [], []
    for g in range(NG):
        seg = y2[:, GN_GROUP * g:GN_GROUP * (g + 1)]
        r = lax.rsqrt(jnp.mean(seg * seg, axis=1, keepdims=True) + NORM_EPS)
        rs.append(r)
        parts.append(seg * r)
    yn = jnp.concatenate(parts, axis=1)
    return y, sz, gate, yn, rs


def _gatenorm_fwd(y_fb, xbc, u, dsk_row, nw_row):
    t = y_fb.shape[0]
    tm = GN_TM

    def body(y_ref, xs_ref, z_ref, dsk_ref, nw_ref, o_ref):
        _, _, _, yn, _ = _gn_forward_vals(y_ref[...], xs_ref[...], z_ref[...], dsk_ref[...])
        o_ref[...] = (yn * nw_ref[...]).astype(BF16)

    blk = pl.BlockSpec((tm, DI), lambda i: (i, 0))
    row = pl.BlockSpec((1, DI), lambda i: (0, 0))
    return pl.pallas_call(
        body, out_shape=jax.ShapeDtypeStruct((t, DI), BF16), grid=(t // tm,),
        in_specs=[blk, blk, pl.BlockSpec((tm, DI), lambda i: (i, OZ // DI)), row, row],
        out_specs=blk, name="gatenorm_fwd", compiler_params=_params(("parallel",)))(y_fb, xbc, u, dsk_row, nw_row)


def _gatenorm_bwd(ds_out, y_fb, xbc, u, du, dsk_row, nw_row, side=None):
    t = y_fb.shape[0]
    tm = GN_TM

    def body(ds_ref, y_ref, xs_ref, z_ref, dsk_ref, nw_ref, sm_ref, du_in, dy_ref, du_out, dnw_ref, dds_ref):
        del du_in
        i = pl.program_id(0)
        xs = xs_ref[...]
        z = z_ref[...]
        y, sz, gate, yn, rs = _gn_forward_vals(y_ref[...], xs, z, dsk_ref[...])
        ds = ds_ref[...]
        gsc = ds * nw_ref[...]
        parts = []
        for g in range(NG):
            sl = slice(GN_GROUP * g, GN_GROUP * (g + 1))
            m = jnp.mean(gsc[:, sl] * yn[:, sl], axis=1, keepdims=True)
            parts.append(rs[g] * (gsc[:, sl] - yn[:, sl] * m))
        dy2 = jnp.concatenate(parts, axis=1)
        dy = dy2 * gate
        dy_ref[...] = dy
        du_out[...] = (dy2 * y * (sz * (1.0 + z * (1.0 - sz)))).astype(du_out.dtype)
        dnw = jnp.broadcast_to(jnp.sum(ds * yn, axis=0, keepdims=True), (8, DI))
        drow = jnp.broadcast_to(jnp.sum(dy * xs, axis=0, keepdims=True), (8, DI))
        dds = _dot01(drow, sm_ref[...])

        @pl.when(i == 0)
        def _():
            dnw_ref[...] = dnw
            dds_ref[...] = dds

        @pl.when(i > 0)
        def _():
            dnw_ref[...] += dnw
            dds_ref[...] += dds

    blk = pl.BlockSpec((tm, DI), lambda i: (i, 0))
    row = pl.BlockSpec((1, DI), lambda i: (0, 0))
    outs, side_outs = _host_call(
        body, side, t // tm,
        out_shape=(jax.ShapeDtypeStruct((t, DI), F32), jax.ShapeDtypeStruct(du.shape, du.dtype),
                   jax.ShapeDtypeStruct((8, DI), F32), jax.ShapeDtypeStruct((8, 128), F32)),
        in_specs=[blk, blk, blk, pl.BlockSpec((tm, DI), lambda i: (i, OZ // DI)), row, row,
                  pl.BlockSpec((DI, 128), lambda i: (0, 0)), pl.BlockSpec(memory_space=pl.ANY)],
        out_specs=(blk, pl.BlockSpec((tm, DI), lambda i: (i, OZ // DI)),
                   pl.BlockSpec((8, DI), lambda i: (0, 0)), pl.BlockSpec((8, 128), lambda i: (0, 0))),
        scratch_shapes=[], args=(ds_out, y_fb, xbc, u, dsk_row, nw_row, _sum_mat(0), du), aliases={7: 1},
        name="gatenorm_bwd", sem=("arbitrary",))
    return (*outs, side_outs)


AT_B = 128
AT_W = AT_B + 2 * ATT_HALF
AT_L = 2 * AH
SCALE = 1.0 / math.sqrt(AH)


def _slope(g, hh):
    return 2.0 ** (-8.0 * (4 * g + hh + 1) / 12.0)


def _qcol(g):
    return lambda p: OQ // AT_L + 2 * g + p


def _kcol(g):
    return lambda p: OKV // AT_L + 4 * g + 2 * p


def _vcol(g):
    return lambda p: OKV // AT_L + 4 * g + 2 * p + 1


def _pcol(p):
    return p


def _sub(d):
    return 4 if d == 1 else 1


def _win_specs(col, t, d):
    tb, hb = AT_B * d * _sub(d), ATT_HALF * d
    per = tb // hb
    nh = t // hb
    return [
        pl.BlockSpec((hb, AT_L), lambda p, i: (jnp.maximum(per * i - 1, 0), col(p))),
        pl.BlockSpec((tb, AT_L), lambda p, i: (i, col(p))),
        pl.BlockSpec((hb, AT_L), lambda p, i: (jnp.minimum(per * (i + 1), nh - 1), col(p))),
    ]


def _blk_spec(col, d):
    return pl.BlockSpec((AT_B * d * _sub(d), AT_L), lambda p, i: (i, col(p)))


def _rows(ref, r, s, d):
    return ref[pl.ds(r, AT_B, stride=d), :] if d > 1 else ref[AT_B * s:AT_B * (s + 1), :]


def _win(p_ref, c_ref, n_ref, r, s, d):
    if d > 1:
        return jnp.concatenate([p_ref[pl.ds(r, ATT_HALF, stride=d), :], c_ref[pl.ds(r, AT_B, stride=d), :],
                                n_ref[pl.ds(r, ATT_HALF, stride=d), :]], axis=0)
    if s == 0:
        return jnp.concatenate([p_ref[...], c_ref[0:AT_B + ATT_HALF, :]], axis=0)
    if s == _sub(d) - 1:
        return jnp.concatenate([c_ref[AT_B * s - ATT_HALF:AT_B * (s + 1), :], n_ref[...]], axis=0)
    return c_ref[AT_B * s - ATT_HALF:AT_B * (s + 1) + ATT_HALF, :]


def _put_rows(ref, r, s, d, val):
    if d > 1:
        ref[pl.ds(r, AT_B, stride=d), :] = val
    else:
        ref[AT_B * s:AT_B * (s + 1), :] = val


def _for_blocks(d, fn):
    if d == 1:
        for s in range(_sub(d)):
            fn(0, s)
    else:
        def step(r, c):
            fn(r, 0)
            return c
        lax.fori_loop(0, d, step, 0, unroll=4)


def _attn_bias(blk, ln, d, g, p_id):
    a = blk * AT_B + _iota((AT_B, AT_W), 0)
    b = blk * AT_B - ATT_HALF + _iota((AT_B, AT_W), 1)
    rel = jnp.abs(a - b)
    valid = (rel <= ATT_HALF) & (b >= 0) & (b < ln)
    dist = (rel * d).astype(F32)
    out = []
    for hh in range(2):
        slope = jnp.where(p_id == 0, _slope(g, hh), _slope(g, 2 + hh))
        out.append(jnp.where(valid, -slope * dist, NEG))
    return out


def _attn_fwd(u, g):
    t = u.shape[0]
    d = DILATIONS[g]
    ln = t // d

    def body(q_ref, kp, kc, kn, vp, vc, vn, o_ref, l_ref):
        p_id = pl.program_id(0)
        i = pl.program_id(1)
        lane = _iota((AT_B, AT_L), 1)
        biases = [_attn_bias(i * _sub(d) + s, ln, d, g, p_id) for s in range(_sub(d))]

        def one(r, s):
            q = _rows(q_ref, r, s, d) * SCALE
            kw = _win(kp, kc, kn, r, s, d).astype(BF16)
            vw = _win(vp, vc, vn, r, s, d).astype(BF16)
            o = jnp.zeros((AT_B, AT_L), F32)
            lse = jnp.zeros((AT_B, AT_L), F32)
            for hh in range(2):
                hm = (lane // AH) == hh
                qm = jnp.where(hm, q, 0.0).astype(BF16)
                sc = _dot_nt(qm, kw) + biases[s][hh]
                m = jnp.max(sc, axis=1, keepdims=True)
                pr = jnp.exp(sc - m)
                den = jnp.sum(pr, axis=1, keepdims=True)
                oh = jnp.dot(pr.astype(BF16), vw, preferred_element_type=F32)
                o = jnp.where(hm, oh / den, o)
                lse = jnp.where(hm, m + jnp.log(den), lse)
            _put_rows(o_ref, r, s, d, o)
            _put_rows(l_ref, r, s, d, lse)

        _for_blocks(d, one)

    oshape = jax.ShapeDtypeStruct((t, 2 * AT_L), F32)
    ospec = _blk_spec(_pcol, d)
    return pl.pallas_call(
        body, out_shape=(oshape, oshape), grid=(2, t // (AT_B * d * _sub(d))),
        in_specs=[_blk_spec(_qcol(g), d)] + _win_specs(_kcol(g), t, d) + _win_specs(_vcol(g), t, d),
        out_specs=(ospec, ospec), name=f"attn_fwd_{g}", compiler_params=_params(("parallel", "parallel")))(
            u, u, u, u, u, u, u)


def _attn_dq(u, du, do, lse, e, g):
    t = u.shape[0]
    d = DILATIONS[g]
    ln = t // d

    def body(q_ref, kp, kc, kn, vp, vc, vn, do_ref, l_ref, e_ref, du_in, dq_ref, dq_scr):
        del du_in
        p_id = pl.program_id(0)
        i = pl.program_id(1)
        lane = _iota((AT_B, AT_L), 1)
        biases = [_attn_bias(i * _sub(d) + s, ln, d, g, p_id) for s in range(_sub(d))]

        def one(r, s):
            q = _rows(q_ref, r, s, d) * SCALE
            kw = _win(kp, kc, kn, r, s, d).astype(BF16)
            vw = _win(vp, vc, vn, r, s, d).astype(BF16)
            do_ = _rows(do_ref, r, s, d)
            lv = _rows(l_ref, r, s, d)
            ev = _rows(e_ref, r, s, d)
            dq = jnp.zeros((AT_B, AT_L), F32)
            for hh in range(2):
                hm = (lane // AH) == hh
                qm = jnp.where(hm, q, 0.0).astype(BF16)
                sc = _dot_nt(qm, kw) + biases[s][hh]
                lcol = jnp.broadcast_to(lv[:, AH * hh:AH * hh + 1], (AT_B, AT_W))
                ecol = jnp.broadcast_to(ev[:, AH * hh:AH * hh + 1], (AT_B, AT_W))
                pr = jnp.exp(sc - lcol)
                dom = jnp.where(hm, do_, 0.0).astype(BF16)
                ds = pr * (_dot_nt(dom, vw) + ecol)
                dqh = jnp.dot(ds.astype(BF16), kw, preferred_element_type=F32) * SCALE
                dq = jnp.where(hm, dqh, dq)
            _put_rows(dq_scr, r, s, d, dq)

        _for_blocks(d, one)
        dq_ref[...] = dq_scr[...].astype(dq_ref.dtype)

    rspec = _blk_spec(_pcol, d)
    return pl.pallas_call(
        body, out_shape=jax.ShapeDtypeStruct(du.shape, du.dtype), grid=(2, t // (AT_B * d * _sub(d))),
        in_specs=[_blk_spec(_qcol(g), d)] + _win_specs(_kcol(g), t, d) + _win_specs(_vcol(g), t, d)
        + [rspec, rspec, rspec, pl.BlockSpec(memory_space=pl.ANY)],
        out_specs=_blk_spec(_qcol(g), d), input_output_aliases={10: 0},
        scratch_shapes=[pltpu.VMEM((AT_B * d * _sub(d), AT_L), F32)],
        name=f"attn_dq_{g}", compiler_params=_params(("parallel", "parallel")))(
            u, u, u, u, u, u, u, do, lse, e, du)


def _attn_dkv(u, du, do, lse, e, g):
    t = u.shape[0]
    d = DILATIONS[g]
    ln = t // d

    def body(k_ref, v_ref, qp, qc, qn, dp_, dc_, dn_, lp, lc, ln_, ep, ec, en, du_in, dkv_ref, dk_scr, dv_scr):
        del du_in
        p_id = pl.program_id(0)
        jb = pl.program_id(1)
        lane = _iota((AT_B, AT_L), 1)
        biases = [_attn_bias(jb * _sub(d) + s, ln, d, g, p_id) for s in range(_sub(d))]

        def one(r, s):
            k = _rows(k_ref, r, s, d) * SCALE
            v = _rows(v_ref, r, s, d)
            qw = _win(qp, qc, qn, r, s, d).astype(BF16)
            dow = _win(dp_, dc_, dn_, r, s, d).astype(BF16)
            lt = _win(lp, lc, ln_, r, s, d).T
            et = _win(ep, ec, en, r, s, d).T
            dk = jnp.zeros((AT_B, AT_L), F32)
            dv = jnp.zeros((AT_B, AT_L), F32)
            for hh in range(2):
                hm = (lane // AH) == hh
                km = jnp.where(hm, k, 0.0).astype(BF16)
                st = _dot_nt(km, qw) + biases[s][hh]
                pt = jnp.exp(st - lt[AH * hh:AH * hh + 1, :])
                dvh = jnp.dot(pt.astype(BF16), dow, preferred_element_type=F32)
                vm = jnp.where(hm, v, 0.0).astype(BF16)
                dst = pt * (_dot_nt(vm, dow) + et[AH * hh:AH * hh + 1, :])
                dkh = jnp.dot(dst.astype(BF16), qw, preferred_element_type=F32) * SCALE
                dk = jnp.where(hm, dkh, dk)
                dv = jnp.where(hm, dvh, dv)
            _put_rows(dk_scr, r, s, d, dk)
            _put_rows(dv_scr, r, s, d, dv)

        _for_blocks(d, one)
        dkv_ref[:, 0:AT_L] = dk_scr[...].astype(dkv_ref.dtype)
        dkv_ref[:, AT_L:2 * AT_L] = dv_scr[...].astype(dkv_ref.dtype)

    return pl.pallas_call(
        body, out_shape=jax.ShapeDtypeStruct(du.shape, du.dtype), grid=(2, t // (AT_B * d * _sub(d))),
        in_specs=[_blk_spec(_kcol(g), d), _blk_spec(_vcol(g), d)]
        + _win_specs(_qcol(g), t, d) + _win_specs(_pcol, t, d) + _win_specs(_pcol, t, d) + _win_specs(_pcol, t, d)
        + [pl.BlockSpec(memory_space=pl.ANY)],
        out_specs=pl.BlockSpec((AT_B * d * _sub(d), 2 * AT_L), lambda p, i: (i, OKV // (2 * AT_L) + 2 * g + p)),
        input_output_aliases={14: 0},
        scratch_shapes=[pltpu.VMEM((AT_B * d * _sub(d), AT_L), F32), pltpu.VMEM((AT_B * d * _sub(d), AT_L), F32)],
        name=f"attn_dkv_{g}", compiler_params=_params(("parallel", "parallel")))(
            u, u, u, u, u, do, do, do, lse, lse, lse, e, e, e, du)


def _combine_weights(l0, l1, l2):
    m = jnp.maximum(jnp.maximum(l0, l1), l2)
    e0, e1, e2 = jnp.exp(l0 - m), jnp.exp(l1 - m), jnp.exp(l2 - m)
    inv = 1.0 / (e0 + e1 + e2)
    return e0 * inv, e1 * inv, e2 * inv


def _combine_proj(os_, ls_, w_pa):
    t = os_[0].shape[0]
    tm = ROW_TM
    nsh, _, ws = w_pa.shape

    def body(o0, o1, o2, l0, l1, l2, w_ref, a_ref, y_ref):
        w0, w1, w2 = _combine_weights(l0[...], l1[...], l2[...])
        att = w0 * o0[...] + w1 * o1[...] + w2 * o2[...]
        a_ref[...] = att
        ab = att.astype(BF16)
        for sh in range(nsh):
            y_ref[:, ws * sh:ws * (sh + 1)] = jnp.dot(ab, w_ref[sh], preferred_element_type=F32)

    blk = pl.BlockSpec((tm, 2 * AT_L), lambda i: (i, 0))
    return pl.pallas_call(
        body, out_shape=(jax.ShapeDtypeStruct((t, 2 * AT_L), F32), jax.ShapeDtypeStruct((t, nsh * ws), F32)),
        grid=(t // tm,), in_specs=[blk] * 6 + [pl.BlockSpec(w_pa.shape, lambda i: (0, 0, 0))],
        out_specs=(blk, pl.BlockSpec((tm, nsh * ws), lambda i: (i, 0))),
        name="combine_proj", compiler_params=_params(("parallel",)))(*os_, *ls_, w_pa)


def _d_att_combine_bwd(dy_att, w_pa, os_, ls_):
    t = dy_att.shape[0]
    tm = ROW_TM
    nsh, _, ws = w_pa.shape

    def body(dy_ref, w_ref, o0, o1, o2, l0, l1, l2, d0, d1, d2, e0, e1, e2):
        da = jnp.zeros((tm, 2 * AT_L), F32)
        for sh in range(nsh):
            da = da + _dot_nt(dy_ref[:, ws * sh:ws * (sh + 1)], w_ref[sh])
        w = _combine_weights(l0[...], l1[...], l2[...])
        att = w[0] * o0[...] + w[1] * o1[...] + w[2] * o2[...]
        r = _iota((2 * AT_L, 2 * AT_L), 0) // AH
        c = _iota((2 * AT_L, 2 * AT_L), 1) // AH
        hs = _dot01(da * att, (r == c).astype(BF16))
        for wg, dref, eref in zip(w, (d0, d1, d2), (e0, e1, e2)):
            dref[...] = wg * da
            eref[...] = -wg * hs

    blk = pl.BlockSpec((tm, 2 * AT_L), lambda i: (i, 0))
    shp = jax.ShapeDtypeStruct((t, 2 * AT_L), F32)
    outs = pl.pallas_call(
        body, out_shape=(shp,) * 6, grid=(t // tm,),
        in_specs=[pl.BlockSpec((tm, nsh * ws), lambda i: (i, 0)), pl.BlockSpec(w_pa.shape, lambda i: (0, 0, 0))] + [blk] * 6,
        out_specs=(blk,) * 6, name="d_att_combine_bwd", compiler_params=_params(("parallel",)))(dy_att, w_pa, *os_, *ls_)
    return outs[0:3], outs[3:6]


ROW_TM = 512


def _ln(x, g, b):
    mu = jnp.mean(x, axis=1, keepdims=True)
    xc = x - mu
    var = jnp.mean(xc * xc, axis=1, keepdims=True)
    rstd = lax.rsqrt(var + NORM_EPS)
    xhat = xc * rstd
    return xhat * g + b, xhat, rstd


def _ln_back(dh, xhat, rstd, g):
    dxh = dh * g
    m1 = jnp.mean(dxh, axis=1, keepdims=True)
    m2 = jnp.mean(dxh * xhat, axis=1, keepdims=True)
    return rstd * (dxh - m1 - xhat * m2)


def _mlp_up(h1, w_up):
    t = h1.shape[0]
    tm, tn = 2 * ROW_TM, D

    def body(a_ref, b_ref, up_ref, act_ref):
        up = jnp.dot(a_ref[...], b_ref[...], preferred_element_type=F32)
        up_ref[...] = up.astype(BF16)
        r = jnp.maximum(up, 0.0)
        act_ref[...] = (r * r).astype(BF16)

    blk = pl.BlockSpec((tm, tn), lambda j, i: (i, j))
    return pl.pallas_call(
        body, out_shape=(jax.ShapeDtypeStruct((t, DFF), BF16), jax.ShapeDtypeStruct((t, DFF), BF16)),
        grid=(DFF // tn, t // tm),
        in_specs=[pl.BlockSpec((tm, D), lambda j, i: (i, 0)), pl.BlockSpec((None, D, tn), lambda j, i: (j, 0, 0))],
        out_specs=(blk, blk), name="mlp_up", compiler_params=_params(("parallel", "parallel")))(h1, w_up)


def _d_up(dpre2, w_down, up):
    t = up.shape[0]
    tm, tk = 2 * ROW_TM, D

    def body(a_ref, b_ref, u_ref, o_ref):
        dact = _dot_nt(a_ref[...], b_ref[...])
        o_ref[...] = (dact * 2.0 * jnp.maximum(u_ref[...].astype(F32), 0.0)).astype(BF16)

    blk = pl.BlockSpec((tm, tk), lambda j, i: (i, j))
    return pl.pallas_call(
        body, out_shape=jax.ShapeDtypeStruct((t, DFF), BF16), grid=(DFF // tk, t // tm),
        in_specs=[pl.BlockSpec((tm, D), lambda j, i: (i, 0)), pl.BlockSpec((tk, D), lambda j, i: (j, 0)), blk],
        out_specs=blk, name="d_up", compiler_params=_params(("parallel", "parallel")))(dpre2, w_down, up)


def _dt_bwd(du, ddt):
    t = ddt.shape[0]
    tm = 1024

    def body(f_ref, du_in, o_ref):
        del du_in
        o_ref[:, 0:128] = f_ref[...].astype(o_ref.dtype)
        o_ref[:, 128:256] = jnp.zeros((tm, 128), o_ref.dtype)

    blk = pl.BlockSpec((tm, 128), lambda i: (i, 0))
    return pl.pallas_call(
        body, out_shape=jax.ShapeDtypeStruct(du.shape, du.dtype), grid=(t // tm,),
        in_specs=[blk, pl.BlockSpec(memory_space=pl.ANY)],
        out_specs=pl.BlockSpec((tm, 256), lambda i: (i, ODT // 256)), input_output_aliases={1: 0},
        name="dt_bwd", compiler_params=_params(("parallel",)))(ddt, du)


def _mix_out_ln1(y_ssd, y_att, u, bg_row, x, w_out, g_row, b_row):
    t = x.shape[0]
    tm = ROW_TM

    def body(ys_ref, ya_ref, g0_ref, g1_ref, b0_ref, b1_ref, x_ref, w_ref, g_ref, b_ref, mixin_ref, pre_ref, h_ref):
        g0 = _sigmoid(g0_ref[...] + b0_ref[...])
        g1 = _sigmoid(g1_ref[...] + b1_ref[...])
        mixin = (g0 * ys_ref[...] + g1 * ya_ref[...]).astype(BF16)
        mixin_ref[...] = mixin
        pre = ALPHA * x_ref[...] + jnp.dot(mixin, w_ref[...], preferred_element_type=F32)
        pre_ref[...] = pre
        h, _, _ = _ln(pre, g_ref[...], b_ref[...])
        h_ref[...] = h.astype(BF16)

    blk = pl.BlockSpec((tm, D), lambda i: (i, 0))
    row = pl.BlockSpec((1, D), lambda i: (0, 0))
    return pl.pallas_call(
        body,
        out_shape=(jax.ShapeDtypeStruct((t, D), BF16), jax.ShapeDtypeStruct((t, D), F32), jax.ShapeDtypeStruct((t, D), BF16)),
        grid=(t // tm,),
        in_specs=[blk, blk, pl.BlockSpec((tm, D), lambda i: (i, OGATE // D)), pl.BlockSpec((tm, D), lambda i: (i, OGATE // D + 1)),
                  row, pl.BlockSpec((1, D), lambda i: (0, 1)), blk, pl.BlockSpec((D, D), lambda i: (0, 0)), row, row],
        out_specs=(blk, blk, blk), name="mix_out_ln1", compiler_params=_params(("parallel",)))(
            y_ssd, y_att, u, u, bg_row, bg_row, x, w_out, g_row, b_row)


def _mlp_down_ln2_loss(act, w_down, pre1, tgt, g1_row, b1_row, g2_row, b2_row):
    t = pre1.shape[0]
    tm = ROW_TM

    def body(a_ref, w_ref, p1_ref, t_ref, g1_ref, b1_ref, g2_ref, b2_ref, dpre_ref, dpreb_ref, acc_ref):
        i = pl.program_id(0)
        f = jnp.dot(a_ref[...], w_ref[...], preferred_element_type=F32)
        h1, _, _ = _ln(p1_ref[...], g1_ref[...], b1_ref[...])
        pre2 = ALPHA * h1 + f
        h2, xhat, rstd = _ln(pre2, g2_ref[...], b2_ref[...])
        err = h2 - t_ref[...]
        dh = err * (1.0 / D)
        dpre = _ln_back(dh, xhat, rstd, g2_ref[...])
        dpre_ref[...] = dpre
        dpreb_ref[...] = dpre.astype(BF16)
        loss = jnp.sum(jnp.sum(err * err, axis=1, keepdims=True), axis=0, keepdims=True) * (0.5 / D)
        part = jnp.concatenate([jnp.sum(dh * xhat, axis=0, keepdims=True), jnp.sum(dh, axis=0, keepdims=True),
                                jnp.broadcast_to(loss, (1, D)), jnp.zeros((5, D), F32)], axis=0)

        @pl.when(i == 0)
        def _():
            acc_ref[...] = part

        @pl.when(i > 0)
        def _():
            acc_ref[...] += part

    blk = pl.BlockSpec((tm, D), lambda i: (i, 0))
    row = pl.BlockSpec((1, D), lambda i: (0, 0))
    return pl.pallas_call(
        body,
        out_shape=(jax.ShapeDtypeStruct((t, D), F32), jax.ShapeDtypeStruct((t, D), BF16), jax.ShapeDtypeStruct((8, D), F32)),
        grid=(t // tm,),
        in_specs=[pl.BlockSpec((tm, DFF), lambda i: (i, 0)), pl.BlockSpec((DFF, D), lambda i: (0, 0)), blk, blk, row, row, row, row],
        out_specs=(blk, blk, pl.BlockSpec((8, D), lambda i: (0, 0))),
        name="mlp_down_ln2_loss", compiler_params=_params(("arbitrary",)))(act, w_down, pre1, tgt, g1_row, b1_row, g2_row, b2_row)


def _d_h1_ln1_bwd(dup, w_up, dpre2, pre1, g_row, b_row):
    t = dup.shape[0]
    tm = ROW_TM
    nsh = w_up.shape[0]

    def body(a_ref, w_ref, add_ref, pre_ref, g_ref, b_ref, dpre_ref, acc_ref):
        i = pl.program_id(0)
        dh_ = ALPHA * add_ref[...]
        for sh in range(nsh):
            dh_ = dh_ + _dot_nt(a_ref[:, D * sh:D * (sh + 1)], w_ref[sh])
        _, xhat, rstd = _ln(pre_ref[...], g_ref[...], b_ref[...])
        dpre_ref[...] = _ln_back(dh_, xhat, rstd, g_ref[...])
        rows = jnp.concatenate([jnp.sum(dh_ * xhat, axis=0, keepdims=True), jnp.sum(dh_, axis=0, keepdims=True),
                                jnp.zeros((6, D), F32)], axis=0)

        @pl.when(i == 0)
        def _():
            acc_ref[...] = rows

        @pl.when(i > 0)
        def _():
            acc_ref[...] += rows

    blk = pl.BlockSpec((tm, D), lambda i: (i, 0))
    row = pl.BlockSpec((1, D), lambda i: (0, 0))
    return pl.pallas_call(
        body, out_shape=(jax.ShapeDtypeStruct((t, D), F32), jax.ShapeDtypeStruct((8, D), F32)),
        grid=(t // tm,),
        in_specs=[pl.BlockSpec((tm, nsh * D), lambda i: (i, 0)), pl.BlockSpec(w_up.shape, lambda i: (0, 0, 0)),
                  blk, blk, row, row],
        out_specs=(blk, pl.BlockSpec((8, D), lambda i: (0, 0))),
        name="d_h1_ln1_bwd", compiler_params=_params(("arbitrary",)))(dup, w_up, dpre2, pre1, g_row, b_row)


def _d_mixin_mix_bwd(dpre1, w_out, y_ssd, y_att, u, bg_row):
    t = y_ssd.shape[0]
    tm = ROW_TM

    def body(a_ref, w_ref, ys_ref, ya_ref, g0_ref, g1_ref, b0_ref, b1_ref, dys_ref, dya_ref, du_ref, db_ref):
        i = pl.program_id(0)
        dm = _dot_nt(a_ref[...].astype(BF16), w_ref[...])
        g0 = _sigmoid(g0_ref[...] + b0_ref[...])
        g1 = _sigmoid(g1_ref[...] + b1_ref[...])
        dys_ref[...] = (dm * g0).astype(BF16)
        dya_ref[...] = (dm * g1).astype(BF16)
        dl0 = dm * ys_ref[...] * g0 * (1.0 - g0)
        dl1 = dm * ya_ref[...] * g1 * (1.0 - g1)
        du_ref[:, 0:D] = dl0.astype(BF16)
        du_ref[:, D:2 * D] = dl1.astype(BF16)
        part = jnp.concatenate([jnp.broadcast_to(jnp.sum(dl0, axis=0, keepdims=True), (8, D)),
                                jnp.broadcast_to(jnp.sum(dl1, axis=0, keepdims=True), (8, D))], axis=1)

        @pl.when(i == 0)
        def _():
            db_ref[...] = part

        @pl.when(i > 0)
        def _():
            db_ref[...] += part

    blk = pl.BlockSpec((tm, D), lambda i: (i, 0))
    return pl.pallas_call(
        body,
        out_shape=(jax.ShapeDtypeStruct((t, D), BF16), jax.ShapeDtypeStruct((t, D), BF16),
                   jax.ShapeDtypeStruct((t, UW), BF16), jax.ShapeDtypeStruct((8, 2 * D), F32)),
        grid=(t // tm,),
        in_specs=[blk, pl.BlockSpec((D, D), lambda i: (0, 0)), blk, blk,
                  pl.BlockSpec((tm, D), lambda i: (i, OGATE // D)), pl.BlockSpec((tm, D), lambda i: (i, OGATE // D + 1)),
                  pl.BlockSpec((1, D), lambda i: (0, 0)), pl.BlockSpec((1, D), lambda i: (0, 1))],
        out_specs=(blk, blk, pl.BlockSpec((tm, 2 * D), lambda i: (i, OGATE // (2 * D))),
                   pl.BlockSpec((8, 2 * D), lambda i: (0, 0))),
        name="d_mixin_mix_bwd", compiler_params=_params(("arbitrary",)))(dpre1, w_out, y_ssd, y_att, u, u, bg_row, bg_row)


def _adamw(w, g, m, v, name):
    r, c = w.shape
    tr, tc = r, c
    for cand in (256, 128, 64, 32, 16, 8):
        if r % cand == 0 and cand * c * 4 <= 2 ** 21:
            tr = cand
            break
    if tr < 64 and c % 256 == 0:
        tr, tc = r, 256
    bc1 = 1.0 / (1.0 - ADAM_B1 ** ADAM_STEP)
    bc2 = 1.0 / (1.0 - ADAM_B2 ** ADAM_STEP)

    def body(w_ref, g_ref, m_ref, v_ref, d_ref, nm_ref, nv_ref):
        gg = g_ref[...]
        nm = ADAM_B1 * m_ref[...] + (1.0 - ADAM_B1) * gg
        nv = ADAM_B2 * v_ref[...] + (1.0 - ADAM_B2) * (gg * gg)
        nm_ref[...] = nm
        nv_ref[...] = nv
        d_ref[...] = -ADAM_LR * ((nm * bc1) / (jnp.sqrt(nv * bc2) + ADAM_EPS) + ADAM_WD * w_ref[...])

    blk = pl.BlockSpec((tr, tc), lambda i, j: (i, j))
    shp = jax.ShapeDtypeStruct((r, c), F32)
    return pl.pallas_call(body, out_shape=(shp, shp, shp), grid=(r // tr, c // tc), in_specs=[blk] * 4,
                          out_specs=(blk,) * 3, name=name, compiler_params=_params(("parallel", "parallel")))(w, g, m, v)


def _segments():
    segs = [(0, 2048), (7488, 9536), (2048, 5120)]
    for g in range(3):
        for p in range(2):
            lo = 256 * g + 128 * p
            segs += [(5952 + lo, 5952 + lo + 128), (6720 + lo, 6720 + lo + 128)]
    segs += [(5184, 5952), (5120, 5184)]
    out, pos = [], 0
    for a, b in segs:
        out.append((a, b, pos))
        pos += b - a
    return out


SHARD_COLS = IN_COLS // 4


def _perm_from_shards(w_shards):
    pieces = []
    for a, b, _ in _segments():
        while a < b:
            s = a // SHARD_COLS
            e = min(b, (s + 1) * SHARD_COLS)
            pieces.append(w_shards[s][:, a - s * SHARD_COLS:e - s * SHARD_COLS])
            a = e
    pieces.append(jnp.zeros((w_shards.shape[1], UW - IN_COLS), w_shards.dtype))
    return jnp.concatenate(pieces, axis=1)


def _shards_from_perm(wp):
    segs = sorted(_segments())
    shards = []
    for s in range(4):
        lo, hi = s * SHARD_COLS, (s + 1) * SHARD_COLS
        pieces = []
        for a, b, pos in segs:
            x, y = max(a, lo), min(b, hi)
            if x < y:
                pieces.append(wp[:, pos + x - a:pos + y - a])
        shards.append(jnp.concatenate(pieces, axis=1))
    return jnp.stack(shards)


def _lanes128(*vecs):
    v = jnp.concatenate([a.reshape(-1) for a in vecs])
    return jnp.pad(v, (0, 128 - v.shape[0])).reshape(1, 128)


EARLY = ("w_proj_ssd", "w_proj_attn", "w_out", "w_up", "w_down")


def _weights_of(gathered):
    g_ps, g_pa, g_o, g_up, g_dn = gathered
    return {"w_proj_ssd": g_ps.reshape(DI, D), "w_proj_attn": g_pa, "w_out": g_o.reshape(D, D), "w_up": g_up,
            "w_down": g_dn.reshape(DFF, D)}


def _local_grads(x, tgt, wts, sm, rs_idx=None):
    row = lambda a: a.reshape(1, -1)
    bg_row, cb_row = row(sm["b_gate"]), row(sm["conv_b"])
    par = jnp.concatenate([_lanes128(sm["dt_bias_f"], sm["dt_bias_b"]), _lanes128(sm["a_log_f"], sm["a_log_b"]),
                           jnp.zeros((6, 128), F32)], axis=0)
    dsk_row = row(jnp.repeat(sm["d_skip"], HP))
    nw_row = row(sm["ssd_norm_w"])
    g1, b1, g2, b2 = row(sm["ln1_g"]), row(sm["ln1_b"]), row(sm["ln2_g"]), row(sm["ln2_b"])

    xb = x.astype(BF16)
    u, gathered = _in_proj(xb, wts["w_in_p"], side=_gather_side(wts["pending"]) if "pending" in wts else None)
    if gathered:
        wts = {**wts, **_weights_of(gathered)}
    xbc = _conv_fwd(u, sm["conv_w"], cb_row)
    y_f, st_f = _ssd_fwd(xbc, u, par, rev=False)
    y_fb, st_b = _ssd_fwd(xbc, u, par, y_f, rev=True)
    s_out = _gatenorm_fwd(y_fb, xbc, u, dsk_row, nw_row)
    y_ssd = _mm_nn(s_out, wts["w_proj_ssd"], tm=1024, tn=1024, name="proj_ssd")
    att_o, att_l = [], []
    for g in range(3):
        o, l = _attn_fwd(u, g)
        att_o.append(o)
        att_l.append(l)
    att, y_att = _combine_proj(att_o, att_l, wts["w_proj_attn"])
    mixin, pre1, h1 = _mix_out_ln1(y_ssd, y_att, u, bg_row, x, wts["w_out"], g1, b1)
    up, act = _mlp_up(h1, wts["w_up"])
    dpre2, dpre2_b, acc2 = _mlp_down_ln2_loss(act, wts["w_down"], pre1, tgt, g1, b1, g2, b2)

    dw_down = _mm_tn(act, dpre2_b, tka=1024, tn=1024, tt=1024, name="dw_down")
    dup = _d_up(dpre2_b, wts["w_down"], up)
    dw_up = _mm_tn(h1, dup, tka=1024, tn=1024, tt=1024, name="dw_up", out_shards=4)
    dpre1, acc1 = _d_h1_ln1_bwd(dup, wts["w_up"], dpre2, pre1, g1, b1)
    dw_out = _mm_tn(mixin, dpre1, tka=1024, tn=1024, tt=1024, name="dw_out")
    dy_ssd, dy_att, du, dbg = _d_mixin_mix_bwd(dpre1, wts["w_out"], y_ssd, y_att, u, bg_row)
    dw_proj_ssd = _mm_tn(s_out, dy_ssd, tka=1024, tn=1024, tt=1024, name="dw_proj_ssd")
    ds_out = _mm_nt(dy_ssd, wts["w_proj_ssd"], tm=1024, tk=1024, tc=1024, name="d_s_out")
    dw_proj_attn = _mm_tn(att, dy_att, tka=256, tn=256, tt=1024, name="dw_proj_attn", out_shards=4)
    do_g, e_g = _d_att_combine_bwd(dy_att, wts["w_proj_attn"], att_o, att_l)
    for g in range(3):
        du = _attn_dq(u, du, do_g[g], att_l[g], e_g[g], g)
        du = _attn_dkv(u, du, do_g[g], att_l[g], e_g[g], g)
    big = {
        "w_proj_ssd": dw_proj_ssd.reshape(4, DI // 4, D),
        "w_proj_attn": dw_proj_attn,
        "w_out": dw_out.reshape(4, D // 4, D),
        "w_up": dw_up,
        "w_down": dw_down.reshape(4, DFF // 4, D),
    }
    early = [big[n] for n in EARLY]
    dy, du, dnw, dds, recv = _gatenorm_bwd(ds_out, y_fb, xbc, u, du, dsk_row, nw_row,
                                           side=_swap_side(early) if rs_idx else None)
    if rs_idx:
        halves = [_add_half(g, r, rs_idx[0], f"rs_add_half_{n}") for g, r, n in zip(early, recv, EARLY)]
    dxs_f, dbc_f, ddt_f, sacc_f, recv = _ssd_bwd(xbc, u, par, dy, st_f, rev=False,
                                                 side=_step1_side([h[1] for h in halves]) if rs_idx else None)
    if rs_idx:
        k = len(EARLY)
        sums1 = [_rs_add1(h[0], ra, rb, rs_idx[1], f"rs_add1_{n}")
                 for h, ra, rb, n in zip(halves, recv[:k], recv[k:], EARLY)]
    dxs, dbc, ddt, sacc_b, recv = _ssd_bwd(
        xbc, u, par, dy, st_b, rev=True, add=(dxs_f, dbc_f, ddt_f),
        side=_step2_side([s1[2] for s1 in sums1], [s1[3] for s1 in sums1]) if rs_idx else None)
    pieces = None
    if rs_idx:
        pieces = {n: _rs_add2(s1[0], s1[1], ra, rb, rs_idx[1], f"rs_add2_{n}")
                  for s1, ra, rb, n in zip(sums1, recv[:k], recv[k:], EARLY)}
    dpre_c, dcw, dcb = _conv_dpre(u, dxs, dy, dbc, dsk_row, sm["conv_w"], cb_row)
    du = _conv_dx(du, dpre_c, sm["conv_w"])
    du = _dt_bwd(du, ddt)
    dw_in_p = _mm_tn(xb, du, tka=1024, tn=2432, tt=1024, name="dw_in")
    big["w_in"] = _shards_from_perm(dw_in_p)
    side = None
    if rs_idx:
        g = big["w_in"]
        half = _add_half(g, _run_side(_swap_side([g]), "rs_swap_halves")[0], rs_idx[0], "rs_add_half_w_in")
        side = _step1_side([half[1]])
    dx, recv = _d_x(du, wts["w_in_p"], dpre1, side)
    if rs_idx:
        s1 = _rs_add1(half[0], recv[0], recv[1], rs_idx[1], "rs_add1_w_in")
        ra2, rb2 = _run_side(_step2_side([s1[2]], [s1[3]]), "rs_step2")
        pieces["w_in"] = _rs_add2(s1[0], s1[1], ra2, rb2, rs_idx[1], "rs_add2_w_in")

    sacc = sacc_f + sacc_b
    small = {
        "b_gate": dbg[0], "conv_w": dcw[0:KCONV], "conv_b": dcb[0],
        "dt_bias_f": sacc[0, 0:32], "dt_bias_b": sacc[0, 32:64], "a_log_f": sacc[1, 0:32], "a_log_b": sacc[1, 32:64],
        "d_skip": dds[0, 0:32], "ssd_norm_w": dnw[0],
        "ln1_g": acc1[0], "ln1_b": acc1[1], "ln2_g": acc2[0], "ln2_b": acc2[1], "loss": acc2[2, 0:1],
    }
    return dx, big, small, pieces


HBM_SPEC = pl.BlockSpec(memory_space=pl.ANY)


def _place():
    x, y, c = lax.axis_index("x"), lax.axis_index("y"), lax.axis_index("c")
    chips = [(1 - x, y), (x, 1 - y), (1 - x, 1 - y)]
    return x, y, c, chips


def _gather_phases(n):
    def tools(ins, outs, send_sems, recv_sems):
        x, y, c, _ = _place()
        slots = (2 * x + y, 2 * (1 - x) + y, 2 * x + 1 - y, 2 * (1 - x) + 1 - y)
        peers = ((1 - x, y, c), (x, 1 - y, c), (x, y, 1 - c))

        def copy(w, k, src, dst, to):
            return pltpu.make_async_remote_copy(src_ref=src, dst_ref=dst, send_sem=send_sems.at[w, k],
                                                recv_sem=recv_sems.at[w, k], device_id=to, device_id_type=MESH)

        def rows(w, core, part):
            rh = ins[w].shape[0] // 2
            if part is None:
                return pl.ds(core * rh, rh)
            return pl.ds(core * rh + part * (rh // 2), rh // 2)

        def same(w, k, slot, core, part, to):
            blk = outs[w].at[slot, rows(w, core, part), :]
            return copy(w, k, blk, blk, to)

        def sends(w):
            q, q_x, q_y, q_d = slots
            x_nbr, y_nbr, sibling = peers
            mine = rows(w, c, None)
            mk = functools.partial
            return [mk(copy, w, 0, ins[w].at[mine, :], outs[w].at[q, mine, :], x_nbr),
                    mk(copy, w, 1, ins[w].at[mine, :], outs[w].at[q, mine, :], y_nbr),
                    mk(same, w, 2, q_x, c, 0, y_nbr), mk(same, w, 3, q_y, c, 1, x_nbr),
                    mk(same, w, 4, q_x, c, None, sibling), mk(same, w, 5, q_y, c, None, sibling),
                    mk(same, w, 6, q_d, c, 0, sibling), mk(same, w, 7, q_d, c, 1, sibling),
                    mk(copy, w, 8, ins[w], outs[w].at[q], sibling)]

        return c, slots, peers, same, sends

    def first(*refs):
        _, _, _, _, sends = tools(*refs)
        for w in range(n):
            cps = sends(w)
            for k in (8, 0, 1):
                cps[k]().start()

    def second(*refs):
        c, (_, q_x, q_y, _), (x_nbr, y_nbr, _), same, sends = tools(*refs)
        for w in range(n):
            cps = sends(w)
            same(w, 0, q_x, c, None, x_nbr).wait_recv()
            cps[2]().start()
            cps[4]().start()
            same(w, 1, q_y, c, None, y_nbr).wait_recv()
            cps[3]().start()
            cps[5]().start()

    def third(*refs):
        c, (_, _, _, q_d), (x_nbr, y_nbr, _), same, sends = tools(*refs)
        for w in range(n):
            cps = sends(w)
            same(w, 2, q_d, c, 0, y_nbr).wait_recv()
            cps[6]().start()
            same(w, 3, q_d, c, 1, x_nbr).wait_recv()
            cps[7]().start()

    def last(*refs):
        c, (_, q_x, q_y, q_d), (_, _, sibling), same, sends = tools(*refs)
        for w in range(n):
            same(w, 4, q_x, 1 - c, None, sibling).wait_recv()
            same(w, 5, q_y, 1 - c, None, sibling).wait_recv()
            same(w, 6, q_d, 1 - c, 0, sibling).wait_recv()
            same(w, 7, q_d, 1 - c, 1, sibling).wait_recv()
            sends(w)[8]().wait_recv()
        for w in range(n):
            for mk_cp in sends(w):
                mk_cp().wait_send()

    return first, second, third, last


def _gather_side(shards):
    first, second, third, last = _gather_phases(len(shards))
    shapes = tuple(jax.ShapeDtypeStruct((4,) + s.shape, s.dtype) for s in shards)
    return _Side(tuple(shards), shapes, (len(shards), 9), None, ((0.0, first), (0.36, second), (0.58, third), (1.0, last)))


class _Side(NamedTuple):
    ins: tuple
    out_shapes: tuple
    nsem: tuple
    make: Callable
    phases: tuple = ()


def _swap_copies(ins, outs, send_sems, recv_sems):
    x, y, c, _ = _place()
    copies = []
    for w in range(len(ins)):
        rh = ins[w].shape[1] // 2
        for p in range(4):
            copies.append(pltpu.make_async_remote_copy(
                src_ref=ins[w].at[p, pl.ds((1 - c) * rh, rh), :], dst_ref=outs[w].at[p],
                send_sem=send_sems.at[w, p], recv_sem=recv_sems.at[w, p],
                device_id=(x, y, 1 - c), device_id_type=MESH))
    return copies


def _swap_side(grads):
    shapes = tuple(jax.ShapeDtypeStruct((4, g.shape[1] // 2, g.shape[2]), F32) for g in grads)
    return _Side(tuple(grads), shapes, (len(grads), 4), _swap_copies)


def _step1_copies(ins, outs, send_sems, recv_sems):
    n = len(ins)
    out_a, out_b = outs[:n], outs[n:]
    x, y, c, _ = _place()
    copies = []
    for w in range(n):
        rq = ins[w].shape[1] // 2
        for i in range(2):
            copies.append(pltpu.make_async_remote_copy(
                src_ref=ins[w].at[2 * (1 - x) + i, pl.ds(0, rq), :], dst_ref=out_a[w].at[i],
                send_sem=send_sems.at[w, i], recv_sem=recv_sems.at[w, i],
                device_id=(1 - x, y, c), device_id_type=MESH))
            copies.append(pltpu.make_async_remote_copy(
                src_ref=ins[w].at[2 * i + 1 - y, pl.ds(rq, rq), :], dst_ref=out_b[w].at[i],
                send_sem=send_sems.at[w, 2 + i], recv_sem=recv_sems.at[w, 2 + i],
                device_id=(x, 1 - y, c), device_id_type=MESH))
    return copies


def _step1_side(parts):
    quarter = tuple(jax.ShapeDtypeStruct((2, p.shape[1] // 2, p.shape[2]), p.dtype) for p in parts)
    return _Side(tuple(parts), quarter + quarter, (len(parts), 4), _step1_copies)


def _step2_copies(ins, outs, send_sems, recv_sems):
    n = len(ins) // 2
    in_a, in_b, out_a, out_b = ins[:n], ins[n:], outs[:n], outs[n:]
    x, y, c, _ = _place()
    copies = []
    for w in range(n):
        copies.append(pltpu.make_async_remote_copy(
            src_ref=in_a[w].at[1 - y], dst_ref=out_a[w], send_sem=send_sems.at[w, 0], recv_sem=recv_sems.at[w, 0],
            device_id=(x, 1 - y, c), device_id_type=MESH))
        copies.append(pltpu.make_async_remote_copy(
            src_ref=in_b[w].at[1 - x], dst_ref=out_b[w], send_sem=send_sems.at[w, 1], recv_sem=recv_sems.at[w, 1],
            device_id=(1 - x, y, c), device_id_type=MESH))
    return copies


def _step2_side(tas, tbs):
    one = tuple(jax.ShapeDtypeStruct(p.shape[1:], p.dtype) for p in tuple(tas) + tuple(tbs))
    return _Side(tuple(tas) + tuple(tbs), one, (len(tas), 2), _step2_copies)


def _phases_of(side, n_steps):
    if side.phases:
        return [(min(int(f * n_steps), n_steps - 1), fn) for f, fn in side.phases]

    def start(*refs):
        for cp in side.make(*refs):
            cp.start()

    def wait(*refs):
        for cp in side.make(*refs):
            cp.wait()

    return [(0, start), (n_steps - 1, wait)]


def _run_side(side, name):
    n_in, n_out = len(side.ins), len(side.out_shapes)

    def body(*refs):
        for _, fn in _phases_of(side, 1):
            fn(refs[:n_in], refs[n_in:n_in + n_out], *refs[n_in + n_out:])

    return pl.pallas_call(
        body, out_shape=list(side.out_shapes), in_specs=[HBM_SPEC] * n_in, out_specs=[HBM_SPEC] * n_out,
        scratch_shapes=[pltpu.SemaphoreType.DMA(side.nsem), pltpu.SemaphoreType.DMA(side.nsem)], name=name)(*side.ins)


def _host_call(body, side, n_steps, *, out_shape, in_specs, out_specs, scratch_shapes, args, aliases, name, sem):
    n_in, n_out, n_scr = len(in_specs), len(out_shape), len(scratch_shapes)
    if side is None:
        outs = pl.pallas_call(body, out_shape=tuple(out_shape), grid=(n_steps,), in_specs=list(in_specs),
                              out_specs=tuple(out_specs), scratch_shapes=list(scratch_shapes),
                              input_output_aliases=aliases, name=name, compiler_params=_params(sem))(*args)
        return tuple(outs), ()
    ns_in, ns_out = len(side.ins), len(side.out_shapes)

    def wrapped(*refs):
        h_in, s_in = refs[:n_in], refs[n_in:n_in + ns_in]
        o0 = n_in + ns_in
        h_out, s_out = refs[o0:o0 + n_out], refs[o0 + n_out:o0 + n_out + ns_out]
        c0 = o0 + n_out + ns_out
        h_scr, sems = refs[c0:c0 + n_scr], refs[c0 + n_scr:]
        step = pl.program_id(0)
        phases = _phases_of(side, n_steps)
        for at, fn in phases[:-1]:
            pl.when(step == at)(functools.partial(fn, s_in, s_out, *sems))
        body(*h_in, *h_out, *h_scr)
        pl.when(step == phases[-1][0])(functools.partial(phases[-1][1], s_in, s_out, *sems))

    outs = pl.pallas_call(
        wrapped, out_shape=tuple(out_shape) + tuple(side.out_shapes), grid=(n_steps,),
        in_specs=list(in_specs) + [HBM_SPEC] * ns_in, out_specs=tuple(out_specs) + (HBM_SPEC,) * ns_out,
        scratch_shapes=list(scratch_shapes) + [pltpu.SemaphoreType.DMA(side.nsem), pltpu.SemaphoreType.DMA(side.nsem)],
        input_output_aliases=aliases, name=name, compiler_params=_params(sem))(*args, *side.ins)
    return tuple(outs[:n_out]), tuple(outs[n_out:])


def _join_halves(pieces):
    n = len(pieces)

    def body(*refs):
        outs = refs[n:2 * n]
        send_sems, recv_sems = refs[2 * n:]
        x, y, c, _ = _place()

        def copy(w, slot):
            return pltpu.make_async_remote_copy(
                src_ref=outs[w].at[slot], dst_ref=outs[w].at[slot], send_sem=send_sems.at[w], recv_sem=recv_sems.at[w],
                device_id=(x, y, 1 - c), device_id_type=MESH)

        for w in range(n):
            copy(w, c).start()
        for w in range(n):
            copy(w, 1 - c).wait_recv()
            copy(w, c).wait_send()

    return pl.pallas_call(
        body, out_shape=[jax.ShapeDtypeStruct(p.shape, F32) for p in pieces],
        in_specs=[HBM_SPEC] * n, out_specs=[HBM_SPEC] * n, input_output_aliases={w: w for w in range(n)},
        scratch_shapes=[pltpu.SemaphoreType.DMA((n,)), pltpu.SemaphoreType.DMA((n,))],
        name="rs_join_halves")(*pieces)


def _add_tile_rows(rh, c):
    for cand in (512, 256, 128, 64, 32, 16, 8):
        if rh % cand == 0 and cand * c * 4 <= 2 ** 21:
            return cand
    return rh


def _add_half(grad, recv, c_idx, name):
    _, r, cc = grad.shape
    rh = r // 2
    tr = _add_tile_rows(rh, cc)
    nb = rh // tr

    def body(c_ref, g_ref, r_ref, o_ref, ob_ref):
        del c_ref
        s = g_ref[...] + r_ref[...]
        o_ref[...] = s
        ob_ref[...] = s.astype(BF16)

    blk = pl.BlockSpec((None, tr, cc), lambda p, i, c_ref: (p, i, 0))
    grid_spec = pltpu.PrefetchScalarGridSpec(
        num_scalar_prefetch=1, grid=(4, nb),
        in_specs=[pl.BlockSpec((None, tr, cc), lambda p, i, c_ref: (p, c_ref[0] * nb + i, 0)), blk],
        out_specs=(blk, blk))
    return pl.pallas_call(
        body, out_shape=(jax.ShapeDtypeStruct((4, rh, cc), F32), jax.ShapeDtypeStruct((4, rh, cc), BF16)),
        grid_spec=grid_spec, name=name, compiler_params=_params(("parallel", "parallel")))(c_idx, grad, recv)


def _rs_add1(part, recv_a, recv_b, xy_idx, name):
    _, rh, cc = part.shape
    rq = rh // 2
    tr = _add_tile_rows(rq, cc)
    nb = rq // tr

    def body(xy_ref, pa_ref, pb_ref, ra_ref, rb_ref, ta_ref, tb_ref, tab_ref, tbb_ref):
        del xy_ref
        ta = pa_ref[...] + ra_ref[...].astype(F32)
        tb = pb_ref[...] + rb_ref[...].astype(F32)
        ta_ref[...] = ta
        tb_ref[...] = tb
        tab_ref[...] = ta.astype(BF16)
        tbb_ref[...] = tb.astype(BF16)

    blk = pl.BlockSpec((None, tr, cc), lambda i, j, xy: (i, j, 0))
    grid_spec = pltpu.PrefetchScalarGridSpec(
        num_scalar_prefetch=1, grid=(2, nb),
        in_specs=[pl.BlockSpec((None, tr, cc), lambda i, j, xy: (2 * xy[0] + i, j, 0)),
                  pl.BlockSpec((None, tr, cc), lambda i, j, xy: (2 * i + xy[1], nb + j, 0)), blk, blk],
        out_specs=(blk, blk, blk, blk))
    f32s, b16s = jax.ShapeDtypeStruct((2, rq, cc), F32), jax.ShapeDtypeStruct((2, rq, cc), BF16)
    return pl.pallas_call(body, out_shape=(f32s, f32s, b16s, b16s), grid_spec=grid_spec, name=name,
                          compiler_params=_params(("parallel", "parallel")))(xy_idx, part, part, recv_a, recv_b)


def _rs_add2(ta, tb, recv_a, recv_b, xy_idx, name):
    _, rq, cc = ta.shape
    tr = _add_tile_rows(rq, cc)
    nb = rq // tr

    def body(xy_ref, ta_ref, tb_ref, ra_ref, rb_ref, o_ref):
        del xy_ref
        s = pl.program_id(0)
        fa = ta_ref[...] + ra_ref[...].astype(F32)
        fb = tb_ref[...] + rb_ref[...].astype(F32)
        o_ref[...] = jnp.where(s == 0, fa, fb)

    rblk = pl.BlockSpec((tr, cc), lambda s, j, xy: (j, 0))
    grid_spec = pltpu.PrefetchScalarGridSpec(
        num_scalar_prefetch=1, grid=(2, nb),
        in_specs=[pl.BlockSpec((None, tr, cc), lambda s, j, xy: (xy[1], j, 0)),
                  pl.BlockSpec((None, tr, cc), lambda s, j, xy: (xy[0], j, 0)), rblk, rblk],
        out_specs=pl.BlockSpec((None, tr, cc), lambda s, j, xy: (xy[2], s * nb + j, 0)))
    return pl.pallas_call(body, out_shape=jax.ShapeDtypeStruct((2, 2 * rq, cc), F32), grid_spec=grid_spec, name=name,
                          compiler_params=_params(("parallel", "parallel")))(xy_idx, ta, tb, recv_a, recv_b)


def _allreduce_small(slab):
    r = slab.shape[0]

    def body(x_ref, o_ref, buf, send_sems, recv_sems):
        x, y, c, _ = _place()
        me = 4 * x + 2 * y + c
        buf[me] = x_ref[...]
        peers = []
        for k in range(1, 8):
            kx, ky, kc = (k >> 2) & 1, (k >> 1) & 1, k & 1
            peers.append((x + kx - 2 * x * kx, y + ky - 2 * y * ky, c + kc - 2 * c * kc))

        def copy(k, slot):
            return pltpu.make_async_remote_copy(src_ref=x_ref, dst_ref=buf.at[slot], send_sem=send_sems.at[k],
                                                recv_sem=recv_sems.at[k], device_id=peers[k], device_id_type=MESH)

        for k in range(7):
            copy(k, me).start()
        for k, (px, py, pc) in enumerate(peers):
            copy(k, 4 * px + 2 * py + pc).wait_recv()
        for k in range(7):
            copy(k, me).wait_send()
        acc = buf[0]
        for j in range(1, 8):
            acc = acc + buf[j]
        o_ref[...] = acc

    vm = pl.BlockSpec(memory_space=pltpu.VMEM)
    return pl.pallas_call(
        body, out_shape=jax.ShapeDtypeStruct((r, 128), F32), in_specs=[vm], out_specs=vm,
        scratch_shapes=[pltpu.VMEM((8, r, 128), F32), pltpu.SemaphoreType.DMA((7,)), pltpu.SemaphoreType.DMA((7,))],
        name="allreduce_small")(slab)


def _pack(arrs):
    rows = []
    for a in arrs:
        v = a.reshape(-1)
        v = jnp.pad(v, (0, (-v.shape[0]) % 128))
        rows.append(v.reshape(-1, 128))
    slab = jnp.concatenate(rows, axis=0)
    return jnp.pad(slab, ((0, (-slab.shape[0]) % 8), (0, 0)))


def _unpack(slab, shapes):
    out, r0 = [], 0
    for shp in shapes:
        size = math.prod(shp)
        nr = -(-size // 128)
        out.append(slab[r0:r0 + nr].reshape(-1)[:size].reshape(shp))
        r0 += nr
    return out


BIG = ("w_in", "w_proj_ssd", "w_proj_attn", "w_out", "w_up", "w_down")
SMALL = ("b_gate", "conv_w", "conv_b", "dt_bias_f", "dt_bias_b", "a_log_f", "a_log_b", "d_skip", "ssd_norm_w",
         "ln1_g", "ln1_b", "ln2_g", "ln2_b")
ORDER = ("w_in", "b_gate", "conv_w", "conv_b", "dt_bias_f", "dt_bias_b", "a_log_f", "a_log_b", "d_skip", "ssd_norm_w",
         "w_proj_ssd", "w_proj_attn", "w_out", "ln1_g", "ln1_b", "w_up", "w_down", "ln2_g", "ln2_b")


def kernel(x, w_in, b_gate, conv_w, conv_b, dt_bias_f, dt_bias_b, a_log_f, a_log_b, d_skip, ssd_norm_w, w_proj_ssd, w_proj_attn, w_out, ln1_g, ln1_b, w_up, w_down, ln2_g, ln2_b, loss_target, m_w_in, m_b_gate, m_conv_w, m_conv_b, m_dt_bias_f, m_dt_bias_b, m_a_log_f, m_a_log_b, m_d_skip, m_ssd_norm_w, m_w_proj_ssd, m_w_proj_attn, m_w_out, m_ln1_g, m_ln1_b, m_w_up, m_w_down, m_ln2_g, m_ln2_b, v_w_in, v_b_gate, v_conv_w, v_conv_b, v_dt_bias_f, v_dt_bias_b, v_a_log_f, v_a_log_b, v_d_skip, v_ssd_norm_w, v_w_proj_ssd, v_w_proj_attn, v_w_out, v_ln1_g, v_ln1_b, v_w_up, v_w_down, v_ln2_g, v_ln2_b):
    w = dict(w_in=w_in, b_gate=b_gate, conv_w=conv_w, conv_b=conv_b, dt_bias_f=dt_bias_f, dt_bias_b=dt_bias_b,
             a_log_f=a_log_f, a_log_b=a_log_b, d_skip=d_skip, ssd_norm_w=ssd_norm_w, w_proj_ssd=w_proj_ssd,
             w_proj_attn=w_proj_attn, w_out=w_out, ln1_g=ln1_g, ln1_b=ln1_b, w_up=w_up, w_down=w_down, ln2_g=ln2_g, ln2_b=ln2_b)
    m = dict(w_in=m_w_in, b_gate=m_b_gate, conv_w=m_conv_w, conv_b=m_conv_b, dt_bias_f=m_dt_bias_f, dt_bias_b=m_dt_bias_b,
             a_log_f=m_a_log_f, a_log_b=m_a_log_b, d_skip=m_d_skip, ssd_norm_w=m_ssd_norm_w, w_proj_ssd=m_w_proj_ssd,
             w_proj_attn=m_w_proj_attn, w_out=m_w_out, ln1_g=m_ln1_g, ln1_b=m_ln1_b, w_up=m_w_up, w_down=m_w_down,
             ln2_g=m_ln2_g, ln2_b=m_ln2_b)
    v = dict(w_in=v_w_in, b_gate=v_b_gate, conv_w=v_conv_w, conv_b=v_conv_b, dt_bias_f=v_dt_bias_f, dt_bias_b=v_dt_bias_b,
             a_log_f=v_a_log_f, a_log_b=v_a_log_b, d_skip=v_d_skip, ssd_norm_w=v_ssd_norm_w, w_proj_ssd=v_w_proj_ssd,
             w_proj_attn=v_w_proj_attn, w_out=v_w_out, ln1_g=v_ln1_g, ln1_b=v_ln1_b, w_up=v_w_up, w_down=v_w_down,
             ln2_g=v_ln2_g, ln2_b=v_ln2_b)
    xi, yi, ci = lax.axis_index("x"), lax.axis_index("y"), lax.axis_index("c")
    shard = 2 * xi + yi

    (g_in,) = _run_side(_gather_side([w["w_in"].astype(BF16)]), "allgather_w_in")
    wts = {"w_in_p": _perm_from_shards(g_in), "pending": [w[n].astype(BF16) for n in EARLY]}

    cw_slab = jnp.zeros((KCONV, 4, CONVD // 4), F32)
    cw_slab = lax.dynamic_update_slice(cw_slab, conv_w[:, None, :] * 0.5, (0, shard, 0))
    conv_w_all = _unpack(_allreduce_small(_pack([cw_slab])), [(KCONV, CONVD)])[0]

    sm = {n: w[n] for n in SMALL}
    sm["conv_w"] = conv_w_all
    c_idx = jnp.reshape(ci, (1,)).astype(jnp.int32)
    xy_idx = jnp.stack([xi, yi, ci]).astype(jnp.int32)
    dx, big, small, pieces = _local_grads(x[0], loss_target[0], wts, sm, rs_idx=(c_idx, xy_idx))

    names = list(SMALL) + ["loss"]
    shapes = [small[n].shape for n in names]
    red = dict(zip(names, _unpack(_allreduce_small(_pack([small[n] for n in names])), shapes)))
    loss = red["loss"].reshape(())
    gsm = {n: red[n] for n in SMALL}
    conv_w_grad_shard = lax.dynamic_slice_in_dim(gsm["conv_w"].reshape(KCONV, 4, CONVD // 4), shard, 1, axis=1)
    gsm["conv_w"] = conv_w_grad_shard.reshape(KCONV, CONVD // 4)

    joined = _join_halves([pieces[n] for n in BIG])
    gbig = {n: j.reshape(w[n].shape) for n, j in zip(BIG, joined)}

    grads, deltas, new_m, new_v = {}, {}, {}, {}
    for n in BIG:
        grads[n] = gbig[n]
        if n == "w_in":
            gt = gbig[n].T
            dlt, nmt, nvt = _adamw(w[n].T, gt, m[n].T, v[n].T, f"adamw_{n}")
            grads[n], deltas[n], new_m[n], new_v[n] = gt.T, dlt.T, nmt.T, nvt.T
            continue
        deltas[n], new_m[n], new_v[n] = _adamw(w[n], gbig[n], m[n], v[n], f"adamw_{n}")
    sshapes = [w[n].shape for n in SMALL]
    d_s, m_s, v_s = _adamw(_pack([w[n] for n in SMALL]), _pack([gsm[n] for n in SMALL]),
                           _pack([m[n] for n in SMALL]), _pack([v[n] for n in SMALL]), "adamw_small")
    for n, dd, mm, vv in zip(SMALL, _unpack(d_s, sshapes), _unpack(m_s, sshapes), _unpack(v_s, sshapes)):
        grads[n], deltas[n], new_m[n], new_v[n] = gsm[n], dd, mm, vv

    return (loss, dx[None], *[grads[n] for n in ORDER], *[deltas[n] for n in ORDER],
            *[new_m[n] for n in ORDER], *[new_v[n] for n in ORDER])
```

```python
import functools
import math
from typing import Callable, NamedTuple

import jax
import numpy as np
import jax.numpy as jnp
from jax import lax
from jax.experimental import pallas as pl
from jax.experimental.pallas import tpu as pltpu

F32, BF16 = jnp.float32, jnp.bfloat16
MESH = pl.DeviceIdType.MESH

D = 1024
DI = 2048
NH = 32
HP = 64
NG = 4
NS = 128
Q = 128
CONVD = 3072
KCONV = 5
DFF = 4096
AH = 64
ATT_HALF = 64
DILATIONS = (1, 4, 16)
IN_COLS = 9536
OZ, OGATE, OXBC, OKV, OQ, ODT, UW = 0, 2048, 4096, 7168, 8704, 9472, 9728
ALPHA = 2.0 ** 0.25
NORM_EPS = 1e-5
ADAM_LR, ADAM_B1, ADAM_B2, ADAM_EPS, ADAM_WD, ADAM_STEP = 0.001, 0.9, 0.999, 1e-8, 0.01, 10
VMEM_LIMIT = 56 * 2 ** 20
NEG = -1e30


def _params(sem):
    return pltpu.CompilerParams(dimension_semantics=sem, vmem_limit_bytes=VMEM_LIMIT)


def _sigmoid(x):
    return 1.0 / (1.0 + jnp.exp(-x))


def _softplus(x):
    e = jnp.exp(-jnp.abs(x))
    small = e * (1.0 - e * (0.5 - e * (1.0 / 3.0)))
    return jnp.maximum(x, 0.0) + jnp.where(e < 0.01, small, jnp.log(1.0 + e))


def _split3(a):
    hi = a.astype(BF16)
    r = a - hi.astype(F32)
    mid = r.astype(BF16)
    lo = (r - mid.astype(F32)).astype(BF16)
    return hi, mid, lo


def _dot01(a, m01):
    hi, mid, lo = _split3(a)
    d = lambda p: jnp.dot(p, m01, preferred_element_type=F32)
    return d(hi) + d(mid) + d(lo)


def _dot01_l(m01, a):
    hi, mid, lo = _split3(a)
    d = lambda p: jnp.dot(m01, p, preferred_element_type=F32)
    return d(hi) + d(mid) + d(lo)


def _dot_nt(a, b):
    return lax.dot_general(a, b, (((1,), (1,)), ((), ())), preferred_element_type=F32)


def _iota(shape, dim):
    return lax.broadcasted_iota(jnp.int32, shape, dim)


def _mm_tn(a, b, *, tka, tn, tt, name, out_shards=None):
    t, ka = a.shape
    n = b.shape[1]
    if out_shards:
        assert tn == n // out_shards
        out_shape = jax.ShapeDtypeStruct((out_shards, ka, tn), F32)
        o_spec = pl.BlockSpec((None, tka, tn), lambda i, j, s: (j, i, 0))
    else:
        out_shape = jax.ShapeDtypeStruct((ka, n), F32)
        o_spec = pl.BlockSpec((tka, tn), lambda i, j, s: (i, j))

    def body(a_ref, b_ref, o_ref):
        s = pl.program_id(2)
        part = lax.dot_general(a_ref[...].astype(BF16), b_ref[...].astype(BF16), (((0,), (0,)), ((), ())),
                               preferred_element_type=F32)

        @pl.when(s == 0)
        def _():
            o_ref[...] = part

        @pl.when(s > 0)
        def _():
            o_ref[...] += part

    return pl.pallas_call(
        body, out_shape=out_shape, grid=(ka // tka, n // tn, t // tt),
        in_specs=[pl.BlockSpec((tt, tka), lambda i, j, s: (s, i)), pl.BlockSpec((tt, tn), lambda i, j, s: (s, j))],
        out_specs=o_spec, name=name, compiler_params=_params(("parallel", "parallel", "arbitrary")))(a, b)


def _d_x(du, w_in_p, dpre1, side=None):
    t = du.shape[0]
    tm, tc = 1024, 2432
    nc = UW // tc

    def body(a_ref, b_ref, add_ref, o_ref):
        c = pl.program_id(0) % nc
        part = _dot_nt(a_ref[...], b_ref[...])

        @pl.when(c == 0)
        def _():
            o_ref[...] = part + ALPHA * add_ref[...]

        @pl.when(c > 0)
        def _():
            o_ref[...] += part

    outs, side_outs = _host_call(
        body, side, (t // tm) * nc, out_shape=(jax.ShapeDtypeStruct((t, D), F32),),
        in_specs=[pl.BlockSpec((tm, tc), lambda s: (s // nc, s % nc)), pl.BlockSpec((D, tc), lambda s: (0, s % nc)),
                  pl.BlockSpec((tm, D), lambda s: (s // nc, 0))],
        out_specs=(pl.BlockSpec((tm, D), lambda s: (s // nc, 0)),),
        scratch_shapes=[], args=(du, w_in_p, dpre1), aliases={}, name="d_x", sem=("arbitrary",))
    return outs[0], side_outs


def _in_proj(xb, w_in_p, side=None):
    t, k = xb.shape
    tm, tn = 1024, 2432
    nm, nn = t // tm, UW // tn

    def body(a_ref, b_ref, o_ref):
        o_ref[...] = jnp.dot(a_ref[...], b_ref[...], preferred_element_type=F32)

    outs, side_outs = _host_call(
        body, side, nm * nn, out_shape=(jax.ShapeDtypeStruct((t, UW), F32),),
        in_specs=[pl.BlockSpec((tm, k), lambda s: (s % nm, 0)), pl.BlockSpec((k, tn), lambda s: (0, s // nm))],
        out_specs=(pl.BlockSpec((tm, tn), lambda s: (s % nm, s // nm)),),
        scratch_shapes=[], args=(xb, w_in_p), aliases={}, name="in_proj", sem=("arbitrary",))
    return outs[0], side_outs


CONV_TM = 512
CONV_TC = 1024
CONV_RC = 64
CONV_CC = 256


def _halo_specs(t, tm, tc, col0):
    nb8 = t // 8
    r8 = tm // 8
    return [
        pl.BlockSpec((8, tc), lambda i, j: (jnp.maximum(i * r8 - 1, 0), col0 + j)),
        pl.BlockSpec((tm, tc), lambda i, j: (i, col0 + j)),
        pl.BlockSpec((8, tc), lambda i, j: (jnp.minimum((i + 1) * r8, nb8 - 1), col0 + j)),
    ]


def _fill_ext(ext, prev_ref, cur_ref, next_ref, tm, i, last):
    ext[0:8, :] = jnp.where(i > 0, prev_ref[...], 0.0)
    ext[8:8 + tm, :] = cur_ref[...]
    ext[8 + tm:16 + tm, :] = jnp.where(i < last, next_ref[...], 0.0)


def _conv_fwd(u, conv_w, conv_b):
    t = u.shape[0]
    tm, tc = CONV_TM, CONV_TC

    def body(prev_ref, cur_ref, next_ref, w_ref, b_ref, o_ref, ext):
        _fill_ext(ext, prev_ref, cur_ref, next_ref, tm, pl.program_id(0), t // tm - 1)
        for c0 in range(0, tc, CONV_CC):
            cs = slice(c0, c0 + CONV_CC)
            w = w_ref[:, cs]
            for r0 in range(0, tm, CONV_RC):
                acc = jnp.broadcast_to(b_ref[:, cs], (CONV_RC, CONV_CC))
                for k in range(KCONV):
                    acc = acc + w[k:k + 1, :] * ext[pl.ds(r0 + 6 + k, CONV_RC), cs]
                o_ref[r0:r0 + CONV_RC, cs] = acc * _sigmoid(acc)

    return pl.pallas_call(
        body, out_shape=jax.ShapeDtypeStruct((t, CONVD), F32), grid=(t // tm, CONVD // tc),
        in_specs=_halo_specs(t, tm, tc, OXBC // tc) + [
            pl.BlockSpec((KCONV, tc), lambda i, j: (0, j)), pl.BlockSpec((1, tc), lambda i, j: (0, j))],
        out_specs=pl.BlockSpec((tm, tc), lambda i, j: (i, j)),
        scratch_shapes=[pltpu.VMEM((tm + 16, tc), F32)],
        name="conv_fwd", compiler_params=_params(("parallel", "parallel")))(u, u, u, conv_w, conv_b)


def _conv_dpre(u, dxs, dy, dbc, dsk_row, conv_w, conv_b):
    t = u.shape[0]
    tm, tc = CONV_TM, CONV_TC
    r8 = tm // 8
    nb8 = t // 8
    c0 = OXBC // tc

    def body(uprev, ucur, unext, f_ref, y_ref, cf_ref, dsk_ref, w_ref, bias_ref, dpre_ref, dw_ref, db_ref, ext):
        j = pl.program_id(0)
        i = pl.program_id(1)
        _fill_ext(ext, uprev, ucur, unext, tm, i, t // tm - 1)
        is_xs = j < 2
        dw_cols, db_cols = [], []
        for c0 in range(0, tc, CONV_CC):
            cs = slice(c0, c0 + CONV_CC)
            w = w_ref[:, cs]
            dsk = dsk_ref[:, cs]
            dw_acc = [jnp.zeros((1, CONV_CC), F32) for _ in range(KCONV)]
            db_acc = jnp.zeros((1, CONV_CC), F32)
            for r0 in range(0, tm, CONV_RC):
                rs = slice(r0, r0 + CONV_RC)
                taps = [ext[pl.ds(r0 + 6 + k, CONV_RC), cs] for k in range(KCONV)]
                pre = jnp.broadcast_to(bias_ref[:, cs], (CONV_RC, CONV_CC))
                for k in range(KCONV):
                    pre = pre + w[k:k + 1, :] * taps[k]
                s = _sigmoid(pre)
                up = jnp.where(is_xs, f_ref[rs, cs] + dsk * y_ref[rs, cs], cf_ref[rs, cs])
                dpre = up * (s * (1.0 + pre * (1.0 - s)))
                dpre_ref[rs, cs] = dpre
                for k in range(KCONV):
                    dw_acc[k] = dw_acc[k] + jnp.sum(dpre * taps[k], axis=0, keepdims=True)
                db_acc = db_acc + jnp.sum(dpre, axis=0, keepdims=True)
            dw_cols.append(jnp.concatenate(dw_acc + [jnp.zeros((8 - KCONV, CONV_CC), F32)], axis=0))
            db_cols.append(jnp.broadcast_to(db_acc, (8, CONV_CC)))
        dw_part = jnp.concatenate(dw_cols, axis=1)
        db_part = jnp.concatenate(db_cols, axis=1)

        @pl.when(i == 0)
        def _():
            dw_ref[...] = dw_part
            db_ref[...] = db_part

        @pl.when(i > 0)
        def _():
            dw_ref[...] += dw_part
            db_ref[...] += db_part

    xs_spec = pl.BlockSpec((tm, tc), lambda j, i: (jnp.where(j < 2, i, 0), jnp.minimum(j, 1)))
    bc_spec = pl.BlockSpec((tm, tc), lambda j, i: (jnp.where(j == 2, i, 0), 0))
    in_specs = [
        pl.BlockSpec((8, tc), lambda j, i: (jnp.maximum(i * r8 - 1, 0), c0 + j)),
        pl.BlockSpec((tm, tc), lambda j, i: (i, c0 + j)),
        pl.BlockSpec((8, tc), lambda j, i: (jnp.minimum((i + 1) * r8, nb8 - 1), c0 + j)),
        xs_spec, xs_spec, bc_spec,
        pl.BlockSpec((1, tc), lambda j, i: (0, jnp.minimum(j, 1))),
        pl.BlockSpec((KCONV, tc), lambda j, i: (0, j)), pl.BlockSpec((1, tc), lambda j, i: (0, j)),
    ]
    return pl.pallas_call(
        body,
        out_shape=(jax.ShapeDtypeStruct((t, CONVD), F32), jax.ShapeDtypeStruct((8, CONVD), F32),
                   jax.ShapeDtypeStruct((8, CONVD), F32)),
        grid=(CONVD // tc, t // tm), in_specs=in_specs,
        out_specs=(pl.BlockSpec((tm, tc), lambda j, i: (i, j)),
                   pl.BlockSpec((8, tc), lambda j, i: (0, j)), pl.BlockSpec((8, tc), lambda j, i: (0, j))),
        scratch_shapes=[pltpu.VMEM((tm + 16, tc), F32)],
        name="conv_dpre", compiler_params=_params(("parallel", "arbitrary")))(
            u, u, u, dxs, dy, dbc, dsk_row, conv_w, conv_b)


def _conv_dx(du, dpre, conv_w):
    t = dpre.shape[0]
    tm, tc = CONV_TM, CONV_TC
    r8 = tm // 8
    nb8 = t // 8

    def body(prev_ref, cur_ref, next_ref, w_ref, du_in, du_out, ext):
        del du_in
        _fill_ext(ext, prev_ref, cur_ref, next_ref, tm, pl.program_id(1), t // tm - 1)
        for c0 in range(0, tc, CONV_CC):
            cs = slice(c0, c0 + CONV_CC)
            w = w_ref[:, cs]
            for r0 in range(0, tm, CONV_RC):
                acc = jnp.zeros((CONV_RC, CONV_CC), F32)
                for k in range(KCONV):
                    acc = acc + w[k:k + 1, :] * ext[pl.ds(r0 + 10 - k, CONV_RC), cs]
                du_out[r0:r0 + CONV_RC, cs] = acc.astype(du_out.dtype)

    in_specs = [
        pl.BlockSpec((8, tc), lambda j, i: (jnp.maximum(i * r8 - 1, 0), j)),
        pl.BlockSpec((tm, tc), lambda j, i: (i, j)),
        pl.BlockSpec((8, tc), lambda j, i: (jnp.minimum((i + 1) * r8, nb8 - 1), j)),
        pl.BlockSpec((KCONV, tc), lambda j, i: (0, j)),
        pl.BlockSpec(memory_space=pl.ANY),
    ]
    return pl.pallas_call(
        body, out_shape=jax.ShapeDtypeStruct(du.shape, du.dtype), grid=(CONVD // tc, t // tm), in_specs=in_specs,
        out_specs=pl.BlockSpec((tm, tc), lambda j, i: (i, OXBC // tc + j)),
        scratch_shapes=[pltpu.VMEM((tm + 16, tc), F32)], input_output_aliases={4: 0},
        name="conv_dx", compiler_params=_params(("parallel", "parallel")))(dpre, dpre, dpre, conv_w, du)


def _ssd_common(dtr_ref, par_ref, rev):
    raw = dtr_ref[...]
    lane = _iota((1, 128), 1)
    mine = (lane >= 32 * rev) & (lane < 32 * rev + 32)
    bias = par_ref[0:1, :]
    arow = jnp.where(mine, -jnp.exp(par_ref[1:2, :]), 0.0)
    dt = _softplus(raw + bias)
    a = dt * arow
    ri = _iota((Q, Q), 0)
    ci = _iota((Q, Q), 1)
    tri = (ci >= ri) if rev else (ci <= ri)
    trit = (ci <= ri) if rev else (ci >= ri)
    cs = _dot01_l(tri.astype(BF16), a)
    return raw, bias, arow, mine, dt, cs, tri, trit


def _expand_mat(rev):
    r = np.arange(128)[:, None]
    c = np.arange(DI)[None, :]
    return jnp.asarray(r == (c // HP) + 32 * rev, BF16)


def _sum_mat(rev):
    r = np.arange(DI)[:, None]
    c = np.arange(128)[None, :]
    return jnp.asarray(c == (r // HP) + 32 * rev, BF16)


def _ssd_fwd(xbc, u, par, y_add=None, *, rev):
    t = xbc.shape[0]
    nc = t // Q
    end = 0 if rev else Q - 1
    cmap = (lambda c: nc - 1 - c) if rev else (lambda c: c)

    def body(xbc_ref, dtr_ref, par_ref, ex_ref, *rest):
        yadd_ref = rest[0] if y_add is not None else None
        y_ref, st_ref, h_scr = rest[-3:]
        step = pl.program_id(0)

        @pl.when(step == 0)
        def _():
            h_scr[...] = jnp.zeros((NS, DI), F32)

        raw, bias, arow, mine, dt, cs, tri, trit = _ssd_common(dtr_ref, par_ref, rev)
        cst = cs.T
        dtt = dt.T
        tot_col = cst[:, end:end + 1]
        wt = dtt * jnp.exp(tot_col - cst)
        ecs_all = jnp.exp(cs)
        gam = jnp.exp(cs[end:end + 1, :])
        gam_x = _dot01(jnp.broadcast_to(gam, (8, 128)), ex_ref[...])[0:1, :]
        lane = _iota((Q, 128), 1)
        sel = lane < HP
        st_ref[...] = h_scr[...]
        for g in range(NG):
            bg = xbc_ref[:, DI + NS * g:DI + NS * (g + 1)]
            cg = xbc_ref[:, DI + NG * NS + NS * g:DI + NG * NS + NS * (g + 1)]
            cb = _dot_nt(cg.astype(BF16), bg.astype(BF16))
            bt = bg.T
            for k in range(4):
                lo = 512 * g + 128 * k
                xp = xbc_ref[:, lo:lo + 128].astype(BF16)
                hp = h_scr[:, lo:lo + 128]
                rhs = jnp.concatenate([xp, hp.astype(BF16)], axis=0)
                lhs, bts = [], []
                for j in range(2):
                    hc = 8 * g + 2 * k + j + 32 * rev
                    csc = jnp.broadcast_to(cs[:, hc:hc + 1], (Q, Q))
                    lm = jnp.exp(jnp.where(tri, csc - cst[hc:hc + 1, :], NEG)) * dtt[hc:hc + 1, :]
                    mh = (cb * lm).astype(BF16)
                    ec = (jnp.broadcast_to(ecs_all[:, hc:hc + 1], (Q, NS)) * cg).astype(BF16)
                    lhs.append(jnp.concatenate([mh, ec], axis=1))
                    bts.append((bt * wt[hc:hc + 1, :]).astype(BF16))
                ys = jnp.dot(jnp.concatenate(lhs, axis=0), rhs, preferred_element_type=F32)
                ss = jnp.dot(jnp.concatenate(bts, axis=0), xp, preferred_element_type=F32)
                yp = jnp.where(sel, ys[0:Q], ys[Q:2 * Q])
                y_ref[:, lo:lo + 128] = yp if yadd_ref is None else yp + yadd_ref[:, lo:lo + 128]
                h_scr[:, lo:lo + 128] = gam_x[:, lo:lo + 128] * hp + jnp.where(sel, ss[0:NS], ss[NS:2 * NS])

    return pl.pallas_call(
        body,
        out_shape=(jax.ShapeDtypeStruct((t, DI), F32), jax.ShapeDtypeStruct((nc, NS, DI), F32)),
        grid=(nc,),
        in_specs=[pl.BlockSpec((Q, CONVD), lambda c: (cmap(c), 0)),
                  pl.BlockSpec((Q, 128), lambda c: (cmap(c), ODT // 128)),
                  pl.BlockSpec((8, 128), lambda c: (0, 0)),
                  pl.BlockSpec((128, DI), lambda c: (0, 0))]
        + ([pl.BlockSpec((Q, DI), lambda c: (cmap(c), 0))] if y_add is not None else []),
        out_specs=(pl.BlockSpec((Q, DI), lambda c: (cmap(c), 0)),
                   pl.BlockSpec((None, NS, DI), lambda c: (cmap(c), 0, 0))),
        scratch_shapes=[pltpu.VMEM((NS, DI), F32)],
        name="ssd_fwd_rev" if rev else "ssd_fwd", compiler_params=_params(("arbitrary",)))(
            xbc, u, par, _expand_mat(rev), *([y_add] if y_add is not None else []))


def _ssd_bwd(xbc, u, par, dy, st, *, rev, add=None, side=None):
    t = xbc.shape[0]
    nc = t // Q
    end = 0 if rev else Q - 1
    cmap = (lambda c: c) if rev else (lambda c: nc - 1 - c)

    def body(xbc_ref, dtr_ref, par_ref, dy_ref, hin_ref, ex_ref, sm_ref, *rest):
        addx_ref, addbc_ref, addt_ref = rest[:3] if add is not None else (None, None, None)
        dxs_ref, dbc_ref, ddt_ref, acc_ref, dh_scr = rest[-5:]
        step = pl.program_id(0)

        @pl.when(step == 0)
        def _():
            dh_scr[...] = jnp.zeros((NS, DI), F32)

        raw, bias, arow, mine, dt, cs, tri, trit = _ssd_common(dtr_ref, par_ref, rev)
        ri = _iota((Q, Q), 0)
        ci = _iota((Q, Q), 1)
        stri = ((ri > ci) if rev else (ri < ci)).astype(BF16)
        strit = ((ci > ri) if rev else (ci < ri)).astype(BF16)
        cst = cs.T
        dtt = dt.T
        et = jnp.exp(cst)
        ecs_all = jnp.exp(cs)
        ws_all = jnp.exp(cs[end:end + 1, :] - cs)
        expand = ex_ref[...]
        summat = sm_ref[...]
        gam = jnp.exp(cs[end:end + 1, :])
        gam_x = _dot01(jnp.broadcast_to(gam, (8, 128)), expand)[0:1, :]
        dt_hi, dt_mid, _ = _split3(dt)
        dtx = (jnp.dot(dt_hi, expand, preferred_element_type=F32)
               + jnp.dot(dt_mid, expand, preferred_element_type=F32))
        lane = _iota((Q, 128), 1)
        sel = lane < HP
        dho = dh_scr[...]
        t3 = jnp.sum(dho * hin_ref[...], axis=0, keepdims=True) * gam_x
        dxs_cols, dxs2_cols, yoff_cols, a1_rows = [], [], [], []
        for g in range(NG):
            bg = xbc_ref[:, DI + NS * g:DI + NS * (g + 1)]
            cg = xbc_ref[:, DI + NG * NS + NS * g:DI + NG * NS + NS * (g + 1)]
            bb = bg.astype(BF16)
            cbf = cg.astype(BF16)
            cb = _dot_nt(cbf, bb)
            cbt = _dot_nt(bb, cbf)
            ct = cg.T
            bdh = jnp.dot(bb, dho[:, 512 * g:512 * (g + 1)].astype(BF16), preferred_element_type=F32)
            dcb = jnp.zeros((Q, Q), F32)
            dcg = jnp.zeros((Q, NS), F32)
            dbg = jnp.zeros((Q, NS), F32)
            for k in range(4):
                lo = 512 * g + 128 * k
                xpf = xbc_ref[:, lo:lo + 128]
                xp = xpf.astype(BF16)
                dyp = dy_ref[:, lo:lo + 128]
                dypb = dyp.astype(BF16)
                hinp = hin_ref[:, lo:lo + 128].astype(BF16)
                dhp = dho[:, lo:lo + 128]
                es, ws, lmds, mts, ctes, dyms, ecbs = [], [], [], [], [], [], []
                for j in range(2):
                    hc = 8 * g + 2 * k + j + 32 * rev
                    csc = jnp.broadcast_to(cs[:, hc:hc + 1], (Q, Q))
                    csr = cst[hc:hc + 1, :]
                    lmds.append(jnp.exp(jnp.where(tri, csc - csr, NEG)) * dtt[hc:hc + 1, :])
                    lmb = jnp.exp(jnp.where(trit, csr - csc, NEG))
                    mts.append((cbt * lmb).astype(BF16))
                    dyms.append(jnp.where(sel if j == 0 else ~sel, dyp, 0.0).astype(BF16))
                    ecs = jnp.broadcast_to(ecs_all[:, hc:hc + 1], (Q, NS))
                    es.append(ecs)
                    ws.append(jnp.broadcast_to(ws_all[:, hc:hc + 1], (Q, NS)))
                    ecbs.append((ecs * cg).astype(BF16))
                    ctes.append((ct * et[hc:hc + 1, :]).astype(BF16))
                by_dy = jnp.dot(jnp.concatenate(mts + ctes, axis=0), dypb, preferred_element_type=F32)
                dmm = _dot_nt(jnp.concatenate(dyms, axis=0), xp)
                dm0, dm1 = dmm[0:Q] * lmds[0], dmm[Q:2 * Q] * lmds[1]
                dcb = dcb + dm0 + dm1
                rr = jnp.dot(jnp.concatenate([dm0 * cb, dm1 * cb], axis=0).astype(BF16), stri, preferred_element_type=F32)
                a1_rows.append(jnp.sum(jnp.where(tri, rr[0:Q], 0.0), axis=0, keepdims=True))
                a1_rows.append(jnp.sum(jnp.where(tri, rr[Q:2 * Q], 0.0), axis=0, keepdims=True))
                yo = jnp.dot(jnp.concatenate(ecbs, axis=0), hinp, preferred_element_type=F32)
                e_p = jnp.where(sel, es[0], es[1])
                w_p = jnp.where(sel, ws[0], ws[1])
                d2 = w_p * bdh[:, 128 * k:128 * (k + 1)]
                dxs2_cols.append(d2)
                dxs_cols.append(jnp.where(sel, by_dy[0:Q], by_dy[Q:2 * Q]) + d2)
                yoff_cols.append(jnp.where(sel, yo[0:Q], yo[Q:2 * Q]))
                dcg = dcg + _dot_nt((e_p * dyp).astype(BF16), hinp)
                dbg = dbg + _dot_nt((w_p * dtx[:, lo:lo + 128] * xpf).astype(BF16), dhp.astype(BF16))
                dh_scr[:, lo:lo + 128] = (gam_x[:, lo:lo + 128] * dhp
                                          + jnp.where(sel, by_dy[2 * Q:3 * Q], by_dy[3 * Q:4 * Q]))
            dcg = dcg + jnp.dot(dcb.astype(BF16), bb, preferred_element_type=F32)
            dbg = dbg + jnp.dot(dcb.T.astype(BF16), cbf, preferred_element_type=F32)
            lo_b, lo_c = NS * g, NG * NS + NS * g
            if addbc_ref is not None:
                dbg = dbg + addbc_ref[:, lo_b:lo_b + NS]
                dcg = dcg + addbc_ref[:, lo_c:lo_c + NS]
            dbc_ref[:, lo_b:lo_b + NS] = dbg
            dbc_ref[:, lo_c:lo_c + NS] = dcg
        dxs = jnp.concatenate(dxs_cols, axis=1)
        dxs_ref[...] = dxs * dtx if addx_ref is None else dxs * dtx + addx_ref[...]
        xs = xbc_ref[:, 0:DI]
        stacked = jnp.concatenate([xs * dxs, xs * jnp.concatenate(dxs2_cols, axis=1),
                                   dy_ref[...] * jnp.concatenate(yoff_cols, axis=1),
                                   jnp.broadcast_to(t3, (8, DI))], axis=0).astype(BF16)
        sums = jnp.dot(stacked, summat, preferred_element_type=F32)
        rx, rx2, ryo, c0 = sums[0:Q], sums[Q:2 * Q], sums[2 * Q:3 * Q], sums[3 * Q:3 * Q + 1]
        zero32 = jnp.zeros((32, Q), F32)
        a1t = jnp.concatenate(([zero32] if rev else []) + a1_rows + [zero32] * (2 if rev else 3), axis=0)
        da = (a1t.T + jnp.dot(trit.astype(BF16), ryo.astype(BF16), preferred_element_type=F32)
              + jnp.dot(strit, (dt * rx2).astype(BF16), preferred_element_type=F32) + jnp.where(mine, c0, 0.0))
        ddt = rx + da * arow
        ddtr = ddt * _sigmoid(raw + bias)
        ddt_ref[...] = ddtr if addt_ref is None else ddtr + addt_ref[...]
        part = jnp.concatenate([jnp.sum(ddtr, axis=0, keepdims=True),
                                jnp.sum(da * dt, axis=0, keepdims=True) * arow,
                                jnp.zeros((6, 128), F32)], axis=0)

        @pl.when(step == 0)
        def _():
            acc_ref[...] = part

        @pl.when(step > 0)
        def _():
            acc_ref[...] += part

    outs, side_outs = _host_call(
        body, side, nc,
        out_shape=(jax.ShapeDtypeStruct((t, DI), F32), jax.ShapeDtypeStruct((t, 2 * NG * NS), F32),
                   jax.ShapeDtypeStruct((t, 128), F32), jax.ShapeDtypeStruct((8, 128), F32)),
        in_specs=[pl.BlockSpec((Q, CONVD), lambda c: (cmap(c), 0)),
                  pl.BlockSpec((Q, 128), lambda c: (cmap(c), ODT // 128)),
                  pl.BlockSpec((8, 128), lambda c: (0, 0)),
                  pl.BlockSpec((Q, DI), lambda c: (cmap(c), 0)),
                  pl.BlockSpec((None, NS, DI), lambda c: (cmap(c), 0, 0)),
                  pl.BlockSpec((128, DI), lambda c: (0, 0)), pl.BlockSpec((DI, 128), lambda c: (0, 0))]
        + ([pl.BlockSpec((Q, DI), lambda c: (cmap(c), 0)), pl.BlockSpec((Q, 2 * NG * NS), lambda c: (cmap(c), 0)),
            pl.BlockSpec((Q, 128), lambda c: (cmap(c), 0))] if add is not None else []),
        out_specs=(pl.BlockSpec((Q, DI), lambda c: (cmap(c), 0)),
                   pl.BlockSpec((Q, 2 * NG * NS), lambda c: (cmap(c), 0)),
                   pl.BlockSpec((Q, 128), lambda c: (cmap(c), 0)),
                   pl.BlockSpec((8, 128), lambda c: (0, 0))),
        scratch_shapes=[pltpu.VMEM((NS, DI), F32)],
        args=(xbc, u, par, dy, st, _expand_mat(rev), _sum_mat(rev)) + (tuple(add) if add is not None else ()), aliases={},
        name="ssd_bwd_rev" if rev else "ssd_bwd", sem=("arbitrary",))
    return (*outs, side_outs)


GN_TM = 256
GN_GROUP = DI // NG


def _gn_forward_vals(y0, xs, z, dsk):
    y = y0 + dsk * xs
    sz = _sigmoid(z)
    gate = z * sz
    y2 = y * gate
    parts, rs = [], []
    for g in range(NG):
        seg = y2[:, GN_GROUP * g:GN_GROUP * (g + 1)]
        r = lax.rsqrt(jnp.mean(seg * seg, axis=1, keepdims=True) + NORM_EPS)
        rs.append(r)
        parts.append(seg * r)
    yn = jnp.concatenate(parts, axis=1)
    return y, sz, gate, yn, rs


def _gatenorm_fwd(y_fb, xbc, u, dsk_row, nw_row, w_ps):
    t = y_fb.shape[0]
    tm = GN_TM

    def body(y_ref, xs_ref, z_ref, dsk_ref, nw_ref, w_ref, o_ref, ys_ref):
        _, _, _, yn, _ = _gn_forward_vals(y_ref[...], xs_ref[...], z_ref[...], dsk_ref[...])
        s_out = (yn * nw_ref[...]).astype(BF16)
        o_ref[...] = s_out
        ys_ref[...] = jnp.dot(s_out, w_ref[...], preferred_element_type=F32)

    blk = pl.BlockSpec((tm, DI), lambda i: (i, 0))
    row = pl.BlockSpec((1, DI), lambda i: (0, 0))
    return pl.pallas_call(
        body, out_shape=(jax.ShapeDtypeStruct((t, DI), BF16), jax.ShapeDtypeStruct((t, D), F32)), grid=(t // tm,),
        in_specs=[blk, blk, pl.BlockSpec((tm, DI), lambda i: (i, OZ // DI)), row, row,
                  pl.BlockSpec((DI, D), lambda i: (0, 0))],
        out_specs=(blk, pl.BlockSpec((tm, D), lambda i: (i, 0))), name="gatenorm_fwd",
        compiler_params=_params(("parallel",)))(y_fb, xbc, u, dsk_row, nw_row, w_ps)


def _gatenorm_bwd(dy_ssd, w_ps, y_fb, xbc, u, du, dsk_row, nw_row, side=None):
    t = y_fb.shape[0]
    tm = GN_TM

    def body(dys_ref, w_ref, y_ref, xs_ref, z_ref, dsk_ref, nw_ref, sm_ref, du_in, dy_ref, du_out, dnw_ref, dds_ref):
        del du_in
        i = pl.program_id(0)
        xs = xs_ref[...]
        z = z_ref[...]
        y, sz, gate, yn, rs = _gn_forward_vals(y_ref[...], xs, z, dsk_ref[...])
        ds = _dot_nt(dys_ref[...], w_ref[...])
        gsc = ds * nw_ref[...]
        parts = []
        for g in range(NG):
            sl = slice(GN_GROUP * g, GN_GROUP * (g + 1))
            m = jnp.mean(gsc[:, sl] * yn[:, sl], axis=1, keepdims=True)
            parts.append(rs[g] * (gsc[:, sl] - yn[:, sl] * m))
        dy2 = jnp.concatenate(parts, axis=1)
        dy = dy2 * gate
        dy_ref[...] = dy
        du_out[...] = (dy2 * y * (sz * (1.0 + z * (1.0 - sz)))).astype(du_out.dtype)
        dnw = jnp.broadcast_to(jnp.sum(ds * yn, axis=0, keepdims=True), (8, DI))
        drow = jnp.broadcast_to(jnp.sum(dy * xs, axis=0, keepdims=True), (8, DI))
        dds = _dot01(drow, sm_ref[...])

        @pl.when(i == 0)
        def _():
            dnw_ref[...] = dnw
            dds_ref[...] = dds

        @pl.when(i > 0)
        def _():
            dnw_ref[...] += dnw
            dds_ref[...] += dds

    blk = pl.BlockSpec((tm, DI), lambda i: (i, 0))
    row = pl.BlockSpec((1, DI), lambda i: (0, 0))
    outs, side_outs = _host_call(
        body, side, t // tm,
        out_shape=(jax.ShapeDtypeStruct((t, DI), F32), jax.ShapeDtypeStruct(du.shape, du.dtype),
                   jax.ShapeDtypeStruct((8, DI), F32), jax.ShapeDtypeStruct((8, 128), F32)),
        in_specs=[pl.BlockSpec((tm, D), lambda i: (i, 0)), pl.BlockSpec((DI, D), lambda i: (0, 0)),
                  blk, blk, pl.BlockSpec((tm, DI), lambda i: (i, OZ // DI)), row, row,
                  pl.BlockSpec((DI, 128), lambda i: (0, 0)), pl.BlockSpec(memory_space=pl.ANY)],
        out_specs=(blk, pl.BlockSpec((tm, DI), lambda i: (i, OZ // DI)),
                   pl.BlockSpec((8, DI), lambda i: (0, 0)), pl.BlockSpec((8, 128), lambda i: (0, 0))),
        scratch_shapes=[], args=(dy_ssd, w_ps, y_fb, xbc, u, dsk_row, nw_row, _sum_mat(0), du), aliases={8: 1},
        name="gatenorm_bwd", sem=("arbitrary",))
    return (*outs, side_outs)


AT_B = 128
AT_W = AT_B + 2 * ATT_HALF
AT_L = 2 * AH
SCALE = 1.0 / math.sqrt(AH)


def _slope(g, hh):
    return 2.0 ** (-8.0 * (4 * g + hh + 1) / 12.0)


def _qcol(g):
    return lambda p: OQ // AT_L + 2 * g + p


def _kcol(g):
    return lambda p: OKV // AT_L + 4 * g + 2 * p


def _vcol(g):
    return lambda p: OKV // AT_L + 4 * g + 2 * p + 1


def _pcol(p):
    return p


def _sub(d):
    return 4 if d == 1 else 1


def _win_specs(col, t, d):
    tb, hb = AT_B * d * _sub(d), ATT_HALF * d
    per = tb // hb
    nh = t // hb
    return [
        pl.BlockSpec((hb, AT_L), lambda p, i: (jnp.maximum(per * i - 1, 0), col(p))),
        pl.BlockSpec((tb, AT_L), lambda p, i: (i, col(p))),
        pl.BlockSpec((hb, AT_L), lambda p, i: (jnp.minimum(per * (i + 1), nh - 1), col(p))),
    ]


def _blk_spec(col, d):
    return pl.BlockSpec((AT_B * d * _sub(d), AT_L), lambda p, i: (i, col(p)))


def _rows(ref, r, s, d):
    return ref[pl.ds(r, AT_B, stride=d), :] if d > 1 else ref[AT_B * s:AT_B * (s + 1), :]


def _win(p_ref, c_ref, n_ref, r, s, d):
    if d > 1:
        return jnp.concatenate([p_ref[pl.ds(r, ATT_HALF, stride=d), :], c_ref[pl.ds(r, AT_B, stride=d), :],
                                n_ref[pl.ds(r, ATT_HALF, stride=d), :]], axis=0)
    if s == 0:
        return jnp.concatenate([p_ref[...], c_ref[0:AT_B + ATT_HALF, :]], axis=0)
    if s == _sub(d) - 1:
        return jnp.concatenate([c_ref[AT_B * s - ATT_HALF:AT_B * (s + 1), :], n_ref[...]], axis=0)
    return c_ref[AT_B * s - ATT_HALF:AT_B * (s + 1) + ATT_HALF, :]


def _put_rows(ref, r, s, d, val):
    if d > 1:
        ref[pl.ds(r, AT_B, stride=d), :] = val
    else:
        ref[AT_B * s:AT_B * (s + 1), :] = val


def _for_blocks(d, fn):
    if d == 1:
        for s in range(_sub(d)):
            fn(0, s)
    else:
        def step(r, c):
            fn(r, 0)
            return c
        lax.fori_loop(0, d, step, 0, unroll=4)


def _attn_bias(blk, ln, d, g, p_id):
    a = blk * AT_B + _iota((AT_B, AT_W), 0)
    b = blk * AT_B - ATT_HALF + _iota((AT_B, AT_W), 1)
    rel = jnp.abs(a - b)
    valid = (rel <= ATT_HALF) & (b >= 0) & (b < ln)
    dist = (rel * d).astype(F32)
    out = []
    for hh in range(2):
        slope = jnp.where(p_id == 0, _slope(g, hh), _slope(g, 2 + hh))
        out.append(jnp.where(valid, -slope * dist, NEG))
    return out


def _attn_fwd(u, g):
    t = u.shape[0]
    d = DILATIONS[g]
    ln = t // d

    def body(q_ref, kp, kc, kn, vp, vc, vn, o_ref, l_ref):
        p_id = pl.program_id(0)
        i = pl.program_id(1)
        lane = _iota((AT_B, AT_L), 1)
        biases = [_attn_bias(i * _sub(d) + s, ln, d, g, p_id) for s in range(_sub(d))]

        def one(r, s):
            q = _rows(q_ref, r, s, d) * SCALE
            kw = _win(kp, kc, kn, r, s, d).astype(BF16)
            vw = _win(vp, vc, vn, r, s, d).astype(BF16)
            o = jnp.zeros((AT_B, AT_L), F32)
            lse = jnp.zeros((AT_B, AT_L), F32)
            for hh in range(2):
                hm = (lane // AH) == hh
                qm = jnp.where(hm, q, 0.0).astype(BF16)
                sc = _dot_nt(qm, kw) + biases[s][hh]
                m = jnp.max(sc, axis=1, keepdims=True)
                pr = jnp.exp(sc - m)
                den = jnp.sum(pr, axis=1, keepdims=True)
                oh = jnp.dot(pr.astype(BF16), vw, preferred_element_type=F32)
                o = jnp.where(hm, oh / den, o)
                lse = jnp.where(hm, m + jnp.log(den), lse)
            _put_rows(o_ref, r, s, d, o)
            _put_rows(l_ref, r, s, d, lse)

        _for_blocks(d, one)

    oshape = jax.ShapeDtypeStruct((t, 2 * AT_L), F32)
    ospec = _blk_spec(_pcol, d)
    return pl.pallas_call(
        body, out_shape=(oshape, oshape), grid=(2, t // (AT_B * d * _sub(d))),
        in_specs=[_blk_spec(_qcol(g), d)] + _win_specs(_kcol(g), t, d) + _win_specs(_vcol(g), t, d),
        out_specs=(ospec, ospec), name=f"attn_fwd_{g}", compiler_params=_params(("parallel", "parallel")))(
            u, u, u, u, u, u, u)


def _attn_dq(u, du, do, lse, e, g):
    t = u.shape[0]
    d = DILATIONS[g]
    ln = t // d

    def body(q_ref, kp, kc, kn, vp, vc, vn, do_ref, l_ref, e_ref, du_in, dq_ref, dq_scr):
        del du_in
        p_id = pl.program_id(0)
        i = pl.program_id(1)
        lane = _iota((AT_B, AT_L), 1)
        biases = [_attn_bias(i * _sub(d) + s, ln, d, g, p_id) for s in range(_sub(d))]

        def one(r, s):
            q = _rows(q_ref, r, s, d) * SCALE
            kw = _win(kp, kc, kn, r, s, d).astype(BF16)
            vw = _win(vp, vc, vn, r, s, d).astype(BF16)
            do_ = _rows(do_ref, r, s, d)
            lv = _rows(l_ref, r, s, d)
            ev = _rows(e_ref, r, s, d)
            dq = jnp.zeros((AT_B, AT_L), F32)
            for hh in range(2):
                hm = (lane // AH) == hh
                qm = jnp.where(hm, q, 0.0).astype(BF16)
                sc = _dot_nt(qm, kw) + biases[s][hh]
                lcol = jnp.broadcast_to(lv[:, AH * hh:AH * hh + 1], (AT_B, AT_W))
                ecol = jnp.broadcast_to(ev[:, AH * hh:AH * hh + 1], (AT_B, AT_W))
                pr = jnp.exp(sc - lcol)
                dom = jnp.where(hm, do_, 0.0).astype(BF16)
                ds = pr * (_dot_nt(dom, vw) + ecol)
                dqh = jnp.dot(ds.astype(BF16), kw, preferred_element_type=F32) * SCALE
                dq = jnp.where(hm, dqh, dq)
            _put_rows(dq_scr, r, s, d, dq)

        _for_blocks(d, one)
        dq_ref[...] = dq_scr[...].astype(dq_ref.dtype)

    rspec = _blk_spec(_pcol, d)
    return pl.pallas_call(
        body, out_shape=jax.ShapeDtypeStruct(du.shape, du.dtype), grid=(2, t // (AT_B * d * _sub(d))),
        in_specs=[_blk_spec(_qcol(g), d)] + _win_specs(_kcol(g), t, d) + _win_specs(_vcol(g), t, d)
        + [rspec, rspec, rspec, pl.BlockSpec(memory_space=pl.ANY)],
        out_specs=_blk_spec(_qcol(g), d), input_output_aliases={10: 0},
        scratch_shapes=[pltpu.VMEM((AT_B * d * _sub(d), AT_L), F32)],
        name=f"attn_dq_{g}", compiler_params=_params(("parallel", "parallel")))(
            u, u, u, u, u, u, u, do, lse, e, du)


def _attn_dkv(u, du, do, lse, e, g):
    t = u.shape[0]
    d = DILATIONS[g]
    ln = t // d

    def body(k_ref, v_ref, qp, qc, qn, dp_, dc_, dn_, lp, lc, ln_, ep, ec, en, du_in, dkv_ref, dk_scr, dv_scr):
        del du_in
        p_id = pl.program_id(0)
        jb = pl.program_id(1)
        lane = _iota((AT_B, AT_L), 1)
        biases = [_attn_bias(jb * _sub(d) + s, ln, d, g, p_id) for s in range(_sub(d))]

        def one(r, s):
            k = _rows(k_ref, r, s, d) * SCALE
            v = _rows(v_ref, r, s, d)
            qw = _win(qp, qc, qn, r, s, d).astype(BF16)
            dow = _win(dp_, dc_, dn_, r, s, d).astype(BF16)
            lt = _win(lp, lc, ln_, r, s, d).T
            et = _win(ep, ec, en, r, s, d).T
            dk = jnp.zeros((AT_B, AT_L), F32)
            dv = jnp.zeros((AT_B, AT_L), F32)
            for hh in range(2):
                hm = (lane // AH) == hh
                km = jnp.where(hm, k, 0.0).astype(BF16)
                st = _dot_nt(km, qw) + biases[s][hh]
                pt = jnp.exp(st - lt[AH * hh:AH * hh + 1, :])
                dvh = jnp.dot(pt.astype(BF16), dow, preferred_element_type=F32)
                vm = jnp.where(hm, v, 0.0).astype(BF16)
                dst = pt * (_dot_nt(vm, dow) + et[AH * hh:AH * hh + 1, :])
                dkh = jnp.dot(dst.astype(BF16), qw, preferred_element_type=F32) * SCALE
                dk = jnp.where(hm, dkh, dk)
                dv = jnp.where(hm, dvh, dv)
            _put_rows(dk_scr, r, s, d, dk)
            _put_rows(dv_scr, r, s, d, dv)

        _for_blocks(d, one)
        dkv_ref[:, 0:AT_L] = dk_scr[...].astype(dkv_ref.dtype)
        dkv_ref[:, AT_L:2 * AT_L] = dv_scr[...].astype(dkv_ref.dtype)

    return pl.pallas_call(
        body, out_shape=jax.ShapeDtypeStruct(du.shape, du.dtype), grid=(2, t // (AT_B * d * _sub(d))),
        in_specs=[_blk_spec(_kcol(g), d), _blk_spec(_vcol(g), d)]
        + _win_specs(_qcol(g), t, d) + _win_specs(_pcol, t, d) + _win_specs(_pcol, t, d) + _win_specs(_pcol, t, d)
        + [pl.BlockSpec(memory_space=pl.ANY)],
        out_specs=pl.BlockSpec((AT_B * d * _sub(d), 2 * AT_L), lambda p, i: (i, OKV // (2 * AT_L) + 2 * g + p)),
        input_output_aliases={14: 0},
        scratch_shapes=[pltpu.VMEM((AT_B * d * _sub(d), AT_L), F32), pltpu.VMEM((AT_B * d * _sub(d), AT_L), F32)],
        name=f"attn_dkv_{g}", compiler_params=_params(("parallel", "parallel")))(
            u, u, u, u, u, do, do, do, lse, lse, lse, e, e, e, du)


def _combine_weights(l0, l1, l2):
    m = jnp.maximum(jnp.maximum(l0, l1), l2)
    e0, e1, e2 = jnp.exp(l0 - m), jnp.exp(l1 - m), jnp.exp(l2 - m)
    inv = 1.0 / (e0 + e1 + e2)
    return e0 * inv, e1 * inv, e2 * inv


def _combine_proj(os_, ls_, w_pa):
    t = os_[0].shape[0]
    tm = ROW_TM
    nsh, _, ws = w_pa.shape

    def body(o0, o1, o2, l0, l1, l2, w_ref, a_ref, y_ref):
        w0, w1, w2 = _combine_weights(l0[...], l1[...], l2[...])
        att = w0 * o0[...] + w1 * o1[...] + w2 * o2[...]
        a_ref[...] = att
        ab = att.astype(BF16)
        for sh in range(nsh):
            y_ref[:, ws * sh:ws * (sh + 1)] = jnp.dot(ab, w_ref[sh], preferred_element_type=F32)

    blk = pl.BlockSpec((tm, 2 * AT_L), lambda i: (i, 0))
    return pl.pallas_call(
        body, out_shape=(jax.ShapeDtypeStruct((t, 2 * AT_L), F32), jax.ShapeDtypeStruct((t, nsh * ws), F32)),
        grid=(t // tm,), in_specs=[blk] * 6 + [pl.BlockSpec(w_pa.shape, lambda i: (0, 0, 0))],
        out_specs=(blk, pl.BlockSpec((tm, nsh * ws), lambda i: (i, 0))),
        name="combine_proj", compiler_params=_params(("parallel",)))(*os_, *ls_, w_pa)


def _d_att_combine_bwd(dy_att, w_pa, os_, ls_):
    t = dy_att.shape[0]
    tm = ROW_TM
    nsh, _, ws = w_pa.shape

    def body(dy_ref, w_ref, o0, o1, o2, l0, l1, l2, d0, d1, d2, e0, e1, e2):
        da = jnp.zeros((tm, 2 * AT_L), F32)
        for sh in range(nsh):
            da = da + _dot_nt(dy_ref[:, ws * sh:ws * (sh + 1)], w_ref[sh])
        w = _combine_weights(l0[...], l1[...], l2[...])
        att = w[0] * o0[...] + w[1] * o1[...] + w[2] * o2[...]
        r = _iota((2 * AT_L, 2 * AT_L), 0) // AH
        c = _iota((2 * AT_L, 2 * AT_L), 1) // AH
        hs = _dot01(da * att, (r == c).astype(BF16))
        for wg, dref, eref in zip(w, (d0, d1, d2), (e0, e1, e2)):
            dref[...] = wg * da
            eref[...] = -wg * hs

    blk = pl.BlockSpec((tm, 2 * AT_L), lambda i: (i, 0))
    shp = jax.ShapeDtypeStruct((t, 2 * AT_L), F32)
    outs = pl.pallas_call(
        body, out_shape=(shp,) * 6, grid=(t // tm,),
        in_specs=[pl.BlockSpec((tm, nsh * ws), lambda i: (i, 0)), pl.BlockSpec(w_pa.shape, lambda i: (0, 0, 0))] + [blk] * 6,
        out_specs=(blk,) * 6, name="d_att_combine_bwd", compiler_params=_params(("parallel",)))(dy_att, w_pa, *os_, *ls_)
    return outs[0:3], outs[3:6]


ROW_TM = 512


def _ln(x, g, b):
    mu = jnp.mean(x, axis=1, keepdims=True)
    xc = x - mu
    var = jnp.mean(xc * xc, axis=1, keepdims=True)
    rstd = lax.rsqrt(var + NORM_EPS)
    xhat = xc * rstd
    return xhat * g + b, xhat, rstd


def _ln_back(dh, xhat, rstd, g):
    dxh = dh * g
    m1 = jnp.mean(dxh, axis=1, keepdims=True)
    m2 = jnp.mean(dxh * xhat, axis=1, keepdims=True)
    return rstd * (dxh - m1 - xhat * m2)


def _mlp_up(h1, w_up):
    t = h1.shape[0]
    tm, tn = 2 * ROW_TM, D

    def body(a_ref, b_ref, up_ref, act_ref):
        up = jnp.dot(a_ref[...], b_ref[...], preferred_element_type=F32)
        up_ref[...] = up.astype(BF16)
        r = jnp.maximum(up, 0.0)
        act_ref[...] = (r * r).astype(BF16)

    blk = pl.BlockSpec((tm, tn), lambda j, i: (i, j))
    return pl.pallas_call(
        body, out_shape=(jax.ShapeDtypeStruct((t, DFF), BF16), jax.ShapeDtypeStruct((t, DFF), BF16)),
        grid=(DFF // tn, t // tm),
        in_specs=[pl.BlockSpec((tm, D), lambda j, i: (i, 0)), pl.BlockSpec((None, D, tn), lambda j, i: (j, 0, 0))],
        out_specs=(blk, blk), name="mlp_up", compiler_params=_params(("parallel", "parallel")))(h1, w_up)


def _d_up(dpre2, w_down, up):
    t = up.shape[0]
    tm, tk = 2 * ROW_TM, D

    def body(a_ref, b_ref, u_ref, o_ref):
        dact = _dot_nt(a_ref[...], b_ref[...])
        o_ref[...] = (dact * 2.0 * jnp.maximum(u_ref[...].astype(F32), 0.0)).astype(BF16)

    blk = pl.BlockSpec((tm, tk), lambda j, i: (i, j))
    return pl.pallas_call(
        body, out_shape=jax.ShapeDtypeStruct((t, DFF), BF16), grid=(DFF // tk, t // tm),
        in_specs=[pl.BlockSpec((tm, D), lambda j, i: (i, 0)), pl.BlockSpec((tk, D), lambda j, i: (j, 0)), blk],
        out_specs=blk, name="d_up", compiler_params=_params(("parallel", "parallel")))(dpre2, w_down, up)


def _dt_bwd(du, ddt):
    t = ddt.shape[0]
    tm = 1024

    def body(f_ref, du_in, o_ref):
        del du_in
        o_ref[:, 0:128] = f_ref[...].astype(o_ref.dtype)
        o_ref[:, 128:256] = jnp.zeros((tm, 128), o_ref.dtype)

    blk = pl.BlockSpec((tm, 128), lambda i: (i, 0))
    return pl.pallas_call(
        body, out_shape=jax.ShapeDtypeStruct(du.shape, du.dtype), grid=(t // tm,),
        in_specs=[blk, pl.BlockSpec(memory_space=pl.ANY)],
        out_specs=pl.BlockSpec((tm, 256), lambda i: (i, ODT // 256)), input_output_aliases={1: 0},
        name="dt_bwd", compiler_params=_params(("parallel",)))(ddt, du)


def _mix_out_ln1(y_ssd, y_att, u, bg_row, x, w_out, g_row, b_row):
    t = x.shape[0]
    tm = ROW_TM

    def body(ys_ref, ya_ref, g0_ref, g1_ref, b0_ref, b1_ref, x_ref, w_ref, g_ref, b_ref, mixin_ref, pre_ref, h_ref):
        g0 = _sigmoid(g0_ref[...] + b0_ref[...])
        g1 = _sigmoid(g1_ref[...] + b1_ref[...])
        mixin = (g0 * ys_ref[...] + g1 * ya_ref[...]).astype(BF16)
        mixin_ref[...] = mixin
        pre = ALPHA * x_ref[...] + jnp.dot(mixin, w_ref[...], preferred_element_type=F32)
        pre_ref[...] = pre
        h, _, _ = _ln(pre, g_ref[...], b_ref[...])
        h_ref[...] = h.astype(BF16)

    blk = pl.BlockSpec((tm, D), lambda i: (i, 0))
    row = pl.BlockSpec((1, D), lambda i: (0, 0))
    return pl.pallas_call(
        body,
        out_shape=(jax.ShapeDtypeStruct((t, D), BF16), jax.ShapeDtypeStruct((t, D), F32), jax.ShapeDtypeStruct((t, D), BF16)),
        grid=(t // tm,),
        in_specs=[blk, blk, pl.BlockSpec((tm, D), lambda i: (i, OGATE // D)), pl.BlockSpec((tm, D), lambda i: (i, OGATE // D + 1)),
                  row, pl.BlockSpec((1, D), lambda i: (0, 1)), blk, pl.BlockSpec((D, D), lambda i: (0, 0)), row, row],
        out_specs=(blk, blk, blk), name="mix_out_ln1", compiler_params=_params(("parallel",)))(
            y_ssd, y_att, u, u, bg_row, bg_row, x, w_out, g_row, b_row)


def _mlp_down_ln2_loss(act, w_down, pre1, tgt, g1_row, b1_row, g2_row, b2_row):
    t = pre1.shape[0]
    tm = ROW_TM

    def body(a_ref, w_ref, p1_ref, t_ref, g1_ref, b1_ref, g2_ref, b2_ref, dpre_ref, dpreb_ref, acc_ref):
        i = pl.program_id(0)
        f = jnp.dot(a_ref[...], w_ref[...], preferred_element_type=F32)
        h1, _, _ = _ln(p1_ref[...], g1_ref[...], b1_ref[...])
        pre2 = ALPHA * h1 + f
        h2, xhat, rstd = _ln(pre2, g2_ref[...], b2_ref[...])
        err = h2 - t_ref[...]
        dh = err * (1.0 / D)
        dpre = _ln_back(dh, xhat, rstd, g2_ref[...])
        dpre_ref[...] = dpre
        dpreb_ref[...] = dpre.astype(BF16)
        loss = jnp.sum(jnp.sum(err * err, axis=1, keepdims=True), axis=0, keepdims=True) * (0.5 / D)
        part = jnp.concatenate([jnp.sum(dh * xhat, axis=0, keepdims=True), jnp.sum(dh, axis=0, keepdims=True),
                                jnp.broadcast_to(loss, (1, D)), jnp.zeros((5, D), F32)], axis=0)

        @pl.when(i == 0)
        def _():
            acc_ref[...] = part

        @pl.when(i > 0)
        def _():
            acc_ref[...] += part

    blk = pl.BlockSpec((tm, D), lambda i: (i, 0))
    row = pl.BlockSpec((1, D), lambda i: (0, 0))
    return pl.pallas_call(
        body,
        out_shape=(jax.ShapeDtypeStruct((t, D), F32), jax.ShapeDtypeStruct((t, D), BF16), jax.ShapeDtypeStruct((8, D), F32)),
        grid=(t // tm,),
        in_specs=[pl.BlockSpec((tm, DFF), lambda i: (i, 0)), pl.BlockSpec((DFF, D), lambda i: (0, 0)), blk, blk, row, row, row, row],
        out_specs=(blk, blk, pl.BlockSpec((8, D), lambda i: (0, 0))),
        name="mlp_down_ln2_loss", compiler_params=_params(("arbitrary",)))(act, w_down, pre1, tgt, g1_row, b1_row, g2_row, b2_row)


def _d_h1_ln1_bwd(dup, w_up, dpre2, pre1, g_row, b_row):
    t = dup.shape[0]
    tm = ROW_TM
    nsh = w_up.shape[0]

    def body(a_ref, w_ref, add_ref, pre_ref, g_ref, b_ref, dpre_ref, acc_ref):
        i = pl.program_id(0)
        dh_ = ALPHA * add_ref[...]
        for sh in range(nsh):
            dh_ = dh_ + _dot_nt(a_ref[:, D * sh:D * (sh + 1)], w_ref[sh])
        _, xhat, rstd = _ln(pre_ref[...], g_ref[...], b_ref[...])
        dpre_ref[...] = _ln_back(dh_, xhat, rstd, g_ref[...])
        rows = jnp.concatenate([jnp.sum(dh_ * xhat, axis=0, keepdims=True), jnp.sum(dh_, axis=0, keepdims=True),
                                jnp.zeros((6, D), F32)], axis=0)

        @pl.when(i == 0)
        def _():
            acc_ref[...] = rows

        @pl.when(i > 0)
        def _():
            acc_ref[...] += rows

    blk = pl.BlockSpec((tm, D), lambda i: (i, 0))
    row = pl.BlockSpec((1, D), lambda i: (0, 0))
    return pl.pallas_call(
        body, out_shape=(jax.ShapeDtypeStruct((t, D), F32), jax.ShapeDtypeStruct((8, D), F32)),
        grid=(t // tm,),
        in_specs=[pl.BlockSpec((tm, nsh * D), lambda i: (i, 0)), pl.BlockSpec(w_up.shape, lambda i: (0, 0, 0)),
                  blk, blk, row, row],
        out_specs=(blk, pl.BlockSpec((8, D), lambda i: (0, 0))),
        name="d_h1_ln1_bwd", compiler_params=_params(("arbitrary",)))(dup, w_up, dpre2, pre1, g_row, b_row)


def _d_mixin_mix_bwd(dpre1, w_out, y_ssd, y_att, u, bg_row):
    t = y_ssd.shape[0]
    tm = ROW_TM

    def body(a_ref, w_ref, ys_ref, ya_ref, g0_ref, g1_ref, b0_ref, b1_ref, dys_ref, dya_ref, du_ref, db_ref):
        i = pl.program_id(0)
        dm = _dot_nt(a_ref[...].astype(BF16), w_ref[...])
        g0 = _sigmoid(g0_ref[...] + b0_ref[...])
        g1 = _sigmoid(g1_ref[...] + b1_ref[...])
        dys_ref[...] = (dm * g0).astype(BF16)
        dya_ref[...] = (dm * g1).astype(BF16)
        dl0 = dm * ys_ref[...] * g0 * (1.0 - g0)
        dl1 = dm * ya_ref[...] * g1 * (1.0 - g1)
        du_ref[:, 0:D] = dl0.astype(BF16)
        du_ref[:, D:2 * D] = dl1.astype(BF16)
        part = jnp.concatenate([jnp.broadcast_to(jnp.sum(dl0, axis=0, keepdims=True), (8, D)),
                                jnp.broadcast_to(jnp.sum(dl1, axis=0, keepdims=True), (8, D))], axis=1)

        @pl.when(i == 0)
        def _():
            db_ref[...] = part

        @pl.when(i > 0)
        def _():
            db_ref[...] += part

    blk = pl.BlockSpec((tm, D), lambda i: (i, 0))
    return pl.pallas_call(
        body,
        out_shape=(jax.ShapeDtypeStruct((t, D), BF16), jax.ShapeDtypeStruct((t, D), BF16),
                   jax.ShapeDtypeStruct((t, UW), BF16), jax.ShapeDtypeStruct((8, 2 * D), F32)),
        grid=(t // tm,),
        in_specs=[blk, pl.BlockSpec((D, D), lambda i: (0, 0)), blk, blk,
                  pl.BlockSpec((tm, D), lambda i: (i, OGATE // D)), pl.BlockSpec((tm, D), lambda i: (i, OGATE // D + 1)),
                  pl.BlockSpec((1, D), lambda i: (0, 0)), pl.BlockSpec((1, D), lambda i: (0, 1))],
        out_specs=(blk, blk, pl.BlockSpec((tm, 2 * D), lambda i: (i, OGATE // (2 * D))),
                   pl.BlockSpec((8, 2 * D), lambda i: (0, 0))),
        name="d_mixin_mix_bwd", compiler_params=_params(("arbitrary",)))(dpre1, w_out, y_ssd, y_att, u, u, bg_row, bg_row)


def _adamw(w, g, m, v, name):
    r, c = w.shape
    tr, tc = r, c
    for cand in (256, 128, 64, 32, 16, 8):
        if r % cand == 0 and cand * c * 4 <= 2 ** 21:
            tr = cand
            break
    if tr < 64 and c % 256 == 0:
        tr, tc = r, 256
    bc1 = 1.0 / (1.0 - ADAM_B1 ** ADAM_STEP)
    bc2 = 1.0 / (1.0 - ADAM_B2 ** ADAM_STEP)

    def body(w_ref, g_ref, m_ref, v_ref, d_ref, nm_ref, nv_ref):
        gg = g_ref[...]
        nm = ADAM_B1 * m_ref[...] + (1.0 - ADAM_B1) * gg
        nv = ADAM_B2 * v_ref[...] + (1.0 - ADAM_B2) * (gg * gg)
        nm_ref[...] = nm
        nv_ref[...] = nv
        d_ref[...] = -ADAM_LR * ((nm * bc1) / (jnp.sqrt(nv * bc2) + ADAM_EPS) + ADAM_WD * w_ref[...])

    blk = pl.BlockSpec((tr, tc), lambda i, j: (i, j))
    shp = jax.ShapeDtypeStruct((r, c), F32)
    return pl.pallas_call(body, out_shape=(shp, shp, shp), grid=(r // tr, c // tc), in_specs=[blk] * 4,
                          out_specs=(blk,) * 3, name=name, compiler_params=_params(("parallel", "parallel")))(w, g, m, v)


def _segments():
    segs = [(0, 2048), (7488, 9536), (2048, 5120)]
    for g in range(3):
        for p in range(2):
            lo = 256 * g + 128 * p
            segs += [(5952 + lo, 5952 + lo + 128), (6720 + lo, 6720 + lo + 128)]
    segs += [(5184, 5952), (5120, 5184)]
    out, pos = [], 0
    for a, b in segs:
        out.append((a, b, pos))
        pos += b - a
    return out


SHARD_COLS = IN_COLS // 4


def _perm_from_shards(w_shards):
    pieces = []
    for a, b, _ in _segments():
        while a < b:
            s = a // SHARD_COLS
            e = min(b, (s + 1) * SHARD_COLS)
            pieces.append(w_shards[s][:, a - s * SHARD_COLS:e - s * SHARD_COLS])
            a = e
    pieces.append(jnp.zeros((w_shards.shape[1], UW - IN_COLS), w_shards.dtype))
    return jnp.concatenate(pieces, axis=1)


def _shards_from_perm(wp):
    segs = sorted(_segments())
    shards = []
    for s in range(4):
        lo, hi = s * SHARD_COLS, (s + 1) * SHARD_COLS
        pieces = []
        for a, b, pos in segs:
            x, y = max(a, lo), min(b, hi)
            if x < y:
                pieces.append(wp[:, pos + x - a:pos + y - a])
        shards.append(jnp.concatenate(pieces, axis=1))
    return jnp.stack(shards)


def _lanes128(*vecs):
    v = jnp.concatenate([a.reshape(-1) for a in vecs])
    return jnp.pad(v, (0, 128 - v.shape[0])).reshape(1, 128)


EARLY = ("w_proj_ssd", "w_proj_attn", "w_out", "w_up", "w_down")


def _weights_of(gathered):
    g_ps, g_pa, g_o, g_up, g_dn = gathered
    return {"w_proj_ssd": g_ps.reshape(DI, D), "w_proj_attn": g_pa, "w_out": g_o.reshape(D, D), "w_up": g_up,
            "w_down": g_dn.reshape(DFF, D)}


def _local_grads(x, tgt, wts, sm, rs_idx=None):
    row = lambda a: a.reshape(1, -1)
    bg_row, cb_row = row(sm["b_gate"]), row(sm["conv_b"])
    par = jnp.concatenate([_lanes128(sm["dt_bias_f"], sm["dt_bias_b"]), _lanes128(sm["a_log_f"], sm["a_log_b"]),
                           jnp.zeros((6, 128), F32)], axis=0)
    dsk_row = row(jnp.repeat(sm["d_skip"], HP))
    nw_row = row(sm["ssd_norm_w"])
    g1, b1, g2, b2 = row(sm["ln1_g"]), row(sm["ln1_b"]), row(sm["ln2_g"]), row(sm["ln2_b"])

    xb = x.astype(BF16)
    u, gathered = _in_proj(xb, wts["w_in_p"], side=_gather_side(wts["pending"]) if "pending" in wts else None)
    if gathered:
        wts = {**wts, **_weights_of(gathered)}
    xbc = _conv_fwd(u, sm["conv_w"], cb_row)
    y_f, st_f = _ssd_fwd(xbc, u, par, rev=False)
    y_fb, st_b = _ssd_fwd(xbc, u, par, y_f, rev=True)
    s_out, y_ssd = _gatenorm_fwd(y_fb, xbc, u, dsk_row, nw_row, wts["w_proj_ssd"])
    att_o, att_l = [], []
    for g in range(3):
        o, l = _attn_fwd(u, g)
        att_o.append(o)
        att_l.append(l)
    att, y_att = _combine_proj(att_o, att_l, wts["w_proj_attn"])
    mixin, pre1, h1 = _mix_out_ln1(y_ssd, y_att, u, bg_row, x, wts["w_out"], g1, b1)
    up, act = _mlp_up(h1, wts["w_up"])
    dpre2, dpre2_b, acc2 = _mlp_down_ln2_loss(act, wts["w_down"], pre1, tgt, g1, b1, g2, b2)

    dw_down = _mm_tn(act, dpre2_b, tka=1024, tn=1024, tt=1024, name="dw_down")
    dup = _d_up(dpre2_b, wts["w_down"], up)
    dw_up = _mm_tn(h1, dup, tka=1024, tn=1024, tt=1024, name="dw_up", out_shards=4)
    dpre1, acc1 = _d_h1_ln1_bwd(dup, wts["w_up"], dpre2, pre1, g1, b1)
    dw_out = _mm_tn(mixin, dpre1, tka=1024, tn=1024, tt=1024, name="dw_out")
    dy_ssd, dy_att, du, dbg = _d_mixin_mix_bwd(dpre1, wts["w_out"], y_ssd, y_att, u, bg_row)
    dw_proj_ssd = _mm_tn(s_out, dy_ssd, tka=1024, tn=1024, tt=1024, name="dw_proj_ssd")
    dw_proj_attn = _mm_tn(att, dy_att, tka=256, tn=256, tt=1024, name="dw_proj_attn", out_shards=4)
    do_g, e_g = _d_att_combine_bwd(dy_att, wts["w_proj_attn"], att_o, att_l)
    for g in range(3):
        du = _attn_dq(u, du, do_g[g], att_l[g], e_g[g], g)
        du = _attn_dkv(u, du, do_g[g], att_l[g], e_g[g], g)
    big = {
        "w_proj_ssd": dw_proj_ssd.reshape(4, DI // 4, D),
        "w_proj_attn": dw_proj_attn,
        "w_out": dw_out.reshape(4, D // 4, D),
        "w_up": dw_up,
        "w_down": dw_down.reshape(4, DFF // 4, D),
    }
    early = [big[n] for n in EARLY]
    dy, du, dnw, dds, recv = _gatenorm_bwd(dy_ssd, wts["w_proj_ssd"], y_fb, xbc, u, du, dsk_row, nw_row,
                                           side=_swap_side(early) if rs_idx else None)
    if rs_idx:
        halves = [_add_half(g, r, rs_idx[0], f"rs_add_half_{n}") for g, r, n in zip(early, recv, EARLY)]
    dxs_f, dbc_f, ddt_f, sacc_f, recv = _ssd_bwd(xbc, u, par, dy, st_f, rev=False,
                                                 side=_step1_side([h[1] for h in halves]) if rs_idx else None)
    if rs_idx:
        k = len(EARLY)
        sums1 = [_rs_add1(h[0], ra, rb, rs_idx[1], f"rs_add1_{n}")
                 for h, ra, rb, n in zip(halves, recv[:k], recv[k:], EARLY)]
    dxs, dbc, ddt, sacc_b, recv = _ssd_bwd(
        xbc, u, par, dy, st_b, rev=True, add=(dxs_f, dbc_f, ddt_f),
        side=_step2_side([s1[2] for s1 in sums1], [s1[3] for s1 in sums1]) if rs_idx else None)
    pieces = None
    if rs_idx:
        pieces = {n: _rs_add2(s1[0], s1[1], ra, rb, rs_idx[1], f"rs_add2_{n}")
                  for s1, ra, rb, n in zip(sums1, recv[:k], recv[k:], EARLY)}
    dpre_c, dcw, dcb = _conv_dpre(u, dxs, dy, dbc, dsk_row, sm["conv_w"], cb_row)
    du = _conv_dx(du, dpre_c, sm["conv_w"])
    du = _dt_bwd(du, ddt)
    dw_in_p = _mm_tn(xb, du, tka=1024, tn=2432, tt=1024, name="dw_in")
    big["w_in"] = _shards_from_perm(dw_in_p)
    side = None
    if rs_idx:
        g = big["w_in"]
        half = _add_half(g, _run_side(_swap_side([g]), "rs_swap_halves")[0], rs_idx[0], "rs_add_half_w_in")
        side = _step1_side([half[1]])
    dx, recv = _d_x(du, wts["w_in_p"], dpre1, side)
    if rs_idx:
        s1 = _rs_add1(half[0], recv[0], recv[1], rs_idx[1], "rs_add1_w_in")
        ra2, rb2 = _run_side(_step2_side([s1[2]], [s1[3]]), "rs_step2")
        pieces["w_in"] = _rs_add2(s1[0], s1[1], ra2, rb2, rs_idx[1], "rs_add2_w_in")

    sacc = sacc_f + sacc_b
    small = {
        "b_gate": dbg[0], "conv_w": dcw[0:KCONV], "conv_b": dcb[0],
        "dt_bias_f": sacc[0, 0:32], "dt_bias_b": sacc[0, 32:64], "a_log_f": sacc[1, 0:32], "a_log_b": sacc[1, 32:64],
        "d_skip": dds[0, 0:32], "ssd_norm_w": dnw[0],
        "ln1_g": acc1[0], "ln1_b": acc1[1], "ln2_g": acc2[0], "ln2_b": acc2[1], "loss": acc2[2, 0:1],
    }
    return dx, big, small, pieces


HBM_SPEC = pl.BlockSpec(memory_space=pl.ANY)


def _place():
    x, y, c = lax.axis_index("x"), lax.axis_index("y"), lax.axis_index("c")
    chips = [(1 - x, y), (x, 1 - y), (1 - x, 1 - y)]
    return x, y, c, chips


def _gather_phases(n):
    def tools(ins, outs, send_sems, recv_sems):
        x, y, c, _ = _place()
        slots = (2 * x + y, 2 * (1 - x) + y, 2 * x + 1 - y, 2 * (1 - x) + 1 - y)
        peers = ((1 - x, y, c), (x, 1 - y, c), (x, y, 1 - c))

        def copy(w, k, src, dst, to):
            return pltpu.make_async_remote_copy(src_ref=src, dst_ref=dst, send_sem=send_sems.at[w, k],
                                                recv_sem=recv_sems.at[w, k], device_id=to, device_id_type=MESH)

        def rows(w, core, part):
            rh = ins[w].shape[0] // 2
            if part is None:
                return pl.ds(core * rh, rh)
            return pl.ds(core * rh + part * (rh // 2), rh // 2)

        def same(w, k, slot, core, part, to):
            blk = outs[w].at[slot, rows(w, core, part), :]
            return copy(w, k, blk, blk, to)

        def sends(w):
            q, q_x, q_y, q_d = slots
            x_nbr, y_nbr, sibling = peers
            mine = rows(w, c, None)
            mk = functools.partial
            return [mk(copy, w, 0, ins[w].at[mine, :], outs[w].at[q, mine, :], x_nbr),
                    mk(copy, w, 1, ins[w].at[mine, :], outs[w].at[q, mine, :], y_nbr),
                    mk(same, w, 2, q_x, c, 0, y_nbr), mk(same, w, 3, q_y, c, 1, x_nbr),
                    mk(same, w, 4, q_x, c, None, sibling), mk(same, w, 5, q_y, c, None, sibling),
                    mk(same, w, 6, q_d, c, 0, sibling), mk(same, w, 7, q_d, c, 1, sibling),
                    mk(copy, w, 8, ins[w], outs[w].at[q], sibling)]

        return c, slots, peers, same, sends

    def first(*refs):
        _, _, _, _, sends = tools(*refs)
        for w in range(n):
            cps = sends(w)
            for k in (8, 0, 1):
                cps[k]().start()

    def second(*refs):
        c, (_, q_x, q_y, _), (x_nbr, y_nbr, _), same, sends = tools(*refs)
        for w in range(n):
            cps = sends(w)
            same(w, 0, q_x, c, None, x_nbr).wait_recv()
            cps[2]().start()
            cps[4]().start()
            same(w, 1, q_y, c, None, y_nbr).wait_recv()
            cps[3]().start()
            cps[5]().start()

    def third(*refs):
        c, (_, _, _, q_d), (x_nbr, y_nbr, _), same, sends = tools(*refs)
        for w in range(n):
            cps = sends(w)
            same(w, 2, q_d, c, 0, y_nbr).wait_recv()
            cps[6]().start()
            same(w, 3, q_d, c, 1, x_nbr).wait_recv()
            cps[7]().start()

    def last(*refs):
        c, (_, q_x, q_y, q_d), (_, _, sibling), same, sends = tools(*refs)
        for w in range(n):
            same(w, 4, q_x, 1 - c, None, sibling).wait_recv()
            same(w, 5, q_y, 1 - c, None, sibling).wait_recv()
            same(w, 6, q_d, 1 - c, 0, sibling).wait_recv()
            same(w, 7, q_d, 1 - c, 1, sibling).wait_recv()
            sends(w)[8]().wait_recv()
        for w in range(n):
            for mk_cp in sends(w):
                mk_cp().wait_send()

    return first, second, third, last


def _gather_side(shards):
    first, second, third, last = _gather_phases(len(shards))
    shapes = tuple(jax.ShapeDtypeStruct((4,) + s.shape, s.dtype) for s in shards)
    return _Side(tuple(shards), shapes, (len(shards), 9), None, ((0.0, first), (0.36, second), (0.58, third), (1.0, last)))


class _Side(NamedTuple):
    ins: tuple
    out_shapes: tuple
    nsem: tuple
    make: Callable
    phases: tuple = ()


def _swap_copies(ins, outs, send_sems, recv_sems):
    x, y, c, _ = _place()
    copies = []
    for w in range(len(ins)):
        rh = ins[w].shape[1] // 2
        for p in range(4):
            copies.append(pltpu.make_async_remote_copy(
                src_ref=ins[w].at[p, pl.ds((1 - c) * rh, rh), :], dst_ref=outs[w].at[p],
                send_sem=send_sems.at[w, p], recv_sem=recv_sems.at[w, p],
                device_id=(x, y, 1 - c), device_id_type=MESH))
    return copies


def _swap_side(grads):
    shapes = tuple(jax.ShapeDtypeStruct((4, g.shape[1] // 2, g.shape[2]), F32) for g in grads)
    return _Side(tuple(grads), shapes, (len(grads), 4), _swap_copies)


def _step1_copies(ins, outs, send_sems, recv_sems):
    n = len(ins)
    out_a, out_b = outs[:n], outs[n:]
    x, y, c, _ = _place()
    copies = []
    for w in range(n):
        rq = ins[w].shape[1] // 2
        for i in range(2):
            copies.append(pltpu.make_async_remote_copy(
                src_ref=ins[w].at[2 * (1 - x) + i, pl.ds(0, rq), :], dst_ref=out_a[w].at[i],
                send_sem=send_sems.at[w, i], recv_sem=recv_sems.at[w, i],
                device_id=(1 - x, y, c), device_id_type=MESH))
            copies.append(pltpu.make_async_remote_copy(
                src_ref=ins[w].at[2 * i + 1 - y, pl.ds(rq, rq), :], dst_ref=out_b[w].at[i],
                send_sem=send_sems.at[w, 2 + i], recv_sem=recv_sems.at[w, 2 + i],
                device_id=(x, 1 - y, c), device_id_type=MESH))
    return copies


def _step1_side(parts):
    quarter = tuple(jax.ShapeDtypeStruct((2, p.shape[1] // 2, p.shape[2]), p.dtype) for p in parts)
    return _Side(tuple(parts), quarter + quarter, (len(parts), 4), _step1_copies)


def _step2_copies(ins, outs, send_sems, recv_sems):
    n = len(ins) // 2
    in_a, in_b, out_a, out_b = ins[:n], ins[n:], outs[:n], outs[n:]
    x, y, c, _ = _place()
    copies = []
    for w in range(n):
        copies.append(pltpu.make_async_remote_copy(
            src_ref=in_a[w].at[1 - y], dst_ref=out_a[w], send_sem=send_sems.at[w, 0], recv_sem=recv_sems.at[w, 0],
            device_id=(x, 1 - y, c), device_id_type=MESH))
        copies.append(pltpu.make_async_remote_copy(
            src_ref=in_b[w].at[1 - x], dst_ref=out_b[w], send_sem=send_sems.at[w, 1], recv_sem=recv_sems.at[w, 1],
            device_id=(1 - x, y, c), device_id_type=MESH))
    return copies


def _step2_side(tas, tbs):
    one = tuple(jax.ShapeDtypeStruct(p.shape[1:], p.dtype) for p in tuple(tas) + tuple(tbs))
    return _Side(tuple(tas) + tuple(tbs), one, (len(tas), 2), _step2_copies)


def _phases_of(side, n_steps):
    if side.phases:
        return [(min(int(f * n_steps), n_steps - 1), fn) for f, fn in side.phases]

    def start(*refs):
        for cp in side.make(*refs):
            cp.start()

    def wait(*refs):
        for cp in side.make(*refs):
            cp.wait()

    return [(0, start), (n_steps - 1, wait)]


def _run_side(side, name):
    n_in, n_out = len(side.ins), len(side.out_shapes)

    def body(*refs):
        for _, fn in _phases_of(side, 1):
            fn(refs[:n_in], refs[n_in:n_in + n_out], *refs[n_in + n_out:])

    return pl.pallas_call(
        body, out_shape=list(side.out_shapes), in_specs=[HBM_SPEC] * n_in, out_specs=[HBM_SPEC] * n_out,
        scratch_shapes=[pltpu.SemaphoreType.DMA(side.nsem), pltpu.SemaphoreType.DMA(side.nsem)], name=name)(*side.ins)


def _host_call(body, side, n_steps, *, out_shape, in_specs, out_specs, scratch_shapes, args, aliases, name, sem):
    n_in, n_out, n_scr = len(in_specs), len(out_shape), len(scratch_shapes)
    if side is None:
        outs = pl.pallas_call(body, out_shape=tuple(out_shape), grid=(n_steps,), in_specs=list(in_specs),
                              out_specs=tuple(out_specs), scratch_shapes=list(scratch_shapes),
                              input_output_aliases=aliases, name=name, compiler_params=_params(sem))(*args)
        return tuple(outs), ()
    ns_in, ns_out = len(side.ins), len(side.out_shapes)

    def wrapped(*refs):
        h_in, s_in = refs[:n_in], refs[n_in:n_in + ns_in]
        o0 = n_in + ns_in
        h_out, s_out = refs[o0:o0 + n_out], refs[o0 + n_out:o0 + n_out + ns_out]
        c0 = o0 + n_out + ns_out
        h_scr, sems = refs[c0:c0 + n_scr], refs[c0 + n_scr:]
        step = pl.program_id(0)
        phases = _phases_of(side, n_steps)
        for at, fn in phases[:-1]:
            pl.when(step == at)(functools.partial(fn, s_in, s_out, *sems))
        body(*h_in, *h_out, *h_scr)
        pl.when(step == phases[-1][0])(functools.partial(phases[-1][1], s_in, s_out, *sems))

    outs = pl.pallas_call(
        wrapped, out_shape=tuple(out_shape) + tuple(side.out_shapes), grid=(n_steps,),
        in_specs=list(in_specs) + [HBM_SPEC] * ns_in, out_specs=tuple(out_specs) + (HBM_SPEC,) * ns_out,
        scratch_shapes=list(scratch_shapes) + [pltpu.SemaphoreType.DMA(side.nsem), pltpu.SemaphoreType.DMA(side.nsem)],
        input_output_aliases=aliases, name=name, compiler_params=_params(sem))(*args, *side.ins)
    return tuple(outs[:n_out]), tuple(outs[n_out:])


def _join_halves(pieces):
    n = len(pieces)

    def body(*refs):
        outs = refs[n:2 * n]
        send_sems, recv_sems = refs[2 * n:]
        x, y, c, _ = _place()

        def copy(w, slot):
            return pltpu.make_async_remote_copy(
                src_ref=outs[w].at[slot], dst_ref=outs[w].at[slot], send_sem=send_sems.at[w], recv_sem=recv_sems.at[w],
                device_id=(x, y, 1 - c), device_id_type=MESH)

        for w in range(n):
            copy(w, c).start()
        for w in range(n):
            copy(w, 1 - c).wait_recv()
            copy(w, c).wait_send()

    return pl.pallas_call(
        body, out_shape=[jax.ShapeDtypeStruct(p.shape, F32) for p in pieces],
        in_specs=[HBM_SPEC] * n, out_specs=[HBM_SPEC] * n, input_output_aliases={w: w for w in range(n)},
        scratch_shapes=[pltpu.SemaphoreType.DMA((n,)), pltpu.SemaphoreType.DMA((n,))],
        name="rs_join_halves")(*pieces)


def _add_tile_rows(rh, c):
    for cand in (512, 256, 128, 64, 32, 16, 8):
        if rh % cand == 0 and cand * c * 4 <= 2 ** 21:
            return cand
    return rh


def _add_half(grad, recv, c_idx, name):
    _, r, cc = grad.shape
    rh = r // 2
    tr = _add_tile_rows(rh, cc)
    nb = rh // tr

    def body(c_ref, g_ref, r_ref, o_ref, ob_ref):
        del c_ref
        s = g_ref[...] + r_ref[...]
        o_ref[...] = s
        ob_ref[...] = s.astype(BF16)

    blk = pl.BlockSpec((None, tr, cc), lambda p, i, c_ref: (p, i, 0))
    grid_spec = pltpu.PrefetchScalarGridSpec(
        num_scalar_prefetch=1, grid=(4, nb),
        in_specs=[pl.BlockSpec((None, tr, cc), lambda p, i, c_ref: (p, c_ref[0] * nb + i, 0)), blk],
        out_specs=(blk, blk))
    return pl.pallas_call(
        body, out_shape=(jax.ShapeDtypeStruct((4, rh, cc), F32), jax.ShapeDtypeStruct((4, rh, cc), BF16)),
        grid_spec=grid_spec, name=name, compiler_params=_params(("parallel", "parallel")))(c_idx, grad, recv)


def _rs_add1(part, recv_a, recv_b, xy_idx, name):
    _, rh, cc = part.shape
    rq = rh // 2
    tr = _add_tile_rows(rq, cc)
    nb = rq // tr

    def body(xy_ref, pa_ref, pb_ref, ra_ref, rb_ref, ta_ref, tb_ref, tab_ref, tbb_ref):
        del xy_ref
        ta = pa_ref[...] + ra_ref[...].astype(F32)
        tb = pb_ref[...] + rb_ref[...].astype(F32)
        ta_ref[...] = ta
        tb_ref[...] = tb
        tab_ref[...] = ta.astype(BF16)
        tbb_ref[...] = tb.astype(BF16)

    blk = pl.BlockSpec((None, tr, cc), lambda i, j, xy: (i, j, 0))
    grid_spec = pltpu.PrefetchScalarGridSpec(
        num_scalar_prefetch=1, grid=(2, nb),
        in_specs=[pl.BlockSpec((None, tr, cc), lambda i, j, xy: (2 * xy[0] + i, j, 0)),
                  pl.BlockSpec((None, tr, cc), lambda i, j, xy: (2 * i + xy[1], nb + j, 0)), blk, blk],
        out_specs=(blk, blk, blk, blk))
    f32s, b16s = jax.ShapeDtypeStruct((2, rq, cc), F32), jax.ShapeDtypeStruct((2, rq, cc), BF16)
    return pl.pallas_call(body, out_shape=(f32s, f32s, b16s, b16s), grid_spec=grid_spec, name=name,
                          compiler_params=_params(("parallel", "parallel")))(xy_idx, part, part, recv_a, recv_b)


def _rs_add2(ta, tb, recv_a, recv_b, xy_idx, name):
    _, rq, cc = ta.shape
    tr = _add_tile_rows(rq, cc)
    nb = rq // tr

    def body(xy_ref, ta_ref, tb_ref, ra_ref, rb_ref, o_ref):
        del xy_ref
        s = pl.program_id(0)
        fa = ta_ref[...] + ra_ref[...].astype(F32)
        fb = tb_ref[...] + rb_ref[...].astype(F32)
        o_ref[...] = jnp.where(s == 0, fa, fb)

    rblk = pl.BlockSpec((tr, cc), lambda s, j, xy: (j, 0))
    grid_spec = pltpu.PrefetchScalarGridSpec(
        num_scalar_prefetch=1, grid=(2, nb),
        in_specs=[pl.BlockSpec((None, tr, cc), lambda s, j, xy: (xy[1], j, 0)),
                  pl.BlockSpec((None, tr, cc), lambda s, j, xy: (xy[0], j, 0)), rblk, rblk],
        out_specs=pl.BlockSpec((None, tr, cc), lambda s, j, xy: (xy[2], s * nb + j, 0)))
    return pl.pallas_call(body, out_shape=jax.ShapeDtypeStruct((2, 2 * rq, cc), F32), grid_spec=grid_spec, name=name,
                          compiler_params=_params(("parallel", "parallel")))(xy_idx, ta, tb, recv_a, recv_b)


def _allreduce_small(slab):
    r = slab.shape[0]

    def body(x_ref, o_ref, buf, send_sems, recv_sems):
        x, y, c, _ = _place()
        me = 4 * x + 2 * y + c
        buf[me] = x_ref[...]
        peers = []
        for k in range(1, 8):
            kx, ky, kc = (k >> 2) & 1, (k >> 1) & 1, k & 1
            peers.append((x + kx - 2 * x * kx, y + ky - 2 * y * ky, c + kc - 2 * c * kc))

        def copy(k, slot):
            return pltpu.make_async_remote_copy(src_ref=x_ref, dst_ref=buf.at[slot], send_sem=send_sems.at[k],
                                                recv_sem=recv_sems.at[k], device_id=peers[k], device_id_type=MESH)

        for k in range(7):
            copy(k, me).start()
        for k, (px, py, pc) in enumerate(peers):
            copy(k, 4 * px + 2 * py + pc).wait_recv()
        for k in range(7):
            copy(k, me).wait_send()
        acc = buf[0]
        for j in range(1, 8):
            acc = acc + buf[j]
        o_ref[...] = acc

    vm = pl.BlockSpec(memory_space=pltpu.VMEM)
    return pl.pallas_call(
        body, out_shape=jax.ShapeDtypeStruct((r, 128), F32), in_specs=[vm], out_specs=vm,
        scratch_shapes=[pltpu.VMEM((8, r, 128), F32), pltpu.SemaphoreType.DMA((7,)), pltpu.SemaphoreType.DMA((7,))],
        name="allreduce_small")(slab)


def _pack(arrs):
    rows = []
    for a in arrs:
        v = a.reshape(-1)
        v = jnp.pad(v, (0, (-v.shape[0]) % 128))
        rows.append(v.reshape(-1, 128))
    slab = jnp.concatenate(rows, axis=0)
    return jnp.pad(slab, ((0, (-slab.shape[0]) % 8), (0, 0)))


def _unpack(slab, shapes):
    out, r0 = [], 0
    for shp in shapes:
        size = math.prod(shp)
        nr = -(-size // 128)
        out.append(slab[r0:r0 + nr].reshape(-1)[:size].reshape(shp))
        r0 += nr
    return out


BIG = ("w_in", "w_proj_ssd", "w_proj_attn", "w_out", "w_up", "w_down")
SMALL = ("b_gate", "conv_w", "conv_b", "dt_bias_f", "dt_bias_b", "a_log_f", "a_log_b", "d_skip", "ssd_norm_w",
         "ln1_g", "ln1_b", "ln2_g", "ln2_b")
ORDER = ("w_in", "b_gate", "conv_w", "conv_b", "dt_bias_f", "dt_bias_b", "a_log_f", "a_log_b", "d_skip", "ssd_norm_w",
         "w_proj_ssd", "w_proj_attn", "w_out", "ln1_g", "ln1_b", "w_up", "w_down", "ln2_g", "ln2_b")


def kernel(x, w_in, b_gate, conv_w, conv_b, dt_bias_f, dt_bias_b, a_log_f, a_log_b, d_skip, ssd_norm_w, w_proj_ssd, w_proj_attn, w_out, ln1_g, ln1_b, w_up, w_down, ln2_g, ln2_b, loss_target, m_w_in, m_b_gate, m_conv_w, m_conv_b, m_dt_bias_f, m_dt_bias_b, m_a_log_f, m_a_log_b, m_d_skip, m_ssd_norm_w, m_w_proj_ssd, m_w_proj_attn, m_w_out, m_ln1_g, m_ln1_b, m_w_up, m_w_down, m_ln2_g, m_ln2_b, v_w_in, v_b_gate, v_conv_w, v_conv_b, v_dt_bias_f, v_dt_bias_b, v_a_log_f, v_a_log_b, v_d_skip, v_ssd_norm_w, v_w_proj_ssd, v_w_proj_attn, v_w_out, v_ln1_g, v_ln1_b, v_w_up, v_w_down, v_ln2_g, v_ln2_b):
    w = dict(w_in=w_in, b_gate=b_gate, conv_w=conv_w, conv_b=conv_b, dt_bias_f=dt_bias_f, dt_bias_b=dt_bias_b,
             a_log_f=a_log_f, a_log_b=a_log_b, d_skip=d_skip, ssd_norm_w=ssd_norm_w, w_proj_ssd=w_proj_ssd,
             w_proj_attn=w_proj_attn, w_out=w_out, ln1_g=ln1_g, ln1_b=ln1_b, w_up=w_up, w_down=w_down, ln2_g=ln2_g, ln2_b=ln2_b)
    m = dict(w_in=m_w_in, b_gate=m_b_gate, conv_w=m_conv_w, conv_b=m_conv_b, dt_bias_f=m_dt_bias_f, dt_bias_b=m_dt_bias_b,
             a_log_f=m_a_log_f, a_log_b=m_a_log_b, d_skip=m_d_skip, ssd_norm_w=m_ssd_norm_w, w_proj_ssd=m_w_proj_ssd,
             w_proj_attn=m_w_proj_attn, w_out=m_w_out, ln1_g=m_ln1_g, ln1_b=m_ln1_b, w_up=m_w_up, w_down=m_w_down,
             ln2_g=m_ln2_g, ln2_b=m_ln2_b)
    v = dict(w_in=v_w_in, b_gate=v_b_gate, conv_w=v_conv_w, conv_b=v_conv_b, dt_bias_f=v_dt_bias_f, dt_bias_b=v_dt_bias_b,
             a_log_f=v_a_log_f, a_log_b=v_a_log_b, d_skip=v_d_skip, ssd_norm_w=v_ssd_norm_w, w_proj_ssd=v_w_proj_ssd,
             w_proj_attn=v_w_proj_attn, w_out=v_w_out, ln1_g=v_ln1_g, ln1_b=v_ln1_b, w_up=v_w_up, w_down=v_w_down,
             ln2_g=v_ln2_g, ln2_b=v_ln2_b)
    xi, yi, ci = lax.axis_index("x"), lax.axis_index("y"), lax.axis_index("c")
    shard = 2 * xi + yi

    (g_in,) = _run_side(_gather_side([w["w_in"].astype(BF16)]), "allgather_w_in")
    wts = {"w_in_p": _perm_from_shards(g_in), "pending": [w[n].astype(BF16) for n in EARLY]}

    cw_slab = jnp.zeros((KCONV, 4, CONVD // 4), F32)
    cw_slab = lax.dynamic_update_slice(cw_slab, conv_w[:, None, :] * 0.5, (0, shard, 0))
    conv_w_all = _unpack(_allreduce_small(_pack([cw_slab])), [(KCONV, CONVD)])[0]

    sm = {n: w[n] for n in SMALL}
    sm["conv_w"] = conv_w_all
    c_idx = jnp.reshape(ci, (1,)).astype(jnp.int32)
    xy_idx = jnp.stack([xi, yi, ci]).astype(jnp.int32)
    dx, big, small, pieces = _local_grads(x[0], loss_target[0], wts, sm, rs_idx=(c_idx, xy_idx))

    names = list(SMALL) + ["loss"]
    shapes = [small[n].shape for n in names]
    red = dict(zip(names, _unpack(_allreduce_small(_pack([small[n] for n in names])), shapes)))
    loss = red["loss"].reshape(())
    gsm = {n: red[n] for n in SMALL}
    conv_w_grad_shard = lax.dynamic_slice_in_dim(gsm["conv_w"].reshape(KCONV, 4, CONVD // 4), shard, 1, axis=1)
    gsm["conv_w"] = conv_w_grad_shard.reshape(KCONV, CONVD // 4)

    joined = _join_halves([pieces[n] for n in BIG])
    gbig = {n: j.reshape(w[n].shape) for n, j in zip(BIG, joined)}

    grads, deltas, new_m, new_v = {}, {}, {}, {}
    for n in BIG:
        grads[n] = gbig[n]
        if n == "w_in":
            gt = gbig[n].T
            dlt, nmt, nvt = _adamw(w[n].T, gt, m[n].T, v[n].T, f"adamw_{n}")
            grads[n], deltas[n], new_m[n], new_v[n] = gt.T, dlt.T, nmt.T, nvt.T
            continue
        deltas[n], new_m[n], new_v[n] = _adamw(w[n], gbig[n], m[n], v[n], f"adamw_{n}")
    sshapes = [w[n].shape for n in SMALL]
    d_s, m_s, v_s = _adamw(_pack([w[n] for n in SMALL]), _pack([gsm[n] for n in SMALL]),
                           _pack([m[n] for n in SMALL]), _pack([v[n] for n in SMALL]), "adamw_small")
    for n, dd, mm, vv in zip(SMALL, _unpack(d_s, sshapes), _unpack(m_s, sshapes), _unpack(v_s, sshapes)):
        grads[n], deltas[n], new_m[n], new_v[n] = gsm[n], dd, mm, vv

    return (loss, dx[None], *[grads[n] for n in ORDER], *[deltas[n] for n in ORDER],
            *[new_m[n] for n in ORDER], *[new_v[n] for n in ORDER])
```

```python
import functools
import math
from typing import Callable, NamedTuple

import jax
import numpy as np
import jax.numpy as jnp
from jax import lax
from jax.experimental import pallas as pl
from jax.experimental.pallas import tpu as pltpu

F32, BF16 = jnp.float32, jnp.bfloat16
MESH = pl.DeviceIdType.MESH

D = 1024
DI = 2048
NH = 32
HP = 64
NG = 4
NS = 128
Q = 128
CONVD = 3072
KCONV = 5
DFF = 4096
AH = 64
ATT_HALF = 64
DILATIONS = (1, 4, 16)
IN_COLS = 9536
OZ, OGATE, OXBC, OKV, OQ, ODT, UW = 0, 2048, 4096, 7168, 8704, 9472, 9728
ALPHA = 2.0 ** 0.25
NORM_EPS = 1e-5
ADAM_LR, ADAM_B1, ADAM_B2, ADAM_EPS, ADAM_WD, ADAM_STEP = 0.001, 0.9, 0.999, 1e-8, 0.01, 10
VMEM_LIMIT = 56 * 2 ** 20
NEG = -1e30


def _params(sem):
    return pltpu.CompilerParams(dimension_semantics=sem, vmem_limit_bytes=VMEM_LIMIT)


def _sigmoid(x):
    return 1.0 / (1.0 + jnp.exp(-x))


def _softplus(x):
    e = jnp.exp(-jnp.abs(x))
    small = e * (1.0 - e * (0.5 - e * (1.0 / 3.0)))
    return jnp.maximum(x, 0.0) + jnp.where(e < 0.01, small, jnp.log(1.0 + e))


def _split3(a):
    hi = a.astype(BF16)
    r = a - hi.astype(F32)
    mid = r.astype(BF16)
    lo = (r - mid.astype(F32)).astype(BF16)
    return hi, mid, lo


def _dot01(a, m01):
    hi, mid, lo = _split3(a)
    d = lambda p: jnp.dot(p, m01, preferred_element_type=F32)
    return d(hi) + d(mid) + d(lo)


def _dot01_l(m01, a):
    hi, mid, lo = _split3(a)
    d = lambda p: jnp.dot(m01, p, preferred_element_type=F32)
    return d(hi) + d(mid) + d(lo)


def _dot_nt(a, b):
    return lax.dot_general(a, b, (((1,), (1,)), ((), ())), preferred_element_type=F32)


def _iota(shape, dim):
    return lax.broadcasted_iota(jnp.int32, shape, dim)


def _mm_tn(a, b, *, tka, tn, tt, name, out_shards=None):
    t, ka = a.shape
    n = b.shape[1]
    if out_shards:
        assert tn == n // out_shards
        out_shape = jax.ShapeDtypeStruct((out_shards, ka, tn), F32)
        o_spec = pl.BlockSpec((None, tka, tn), lambda i, j, s: (j, i, 0))
    else:
        out_shape = jax.ShapeDtypeStruct((ka, n), F32)
        o_spec = pl.BlockSpec((tka, tn), lambda i, j, s: (i, j))

    def body(a_ref, b_ref, o_ref):
        s = pl.program_id(2)
        part = lax.dot_general(a_ref[...].astype(BF16), b_ref[...].astype(BF16), (((0,), (0,)), ((), ())),
                               preferred_element_type=F32)

        @pl.when(s == 0)
        def _():
            o_ref[...] = part

        @pl.when(s > 0)
        def _():
            o_ref[...] += part

    return pl.pallas_call(
        body, out_shape=out_shape, grid=(ka // tka, n // tn, t // tt),
        in_specs=[pl.BlockSpec((tt, tka), lambda i, j, s: (s, i)), pl.BlockSpec((tt, tn), lambda i, j, s: (s, j))],
        out_specs=o_spec, name=name, compiler_params=_params(("parallel", "parallel", "arbitrary")))(a, b)


def _d_x(du, w_in_p, dpre1, side=None):
    t = du.shape[0]
    tm, tc = 1024, 2432
    nc = UW // tc

    def body(a_ref, b_ref, add_ref, o_ref):
        c = pl.program_id(0) % nc
        part = _dot_nt(a_ref[...], b_ref[...])

        @pl.when(c == 0)
        def _():
            o_ref[...] = part + ALPHA * add_ref[...]

        @pl.when(c > 0)
        def _():
            o_ref[...] += part

    outs, side_outs = _host_call(
        body, side, (t // tm) * nc, out_shape=(jax.ShapeDtypeStruct((t, D), F32),),
        in_specs=[pl.BlockSpec((tm, tc), lambda s: (s // nc, s % nc)), pl.BlockSpec((D, tc), lambda s: (0, s % nc)),
                  pl.BlockSpec((tm, D), lambda s: (s // nc, 0))],
        out_specs=(pl.BlockSpec((tm, D), lambda s: (s // nc, 0)),),
        scratch_shapes=[], args=(du, w_in_p, dpre1), aliases={}, name="d_x", sem=("arbitrary",))
    return outs[0], side_outs


def _in_proj(xb, w_in_p, side=None):
    t, k = xb.shape
    tm, tn = 1024, 2432
    nm, nn = t // tm, UW // tn

    def body(a_ref, b_ref, o_ref):
        o_ref[...] = jnp.dot(a_ref[...], b_ref[...], preferred_element_type=F32)

    outs, side_outs = _host_call(
        body, side, nm * nn, out_shape=(jax.ShapeDtypeStruct((t, UW), F32),),
        in_specs=[pl.BlockSpec((tm, k), lambda s: (s % nm, 0)), pl.BlockSpec((k, tn), lambda s: (0, s // nm))],
        out_specs=(pl.BlockSpec((tm, tn), lambda s: (s % nm, s // nm)),),
        scratch_shapes=[], args=(xb, w_in_p), aliases={}, name="in_proj", sem=("arbitrary",))
    return outs[0], side_outs


CONV_TM = 512
CONV_TC = 1024
CONV_RC = 64
CONV_CC = 256


def _halo_specs(t, tm, tc, col0):
    nb8 = t // 8
    r8 = tm // 8
    return [
        pl.BlockSpec((8, tc), lambda i, j: (jnp.maximum(i * r8 - 1, 0), col0 + j)),
        pl.BlockSpec((tm, tc), lambda i, j: (i, col0 + j)),
        pl.BlockSpec((8, tc), lambda i, j: (jnp.minimum((i + 1) * r8, nb8 - 1), col0 + j)),
    ]


def _fill_ext(ext, prev_ref, cur_ref, next_ref, tm, i, last):
    ext[0:8, :] = jnp.where(i > 0, prev_ref[...], 0.0)
    ext[8:8 + tm, :] = cur_ref[...]
    ext[8 + tm:16 + tm, :] = jnp.where(i < last, next_ref[...], 0.0)


def _conv_fwd(u, conv_w, conv_b):
    t = u.shape[0]
    tm, tc = CONV_TM, CONV_TC

    def body(prev_ref, cur_ref, next_ref, w_ref, b_ref, o_ref, ext):
        _fill_ext(ext, prev_ref, cur_ref, next_ref, tm, pl.program_id(0), t // tm - 1)
        for c0 in range(0, tc, CONV_CC):
            cs = slice(c0, c0 + CONV_CC)
            w = w_ref[:, cs]
            for r0 in range(0, tm, CONV_RC):
                acc = jnp.broadcast_to(b_ref[:, cs], (CONV_RC, CONV_CC))
                for k in range(KCONV):
                    acc = acc + w[k:k + 1, :] * ext[pl.ds(r0 + 6 + k, CONV_RC), cs]
                o_ref[r0:r0 + CONV_RC, cs] = acc * _sigmoid(acc)

    return pl.pallas_call(
        body, out_shape=jax.ShapeDtypeStruct((t, CONVD), F32), grid=(t // tm, CONVD // tc),
        in_specs=_halo_specs(t, tm, tc, OXBC // tc) + [
            pl.BlockSpec((KCONV, tc), lambda i, j: (0, j)), pl.BlockSpec((1, tc), lambda i, j: (0, j))],
        out_specs=pl.BlockSpec((tm, tc), lambda i, j: (i, j)),
        scratch_shapes=[pltpu.VMEM((tm + 16, tc), F32)],
        name="conv_fwd", compiler_params=_params(("parallel", "parallel")))(u, u, u, conv_w, conv_b)


def _conv_dpre(u, dxs, dy, dbc, dsk_row, conv_w, conv_b):
    t = u.shape[0]
    tm, tc = CONV_TM, CONV_TC
    r8 = tm // 8
    nb8 = t // 8
    c0 = OXBC // tc

    def body(uprev, ucur, unext, f_ref, y_ref, cf_ref, dsk_ref, w_ref, bias_ref, dpre_ref, dw_ref, db_ref, ext):
        j = pl.program_id(0)
        i = pl.program_id(1)
        _fill_ext(ext, uprev, ucur, unext, tm, i, t // tm - 1)
        is_xs = j < 2
        dw_cols, db_cols = [], []
        for c0 in range(0, tc, CONV_CC):
            cs = slice(c0, c0 + CONV_CC)
            w = w_ref[:, cs]
            dsk = dsk_ref[:, cs]
            dw_acc = [jnp.zeros((1, CONV_CC), F32) for _ in range(KCONV)]
            db_acc = jnp.zeros((1, CONV_CC), F32)
            for r0 in range(0, tm, CONV_RC):
                rs = slice(r0, r0 + CONV_RC)
                taps = [ext[pl.ds(r0 + 6 + k, CONV_RC), cs] for k in range(KCONV)]
                pre = jnp.broadcast_to(bias_ref[:, cs], (CONV_RC, CONV_CC))
                for k in range(KCONV):
                    pre = pre + w[k:k + 1, :] * taps[k]
                s = _sigmoid(pre)
                up = jnp.where(is_xs, f_ref[rs, cs] + dsk * y_ref[rs, cs], cf_ref[rs, cs])
                dpre = up * (s * (1.0 + pre * (1.0 - s)))
                dpre_ref[rs, cs] = dpre
                for k in range(KCONV):
                    dw_acc[k] = dw_acc[k] + jnp.sum(dpre * taps[k], axis=0, keepdims=True)
                db_acc = db_acc + jnp.sum(dpre, axis=0, keepdims=True)
            dw_cols.append(jnp.concatenate(dw_acc + [jnp.zeros((8 - KCONV, CONV_CC), F32)], axis=0))
            db_cols.append(jnp.broadcast_to(db_acc, (8, CONV_CC)))
        dw_part = jnp.concatenate(dw_cols, axis=1)
        db_part = jnp.concatenate(db_cols, axis=1)

        @pl.when(i == 0)
        def _():
            dw_ref[...] = dw_part
            db_ref[...] = db_part

        @pl.when(i > 0)
        def _():
            dw_ref[...] += dw_part
            db_ref[...] += db_part

    xs_spec = pl.BlockSpec((tm, tc), lambda j, i: (jnp.where(j < 2, i, 0), jnp.minimum(j, 1)))
    bc_spec = pl.BlockSpec((tm, tc), lambda j, i: (jnp.where(j == 2, i, 0), 0))
    in_specs = [
        pl.BlockSpec((8, tc), lambda j, i: (jnp.maximum(i * r8 - 1, 0), c0 + j)),
        pl.BlockSpec((tm, tc), lambda j, i: (i, c0 + j)),
        pl.BlockSpec((8, tc), lambda j, i: (jnp.minimum((i + 1) * r8, nb8 - 1), c0 + j)),
        xs_spec, xs_spec, bc_spec,
        pl.BlockSpec((1, tc), lambda j, i: (0, jnp.minimum(j, 1))),
        pl.BlockSpec((KCONV, tc), lambda j, i: (0, j)), pl.BlockSpec((1, tc), lambda j, i: (0, j)),
    ]
    return pl.pallas_call(
        body,
        out_shape=(jax.ShapeDtypeStruct((t, CONVD), F32), jax.ShapeDtypeStruct((8, CONVD), F32),
                   jax.ShapeDtypeStruct((8, CONVD), F32)),
        grid=(CONVD // tc, t // tm), in_specs=in_specs,
        out_specs=(pl.BlockSpec((tm, tc), lambda j, i: (i, j)),
                   pl.BlockSpec((8, tc), lambda j, i: (0, j)), pl.BlockSpec((8, tc), lambda j, i: (0, j))),
        scratch_shapes=[pltpu.VMEM((tm + 16, tc), F32)],
        name="conv_dpre", compiler_params=_params(("parallel", "arbitrary")))(
            u, u, u, dxs, dy, dbc, dsk_row, conv_w, conv_b)


def _conv_dx(du, dpre, conv_w):
    t = dpre.shape[0]
    tm, tc = CONV_TM, CONV_TC
    r8 = tm // 8
    nb8 = t // 8

    def body(prev_ref, cur_ref, next_ref, w_ref, du_in, du_out, ext):
        del du_in
        _fill_ext(ext, prev_ref, cur_ref, next_ref, tm, pl.program_id(1), t // tm - 1)
        for c0 in range(0, tc, CONV_CC):
            cs = slice(c0, c0 + CONV_CC)
            w = w_ref[:, cs]
            for r0 in range(0, tm, CONV_RC):
                acc = jnp.zeros((CONV_RC, CONV_CC), F32)
                for k in range(KCONV):
                    acc = acc + w[k:k + 1, :] * ext[pl.ds(r0 + 10 - k, CONV_RC), cs]
                du_out[r0:r0 + CONV_RC, cs] = acc.astype(du_out.dtype)

    in_specs = [
        pl.BlockSpec((8, tc), lambda j, i: (jnp.maximum(i * r8 - 1, 0), j)),
        pl.BlockSpec((tm, tc), lambda j, i: (i, j)),
        pl.BlockSpec((8, tc), lambda j, i: (jnp.minimum((i + 1) * r8, nb8 - 1), j)),
        pl.BlockSpec((KCONV, tc), lambda j, i: (0, j)),
        pl.BlockSpec(memory_space=pl.ANY),
    ]
    return pl.pallas_call(
        body, out_shape=jax.ShapeDtypeStruct(du.shape, du.dtype), grid=(CONVD // tc, t // tm), in_specs=in_specs,
        out_specs=pl.BlockSpec((tm, tc), lambda j, i: (i, OXBC // tc + j)),
        scratch_shapes=[pltpu.VMEM((tm + 16, tc), F32)], input_output_aliases={4: 0},
        name="conv_dx", compiler_params=_params(("parallel", "parallel")))(dpre, dpre, dpre, conv_w, du)


def _ssd_prep(u, par):
    t = u.shape[0]

    def body(dtr_ref, par_ref, dt_ref, cs_ref):
        lane = _iota((1, 128), 1)
        arow = jnp.where(lane < 2 * 32, -jnp.exp(par_ref[1:2, :]), 0.0)
        dt = _softplus(dtr_ref[...] + par_ref[0:1, :])
        a = dt * arow
        ri = _iota((Q, Q), 0)
        ci = _iota((Q, Q), 1)
        cs_f = _dot01_l((ci <= ri).astype(BF16), a)
        cs_b = _dot01_l((ci >= ri).astype(BF16), a)
        dt_ref[...] = dt
        cs_ref[...] = jnp.where(lane < 32, cs_f, cs_b)

    blk = pl.BlockSpec((Q, 128), lambda c: (c, 0))
    shp = jax.ShapeDtypeStruct((t, 128), F32)
    return pl.pallas_call(
        body, out_shape=(shp, shp), grid=(t // Q,),
        in_specs=[pl.BlockSpec((Q, 128), lambda c: (c, ODT // 128)), pl.BlockSpec((8, 128), lambda c: (0, 0))],
        out_specs=(blk, blk), name="ssd_prep", compiler_params=_params(("parallel",)))(u, par)


def _ssd_common(dtr_ref, par_ref, dtv_ref, csv_ref, rev):
    raw = dtr_ref[...]
    lane = _iota((1, 128), 1)
    mine = (lane >= 32 * rev) & (lane < 32 * rev + 32)
    bias = par_ref[0:1, :]
    arow = jnp.where(mine, -jnp.exp(par_ref[1:2, :]), 0.0)
    dt = dtv_ref[...]
    ri = _iota((Q, Q), 0)
    ci = _iota((Q, Q), 1)
    tri = (ci >= ri) if rev else (ci <= ri)
    trit = (ci <= ri) if rev else (ci >= ri)
    cs = jnp.where(mine, csv_ref[...], 0.0)
    return raw, bias, arow, mine, dt, cs, tri, trit


def _expand_mat(rev):
    r = np.arange(128)[:, None]
    c = np.arange(DI)[None, :]
    return jnp.asarray(r == (c // HP) + 32 * rev, BF16)


def _sum_mat(rev):
    r = np.arange(DI)[:, None]
    c = np.arange(128)[None, :]
    return jnp.asarray(c == (r // HP) + 32 * rev, BF16)


def _ssd_fwd(xbc, u, par, dtv, csv, y_add=None, *, rev):
    t = xbc.shape[0]
    nc = t // Q
    end = 0 if rev else Q - 1
    cmap = (lambda c: nc - 1 - c) if rev else (lambda c: c)

    def body(xbc_ref, dtr_ref, par_ref, dtv_ref, csv_ref, ex_ref, *rest):
        yadd_ref = rest[0] if y_add is not None else None
        y_ref, st_ref, h_scr = rest[-3:]
        step = pl.program_id(0)

        @pl.when(step == 0)
        def _():
            h_scr[...] = jnp.zeros((NS, DI), F32)

        raw, bias, arow, mine, dt, cs, tri, trit = _ssd_common(dtr_ref, par_ref, dtv_ref, csv_ref, rev)
        cst = cs.T
        dtt = dt.T
        tot_col = cst[:, end:end + 1]
        wt = dtt * jnp.exp(tot_col - cst)
        ecs_all = jnp.exp(cs)
        gam = jnp.exp(cs[end:end + 1, :])
        gam_x = _dot01(jnp.broadcast_to(gam, (8, 128)), ex_ref[...])[0:1, :]
        lane = _iota((Q, 128), 1)
        sel = lane < HP
        st_ref[...] = h_scr[...]
        for g in range(NG):
            bg = xbc_ref[:, DI + NS * g:DI + NS * (g + 1)]
            cg = xbc_ref[:, DI + NG * NS + NS * g:DI + NG * NS + NS * (g + 1)]
            cb = _dot_nt(cg.astype(BF16), bg.astype(BF16))
            bt = bg.T
            for k in range(4):
                lo = 512 * g + 128 * k
                xp = xbc_ref[:, lo:lo + 128].astype(BF16)
                hp = h_scr[:, lo:lo + 128]
                rhs = jnp.concatenate([xp, hp.astype(BF16)], axis=0)
                lhs, bts = [], []
                for j in range(2):
                    hc = 8 * g + 2 * k + j + 32 * rev
                    csc = jnp.broadcast_to(cs[:, hc:hc + 1], (Q, Q))
                    lm = jnp.exp(jnp.where(tri, csc - cst[hc:hc + 1, :], NEG)) * dtt[hc:hc + 1, :]
                    mh = (cb * lm).astype(BF16)
                    ec = (jnp.broadcast_to(ecs_all[:, hc:hc + 1], (Q, NS)) * cg).astype(BF16)
                    lhs.append(jnp.concatenate([mh, ec], axis=1))
                    bts.append((bt * wt[hc:hc + 1, :]).astype(BF16))
                ys = jnp.dot(jnp.concatenate(lhs, axis=0), rhs, preferred_element_type=F32)
                ss = jnp.dot(jnp.concatenate(bts, axis=0), xp, preferred_element_type=F32)
                yp = jnp.where(sel, ys[0:Q], ys[Q:2 * Q])
                y_ref[:, lo:lo + 128] = yp if yadd_ref is None else yp + yadd_ref[:, lo:lo + 128]
                h_scr[:, lo:lo + 128] = gam_x[:, lo:lo + 128] * hp + jnp.where(sel, ss[0:NS], ss[NS:2 * NS])

    return pl.pallas_call(
        body,
        out_shape=(jax.ShapeDtypeStruct((t, DI), F32), jax.ShapeDtypeStruct((nc, NS, DI), F32)),
        grid=(nc,),
        in_specs=[pl.BlockSpec((Q, CONVD), lambda c: (cmap(c), 0)),
                  pl.BlockSpec((Q, 128), lambda c: (cmap(c), ODT // 128)),
                  pl.BlockSpec((8, 128), lambda c: (0, 0)),
                  pl.BlockSpec((Q, 128), lambda c: (cmap(c), 0)), pl.BlockSpec((Q, 128), lambda c: (cmap(c), 0)),
                  pl.BlockSpec((128, DI), lambda c: (0, 0))]
        + ([pl.BlockSpec((Q, DI), lambda c: (cmap(c), 0))] if y_add is not None else []),
        out_specs=(pl.BlockSpec((Q, DI), lambda c: (cmap(c), 0)),
                   pl.BlockSpec((None, NS, DI), lambda c: (cmap(c), 0, 0))),
        scratch_shapes=[pltpu.VMEM((NS, DI), F32)],
        name="ssd_fwd_rev" if rev else "ssd_fwd", compiler_params=_params(("arbitrary",)))(
            xbc, u, par, dtv, csv, _expand_mat(rev), *([y_add] if y_add is not None else []))


def _ssd_bwd(xbc, u, par, dtv, csv, dy, st, *, rev, add=None, side=None):
    t = xbc.shape[0]
    nc = t // Q
    end = 0 if rev else Q - 1
    cmap = (lambda c: c) if rev else (lambda c: nc - 1 - c)

    def body(xbc_ref, dtr_ref, par_ref, dtv_ref, csv_ref, dy_ref, hin_ref, ex_ref, sm_ref, *rest):
        addx_ref, addbc_ref, addt_ref = rest[:3] if add is not None else (None, None, None)
        dxs_ref, dbc_ref, ddt_ref, acc_ref, dh_scr = rest[-5:]
        step = pl.program_id(0)

        @pl.when(step == 0)
        def _():
            dh_scr[...] = jnp.zeros((NS, DI), F32)

        raw, bias, arow, mine, dt, cs, tri, trit = _ssd_common(dtr_ref, par_ref, dtv_ref, csv_ref, rev)
        ri = _iota((Q, Q), 0)
        ci = _iota((Q, Q), 1)
        stri = ((ri > ci) if rev else (ri < ci)).astype(BF16)
        strit = ((ci > ri) if rev else (ci < ri)).astype(BF16)
        cst = cs.T
        dtt = dt.T
        et = jnp.exp(cst)
        ecs_all = jnp.exp(cs)
        ws_all = jnp.exp(cs[end:end + 1, :] - cs)
        expand = ex_ref[...]
        summat = sm_ref[...]
        gam = jnp.exp(cs[end:end + 1, :])
        gam_x = _dot01(jnp.broadcast_to(gam, (8, 128)), expand)[0:1, :]
        dt_hi, dt_mid, _ = _split3(dt)
        dtx = (jnp.dot(dt_hi, expand, preferred_element_type=F32)
               + jnp.dot(dt_mid, expand, preferred_element_type=F32))
        lane = _iota((Q, 128), 1)
        sel = lane < HP
        dho = dh_scr[...]
        t3 = jnp.sum(dho * hin_ref[...], axis=0, keepdims=True) * gam_x
        dxs_cols, dxs2_cols, yoff_cols, a1_rows = [], [], [], []
        for g in range(NG):
            bg = xbc_ref[:, DI + NS * g:DI + NS * (g + 1)]
            cg = xbc_ref[:, DI + NG * NS + NS * g:DI + NG * NS + NS * (g + 1)]
            bb = bg.astype(BF16)
            cbf = cg.astype(BF16)
            cb = _dot_nt(cbf, bb)
            cbt = _dot_nt(bb, cbf)
            ct = cg.T
            bdh = jnp.dot(bb, dho[:, 512 * g:512 * (g + 1)].astype(BF16), preferred_element_type=F32)
            dcb = jnp.zeros((Q, Q), F32)
            dcg = jnp.zeros((Q, NS), F32)
            dbg = jnp.zeros((Q, NS), F32)
            for k in range(4):
                lo = 512 * g + 128 * k
                xpf = xbc_ref[:, lo:lo + 128]
                xp = xpf.astype(BF16)
                dyp = dy_ref[:, lo:lo + 128]
                dypb = dyp.astype(BF16)
                hinp = hin_ref[:, lo:lo + 128].astype(BF16)
                dhp = dho[:, lo:lo + 128]
                es, ws, lmds, mts, ctes, dyms, ecbs = [], [], [], [], [], [], []
                for j in range(2):
                    hc = 8 * g + 2 * k + j + 32 * rev
                    csc = jnp.broadcast_to(cs[:, hc:hc + 1], (Q, Q))
                    csr = cst[hc:hc + 1, :]
                    lmds.append(jnp.exp(jnp.where(tri, csc - csr, NEG)) * dtt[hc:hc + 1, :])
                    lmb = jnp.exp(jnp.where(trit, csr - csc, NEG))
                    mts.append((cbt * lmb).astype(BF16))
                    dyms.append(jnp.where(sel if j == 0 else ~sel, dyp, 0.0).astype(BF16))
                    ecs = jnp.broadcast_to(ecs_all[:, hc:hc + 1], (Q, NS))
                    es.append(ecs)
                    ws.append(jnp.broadcast_to(ws_all[:, hc:hc + 1], (Q, NS)))
                    ecbs.append((ecs * cg).astype(BF16))
                    ctes.append((ct * et[hc:hc + 1, :]).astype(BF16))
                by_dy = jnp.dot(jnp.concatenate(mts + ctes, axis=0), dypb, preferred_element_type=F32)
                dmm = _dot_nt(jnp.concatenate(dyms, axis=0), xp)
                dm0, dm1 = dmm[0:Q] * lmds[0], dmm[Q:2 * Q] * lmds[1]
                dcb = dcb + dm0 + dm1
                rr = jnp.dot(jnp.concatenate([dm0 * cb, dm1 * cb], axis=0).astype(BF16), stri, preferred_element_type=F32)
                a1_rows.append(jnp.sum(jnp.where(tri, rr[0:Q], 0.0), axis=0, keepdims=True))
                a1_rows.append(jnp.sum(jnp.where(tri, rr[Q:2 * Q], 0.0), axis=0, keepdims=True))
                yo = jnp.dot(jnp.concatenate(ecbs, axis=0), hinp, preferred_element_type=F32)
                e_p = jnp.where(sel, es[0], es[1])
                w_p = jnp.where(sel, ws[0], ws[1])
                d2 = w_p * bdh[:, 128 * k:128 * (k + 1)]
                dxs2_cols.append(d2)
                dxs_cols.append(jnp.where(sel, by_dy[0:Q], by_dy[Q:2 * Q]) + d2)
                yoff_cols.append(jnp.where(sel, yo[0:Q], yo[Q:2 * Q]))
                dcg = dcg + _dot_nt((e_p * dyp).astype(BF16), hinp)
                dbg = dbg + _dot_nt((w_p * dtx[:, lo:lo + 128] * xpf).astype(BF16), dhp.astype(BF16))
                dh_scr[:, lo:lo + 128] = (gam_x[:, lo:lo + 128] * dhp
                                          + jnp.where(sel, by_dy[2 * Q:3 * Q], by_dy[3 * Q:4 * Q]))
            dcg = dcg + jnp.dot(dcb.astype(BF16), bb, preferred_element_type=F32)
            dbg = dbg + jnp.dot(dcb.T.astype(BF16), cbf, preferred_element_type=F32)
            lo_b, lo_c = NS * g, NG * NS + NS * g
            if addbc_ref is not None:
                dbg = dbg + addbc_ref[:, lo_b:lo_b + NS]
                dcg = dcg + addbc_ref[:, lo_c:lo_c + NS]
            dbc_ref[:, lo_b:lo_b + NS] = dbg
            dbc_ref[:, lo_c:lo_c + NS] = dcg
        dxs = jnp.concatenate(dxs_cols, axis=1)
        dxs_ref[...] = dxs * dtx if addx_ref is None else dxs * dtx + addx_ref[...]
        xs = xbc_ref[:, 0:DI]
        stacked = jnp.concatenate([xs * dxs, xs * jnp.concatenate(dxs2_cols, axis=1),
                                   dy_ref[...] * jnp.concatenate(yoff_cols, axis=1),
                                   jnp.broadcast_to(t3, (8, DI))], axis=0).astype(BF16)
        sums = jnp.dot(stacked, summat, preferred_element_type=F32)
        rx, rx2, ryo, c0 = sums[0:Q], sums[Q:2 * Q], sums[2 * Q:3 * Q], sums[3 * Q:3 * Q + 1]
        zero32 = jnp.zeros((32, Q), F32)
        a1t = jnp.concatenate(([zero32] if rev else []) + a1_rows + [zero32] * (2 if rev else 3), axis=0)
        da = (a1t.T + jnp.dot(trit.astype(BF16), ryo.astype(BF16), preferred_element_type=F32)
              + jnp.dot(strit, (dt * rx2).astype(BF16), preferred_element_type=F32) + jnp.where(mine, c0, 0.0))
        ddt = rx + da * arow
        ddtr = ddt * _sigmoid(raw + bias)
        ddt_ref[...] = ddtr if addt_ref is None else ddtr + addt_ref[...]
        part = jnp.concatenate([jnp.sum(ddtr, axis=0, keepdims=True),
                                jnp.sum(da * dt, axis=0, keepdims=True) * arow,
                                jnp.zeros((6, 128), F32)], axis=0)

        @pl.when(step == 0)
        def _():
            acc_ref[...] = part

        @pl.when(step > 0)
        def _():
            acc_ref[...] += part

    outs, side_outs = _host_call(
        body, side, nc,
        out_shape=(jax.ShapeDtypeStruct((t, DI), F32), jax.ShapeDtypeStruct((t, 2 * NG * NS), F32),
                   jax.ShapeDtypeStruct((t, 128), F32), jax.ShapeDtypeStruct((8, 128), F32)),
        in_specs=[pl.BlockSpec((Q, CONVD), lambda c: (cmap(c), 0)),
                  pl.BlockSpec((Q, 128), lambda c: (cmap(c), ODT // 128)),
                  pl.BlockSpec((8, 128), lambda c: (0, 0)),
                  pl.BlockSpec((Q, 128), lambda c: (cmap(c), 0)), pl.BlockSpec((Q, 128), lambda c: (cmap(c), 0)),
                  pl.BlockSpec((Q, DI), lambda c: (cmap(c), 0)),
                  pl.BlockSpec((None, NS, DI), lambda c: (cmap(c), 0, 0)),
                  pl.BlockSpec((128, DI), lambda c: (0, 0)), pl.BlockSpec((DI, 128), lambda c: (0, 0))]
        + ([pl.BlockSpec((Q, DI), lambda c: (cmap(c), 0)), pl.BlockSpec((Q, 2 * NG * NS), lambda c: (cmap(c), 0)),
            pl.BlockSpec((Q, 128), lambda c: (cmap(c), 0))] if add is not None else []),
        out_specs=(pl.BlockSpec((Q, DI), lambda c: (cmap(c), 0)),
                   pl.BlockSpec((Q, 2 * NG * NS), lambda c: (cmap(c), 0)),
                   pl.BlockSpec((Q, 128), lambda c: (cmap(c), 0)),
                   pl.BlockSpec((8, 128), lambda c: (0, 0))),
        scratch_shapes=[pltpu.VMEM((NS, DI), F32)],
        args=(xbc, u, par, dtv, csv, dy, st, _expand_mat(rev), _sum_mat(rev)) + (tuple(add) if add is not None else ()),
        aliases={},
        name="ssd_bwd_rev" if rev else "ssd_bwd", sem=("arbitrary",))
    return (*outs, side_outs)


GN_TM = 256
GN_GROUP = DI // NG


def _gn_forward_vals(y0, xs, z, dsk):
    y = y0 + dsk * xs
    sz = _sigmoid(z)
    gate = z * sz
    y2 = y * gate
    parts, rs = [], []
    for g in range(NG):
        seg = y2[:, GN_GROUP * g:GN_GROUP * (g + 1)]
        r = lax.rsqrt(jnp.mean(seg * seg, axis=1, keepdims=True) + NORM_EPS)
        rs.append(r)
        parts.append(seg * r)
    yn = jnp.concatenate(parts, axis=1)
    return y, sz, gate, yn, rs


def _gatenorm_fwd(y_fb, xbc, u, dsk_row, nw_row, w_ps):
    t = y_fb.shape[0]
    tm = GN_TM

    def body(y_ref, xs_ref, z_ref, dsk_ref, nw_ref, w_ref, o_ref, ys_ref):
        _, _, _, yn, _ = _gn_forward_vals(y_ref[...], xs_ref[...], z_ref[...], dsk_ref[...])
        s_out = (yn * nw_ref[...]).astype(BF16)
        o_ref[...] = s_out
        ys_ref[...] = jnp.dot(s_out, w_ref[...], preferred_element_type=F32)

    blk = pl.BlockSpec((tm, DI), lambda i: (i, 0))
    row = pl.BlockSpec((1, DI), lambda i: (0, 0))
    return pl.pallas_call(
        body, out_shape=(jax.ShapeDtypeStruct((t, DI), BF16), jax.ShapeDtypeStruct((t, D), F32)), grid=(t // tm,),
        in_specs=[blk, blk, pl.BlockSpec((tm, DI), lambda i: (i, OZ // DI)), row, row,
                  pl.BlockSpec((DI, D), lambda i: (0, 0))],
        out_specs=(blk, pl.BlockSpec((tm, D), lambda i: (i, 0))), name="gatenorm_fwd",
        compiler_params=_params(("parallel",)))(y_fb, xbc, u, dsk_row, nw_row, w_ps)


def _gatenorm_bwd(dy_ssd, w_ps, y_fb, xbc, u, du, dsk_row, nw_row, side=None):
    t = y_fb.shape[0]
    tm = GN_TM

    def body(dys_ref, w_ref, y_ref, xs_ref, z_ref, dsk_ref, nw_ref, sm_ref, du_in, dy_ref, du_out, dnw_ref, dds_ref):
        del du_in
        i = pl.program_id(0)
        xs = xs_ref[...]
        z = z_ref[...]
        y, sz, gate, yn, rs = _gn_forward_vals(y_ref[...], xs, z, dsk_ref[...])
        ds = _dot_nt(dys_ref[...], w_ref[...])
        gsc = ds * nw_ref[...]
        parts = []
        for g in range(NG):
            sl = slice(GN_GROUP * g, GN_GROUP * (g + 1))
            m = jnp.mean(gsc[:, sl] * yn[:, sl], axis=1, keepdims=True)
            parts.append(rs[g] * (gsc[:, sl] - yn[:, sl] * m))
        dy2 = jnp.concatenate(parts, axis=1)
        dy = dy2 * gate
        dy_ref[...] = dy
        du_out[...] = (dy2 * y * (sz * (1.0 + z * (1.0 - sz)))).astype(du_out.dtype)
        dnw = jnp.broadcast_to(jnp.sum(ds * yn, axis=0, keepdims=True), (8, DI))
        drow = jnp.broadcast_to(jnp.sum(dy * xs, axis=0, keepdims=True), (8, DI))
        dds = _dot01(drow, sm_ref[...])

        @pl.when(i == 0)
        def _():
            dnw_ref[...] = dnw
            dds_ref[...] = dds

        @pl.when(i > 0)
        def _():
            dnw_ref[...] += dnw
            dds_ref[...] += dds

    blk = pl.BlockSpec((tm, DI), lambda i: (i, 0))
    row = pl.BlockSpec((1, DI), lambda i: (0, 0))
    outs, side_outs = _host_call(
        body, side, t // tm,
        out_shape=(jax.ShapeDtypeStruct((t, DI), F32), jax.ShapeDtypeStruct(du.shape, du.dtype),
                   jax.ShapeDtypeStruct((8, DI), F32), jax.ShapeDtypeStruct((8, 128), F32)),
        in_specs=[pl.BlockSpec((tm, D), lambda i: (i, 0)), pl.BlockSpec((DI, D), lambda i: (0, 0)),
                  blk, blk, pl.BlockSpec((tm, DI), lambda i: (i, OZ // DI)), row, row,
                  pl.BlockSpec((DI, 128), lambda i: (0, 0)), pl.BlockSpec(memory_space=pl.ANY)],
        out_specs=(blk, pl.BlockSpec((tm, DI), lambda i: (i, OZ // DI)),
                   pl.BlockSpec((8, DI), lambda i: (0, 0)), pl.BlockSpec((8, 128), lambda i: (0, 0))),
        scratch_shapes=[], args=(dy_ssd, w_ps, y_fb, xbc, u, dsk_row, nw_row, _sum_mat(0), du), aliases={8: 1},
        name="gatenorm_bwd", sem=("arbitrary",))
    return (*outs, side_outs)


AT_B = 128
AT_W = AT_B + 2 * ATT_HALF
AT_L = 2 * AH
SCALE = 1.0 / math.sqrt(AH)


def _slope(g, hh):
    return 2.0 ** (-8.0 * (4 * g + hh + 1) / 12.0)


def _qcol(g):
    return lambda p: OQ // AT_L + 2 * g + p


def _kcol(g):
    return lambda p: OKV // AT_L + 4 * g + 2 * p


def _vcol(g):
    return lambda p: OKV // AT_L + 4 * g + 2 * p + 1


def _pcol(p):
    return p


def _sub(d):
    return 4 if d == 1 else 1


def _win_specs(col, t, d):
    tb, hb = AT_B * d * _sub(d), ATT_HALF * d
    per = tb // hb
    nh = t // hb
    return [
        pl.BlockSpec((hb, AT_L), lambda p, i: (jnp.maximum(per * i - 1, 0), col(p))),
        pl.BlockSpec((tb, AT_L), lambda p, i: (i, col(p))),
        pl.BlockSpec((hb, AT_L), lambda p, i: (jnp.minimum(per * (i + 1), nh - 1), col(p))),
    ]


def _blk_spec(col, d):
    return pl.BlockSpec((AT_B * d * _sub(d), AT_L), lambda p, i: (i, col(p)))


def _rows(ref, r, s, d):
    return ref[pl.ds(r, AT_B, stride=d), :] if d > 1 else ref[AT_B * s:AT_B * (s + 1), :]


def _win(p_ref, c_ref, n_ref, r, s, d):
    if d > 1:
        return jnp.concatenate([p_ref[pl.ds(r, ATT_HALF, stride=d), :], c_ref[pl.ds(r, AT_B, stride=d), :],
                                n_ref[pl.ds(r, ATT_HALF, stride=d), :]], axis=0)
    if s == 0:
        return jnp.concatenate([p_ref[...], c_ref[0:AT_B + ATT_HALF, :]], axis=0)
    if s == _sub(d) - 1:
        return jnp.concatenate([c_ref[AT_B * s - ATT_HALF:AT_B * (s + 1), :], n_ref[...]], axis=0)
    return c_ref[AT_B * s - ATT_HALF:AT_B * (s + 1) + ATT_HALF, :]


def _put_rows(ref, r, s, d, val):
    if d > 1:
        ref[pl.ds(r, AT_B, stride=d), :] = val
    else:
        ref[AT_B * s:AT_B * (s + 1), :] = val


def _for_blocks(d, fn):
    if d == 1:
        for s in range(_sub(d)):
            fn(0, s)
    else:
        def step(r, c):
            fn(r, 0)
            return c
        lax.fori_loop(0, d, step, 0, unroll=4)


def _attn_bias(blk, ln, d, g, p_id):
    a = blk * AT_B + _iota((AT_B, AT_W), 0)
    b = blk * AT_B - ATT_HALF + _iota((AT_B, AT_W), 1)
    rel = jnp.abs(a - b)
    valid = (rel <= ATT_HALF) & (b >= 0) & (b < ln)
    dist = (rel * d).astype(F32)
    out = []
    for hh in range(2):
        slope = jnp.where(p_id == 0, _slope(g, hh), _slope(g, 2 + hh))
        out.append(jnp.where(valid, -slope * dist, NEG))
    return out


def _attn_fwd(u, g):
    t = u.shape[0]
    d = DILATIONS[g]
    ln = t // d

    def body(q_ref, kp, kc, kn, vp, vc, vn, o_ref, l_ref):
        p_id = pl.program_id(0)
        i = pl.program_id(1)
        lane = _iota((AT_B, AT_L), 1)
        biases = [_attn_bias(i * _sub(d) + s, ln, d, g, p_id) for s in range(_sub(d))]

        def one(r, s):
            q = _rows(q_ref, r, s, d) * SCALE
            kw = _win(kp, kc, kn, r, s, d).astype(BF16)
            vw = _win(vp, vc, vn, r, s, d).astype(BF16)
            o = jnp.zeros((AT_B, AT_L), F32)
            lse = jnp.zeros((AT_B, AT_L), F32)
            for hh in range(2):
                hm = (lane // AH) == hh
                qm = jnp.where(hm, q, 0.0).astype(BF16)
                sc = _dot_nt(qm, kw) + biases[s][hh]
                m = jnp.max(sc, axis=1, keepdims=True)
                pr = jnp.exp(sc - m)
                den = jnp.sum(pr, axis=1, keepdims=True)
                oh = jnp.dot(pr.astype(BF16), vw, preferred_element_type=F32)
                o = jnp.where(hm, oh / den, o)
                lse = jnp.where(hm, m + jnp.log(den), lse)
            _put_rows(o_ref, r, s, d, o)
            _put_rows(l_ref, r, s, d, lse)

        _for_blocks(d, one)

    oshape = jax.ShapeDtypeStruct((t, 2 * AT_L), F32)
    ospec = _blk_spec(_pcol, d)
    return pl.pallas_call(
        body, out_shape=(oshape, oshape), grid=(2, t // (AT_B * d * _sub(d))),
        in_specs=[_blk_spec(_qcol(g), d)] + _win_specs(_kcol(g), t, d) + _win_specs(_vcol(g), t, d),
        out_specs=(ospec, ospec), name=f"attn_fwd_{g}", compiler_params=_params(("parallel", "parallel")))(
            u, u, u, u, u, u, u)


def _attn_dq(u, du, do, lse, e, g):
    t = u.shape[0]
    d = DILATIONS[g]
    ln = t // d

    def body(q_ref, kp, kc, kn, vp, vc, vn, do_ref, l_ref, e_ref, du_in, dq_ref, dq_scr):
        del du_in
        p_id = pl.program_id(0)
        i = pl.program_id(1)
        lane = _iota((AT_B, AT_L), 1)
        biases = [_attn_bias(i * _sub(d) + s, ln, d, g, p_id) for s in range(_sub(d))]

        def one(r, s):
            q = _rows(q_ref, r, s, d) * SCALE
            kw = _win(kp, kc, kn, r, s, d).astype(BF16)
            vw = _win(vp, vc, vn, r, s, d).astype(BF16)
            do_ = _rows(do_ref, r, s, d)
            lv = _rows(l_ref, r, s, d)
            ev = _rows(e_ref, r, s, d)
            dq = jnp.zeros((AT_B, AT_L), F32)
            for hh in range(2):
                hm = (lane // AH) == hh
                qm = jnp.where(hm, q, 0.0).astype(BF16)
                sc = _dot_nt(qm, kw) + biases[s][hh]
                lcol = jnp.broadcast_to(lv[:, AH * hh:AH * hh + 1], (AT_B, AT_W))
                ecol = jnp.broadcast_to(ev[:, AH * hh:AH * hh + 1], (AT_B, AT_W))
                pr = jnp.exp(sc - lcol)
                dom = jnp.where(hm, do_, 0.0).astype(BF16)
                ds = pr * (_dot_nt(dom, vw) + ecol)
                dqh = jnp.dot(ds.astype(BF16), kw, preferred_element_type=F32) * SCALE
                dq = jnp.where(hm, dqh, dq)
            _put_rows(dq_scr, r, s, d, dq)

        _for_blocks(d, one)
        dq_ref[...] = dq_scr[...].astype(dq_ref.dtype)

    rspec = _blk_spec(_pcol, d)
    return pl.pallas_call(
        body, out_shape=jax.ShapeDtypeStruct(du.shape, du.dtype), grid=(2, t // (AT_B * d * _sub(d))),
        in_specs=[_blk_spec(_qcol(g), d)] + _win_specs(_kcol(g), t, d) + _win_specs(_vcol(g), t, d)
        + [rspec, rspec, rspec, pl.BlockSpec(memory_space=pl.ANY)],
        out_specs=_blk_spec(_qcol(g), d), input_output_aliases={10: 0},
        scratch_shapes=[pltpu.VMEM((AT_B * d * _sub(d), AT_L), F32)],
        name=f"attn_dq_{g}", compiler_params=_params(("parallel", "parallel")))(
            u, u, u, u, u, u, u, do, lse, e, du)


def _attn_dkv(u, du, do, lse, e, g):
    t = u.shape[0]
    d = DILATIONS[g]
    ln = t // d

    def body(k_ref, v_ref, qp, qc, qn, dp_, dc_, dn_, lp, lc, ln_, ep, ec, en, du_in, dkv_ref, dk_scr, dv_scr):
        del du_in
        p_id = pl.program_id(0)
        jb = pl.program_id(1)
        lane = _iota((AT_B, AT_L), 1)
        biases = [_attn_bias(jb * _sub(d) + s, ln, d, g, p_id) for s in range(_sub(d))]

        def one(r, s):
            k = _rows(k_ref, r, s, d) * SCALE
            v = _rows(v_ref, r, s, d)
            qw = _win(qp, qc, qn, r, s, d).astype(BF16)
            dow = _win(dp_, dc_, dn_, r, s, d).astype(BF16)
            lt = _win(lp, lc, ln_, r, s, d).T
            et = _win(ep, ec, en, r, s, d).T
            dk = jnp.zeros((AT_B, AT_L), F32)
            dv = jnp.zeros((AT_B, AT_L), F32)
            for hh in range(2):
                hm = (lane // AH) == hh
                km = jnp.where(hm, k, 0.0).astype(BF16)
                st = _dot_nt(km, qw) + biases[s][hh]
                pt = jnp.exp(st - lt[AH * hh:AH * hh + 1, :])
                dvh = jnp.dot(pt.astype(BF16), dow, preferred_element_type=F32)
                vm = jnp.where(hm, v, 0.0).astype(BF16)
                dst = pt * (_dot_nt(vm, dow) + et[AH * hh:AH * hh + 1, :])
                dkh = jnp.dot(dst.astype(BF16), qw, preferred_element_type=F32) * SCALE
                dk = jnp.where(hm, dkh, dk)
                dv = jnp.where(hm, dvh, dv)
            _put_rows(dk_scr, r, s, d, dk)
            _put_rows(dv_scr, r, s, d, dv)

        _for_blocks(d, one)
        dkv_ref[:, 0:AT_L] = dk_scr[...].astype(dkv_ref.dtype)
        dkv_ref[:, AT_L:2 * AT_L] = dv_scr[...].astype(dkv_ref.dtype)

    return pl.pallas_call(
        body, out_shape=jax.ShapeDtypeStruct(du.shape, du.dtype), grid=(2, t // (AT_B * d * _sub(d))),
        in_specs=[_blk_spec(_kcol(g), d), _blk_spec(_vcol(g), d)]
        + _win_specs(_qcol(g), t, d) + _win_specs(_pcol, t, d) + _win_specs(_pcol, t, d) + _win_specs(_pcol, t, d)
        + [pl.BlockSpec(memory_space=pl.ANY)],
        out_specs=pl.BlockSpec((AT_B * d * _sub(d), 2 * AT_L), lambda p, i: (i, OKV // (2 * AT_L) + 2 * g + p)),
        input_output_aliases={14: 0},
        scratch_shapes=[pltpu.VMEM((AT_B * d * _sub(d), AT_L), F32), pltpu.VMEM((AT_B * d * _sub(d), AT_L), F32)],
        name=f"attn_dkv_{g}", compiler_params=_params(("parallel", "parallel")))(
            u, u, u, u, u, do, do, do, lse, lse, lse, e, e, e, du)


def _combine_weights(l0, l1, l2):
    m = jnp.maximum(jnp.maximum(l0, l1), l2)
    e0, e1, e2 = jnp.exp(l0 - m), jnp.exp(l1 - m), jnp.exp(l2 - m)
    inv = 1.0 / (e0 + e1 + e2)
    return e0 * inv, e1 * inv, e2 * inv


def _combine_proj(os_, ls_, w_pa):
    t = os_[0].shape[0]
    tm = ROW_TM
    nsh, _, ws = w_pa.shape

    def body(o0, o1, o2, l0, l1, l2, w_ref, a_ref, y_ref):
        w0, w1, w2 = _combine_weights(l0[...], l1[...], l2[...])
        att = w0 * o0[...] + w1 * o1[...] + w2 * o2[...]
        a_ref[...] = att
        ab = att.astype(BF16)
        for sh in range(nsh):
            y_ref[:, ws * sh:ws * (sh + 1)] = jnp.dot(ab, w_ref[sh], preferred_element_type=F32)

    blk = pl.BlockSpec((tm, 2 * AT_L), lambda i: (i, 0))
    return pl.pallas_call(
        body, out_shape=(jax.ShapeDtypeStruct((t, 2 * AT_L), F32), jax.ShapeDtypeStruct((t, nsh * ws), F32)),
        grid=(t // tm,), in_specs=[blk] * 6 + [pl.BlockSpec(w_pa.shape, lambda i: (0, 0, 0))],
        out_specs=(blk, pl.BlockSpec((tm, nsh * ws), lambda i: (i, 0))),
        name="combine_proj", compiler_params=_params(("parallel",)))(*os_, *ls_, w_pa)


def _d_att_combine_bwd(dy_att, w_pa, os_, ls_):
    t = dy_att.shape[0]
    tm = ROW_TM
    nsh, _, ws = w_pa.shape

    def body(dy_ref, w_ref, o0, o1, o2, l0, l1, l2, d0, d1, d2, e0, e1, e2):
        da = jnp.zeros((tm, 2 * AT_L), F32)
        for sh in range(nsh):
            da = da + _dot_nt(dy_ref[:, ws * sh:ws * (sh + 1)], w_ref[sh])
        w = _combine_weights(l0[...], l1[...], l2[...])
        att = w[0] * o0[...] + w[1] * o1[...] + w[2] * o2[...]
        r = _iota((2 * AT_L, 2 * AT_L), 0) // AH
        c = _iota((2 * AT_L, 2 * AT_L), 1) // AH
        hs = _dot01(da * att, (r == c).astype(BF16))
        for wg, dref, eref in zip(w, (d0, d1, d2), (e0, e1, e2)):
            dref[...] = wg * da
            eref[...] = -wg * hs

    blk = pl.BlockSpec((tm, 2 * AT_L), lambda i: (i, 0))
    shp = jax.ShapeDtypeStruct((t, 2 * AT_L), F32)
    outs = pl.pallas_call(
        body, out_shape=(shp,) * 6, grid=(t // tm,),
        in_specs=[pl.BlockSpec((tm, nsh * ws), lambda i: (i, 0)), pl.BlockSpec(w_pa.shape, lambda i: (0, 0, 0))] + [blk] * 6,
        out_specs=(blk,) * 6, name="d_att_combine_bwd", compiler_params=_params(("parallel",)))(dy_att, w_pa, *os_, *ls_)
    return outs[0:3], outs[3:6]


ROW_TM = 512


def _ln(x, g, b):
    mu = jnp.mean(x, axis=1, keepdims=True)
    xc = x - mu
    var = jnp.mean(xc * xc, axis=1, keepdims=True)
    rstd = lax.rsqrt(var + NORM_EPS)
    xhat = xc * rstd
    return xhat * g + b, xhat, rstd


def _ln_back(dh, xhat, rstd, g):
    dxh = dh * g
    m1 = jnp.mean(dxh, axis=1, keepdims=True)
    m2 = jnp.mean(dxh * xhat, axis=1, keepdims=True)
    return rstd * (dxh - m1 - xhat * m2)


def _mlp_up(h1, w_up):
    t = h1.shape[0]
    tm, tn = 2 * ROW_TM, D

    def body(a_ref, b_ref, up_ref, act_ref):
        up = jnp.dot(a_ref[...], b_ref[...], preferred_element_type=F32)
        up_ref[...] = up.astype(BF16)
        r = jnp.maximum(up, 0.0)
        act_ref[...] = (r * r).astype(BF16)

    blk = pl.BlockSpec((tm, tn), lambda j, i: (i, j))
    return pl.pallas_call(
        body, out_shape=(jax.ShapeDtypeStruct((t, DFF), BF16), jax.ShapeDtypeStruct((t, DFF), BF16)),
        grid=(DFF // tn, t // tm),
        in_specs=[pl.BlockSpec((tm, D), lambda j, i: (i, 0)), pl.BlockSpec((None, D, tn), lambda j, i: (j, 0, 0))],
        out_specs=(blk, blk), name="mlp_up", compiler_params=_params(("parallel", "parallel")))(h1, w_up)


def _d_up(dpre2, w_down, up):
    t = up.shape[0]
    tm, tk = 2 * ROW_TM, D

    def body(a_ref, b_ref, u_ref, o_ref):
        dact = _dot_nt(a_ref[...], b_ref[...])
        o_ref[...] = (dact * 2.0 * jnp.maximum(u_ref[...].astype(F32), 0.0)).astype(BF16)

    blk = pl.BlockSpec((tm, tk), lambda j, i: (i, j))
    return pl.pallas_call(
        body, out_shape=jax.ShapeDtypeStruct((t, DFF), BF16), grid=(DFF // tk, t // tm),
        in_specs=[pl.BlockSpec((tm, D), lambda j, i: (i, 0)), pl.BlockSpec((tk, D), lambda j, i: (j, 0)), blk],
        out_specs=blk, name="d_up", compiler_params=_params(("parallel", "parallel")))(dpre2, w_down, up)


def _dt_bwd(du, ddt):
    t = ddt.shape[0]
    tm = 1024

    def body(f_ref, du_in, o_ref):
        del du_in
        o_ref[:, 0:128] = f_ref[...].astype(o_ref.dtype)
        o_ref[:, 128:256] = jnp.zeros((tm, 128), o_ref.dtype)

    blk = pl.BlockSpec((tm, 128), lambda i: (i, 0))
    return pl.pallas_call(
        body, out_shape=jax.ShapeDtypeStruct(du.shape, du.dtype), grid=(t // tm,),
        in_specs=[blk, pl.BlockSpec(memory_space=pl.ANY)],
        out_specs=pl.BlockSpec((tm, 256), lambda i: (i, ODT // 256)), input_output_aliases={1: 0},
        name="dt_bwd", compiler_params=_params(("parallel",)))(ddt, du)


def _mix_out_ln1(y_ssd, y_att, u, bg_row, x, w_out, g_row, b_row):
    t = x.shape[0]
    tm = ROW_TM

    def body(ys_ref, ya_ref, g0_ref, g1_ref, b0_ref, b1_ref, x_ref, w_ref, g_ref, b_ref, mixin_ref, pre_ref, h_ref):
        g0 = _sigmoid(g0_ref[...] + b0_ref[...])
        g1 = _sigmoid(g1_ref[...] + b1_ref[...])
        mixin = (g0 * ys_ref[...] + g1 * ya_ref[...]).astype(BF16)
        mixin_ref[...] = mixin
        pre = ALPHA * x_ref[...] + jnp.dot(mixin, w_ref[...], preferred_element_type=F32)
        pre_ref[...] = pre
        h, _, _ = _ln(pre, g_ref[...], b_ref[...])
        h_ref[...] = h.astype(BF16)

    blk = pl.BlockSpec((tm, D), lambda i: (i, 0))
    row = pl.BlockSpec((1, D), lambda i: (0, 0))
    return pl.pallas_call(
        body,
        out_shape=(jax.ShapeDtypeStruct((t, D), BF16), jax.ShapeDtypeStruct((t, D), F32), jax.ShapeDtypeStruct((t, D), BF16)),
        grid=(t // tm,),
        in_specs=[blk, blk, pl.BlockSpec((tm, D), lambda i: (i, OGATE // D)), pl.BlockSpec((tm, D), lambda i: (i, OGATE // D + 1)),
                  row, pl.BlockSpec((1, D), lambda i: (0, 1)), blk, pl.BlockSpec((D, D), lambda i: (0, 0)), row, row],
        out_specs=(blk, blk, blk), name="mix_out_ln1", compiler_params=_params(("parallel",)))(
            y_ssd, y_att, u, u, bg_row, bg_row, x, w_out, g_row, b_row)


def _mlp_down_ln2_loss(act, w_down, pre1, tgt, g1_row, b1_row, g2_row, b2_row):
    t = pre1.shape[0]
    tm = ROW_TM

    def body(a_ref, w_ref, p1_ref, t_ref, g1_ref, b1_ref, g2_ref, b2_ref, dpre_ref, dpreb_ref, acc_ref):
        i = pl.program_id(0)
        f = jnp.dot(a_ref[...], w_ref[...], preferred_element_type=F32)
        h1, _, _ = _ln(p1_ref[...], g1_ref[...], b1_ref[...])
        pre2 = ALPHA * h1 + f
        h2, xhat, rstd = _ln(pre2, g2_ref[...], b2_ref[...])
        err = h2 - t_ref[...]
        dh = err * (1.0 / D)
        dpre = _ln_back(dh, xhat, rstd, g2_ref[...])
        dpre_ref[...] = dpre
        dpreb_ref[...] = dpre.astype(BF16)
        loss = jnp.sum(jnp.sum(err * err, axis=1, keepdims=True), axis=0, keepdims=True) * (0.5 / D)
        part = jnp.concatenate([jnp.sum(dh * xhat, axis=0, keepdims=True), jnp.sum(dh, axis=0, keepdims=True),
                                jnp.broadcast_to(loss, (1, D)), jnp.zeros((5, D), F32)], axis=0)

        @pl.when(i == 0)
        def _():
            acc_ref[...] = part

        @pl.when(i > 0)
        def _():
            acc_ref[...] += part

    blk = pl.BlockSpec((tm, D), lambda i: (i, 0))
    row = pl.BlockSpec((1, D), lambda i: (0, 0))
    return pl.pallas_call(
        body,
        out_shape=(jax.ShapeDtypeStruct((t, D), F32), jax.ShapeDtypeStruct((t, D), BF16), jax.ShapeDtypeStruct((8, D), F32)),
        grid=(t // tm,),
        in_specs=[pl.BlockSpec((tm, DFF), lambda i: (i, 0)), pl.BlockSpec((DFF, D), lambda i: (0, 0)), blk, blk, row, row, row, row],
        out_specs=(blk, blk, pl.BlockSpec((8, D), lambda i: (0, 0))),
        name="mlp_down_ln2_loss", compiler_params=_params(("arbitrary",)))(act, w_down, pre1, tgt, g1_row, b1_row, g2_row, b2_row)


def _d_h1_ln1_bwd(dup, w_up, dpre2, pre1, g_row, b_row):
    t = dup.shape[0]
    tm = ROW_TM
    nsh = w_up.shape[0]

    def body(a_ref, w_ref, add_ref, pre_ref, g_ref, b_ref, dpre_ref, acc_ref):
        i = pl.program_id(0)
        dh_ = ALPHA * add_ref[...]
        for sh in range(nsh):
            dh_ = dh_ + _dot_nt(a_ref[:, D * sh:D * (sh + 1)], w_ref[sh])
        _, xhat, rstd = _ln(pre_ref[...], g_ref[...], b_ref[...])
        dpre_ref[...] = _ln_back(dh_, xhat, rstd, g_ref[...])
        rows = jnp.concatenate([jnp.sum(dh_ * xhat, axis=0, keepdims=True), jnp.sum(dh_, axis=0, keepdims=True),
                                jnp.zeros((6, D), F32)], axis=0)

        @pl.when(i == 0)
        def _():
            acc_ref[...] = rows

        @pl.when(i > 0)
        def _():
            acc_ref[...] += rows

    blk = pl.BlockSpec((tm, D), lambda i: (i, 0))
    row = pl.BlockSpec((1, D), lambda i: (0, 0))
    return pl.pallas_call(
        body, out_shape=(jax.ShapeDtypeStruct((t, D), F32), jax.ShapeDtypeStruct((8, D), F32)),
        grid=(t // tm,),
        in_specs=[pl.BlockSpec((tm, nsh * D), lambda i: (i, 0)), pl.BlockSpec(w_up.shape, lambda i: (0, 0, 0)),
                  blk, blk, row, row],
        out_specs=(blk, pl.BlockSpec((8, D), lambda i: (0, 0))),
        name="d_h1_ln1_bwd", compiler_params=_params(("arbitrary",)))(dup, w_up, dpre2, pre1, g_row, b_row)


def _d_mixin_mix_bwd(dpre1, w_out, y_ssd, y_att, u, bg_row):
    t = y_ssd.shape[0]
    tm = ROW_TM

    def body(a_ref, w_ref, ys_ref, ya_ref, g0_ref, g1_ref, b0_ref, b1_ref, dys_ref, dya_ref, du_ref, db_ref):
        i = pl.program_id(0)
        dm = _dot_nt(a_ref[...].astype(BF16), w_ref[...])
        g0 = _sigmoid(g0_ref[...] + b0_ref[...])
        g1 = _sigmoid(g1_ref[...] + b1_ref[...])
        dys_ref[...] = (dm * g0).astype(BF16)
        dya_ref[...] = (dm * g1).astype(BF16)
        dl0 = dm * ys_ref[...] * g0 * (1.0 - g0)
        dl1 = dm * ya_ref[...] * g1 * (1.0 - g1)
        du_ref[:, 0:D] = dl0.astype(BF16)
        du_ref[:, D:2 * D] = dl1.astype(BF16)
        part = jnp.concatenate([jnp.broadcast_to(jnp.sum(dl0, axis=0, keepdims=True), (8, D)),
                                jnp.broadcast_to(jnp.sum(dl1, axis=0, keepdims=True), (8, D))], axis=1)

        @pl.when(i == 0)
        def _():
            db_ref[...] = part

        @pl.when(i > 0)
        def _():
            db_ref[...] += part

    blk = pl.BlockSpec((tm, D), lambda i: (i, 0))
    return pl.pallas_call(
        body,
        out_shape=(jax.ShapeDtypeStruct((t, D), BF16), jax.ShapeDtypeStruct((t, D), BF16),
                   jax.ShapeDtypeStruct((t, UW), BF16), jax.ShapeDtypeStruct((8, 2 * D), F32)),
        grid=(t // tm,),
        in_specs=[blk, pl.BlockSpec((D, D), lambda i: (0, 0)), blk, blk,
                  pl.BlockSpec((tm, D), lambda i: (i, OGATE // D)), pl.BlockSpec((tm, D), lambda i: (i, OGATE // D + 1)),
                  pl.BlockSpec((1, D), lambda i: (0, 0)), pl.BlockSpec((1, D), lambda i: (0, 1))],
        out_specs=(blk, blk, pl.BlockSpec((tm, 2 * D), lambda i: (i, OGATE // (2 * D))),
                   pl.BlockSpec((8, 2 * D), lambda i: (0, 0))),
        name="d_mixin_mix_bwd", compiler_params=_params(("arbitrary",)))(dpre1, w_out, y_ssd, y_att, u, u, bg_row, bg_row)


def _adamw(w, g, m, v, name):
    r, c = w.shape
    tr, tc = r, c
    for cand in (256, 128, 64, 32, 16, 8):
        if r % cand == 0 and cand * c * 4 <= 2 ** 21:
            tr = cand
            break
    if tr < 64 and c % 256 == 0:
        tr, tc = r, 256
    bc1 = 1.0 / (1.0 - ADAM_B1 ** ADAM_STEP)
    bc2 = 1.0 / (1.0 - ADAM_B2 ** ADAM_STEP)

    def body(w_ref, g_ref, m_ref, v_ref, d_ref, nm_ref, nv_ref):
        gg = g_ref[...]
        nm = ADAM_B1 * m_ref[...] + (1.0 - ADAM_B1) * gg
        nv = ADAM_B2 * v_ref[...] + (1.0 - ADAM_B2) * (gg * gg)
        nm_ref[...] = nm
        nv_ref[...] = nv
        d_ref[...] = -ADAM_LR * ((nm * bc1) / (jnp.sqrt(nv * bc2) + ADAM_EPS) + ADAM_WD * w_ref[...])

    blk = pl.BlockSpec((tr, tc), lambda i, j: (i, j))
    shp = jax.ShapeDtypeStruct((r, c), F32)
    return pl.pallas_call(body, out_shape=(shp, shp, shp), grid=(r // tr, c // tc), in_specs=[blk] * 4,
                          out_specs=(blk,) * 3, name=name, compiler_params=_params(("parallel", "parallel")))(w, g, m, v)


def _segments():
    segs = [(0, 2048), (7488, 9536), (2048, 5120)]
    for g in range(3):
        for p in range(2):
            lo = 256 * g + 128 * p
            segs += [(5952 + lo, 5952 + lo + 128), (6720 + lo, 6720 + lo + 128)]
    segs += [(5184, 5952), (5120, 5184)]
    out, pos = [], 0
    for a, b in segs:
        out.append((a, b, pos))
        pos += b - a
    return out


SHARD_COLS = IN_COLS // 4


def _perm_from_shards(w_shards):
    pieces = []
    for a, b, _ in _segments():
        while a < b:
            s = a // SHARD_COLS
            e = min(b, (s + 1) * SHARD_COLS)
            pieces.append(w_shards[s][:, a - s * SHARD_COLS:e - s * SHARD_COLS])
            a = e
    pieces.append(jnp.zeros((w_shards.shape[1], UW - IN_COLS), w_shards.dtype))
    return jnp.concatenate(pieces, axis=1)


def _shards_from_perm(wp):
    segs = sorted(_segments())
    shards = []
    for s in range(4):
        lo, hi = s * SHARD_COLS, (s + 1) * SHARD_COLS
        pieces = []
        for a, b, pos in segs:
            x, y = max(a, lo), min(b, hi)
            if x < y:
                pieces.append(wp[:, pos + x - a:pos + y - a])
        shards.append(jnp.concatenate(pieces, axis=1))
    return jnp.stack(shards)


def _lanes128(*vecs):
    v = jnp.concatenate([a.reshape(-1) for a in vecs])
    return jnp.pad(v, (0, 128 - v.shape[0])).reshape(1, 128)


EARLY = ("w_proj_ssd", "w_proj_attn", "w_out", "w_up", "w_down")


def _weights_of(gathered):
    g_ps, g_pa, g_o, g_up, g_dn = gathered
    return {"w_proj_ssd": g_ps.reshape(DI, D), "w_proj_attn": g_pa, "w_out": g_o.reshape(D, D), "w_up": g_up,
            "w_down": g_dn.reshape(DFF, D)}


def _local_grads(x, tgt, wts, sm, rs_idx=None):
    row = lambda a: a.reshape(1, -1)
    bg_row, cb_row = row(sm["b_gate"]), row(sm["conv_b"])
    par = jnp.concatenate([_lanes128(sm["dt_bias_f"], sm["dt_bias_b"]), _lanes128(sm["a_log_f"], sm["a_log_b"]),
                           jnp.zeros((6, 128), F32)], axis=0)
    dsk_row = row(jnp.repeat(sm["d_skip"], HP))
    nw_row = row(sm["ssd_norm_w"])
    g1, b1, g2, b2 = row(sm["ln1_g"]), row(sm["ln1_b"]), row(sm["ln2_g"]), row(sm["ln2_b"])

    xb = x.astype(BF16)
    u, gathered = _in_proj(xb, wts["w_in_p"], side=_gather_side(wts["pending"]) if "pending" in wts else None)
    if gathered:
        wts = {**wts, **_weights_of(gathered)}
    xbc = _conv_fwd(u, sm["conv_w"], cb_row)
    dtv, csv = _ssd_prep(u, par)
    y_f, st_f = _ssd_fwd(xbc, u, par, dtv, csv, rev=False)
    y_fb, st_b = _ssd_fwd(xbc, u, par, dtv, csv, y_f, rev=True)
    s_out, y_ssd = _gatenorm_fwd(y_fb, xbc, u, dsk_row, nw_row, wts["w_proj_ssd"])
    att_o, att_l = [], []
    for g in range(3):
        o, l = _attn_fwd(u, g)
        att_o.append(o)
        att_l.append(l)
    att, y_att = _combine_proj(att_o, att_l, wts["w_proj_attn"])
    mixin, pre1, h1 = _mix_out_ln1(y_ssd, y_att, u, bg_row, x, wts["w_out"], g1, b1)
    up, act = _mlp_up(h1, wts["w_up"])
    dpre2, dpre2_b, acc2 = _mlp_down_ln2_loss(act, wts["w_down"], pre1, tgt, g1, b1, g2, b2)

    dw_down = _mm_tn(act, dpre2_b, tka=1024, tn=1024, tt=1024, name="dw_down")
    dup = _d_up(dpre2_b, wts["w_down"], up)
    dw_up = _mm_tn(h1, dup, tka=1024, tn=1024, tt=1024, name="dw_up", out_shards=4)
    dpre1, acc1 = _d_h1_ln1_bwd(dup, wts["w_up"], dpre2, pre1, g1, b1)
    dw_out = _mm_tn(mixin, dpre1, tka=1024, tn=1024, tt=1024, name="dw_out")
    dy_ssd, dy_att, du, dbg = _d_mixin_mix_bwd(dpre1, wts["w_out"], y_ssd, y_att, u, bg_row)
    dw_proj_ssd = _mm_tn(s_out, dy_ssd, tka=1024, tn=1024, tt=1024, name="dw_proj_ssd")
    dw_proj_attn = _mm_tn(att, dy_att, tka=256, tn=256, tt=1024, name="dw_proj_attn", out_shards=4)
    do_g, e_g = _d_att_combine_bwd(dy_att, wts["w_proj_attn"], att_o, att_l)
    for g in range(3):
        du = _attn_dq(u, du, do_g[g], att_l[g], e_g[g], g)
        du = _attn_dkv(u, du, do_g[g], att_l[g], e_g[g], g)
    big = {
        "w_proj_ssd": dw_proj_ssd.reshape(4, DI // 4, D),
        "w_proj_attn": dw_proj_attn,
        "w_out": dw_out.reshape(4, D // 4, D),
        "w_up": dw_up,
        "w_down": dw_down.reshape(4, DFF // 4, D),
    }
    early = [big[n] for n in EARLY]
    dy, du, dnw, dds, recv = _gatenorm_bwd(dy_ssd, wts["w_proj_ssd"], y_fb, xbc, u, du, dsk_row, nw_row,
                                           side=_swap_side(early) if rs_idx else None)
    if rs_idx:
        halves = [_add_half(g, r, rs_idx[0], f"rs_add_half_{n}") for g, r, n in zip(early, recv, EARLY)]
    dxs_f, dbc_f, ddt_f, sacc_f, recv = _ssd_bwd(xbc, u, par, dtv, csv, dy, st_f, rev=False,
                                                 side=_step1_side([h[1] for h in halves]) if rs_idx else None)
    if rs_idx:
        k = len(EARLY)
        sums1 = [_rs_add1(h[0], ra, rb, rs_idx[1], f"rs_add1_{n}")
                 for h, ra, rb, n in zip(halves, recv[:k], recv[k:], EARLY)]
    dxs, dbc, ddt, sacc_b, recv = _ssd_bwd(
        xbc, u, par, dtv, csv, dy, st_b, rev=True, add=(dxs_f, dbc_f, ddt_f),
        side=_step2_side([s1[2] for s1 in sums1], [s1[3] for s1 in sums1]) if rs_idx else None)
    pieces = None
    if rs_idx:
        pieces = {n: _rs_add2(s1[0], s1[1], ra, rb, rs_idx[1], f"rs_add2_{n}")
                  for s1, ra, rb, n in zip(sums1, recv[:k], recv[k:], EARLY)}
    dpre_c, dcw, dcb = _conv_dpre(u, dxs, dy, dbc, dsk_row, sm["conv_w"], cb_row)
    du = _conv_dx(du, dpre_c, sm["conv_w"])
    du = _dt_bwd(du, ddt)
    dw_in_p = _mm_tn(xb, du, tka=1024, tn=2432, tt=1024, name="dw_in")
    big["w_in"] = _shards_from_perm(dw_in_p)
    side = None
    if rs_idx:
        g = big["w_in"]
        half = _add_half(g, _run_side(_swap_side([g]), "rs_swap_halves")[0], rs_idx[0], "rs_add_half_w_in")
        side = _step1_side([half[1]])
    dx, recv = _d_x(du, wts["w_in_p"], dpre1, side)
    if rs_idx:
        s1 = _rs_add1(half[0], recv[0], recv[1], rs_idx[1], "rs_add1_w_in")
        ra2, rb2 = _run_side(_step2_side([s1[2]], [s1[3]]), "rs_step2")
        pieces["w_in"] = _rs_add2(s1[0], s1[1], ra2, rb2, rs_idx[1], "rs_add2_w_in")

    sacc = sacc_f + sacc_b
    small = {
        "b_gate": dbg[0], "conv_w": dcw[0:KCONV], "conv_b": dcb[0],
        "dt_bias_f": sacc[0, 0:32], "dt_bias_b": sacc[0, 32:64], "a_log_f": sacc[1, 0:32], "a_log_b": sacc[1, 32:64],
        "d_skip": dds[0, 0:32], "ssd_norm_w": dnw[0],
        "ln1_g": acc1[0], "ln1_b": acc1[1], "ln2_g": acc2[0], "ln2_b": acc2[1], "loss": acc2[2, 0:1],
    }
    return dx, big, small, pieces


HBM_SPEC = pl.BlockSpec(memory_space=pl.ANY)


def _place():
    x, y, c = lax.axis_index("x"), lax.axis_index("y"), lax.axis_index("c")
    chips = [(1 - x, y), (x, 1 - y), (1 - x, 1 - y)]
    return x, y, c, chips


def _gather_phases(n):
    def tools(ins, outs, send_sems, recv_sems):
        x, y, c, _ = _place()
        slots = (2 * x + y, 2 * (1 - x) + y, 2 * x + 1 - y, 2 * (1 - x) + 1 - y)
        peers = ((1 - x, y, c), (x, 1 - y, c), (x, y, 1 - c))

        def copy(w, k, src, dst, to):
            return pltpu.make_async_remote_copy(src_ref=src, dst_ref=dst, send_sem=send_sems.at[w, k],
                                                recv_sem=recv_sems.at[w, k], device_id=to, device_id_type=MESH)

        def rows(w, core, part):
            rh = ins[w].shape[0] // 2
            if part is None:
                return pl.ds(core * rh, rh)
            return pl.ds(core * rh + part * (rh // 2), rh // 2)

        def same(w, k, slot, core, part, to):
            blk = outs[w].at[slot, rows(w, core, part), :]
            return copy(w, k, blk, blk, to)

        def sends(w):
            q, q_x, q_y, q_d = slots
            x_nbr, y_nbr, sibling = peers
            mine = rows(w, c, None)
            mk = functools.partial
            return [mk(copy, w, 0, ins[w].at[mine, :], outs[w].at[q, mine, :], x_nbr),
                    mk(copy, w, 1, ins[w].at[mine, :], outs[w].at[q, mine, :], y_nbr),
                    mk(same, w, 2, q_x, c, 0, y_nbr), mk(same, w, 3, q_y, c, 1, x_nbr),
                    mk(same, w, 4, q_x, c, None, sibling), mk(same, w, 5, q_y, c, None, sibling),
                    mk(same, w, 6, q_d, c, 0, sibling), mk(same, w, 7, q_d, c, 1, sibling),
                    mk(copy, w, 8, ins[w], outs[w].at[q], sibling)]

        return c, slots, peers, same, sends

    def first(*refs):
        _, _, _, _, sends = tools(*refs)
        for w in range(n):
            cps = sends(w)
            for k in (8, 0, 1):
                cps[k]().start()

    def second(*refs):
        c, (_, q_x, q_y, _), (x_nbr, y_nbr, _), same, sends = tools(*refs)
        for w in range(n):
            cps = sends(w)
            same(w, 0, q_x, c, None, x_nbr).wait_recv()
            cps[2]().start()
            cps[4]().start()
            same(w, 1, q_y, c, None, y_nbr).wait_recv()
            cps[3]().start()
            cps[5]().start()

    def third(*refs):
        c, (_, _, _, q_d), (x_nbr, y_nbr, _), same, sends = tools(*refs)
        for w in range(n):
            cps = sends(w)
            same(w, 2, q_d, c, 0, y_nbr).wait_recv()
            cps[6]().start()
            same(w, 3, q_d, c, 1, x_nbr).wait_recv()
            cps[7]().start()

    def last(*refs):
        c, (_, q_x, q_y, q_d), (_, _, sibling), same, sends = tools(*refs)
        for w in range(n):
            same(w, 4, q_x, 1 - c, None, sibling).wait_recv()
            same(w, 5, q_y, 1 - c, None, sibling).wait_recv()
            same(w, 6, q_d, 1 - c, 0, sibling).wait_recv()
            same(w, 7, q_d, 1 - c, 1, sibling).wait_recv()
            sends(w)[8]().wait_recv()
        for w in range(n):
            for mk_cp in sends(w):
                mk_cp().wait_send()

    return first, second, third, last


def _gather_side(shards):
    first, second, third, last = _gather_phases(len(shards))
    shapes = tuple(jax.ShapeDtypeStruct((4,) + s.shape, s.dtype) for s in shards)
    return _Side(tuple(shards), shapes, (len(shards), 9), None, ((0.0, first), (0.36, second), (0.58, third), (1.0, last)))


class _Side(NamedTuple):
    ins: tuple
    out_shapes: tuple
    nsem: tuple
    make: Callable
    phases: tuple = ()


def _swap_copies(ins, outs, send_sems, recv_sems):
    x, y, c, _ = _place()
    copies = []
    for w in range(len(ins)):
        rh = ins[w].shape[1] // 2
        for p in range(4):
            copies.append(pltpu.make_async_remote_copy(
                src_ref=ins[w].at[p, pl.ds((1 - c) * rh, rh), :], dst_ref=outs[w].at[p],
                send_sem=send_sems.at[w, p], recv_sem=recv_sems.at[w, p],
                device_id=(x, y, 1 - c), device_id_type=MESH))
    return copies


def _swap_side(grads):
    shapes = tuple(jax.ShapeDtypeStruct((4, g.shape[1] // 2, g.shape[2]), F32) for g in grads)
    return _Side(tuple(grads), shapes, (len(grads), 4), _swap_copies)


def _step1_copies(ins, outs, send_sems, recv_sems):
    n = len(ins)
    out_a, out_b = outs[:n], outs[n:]
    x, y, c, _ = _place()
    copies = []
    for w in range(n):
        rq = ins[w].shape[1] // 2
        for i in range(2):
            copies.append(pltpu.make_async_remote_copy(
                src_ref=ins[w].at[2 * (1 - x) + i, pl.ds(0, rq), :], dst_ref=out_a[w].at[i],
                send_sem=send_sems.at[w, i], recv_sem=recv_sems.at[w, i],
                device_id=(1 - x, y, c), device_id_type=MESH))
            copies.append(pltpu.make_async_remote_copy(
                src_ref=ins[w].at[2 * i + 1 - y, pl.ds(rq, rq), :], dst_ref=out_b[w].at[i],
                send_sem=send_sems.at[w, 2 + i], recv_sem=recv_sems.at[w, 2 + i],
                device_id=(x, 1 - y, c), device_id_type=MESH))
    return copies


def _step1_side(parts):
    quarter = tuple(jax.ShapeDtypeStruct((2, p.shape[1] // 2, p.shape[2]), p.dtype) for p in parts)
    return _Side(tuple(parts), quarter + quarter, (len(parts), 4), _step1_copies)


def _step2_copies(ins, outs, send_sems, recv_sems):
    n = len(ins) // 2
    in_a, in_b, out_a, out_b = ins[:n], ins[n:], outs[:n], outs[n:]
    x, y, c, _ = _place()
    copies = []
    for w in range(n):
        copies.append(pltpu.make_async_remote_copy(
            src_ref=in_a[w].at[1 - y], dst_ref=out_a[w], send_sem=send_sems.at[w, 0], recv_sem=recv_sems.at[w, 0],
            device_id=(x, 1 - y, c), device_id_type=MESH))
        copies.append(pltpu.make_async_remote_copy(
            src_ref=in_b[w].at[1 - x], dst_ref=out_b[w], send_sem=send_sems.at[w, 1], recv_sem=recv_sems.at[w, 1],
            device_id=(1 - x, y, c), device_id_type=MESH))
    return copies


def _step2_side(tas, tbs):
    one = tuple(jax.ShapeDtypeStruct(p.shape[1:], p.dtype) for p in tuple(tas) + tuple(tbs))
    return _Side(tuple(tas) + tuple(tbs), one, (len(tas), 2), _step2_copies)


def _phases_of(side, n_steps):
    if side.phases:
        return [(min(int(f * n_steps), n_steps - 1), fn) for f, fn in side.phases]

    def start(*refs):
        for cp in side.make(*refs):
            cp.start()

    def wait(*refs):
        for cp in side.make(*refs):
            cp.wait()

    return [(0, start), (n_steps - 1, wait)]


def _run_side(side, name):
    n_in, n_out = len(side.ins), len(side.out_shapes)

    def body(*refs):
        for _, fn in _phases_of(side, 1):
            fn(refs[:n_in], refs[n_in:n_in + n_out], *refs[n_in + n_out:])

    return pl.pallas_call(
        body, out_shape=list(side.out_shapes), in_specs=[HBM_SPEC] * n_in, out_specs=[HBM_SPEC] * n_out,
        scratch_shapes=[pltpu.SemaphoreType.DMA(side.nsem), pltpu.SemaphoreType.DMA(side.nsem)], name=name)(*side.ins)


def _host_call(body, side, n_steps, *, out_shape, in_specs, out_specs, scratch_shapes, args, aliases, name, sem):
    n_in, n_out, n_scr = len(in_specs), len(out_shape), len(scratch_shapes)
    if side is None:
        outs = pl.pallas_call(body, out_shape=tuple(out_shape), grid=(n_steps,), in_specs=list(in_specs),
                              out_specs=tuple(out_specs), scratch_shapes=list(scratch_shapes),
                              input_output_aliases=aliases, name=name, compiler_params=_params(sem))(*args)
        return tuple(outs), ()
    ns_in, ns_out = len(side.ins), len(side.out_shapes)

    def wrapped(*refs):
        h_in, s_in = refs[:n_in], refs[n_in:n_in + ns_in]
        o0 = n_in + ns_in
        h_out, s_out = refs[o0:o0 + n_out], refs[o0 + n_out:o0 + n_out + ns_out]
        c0 = o0 + n_out + ns_out
        h_scr, sems = refs[c0:c0 + n_scr], refs[c0 + n_scr:]
        step = pl.program_id(0)
        phases = _phases_of(side, n_steps)
        for at, fn in phases[:-1]:
            pl.when(step == at)(functools.partial(fn, s_in, s_out, *sems))
        body(*h_in, *h_out, *h_scr)
        pl.when(step == phases[-1][0])(functools.partial(phases[-1][1], s_in, s_out, *sems))

    outs = pl.pallas_call(
        wrapped, out_shape=tuple(out_shape) + tuple(side.out_shapes), grid=(n_steps,),
        in_specs=list(in_specs) + [HBM_SPEC] * ns_in, out_specs=tuple(out_specs) + (HBM_SPEC,) * ns_out,
        scratch_shapes=list(scratch_shapes) + [pltpu.SemaphoreType.DMA(side.nsem), pltpu.SemaphoreType.DMA(side.nsem)],
        input_output_aliases=aliases, name=name, compiler_params=_params(sem))(*args, *side.ins)
    return tuple(outs[:n_out]), tuple(outs[n_out:])


def _join_halves(pieces):
    n = len(pieces)

    def body(*refs):
        outs = refs[n:2 * n]
        send_sems, recv_sems = refs[2 * n:]
        x, y, c, _ = _place()

        def copy(w, slot):
            return pltpu.make_async_remote_copy(
                src_ref=outs[w].at[slot], dst_ref=outs[w].at[slot], send_sem=send_sems.at[w], recv_sem=recv_sems.at[w],
                device_id=(x, y, 1 - c), device_id_type=MESH)

        for w in range(n):
            copy(w, c).start()
        for w in range(n):
            copy(w, 1 - c).wait_recv()
            copy(w, c).wait_send()

    return pl.pallas_call(
        body, out_shape=[jax.ShapeDtypeStruct(p.shape, F32) for p in pieces],
        in_specs=[HBM_SPEC] * n, out_specs=[HBM_SPEC] * n, input_output_aliases={w: w for w in range(n)},
        scratch_shapes=[pltpu.SemaphoreType.DMA((n,)), pltpu.SemaphoreType.DMA((n,))],
        name="rs_join_halves")(*pieces)


def _add_tile_rows(rh, c):
    for cand in (512, 256, 128, 64, 32, 16, 8):
        if rh % cand == 0 and cand * c * 4 <= 2 ** 21:
            return cand
    return rh


def _add_half(grad, recv, c_idx, name):
    _, r, cc = grad.shape
    rh = r // 2
    tr = _add_tile_rows(rh, cc)
    nb = rh // tr

    def body(c_ref, g_ref, r_ref, o_ref, ob_ref):
        del c_ref
        s = g_ref[...] + r_ref[...]
        o_ref[...] = s
        ob_ref[...] = s.astype(BF16)

    blk = pl.BlockSpec((None, tr, cc), lambda p, i, c_ref: (p, i, 0))
    grid_spec = pltpu.PrefetchScalarGridSpec(
        num_scalar_prefetch=1, grid=(4, nb),
        in_specs=[pl.BlockSpec((None, tr, cc), lambda p, i, c_ref: (p, c_ref[0] * nb + i, 0)), blk],
        out_specs=(blk, blk))
    return pl.pallas_call(
        body, out_shape=(jax.ShapeDtypeStruct((4, rh, cc), F32), jax.ShapeDtypeStruct((4, rh, cc), BF16)),
        grid_spec=grid_spec, name=name, compiler_params=_params(("parallel", "parallel")))(c_idx, grad, recv)


def _rs_add1(part, recv_a, recv_b, xy_idx, name):
    _, rh, cc = part.shape
    rq = rh // 2
    tr = _add_tile_rows(rq, cc)
    nb = rq // tr

    def body(xy_ref, pa_ref, pb_ref, ra_ref, rb_ref, ta_ref, tb_ref, tab_ref, tbb_ref):
        del xy_ref
        ta = pa_ref[...] + ra_ref[...].astype(F32)
        tb = pb_ref[...] + rb_ref[...].astype(F32)
        ta_ref[...] = ta
        tb_ref[...] = tb
        tab_ref[...] = ta.astype(BF16)
        tbb_ref[...] = tb.astype(BF16)

    blk = pl.BlockSpec((None, tr, cc), lambda i, j, xy: (i, j, 0))
    grid_spec = pltpu.PrefetchScalarGridSpec(
        num_scalar_prefetch=1, grid=(2, nb),
        in_specs=[pl.BlockSpec((None, tr, cc), lambda i, j, xy: (2 * xy[0] + i, j, 0)),
                  pl.BlockSpec((None, tr, cc), lambda i, j, xy: (2 * i + xy[1], nb + j, 0)), blk, blk],
        out_specs=(blk, blk, blk, blk))
    f32s, b16s = jax.ShapeDtypeStruct((2, rq, cc), F32), jax.ShapeDtypeStruct((2, rq, cc), BF16)
    return pl.pallas_call(body, out_shape=(f32s, f32s, b16s, b16s), grid_spec=grid_spec, name=name,
                          compiler_params=_params(("parallel", "parallel")))(xy_idx, part, part, recv_a, recv_b)


def _rs_add2(ta, tb, recv_a, recv_b, xy_idx, name):
    _, rq, cc = ta.shape
    tr = _add_tile_rows(rq, cc)
    nb = rq // tr

    def body(xy_ref, ta_ref, tb_ref, ra_ref, rb_ref, o_ref):
        del xy_ref
        s = pl.program_id(0)
        fa = ta_ref[...] + ra_ref[...].astype(F32)
        fb = tb_ref[...] + rb_ref[...].astype(F32)
        o_ref[...] = jnp.where(s == 0, fa, fb)

    rblk = pl.BlockSpec((tr, cc), lambda s, j, xy: (j, 0))
    grid_spec = pltpu.PrefetchScalarGridSpec(
        num_scalar_prefetch=1, grid=(2, nb),
        in_specs=[pl.BlockSpec((None, tr, cc), lambda s, j, xy: (xy[1], j, 0)),
                  pl.BlockSpec((None, tr, cc), lambda s, j, xy: (xy[0], j, 0)), rblk, rblk],
        out_specs=pl.BlockSpec((None, tr, cc), lambda s, j, xy: (xy[2], s * nb + j, 0)))
    return pl.pallas_call(body, out_shape=jax.ShapeDtypeStruct((2, 2 * rq, cc), F32), grid_spec=grid_spec, name=name,
                          compiler_params=_params(("parallel", "parallel")))(xy_idx, ta, tb, recv_a, recv_b)


def _allreduce_small(slab):
    r = slab.shape[0]

    def body(x_ref, o_ref, buf, send_sems, recv_sems):
        x, y, c, _ = _place()
        me = 4 * x + 2 * y + c
        buf[me] = x_ref[...]
        peers = []
        for k in range(1, 8):
            kx, ky, kc = (k >> 2) & 1, (k >> 1) & 1, k & 1
            peers.append((x + kx - 2 * x * kx, y + ky - 2 * y * ky, c + kc - 2 * c * kc))

        def copy(k, slot):
            return pltpu.make_async_remote_copy(src_ref=x_ref, dst_ref=buf.at[slot], send_sem=send_sems.at[k],
                                                recv_sem=recv_sems.at[k], device_id=peers[k], device_id_type=MESH)

        for k in range(7):
            copy(k, me).start()
        for k, (px, py, pc) in enumerate(peers):
            copy(k, 4 * px + 2 * py + pc).wait_recv()
        for k in range(7):
            copy(k, me).wait_send()
        acc = buf[0]
        for j in range(1, 8):
            acc = acc + buf[j]
        o_ref[...] = acc

    vm = pl.BlockSpec(memory_space=pltpu.VMEM)
    return pl.pallas_call(
        body, out_shape=jax.ShapeDtypeStruct((r, 128), F32), in_specs=[vm], out_specs=vm,
        scratch_shapes=[pltpu.VMEM((8, r, 128), F32), pltpu.SemaphoreType.DMA((7,)), pltpu.SemaphoreType.DMA((7,))],
        name="allreduce_small")(slab)


def _pack(arrs):
    rows = []
    for a in arrs:
        v = a.reshape(-1)
        v = jnp.pad(v, (0, (-v.shape[0]) % 128))
        rows.append(v.reshape(-1, 128))
    slab = jnp.concatenate(rows, axis=0)
    return jnp.pad(slab, ((0, (-slab.shape[0]) % 8), (0, 0)))


def _unpack(slab, shapes):
    out, r0 = [], 0
    for shp in shapes:
        size = math.prod(shp)
        nr = -(-size // 128)
        out.append(slab[r0:r0 + nr].reshape(-1)[:size].reshape(shp))
        r0 += nr
    return out


BIG = ("w_in", "w_proj_ssd", "w_proj_attn", "w_out", "w_up", "w_down")
SMALL = ("b_gate", "conv_w", "conv_b", "dt_bias_f", "dt_bias_b", "a_log_f", "a_log_b", "d_skip", "ssd_norm_w",
         "ln1_g", "ln1_b", "ln2_g", "ln2_b")
ORDER = ("w_in", "b_gate", "conv_w", "conv_b", "dt_bias_f", "dt_bias_b", "a_log_f", "a_log_b", "d_skip", "ssd_norm_w",
         "w_proj_ssd", "w_proj_attn", "w_out", "ln1_g", "ln1_b", "w_up", "w_down", "ln2_g", "ln2_b")


def kernel(x, w_in, b_gate, conv_w, conv_b, dt_bias_f, dt_bias_b, a_log_f, a_log_b, d_skip, ssd_norm_w, w_proj_ssd, w_proj_attn, w_out, ln1_g, ln1_b, w_up, w_down, ln2_g, ln2_b, loss_target, m_w_in, m_b_gate, m_conv_w, m_conv_b, m_dt_bias_f, m_dt_bias_b, m_a_log_f, m_a_log_b, m_d_skip, m_ssd_norm_w, m_w_proj_ssd, m_w_proj_attn, m_w_out, m_ln1_g, m_ln1_b, m_w_up, m_w_down, m_ln2_g, m_ln2_b, v_w_in, v_b_gate, v_conv_w, v_conv_b, v_dt_bias_f, v_dt_bias_b, v_a_log_f, v_a_log_b, v_d_skip, v_ssd_norm_w, v_w_proj_ssd, v_w_proj_attn, v_w_out, v_ln1_g, v_ln1_b, v_w_up, v_w_down, v_ln2_g, v_ln2_b):
    w = dict(w_in=w_in, b_gate=b_gate, conv_w=conv_w, conv_b=conv_b, dt_bias_f=dt_bias_f, dt_bias_b=dt_bias_b,
             a_log_f=a_log_f, a_log_b=a_log_b, d_skip=d_skip, ssd_norm_w=ssd_norm_w, w_proj_ssd=w_proj_ssd,
             w_proj_attn=w_proj_attn, w_out=w_out, ln1_g=ln1_g, ln1_b=ln1_b, w_up=w_up, w_down=w_down, ln2_g=ln2_g, ln2_b=ln2_b)
    m = dict(w_in=m_w_in, b_gate=m_b_gate, conv_w=m_conv_w, conv_b=m_conv_b, dt_bias_f=m_dt_bias_f, dt_bias_b=m_dt_bias_b,
             a_log_f=m_a_log_f, a_log_b=m_a_log_b, d_skip=m_d_skip, ssd_norm_w=m_ssd_norm_w, w_proj_ssd=m_w_proj_ssd,
             w_proj_attn=m_w_proj_attn, w_out=m_w_out, ln1_g=m_ln1_g, ln1_b=m_ln1_b, w_up=m_w_up, w_down=m_w_down,
             ln2_g=m_ln2_g, ln2_b=m_ln2_b)
    v = dict(w_in=v_w_in, b_gate=v_b_gate, conv_w=v_conv_w, conv_b=v_conv_b, dt_bias_f=v_dt_bias_f, dt_bias_b=v_dt_bias_b,
             a_log_f=v_a_log_f, a_log_b=v_a_log_b, d_skip=v_d_skip, ssd_norm_w=v_ssd_norm_w, w_proj_ssd=v_w_proj_ssd,
             w_proj_attn=v_w_proj_attn, w_out=v_w_out, ln1_g=v_ln1_g, ln1_b=v_ln1_b, w_up=v_w_up, w_down=v_w_down,
             ln2_g=v_ln2_g, ln2_b=v_ln2_b)
    xi, yi, ci = lax.axis_index("x"), lax.axis_index("y"), lax.axis_index("c")
    shard = 2 * xi + yi

    (g_in,) = _run_side(_gather_side([w["w_in"].astype(BF16)]), "allgather_w_in")
    wts = {"w_in_p": _perm_from_shards(g_in), "pending": [w[n].astype(BF16) for n in EARLY]}

    cw_slab = jnp.zeros((KCONV, 4, CONVD // 4), F32)
    cw_slab = lax.dynamic_update_slice(cw_slab, conv_w[:, None, :] * 0.5, (0, shard, 0))
    conv_w_all = _unpack(_allreduce_small(_pack([cw_slab])), [(KCONV, CONVD)])[0]

    sm = {n: w[n] for n in SMALL}
    sm["conv_w"] = conv_w_all
    c_idx = jnp.reshape(ci, (1,)).astype(jnp.int32)
    xy_idx = jnp.stack([xi, yi, ci]).astype(jnp.int32)
    dx, big, small, pieces = _local_grads(x[0], loss_target[0], wts, sm, rs_idx=(c_idx, xy_idx))

    names = list(SMALL) + ["loss"]
    shapes = [small[n].shape for n in names]
    red = dict(zip(names, _unpack(_allreduce_small(_pack([small[n] for n in names])), shapes)))
    loss = red["loss"].reshape(())
    gsm = {n: red[n] for n in SMALL}
    conv_w_grad_shard = lax.dynamic_slice_in_dim(gsm["conv_w"].reshape(KCONV, 4, CONVD // 4), shard, 1, axis=1)
    gsm["conv_w"] = conv_w_grad_shard.reshape(KCONV, CONVD // 4)

    joined = _join_halves([pieces[n] for n in BIG])
    gbig = {n: j.reshape(w[n].shape) for n, j in zip(BIG, joined)}

    grads, deltas, new_m, new_v = {}, {}, {}, {}
    for n in BIG:
        grads[n] = gbig[n]
        if n == "w_in":
            gt = gbig[n].T
            dlt, nmt, nvt = _adamw(w[n].T, gt, m[n].T, v[n].T, f"adamw_{n}")
            grads[n], deltas[n], new_m[n], new_v[n] = gt.T, dlt.T, nmt.T, nvt.T
            continue
        deltas[n], new_m[n], new_v[n] = _adamw(w[n], gbig[n], m[n], v[n], f"adamw_{n}")
    sshapes = [w[n].shape for n in SMALL]
    d_s, m_s, v_s = _adamw(_pack([w[n] for n in SMALL]), _pack([gsm[n] for n in SMALL]),
                           _pack([m[n] for n in SMALL]), _pack([v[n] for n in SMALL]), "adamw_small")
    for n, dd, mm, vv in zip(SMALL, _unpack(d_s, sshapes), _unpack(m_s, sshapes), _unpack(v_s, sshapes)):
        grads[n], deltas[n], new_m[n], new_v[n] = gsm[n], dd, mm, vv

    return (loss, dx[None], *[grads[n] for n in ORDER], *[deltas[n] for n in ORDER],
            *[new_m[n] for n in ORDER], *[new_v[n] for n in ORDER])
```

```python
import functools
import math
from typing import Callable, NamedTuple

import jax
import numpy as np
import jax.numpy as jnp
from jax import lax
from jax.experimental import pallas as pl
from jax.experimental.pallas import tpu as pltpu

F32, BF16 = jnp.float32, jnp.bfloat16
MESH = pl.DeviceIdType.MESH

D = 1024
DI = 2048
NH = 32
HP = 64
NG = 4
NS = 128
Q = 128
CONVD = 3072
KCONV = 5
DFF = 4096
AH = 64
ATT_HALF = 64
DILATIONS = (1, 4, 16)
IN_COLS = 9536
OZ, OGATE, OXBC, OKV, OQ, ODT, UW = 0, 2048, 4096, 7168, 8704, 9472, 9728
ALPHA = 2.0 ** 0.25
NORM_EPS = 1e-5
ADAM_LR, ADAM_B1, ADAM_B2, ADAM_EPS, ADAM_WD, ADAM_STEP = 0.001, 0.9, 0.999, 1e-8, 0.01, 10
VMEM_LIMIT = 56 * 2 ** 20
NEG = -1e30


def _params(sem):
    return pltpu.CompilerParams(dimension_semantics=sem, vmem_limit_bytes=VMEM_LIMIT)


def _sigmoid(x):
    return 1.0 / (1.0 + jnp.exp(-x))


def _softplus(x):
    e = jnp.exp(-jnp.abs(x))
    small = e * (1.0 - e * (0.5 - e * (1.0 / 3.0)))
    return jnp.maximum(x, 0.0) + jnp.where(e < 0.01, small, jnp.log(1.0 + e))


def _split3(a):
    hi = a.astype(BF16)
    r = a - hi.astype(F32)
    mid = r.astype(BF16)
    lo = (r - mid.astype(F32)).astype(BF16)
    return hi, mid, lo


def _dot01(a, m01):
    hi, mid, lo = _split3(a)
    d = lambda p: jnp.dot(p, m01, preferred_element_type=F32)
    return d(hi) + d(mid) + d(lo)


def _dot01_l(m01, a):
    hi, mid, lo = _split3(a)
    d = lambda p: jnp.dot(m01, p, preferred_element_type=F32)
    return d(hi) + d(mid) + d(lo)


def _dot_nt(a, b):
    return lax.dot_general(a, b, (((1,), (1,)), ((), ())), preferred_element_type=F32)


def _iota(shape, dim):
    return lax.broadcasted_iota(jnp.int32, shape, dim)


def _mm_tn(a, b, *, tka, tn, tt, name, out_shards=None):
    t, ka = a.shape
    n = b.shape[1]
    if out_shards:
        assert tn == n // out_shards
        out_shape = jax.ShapeDtypeStruct((out_shards, ka, tn), F32)
        o_spec = pl.BlockSpec((None, tka, tn), lambda i, j, s: (j, i, 0))
    else:
        out_shape = jax.ShapeDtypeStruct((ka, n), F32)
        o_spec = pl.BlockSpec((tka, tn), lambda i, j, s: (i, j))

    def body(a_ref, b_ref, o_ref):
        s = pl.program_id(2)
        part = lax.dot_general(a_ref[...].astype(BF16), b_ref[...].astype(BF16), (((0,), (0,)), ((), ())),
                               preferred_element_type=F32)

        @pl.when(s == 0)
        def _():
            o_ref[...] = part

        @pl.when(s > 0)
        def _():
            o_ref[...] += part

    return pl.pallas_call(
        body, out_shape=out_shape, grid=(ka // tka, n // tn, t // tt),
        in_specs=[pl.BlockSpec((tt, tka), lambda i, j, s: (s, i)), pl.BlockSpec((tt, tn), lambda i, j, s: (s, j))],
        out_specs=o_spec, name=name, compiler_params=_params(("parallel", "parallel", "arbitrary")))(a, b)


def _d_x(du, w_in_p, dpre1, side=None):
    t = du.shape[0]
    tm, tc = 1024, 2432
    nc = UW // tc

    def body(a_ref, b_ref, add_ref, o_ref):
        c = pl.program_id(0) % nc
        part = _dot_nt(a_ref[...], b_ref[...])

        @pl.when(c == 0)
        def _():
            o_ref[...] = part + ALPHA * add_ref[...]

        @pl.when(c > 0)
        def _():
            o_ref[...] += part

    outs, side_outs = _host_call(
        body, side, (t // tm) * nc, out_shape=(jax.ShapeDtypeStruct((t, D), F32),),
        in_specs=[pl.BlockSpec((tm, tc), lambda s: (s // nc, s % nc)), pl.BlockSpec((D, tc), lambda s: (0, s % nc)),
                  pl.BlockSpec((tm, D), lambda s: (s // nc, 0))],
        out_specs=(pl.BlockSpec((tm, D), lambda s: (s // nc, 0)),),
        scratch_shapes=[], args=(du, w_in_p, dpre1), aliases={}, name="d_x", sem=("arbitrary",))
    return outs[0], side_outs


def _in_proj(xb, w_in_p, side=None):
    t, k = xb.shape
    tm, tn = 1024, 2432
    nm, nn = t // tm, UW // tn

    def body(a_ref, b_ref, o_ref):
        o_ref[...] = jnp.dot(a_ref[...], b_ref[...], preferred_element_type=F32)

    outs, side_outs = _host_call(
        body, side, nm * nn, out_shape=(jax.ShapeDtypeStruct((t, UW), F32),),
        in_specs=[pl.BlockSpec((tm, k), lambda s: (s % nm, 0)), pl.BlockSpec((k, tn), lambda s: (0, s // nm))],
        out_specs=(pl.BlockSpec((tm, tn), lambda s: (s % nm, s // nm)),),
        scratch_shapes=[], args=(xb, w_in_p), aliases={}, name="in_proj", sem=("arbitrary",))
    return outs[0], side_outs


CONV_TM = 512
CONV_TC = 1024
CONV_RC = 64
CONV_CC = 256


def _halo_specs(t, tm, tc, col0):
    nb8 = t // 8
    r8 = tm // 8
    return [
        pl.BlockSpec((8, tc), lambda i, j: (jnp.maximum(i * r8 - 1, 0), col0 + j)),
        pl.BlockSpec((tm, tc), lambda i, j: (i, col0 + j)),
        pl.BlockSpec((8, tc), lambda i, j: (jnp.minimum((i + 1) * r8, nb8 - 1), col0 + j)),
    ]


def _fill_ext(ext, prev_ref, cur_ref, next_ref, tm, i, last):
    ext[0:8, :] = jnp.where(i > 0, prev_ref[...], 0.0)
    ext[8:8 + tm, :] = cur_ref[...]
    ext[8 + tm:16 + tm, :] = jnp.where(i < last, next_ref[...], 0.0)


def _conv_fwd(u, conv_w, conv_b):
    t = u.shape[0]
    tm, tc = CONV_TM, CONV_TC

    def body(prev_ref, cur_ref, next_ref, w_ref, b_ref, o_ref, ext):
        _fill_ext(ext, prev_ref, cur_ref, next_ref, tm, pl.program_id(0), t // tm - 1)
        for c0 in range(0, tc, CONV_CC):
            cs = slice(c0, c0 + CONV_CC)
            w = w_ref[:, cs]
            for r0 in range(0, tm, CONV_RC):
                acc = jnp.broadcast_to(b_ref[:, cs], (CONV_RC, CONV_CC))
                for k in range(KCONV):
                    acc = acc + w[k:k + 1, :] * ext[pl.ds(r0 + 6 + k, CONV_RC), cs]
                o_ref[r0:r0 + CONV_RC, cs] = acc * _sigmoid(acc)

    return pl.pallas_call(
        body, out_shape=jax.ShapeDtypeStruct((t, CONVD), F32), grid=(t // tm, CONVD // tc),
        in_specs=_halo_specs(t, tm, tc, OXBC // tc) + [
            pl.BlockSpec((KCONV, tc), lambda i, j: (0, j)), pl.BlockSpec((1, tc), lambda i, j: (0, j))],
        out_specs=pl.BlockSpec((tm, tc), lambda i, j: (i, j)),
        scratch_shapes=[pltpu.VMEM((tm + 16, tc), F32)],
        name="conv_fwd", compiler_params=_params(("parallel", "parallel")))(u, u, u, conv_w, conv_b)


def _conv_dpre(u, dxs, dy, dbc, dsk_row, conv_w, conv_b):
    t = u.shape[0]
    tm, tc = CONV_TM, CONV_TC
    r8 = tm // 8
    nb8 = t // 8
    c0 = OXBC // tc

    def body(uprev, ucur, unext, f_ref, y_ref, cf_ref, dsk_ref, w_ref, bias_ref, dpre_ref, dw_ref, db_ref, ext):
        j = pl.program_id(0)
        i = pl.program_id(1)
        _fill_ext(ext, uprev, ucur, unext, tm, i, t // tm - 1)
        is_xs = j < 2
        dw_cols, db_cols = [], []
        for c0 in range(0, tc, CONV_CC):
            cs = slice(c0, c0 + CONV_CC)
            w = w_ref[:, cs]
            dsk = dsk_ref[:, cs]
            dw_acc = [jnp.zeros((1, CONV_CC), F32) for _ in range(KCONV)]
            db_acc = jnp.zeros((1, CONV_CC), F32)
            for r0 in range(0, tm, CONV_RC):
                rs = slice(r0, r0 + CONV_RC)
                taps = [ext[pl.ds(r0 + 6 + k, CONV_RC), cs] for k in range(KCONV)]
                pre = jnp.broadcast_to(bias_ref[:, cs], (CONV_RC, CONV_CC))
                for k in range(KCONV):
                    pre = pre + w[k:k + 1, :] * taps[k]
                s = _sigmoid(pre)
                up = jnp.where(is_xs, f_ref[rs, cs] + dsk * y_ref[rs, cs], cf_ref[rs, cs])
                dpre = up * (s * (1.0 + pre * (1.0 - s)))
                dpre_ref[rs, cs] = dpre
                for k in range(KCONV):
                    dw_acc[k] = dw_acc[k] + jnp.sum(dpre * taps[k], axis=0, keepdims=True)
                db_acc = db_acc + jnp.sum(dpre, axis=0, keepdims=True)
            dw_cols.append(jnp.concatenate(dw_acc + [jnp.zeros((8 - KCONV, CONV_CC), F32)], axis=0))
            db_cols.append(jnp.broadcast_to(db_acc, (8, CONV_CC)))
        dw_part = jnp.concatenate(dw_cols, axis=1)
        db_part = jnp.concatenate(db_cols, axis=1)

        @pl.when(i == 0)
        def _():
            dw_ref[...] = dw_part
            db_ref[...] = db_part

        @pl.when(i > 0)
        def _():
            dw_ref[...] += dw_part
            db_ref[...] += db_part

    xs_spec = pl.BlockSpec((tm, tc), lambda j, i: (jnp.where(j < 2, i, 0), jnp.minimum(j, 1)))
    bc_spec = pl.BlockSpec((tm, tc), lambda j, i: (jnp.where(j == 2, i, 0), 0))
    in_specs = [
        pl.BlockSpec((8, tc), lambda j, i: (jnp.maximum(i * r8 - 1, 0), c0 + j)),
        pl.BlockSpec((tm, tc), lambda j, i: (i, c0 + j)),
        pl.BlockSpec((8, tc), lambda j, i: (jnp.minimum((i + 1) * r8, nb8 - 1), c0 + j)),
        xs_spec, xs_spec, bc_spec,
        pl.BlockSpec((1, tc), lambda j, i: (0, jnp.minimum(j, 1))),
        pl.BlockSpec((KCONV, tc), lambda j, i: (0, j)), pl.BlockSpec((1, tc), lambda j, i: (0, j)),
    ]
    return pl.pallas_call(
        body,
        out_shape=(jax.ShapeDtypeStruct((t, CONVD), F32), jax.ShapeDtypeStruct((8, CONVD), F32),
                   jax.ShapeDtypeStruct((8, CONVD), F32)),
        grid=(CONVD // tc, t // tm), in_specs=in_specs,
        out_specs=(pl.BlockSpec((tm, tc), lambda j, i: (i, j)),
                   pl.BlockSpec((8, tc), lambda j, i: (0, j)), pl.BlockSpec((8, tc), lambda j, i: (0, j))),
        scratch_shapes=[pltpu.VMEM((tm + 16, tc), F32)],
        name="conv_dpre", compiler_params=_params(("parallel", "arbitrary")))(
            u, u, u, dxs, dy, dbc, dsk_row, conv_w, conv_b)


def _conv_dx(du, dpre, conv_w):
    t = dpre.shape[0]
    tm, tc = CONV_TM, CONV_TC
    r8 = tm // 8
    nb8 = t // 8

    def body(prev_ref, cur_ref, next_ref, w_ref, du_in, du_out, ext):
        del du_in
        _fill_ext(ext, prev_ref, cur_ref, next_ref, tm, pl.program_id(1), t // tm - 1)
        for c0 in range(0, tc, CONV_CC):
            cs = slice(c0, c0 + CONV_CC)
            w = w_ref[:, cs]
            for r0 in range(0, tm, CONV_RC):
                acc = jnp.zeros((CONV_RC, CONV_CC), F32)
                for k in range(KCONV):
                    acc = acc + w[k:k + 1, :] * ext[pl.ds(r0 + 10 - k, CONV_RC), cs]
                du_out[r0:r0 + CONV_RC, cs] = acc.astype(du_out.dtype)

    in_specs = [
        pl.BlockSpec((8, tc), lambda j, i: (jnp.maximum(i * r8 - 1, 0), j)),
        pl.BlockSpec((tm, tc), lambda j, i: (i, j)),
        pl.BlockSpec((8, tc), lambda j, i: (jnp.minimum((i + 1) * r8, nb8 - 1), j)),
        pl.BlockSpec((KCONV, tc), lambda j, i: (0, j)),
        pl.BlockSpec(memory_space=pl.ANY),
    ]
    return pl.pallas_call(
        body, out_shape=jax.ShapeDtypeStruct(du.shape, du.dtype), grid=(CONVD // tc, t // tm), in_specs=in_specs,
        out_specs=pl.BlockSpec((tm, tc), lambda j, i: (i, OXBC // tc + j)),
        scratch_shapes=[pltpu.VMEM((tm + 16, tc), F32)], input_output_aliases={4: 0},
        name="conv_dx", compiler_params=_params(("parallel", "parallel")))(dpre, dpre, dpre, conv_w, du)


def _ssd_common(dtr_ref, par_ref, rev):
    raw = dtr_ref[...]
    lane = _iota((1, 128), 1)
    mine = (lane >= 32 * rev) & (lane < 32 * rev + 32)
    bias = par_ref[0:1, :]
    arow = jnp.where(mine, -jnp.exp(par_ref[1:2, :]), 0.0)
    dt = _softplus(raw + bias)
    a = dt * arow
    ri = _iota((Q, Q), 0)
    ci = _iota((Q, Q), 1)
    tri = (ci >= ri) if rev else (ci <= ri)
    trit = (ci <= ri) if rev else (ci >= ri)
    cs = _dot01_l(tri.astype(BF16), a)
    return raw, bias, arow, mine, dt, cs, tri, trit


def _expand_mat(rev):
    r = np.arange(128)[:, None]
    c = np.arange(DI)[None, :]
    return jnp.asarray(r == (c // HP) + 32 * rev, BF16)


def _sum_mat(rev):
    r = np.arange(DI)[:, None]
    c = np.arange(128)[None, :]
    return jnp.asarray(c == (r // HP) + 32 * rev, BF16)


def _ssd_fwd(xbc, u, par, y_add=None, *, rev):
    t = xbc.shape[0]
    nc = t // Q
    end = 0 if rev else Q - 1
    cmap = (lambda c: nc - 1 - c) if rev else (lambda c: c)

    def body(xbc_ref, dtr_ref, par_ref, ex_ref, *rest):
        yadd_ref = rest[0] if y_add is not None else None
        y_ref, st_ref, h_scr = rest[-3:]
        step = pl.program_id(0)

        @pl.when(step == 0)
        def _():
            h_scr[...] = jnp.zeros((NS, DI), F32)

        raw, bias, arow, mine, dt, cs, tri, trit = _ssd_common(dtr_ref, par_ref, rev)
        cst = cs.T
        dtt = dt.T
        tot_col = cst[:, end:end + 1]
        wt = dtt * jnp.exp(tot_col - cst)
        ecs_all = jnp.exp(cs)
        gam = jnp.exp(cs[end:end + 1, :])
        gam_x = _dot01(jnp.broadcast_to(gam, (8, 128)), ex_ref[...])[0:1, :]
        lane = _iota((Q, 128), 1)
        sel = lane < HP
        st_ref[...] = h_scr[...]
        for g in range(NG):
            bg = xbc_ref[:, DI + NS * g:DI + NS * (g + 1)]
            cg = xbc_ref[:, DI + NG * NS + NS * g:DI + NG * NS + NS * (g + 1)]
            cb = _dot_nt(cg.astype(BF16), bg.astype(BF16))
            bt = bg.T
            for k in range(4):
                lo = 512 * g + 128 * k
                xp = xbc_ref[:, lo:lo + 128].astype(BF16)
                hp = h_scr[:, lo:lo + 128]
                rhs = jnp.concatenate([xp, hp.astype(BF16)], axis=0)
                lhs, bts = [], []
                for j in range(2):
                    hc = 8 * g + 2 * k + j + 32 * rev
                    csc = jnp.broadcast_to(cs[:, hc:hc + 1], (Q, Q))
                    lm = jnp.exp(jnp.where(tri, csc - cst[hc:hc + 1, :], NEG)) * dtt[hc:hc + 1, :]
                    mh = (cb * lm).astype(BF16)
                    ec = (jnp.broadcast_to(ecs_all[:, hc:hc + 1], (Q, NS)) * cg).astype(BF16)
                    lhs.append(jnp.concatenate([mh, ec], axis=1))
                    bts.append((bt * wt[hc:hc + 1, :]).astype(BF16))
                ys = jnp.dot(jnp.concatenate(lhs, axis=0), rhs, preferred_element_type=F32)
                ss = jnp.dot(jnp.concatenate(bts, axis=0), xp, preferred_element_type=F32)
                yp = jnp.where(sel, ys[0:Q], ys[Q:2 * Q])
                y_ref[:, lo:lo + 128] = yp if yadd_ref is None else yp + yadd_ref[:, lo:lo + 128]
                h_scr[:, lo:lo + 128] = gam_x[:, lo:lo + 128] * hp + jnp.where(sel, ss[0:NS], ss[NS:2 * NS])

    return pl.pallas_call(
        body,
        out_shape=(jax.ShapeDtypeStruct((t, DI), F32), jax.ShapeDtypeStruct((nc, NS, DI), F32)),
        grid=(nc,),
        in_specs=[pl.BlockSpec((Q, CONVD), lambda c: (cmap(c), 0)),
                  pl.BlockSpec((Q, 128), lambda c: (cmap(c), ODT // 128)),
                  pl.BlockSpec((8, 128), lambda c: (0, 0)),
                  pl.BlockSpec((128, DI), lambda c: (0, 0))]
        + ([pl.BlockSpec((Q, DI), lambda c: (cmap(c), 0))] if y_add is not None else []),
        out_specs=(pl.BlockSpec((Q, DI), lambda c: (cmap(c), 0)),
                   pl.BlockSpec((None, NS, DI), lambda c: (cmap(c), 0, 0))),
        scratch_shapes=[pltpu.VMEM((NS, DI), F32)],
        name="ssd_fwd_rev" if rev else "ssd_fwd", compiler_params=_params(("arbitrary",)))(
            xbc, u, par, _expand_mat(rev), *([y_add] if y_add is not None else []))


def _ssd_bwd(xbc, u, par, dy, st, *, rev, add=None, side=None):
    t = xbc.shape[0]
    nc = t // Q
    end = 0 if rev else Q - 1
    cmap = (lambda c: c) if rev else (lambda c: nc - 1 - c)

    def body(xbc_ref, dtr_ref, par_ref, dy_ref, hin_ref, ex_ref, sm_ref, *rest):
        addx_ref, addbc_ref, addt_ref = rest[:3] if add is not None else (None, None, None)
        dxs_ref, dbc_ref, ddt_ref, acc_ref, dh_scr = rest[-5:]
        step = pl.program_id(0)

        @pl.when(step == 0)
        def _():
            dh_scr[...] = jnp.zeros((NS, DI), F32)

        raw, bias, arow, mine, dt, cs, tri, trit = _ssd_common(dtr_ref, par_ref, rev)
        ri = _iota((Q, Q), 0)
        ci = _iota((Q, Q), 1)
        stri = ((ri > ci) if rev else (ri < ci)).astype(BF16)
        strit = ((ci > ri) if rev else (ci < ri)).astype(BF16)
        cst = cs.T
        dtt = dt.T
        et = jnp.exp(cst)
        ecs_all = jnp.exp(cs)
        ws_all = jnp.exp(cs[end:end + 1, :] - cs)
        expand = ex_ref[...]
        summat = sm_ref[...]
        gam = jnp.exp(cs[end:end + 1, :])
        gam_x = _dot01(jnp.broadcast_to(gam, (8, 128)), expand)[0:1, :]
        dt_hi, dt_mid, _ = _split3(dt)
        dtx = (jnp.dot(dt_hi, expand, preferred_element_type=F32)
               + jnp.dot(dt_mid, expand, preferred_element_type=F32))
        lane = _iota((Q, 128), 1)
        sel = lane < HP
        dho = dh_scr[...]
        t3 = jnp.sum(dho * hin_ref[...], axis=0, keepdims=True) * gam_x
        dxs_cols, dxs2_cols, yoff_cols, a1_rows = [], [], [], []
        for g in range(NG):
            bg = xbc_ref[:, DI + NS * g:DI + NS * (g + 1)]
            cg = xbc_ref[:, DI + NG * NS + NS * g:DI + NG * NS + NS * (g + 1)]
            bb = bg.astype(BF16)
            cbf = cg.astype(BF16)
            cb = _dot_nt(cbf, bb)
            cbt = _dot_nt(bb, cbf)
            ct = cg.T
            bdh = jnp.dot(bb, dho[:, 512 * g:512 * (g + 1)].astype(BF16), preferred_element_type=F32)
            dcb = jnp.zeros((Q, Q), F32)
            dcg = jnp.zeros((Q, NS), F32)
            dbg = jnp.zeros((Q, NS), F32)
            for k in range(4):
                lo = 512 * g + 128 * k
                xpf = xbc_ref[:, lo:lo + 128]
                xp = xpf.astype(BF16)
                dyp = dy_ref[:, lo:lo + 128]
                dypb = dyp.astype(BF16)
                hinp = hin_ref[:, lo:lo + 128].astype(BF16)
                dhp = dho[:, lo:lo + 128]
                es, ws, lmds, mts, ctes, dyms, ecbs = [], [], [], [], [], [], []
                for j in range(2):
                    hc = 8 * g + 2 * k + j + 32 * rev
                    csc = jnp.broadcast_to(cs[:, hc:hc + 1], (Q, Q))
                    csr = cst[hc:hc + 1, :]
                    lmds.append(jnp.exp(jnp.where(tri, csc - csr, NEG)) * dtt[hc:hc + 1, :])
                    lmb = jnp.exp(jnp.where(trit, csr - csc, NEG))
                    mts.append((cbt * lmb).astype(BF16))
                    dyms.append(jnp.where(sel if j == 0 else ~sel, dyp, 0.0).astype(BF16))
                    ecs = jnp.broadcast_to(ecs_all[:, hc:hc + 1], (Q, NS))
                    es.append(ecs)
                    ws.append(jnp.broadcast_to(ws_all[:, hc:hc + 1], (Q, NS)))
                    ecbs.append((ecs * cg).astype(BF16))
                    ctes.append((ct * et[hc:hc + 1, :]).astype(BF16))
                by_dy = jnp.dot(jnp.concatenate(mts + ctes, axis=0), dypb, preferred_element_type=F32)
                dmm = _dot_nt(jnp.concatenate(dyms, axis=0), xp)
                dm0, dm1 = dmm[0:Q] * lmds[0], dmm[Q:2 * Q] * lmds[1]
                dcb = dcb + dm0 + dm1
                rr = jnp.dot(jnp.concatenate([dm0 * cb, dm1 * cb], axis=0).astype(BF16), stri, preferred_element_type=F32)
                a1_rows.append(jnp.sum(jnp.where(tri, rr[0:Q], 0.0), axis=0, keepdims=True))
                a1_rows.append(jnp.sum(jnp.where(tri, rr[Q:2 * Q], 0.0), axis=0, keepdims=True))
                yo = jnp.dot(jnp.concatenate(ecbs, axis=0), hinp, preferred_element_type=F32)
                e_p = jnp.where(sel, es[0], es[1])
                w_p = jnp.where(sel, ws[0], ws[1])
                d2 = w_p * bdh[:, 128 * k:128 * (k + 1)]
                dxs2_cols.append(d2)
                dxs_cols.append(jnp.where(sel, by_dy[0:Q], by_dy[Q:2 * Q]) + d2)
                yoff_cols.append(jnp.where(sel, yo[0:Q], yo[Q:2 * Q]))
                dcg = dcg + _dot_nt((e_p * dyp).astype(BF16), hinp)
                dbg = dbg + _dot_nt((w_p * dtx[:, lo:lo + 128] * xpf).astype(BF16), dhp.astype(BF16))
                dh_scr[:, lo:lo + 128] = (gam_x[:, lo:lo + 128] * dhp
                                          + jnp.where(sel, by_dy[2 * Q:3 * Q], by_dy[3 * Q:4 * Q]))
            dcg = dcg + jnp.dot(dcb.astype(BF16), bb, preferred_element_type=F32)
            dbg = dbg + jnp.dot(dcb.T.astype(BF16), cbf, preferred_element_type=F32)
            lo_b, lo_c = NS * g, NG * NS + NS * g
            if addbc_ref is not None:
                dbg = dbg + addbc_ref[:, lo_b:lo_b + NS]
                dcg = dcg + addbc_ref[:, lo_c:lo_c + NS]
            dbc_ref[:, lo_b:lo_b + NS] = dbg
            dbc_ref[:, lo_c:lo_c + NS] = dcg
        dxs = jnp.concatenate(dxs_cols, axis=1)
        dxs_ref[...] = dxs * dtx if addx_ref is None else dxs * dtx + addx_ref[...]
        xs = xbc_ref[:, 0:DI]
        stacked = jnp.concatenate([xs * dxs, xs * jnp.concatenate(dxs2_cols, axis=1),
                                   dy_ref[...] * jnp.concatenate(yoff_cols, axis=1),
                                   jnp.broadcast_to(t3, (8, DI))], axis=0).astype(BF16)
        sums = jnp.dot(stacked, summat, preferred_element_type=F32)
        rx, rx2, ryo, c0 = sums[0:Q], sums[Q:2 * Q], sums[2 * Q:3 * Q], sums[3 * Q:3 * Q + 1]
        zero32 = jnp.zeros((32, Q), F32)
        a1t = jnp.concatenate(([zero32] if rev else []) + a1_rows + [zero32] * (2 if rev else 3), axis=0)
        da = (a1t.T + jnp.dot(trit.astype(BF16), ryo.astype(BF16), preferred_element_type=F32)
              + jnp.dot(strit, (dt * rx2).astype(BF16), preferred_element_type=F32) + jnp.where(mine, c0, 0.0))
        ddt = rx + da * arow
        ddtr = ddt * _sigmoid(raw + bias)
        ddt_ref[...] = ddtr if addt_ref is None else ddtr + addt_ref[...]
        part = jnp.concatenate([jnp.sum(ddtr, axis=0, keepdims=True),
                                jnp.sum(da * dt, axis=0, keepdims=True) * arow,
                                jnp.zeros((6, 128), F32)], axis=0)

        @pl.when(step == 0)
        def _():
            acc_ref[...] = part

        @pl.when(step > 0)
        def _():
            acc_ref[...] += part

    outs, side_outs = _host_call(
        body, side, nc,
        out_shape=(jax.ShapeDtypeStruct((t, DI), F32), jax.ShapeDtypeStruct((t, 2 * NG * NS), F32),
                   jax.ShapeDtypeStruct((t, 128), F32), jax.ShapeDtypeStruct((8, 128), F32)),
        in_specs=[pl.BlockSpec((Q, CONVD), lambda c: (cmap(c), 0)),
                  pl.BlockSpec((Q, 128), lambda c: (cmap(c), ODT // 128)),
                  pl.BlockSpec((8, 128), lambda c: (0, 0)),
                  pl.BlockSpec((Q, DI), lambda c: (cmap(c), 0)),
                  pl.BlockSpec((None, NS, DI), lambda c: (cmap(c), 0, 0)),
                  pl.BlockSpec((128, DI), lambda c: (0, 0)), pl.BlockSpec((DI, 128), lambda c: (0, 0))]
        + ([pl.BlockSpec((Q, DI), lambda c: (cmap(c), 0)), pl.BlockSpec((Q, 2 * NG * NS), lambda c: (cmap(c), 0)),
            pl.BlockSpec((Q, 128), lambda c: (cmap(c), 0))] if add is not None else []),
        out_specs=(pl.BlockSpec((Q, DI), lambda c: (cmap(c), 0)),
                   pl.BlockSpec((Q, 2 * NG * NS), lambda c: (cmap(c), 0)),
                   pl.BlockSpec((Q, 128), lambda c: (cmap(c), 0)),
                   pl.BlockSpec((8, 128), lambda c: (0, 0))),
        scratch_shapes=[pltpu.VMEM((NS, DI), F32)],
        args=(xbc, u, par, dy, st, _expand_mat(rev), _sum_mat(rev)) + (tuple(add) if add is not None else ()), aliases={},
        name="ssd_bwd_rev" if rev else "ssd_bwd", sem=("arbitrary",))
    return (*outs, side_outs)


GN_TM = 256
GN_GROUP = DI // NG


def _gn_forward_vals(y0, xs, z, dsk):
    y = y0 + dsk * xs
    sz = _sigmoid(z)
    gate = z * sz
    y2 = y * gate
    parts, rs = [], []
    for g in range(NG):
        seg = y2[:, GN_GROUP * g:GN_GROUP * (g + 1)]
        r = lax.rsqrt(jnp.mean(seg * seg, axis=1, keepdims=True) + NORM_EPS)
        rs.append(r)
        parts.append(seg * r)
    yn = jnp.concatenate(parts, axis=1)
    return y, sz, gate, yn, rs


def _gatenorm_fwd(y_fb, xbc, u, dsk_row, nw_row, w_ps):
    t = y_fb.shape[0]
    tm = GN_TM

    def body(y_ref, xs_ref, z_ref, dsk_ref, nw_ref, w_ref, o_ref, ys_ref):
        _, _, _, yn, _ = _gn_forward_vals(y_ref[...], xs_ref[...], z_ref[...], dsk_ref[...])
        s_out = (yn * nw_ref[...]).astype(BF16)
        o_ref[...] = s_out
        ys_ref[...] = jnp.dot(s_out, w_ref[...], preferred_element_type=F32)

    blk = pl.BlockSpec((tm, DI), lambda i: (i, 0))
    row = pl.BlockSpec((1, DI), lambda i: (0, 0))
    return pl.pallas_call(
        body, out_shape=(jax.ShapeDtypeStruct((t, DI), BF16), jax.ShapeDtypeStruct((t, D), F32)), grid=(t // tm,),
        in_specs=[blk, blk, pl.BlockSpec((tm, DI), lambda i: (i, OZ // DI)), row, row,
                  pl.BlockSpec((DI, D), lambda i: (0, 0))],
        out_specs=(blk, pl.BlockSpec((tm, D), lambda i: (i, 0))), name="gatenorm_fwd",
        compiler_params=_params(("parallel",)))(y_fb, xbc, u, dsk_row, nw_row, w_ps)


def _gatenorm_bwd(dy_ssd, w_ps, y_fb, xbc, u, du, dsk_row, nw_row, side=None):
    t = y_fb.shape[0]
    tm = GN_TM

    def body(dys_ref, w_ref, y_ref, xs_ref, z_ref, dsk_ref, nw_ref, sm_ref, du_in, dy_ref, du_out, dnw_ref, dds_ref):
        del du_in
        i = pl.program_id(0)
        xs = xs_ref[...]
        z = z_ref[...]
        y, sz, gate, yn, rs = _gn_forward_vals(y_ref[...], xs, z, dsk_ref[...])
        ds = _dot_nt(dys_ref[...], w_ref[...])
        gsc = ds * nw_ref[...]
        parts = []
        for g in range(NG):
            sl = slice(GN_GROUP * g, GN_GROUP * (g + 1))
            m = jnp.mean(gsc[:, sl] * yn[:, sl], axis=1, keepdims=True)
            parts.append(rs[g] * (gsc[:, sl] - yn[:, sl] * m))
        dy2 = jnp.concatenate(parts, axis=1)
        dy = dy2 * gate
        dy_ref[...] = dy
        du_out[...] = (dy2 * y * (sz * (1.0 + z * (1.0 - sz)))).astype(du_out.dtype)
        dnw = jnp.broadcast_to(jnp.sum(ds * yn, axis=0, keepdims=True), (8, DI))
        drow = jnp.broadcast_to(jnp.sum(dy * xs, axis=0, keepdims=True), (8, DI))
        dds = _dot01(drow, sm_ref[...])

        @pl.when(i == 0)
        def _():
            dnw_ref[...] = dnw
            dds_ref[...] = dds

        @pl.when(i > 0)
        def _():
            dnw_ref[...] += dnw
            dds_ref[...] += dds

    blk = pl.BlockSpec((tm, DI), lambda i: (i, 0))
    row = pl.BlockSpec((1, DI), lambda i: (0, 0))
    outs, side_outs = _host_call(
        body, side, t // tm,
        out_shape=(jax.ShapeDtypeStruct((t, DI), F32), jax.ShapeDtypeStruct(du.shape, du.dtype),
                   jax.ShapeDtypeStruct((8, DI), F32), jax.ShapeDtypeStruct((8, 128), F32)),
        in_specs=[pl.BlockSpec((tm, D), lambda i: (i, 0)), pl.BlockSpec((DI, D), lambda i: (0, 0)),
                  blk, blk, pl.BlockSpec((tm, DI), lambda i: (i, OZ // DI)), row, row,
                  pl.BlockSpec((DI, 128), lambda i: (0, 0)), pl.BlockSpec(memory_space=pl.ANY)],
        out_specs=(blk, pl.BlockSpec((tm, DI), lambda i: (i, OZ // DI)),
                   pl.BlockSpec((8, DI), lambda i: (0, 0)), pl.BlockSpec((8, 128), lambda i: (0, 0))),
        scratch_shapes=[], args=(dy_ssd, w_ps, y_fb, xbc, u, dsk_row, nw_row, _sum_mat(0), du), aliases={8: 1},
        name="gatenorm_bwd", sem=("arbitrary",))
    return (*outs, side_outs)


AT_B = 128
AT_W = AT_B + 2 * ATT_HALF
AT_L = 2 * AH
SCALE = 1.0 / math.sqrt(AH)


def _slope(g, hh):
    return 2.0 ** (-8.0 * (4 * g + hh + 1) / 12.0)


def _qcol(g):
    return lambda p: OQ // AT_L + 2 * g + p


def _kcol(g):
    return lambda p: OKV // AT_L + 4 * g + 2 * p


def _vcol(g):
    return lambda p: OKV // AT_L + 4 * g + 2 * p + 1


def _pcol(p):
    return p


def _sub(d):
    return 4 if d == 1 else 1


def _win_specs(col, t, d):
    tb, hb = AT_B * d * _sub(d), ATT_HALF * d
    per = tb // hb
    nh = t // hb
    return [
        pl.BlockSpec((hb, AT_L), lambda p, i: (jnp.maximum(per * i - 1, 0), col(p))),
        pl.BlockSpec((tb, AT_L), lambda p, i: (i, col(p))),
        pl.BlockSpec((hb, AT_L), lambda p, i: (jnp.minimum(per * (i + 1), nh - 1), col(p))),
    ]


def _blk_spec(col, d):
    return pl.BlockSpec((AT_B * d * _sub(d), AT_L), lambda p, i: (i, col(p)))


def _rows(ref, r, s, d):
    return ref[pl.ds(r, AT_B, stride=d), :] if d > 1 else ref[AT_B * s:AT_B * (s + 1), :]


def _win(p_ref, c_ref, n_ref, r, s, d):
    if d > 1:
        return jnp.concatenate([p_ref[pl.ds(r, ATT_HALF, stride=d), :], c_ref[pl.ds(r, AT_B, stride=d), :],
                                n_ref[pl.ds(r, ATT_HALF, stride=d), :]], axis=0)
    if s == 0:
        return jnp.concatenate([p_ref[...], c_ref[0:AT_B + ATT_HALF, :]], axis=0)
    if s == _sub(d) - 1:
        return jnp.concatenate([c_ref[AT_B * s - ATT_HALF:AT_B * (s + 1), :], n_ref[...]], axis=0)
    return c_ref[AT_B * s - ATT_HALF:AT_B * (s + 1) + ATT_HALF, :]


def _put_rows(ref, r, s, d, val):
    if d > 1:
        ref[pl.ds(r, AT_B, stride=d), :] = val
    else:
        ref[AT_B * s:AT_B * (s + 1), :] = val


def _for_blocks(d, fn):
    if d == 1:
        for s in range(_sub(d)):
            fn(0, s)
    else:
        def step(r, c):
            fn(r, 0)
            return c
        lax.fori_loop(0, d, step, 0, unroll=4)


def _attn_bias(blk, ln, d, g, p_id):
    a = blk * AT_B + _iota((AT_B, AT_W), 0)
    b = blk * AT_B - ATT_HALF + _iota((AT_B, AT_W), 1)
    rel = jnp.abs(a - b)
    valid = (rel <= ATT_HALF) & (b >= 0) & (b < ln)
    dist = (rel * d).astype(F32)
    out = []
    for hh in range(2):
        slope = jnp.where(p_id == 0, _slope(g, hh), _slope(g, 2 + hh))
        out.append(jnp.where(valid, -slope * dist, NEG))
    return out


def _attn_fwd(u, g):
    t = u.shape[0]
    d = DILATIONS[g]
    ln = t // d

    def body(q_ref, kp, kc, kn, vp, vc, vn, o_ref, l_ref):
        p_id = pl.program_id(0)
        i = pl.program_id(1)
        lane = _iota((AT_B, AT_L), 1)
        biases = [_attn_bias(i * _sub(d) + s, ln, d, g, p_id) for s in range(_sub(d))]

        def one(r, s):
            q = _rows(q_ref, r, s, d) * SCALE
            kw = _win(kp, kc, kn, r, s, d).astype(BF16)
            vw = _win(vp, vc, vn, r, s, d).astype(BF16)
            o = jnp.zeros((AT_B, AT_L), F32)
            lse = jnp.zeros((AT_B, AT_L), F32)
            for hh in range(2):
                hm = (lane // AH) == hh
                qm = jnp.where(hm, q, 0.0).astype(BF16)
                sc = _dot_nt(qm, kw) + biases[s][hh]
                m = jnp.max(sc, axis=1, keepdims=True)
                pr = jnp.exp(sc - m)
                den = jnp.sum(pr, axis=1, keepdims=True)
                oh = jnp.dot(pr.astype(BF16), vw, preferred_element_type=F32)
                o = jnp.where(hm, oh / den, o)
                lse = jnp.where(hm, m + jnp.log(den), lse)
            _put_rows(o_ref, r, s, d, o)
            _put_rows(l_ref, r, s, d, lse)

        _for_blocks(d, one)

    oshape = jax.ShapeDtypeStruct((t, 2 * AT_L), F32)
    ospec = _blk_spec(_pcol, d)
    return pl.pallas_call(
        body, out_shape=(oshape, oshape), grid=(2, t // (AT_B * d * _sub(d))),
        in_specs=[_blk_spec(_qcol(g), d)] + _win_specs(_kcol(g), t, d) + _win_specs(_vcol(g), t, d),
        out_specs=(ospec, ospec), name=f"attn_fwd_{g}", compiler_params=_params(("parallel", "parallel")))(
            u, u, u, u, u, u, u)


def _attn_dq(u, du, do, lse, e, g):
    t = u.shape[0]
    d = DILATIONS[g]
    ln = t // d

    def body(q_ref, kp, kc, kn, vp, vc, vn, do_ref, l_ref, e_ref, du_in, dq_ref, dq_scr):
        del du_in
        p_id = pl.program_id(0)
        i = pl.program_id(1)
        lane = _iota((AT_B, AT_L), 1)
        biases = [_attn_bias(i * _sub(d) + s, ln, d, g, p_id) for s in range(_sub(d))]

        def one(r, s):
            q = _rows(q_ref, r, s, d) * SCALE
            kw = _win(kp, kc, kn, r, s, d).astype(BF16)
            vw = _win(vp, vc, vn, r, s, d).astype(BF16)
            do_ = _rows(do_ref, r, s, d)
            lv = _rows(l_ref, r, s, d)
            ev = _rows(e_ref, r, s, d)
            dq = jnp.zeros((AT_B, AT_L), F32)
            for hh in range(2):
                hm = (lane // AH) == hh
                qm = jnp.where(hm, q, 0.0).astype(BF16)
                sc = _dot_nt(qm, kw) + biases[s][hh]
                lcol = jnp.broadcast_to(lv[:, AH * hh:AH * hh + 1], (AT_B, AT_W))
                ecol = jnp.broadcast_to(ev[:, AH * hh:AH * hh + 1], (AT_B, AT_W))
                pr = jnp.exp(sc - lcol)
                dom = jnp.where(hm, do_, 0.0).astype(BF16)
                ds = pr * (_dot_nt(dom, vw) + ecol)
                dqh = jnp.dot(ds.astype(BF16), kw, preferred_element_type=F32) * SCALE
                dq = jnp.where(hm, dqh, dq)
            _put_rows(dq_scr, r, s, d, dq)

        _for_blocks(d, one)
        dq_ref[...] = dq_scr[...].astype(dq_ref.dtype)

    rspec = _blk_spec(_pcol, d)
    return pl.pallas_call(
        body, out_shape=jax.ShapeDtypeStruct(du.shape, du.dtype), grid=(2, t // (AT_B * d * _sub(d))),
        in_specs=[_blk_spec(_qcol(g), d)] + _win_specs(_kcol(g), t, d) + _win_specs(_vcol(g), t, d)
        + [rspec, rspec, rspec, pl.BlockSpec(memory_space=pl.ANY)],
        out_specs=_blk_spec(_qcol(g), d), input_output_aliases={10: 0},
        scratch_shapes=[pltpu.VMEM((AT_B * d * _sub(d), AT_L), F32)],
        name=f"attn_dq_{g}", compiler_params=_params(("parallel", "parallel")))(
            u, u, u, u, u, u, u, do, lse, e, du)


def _attn_dkv(u, du, do, lse, e, g):
    t = u.shape[0]
    d = DILATIONS[g]
    ln = t // d

    def body(k_ref, v_ref, qp, qc, qn, dp_, dc_, dn_, lp, lc, ln_, ep, ec, en, du_in, dkv_ref, dk_scr, dv_scr):
        del du_in
        p_id = pl.program_id(0)
        jb = pl.program_id(1)
        lane = _iota((AT_B, AT_L), 1)
        biases = [_attn_bias(jb * _sub(d) + s, ln, d, g, p_id) for s in range(_sub(d))]

        def one(r, s):
            k = _rows(k_ref, r, s, d) * SCALE
            v = _rows(v_ref, r, s, d)
            qw = _win(qp, qc, qn, r, s, d).astype(BF16)
            dow = _win(dp_, dc_, dn_, r, s, d).astype(BF16)
            lt = _win(lp, lc, ln_, r, s, d).T
            et = _win(ep, ec, en, r, s, d).T
            dk = jnp.zeros((AT_B, AT_L), F32)
            dv = jnp.zeros((AT_B, AT_L), F32)
            for hh in range(2):
                hm = (lane // AH) == hh
                km = jnp.where(hm, k, 0.0).astype(BF16)
                st = _dot_nt(km, qw) + biases[s][hh]
                pt = jnp.exp(st - lt[AH * hh:AH * hh + 1, :])
                dvh = jnp.dot(pt.astype(BF16), dow, preferred_element_type=F32)
                vm = jnp.where(hm, v, 0.0).astype(BF16)
                dst = pt * (_dot_nt(vm, dow) + et[AH * hh:AH * hh + 1, :])
                dkh = jnp.dot(dst.astype(BF16), qw, preferred_element_type=F32) * SCALE
                dk = jnp.where(hm, dkh, dk)
                dv = jnp.where(hm, dvh, dv)
            _put_rows(dk_scr, r, s, d, dk)
            _put_rows(dv_scr, r, s, d, dv)

        _for_blocks(d, one)
        dkv_ref[:, 0:AT_L] = dk_scr[...].astype(dkv_ref.dtype)
        dkv_ref[:, AT_L:2 * AT_L] = dv_scr[...].astype(dkv_ref.dtype)

    return pl.pallas_call(
        body, out_shape=jax.ShapeDtypeStruct(du.shape, du.dtype), grid=(2, t // (AT_B * d * _sub(d))),
        in_specs=[_blk_spec(_kcol(g), d), _blk_spec(_vcol(g), d)]
        + _win_specs(_qcol(g), t, d) + _win_specs(_pcol, t, d) + _win_specs(_pcol, t, d) + _win_specs(_pcol, t, d)
        + [pl.BlockSpec(memory_space=pl.ANY)],
        out_specs=pl.BlockSpec((AT_B * d * _sub(d), 2 * AT_L), lambda p, i: (i, OKV // (2 * AT_L) + 2 * g + p)),
        input_output_aliases={14: 0},
        scratch_shapes=[pltpu.VMEM((AT_B * d * _sub(d), AT_L), F32), pltpu.VMEM((AT_B * d * _sub(d), AT_L), F32)],
        name=f"attn_dkv_{g}", compiler_params=_params(("parallel", "parallel")))(
            u, u, u, u, u, do, do, do, lse, lse, lse, e, e, e, du)


def _combine_weights(l0, l1, l2):
    m = jnp.maximum(jnp.maximum(l0, l1), l2)
    e0, e1, e2 = jnp.exp(l0 - m), jnp.exp(l1 - m), jnp.exp(l2 - m)
    inv = 1.0 / (e0 + e1 + e2)
    return e0 * inv, e1 * inv, e2 * inv


def _att_proj(att, w_ref):
    ab = att.astype(BF16)
    return jnp.concatenate([jnp.dot(ab, w_ref[sh], preferred_element_type=F32) for sh in range(w_ref.shape[0])], axis=1)


def _d_att_combine_bwd(dy_att, w_pa, os_, ls_):
    t = dy_att.shape[0]
    tm = ROW_TM
    nsh, _, ws = w_pa.shape

    def body(dy_ref, w_ref, o0, o1, o2, l0, l1, l2, d0, d1, d2, e0, e1, e2):
        da = jnp.zeros((tm, 2 * AT_L), F32)
        for sh in range(nsh):
            da = da + _dot_nt(dy_ref[:, ws * sh:ws * (sh + 1)], w_ref[sh])
        w = _combine_weights(l0[...], l1[...], l2[...])
        att = w[0] * o0[...] + w[1] * o1[...] + w[2] * o2[...]
        r = _iota((2 * AT_L, 2 * AT_L), 0) // AH
        c = _iota((2 * AT_L, 2 * AT_L), 1) // AH
        hs = _dot01(da * att, (r == c).astype(BF16))
        for wg, dref, eref in zip(w, (d0, d1, d2), (e0, e1, e2)):
            dref[...] = wg * da
            eref[...] = -wg * hs

    blk = pl.BlockSpec((tm, 2 * AT_L), lambda i: (i, 0))
    shp = jax.ShapeDtypeStruct((t, 2 * AT_L), F32)
    outs = pl.pallas_call(
        body, out_shape=(shp,) * 6, grid=(t // tm,),
        in_specs=[pl.BlockSpec((tm, nsh * ws), lambda i: (i, 0)), pl.BlockSpec(w_pa.shape, lambda i: (0, 0, 0))] + [blk] * 6,
        out_specs=(blk,) * 6, name="d_att_combine_bwd", compiler_params=_params(("parallel",)))(dy_att, w_pa, *os_, *ls_)
    return outs[0:3], outs[3:6]


ROW_TM = 512


def _ln(x, g, b):
    mu = jnp.mean(x, axis=1, keepdims=True)
    xc = x - mu
    var = jnp.mean(xc * xc, axis=1, keepdims=True)
    rstd = lax.rsqrt(var + NORM_EPS)
    xhat = xc * rstd
    return xhat * g + b, xhat, rstd


def _ln_back(dh, xhat, rstd, g):
    dxh = dh * g
    m1 = jnp.mean(dxh, axis=1, keepdims=True)
    m2 = jnp.mean(dxh * xhat, axis=1, keepdims=True)
    return rstd * (dxh - m1 - xhat * m2)


def _mlp_up(h1, w_up):
    t = h1.shape[0]
    tm, tn = 2 * ROW_TM, D

    def body(a_ref, b_ref, up_ref, act_ref):
        up = jnp.dot(a_ref[...], b_ref[...], preferred_element_type=F32)
        up_ref[...] = up.astype(BF16)
        r = jnp.maximum(up, 0.0)
        act_ref[...] = (r * r).astype(BF16)

    blk = pl.BlockSpec((tm, tn), lambda j, i: (i, j))
    return pl.pallas_call(
        body, out_shape=(jax.ShapeDtypeStruct((t, DFF), BF16), jax.ShapeDtypeStruct((t, DFF), BF16)),
        grid=(DFF // tn, t // tm),
        in_specs=[pl.BlockSpec((tm, D), lambda j, i: (i, 0)), pl.BlockSpec((None, D, tn), lambda j, i: (j, 0, 0))],
        out_specs=(blk, blk), name="mlp_up", compiler_params=_params(("parallel", "parallel")))(h1, w_up)


def _d_up(dpre2, w_down, up):
    t = up.shape[0]
    tm, tk = 2 * ROW_TM, D

    def body(a_ref, b_ref, u_ref, o_ref):
        dact = _dot_nt(a_ref[...], b_ref[...])
        o_ref[...] = (dact * 2.0 * jnp.maximum(u_ref[...].astype(F32), 0.0)).astype(BF16)

    blk = pl.BlockSpec((tm, tk), lambda j, i: (i, j))
    return pl.pallas_call(
        body, out_shape=jax.ShapeDtypeStruct((t, DFF), BF16), grid=(DFF // tk, t // tm),
        in_specs=[pl.BlockSpec((tm, D), lambda j, i: (i, 0)), pl.BlockSpec((tk, D), lambda j, i: (j, 0)), blk],
        out_specs=blk, name="d_up", compiler_params=_params(("parallel", "parallel")))(dpre2, w_down, up)


def _dt_bwd(du, ddt):
    t = ddt.shape[0]
    tm = 1024

    def body(f_ref, du_in, o_ref):
        del du_in
        o_ref[:, 0:128] = f_ref[...].astype(o_ref.dtype)
        o_ref[:, 128:256] = jnp.zeros((tm, 128), o_ref.dtype)

    blk = pl.BlockSpec((tm, 128), lambda i: (i, 0))
    return pl.pallas_call(
        body, out_shape=jax.ShapeDtypeStruct(du.shape, du.dtype), grid=(t // tm,),
        in_specs=[blk, pl.BlockSpec(memory_space=pl.ANY)],
        out_specs=pl.BlockSpec((tm, 256), lambda i: (i, ODT // 256)), input_output_aliases={1: 0},
        name="dt_bwd", compiler_params=_params(("parallel",)))(ddt, du)


def _mix_out_ln1(y_ssd, os_, ls_, w_pa, u, bg_row, x, w_out, g_row, b_row):
    t = x.shape[0]
    tm = ROW_TM

    def body(ys_ref, o0, o1, o2, l0, l1, l2, wpa_ref, g0_ref, g1_ref, b0_ref, b1_ref, x_ref, w_ref, g_ref, b_ref,
             att_ref, mixin_ref, pre_ref, h_ref):
        w0, w1, w2 = _combine_weights(l0[...], l1[...], l2[...])
        att = w0 * o0[...] + w1 * o1[...] + w2 * o2[...]
        att_ref[...] = att
        g0 = _sigmoid(g0_ref[...] + b0_ref[...])
        g1 = _sigmoid(g1_ref[...] + b1_ref[...])
        mixin = (g0 * ys_ref[...] + g1 * _att_proj(att, wpa_ref)).astype(BF16)
        mixin_ref[...] = mixin
        pre = ALPHA * x_ref[...] + jnp.dot(mixin, w_ref[...], preferred_element_type=F32)
        pre_ref[...] = pre
        h, _, _ = _ln(pre, g_ref[...], b_ref[...])
        h_ref[...] = h.astype(BF16)

    blk = pl.BlockSpec((tm, D), lambda i: (i, 0))
    ablk = pl.BlockSpec((tm, 2 * AT_L), lambda i: (i, 0))
    row = pl.BlockSpec((1, D), lambda i: (0, 0))
    return pl.pallas_call(
        body,
        out_shape=(jax.ShapeDtypeStruct((t, 2 * AT_L), F32), jax.ShapeDtypeStruct((t, D), BF16),
                   jax.ShapeDtypeStruct((t, D), F32), jax.ShapeDtypeStruct((t, D), BF16)),
        grid=(t // tm,),
        in_specs=[blk] + [ablk] * 6 + [pl.BlockSpec(w_pa.shape, lambda i: (0, 0, 0)),
                  pl.BlockSpec((tm, D), lambda i: (i, OGATE // D)), pl.BlockSpec((tm, D), lambda i: (i, OGATE // D + 1)),
                  row, pl.BlockSpec((1, D), lambda i: (0, 1)), blk, pl.BlockSpec((D, D), lambda i: (0, 0)), row, row],
        out_specs=(ablk, blk, blk, blk), name="mix_out_ln1", compiler_params=_params(("parallel",)))(
            y_ssd, *os_, *ls_, w_pa, u, u, bg_row, bg_row, x, w_out, g_row, b_row)


def _mlp_down_ln2_loss(act, w_down, pre1, tgt, g1_row, b1_row, g2_row, b2_row):
    t = pre1.shape[0]
    tm = ROW_TM

    def body(a_ref, w_ref, p1_ref, t_ref, g1_ref, b1_ref, g2_ref, b2_ref, dpre_ref, dpreb_ref, acc_ref):
        i = pl.program_id(0)
        f = jnp.dot(a_ref[...], w_ref[...], preferred_element_type=F32)
        h1, _, _ = _ln(p1_ref[...], g1_ref[...], b1_ref[...])
        pre2 = ALPHA * h1 + f
        h2, xhat, rstd = _ln(pre2, g2_ref[...], b2_ref[...])
        err = h2 - t_ref[...]
        dh = err * (1.0 / D)
        dpre = _ln_back(dh, xhat, rstd, g2_ref[...])
        dpre_ref[...] = dpre
        dpreb_ref[...] = dpre.astype(BF16)
        loss = jnp.sum(jnp.sum(err * err, axis=1, keepdims=True), axis=0, keepdims=True) * (0.5 / D)
        part = jnp.concatenate([jnp.sum(dh * xhat, axis=0, keepdims=True), jnp.sum(dh, axis=0, keepdims=True),
                                jnp.broadcast_to(loss, (1, D)), jnp.zeros((5, D), F32)], axis=0)

        @pl.when(i == 0)
        def _():
            acc_ref[...] = part

        @pl.when(i > 0)
        def _():
            acc_ref[...] += part

    blk = pl.BlockSpec((tm, D), lambda i: (i, 0))
    row = pl.BlockSpec((1, D), lambda i: (0, 0))
    return pl.pallas_call(
        body,
        out_shape=(jax.ShapeDtypeStruct((t, D), F32), jax.ShapeDtypeStruct((t, D), BF16), jax.ShapeDtypeStruct((8, D), F32)),
        grid=(t // tm,),
        in_specs=[pl.BlockSpec((tm, DFF), lambda i: (i, 0)), pl.BlockSpec((DFF, D), lambda i: (0, 0)), blk, blk, row, row, row, row],
        out_specs=(blk, blk, pl.BlockSpec((8, D), lambda i: (0, 0))),
        name="mlp_down_ln2_loss", compiler_params=_params(("arbitrary",)))(act, w_down, pre1, tgt, g1_row, b1_row, g2_row, b2_row)


def _d_h1_ln1_bwd(dup, w_up, dpre2, pre1, g_row, b_row):
    t = dup.shape[0]
    tm = ROW_TM
    nsh = w_up.shape[0]

    def body(a_ref, w_ref, add_ref, pre_ref, g_ref, b_ref, dpre_ref, acc_ref):
        i = pl.program_id(0)
        dh_ = ALPHA * add_ref[...]
        for sh in range(nsh):
            dh_ = dh_ + _dot_nt(a_ref[:, D * sh:D * (sh + 1)], w_ref[sh])
        _, xhat, rstd = _ln(pre_ref[...], g_ref[...], b_ref[...])
        dpre_ref[...] = _ln_back(dh_, xhat, rstd, g_ref[...])
        rows = jnp.concatenate([jnp.sum(dh_ * xhat, axis=0, keepdims=True), jnp.sum(dh_, axis=0, keepdims=True),
                                jnp.zeros((6, D), F32)], axis=0)

        @pl.when(i == 0)
        def _():
            acc_ref[...] = rows

        @pl.when(i > 0)
        def _():
            acc_ref[...] += rows

    blk = pl.BlockSpec((tm, D), lambda i: (i, 0))
    row = pl.BlockSpec((1, D), lambda i: (0, 0))
    return pl.pallas_call(
        body, out_shape=(jax.ShapeDtypeStruct((t, D), F32), jax.ShapeDtypeStruct((8, D), F32)),
        grid=(t // tm,),
        in_specs=[pl.BlockSpec((tm, nsh * D), lambda i: (i, 0)), pl.BlockSpec(w_up.shape, lambda i: (0, 0, 0)),
                  blk, blk, row, row],
        out_specs=(blk, pl.BlockSpec((8, D), lambda i: (0, 0))),
        name="d_h1_ln1_bwd", compiler_params=_params(("arbitrary",)))(dup, w_up, dpre2, pre1, g_row, b_row)


def _d_mixin_mix_bwd(dpre1, w_out, y_ssd, att, w_pa, u, bg_row):
    t = y_ssd.shape[0]
    tm = ROW_TM

    def body(a_ref, w_ref, ys_ref, att_ref, wpa_ref, g0_ref, g1_ref, b0_ref, b1_ref, dys_ref, dya_ref, du_ref, db_ref):
        i = pl.program_id(0)
        dm = _dot_nt(a_ref[...].astype(BF16), w_ref[...])
        g0 = _sigmoid(g0_ref[...] + b0_ref[...])
        g1 = _sigmoid(g1_ref[...] + b1_ref[...])
        dys_ref[...] = (dm * g0).astype(BF16)
        dya_ref[...] = (dm * g1).astype(BF16)
        dl0 = dm * ys_ref[...] * g0 * (1.0 - g0)
        dl1 = dm * _att_proj(att_ref[...], wpa_ref) * g1 * (1.0 - g1)
        du_ref[:, 0:D] = dl0.astype(BF16)
        du_ref[:, D:2 * D] = dl1.astype(BF16)
        part = jnp.concatenate([jnp.broadcast_to(jnp.sum(dl0, axis=0, keepdims=True), (8, D)),
                                jnp.broadcast_to(jnp.sum(dl1, axis=0, keepdims=True), (8, D))], axis=1)

        @pl.when(i == 0)
        def _():
            db_ref[...] = part

        @pl.when(i > 0)
        def _():
            db_ref[...] += part

    blk = pl.BlockSpec((tm, D), lambda i: (i, 0))
    return pl.pallas_call(
        body,
        out_shape=(jax.ShapeDtypeStruct((t, D), BF16), jax.ShapeDtypeStruct((t, D), BF16),
                   jax.ShapeDtypeStruct((t, UW), BF16), jax.ShapeDtypeStruct((8, 2 * D), F32)),
        grid=(t // tm,),
        in_specs=[blk, pl.BlockSpec((D, D), lambda i: (0, 0)), blk,
                  pl.BlockSpec((tm, 2 * AT_L), lambda i: (i, 0)), pl.BlockSpec(w_pa.shape, lambda i: (0, 0, 0)),
                  pl.BlockSpec((tm, D), lambda i: (i, OGATE // D)), pl.BlockSpec((tm, D), lambda i: (i, OGATE // D + 1)),
                  pl.BlockSpec((1, D), lambda i: (0, 0)), pl.BlockSpec((1, D), lambda i: (0, 1))],
        out_specs=(blk, blk, pl.BlockSpec((tm, 2 * D), lambda i: (i, OGATE // (2 * D))),
                   pl.BlockSpec((8, 2 * D), lambda i: (0, 0))),
        name="d_mixin_mix_bwd", compiler_params=_params(("arbitrary",)))(dpre1, w_out, y_ssd, att, w_pa, u, u, bg_row, bg_row)


def _adamw(w, g, m, v, name):
    r, c = w.shape
    tr, tc = r, c
    for cand in (256, 128, 64, 32, 16, 8):
        if r % cand == 0 and cand * c * 4 <= 2 ** 21:
            tr = cand
            break
    if tr < 64 and c % 256 == 0:
        tr, tc = r, 256
    bc1 = 1.0 / (1.0 - ADAM_B1 ** ADAM_STEP)
    bc2 = 1.0 / (1.0 - ADAM_B2 ** ADAM_STEP)

    def body(w_ref, g_ref, m_ref, v_ref, d_ref, nm_ref, nv_ref):
        gg = g_ref[...]
        nm = ADAM_B1 * m_ref[...] + (1.0 - ADAM_B1) * gg
        nv = ADAM_B2 * v_ref[...] + (1.0 - ADAM_B2) * (gg * gg)
        nm_ref[...] = nm
        nv_ref[...] = nv
        d_ref[...] = -ADAM_LR * ((nm * bc1) / (jnp.sqrt(nv * bc2) + ADAM_EPS) + ADAM_WD * w_ref[...])

    blk = pl.BlockSpec((tr, tc), lambda i, j: (i, j))
    shp = jax.ShapeDtypeStruct((r, c), F32)
    return pl.pallas_call(body, out_shape=(shp, shp, shp), grid=(r // tr, c // tc), in_specs=[blk] * 4,
                          out_specs=(blk,) * 3, name=name, compiler_params=_params(("parallel", "parallel")))(w, g, m, v)


def _segments():
    segs = [(0, 2048), (7488, 9536), (2048, 5120)]
    for g in range(3):
        for p in range(2):
            lo = 256 * g + 128 * p
            segs += [(5952 + lo, 5952 + lo + 128), (6720 + lo, 6720 + lo + 128)]
    segs += [(5184, 5952), (5120, 5184)]
    out, pos = [], 0
    for a, b in segs:
        out.append((a, b, pos))
        pos += b - a
    return out


SHARD_COLS = IN_COLS // 4


def _perm_from_shards(w_shards):
    pieces = []
    for a, b, _ in _segments():
        while a < b:
            s = a // SHARD_COLS
            e = min(b, (s + 1) * SHARD_COLS)
            pieces.append(w_shards[s][:, a - s * SHARD_COLS:e - s * SHARD_COLS])
            a = e
    pieces.append(jnp.zeros((w_shards.shape[1], UW - IN_COLS), w_shards.dtype))
    return jnp.concatenate(pieces, axis=1)


def _shards_from_perm(wp):
    segs = sorted(_segments())
    shards = []
    for s in range(4):
        lo, hi = s * SHARD_COLS, (s + 1) * SHARD_COLS
        pieces = []
        for a, b, pos in segs:
            x, y = max(a, lo), min(b, hi)
            if x < y:
                pieces.append(wp[:, pos + x - a:pos + y - a])
        shards.append(jnp.concatenate(pieces, axis=1))
    return jnp.stack(shards)


def _lanes128(*vecs):
    v = jnp.concatenate([a.reshape(-1) for a in vecs])
    return jnp.pad(v, (0, 128 - v.shape[0])).reshape(1, 128)


EARLY = ("w_proj_ssd", "w_proj_attn", "w_out", "w_up", "w_down")


def _weights_of(gathered):
    g_ps, g_pa, g_o, g_up, g_dn = gathered
    return {"w_proj_ssd": g_ps.reshape(DI, D), "w_proj_attn": g_pa, "w_out": g_o.reshape(D, D), "w_up": g_up,
            "w_down": g_dn.reshape(DFF, D)}


def _local_grads(x, tgt, wts, sm, rs_idx=None):
    row = lambda a: a.reshape(1, -1)
    bg_row, cb_row = row(sm["b_gate"]), row(sm["conv_b"])
    par = jnp.concatenate([_lanes128(sm["dt_bias_f"], sm["dt_bias_b"]), _lanes128(sm["a_log_f"], sm["a_log_b"]),
                           jnp.zeros((6, 128), F32)], axis=0)
    dsk_row = row(jnp.repeat(sm["d_skip"], HP))
    nw_row = row(sm["ssd_norm_w"])
    g1, b1, g2, b2 = row(sm["ln1_g"]), row(sm["ln1_b"]), row(sm["ln2_g"]), row(sm["ln2_b"])

    xb = x.astype(BF16)
    u, gathered = _in_proj(xb, wts["w_in_p"], side=_gather_side(wts["pending"]) if "pending" in wts else None)
    if gathered:
        wts = {**wts, **_weights_of(gathered)}
    xbc = _conv_fwd(u, sm["conv_w"], cb_row)
    y_f, st_f = _ssd_fwd(xbc, u, par, rev=False)
    y_fb, st_b = _ssd_fwd(xbc, u, par, y_f, rev=True)
    s_out, y_ssd = _gatenorm_fwd(y_fb, xbc, u, dsk_row, nw_row, wts["w_proj_ssd"])
    att_o, att_l = [], []
    for g in range(3):
        o, l = _attn_fwd(u, g)
        att_o.append(o)
        att_l.append(l)
    att, mixin, pre1, h1 = _mix_out_ln1(y_ssd, att_o, att_l, wts["w_proj_attn"], u, bg_row, x, wts["w_out"], g1, b1)
    up, act = _mlp_up(h1, wts["w_up"])
    dpre2, dpre2_b, acc2 = _mlp_down_ln2_loss(act, wts["w_down"], pre1, tgt, g1, b1, g2, b2)

    dw_down = _mm_tn(act, dpre2_b, tka=1024, tn=1024, tt=1024, name="dw_down")
    dup = _d_up(dpre2_b, wts["w_down"], up)
    dw_up = _mm_tn(h1, dup, tka=1024, tn=1024, tt=1024, name="dw_up", out_shards=4)
    dpre1, acc1 = _d_h1_ln1_bwd(dup, wts["w_up"], dpre2, pre1, g1, b1)
    dw_out = _mm_tn(mixin, dpre1, tka=1024, tn=1024, tt=1024, name="dw_out")
    dy_ssd, dy_att, du, dbg = _d_mixin_mix_bwd(dpre1, wts["w_out"], y_ssd, att, wts["w_proj_attn"], u, bg_row)
    dw_proj_ssd = _mm_tn(s_out, dy_ssd, tka=1024, tn=1024, tt=1024, name="dw_proj_ssd")
    dw_proj_attn = _mm_tn(att, dy_att, tka=256, tn=256, tt=1024, name="dw_proj_attn", out_shards=4)
    do_g, e_g = _d_att_combine_bwd(dy_att, wts["w_proj_attn"], att_o, att_l)
    for g in range(3):
        du = _attn_dq(u, du, do_g[g], att_l[g], e_g[g], g)
        du = _attn_dkv(u, du, do_g[g], att_l[g], e_g[g], g)
    big = {
        "w_proj_ssd": dw_proj_ssd.reshape(4, DI // 4, D),
        "w_proj_attn": dw_proj_attn,
        "w_out": dw_out.reshape(4, D // 4, D),
        "w_up": dw_up,
        "w_down": dw_down.reshape(4, DFF // 4, D),
    }
    early = [big[n] for n in EARLY]
    dy, du, dnw, dds, recv = _gatenorm_bwd(dy_ssd, wts["w_proj_ssd"], y_fb, xbc, u, du, dsk_row, nw_row,
                                           side=_swap_side(early) if rs_idx else None)
    if rs_idx:
        halves = [_add_half(g, r, rs_idx[0], f"rs_add_half_{n}") for g, r, n in zip(early, recv, EARLY)]
    dxs_f, dbc_f, ddt_f, sacc_f, recv = _ssd_bwd(xbc, u, par, dy, st_f, rev=False,
                                                 side=_step1_side([h[1] for h in halves]) if rs_idx else None)
    if rs_idx:
        k = len(EARLY)
        sums1 = [_rs_add1(h[0], ra, rb, rs_idx[1], f"rs_add1_{n}")
                 for h, ra, rb, n in zip(halves, recv[:k], recv[k:], EARLY)]
    dxs, dbc, ddt, sacc_b, recv = _ssd_bwd(
        xbc, u, par, dy, st_b, rev=True, add=(dxs_f, dbc_f, ddt_f),
        side=_step2_side([s1[2] for s1 in sums1], [s1[3] for s1 in sums1]) if rs_idx else None)
    pieces = None
    if rs_idx:
        pieces = {n: _rs_add2(s1[0], s1[1], ra, rb, rs_idx[1], f"rs_add2_{n}")
                  for s1, ra, rb, n in zip(sums1, recv[:k], recv[k:], EARLY)}
    dpre_c, dcw, dcb = _conv_dpre(u, dxs, dy, dbc, dsk_row, sm["conv_w"], cb_row)
    du = _conv_dx(du, dpre_c, sm["conv_w"])
    du = _dt_bwd(du, ddt)
    dw_in_p = _mm_tn(xb, du, tka=1024, tn=2432, tt=1024, name="dw_in")
    big["w_in"] = _shards_from_perm(dw_in_p)
    side = None
    if rs_idx:
        g = big["w_in"]
        half = _add_half(g, _run_side(_swap_side([g]), "rs_swap_halves")[0], rs_idx[0], "rs_add_half_w_in")
        side = _step1_side([half[1]])
    dx, recv = _d_x(du, wts["w_in_p"], dpre1, side)
    if rs_idx:
        s1 = _rs_add1(half[0], recv[0], recv[1], rs_idx[1], "rs_add1_w_in")
        ra2, rb2 = _run_side(_step2_side([s1[2]], [s1[3]]), "rs_step2")
        pieces["w_in"] = _rs_add2(s1[0], s1[1], ra2, rb2, rs_idx[1], "rs_add2_w_in")

    sacc = sacc_f + sacc_b
    small = {
        "b_gate": dbg[0], "conv_w": dcw[0:KCONV], "conv_b": dcb[0],
        "dt_bias_f": sacc[0, 0:32], "dt_bias_b": sacc[0, 32:64], "a_log_f": sacc[1, 0:32], "a_log_b": sacc[1, 32:64],
        "d_skip": dds[0, 0:32], "ssd_norm_w": dnw[0],
        "ln1_g": acc1[0], "ln1_b": acc1[1], "ln2_g": acc2[0], "ln2_b": acc2[1], "loss": acc2[2, 0:1],
    }
    return dx, big, small, pieces


HBM_SPEC = pl.BlockSpec(memory_space=pl.ANY)


def _place():
    x, y, c = lax.axis_index("x"), lax.axis_index("y"), lax.axis_index("c")
    chips = [(1 - x, y), (x, 1 - y), (1 - x, 1 - y)]
    return x, y, c, chips


def _gather_phases(n):
    def tools(ins, outs, send_sems, recv_sems):
        x, y, c, _ = _place()
        slots = (2 * x + y, 2 * (1 - x) + y, 2 * x + 1 - y, 2 * (1 - x) + 1 - y)
        peers = ((1 - x, y, c), (x, 1 - y, c), (x, y, 1 - c))

        def copy(w, k, src, dst, to):
            return pltpu.make_async_remote_copy(src_ref=src, dst_ref=dst, send_sem=send_sems.at[w, k],
                                                recv_sem=recv_sems.at[w, k], device_id=to, device_id_type=MESH)

        def rows(w, core, part):
            rh = ins[w].shape[0] // 2
            if part is None:
                return pl.ds(core * rh, rh)
            return pl.ds(core * rh + part * (rh // 2), rh // 2)

        def same(w, k, slot, core, part, to):
            blk = outs[w].at[slot, rows(w, core, part), :]
            return copy(w, k, blk, blk, to)

        def sends(w):
            q, q_x, q_y, q_d = slots
            x_nbr, y_nbr, sibling = peers
            mine = rows(w, c, None)
            mk = functools.partial
            return [mk(copy, w, 0, ins[w].at[mine, :], outs[w].at[q, mine, :], x_nbr),
                    mk(copy, w, 1, ins[w].at[mine, :], outs[w].at[q, mine, :], y_nbr),
                    mk(same, w, 2, q_x, c, 0, y_nbr), mk(same, w, 3, q_y, c, 1, x_nbr),
                    mk(same, w, 4, q_x, c, None, sibling), mk(same, w, 5, q_y, c, None, sibling),
                    mk(same, w, 6, q_d, c, 0, sibling), mk(same, w, 7, q_d, c, 1, sibling),
                    mk(copy, w, 8, ins[w], outs[w].at[q], sibling)]

        return c, slots, peers, same, sends

    def first(*refs):
        _, _, _, _, sends = tools(*refs)
        for w in range(n):
            cps = sends(w)
            for k in (8, 0, 1):
                cps[k]().start()

    def second(*refs):
        c, (_, q_x, q_y, _), (x_nbr, y_nbr, _), same, sends = tools(*refs)
        for w in range(n):
            cps = sends(w)
            same(w, 0, q_x, c, None, x_nbr).wait_recv()
            cps[2]().start()
            cps[4]().start()
            same(w, 1, q_y, c, None, y_nbr).wait_recv()
            cps[3]().start()
            cps[5]().start()

    def third(*refs):
        c, (_, _, _, q_d), (x_nbr, y_nbr, _), same, sends = tools(*refs)
        for w in range(n):
            cps = sends(w)
            same(w, 2, q_d, c, 0, y_nbr).wait_recv()
            cps[6]().start()
            same(w, 3, q_d, c, 1, x_nbr).wait_recv()
            cps[7]().start()

    def last(*refs):
        c, (_, q_x, q_y, q_d), (_, _, sibling), same, sends = tools(*refs)
        for w in range(n):
            same(w, 4, q_x, 1 - c, None, sibling).wait_recv()
            same(w, 5, q_y, 1 - c, None, sibling).wait_recv()
            same(w, 6, q_d, 1 - c, 0, sibling).wait_recv()
            same(w, 7, q_d, 1 - c, 1, sibling).wait_recv()
            sends(w)[8]().wait_recv()
        for w in range(n):
            for mk_cp in sends(w):
                mk_cp().wait_send()

    return first, second, third, last


def _gather_side(shards):
    first, second, third, last = _gather_phases(len(shards))
    shapes = tuple(jax.ShapeDtypeStruct((4,) + s.shape, s.dtype) for s in shards)
    return _Side(tuple(shards), shapes, (len(shards), 9), None, ((0.0, first), (0.36, second), (0.58, third), (1.0, last)))


class _Side(NamedTuple):
    ins: tuple
    out_shapes: tuple
    nsem: tuple
    make: Callable
    phases: tuple = ()


def _swap_copies(ins, outs, send_sems, recv_sems):
    x, y, c, _ = _place()
    copies = []
    for w in range(len(ins)):
        rh = ins[w].shape[1] // 2
        for p in range(4):
            copies.append(pltpu.make_async_remote_copy(
                src_ref=ins[w].at[p, pl.ds((1 - c) * rh, rh), :], dst_ref=outs[w].at[p],
                send_sem=send_sems.at[w, p], recv_sem=recv_sems.at[w, p],
                device_id=(x, y, 1 - c), device_id_type=MESH))
    return copies


def _swap_side(grads):
    shapes = tuple(jax.ShapeDtypeStruct((4, g.shape[1] // 2, g.shape[2]), F32) for g in grads)
    return _Side(tuple(grads), shapes, (len(grads), 4), _swap_copies)


def _step1_copies(ins, outs, send_sems, recv_sems):
    n = len(ins)
    out_a, out_b = outs[:n], outs[n:]
    x, y, c, _ = _place()
    copies = []
    for w in range(n):
        rq = ins[w].shape[1] // 2
        for i in range(2):
            copies.append(pltpu.make_async_remote_copy(
                src_ref=ins[w].at[2 * (1 - x) + i, pl.ds(0, rq), :], dst_ref=out_a[w].at[i],
                send_sem=send_sems.at[w, i], recv_sem=recv_sems.at[w, i],
                device_id=(1 - x, y, c), device_id_type=MESH))
            copies.append(pltpu.make_async_remote_copy(
                src_ref=ins[w].at[2 * i + 1 - y, pl.ds(rq, rq), :], dst_ref=out_b[w].at[i],
                send_sem=send_sems.at[w, 2 + i], recv_sem=recv_sems.at[w, 2 + i],
                device_id=(x, 1 - y, c), device_id_type=MESH))
    return copies


def _step1_side(parts):
    quarter = tuple(jax.ShapeDtypeStruct((2, p.shape[1] // 2, p.shape[2]), p.dtype) for p in parts)
    return _Side(tuple(parts), quarter + quarter, (len(parts), 4), _step1_copies)


def _step2_copies(ins, outs, send_sems, recv_sems):
    n = len(ins) // 2
    in_a, in_b, out_a, out_b = ins[:n], ins[n:], outs[:n], outs[n:]
    x, y, c, _ = _place()
    copies = []
    for w in range(n):
        copies.append(pltpu.make_async_remote_copy(
            src_ref=in_a[w].at[1 - y], dst_ref=out_a[w], send_sem=send_sems.at[w, 0], recv_sem=recv_sems.at[w, 0],
            device_id=(x, 1 - y, c), device_id_type=MESH))
        copies.append(pltpu.make_async_remote_copy(
            src_ref=in_b[w].at[1 - x], dst_ref=out_b[w], send_sem=send_sems.at[w, 1], recv_sem=recv_sems.at[w, 1],
            device_id=(1 - x, y, c), device_id_type=MESH))
    return copies


def _step2_side(tas, tbs):
    one = tuple(jax.ShapeDtypeStruct(p.shape[1:], p.dtype) for p in tuple(tas) + tuple(tbs))
    return _Side(tuple(tas) + tuple(tbs), one, (len(tas), 2), _step2_copies)


def _phases_of(side, n_steps):
    if side.phases:
        return [(min(int(f * n_steps), n_steps - 1), fn) for f, fn in side.phases]

    def start(*refs):
        for cp in side.make(*refs):
            cp.start()

    def wait(*refs):
        for cp in side.make(*refs):
            cp.wait()

    return [(0, start), (n_steps - 1, wait)]


def _run_side(side, name):
    n_in, n_out = len(side.ins), len(side.out_shapes)

    def body(*refs):
        for _, fn in _phases_of(side, 1):
            fn(refs[:n_in], refs[n_in:n_in + n_out], *refs[n_in + n_out:])

    return pl.pallas_call(
        body, out_shape=list(side.out_shapes), in_specs=[HBM_SPEC] * n_in, out_specs=[HBM_SPEC] * n_out,
        scratch_shapes=[pltpu.SemaphoreType.DMA(side.nsem), pltpu.SemaphoreType.DMA(side.nsem)], name=name)(*side.ins)


def _host_call(body, side, n_steps, *, out_shape, in_specs, out_specs, scratch_shapes, args, aliases, name, sem):
    n_in, n_out, n_scr = len(in_specs), len(out_shape), len(scratch_shapes)
    if side is None:
        outs = pl.pallas_call(body, out_shape=tuple(out_shape), grid=(n_steps,), in_specs=list(in_specs),
                              out_specs=tuple(out_specs), scratch_shapes=list(scratch_shapes),
                              input_output_aliases=aliases, name=name, compiler_params=_params(sem))(*args)
        return tuple(outs), ()
    ns_in, ns_out = len(side.ins), len(side.out_shapes)

    def wrapped(*refs):
        h_in, s_in = refs[:n_in], refs[n_in:n_in + ns_in]
        o0 = n_in + ns_in
        h_out, s_out = refs[o0:o0 + n_out], refs[o0 + n_out:o0 + n_out + ns_out]
        c0 = o0 + n_out + ns_out
        h_scr, sems = refs[c0:c0 + n_scr], refs[c0 + n_scr:]
        step = pl.program_id(0)
        phases = _phases_of(side, n_steps)
        for at, fn in phases[:-1]:
            pl.when(step == at)(functools.partial(fn, s_in, s_out, *sems))
        body(*h_in, *h_out, *h_scr)
        pl.when(step == phases[-1][0])(functools.partial(phases[-1][1], s_in, s_out, *sems))

    outs = pl.pallas_call(
        wrapped, out_shape=tuple(out_shape) + tuple(side.out_shapes), grid=(n_steps,),
        in_specs=list(in_specs) + [HBM_SPEC] * ns_in, out_specs=tuple(out_specs) + (HBM_SPEC,) * ns_out,
        scratch_shapes=list(scratch_shapes) + [pltpu.SemaphoreType.DMA(side.nsem), pltpu.SemaphoreType.DMA(side.nsem)],
        input_output_aliases=aliases, name=name, compiler_params=_params(sem))(*args, *side.ins)
    return tuple(outs[:n_out]), tuple(outs[n_out:])


def _join_halves(pieces):
    n = len(pieces)

    def body(*refs):
        outs = refs[n:2 * n]
        send_sems, recv_sems = refs[2 * n:]
        x, y, c, _ = _place()

        def copy(w, slot):
            return pltpu.make_async_remote_copy(
                src_ref=outs[w].at[slot], dst_ref=outs[w].at[slot], send_sem=send_sems.at[w], recv_sem=recv_sems.at[w],
                device_id=(x, y, 1 - c), device_id_type=MESH)

        for w in range(n):
            copy(w, c).start()
        for w in range(n):
            copy(w, 1 - c).wait_recv()
            copy(w, c).wait_send()

    return pl.pallas_call(
        body, out_shape=[jax.ShapeDtypeStruct(p.shape, F32) for p in pieces],
        in_specs=[HBM_SPEC] * n, out_specs=[HBM_SPEC] * n, input_output_aliases={w: w for w in range(n)},
        scratch_shapes=[pltpu.SemaphoreType.DMA((n,)), pltpu.SemaphoreType.DMA((n,))],
        name="rs_join_halves")(*pieces)


def _add_tile_rows(rh, c):
    for cand in (512, 256, 128, 64, 32, 16, 8):
        if rh % cand == 0 and cand * c * 4 <= 2 ** 21:
            return cand
    return rh


def _add_half(grad, recv, c_idx, name):
    _, r, cc = grad.shape
    rh = r // 2
    tr = _add_tile_rows(rh, cc)
    nb = rh // tr

    def body(c_ref, g_ref, r_ref, o_ref, ob_ref):
        del c_ref
        s = g_ref[...] + r_ref[...]
        o_ref[...] = s
        ob_ref[...] = s.astype(BF16)

    blk = pl.BlockSpec((None, tr, cc), lambda p, i, c_ref: (p, i, 0))
    grid_spec = pltpu.PrefetchScalarGridSpec(
        num_scalar_prefetch=1, grid=(4, nb),
        in_specs=[pl.BlockSpec((None, tr, cc), lambda p, i, c_ref: (p, c_ref[0] * nb + i, 0)), blk],
        out_specs=(blk, blk))
    return pl.pallas_call(
        body, out_shape=(jax.ShapeDtypeStruct((4, rh, cc), F32), jax.ShapeDtypeStruct((4, rh, cc), BF16)),
        grid_spec=grid_spec, name=name, compiler_params=_params(("parallel", "parallel")))(c_idx, grad, recv)


def _rs_add1(part, recv_a, recv_b, xy_idx, name):
    _, rh, cc = part.shape
    rq = rh // 2
    tr = _add_tile_rows(rq, cc)
    nb = rq // tr

    def body(xy_ref, pa_ref, pb_ref, ra_ref, rb_ref, ta_ref, tb_ref, tab_ref, tbb_ref):
        del xy_ref
        ta = pa_ref[...] + ra_ref[...].astype(F32)
        tb = pb_ref[...] + rb_ref[...].astype(F32)
        ta_ref[...] = ta
        tb_ref[...] = tb
        tab_ref[...] = ta.astype(BF16)
        tbb_ref[...] = tb.astype(BF16)

    blk = pl.BlockSpec((None, tr, cc), lambda i, j, xy: (i, j, 0))
    grid_spec = pltpu.PrefetchScalarGridSpec(
        num_scalar_prefetch=1, grid=(2, nb),
        in_specs=[pl.BlockSpec((None, tr, cc), lambda i, j, xy: (2 * xy[0] + i, j, 0)),
                  pl.BlockSpec((None, tr, cc), lambda i, j, xy: (2 * i + xy[1], nb + j, 0)), blk, blk],
        out_specs=(blk, blk, blk, blk))
    f32s, b16s = jax.ShapeDtypeStruct((2, rq, cc), F32), jax.ShapeDtypeStruct((2, rq, cc), BF16)
    return pl.pallas_call(body, out_shape=(f32s, f32s, b16s, b16s), grid_spec=grid_spec, name=name,
                          compiler_params=_params(("parallel", "parallel")))(xy_idx, part, part, recv_a, recv_b)


def _rs_add2(ta, tb, recv_a, recv_b, xy_idx, name):
    _, rq, cc = ta.shape
    tr = _add_tile_rows(rq, cc)
    nb = rq // tr

    def body(xy_ref, ta_ref, tb_ref, ra_ref, rb_ref, o_ref):
        del xy_ref
        s = pl.program_id(0)
        fa = ta_ref[...] + ra_ref[...].astype(F32)
        fb = tb_ref[...] + rb_ref[...].astype(F32)
        o_ref[...] = jnp.where(s == 0, fa, fb)

    rblk = pl.BlockSpec((tr, cc), lambda s, j, xy: (j, 0))
    grid_spec = pltpu.PrefetchScalarGridSpec(
        num_scalar_prefetch=1, grid=(2, nb),
        in_specs=[pl.BlockSpec((None, tr, cc), lambda s, j, xy: (xy[1], j, 0)),
                  pl.BlockSpec((None, tr, cc), lambda s, j, xy: (xy[0], j, 0)), rblk, rblk],
        out_specs=pl.BlockSpec((None, tr, cc), lambda s, j, xy: (xy[2], s * nb + j, 0)))
    return pl.pallas_call(body, out_shape=jax.ShapeDtypeStruct((2, 2 * rq, cc), F32), grid_spec=grid_spec, name=name,
                          compiler_params=_params(("parallel", "parallel")))(xy_idx, ta, tb, recv_a, recv_b)


def _allreduce_small(slab):
    r = slab.shape[0]

    def body(x_ref, o_ref, buf, send_sems, recv_sems):
        x, y, c, _ = _place()
        me = 4 * x + 2 * y + c
        buf[me] = x_ref[...]
        peers = []
        for k in range(1, 8):
            kx, ky, kc = (k >> 2) & 1, (k >> 1) & 1, k & 1
            peers.append((x + kx - 2 * x * kx, y + ky - 2 * y * ky, c + kc - 2 * c * kc))

        def copy(k, slot):
            return pltpu.make_async_remote_copy(src_ref=x_ref, dst_ref=buf.at[slot], send_sem=send_sems.at[k],
                                                recv_sem=recv_sems.at[k], device_id=peers[k], device_id_type=MESH)

        for k in range(7):
            copy(k, me).start()
        for k, (px, py, pc) in enumerate(peers):
            copy(k, 4 * px + 2 * py + pc).wait_recv()
        for k in range(7):
            copy(k, me).wait_send()
        acc = buf[0]
        for j in range(1, 8):
            acc = acc + buf[j]
        o_ref[...] = acc

    vm = pl.BlockSpec(memory_space=pltpu.VMEM)
    return pl.pallas_call(
        body, out_shape=jax.ShapeDtypeStruct((r, 128), F32), in_specs=[vm], out_specs=vm,
        scratch_shapes=[pltpu.VMEM((8, r, 128), F32), pltpu.SemaphoreType.DMA((7,)), pltpu.SemaphoreType.DMA((7,))],
        name="allreduce_small")(slab)


def _pack(arrs):
    rows = []
    for a in arrs:
        v = a.reshape(-1)
        v = jnp.pad(v, (0, (-v.shape[0]) % 128))
        rows.append(v.reshape(-1, 128))
    slab = jnp.concatenate(rows, axis=0)
    return jnp.pad(slab, ((0, (-slab.shape[0]) % 8), (0, 0)))


def _unpack(slab, shapes):
    out, r0 = [], 0
    for shp in shapes:
        size = math.prod(shp)
        nr = -(-size // 128)
        out.append(slab[r0:r0 + nr].reshape(-1)[:size].reshape(shp))
        r0 += nr
    return out


BIG = ("w_in", "w_proj_ssd", "w_proj_attn", "w_out", "w_up", "w_down")
SMALL = ("b_gate", "conv_w", "conv_b", "dt_bias_f", "dt_bias_b", "a_log_f", "a_log_b", "d_skip", "ssd_norm_w",
         "ln1_g", "ln1_b", "ln2_g", "ln2_b")
ORDER = ("w_in", "b_gate", "conv_w", "conv_b", "dt_bias_f", "dt_bias_b", "a_log_f", "a_log_b", "d_skip", "ssd_norm_w",
         "w_proj_ssd", "w_proj_attn", "w_out", "ln1_g", "ln1_b", "w_up", "w_down", "ln2_g", "ln2_b")


def kernel(x, w_in, b_gate, conv_w, conv_b, dt_bias_f, dt_bias_b, a_log_f, a_log_b, d_skip, ssd_norm_w, w_proj_ssd, w_proj_attn, w_out, ln1_g, ln1_b, w_up, w_down, ln2_g, ln2_b, loss_target, m_w_in, m_b_gate, m_conv_w, m_conv_b, m_dt_bias_f, m_dt_bias_b, m_a_log_f, m_a_log_b, m_d_skip, m_ssd_norm_w, m_w_proj_ssd, m_w_proj_attn, m_w_out, m_ln1_g, m_ln1_b, m_w_up, m_w_down, m_ln2_g, m_ln2_b, v_w_in, v_b_gate, v_conv_w, v_conv_b, v_dt_bias_f, v_dt_bias_b, v_a_log_f, v_a_log_b, v_d_skip, v_ssd_norm_w, v_w_proj_ssd, v_w_proj_attn, v_w_out, v_ln1_g, v_ln1_b, v_w_up, v_w_down, v_ln2_g, v_ln2_b):
    w = dict(w_in=w_in, b_gate=b_gate, conv_w=conv_w, conv_b=conv_b, dt_bias_f=dt_bias_f, dt_bias_b=dt_bias_b,
             a_log_f=a_log_f, a_log_b=a_log_b, d_skip=d_skip, ssd_norm_w=ssd_norm_w, w_proj_ssd=w_proj_ssd,
             w_proj_attn=w_proj_attn, w_out=w_out, ln1_g=ln1_g, ln1_b=ln1_b, w_up=w_up, w_down=w_down, ln2_g=ln2_g, ln2_b=ln2_b)
    m = dict(w_in=m_w_in, b_gate=m_b_gate, conv_w=m_conv_w, conv_b=m_conv_b, dt_bias_f=m_dt_bias_f, dt_bias_b=m_dt_bias_b,
             a_log_f=m_a_log_f, a_log_b=m_a_log_b, d_skip=m_d_skip, ssd_norm_w=m_ssd_norm_w, w_proj_ssd=m_w_proj_ssd,
             w_proj_attn=m_w_proj_attn, w_out=m_w_out, ln1_g=m_ln1_g, ln1_b=m_ln1_b, w_up=m_w_up, w_down=m_w_down,
             ln2_g=m_ln2_g, ln2_b=m_ln2_b)
    v = dict(w_in=v_w_in, b_gate=v_b_gate, conv_w=v_conv_w, conv_b=v_conv_b, dt_bias_f=v_dt_bias_f, dt_bias_b=v_dt_bias_b,
             a_log_f=v_a_log_f, a_log_b=v_a_log_b, d_skip=v_d_skip, ssd_norm_w=v_ssd_norm_w, w_proj_ssd=v_w_proj_ssd,
             w_proj_attn=v_w_proj_attn, w_out=v_w_out, ln1_g=v_ln1_g, ln1_b=v_ln1_b, w_up=v_w_up, w_down=v_w_down,
             ln2_g=v_ln2_g, ln2_b=v_ln2_b)
    xi, yi, ci = lax.axis_index("x"), lax.axis_index("y"), lax.axis_index("c")
    shard = 2 * xi + yi

    (g_in,) = _run_side(_gather_side([w["w_in"].astype(BF16)]), "allgather_w_in")
    wts = {"w_in_p": _perm_from_shards(g_in), "pending": [w[n].astype(BF16) for n in EARLY]}

    cw_slab = jnp.zeros((KCONV, 4, CONVD // 4), F32)
    cw_slab = lax.dynamic_update_slice(cw_slab, conv_w[:, None, :] * 0.5, (0, shard, 0))
    conv_w_all = _unpack(_allreduce_small(_pack([cw_slab])), [(KCONV, CONVD)])[0]

    sm = {n: w[n] for n in SMALL}
    sm["conv_w"] = conv_w_all
    c_idx = jnp.reshape(ci, (1,)).astype(jnp.int32)
    xy_idx = jnp.stack([xi, yi, ci]).astype(jnp.int32)
    dx, big, small, pieces = _local_grads(x[0], loss_target[0], wts, sm, rs_idx=(c_idx, xy_idx))

    names = list(SMALL) + ["loss"]
    shapes = [small[n].shape for n in names]
    red = dict(zip(names, _unpack(_allreduce_small(_pack([small[n] for n in names])), shapes)))
    loss = red["loss"].reshape(())
    gsm = {n: red[n] for n in SMALL}
    conv_w_grad_shard = lax.dynamic_slice_in_dim(gsm["conv_w"].reshape(KCONV, 4, CONVD // 4), shard, 1, axis=1)
    gsm["conv_w"] = conv_w_grad_shard.reshape(KCONV, CONVD // 4)

    joined = _join_halves([pieces[n] for n in BIG])
    gbig = {n: j.reshape(w[n].shape) for n, j in zip(BIG, joined)}

    grads, deltas, new_m, new_v = {}, {}, {}, {}
    for n in BIG:
        grads[n] = gbig[n]
        if n == "w_in":
            gt = gbig[n].T
            dlt, nmt, nvt = _adamw(w[n].T, gt, m[n].T, v[n].T, f"adamw_{n}")
            grads[n], deltas[n], new_m[n], new_v[n] = gt.T, dlt.T, nmt.T, nvt.T
            continue
        deltas[n], new_m[n], new_v[n] = _adamw(w[n], gbig[n], m[n], v[n], f"adamw_{n}")
    sshapes = [w[n].shape for n in SMALL]
    d_s, m_s, v_s = _adamw(_pack([w[n] for n in SMALL]), _pack([gsm[n] for n in SMALL]),
                           _pack([m[n] for n in SMALL]), _pack([v[n] for n in SMALL]), "adamw_small")
    for n, dd, mm, vv in zip(SMALL, _unpack(d_s, sshapes), _unpack(m_s, sshapes), _unpack(v_s, sshapes)):
        grads[n], deltas[n], new_m[n], new_v[n] = gsm[n], dd, mm, vv

    return (loss, dx[None], *[grads[n] for n in ORDER], *[deltas[n] for n in ORDER],
            *[new_m[n] for n in ORDER], *[new_v[n] for n in ORDER])
```

```python
import functools
import math
from typing import Callable, NamedTuple

import jax
import numpy as np
import jax.numpy as jnp
from jax import lax
from jax.experimental import pallas as pl
from jax.experimental.pallas import tpu as pltpu

F32, BF16 = jnp.float32, jnp.bfloat16
MESH = pl.DeviceIdType.MESH

D = 1024
DI = 2048
NH = 32
HP = 64
NG = 4
NS = 128
Q = 128
CONVD = 3072
KCONV = 5
DFF = 4096
AH = 64
ATT_HALF = 64
DILATIONS = (1, 4, 16)
IN_COLS = 9536
OZ, OGATE, OXBC, OKV, OQ, ODT, UW = 0, 2048, 4096, 7168, 8704, 9472, 9728
ALPHA = 2.0 ** 0.25
NORM_EPS = 1e-5
ADAM_LR, ADAM_B1, ADAM_B2, ADAM_EPS, ADAM_WD, ADAM_STEP = 0.001, 0.9, 0.999, 1e-8, 0.01, 10
VMEM_LIMIT = 56 * 2 ** 20
NEG = -1e30


def _params(sem):
    return pltpu.CompilerParams(dimension_semantics=sem, vmem_limit_bytes=VMEM_LIMIT)


def _sigmoid(x):
    return 1.0 / (1.0 + jnp.exp(-x))


def _softplus(x):
    e = jnp.exp(-jnp.abs(x))
    small = e * (1.0 - e * (0.5 - e * (1.0 / 3.0)))
    return jnp.maximum(x, 0.0) + jnp.where(e < 0.01, small, jnp.log(1.0 + e))


def _split3(a):
    hi = a.astype(BF16)
    r = a - hi.astype(F32)
    mid = r.astype(BF16)
    lo = (r - mid.astype(F32)).astype(BF16)
    return hi, mid, lo


def _dot01(a, m01):
    hi, mid, lo = _split3(a)
    d = lambda p: jnp.dot(p, m01, preferred_element_type=F32)
    return d(hi) + d(mid) + d(lo)


def _dot01_l(m01, a):
    hi, mid, lo = _split3(a)
    d = lambda p: jnp.dot(m01, p, preferred_element_type=F32)
    return d(hi) + d(mid) + d(lo)


def _dot_nt(a, b):
    return lax.dot_general(a, b, (((1,), (1,)), ((), ())), preferred_element_type=F32)


def _iota(shape, dim):
    return lax.broadcasted_iota(jnp.int32, shape, dim)


def _mm_tn(a, b, *, tka, tn, tt, name, out_shards=None):
    t, ka = a.shape
    n = b.shape[1]
    if out_shards:
        assert tn == n // out_shards
        out_shape = jax.ShapeDtypeStruct((out_shards, ka, tn), F32)
        o_spec = pl.BlockSpec((None, tka, tn), lambda i, j, s: (j, i, 0))
    else:
        out_shape = jax.ShapeDtypeStruct((ka, n), F32)
        o_spec = pl.BlockSpec((tka, tn), lambda i, j, s: (i, j))

    def body(a_ref, b_ref, o_ref):
        s = pl.program_id(2)
        part = lax.dot_general(a_ref[...].astype(BF16), b_ref[...].astype(BF16), (((0,), (0,)), ((), ())),
                               preferred_element_type=F32)

        @pl.when(s == 0)
        def _():
            o_ref[...] = part

        @pl.when(s > 0)
        def _():
            o_ref[...] += part

    return pl.pallas_call(
        body, out_shape=out_shape, grid=(ka // tka, n // tn, t // tt),
        in_specs=[pl.BlockSpec((tt, tka), lambda i, j, s: (s, i)), pl.BlockSpec((tt, tn), lambda i, j, s: (s, j))],
        out_specs=o_spec, name=name, compiler_params=_params(("parallel", "parallel", "arbitrary")))(a, b)


def _d_x(du, w_in_p, dpre1, side=None):
    t = du.shape[0]
    tm, tc = 1024, 2432
    nc = UW // tc

    def body(a_ref, b_ref, add_ref, o_ref):
        c = pl.program_id(0) % nc
        part = _dot_nt(a_ref[...], b_ref[...])

        @pl.when(c == 0)
        def _():
            o_ref[...] = part + ALPHA * add_ref[...]

        @pl.when(c > 0)
        def _():
            o_ref[...] += part

    outs, side_outs = _host_call(
        body, side, (t // tm) * nc, out_shape=(jax.ShapeDtypeStruct((t, D), F32),),
        in_specs=[pl.BlockSpec((tm, tc), lambda s: (s // nc, s % nc)), pl.BlockSpec((D, tc), lambda s: (0, s % nc)),
                  pl.BlockSpec((tm, D), lambda s: (s // nc, 0))],
        out_specs=(pl.BlockSpec((tm, D), lambda s: (s // nc, 0)),),
        scratch_shapes=[], args=(du, w_in_p, dpre1), aliases={}, name="d_x", sem=("arbitrary",))
    return outs[0], side_outs


def _in_proj(xb, w_in_p, side=None):
    t, k = xb.shape
    tm, tn = 1024, 2432
    nm, nn = t // tm, UW // tn

    def body(a_ref, b_ref, o_ref):
        o_ref[...] = jnp.dot(a_ref[...], b_ref[...], preferred_element_type=F32)

    outs, side_outs = _host_call(
        body, side, nm * nn, out_shape=(jax.ShapeDtypeStruct((t, UW), F32),),
        in_specs=[pl.BlockSpec((tm, k), lambda s: (s % nm, 0)), pl.BlockSpec((k, tn), lambda s: (0, s // nm))],
        out_specs=(pl.BlockSpec((tm, tn), lambda s: (s % nm, s // nm)),),
        scratch_shapes=[], args=(xb, w_in_p), aliases={}, name="in_proj", sem=("arbitrary",))
    return outs[0], side_outs


CONV_TM = 512
CONV_TC = 1024
CONV_RC = 64
CONV_CC = 256


def _halo_specs(t, tm, tc, col0):
    nb8 = t // 8
    r8 = tm // 8
    return [
        pl.BlockSpec((8, tc), lambda i, j: (jnp.maximum(i * r8 - 1, 0), col0 + j)),
        pl.BlockSpec((tm, tc), lambda i, j: (i, col0 + j)),
        pl.BlockSpec((8, tc), lambda i, j: (jnp.minimum((i + 1) * r8, nb8 - 1), col0 + j)),
    ]


def _fill_ext(ext, prev_ref, cur_ref, next_ref, tm, i, last):
    ext[0:8, :] = jnp.where(i > 0, prev_ref[...], 0.0)
    ext[8:8 + tm, :] = cur_ref[...]
    ext[8 + tm:16 + tm, :] = jnp.where(i < last, next_ref[...], 0.0)


def _conv_fwd(u, conv_w, conv_b):
    t = u.shape[0]
    tm, tc = CONV_TM, CONV_TC

    def body(prev_ref, cur_ref, next_ref, w_ref, b_ref, o_ref, ext):
        _fill_ext(ext, prev_ref, cur_ref, next_ref, tm, pl.program_id(0), t // tm - 1)
        for c0 in range(0, tc, CONV_CC):
            cs = slice(c0, c0 + CONV_CC)
            w = w_ref[:, cs]
            for r0 in range(0, tm, CONV_RC):
                acc = jnp.broadcast_to(b_ref[:, cs], (CONV_RC, CONV_CC))
                for k in range(KCONV):
                    acc = acc + w[k:k + 1, :] * ext[pl.ds(r0 + 6 + k, CONV_RC), cs]
                o_ref[r0:r0 + CONV_RC, cs] = acc * _sigmoid(acc)

    return pl.pallas_call(
        body, out_shape=jax.ShapeDtypeStruct((t, CONVD), F32), grid=(t // tm, CONVD // tc),
        in_specs=_halo_specs(t, tm, tc, OXBC // tc) + [
            pl.BlockSpec((KCONV, tc), lambda i, j: (0, j)), pl.BlockSpec((1, tc), lambda i, j: (0, j))],
        out_specs=pl.BlockSpec((tm, tc), lambda i, j: (i, j)),
        scratch_shapes=[pltpu.VMEM((tm + 16, tc), F32)],
        name="conv_fwd", compiler_params=_params(("parallel", "parallel")))(u, u, u, conv_w, conv_b)


def _conv_dpre(u, dxs, dy, dbc, dsk_row, conv_w, conv_b):
    t = u.shape[0]
    tm, tc = CONV_TM, CONV_TC
    r8 = tm // 8
    nb8 = t // 8
    c0 = OXBC // tc

    def body(uprev, ucur, unext, f_ref, y_ref, cf_ref, dsk_ref, w_ref, bias_ref, dpre_ref, dw_ref, db_ref, ext):
        j = pl.program_id(0)
        i = pl.program_id(1)
        _fill_ext(ext, uprev, ucur, unext, tm, i, t // tm - 1)
        is_xs = j < 2
        dw_cols, db_cols = [], []
        for c0 in range(0, tc, CONV_CC):
            cs = slice(c0, c0 + CONV_CC)
            w = w_ref[:, cs]
            dsk = dsk_ref[:, cs]
            dw_acc = [jnp.zeros((1, CONV_CC), F32) for _ in range(KCONV)]
            db_acc = jnp.zeros((1, CONV_CC), F32)
            for r0 in range(0, tm, CONV_RC):
                rs = slice(r0, r0 + CONV_RC)
                taps = [ext[pl.ds(r0 + 6 + k, CONV_RC), cs] for k in range(KCONV)]
                pre = jnp.broadcast_to(bias_ref[:, cs], (CONV_RC, CONV_CC))
                for k in range(KCONV):
                    pre = pre + w[k:k + 1, :] * taps[k]
                s = _sigmoid(pre)
                up = jnp.where(is_xs, f_ref[rs, cs] + dsk * y_ref[rs, cs], cf_ref[rs, cs])
                dpre = up * (s * (1.0 + pre * (1.0 - s)))
                dpre_ref[rs, cs] = dpre
                for k in range(KCONV):
                    dw_acc[k] = dw_acc[k] + jnp.sum(dpre * taps[k], axis=0, keepdims=True)
                db_acc = db_acc + jnp.sum(dpre, axis=0, keepdims=True)
            dw_cols.append(jnp.concatenate(dw_acc + [jnp.zeros((8 - KCONV, CONV_CC), F32)], axis=0))
            db_cols.append(jnp.broadcast_to(db_acc, (8, CONV_CC)))
        dw_part = jnp.concatenate(dw_cols, axis=1)
        db_part = jnp.concatenate(db_cols, axis=1)

        @pl.when(i == 0)
        def _():
            dw_ref[...] = dw_part
            db_ref[...] = db_part

        @pl.when(i > 0)
        def _():
            dw_ref[...] += dw_part
            db_ref[...] += db_part

    xs_spec = pl.BlockSpec((tm, tc), lambda j, i: (jnp.where(j < 2, i, 0), jnp.minimum(j, 1)))
    bc_spec = pl.BlockSpec((tm, tc), lambda j, i: (jnp.where(j == 2, i, 0), 0))
    in_specs = [
        pl.BlockSpec((8, tc), lambda j, i: (jnp.maximum(i * r8 - 1, 0), c0 + j)),
        pl.BlockSpec((tm, tc), lambda j, i: (i, c0 + j)),
        pl.BlockSpec((8, tc), lambda j, i: (jnp.minimum((i + 1) * r8, nb8 - 1), c0 + j)),
        xs_spec, xs_spec, bc_spec,
        pl.BlockSpec((1, tc), lambda j, i: (0, jnp.minimum(j, 1))),
        pl.BlockSpec((KCONV, tc), lambda j, i: (0, j)), pl.BlockSpec((1, tc), lambda j, i: (0, j)),
    ]
    return pl.pallas_call(
        body,
        out_shape=(jax.ShapeDtypeStruct((t, CONVD), F32), jax.ShapeDtypeStruct((8, CONVD), F32),
                   jax.ShapeDtypeStruct((8, CONVD), F32)),
        grid=(CONVD // tc, t // tm), in_specs=in_specs,
        out_specs=(pl.BlockSpec((tm, tc), lambda j, i: (i, j)),
                   pl.BlockSpec((8, tc), lambda j, i: (0, j)), pl.BlockSpec((8, tc), lambda j, i: (0, j))),
        scratch_shapes=[pltpu.VMEM((tm + 16, tc), F32)],
        name="conv_dpre", compiler_params=_params(("parallel", "arbitrary")))(
            u, u, u, dxs, dy, dbc, dsk_row, conv_w, conv_b)


def _conv_dx(du, dpre, conv_w):
    t = dpre.shape[0]
    tm, tc = CONV_TM, CONV_TC
    r8 = tm // 8
    nb8 = t // 8

    def body(prev_ref, cur_ref, next_ref, w_ref, du_in, du_out, ext):
        del du_in
        _fill_ext(ext, prev_ref, cur_ref, next_ref, tm, pl.program_id(1), t // tm - 1)
        for c0 in range(0, tc, CONV_CC):
            cs = slice(c0, c0 + CONV_CC)
            w = w_ref[:, cs]
            for r0 in range(0, tm, CONV_RC):
                acc = jnp.zeros((CONV_RC, CONV_CC), F32)
                for k in range(KCONV):
                    acc = acc + w[k:k + 1, :] * ext[pl.ds(r0 + 10 - k, CONV_RC), cs]
                du_out[r0:r0 + CONV_RC, cs] = acc.astype(du_out.dtype)

    in_specs = [
        pl.BlockSpec((8, tc), lambda j, i: (jnp.maximum(i * r8 - 1, 0), j)),
        pl.BlockSpec((tm, tc), lambda j, i: (i, j)),
        pl.BlockSpec((8, tc), lambda j, i: (jnp.minimum((i + 1) * r8, nb8 - 1), j)),
        pl.BlockSpec((KCONV, tc), lambda j, i: (0, j)),
        pl.BlockSpec(memory_space=pl.ANY),
    ]
    return pl.pallas_call(
        body, out_shape=jax.ShapeDtypeStruct(du.shape, du.dtype), grid=(CONVD // tc, t // tm), in_specs=in_specs,
        out_specs=pl.BlockSpec((tm, tc), lambda j, i: (i, OXBC // tc + j)),
        scratch_shapes=[pltpu.VMEM((tm + 16, tc), F32)], input_output_aliases={4: 0},
        name="conv_dx", compiler_params=_params(("parallel", "parallel")))(dpre, dpre, dpre, conv_w, du)


def _ssd_common(dtr_ref, par_ref, rev):
    raw = dtr_ref[...]
    lane = _iota((1, 128), 1)
    mine = (lane >= 32 * rev) & (lane < 32 * rev + 32)
    bias = par_ref[0:1, :]
    arow = jnp.where(mine, -jnp.exp(par_ref[1:2, :]), 0.0)
    dt = _softplus(raw + bias)
    a = dt * arow
    ri = _iota((Q, Q), 0)
    ci = _iota((Q, Q), 1)
    tri = (ci >= ri) if rev else (ci <= ri)
    trit = (ci <= ri) if rev else (ci >= ri)
    cs = _dot01_l(tri.astype(BF16), a)
    return raw, bias, arow, mine, dt, cs, tri, trit


def _expand_mat(rev):
    r = np.arange(128)[:, None]
    c = np.arange(DI)[None, :]
    return jnp.asarray(r == (c // HP) + 32 * rev, BF16)


def _sum_mat(rev):
    r = np.arange(DI)[:, None]
    c = np.arange(128)[None, :]
    return jnp.asarray(c == (r // HP) + 32 * rev, BF16)


def _ssd_fwd(xbc, u, par, y_add=None, *, rev):
    t = xbc.shape[0]
    nc = t // Q
    end = 0 if rev else Q - 1
    cmap = (lambda c: nc - 1 - c) if rev else (lambda c: c)

    def body(xbc_ref, dtr_ref, par_ref, ex_ref, *rest):
        yadd_ref = rest[0] if y_add is not None else None
        y_ref, st_ref, h_scr = rest[-3:]
        step = pl.program_id(0)

        @pl.when(step == 0)
        def _():
            h_scr[...] = jnp.zeros((NS, DI), F32)

        raw, bias, arow, mine, dt, cs, tri, trit = _ssd_common(dtr_ref, par_ref, rev)
        cst = cs.T
        dtt = dt.T
        tot_col = cst[:, end:end + 1]
        wt = dtt * jnp.exp(tot_col - cst)
        ecs_all = jnp.exp(cs)
        gam = jnp.exp(cs[end:end + 1, :])
        gam_x = _dot01(jnp.broadcast_to(gam, (8, 128)), ex_ref[...])[0:1, :]
        lane = _iota((Q, 128), 1)
        sel = lane < HP
        st_ref[...] = h_scr[...]
        for g in range(NG):
            bg = xbc_ref[:, DI + NS * g:DI + NS * (g + 1)]
            cg = xbc_ref[:, DI + NG * NS + NS * g:DI + NG * NS + NS * (g + 1)]
            cb = _dot_nt(cg.astype(BF16), bg.astype(BF16))
            bt = bg.T
            for k in range(4):
                lo = 512 * g + 128 * k
                xp = xbc_ref[:, lo:lo + 128].astype(BF16)
                hp = h_scr[:, lo:lo + 128]
                rhs = jnp.concatenate([xp, hp.astype(BF16)], axis=0)
                lhs, bts = [], []
                for j in range(2):
                    hc = 8 * g + 2 * k + j + 32 * rev
                    csc = jnp.broadcast_to(cs[:, hc:hc + 1], (Q, Q))
                    lm = jnp.exp(jnp.where(tri, csc - cst[hc:hc + 1, :], NEG)) * dtt[hc:hc + 1, :]
                    mh = (cb * lm).astype(BF16)
                    ec = (jnp.broadcast_to(ecs_all[:, hc:hc + 1], (Q, NS)) * cg).astype(BF16)
                    lhs.append(jnp.concatenate([mh, ec], axis=1))
                    bts.append((bt * wt[hc:hc + 1, :]).astype(BF16))
                ys = jnp.dot(jnp.concatenate(lhs, axis=0), rhs, preferred_element_type=F32)
                ss = jnp.dot(jnp.concatenate(bts, axis=0), xp, preferred_element_type=F32)
                yp = jnp.where(sel, ys[0:Q], ys[Q:2 * Q])
                y_ref[:, lo:lo + 128] = yp if yadd_ref is None else yp + yadd_ref[:, lo:lo + 128]
                h_scr[:, lo:lo + 128] = gam_x[:, lo:lo + 128] * hp + jnp.where(sel, ss[0:NS], ss[NS:2 * NS])

    return pl.pallas_call(
        body,
        out_shape=(jax.ShapeDtypeStruct((t, DI), F32), jax.ShapeDtypeStruct((nc, NS, DI), F32)),
        grid=(nc,),
        in_specs=[pl.BlockSpec((Q, CONVD), lambda c: (cmap(c), 0)),
                  pl.BlockSpec((Q, 128), lambda c: (cmap(c), ODT // 128)),
                  pl.BlockSpec((8, 128), lambda c: (0, 0)),
                  pl.BlockSpec((128, DI), lambda c: (0, 0))]
        + ([pl.BlockSpec((Q, DI), lambda c: (cmap(c), 0))] if y_add is not None else []),
        out_specs=(pl.BlockSpec((Q, DI), lambda c: (cmap(c), 0)),
                   pl.BlockSpec((None, NS, DI), lambda c: (cmap(c), 0, 0))),
        scratch_shapes=[pltpu.VMEM((NS, DI), F32)],
        name="ssd_fwd_rev" if rev else "ssd_fwd", compiler_params=_params(("arbitrary",)))(
            xbc, u, par, _expand_mat(rev), *([y_add] if y_add is not None else []))


def _ssd_bwd(xbc, u, par, dy, st, *, rev, add=None, side=None):
    t = xbc.shape[0]
    nc = t // Q
    end = 0 if rev else Q - 1
    cmap = (lambda c: c) if rev else (lambda c: nc - 1 - c)

    def body(xbc_ref, dtr_ref, par_ref, dy_ref, hin_ref, ex_ref, sm_ref, *rest):
        addx_ref, addbc_ref, addt_ref = rest[:3] if add is not None else (None, None, None)
        dxs_ref, dbc_ref, ddt_ref, acc_ref, dh_scr = rest[-5:]
        step = pl.program_id(0)

        @pl.when(step == 0)
        def _():
            dh_scr[...] = jnp.zeros((NS, DI), F32)

        raw, bias, arow, mine, dt, cs, tri, trit = _ssd_common(dtr_ref, par_ref, rev)
        ri = _iota((Q, Q), 0)
        ci = _iota((Q, Q), 1)
        stri = ((ri > ci) if rev else (ri < ci)).astype(BF16)
        strit = ((ci > ri) if rev else (ci < ri)).astype(BF16)
        cst = cs.T
        dtt = dt.T
        et = jnp.exp(cst)
        ecs_all = jnp.exp(cs)
        ws_all = jnp.exp(cs[end:end + 1, :] - cs)
        expand = ex_ref[...]
        summat = sm_ref[...]
        gam = jnp.exp(cs[end:end + 1, :])
        gam_x = _dot01(jnp.broadcast_to(gam, (8, 128)), expand)[0:1, :]
        dt_hi, dt_mid, _ = _split3(dt)
        dtx = (jnp.dot(dt_hi, expand, preferred_element_type=F32)
               + jnp.dot(dt_mid, expand, preferred_element_type=F32))
        lane = _iota((Q, 128), 1)
        sel = lane < HP
        dho = dh_scr[...]
        t3 = jnp.sum(dho * hin_ref[...], axis=0, keepdims=True) * gam_x
        dxs_cols, dxs2_cols, yoff_cols, a1_rows = [], [], [], []
        for g in range(NG):
            bg = xbc_ref[:, DI + NS * g:DI + NS * (g + 1)]
            cg = xbc_ref[:, DI + NG * NS + NS * g:DI + NG * NS + NS * (g + 1)]
            bb = bg.astype(BF16)
            cbf = cg.astype(BF16)
            cb = _dot_nt(cbf, bb)
            cbt = _dot_nt(bb, cbf)
            ct = cg.T
            bdh = jnp.dot(bb, dho[:, 512 * g:512 * (g + 1)].astype(BF16), preferred_element_type=F32)
            dcb = jnp.zeros((Q, Q), F32)
            dcg = jnp.zeros((Q, NS), F32)
            dbg = jnp.zeros((Q, NS), F32)
            for k in range(4):
                lo = 512 * g + 128 * k
                xpf = xbc_ref[:, lo:lo + 128]
                xp = xpf.astype(BF16)
                dyp = dy_ref[:, lo:lo + 128]
                dypb = dyp.astype(BF16)
                hinp = hin_ref[:, lo:lo + 128].astype(BF16)
                dhp = dho[:, lo:lo + 128]
                es, ws, lmds, mts, ctes, dyms, ecbs = [], [], [], [], [], [], []
                for j in range(2):
                    hc = 8 * g + 2 * k + j + 32 * rev
                    csc = jnp.broadcast_to(cs[:, hc:hc + 1], (Q, Q))
                    csr = cst[hc:hc + 1, :]
                    lmds.append(jnp.exp(jnp.where(tri, csc - csr, NEG)) * dtt[hc:hc + 1, :])
                    lmb = jnp.exp(jnp.where(trit, csr - csc, NEG))
                    mts.append((cbt * lmb).astype(BF16))
                    dyms.append(jnp.where(sel if j == 0 else ~sel, dyp, 0.0).astype(BF16))
                    ecs = jnp.broadcast_to(ecs_all[:, hc:hc + 1], (Q, NS))
                    es.append(ecs)
                    ws.append(jnp.broadcast_to(ws_all[:, hc:hc + 1], (Q, NS)))
                    ecbs.append((ecs * cg).astype(BF16))
                    ctes.append((ct * et[hc:hc + 1, :]).astype(BF16))
                by_dy = jnp.dot(jnp.concatenate(mts + ctes, axis=0), dypb, preferred_element_type=F32)
                dmm = _dot_nt(jnp.concatenate(dyms, axis=0), xp)
                dm0, dm1 = dmm[0:Q] * lmds[0], dmm[Q:2 * Q] * lmds[1]
                dcb = dcb + dm0 + dm1
                rr = jnp.dot(jnp.concatenate([dm0 * cb, dm1 * cb], axis=0).astype(BF16), stri, preferred_element_type=F32)
                a1_rows.append(jnp.sum(jnp.where(tri, rr[0:Q], 0.0), axis=0, keepdims=True))
                a1_rows.append(jnp.sum(jnp.where(tri, rr[Q:2 * Q], 0.0), axis=0, keepdims=True))
                yo = jnp.dot(jnp.concatenate(ecbs, axis=0), hinp, preferred_element_type=F32)
                e_p = jnp.where(sel, es[0], es[1])
                w_p = jnp.where(sel, ws[0], ws[1])
                d2 = w_p * bdh[:, 128 * k:128 * (k + 1)]
                dxs2_cols.append(d2)
                dxs_cols.append(jnp.where(sel, by_dy[0:Q], by_dy[Q:2 * Q]) + d2)
                yoff_cols.append(jnp.where(sel, yo[0:Q], yo[Q:2 * Q]))
                dcg = dcg + _dot_nt((e_p * dyp).astype(BF16), hinp)
                dbg = dbg + _dot_nt((w_p * dtx[:, lo:lo + 128] * xpf).astype(BF16), dhp.astype(BF16))
                dh_scr[:, lo:lo + 128] = (gam_x[:, lo:lo + 128] * dhp
                                          + jnp.where(sel, by_dy[2 * Q:3 * Q], by_dy[3 * Q:4 * Q]))
            dcg = dcg + jnp.dot(dcb.astype(BF16), bb, preferred_element_type=F32)
            dbg = dbg + jnp.dot(dcb.T.astype(BF16), cbf, preferred_element_type=F32)
            lo_b, lo_c = NS * g, NG * NS + NS * g
            if addbc_ref is not None:
                dbg = dbg + addbc_ref[:, lo_b:lo_b + NS]
                dcg = dcg + addbc_ref[:, lo_c:lo_c + NS]
            dbc_ref[:, lo_b:lo_b + NS] = dbg
            dbc_ref[:, lo_c:lo_c + NS] = dcg
        dxs = jnp.concatenate(dxs_cols, axis=1)
        dxs_ref[...] = dxs * dtx if addx_ref is None else dxs * dtx + addx_ref[...]
        xs = xbc_ref[:, 0:DI]
        stacked = jnp.concatenate([xs * dxs, xs * jnp.concatenate(dxs2_cols, axis=1),
                                   dy_ref[...] * jnp.concatenate(yoff_cols, axis=1),
                                   jnp.broadcast_to(t3, (8, DI))], axis=0).astype(BF16)
        sums = jnp.dot(stacked, summat, preferred_element_type=F32)
        rx, rx2, ryo, c0 = sums[0:Q], sums[Q:2 * Q], sums[2 * Q:3 * Q], sums[3 * Q:3 * Q + 1]
        zero32 = jnp.zeros((32, Q), F32)
        a1t = jnp.concatenate(([zero32] if rev else []) + a1_rows + [zero32] * (2 if rev else 3), axis=0)
        da = (a1t.T + jnp.dot(trit.astype(BF16), ryo.astype(BF16), preferred_element_type=F32)
              + jnp.dot(strit, (dt * rx2).astype(BF16), preferred_element_type=F32) + jnp.where(mine, c0, 0.0))
        ddt = rx + da * arow
        ddtr = ddt * _sigmoid(raw + bias)
        ddt_ref[...] = ddtr if addt_ref is None else ddtr + addt_ref[...]
        part = jnp.concatenate([jnp.sum(ddtr, axis=0, keepdims=True),
                                jnp.sum(da * dt, axis=0, keepdims=True) * arow,
                                jnp.zeros((6, 128), F32)], axis=0)

        @pl.when(step == 0)
        def _():
            acc_ref[...] = part

        @pl.when(step > 0)
        def _():
            acc_ref[...] += part

    outs, side_outs = _host_call(
        body, side, nc,
        out_shape=(jax.ShapeDtypeStruct((t, DI), F32), jax.ShapeDtypeStruct((t, 2 * NG * NS), F32),
                   jax.ShapeDtypeStruct((t, 128), F32), jax.ShapeDtypeStruct((8, 128), F32)),
        in_specs=[pl.BlockSpec((Q, CONVD), lambda c: (cmap(c), 0)),
                  pl.BlockSpec((Q, 128), lambda c: (cmap(c), ODT // 128)),
                  pl.BlockSpec((8, 128), lambda c: (0, 0)),
                  pl.BlockSpec((Q, DI), lambda c: (cmap(c), 0)),
                  pl.BlockSpec((None, NS, DI), lambda c: (cmap(c), 0, 0)),
                  pl.BlockSpec((128, DI), lambda c: (0, 0)), pl.BlockSpec((DI, 128), lambda c: (0, 0))]
        + ([pl.BlockSpec((Q, DI), lambda c: (cmap(c), 0)), pl.BlockSpec((Q, 2 * NG * NS), lambda c: (cmap(c), 0)),
            pl.BlockSpec((Q, 128), lambda c: (cmap(c), 0))] if add is not None else []),
        out_specs=(pl.BlockSpec((Q, DI), lambda c: (cmap(c), 0)),
                   pl.BlockSpec((Q, 2 * NG * NS), lambda c: (cmap(c), 0)),
                   pl.BlockSpec((Q, 128), lambda c: (cmap(c), 0)),
                   pl.BlockSpec((8, 128), lambda c: (0, 0))),
        scratch_shapes=[pltpu.VMEM((NS, DI), F32)],
        args=(xbc, u, par, dy, st, _expand_mat(rev), _sum_mat(rev)) + (tuple(add) if add is not None else ()), aliases={},
        name="ssd_bwd_rev" if rev else "ssd_bwd", sem=("arbitrary",))
    return (*outs, side_outs)


GN_TM = 256
GN_GROUP = DI // NG


def _gn_forward_vals(y0, xs, z, dsk):
    y = y0 + dsk * xs
    sz = _sigmoid(z)
    gate = z * sz
    y2 = y * gate
    parts, rs = [], []
    for g in range(NG):
        seg = y2[:, GN_GROUP * g:GN_GROUP * (g + 1)]
        r = lax.rsqrt(jnp.mean(seg * seg, axis=1, keepdims=True) + NORM_EPS)
        rs.append(r)
        parts.append(seg * r)
    yn = jnp.concatenate(parts, axis=1)
    return y, sz, gate, yn, rs


def _gatenorm_fwd(y_fb, xbc, u, dsk_row, nw_row, w_ps):
    t = y_fb.shape[0]
    tm = GN_TM

    def body(y_ref, xs_ref, z_ref, dsk_ref, nw_ref, w_ref, o_ref, ys_ref):
        _, _, _, yn, _ = _gn_forward_vals(y_ref[...], xs_ref[...], z_ref[...], dsk_ref[...])
        s_out = (yn * nw_ref[...]).astype(BF16)
        o_ref[...] = s_out
        ys_ref[...] = jnp.dot(s_out, w_ref[...], preferred_element_type=F32)

    blk = pl.BlockSpec((tm, DI), lambda i: (i, 0))
    row = pl.BlockSpec((1, DI), lambda i: (0, 0))
    return pl.pallas_call(
        body, out_shape=(jax.ShapeDtypeStruct((t, DI), BF16), jax.ShapeDtypeStruct((t, D), F32)), grid=(t // tm,),
        in_specs=[blk, blk, pl.BlockSpec((tm, DI), lambda i: (i, OZ // DI)), row, row,
                  pl.BlockSpec((DI, D), lambda i: (0, 0))],
        out_specs=(blk, pl.BlockSpec((tm, D), lambda i: (i, 0))), name="gatenorm_fwd",
        compiler_params=_params(("parallel",)))(y_fb, xbc, u, dsk_row, nw_row, w_ps)


def _gatenorm_bwd(dy_ssd, w_ps, y_fb, xbc, u, du, dsk_row, nw_row, side=None):
    t = y_fb.shape[0]
    tm = GN_TM

    def body(dys_ref, w_ref, y_ref, xs_ref, z_ref, dsk_ref, nw_ref, sm_ref, du_in, dy_ref, du_out, dnw_ref, dds_ref):
        del du_in
        i = pl.program_id(0)
        xs = xs_ref[...]
        z = z_ref[...]
        y, sz, gate, yn, rs = _gn_forward_vals(y_ref[...], xs, z, dsk_ref[...])
        ds = _dot_nt(dys_ref[...], w_ref[...])
        gsc = ds * nw_ref[...]
        parts = []
        for g in range(NG):
            sl = slice(GN_GROUP * g, GN_GROUP * (g + 1))
            m = jnp.mean(gsc[:, sl] * yn[:, sl], axis=1, keepdims=True)
            parts.append(rs[g] * (gsc[:, sl] - yn[:, sl] * m))
        dy2 = jnp.concatenate(parts, axis=1)
        dy = dy2 * gate
        dy_ref[...] = dy
        du_out[...] = (dy2 * y * (sz * (1.0 + z * (1.0 - sz)))).astype(du_out.dtype)
        dnw = jnp.broadcast_to(jnp.sum(ds * yn, axis=0, keepdims=True), (8, DI))
        drow = jnp.broadcast_to(jnp.sum(dy * xs, axis=0, keepdims=True), (8, DI))
        dds = _dot01(drow, sm_ref[...])

        @pl.when(i == 0)
        def _():
            dnw_ref[...] = dnw
            dds_ref[...] = dds

        @pl.when(i > 0)
        def _():
            dnw_ref[...] += dnw
            dds_ref[...] += dds

    blk = pl.BlockSpec((tm, DI), lambda i: (i, 0))
    row = pl.BlockSpec((1, DI), lambda i: (0, 0))
    outs, side_outs = _host_call(
        body, side, t // tm,
        out_shape=(jax.ShapeDtypeStruct((t, DI), F32), jax.ShapeDtypeStruct(du.shape, du.dtype),
                   jax.ShapeDtypeStruct((8, DI), F32), jax.ShapeDtypeStruct((8, 128), F32)),
        in_specs=[pl.BlockSpec((tm, D), lambda i: (i, 0)), pl.BlockSpec((DI, D), lambda i: (0, 0)),
                  blk, blk, pl.BlockSpec((tm, DI), lambda i: (i, OZ // DI)), row, row,
                  pl.BlockSpec((DI, 128), lambda i: (0, 0)), pl.BlockSpec(memory_space=pl.ANY)],
        out_specs=(blk, pl.BlockSpec((tm, DI), lambda i: (i, OZ // DI)),
                   pl.BlockSpec((8, DI), lambda i: (0, 0)), pl.BlockSpec((8, 128), lambda i: (0, 0))),
        scratch_shapes=[], args=(dy_ssd, w_ps, y_fb, xbc, u, dsk_row, nw_row, _sum_mat(0), du), aliases={8: 1},
        name="gatenorm_bwd", sem=("arbitrary",))
    return (*outs, side_outs)


AT_B = 128
AT_W = AT_B + 2 * ATT_HALF
AT_L = 2 * AH
SCALE = 1.0 / math.sqrt(AH)


def _slope(g, hh):
    return 2.0 ** (-8.0 * (4 * g + hh + 1) / 12.0)


def _qcol(g):
    return lambda p: OQ // AT_L + 2 * g + p


def _kcol(g):
    return lambda p: OKV // AT_L + 4 * g + 2 * p


def _vcol(g):
    return lambda p: OKV // AT_L + 4 * g + 2 * p + 1


def _pcol(p):
    return p


def _sub(d):
    return 4 if d == 1 else 1


def _win_specs(col, t, d):
    tb, hb = AT_B * d * _sub(d), ATT_HALF * d
    per = tb // hb
    nh = t // hb
    return [
        pl.BlockSpec((hb, AT_L), lambda p, i: (jnp.maximum(per * i - 1, 0), col(p))),
        pl.BlockSpec((tb, AT_L), lambda p, i: (i, col(p))),
        pl.BlockSpec((hb, AT_L), lambda p, i: (jnp.minimum(per * (i + 1), nh - 1), col(p))),
    ]


def _blk_spec(col, d):
    return pl.BlockSpec((AT_B * d * _sub(d), AT_L), lambda p, i: (i, col(p)))


def _rows(ref, r, s, d):
    return ref[pl.ds(r, AT_B, stride=d), :] if d > 1 else ref[AT_B * s:AT_B * (s + 1), :]


def _win(p_ref, c_ref, n_ref, r, s, d):
    if d > 1:
        return jnp.concatenate([p_ref[pl.ds(r, ATT_HALF, stride=d), :], c_ref[pl.ds(r, AT_B, stride=d), :],
                                n_ref[pl.ds(r, ATT_HALF, stride=d), :]], axis=0)
    if s == 0:
        return jnp.concatenate([p_ref[...], c_ref[0:AT_B + ATT_HALF, :]], axis=0)
    if s == _sub(d) - 1:
        return jnp.concatenate([c_ref[AT_B * s - ATT_HALF:AT_B * (s + 1), :], n_ref[...]], axis=0)
    return c_ref[AT_B * s - ATT_HALF:AT_B * (s + 1) + ATT_HALF, :]


def _put_rows(ref, r, s, d, val):
    if d > 1:
        ref[pl.ds(r, AT_B, stride=d), :] = val
    else:
        ref[AT_B * s:AT_B * (s + 1), :] = val


def _for_blocks(d, fn):
    if d == 1:
        for s in range(_sub(d)):
            fn(0, s)
    else:
        def step(r, c):
            fn(r, 0)
            return c
        lax.fori_loop(0, d, step, 0, unroll=4)


def _attn_bias(blk, ln, d, g, p_id):
    a = blk * AT_B + _iota((AT_B, AT_W), 0)
    b = blk * AT_B - ATT_HALF + _iota((AT_B, AT_W), 1)
    rel = jnp.abs(a - b)
    valid = (rel <= ATT_HALF) & (b >= 0) & (b < ln)
    dist = (rel * d).astype(F32)
    out = []
    for hh in range(2):
        slope = jnp.where(p_id == 0, _slope(g, hh), _slope(g, 2 + hh))
        out.append(jnp.where(valid, -slope * dist, NEG))
    return out


def _attn_fwd(u, g):
    t = u.shape[0]
    d = DILATIONS[g]
    ln = t // d

    def body(q_ref, kp, kc, kn, vp, vc, vn, o_ref, l_ref):
        p_id = pl.program_id(0)
        i = pl.program_id(1)
        lane = _iota((AT_B, AT_L), 1)
        biases = [_attn_bias(i * _sub(d) + s, ln, d, g, p_id) for s in range(_sub(d))]

        def one(r, s):
            q = _rows(q_ref, r, s, d) * SCALE
            kw = _win(kp, kc, kn, r, s, d).astype(BF16)
            vw = _win(vp, vc, vn, r, s, d).astype(BF16)
            o = jnp.zeros((AT_B, AT_L), F32)
            lse = jnp.zeros((AT_B, AT_L), F32)
            for hh in range(2):
                hm = (lane // AH) == hh
                qm = jnp.where(hm, q, 0.0).astype(BF16)
                sc = _dot_nt(qm, kw) + biases[s][hh]
                m = jnp.max(sc, axis=1, keepdims=True)
                pr = jnp.exp(sc - m)
                den = jnp.sum(pr, axis=1, keepdims=True)
                oh = jnp.dot(pr.astype(BF16), vw, preferred_element_type=F32)
                o = jnp.where(hm, oh / den, o)
                lse = jnp.where(hm, m + jnp.log(den), lse)
            _put_rows(o_ref, r, s, d, o)
            _put_rows(l_ref, r, s, d, lse)

        _for_blocks(d, one)

    oshape = jax.ShapeDtypeStruct((t, 2 * AT_L), F32)
    ospec = _blk_spec(_pcol, d)
    return pl.pallas_call(
        body, out_shape=(oshape, oshape), grid=(2, t // (AT_B * d * _sub(d))),
        in_specs=[_blk_spec(_qcol(g), d)] + _win_specs(_kcol(g), t, d) + _win_specs(_vcol(g), t, d),
        out_specs=(ospec, ospec), name=f"attn_fwd_{g}", compiler_params=_params(("parallel", "parallel")))(
            u, u, u, u, u, u, u)


def _attn_dq(u, du, do, lse, e, g):
    t = u.shape[0]
    d = DILATIONS[g]
    ln = t // d

    def body(q_ref, kp, kc, kn, vp, vc, vn, do_ref, l_ref, e_ref, du_in, dq_ref, dq_scr):
        del du_in
        p_id = pl.program_id(0)
        i = pl.program_id(1)
        lane = _iota((AT_B, AT_L), 1)
        biases = [_attn_bias(i * _sub(d) + s, ln, d, g, p_id) for s in range(_sub(d))]

        def one(r, s):
            q = _rows(q_ref, r, s, d) * SCALE
            kw = _win(kp, kc, kn, r, s, d).astype(BF16)
            vw = _win(vp, vc, vn, r, s, d).astype(BF16)
            do_ = _rows(do_ref, r, s, d)
            lv = _rows(l_ref, r, s, d)
            ev = _rows(e_ref, r, s, d)
            dq = jnp.zeros((AT_B, AT_L), F32)
            for hh in range(2):
                hm = (lane // AH) == hh
                qm = jnp.where(hm, q, 0.0).astype(BF16)
                sc = _dot_nt(qm, kw) + biases[s][hh]
                lcol = jnp.broadcast_to(lv[:, AH * hh:AH * hh + 1], (AT_B, AT_W))
                ecol = jnp.broadcast_to(ev[:, AH * hh:AH * hh + 1], (AT_B, AT_W))
                pr = jnp.exp(sc - lcol)
                dom = jnp.where(hm, do_, 0.0).astype(BF16)
                ds = pr * (_dot_nt(dom, vw) + ecol)
                dqh = jnp.dot(ds.astype(BF16), kw, preferred_element_type=F32) * SCALE
                dq = jnp.where(hm, dqh, dq)
            _put_rows(dq_scr, r, s, d, dq)

        _for_blocks(d, one)
        dq_ref[...] = dq_scr[...].astype(dq_ref.dtype)

    rspec = _blk_spec(_pcol, d)
    return pl.pallas_call(
        body, out_shape=jax.ShapeDtypeStruct(du.shape, du.dtype), grid=(2, t // (AT_B * d * _sub(d))),
        in_specs=[_blk_spec(_qcol(g), d)] + _win_specs(_kcol(g), t, d) + _win_specs(_vcol(g), t, d)
        + [rspec, rspec, rspec, pl.BlockSpec(memory_space=pl.ANY)],
        out_specs=_blk_spec(_qcol(g), d), input_output_aliases={10: 0},
        scratch_shapes=[pltpu.VMEM((AT_B * d * _sub(d), AT_L), F32)],
        name=f"attn_dq_{g}", compiler_params=_params(("parallel", "parallel")))(
            u, u, u, u, u, u, u, do, lse, e, du)


def _attn_dkv(u, du, do, lse, e, g):
    t = u.shape[0]
    d = DILATIONS[g]
    ln = t // d

    def body(k_ref, v_ref, qp, qc, qn, dp_, dc_, dn_, lp, lc, ln_, ep, ec, en, du_in, dkv_ref, dk_scr, dv_scr):
        del du_in
        p_id = pl.program_id(0)
        jb = pl.program_id(1)
        lane = _iota((AT_B, AT_L), 1)
        biases = [_attn_bias(jb * _sub(d) + s, ln, d, g, p_id) for s in range(_sub(d))]

        def one(r, s):
            k = _rows(k_ref, r, s, d) * SCALE
            v = _rows(v_ref, r, s, d)
            qw = _win(qp, qc, qn, r, s, d).astype(BF16)
            dow = _win(dp_, dc_, dn_, r, s, d).astype(BF16)
            lt = _win(lp, lc, ln_, r, s, d).T
            et = _win(ep, ec, en, r, s, d).T
            dk = jnp.zeros((AT_B, AT_L), F32)
            dv = jnp.zeros((AT_B, AT_L), F32)
            for hh in range(2):
                hm = (lane // AH) == hh
                km = jnp.where(hm, k, 0.0).astype(BF16)
                st = _dot_nt(km, qw) + biases[s][hh]
                pt = jnp.exp(st - lt[AH * hh:AH * hh + 1, :])
                dvh = jnp.dot(pt.astype(BF16), dow, preferred_element_type=F32)
                vm = jnp.where(hm, v, 0.0).astype(BF16)
                dst = pt * (_dot_nt(vm, dow) + et[AH * hh:AH * hh + 1, :])
                dkh = jnp.dot(dst.astype(BF16), qw, preferred_element_type=F32) * SCALE
                dk = jnp.where(hm, dkh, dk)
                dv = jnp.where(hm, dvh, dv)
            _put_rows(dk_scr, r, s, d, dk)
            _put_rows(dv_scr, r, s, d, dv)

        _for_blocks(d, one)
        dkv_ref[:, 0:AT_L] = dk_scr[...].astype(dkv_ref.dtype)
        dkv_ref[:, AT_L:2 * AT_L] = dv_scr[...].astype(dkv_ref.dtype)

    return pl.pallas_call(
        body, out_shape=jax.ShapeDtypeStruct(du.shape, du.dtype), grid=(2, t // (AT_B * d * _sub(d))),
        in_specs=[_blk_spec(_kcol(g), d), _blk_spec(_vcol(g), d)]
        + _win_specs(_qcol(g), t, d) + _win_specs(_pcol, t, d) + _win_specs(_pcol, t, d) + _win_specs(_pcol, t, d)
        + [pl.BlockSpec(memory_space=pl.ANY)],
        out_specs=pl.BlockSpec((AT_B * d * _sub(d), 2 * AT_L), lambda p, i: (i, OKV // (2 * AT_L) + 2 * g + p)),
        input_output_aliases={14: 0},
        scratch_shapes=[pltpu.VMEM((AT_B * d * _sub(d), AT_L), F32), pltpu.VMEM((AT_B * d * _sub(d), AT_L), F32)],
        name=f"attn_dkv_{g}", compiler_params=_params(("parallel", "parallel")))(
            u, u, u, u, u, do, do, do, lse, lse, lse, e, e, e, du)


def _combine_weights(l0, l1, l2):
    m = jnp.maximum(jnp.maximum(l0, l1), l2)
    e0, e1, e2 = jnp.exp(l0 - m), jnp.exp(l1 - m), jnp.exp(l2 - m)
    inv = 1.0 / (e0 + e1 + e2)
    return e0 * inv, e1 * inv, e2 * inv


def _att_proj(att, w_ref):
    ab = att.astype(BF16)
    return jnp.concatenate([jnp.dot(ab, w_ref[sh], preferred_element_type=F32) for sh in range(w_ref.shape[0])], axis=1)


ROW_TM = 512


def _ln(x, g, b):
    mu = jnp.mean(x, axis=1, keepdims=True)
    xc = x - mu
    var = jnp.mean(xc * xc, axis=1, keepdims=True)
    rstd = lax.rsqrt(var + NORM_EPS)
    xhat = xc * rstd
    return xhat * g + b, xhat, rstd


def _ln_back(dh, xhat, rstd, g):
    dxh = dh * g
    m1 = jnp.mean(dxh, axis=1, keepdims=True)
    m2 = jnp.mean(dxh * xhat, axis=1, keepdims=True)
    return rstd * (dxh - m1 - xhat * m2)


def _mlp_up(h1, w_up):
    t = h1.shape[0]
    tm, tn = 2 * ROW_TM, D

    def body(a_ref, b_ref, up_ref, act_ref):
        up = jnp.dot(a_ref[...], b_ref[...], preferred_element_type=F32)
        up_ref[...] = up.astype(BF16)
        r = jnp.maximum(up, 0.0)
        act_ref[...] = (r * r).astype(BF16)

    blk = pl.BlockSpec((tm, tn), lambda j, i: (i, j))
    return pl.pallas_call(
        body, out_shape=(jax.ShapeDtypeStruct((t, DFF), BF16), jax.ShapeDtypeStruct((t, DFF), BF16)),
        grid=(DFF // tn, t // tm),
        in_specs=[pl.BlockSpec((tm, D), lambda j, i: (i, 0)), pl.BlockSpec((None, D, tn), lambda j, i: (j, 0, 0))],
        out_specs=(blk, blk), name="mlp_up", compiler_params=_params(("parallel", "parallel")))(h1, w_up)


def _d_up(dpre2, w_down, up):
    t = up.shape[0]
    tm, tk = 2 * ROW_TM, D

    def body(a_ref, b_ref, u_ref, o_ref):
        dact = _dot_nt(a_ref[...], b_ref[...])
        o_ref[...] = (dact * 2.0 * jnp.maximum(u_ref[...].astype(F32), 0.0)).astype(BF16)

    blk = pl.BlockSpec((tm, tk), lambda j, i: (i, j))
    return pl.pallas_call(
        body, out_shape=jax.ShapeDtypeStruct((t, DFF), BF16), grid=(DFF // tk, t // tm),
        in_specs=[pl.BlockSpec((tm, D), lambda j, i: (i, 0)), pl.BlockSpec((tk, D), lambda j, i: (j, 0)), blk],
        out_specs=blk, name="d_up", compiler_params=_params(("parallel", "parallel")))(dpre2, w_down, up)


def _dt_bwd(du, ddt):
    t = ddt.shape[0]
    tm = 1024

    def body(f_ref, du_in, o_ref):
        del du_in
        o_ref[:, 0:128] = f_ref[...].astype(o_ref.dtype)
        o_ref[:, 128:256] = jnp.zeros((tm, 128), o_ref.dtype)

    blk = pl.BlockSpec((tm, 128), lambda i: (i, 0))
    return pl.pallas_call(
        body, out_shape=jax.ShapeDtypeStruct(du.shape, du.dtype), grid=(t // tm,),
        in_specs=[blk, pl.BlockSpec(memory_space=pl.ANY)],
        out_specs=pl.BlockSpec((tm, 256), lambda i: (i, ODT // 256)), input_output_aliases={1: 0},
        name="dt_bwd", compiler_params=_params(("parallel",)))(ddt, du)


def _mix_out_ln1(y_ssd, os_, ls_, w_pa, u, bg_row, x, w_out, g_row, b_row):
    t = x.shape[0]
    tm = ROW_TM

    def body(ys_ref, o0, o1, o2, l0, l1, l2, wpa_ref, g0_ref, g1_ref, b0_ref, b1_ref, x_ref, w_ref, g_ref, b_ref,
             att_ref, mixin_ref, pre_ref, h_ref):
        w0, w1, w2 = _combine_weights(l0[...], l1[...], l2[...])
        att = w0 * o0[...] + w1 * o1[...] + w2 * o2[...]
        att_ref[...] = att
        g0 = _sigmoid(g0_ref[...] + b0_ref[...])
        g1 = _sigmoid(g1_ref[...] + b1_ref[...])
        mixin = (g0 * ys_ref[...] + g1 * _att_proj(att, wpa_ref)).astype(BF16)
        mixin_ref[...] = mixin
        pre = ALPHA * x_ref[...] + jnp.dot(mixin, w_ref[...], preferred_element_type=F32)
        pre_ref[...] = pre
        h, _, _ = _ln(pre, g_ref[...], b_ref[...])
        h_ref[...] = h.astype(BF16)

    blk = pl.BlockSpec((tm, D), lambda i: (i, 0))
    ablk = pl.BlockSpec((tm, 2 * AT_L), lambda i: (i, 0))
    row = pl.BlockSpec((1, D), lambda i: (0, 0))
    return pl.pallas_call(
        body,
        out_shape=(jax.ShapeDtypeStruct((t, 2 * AT_L), F32), jax.ShapeDtypeStruct((t, D), BF16),
                   jax.ShapeDtypeStruct((t, D), F32), jax.ShapeDtypeStruct((t, D), BF16)),
        grid=(t // tm,),
        in_specs=[blk] + [ablk] * 6 + [pl.BlockSpec(w_pa.shape, lambda i: (0, 0, 0)),
                  pl.BlockSpec((tm, D), lambda i: (i, OGATE // D)), pl.BlockSpec((tm, D), lambda i: (i, OGATE // D + 1)),
                  row, pl.BlockSpec((1, D), lambda i: (0, 1)), blk, pl.BlockSpec((D, D), lambda i: (0, 0)), row, row],
        out_specs=(ablk, blk, blk, blk), name="mix_out_ln1", compiler_params=_params(("parallel",)))(
            y_ssd, *os_, *ls_, w_pa, u, u, bg_row, bg_row, x, w_out, g_row, b_row)


def _mlp_down_ln2_loss(act, w_down, pre1, tgt, g1_row, b1_row, g2_row, b2_row):
    t = pre1.shape[0]
    tm = ROW_TM

    def body(a_ref, w_ref, p1_ref, t_ref, g1_ref, b1_ref, g2_ref, b2_ref, dpre_ref, dpreb_ref, acc_ref):
        i = pl.program_id(0)
        f = jnp.dot(a_ref[...], w_ref[...], preferred_element_type=F32)
        h1, _, _ = _ln(p1_ref[...], g1_ref[...], b1_ref[...])
        pre2 = ALPHA * h1 + f
        h2, xhat, rstd = _ln(pre2, g2_ref[...], b2_ref[...])
        err = h2 - t_ref[...]
        dh = err * (1.0 / D)
        dpre = _ln_back(dh, xhat, rstd, g2_ref[...])
        dpre_ref[...] = dpre
        dpreb_ref[...] = dpre.astype(BF16)
        loss = jnp.sum(jnp.sum(err * err, axis=1, keepdims=True), axis=0, keepdims=True) * (0.5 / D)
        part = jnp.concatenate([jnp.sum(dh * xhat, axis=0, keepdims=True), jnp.sum(dh, axis=0, keepdims=True),
                                jnp.broadcast_to(loss, (1, D)), jnp.zeros((5, D), F32)], axis=0)

        @pl.when(i == 0)
        def _():
            acc_ref[...] = part

        @pl.when(i > 0)
        def _():
            acc_ref[...] += part

    blk = pl.BlockSpec((tm, D), lambda i: (i, 0))
    row = pl.BlockSpec((1, D), lambda i: (0, 0))
    return pl.pallas_call(
        body,
        out_shape=(jax.ShapeDtypeStruct((t, D), F32), jax.ShapeDtypeStruct((t, D), BF16), jax.ShapeDtypeStruct((8, D), F32)),
        grid=(t // tm,),
        in_specs=[pl.BlockSpec((tm, DFF), lambda i: (i, 0)), pl.BlockSpec((DFF, D), lambda i: (0, 0)), blk, blk, row, row, row, row],
        out_specs=(blk, blk, pl.BlockSpec((8, D), lambda i: (0, 0))),
        name="mlp_down_ln2_loss", compiler_params=_params(("arbitrary",)))(act, w_down, pre1, tgt, g1_row, b1_row, g2_row, b2_row)


def _d_h1_ln1_bwd(dup, w_up, dpre2, pre1, g_row, b_row):
    t = dup.shape[0]
    tm = ROW_TM
    nsh = w_up.shape[0]

    def body(a_ref, w_ref, add_ref, pre_ref, g_ref, b_ref, dpre_ref, acc_ref):
        i = pl.program_id(0)
        dh_ = ALPHA * add_ref[...]
        for sh in range(nsh):
            dh_ = dh_ + _dot_nt(a_ref[:, D * sh:D * (sh + 1)], w_ref[sh])
        _, xhat, rstd = _ln(pre_ref[...], g_ref[...], b_ref[...])
        dpre_ref[...] = _ln_back(dh_, xhat, rstd, g_ref[...])
        rows = jnp.concatenate([jnp.sum(dh_ * xhat, axis=0, keepdims=True), jnp.sum(dh_, axis=0, keepdims=True),
                                jnp.zeros((6, D), F32)], axis=0)

        @pl.when(i == 0)
        def _():
            acc_ref[...] = rows

        @pl.when(i > 0)
        def _():
            acc_ref[...] += rows

    blk = pl.BlockSpec((tm, D), lambda i: (i, 0))
    row = pl.BlockSpec((1, D), lambda i: (0, 0))
    return pl.pallas_call(
        body, out_shape=(jax.ShapeDtypeStruct((t, D), F32), jax.ShapeDtypeStruct((8, D), F32)),
        grid=(t // tm,),
        in_specs=[pl.BlockSpec((tm, nsh * D), lambda i: (i, 0)), pl.BlockSpec(w_up.shape, lambda i: (0, 0, 0)),
                  blk, blk, row, row],
        out_specs=(blk, pl.BlockSpec((8, D), lambda i: (0, 0))),
        name="d_h1_ln1_bwd", compiler_params=_params(("arbitrary",)))(dup, w_up, dpre2, pre1, g_row, b_row)


def _d_mixin_mix_bwd(dpre1, w_out, y_ssd, att, ls_, w_pa, u, bg_row):
    t = y_ssd.shape[0]
    tm = ROW_TM
    nsh, _, ws = w_pa.shape

    def body(a_ref, w_ref, ys_ref, att_ref, l0, l1, l2, wpa_ref, g0_ref, g1_ref, b0_ref, b1_ref,
             dys_ref, dya_ref, du_ref, db_ref, d0, d1, d2, e0, e1, e2):
        i = pl.program_id(0)
        dm = _dot_nt(a_ref[...].astype(BF16), w_ref[...])
        g0 = _sigmoid(g0_ref[...] + b0_ref[...])
        g1 = _sigmoid(g1_ref[...] + b1_ref[...])
        dys_ref[...] = (dm * g0).astype(BF16)
        dya = (dm * g1).astype(BF16)
        dya_ref[...] = dya
        att = att_ref[...]
        da = jnp.zeros((tm, 2 * AT_L), F32)
        for sh in range(nsh):
            da = da + _dot_nt(dya[:, ws * sh:ws * (sh + 1)], wpa_ref[sh])
        r = _iota((2 * AT_L, 2 * AT_L), 0) // AH
        c = _iota((2 * AT_L, 2 * AT_L), 1) // AH
        hs = _dot01(da * att, (r == c).astype(BF16))
        for wg, dref, eref in zip(_combine_weights(l0[...], l1[...], l2[...]), (d0, d1, d2), (e0, e1, e2)):
            dref[...] = wg * da
            eref[...] = -wg * hs
        dl0 = dm * ys_ref[...] * g0 * (1.0 - g0)
        dl1 = dm * _att_proj(att, wpa_ref) * g1 * (1.0 - g1)
        du_ref[:, 0:D] = dl0.astype(BF16)
        du_ref[:, D:2 * D] = dl1.astype(BF16)
        part = jnp.concatenate([jnp.broadcast_to(jnp.sum(dl0, axis=0, keepdims=True), (8, D)),
                                jnp.broadcast_to(jnp.sum(dl1, axis=0, keepdims=True), (8, D))], axis=1)

        @pl.when(i == 0)
        def _():
            db_ref[...] = part

        @pl.when(i > 0)
        def _():
            db_ref[...] += part

    blk = pl.BlockSpec((tm, D), lambda i: (i, 0))
    ablk = pl.BlockSpec((tm, 2 * AT_L), lambda i: (i, 0))
    ashp = jax.ShapeDtypeStruct((t, 2 * AT_L), F32)
    outs = pl.pallas_call(
        body,
        out_shape=(jax.ShapeDtypeStruct((t, D), BF16), jax.ShapeDtypeStruct((t, D), BF16),
                   jax.ShapeDtypeStruct((t, UW), BF16), jax.ShapeDtypeStruct((8, 2 * D), F32)) + (ashp,) * 6,
        grid=(t // tm,),
        in_specs=[blk, pl.BlockSpec((D, D), lambda i: (0, 0)), blk] + [ablk] * 4 + [
                  pl.BlockSpec(w_pa.shape, lambda i: (0, 0, 0)),
                  pl.BlockSpec((tm, D), lambda i: (i, OGATE // D)), pl.BlockSpec((tm, D), lambda i: (i, OGATE // D + 1)),
                  pl.BlockSpec((1, D), lambda i: (0, 0)), pl.BlockSpec((1, D), lambda i: (0, 1))],
        out_specs=(blk, blk, pl.BlockSpec((tm, 2 * D), lambda i: (i, OGATE // (2 * D))),
                   pl.BlockSpec((8, 2 * D), lambda i: (0, 0))) + (ablk,) * 6,
        name="d_mixin_mix_bwd", compiler_params=_params(("arbitrary",)))(
            dpre1, w_out, y_ssd, att, *ls_, w_pa, u, u, bg_row, bg_row)
    return outs[0], outs[1], outs[2], outs[3], outs[4:7], outs[7:10]


def _adamw(w, g, m, v, name):
    r, c = w.shape
    tr, tc = r, c
    for cand in (256, 128, 64, 32, 16, 8):
        if r % cand == 0 and cand * c * 4 <= 2 ** 21:
            tr = cand
            break
    if tr < 64 and c % 256 == 0:
        tr, tc = r, 256
    bc1 = 1.0 / (1.0 - ADAM_B1 ** ADAM_STEP)
    bc2 = 1.0 / (1.0 - ADAM_B2 ** ADAM_STEP)

    def body(w_ref, g_ref, m_ref, v_ref, d_ref, nm_ref, nv_ref):
        gg = g_ref[...]
        nm = ADAM_B1 * m_ref[...] + (1.0 - ADAM_B1) * gg
        nv = ADAM_B2 * v_ref[...] + (1.0 - ADAM_B2) * (gg * gg)
        nm_ref[...] = nm
        nv_ref[...] = nv
        d_ref[...] = -ADAM_LR * ((nm * bc1) / (jnp.sqrt(nv * bc2) + ADAM_EPS) + ADAM_WD * w_ref[...])

    blk = pl.BlockSpec((tr, tc), lambda i, j: (i, j))
    shp = jax.ShapeDtypeStruct((r, c), F32)
    return pl.pallas_call(body, out_shape=(shp, shp, shp), grid=(r // tr, c // tc), in_specs=[blk] * 4,
                          out_specs=(blk,) * 3, name=name, compiler_params=_params(("parallel", "parallel")))(w, g, m, v)


def _segments():
    segs = [(0, 2048), (7488, 9536), (2048, 5120)]
    for g in range(3):
        for p in range(2):
            lo = 256 * g + 128 * p
            segs += [(5952 + lo, 5952 + lo + 128), (6720 + lo, 6720 + lo + 128)]
    segs += [(5184, 5952), (5120, 5184)]
    out, pos = [], 0
    for a, b in segs:
        out.append((a, b, pos))
        pos += b - a
    return out


SHARD_COLS = IN_COLS // 4


def _perm_from_shards(w_shards):
    pieces = []
    for a, b, _ in _segments():
        while a < b:
            s = a // SHARD_COLS
            e = min(b, (s + 1) * SHARD_COLS)
            pieces.append(w_shards[s][:, a - s * SHARD_COLS:e - s * SHARD_COLS])
            a = e
    pieces.append(jnp.zeros((w_shards.shape[1], UW - IN_COLS), w_shards.dtype))
    return jnp.concatenate(pieces, axis=1)


def _shards_from_perm(wp):
    segs = sorted(_segments())
    shards = []
    for s in range(4):
        lo, hi = s * SHARD_COLS, (s + 1) * SHARD_COLS
        pieces = []
        for a, b, pos in segs:
            x, y = max(a, lo), min(b, hi)
            if x < y:
                pieces.append(wp[:, pos + x - a:pos + y - a])
        shards.append(jnp.concatenate(pieces, axis=1))
    return jnp.stack(shards)


def _lanes128(*vecs):
    v = jnp.concatenate([a.reshape(-1) for a in vecs])
    return jnp.pad(v, (0, 128 - v.shape[0])).reshape(1, 128)


EARLY = ("w_proj_ssd", "w_proj_attn", "w_out", "w_up", "w_down")


def _weights_of(gathered):
    g_ps, g_pa, g_o, g_up, g_dn = gathered
    return {"w_proj_ssd": g_ps.reshape(DI, D), "w_proj_attn": g_pa, "w_out": g_o.reshape(D, D), "w_up": g_up,
            "w_down": g_dn.reshape(DFF, D)}


def _local_grads(x, tgt, wts, sm, rs_idx=None):
    row = lambda a: a.reshape(1, -1)
    bg_row, cb_row = row(sm["b_gate"]), row(sm["conv_b"])
    par = jnp.concatenate([_lanes128(sm["dt_bias_f"], sm["dt_bias_b"]), _lanes128(sm["a_log_f"], sm["a_log_b"]),
                           jnp.zeros((6, 128), F32)], axis=0)
    dsk_row = row(jnp.repeat(sm["d_skip"], HP))
    nw_row = row(sm["ssd_norm_w"])
    g1, b1, g2, b2 = row(sm["ln1_g"]), row(sm["ln1_b"]), row(sm["ln2_g"]), row(sm["ln2_b"])

    xb = x.astype(BF16)
    u, gathered = _in_proj(xb, wts["w_in_p"], side=_gather_side(wts["pending"]) if "pending" in wts else None)
    if gathered:
        wts = {**wts, **_weights_of(gathered)}
    xbc = _conv_fwd(u, sm["conv_w"], cb_row)
    y_f, st_f = _ssd_fwd(xbc, u, par, rev=False)
    y_fb, st_b = _ssd_fwd(xbc, u, par, y_f, rev=True)
    s_out, y_ssd = _gatenorm_fwd(y_fb, xbc, u, dsk_row, nw_row, wts["w_proj_ssd"])
    att_o, att_l = [], []
    for g in range(3):
        o, l = _attn_fwd(u, g)
        att_o.append(o)
        att_l.append(l)
    att, mixin, pre1, h1 = _mix_out_ln1(y_ssd, att_o, att_l, wts["w_proj_attn"], u, bg_row, x, wts["w_out"], g1, b1)
    up, act = _mlp_up(h1, wts["w_up"])
    dpre2, dpre2_b, acc2 = _mlp_down_ln2_loss(act, wts["w_down"], pre1, tgt, g1, b1, g2, b2)

    dw_down = _mm_tn(act, dpre2_b, tka=1024, tn=1024, tt=1024, name="dw_down")
    dup = _d_up(dpre2_b, wts["w_down"], up)
    dw_up = _mm_tn(h1, dup, tka=1024, tn=1024, tt=1024, name="dw_up", out_shards=4)
    dpre1, acc1 = _d_h1_ln1_bwd(dup, wts["w_up"], dpre2, pre1, g1, b1)
    dw_out = _mm_tn(mixin, dpre1, tka=1024, tn=1024, tt=1024, name="dw_out")
    dy_ssd, dy_att, du, dbg, do_g, e_g = _d_mixin_mix_bwd(dpre1, wts["w_out"], y_ssd, att, att_l, wts["w_proj_attn"], u, bg_row)
    dw_proj_ssd = _mm_tn(s_out, dy_ssd, tka=1024, tn=1024, tt=1024, name="dw_proj_ssd")
    dw_proj_attn = _mm_tn(att, dy_att, tka=256, tn=256, tt=1024, name="dw_proj_attn", out_shards=4)
    for g in range(3):
        du = _attn_dq(u, du, do_g[g], att_l[g], e_g[g], g)
        du = _attn_dkv(u, du, do_g[g], att_l[g], e_g[g], g)
    big = {
        "w_proj_ssd": dw_proj_ssd.reshape(4, DI // 4, D),
        "w_proj_attn": dw_proj_attn,
        "w_out": dw_out.reshape(4, D // 4, D),
        "w_up": dw_up,
        "w_down": dw_down.reshape(4, DFF // 4, D),
    }
    early = [big[n] for n in EARLY]
    dy, du, dnw, dds, recv = _gatenorm_bwd(dy_ssd, wts["w_proj_ssd"], y_fb, xbc, u, du, dsk_row, nw_row,
                                           side=_swap_side(early) if rs_idx else None)
    if rs_idx:
        halves = [_add_half(g, r, rs_idx[0], f"rs_add_half_{n}") for g, r, n in zip(early, recv, EARLY)]
    dxs_f, dbc_f, ddt_f, sacc_f, recv = _ssd_bwd(xbc, u, par, dy, st_f, rev=False,
                                                 side=_step1_side([h[1] for h in halves]) if rs_idx else None)
    if rs_idx:
        k = len(EARLY)
        sums1 = [_rs_add1(h[0], ra, rb, rs_idx[1], f"rs_add1_{n}")
                 for h, ra, rb, n in zip(halves, recv[:k], recv[k:], EARLY)]
    dxs, dbc, ddt, sacc_b, recv = _ssd_bwd(
        xbc, u, par, dy, st_b, rev=True, add=(dxs_f, dbc_f, ddt_f),
        side=_step2_side([s1[2] for s1 in sums1], [s1[3] for s1 in sums1]) if rs_idx else None)
    pieces = None
    if rs_idx:
        pieces = {n: _rs_add2(s1[0], s1[1], ra, rb, rs_idx[1], f"rs_add2_{n}")
                  for s1, ra, rb, n in zip(sums1, recv[:k], recv[k:], EARLY)}
    dpre_c, dcw, dcb = _conv_dpre(u, dxs, dy, dbc, dsk_row, sm["conv_w"], cb_row)
    du = _conv_dx(du, dpre_c, sm["conv_w"])
    du = _dt_bwd(du, ddt)
    dw_in_p = _mm_tn(xb, du, tka=1024, tn=2432, tt=1024, name="dw_in")
    big["w_in"] = _shards_from_perm(dw_in_p)
    side = None
    if rs_idx:
        g = big["w_in"]
        half = _add_half(g, _run_side(_swap_side([g]), "rs_swap_halves")[0], rs_idx[0], "rs_add_half_w_in")
        side = _step1_side([half[1]])
    dx, recv = _d_x(du, wts["w_in_p"], dpre1, side)
    if rs_idx:
        s1 = _rs_add1(half[0], recv[0], recv[1], rs_idx[1], "rs_add1_w_in")
        ra2, rb2 = _run_side(_step2_side([s1[2]], [s1[3]]), "rs_step2")
        pieces["w_in"] = _rs_add2(s1[0], s1[1], ra2, rb2, rs_idx[1], "rs_add2_w_in")

    sacc = sacc_f + sacc_b
    small = {
        "b_gate": dbg[0], "conv_w": dcw[0:KCONV], "conv_b": dcb[0],
        "dt_bias_f": sacc[0, 0:32], "dt_bias_b": sacc[0, 32:64], "a_log_f": sacc[1, 0:32], "a_log_b": sacc[1, 32:64],
        "d_skip": dds[0, 0:32], "ssd_norm_w": dnw[0],
        "ln1_g": acc1[0], "ln1_b": acc1[1], "ln2_g": acc2[0], "ln2_b": acc2[1], "loss": acc2[2, 0:1],
    }
    return dx, big, small, pieces


HBM_SPEC = pl.BlockSpec(memory_space=pl.ANY)


def _place():
    x, y, c = lax.axis_index("x"), lax.axis_index("y"), lax.axis_index("c")
    chips = [(1 - x, y), (x, 1 - y), (1 - x, 1 - y)]
    return x, y, c, chips


def _gather_phases(n):
    def tools(ins, outs, send_sems, recv_sems):
        x, y, c, _ = _place()
        slots = (2 * x + y, 2 * (1 - x) + y, 2 * x + 1 - y, 2 * (1 - x) + 1 - y)
        peers = ((1 - x, y, c), (x, 1 - y, c), (x, y, 1 - c))

        def copy(w, k, src, dst, to):
            return pltpu.make_async_remote_copy(src_ref=src, dst_ref=dst, send_sem=send_sems.at[w, k],
                                                recv_sem=recv_sems.at[w, k], device_id=to, device_id_type=MESH)

        def rows(w, core, part):
            rh = ins[w].shape[0] // 2
            if part is None:
                return pl.ds(core * rh, rh)
            return pl.ds(core * rh + part * (rh // 2), rh // 2)

        def same(w, k, slot, core, part, to):
            blk = outs[w].at[slot, rows(w, core, part), :]
            return copy(w, k, blk, blk, to)

        def sends(w):
            q, q_x, q_y, q_d = slots
            x_nbr, y_nbr, sibling = peers
            mine = rows(w, c, None)
            mk = functools.partial
            return [mk(copy, w, 0, ins[w].at[mine, :], outs[w].at[q, mine, :], x_nbr),
                    mk(copy, w, 1, ins[w].at[mine, :], outs[w].at[q, mine, :], y_nbr),
                    mk(same, w, 2, q_x, c, 0, y_nbr), mk(same, w, 3, q_y, c, 1, x_nbr),
                    mk(same, w, 4, q_x, c, None, sibling), mk(same, w, 5, q_y, c, None, sibling),
                    mk(same, w, 6, q_d, c, 0, sibling), mk(same, w, 7, q_d, c, 1, sibling),
                    mk(copy, w, 8, ins[w], outs[w].at[q], sibling)]

        return c, slots, peers, same, sends

    def first(*refs):
        _, _, _, _, sends = tools(*refs)
        for w in range(n):
            cps = sends(w)
            for k in (8, 0, 1):
                cps[k]().start()

    def second(*refs):
        c, (_, q_x, q_y, _), (x_nbr, y_nbr, _), same, sends = tools(*refs)
        for w in range(n):
            cps = sends(w)
            same(w, 0, q_x, c, None, x_nbr).wait_recv()
            cps[2]().start()
            cps[4]().start()
            same(w, 1, q_y, c, None, y_nbr).wait_recv()
            cps[3]().start()
            cps[5]().start()

    def third(*refs):
        c, (_, _, _, q_d), (x_nbr, y_nbr, _), same, sends = tools(*refs)
        for w in range(n):
            cps = sends(w)
            same(w, 2, q_d, c, 0, y_nbr).wait_recv()
            cps[6]().start()
            same(w, 3, q_d, c, 1, x_nbr).wait_recv()
            cps[7]().start()

    def last(*refs):
        c, (_, q_x, q_y, q_d), (_, _, sibling), same, sends = tools(*refs)
        for w in range(n):
            same(w, 4, q_x, 1 - c, None, sibling).wait_recv()
            same(w, 5, q_y, 1 - c, None, sibling).wait_recv()
            same(w, 6, q_d, 1 - c, 0, sibling).wait_recv()
            same(w, 7, q_d, 1 - c, 1, sibling).wait_recv()
            sends(w)[8]().wait_recv()
        for w in range(n):
            for mk_cp in sends(w):
                mk_cp().wait_send()

    return first, second, third, last


def _gather_side(shards):
    first, second, third, last = _gather_phases(len(shards))
    shapes = tuple(jax.ShapeDtypeStruct((4,) + s.shape, s.dtype) for s in shards)
    return _Side(tuple(shards), shapes, (len(shards), 9), None, ((0.0, first), (0.36, second), (0.58, third), (1.0, last)))


class _Side(NamedTuple):
    ins: tuple
    out_shapes: tuple
    nsem: tuple
    make: Callable
    phases: tuple = ()


def _swap_copies(ins, outs, send_sems, recv_sems):
    x, y, c, _ = _place()
    copies = []
    for w in range(len(ins)):
        rh = ins[w].shape[1] // 2
        for p in range(4):
            copies.append(pltpu.make_async_remote_copy(
                src_ref=ins[w].at[p, pl.ds((1 - c) * rh, rh), :], dst_ref=outs[w].at[p],
                send_sem=send_sems.at[w, p], recv_sem=recv_sems.at[w, p],
                device_id=(x, y, 1 - c), device_id_type=MESH))
    return copies


def _swap_side(grads):
    shapes = tuple(jax.ShapeDtypeStruct((4, g.shape[1] // 2, g.shape[2]), F32) for g in grads)
    return _Side(tuple(grads), shapes, (len(grads), 4), _swap_copies)


def _step1_copies(ins, outs, send_sems, recv_sems):
    n = len(ins)
    out_a, out_b = outs[:n], outs[n:]
    x, y, c, _ = _place()
    copies = []
    for w in range(n):
        rq = ins[w].shape[1] // 2
        for i in range(2):
            copies.append(pltpu.make_async_remote_copy(
                src_ref=ins[w].at[2 * (1 - x) + i, pl.ds(0, rq), :], dst_ref=out_a[w].at[i],
                send_sem=send_sems.at[w, i], recv_sem=recv_sems.at[w, i],
                device_id=(1 - x, y, c), device_id_type=MESH))
            copies.append(pltpu.make_async_remote_copy(
                src_ref=ins[w].at[2 * i + 1 - y, pl.ds(rq, rq), :], dst_ref=out_b[w].at[i],
                send_sem=send_sems.at[w, 2 + i], recv_sem=recv_sems.at[w, 2 + i],
                device_id=(x, 1 - y, c), device_id_type=MESH))
    return copies


def _step1_side(parts):
    quarter = tuple(jax.ShapeDtypeStruct((2, p.shape[1] // 2, p.shape[2]), p.dtype) for p in parts)
    return _Side(tuple(parts), quarter + quarter, (len(parts), 4), _step1_copies)


def _step2_copies(ins, outs, send_sems, recv_sems):
    n = len(ins) // 2
    in_a, in_b, out_a, out_b = ins[:n], ins[n:], outs[:n], outs[n:]
    x, y, c, _ = _place()
    copies = []
    for w in range(n):
        copies.append(pltpu.make_async_remote_copy(
            src_ref=in_a[w].at[1 - y], dst_ref=out_a[w], send_sem=send_sems.at[w, 0], recv_sem=recv_sems.at[w, 0],
            device_id=(x, 1 - y, c), device_id_type=MESH))
        copies.append(pltpu.make_async_remote_copy(
            src_ref=in_b[w].at[1 - x], dst_ref=out_b[w], send_sem=send_sems.at[w, 1], recv_sem=recv_sems.at[w, 1],
            device_id=(1 - x, y, c), device_id_type=MESH))
    return copies


def _step2_side(tas, tbs):
    one = tuple(jax.ShapeDtypeStruct(p.shape[1:], p.dtype) for p in tuple(tas) + tuple(tbs))
    return _Side(tuple(tas) + tuple(tbs), one, (len(tas), 2), _step2_copies)


def _phases_of(side, n_steps):
    if side.phases:
        return [(min(int(f * n_steps), n_steps - 1), fn) for f, fn in side.phases]

    def start(*refs):
        for cp in side.make(*refs):
            cp.start()

    def wait(*refs):
        for cp in side.make(*refs):
            cp.wait()

    return [(0, start), (n_steps - 1, wait)]


def _run_side(side, name):
    n_in, n_out = len(side.ins), len(side.out_shapes)

    def body(*refs):
        for _, fn in _phases_of(side, 1):
            fn(refs[:n_in], refs[n_in:n_in + n_out], *refs[n_in + n_out:])

    return pl.pallas_call(
        body, out_shape=list(side.out_shapes), in_specs=[HBM_SPEC] * n_in, out_specs=[HBM_SPEC] * n_out,
        scratch_shapes=[pltpu.SemaphoreType.DMA(side.nsem), pltpu.SemaphoreType.DMA(side.nsem)], name=name)(*side.ins)


def _host_call(body, side, n_steps, *, out_shape, in_specs, out_specs, scratch_shapes, args, aliases, name, sem):
    n_in, n_out, n_scr = len(in_specs), len(out_shape), len(scratch_shapes)
    if side is None:
        outs = pl.pallas_call(body, out_shape=tuple(out_shape), grid=(n_steps,), in_specs=list(in_specs),
                              out_specs=tuple(out_specs), scratch_shapes=list(scratch_shapes),
                              input_output_aliases=aliases, name=name, compiler_params=_params(sem))(*args)
        return tuple(outs), ()
    ns_in, ns_out = len(side.ins), len(side.out_shapes)

    def wrapped(*refs):
        h_in, s_in = refs[:n_in], refs[n_in:n_in + ns_in]
        o0 = n_in + ns_in
        h_out, s_out = refs[o0:o0 + n_out], refs[o0 + n_out:o0 + n_out + ns_out]
        c0 = o0 + n_out + ns_out
        h_scr, sems = refs[c0:c0 + n_scr], refs[c0 + n_scr:]
        step = pl.program_id(0)
        phases = _phases_of(side, n_steps)
        for at, fn in phases[:-1]:
            pl.when(step == at)(functools.partial(fn, s_in, s_out, *sems))
        body(*h_in, *h_out, *h_scr)
        pl.when(step == phases[-1][0])(functools.partial(phases[-1][1], s_in, s_out, *sems))

    outs = pl.pallas_call(
        wrapped, out_shape=tuple(out_shape) + tuple(side.out_shapes), grid=(n_steps,),
        in_specs=list(in_specs) + [HBM_SPEC] * ns_in, out_specs=tuple(out_specs) + (HBM_SPEC,) * ns_out,
        scratch_shapes=list(scratch_shapes) + [pltpu.SemaphoreType.DMA(side.nsem), pltpu.SemaphoreType.DMA(side.nsem)],
        input_output_aliases=aliases, name=name, compiler_params=_params(sem))(*args, *side.ins)
    return tuple(outs[:n_out]), tuple(outs[n_out:])


def _join_halves(pieces):
    n = len(pieces)

    def body(*refs):
        outs = refs[n:2 * n]
        send_sems, recv_sems = refs[2 * n:]
        x, y, c, _ = _place()

        def copy(w, slot):
            return pltpu.make_async_remote_copy(
                src_ref=outs[w].at[slot], dst_ref=outs[w].at[slot], send_sem=send_sems.at[w], recv_sem=recv_sems.at[w],
                device_id=(x, y, 1 - c), device_id_type=MESH)

        for w in range(n):
            copy(w, c).start()
        for w in range(n):
            copy(w, 1 - c).wait_recv()
            copy(w, c).wait_send()

    return pl.pallas_call(
        body, out_shape=[jax.ShapeDtypeStruct(p.shape, F32) for p in pieces],
        in_specs=[HBM_SPEC] * n, out_specs=[HBM_SPEC] * n, input_output_aliases={w: w for w in range(n)},
        scratch_shapes=[pltpu.SemaphoreType.DMA((n,)), pltpu.SemaphoreType.DMA((n,))],
        name="rs_join_halves")(*pieces)


def _add_tile_rows(rh, c):
    for cand in (512, 256, 128, 64, 32, 16, 8):
        if rh % cand == 0 and cand * c * 4 <= 2 ** 21:
            return cand
    return rh


def _add_half(grad, recv, c_idx, name):
    _, r, cc = grad.shape
    rh = r // 2
    tr = _add_tile_rows(rh, cc)
    nb = rh // tr

    def body(c_ref, g_ref, r_ref, o_ref, ob_ref):
        del c_ref
        s = g_ref[...] + r_ref[...]
        o_ref[...] = s
        ob_ref[...] = s.astype(BF16)

    blk = pl.BlockSpec((None, tr, cc), lambda p, i, c_ref: (p, i, 0))
    grid_spec = pltpu.PrefetchScalarGridSpec(
        num_scalar_prefetch=1, grid=(4, nb),
        in_specs=[pl.BlockSpec((None, tr, cc), lambda p, i, c_ref: (p, c_ref[0] * nb + i, 0)), blk],
        out_specs=(blk, blk))
    return pl.pallas_call(
        body, out_shape=(jax.ShapeDtypeStruct((4, rh, cc), F32), jax.ShapeDtypeStruct((4, rh, cc), BF16)),
        grid_spec=grid_spec, name=name, compiler_params=_params(("parallel", "parallel")))(c_idx, grad, recv)


def _rs_add1(part, recv_a, recv_b, xy_idx, name):
    _, rh, cc = part.shape
    rq = rh // 2
    tr = _add_tile_rows(rq, cc)
    nb = rq // tr

    def body(xy_ref, pa_ref, pb_ref, ra_ref, rb_ref, ta_ref, tb_ref, tab_ref, tbb_ref):
        del xy_ref
        ta = pa_ref[...] + ra_ref[...].astype(F32)
        tb = pb_ref[...] + rb_ref[...].astype(F32)
        ta_ref[...] = ta
        tb_ref[...] = tb
        tab_ref[...] = ta.astype(BF16)
        tbb_ref[...] = tb.astype(BF16)

    blk = pl.BlockSpec((None, tr, cc), lambda i, j, xy: (i, j, 0))
    grid_spec = pltpu.PrefetchScalarGridSpec(
        num_scalar_prefetch=1, grid=(2, nb),
        in_specs=[pl.BlockSpec((None, tr, cc), lambda i, j, xy: (2 * xy[0] + i, j, 0)),
                  pl.BlockSpec((None, tr, cc), lambda i, j, xy: (2 * i + xy[1], nb + j, 0)), blk, blk],
        out_specs=(blk, blk, blk, blk))
    f32s, b16s = jax.ShapeDtypeStruct((2, rq, cc), F32), jax.ShapeDtypeStruct((2, rq, cc), BF16)
    return pl.pallas_call(body, out_shape=(f32s, f32s, b16s, b16s), grid_spec=grid_spec, name=name,
                          compiler_params=_params(("parallel", "parallel")))(xy_idx, part, part, recv_a, recv_b)


def _rs_add2(ta, tb, recv_a, recv_b, xy_idx, name):
    _, rq, cc = ta.shape
    tr = _add_tile_rows(rq, cc)
    nb = rq // tr

    def body(xy_ref, ta_ref, tb_ref, ra_ref, rb_ref, o_ref):
        del xy_ref
        s = pl.program_id(0)
        fa = ta_ref[...] + ra_ref[...].astype(F32)
        fb = tb_ref[...] + rb_ref[...].astype(F32)
        o_ref[...] = jnp.where(s == 0, fa, fb)

    rblk = pl.BlockSpec((tr, cc), lambda s, j, xy: (j, 0))
    grid_spec = pltpu.PrefetchScalarGridSpec(
        num_scalar_prefetch=1, grid=(2, nb),
        in_specs=[pl.BlockSpec((None, tr, cc), lambda s, j, xy: (xy[1], j, 0)),
                  pl.BlockSpec((None, tr, cc), lambda s, j, xy: (xy[0], j, 0)), rblk, rblk],
        out_specs=pl.BlockSpec((None, tr, cc), lambda s, j, xy: (xy[2], s * nb + j, 0)))
    return pl.pallas_call(body, out_shape=jax.ShapeDtypeStruct((2, 2 * rq, cc), F32), grid_spec=grid_spec, name=name,
                          compiler_params=_params(("parallel", "parallel")))(xy_idx, ta, tb, recv_a, recv_b)


def _allreduce_small(slab):
    r = slab.shape[0]

    def body(x_ref, o_ref, buf, send_sems, recv_sems):
        x, y, c, _ = _place()
        me = 4 * x + 2 * y + c
        buf[me] = x_ref[...]
        peers = []
        for k in range(1, 8):
            kx, ky, kc = (k >> 2) & 1, (k >> 1) & 1, k & 1
            peers.append((x + kx - 2 * x * kx, y + ky - 2 * y * ky, c + kc - 2 * c * kc))

        def copy(k, slot):
            return pltpu.make_async_remote_copy(src_ref=x_ref, dst_ref=buf.at[slot], send_sem=send_sems.at[k],
                                                recv_sem=recv_sems.at[k], device_id=peers[k], device_id_type=MESH)

        for k in range(7):
            copy(k, me).start()
        for k, (px, py, pc) in enumerate(peers):
            copy(k, 4 * px + 2 * py + pc).wait_recv()
        for k in range(7):
            copy(k, me).wait_send()
        acc = buf[0]
        for j in range(1, 8):
            acc = acc + buf[j]
        o_ref[...] = acc

    vm = pl.BlockSpec(memory_space=pltpu.VMEM)
    return pl.pallas_call(
        body, out_shape=jax.ShapeDtypeStruct((r, 128), F32), in_specs=[vm], out_specs=vm,
        scratch_shapes=[pltpu.VMEM((8, r, 128), F32), pltpu.SemaphoreType.DMA((7,)), pltpu.SemaphoreType.DMA((7,))],
        name="allreduce_small")(slab)


def _pack(arrs):
    rows = []
    for a in arrs:
        v = a.reshape(-1)
        v = jnp.pad(v, (0, (-v.shape[0]) % 128))
        rows.append(v.reshape(-1, 128))
    slab = jnp.concatenate(rows, axis=0)
    return jnp.pad(slab, ((0, (-slab.shape[0]) % 8), (0, 0)))


def _unpack(slab, shapes):
    out, r0 = [], 0
    for shp in shapes:
        size = math.prod(shp)
        nr = -(-size // 128)
        out.append(slab[r0:r0 + nr].reshape(-1)[:size].reshape(shp))
        r0 += nr
    return out


BIG = ("w_in", "w_proj_ssd", "w_proj_attn", "w_out", "w_up", "w_down")
SMALL = ("b_gate", "conv_w", "conv_b", "dt_bias_f", "dt_bias_b", "a_log_f", "a_log_b", "d_skip", "ssd_norm_w",
         "ln1_g", "ln1_b", "ln2_g", "ln2_b")
ORDER = ("w_in", "b_gate", "conv_w", "conv_b", "dt_bias_f", "dt_bias_b", "a_log_f", "a_log_b", "d_skip", "ssd_norm_w",
         "w_proj_ssd", "w_proj_attn", "w_out", "ln1_g", "ln1_b", "w_up", "w_down", "ln2_g", "ln2_b")


def kernel(x, w_in, b_gate, conv_w, conv_b, dt_bias_f, dt_bias_b, a_log_f, a_log_b, d_skip, ssd_norm_w, w_proj_ssd, w_proj_attn, w_out, ln1_g, ln1_b, w_up, w_down, ln2_g, ln2_b, loss_target, m_w_in, m_b_gate, m_conv_w, m_conv_b, m_dt_bias_f, m_dt_bias_b, m_a_log_f, m_a_log_b, m_d_skip, m_ssd_norm_w, m_w_proj_ssd, m_w_proj_attn, m_w_out, m_ln1_g, m_ln1_b, m_w_up, m_w_down, m_ln2_g, m_ln2_b, v_w_in, v_b_gate, v_conv_w, v_conv_b, v_dt_bias_f, v_dt_bias_b, v_a_log_f, v_a_log_b, v_d_skip, v_ssd_norm_w, v_w_proj_ssd, v_w_proj_attn, v_w_out, v_ln1_g, v_ln1_b, v_w_up, v_w_down, v_ln2_g, v_ln2_b):
    w = dict(w_in=w_in, b_gate=b_gate, conv_w=conv_w, conv_b=conv_b, dt_bias_f=dt_bias_f, dt_bias_b=dt_bias_b,
             a_log_f=a_log_f, a_log_b=a_log_b, d_skip=d_skip, ssd_norm_w=ssd_norm_w, w_proj_ssd=w_proj_ssd,
             w_proj_attn=w_proj_attn, w_out=w_out, ln1_g=ln1_g, ln1_b=ln1_b, w_up=w_up, w_down=w_down, ln2_g=ln2_g, ln2_b=ln2_b)
    m = dict(w_in=m_w_in, b_gate=m_b_gate, conv_w=m_conv_w, conv_b=m_conv_b, dt_bias_f=m_dt_bias_f, dt_bias_b=m_dt_bias_b,
             a_log_f=m_a_log_f, a_log_b=m_a_log_b, d_skip=m_d_skip, ssd_norm_w=m_ssd_norm_w, w_proj_ssd=m_w_proj_ssd,
             w_proj_attn=m_w_proj_attn, w_out=m_w_out, ln1_g=m_ln1_g, ln1_b=m_ln1_b, w_up=m_w_up, w_down=m_w_down,
             ln2_g=m_ln2_g, ln2_b=m_ln2_b)
    v = dict(w_in=v_w_in, b_gate=v_b_gate, conv_w=v_conv_w, conv_b=v_conv_b, dt_bias_f=v_dt_bias_f, dt_bias_b=v_dt_bias_b,
             a_log_f=v_a_log_f, a_log_b=v_a_log_b, d_skip=v_d_skip, ssd_norm_w=v_ssd_norm_w, w_proj_ssd=v_w_proj_ssd,
             w_proj_attn=v_w_proj_attn, w_out=v_w_out, ln1_g=v_ln1_g, ln1_b=v_ln1_b, w_up=v_w_up, w_down=v_w_down,
             ln2_g=v_ln2_g, ln2_b=v_ln2_b)
    xi, yi, ci = lax.axis_index("x"), lax.axis_index("y"), lax.axis_index("c")
    shard = 2 * xi + yi

    (g_in,) = _run_side(_gather_side([w["w_in"].astype(BF16)]), "allgather_w_in")
    wts = {"w_in_p": _perm_from_shards(g_in), "pending": [w[n].astype(BF16) for n in EARLY]}

    cw_slab = jnp.zeros((KCONV, 4, CONVD // 4), F32)
    cw_slab = lax.dynamic_update_slice(cw_slab, conv_w[:, None, :] * 0.5, (0, shard, 0))
    conv_w_all = _unpack(_allreduce_small(_pack([cw_slab])), [(KCONV, CONVD)])[0]

    sm = {n: w[n] for n in SMALL}
    sm["conv_w"] = conv_w_all
    c_idx = jnp.reshape(ci, (1,)).astype(jnp.int32)
    xy_idx = jnp.stack([xi, yi, ci]).astype(jnp.int32)
    dx, big, small, pieces = _local_grads(x[0], loss_target[0], wts, sm, rs_idx=(c_idx, xy_idx))

    names = list(SMALL) + ["loss"]
    shapes = [small[n].shape for n in names]
    red = dict(zip(names, _unpack(_allreduce_small(_pack([small[n] for n in names])), shapes)))
    loss = red["loss"].reshape(())
    gsm = {n: red[n] for n in SMALL}
    conv_w_grad_shard = lax.dynamic_slice_in_dim(gsm["conv_w"].reshape(KCONV, 4, CONVD // 4), shard, 1, axis=1)
    gsm["conv_w"] = conv_w_grad_shard.reshape(KCONV, CONVD // 4)

    joined = _join_halves([pieces[n] for n in BIG])
    gbig = {n: j.reshape(w[n].shape) for n, j in zip(BIG, joined)}

    grads, deltas, new_m, new_v = {}, {}, {}, {}
    for n in BIG:
        grads[n] = gbig[n]
        if n == "w_in":
            gt = gbig[n].T
            dlt, nmt, nvt = _adamw(w[n].T, gt, m[n].T, v[n].T, f"adamw_{n}")
            grads[n], deltas[n], new_m[n], new_v[n] = gt.T, dlt.T, nmt.T, nvt.T
            continue
        deltas[n], new_m[n], new_v[n] = _adamw(w[n], gbig[n], m[n], v[n], f"adamw_{n}")
    sshapes = [w[n].shape for n in SMALL]
    d_s, m_s, v_s = _adamw(_pack([w[n] for n in SMALL]), _pack([gsm[n] for n in SMALL]),
                           _pack([m[n] for n in SMALL]), _pack([v[n] for n in SMALL]), "adamw_small")
    for n, dd, mm, vv in zip(SMALL, _unpack(d_s, sshapes), _unpack(m_s, sshapes), _unpack(v_s, sshapes)):
        grads[n], deltas[n], new_m[n], new_v[n] = gsm[n], dd, mm, vv

    return (loss, dx[None], *[grads[n] for n in ORDER], *[deltas[n] for n in ORDER],
            *[new_m[n] for n in ORDER], *[new_v[n] for n in ORDER])
```
